```python
import jax, jax.numpy as jnp
from jax import lax
import numpy as np

D_MODEL = 1024
BATCH = 8
SEQ = 8192
DEPTH = 1

PLE_DIM = 256
MIX_WIDTH = D_MODEL
HEAD_DIM = 64
ATT_WIDTH = MIX_WIDTH // 2
N_ATT_HEADS = ATT_WIDTH // HEAD_DIM
SG_WIDTH = MIX_WIDTH - ATT_WIDTH
N_SG_GROUPS = 8
SG_GROUP_DIM = SG_WIDTH // N_SG_GROUPS
CHUNK = 128
Q_BLOCK = 128
D_FF = 4 * D_MODEL
IN_PROJ_WIDTH = 3 * ATT_WIDTH + N_ATT_HEADS + 2 * SG_WIDTH
EPS = 1e-6

kernel_name = "hybrid_fox_gmlp_sandwich_ple"


def rms_norm(x, g):
    xf = x.astype(jnp.float32)
    y = xf * lax.rsqrt(jnp.mean(xf * xf, axis=-1, keepdims=True) + EPS)
    return (y * g.astype(jnp.float32)).astype(x.dtype)


def layer_norm(x, g, b):
    xf = x.astype(jnp.float32)
    mu = jnp.mean(xf, axis=-1, keepdims=True)
    xc = xf - mu
    y = xc * lax.rsqrt(jnp.mean(xc * xc, axis=-1, keepdims=True) + EPS)
    return (y * g.astype(jnp.float32) + b.astype(jnp.float32)).astype(x.dtype)


def forgetting_attention(q, k, v, f_logit):
    B, S, H, Dh = q.shape
    nb = S // Q_BLOCK
    log_f = jax.nn.log_sigmoid(f_logit.astype(jnp.float32))
    c = jnp.transpose(jnp.cumsum(log_f, axis=1), (0, 2, 1))
    qh = jnp.transpose(q, (0, 2, 1, 3)).astype(jnp.float32) * (Dh ** -0.5)
    kh = jnp.transpose(k, (0, 2, 1, 3)).astype(jnp.float32)
    vh = jnp.transpose(v, (0, 2, 1, 3)).astype(jnp.float32)
    q_blocks = jnp.transpose(qh.reshape(B, H, nb, Q_BLOCK, Dh), (2, 0, 1, 3, 4))
    cq_blocks = jnp.transpose(c.reshape(B, H, nb, Q_BLOCK), (2, 0, 1, 3))
    k_pos = jnp.arange(S)

    def one_block(args):
        qb, cqb, bi = args
        q_pos = bi * Q_BLOCK + jnp.arange(Q_BLOCK)
        logits = (jnp.einsum('bhqd,bhkd->bhqk', qb, kh)
                  + cqb[..., :, None] - c[..., None, :])
        causal = k_pos[None, :] <= q_pos[:, None]
        logits = jnp.where(causal, logits, -jnp.inf)
        probs = jax.nn.softmax(logits, axis=-1)
        return jnp.einsum('bhqk,bhkd->bhqd', probs, vh)

    out = lax.map(one_block, (q_blocks, cq_blocks, jnp.arange(nb)))
    out = jnp.transpose(out, (1, 0, 3, 2, 4)).reshape(B, S, H * Dh)
    return out.astype(q.dtype)


def chunked_spatial_gating(u, v, ln_g, ln_b, w_s, b_s):
    B, S, _ = u.shape
    nc = S // CHUNK
    u = jax.nn.gelu(u)
    v = layer_norm(jax.nn.gelu(v), ln_g, ln_b)
    vc = v.reshape(B, nc, CHUNK, N_SG_GROUPS, SG_GROUP_DIM)
    mask = jnp.tril(jnp.ones((CHUNK, CHUNK), dtype=w_s.dtype))
    w = w_s * mask[None]
    mixed = (jnp.einsum('gts,bcsgd->bctgd', w, vc)
             + jnp.transpose(b_s)[None, None, :, :, None])
    return u * mixed.reshape(B, S, SG_WIDTH)


def _fwd_setup_inputs(seed: int = 0) -> dict:
    key = jax.random.key(seed)
    ks = jax.random.split(key, 24)
    n = jax.random.normal
    f32 = jnp.float32

    def gain(k, shape):
        return 1.0 + 0.05 * n(k, shape, f32)

    x = n(ks[0], (BATCH, SEQ, D_MODEL), f32)
    p = n(ks[1], (DEPTH, BATCH, SEQ, PLE_DIM), f32)
    w_in = n(ks[2], (DEPTH, D_MODEL, IN_PROJ_WIDTH), f32) * D_MODEL ** -0.5
    f_bias = (jnp.linspace(1.0, 5.0, N_ATT_HEADS, dtype=f32)[None, :]
              + 0.1 * n(ks[3], (DEPTH, N_ATT_HEADS), f32))
    sg_ln_g = gain(ks[4], (DEPTH, SG_WIDTH))
    sg_ln_b = 0.02 * n(ks[5], (DEPTH, SG_WIDTH), f32)
    sg_w = n(ks[6], (DEPTH, N_SG_GROUPS, CHUNK, CHUNK), f32) * CHUNK ** -0.5
    sg_b = 1.0 + 0.02 * n(ks[7], (DEPTH, N_SG_GROUPS, CHUNK), f32)
    att_out_g = gain(ks[8], (DEPTH, ATT_WIDTH))
    sg_out_g = gain(ks[9], (DEPTH, SG_WIDTH))
    w_out = n(ks[10], (DEPTH, MIX_WIDTH, D_MODEL), f32) * MIX_WIDTH ** -0.5
    pre_mix_g = gain(ks[11], (DEPTH, D_MODEL))
    post_mix_g = gain(ks[12], (DEPTH, D_MODEL))
    pre_ffn_g = gain(ks[13], (DEPTH, D_MODEL))
    post_ffn_g = gain(ks[14], (DEPTH, D_MODEL))
    w_ff1 = n(ks[15], (DEPTH, D_MODEL, D_FF), f32) * D_MODEL ** -0.5
    w_ff2 = n(ks[16], (DEPTH, D_FF, D_MODEL), f32) * D_FF ** -0.5
    ple_w = n(ks[17], (DEPTH, PLE_DIM, D_MODEL), f32) * PLE_DIM ** -0.5
    ple_gate_w = n(ks[18], (DEPTH, D_MODEL, D_MODEL), f32) * D_MODEL ** -0.5
    ple_gate_b = 0.02 * n(ks[19], (DEPTH, D_MODEL), f32)
    return {"x": x, "p": p, "w_in": w_in, "f_bias": f_bias,
            "sg_ln_g": sg_ln_g, "sg_ln_b": sg_ln_b, "sg_w": sg_w, "sg_b": sg_b,
            "att_out_g": att_out_g, "sg_out_g": sg_out_g, "w_out": w_out,
            "pre_mix_g": pre_mix_g, "post_mix_g": post_mix_g,
            "pre_ffn_g": pre_ffn_g, "post_ffn_g": post_ffn_g,
            "w_ff1": w_ff1, "w_ff2": w_ff2,
            "ple_w": ple_w, "ple_gate_w": ple_gate_w, "ple_gate_b": ple_gate_b}


def _fwd_reference(x, p, w_in, f_bias, sg_ln_g, sg_ln_b, sg_w, sg_b, att_out_g, sg_out_g,
              w_out, pre_mix_g, post_mix_g, pre_ffn_g, post_ffn_g, w_ff1, w_ff2,
              ple_w, ple_gate_w, ple_gate_b):
    B, S, _ = x.shape
    splits = [ATT_WIDTH, 2 * ATT_WIDTH, 3 * ATT_WIDTH,
              3 * ATT_WIDTH + N_ATT_HEADS, 3 * ATT_WIDTH + N_ATT_HEADS + SG_WIDTH]
    h = x
    for i in range(DEPTH):
        a = rms_norm(h, pre_mix_g[i])
        z = a @ w_in[i]
        q, k, v, f, u_sg, v_sg = jnp.split(z, splits, axis=-1)
        q = q.reshape(B, S, N_ATT_HEADS, HEAD_DIM)
        k = k.reshape(B, S, N_ATT_HEADS, HEAD_DIM)
        v = v.reshape(B, S, N_ATT_HEADS, HEAD_DIM)
        y_att = forgetting_attention(q, k, v, f + f_bias[i])
        y_sg = chunked_spatial_gating(u_sg, v_sg, sg_ln_g[i], sg_ln_b[i],
                                      sg_w[i], sg_b[i])
        y = jnp.concatenate([rms_norm(y_att, att_out_g[i]),
                             rms_norm(y_sg, sg_out_g[i])], axis=-1)
        h = h + rms_norm(y @ w_out[i], post_mix_g[i])
        c = rms_norm(h, pre_ffn_g[i])
        ff = jnp.square(jax.nn.relu(c @ w_ff1[i])) @ w_ff2[i]
        h = h + rms_norm(ff, post_ffn_g[i])
        gate = jax.nn.sigmoid(h @ ple_gate_w[i] + ple_gate_b[i])
        h = h + gate * (p[i] @ ple_w[i])
    return h


import jax as _jax
import jax.numpy as _jnp

TWIN_FORMAT = 'train_step'
FWD_PARAMS = ['x', 'p', 'w_in', 'f_bias', 'sg_ln_g', 'sg_ln_b', 'sg_w', 'sg_b', 'att_out_g', 'sg_out_g', 'w_out', 'pre_mix_g', 'post_mix_g', 'pre_ffn_g', 'post_ffn_g', 'w_ff1', 'w_ff2', 'ple_w', 'ple_gate_w', 'ple_gate_b']
TWIN_WEIGHTS = ['w_in', 'f_bias', 'sg_ln_g', 'sg_ln_b', 'sg_w', 'sg_b', 'att_out_g', 'sg_out_g', 'w_out', 'pre_mix_g', 'post_mix_g', 'pre_ffn_g', 'post_ffn_g', 'w_ff1', 'w_ff2', 'ple_w', 'ple_gate_w', 'ple_gate_b']
TWIN_DIFF_INPUT = 'x'
TWIN_INPUTS = ['x', 'p', 'w_in', 'f_bias', 'sg_ln_g', 'sg_ln_b', 'sg_w', 'sg_b', 'att_out_g', 'sg_out_g', 'w_out', 'pre_mix_g', 'post_mix_g', 'pre_ffn_g', 'post_ffn_g', 'w_ff1', 'w_ff2', 'ple_w', 'ple_gate_w', 'ple_gate_b', 'loss_target', 'm_w_in', 'm_f_bias', 'm_sg_ln_g', 'm_sg_ln_b', 'm_sg_w', 'm_sg_b', 'm_att_out_g', 'm_sg_out_g', 'm_w_out', 'm_pre_mix_g', 'm_post_mix_g', 'm_pre_ffn_g', 'm_post_ffn_g', 'm_w_ff1', 'm_w_ff2', 'm_ple_w', 'm_ple_gate_w', 'm_ple_gate_b', 'v_w_in', 'v_f_bias', 'v_sg_ln_g', 'v_sg_ln_b', 'v_sg_w', 'v_sg_b', 'v_att_out_g', 'v_sg_out_g', 'v_w_out', 'v_pre_mix_g', 'v_post_mix_g', 'v_pre_ffn_g', 'v_post_ffn_g', 'v_w_ff1', 'v_w_ff2', 'v_ple_w', 'v_ple_gate_w', 'v_ple_gate_b']
TWIN_OUTPUTS = ['loss', 'grad_x', 'grad_w_in', 'grad_f_bias', 'grad_sg_ln_g', 'grad_sg_ln_b', 'grad_sg_w', 'grad_sg_b', 'grad_att_out_g', 'grad_sg_out_g', 'grad_w_out', 'grad_pre_mix_g', 'grad_post_mix_g', 'grad_pre_ffn_g', 'grad_post_ffn_g', 'grad_w_ff1', 'grad_w_ff2', 'grad_ple_w', 'grad_ple_gate_w', 'grad_ple_gate_b', 'delta_w_in', 'delta_f_bias', 'delta_sg_ln_g', 'delta_sg_ln_b', 'delta_sg_w', 'delta_sg_b', 'delta_att_out_g', 'delta_sg_out_g', 'delta_w_out', 'delta_pre_mix_g', 'delta_post_mix_g', 'delta_pre_ffn_g', 'delta_post_ffn_g', 'delta_w_ff1', 'delta_w_ff2', 'delta_ple_w', 'delta_ple_gate_w', 'delta_ple_gate_b', 'new_m_w_in', 'new_m_f_bias', 'new_m_sg_ln_g', 'new_m_sg_ln_b', 'new_m_sg_w', 'new_m_sg_b', 'new_m_att_out_g', 'new_m_sg_out_g', 'new_m_w_out', 'new_m_pre_mix_g', 'new_m_post_mix_g', 'new_m_pre_ffn_g', 'new_m_post_ffn_g', 'new_m_w_ff1', 'new_m_w_ff2', 'new_m_ple_w', 'new_m_ple_gate_w', 'new_m_ple_gate_b', 'new_v_w_in', 'new_v_f_bias', 'new_v_sg_ln_g', 'new_v_sg_ln_b', 'new_v_sg_w', 'new_v_sg_b', 'new_v_att_out_g', 'new_v_sg_out_g', 'new_v_w_out', 'new_v_pre_mix_g', 'new_v_post_mix_g', 'new_v_pre_ffn_g', 'new_v_post_ffn_g', 'new_v_w_ff1', 'new_v_w_ff2', 'new_v_ple_w', 'new_v_ple_gate_w', 'new_v_ple_gate_b']
TWIN_LEAF_KINDS = {'loss': 'loss', 'grad_x': 'grad_x', 'grad_w_in': 'grad_w', 'grad_f_bias': 'grad_w', 'grad_sg_ln_g': 'grad_w', 'grad_sg_ln_b': 'grad_w', 'grad_sg_w': 'grad_w', 'grad_sg_b': 'grad_w', 'grad_att_out_g': 'grad_w', 'grad_sg_out_g': 'grad_w', 'grad_w_out': 'grad_w', 'grad_pre_mix_g': 'grad_w', 'grad_post_mix_g': 'grad_w', 'grad_pre_ffn_g': 'grad_w', 'grad_post_ffn_g': 'grad_w', 'grad_w_ff1': 'grad_w', 'grad_w_ff2': 'grad_w', 'grad_ple_w': 'grad_w', 'grad_ple_gate_w': 'grad_w', 'grad_ple_gate_b': 'grad_w', 'delta_w_in': 'delta_w', 'delta_f_bias': 'delta_w', 'delta_sg_ln_g': 'delta_w', 'delta_sg_ln_b': 'delta_w', 'delta_sg_w': 'delta_w', 'delta_sg_b': 'delta_w', 'delta_att_out_g': 'delta_w', 'delta_sg_out_g': 'delta_w', 'delta_w_out': 'delta_w', 'delta_pre_mix_g': 'delta_w', 'delta_post_mix_g': 'delta_w', 'delta_pre_ffn_g': 'delta_w', 'delta_post_ffn_g': 'delta_w', 'delta_w_ff1': 'delta_w', 'delta_w_ff2': 'delta_w', 'delta_ple_w': 'delta_w', 'delta_ple_gate_w': 'delta_w', 'delta_ple_gate_b': 'delta_w', 'new_m_w_in': 'new_m', 'new_m_f_bias': 'new_m', 'new_m_sg_ln_g': 'new_m', 'new_m_sg_ln_b': 'new_m', 'new_m_sg_w': 'new_m', 'new_m_sg_b': 'new_m', 'new_m_att_out_g': 'new_m', 'new_m_sg_out_g': 'new_m', 'new_m_w_out': 'new_m', 'new_m_pre_mix_g': 'new_m', 'new_m_post_mix_g': 'new_m', 'new_m_pre_ffn_g': 'new_m', 'new_m_post_ffn_g': 'new_m', 'new_m_w_ff1': 'new_m', 'new_m_w_ff2': 'new_m', 'new_m_ple_w': 'new_m', 'new_m_ple_gate_w': 'new_m', 'new_m_ple_gate_b': 'new_m', 'new_v_w_in': 'new_v', 'new_v_f_bias': 'new_v', 'new_v_sg_ln_g': 'new_v', 'new_v_sg_ln_b': 'new_v', 'new_v_sg_w': 'new_v', 'new_v_sg_b': 'new_v', 'new_v_att_out_g': 'new_v', 'new_v_sg_out_g': 'new_v', 'new_v_w_out': 'new_v', 'new_v_pre_mix_g': 'new_v', 'new_v_post_mix_g': 'new_v', 'new_v_pre_ffn_g': 'new_v', 'new_v_post_ffn_g': 'new_v', 'new_v_w_ff1': 'new_v', 'new_v_w_ff2': 'new_v', 'new_v_ple_w': 'new_v', 'new_v_ple_gate_w': 'new_v', 'new_v_ple_gate_b': 'new_v'}


def _forward(args):
    return _fwd_reference(*[args[k] for k in FWD_PARAMS])


def _output_shape():
    def fwd():
        inp = _fwd_setup_inputs(0)
        return _fwd_reference(*[inp[k] for k in FWD_PARAMS])
    out = _jax.eval_shape(fwd)
    return out.shape, out.dtype

N_MICROBATCH = 1
ADAM_LR = 0.001
ADAM_B1 = 0.9
ADAM_B2 = 0.999
ADAM_EPS = 1e-08
ADAM_WD = 0.01
ADAM_STEP = 10
PER_EXAMPLE_BATCH_AXIS = {'x': 0, 'p': 1, 'loss_target': 0}
SHARED_INPUTS = []
_WEIGHT_DTYPES = {'w_in': _jnp.float32, 'f_bias': _jnp.float32, 'sg_ln_g': _jnp.float32, 'sg_ln_b': _jnp.float32, 'sg_w': _jnp.float32, 'sg_b': _jnp.float32, 'att_out_g': _jnp.float32, 'sg_out_g': _jnp.float32, 'w_out': _jnp.float32, 'pre_mix_g': _jnp.float32, 'post_mix_g': _jnp.float32, 'pre_ffn_g': _jnp.float32, 'post_ffn_g': _jnp.float32, 'w_ff1': _jnp.float32, 'w_ff2': _jnp.float32, 'ple_w': _jnp.float32, 'ple_gate_w': _jnp.float32, 'ple_gate_b': _jnp.float32}
MOMENT_SCALE = {'w_in': 7.688897e-01, 'f_bias': 3.136409e+00, 'sg_ln_g': 4.464528e-01, 'sg_ln_b': 5.332499e-01, 'sg_w': 2.455234e-01, 'sg_b': 5.040168e-01, 'att_out_g': 1.969371e+00, 'sg_out_g': 1.027345e+01, 'w_out': 7.103927e+00, 'pre_mix_g': 1.269907e+00, 'post_mix_g': 6.588852e+01, 'pre_ffn_g': 2.575018e+00, 'post_ffn_g': 6.721574e+01, 'w_ff1': 1.235269e+00, 'w_ff2': 7.038766e+00, 'ple_w': 1.025286e+00, 'ple_gate_w': 2.455522e+00, 'ple_gate_b': 5.303998e+00}


def _to_microbatches(a, axis):
    t = _jnp.moveaxis(a, axis, 0)
    t = t.reshape((N_MICROBATCH, t.shape[0] // N_MICROBATCH) + t.shape[1:])
    return _jnp.moveaxis(t, 1, axis + 1)


def setup_inputs(seed: int = 0) -> dict:
    inp = _fwd_setup_inputs(seed)
    key = _jax.random.fold_in(_jax.random.key(seed), 7919)
    shape, _ = _output_shape()
    out = dict(inp)
    out["loss_target"] = _jax.random.normal(_jax.random.fold_in(key, 0), shape, _jnp.float32)
    for i, name in enumerate(TWIN_WEIGHTS):
        w = inp[name].astype(_jnp.float32)
        if MOMENT_SCALE is None:
            s = _jnp.sqrt(_jnp.mean(_jnp.square(w)) + 1e-30)
        else:
            s = MOMENT_SCALE[name]
        km, kv = _jax.random.split(_jax.random.fold_in(key, i + 1))
        out[name] = w
        out["m_" + name] = s * _jax.random.normal(km, w.shape, _jnp.float32)
        out["v_" + name] = (s * s) * _jax.random.uniform(kv, w.shape, _jnp.float32, 0.5, 1.5)
    if N_MICROBATCH > 1:
        for name, axis in PER_EXAMPLE_BATCH_AXIS.items():
            out[name] = _to_microbatches(out[name], axis)
    return {'x': out['x'], 'p': out['p'], 'w_in': out['w_in'], 'f_bias': out['f_bias'], 'sg_ln_g': out['sg_ln_g'], 'sg_ln_b': out['sg_ln_b'], 'sg_w': out['sg_w'], 'sg_b': out['sg_b'], 'att_out_g': out['att_out_g'], 'sg_out_g': out['sg_out_g'], 'w_out': out['w_out'], 'pre_mix_g': out['pre_mix_g'], 'post_mix_g': out['post_mix_g'], 'pre_ffn_g': out['pre_ffn_g'], 'post_ffn_g': out['post_ffn_g'], 'w_ff1': out['w_ff1'], 'w_ff2': out['w_ff2'], 'ple_w': out['ple_w'], 'ple_gate_w': out['ple_gate_w'], 'ple_gate_b': out['ple_gate_b'], 'loss_target': out['loss_target'], 'm_w_in': out['m_w_in'], 'm_f_bias': out['m_f_bias'], 'm_sg_ln_g': out['m_sg_ln_g'], 'm_sg_ln_b': out['m_sg_ln_b'], 'm_sg_w': out['m_sg_w'], 'm_sg_b': out['m_sg_b'], 'm_att_out_g': out['m_att_out_g'], 'm_sg_out_g': out['m_sg_out_g'], 'm_w_out': out['m_w_out'], 'm_pre_mix_g': out['m_pre_mix_g'], 'm_post_mix_g': out['m_post_mix_g'], 'm_pre_ffn_g': out['m_pre_ffn_g'], 'm_post_ffn_g': out['m_post_ffn_g'], 'm_w_ff1': out['m_w_ff1'], 'm_w_ff2': out['m_w_ff2'], 'm_ple_w': out['m_ple_w'], 'm_ple_gate_w': out['m_ple_gate_w'], 'm_ple_gate_b': out['m_ple_gate_b'], 'v_w_in': out['v_w_in'], 'v_f_bias': out['v_f_bias'], 'v_sg_ln_g': out['v_sg_ln_g'], 'v_sg_ln_b': out['v_sg_ln_b'], 'v_sg_w': out['v_sg_w'], 'v_sg_b': out['v_sg_b'], 'v_att_out_g': out['v_att_out_g'], 'v_sg_out_g': out['v_sg_out_g'], 'v_w_out': out['v_w_out'], 'v_pre_mix_g': out['v_pre_mix_g'], 'v_post_mix_g': out['v_post_mix_g'], 'v_pre_ffn_g': out['v_pre_ffn_g'], 'v_post_ffn_g': out['v_post_ffn_g'], 'v_w_ff1': out['v_w_ff1'], 'v_w_ff2': out['v_w_ff2'], 'v_ple_w': out['v_ple_w'], 'v_ple_gate_w': out['v_ple_gate_w'], 'v_ple_gate_b': out['v_ple_gate_b']}


def _loss(weights, diff, rest, loss_target):
    with _jax.named_scope("forward"):
        args = {**rest, TWIN_DIFF_INPUT: diff, **{k: w.astype(_WEIGHT_DTYPES[k]) for k, w in weights.items()}}
        y = _forward(args)
    with _jax.named_scope("loss_head"):
        err = _jnp.square(y.astype(_jnp.float32) - loss_target)
        return 0.5 * _jnp.sum(_jnp.mean(err, axis=-1)) if err.ndim else 0.5 * err


def _adamw(w, g, m, v):
    m = ADAM_B1 * m + (1.0 - ADAM_B1) * g
    v = ADAM_B2 * v + (1.0 - ADAM_B2) * _jnp.square(g)
    m_hat = m / (1.0 - ADAM_B1 ** ADAM_STEP)
    v_hat = v / (1.0 - ADAM_B2 ** ADAM_STEP)
    delta = -ADAM_LR * (m_hat / (_jnp.sqrt(v_hat) + ADAM_EPS) + ADAM_WD * w)
    return delta, m, v


def reference(x, p, w_in, f_bias, sg_ln_g, sg_ln_b, sg_w, sg_b, att_out_g, sg_out_g, w_out, pre_mix_g, post_mix_g, pre_ffn_g, post_ffn_g, w_ff1, w_ff2, ple_w, ple_gate_w, ple_gate_b, loss_target, m_w_in, m_f_bias, m_sg_ln_g, m_sg_ln_b, m_sg_w, m_sg_b, m_att_out_g, m_sg_out_g, m_w_out, m_pre_mix_g, m_post_mix_g, m_pre_ffn_g, m_post_ffn_g, m_w_ff1, m_w_ff2, m_ple_w, m_ple_gate_w, m_ple_gate_b, v_w_in, v_f_bias, v_sg_ln_g, v_sg_ln_b, v_sg_w, v_sg_b, v_att_out_g, v_sg_out_g, v_w_out, v_pre_mix_g, v_post_mix_g, v_pre_ffn_g, v_post_ffn_g, v_w_ff1, v_w_ff2, v_ple_w, v_ple_gate_w, v_ple_gate_b):
    given = dict(x=x, p=p, w_in=w_in, f_bias=f_bias, sg_ln_g=sg_ln_g, sg_ln_b=sg_ln_b, sg_w=sg_w, sg_b=sg_b, att_out_g=att_out_g, sg_out_g=sg_out_g, w_out=w_out, pre_mix_g=pre_mix_g, post_mix_g=post_mix_g, pre_ffn_g=pre_ffn_g, post_ffn_g=post_ffn_g, w_ff1=w_ff1, w_ff2=w_ff2, ple_w=ple_w, ple_gate_w=ple_gate_w, ple_gate_b=ple_gate_b, loss_target=loss_target, m_w_in=m_w_in, m_f_bias=m_f_bias, m_sg_ln_g=m_sg_ln_g, m_sg_ln_b=m_sg_ln_b, m_sg_w=m_sg_w, m_sg_b=m_sg_b, m_att_out_g=m_att_out_g, m_sg_out_g=m_sg_out_g, m_w_out=m_w_out, m_pre_mix_g=m_pre_mix_g, m_post_mix_g=m_post_mix_g, m_pre_ffn_g=m_pre_ffn_g, m_post_ffn_g=m_post_ffn_g, m_w_ff1=m_w_ff1, m_w_ff2=m_w_ff2, m_ple_w=m_ple_w, m_ple_gate_w=m_ple_gate_w, m_ple_gate_b=m_ple_gate_b, v_w_in=v_w_in, v_f_bias=v_f_bias, v_sg_ln_g=v_sg_ln_g, v_sg_ln_b=v_sg_ln_b, v_sg_w=v_sg_w, v_sg_b=v_sg_b, v_att_out_g=v_att_out_g, v_sg_out_g=v_sg_out_g, v_w_out=v_w_out, v_pre_mix_g=v_pre_mix_g, v_post_mix_g=v_post_mix_g, v_pre_ffn_g=v_pre_ffn_g, v_post_ffn_g=v_post_ffn_g, v_w_ff1=v_w_ff1, v_w_ff2=v_w_ff2, v_ple_w=v_ple_w, v_ple_gate_w=v_ple_gate_w, v_ple_gate_b=v_ple_gate_b)
    weights = {n: given[n] for n in TWIN_WEIGHTS}
    shared = {n: given[n] for n in SHARED_INPUTS}
    per_example = {n: given[n] for n in ['x', 'p']}
    grad_fn = _jax.value_and_grad(_loss, argnums=(0, 1))

    def one_microbatch(ex, loss_target):
        ex = dict(ex)
        diff = ex.pop(TWIN_DIFF_INPUT)
        return grad_fn(weights, diff, {**shared, **ex}, loss_target)

    if N_MICROBATCH == 1:
        loss, (grad_w, grad_x) = one_microbatch(per_example, given["loss_target"])
    else:
        def body(carry, xs):
            loss_sum, grad_sum = carry
            l_k, (gw_k, gx_k) = one_microbatch(xs[0], xs[1])
            with _jax.named_scope("update"):
                return (loss_sum + l_k, _jax.tree.map(_jnp.add, grad_sum, gw_k)), gx_k

        init = (_jnp.zeros((), _jnp.float32), _jax.tree.map(_jnp.zeros_like, weights))
        (loss, grad_w), grad_x = _jax.lax.scan(body, init, (per_example, given["loss_target"]))
    with _jax.named_scope("update"):
        delta_w, new_m, new_v = {}, {}, {}
        for n in TWIN_WEIGHTS:
            delta_w[n], new_m[n], new_v[n] = _adamw(weights[n], grad_w[n], given["m_" + n], given["v_" + n])
    return (loss, grad_x, *[grad_w[n] for n in TWIN_WEIGHTS], *[delta_w[n] for n in TWIN_WEIGHTS],
            *[new_m[n] for n in TWIN_WEIGHTS], *[new_v[n] for n in TWIN_WEIGHTS])
```

```python
import jax
import jax.numpy as jnp
from jax import lax
from jax.experimental import pallas as pl
from jax.experimental.pallas import tpu as pltpu

F32 = jnp.float32
BF16 = jnp.bfloat16

NDEV = 8
D = 1024
AW = 512
SW = 512
HD = 64
CH = 128
DFF = 4096
PLE = 256
NQKV = 3 * AW
NUS = 2 * SW
FPAD = 128
ZP = NQKV + NUS + FPAD
W_IN_COLS = 2568
EPS = 1e-6
MASKV = -1e30
GELU_K = 0.7978845608028654
GELU_C = 0.044715

ADAM_LR = 0.001
ADAM_B1 = 0.9
ADAM_B2 = 0.999
ADAM_EPS = 1e-08
ADAM_WD = 0.01
ADAM_STEP = 10

VMEM_LIMIT = 48 * 1024 * 1024


def _nn(a, b):
    return jnp.dot(a, b, preferred_element_type=F32)


def _nt(a, b):
    return lax.dot_general(a, b, (((1,), (1,)), ((), ())), preferred_element_type=F32)


def _tn(a, b):
    return lax.dot_general(a, b, (((0,), (0,)), ((), ())), preferred_element_type=F32)


def _tile(n, pref):
    return min(n, pref)


def _params(n_axes):
    return pltpu.CompilerParams(dimension_semantics=("arbitrary",) * n_axes, vmem_limit_bytes=VMEM_LIMIT)


def _full(shape):
    nd = len(shape)
    return pl.BlockSpec(shape, lambda *_: (0,) * nd)


def _rms_fwd(x, g):
    r = lax.rsqrt(jnp.mean(x * x, axis=-1, keepdims=True) + EPS)
    return x * r * g


def _rms_bwd(dy, x, g):
    n = x.shape[-1]
    r = lax.rsqrt(jnp.mean(x * x, axis=-1, keepdims=True) + EPS)
    u = dy * g
    s = jnp.sum(x * u, axis=-1, keepdims=True)
    dx = r * u - x * (r * r * r * (s * (1.0 / n)))
    dg = jnp.sum(dy * (x * r), axis=0, keepdims=True)
    return dx, dg


def _gelu(x):
    t = jnp.tanh(GELU_K * (x + GELU_C * (x * x * x)))
    return x * (0.5 * (1.0 + t)), t


def _gelu_grad(x, t):
    return 0.5 * (1.0 + t) + 0.5 * x * (1.0 - t * t) * (GELU_K * (1.0 + 3.0 * GELU_C * x * x))


def _xchg(arrs, gather, name):
    n = len(arrs)
    out_shapes = [jax.ShapeDtypeStruct(((NDEV,) + a.shape) if gather else a.shape, a.dtype) for a in arrs]

    def body(*refs):
        ins, outs = refs[:n], refs[n:2 * n]
        send, recv, loc = refs[2 * n:]
        x, y, c = lax.axis_index("x"), lax.axis_index("y"), lax.axis_index("c")
        me = 4 * x + 2 * y + c
        local = []
        for a in range(n):
            cp = pltpu.make_async_copy(ins[a] if gather else ins[a].at[me], outs[a].at[me], loc.at[a])
            cp.start()
            local.append(cp)
        remote = []
        for k in range(1, NDEV):
            px = 1 - x if (k >> 2) & 1 else x
            py = 1 - y if (k >> 1) & 1 else y
            pc = 1 - c if k & 1 else c
            peer = 4 * px + 2 * py + pc
            for a in range(n):
                cp = pltpu.make_async_remote_copy(
                    src_ref=ins[a] if gather else ins[a].at[peer],
                    dst_ref=outs[a].at[me],
                    send_sem=send.at[a, k - 1],
                    recv_sem=recv.at[a, k - 1],
                    device_id=(px, py, pc),
                    device_id_type=pl.DeviceIdType.MESH,
                )
                cp.start()
                remote.append(cp)
        for cp in remote:
            cp.wait_send()
            cp.wait_recv()
        for cp in local:
            cp.wait()

    hbm = pl.BlockSpec(memory_space=pltpu.HBM)
    return pl.pallas_call(
        body,
        name=name,
        out_shape=out_shapes,
        in_specs=[hbm] * n,
        out_specs=[hbm] * n,
        scratch_shapes=[
            pltpu.SemaphoreType.DMA((n, NDEV - 1)),
            pltpu.SemaphoreType.DMA((n, NDEV - 1)),
            pltpu.SemaphoreType.DMA((n,)),
        ],
    )(*arrs)


def _inproj(x, g, wp):
    T = x.shape[0]
    tm = _tile(T, 512)

    def body(x_ref, g_ref, w_ref, qkv_ref, us_ref, fz_ref, ab_ref):
        a = _rms_fwd(x_ref[...], g_ref[...]).astype(BF16)
        ab_ref[...] = a
        qkv_ref[...] = _nn(a, w_ref[:, 0:NQKV]).astype(BF16)
        us_ref[...] = _nn(a, w_ref[:, NQKV:NQKV + NUS])
        fz_ref[...] = _nn(a, w_ref[:, NQKV + NUS:ZP])

    row = lambda n: pl.BlockSpec((tm, n), lambda i: (i, 0))
    return pl.pallas_call(
        body,
        name="inproj",
        grid=(T // tm,),
        in_specs=[row(D), _full((1, D)), _full((D, ZP))],
        out_specs=[row(NQKV), row(NUS), row(FPAD), row(D)],
        out_shape=[
            jax.ShapeDtypeStruct((T, NQKV), BF16),
            jax.ShapeDtypeStruct((T, NUS), F32),
            jax.ShapeDtypeStruct((T, FPAD), F32),
            jax.ShapeDtypeStruct((T, D), BF16),
        ],
        compiler_params=_params(1),
    )(x, g, wp)


def _log_sigmoid(z):
    return jnp.minimum(z, 0.0) - jnp.log1p(jnp.exp(-jnp.abs(z)))


def _fcum(fz, fb):
    T = fz.shape[0]
    tb = _tile(T, 512)

    def body(fz_ref, fb_ref, c_ref, ct_ref, carry):
        @pl.when(pl.program_id(0) == 0)
        def _():
            carry[...] = jnp.zeros_like(carry)

        lf = _log_sigmoid(fz_ref[...] + fb_ref[...])
        r = lax.broadcasted_iota(jnp.int32, (tb, tb), 0)
        cc = lax.broadcasted_iota(jnp.int32, (tb, tb), 1)
        tri = (cc <= r).astype(F32)
        cs = jnp.dot(tri, lf, precision=lax.Precision.HIGHEST, preferred_element_type=F32) + carry[...]
        c_ref[...] = cs
        ct_ref[...] = cs.T[0:8, :]
        carry[...] = carry[...] + jnp.sum(lf, axis=0, keepdims=True)

    return pl.pallas_call(
        body,
        name="fcum",
        grid=(T // tb,),
        in_specs=[pl.BlockSpec((tb, FPAD), lambda i: (i, 0)), _full((1, FPAD))],
        out_specs=[pl.BlockSpec((tb, FPAD), lambda i: (i, 0)), pl.BlockSpec((8, tb), lambda i: (0, i))],
        out_shape=[jax.ShapeDtypeStruct((T, FPAD), F32), jax.ShapeDtypeStruct((8, T), F32)],
        scratch_shapes=[pltpu.VMEM((1, FPAD), F32)],
        compiler_params=_params(1),
    )(fz, fb)


def _head_bias(ctile, cttile, h):
    lane = lax.broadcasted_iota(jnp.int32, ctile.shape, 1)
    sub = lax.broadcasted_iota(jnp.int32, cttile.shape, 0)
    cq = jnp.sum(jnp.where(lane == h, ctile, 0.0), axis=1, keepdims=True)
    ck = jnp.sum(jnp.where(sub == h, cttile, 0.0), axis=0, keepdims=True)
    return cq, ck


def _attn_fwd(qkv, c, ct):
    T = qkv.shape[0]
    tq = _tile(T, 512)
    tk = tq
    nq = T // tq

    def body(q_ref, k_ref, v_ref, c_ref, ct_ref, o_ref, lse_ref, m_s, l_s, acc_s):
        hp, i, j = pl.program_id(0), pl.program_id(1), pl.program_id(2)
        is_a = lax.broadcasted_iota(jnp.int32, (1, 128), 1) < HD

        @pl.when(j == 0)
        def _():
            m_s[...] = jnp.full_like(m_s, MASKV)
            l_s[...] = jnp.zeros_like(l_s)
            acc_s[...] = jnp.zeros_like(acc_s)

        @pl.when(j <= i)
        def _():
            q2, k2, v2 = q_ref[...], k_ref[...], v_ref[...]
            rows = i * tq + lax.broadcasted_iota(jnp.int32, (tq, tk), 0)
            cols = j * tk + lax.broadcasted_iota(jnp.int32, (tq, tk), 1)
            causal = cols <= rows
            alphas, pvs = [], []
            for hh in range(2):
                sel = is_a if hh == 0 else jnp.logical_not(is_a)
                qh = jnp.where(sel, q2, jnp.zeros_like(q2)) * 0.125
                cq, ck = _head_bias(c_ref[...], ct_ref[...], 2 * hp + hh)
                s = _nt(qh, k2) + (cq - ck)
                s = jnp.where(causal, s, MASKV)
                m_prev = m_s[hh]
                m_new = jnp.maximum(m_prev, jnp.max(s, axis=1, keepdims=True))
                pr = jnp.exp(s - m_new)
                alpha = jnp.exp(m_prev - m_new)
                l_s[hh] = alpha * l_s[hh] + jnp.sum(pr, axis=1, keepdims=True)
                m_s[hh] = m_new
                alphas.append(alpha)
                pvs.append(_nn(pr.astype(BF16), v2))
            acc_s[...] = jnp.where(is_a, alphas[0], alphas[1]) * acc_s[...] + jnp.where(is_a, pvs[0], pvs[1])

        @pl.when(j == i)
        def _():
            o_ref[...] = acc_s[...] * jnp.where(is_a, 1.0 / l_s[0], 1.0 / l_s[1])
            lse_ref[...] = jnp.where(is_a, m_s[0] + jnp.log(l_s[0]), m_s[1] + jnp.log(l_s[1]))

    kv = lambda base: pl.BlockSpec((tk, 128), lambda hp, i, j: (jnp.minimum(j, i), base + hp))
    return pl.pallas_call(
        body,
        name="attn_fwd",
        grid=(4, nq, nq),
        in_specs=[
            pl.BlockSpec((tq, 128), lambda hp, i, j: (i, hp)),
            kv(4),
            kv(8),
            pl.BlockSpec((tq, FPAD), lambda hp, i, j: (i, 0)),
            pl.BlockSpec((8, tk), lambda hp, i, j: (0, jnp.minimum(j, i))),
        ],
        out_specs=[pl.BlockSpec((tq, 128), lambda hp, i, j: (i, hp))] * 2,
        out_shape=[jax.ShapeDtypeStruct((T, AW), F32)] * 2,
        scratch_shapes=[pltpu.VMEM((2, tq, 1), F32), pltpu.VMEM((2, tq, 1), F32), pltpu.VMEM((tq, 128), F32)],
        compiler_params=_params(3),
    )(qkv, qkv, qkv, c, ct)


def _sgu_forward(us_ref, lng, lnb, w_ref, bt_ref, mixed_s, vnb_s, tm):
    is_a = lax.broadcasted_iota(jnp.int32, (1, 128), 1) < HD
    u = us_ref[:, 0:SW]
    vs = us_ref[:, SW:NUS]
    ug, tu = _gelu(u)
    vg, tv = _gelu(vs)
    mu = jnp.mean(vg, axis=-1, keepdims=True)
    xc = vg - mu
    rstd = lax.rsqrt(jnp.mean(xc * xc, axis=-1, keepdims=True) + EPS)
    vhat = xc * rstd
    vnb_s[...] = (vhat * lng + lnb).astype(BF16)
    rr = lax.broadcasted_iota(jnp.int32, (CH, CH), 0)
    cc = lax.broadcasted_iota(jnp.int32, (CH, CH), 1)
    tril = cc <= rr
    for jj in range(4):
        wa = jnp.where(tril, w_ref[2 * jj], 0.0).astype(BF16)
        wb = jnp.where(tril, w_ref[2 * jj + 1], 0.0).astype(BF16)
        ba = bt_ref[:, 2 * jj:2 * jj + 1]
        bb = bt_ref[:, 2 * jj + 1:2 * jj + 2]
        for ch in range(tm // CH):
            rs, cs = slice(ch * CH, (ch + 1) * CH), slice(jj * 128, (jj + 1) * 128)
            vn2 = vnb_s[rs, cs]
            mixed_s[rs, cs] = jnp.where(is_a, _nn(wa, vn2) + ba, _nn(wb, vn2) + bb)
    mixed = mixed_s[...]
    return u, vs, ug, tu, tv, vhat, rstd, mixed, ug * mixed


def _sgu_out(us, yatt, x, lng, lnb, sgw, sgbt, gatt, gsg, wout, gpm):
    T = us.shape[0]
    tm = _tile(T, 256)

    def body(us_ref, ya_ref, x_ref, lng_ref, lnb_ref, w_ref, bt_ref, ga_ref, gs_ref, wo_ref, gp_ref,
             h1_ref, yb_ref, o_ref, mixed_s, vnb_s):
        ysg = _sgu_forward(us_ref, lng_ref[...], lnb_ref[...], w_ref, bt_ref, mixed_s, vnb_s, tm)[-1]
        yb_ref[:, 0:AW] = _rms_fwd(ya_ref[...], ga_ref[...]).astype(BF16)
        yb_ref[:, AW:D] = _rms_fwd(ysg, gs_ref[...]).astype(BF16)
        o = _nn(yb_ref[...], wo_ref[...])
        o_ref[...] = o
        h1_ref[...] = x_ref[...] + _rms_fwd(o, gp_ref[...])

    row = lambda n: pl.BlockSpec((tm, n), lambda i: (i, 0))
    return pl.pallas_call(
        body,
        name="sgu_out",
        grid=(T // tm,),
        in_specs=[row(NUS), row(AW), row(D), _full((1, SW)), _full((1, SW)), _full((8, CH, CH)), _full((CH, 8)),
                  _full((1, AW)), _full((1, SW)), _full((D, D)), _full((1, D))],
        out_specs=[row(D), row(D), row(D)],
        out_shape=[jax.ShapeDtypeStruct((T, D), F32), jax.ShapeDtypeStruct((T, D), BF16), jax.ShapeDtypeStruct((T, D), F32)],
        scratch_shapes=[pltpu.VMEM((tm, SW), F32), pltpu.VMEM((tm, SW), BF16)],
        compiler_params=_params(1),
    )(us, yatt, x, lng, lnb, sgw, sgbt, gatt, gsg, wout, gpm)


def _ffn_fwd(h1, gpre, w1g, w2g, gpost):
    T = h1.shape[0]
    tm = _tile(T, 512)
    nb, hb = w1g.shape[0], w1g.shape[2]

    def body(h1_ref, gpre_ref, w1_ref, w2_ref, gpost_ref, c2_ref, f1_ref, act_ref, ff_ref, h2_ref, h2b_ref, acc_s):
        j = pl.program_id(1)

        @pl.when(j == 0)
        def _():
            c2_ref[...] = _rms_fwd(h1_ref[...], gpre_ref[...]).astype(BF16)
            acc_s[...] = jnp.zeros_like(acc_s)

        f1 = _nn(c2_ref[...], w1_ref[...])
        f1_ref[...] = f1
        r = jnp.maximum(f1, 0.0)
        act = (r * r).astype(BF16)
        act_ref[...] = act
        acc_s[...] += _nn(act, w2_ref[...])

        @pl.when(j == nb - 1)
        def _():
            ff = acc_s[...]
            ff_ref[...] = ff
            h2 = h1_ref[...] + _rms_fwd(ff, gpost_ref[...])
            h2_ref[...] = h2
            h2b_ref[...] = h2.astype(BF16)

    row = lambda n: pl.BlockSpec((tm, n), lambda i, j: (i, 0))
    return pl.pallas_call(
        body,
        name="ffn_fwd",
        grid=(T // tm, nb),
        in_specs=[row(D), _full((1, D)), pl.BlockSpec((None, D, hb), lambda i, j: (j, 0, 0)),
                  pl.BlockSpec((None, hb, D), lambda i, j: (j, 0, 0)), _full((1, D))],
        out_specs=[row(D), pl.BlockSpec((tm, hb), lambda i, j: (i, j)), pl.BlockSpec((tm, hb), lambda i, j: (i, j)),
                   row(D), row(D), row(D)],
        out_shape=[jax.ShapeDtypeStruct((T, D), BF16), jax.ShapeDtypeStruct((T, DFF), F32),
                   jax.ShapeDtypeStruct((T, DFF), BF16), jax.ShapeDtypeStruct((T, D), F32),
                   jax.ShapeDtypeStruct((T, D), F32), jax.ShapeDtypeStruct((T, D), BF16)],
        scratch_shapes=[pltpu.VMEM((tm, D), F32)],
        compiler_params=_params(2),
    )(h1, gpre, w1g, w2g, gpost)


def _ple_loss(h2, p, tgt, wg, bg, wpe):
    T = h2.shape[0]
    tm = _tile(T, 512)

    def body(h2_ref, p_ref, t_ref, wg_ref, bg_ref, wpe_ref, dh2_ref, dpre_ref, dpe_ref, pb_ref, dbg_ref, loss_ref):
        @pl.when(pl.program_id(0) == 0)
        def _():
            dbg_ref[...] = jnp.zeros_like(dbg_ref)
            loss_ref[...] = jnp.zeros_like(loss_ref)

        h2 = h2_ref[...]
        gate = jax.nn.sigmoid(_nn(h2.astype(BF16), wg_ref[...]) + bg_ref[...])
        pb = p_ref[...].astype(BF16)
        pb_ref[...] = pb
        pe = _nn(pb, wpe_ref[...])
        diff = (h2 + gate * pe) - t_ref[...]
        loss_ref[...] += jnp.sum(diff * diff)
        dh3 = diff * (1.0 / D)
        dpre = (dh3 * pe) * (gate * (1.0 - gate))
        dpre_b = dpre.astype(BF16)
        dpre_ref[...] = dpre_b
        dpe_ref[...] = (dh3 * gate).astype(BF16)
        dbg_ref[...] += jnp.sum(dpre, axis=0, keepdims=True)
        dh2_ref[...] = dh3 + _nt(dpre_b, wg_ref[...])

    row = lambda n: pl.BlockSpec((tm, n), lambda i: (i, 0))
    return pl.pallas_call(
        body,
        name="ple_loss",
        grid=(T // tm,),
        in_specs=[row(D), row(PLE), row(D), _full((D, D)), _full((1, D)), _full((PLE, D))],
        out_specs=[row(D), row(D), row(D), row(PLE), _full((1, D)), _full((8, 128))],
        out_shape=[jax.ShapeDtypeStruct((T, D), F32), jax.ShapeDtypeStruct((T, D), BF16),
                   jax.ShapeDtypeStruct((T, D), BF16), jax.ShapeDtypeStruct((T, PLE), BF16),
                   jax.ShapeDtypeStruct((1, D), F32), jax.ShapeDtypeStruct((8, 128), F32)],
        compiler_params=_params(1),
    )(h2, p, tgt, wg, bg, wpe)


def _ffn_bwd(dh2, ff, h1, f1, w1g, w2g, gpost, gpre):
    T = dh2.shape[0]
    tm = _tile(T, 512)
    nb, hb = w1g.shape[0], w1g.shape[2]

    def body(dh2_ref, ff_ref, h1_ref, f1_ref, w1_ref, w2_ref, gpost_ref, gpre_ref,
             dffb_ref, df1_ref, dh1_ref, dgpost_ref, dgpre_ref, acc_s):
        i, j = pl.program_id(0), pl.program_id(1)

        @pl.when((i == 0) & (j == 0))
        def _():
            dgpost_ref[...] = jnp.zeros_like(dgpost_ref)
            dgpre_ref[...] = jnp.zeros_like(dgpre_ref)

        @pl.when(j == 0)
        def _():
            dff, dg = _rms_bwd(dh2_ref[...], ff_ref[...], gpost_ref[...])
            dffb_ref[...] = dff.astype(BF16)
            dgpost_ref[...] += dg
            acc_s[...] = jnp.zeros_like(acc_s)

        dact = _nt(dffb_ref[...], w2_ref[...])
        df1 = (dact * (2.0 * jnp.maximum(f1_ref[...], 0.0))).astype(BF16)
        df1_ref[...] = df1
        acc_s[...] += _nt(df1, w1_ref[...])

        @pl.when(j == nb - 1)
        def _():
            dx, dg = _rms_bwd(acc_s[...], h1_ref[...], gpre_ref[...])
            dh1_ref[...] = dh2_ref[...] + dx
            dgpre_ref[...] += dg

    row = lambda n: pl.BlockSpec((tm, n), lambda i, j: (i, 0))
    blk = pl.BlockSpec((tm, hb), lambda i, j: (i, j))
    return pl.pallas_call(
        body,
        name="ffn_bwd",
        grid=(T // tm, nb),
        in_specs=[row(D), row(D), row(D), blk, pl.BlockSpec((None, D, hb), lambda i, j: (j, 0, 0)),
                  pl.BlockSpec((None, hb, D), lambda i, j: (j, 0, 0)), _full((1, D)), _full((1, D))],
        out_specs=[row(D), blk, row(D), _full((1, D)), _full((1, D))],
        out_shape=[jax.ShapeDtypeStruct((T, D), BF16), jax.ShapeDtypeStruct((T, DFF), BF16),
                   jax.ShapeDtypeStruct((T, D), F32), jax.ShapeDtypeStruct((1, D), F32),
                   jax.ShapeDtypeStruct((1, D), F32)],
        scratch_shapes=[pltpu.VMEM((tm, D), F32)],
        compiler_params=_params(2),
    )(dh2, ff, h1, f1, w1g, w2g, gpost, gpre)


def _mix_bwd(dh1, o, us, yatt, lng, lnb, sgw, sgwt, sgbt, gatt, gsg, wout, gpm):
    T = dh1.shape[0]
    tm = _tile(T, 256)

    def body(dh1_ref, o_ref, us_ref, ya_ref, lng_ref, lnb_ref, w_ref, wt_ref, bt_ref, ga_ref, gs_ref, wo_ref, gp_ref,
             dob_ref, dya_ref, dus_ref, dw_ref, dbt_ref, dlng_ref, dlnb_ref, dga_ref, dgs_ref, dgp_ref,
             mixed_s, vnb_s, dvn_s):
        @pl.when(pl.program_id(0) == 0)
        def _():
            for r in (dw_ref, dbt_ref, dlng_ref, dlnb_ref, dga_ref, dgs_ref, dgp_ref):
                r[...] = jnp.zeros_like(r)

        is_a = lax.broadcasted_iota(jnp.int32, (1, 128), 1) < HD
        lane = lax.broadcasted_iota(jnp.int32, (1, 128), 1)
        do, dg = _rms_bwd(dh1_ref[...], o_ref[...], gp_ref[...])
        dgp_ref[...] += dg
        dob = do.astype(BF16)
        dob_ref[...] = dob
        dy = _nt(dob, wo_ref[...])
        datt, dg = _rms_bwd(dy[:, 0:AW], ya_ref[...], ga_ref[...])
        dga_ref[...] += dg
        dya_ref[...] = datt

        lng = lng_ref[...]
        u, vs, ug, tu, tv, vhat, rstd, mixed, ysg = _sgu_forward(us_ref, lng, lnb_ref[...], w_ref, bt_ref, mixed_s, vnb_s, tm)
        dysg, dg = _rms_bwd(dy[:, AW:D], ysg, gs_ref[...])
        dgs_ref[...] += dg
        dus_ref[:, 0:SW] = ((dysg * mixed) * _gelu_grad(u, tu)).astype(BF16)
        dmix = dysg * ug

        rr = lax.broadcasted_iota(jnp.int32, (CH, CH), 0)
        cc = lax.broadcasted_iota(jnp.int32, (CH, CH), 1)
        tril = cc <= rr
        triu = cc >= rr
        for jj in range(4):
            wta = jnp.where(triu, wt_ref[2 * jj], 0.0).astype(BF16)
            wtb = jnp.where(triu, wt_ref[2 * jj + 1], 0.0).astype(BF16)
            for ch in range(tm // CH):
                rs, cs = slice(ch * CH, (ch + 1) * CH), slice(jj * 128, (jj + 1) * 128)
                dm2 = dmix[rs, cs]
                dma = jnp.where(is_a, dm2, 0.0)
                dmb = jnp.where(is_a, 0.0, dm2)
                dma_b, dmb_b = dma.astype(BF16), dmb.astype(BF16)
                vn2 = vnb_s[rs, cs]
                dw_ref[2 * jj] += jnp.where(tril, _nt(dma_b, vn2), 0.0)
                dw_ref[2 * jj + 1] += jnp.where(tril, _nt(dmb_b, vn2), 0.0)
                dvn_s[rs, cs] = _nn(wta, dma_b) + _nn(wtb, dmb_b)
                dba = jnp.sum(dma, axis=1, keepdims=True)
                dbb = jnp.sum(dmb, axis=1, keepdims=True)
                dbt_ref[...] += jnp.where(lane == 2 * jj, dba, 0.0) + jnp.where(lane == 2 * jj + 1, dbb, 0.0)

        dvn = dvn_s[...]
        dlng_ref[...] += jnp.sum(dvn * vhat, axis=0, keepdims=True)
        dlnb_ref[...] += jnp.sum(dvn, axis=0, keepdims=True)
        dvh = dvn * lng
        dvg = rstd * (dvh - jnp.mean(dvh, axis=-1, keepdims=True) - vhat * jnp.mean(dvh * vhat, axis=-1, keepdims=True))
        dus_ref[:, SW:NUS] = (dvg * _gelu_grad(vs, tv)).astype(BF16)

    row = lambda n: pl.BlockSpec((tm, n), lambda i: (i, 0))
    return pl.pallas_call(
        body,
        name="mix_bwd",
        grid=(T // tm,),
        in_specs=[row(D), row(D), row(NUS), row(AW), _full((1, SW)), _full((1, SW)), _full((8, CH, CH)), _full((8, CH, CH)),
                  _full((CH, 8)), _full((1, AW)), _full((1, SW)), _full((D, D)), _full((1, D))],
        out_specs=[row(D), row(AW), row(NUS), _full((8, CH, CH)), _full((CH, 128)), _full((1, SW)), _full((1, SW)),
                   _full((1, AW)), _full((1, SW)), _full((1, D))],
        out_shape=[jax.ShapeDtypeStruct((T, D), BF16), jax.ShapeDtypeStruct((T, AW), F32), jax.ShapeDtypeStruct((T, NUS), BF16),
                   jax.ShapeDtypeStruct((8, CH, CH), F32), jax.ShapeDtypeStruct((CH, 128), F32),
                   jax.ShapeDtypeStruct((1, SW), F32), jax.ShapeDtypeStruct((1, SW), F32),
                   jax.ShapeDtypeStruct((1, AW), F32), jax.ShapeDtypeStruct((1, SW), F32), jax.ShapeDtypeStruct((1, D), F32)],
        scratch_shapes=[pltpu.VMEM((tm, SW), F32), pltpu.VMEM((tm, SW), BF16), pltpu.VMEM((tm, SW), F32)],
        compiler_params=_params(1),
    )(dh1, o, us, yatt, lng, lnb, sgw, sgwt, sgbt, gatt, gsg, wout, gpm)


def _attn_bwd(qkv, dya, yatt, lse, c, ct):
    T = qkv.shape[0]
    tq = _tile(T, 512)
    tk = tq
    nq = T // tq

    def body(q_ref, k_ref, v_ref, do_ref, o_ref, lse_ref, c_ref, ct_ref, dq_ref, dcq_ref, dk_ref, dv_ref, dck_ref,
             dk_s, dv_s, dc_s):
        hp, j, i = pl.program_id(0), pl.program_id(1), pl.program_id(2)
        is_a = lax.broadcasted_iota(jnp.int32, (1, 128), 1) < HD
        sub = lax.broadcasted_iota(jnp.int32, (8, tk), 0)
        lane = lax.broadcasted_iota(jnp.int32, (1, 128), 1)

        @pl.when(i == 0)
        def _():
            dk_s[...] = jnp.zeros_like(dk_s)
            dv_s[...] = jnp.zeros_like(dv_s)
            dc_s[...] = jnp.zeros_like(dc_s)

        @pl.when(i >= j)
        def _():
            q2, k2, v2 = q_ref[...], k_ref[...], v_ref[...]
            do2 = do_ref[...]
            dd = do2 * o_ref[...]
            lse2 = lse_ref[...]
            rows = i * tq + lax.broadcasted_iota(jnp.int32, (tq, tk), 0)
            cols = j * tk + lax.broadcasted_iota(jnp.int32, (tq, tk), 1)
            causal = cols <= rows
            dq = jnp.zeros((tq, 128), F32)
            dk = jnp.zeros((tk, 128), F32)
            dv = jnp.zeros((tk, 128), F32)
            dc = jnp.zeros((8, tk), F32)
            dcq = jnp.zeros((tq, 128), F32)
            for hh in range(2):
                sel = is_a if hh == 0 else jnp.logical_not(is_a)
                qh = jnp.where(sel, q2, jnp.zeros_like(q2)) * 0.125
                kh = jnp.where(sel, k2, jnp.zeros_like(k2)) * 0.125
                doh = jnp.where(sel, do2, 0.0).astype(BF16)
                cq, ck = _head_bias(c_ref[...], ct_ref[...], 2 * hp + hh)
                lse_h = lse2[:, hh * HD:hh * HD + 1]
                s = _nt(qh, k2) + (cq - ck)
                pr = jnp.where(causal, jnp.exp(s - lse_h), 0.0)
                dsum = jnp.sum(jnp.where(sel, dd, 0.0), axis=1, keepdims=True)
                ds = pr * (_nt(doh, v2) - dsum)
                ds_b = ds.astype(BF16)
                dv = dv + _tn(pr.astype(BF16), doh)
                dk = dk + _tn(ds_b, qh)
                dq = dq + _nn(ds_b, kh)
                dc = dc - jnp.where(sub == hh, jnp.sum(ds, axis=0, keepdims=True), 0.0)
                dcq = dcq + jnp.where(lane == hh, jnp.sum(ds, axis=1, keepdims=True), 0.0)
            dk_s[...] += dk
            dv_s[...] += dv
            dc_s[...] += dc
            rows_q = pl.ds(pl.multiple_of(i * tq, tq), tq)

            @pl.when(j == 0)
            def _():
                dq_ref[rows_q, :] = dq
                dcq_ref[rows_q, :] = dcq

            @pl.when(j > 0)
            def _():
                dq_ref[rows_q, :] += dq
                dcq_ref[rows_q, :] += dcq

        @pl.when(i == nq - 1)
        def _():
            dk_ref[...] = dk_s[...].astype(BF16)
            dv_ref[...] = dv_s[...].astype(BF16)
            dck_ref[...] = dc_s[...]

    qrow = lambda hp, j, i: (jnp.maximum(i, j), hp)
    return pl.pallas_call(
        body,
        name="attn_bwd",
        grid=(4, nq, nq),
        in_specs=[
            pl.BlockSpec((tq, 128), qrow),
            pl.BlockSpec((tk, 128), lambda hp, j, i: (j, 4 + hp)),
            pl.BlockSpec((tk, 128), lambda hp, j, i: (j, 8 + hp)),
            pl.BlockSpec((tq, 128), qrow),
            pl.BlockSpec((tq, 128), qrow),
            pl.BlockSpec((tq, 128), qrow),
            pl.BlockSpec((tq, FPAD), lambda hp, j, i: (jnp.maximum(i, j), 0)),
            pl.BlockSpec((8, tk), lambda hp, j, i: (0, j)),
        ],
        out_specs=[
            pl.BlockSpec((T, 128), lambda hp, j, i: (0, hp)),
            pl.BlockSpec((T, 128), lambda hp, j, i: (0, hp)),
            pl.BlockSpec((tk, 128), lambda hp, j, i: (j, hp)),
            pl.BlockSpec((tk, 128), lambda hp, j, i: (j, hp)),
            pl.BlockSpec((None, 8, tk), lambda hp, j, i: (hp, 0, j)),
        ],
        out_shape=[jax.ShapeDtypeStruct((T, AW), F32), jax.ShapeDtypeStruct((T, AW), F32),
                   jax.ShapeDtypeStruct((T, AW), BF16), jax.ShapeDtypeStruct((T, AW), BF16),
                   jax.ShapeDtypeStruct((4, 8, T), F32)],
        scratch_shapes=[pltpu.VMEM((tk, 128), F32), pltpu.VMEM((tk, 128), F32), pltpu.VMEM((8, tk), F32)],
        compiler_params=_params(3),
    )(qkv, qkv, qkv, dya, yatt, lse, c, ct)


def _fgate_bwd(dcq, dck, fz, fb):
    T = dck.shape[0]
    tb = _tile(T, 512)
    nb = T // tb

    def body(dcq_ref, dck_ref, fz_ref, fb_ref, df_ref, dfb_ref, carry):
        @pl.when(pl.program_id(0) == 0)
        def _():
            carry[...] = jnp.zeros_like(carry)
            dfb_ref[...] = jnp.zeros_like(dfb_ref)

        dcv = dcq_ref[...] + dck_ref[...]
        r = lax.broadcasted_iota(jnp.int32, (tb, tb), 0)
        cc = lax.broadcasted_iota(jnp.int32, (tb, tb), 1)
        tri = (cc >= r).astype(F32)
        dlf = jnp.dot(tri, dcv, precision=lax.Precision.HIGHEST, preferred_element_type=F32) + carry[...]
        carry[...] = carry[...] + jnp.sum(dcv, axis=0, keepdims=True)
        lane = lax.broadcasted_iota(jnp.int32, (tb, FPAD), 1)
        df = jnp.where(lane < 8, dlf * jax.nn.sigmoid(-(fz_ref[...] + fb_ref[...])), 0.0)
        df_ref[...] = df.astype(BF16)
        dfb_ref[...] += jnp.sum(df, axis=0, keepdims=True)

    rev = pl.BlockSpec((tb, FPAD), lambda i: (nb - 1 - i, 0))
    return pl.pallas_call(
        body,
        name="fgate_bwd",
        grid=(nb,),
        in_specs=[rev, rev, rev, _full((1, FPAD))],
        out_specs=[rev, _full((1, FPAD))],
        out_shape=[jax.ShapeDtypeStruct((T, FPAD), BF16), jax.ShapeDtypeStruct((1, FPAD), F32)],
        scratch_shapes=[pltpu.VMEM((1, FPAD), F32)],
        compiler_params=_params(1),
    )(dcq, dck, fz, fb)


def _inproj_bwd(dq, dk, dv, dus, dfz, wp, x, dh1, g):
    T = x.shape[0]
    tm = _tile(T, 512)

    def body(dq_ref, dk_ref, dv_ref, dus_ref, dfz_ref, w_ref, x_ref, dh1_ref, g_ref, gx_ref, dg_ref):
        @pl.when(pl.program_id(0) == 0)
        def _():
            dg_ref[...] = jnp.zeros_like(dg_ref)

        da = _nt(dq_ref[...].astype(BF16), w_ref[:, 0:AW])
        da += _nt(dk_ref[...], w_ref[:, AW:2 * AW])
        da += _nt(dv_ref[...], w_ref[:, 2 * AW:NQKV])
        da += _nt(dus_ref[...], w_ref[:, NQKV:NQKV + NUS])
        da += _nt(dfz_ref[...], w_ref[:, NQKV + NUS:ZP])
        dx, dg = _rms_bwd(da, x_ref[...], g_ref[...])
        gx_ref[...] = dh1_ref[...] + dx
        dg_ref[...] += dg

    row = lambda n: pl.BlockSpec((tm, n), lambda i: (i, 0))
    return pl.pallas_call(
        body,
        name="inproj_bwd",
        grid=(T // tm,),
        in_specs=[row(AW), row(AW), row(AW), row(NUS), row(FPAD), _full((D, ZP)), row(D), row(D), _full((1, D))],
        out_specs=[row(D), _full((1, D))],
        out_shape=[jax.ShapeDtypeStruct((T, D), F32), jax.ShapeDtypeStruct((1, D), F32)],
        compiler_params=_params(1),
    )(dq, dk, dv, dus, dfz, wp, x, dh1, g)


def _wgrad(a, b, name, colblk=None):
    T, K = a.shape
    N = b.shape[1]
    bk = min(K, 1024)
    bn = colblk if colblk else min(N, 1024)
    tt = _tile(T, 512)

    def body(a_ref, b_ref, o_ref):
        @pl.when(pl.program_id(2) == 0)
        def _():
            o_ref[...] = jnp.zeros_like(o_ref)

        o_ref[...] += _tn(a_ref[...].astype(BF16), b_ref[...].astype(BF16))

    if colblk:
        out_shape = jax.ShapeDtypeStruct((N // bn, K, bn), F32)
        out_spec = pl.BlockSpec((None, bk, bn), lambda k, n, t: (n, k, 0))
    else:
        out_shape = jax.ShapeDtypeStruct((K, N), F32)
        out_spec = pl.BlockSpec((bk, bn), lambda k, n, t: (k, n))
    return pl.pallas_call(
        body,
        name=name,
        grid=(K // bk, N // bn, T // tt),
        in_specs=[pl.BlockSpec((tt, bk), lambda k, n, t: (t, k)), pl.BlockSpec((tt, bn), lambda k, n, t: (t, n))],
        out_specs=out_spec,
        out_shape=out_shape,
        compiler_params=_params(3),
    )(a, b)


def _adam_math(w, g, m, v):
    m = ADAM_B1 * m + (1.0 - ADAM_B1) * g
    v = ADAM_B2 * v + (1.0 - ADAM_B2) * (g * g)
    m_hat = m / (1.0 - ADAM_B1 ** ADAM_STEP)
    v_hat = v / (1.0 - ADAM_B2 ** ADAM_STEP)
    delta = -ADAM_LR * (m_hat / (jnp.sqrt(v_hat) + ADAM_EPS) + ADAM_WD * w)
    return delta, m, v


def _adam(parts, w, m, v, name):
    R, C = w.shape
    br = 128 if R % 128 == 0 else R

    def body(p_ref, w_ref, m_ref, v_ref, g_ref, d_ref, nm_ref, nv_ref):
        g = p_ref[0]
        for s in range(1, NDEV):
            g = g + p_ref[s]
        g_ref[...] = g
        d_ref[...], nm_ref[...], nv_ref[...] = _adam_math(w_ref[...], g, m_ref[...], v_ref[...])

    blk = pl.BlockSpec((br, C), lambda i: (i, 0))
    return pl.pallas_call(
        body,
        name=name,
        grid=(R // br,),
        in_specs=[pl.BlockSpec((NDEV, br, C), lambda i: (0, i, 0)), blk, blk, blk],
        out_specs=[blk] * 4,
        out_shape=[jax.ShapeDtypeStruct((R, C), F32)] * 4,
        compiler_params=_params(1),
    )(parts, w, m, v)


_SMALL = (("sg_w", 8 * CH * CH), ("f_bias", 8), ("sg_ln_g", SW), ("sg_ln_b", SW), ("sg_b", 8 * CH), ("att_out_g", AW),
          ("sg_out_g", SW), ("pre_mix_g", D), ("post_mix_g", D), ("pre_ffn_g", D), ("post_ffn_g", D), ("ple_gate_b", D))
_SEG = 8 * 128


def _seg_rows(size):
    return 8 * (-(-size // _SEG))


def _pack(vals):
    parts = []
    for name, size in _SMALL:
        flat = vals[name].reshape(-1)
        rows = _seg_rows(size)
        parts.append(jnp.pad(flat, (0, rows * 128 - size)).reshape(rows, 128))
    return jnp.concatenate(parts, axis=0)


def _unpack(packed, shapes):
    out, r = {}, 0
    for name, size in _SMALL:
        rows = _seg_rows(size)
        out[name] = packed[r:r + rows].reshape(-1)[:size].reshape(shapes[name])
        r += rows
    return out


def kernel(x, p, w_in, f_bias, sg_ln_g, sg_ln_b, sg_w, sg_b, att_out_g, sg_out_g, w_out, pre_mix_g, post_mix_g, pre_ffn_g, post_ffn_g, w_ff1, w_ff2, ple_w, ple_gate_w, ple_gate_b, loss_target, m_w_in, m_f_bias, m_sg_ln_g, m_sg_ln_b, m_sg_w, m_sg_b, m_att_out_g, m_sg_out_g, m_w_out, m_pre_mix_g, m_post_mix_g, m_pre_ffn_g, m_post_ffn_g, m_w_ff1, m_w_ff2, m_ple_w, m_ple_gate_w, m_ple_gate_b, v_w_in, v_f_bias, v_sg_ln_g, v_sg_ln_b, v_sg_w, v_sg_b, v_att_out_g, v_sg_out_g, v_w_out, v_pre_mix_g, v_post_mix_g, v_pre_ffn_g, v_post_ffn_g, v_w_ff1, v_w_ff2, v_ple_w, v_ple_gate_w, v_ple_gate_b):
    small_w = dict(sg_w=sg_w, f_bias=f_bias, sg_ln_g=sg_ln_g, sg_ln_b=sg_ln_b, sg_b=sg_b, att_out_g=att_out_g,
                   sg_out_g=sg_out_g, pre_mix_g=pre_mix_g, post_mix_g=post_mix_g, pre_ffn_g=pre_ffn_g,
                   post_ffn_g=post_ffn_g, ple_gate_b=ple_gate_b)
    small_m = dict(sg_w=m_sg_w, f_bias=m_f_bias, sg_ln_g=m_sg_ln_g, sg_ln_b=m_sg_ln_b, sg_b=m_sg_b, att_out_g=m_att_out_g,
                   sg_out_g=m_sg_out_g, pre_mix_g=m_pre_mix_g, post_mix_g=m_post_mix_g, pre_ffn_g=m_pre_ffn_g,
                   post_ffn_g=m_post_ffn_g, ple_gate_b=m_ple_gate_b)
    small_v = dict(sg_w=v_sg_w, f_bias=v_f_bias, sg_ln_g=v_sg_ln_g, sg_ln_b=v_sg_ln_b, sg_b=v_sg_b, att_out_g=v_att_out_g,
                   sg_out_g=v_sg_out_g, pre_mix_g=v_pre_mix_g, post_mix_g=v_post_mix_g, pre_ffn_g=v_pre_ffn_g,
                   post_ffn_g=v_post_ffn_g, ple_gate_b=v_ple_gate_b)
    big = dict(w_in=(w_in, m_w_in, v_w_in), w_out=(w_out, m_w_out, v_w_out), w_ff1=(w_ff1, m_w_ff1, v_w_ff1),
               w_ff2=(w_ff2, m_w_ff2, v_w_ff2), ple_w=(ple_w, m_ple_w, v_ple_w),
               ple_gate_w=(ple_gate_w, m_ple_gate_w, v_ple_gate_w))

    xt, pt, tgt = x[0], p[0, 0], loss_target[0]
    ws = W_IN_COLS // NDEV

    gw_in, gw_out, gw1, gw2, gwpe, gwg = _xchg(
        [w_in[0].astype(BF16), w_out[0].astype(BF16), w_ff1[0].astype(BF16), w_ff2[0].astype(BF16),
         ple_w[0].astype(BF16), ple_gate_w[0].astype(BF16)], True, "gather_weights")
    win = jnp.transpose(gw_in, (1, 0, 2)).reshape(D, W_IN_COLS)
    wp = jnp.concatenate([win[:, 0:NQKV], win[:, NQKV + 8:W_IN_COLS], win[:, NQKV:NQKV + 8],
                          jnp.zeros((D, FPAD - 8), BF16)], axis=1)
    wout = gw_out.reshape(D, D)
    wg = gwg.reshape(D, D)
    wpe = jnp.transpose(gwpe, (1, 0, 2)).reshape(PLE, D)

    fb = jnp.pad(f_bias.astype(F32), ((0, 0), (0, FPAD - 8)))
    sgw = sg_w[0]
    sgwt = jnp.transpose(sg_w[0], (0, 2, 1))
    sgbt = jnp.transpose(sg_b[0])

    qkv, us, fz, ab = _inproj(xt, pre_mix_g, wp)
    c, ct = _fcum(fz, fb)
    yatt, lse = _attn_fwd(qkv, c, ct)
    h1, yb, o = _sgu_out(us, yatt, xt, sg_ln_g, sg_ln_b, sgw, sgbt, att_out_g, sg_out_g, wout, post_mix_g)
    c2b, f1, act, ff, h2, h2b = _ffn_fwd(h1, pre_ffn_g, gw1, gw2, post_ffn_g)
    dh2, dpre, dpe, pb, dbg, loss_acc = _ple_loss(h2, pt, tgt, wg, ple_gate_b, wpe)
    loss = lax.psum(loss_acc[0, 0] * (0.5 / D), ("x", "y", "c"))

    dffb, df1, dh1, dgpostffn, dgpreffn = _ffn_bwd(dh2, ff, h1, f1, gw1, gw2, post_ffn_g, pre_ffn_g)
    dob, dya, dus, dsgw, dsgbt, dlng, dlnb, dgatt, dgsg, dgpostmix = _mix_bwd(
        dh1, o, us, yatt, sg_ln_g, sg_ln_b, sgw, sgwt, sgbt, att_out_g, sg_out_g, wout, post_mix_g)
    dq, dcq, dk, dv, dck = _attn_bwd(qkv, dya, yatt, lse, c, ct)
    dcq = jnp.pad(dcq.reshape(-1, 4, 128)[:, :, 0:2].reshape(-1, 8), ((0, 0), (0, FPAD - 8)))
    dck = jnp.pad(jnp.transpose(dck[:, 0:2, :].reshape(8, -1)), ((0, 0), (0, FPAD - 8)))
    dfz, dfb = _fgate_bwd(dcq, dck, fz, fb)
    grad_x, dgpremix = _inproj_bwd(dq, dk, dv, dus, dfz, wp, xt, dh1, pre_mix_g)

    gq = _wgrad(ab, dq, "wgrad_q")
    gk = _wgrad(ab, dk, "wgrad_k")
    gv = _wgrad(ab, dv, "wgrad_v")
    gus = _wgrad(ab, dus, "wgrad_us")
    gf = _wgrad(ab, dfz, "wgrad_f")
    g_in = jnp.concatenate([gq, gk, gv, gf[:, 0:8], gus], axis=1)
    g_in = jnp.transpose(g_in.reshape(D, NDEV, ws), (1, 0, 2))
    g_out = _wgrad(yb, dob, "wgrad_out").reshape(NDEV, D // NDEV, D)
    g_1 = _wgrad(c2b, df1, "wgrad_ff1", colblk=DFF // NDEV)
    g_2 = _wgrad(act, dffb, "wgrad_ff2").reshape(NDEV, DFF // NDEV, D)
    g_pe = _wgrad(pb, dpe, "wgrad_ple", colblk=D // NDEV)
    g_g = _wgrad(h2b, dpre, "wgrad_gate").reshape(NDEV, D // NDEV, D)

    small_g = dict(sg_w=dsgw, f_bias=dfb[:, 0:8], sg_ln_g=dlng, sg_ln_b=dlnb, sg_b=jnp.transpose(dsgbt[:, 0:8]),
                   att_out_g=dgatt, sg_out_g=dgsg, pre_mix_g=dgpremix, post_mix_g=dgpostmix, pre_ffn_g=dgpreffn,
                   post_ffn_g=dgpostffn, ple_gate_b=dbg)

    r_in, r_out, r_1, r_2, r_pe, r_g = _xchg([g_in, g_out, g_1, g_2, g_pe, g_g], False, "scatter_grads")
    (r_small,) = _xchg([_pack(small_g)], True, "gather_small_grads")

    res = {}
    for name, parts in (("w_in", r_in), ("w_out", r_out), ("w_ff1", r_1), ("w_ff2", r_2), ("ple_w", r_pe),
                        ("ple_gate_w", r_g)):
        w, m, v = big[name]
        res[name] = [t[None] for t in _adam(parts, w[0], m[0], v[0], "adam_" + name)]
    shapes = {k: a.shape for k, a in small_w.items()}
    small = [_unpack(t, shapes) for t in _adam(r_small, _pack(small_w), _pack(small_m), _pack(small_v), "adam_small")]
    for name, _ in _SMALL:
        res[name] = [s[name] for s in small]

    order = ["w_in", "f_bias", "sg_ln_g", "sg_ln_b", "sg_w", "sg_b", "att_out_g", "sg_out_g", "w_out", "pre_mix_g",
             "post_mix_g", "pre_ffn_g", "post_ffn_g", "w_ff1", "w_ff2", "ple_w", "ple_gate_w", "ple_gate_b"]
    outs = [loss, grad_x[None]]
    for kind in range(4):
        outs += [res[name][kind] for name in order]
    return tuple(outs)
```

```python
import jax
import jax.numpy as jnp
from jax import lax
from jax.experimental import pallas as pl
from jax.experimental.pallas import tpu as pltpu

F32 = jnp.float32
BF16 = jnp.bfloat16

NDEV = 8
D = 1024
AW = 512
SW = 512
HD = 64
CH = 128
DFF = 4096
PLE = 256
NQKV = 3 * AW
NUS = 2 * SW
FPAD = 128
ZP = NQKV + NUS + FPAD
W_IN_COLS = 2568
EPS = 1e-6
MASKV = -1e30
GELU_K = 0.7978845608028654
GELU_C = 0.044715

ADAM_LR = 0.001
ADAM_B1 = 0.9
ADAM_B2 = 0.999
ADAM_EPS = 1e-08
ADAM_WD = 0.01
ADAM_STEP = 10

VMEM_LIMIT = 48 * 1024 * 1024


def _nn(a, b):
    return jnp.dot(a, b, preferred_element_type=F32)


def _nt(a, b):
    return lax.dot_general(a, b, (((1,), (1,)), ((), ())), preferred_element_type=F32)


def _tn(a, b):
    return lax.dot_general(a, b, (((0,), (0,)), ((), ())), preferred_element_type=F32)


def _tile(n, pref):
    return min(n, pref)


def _params(n_axes):
    return pltpu.CompilerParams(dimension_semantics=("arbitrary",) * n_axes, vmem_limit_bytes=VMEM_LIMIT)


def _full(shape):
    nd = len(shape)
    return pl.BlockSpec(shape, lambda *_: (0,) * nd)


def _rms_fwd(x, g):
    r = lax.rsqrt(jnp.mean(x * x, axis=-1, keepdims=True) + EPS)
    return x * r * g


def _rms_bwd(dy, x, g):
    n = x.shape[-1]
    r = lax.rsqrt(jnp.mean(x * x, axis=-1, keepdims=True) + EPS)
    u = dy * g
    s = jnp.sum(x * u, axis=-1, keepdims=True)
    dx = r * u - x * (r * r * r * (s * (1.0 / n)))
    dg = jnp.sum(dy * (x * r), axis=0, keepdims=True)
    return dx, dg


def _gelu(x):
    t = jnp.tanh(GELU_K * (x + GELU_C * (x * x * x)))
    return x * (0.5 * (1.0 + t)), t


def _gelu_grad(x, t):
    return 0.5 * (1.0 + t) + 0.5 * x * (1.0 - t * t) * (GELU_K * (1.0 + 3.0 * GELU_C * x * x))


def _xchg(arrs, gather, name):
    n = len(arrs)
    out_shapes = [jax.ShapeDtypeStruct(((NDEV,) + a.shape) if gather else a.shape, a.dtype) for a in arrs]

    def body(*refs):
        ins, outs = refs[:n], refs[n:2 * n]
        send, recv, loc = refs[2 * n:]
        x, y, c = lax.axis_index("x"), lax.axis_index("y"), lax.axis_index("c")
        me = 4 * x + 2 * y + c
        local = []
        for a in range(n):
            cp = pltpu.make_async_copy(ins[a] if gather else ins[a].at[me], outs[a].at[me], loc.at[a])
            cp.start()
            local.append(cp)
        remote = []
        for k in range(1, NDEV):
            px = 1 - x if (k >> 2) & 1 else x
            py = 1 - y if (k >> 1) & 1 else y
            pc = 1 - c if k & 1 else c
            peer = 4 * px + 2 * py + pc
            for a in range(n):
                cp = pltpu.make_async_remote_copy(
                    src_ref=ins[a] if gather else ins[a].at[peer],
                    dst_ref=outs[a].at[me],
                    send_sem=send.at[a, k - 1],
                    recv_sem=recv.at[a, k - 1],
                    device_id=(px, py, pc),
                    device_id_type=pl.DeviceIdType.MESH,
                )
                cp.start()
                remote.append(cp)
        for cp in remote:
            cp.wait_send()
            cp.wait_recv()
        for cp in local:
            cp.wait()

    hbm = pl.BlockSpec(memory_space=pltpu.HBM)
    return pl.pallas_call(
        body,
        name=name,
        out_shape=out_shapes,
        in_specs=[hbm] * n,
        out_specs=[hbm] * n,
        scratch_shapes=[
            pltpu.SemaphoreType.DMA((n, NDEV - 1)),
            pltpu.SemaphoreType.DMA((n, NDEV - 1)),
            pltpu.SemaphoreType.DMA((n,)),
        ],
    )(*arrs)


def _inproj(x, g, wp):
    T = x.shape[0]
    tm = _tile(T, 512)

    def body(x_ref, g_ref, w_ref, qkv_ref, kt_ref, vt_ref, us_ref, fz_ref, ab_ref):
        a = _rms_fwd(x_ref[...], g_ref[...]).astype(BF16)
        ab_ref[...] = a
        qkv_ref[:, 0:AW] = _nn(a, w_ref[:, 0:AW]).astype(BF16)
        kk = _nn(a, w_ref[:, AW:2 * AW])
        qkv_ref[:, AW:2 * AW] = kk.astype(BF16)
        kt_ref[...] = kk.T.astype(BF16)
        vv = _nn(a, w_ref[:, 2 * AW:NQKV])
        qkv_ref[:, 2 * AW:NQKV] = vv.astype(BF16)
        vt_ref[...] = vv.T.astype(BF16)
        us_ref[...] = _nn(a, w_ref[:, NQKV:NQKV + NUS])
        fz_ref[...] = _nn(a, w_ref[:, NQKV + NUS:ZP])

    row = lambda n: pl.BlockSpec((tm, n), lambda i: (i, 0))
    col = pl.BlockSpec((AW, tm), lambda i: (0, i))
    return pl.pallas_call(
        body,
        name="inproj",
        grid=(T // tm,),
        in_specs=[row(D), _full((1, D)), _full((D, ZP))],
        out_specs=[row(NQKV), col, col, row(NUS), row(FPAD), row(D)],
        out_shape=[
            jax.ShapeDtypeStruct((T, NQKV), BF16),
            jax.ShapeDtypeStruct((AW, T), BF16),
            jax.ShapeDtypeStruct((AW, T), BF16),
            jax.ShapeDtypeStruct((T, NUS), F32),
            jax.ShapeDtypeStruct((T, FPAD), F32),
            jax.ShapeDtypeStruct((T, D), BF16),
        ],
        compiler_params=_params(1),
    )(x, g, wp)


def _log_sigmoid(z):
    return jnp.minimum(z, 0.0) - jnp.log1p(jnp.exp(-jnp.abs(z)))


def _fcum(fz, fb):
    T = fz.shape[0]
    tb = _tile(T, 512)

    def body(fz_ref, fb_ref, c_ref, ct_ref, carry):
        @pl.when(pl.program_id(0) == 0)
        def _():
            carry[...] = jnp.zeros_like(carry)

        lf = _log_sigmoid(fz_ref[...] + fb_ref[...])
        r = lax.broadcasted_iota(jnp.int32, (tb, tb), 0)
        cc = lax.broadcasted_iota(jnp.int32, (tb, tb), 1)
        tri = (cc <= r).astype(F32)
        cs = jnp.dot(tri, lf, precision=lax.Precision.HIGHEST, preferred_element_type=F32) + carry[...]
        c_ref[...] = cs
        ct_ref[...] = cs.T[0:8, :]
        carry[...] = carry[...] + jnp.sum(lf, axis=0, keepdims=True)

    return pl.pallas_call(
        body,
        name="fcum",
        grid=(T // tb,),
        in_specs=[pl.BlockSpec((tb, FPAD), lambda i: (i, 0)), _full((1, FPAD))],
        out_specs=[pl.BlockSpec((tb, FPAD), lambda i: (i, 0)), pl.BlockSpec((8, tb), lambda i: (0, i))],
        out_shape=[jax.ShapeDtypeStruct((T, FPAD), F32), jax.ShapeDtypeStruct((8, T), F32)],
        scratch_shapes=[pltpu.VMEM((1, FPAD), F32)],
        compiler_params=_params(1),
    )(fz, fb)


def _key_bias(c_keys, ct_q, h):
    lane = lax.broadcasted_iota(jnp.int32, c_keys.shape, 1)
    sub = lax.broadcasted_iota(jnp.int32, (8, 1), 0)
    ck = jnp.sum(jnp.where(lane == h, c_keys, 0.0), axis=1, keepdims=True)
    cref = jnp.sum(jnp.where(sub == h, ct_q[:, 0:1], 0.0), axis=0, keepdims=True)
    return cref - ck


def _stack_heads(x2, dst, tq, scale):
    is_a = lax.broadcasted_iota(jnp.int32, (1, 128), 1) < HD
    zero = jnp.zeros_like(x2)
    dst[0:tq, :] = (jnp.where(is_a, x2, zero) * scale).astype(dst.dtype)
    dst[tq:2 * tq, :] = (jnp.where(is_a, zero, x2) * scale).astype(dst.dtype)


def _attn_fwd(qkv, vt, c, ct):
    T = qkv.shape[0]
    tq = _tile(T, 512)
    tk = tq
    nq = T // tq

    def body(q_ref, k_ref, vt_ref, c_ref, ctq_ref, o_ref, lse_ref, qw_s, m_s, l_s, acc_s):
        hp, i, j = pl.program_id(0), pl.program_id(1), pl.program_id(2)

        @pl.when(j == 0)
        def _():
            _stack_heads(q_ref[...], qw_s, tq, 0.125)
            m_s[...] = jnp.full_like(m_s, MASKV)
            l_s[...] = jnp.zeros_like(l_s)
            acc_s[...] = jnp.zeros_like(acc_s)

        def step(masked):
            s2 = _nt(k_ref[...], qw_s[...])
            vt2 = vt_ref[...]
            for hh in range(2):
                s = s2[:, hh * tq:(hh + 1) * tq] + _key_bias(c_ref[...], ctq_ref[...], 2 * hp + hh)
                if masked:
                    key = lax.broadcasted_iota(jnp.int32, (tk, tq), 0)
                    qry = lax.broadcasted_iota(jnp.int32, (tk, tq), 1)
                    s = jnp.where(key <= qry, s, MASKV)
                m_prev = m_s[hh]
                m_new = jnp.maximum(m_prev, jnp.max(s, axis=0, keepdims=True))
                pr = jnp.exp(s - m_new)
                alpha = jnp.exp(m_prev - m_new)
                l_s[hh] = alpha * l_s[hh] + jnp.sum(pr, axis=0, keepdims=True)
                m_s[hh] = m_new
                acc_s[hh] = alpha * acc_s[hh] + _nn(vt2, pr.astype(BF16))

        @pl.when(j < i)
        def _():
            step(False)

        @pl.when(j == i)
        def _():
            step(True)
            sub = lax.broadcasted_iota(jnp.int32, (128, 1), 0)
            ot = jnp.where(sub < HD, acc_s[0] * (1.0 / l_s[0]), acc_s[1] * (1.0 / l_s[1]))
            o_ref[...] = ot.T
            sub8 = lax.broadcasted_iota(jnp.int32, (8, 1), 0)
            lse_ref[...] = jnp.where(sub8 == 0, m_s[0] + jnp.log(l_s[0]),
                                     jnp.where(sub8 == 1, m_s[1] + jnp.log(l_s[1]), 0.0))

    return pl.pallas_call(
        body,
        name="attn_fwd",
        grid=(4, nq, nq),
        in_specs=[
            pl.BlockSpec((tq, 128), lambda hp, i, j: (i, hp)),
            pl.BlockSpec((tk, 128), lambda hp, i, j: (jnp.minimum(j, i), 4 + hp)),
            pl.BlockSpec((128, tk), lambda hp, i, j: (hp, jnp.minimum(j, i))),
            pl.BlockSpec((tk, FPAD), lambda hp, i, j: (jnp.minimum(j, i), 0)),
            pl.BlockSpec((8, tq), lambda hp, i, j: (0, i)),
        ],
        out_specs=[pl.BlockSpec((tq, 128), lambda hp, i, j: (i, hp)),
                   pl.BlockSpec((None, 8, tq), lambda hp, i, j: (hp, 0, i))],
        out_shape=[jax.ShapeDtypeStruct((T, AW), F32), jax.ShapeDtypeStruct((4, 8, T), F32)],
        scratch_shapes=[pltpu.VMEM((2 * tq, 128), BF16), pltpu.VMEM((2, 1, tq), F32), pltpu.VMEM((2, 1, tq), F32),
                        pltpu.VMEM((2, 128, tq), F32)],
        compiler_params=_params(3),
    )(qkv, qkv, vt, c, ct)


def _sgu_forward(us_ref, lng, lnb, w_ref, bt_ref, mixed_s, vnb_s, tm):
    is_a = lax.broadcasted_iota(jnp.int32, (1, 128), 1) < HD
    u = us_ref[:, 0:SW]
    vs = us_ref[:, SW:NUS]
    ug, tu = _gelu(u)
    vg, tv = _gelu(vs)
    mu = jnp.mean(vg, axis=-1, keepdims=True)
    xc = vg - mu
    rstd = lax.rsqrt(jnp.mean(xc * xc, axis=-1, keepdims=True) + EPS)
    vhat = xc * rstd
    vnb_s[...] = (vhat * lng + lnb).astype(BF16)
    rr = lax.broadcasted_iota(jnp.int32, (CH, CH), 0)
    cc = lax.broadcasted_iota(jnp.int32, (CH, CH), 1)
    tril = cc <= rr
    for jj in range(4):
        wa = jnp.where(tril, w_ref[2 * jj], 0.0).astype(BF16)
        wb = jnp.where(tril, w_ref[2 * jj + 1], 0.0).astype(BF16)
        ba = bt_ref[:, 2 * jj:2 * jj + 1]
        bb = bt_ref[:, 2 * jj + 1:2 * jj + 2]
        for ch in range(tm // CH):
            rs, cs = slice(ch * CH, (ch + 1) * CH), slice(jj * 128, (jj + 1) * 128)
            vn2 = vnb_s[rs, cs]
            mixed_s[rs, cs] = jnp.where(is_a, _nn(wa, vn2) + ba, _nn(wb, vn2) + bb)
    mixed = mixed_s[...]
    return u, vs, ug, tu, tv, vhat, rstd, mixed, ug * mixed


def _sgu_out(us, yatt, x, lng, lnb, sgw, sgbt, gatt, gsg, wout, gpm):
    T = us.shape[0]
    tm = _tile(T, 256)

    def body(us_ref, ya_ref, x_ref, lng_ref, lnb_ref, w_ref, bt_ref, ga_ref, gs_ref, wo_ref, gp_ref,
             h1_ref, yb_ref, o_ref, mixed_s, vnb_s):
        ysg = _sgu_forward(us_ref, lng_ref[...], lnb_ref[...], w_ref, bt_ref, mixed_s, vnb_s, tm)[-1]
        yb_ref[:, 0:AW] = _rms_fwd(ya_ref[...], ga_ref[...]).astype(BF16)
        yb_ref[:, AW:D] = _rms_fwd(ysg, gs_ref[...]).astype(BF16)
        o = _nn(yb_ref[...], wo_ref[...])
        o_ref[...] = o
        h1_ref[...] = x_ref[...] + _rms_fwd(o, gp_ref[...])

    row = lambda n: pl.BlockSpec((tm, n), lambda i: (i, 0))
    return pl.pallas_call(
        body,
        name="sgu_out",
        grid=(T // tm,),
        in_specs=[row(NUS), row(AW), row(D), _full((1, SW)), _full((1, SW)), _full((8, CH, CH)), _full((CH, 8)),
                  _full((1, AW)), _full((1, SW)), _full((D, D)), _full((1, D))],
        out_specs=[row(D), row(D), row(D)],
        out_shape=[jax.ShapeDtypeStruct((T, D), F32), jax.ShapeDtypeStruct((T, D), BF16), jax.ShapeDtypeStruct((T, D), F32)],
        scratch_shapes=[pltpu.VMEM((tm, SW), F32), pltpu.VMEM((tm, SW), BF16)],
        compiler_params=_params(1),
    )(us, yatt, x, lng, lnb, sgw, sgbt, gatt, gsg, wout, gpm)


def _ffn_fwd(h1, gpre, w1g, w2g, gpost):
    T = h1.shape[0]
    tm = _tile(T, 512)
    nb, hb = w1g.shape[0], w1g.shape[2]

    def body(h1_ref, gpre_ref, w1_ref, w2_ref, gpost_ref, c2_ref, f1_ref, act_ref, ff_ref, h2_ref, h2b_ref, acc_s):
        j = pl.program_id(1)

        @pl.when(j == 0)
        def _():
            c2_ref[...] = _rms_fwd(h1_ref[...], gpre_ref[...]).astype(BF16)
            acc_s[...] = jnp.zeros_like(acc_s)

        f1 = _nn(c2_ref[...], w1_ref[...])
        f1_ref[...] = f1
        r = jnp.maximum(f1, 0.0)
        act = (r * r).astype(BF16)
        act_ref[...] = act
        acc_s[...] += _nn(act, w2_ref[...])

        @pl.when(j == nb - 1)
        def _():
            ff = acc_s[...]
            ff_ref[...] = ff
            h2 = h1_ref[...] + _rms_fwd(ff, gpost_ref[...])
            h2_ref[...] = h2
            h2b_ref[...] = h2.astype(BF16)

    row = lambda n: pl.BlockSpec((tm, n), lambda i, j: (i, 0))
    return pl.pallas_call(
        body,
        name="ffn_fwd",
        grid=(T // tm, nb),
        in_specs=[row(D), _full((1, D)), pl.BlockSpec((None, D, hb), lambda i, j: (j, 0, 0)),
                  pl.BlockSpec((None, hb, D), lambda i, j: (j, 0, 0)), _full((1, D))],
        out_specs=[row(D), pl.BlockSpec((tm, hb), lambda i, j: (i, j)), pl.BlockSpec((tm, hb), lambda i, j: (i, j)),
                   row(D), row(D), row(D)],
        out_shape=[jax.ShapeDtypeStruct((T, D), BF16), jax.ShapeDtypeStruct((T, DFF), F32),
                   jax.ShapeDtypeStruct((T, DFF), BF16), jax.ShapeDtypeStruct((T, D), F32),
                   jax.ShapeDtypeStruct((T, D), F32), jax.ShapeDtypeStruct((T, D), BF16)],
        scratch_shapes=[pltpu.VMEM((tm, D), F32)],
        compiler_params=_params(2),
    )(h1, gpre, w1g, w2g, gpost)


def _ple_loss(h2, p, tgt, wg, bg, wpe):
    T = h2.shape[0]
    tm = _tile(T, 512)

    def body(h2_ref, p_ref, t_ref, wg_ref, bg_ref, wpe_ref, dh2_ref, dpre_ref, dpe_ref, pb_ref, dbg_ref, loss_ref):
        @pl.when(pl.program_id(0) == 0)
        def _():
            dbg_ref[...] = jnp.zeros_like(dbg_ref)
            loss_ref[...] = jnp.zeros_like(loss_ref)

        h2 = h2_ref[...]
        gate = jax.nn.sigmoid(_nn(h2.astype(BF16), wg_ref[...]) + bg_ref[...])
        pb = p_ref[...].astype(BF16)
        pb_ref[...] = pb
        pe = _nn(pb, wpe_ref[...])
        diff = (h2 + gate * pe) - t_ref[...]
        loss_ref[...] += jnp.sum(diff * diff)
        dh3 = diff * (1.0 / D)
        dpre = (dh3 * pe) * (gate * (1.0 - gate))
        dpre_b = dpre.astype(BF16)
        dpre_ref[...] = dpre_b
        dpe_ref[...] = (dh3 * gate).astype(BF16)
        dbg_ref[...] += jnp.sum(dpre, axis=0, keepdims=True)
        dh2_ref[...] = dh3 + _nt(dpre_b, wg_ref[...])

    row = lambda n: pl.BlockSpec((tm, n), lambda i: (i, 0))
    return pl.pallas_call(
        body,
        name="ple_loss",
        grid=(T // tm,),
        in_specs=[row(D), row(PLE), row(D), _full((D, D)), _full((1, D)), _full((PLE, D))],
        out_specs=[row(D), row(D), row(D), row(PLE), _full((1, D)), _full((8, 128))],
        out_shape=[jax.ShapeDtypeStruct((T, D), F32), jax.ShapeDtypeStruct((T, D), BF16),
                   jax.ShapeDtypeStruct((T, D), BF16), jax.ShapeDtypeStruct((T, PLE), BF16),
                   jax.ShapeDtypeStruct((1, D), F32), jax.ShapeDtypeStruct((8, 128), F32)],
        compiler_params=_params(1),
    )(h2, p, tgt, wg, bg, wpe)


def _ffn_bwd(dh2, ff, h1, f1, w1g, w2g, gpost, gpre):
    T = dh2.shape[0]
    tm = _tile(T, 512)
    nb, hb = w1g.shape[0], w1g.shape[2]

    def body(dh2_ref, ff_ref, h1_ref, f1_ref, w1_ref, w2_ref, gpost_ref, gpre_ref,
             dffb_ref, df1_ref, dh1_ref, dgpost_ref, dgpre_ref, acc_s):
        i, j = pl.program_id(0), pl.program_id(1)

        @pl.when((i == 0) & (j == 0))
        def _():
            dgpost_ref[...] = jnp.zeros_like(dgpost_ref)
            dgpre_ref[...] = jnp.zeros_like(dgpre_ref)

        @pl.when(j == 0)
        def _():
            dff, dg = _rms_bwd(dh2_ref[...], ff_ref[...], gpost_ref[...])
            dffb_ref[...] = dff.astype(BF16)
            dgpost_ref[...] += dg
            acc_s[...] = jnp.zeros_like(acc_s)

        dact = _nt(dffb_ref[...], w2_ref[...])
        df1 = (dact * (2.0 * jnp.maximum(f1_ref[...], 0.0))).astype(BF16)
        df1_ref[...] = df1
        acc_s[...] += _nt(df1, w1_ref[...])

        @pl.when(j == nb - 1)
        def _():
            dx, dg = _rms_bwd(acc_s[...], h1_ref[...], gpre_ref[...])
            dh1_ref[...] = dh2_ref[...] + dx
            dgpre_ref[...] += dg

    row = lambda n: pl.BlockSpec((tm, n), lambda i, j: (i, 0))
    blk = pl.BlockSpec((tm, hb), lambda i, j: (i, j))
    return pl.pallas_call(
        body,
        name="ffn_bwd",
        grid=(T // tm, nb),
        in_specs=[row(D), row(D), row(D), blk, pl.BlockSpec((None, D, hb), lambda i, j: (j, 0, 0)),
                  pl.BlockSpec((None, hb, D), lambda i, j: (j, 0, 0)), _full((1, D)), _full((1, D))],
        out_specs=[row(D), blk, row(D), _full((1, D)), _full((1, D))],
        out_shape=[jax.ShapeDtypeStruct((T, D), BF16), jax.ShapeDtypeStruct((T, DFF), BF16),
                   jax.ShapeDtypeStruct((T, D), F32), jax.ShapeDtypeStruct((1, D), F32),
                   jax.ShapeDtypeStruct((1, D), F32)],
        scratch_shapes=[pltpu.VMEM((tm, D), F32)],
        compiler_params=_params(2),
    )(dh2, ff, h1, f1, w1g, w2g, gpost, gpre)


def _mix_bwd(dh1, o, us, yatt, lng, lnb, sgw, sgwt, sgbt, gatt, gsg, wout, gpm):
    T = dh1.shape[0]
    tm = _tile(T, 256)

    def body(dh1_ref, o_ref, us_ref, ya_ref, lng_ref, lnb_ref, w_ref, wt_ref, bt_ref, ga_ref, gs_ref, wo_ref, gp_ref,
             dob_ref, dya_ref, dus_ref, dw_ref, dbt_ref, dlng_ref, dlnb_ref, dga_ref, dgs_ref, dgp_ref,
             mixed_s, vnb_s, dvn_s):
        @pl.when(pl.program_id(0) == 0)
        def _():
            for r in (dw_ref, dbt_ref, dlng_ref, dlnb_ref, dga_ref, dgs_ref, dgp_ref):
                r[...] = jnp.zeros_like(r)

        is_a = lax.broadcasted_iota(jnp.int32, (1, 128), 1) < HD
        lane = lax.broadcasted_iota(jnp.int32, (1, 128), 1)
        do, dg = _rms_bwd(dh1_ref[...], o_ref[...], gp_ref[...])
        dgp_ref[...] += dg
        dob = do.astype(BF16)
        dob_ref[...] = dob
        dy = _nt(dob, wo_ref[...])
        datt, dg = _rms_bwd(dy[:, 0:AW], ya_ref[...], ga_ref[...])
        dga_ref[...] += dg
        dya_ref[...] = datt

        lng = lng_ref[...]
        u, vs, ug, tu, tv, vhat, rstd, mixed, ysg = _sgu_forward(us_ref, lng, lnb_ref[...], w_ref, bt_ref, mixed_s, vnb_s, tm)
        dysg, dg = _rms_bwd(dy[:, AW:D], ysg, gs_ref[...])
        dgs_ref[...] += dg
        dus_ref[:, 0:SW] = ((dysg * mixed) * _gelu_grad(u, tu)).astype(BF16)
        dmix = dysg * ug

        rr = lax.broadcasted_iota(jnp.int32, (CH, CH), 0)
        cc = lax.broadcasted_iota(jnp.int32, (CH, CH), 1)
        tril = cc <= rr
        triu = cc >= rr
        for jj in range(4):
            wta = jnp.where(triu, wt_ref[2 * jj], 0.0).astype(BF16)
            wtb = jnp.where(triu, wt_ref[2 * jj + 1], 0.0).astype(BF16)
            for ch in range(tm // CH):
                rs, cs = slice(ch * CH, (ch + 1) * CH), slice(jj * 128, (jj + 1) * 128)
                dm2 = dmix[rs, cs]
                dma = jnp.where(is_a, dm2, 0.0)
                dmb = jnp.where(is_a, 0.0, dm2)
                dma_b, dmb_b = dma.astype(BF16), dmb.astype(BF16)
                vn2 = vnb_s[rs, cs]
                dw_ref[2 * jj] += jnp.where(tril, _nt(dma_b, vn2), 0.0)
                dw_ref[2 * jj + 1] += jnp.where(tril, _nt(dmb_b, vn2), 0.0)
                dvn_s[rs, cs] = _nn(wta, dma_b) + _nn(wtb, dmb_b)
                dba = jnp.sum(dma, axis=1, keepdims=True)
                dbb = jnp.sum(dmb, axis=1, keepdims=True)
                dbt_ref[...] += jnp.where(lane == 2 * jj, dba, 0.0) + jnp.where(lane == 2 * jj + 1, dbb, 0.0)

        dvn = dvn_s[...]
        dlng_ref[...] += jnp.sum(dvn * vhat, axis=0, keepdims=True)
        dlnb_ref[...] += jnp.sum(dvn, axis=0, keepdims=True)
        dvh = dvn * lng
        dvg = rstd * (dvh - jnp.mean(dvh, axis=-1, keepdims=True) - vhat * jnp.mean(dvh * vhat, axis=-1, keepdims=True))
        dus_ref[:, SW:NUS] = (dvg * _gelu_grad(vs, tv)).astype(BF16)

    row = lambda n: pl.BlockSpec((tm, n), lambda i: (i, 0))
    return pl.pallas_call(
        body,
        name="mix_bwd",
        grid=(T // tm,),
        in_specs=[row(D), row(D), row(NUS), row(AW), _full((1, SW)), _full((1, SW)), _full((8, CH, CH)), _full((8, CH, CH)),
                  _full((CH, 8)), _full((1, AW)), _full((1, SW)), _full((D, D)), _full((1, D))],
        out_specs=[row(D), row(AW), row(NUS), _full((8, CH, CH)), _full((CH, 128)), _full((1, SW)), _full((1, SW)),
                   _full((1, AW)), _full((1, SW)), _full((1, D))],
        out_shape=[jax.ShapeDtypeStruct((T, D), BF16), jax.ShapeDtypeStruct((T, AW), F32), jax.ShapeDtypeStruct((T, NUS), BF16),
                   jax.ShapeDtypeStruct((8, CH, CH), F32), jax.ShapeDtypeStruct((CH, 128), F32),
                   jax.ShapeDtypeStruct((1, SW), F32), jax.ShapeDtypeStruct((1, SW), F32),
                   jax.ShapeDtypeStruct((1, AW), F32), jax.ShapeDtypeStruct((1, SW), F32), jax.ShapeDtypeStruct((1, D), F32)],
        scratch_shapes=[pltpu.VMEM((tm, SW), F32), pltpu.VMEM((tm, SW), BF16), pltpu.VMEM((tm, SW), F32)],
        compiler_params=_params(1),
    )(dh1, o, us, yatt, lng, lnb, sgw, sgwt, sgbt, gatt, gsg, wout, gpm)


def _attn_delta(dya, yatt):
    T = dya.shape[0]
    tm = _tile(T, 512)

    def body(do_ref, o_ref, d_ref):
        head = lax.broadcasted_iota(jnp.int32, (8, AW), 0)
        feat = lax.broadcasted_iota(jnp.int32, (8, AW), 1)
        sel = jnp.where((feat >= head * HD) & (feat < (head + 1) * HD), 1.0, 0.0)
        d_ref[...] = lax.dot_general(sel, do_ref[...] * o_ref[...], (((1,), (1,)), ((), ())),
                                     precision=lax.Precision.HIGHEST, preferred_element_type=F32)

    row = pl.BlockSpec((tm, AW), lambda i: (i, 0))
    return pl.pallas_call(
        body,
        name="attn_delta",
        grid=(T // tm,),
        in_specs=[row, row],
        out_specs=pl.BlockSpec((8, tm), lambda i: (0, i)),
        out_shape=jax.ShapeDtypeStruct((8, T), F32),
        compiler_params=_params(1),
    )(dya, yatt)


def _attn_bwd(qkv, kt, dya, delta, lse, c, ct):
    T = qkv.shape[0]
    tq = _tile(T, 512)
    tk = tq
    nq = T // tq

    def body(q_ref, k_ref, v_ref, kt_ref, do_ref, d_ref, lse_ref, c_ref, ctq_ref,
             dqt_ref, dcq_ref, dk_ref, dv_ref, dck_ref, qw_s, dow_s, pb_s, dsb_s, dk_s, dv_s, dck_s):
        hp, j, i = pl.program_id(0), pl.program_id(1), pl.program_id(2)
        sub8 = lax.broadcasted_iota(jnp.int32, (8, 1), 0)
        lane = lax.broadcasted_iota(jnp.int32, (1, 128), 1)

        @pl.when((j == 0) & (i == 0))
        def _():
            dqt_ref[...] = jnp.zeros_like(dqt_ref)
            dcq_ref[...] = jnp.zeros_like(dcq_ref)

        @pl.when(i == 0)
        def _():
            dk_s[...] = jnp.zeros_like(dk_s)
            dv_s[...] = jnp.zeros_like(dv_s)
            dck_s[...] = jnp.zeros_like(dck_s)

        def step(masked):
            _stack_heads(q_ref[...], qw_s, tq, 0.125)
            _stack_heads(do_ref[...], dow_s, tq, 1.0)
            s2 = _nt(k_ref[...], qw_s[...])
            dp2 = _nt(v_ref[...], dow_s[...])
            cols = pl.ds(pl.multiple_of(i * tq, tq), tq)
            dcq = jnp.zeros((8, tq), F32)
            dck = jnp.zeros((tk, 128), F32)
            for hh in range(2):
                h = 2 * hp + hh
                half = slice(hh * tq, (hh + 1) * tq)
                lse_h = jnp.sum(jnp.where(sub8 == h, lse_ref[...], 0.0), axis=0, keepdims=True)
                d_h = jnp.sum(jnp.where(sub8 == h, d_ref[...], 0.0), axis=0, keepdims=True)
                pr = jnp.exp(s2[:, half] + _key_bias(c_ref[...], ctq_ref[...], h) - lse_h)
                if masked:
                    key = lax.broadcasted_iota(jnp.int32, (tk, tq), 0)
                    qry = lax.broadcasted_iota(jnp.int32, (tk, tq), 1)
                    pr = jnp.where(key <= qry, pr, 0.0)
                ds = pr * (dp2[:, half] - d_h)
                pb_s[:, half] = pr.astype(BF16)
                dsb_s[:, half] = ds.astype(BF16)
                dcq = dcq + jnp.where(sub8 == hh, jnp.sum(ds, axis=0, keepdims=True), 0.0)
                dck = dck - jnp.where(lane == hh, jnp.sum(ds, axis=1, keepdims=True), 0.0)
            dv_s[...] += _nn(pb_s[...], dow_s[...])
            dk_s[...] += _nn(dsb_s[...], qw_s[...])
            dck_s[...] += dck
            dcq_ref[:, cols] += dcq
            dqt2 = _nn(kt_ref[...] * 0.125, dsb_s[...])
            sub = lax.broadcasted_iota(jnp.int32, (128, 1), 0)
            dqt_ref[:, cols] += jnp.where(sub < HD, dqt2[:, 0:tq], dqt2[:, tq:2 * tq])

        @pl.when(i > j)
        def _():
            step(False)

        @pl.when(i == j)
        def _():
            step(True)

        @pl.when(i == nq - 1)
        def _():
            dk_ref[...] = dk_s[...].astype(BF16)
            dv_ref[...] = dv_s[...].astype(BF16)
            dck_ref[...] = dck_s[...]

    qrow = lambda hp, j, i: (jnp.maximum(i, j), hp)
    qcol = lambda hp, j, i: (0, jnp.maximum(i, j))
    return pl.pallas_call(
        body,
        name="attn_bwd",
        grid=(4, nq, nq),
        in_specs=[
            pl.BlockSpec((tq, 128), qrow),
            pl.BlockSpec((tk, 128), lambda hp, j, i: (j, 4 + hp)),
            pl.BlockSpec((tk, 128), lambda hp, j, i: (j, 8 + hp)),
            pl.BlockSpec((128, tk), lambda hp, j, i: (hp, j)),
            pl.BlockSpec((tq, 128), qrow),
            pl.BlockSpec((8, tq), qcol),
            pl.BlockSpec((8, tq), qcol),
            pl.BlockSpec((tk, FPAD), lambda hp, j, i: (j, 0)),
            pl.BlockSpec((8, tq), qcol),
        ],
        out_specs=[
            pl.BlockSpec((128, T), lambda hp, j, i: (hp, 0)),
            pl.BlockSpec((None, 8, T), lambda hp, j, i: (hp, 0, 0)),
            pl.BlockSpec((tk, 128), lambda hp, j, i: (j, hp)),
            pl.BlockSpec((tk, 128), lambda hp, j, i: (j, hp)),
            pl.BlockSpec((tk, 128), lambda hp, j, i: (j, hp)),
        ],
        out_shape=[jax.ShapeDtypeStruct((AW, T), F32), jax.ShapeDtypeStruct((4, 8, T), F32),
                   jax.ShapeDtypeStruct((T, AW), BF16), jax.ShapeDtypeStruct((T, AW), BF16),
                   jax.ShapeDtypeStruct((T, AW), F32)],
        scratch_shapes=[pltpu.VMEM((2 * tq, 128), BF16), pltpu.VMEM((2 * tq, 128), BF16),
                        pltpu.VMEM((tk, 2 * tq), BF16), pltpu.VMEM((tk, 2 * tq), BF16),
                        pltpu.VMEM((tk, 128), F32), pltpu.VMEM((tk, 128), F32), pltpu.VMEM((tk, 128), F32)],
        compiler_params=_params(3),
    )(qkv, qkv, qkv, kt, dya, delta, lse, c, ct)


def _fgate_bwd(dcq, dck, fz, fb):
    T = dck.shape[0]
    tb = _tile(T, 512)
    nb = T // tb

    def body(dcq_ref, dck_ref, fz_ref, fb_ref, df_ref, dfb_ref, carry):
        @pl.when(pl.program_id(0) == 0)
        def _():
            carry[...] = jnp.zeros_like(carry)
            dfb_ref[...] = jnp.zeros_like(dfb_ref)

        dcv = dcq_ref[...] + dck_ref[...]
        r = lax.broadcasted_iota(jnp.int32, (tb, tb), 0)
        cc = lax.broadcasted_iota(jnp.int32, (tb, tb), 1)
        tri = (cc >= r).astype(F32)
        dlf = jnp.dot(tri, dcv, precision=lax.Precision.HIGHEST, preferred_element_type=F32) + carry[...]
        carry[...] = carry[...] + jnp.sum(dcv, axis=0, keepdims=True)
        lane = lax.broadcasted_iota(jnp.int32, (tb, FPAD), 1)
        df = jnp.where(lane < 8, dlf * jax.nn.sigmoid(-(fz_ref[...] + fb_ref[...])), 0.0)
        df_ref[...] = df.astype(BF16)
        dfb_ref[...] += jnp.sum(df, axis=0, keepdims=True)

    rev = pl.BlockSpec((tb, FPAD), lambda i: (nb - 1 - i, 0))
    return pl.pallas_call(
        body,
        name="fgate_bwd",
        grid=(nb,),
        in_specs=[rev, rev, rev, _full((1, FPAD))],
        out_specs=[rev, _full((1, FPAD))],
        out_shape=[jax.ShapeDtypeStruct((T, FPAD), BF16), jax.ShapeDtypeStruct((1, FPAD), F32)],
        scratch_shapes=[pltpu.VMEM((1, FPAD), F32)],
        compiler_params=_params(1),
    )(dcq, dck, fz, fb)


def _inproj_bwd(dq, dk, dv, dus, dfz, wp, x, dh1, g):
    T = x.shape[0]
    tm = _tile(T, 512)

    def body(dq_ref, dk_ref, dv_ref, dus_ref, dfz_ref, w_ref, x_ref, dh1_ref, g_ref, gx_ref, dg_ref):
        @pl.when(pl.program_id(0) == 0)
        def _():
            dg_ref[...] = jnp.zeros_like(dg_ref)

        da = _nt(dq_ref[...].astype(BF16), w_ref[:, 0:AW])
        da += _nt(dk_ref[...], w_ref[:, AW:2 * AW])
        da += _nt(dv_ref[...], w_ref[:, 2 * AW:NQKV])
        da += _nt(dus_ref[...], w_ref[:, NQKV:NQKV + NUS])
        da += _nt(dfz_ref[...], w_ref[:, NQKV + NUS:ZP])
        dx, dg = _rms_bwd(da, x_ref[...], g_ref[...])
        gx_ref[...] = dh1_ref[...] + dx
        dg_ref[...] += dg

    row = lambda n: pl.BlockSpec((tm, n), lambda i: (i, 0))
    return pl.pallas_call(
        body,
        name="inproj_bwd",
        grid=(T // tm,),
        in_specs=[row(AW), row(AW), row(AW), row(NUS), row(FPAD), _full((D, ZP)), row(D), row(D), _full((1, D))],
        out_specs=[row(D), _full((1, D))],
        out_shape=[jax.ShapeDtypeStruct((T, D), F32), jax.ShapeDtypeStruct((1, D), F32)],
        compiler_params=_params(1),
    )(dq, dk, dv, dus, dfz, wp, x, dh1, g)


def _wgrad(a, b, name, colblk=None):
    T, K = a.shape
    N = b.shape[1]
    bk = min(K, 1024)
    bn = colblk if colblk else min(N, 1024)
    tt = _tile(T, 512)

    def body(a_ref, b_ref, o_ref):
        @pl.when(pl.program_id(2) == 0)
        def _():
            o_ref[...] = jnp.zeros_like(o_ref)

        o_ref[...] += _tn(a_ref[...].astype(BF16), b_ref[...].astype(BF16))

    if colblk:
        out_shape = jax.ShapeDtypeStruct((N // bn, K, bn), F32)
        out_spec = pl.BlockSpec((None, bk, bn), lambda k, n, t: (n, k, 0))
    else:
        out_shape = jax.ShapeDtypeStruct((K, N), F32)
        out_spec = pl.BlockSpec((bk, bn), lambda k, n, t: (k, n))
    return pl.pallas_call(
        body,
        name=name,
        grid=(K // bk, N // bn, T // tt),
        in_specs=[pl.BlockSpec((tt, bk), lambda k, n, t: (t, k)), pl.BlockSpec((tt, bn), lambda k, n, t: (t, n))],
        out_specs=out_spec,
        out_shape=out_shape,
        compiler_params=_params(3),
    )(a, b)


def _adam_math(w, g, m, v):
    m = ADAM_B1 * m + (1.0 - ADAM_B1) * g
    v = ADAM_B2 * v + (1.0 - ADAM_B2) * (g * g)
    m_hat = m / (1.0 - ADAM_B1 ** ADAM_STEP)
    v_hat = v / (1.0 - ADAM_B2 ** ADAM_STEP)
    delta = -ADAM_LR * (m_hat / (jnp.sqrt(v_hat) + ADAM_EPS) + ADAM_WD * w)
    return delta, m, v


def _adam(parts, w, m, v, name):
    R, C = w.shape
    br = 128 if R % 128 == 0 else R

    def body(p_ref, w_ref, m_ref, v_ref, g_ref, d_ref, nm_ref, nv_ref):
        g = p_ref[0]
        for s in range(1, NDEV):
            g = g + p_ref[s]
        g_ref[...] = g
        d_ref[...], nm_ref[...], nv_ref[...] = _adam_math(w_ref[...], g, m_ref[...], v_ref[...])

    blk = pl.BlockSpec((br, C), lambda i: (i, 0))
    return pl.pallas_call(
        body,
        name=name,
        grid=(R // br,),
        in_specs=[pl.BlockSpec((NDEV, br, C), lambda i: (0, i, 0)), blk, blk, blk],
        out_specs=[blk] * 4,
        out_shape=[jax.ShapeDtypeStruct((R, C), F32)] * 4,
        compiler_params=_params(1),
    )(parts, w, m, v)


_SMALL = (("sg_w", 8 * CH * CH), ("f_bias", 8), ("sg_ln_g", SW), ("sg_ln_b", SW), ("sg_b", 8 * CH), ("att_out_g", AW),
          ("sg_out_g", SW), ("pre_mix_g", D), ("post_mix_g", D), ("pre_ffn_g", D), ("post_ffn_g", D), ("ple_gate_b", D))
_SEG = 8 * 128


def _seg_rows(size):
    return 8 * (-(-size // _SEG))


def _pack(vals):
    parts = []
    for name, size in _SMALL:
        flat = vals[name].reshape(-1)
        rows = _seg_rows(size)
        parts.append(jnp.pad(flat, (0, rows * 128 - size)).reshape(rows, 128))
    return jnp.concatenate(parts, axis=0)


def _unpack(packed, shapes):
    out, r = {}, 0
    for name, size in _SMALL:
        rows = _seg_rows(size)
        out[name] = packed[r:r + rows].reshape(-1)[:size].reshape(shapes[name])
        r += rows
    return out


def kernel(x, p, w_in, f_bias, sg_ln_g, sg_ln_b, sg_w, sg_b, att_out_g, sg_out_g, w_out, pre_mix_g, post_mix_g, pre_ffn_g, post_ffn_g, w_ff1, w_ff2, ple_w, ple_gate_w, ple_gate_b, loss_target, m_w_in, m_f_bias, m_sg_ln_g, m_sg_ln_b, m_sg_w, m_sg_b, m_att_out_g, m_sg_out_g, m_w_out, m_pre_mix_g, m_post_mix_g, m_pre_ffn_g, m_post_ffn_g, m_w_ff1, m_w_ff2, m_ple_w, m_ple_gate_w, m_ple_gate_b, v_w_in, v_f_bias, v_sg_ln_g, v_sg_ln_b, v_sg_w, v_sg_b, v_att_out_g, v_sg_out_g, v_w_out, v_pre_mix_g, v_post_mix_g, v_pre_ffn_g, v_post_ffn_g, v_w_ff1, v_w_ff2, v_ple_w, v_ple_gate_w, v_ple_gate_b):
    small_w = dict(sg_w=sg_w, f_bias=f_bias, sg_ln_g=sg_ln_g, sg_ln_b=sg_ln_b, sg_b=sg_b, att_out_g=att_out_g,
                   sg_out_g=sg_out_g, pre_mix_g=pre_mix_g, post_mix_g=post_mix_g, pre_ffn_g=pre_ffn_g,
                   post_ffn_g=post_ffn_g, ple_gate_b=ple_gate_b)
    small_m = dict(sg_w=m_sg_w, f_bias=m_f_bias, sg_ln_g=m_sg_ln_g, sg_ln_b=m_sg_ln_b, sg_b=m_sg_b, att_out_g=m_att_out_g,
                   sg_out_g=m_sg_out_g, pre_mix_g=m_pre_mix_g, post_mix_g=m_post_mix_g, pre_ffn_g=m_pre_ffn_g,
                   post_ffn_g=m_post_ffn_g, ple_gate_b=m_ple_gate_b)
    small_v = dict(sg_w=v_sg_w, f_bias=v_f_bias, sg_ln_g=v_sg_ln_g, sg_ln_b=v_sg_ln_b, sg_b=v_sg_b, att_out_g=v_att_out_g,
                   sg_out_g=v_sg_out_g, pre_mix_g=v_pre_mix_g, post_mix_g=v_post_mix_g, pre_ffn_g=v_pre_ffn_g,
                   post_ffn_g=v_post_ffn_g, ple_gate_b=v_ple_gate_b)
    big = dict(w_in=(w_in, m_w_in, v_w_in), w_out=(w_out, m_w_out, v_w_out), w_ff1=(w_ff1, m_w_ff1, v_w_ff1),
               w_ff2=(w_ff2, m_w_ff2, v_w_ff2), ple_w=(ple_w, m_ple_w, v_ple_w),
               ple_gate_w=(ple_gate_w, m_ple_gate_w, v_ple_gate_w))

    xt, pt, tgt = x[0], p[0, 0], loss_target[0]
    ws = W_IN_COLS // NDEV

    gw_in, gw_out, gw1, gw2, gwpe, gwg = _xchg(
        [w_in[0].astype(BF16), w_out[0].astype(BF16), w_ff1[0].astype(BF16), w_ff2[0].astype(BF16),
         ple_w[0].astype(BF16), ple_gate_w[0].astype(BF16)], True, "gather_weights")
    win = jnp.transpose(gw_in, (1, 0, 2)).reshape(D, W_IN_COLS)
    wp = jnp.concatenate([win[:, 0:NQKV], win[:, NQKV + 8:W_IN_COLS], win[:, NQKV:NQKV + 8],
                          jnp.zeros((D, FPAD - 8), BF16)], axis=1)
    wout = gw_out.reshape(D, D)
    wg = gwg.reshape(D, D)
    wpe = jnp.transpose(gwpe, (1, 0, 2)).reshape(PLE, D)

    fb = jnp.pad(f_bias.astype(F32), ((0, 0), (0, FPAD - 8)))
    sgw = sg_w[0]
    sgwt = jnp.transpose(sg_w[0], (0, 2, 1))
    sgbt = jnp.transpose(sg_b[0])

    qkv, kt, vt, us, fz, ab = _inproj(xt, pre_mix_g, wp)
    c, ct = _fcum(fz, fb)
    yatt, lse = _attn_fwd(qkv, vt, c, ct)
    lse = lse[:, 0:2, :].reshape(8, -1)
    h1, yb, o = _sgu_out(us, yatt, xt, sg_ln_g, sg_ln_b, sgw, sgbt, att_out_g, sg_out_g, wout, post_mix_g)
    c2b, f1, act, ff, h2, h2b = _ffn_fwd(h1, pre_ffn_g, gw1, gw2, post_ffn_g)
    dh2, dpre, dpe, pb, dbg, loss_acc = _ple_loss(h2, pt, tgt, wg, ple_gate_b, wpe)
    loss = lax.psum(loss_acc[0, 0] * (0.5 / D), ("x", "y", "c"))

    dffb, df1, dh1, dgpostffn, dgpreffn = _ffn_bwd(dh2, ff, h1, f1, gw1, gw2, post_ffn_g, pre_ffn_g)
    dob, dya, dus, dsgw, dsgbt, dlng, dlnb, dgatt, dgsg, dgpostmix = _mix_bwd(
        dh1, o, us, yatt, sg_ln_g, sg_ln_b, sgw, sgwt, sgbt, att_out_g, sg_out_g, wout, post_mix_g)
    dqt, dcq, dk, dv, dck = _attn_bwd(qkv, kt, dya, _attn_delta(dya, yatt), lse, c, ct)
    dq = jnp.transpose(dqt)
    dcq = jnp.pad(jnp.transpose(dcq[:, 0:2, :].reshape(8, -1)), ((0, 0), (0, FPAD - 8)))
    dck = jnp.pad(dck.reshape(-1, 4, 128)[:, :, 0:2].reshape(-1, 8), ((0, 0), (0, FPAD - 8)))
    dfz, dfb = _fgate_bwd(dcq, dck, fz, fb)
    grad_x, dgpremix = _inproj_bwd(dq, dk, dv, dus, dfz, wp, xt, dh1, pre_mix_g)

    gq = _wgrad(ab, dq, "wgrad_q")
    gk = _wgrad(ab, dk, "wgrad_k")
    gv = _wgrad(ab, dv, "wgrad_v")
    gus = _wgrad(ab, dus, "wgrad_us")
    gf = _wgrad(ab, dfz, "wgrad_f")
    g_in = jnp.concatenate([gq, gk, gv, gf[:, 0:8], gus], axis=1)
    g_in = jnp.transpose(g_in.reshape(D, NDEV, ws), (1, 0, 2))
    g_out = _wgrad(yb, dob, "wgrad_out").reshape(NDEV, D // NDEV, D)
    g_1 = _wgrad(c2b, df1, "wgrad_ff1", colblk=DFF // NDEV)
    g_2 = _wgrad(act, dffb, "wgrad_ff2").reshape(NDEV, DFF // NDEV, D)
    g_pe = _wgrad(pb, dpe, "wgrad_ple", colblk=D // NDEV)
    g_g = _wgrad(h2b, dpre, "wgrad_gate").reshape(NDEV, D // NDEV, D)

    small_g = dict(sg_w=dsgw, f_bias=dfb[:, 0:8], sg_ln_g=dlng, sg_ln_b=dlnb, sg_b=jnp.transpose(dsgbt[:, 0:8]),
                   att_out_g=dgatt, sg_out_g=dgsg, pre_mix_g=dgpremix, post_mix_g=dgpostmix, pre_ffn_g=dgpreffn,
                   post_ffn_g=dgpostffn, ple_gate_b=dbg)

    r_in, r_out, r_1, r_2, r_pe, r_g = _xchg([g_in, g_out, g_1, g_2, g_pe, g_g], False, "scatter_grads")
    (r_small,) = _xchg([_pack(small_g)], True, "gather_small_grads")

    res = {}
    for name, parts in (("w_in", r_in), ("w_out", r_out), ("w_ff1", r_1), ("w_ff2", r_2), ("ple_w", r_pe),
                        ("ple_gate_w", r_g)):
        w, m, v = big[name]
        res[name] = [t[None] for t in _adam(parts, w[0], m[0], v[0], "adam_" + name)]
    shapes = {k: a.shape for k, a in small_w.items()}
    small = [_unpack(t, shapes) for t in _adam(r_small, _pack(small_w), _pack(small_m), _pack(small_v), "adam_small")]
    for name, _ in _SMALL:
        res[name] = [s[name] for s in small]

    order = ["w_in", "f_bias", "sg_ln_g", "sg_ln_b", "sg_w", "sg_b", "att_out_g", "sg_out_g", "w_out", "pre_mix_g",
             "post_mix_g", "pre_ffn_g", "post_ffn_g", "w_ff1", "w_ff2", "ple_w", "ple_gate_w", "ple_gate_b"]
    outs = [loss, grad_x[None]]
    for kind in range(4):
        outs += [res[name][kind] for name in order]
    return tuple(outs)
```

```python
import jax
import jax.numpy as jnp
from jax import lax
from jax.experimental import pallas as pl
from jax.experimental.pallas import tpu as pltpu

F32 = jnp.float32
BF16 = jnp.bfloat16

NDEV = 8
D = 1024
AW = 512
SW = 512
HD = 64
CH = 128
DFF = 4096
PLE = 256
NQKV = 3 * AW
NUS = 2 * SW
FPAD = 128
ZP = NQKV + NUS + FPAD
W_IN_COLS = 2568
EPS = 1e-6
MASKV = -1e30
GELU_K = 0.7978845608028654
GELU_C = 0.044715

ADAM_LR = 0.001
ADAM_B1 = 0.9
ADAM_B2 = 0.999
ADAM_EPS = 1e-08
ADAM_WD = 0.01
ADAM_STEP = 10

VMEM_LIMIT = 48 * 1024 * 1024


def _nn(a, b):
    return jnp.dot(a, b, preferred_element_type=F32)


def _nt(a, b):
    return lax.dot_general(a, b, (((1,), (1,)), ((), ())), preferred_element_type=F32)


def _tn(a, b):
    return lax.dot_general(a, b, (((0,), (0,)), ((), ())), preferred_element_type=F32)


def _tile(n, pref):
    return min(n, pref)


def _params(n_axes):
    return pltpu.CompilerParams(dimension_semantics=("arbitrary",) * n_axes, vmem_limit_bytes=VMEM_LIMIT)


def _full(shape):
    nd = len(shape)
    return pl.BlockSpec(shape, lambda *_: (0,) * nd)


def _rms_fwd(x, g):
    r = lax.rsqrt(jnp.mean(x * x, axis=-1, keepdims=True) + EPS)
    return x * r * g


def _rms_bwd(dy, x, g):
    n = x.shape[-1]
    r = lax.rsqrt(jnp.mean(x * x, axis=-1, keepdims=True) + EPS)
    u = dy * g
    s = jnp.sum(x * u, axis=-1, keepdims=True)
    dx = r * u - x * (r * r * r * (s * (1.0 / n)))
    dg = jnp.sum(dy * (x * r), axis=0, keepdims=True)
    return dx, dg


def _gelu(x):
    t = jnp.tanh(GELU_K * (x + GELU_C * (x * x * x)))
    return x * (0.5 * (1.0 + t)), t


def _gelu_grad(x, t):
    return 0.5 * (1.0 + t) + 0.5 * x * (1.0 - t * t) * (GELU_K * (1.0 + 3.0 * GELU_C * x * x))


def _xchg(arrs, gather, name):
    n = len(arrs)
    out_shapes = [jax.ShapeDtypeStruct(((NDEV,) + a.shape) if gather else a.shape, a.dtype) for a in arrs]

    def body(*refs):
        ins, outs = refs[:n], refs[n:2 * n]
        send, recv, loc = refs[2 * n:]
        x, y, c = lax.axis_index("x"), lax.axis_index("y"), lax.axis_index("c")
        me = 4 * x + 2 * y + c
        local = []
        for a in range(n):
            cp = pltpu.make_async_copy(ins[a] if gather else ins[a].at[me], outs[a].at[me], loc.at[a])
            cp.start()
            local.append(cp)
        remote = []
        for k in range(1, NDEV):
            px = 1 - x if (k >> 2) & 1 else x
            py = 1 - y if (k >> 1) & 1 else y
            pc = 1 - c if k & 1 else c
            peer = 4 * px + 2 * py + pc
            for a in range(n):
                cp = pltpu.make_async_remote_copy(
                    src_ref=ins[a] if gather else ins[a].at[peer],
                    dst_ref=outs[a].at[me],
                    send_sem=send.at[a, k - 1],
                    recv_sem=recv.at[a, k - 1],
                    device_id=(px, py, pc),
                    device_id_type=pl.DeviceIdType.MESH,
                )
                cp.start()
                remote.append(cp)
        for cp in remote:
            cp.wait_send()
            cp.wait_recv()
        for cp in local:
            cp.wait()

    hbm = pl.BlockSpec(memory_space=pltpu.HBM)
    return pl.pallas_call(
        body,
        name=name,
        out_shape=out_shapes,
        in_specs=[hbm] * n,
        out_specs=[hbm] * n,
        scratch_shapes=[
            pltpu.SemaphoreType.DMA((n, NDEV - 1)),
            pltpu.SemaphoreType.DMA((n, NDEV - 1)),
            pltpu.SemaphoreType.DMA((n,)),
        ],
    )(*arrs)


def _peers(x, y, c):
    out = []
    for k in range(1, NDEV):
        out.append((1 - x if (k >> 2) & 1 else x, 1 - y if (k >> 1) & 1 else y, 1 - c if k & 1 else c))
    return out


def _xchg_start(arrs, gather, name):
    n = len(arrs)
    me = 4 * lax.axis_index("x") + 2 * lax.axis_index("y") + lax.axis_index("c")
    lands = []
    for a in arrs:
        shape = ((NDEV,) + a.shape) if gather else a.shape
        own = a[None] if gather else lax.dynamic_slice_in_dim(a, me, 1, axis=0)
        lands.append(lax.dynamic_update_slice_in_dim(lax.empty(shape, a.dtype), own, me, axis=0))

    def body(*refs):
        ins, lnd = refs[:n], refs[n:2 * n]
        send, recv, token = refs[2 * n:3 * n], refs[3 * n:4 * n], refs[-1]
        x, y, c = lax.axis_index("x"), lax.axis_index("y"), lax.axis_index("c")
        mine = 4 * x + 2 * y + c
        for px, py, pc in _peers(x, y, c):
            peer = 4 * px + 2 * py + pc
            for a in range(n):
                pltpu.make_async_remote_copy(
                    src_ref=ins[a] if gather else ins[a].at[peer],
                    dst_ref=lnd[a].at[mine],
                    send_sem=send[a],
                    recv_sem=recv[a],
                    device_id=(px, py, pc),
                    device_id_type=pl.DeviceIdType.MESH,
                ).start()
        token[...] = jnp.zeros_like(token)

    hbm = pl.BlockSpec(memory_space=pltpu.HBM)
    sem = pl.BlockSpec(memory_space=pltpu.SEMAPHORE)
    res = pl.pallas_call(
        body,
        name=name,
        out_shape=(*[pltpu.SemaphoreType.DMA(())] * (2 * n),
                   *[pltpu.HBM(a.shape, a.dtype) for a in arrs], *[pltpu.HBM(l.shape, l.dtype) for l in lands],
                   jax.ShapeDtypeStruct((8, 128), F32)),
        in_specs=[hbm] * (2 * n),
        out_specs=(*([sem] * (2 * n)), *([hbm] * (2 * n)), pl.BlockSpec(memory_space=pltpu.VMEM)),
        input_output_aliases={i: 2 * n + i for i in range(2 * n)},
        compiler_params=pltpu.CompilerParams(has_side_effects=pltpu.SideEffectType.DATAFLOW_SIDE_EFFECTING),
    )(*[pltpu.with_memory_space_constraint(a, pltpu.HBM) for a in arrs],
      *[pltpu.with_memory_space_constraint(l, pltpu.HBM) for l in lands])
    return list(res[0:n]), list(res[n:2 * n]), list(res[2 * n:3 * n]), list(res[3 * n:4 * n]), res[-1]


def _xchg_wait(started, gather, after, name):
    send, recv, srcs, lands, _ = started
    n = len(srcs)

    def body(*refs):
        lnd = refs[n:2 * n]
        send, recv = refs[2 * n:3 * n], refs[3 * n:4 * n]
        me = (lax.axis_index("x"), lax.axis_index("y"), lax.axis_index("c"))
        for a in range(n):
            seven = lnd[a].at[pl.ds(0, NDEV - 1)]
            cp = pltpu.make_async_remote_copy(src_ref=seven, dst_ref=seven, send_sem=send[a], recv_sem=recv[a],
                                              device_id=me, device_id_type=pl.DeviceIdType.MESH)
            cp.wait_send()
            cp.wait_recv()

    hbm = pl.BlockSpec(memory_space=pltpu.HBM)
    sem = pl.BlockSpec(memory_space=pltpu.SEMAPHORE)
    res = pl.pallas_call(
        body,
        name=name,
        out_shape=tuple([pltpu.HBM(a.shape, a.dtype) for a in srcs] + [pltpu.HBM(l.shape, l.dtype) for l in lands]),
        in_specs=[hbm] * (2 * n) + [sem] * (2 * n) + [pl.BlockSpec(memory_space=pl.ANY)],
        out_specs=tuple([hbm] * (2 * n)),
        input_output_aliases={i: i for i in range(2 * n)},
        compiler_params=pltpu.CompilerParams(has_side_effects=pltpu.SideEffectType.DATAFLOW_SIDE_EFFECTING),
    )(*srcs, *lands, *send, *recv, after)
    return list(res[n:])


def _inproj(x, g, wp):
    T = x.shape[0]
    tm = _tile(T, 512)

    def body(x_ref, g_ref, w_ref, qkv_ref, kt_ref, vt_ref, us_ref, fz_ref, ab_ref):
        a = _rms_fwd(x_ref[...], g_ref[...]).astype(BF16)
        ab_ref[...] = a
        qkv_ref[:, 0:AW] = _nn(a, w_ref[:, 0:AW]).astype(BF16)
        kk = _nn(a, w_ref[:, AW:2 * AW])
        qkv_ref[:, AW:2 * AW] = kk.astype(BF16)
        kt_ref[...] = kk.T.astype(BF16)
        vv = _nn(a, w_ref[:, 2 * AW:NQKV])
        qkv_ref[:, 2 * AW:NQKV] = vv.astype(BF16)
        vt_ref[...] = vv.T.astype(BF16)
        us_ref[...] = _nn(a, w_ref[:, NQKV:NQKV + NUS])
        fz_ref[...] = _nn(a, w_ref[:, NQKV + NUS:ZP])

    row = lambda n: pl.BlockSpec((tm, n), lambda i: (i, 0))
    col = pl.BlockSpec((AW, tm), lambda i: (0, i))
    return pl.pallas_call(
        body,
        name="inproj",
        grid=(T // tm,),
        in_specs=[row(D), _full((1, D)), _full((D, ZP))],
        out_specs=[row(NQKV), col, col, row(NUS), row(FPAD), row(D)],
        out_shape=[
            jax.ShapeDtypeStruct((T, NQKV), BF16),
            jax.ShapeDtypeStruct((AW, T), BF16),
            jax.ShapeDtypeStruct((AW, T), BF16),
            jax.ShapeDtypeStruct((T, NUS), F32),
            jax.ShapeDtypeStruct((T, FPAD), F32),
            jax.ShapeDtypeStruct((T, D), BF16),
        ],
        compiler_params=_params(1),
    )(x, g, wp)


def _log_sigmoid(z):
    return jnp.minimum(z, 0.0) - jnp.log1p(jnp.exp(-jnp.abs(z)))


def _fcum(fz, fb):
    T = fz.shape[0]
    tb = _tile(T, 512)

    def body(fz_ref, fb_ref, c_ref, ct_ref, carry):
        @pl.when(pl.program_id(0) == 0)
        def _():
            carry[...] = jnp.zeros_like(carry)

        lf = _log_sigmoid(fz_ref[...] + fb_ref[...])
        r = lax.broadcasted_iota(jnp.int32, (tb, tb), 0)
        cc = lax.broadcasted_iota(jnp.int32, (tb, tb), 1)
        tri = (cc <= r).astype(F32)
        cs = jnp.dot(tri, lf, precision=lax.Precision.HIGHEST, preferred_element_type=F32) + carry[...]
        c_ref[...] = cs
        ct_ref[...] = cs.T[0:8, :]
        carry[...] = carry[...] + jnp.sum(lf, axis=0, keepdims=True)

    return pl.pallas_call(
        body,
        name="fcum",
        grid=(T // tb,),
        in_specs=[pl.BlockSpec((tb, FPAD), lambda i: (i, 0)), _full((1, FPAD))],
        out_specs=[pl.BlockSpec((tb, FPAD), lambda i: (i, 0)), pl.BlockSpec((8, tb), lambda i: (0, i))],
        out_shape=[jax.ShapeDtypeStruct((T, FPAD), F32), jax.ShapeDtypeStruct((8, T), F32)],
        scratch_shapes=[pltpu.VMEM((1, FPAD), F32)],
        compiler_params=_params(1),
    )(fz, fb)


def _key_bias(c_keys, ct_q, h):
    lane = lax.broadcasted_iota(jnp.int32, c_keys.shape, 1)
    sub = lax.broadcasted_iota(jnp.int32, (8, 1), 0)
    ck = jnp.sum(jnp.where(lane == h, c_keys, 0.0), axis=1, keepdims=True)
    cref = jnp.sum(jnp.where(sub == h, ct_q[:, 0:1], 0.0), axis=0, keepdims=True)
    return cref - ck


def _stack_heads(x2, dst, tq, scale):
    is_a = lax.broadcasted_iota(jnp.int32, (1, 128), 1) < HD
    zero = jnp.zeros_like(x2)
    dst[0:tq, :] = (jnp.where(is_a, x2, zero) * scale).astype(dst.dtype)
    dst[tq:2 * tq, :] = (jnp.where(is_a, zero, x2) * scale).astype(dst.dtype)


def _attn_fwd(qkv, vt, c, ct):
    T = qkv.shape[0]
    tq = _tile(T, 512)
    tk = tq
    nq = T // tq

    def body(q_ref, k_ref, vt_ref, c_ref, ctq_ref, o_ref, lse_ref, qw_s, m_s, l_s, acc_s):
        hp, i, j = pl.program_id(0), pl.program_id(1), pl.program_id(2)

        @pl.when(j == 0)
        def _():
            _stack_heads(q_ref[...], qw_s, tq, 0.125)
            m_s[...] = jnp.full_like(m_s, MASKV)
            l_s[...] = jnp.zeros_like(l_s)
            acc_s[...] = jnp.zeros_like(acc_s)

        def step(masked):
            s2 = _nt(k_ref[...], qw_s[...])
            vt2 = vt_ref[...]
            for hh in range(2):
                s = s2[:, hh * tq:(hh + 1) * tq] + _key_bias(c_ref[...], ctq_ref[...], 2 * hp + hh)
                if masked:
                    key = lax.broadcasted_iota(jnp.int32, (tk, tq), 0)
                    qry = lax.broadcasted_iota(jnp.int32, (tk, tq), 1)
                    s = jnp.where(key <= qry, s, MASKV)
                m_prev = m_s[hh]
                m_new = jnp.maximum(m_prev, jnp.max(s, axis=0, keepdims=True))
                pr = jnp.exp(s - m_new)
                alpha = jnp.exp(m_prev - m_new)
                l_s[hh] = alpha * l_s[hh] + jnp.sum(pr, axis=0, keepdims=True)
                m_s[hh] = m_new
                acc_s[hh] = alpha * acc_s[hh] + _nn(vt2, pr.astype(BF16))

        @pl.when(j < i)
        def _():
            step(False)

        @pl.when(j == i)
        def _():
            step(True)
            sub = lax.broadcasted_iota(jnp.int32, (128, 1), 0)
            ot = jnp.where(sub < HD, acc_s[0] * (1.0 / l_s[0]), acc_s[1] * (1.0 / l_s[1]))
            o_ref[...] = ot.T
            sub8 = lax.broadcasted_iota(jnp.int32, (8, 1), 0)
            lse_ref[...] = jnp.where(sub8 == 0, m_s[0] + jnp.log(l_s[0]),
                                     jnp.where(sub8 == 1, m_s[1] + jnp.log(l_s[1]), 0.0))

    return pl.pallas_call(
        body,
        name="attn_fwd",
        grid=(4, nq, nq),
        in_specs=[
            pl.BlockSpec((tq, 128), lambda hp, i, j: (i, hp)),
            pl.BlockSpec((tk, 128), lambda hp, i, j: (jnp.minimum(j, i), 4 + hp)),
            pl.BlockSpec((128, tk), lambda hp, i, j: (hp, jnp.minimum(j, i))),
            pl.BlockSpec((tk, FPAD), lambda hp, i, j: (jnp.minimum(j, i), 0)),
            pl.BlockSpec((8, tq), lambda hp, i, j: (0, i)),
        ],
        out_specs=[pl.BlockSpec((tq, 128), lambda hp, i, j: (i, hp)),
                   pl.BlockSpec((None, 8, tq), lambda hp, i, j: (hp, 0, i))],
        out_shape=[jax.ShapeDtypeStruct((T, AW), F32), jax.ShapeDtypeStruct((4, 8, T), F32)],
        scratch_shapes=[pltpu.VMEM((2 * tq, 128), BF16), pltpu.VMEM((2, 1, tq), F32), pltpu.VMEM((2, 1, tq), F32),
                        pltpu.VMEM((2, 128, tq), F32)],
        compiler_params=_params(3),
    )(qkv, qkv, vt, c, ct)


def _sgu_forward(us_ref, lng, lnb, w_ref, bt_ref, mixed_s, vnb_s, tm):
    is_a = lax.broadcasted_iota(jnp.int32, (1, 128), 1) < HD
    u = us_ref[:, 0:SW]
    vs = us_ref[:, SW:NUS]
    ug, tu = _gelu(u)
    vg, tv = _gelu(vs)
    mu = jnp.mean(vg, axis=-1, keepdims=True)
    xc = vg - mu
    rstd = lax.rsqrt(jnp.mean(xc * xc, axis=-1, keepdims=True) + EPS)
    vhat = xc * rstd
    vnb_s[...] = (vhat * lng + lnb).astype(BF16)
    rr = lax.broadcasted_iota(jnp.int32, (CH, CH), 0)
    cc = lax.broadcasted_iota(jnp.int32, (CH, CH), 1)
    tril = cc <= rr
    for jj in range(4):
        wa = jnp.where(tril, w_ref[2 * jj], 0.0).astype(BF16)
        wb = jnp.where(tril, w_ref[2 * jj + 1], 0.0).astype(BF16)
        ba = bt_ref[:, 2 * jj:2 * jj + 1]
        bb = bt_ref[:, 2 * jj + 1:2 * jj + 2]
        for ch in range(tm // CH):
            rs, cs = slice(ch * CH, (ch + 1) * CH), slice(jj * 128, (jj + 1) * 128)
            vn2 = vnb_s[rs, cs]
            mixed_s[rs, cs] = jnp.where(is_a, _nn(wa, vn2) + ba, _nn(wb, vn2) + bb)
    mixed = mixed_s[...]
    return u, vs, ug, tu, tv, vhat, rstd, mixed, ug * mixed


def _sgu_out(us, yatt, x, lng, lnb, sgw, sgbt, gatt, gsg, wout, gpm):
    T = us.shape[0]
    tm = _tile(T, 256)

    def body(us_ref, ya_ref, x_ref, lng_ref, lnb_ref, w_ref, bt_ref, ga_ref, gs_ref, wo_ref, gp_ref,
             h1_ref, yb_ref, o_ref, mixed_s, vnb_s):
        ysg = _sgu_forward(us_ref, lng_ref[...], lnb_ref[...], w_ref, bt_ref, mixed_s, vnb_s, tm)[-1]
        yb_ref[:, 0:AW] = _rms_fwd(ya_ref[...], ga_ref[...]).astype(BF16)
        yb_ref[:, AW:D] = _rms_fwd(ysg, gs_ref[...]).astype(BF16)
        o = _nn(yb_ref[...], wo_ref[...])
        o_ref[...] = o
        h1_ref[...] = x_ref[...] + _rms_fwd(o, gp_ref[...])

    row = lambda n: pl.BlockSpec((tm, n), lambda i: (i, 0))
    return pl.pallas_call(
        body,
        name="sgu_out",
        grid=(T // tm,),
        in_specs=[row(NUS), row(AW), row(D), _full((1, SW)), _full((1, SW)), _full((8, CH, CH)), _full((CH, 8)),
                  _full((1, AW)), _full((1, SW)), _full((D, D)), _full((1, D))],
        out_specs=[row(D), row(D), row(D)],
        out_shape=[jax.ShapeDtypeStruct((T, D), F32), jax.ShapeDtypeStruct((T, D), BF16), jax.ShapeDtypeStruct((T, D), F32)],
        scratch_shapes=[pltpu.VMEM((tm, SW), F32), pltpu.VMEM((tm, SW), BF16)],
        compiler_params=_params(1),
    )(us, yatt, x, lng, lnb, sgw, sgbt, gatt, gsg, wout, gpm)


def _ffn_fwd(h1, gpre, w1g, w2g, gpost):
    T = h1.shape[0]
    tm = _tile(T, 512)
    nb, hb = w1g.shape[0], w1g.shape[2]

    def body(h1_ref, gpre_ref, w1_ref, w2_ref, gpost_ref, c2_ref, f1_ref, act_ref, ff_ref, h2_ref, h2b_ref, acc_s):
        j = pl.program_id(1)

        @pl.when(j == 0)
        def _():
            c2_ref[...] = _rms_fwd(h1_ref[...], gpre_ref[...]).astype(BF16)
            acc_s[...] = jnp.zeros_like(acc_s)

        f1 = _nn(c2_ref[...], w1_ref[...])
        f1_ref[...] = f1
        r = jnp.maximum(f1, 0.0)
        act = (r * r).astype(BF16)
        act_ref[...] = act
        acc_s[...] += _nn(act, w2_ref[...])

        @pl.when(j == nb - 1)
        def _():
            ff = acc_s[...]
            ff_ref[...] = ff
            h2 = h1_ref[...] + _rms_fwd(ff, gpost_ref[...])
            h2_ref[...] = h2
            h2b_ref[...] = h2.astype(BF16)

    row = lambda n: pl.BlockSpec((tm, n), lambda i, j: (i, 0))
    return pl.pallas_call(
        body,
        name="ffn_fwd",
        grid=(T // tm, nb),
        in_specs=[row(D), _full((1, D)), pl.BlockSpec((None, D, hb), lambda i, j: (j, 0, 0)),
                  pl.BlockSpec((None, hb, D), lambda i, j: (j, 0, 0)), _full((1, D))],
        out_specs=[row(D), pl.BlockSpec((tm, hb), lambda i, j: (i, j)), pl.BlockSpec((tm, hb), lambda i, j: (i, j)),
                   row(D), row(D), row(D)],
        out_shape=[jax.ShapeDtypeStruct((T, D), BF16), jax.ShapeDtypeStruct((T, DFF), F32),
                   jax.ShapeDtypeStruct((T, DFF), BF16), jax.ShapeDtypeStruct((T, D), F32),
                   jax.ShapeDtypeStruct((T, D), F32), jax.ShapeDtypeStruct((T, D), BF16)],
        scratch_shapes=[pltpu.VMEM((tm, D), F32)],
        compiler_params=_params(2),
    )(h1, gpre, w1g, w2g, gpost)


def _ple_loss(h2, p, tgt, wg, bg, wpe):
    T = h2.shape[0]
    tm = _tile(T, 512)

    def body(h2_ref, p_ref, t_ref, wg_ref, bg_ref, wpe_ref, dh2_ref, dpre_ref, dpe_ref, pb_ref, dbg_ref, loss_ref):
        @pl.when(pl.program_id(0) == 0)
        def _():
            dbg_ref[...] = jnp.zeros_like(dbg_ref)
            loss_ref[...] = jnp.zeros_like(loss_ref)

        h2 = h2_ref[...]
        gate = jax.nn.sigmoid(_nn(h2.astype(BF16), wg_ref[...]) + bg_ref[...])
        pb = p_ref[...].astype(BF16)
        pb_ref[...] = pb
        pe = _nn(pb, wpe_ref[...])
        diff = (h2 + gate * pe) - t_ref[...]
        loss_ref[...] += jnp.sum(diff * diff)
        dh3 = diff * (1.0 / D)
        dpre = (dh3 * pe) * (gate * (1.0 - gate))
        dpre_b = dpre.astype(BF16)
        dpre_ref[...] = dpre_b
        dpe_ref[...] = (dh3 * gate).astype(BF16)
        dbg_ref[...] += jnp.sum(dpre, axis=0, keepdims=True)
        dh2_ref[...] = dh3 + _nt(dpre_b, wg_ref[...])

    row = lambda n: pl.BlockSpec((tm, n), lambda i: (i, 0))
    return pl.pallas_call(
        body,
        name="ple_loss",
        grid=(T // tm,),
        in_specs=[row(D), row(PLE), row(D), _full((D, D)), _full((1, D)), _full((PLE, D))],
        out_specs=[row(D), row(D), row(D), row(PLE), _full((1, D)), _full((8, 128))],
        out_shape=[jax.ShapeDtypeStruct((T, D), F32), jax.ShapeDtypeStruct((T, D), BF16),
                   jax.ShapeDtypeStruct((T, D), BF16), jax.ShapeDtypeStruct((T, PLE), BF16),
                   jax.ShapeDtypeStruct((1, D), F32), jax.ShapeDtypeStruct((8, 128), F32)],
        compiler_params=_params(1),
    )(h2, p, tgt, wg, bg, wpe)


def _ffn_bwd(dh2, ff, h1, f1, w1g, w2g, gpost, gpre):
    T = dh2.shape[0]
    tm = _tile(T, 512)
    nb, hb = w1g.shape[0], w1g.shape[2]

    def body(dh2_ref, ff_ref, h1_ref, f1_ref, w1_ref, w2_ref, gpost_ref, gpre_ref,
             dffb_ref, df1_ref, dh1_ref, dgpost_ref, dgpre_ref, acc_s):
        i, j = pl.program_id(0), pl.program_id(1)

        @pl.when((i == 0) & (j == 0))
        def _():
            dgpost_ref[...] = jnp.zeros_like(dgpost_ref)
            dgpre_ref[...] = jnp.zeros_like(dgpre_ref)

        @pl.when(j == 0)
        def _():
            dff, dg = _rms_bwd(dh2_ref[...], ff_ref[...], gpost_ref[...])
            dffb_ref[...] = dff.astype(BF16)
            dgpost_ref[...] += dg
            acc_s[...] = jnp.zeros_like(acc_s)

        dact = _nt(dffb_ref[...], w2_ref[...])
        df1 = (dact * (2.0 * jnp.maximum(f1_ref[...], 0.0))).astype(BF16)
        df1_ref[...] = df1
        acc_s[...] += _nt(df1, w1_ref[...])

        @pl.when(j == nb - 1)
        def _():
            dx, dg = _rms_bwd(acc_s[...], h1_ref[...], gpre_ref[...])
            dh1_ref[...] = dh2_ref[...] + dx
            dgpre_ref[...] += dg

    row = lambda n: pl.BlockSpec((tm, n), lambda i, j: (i, 0))
    blk = pl.BlockSpec((tm, hb), lambda i, j: (i, j))
    return pl.pallas_call(
        body,
        name="ffn_bwd",
        grid=(T // tm, nb),
        in_specs=[row(D), row(D), row(D), blk, pl.BlockSpec((None, D, hb), lambda i, j: (j, 0, 0)),
                  pl.BlockSpec((None, hb, D), lambda i, j: (j, 0, 0)), _full((1, D)), _full((1, D))],
        out_specs=[row(D), blk, row(D), _full((1, D)), _full((1, D))],
        out_shape=[jax.ShapeDtypeStruct((T, D), BF16), jax.ShapeDtypeStruct((T, DFF), BF16),
                   jax.ShapeDtypeStruct((T, D), F32), jax.ShapeDtypeStruct((1, D), F32),
                   jax.ShapeDtypeStruct((1, D), F32)],
        scratch_shapes=[pltpu.VMEM((tm, D), F32)],
        compiler_params=_params(2),
    )(dh2, ff, h1, f1, w1g, w2g, gpost, gpre)


def _mix_bwd(dh1, o, us, yatt, lng, lnb, sgw, sgwt, sgbt, gatt, gsg, wout, gpm):
    T = dh1.shape[0]
    tm = _tile(T, 256)

    def body(dh1_ref, o_ref, us_ref, ya_ref, lng_ref, lnb_ref, w_ref, wt_ref, bt_ref, ga_ref, gs_ref, wo_ref, gp_ref,
             dob_ref, dya_ref, dus_ref, dw_ref, dbt_ref, dlng_ref, dlnb_ref, dga_ref, dgs_ref, dgp_ref,
             mixed_s, vnb_s, dvn_s):
        @pl.when(pl.program_id(0) == 0)
        def _():
            for r in (dw_ref, dbt_ref, dlng_ref, dlnb_ref, dga_ref, dgs_ref, dgp_ref):
                r[...] = jnp.zeros_like(r)

        is_a = lax.broadcasted_iota(jnp.int32, (1, 128), 1) < HD
        lane = lax.broadcasted_iota(jnp.int32, (1, 128), 1)
        do, dg = _rms_bwd(dh1_ref[...], o_ref[...], gp_ref[...])
        dgp_ref[...] += dg
        dob = do.astype(BF16)
        dob_ref[...] = dob
        dy = _nt(dob, wo_ref[...])
        datt, dg = _rms_bwd(dy[:, 0:AW], ya_ref[...], ga_ref[...])
        dga_ref[...] += dg
        dya_ref[...] = datt

        lng = lng_ref[...]
        u, vs, ug, tu, tv, vhat, rstd, mixed, ysg = _sgu_forward(us_ref, lng, lnb_ref[...], w_ref, bt_ref, mixed_s, vnb_s, tm)
        dysg, dg = _rms_bwd(dy[:, AW:D], ysg, gs_ref[...])
        dgs_ref[...] += dg
        dus_ref[:, 0:SW] = ((dysg * mixed) * _gelu_grad(u, tu)).astype(BF16)
        dmix = dysg * ug

        rr = lax.broadcasted_iota(jnp.int32, (CH, CH), 0)
        cc = lax.broadcasted_iota(jnp.int32, (CH, CH), 1)
        tril = cc <= rr
        triu = cc >= rr
        for jj in range(4):
            wta = jnp.where(triu, wt_ref[2 * jj], 0.0).astype(BF16)
            wtb = jnp.where(triu, wt_ref[2 * jj + 1], 0.0).astype(BF16)
            for ch in range(tm // CH):
                rs, cs = slice(ch * CH, (ch + 1) * CH), slice(jj * 128, (jj + 1) * 128)
                dm2 = dmix[rs, cs]
                dma = jnp.where(is_a, dm2, 0.0)
                dmb = jnp.where(is_a, 0.0, dm2)
                dma_b, dmb_b = dma.astype(BF16), dmb.astype(BF16)
                vn2 = vnb_s[rs, cs]
                dw_ref[2 * jj] += jnp.where(tril, _nt(dma_b, vn2), 0.0)
                dw_ref[2 * jj + 1] += jnp.where(tril, _nt(dmb_b, vn2), 0.0)
                dvn_s[rs, cs] = _nn(wta, dma_b) + _nn(wtb, dmb_b)
                dba = jnp.sum(dma, axis=1, keepdims=True)
                dbb = jnp.sum(dmb, axis=1, keepdims=True)
                dbt_ref[...] += jnp.where(lane == 2 * jj, dba, 0.0) + jnp.where(lane == 2 * jj + 1, dbb, 0.0)

        dvn = dvn_s[...]
        dlng_ref[...] += jnp.sum(dvn * vhat, axis=0, keepdims=True)
        dlnb_ref[...] += jnp.sum(dvn, axis=0, keepdims=True)
        dvh = dvn * lng
        dvg = rstd * (dvh - jnp.mean(dvh, axis=-1, keepdims=True) - vhat * jnp.mean(dvh * vhat, axis=-1, keepdims=True))
        dus_ref[:, SW:NUS] = (dvg * _gelu_grad(vs, tv)).astype(BF16)

    row = lambda n: pl.BlockSpec((tm, n), lambda i: (i, 0))
    return pl.pallas_call(
        body,
        name="mix_bwd",
        grid=(T // tm,),
        in_specs=[row(D), row(D), row(NUS), row(AW), _full((1, SW)), _full((1, SW)), _full((8, CH, CH)), _full((8, CH, CH)),
                  _full((CH, 8)), _full((1, AW)), _full((1, SW)), _full((D, D)), _full((1, D))],
        out_specs=[row(D), row(AW), row(NUS), _full((8, CH, CH)), _full((CH, 128)), _full((1, SW)), _full((1, SW)),
                   _full((1, AW)), _full((1, SW)), _full((1, D))],
        out_shape=[jax.ShapeDtypeStruct((T, D), BF16), jax.ShapeDtypeStruct((T, AW), F32), jax.ShapeDtypeStruct((T, NUS), BF16),
                   jax.ShapeDtypeStruct((8, CH, CH), F32), jax.ShapeDtypeStruct((CH, 128), F32),
                   jax.ShapeDtypeStruct((1, SW), F32), jax.ShapeDtypeStruct((1, SW), F32),
                   jax.ShapeDtypeStruct((1, AW), F32), jax.ShapeDtypeStruct((1, SW), F32), jax.ShapeDtypeStruct((1, D), F32)],
        scratch_shapes=[pltpu.VMEM((tm, SW), F32), pltpu.VMEM((tm, SW), BF16), pltpu.VMEM((tm, SW), F32)],
        compiler_params=_params(1),
    )(dh1, o, us, yatt, lng, lnb, sgw, sgwt, sgbt, gatt, gsg, wout, gpm)


def _attn_delta(dya, yatt):
    T = dya.shape[0]
    tm = _tile(T, 512)

    def body(do_ref, o_ref, d_ref):
        head = lax.broadcasted_iota(jnp.int32, (8, AW), 0)
        feat = lax.broadcasted_iota(jnp.int32, (8, AW), 1)
        sel = jnp.where((feat >= head * HD) & (feat < (head + 1) * HD), 1.0, 0.0)
        d_ref[...] = lax.dot_general(sel, do_ref[...] * o_ref[...], (((1,), (1,)), ((), ())),
                                     precision=lax.Precision.HIGHEST, preferred_element_type=F32)

    row = pl.BlockSpec((tm, AW), lambda i: (i, 0))
    return pl.pallas_call(
        body,
        name="attn_delta",
        grid=(T // tm,),
        in_specs=[row, row],
        out_specs=pl.BlockSpec((8, tm), lambda i: (0, i)),
        out_shape=jax.ShapeDtypeStruct((8, T), F32),
        compiler_params=_params(1),
    )(dya, yatt)


def _attn_bwd(qkv, kt, dya, delta, lse, c, ct):
    T = qkv.shape[0]
    tq = _tile(T, 512)
    tk = tq
    nq = T // tq

    def body(q_ref, k_ref, v_ref, kt_ref, do_ref, d_ref, lse_ref, c_ref, ctq_ref,
             dqt_ref, dcq_ref, dk_ref, dv_ref, dck_ref, qw_s, dow_s, pb_s, dsb_s, dk_s, dv_s, dck_s):
        hp, j, i = pl.program_id(0), pl.program_id(1), pl.program_id(2)
        sub8 = lax.broadcasted_iota(jnp.int32, (8, 1), 0)
        lane = lax.broadcasted_iota(jnp.int32, (1, 128), 1)

        @pl.when((j == 0) & (i == 0))
        def _():
            dqt_ref[...] = jnp.zeros_like(dqt_ref)
            dcq_ref[...] = jnp.zeros_like(dcq_ref)

        @pl.when(i == 0)
        def _():
            dk_s[...] = jnp.zeros_like(dk_s)
            dv_s[...] = jnp.zeros_like(dv_s)
            dck_s[...] = jnp.zeros_like(dck_s)

        def step(masked):
            _stack_heads(q_ref[...], qw_s, tq, 0.125)
            _stack_heads(do_ref[...], dow_s, tq, 1.0)
            s2 = _nt(k_ref[...], qw_s[...])
            dp2 = _nt(v_ref[...], dow_s[...])
            cols = pl.ds(pl.multiple_of(i * tq, tq), tq)
            dcq = jnp.zeros((8, tq), F32)
            dck = jnp.zeros((tk, 128), F32)
            for hh in range(2):
                h = 2 * hp + hh
                half = slice(hh * tq, (hh + 1) * tq)
                lse_h = jnp.sum(jnp.where(sub8 == h, lse_ref[...], 0.0), axis=0, keepdims=True)
                d_h = jnp.sum(jnp.where(sub8 == h, d_ref[...], 0.0), axis=0, keepdims=True)
                pr = jnp.exp(s2[:, half] + _key_bias(c_ref[...], ctq_ref[...], h) - lse_h)
                if masked:
                    key = lax.broadcasted_iota(jnp.int32, (tk, tq), 0)
                    qry = lax.broadcasted_iota(jnp.int32, (tk, tq), 1)
                    pr = jnp.where(key <= qry, pr, 0.0)
                ds = pr * (dp2[:, half] - d_h)
                pb_s[:, half] = pr.astype(BF16)
                dsb_s[:, half] = ds.astype(BF16)
                dcq = dcq + jnp.where(sub8 == hh, jnp.sum(ds, axis=0, keepdims=True), 0.0)
                dck = dck - jnp.where(lane == hh, jnp.sum(ds, axis=1, keepdims=True), 0.0)
            dv_s[...] += _nn(pb_s[...], dow_s[...])
            dk_s[...] += _nn(dsb_s[...], qw_s[...])
            dck_s[...] += dck
            dcq_ref[:, cols] += dcq
            dqt2 = _nn(kt_ref[...] * 0.125, dsb_s[...])
            sub = lax.broadcasted_iota(jnp.int32, (128, 1), 0)
            dqt_ref[:, cols] += jnp.where(sub < HD, dqt2[:, 0:tq], dqt2[:, tq:2 * tq])

        @pl.when(i > j)
        def _():
            step(False)

        @pl.when(i == j)
        def _():
            step(True)

        @pl.when(i == nq - 1)
        def _():
            dk_ref[...] = dk_s[...].astype(BF16)
            dv_ref[...] = dv_s[...].astype(BF16)
            dck_ref[...] = dck_s[...]

    qrow = lambda hp, j, i: (jnp.maximum(i, j), hp)
    qcol = lambda hp, j, i: (0, jnp.maximum(i, j))
    return pl.pallas_call(
        body,
        name="attn_bwd",
        grid=(4, nq, nq),
        in_specs=[
            pl.BlockSpec((tq, 128), qrow),
            pl.BlockSpec((tk, 128), lambda hp, j, i: (j, 4 + hp)),
            pl.BlockSpec((tk, 128), lambda hp, j, i: (j, 8 + hp)),
            pl.BlockSpec((128, tk), lambda hp, j, i: (hp, j)),
            pl.BlockSpec((tq, 128), qrow),
            pl.BlockSpec((8, tq), qcol),
            pl.BlockSpec((8, tq), qcol),
            pl.BlockSpec((tk, FPAD), lambda hp, j, i: (j, 0)),
            pl.BlockSpec((8, tq), qcol),
        ],
        out_specs=[
            pl.BlockSpec((128, T), lambda hp, j, i: (hp, 0)),
            pl.BlockSpec((None, 8, T), lambda hp, j, i: (hp, 0, 0)),
            pl.BlockSpec((tk, 128), lambda hp, j, i: (j, hp)),
            pl.BlockSpec((tk, 128), lambda hp, j, i: (j, hp)),
            pl.BlockSpec((tk, 128), lambda hp, j, i: (j, hp)),
        ],
        out_shape=[jax.ShapeDtypeStruct((AW, T), F32), jax.ShapeDtypeStruct((4, 8, T), F32),
                   jax.ShapeDtypeStruct((T, AW), BF16), jax.ShapeDtypeStruct((T, AW), BF16),
                   jax.ShapeDtypeStruct((T, AW), F32)],
        scratch_shapes=[pltpu.VMEM((2 * tq, 128), BF16), pltpu.VMEM((2 * tq, 128), BF16),
                        pltpu.VMEM((tk, 2 * tq), BF16), pltpu.VMEM((tk, 2 * tq), BF16),
                        pltpu.VMEM((tk, 128), F32), pltpu.VMEM((tk, 128), F32), pltpu.VMEM((tk, 128), F32)],
        compiler_params=_params(3),
    )(qkv, qkv, qkv, kt, dya, delta, lse, c, ct)


def _fgate_bwd(dcq, dck, fz, fb):
    T = dck.shape[0]
    tb = _tile(T, 512)
    nb = T // tb

    def body(dcq_ref, dck_ref, fz_ref, fb_ref, df_ref, dfb_ref, carry):
        @pl.when(pl.program_id(0) == 0)
        def _():
            carry[...] = jnp.zeros_like(carry)
            dfb_ref[...] = jnp.zeros_like(dfb_ref)

        dcv = dcq_ref[...] + dck_ref[...]
        r = lax.broadcasted_iota(jnp.int32, (tb, tb), 0)
        cc = lax.broadcasted_iota(jnp.int32, (tb, tb), 1)
        tri = (cc >= r).astype(F32)
        dlf = jnp.dot(tri, dcv, precision=lax.Precision.HIGHEST, preferred_element_type=F32) + carry[...]
        carry[...] = carry[...] + jnp.sum(dcv, axis=0, keepdims=True)
        lane = lax.broadcasted_iota(jnp.int32, (tb, FPAD), 1)
        df = jnp.where(lane < 8, dlf * jax.nn.sigmoid(-(fz_ref[...] + fb_ref[...])), 0.0)
        df_ref[...] = df.astype(BF16)
        dfb_ref[...] += jnp.sum(df, axis=0, keepdims=True)

    rev = pl.BlockSpec((tb, FPAD), lambda i: (nb - 1 - i, 0))
    return pl.pallas_call(
        body,
        name="fgate_bwd",
        grid=(nb,),
        in_specs=[rev, rev, rev, _full((1, FPAD))],
        out_specs=[rev, _full((1, FPAD))],
        out_shape=[jax.ShapeDtypeStruct((T, FPAD), BF16), jax.ShapeDtypeStruct((1, FPAD), F32)],
        scratch_shapes=[pltpu.VMEM((1, FPAD), F32)],
        compiler_params=_params(1),
    )(dcq, dck, fz, fb)


def _inproj_bwd(dq, dk, dv, dus, dfz, wp, x, dh1, g):
    T = x.shape[0]
    tm = _tile(T, 512)

    def body(dq_ref, dk_ref, dv_ref, dus_ref, dfz_ref, w_ref, x_ref, dh1_ref, g_ref, gx_ref, dg_ref):
        @pl.when(pl.program_id(0) == 0)
        def _():
            dg_ref[...] = jnp.zeros_like(dg_ref)

        da = _nt(dq_ref[...].astype(BF16), w_ref[:, 0:AW])
        da += _nt(dk_ref[...], w_ref[:, AW:2 * AW])
        da += _nt(dv_ref[...], w_ref[:, 2 * AW:NQKV])
        da += _nt(dus_ref[...], w_ref[:, NQKV:NQKV + NUS])
        da += _nt(dfz_ref[...], w_ref[:, NQKV + NUS:ZP])
        dx, dg = _rms_bwd(da, x_ref[...], g_ref[...])
        gx_ref[...] = dh1_ref[...] + dx
        dg_ref[...] += dg

    row = lambda n: pl.BlockSpec((tm, n), lambda i: (i, 0))
    return pl.pallas_call(
        body,
        name="inproj_bwd",
        grid=(T // tm,),
        in_specs=[row(AW), row(AW), row(AW), row(NUS), row(FPAD), _full((D, ZP)), row(D), row(D), _full((1, D))],
        out_specs=[row(D), _full((1, D))],
        out_shape=[jax.ShapeDtypeStruct((T, D), F32), jax.ShapeDtypeStruct((1, D), F32)],
        compiler_params=_params(1),
    )(dq, dk, dv, dus, dfz, wp, x, dh1, g)


def _wgrad(a, b, name, colblk=None):
    T, K = a.shape
    N = b.shape[1]
    bk = min(K, 1024)
    bn = colblk if colblk else min(N, 1024)
    tt = _tile(T, 512)

    def body(a_ref, b_ref, o_ref):
        @pl.when(pl.program_id(2) == 0)
        def _():
            o_ref[...] = jnp.zeros_like(o_ref)

        o_ref[...] += _tn(a_ref[...].astype(BF16), b_ref[...].astype(BF16))

    if colblk:
        out_shape = jax.ShapeDtypeStruct((N // bn, K, bn), F32)
        out_spec = pl.BlockSpec((None, bk, bn), lambda k, n, t: (n, k, 0))
    else:
        out_shape = jax.ShapeDtypeStruct((K, N), F32)
        out_spec = pl.BlockSpec((bk, bn), lambda k, n, t: (k, n))
    return pl.pallas_call(
        body,
        name=name,
        grid=(K // bk, N // bn, T // tt),
        in_specs=[pl.BlockSpec((tt, bk), lambda k, n, t: (t, k)), pl.BlockSpec((tt, bn), lambda k, n, t: (t, n))],
        out_specs=out_spec,
        out_shape=out_shape,
        compiler_params=_params(3),
    )(a, b)


def _adam_math(w, g, m, v):
    m = ADAM_B1 * m + (1.0 - ADAM_B1) * g
    v = ADAM_B2 * v + (1.0 - ADAM_B2) * (g * g)
    m_hat = m / (1.0 - ADAM_B1 ** ADAM_STEP)
    v_hat = v / (1.0 - ADAM_B2 ** ADAM_STEP)
    delta = -ADAM_LR * (m_hat / (jnp.sqrt(v_hat) + ADAM_EPS) + ADAM_WD * w)
    return delta, m, v


def _adam(parts, w, m, v, name):
    R, C = w.shape
    br = 128 if R % 128 == 0 else R

    def body(p_ref, w_ref, m_ref, v_ref, g_ref, d_ref, nm_ref, nv_ref):
        g = p_ref[0]
        for s in range(1, NDEV):
            g = g + p_ref[s]
        g_ref[...] = g
        d_ref[...], nm_ref[...], nv_ref[...] = _adam_math(w_ref[...], g, m_ref[...], v_ref[...])

    blk = pl.BlockSpec((br, C), lambda i: (i, 0))
    return pl.pallas_call(
        body,
        name=name,
        grid=(R // br,),
        in_specs=[pl.BlockSpec((NDEV, br, C), lambda i: (0, i, 0)), blk, blk, blk],
        out_specs=[blk] * 4,
        out_shape=[jax.ShapeDtypeStruct((R, C), F32)] * 4,
        compiler_params=_params(1),
    )(parts, w, m, v)


_SMALL = (("sg_w", 8 * CH * CH), ("f_bias", 8), ("sg_ln_g", SW), ("sg_ln_b", SW), ("sg_b", 8 * CH), ("att_out_g", AW),
          ("sg_out_g", SW), ("pre_mix_g", D), ("post_mix_g", D), ("pre_ffn_g", D), ("post_ffn_g", D), ("ple_gate_b", D))
_SEG = 8 * 128


def _seg_rows(size):
    return 8 * (-(-size // _SEG))


def _pack(vals):
    parts = []
    for name, size in _SMALL:
        flat = vals[name].reshape(-1)
        rows = _seg_rows(size)
        parts.append(jnp.pad(flat, (0, rows * 128 - size)).reshape(rows, 128))
    return jnp.concatenate(parts, axis=0)


def _unpack(packed, shapes):
    out, r = {}, 0
    for name, size in _SMALL:
        rows = _seg_rows(size)
        out[name] = packed[r:r + rows].reshape(-1)[:size].reshape(shapes[name])
        r += rows
    return out


def kernel(x, p, w_in, f_bias, sg_ln_g, sg_ln_b, sg_w, sg_b, att_out_g, sg_out_g, w_out, pre_mix_g, post_mix_g, pre_ffn_g, post_ffn_g, w_ff1, w_ff2, ple_w, ple_gate_w, ple_gate_b, loss_target, m_w_in, m_f_bias, m_sg_ln_g, m_sg_ln_b, m_sg_w, m_sg_b, m_att_out_g, m_sg_out_g, m_w_out, m_pre_mix_g, m_post_mix_g, m_pre_ffn_g, m_post_ffn_g, m_w_ff1, m_w_ff2, m_ple_w, m_ple_gate_w, m_ple_gate_b, v_w_in, v_f_bias, v_sg_ln_g, v_sg_ln_b, v_sg_w, v_sg_b, v_att_out_g, v_sg_out_g, v_w_out, v_pre_mix_g, v_post_mix_g, v_pre_ffn_g, v_post_ffn_g, v_w_ff1, v_w_ff2, v_ple_w, v_ple_gate_w, v_ple_gate_b):
    small_w = dict(sg_w=sg_w, f_bias=f_bias, sg_ln_g=sg_ln_g, sg_ln_b=sg_ln_b, sg_b=sg_b, att_out_g=att_out_g,
                   sg_out_g=sg_out_g, pre_mix_g=pre_mix_g, post_mix_g=post_mix_g, pre_ffn_g=pre_ffn_g,
                   post_ffn_g=post_ffn_g, ple_gate_b=ple_gate_b)
    small_m = dict(sg_w=m_sg_w, f_bias=m_f_bias, sg_ln_g=m_sg_ln_g, sg_ln_b=m_sg_ln_b, sg_b=m_sg_b, att_out_g=m_att_out_g,
                   sg_out_g=m_sg_out_g, pre_mix_g=m_pre_mix_g, post_mix_g=m_post_mix_g, pre_ffn_g=m_pre_ffn_g,
                   post_ffn_g=m_post_ffn_g, ple_gate_b=m_ple_gate_b)
    small_v = dict(sg_w=v_sg_w, f_bias=v_f_bias, sg_ln_g=v_sg_ln_g, sg_ln_b=v_sg_ln_b, sg_b=v_sg_b, att_out_g=v_att_out_g,
                   sg_out_g=v_sg_out_g, pre_mix_g=v_pre_mix_g, post_mix_g=v_post_mix_g, pre_ffn_g=v_pre_ffn_g,
                   post_ffn_g=v_post_ffn_g, ple_gate_b=v_ple_gate_b)
    big = dict(w_in=(w_in, m_w_in, v_w_in), w_out=(w_out, m_w_out, v_w_out), w_ff1=(w_ff1, m_w_ff1, v_w_ff1),
               w_ff2=(w_ff2, m_w_ff2, v_w_ff2), ple_w=(ple_w, m_ple_w, v_ple_w),
               ple_gate_w=(ple_gate_w, m_ple_gate_w, v_ple_gate_w))

    xt, pt, tgt = x[0], p[0, 0], loss_target[0]
    ws = W_IN_COLS // NDEV

    (gw_in,) = _xchg([w_in[0].astype(BF16)], True, "gather_w_in")
    rest = _xchg_start([w_out[0].astype(BF16), w_ff1[0].astype(BF16), w_ff2[0].astype(BF16), ple_w[0].astype(BF16),
                        ple_gate_w[0].astype(BF16)], True, "gather_rest_start")
    win = jnp.transpose(gw_in, (1, 0, 2)).reshape(D, W_IN_COLS)
    wp = jnp.concatenate([win[:, 0:NQKV], win[:, NQKV + 8:W_IN_COLS], win[:, NQKV:NQKV + 8],
                          jnp.zeros((D, FPAD - 8), BF16)], axis=1)

    fb = jnp.pad(f_bias.astype(F32), ((0, 0), (0, FPAD - 8)))
    sgw = sg_w[0]
    sgwt = jnp.transpose(sg_w[0], (0, 2, 1))
    sgbt = jnp.transpose(sg_b[0])

    qkv, kt, vt, us, fz, ab = _inproj(xt, pre_mix_g + rest[4][0, 0], wp)
    c, ct = _fcum(fz, fb)
    yatt, lse = _attn_fwd(qkv, vt, c, ct)
    lse = lse[:, 0:2, :].reshape(8, -1)
    gw_out, gw1, gw2, gwpe, gwg = _xchg_wait(rest, True, yatt, "gather_rest_wait")
    wout = gw_out.reshape(D, D)
    wg = gwg.reshape(D, D)
    wpe = jnp.transpose(gwpe, (1, 0, 2)).reshape(PLE, D)
    h1, yb, o = _sgu_out(us, yatt, xt, sg_ln_g, sg_ln_b, sgw, sgbt, att_out_g, sg_out_g, wout, post_mix_g)
    c2b, f1, act, ff, h2, h2b = _ffn_fwd(h1, pre_ffn_g, gw1, gw2, post_ffn_g)
    dh2, dpre, dpe, pb, dbg, loss_acc = _ple_loss(h2, pt, tgt, wg, ple_gate_b, wpe)
    loss = lax.psum(loss_acc[0, 0] * (0.5 / D), ("x", "y", "c"))

    g_pe = _wgrad(pb, dpe, "wgrad_ple", colblk=D // NDEV)
    g_g = _wgrad(h2b, dpre, "wgrad_gate").reshape(NDEV, D // NDEV, D)
    dffb, df1, dh1, dgpostffn, dgpreffn = _ffn_bwd(dh2, ff, h1, f1, gw1, gw2, post_ffn_g, pre_ffn_g)
    g_1 = _wgrad(c2b, df1, "wgrad_ff1", colblk=DFF // NDEV)
    g_2 = _wgrad(act, dffb, "wgrad_ff2").reshape(NDEV, DFF // NDEV, D)
    early = _xchg_start([g_1, g_2, g_pe, g_g], False, "scatter_early_start")
    dob, dya, dus, dsgw, dsgbt, dlng, dlnb, dgatt, dgsg, dgpostmix = _mix_bwd(
        dh1, o, us, yatt, sg_ln_g, sg_ln_b, sgw, sgwt, sgbt, att_out_g, sg_out_g, wout, post_mix_g + early[4][0, 0])
    dqt, dcq, dk, dv, dck = _attn_bwd(qkv, kt, dya, _attn_delta(dya, yatt), lse, c, ct)
    dq = jnp.transpose(dqt)
    dcq = jnp.pad(jnp.transpose(dcq[:, 0:2, :].reshape(8, -1)), ((0, 0), (0, FPAD - 8)))
    dck = jnp.pad(dck.reshape(-1, 4, 128)[:, :, 0:2].reshape(-1, 8), ((0, 0), (0, FPAD - 8)))
    dfz, dfb = _fgate_bwd(dcq, dck, fz, fb)
    grad_x, dgpremix = _inproj_bwd(dq, dk, dv, dus, dfz, wp, xt, dh1, pre_mix_g)

    gq = _wgrad(ab, dq, "wgrad_q")
    gk = _wgrad(ab, dk, "wgrad_k")
    gv = _wgrad(ab, dv, "wgrad_v")
    gus = _wgrad(ab, dus, "wgrad_us")
    gf = _wgrad(ab, dfz, "wgrad_f")
    g_in = jnp.concatenate([gq, gk, gv, gf[:, 0:8], gus], axis=1)
    g_in = jnp.transpose(g_in.reshape(D, NDEV, ws), (1, 0, 2))
    g_out = _wgrad(yb, dob, "wgrad_out").reshape(NDEV, D // NDEV, D)

    small_g = dict(sg_w=dsgw, f_bias=dfb[:, 0:8], sg_ln_g=dlng, sg_ln_b=dlnb, sg_b=jnp.transpose(dsgbt[:, 0:8]),
                   att_out_g=dgatt, sg_out_g=dgsg, pre_mix_g=dgpremix, post_mix_g=dgpostmix, pre_ffn_g=dgpreffn,
                   post_ffn_g=dgpostffn, ple_gate_b=dbg)

    r_1, r_2, r_pe, r_g = _xchg_wait(early, False, grad_x, "scatter_early_wait")
    r_in, r_out = _xchg([g_in, g_out], False, "scatter_late")
    (r_small,) = _xchg([_pack(small_g)], True, "gather_small_grads")

    res = {}
    for name, parts in (("w_in", r_in), ("w_out", r_out), ("w_ff1", r_1), ("w_ff2", r_2), ("ple_w", r_pe),
                        ("ple_gate_w", r_g)):
        w, m, v = big[name]
        res[name] = [t[None] for t in _adam(parts, w[0], m[0], v[0], "adam_" + name)]
    shapes = {k: a.shape for k, a in small_w.items()}
    small = [_unpack(t, shapes) for t in _adam(r_small, _pack(small_w), _pack(small_m), _pack(small_v), "adam_small")]
    for name, _ in _SMALL:
        res[name] = [s[name] for s in small]

    order = ["w_in", "f_bias", "sg_ln_g", "sg_ln_b", "sg_w", "sg_b", "att_out_g", "sg_out_g", "w_out", "pre_mix_g",
             "post_mix_g", "pre_ffn_g", "post_ffn_g", "w_ff1", "w_ff2", "ple_w", "ple_gate_w", "ple_gate_b"]
    outs = [loss, grad_x[None]]
    for kind in range(4):
        outs += [res[name][kind] for name in order]
    return tuple(outs)
```

```python
import jax
import jax.numpy as jnp
from jax import lax
from jax.experimental import pallas as pl
from jax.experimental.pallas import tpu as pltpu

F32 = jnp.float32
BF16 = jnp.bfloat16

NDEV = 8
D = 1024
AW = 512
SW = 512
HD = 64
CH = 128
DFF = 4096
PLE = 256
NQKV = 3 * AW
NUS = 2 * SW
FPAD = 128
ZP = NQKV + NUS + FPAD
W_IN_COLS = 2568
EPS = 1e-6
MASKV = -1e30
GELU_K = 0.7978845608028654
GELU_C = 0.044715

ADAM_LR = 0.001
ADAM_B1 = 0.9
ADAM_B2 = 0.999
ADAM_EPS = 1e-08
ADAM_WD = 0.01
ADAM_STEP = 10

VMEM_LIMIT = 48 * 1024 * 1024


def _nn(a, b):
    return jnp.dot(a, b, preferred_element_type=F32)


def _nt(a, b):
    return lax.dot_general(a, b, (((1,), (1,)), ((), ())), preferred_element_type=F32)


def _tn(a, b):
    return lax.dot_general(a, b, (((0,), (0,)), ((), ())), preferred_element_type=F32)


def _tile(n, pref):
    return min(n, pref)


def _params(n_axes):
    return pltpu.CompilerParams(dimension_semantics=("arbitrary",) * n_axes, vmem_limit_bytes=VMEM_LIMIT)


def _full(shape):
    nd = len(shape)
    return pl.BlockSpec(shape, lambda *_: (0,) * nd)


def _rms_fwd(x, g):
    r = lax.rsqrt(jnp.mean(x * x, axis=-1, keepdims=True) + EPS)
    return x * r * g


def _rms_bwd(dy, x, g):
    n = x.shape[-1]
    r = lax.rsqrt(jnp.mean(x * x, axis=-1, keepdims=True) + EPS)
    u = dy * g
    s = jnp.sum(x * u, axis=-1, keepdims=True)
    dx = r * u - x * (r * r * r * (s * (1.0 / n)))
    dg = jnp.sum(dy * (x * r), axis=0, keepdims=True)
    return dx, dg


def _gelu(x):
    t = jnp.tanh(GELU_K * (x + GELU_C * (x * x * x)))
    return x * (0.5 * (1.0 + t)), t


def _gelu_grad(x, t):
    return 0.5 * (1.0 + t) + 0.5 * x * (1.0 - t * t) * (GELU_K * (1.0 + 3.0 * GELU_C * x * x))


def _xchg(arrs, gather, name):
    n = len(arrs)
    out_shapes = [jax.ShapeDtypeStruct(((NDEV,) + a.shape) if gather else a.shape, a.dtype) for a in arrs]

    def body(*refs):
        ins, outs = refs[:n], refs[n:2 * n]
        send, recv, loc = refs[2 * n:]
        x, y, c = lax.axis_index("x"), lax.axis_index("y"), lax.axis_index("c")
        me = 4 * x + 2 * y + c
        local = []
        for a in range(n):
            cp = pltpu.make_async_copy(ins[a] if gather else ins[a].at[me], outs[a].at[me], loc.at[a])
            cp.start()
            local.append(cp)
        remote = []
        for k in range(1, NDEV):
            px = 1 - x if (k >> 2) & 1 else x
            py = 1 - y if (k >> 1) & 1 else y
            pc = 1 - c if k & 1 else c
            peer = 4 * px + 2 * py + pc
            for a in range(n):
                cp = pltpu.make_async_remote_copy(
                    src_ref=ins[a] if gather else ins[a].at[peer],
                    dst_ref=outs[a].at[me],
                    send_sem=send.at[a, k - 1],
                    recv_sem=recv.at[a, k - 1],
                    device_id=(px, py, pc),
                    device_id_type=pl.DeviceIdType.MESH,
                )
                cp.start()
                remote.append(cp)
        for cp in remote:
            cp.wait_send()
            cp.wait_recv()
        for cp in local:
            cp.wait()

    hbm = pl.BlockSpec(memory_space=pltpu.HBM)
    return pl.pallas_call(
        body,
        name=name,
        out_shape=out_shapes,
        in_specs=[hbm] * n,
        out_specs=[hbm] * n,
        scratch_shapes=[
            pltpu.SemaphoreType.DMA((n, NDEV - 1)),
            pltpu.SemaphoreType.DMA((n, NDEV - 1)),
            pltpu.SemaphoreType.DMA((n,)),
        ],
    )(*arrs)


def _peers(x, y, c):
    out = []
    for k in range(1, NDEV):
        out.append((1 - x if (k >> 2) & 1 else x, 1 - y if (k >> 1) & 1 else y, 1 - c if k & 1 else c))
    return out


def _xchg_start(arrs, gather, name):
    n = len(arrs)
    me = 4 * lax.axis_index("x") + 2 * lax.axis_index("y") + lax.axis_index("c")
    lands = []
    for a in arrs:
        shape = ((NDEV,) + a.shape) if gather else a.shape
        own = a[None] if gather else lax.dynamic_slice_in_dim(a, me, 1, axis=0)
        lands.append(lax.dynamic_update_slice_in_dim(lax.empty(shape, a.dtype), own, me, axis=0))

    def body(*refs):
        ins, lnd = refs[:n], refs[n:2 * n]
        send, recv, token = refs[2 * n:3 * n], refs[3 * n:4 * n], refs[-1]
        x, y, c = lax.axis_index("x"), lax.axis_index("y"), lax.axis_index("c")
        mine = 4 * x + 2 * y + c
        for px, py, pc in _peers(x, y, c):
            peer = 4 * px + 2 * py + pc
            for a in range(n):
                pltpu.make_async_remote_copy(
                    src_ref=ins[a] if gather else ins[a].at[peer],
                    dst_ref=lnd[a].at[mine],
                    send_sem=send[a],
                    recv_sem=recv[a],
                    device_id=(px, py, pc),
                    device_id_type=pl.DeviceIdType.MESH,
                ).start()
        token[...] = jnp.zeros_like(token)

    hbm = pl.BlockSpec(memory_space=pltpu.HBM)
    sem = pl.BlockSpec(memory_space=pltpu.SEMAPHORE)
    res = pl.pallas_call(
        body,
        name=name,
        out_shape=(*[pltpu.SemaphoreType.DMA(())] * (2 * n),
                   *[pltpu.HBM(a.shape, a.dtype) for a in arrs], *[pltpu.HBM(l.shape, l.dtype) for l in lands],
                   jax.ShapeDtypeStruct((8, 128), F32)),
        in_specs=[hbm] * (2 * n),
        out_specs=(*([sem] * (2 * n)), *([hbm] * (2 * n)), pl.BlockSpec(memory_space=pltpu.VMEM)),
        input_output_aliases={i: 2 * n + i for i in range(2 * n)},
        compiler_params=pltpu.CompilerParams(has_side_effects=pltpu.SideEffectType.DATAFLOW_SIDE_EFFECTING),
    )(*[pltpu.with_memory_space_constraint(a, pltpu.HBM) for a in arrs],
      *[pltpu.with_memory_space_constraint(l, pltpu.HBM) for l in lands])
    return list(res[0:n]), list(res[n:2 * n]), list(res[2 * n:3 * n]), list(res[3 * n:4 * n]), res[-1]


def _xchg_wait(started, gather, after, name):
    send, recv, srcs, lands, _ = started
    n = len(srcs)

    def body(*refs):
        lnd = refs[n:2 * n]
        send, recv = refs[2 * n:3 * n], refs[3 * n:4 * n]
        me = (lax.axis_index("x"), lax.axis_index("y"), lax.axis_index("c"))
        for a in range(n):
            seven = lnd[a].at[pl.ds(0, NDEV - 1)]
            cp = pltpu.make_async_remote_copy(src_ref=seven, dst_ref=seven, send_sem=send[a], recv_sem=recv[a],
                                              device_id=me, device_id_type=pl.DeviceIdType.MESH)
            cp.wait_send()
            cp.wait_recv()

    hbm = pl.BlockSpec(memory_space=pltpu.HBM)
    sem = pl.BlockSpec(memory_space=pltpu.SEMAPHORE)
    res = pl.pallas_call(
        body,
        name=name,
        out_shape=tuple([pltpu.HBM(a.shape, a.dtype) for a in srcs] + [pltpu.HBM(l.shape, l.dtype) for l in lands]),
        in_specs=[hbm] * (2 * n) + [sem] * (2 * n) + [pl.BlockSpec(memory_space=pl.ANY)],
        out_specs=tuple([hbm] * (2 * n)),
        input_output_aliases={i: i for i in range(2 * n)},
        compiler_params=pltpu.CompilerParams(has_side_effects=pltpu.SideEffectType.DATAFLOW_SIDE_EFFECTING),
    )(*srcs, *lands, *send, *recv, after)
    return list(res[n:])


def _inproj(x, g, wp):
    T = x.shape[0]
    tm = _tile(T, 512)

    def body(x_ref, g_ref, w_ref, qkv_ref, kt_ref, vt_ref, us_ref, fz_ref, ab_ref):
        a = _rms_fwd(x_ref[...], g_ref[...]).astype(BF16)
        ab_ref[...] = a
        qkv_ref[:, 0:AW] = _nn(a, w_ref[:, 0:AW]).astype(BF16)
        kk = _nn(a, w_ref[:, AW:2 * AW])
        qkv_ref[:, AW:2 * AW] = kk.astype(BF16)
        kt_ref[...] = kk.T.astype(BF16)
        vv = _nn(a, w_ref[:, 2 * AW:NQKV])
        qkv_ref[:, 2 * AW:NQKV] = vv.astype(BF16)
        vt_ref[...] = vv.T.astype(BF16)
        us_ref[...] = _nn(a, w_ref[:, NQKV:NQKV + NUS])
        fz_ref[...] = _nn(a, w_ref[:, NQKV + NUS:ZP])

    row = lambda n: pl.BlockSpec((tm, n), lambda i: (i, 0))
    col = pl.BlockSpec((AW, tm), lambda i: (0, i))
    return pl.pallas_call(
        body,
        name="inproj",
        grid=(T // tm,),
        in_specs=[row(D), _full((1, D)), _full((D, ZP))],
        out_specs=[row(NQKV), col, col, row(NUS), row(FPAD), row(D)],
        out_shape=[
            jax.ShapeDtypeStruct((T, NQKV), BF16),
            jax.ShapeDtypeStruct((AW, T), BF16),
            jax.ShapeDtypeStruct((AW, T), BF16),
            jax.ShapeDtypeStruct((T, NUS), F32),
            jax.ShapeDtypeStruct((T, FPAD), F32),
            jax.ShapeDtypeStruct((T, D), BF16),
        ],
        compiler_params=_params(1),
    )(x, g, wp)


def _log_sigmoid(z):
    return jnp.minimum(z, 0.0) - jnp.log1p(jnp.exp(-jnp.abs(z)))


def _fcum(fz, fb):
    T = fz.shape[0]
    tb = _tile(T, 512)

    def body(fz_ref, fb_ref, c_ref, ct_ref, carry):
        @pl.when(pl.program_id(0) == 0)
        def _():
            carry[...] = jnp.zeros_like(carry)

        lf = _log_sigmoid(fz_ref[...] + fb_ref[...])
        r = lax.broadcasted_iota(jnp.int32, (tb, tb), 0)
        cc = lax.broadcasted_iota(jnp.int32, (tb, tb), 1)
        tri = (cc <= r).astype(F32)
        cs = jnp.dot(tri, lf, precision=lax.Precision.HIGHEST, preferred_element_type=F32) + carry[...]
        c_ref[...] = cs
        ct_ref[...] = cs.T[0:8, :]
        carry[...] = carry[...] + jnp.sum(lf, axis=0, keepdims=True)

    return pl.pallas_call(
        body,
        name="fcum",
        grid=(T // tb,),
        in_specs=[pl.BlockSpec((tb, FPAD), lambda i: (i, 0)), _full((1, FPAD))],
        out_specs=[pl.BlockSpec((tb, FPAD), lambda i: (i, 0)), pl.BlockSpec((8, tb), lambda i: (0, i))],
        out_shape=[jax.ShapeDtypeStruct((T, FPAD), F32), jax.ShapeDtypeStruct((8, T), F32)],
        scratch_shapes=[pltpu.VMEM((1, FPAD), F32)],
        compiler_params=_params(1),
    )(fz, fb)


def _key_bias(c_keys, ct_q, h):
    lane = lax.broadcasted_iota(jnp.int32, c_keys.shape, 1)
    sub = lax.broadcasted_iota(jnp.int32, (8, 1), 0)
    ck = jnp.sum(jnp.where(lane == h, c_keys, 0.0), axis=1, keepdims=True)
    cref = jnp.sum(jnp.where(sub == h, ct_q[:, 0:1], 0.0), axis=0, keepdims=True)
    return cref - ck


def _stack_heads(x2, dst, tq, scale):
    is_a = lax.broadcasted_iota(jnp.int32, (1, 128), 1) < HD
    zero = jnp.zeros_like(x2)
    dst[0:tq, :] = (jnp.where(is_a, x2, zero) * scale).astype(dst.dtype)
    dst[tq:2 * tq, :] = (jnp.where(is_a, zero, x2) * scale).astype(dst.dtype)


def _tri_q_major(s, nq):
    i = sum(jnp.where(s >= k * (k + 1) // 2, 1, 0) for k in range(1, nq))
    return i, s - i * (i + 1) // 2


def _tri_k_major(s, nq):
    j = sum(jnp.where(s >= k * nq - k * (k - 1) // 2, 1, 0) for k in range(1, nq))
    return j, j + s - (j * nq - j * (j - 1) // 2)


def _attn_fwd(qkv, vt, c, ct):
    T = qkv.shape[0]
    tq = _tile(T, 512)
    tk = tq
    nq = T // tq

    def body(q_ref, k_ref, vt_ref, c_ref, ctq_ref, o_ref, lse_ref, qw_s, m_s, l_s, acc_s):
        hp = pl.program_id(0)
        i, j = _tri_q_major(pl.program_id(1), nq)

        @pl.when(j == 0)
        def _():
            _stack_heads(q_ref[...], qw_s, tq, 0.125)
            m_s[...] = jnp.full_like(m_s, MASKV)
            l_s[...] = jnp.zeros_like(l_s)
            acc_s[...] = jnp.zeros_like(acc_s)

        def step(masked):
            s2 = _nt(k_ref[...], qw_s[...])
            vt2 = vt_ref[...]
            for hh in range(2):
                s = s2[:, hh * tq:(hh + 1) * tq] + _key_bias(c_ref[...], ctq_ref[...], 2 * hp + hh)
                if masked:
                    key = lax.broadcasted_iota(jnp.int32, (tk, tq), 0)
                    qry = lax.broadcasted_iota(jnp.int32, (tk, tq), 1)
                    s = jnp.where(key <= qry, s, MASKV)
                m_prev = m_s[hh]
                m_new = jnp.maximum(m_prev, jnp.max(s, axis=0, keepdims=True))
                pr = jnp.exp(s - m_new)
                alpha = jnp.exp(m_prev - m_new)
                l_s[hh] = alpha * l_s[hh] + jnp.sum(pr, axis=0, keepdims=True)
                m_s[hh] = m_new
                acc_s[hh] = alpha * acc_s[hh] + _nn(vt2, pr.astype(BF16))

        @pl.when(j < i)
        def _():
            step(False)

        @pl.when(j == i)
        def _():
            step(True)
            sub = lax.broadcasted_iota(jnp.int32, (128, 1), 0)
            ot = jnp.where(sub < HD, acc_s[0] * (1.0 / l_s[0]), acc_s[1] * (1.0 / l_s[1]))
            o_ref[...] = ot.T
            sub8 = lax.broadcasted_iota(jnp.int32, (8, 1), 0)
            lse_ref[...] = jnp.where(sub8 == 0, m_s[0] + jnp.log(l_s[0]),
                                     jnp.where(sub8 == 1, m_s[1] + jnp.log(l_s[1]), 0.0))

    qi = lambda s: _tri_q_major(s, nq)[0]
    kj = lambda s: _tri_q_major(s, nq)[1]
    return pl.pallas_call(
        body,
        name="attn_fwd",
        grid=(4, nq * (nq + 1) // 2),
        in_specs=[
            pl.BlockSpec((tq, 128), lambda hp, s: (qi(s), hp)),
            pl.BlockSpec((tk, 128), lambda hp, s: (kj(s), 4 + hp)),
            pl.BlockSpec((128, tk), lambda hp, s: (hp, kj(s))),
            pl.BlockSpec((tk, FPAD), lambda hp, s: (kj(s), 0)),
            pl.BlockSpec((8, tq), lambda hp, s: (0, qi(s))),
        ],
        out_specs=[pl.BlockSpec((tq, 128), lambda hp, s: (qi(s), hp)),
                   pl.BlockSpec((None, 8, tq), lambda hp, s: (hp, 0, qi(s)))],
        out_shape=[jax.ShapeDtypeStruct((T, AW), F32), jax.ShapeDtypeStruct((4, 8, T), F32)],
        scratch_shapes=[pltpu.VMEM((2 * tq, 128), BF16), pltpu.VMEM((2, 1, tq), F32), pltpu.VMEM((2, 1, tq), F32),
                        pltpu.VMEM((2, 128, tq), F32)],
        compiler_params=_params(2),
    )(qkv, qkv, vt, c, ct)


def _sgu_forward(us_ref, lng, lnb, w_ref, bt_ref, mixed_s, vnb_s, tm):
    is_a = lax.broadcasted_iota(jnp.int32, (1, 128), 1) < HD
    u = us_ref[:, 0:SW]
    vs = us_ref[:, SW:NUS]
    ug, tu = _gelu(u)
    vg, tv = _gelu(vs)
    mu = jnp.mean(vg, axis=-1, keepdims=True)
    xc = vg - mu
    rstd = lax.rsqrt(jnp.mean(xc * xc, axis=-1, keepdims=True) + EPS)
    vhat = xc * rstd
    vnb_s[...] = (vhat * lng + lnb).astype(BF16)
    rr = lax.broadcasted_iota(jnp.int32, (CH, CH), 0)
    cc = lax.broadcasted_iota(jnp.int32, (CH, CH), 1)
    tril = cc <= rr
    for jj in range(4):
        wa = jnp.where(tril, w_ref[2 * jj], 0.0).astype(BF16)
        wb = jnp.where(tril, w_ref[2 * jj + 1], 0.0).astype(BF16)
        ba = bt_ref[:, 2 * jj:2 * jj + 1]
        bb = bt_ref[:, 2 * jj + 1:2 * jj + 2]
        for ch in range(tm // CH):
            rs, cs = slice(ch * CH, (ch + 1) * CH), slice(jj * 128, (jj + 1) * 128)
            vn2 = vnb_s[rs, cs]
            mixed_s[rs, cs] = jnp.where(is_a, _nn(wa, vn2) + ba, _nn(wb, vn2) + bb)
    mixed = mixed_s[...]
    return u, vs, ug, tu, tv, vhat, rstd, mixed, ug * mixed


def _sgu_out(us, yatt, x, lng, lnb, sgw, sgbt, gatt, gsg, wout, gpm):
    T = us.shape[0]
    tm = _tile(T, 256)

    def body(us_ref, ya_ref, x_ref, lng_ref, lnb_ref, w_ref, bt_ref, ga_ref, gs_ref, wo_ref, gp_ref,
             h1_ref, yb_ref, o_ref, mixed_s, vnb_s):
        ysg = _sgu_forward(us_ref, lng_ref[...], lnb_ref[...], w_ref, bt_ref, mixed_s, vnb_s, tm)[-1]
        yb_ref[:, 0:AW] = _rms_fwd(ya_ref[...], ga_ref[...]).astype(BF16)
        yb_ref[:, AW:D] = _rms_fwd(ysg, gs_ref[...]).astype(BF16)
        o = _nn(yb_ref[...], wo_ref[...])
        o_ref[...] = o
        h1_ref[...] = x_ref[...] + _rms_fwd(o, gp_ref[...])

    row = lambda n: pl.BlockSpec((tm, n), lambda i: (i, 0))
    return pl.pallas_call(
        body,
        name="sgu_out",
        grid=(T // tm,),
        in_specs=[row(NUS), row(AW), row(D), _full((1, SW)), _full((1, SW)), _full((8, CH, CH)), _full((CH, 8)),
                  _full((1, AW)), _full((1, SW)), _full((D, D)), _full((1, D))],
        out_specs=[row(D), row(D), row(D)],
        out_shape=[jax.ShapeDtypeStruct((T, D), F32), jax.ShapeDtypeStruct((T, D), BF16), jax.ShapeDtypeStruct((T, D), F32)],
        scratch_shapes=[pltpu.VMEM((tm, SW), F32), pltpu.VMEM((tm, SW), BF16)],
        compiler_params=_params(1),
    )(us, yatt, x, lng, lnb, sgw, sgbt, gatt, gsg, wout, gpm)


def _ffn_fwd(h1, gpre, w1g, w2g, gpost):
    T = h1.shape[0]
    tm = _tile(T, 512)
    nb, hb = w1g.shape[0], w1g.shape[2]

    def body(h1_ref, gpre_ref, w1_ref, w2_ref, gpost_ref, c2_ref, f1_ref, act_ref, ff_ref, h2_ref, h2b_ref, acc_s):
        j = pl.program_id(1)

        @pl.when(j == 0)
        def _():
            c2_ref[...] = _rms_fwd(h1_ref[...], gpre_ref[...]).astype(BF16)
            acc_s[...] = jnp.zeros_like(acc_s)

        f1 = _nn(c2_ref[...], w1_ref[...])
        f1_ref[...] = f1
        r = jnp.maximum(f1, 0.0)
        act = (r * r).astype(BF16)
        act_ref[...] = act
        acc_s[...] += _nn(act, w2_ref[...])

        @pl.when(j == nb - 1)
        def _():
            ff = acc_s[...]
            ff_ref[...] = ff
            h2 = h1_ref[...] + _rms_fwd(ff, gpost_ref[...])
            h2_ref[...] = h2
            h2b_ref[...] = h2.astype(BF16)

    row = lambda n: pl.BlockSpec((tm, n), lambda i, j: (i, 0))
    return pl.pallas_call(
        body,
        name="ffn_fwd",
        grid=(T // tm, nb),
        in_specs=[row(D), _full((1, D)), pl.BlockSpec((None, D, hb), lambda i, j: (j, 0, 0)),
                  pl.BlockSpec((None, hb, D), lambda i, j: (j, 0, 0)), _full((1, D))],
        out_specs=[row(D), pl.BlockSpec((tm, hb), lambda i, j: (i, j)), pl.BlockSpec((tm, hb), lambda i, j: (i, j)),
                   row(D), row(D), row(D)],
        out_shape=[jax.ShapeDtypeStruct((T, D), BF16), jax.ShapeDtypeStruct((T, DFF), F32),
                   jax.ShapeDtypeStruct((T, DFF), BF16), jax.ShapeDtypeStruct((T, D), F32),
                   jax.ShapeDtypeStruct((T, D), F32), jax.ShapeDtypeStruct((T, D), BF16)],
        scratch_shapes=[pltpu.VMEM((tm, D), F32)],
        compiler_params=_params(2),
    )(h1, gpre, w1g, w2g, gpost)


def _ple_loss(h2, p, tgt, wg, bg, wpe):
    T = h2.shape[0]
    tm = _tile(T, 512)

    def body(h2_ref, p_ref, t_ref, wg_ref, bg_ref, wpe_ref, dh2_ref, dpre_ref, dpe_ref, pb_ref, dbg_ref, loss_ref):
        @pl.when(pl.program_id(0) == 0)
        def _():
            dbg_ref[...] = jnp.zeros_like(dbg_ref)
            loss_ref[...] = jnp.zeros_like(loss_ref)

        h2 = h2_ref[...]
        gate = jax.nn.sigmoid(_nn(h2.astype(BF16), wg_ref[...]) + bg_ref[...])
        pb = p_ref[...].astype(BF16)
        pb_ref[...] = pb
        pe = _nn(pb, wpe_ref[...])
        diff = (h2 + gate * pe) - t_ref[...]
        loss_ref[...] += jnp.sum(diff * diff)
        dh3 = diff * (1.0 / D)
        dpre = (dh3 * pe) * (gate * (1.0 - gate))
        dpre_b = dpre.astype(BF16)
        dpre_ref[...] = dpre_b
        dpe_ref[...] = (dh3 * gate).astype(BF16)
        dbg_ref[...] += jnp.sum(dpre, axis=0, keepdims=True)
        dh2_ref[...] = dh3 + _nt(dpre_b, wg_ref[...])

    row = lambda n: pl.BlockSpec((tm, n), lambda i: (i, 0))
    return pl.pallas_call(
        body,
        name="ple_loss",
        grid=(T // tm,),
        in_specs=[row(D), row(PLE), row(D), _full((D, D)), _full((1, D)), _full((PLE, D))],
        out_specs=[row(D), row(D), row(D), row(PLE), _full((1, D)), _full((8, 128))],
        out_shape=[jax.ShapeDtypeStruct((T, D), F32), jax.ShapeDtypeStruct((T, D), BF16),
                   jax.ShapeDtypeStruct((T, D), BF16), jax.ShapeDtypeStruct((T, PLE), BF16),
                   jax.ShapeDtypeStruct((1, D), F32), jax.ShapeDtypeStruct((8, 128), F32)],
        compiler_params=_params(1),
    )(h2, p, tgt, wg, bg, wpe)


def _ffn_bwd(dh2, ff, h1, f1, w1g, w2g, gpost, gpre):
    T = dh2.shape[0]
    tm = _tile(T, 512)
    nb, hb = w1g.shape[0], w1g.shape[2]

    def body(dh2_ref, ff_ref, h1_ref, f1_ref, w1_ref, w2_ref, gpost_ref, gpre_ref,
             dffb_ref, df1_ref, dh1_ref, dgpost_ref, dgpre_ref, acc_s):
        i, j = pl.program_id(0), pl.program_id(1)

        @pl.when((i == 0) & (j == 0))
        def _():
            dgpost_ref[...] = jnp.zeros_like(dgpost_ref)
            dgpre_ref[...] = jnp.zeros_like(dgpre_ref)

        @pl.when(j == 0)
        def _():
            dff, dg = _rms_bwd(dh2_ref[...], ff_ref[...], gpost_ref[...])
            dffb_ref[...] = dff.astype(BF16)
            dgpost_ref[...] += dg
            acc_s[...] = jnp.zeros_like(acc_s)

        dact = _nt(dffb_ref[...], w2_ref[...])
        df1 = (dact * (2.0 * jnp.maximum(f1_ref[...], 0.0))).astype(BF16)
        df1_ref[...] = df1
        acc_s[...] += _nt(df1, w1_ref[...])

        @pl.when(j == nb - 1)
        def _():
            dx, dg = _rms_bwd(acc_s[...], h1_ref[...], gpre_ref[...])
            dh1_ref[...] = dh2_ref[...] + dx
            dgpre_ref[...] += dg

    row = lambda n: pl.BlockSpec((tm, n), lambda i, j: (i, 0))
    blk = pl.BlockSpec((tm, hb), lambda i, j: (i, j))
    return pl.pallas_call(
        body,
        name="ffn_bwd",
        grid=(T // tm, nb),
        in_specs=[row(D), row(D), row(D), blk, pl.BlockSpec((None, D, hb), lambda i, j: (j, 0, 0)),
                  pl.BlockSpec((None, hb, D), lambda i, j: (j, 0, 0)), _full((1, D)), _full((1, D))],
        out_specs=[row(D), blk, row(D), _full((1, D)), _full((1, D))],
        out_shape=[jax.ShapeDtypeStruct((T, D), BF16), jax.ShapeDtypeStruct((T, DFF), BF16),
                   jax.ShapeDtypeStruct((T, D), F32), jax.ShapeDtypeStruct((1, D), F32),
                   jax.ShapeDtypeStruct((1, D), F32)],
        scratch_shapes=[pltpu.VMEM((tm, D), F32)],
        compiler_params=_params(2),
    )(dh2, ff, h1, f1, w1g, w2g, gpost, gpre)


def _mix_bwd(dh1, o, us, yatt, lng, lnb, sgw, sgwt, sgbt, gatt, gsg, wout, gpm):
    T = dh1.shape[0]
    tm = _tile(T, 256)

    def body(dh1_ref, o_ref, us_ref, ya_ref, lng_ref, lnb_ref, w_ref, wt_ref, bt_ref, ga_ref, gs_ref, wo_ref, gp_ref,
             dob_ref, dya_ref, dus_ref, dw_ref, dbt_ref, dlng_ref, dlnb_ref, dga_ref, dgs_ref, dgp_ref,
             mixed_s, vnb_s, dvn_s):
        @pl.when(pl.program_id(0) == 0)
        def _():
            for r in (dw_ref, dbt_ref, dlng_ref, dlnb_ref, dga_ref, dgs_ref, dgp_ref):
                r[...] = jnp.zeros_like(r)

        is_a = lax.broadcasted_iota(jnp.int32, (1, 128), 1) < HD
        lane = lax.broadcasted_iota(jnp.int32, (1, 128), 1)
        do, dg = _rms_bwd(dh1_ref[...], o_ref[...], gp_ref[...])
        dgp_ref[...] += dg
        dob = do.astype(BF16)
        dob_ref[...] = dob
        dy = _nt(dob, wo_ref[...])
        datt, dg = _rms_bwd(dy[:, 0:AW], ya_ref[...], ga_ref[...])
        dga_ref[...] += dg
        dya_ref[...] = datt

        lng = lng_ref[...]
        u, vs, ug, tu, tv, vhat, rstd, mixed, ysg = _sgu_forward(us_ref, lng, lnb_ref[...], w_ref, bt_ref, mixed_s, vnb_s, tm)
        dysg, dg = _rms_bwd(dy[:, AW:D], ysg, gs_ref[...])
        dgs_ref[...] += dg
        dus_ref[:, 0:SW] = ((dysg * mixed) * _gelu_grad(u, tu)).astype(BF16)
        dmix = dysg * ug

        rr = lax.broadcasted_iota(jnp.int32, (CH, CH), 0)
        cc = lax.broadcasted_iota(jnp.int32, (CH, CH), 1)
        tril = cc <= rr
        triu = cc >= rr
        for jj in range(4):
            wta = jnp.where(triu, wt_ref[2 * jj], 0.0).astype(BF16)
            wtb = jnp.where(triu, wt_ref[2 * jj + 1], 0.0).astype(BF16)
            for ch in range(tm // CH):
                rs, cs = slice(ch * CH, (ch + 1) * CH), slice(jj * 128, (jj + 1) * 128)
                dm2 = dmix[rs, cs]
                dma = jnp.where(is_a, dm2, 0.0)
                dmb = jnp.where(is_a, 0.0, dm2)
                dma_b, dmb_b = dma.astype(BF16), dmb.astype(BF16)
                vn2 = vnb_s[rs, cs]
                dw_ref[2 * jj] += jnp.where(tril, _nt(dma_b, vn2), 0.0)
                dw_ref[2 * jj + 1] += jnp.where(tril, _nt(dmb_b, vn2), 0.0)
                dvn_s[rs, cs] = _nn(wta, dma_b) + _nn(wtb, dmb_b)
                dba = jnp.sum(dma, axis=1, keepdims=True)
                dbb = jnp.sum(dmb, axis=1, keepdims=True)
                dbt_ref[...] += jnp.where(lane == 2 * jj, dba, 0.0) + jnp.where(lane == 2 * jj + 1, dbb, 0.0)

        dvn = dvn_s[...]
        dlng_ref[...] += jnp.sum(dvn * vhat, axis=0, keepdims=True)
        dlnb_ref[...] += jnp.sum(dvn, axis=0, keepdims=True)
        dvh = dvn * lng
        dvg = rstd * (dvh - jnp.mean(dvh, axis=-1, keepdims=True) - vhat * jnp.mean(dvh * vhat, axis=-1, keepdims=True))
        dus_ref[:, SW:NUS] = (dvg * _gelu_grad(vs, tv)).astype(BF16)

    row = lambda n: pl.BlockSpec((tm, n), lambda i: (i, 0))
    return pl.pallas_call(
        body,
        name="mix_bwd",
        grid=(T // tm,),
        in_specs=[row(D), row(D), row(NUS), row(AW), _full((1, SW)), _full((1, SW)), _full((8, CH, CH)), _full((8, CH, CH)),
                  _full((CH, 8)), _full((1, AW)), _full((1, SW)), _full((D, D)), _full((1, D))],
        out_specs=[row(D), row(AW), row(NUS), _full((8, CH, CH)), _full((CH, 128)), _full((1, SW)), _full((1, SW)),
                   _full((1, AW)), _full((1, SW)), _full((1, D))],
        out_shape=[jax.ShapeDtypeStruct((T, D), BF16), jax.ShapeDtypeStruct((T, AW), F32), jax.ShapeDtypeStruct((T, NUS), BF16),
                   jax.ShapeDtypeStruct((8, CH, CH), F32), jax.ShapeDtypeStruct((CH, 128), F32),
                   jax.ShapeDtypeStruct((1, SW), F32), jax.ShapeDtypeStruct((1, SW), F32),
                   jax.ShapeDtypeStruct((1, AW), F32), jax.ShapeDtypeStruct((1, SW), F32), jax.ShapeDtypeStruct((1, D), F32)],
        scratch_shapes=[pltpu.VMEM((tm, SW), F32), pltpu.VMEM((tm, SW), BF16), pltpu.VMEM((tm, SW), F32)],
        compiler_params=_params(1),
    )(dh1, o, us, yatt, lng, lnb, sgw, sgwt, sgbt, gatt, gsg, wout, gpm)


def _attn_delta(dya, yatt):
    T = dya.shape[0]
    tm = _tile(T, 512)

    def body(do_ref, o_ref, d_ref):
        head = lax.broadcasted_iota(jnp.int32, (8, AW), 0)
        feat = lax.broadcasted_iota(jnp.int32, (8, AW), 1)
        sel = jnp.where((feat >= head * HD) & (feat < (head + 1) * HD), 1.0, 0.0)
        d_ref[...] = lax.dot_general(sel, do_ref[...] * o_ref[...], (((1,), (1,)), ((), ())),
                                     precision=lax.Precision.HIGHEST, preferred_element_type=F32)

    row = pl.BlockSpec((tm, AW), lambda i: (i, 0))
    return pl.pallas_call(
        body,
        name="attn_delta",
        grid=(T // tm,),
        in_specs=[row, row],
        out_specs=pl.BlockSpec((8, tm), lambda i: (0, i)),
        out_shape=jax.ShapeDtypeStruct((8, T), F32),
        compiler_params=_params(1),
    )(dya, yatt)


def _attn_bwd(qkv, kt, dya, delta, lse, c, ct):
    T = qkv.shape[0]
    tq = _tile(T, 512)
    tk = tq
    nq = T // tq

    def body(q_ref, k_ref, v_ref, kt_ref, do_ref, d_ref, lse_ref, c_ref, ctq_ref,
             dqt_ref, dcq_ref, dk_ref, dv_ref, dck_ref, qw_s, dow_s, pb_s, dsb_s, dk_s, dv_s, dck_s):
        hp = pl.program_id(0)
        j, i = _tri_k_major(pl.program_id(1), nq)
        sub8 = lax.broadcasted_iota(jnp.int32, (8, 1), 0)
        lane = lax.broadcasted_iota(jnp.int32, (1, 128), 1)

        @pl.when(pl.program_id(1) == 0)
        def _():
            dqt_ref[...] = jnp.zeros_like(dqt_ref)
            dcq_ref[...] = jnp.zeros_like(dcq_ref)

        @pl.when(i == j)
        def _():
            dk_s[...] = jnp.zeros_like(dk_s)
            dv_s[...] = jnp.zeros_like(dv_s)
            dck_s[...] = jnp.zeros_like(dck_s)

        def step(masked):
            _stack_heads(q_ref[...], qw_s, tq, 0.125)
            _stack_heads(do_ref[...], dow_s, tq, 1.0)
            s2 = _nt(k_ref[...], qw_s[...])
            dp2 = _nt(v_ref[...], dow_s[...])
            cols = pl.ds(pl.multiple_of(i * tq, tq), tq)
            dcq = jnp.zeros((8, tq), F32)
            dck = jnp.zeros((tk, 128), F32)
            for hh in range(2):
                h = 2 * hp + hh
                half = slice(hh * tq, (hh + 1) * tq)
                lse_h = jnp.sum(jnp.where(sub8 == h, lse_ref[...], 0.0), axis=0, keepdims=True)
                d_h = jnp.sum(jnp.where(sub8 == h, d_ref[...], 0.0), axis=0, keepdims=True)
                pr = jnp.exp(s2[:, half] + _key_bias(c_ref[...], ctq_ref[...], h) - lse_h)
                if masked:
                    key = lax.broadcasted_iota(jnp.int32, (tk, tq), 0)
                    qry = lax.broadcasted_iota(jnp.int32, (tk, tq), 1)
                    pr = jnp.where(key <= qry, pr, 0.0)
                ds = pr * (dp2[:, half] - d_h)
                pb_s[:, half] = pr.astype(BF16)
                dsb_s[:, half] = ds.astype(BF16)
                dcq = dcq + jnp.where(sub8 == hh, jnp.sum(ds, axis=0, keepdims=True), 0.0)
                dck = dck - jnp.where(lane == hh, jnp.sum(ds, axis=1, keepdims=True), 0.0)
            dv_s[...] += _nn(pb_s[...], dow_s[...])
            dk_s[...] += _nn(dsb_s[...], qw_s[...])
            dck_s[...] += dck
            dcq_ref[:, cols] += dcq
            dqt2 = _nn(kt_ref[...] * 0.125, dsb_s[...])
            sub = lax.broadcasted_iota(jnp.int32, (128, 1), 0)
            dqt_ref[:, cols] += jnp.where(sub < HD, dqt2[:, 0:tq], dqt2[:, tq:2 * tq])

        @pl.when(i > j)
        def _():
            step(False)

        @pl.when(i == j)
        def _():
            step(True)

        @pl.when(i == nq - 1)
        def _():
            dk_ref[...] = dk_s[...].astype(BF16)
            dv_ref[...] = dv_s[...].astype(BF16)
            dck_ref[...] = dck_s[...]

    kj = lambda s: _tri_k_major(s, nq)[0]
    qi = lambda s: _tri_k_major(s, nq)[1]
    qrow = lambda hp, s: (qi(s), hp)
    qcol = lambda hp, s: (0, qi(s))
    krow = lambda hp, s: (kj(s), hp)
    return pl.pallas_call(
        body,
        name="attn_bwd",
        grid=(4, nq * (nq + 1) // 2),
        in_specs=[
            pl.BlockSpec((tq, 128), qrow),
            pl.BlockSpec((tk, 128), lambda hp, s: (kj(s), 4 + hp)),
            pl.BlockSpec((tk, 128), lambda hp, s: (kj(s), 8 + hp)),
            pl.BlockSpec((128, tk), lambda hp, s: (hp, kj(s))),
            pl.BlockSpec((tq, 128), qrow),
            pl.BlockSpec((8, tq), qcol),
            pl.BlockSpec((8, tq), qcol),
            pl.BlockSpec((tk, FPAD), lambda hp, s: (kj(s), 0)),
            pl.BlockSpec((8, tq), qcol),
        ],
        out_specs=[
            pl.BlockSpec((128, T), lambda hp, s: (hp, 0)),
            pl.BlockSpec((None, 8, T), lambda hp, s: (hp, 0, 0)),
            pl.BlockSpec((tk, 128), krow),
            pl.BlockSpec((tk, 128), krow),
            pl.BlockSpec((tk, 128), krow),
        ],
        out_shape=[jax.ShapeDtypeStruct((AW, T), F32), jax.ShapeDtypeStruct((4, 8, T), F32),
                   jax.ShapeDtypeStruct((T, AW), BF16), jax.ShapeDtypeStruct((T, AW), BF16),
                   jax.ShapeDtypeStruct((T, AW), F32)],
        scratch_shapes=[pltpu.VMEM((2 * tq, 128), BF16), pltpu.VMEM((2 * tq, 128), BF16),
                        pltpu.VMEM((tk, 2 * tq), BF16), pltpu.VMEM((tk, 2 * tq), BF16),
                        pltpu.VMEM((tk, 128), F32), pltpu.VMEM((tk, 128), F32), pltpu.VMEM((tk, 128), F32)],
        compiler_params=_params(2),
    )(qkv, qkv, qkv, kt, dya, delta, lse, c, ct)


def _fgate_bwd(dcq, dck, fz, fb):
    T = dck.shape[0]
    tb = _tile(T, 512)
    nb = T // tb

    def body(dcq_ref, dck_ref, fz_ref, fb_ref, df_ref, dfb_ref, carry):
        @pl.when(pl.program_id(0) == 0)
        def _():
            carry[...] = jnp.zeros_like(carry)
            dfb_ref[...] = jnp.zeros_like(dfb_ref)

        dcv = dcq_ref[...] + dck_ref[...]
        r = lax.broadcasted_iota(jnp.int32, (tb, tb), 0)
        cc = lax.broadcasted_iota(jnp.int32, (tb, tb), 1)
        tri = (cc >= r).astype(F32)
        dlf = jnp.dot(tri, dcv, precision=lax.Precision.HIGHEST, preferred_element_type=F32) + carry[...]
        carry[...] = carry[...] + jnp.sum(dcv, axis=0, keepdims=True)
        lane = lax.broadcasted_iota(jnp.int32, (tb, FPAD), 1)
        df = jnp.where(lane < 8, dlf * jax.nn.sigmoid(-(fz_ref[...] + fb_ref[...])), 0.0)
        df_ref[...] = df.astype(BF16)
        dfb_ref[...] += jnp.sum(df, axis=0, keepdims=True)

    rev = pl.BlockSpec((tb, FPAD), lambda i: (nb - 1 - i, 0))
    return pl.pallas_call(
        body,
        name="fgate_bwd",
        grid=(nb,),
        in_specs=[rev, rev, rev, _full((1, FPAD))],
        out_specs=[rev, _full((1, FPAD))],
        out_shape=[jax.ShapeDtypeStruct((T, FPAD), BF16), jax.ShapeDtypeStruct((1, FPAD), F32)],
        scratch_shapes=[pltpu.VMEM((1, FPAD), F32)],
        compiler_params=_params(1),
    )(dcq, dck, fz, fb)


def _inproj_bwd(dq, dk, dv, dus, dfz, wp, x, dh1, g):
    T = x.shape[0]
    tm = _tile(T, 512)

    def body(dq_ref, dk_ref, dv_ref, dus_ref, dfz_ref, w_ref, x_ref, dh1_ref, g_ref, gx_ref, dg_ref):
        @pl.when(pl.program_id(0) == 0)
        def _():
            dg_ref[...] = jnp.zeros_like(dg_ref)

        da = _nt(dq_ref[...].astype(BF16), w_ref[:, 0:AW])
        da += _nt(dk_ref[...], w_ref[:, AW:2 * AW])
        da += _nt(dv_ref[...], w_ref[:, 2 * AW:NQKV])
        da += _nt(dus_ref[...], w_ref[:, NQKV:NQKV + NUS])
        da += _nt(dfz_ref[...], w_ref[:, NQKV + NUS:ZP])
        dx, dg = _rms_bwd(da, x_ref[...], g_ref[...])
        gx_ref[...] = dh1_ref[...] + dx
        dg_ref[...] += dg

    row = lambda n: pl.BlockSpec((tm, n), lambda i: (i, 0))
    return pl.pallas_call(
        body,
        name="inproj_bwd",
        grid=(T // tm,),
        in_specs=[row(AW), row(AW), row(AW), row(NUS), row(FPAD), _full((D, ZP)), row(D), row(D), _full((1, D))],
        out_specs=[row(D), _full((1, D))],
        out_shape=[jax.ShapeDtypeStruct((T, D), F32), jax.ShapeDtypeStruct((1, D), F32)],
        compiler_params=_params(1),
    )(dq, dk, dv, dus, dfz, wp, x, dh1, g)


def _wgrad(a, b, name, colblk=None):
    T, K = a.shape
    N = b.shape[1]
    bk = min(K, 1024)
    bn = colblk if colblk else min(N, 1024)
    tt = _tile(T, 512)

    def body(a_ref, b_ref, o_ref):
        @pl.when(pl.program_id(2) == 0)
        def _():
            o_ref[...] = jnp.zeros_like(o_ref)

        o_ref[...] += _tn(a_ref[...].astype(BF16), b_ref[...].astype(BF16))

    if colblk:
        out_shape = jax.ShapeDtypeStruct((N // bn, K, bn), F32)
        out_spec = pl.BlockSpec((None, bk, bn), lambda k, n, t: (n, k, 0))
    else:
        out_shape = jax.ShapeDtypeStruct((K, N), F32)
        out_spec = pl.BlockSpec((bk, bn), lambda k, n, t: (k, n))
    return pl.pallas_call(
        body,
        name=name,
        grid=(K // bk, N // bn, T // tt),
        in_specs=[pl.BlockSpec((tt, bk), lambda k, n, t: (t, k)), pl.BlockSpec((tt, bn), lambda k, n, t: (t, n))],
        out_specs=out_spec,
        out_shape=out_shape,
        compiler_params=_params(3),
    )(a, b)


def _adam_math(w, g, m, v):
    m = ADAM_B1 * m + (1.0 - ADAM_B1) * g
    v = ADAM_B2 * v + (1.0 - ADAM_B2) * (g * g)
    m_hat = m / (1.0 - ADAM_B1 ** ADAM_STEP)
    v_hat = v / (1.0 - ADAM_B2 ** ADAM_STEP)
    delta = -ADAM_LR * (m_hat / (jnp.sqrt(v_hat) + ADAM_EPS) + ADAM_WD * w)
    return delta, m, v


def _adam(parts, w, m, v, name):
    R, C = w.shape
    br = 128 if R % 128 == 0 else R

    def body(p_ref, w_ref, m_ref, v_ref, g_ref, d_ref, nm_ref, nv_ref):
        g = p_ref[0]
        for s in range(1, NDEV):
            g = g + p_ref[s]
        g_ref[...] = g
        d_ref[...], nm_ref[...], nv_ref[...] = _adam_math(w_ref[...], g, m_ref[...], v_ref[...])

    blk = pl.BlockSpec((br, C), lambda i: (i, 0))
    return pl.pallas_call(
        body,
        name=name,
        grid=(R // br,),
        in_specs=[pl.BlockSpec((NDEV, br, C), lambda i: (0, i, 0)), blk, blk, blk],
        out_specs=[blk] * 4,
        out_shape=[jax.ShapeDtypeStruct((R, C), F32)] * 4,
        compiler_params=_params(1),
    )(parts, w, m, v)


_SMALL = (("sg_w", 8 * CH * CH), ("f_bias", 8), ("sg_ln_g", SW), ("sg_ln_b", SW), ("sg_b", 8 * CH), ("att_out_g", AW),
          ("sg_out_g", SW), ("pre_mix_g", D), ("post_mix_g", D), ("pre_ffn_g", D), ("post_ffn_g", D), ("ple_gate_b", D))
_SEG = 8 * 128


def _seg_rows(size):
    return 8 * (-(-size // _SEG))


def _pack(vals):
    parts = []
    for name, size in _SMALL:
        flat = vals[name].reshape(-1)
        rows = _seg_rows(size)
        parts.append(jnp.pad(flat, (0, rows * 128 - size)).reshape(rows, 128))
    return jnp.concatenate(parts, axis=0)


def _unpack(packed, shapes):
    out, r = {}, 0
    for name, size in _SMALL:
        rows = _seg_rows(size)
        out[name] = packed[r:r + rows].reshape(-1)[:size].reshape(shapes[name])
        r += rows
    return out


def kernel(x, p, w_in, f_bias, sg_ln_g, sg_ln_b, sg_w, sg_b, att_out_g, sg_out_g, w_out, pre_mix_g, post_mix_g, pre_ffn_g, post_ffn_g, w_ff1, w_ff2, ple_w, ple_gate_w, ple_gate_b, loss_target, m_w_in, m_f_bias, m_sg_ln_g, m_sg_ln_b, m_sg_w, m_sg_b, m_att_out_g, m_sg_out_g, m_w_out, m_pre_mix_g, m_post_mix_g, m_pre_ffn_g, m_post_ffn_g, m_w_ff1, m_w_ff2, m_ple_w, m_ple_gate_w, m_ple_gate_b, v_w_in, v_f_bias, v_sg_ln_g, v_sg_ln_b, v_sg_w, v_sg_b, v_att_out_g, v_sg_out_g, v_w_out, v_pre_mix_g, v_post_mix_g, v_pre_ffn_g, v_post_ffn_g, v_w_ff1, v_w_ff2, v_ple_w, v_ple_gate_w, v_ple_gate_b):
    small_w = dict(sg_w=sg_w, f_bias=f_bias, sg_ln_g=sg_ln_g, sg_ln_b=sg_ln_b, sg_b=sg_b, att_out_g=att_out_g,
                   sg_out_g=sg_out_g, pre_mix_g=pre_mix_g, post_mix_g=post_mix_g, pre_ffn_g=pre_ffn_g,
                   post_ffn_g=post_ffn_g, ple_gate_b=ple_gate_b)
    small_m = dict(sg_w=m_sg_w, f_bias=m_f_bias, sg_ln_g=m_sg_ln_g, sg_ln_b=m_sg_ln_b, sg_b=m_sg_b, att_out_g=m_att_out_g,
                   sg_out_g=m_sg_out_g, pre_mix_g=m_pre_mix_g, post_mix_g=m_post_mix_g, pre_ffn_g=m_pre_ffn_g,
                   post_ffn_g=m_post_ffn_g, ple_gate_b=m_ple_gate_b)
    small_v = dict(sg_w=v_sg_w, f_bias=v_f_bias, sg_ln_g=v_sg_ln_g, sg_ln_b=v_sg_ln_b, sg_b=v_sg_b, att_out_g=v_att_out_g,
                   sg_out_g=v_sg_out_g, pre_mix_g=v_pre_mix_g, post_mix_g=v_post_mix_g, pre_ffn_g=v_pre_ffn_g,
                   post_ffn_g=v_post_ffn_g, ple_gate_b=v_ple_gate_b)
    big = dict(w_in=(w_in, m_w_in, v_w_in), w_out=(w_out, m_w_out, v_w_out), w_ff1=(w_ff1, m_w_ff1, v_w_ff1),
               w_ff2=(w_ff2, m_w_ff2, v_w_ff2), ple_w=(ple_w, m_ple_w, v_ple_w),
               ple_gate_w=(ple_gate_w, m_ple_gate_w, v_ple_gate_w))

    xt, pt, tgt = x[0], p[0, 0], loss_target[0]
    ws = W_IN_COLS // NDEV

    (gw_in,) = _xchg([w_in[0].astype(BF16)], True, "gather_w_in")
    rest = _xchg_start([w_out[0].astype(BF16), w_ff1[0].astype(BF16), w_ff2[0].astype(BF16), ple_w[0].astype(BF16),
                        ple_gate_w[0].astype(BF16)], True, "gather_rest_start")
    win = jnp.transpose(gw_in, (1, 0, 2)).reshape(D, W_IN_COLS)
    wp = jnp.concatenate([win[:, 0:NQKV], win[:, NQKV + 8:W_IN_COLS], win[:, NQKV:NQKV + 8],
                          jnp.zeros((D, FPAD - 8), BF16)], axis=1)

    fb = jnp.pad(f_bias.astype(F32), ((0, 0), (0, FPAD - 8)))
    sgw = sg_w[0]
    sgwt = jnp.transpose(sg_w[0], (0, 2, 1))
    sgbt = jnp.transpose(sg_b[0])

    qkv, kt, vt, us, fz, ab = _inproj(xt, pre_mix_g + rest[4][0, 0], wp)
    c, ct = _fcum(fz, fb)
    yatt, lse = _attn_fwd(qkv, vt, c, ct)
    lse = lse[:, 0:2, :].reshape(8, -1)
    gw_out, gw1, gw2, gwpe, gwg = _xchg_wait(rest, True, yatt, "gather_rest_wait")
    wout = gw_out.reshape(D, D)
    wg = gwg.reshape(D, D)
    wpe = jnp.transpose(gwpe, (1, 0, 2)).reshape(PLE, D)
    h1, yb, o = _sgu_out(us, yatt, xt, sg_ln_g, sg_ln_b, sgw, sgbt, att_out_g, sg_out_g, wout, post_mix_g)
    c2b, f1, act, ff, h2, h2b = _ffn_fwd(h1, pre_ffn_g, gw1, gw2, post_ffn_g)
    dh2, dpre, dpe, pb, dbg, loss_acc = _ple_loss(h2, pt, tgt, wg, ple_gate_b, wpe)
    loss = lax.psum(loss_acc[0, 0] * (0.5 / D), ("x", "y", "c"))

    g_pe = _wgrad(pb, dpe, "wgrad_ple", colblk=D // NDEV)
    g_g = _wgrad(h2b, dpre, "wgrad_gate").reshape(NDEV, D // NDEV, D)
    dffb, df1, dh1, dgpostffn, dgpreffn = _ffn_bwd(dh2, ff, h1, f1, gw1, gw2, post_ffn_g, pre_ffn_g)
    g_1 = _wgrad(c2b, df1, "wgrad_ff1", colblk=DFF // NDEV)
    g_2 = _wgrad(act, dffb, "wgrad_ff2").reshape(NDEV, DFF // NDEV, D)
    early = _xchg_start([g_1, g_2, g_pe, g_g], False, "scatter_early_start")
    dob, dya, dus, dsgw, dsgbt, dlng, dlnb, dgatt, dgsg, dgpostmix = _mix_bwd(
        dh1, o, us, yatt, sg_ln_g, sg_ln_b, sgw, sgwt, sgbt, att_out_g, sg_out_g, wout, post_mix_g + early[4][0, 0])
    g_out = _wgrad(yb, dob, "wgrad_out").reshape(NDEV, D // NDEV, D)
    mid = _xchg_start([g_out], False, "scatter_mid_start")
    dqt, dcq, dk, dv, dck = _attn_bwd(qkv, kt, dya, _attn_delta(dya, yatt), lse + mid[4][0, 0], c, ct)
    dq = jnp.transpose(dqt)
    dcq = jnp.pad(jnp.transpose(dcq[:, 0:2, :].reshape(8, -1)), ((0, 0), (0, FPAD - 8)))
    dck = jnp.pad(dck.reshape(-1, 4, 128)[:, :, 0:2].reshape(-1, 8), ((0, 0), (0, FPAD - 8)))
    dfz, dfb = _fgate_bwd(dcq, dck, fz, fb)
    grad_x, dgpremix = _inproj_bwd(dq, dk, dv, dus, dfz, wp, xt, dh1, pre_mix_g)

    gq = _wgrad(ab, dq, "wgrad_q")
    gk = _wgrad(ab, dk, "wgrad_k")
    gv = _wgrad(ab, dv, "wgrad_v")
    gus = _wgrad(ab, dus, "wgrad_us")
    gf = _wgrad(ab, dfz, "wgrad_f")
    g_in = jnp.concatenate([gq, gk, gv, gf[:, 0:8], gus], axis=1)
    g_in = jnp.transpose(g_in.reshape(D, NDEV, ws), (1, 0, 2))

    small_g = dict(sg_w=dsgw, f_bias=dfb[:, 0:8], sg_ln_g=dlng, sg_ln_b=dlnb, sg_b=jnp.transpose(dsgbt[:, 0:8]),
                   att_out_g=dgatt, sg_out_g=dgsg, pre_mix_g=dgpremix, post_mix_g=dgpostmix, pre_ffn_g=dgpreffn,
                   post_ffn_g=dgpostffn, ple_gate_b=dbg)

    r_1, r_2, r_pe, r_g = _xchg_wait(early, False, grad_x, "scatter_early_wait")
    (r_out,) = _xchg_wait(mid, False, grad_x, "scatter_mid_wait")
    late = _xchg_start([g_in], False, "scatter_late_start")
    (r_small,) = _xchg([_pack(small_g) + late[4][0, 0]], True, "gather_small_grads")

    res = {}

    def adam_big(name, parts):
        w, m, v = big[name]
        res[name] = [t[None] for t in _adam(parts, w[0], m[0], v[0], "adam_" + name)]

    for name, parts in (("w_out", r_out), ("w_ff1", r_1), ("w_ff2", r_2), ("ple_w", r_pe), ("ple_gate_w", r_g)):
        adam_big(name, parts)
    (r_in,) = _xchg_wait(late, False, res["w_ff1"][0], "scatter_late_wait")
    adam_big("w_in", r_in)
    shapes = {k: a.shape for k, a in small_w.items()}
    small = [_unpack(t, shapes) for t in _adam(r_small, _pack(small_w), _pack(small_m), _pack(small_v), "adam_small")]
    for name, _ in _SMALL:
        res[name] = [s[name] for s in small]

    order = ["w_in", "f_bias", "sg_ln_g", "sg_ln_b", "sg_w", "sg_b", "att_out_g", "sg_out_g", "w_out", "pre_mix_g",
             "post_mix_g", "pre_ffn_g", "post_ffn_g", "w_ff1", "w_ff2", "ple_w", "ple_gate_w", "ple_gate_b"]
    outs = [loss, grad_x[None]]
    for kind in range(4):
        outs += [res[name][kind] for name in order]
    return tuple(outs)
```

```python
import jax
import jax.numpy as jnp
from jax import lax
from jax.experimental import pallas as pl
from jax.experimental.pallas import tpu as pltpu

F32 = jnp.float32
BF16 = jnp.bfloat16

NDEV = 8
D = 1024
AW = 512
SW = 512
HD = 64
CH = 128
DFF = 4096
PLE = 256
NQKV = 3 * AW
NUS = 2 * SW
FPAD = 128
ZP = NQKV + NUS + FPAD
W_IN_COLS = 2568
EPS = 1e-6
MASKV = -1e30
GELU_K = 0.7978845608028654
GELU_C = 0.044715

ADAM_LR = 0.001
ADAM_B1 = 0.9
ADAM_B2 = 0.999
ADAM_EPS = 1e-08
ADAM_WD = 0.01
ADAM_STEP = 10

VMEM_LIMIT = 48 * 1024 * 1024
VMEM_LIMIT_BIG = 60 * 1024 * 1024


def _nn(a, b):
    return jnp.dot(a, b, preferred_element_type=F32)


def _nt(a, b):
    return lax.dot_general(a, b, (((1,), (1,)), ((), ())), preferred_element_type=F32)


def _tn(a, b):
    return lax.dot_general(a, b, (((0,), (0,)), ((), ())), preferred_element_type=F32)


def _tile(n, pref):
    return min(n, pref)


def _params(n_axes, vmem=VMEM_LIMIT):
    return pltpu.CompilerParams(dimension_semantics=("arbitrary",) * n_axes, vmem_limit_bytes=vmem)


def _full(shape):
    nd = len(shape)
    return pl.BlockSpec(shape, lambda *_: (0,) * nd)


def _rms_fwd(x, g):
    r = lax.rsqrt(jnp.mean(x * x, axis=-1, keepdims=True) + EPS)
    return x * r * g


def _rms_bwd(dy, x, g):
    n = x.shape[-1]
    r = lax.rsqrt(jnp.mean(x * x, axis=-1, keepdims=True) + EPS)
    u = dy * g
    s = jnp.sum(x * u, axis=-1, keepdims=True)
    dx = r * u - x * (r * r * r * (s * (1.0 / n)))
    dg = jnp.sum(dy * (x * r), axis=0, keepdims=True)
    return dx, dg


def _gelu(x):
    t = jnp.tanh(GELU_K * (x + GELU_C * (x * x * x)))
    return x * (0.5 * (1.0 + t)), t


def _gelu_grad(x, t):
    return 0.5 * (1.0 + t) + 0.5 * x * (1.0 - t * t) * (GELU_K * (1.0 + 3.0 * GELU_C * x * x))


def _xchg(arrs, gather, name):
    n = len(arrs)
    out_shapes = [jax.ShapeDtypeStruct(((NDEV,) + a.shape) if gather else a.shape, a.dtype) for a in arrs]

    def body(*refs):
        ins, outs = refs[:n], refs[n:2 * n]
        send, recv, loc = refs[2 * n:]
        x, y, c = lax.axis_index("x"), lax.axis_index("y"), lax.axis_index("c")
        me = 4 * x + 2 * y + c
        local = []
        for a in range(n):
            cp = pltpu.make_async_copy(ins[a] if gather else ins[a].at[me], outs[a].at[me], loc.at[a])
            cp.start()
            local.append(cp)
        remote = []
        for k in range(1, NDEV):
            px = 1 - x if (k >> 2) & 1 else x
            py = 1 - y if (k >> 1) & 1 else y
            pc = 1 - c if k & 1 else c
            peer = 4 * px + 2 * py + pc
            for a in range(n):
                cp = pltpu.make_async_remote_copy(
                    src_ref=ins[a] if gather else ins[a].at[peer],
                    dst_ref=outs[a].at[me],
                    send_sem=send.at[a, k - 1],
                    recv_sem=recv.at[a, k - 1],
                    device_id=(px, py, pc),
                    device_id_type=pl.DeviceIdType.MESH,
                )
                cp.start()
                remote.append(cp)
        for cp in remote:
            cp.wait_send()
            cp.wait_recv()
        for cp in local:
            cp.wait()

    hbm = pl.BlockSpec(memory_space=pltpu.HBM)
    return pl.pallas_call(
        body,
        name=name,
        out_shape=out_shapes,
        in_specs=[hbm] * n,
        out_specs=[hbm] * n,
        scratch_shapes=[
            pltpu.SemaphoreType.DMA((n, NDEV - 1)),
            pltpu.SemaphoreType.DMA((n, NDEV - 1)),
            pltpu.SemaphoreType.DMA((n,)),
        ],
    )(*arrs)


def _peers(x, y, c):
    out = []
    for k in range(1, NDEV):
        out.append((1 - x if (k >> 2) & 1 else x, 1 - y if (k >> 1) & 1 else y, 1 - c if k & 1 else c))
    return out


def _xchg_start(arrs, gather, name):
    n = len(arrs)
    me = 4 * lax.axis_index("x") + 2 * lax.axis_index("y") + lax.axis_index("c")
    lands = []
    for a in arrs:
        shape = ((NDEV,) + a.shape) if gather else a.shape
        own = a[None] if gather else lax.dynamic_slice_in_dim(a, me, 1, axis=0)
        lands.append(lax.dynamic_update_slice_in_dim(lax.empty(shape, a.dtype), own, me, axis=0))

    def body(*refs):
        ins, lnd = refs[:n], refs[n:2 * n]
        send, recv, token = refs[2 * n:3 * n], refs[3 * n:4 * n], refs[-1]
        x, y, c = lax.axis_index("x"), lax.axis_index("y"), lax.axis_index("c")
        mine = 4 * x + 2 * y + c
        for px, py, pc in _peers(x, y, c):
            peer = 4 * px + 2 * py + pc
            for a in range(n):
                pltpu.make_async_remote_copy(
                    src_ref=ins[a] if gather else ins[a].at[peer],
                    dst_ref=lnd[a].at[mine],
                    send_sem=send[a],
                    recv_sem=recv[a],
                    device_id=(px, py, pc),
                    device_id_type=pl.DeviceIdType.MESH,
                ).start()
        token[...] = jnp.zeros_like(token)

    hbm = pl.BlockSpec(memory_space=pltpu.HBM)
    sem = pl.BlockSpec(memory_space=pltpu.SEMAPHORE)
    res = pl.pallas_call(
        body,
        name=name,
        out_shape=(*[pltpu.SemaphoreType.DMA(())] * (2 * n),
                   *[pltpu.HBM(a.shape, a.dtype) for a in arrs], *[pltpu.HBM(l.shape, l.dtype) for l in lands],
                   jax.ShapeDtypeStruct((8, 128), F32)),
        in_specs=[hbm] * (2 * n),
        out_specs=(*([sem] * (2 * n)), *([hbm] * (2 * n)), pl.BlockSpec(memory_space=pltpu.VMEM)),
        input_output_aliases={i: 2 * n + i for i in range(2 * n)},
        compiler_params=pltpu.CompilerParams(has_side_effects=pltpu.SideEffectType.DATAFLOW_SIDE_EFFECTING),
    )(*[pltpu.with_memory_space_constraint(a, pltpu.HBM) for a in arrs],
      *[pltpu.with_memory_space_constraint(l, pltpu.HBM) for l in lands])
    return list(res[0:n]), list(res[n:2 * n]), list(res[2 * n:3 * n]), list(res[3 * n:4 * n]), res[-1]


def _xchg_wait(started, gather, after, name):
    send, recv, srcs, lands, _ = started
    n = len(srcs)

    def body(*refs):
        lnd = refs[n:2 * n]
        send, recv = refs[2 * n:3 * n], refs[3 * n:4 * n]
        me = (lax.axis_index("x"), lax.axis_index("y"), lax.axis_index("c"))
        for a in range(n):
            seven = lnd[a].at[pl.ds(0, NDEV - 1)]
            cp = pltpu.make_async_remote_copy(src_ref=seven, dst_ref=seven, send_sem=send[a], recv_sem=recv[a],
                                              device_id=me, device_id_type=pl.DeviceIdType.MESH)
            cp.wait_send()
            cp.wait_recv()

    hbm = pl.BlockSpec(memory_space=pltpu.HBM)
    sem = pl.BlockSpec(memory_space=pltpu.SEMAPHORE)
    res = pl.pallas_call(
        body,
        name=name,
        out_shape=tuple([pltpu.HBM(a.shape, a.dtype) for a in srcs] + [pltpu.HBM(l.shape, l.dtype) for l in lands]),
        in_specs=[hbm] * (2 * n) + [sem] * (2 * n) + [pl.BlockSpec(memory_space=pl.ANY)],
        out_specs=tuple([hbm] * (2 * n)),
        input_output_aliases={i: i for i in range(2 * n)},
        compiler_params=pltpu.CompilerParams(has_side_effects=pltpu.SideEffectType.DATAFLOW_SIDE_EFFECTING),
    )(*srcs, *lands, *send, *recv, after)
    return list(res[n:])


def _inproj(x, g, wp):
    T = x.shape[0]
    tm = _tile(T, 512)

    def body(x_ref, g_ref, w_ref, qkv_ref, kt_ref, vt_ref, us_ref, fz_ref, ab_ref):
        a = _rms_fwd(x_ref[...], g_ref[...]).astype(BF16)
        ab_ref[...] = a
        qkv_ref[:, 0:AW] = _nn(a, w_ref[:, 0:AW]).astype(BF16)
        kk = _nn(a, w_ref[:, AW:2 * AW])
        qkv_ref[:, AW:2 * AW] = kk.astype(BF16)
        kt_ref[...] = kk.T.astype(BF16)
        vv = _nn(a, w_ref[:, 2 * AW:NQKV])
        qkv_ref[:, 2 * AW:NQKV] = vv.astype(BF16)
        vt_ref[...] = vv.T.astype(BF16)
        us_ref[...] = _nn(a, w_ref[:, NQKV:NQKV + NUS])
        fz_ref[...] = _nn(a, w_ref[:, NQKV + NUS:ZP])

    row = lambda n: pl.BlockSpec((tm, n), lambda i: (i, 0))
    col = pl.BlockSpec((AW, tm), lambda i: (0, i))
    return pl.pallas_call(
        body,
        name="inproj",
        grid=(T // tm,),
        in_specs=[row(D), _full((1, D)), _full((D, ZP))],
        out_specs=[row(NQKV), col, col, row(NUS), row(FPAD), row(D)],
        out_shape=[
            jax.ShapeDtypeStruct((T, NQKV), BF16),
            jax.ShapeDtypeStruct((AW, T), BF16),
            jax.ShapeDtypeStruct((AW, T), BF16),
            jax.ShapeDtypeStruct((T, NUS), F32),
            jax.ShapeDtypeStruct((T, FPAD), F32),
            jax.ShapeDtypeStruct((T, D), BF16),
        ],
        compiler_params=_params(1),
    )(x, g, wp)


def _log_sigmoid(z):
    return jnp.minimum(z, 0.0) - jnp.log1p(jnp.exp(-jnp.abs(z)))


def _fcum(fz, fb):
    T = fz.shape[0]
    tb = _tile(T, 512)

    def body(fz_ref, fb_ref, c_ref, ct_ref, carry):
        @pl.when(pl.program_id(0) == 0)
        def _():
            carry[...] = jnp.zeros_like(carry)

        lf = _log_sigmoid(fz_ref[...] + fb_ref[...])
        r = lax.broadcasted_iota(jnp.int32, (tb, tb), 0)
        cc = lax.broadcasted_iota(jnp.int32, (tb, tb), 1)
        tri = (cc <= r).astype(F32)
        cs = jnp.dot(tri, lf, precision=lax.Precision.HIGHEST, preferred_element_type=F32) + carry[...]
        c_ref[...] = cs
        ct_ref[...] = cs.T[0:8, :]
        carry[...] = carry[...] + jnp.sum(lf, axis=0, keepdims=True)

    return pl.pallas_call(
        body,
        name="fcum",
        grid=(T // tb,),
        in_specs=[pl.BlockSpec((tb, FPAD), lambda i: (i, 0)), _full((1, FPAD))],
        out_specs=[pl.BlockSpec((tb, FPAD), lambda i: (i, 0)), pl.BlockSpec((8, tb), lambda i: (0, i))],
        out_shape=[jax.ShapeDtypeStruct((T, FPAD), F32), jax.ShapeDtypeStruct((8, T), F32)],
        scratch_shapes=[pltpu.VMEM((1, FPAD), F32)],
        compiler_params=_params(1),
    )(fz, fb)


def _key_bias(c_keys, ct_q, h):
    lane = lax.broadcasted_iota(jnp.int32, c_keys.shape, 1)
    sub = lax.broadcasted_iota(jnp.int32, (8, 1), 0)
    ck = jnp.sum(jnp.where(lane == h, c_keys, 0.0), axis=1, keepdims=True)
    cref = jnp.sum(jnp.where(sub == h, ct_q[:, 0:1], 0.0), axis=0, keepdims=True)
    return cref - ck


def _stack_heads(x2, dst, tq, scale):
    is_a = lax.broadcasted_iota(jnp.int32, (1, 128), 1) < HD
    zero = jnp.zeros_like(x2)
    dst[0:tq, :] = (jnp.where(is_a, x2, zero) * scale).astype(dst.dtype)
    dst[tq:2 * tq, :] = (jnp.where(is_a, zero, x2) * scale).astype(dst.dtype)


def _tri_q_major(s, nq):
    i = sum(jnp.where(s >= k * (k + 1) // 2, 1, 0) for k in range(1, nq))
    return i, s - i * (i + 1) // 2


def _tri_k_major(s, nq):
    j = sum(jnp.where(s >= k * nq - k * (k - 1) // 2, 1, 0) for k in range(1, nq))
    return j, j + s - (j * nq - j * (j - 1) // 2)


def _attn_fwd(qkv, vt, c, ct):
    T = qkv.shape[0]
    tq = _tile(T, 512)
    tk = tq
    nq = T // tq

    def body(q_ref, k_ref, vt_ref, c_ref, ctq_ref, o_ref, lse_ref, qw_s, m_s, l_s, acc_s):
        hp = pl.program_id(0)
        i, j = _tri_q_major(pl.program_id(1), nq)

        @pl.when(j == 0)
        def _():
            _stack_heads(q_ref[...], qw_s, tq, 0.125)
            m_s[...] = jnp.full_like(m_s, MASKV)
            l_s[...] = jnp.zeros_like(l_s)
            acc_s[...] = jnp.zeros_like(acc_s)

        def step(masked):
            s2 = _nt(k_ref[...], qw_s[...])
            vt2 = vt_ref[...]
            for hh in range(2):
                s = s2[:, hh * tq:(hh + 1) * tq] + _key_bias(c_ref[...], ctq_ref[...], 2 * hp + hh)
                if masked:
                    key = lax.broadcasted_iota(jnp.int32, (tk, tq), 0)
                    qry = lax.broadcasted_iota(jnp.int32, (tk, tq), 1)
                    s = jnp.where(key <= qry, s, MASKV)
                m_prev = m_s[hh]
                m_new = jnp.maximum(m_prev, jnp.max(s, axis=0, keepdims=True))
                pr = jnp.exp(s - m_new)
                alpha = jnp.exp(m_prev - m_new)
                l_s[hh] = alpha * l_s[hh] + jnp.sum(pr, axis=0, keepdims=True)
                m_s[hh] = m_new
                acc_s[hh] = alpha * acc_s[hh] + _nn(vt2, pr.astype(BF16))

        @pl.when(j < i)
        def _():
            step(False)

        @pl.when(j == i)
        def _():
            step(True)
            sub = lax.broadcasted_iota(jnp.int32, (128, 1), 0)
            ot = jnp.where(sub < HD, acc_s[0] * (1.0 / l_s[0]), acc_s[1] * (1.0 / l_s[1]))
            o_ref[...] = ot.T
            sub8 = lax.broadcasted_iota(jnp.int32, (8, 1), 0)
            lse_ref[...] = jnp.where(sub8 == 0, m_s[0] + jnp.log(l_s[0]),
                                     jnp.where(sub8 == 1, m_s[1] + jnp.log(l_s[1]), 0.0))

    qi = lambda s: _tri_q_major(s, nq)[0]
    kj = lambda s: _tri_q_major(s, nq)[1]
    return pl.pallas_call(
        body,
        name="attn_fwd",
        grid=(4, nq * (nq + 1) // 2),
        in_specs=[
            pl.BlockSpec((tq, 128), lambda hp, s: (qi(s), hp)),
            pl.BlockSpec((tk, 128), lambda hp, s: (kj(s), 4 + hp)),
            pl.BlockSpec((128, tk), lambda hp, s: (hp, kj(s))),
            pl.BlockSpec((tk, FPAD), lambda hp, s: (kj(s), 0)),
            pl.BlockSpec((8, tq), lambda hp, s: (0, qi(s))),
        ],
        out_specs=[pl.BlockSpec((tq, 128), lambda hp, s: (qi(s), hp)),
                   pl.BlockSpec((None, 8, tq), lambda hp, s: (hp, 0, qi(s)))],
        out_shape=[jax.ShapeDtypeStruct((T, AW), F32), jax.ShapeDtypeStruct((4, 8, T), F32)],
        scratch_shapes=[pltpu.VMEM((2 * tq, 128), BF16), pltpu.VMEM((2, 1, tq), F32), pltpu.VMEM((2, 1, tq), F32),
                        pltpu.VMEM((2, 128, tq), F32)],
        compiler_params=_params(2),
    )(qkv, qkv, vt, c, ct)


def _sgu_forward(us_ref, lng, lnb, w_ref, bt_ref, mixed_s, vnb_s, tm):
    is_a = lax.broadcasted_iota(jnp.int32, (1, 128), 1) < HD
    u = us_ref[:, 0:SW]
    vs = us_ref[:, SW:NUS]
    ug, tu = _gelu(u)
    vg, tv = _gelu(vs)
    mu = jnp.mean(vg, axis=-1, keepdims=True)
    xc = vg - mu
    rstd = lax.rsqrt(jnp.mean(xc * xc, axis=-1, keepdims=True) + EPS)
    vhat = xc * rstd
    vnb_s[...] = (vhat * lng + lnb).astype(BF16)
    rr = lax.broadcasted_iota(jnp.int32, (CH, CH), 0)
    cc = lax.broadcasted_iota(jnp.int32, (CH, CH), 1)
    tril = cc <= rr
    for jj in range(4):
        wa = jnp.where(tril, w_ref[2 * jj], 0.0).astype(BF16)
        wb = jnp.where(tril, w_ref[2 * jj + 1], 0.0).astype(BF16)
        ba = bt_ref[:, 2 * jj:2 * jj + 1]
        bb = bt_ref[:, 2 * jj + 1:2 * jj + 2]
        for ch in range(tm // CH):
            rs, cs = slice(ch * CH, (ch + 1) * CH), slice(jj * 128, (jj + 1) * 128)
            vn2 = vnb_s[rs, cs]
            mixed_s[rs, cs] = jnp.where(is_a, _nn(wa, vn2) + ba, _nn(wb, vn2) + bb)
    mixed = mixed_s[...]
    return u, vs, ug, tu, tv, vhat, rstd, mixed, ug * mixed


def _sgu_out(us, yatt, x, lng, lnb, sgw, sgbt, gatt, gsg, wout, gpm):
    T = us.shape[0]
    tm = _tile(T, 256)

    def body(us_ref, ya_ref, x_ref, lng_ref, lnb_ref, w_ref, bt_ref, ga_ref, gs_ref, wo_ref, gp_ref,
             h1_ref, yb_ref, o_ref, mixed_s, vnb_s):
        ysg = _sgu_forward(us_ref, lng_ref[...], lnb_ref[...], w_ref, bt_ref, mixed_s, vnb_s, tm)[-1]
        yb_ref[:, 0:AW] = _rms_fwd(ya_ref[...], ga_ref[...]).astype(BF16)
        yb_ref[:, AW:D] = _rms_fwd(ysg, gs_ref[...]).astype(BF16)
        o = _nn(yb_ref[...], wo_ref[...])
        o_ref[...] = o
        h1_ref[...] = x_ref[...] + _rms_fwd(o, gp_ref[...])

    row = lambda n: pl.BlockSpec((tm, n), lambda i: (i, 0))
    return pl.pallas_call(
        body,
        name="sgu_out",
        grid=(T // tm,),
        in_specs=[row(NUS), row(AW), row(D), _full((1, SW)), _full((1, SW)), _full((8, CH, CH)), _full((CH, 8)),
                  _full((1, AW)), _full((1, SW)), _full((D, D)), _full((1, D))],
        out_specs=[row(D), row(D), row(D)],
        out_shape=[jax.ShapeDtypeStruct((T, D), F32), jax.ShapeDtypeStruct((T, D), BF16), jax.ShapeDtypeStruct((T, D), F32)],
        scratch_shapes=[pltpu.VMEM((tm, SW), F32), pltpu.VMEM((tm, SW), BF16)],
        compiler_params=_params(1),
    )(us, yatt, x, lng, lnb, sgw, sgbt, gatt, gsg, wout, gpm)


def _ffn_fwd(h1, gpre, w1g, w2g, gpost):
    T = h1.shape[0]
    tm = _tile(T, 512)
    nb, hb = w1g.shape[0], w1g.shape[2]

    def body(h1_ref, gpre_ref, w1_ref, w2_ref, gpost_ref, c2_ref, f1_ref, ff_ref, h2_ref, h2b_ref):
        h1 = h1_ref[...]
        c2 = _rms_fwd(h1, gpre_ref[...]).astype(BF16)
        c2_ref[...] = c2
        ff = jnp.zeros((tm, D), F32)
        for j in range(nb):
            f1 = _nn(c2, w1_ref[j])
            f1_ref[:, j * hb:(j + 1) * hb] = f1.astype(BF16)
            r = jnp.maximum(f1, 0.0)
            ff = ff + _nn((r * r).astype(BF16), w2_ref[j])
        ff_ref[...] = ff
        h2 = h1 + _rms_fwd(ff, gpost_ref[...])
        h2_ref[...] = h2
        h2b_ref[...] = h2.astype(BF16)

    row = lambda n: pl.BlockSpec((tm, n), lambda i: (i, 0))
    once = lambda shape: pl.BlockSpec(shape, lambda i: (0,) * len(shape), pipeline_mode=pl.Buffered(1))
    return pl.pallas_call(
        body,
        name="ffn_fwd",
        grid=(T // tm,),
        in_specs=[row(D), _full((1, D)), once((nb, D, hb)), once((nb, hb, D)), _full((1, D))],
        out_specs=[row(D), row(DFF), row(D), row(D), row(D)],
        out_shape=[jax.ShapeDtypeStruct((T, D), BF16), jax.ShapeDtypeStruct((T, DFF), BF16),
                   jax.ShapeDtypeStruct((T, D), F32), jax.ShapeDtypeStruct((T, D), F32),
                   jax.ShapeDtypeStruct((T, D), BF16)],
        compiler_params=_params(1, VMEM_LIMIT_BIG),
    )(h1, gpre, w1g, w2g, gpost)


def _ple_loss(h2, p, tgt, wg, bg, wpe):
    T = h2.shape[0]
    tm = _tile(T, 512)

    def body(h2_ref, p_ref, t_ref, wg_ref, bg_ref, wpe_ref, dh2_ref, dpre_ref, dpe_ref, pb_ref, dbg_ref, loss_ref):
        @pl.when(pl.program_id(0) == 0)
        def _():
            dbg_ref[...] = jnp.zeros_like(dbg_ref)
            loss_ref[...] = jnp.zeros_like(loss_ref)

        h2 = h2_ref[...]
        gate = jax.nn.sigmoid(_nn(h2.astype(BF16), wg_ref[...]) + bg_ref[...])
        pb = p_ref[...].astype(BF16)
        pb_ref[...] = pb
        pe = _nn(pb, wpe_ref[...])
        diff = (h2 + gate * pe) - t_ref[...]
        loss_ref[...] += jnp.sum(diff * diff)
        dh3 = diff * (1.0 / D)
        dpre = (dh3 * pe) * (gate * (1.0 - gate))
        dpre_b = dpre.astype(BF16)
        dpre_ref[...] = dpre_b
        dpe_ref[...] = (dh3 * gate).astype(BF16)
        dbg_ref[...] += jnp.sum(dpre, axis=0, keepdims=True)
        dh2_ref[...] = dh3 + _nt(dpre_b, wg_ref[...])

    row = lambda n: pl.BlockSpec((tm, n), lambda i: (i, 0))
    return pl.pallas_call(
        body,
        name="ple_loss",
        grid=(T // tm,),
        in_specs=[row(D), row(PLE), row(D), _full((D, D)), _full((1, D)), _full((PLE, D))],
        out_specs=[row(D), row(D), row(D), row(PLE), _full((1, D)), _full((8, 128))],
        out_shape=[jax.ShapeDtypeStruct((T, D), F32), jax.ShapeDtypeStruct((T, D), BF16),
                   jax.ShapeDtypeStruct((T, D), BF16), jax.ShapeDtypeStruct((T, PLE), BF16),
                   jax.ShapeDtypeStruct((1, D), F32), jax.ShapeDtypeStruct((8, 128), F32)],
        compiler_params=_params(1),
    )(h2, p, tgt, wg, bg, wpe)


def _ffn_bwd(dh2, ff, h1, f1, w1g, w2g, gpost, gpre):
    T = dh2.shape[0]
    tm = _tile(T, 512)
    nb, hb = w1g.shape[0], w1g.shape[2]

    def body(dh2_ref, ff_ref, h1_ref, f1_ref, w1_ref, w2_ref, gpost_ref, gpre_ref,
             dffb_ref, df1_ref, dh1_ref, dgpost_ref, dgpre_ref):
        @pl.when(pl.program_id(0) == 0)
        def _():
            dgpost_ref[...] = jnp.zeros_like(dgpost_ref)
            dgpre_ref[...] = jnp.zeros_like(dgpre_ref)

        dh2 = dh2_ref[...]
        dff, dg = _rms_bwd(dh2, ff_ref[...], gpost_ref[...])
        dffb = dff.astype(BF16)
        dffb_ref[...] = dffb
        dgpost_ref[...] += dg
        dc2 = jnp.zeros((tm, D), F32)
        for j in range(nb):
            cols = slice(j * hb, (j + 1) * hb)
            dact = _nt(dffb, w2_ref[j])
            df1 = (dact * (2.0 * jnp.maximum(f1_ref[:, cols].astype(F32), 0.0))).astype(BF16)
            df1_ref[:, cols] = df1
            dc2 = dc2 + _nt(df1, w1_ref[j])
        dx, dg = _rms_bwd(dc2, h1_ref[...], gpre_ref[...])
        dh1_ref[...] = dh2 + dx
        dgpre_ref[...] += dg

    row = lambda n: pl.BlockSpec((tm, n), lambda i: (i, 0))
    once = lambda shape: pl.BlockSpec(shape, lambda i: (0,) * len(shape), pipeline_mode=pl.Buffered(1))
    return pl.pallas_call(
        body,
        name="ffn_bwd",
        grid=(T // tm,),
        in_specs=[row(D), row(D), row(D), row(DFF), once((nb, D, hb)), once((nb, hb, D)), _full((1, D)), _full((1, D))],
        out_specs=[row(D), row(DFF), row(D), _full((1, D)), _full((1, D))],
        out_shape=[jax.ShapeDtypeStruct((T, D), BF16), jax.ShapeDtypeStruct((T, DFF), BF16),
                   jax.ShapeDtypeStruct((T, D), F32), jax.ShapeDtypeStruct((1, D), F32),
                   jax.ShapeDtypeStruct((1, D), F32)],
        compiler_params=_params(1, VMEM_LIMIT_BIG),
    )(dh2, ff, h1, f1, w1g, w2g, gpost, gpre)


def _mix_bwd(dh1, o, us, yatt, lng, lnb, sgw, sgwt, sgbt, gatt, gsg, wout, gpm):
    T = dh1.shape[0]
    tm = _tile(T, 256)

    def body(dh1_ref, o_ref, us_ref, ya_ref, lng_ref, lnb_ref, w_ref, wt_ref, bt_ref, ga_ref, gs_ref, wo_ref, gp_ref,
             dob_ref, dya_ref, dus_ref, dw_ref, dbt_ref, dlng_ref, dlnb_ref, dga_ref, dgs_ref, dgp_ref,
             mixed_s, vnb_s, dvn_s):
        @pl.when(pl.program_id(0) == 0)
        def _():
            for r in (dw_ref, dbt_ref, dlng_ref, dlnb_ref, dga_ref, dgs_ref, dgp_ref):
                r[...] = jnp.zeros_like(r)

        is_a = lax.broadcasted_iota(jnp.int32, (1, 128), 1) < HD
        lane = lax.broadcasted_iota(jnp.int32, (1, 128), 1)
        do, dg = _rms_bwd(dh1_ref[...], o_ref[...], gp_ref[...])
        dgp_ref[...] += dg
        dob = do.astype(BF16)
        dob_ref[...] = dob
        dy = _nt(dob, wo_ref[...])
        datt, dg = _rms_bwd(dy[:, 0:AW], ya_ref[...], ga_ref[...])
        dga_ref[...] += dg
        dya_ref[...] = datt

        lng = lng_ref[...]
        u, vs, ug, tu, tv, vhat, rstd, mixed, ysg = _sgu_forward(us_ref, lng, lnb_ref[...], w_ref, bt_ref, mixed_s, vnb_s, tm)
        dysg, dg = _rms_bwd(dy[:, AW:D], ysg, gs_ref[...])
        dgs_ref[...] += dg
        dus_ref[:, 0:SW] = ((dysg * mixed) * _gelu_grad(u, tu)).astype(BF16)
        dmix = dysg * ug

        rr = lax.broadcasted_iota(jnp.int32, (CH, CH), 0)
        cc = lax.broadcasted_iota(jnp.int32, (CH, CH), 1)
        tril = cc <= rr
        triu = cc >= rr
        for jj in range(4):
            wta = jnp.where(triu, wt_ref[2 * jj], 0.0).astype(BF16)
            wtb = jnp.where(triu, wt_ref[2 * jj + 1], 0.0).astype(BF16)
            for ch in range(tm // CH):
                rs, cs = slice(ch * CH, (ch + 1) * CH), slice(jj * 128, (jj + 1) * 128)
                dm2 = dmix[rs, cs]
                dma = jnp.where(is_a, dm2, 0.0)
                dmb = jnp.where(is_a, 0.0, dm2)
                dma_b, dmb_b = dma.astype(BF16), dmb.astype(BF16)
                vn2 = vnb_s[rs, cs]
                dw_ref[2 * jj] += jnp.where(tril, _nt(dma_b, vn2), 0.0)
                dw_ref[2 * jj + 1] += jnp.where(tril, _nt(dmb_b, vn2), 0.0)
                dvn_s[rs, cs] = _nn(wta, dma_b) + _nn(wtb, dmb_b)
                dba = jnp.sum(dma, axis=1, keepdims=True)
                dbb = jnp.sum(dmb, axis=1, keepdims=True)
                dbt_ref[...] += jnp.where(lane == 2 * jj, dba, 0.0) + jnp.where(lane == 2 * jj + 1, dbb, 0.0)

        dvn = dvn_s[...]
        dlng_ref[...] += jnp.sum(dvn * vhat, axis=0, keepdims=True)
        dlnb_ref[...] += jnp.sum(dvn, axis=0, keepdims=True)
        dvh = dvn * lng
        dvg = rstd * (dvh - jnp.mean(dvh, axis=-1, keepdims=True) - vhat * jnp.mean(dvh * vhat, axis=-1, keepdims=True))
        dus_ref[:, SW:NUS] = (dvg * _gelu_grad(vs, tv)).astype(BF16)

    row = lambda n: pl.BlockSpec((tm, n), lambda i: (i, 0))
    return pl.pallas_call(
        body,
        name="mix_bwd",
        grid=(T // tm,),
        in_specs=[row(D), row(D), row(NUS), row(AW), _full((1, SW)), _full((1, SW)), _full((8, CH, CH)), _full((8, CH, CH)),
                  _full((CH, 8)), _full((1, AW)), _full((1, SW)), _full((D, D)), _full((1, D))],
        out_specs=[row(D), row(AW), row(NUS), _full((8, CH, CH)), _full((CH, 128)), _full((1, SW)), _full((1, SW)),
                   _full((1, AW)), _full((1, SW)), _full((1, D))],
        out_shape=[jax.ShapeDtypeStruct((T, D), BF16), jax.ShapeDtypeStruct((T, AW), F32), jax.ShapeDtypeStruct((T, NUS), BF16),
                   jax.ShapeDtypeStruct((8, CH, CH), F32), jax.ShapeDtypeStruct((CH, 128), F32),
                   jax.ShapeDtypeStruct((1, SW), F32), jax.ShapeDtypeStruct((1, SW), F32),
                   jax.ShapeDtypeStruct((1, AW), F32), jax.ShapeDtypeStruct((1, SW), F32), jax.ShapeDtypeStruct((1, D), F32)],
        scratch_shapes=[pltpu.VMEM((tm, SW), F32), pltpu.VMEM((tm, SW), BF16), pltpu.VMEM((tm, SW), F32)],
        compiler_params=_params(1),
    )(dh1, o, us, yatt, lng, lnb, sgw, sgwt, sgbt, gatt, gsg, wout, gpm)


def _attn_delta(dya, yatt):
    T = dya.shape[0]
    tm = _tile(T, 512)

    def body(do_ref, o_ref, d_ref):
        head = lax.broadcasted_iota(jnp.int32, (8, AW), 0)
        feat = lax.broadcasted_iota(jnp.int32, (8, AW), 1)
        sel = jnp.where((feat >= head * HD) & (feat < (head + 1) * HD), 1.0, 0.0)
        d_ref[...] = lax.dot_general(sel, do_ref[...] * o_ref[...], (((1,), (1,)), ((), ())),
                                     precision=lax.Precision.HIGHEST, preferred_element_type=F32)

    row = pl.BlockSpec((tm, AW), lambda i: (i, 0))
    return pl.pallas_call(
        body,
        name="attn_delta",
        grid=(T // tm,),
        in_specs=[row, row],
        out_specs=pl.BlockSpec((8, tm), lambda i: (0, i)),
        out_shape=jax.ShapeDtypeStruct((8, T), F32),
        compiler_params=_params(1),
    )(dya, yatt)


def _attn_bwd(qkv, kt, dya, delta, lse, c, ct):
    T = qkv.shape[0]
    tq = _tile(T, 512)
    tk = tq
    nq = T // tq

    def body(q_ref, k_ref, v_ref, kt_ref, do_ref, d_ref, lse_ref, c_ref, ctq_ref,
             dqt_ref, dcq_ref, dk_ref, dv_ref, dck_ref, qw_s, dow_s, pb_s, dsb_s, dk_s, dv_s, dck_s):
        hp = pl.program_id(0)
        j, i = _tri_k_major(pl.program_id(1), nq)
        sub8 = lax.broadcasted_iota(jnp.int32, (8, 1), 0)
        lane = lax.broadcasted_iota(jnp.int32, (1, 128), 1)

        @pl.when(pl.program_id(1) == 0)
        def _():
            dqt_ref[...] = jnp.zeros_like(dqt_ref)
            dcq_ref[...] = jnp.zeros_like(dcq_ref)

        @pl.when(i == j)
        def _():
            dk_s[...] = jnp.zeros_like(dk_s)
            dv_s[...] = jnp.zeros_like(dv_s)
            dck_s[...] = jnp.zeros_like(dck_s)

        def step(masked):
            _stack_heads(q_ref[...], qw_s, tq, 0.125)
            _stack_heads(do_ref[...], dow_s, tq, 1.0)
            s2 = _nt(k_ref[...], qw_s[...])
            dp2 = _nt(v_ref[...], dow_s[...])
            cols = pl.ds(pl.multiple_of(i * tq, tq), tq)
            dcq = jnp.zeros((8, tq), F32)
            dck = jnp.zeros((tk, 128), F32)
            for hh in range(2):
                h = 2 * hp + hh
                half = slice(hh * tq, (hh + 1) * tq)
                lse_h = jnp.sum(jnp.where(sub8 == h, lse_ref[...], 0.0), axis=0, keepdims=True)
                d_h = jnp.sum(jnp.where(sub8 == h, d_ref[...], 0.0), axis=0, keepdims=True)
                pr = jnp.exp(s2[:, half] + _key_bias(c_ref[...], ctq_ref[...], h) - lse_h)
                if masked:
                    key = lax.broadcasted_iota(jnp.int32, (tk, tq), 0)
                    qry = lax.broadcasted_iota(jnp.int32, (tk, tq), 1)
                    pr = jnp.where(key <= qry, pr, 0.0)
                ds = pr * (dp2[:, half] - d_h)
                pb_s[:, half] = pr.astype(BF16)
                dsb_s[:, half] = ds.astype(BF16)
                dcq = dcq + jnp.where(sub8 == hh, jnp.sum(ds, axis=0, keepdims=True), 0.0)
                dck = dck - jnp.where(lane == hh, jnp.sum(ds, axis=1, keepdims=True), 0.0)
            dv_s[...] += _nn(pb_s[...], dow_s[...])
            dk_s[...] += _nn(dsb_s[...], qw_s[...])
            dck_s[...] += dck
            dcq_ref[:, cols] += dcq
            dqt2 = _nn(kt_ref[...] * 0.125, dsb_s[...])
            sub = lax.broadcasted_iota(jnp.int32, (128, 1), 0)
            dqt_ref[:, cols] += jnp.where(sub < HD, dqt2[:, 0:tq], dqt2[:, tq:2 * tq])

        @pl.when(i > j)
        def _():
            step(False)

        @pl.when(i == j)
        def _():
            step(True)

        @pl.when(i == nq - 1)
        def _():
            dk_ref[...] = dk_s[...].astype(BF16)
            dv_ref[...] = dv_s[...].astype(BF16)
            dck_ref[...] = dck_s[...]

    kj = lambda s: _tri_k_major(s, nq)[0]
    qi = lambda s: _tri_k_major(s, nq)[1]
    qrow = lambda hp, s: (qi(s), hp)
    qcol = lambda hp, s: (0, qi(s))
    krow = lambda hp, s: (kj(s), hp)
    return pl.pallas_call(
        body,
        name="attn_bwd",
        grid=(4, nq * (nq + 1) // 2),
        in_specs=[
            pl.BlockSpec((tq, 128), qrow),
            pl.BlockSpec((tk, 128), lambda hp, s: (kj(s), 4 + hp)),
            pl.BlockSpec((tk, 128), lambda hp, s: (kj(s), 8 + hp)),
            pl.BlockSpec((128, tk), lambda hp, s: (hp, kj(s))),
            pl.BlockSpec((tq, 128), qrow),
            pl.BlockSpec((8, tq), qcol),
            pl.BlockSpec((8, tq), qcol),
            pl.BlockSpec((tk, FPAD), lambda hp, s: (kj(s), 0)),
            pl.BlockSpec((8, tq), qcol),
        ],
        out_specs=[
            pl.BlockSpec((128, T), lambda hp, s: (hp, 0)),
            pl.BlockSpec((None, 8, T), lambda hp, s: (hp, 0, 0)),
            pl.BlockSpec((tk, 128), krow),
            pl.BlockSpec((tk, 128), krow),
            pl.BlockSpec((tk, 128), krow),
        ],
        out_shape=[jax.ShapeDtypeStruct((AW, T), F32), jax.ShapeDtypeStruct((4, 8, T), F32),
                   jax.ShapeDtypeStruct((T, AW), BF16), jax.ShapeDtypeStruct((T, AW), BF16),
                   jax.ShapeDtypeStruct((T, AW), F32)],
        scratch_shapes=[pltpu.VMEM((2 * tq, 128), BF16), pltpu.VMEM((2 * tq, 128), BF16),
                        pltpu.VMEM((tk, 2 * tq), BF16), pltpu.VMEM((tk, 2 * tq), BF16),
                        pltpu.VMEM((tk, 128), F32), pltpu.VMEM((tk, 128), F32), pltpu.VMEM((tk, 128), F32)],
        compiler_params=_params(2),
    )(qkv, qkv, qkv, kt, dya, delta, lse, c, ct)


def _fgate_bwd(dcq, dck, fz, fb):
    T = dck.shape[0]
    tb = _tile(T, 512)
    nb = T // tb

    def body(dcq_ref, dck_ref, fz_ref, fb_ref, df_ref, dfb_ref, carry):
        @pl.when(pl.program_id(0) == 0)
        def _():
            carry[...] = jnp.zeros_like(carry)
            dfb_ref[...] = jnp.zeros_like(dfb_ref)

        dcv = dcq_ref[...] + dck_ref[...]
        r = lax.broadcasted_iota(jnp.int32, (tb, tb), 0)
        cc = lax.broadcasted_iota(jnp.int32, (tb, tb), 1)
        tri = (cc >= r).astype(F32)
        dlf = jnp.dot(tri, dcv, precision=lax.Precision.HIGHEST, preferred_element_type=F32) + carry[...]
        carry[...] = carry[...] + jnp.sum(dcv, axis=0, keepdims=True)
        lane = lax.broadcasted_iota(jnp.int32, (tb, FPAD), 1)
        df = jnp.where(lane < 8, dlf * jax.nn.sigmoid(-(fz_ref[...] + fb_ref[...])), 0.0)
        df_ref[...] = df.astype(BF16)
        dfb_ref[...] += jnp.sum(df, axis=0, keepdims=True)

    rev = pl.BlockSpec((tb, FPAD), lambda i: (nb - 1 - i, 0))
    return pl.pallas_call(
        body,
        name="fgate_bwd",
        grid=(nb,),
        in_specs=[rev, rev, rev, _full((1, FPAD))],
        out_specs=[rev, _full((1, FPAD))],
        out_shape=[jax.ShapeDtypeStruct((T, FPAD), BF16), jax.ShapeDtypeStruct((1, FPAD), F32)],
        scratch_shapes=[pltpu.VMEM((1, FPAD), F32)],
        compiler_params=_params(1),
    )(dcq, dck, fz, fb)


def _inproj_bwd(dq, dk, dv, dus, dfz, wp, x, dh1, g):
    T = x.shape[0]
    tm = _tile(T, 512)

    def body(dq_ref, dk_ref, dv_ref, dus_ref, dfz_ref, w_ref, x_ref, dh1_ref, g_ref, gx_ref, dg_ref):
        @pl.when(pl.program_id(0) == 0)
        def _():
            dg_ref[...] = jnp.zeros_like(dg_ref)

        da = _nt(dq_ref[...].astype(BF16), w_ref[:, 0:AW])
        da += _nt(dk_ref[...], w_ref[:, AW:2 * AW])
        da += _nt(dv_ref[...], w_ref[:, 2 * AW:NQKV])
        da += _nt(dus_ref[...], w_ref[:, NQKV:NQKV + NUS])
        da += _nt(dfz_ref[...], w_ref[:, NQKV + NUS:ZP])
        dx, dg = _rms_bwd(da, x_ref[...], g_ref[...])
        gx_ref[...] = dh1_ref[...] + dx
        dg_ref[...] += dg

    row = lambda n: pl.BlockSpec((tm, n), lambda i: (i, 0))
    return pl.pallas_call(
        body,
        name="inproj_bwd",
        grid=(T // tm,),
        in_specs=[row(AW), row(AW), row(AW), row(NUS), row(FPAD), _full((D, ZP)), row(D), row(D), _full((1, D))],
        out_specs=[row(D), _full((1, D))],
        out_shape=[jax.ShapeDtypeStruct((T, D), F32), jax.ShapeDtypeStruct((1, D), F32)],
        compiler_params=_params(1),
    )(dq, dk, dv, dus, dfz, wp, x, dh1, g)


def _sq_relu(f1):
    r = jnp.maximum(f1.astype(F32), 0.0)
    return (r * r).astype(BF16)


def _wgrad(a, bs, name, a_fn=None):
    T, K = a.shape
    tt = _tile(T, 512)
    nb = len(bs)

    def out_dims(b, layout):
        N = b.shape[1]
        if layout is None:
            return (K, N)
        return (N // layout[1], K, layout[1]) if layout[0] == "col" else (K // layout[1], layout[1], N)

    shapes = [out_dims(b, layout) for b, layout in bs]

    def body(*refs):
        a_ref, b_refs, o_refs = refs[0], refs[1:1 + nb], refs[1 + nb:]

        @pl.when(pl.program_id(0) == 0)
        def _():
            for o_ref in o_refs:
                o_ref[...] = jnp.zeros_like(o_ref)

        av = a_ref[...] if a_fn is None else a_fn(a_ref[...])
        at = av.astype(BF16).T
        for (b, layout), b_ref, o_ref in zip(bs, b_refs, o_refs):
            if layout is None:
                o_ref[...] += _nn(at, b_ref[...].astype(BF16))
            elif layout[0] == "col":
                n = layout[1]
                for k in range(b.shape[1] // n):
                    o_ref[k] += _nn(at, b_ref[:, k * n:(k + 1) * n].astype(BF16))
            else:
                n = layout[1]
                bv = b_ref[...].astype(BF16)
                for k in range(K // n):
                    o_ref[k] += _nn(at[k * n:(k + 1) * n, :], bv)

    once = lambda shape: pl.BlockSpec(shape, lambda t: (0,) * len(shape), pipeline_mode=pl.Buffered(1))
    res = pl.pallas_call(
        body,
        name=name,
        grid=(T // tt,),
        in_specs=[pl.BlockSpec((tt, K), lambda t: (t, 0))] + [pl.BlockSpec((tt, b.shape[1]), lambda t: (t, 0)) for b, _ in bs],
        out_specs=[once(s) for s in shapes],
        out_shape=[jax.ShapeDtypeStruct(s, F32) for s in shapes],
        compiler_params=_params(1, VMEM_LIMIT_BIG),
    )(a, *[b for b, _ in bs])
    return res


def _adam_math(w, g, m, v):
    m = ADAM_B1 * m + (1.0 - ADAM_B1) * g
    v = ADAM_B2 * v + (1.0 - ADAM_B2) * (g * g)
    m_hat = m / (1.0 - ADAM_B1 ** ADAM_STEP)
    v_hat = v / (1.0 - ADAM_B2 ** ADAM_STEP)
    delta = -ADAM_LR * (m_hat / (jnp.sqrt(v_hat) + ADAM_EPS) + ADAM_WD * w)
    return delta, m, v


def _adam(parts, w, m, v, name):
    R, C = w.shape
    br = 128 if R % 128 == 0 else R

    def body(p_ref, w_ref, m_ref, v_ref, g_ref, d_ref, nm_ref, nv_ref):
        g = p_ref[0]
        for s in range(1, NDEV):
            g = g + p_ref[s]
        g_ref[...] = g
        d_ref[...], nm_ref[...], nv_ref[...] = _adam_math(w_ref[...], g, m_ref[...], v_ref[...])

    blk = pl.BlockSpec((br, C), lambda i: (i, 0))
    return pl.pallas_call(
        body,
        name=name,
        grid=(R // br,),
        in_specs=[pl.BlockSpec((NDEV, br, C), lambda i: (0, i, 0)), blk, blk, blk],
        out_specs=[blk] * 4,
        out_shape=[jax.ShapeDtypeStruct((R, C), F32)] * 4,
        compiler_params=_params(1),
    )(parts, w, m, v)


_SMALL = (("sg_w", 8 * CH * CH), ("f_bias", 8), ("sg_ln_g", SW), ("sg_ln_b", SW), ("sg_b", 8 * CH), ("att_out_g", AW),
          ("sg_out_g", SW), ("pre_mix_g", D), ("post_mix_g", D), ("pre_ffn_g", D), ("post_ffn_g", D), ("ple_gate_b", D))
_SEG = 8 * 128


def _seg_rows(size):
    return 8 * (-(-size // _SEG))


def _pack(vals):
    parts = []
    for name, size in _SMALL:
        flat = vals[name].reshape(-1)
        rows = _seg_rows(size)
        parts.append(jnp.pad(flat, (0, rows * 128 - size)).reshape(rows, 128))
    return jnp.concatenate(parts, axis=0)


def _unpack(packed, shapes):
    out, r = {}, 0
    for name, size in _SMALL:
        rows = _seg_rows(size)
        out[name] = packed[r:r + rows].reshape(-1)[:size].reshape(shapes[name])
        r += rows
    return out


def kernel(x, p, w_in, f_bias, sg_ln_g, sg_ln_b, sg_w, sg_b, att_out_g, sg_out_g, w_out, pre_mix_g, post_mix_g, pre_ffn_g, post_ffn_g, w_ff1, w_ff2, ple_w, ple_gate_w, ple_gate_b, loss_target, m_w_in, m_f_bias, m_sg_ln_g, m_sg_ln_b, m_sg_w, m_sg_b, m_att_out_g, m_sg_out_g, m_w_out, m_pre_mix_g, m_post_mix_g, m_pre_ffn_g, m_post_ffn_g, m_w_ff1, m_w_ff2, m_ple_w, m_ple_gate_w, m_ple_gate_b, v_w_in, v_f_bias, v_sg_ln_g, v_sg_ln_b, v_sg_w, v_sg_b, v_att_out_g, v_sg_out_g, v_w_out, v_pre_mix_g, v_post_mix_g, v_pre_ffn_g, v_post_ffn_g, v_w_ff1, v_w_ff2, v_ple_w, v_ple_gate_w, v_ple_gate_b):
    small_w = dict(sg_w=sg_w, f_bias=f_bias, sg_ln_g=sg_ln_g, sg_ln_b=sg_ln_b, sg_b=sg_b, att_out_g=att_out_g,
                   sg_out_g=sg_out_g, pre_mix_g=pre_mix_g, post_mix_g=post_mix_g, pre_ffn_g=pre_ffn_g,
                   post_ffn_g=post_ffn_g, ple_gate_b=ple_gate_b)
    small_m = dict(sg_w=m_sg_w, f_bias=m_f_bias, sg_ln_g=m_sg_ln_g, sg_ln_b=m_sg_ln_b, sg_b=m_sg_b, att_out_g=m_att_out_g,
                   sg_out_g=m_sg_out_g, pre_mix_g=m_pre_mix_g, post_mix_g=m_post_mix_g, pre_ffn_g=m_pre_ffn_g,
                   post_ffn_g=m_post_ffn_g, ple_gate_b=m_ple_gate_b)
    small_v = dict(sg_w=v_sg_w, f_bias=v_f_bias, sg_ln_g=v_sg_ln_g, sg_ln_b=v_sg_ln_b, sg_b=v_sg_b, att_out_g=v_att_out_g,
                   sg_out_g=v_sg_out_g, pre_mix_g=v_pre_mix_g, post_mix_g=v_post_mix_g, pre_ffn_g=v_pre_ffn_g,
                   post_ffn_g=v_post_ffn_g, ple_gate_b=v_ple_gate_b)
    big = dict(w_in=(w_in, m_w_in, v_w_in), w_out=(w_out, m_w_out, v_w_out), w_ff1=(w_ff1, m_w_ff1, v_w_ff1),
               w_ff2=(w_ff2, m_w_ff2, v_w_ff2), ple_w=(ple_w, m_ple_w, v_ple_w),
               ple_gate_w=(ple_gate_w, m_ple_gate_w, v_ple_gate_w))

    xt, pt, tgt = x[0], p[0, 0], loss_target[0]
    ws = W_IN_COLS // NDEV

    (gw_in,) = _xchg([w_in[0].astype(BF16)], True, "gather_w_in")
    rest = _xchg_start([w_out[0].astype(BF16), w_ff1[0].astype(BF16), w_ff2[0].astype(BF16), ple_w[0].astype(BF16),
                        ple_gate_w[0].astype(BF16)], True, "gather_rest_start")
    win = jnp.transpose(gw_in, (1, 0, 2)).reshape(D, W_IN_COLS)
    wp = jnp.concatenate([win[:, 0:NQKV], win[:, NQKV + 8:W_IN_COLS], win[:, NQKV:NQKV + 8],
                          jnp.zeros((D, FPAD - 8), BF16)], axis=1)

    fb = jnp.pad(f_bias.astype(F32), ((0, 0), (0, FPAD - 8)))
    sgw = sg_w[0]
    sgwt = jnp.transpose(sg_w[0], (0, 2, 1))
    sgbt = jnp.transpose(sg_b[0])

    qkv, kt, vt, us, fz, ab = _inproj(xt, pre_mix_g + rest[4][0, 0], wp)
    c, ct = _fcum(fz, fb)
    yatt, lse = _attn_fwd(qkv, vt, c, ct)
    lse = lse[:, 0:2, :].reshape(8, -1)
    gw_out, gw1, gw2, gwpe, gwg = _xchg_wait(rest, True, yatt, "gather_rest_wait")
    wout = gw_out.reshape(D, D)
    wg = gwg.reshape(D, D)
    wpe = jnp.transpose(gwpe, (1, 0, 2)).reshape(PLE, D)
    h1, yb, o = _sgu_out(us, yatt, xt, sg_ln_g, sg_ln_b, sgw, sgbt, att_out_g, sg_out_g, wout, post_mix_g)
    c2b, f1b, ff, h2, h2b = _ffn_fwd(h1, pre_ffn_g, gw1, gw2, post_ffn_g)
    dh2, dpre, dpe, pb, dbg, loss_acc = _ple_loss(h2, pt, tgt, wg, ple_gate_b, wpe)
    loss = lax.psum(loss_acc[0, 0] * (0.5 / D), ("x", "y", "c"))

    (g_pe,) = _wgrad(pb, [(dpe, ("col", D // NDEV))], "wgrad_ple")
    (g_g,) = _wgrad(h2b, [(dpre, ("row", D // NDEV))], "wgrad_gate")
    dffb, df1, dh1, dgpostffn, dgpreffn = _ffn_bwd(dh2, ff, h1, f1b, gw1, gw2, post_ffn_g, pre_ffn_g)
    (g_1,) = _wgrad(c2b, [(df1, ("col", DFF // NDEV))], "wgrad_ff1")
    (g_2,) = _wgrad(f1b, [(dffb, ("row", DFF // NDEV))], "wgrad_ff2", a_fn=_sq_relu)
    early = _xchg_start([g_1, g_2, g_pe, g_g], False, "scatter_early_start")
    dob, dya, dus, dsgw, dsgbt, dlng, dlnb, dgatt, dgsg, dgpostmix = _mix_bwd(
        dh1, o, us, yatt, sg_ln_g, sg_ln_b, sgw, sgwt, sgbt, att_out_g, sg_out_g, wout, post_mix_g + early[4][0, 0])
    (g_out,) = _wgrad(yb, [(dob, ("row", D // NDEV))], "wgrad_out")
    mid = _xchg_start([g_out], False, "scatter_mid_start")
    dqt, dcq, dk, dv, dck = _attn_bwd(qkv, kt, dya, _attn_delta(dya, yatt), lse + mid[4][0, 0], c, ct)
    dq = jnp.transpose(dqt)
    dcq = jnp.pad(jnp.transpose(dcq[:, 0:2, :].reshape(8, -1)), ((0, 0), (0, FPAD - 8)))
    dck = jnp.pad(dck.reshape(-1, 4, 128)[:, :, 0:2].reshape(-1, 8), ((0, 0), (0, FPAD - 8)))
    dfz, dfb = _fgate_bwd(dcq, dck, fz, fb)

    gq, gk, gv, gus, gf = _wgrad(ab, [(dq, None), (dk, None), (dv, None), (dus, None), (dfz, None)], "wgrad_in")
    g_in = jnp.concatenate([gq, gk, gv, gf[:, 0:8], gus], axis=1)
    g_in = jnp.transpose(g_in.reshape(D, NDEV, ws), (1, 0, 2))
    late = _xchg_start([g_in], False, "scatter_late_start")
    grad_x, dgpremix = _inproj_bwd(dq, dk, dv, dus, dfz, wp, xt, dh1, pre_mix_g + late[4][0, 0])

    small_g = dict(sg_w=dsgw, f_bias=dfb[:, 0:8], sg_ln_g=dlng, sg_ln_b=dlnb, sg_b=jnp.transpose(dsgbt[:, 0:8]),
                   att_out_g=dgatt, sg_out_g=dgsg, pre_mix_g=dgpremix, post_mix_g=dgpostmix, pre_ffn_g=dgpreffn,
                   post_ffn_g=dgpostffn, ple_gate_b=dbg)

    r_1, r_2, r_pe, r_g = _xchg_wait(early, False, grad_x, "scatter_early_wait")
    (r_out,) = _xchg_wait(mid, False, grad_x, "scatter_mid_wait")
    (r_small,) = _xchg([_pack(small_g)], True, "gather_small_grads")

    res = {}

    def adam_big(name, parts):
        w, m, v = big[name]
        res[name] = [t[None] for t in _adam(parts, w[0], m[0], v[0], "adam_" + name)]

    for name, parts in (("w_out", r_out), ("w_ff1", r_1), ("w_ff2", r_2), ("ple_w", r_pe), ("ple_gate_w", r_g)):
        adam_big(name, parts)
    (r_in,) = _xchg_wait(late, False, res["w_ff1"][0], "scatter_late_wait")
    adam_big("w_in", r_in)
    shapes = {k: a.shape for k, a in small_w.items()}
    small = [_unpack(t, shapes) for t in _adam(r_small, _pack(small_w), _pack(small_m), _pack(small_v), "adam_small")]
    for name, _ in _SMALL:
        res[name] = [s[name] for s in small]

    order = ["w_in", "f_bias", "sg_ln_g", "sg_ln_b", "sg_w", "sg_b", "att_out_g", "sg_out_g", "w_out", "pre_mix_g",
             "post_mix_g", "pre_ffn_g", "post_ffn_g", "w_ff1", "w_ff2", "ple_w", "ple_gate_w", "ple_gate_b"]
    outs = [loss, grad_x[None]]
    for kind in range(4):
        outs += [res[name][kind] for name in order]
    return tuple(outs)
```

```python
import jax
import jax.numpy as jnp
from jax import lax
from jax.experimental import pallas as pl
from jax.experimental.pallas import tpu as pltpu

F32 = jnp.float32
BF16 = jnp.bfloat16

NDEV = 8
D = 1024
AW = 512
SW = 512
HD = 64
CH = 128
DFF = 4096
PLE = 256
NQKV = 3 * AW
NUS = 2 * SW
FPAD = 128
ZP = NQKV + NUS + FPAD
W_IN_COLS = 2568
EPS = 1e-6
MASKV = -1e30
GELU_K = 0.7978845608028654
GELU_C = 0.044715

ADAM_LR = 0.001
ADAM_B1 = 0.9
ADAM_B2 = 0.999
ADAM_EPS = 1e-08
ADAM_WD = 0.01
ADAM_STEP = 10

VMEM_LIMIT = 48 * 1024 * 1024
VMEM_LIMIT_BIG = 60 * 1024 * 1024


def _nn(a, b):
    return jnp.dot(a, b, preferred_element_type=F32)


def _nt(a, b):
    return lax.dot_general(a, b, (((1,), (1,)), ((), ())), preferred_element_type=F32)


def _tn(a, b):
    return lax.dot_general(a, b, (((0,), (0,)), ((), ())), preferred_element_type=F32)


def _tile(n, pref):
    return min(n, pref)


def _params(n_axes, vmem=VMEM_LIMIT):
    return pltpu.CompilerParams(dimension_semantics=("arbitrary",) * n_axes, vmem_limit_bytes=vmem)


def _full(shape):
    nd = len(shape)
    return pl.BlockSpec(shape, lambda *_: (0,) * nd)


def _rms_fwd(x, g):
    r = lax.rsqrt(jnp.mean(x * x, axis=-1, keepdims=True) + EPS)
    return x * r * g


def _rms_bwd(dy, x, g):
    n = x.shape[-1]
    r = lax.rsqrt(jnp.mean(x * x, axis=-1, keepdims=True) + EPS)
    u = dy * g
    s = jnp.sum(x * u, axis=-1, keepdims=True)
    dx = r * u - x * (r * r * r * (s * (1.0 / n)))
    dg = jnp.sum(dy * (x * r), axis=0, keepdims=True)
    return dx, dg


def _gelu(x):
    t = jnp.tanh(GELU_K * (x + GELU_C * (x * x * x)))
    return x * (0.5 * (1.0 + t)), t


def _gelu_grad(x, t):
    return 0.5 * (1.0 + t) + 0.5 * x * (1.0 - t * t) * (GELU_K * (1.0 + 3.0 * GELU_C * x * x))


def _xchg(arrs, gather, name):
    n = len(arrs)
    out_shapes = [jax.ShapeDtypeStruct(((NDEV,) + a.shape) if gather else a.shape, a.dtype) for a in arrs]

    def body(*refs):
        ins, outs = refs[:n], refs[n:2 * n]
        send, recv, loc = refs[2 * n:]
        x, y, c = lax.axis_index("x"), lax.axis_index("y"), lax.axis_index("c")
        me = 4 * x + 2 * y + c
        local = []
        for a in range(n):
            cp = pltpu.make_async_copy(ins[a] if gather else ins[a].at[me], outs[a].at[me], loc.at[a])
            cp.start()
            local.append(cp)
        remote = []
        for k in range(1, NDEV):
            px = 1 - x if (k >> 2) & 1 else x
            py = 1 - y if (k >> 1) & 1 else y
            pc = 1 - c if k & 1 else c
            peer = 4 * px + 2 * py + pc
            for a in range(n):
                cp = pltpu.make_async_remote_copy(
                    src_ref=ins[a] if gather else ins[a].at[peer],
                    dst_ref=outs[a].at[me],
                    send_sem=send.at[a, k - 1],
                    recv_sem=recv.at[a, k - 1],
                    device_id=(px, py, pc),
                    device_id_type=pl.DeviceIdType.MESH,
                )
                cp.start()
                remote.append(cp)
        for cp in remote:
            cp.wait_send()
            cp.wait_recv()
        for cp in local:
            cp.wait()

    hbm = pl.BlockSpec(memory_space=pltpu.HBM)
    return pl.pallas_call(
        body,
        name=name,
        out_shape=out_shapes,
        in_specs=[hbm] * n,
        out_specs=[hbm] * n,
        scratch_shapes=[
            pltpu.SemaphoreType.DMA((n, NDEV - 1)),
            pltpu.SemaphoreType.DMA((n, NDEV - 1)),
            pltpu.SemaphoreType.DMA((n,)),
        ],
    )(*arrs)


def _peers(x, y, c):
    out = []
    for k in range(1, NDEV):
        out.append((1 - x if (k >> 2) & 1 else x, 1 - y if (k >> 1) & 1 else y, 1 - c if k & 1 else c))
    return out


def _xchg_start(arrs, gather, name):
    n = len(arrs)
    me = 4 * lax.axis_index("x") + 2 * lax.axis_index("y") + lax.axis_index("c")
    lands = []
    for a in arrs:
        shape = ((NDEV,) + a.shape) if gather else a.shape
        own = a[None] if gather else lax.dynamic_slice_in_dim(a, me, 1, axis=0)
        lands.append(lax.dynamic_update_slice_in_dim(lax.empty(shape, a.dtype), own, me, axis=0))

    def body(*refs):
        ins, lnd = refs[:n], refs[n:2 * n]
        send, recv, token = refs[2 * n:3 * n], refs[3 * n:4 * n], refs[-1]
        x, y, c = lax.axis_index("x"), lax.axis_index("y"), lax.axis_index("c")
        mine = 4 * x + 2 * y + c
        for px, py, pc in _peers(x, y, c):
            peer = 4 * px + 2 * py + pc
            for a in range(n):
                pltpu.make_async_remote_copy(
                    src_ref=ins[a] if gather else ins[a].at[peer],
                    dst_ref=lnd[a].at[mine],
                    send_sem=send[a],
                    recv_sem=recv[a],
                    device_id=(px, py, pc),
                    device_id_type=pl.DeviceIdType.MESH,
                ).start()
        token[...] = jnp.zeros_like(token)

    hbm = pl.BlockSpec(memory_space=pltpu.HBM)
    sem = pl.BlockSpec(memory_space=pltpu.SEMAPHORE)
    res = pl.pallas_call(
        body,
        name=name,
        out_shape=(*[pltpu.SemaphoreType.DMA(())] * (2 * n),
                   *[pltpu.HBM(a.shape, a.dtype) for a in arrs], *[pltpu.HBM(l.shape, l.dtype) for l in lands],
                   jax.ShapeDtypeStruct((8, 128), F32)),
        in_specs=[hbm] * (2 * n),
        out_specs=(*([sem] * (2 * n)), *([hbm] * (2 * n)), pl.BlockSpec(memory_space=pltpu.VMEM)),
        input_output_aliases={i: 2 * n + i for i in range(2 * n)},
        compiler_params=pltpu.CompilerParams(has_side_effects=pltpu.SideEffectType.DATAFLOW_SIDE_EFFECTING),
    )(*[pltpu.with_memory_space_constraint(a, pltpu.HBM) for a in arrs],
      *[pltpu.with_memory_space_constraint(l, pltpu.HBM) for l in lands])
    return list(res[0:n]), list(res[n:2 * n]), list(res[2 * n:3 * n]), list(res[3 * n:4 * n]), res[-1]


def _xchg_wait(started, gather, after, name):
    send, recv, srcs, lands, _ = started
    n = len(srcs)

    def body(*refs):
        lnd = refs[n:2 * n]
        send, recv = refs[2 * n:3 * n], refs[3 * n:4 * n]
        me = (lax.axis_index("x"), lax.axis_index("y"), lax.axis_index("c"))
        for a in range(n):
            seven = lnd[a].at[pl.ds(0, NDEV - 1)]
            cp = pltpu.make_async_remote_copy(src_ref=seven, dst_ref=seven, send_sem=send[a], recv_sem=recv[a],
                                              device_id=me, device_id_type=pl.DeviceIdType.MESH)
            cp.wait_send()
            cp.wait_recv()

    hbm = pl.BlockSpec(memory_space=pltpu.HBM)
    sem = pl.BlockSpec(memory_space=pltpu.SEMAPHORE)
    res = pl.pallas_call(
        body,
        name=name,
        out_shape=tuple([pltpu.HBM(a.shape, a.dtype) for a in srcs] + [pltpu.HBM(l.shape, l.dtype) for l in lands]),
        in_specs=[hbm] * (2 * n) + [sem] * (2 * n) + [pl.BlockSpec(memory_space=pl.ANY)],
        out_specs=tuple([hbm] * (2 * n)),
        input_output_aliases={i: i for i in range(2 * n)},
        compiler_params=pltpu.CompilerParams(has_side_effects=pltpu.SideEffectType.DATAFLOW_SIDE_EFFECTING),
    )(*srcs, *lands, *send, *recv, after)
    return list(res[n:])


def _inproj(x, g, wp):
    T = x.shape[0]
    tm = _tile(T, 512)

    def body(x_ref, g_ref, w_ref, qkv_ref, kt_ref, vt_ref, us_ref, fz_ref, ab_ref):
        a = _rms_fwd(x_ref[...], g_ref[...]).astype(BF16)
        ab_ref[...] = a
        qkv_ref[:, 0:AW] = _nn(a, w_ref[:, 0:AW]).astype(BF16)
        kk = _nn(a, w_ref[:, AW:2 * AW])
        qkv_ref[:, AW:2 * AW] = kk.astype(BF16)
        kt_ref[...] = kk.T.astype(BF16)
        vv = _nn(a, w_ref[:, 2 * AW:NQKV])
        qkv_ref[:, 2 * AW:NQKV] = vv.astype(BF16)
        vt_ref[...] = vv.T.astype(BF16)
        us_ref[...] = _nn(a, w_ref[:, NQKV:NQKV + NUS])
        fz_ref[...] = _nn(a, w_ref[:, NQKV + NUS:ZP])

    row = lambda n: pl.BlockSpec((tm, n), lambda i: (i, 0))
    col = pl.BlockSpec((AW, tm), lambda i: (0, i))
    return pl.pallas_call(
        body,
        name="inproj",
        grid=(T // tm,),
        in_specs=[row(D), _full((1, D)), _full((D, ZP))],
        out_specs=[row(NQKV), col, col, row(NUS), row(FPAD), row(D)],
        out_shape=[
            jax.ShapeDtypeStruct((T, NQKV), BF16),
            jax.ShapeDtypeStruct((AW, T), BF16),
            jax.ShapeDtypeStruct((AW, T), BF16),
            jax.ShapeDtypeStruct((T, NUS), F32),
            jax.ShapeDtypeStruct((T, FPAD), F32),
            jax.ShapeDtypeStruct((T, D), BF16),
        ],
        compiler_params=_params(1),
    )(x, g, wp)


def _log_sigmoid(z):
    return jnp.minimum(z, 0.0) - jnp.log1p(jnp.exp(-jnp.abs(z)))


def _split3(x):
    hi = x.astype(BF16)
    r1 = x - hi.astype(F32)
    mid = r1.astype(BF16)
    lo = (r1 - mid.astype(F32)).astype(BF16)
    return hi, mid, lo


AUG_A, AUG_B, AUG_ONE = 0, 3, 6


def _aug_lanes(rows, first_one, pieces):
    lane = lax.broadcasted_iota(jnp.int32, (rows, 128), 1)
    out = jnp.zeros((rows, 128), F32)
    if first_one is not None:
        out = jnp.where((lane >= first_one) & (lane < first_one + 3), 1.0, out)
    for n, piece in enumerate(pieces):
        out = jnp.where(lane == AUG_ONE + n, piece.astype(F32), out)
    return out.astype(BF16)


def _fcum(fz, fb, qkv):
    T = fz.shape[0]
    tb = _tile(T, 512)

    def body(fz_ref, fb_ref, k_ref, v_ref, ct_ref, kk_ref, vk_ref, carry):
        @pl.when(pl.program_id(0) == 0)
        def _():
            carry[...] = jnp.zeros_like(carry)

        lf = _log_sigmoid(fz_ref[...] + fb_ref[...])
        r = lax.broadcasted_iota(jnp.int32, (tb, tb), 0)
        cc = lax.broadcasted_iota(jnp.int32, (tb, tb), 1)
        tri = (cc <= r).astype(F32)
        cs = jnp.dot(tri, lf, precision=lax.Precision.HIGHEST, preferred_element_type=F32) + carry[...]
        ct_ref[...] = cs.T[0:8, :]
        carry[...] = carry[...] + jnp.sum(lf, axis=0, keepdims=True)

        pieces = _split3(cs)
        src = lax.broadcasted_iota(jnp.int32, (128, 128), 0)
        dst = lax.broadcasted_iota(jnp.int32, (128, 128), 1)
        ones = jnp.where((dst[0:1, :] >= AUG_ONE) & (dst[0:1, :] < AUG_ONE + 3), 1.0, 0.0)
        for hp in range(4):
            aug = jnp.zeros((tb, 128), F32) + ones
            for n, piece in enumerate(pieces):
                sel = jnp.where(((src == 2 * hp) & (dst == AUG_A + n)) | ((src == 2 * hp + 1) & (dst == AUG_B + n)), -1.0, 0.0)
                aug = aug + _nn(piece, sel.astype(BF16))
            aug = aug.astype(BF16)
            kk_ref[:, hp * 256:hp * 256 + 128] = k_ref[:, hp * 128:(hp + 1) * 128]
            kk_ref[:, hp * 256 + 128:(hp + 1) * 256] = aug
            vk_ref[:, hp * 256:hp * 256 + 128] = v_ref[:, hp * 128:(hp + 1) * 128]
            vk_ref[:, hp * 256 + 128:(hp + 1) * 256] = aug

    row = lambda n, c: pl.BlockSpec((tb, n), lambda i: (i, c))
    return pl.pallas_call(
        body,
        name="fcum",
        grid=(T // tb,),
        in_specs=[row(FPAD, 0), _full((1, FPAD)), row(AW, 1), row(AW, 2)],
        out_specs=[pl.BlockSpec((8, tb), lambda i: (0, i)), row(2 * AW, 0), row(2 * AW, 0)],
        out_shape=[jax.ShapeDtypeStruct((8, T), F32), jax.ShapeDtypeStruct((T, 2 * AW), BF16),
                   jax.ShapeDtypeStruct((T, 2 * AW), BF16)],
        scratch_shapes=[pltpu.VMEM((1, FPAD), F32)],
        compiler_params=_params(1),
    )(fz, fb, qkv, qkv)


def _fold_rows(r, t, nq):
    if nq == 1:
        return r, t
    low = t <= r
    return jnp.where(low, r, nq - 1 - r), jnp.where(low, t, t - r - 1)


def _fold_cols(r, t, nq):
    if nq == 1:
        return r, t
    first = t < nq - r
    j = jnp.where(first, r, nq - 1 - r)
    return j, jnp.where(first, r + t, nq - 1 - r + (t - (nq - r)))


def _fold_grid(nq):
    assert nq == 1 or nq % 2 == 0
    return (4, 1, 1) if nq == 1 else (4, nq // 2, nq + 1)


def _head_rows(x2, hh, scale):
    is_a = lax.broadcasted_iota(jnp.int32, (1, 128), 1) < HD
    keep = is_a if hh == 0 else jnp.logical_not(is_a)
    return jnp.where(keep, x2, jnp.zeros_like(x2)) * scale


def _attn_fwd(qkv, kk, vt, ct):
    T = qkv.shape[0]
    tq = _tile(T, 512)
    tk = tq
    nq = T // tq

    def body(q_ref, kk_ref, vt_ref, ctq_ref, o_ref, lsec_ref, qw_s, m_s, l_s, acc_s):
        hp = pl.program_id(0)
        i, j = _fold_rows(pl.program_id(1), pl.program_id(2), nq)
        sub8 = lax.broadcasted_iota(jnp.int32, (8, 1), 0)

        @pl.when(j == 0)
        def _():
            q2 = q_ref[...]
            for hh in range(2):
                cref = jnp.sum(jnp.where(sub8 == 2 * hp + hh, ctq_ref[:, 0:1], 0.0), axis=0, keepdims=True)
                qw_s[hh * tq:(hh + 1) * tq, 0:128] = _head_rows(q2, hh, 0.125)
                qw_s[hh * tq:(hh + 1) * tq, 128:256] = _aug_lanes(tq, AUG_A if hh == 0 else AUG_B, _split3(cref))
            m_s[...] = jnp.full_like(m_s, MASKV)
            l_s[...] = jnp.zeros_like(l_s)
            acc_s[...] = jnp.zeros_like(acc_s)

        def step(masked):
            s2 = _nt(kk_ref[...], qw_s[...])
            vt2 = vt_ref[...]
            for hh in range(2):
                s = s2[:, hh * tq:(hh + 1) * tq]
                if masked:
                    key = lax.broadcasted_iota(jnp.int32, (tk, tq), 0)
                    qry = lax.broadcasted_iota(jnp.int32, (tk, tq), 1)
                    s = jnp.where(key <= qry, s, MASKV)
                m_prev = m_s[hh]
                m_new = jnp.maximum(m_prev, jnp.max(s, axis=0, keepdims=True))
                pr = jnp.exp(s - m_new)
                alpha = jnp.exp(m_prev - m_new)
                l_s[hh] = alpha * l_s[hh] + jnp.sum(pr, axis=0, keepdims=True)
                m_s[hh] = m_new
                acc_s[hh] = alpha * acc_s[hh] + _nn(vt2, pr.astype(BF16))

        @pl.when(j < i)
        def _():
            step(False)

        @pl.when(j == i)
        def _():
            step(True)
            sub = lax.broadcasted_iota(jnp.int32, (128, 1), 0)
            ot = jnp.where(sub < HD, acc_s[0] * (1.0 / l_s[0]), acc_s[1] * (1.0 / l_s[1]))
            o_ref[...] = ot.T
            lrow = []
            for hh in range(2):
                cref = jnp.sum(jnp.where(sub8 == 2 * hp + hh, ctq_ref[:, 0:1], 0.0), axis=0, keepdims=True)
                lrow.append(m_s[hh] + jnp.log(l_s[hh]) - cref)
            lsec_ref[...] = jnp.where(sub == 0, lrow[0], jnp.where(sub == 1, lrow[1], 0.0)).T

    qi = lambda r, t: _fold_rows(r, t, nq)[0]
    kj = lambda r, t: _fold_rows(r, t, nq)[1]
    return pl.pallas_call(
        body,
        name="attn_fwd",
        grid=_fold_grid(nq),
        in_specs=[
            pl.BlockSpec((tq, 128), lambda hp, r, t: (qi(r, t), hp)),
            pl.BlockSpec((tk, 256), lambda hp, r, t: (kj(r, t), hp)),
            pl.BlockSpec((128, tk), lambda hp, r, t: (hp, kj(r, t))),
            pl.BlockSpec((8, tq), lambda hp, r, t: (0, qi(r, t))),
        ],
        out_specs=[pl.BlockSpec((tq, 128), lambda hp, r, t: (qi(r, t), hp))] * 2,
        out_shape=[jax.ShapeDtypeStruct((T, AW), F32)] * 2,
        scratch_shapes=[pltpu.VMEM((2 * tq, 256), BF16), pltpu.VMEM((2, 1, tq), F32), pltpu.VMEM((2, 1, tq), F32),
                        pltpu.VMEM((2, 128, tq), F32)],
        compiler_params=_params(3),
    )(qkv, kk, vt, ct)


def _sgu_forward(us_ref, lng, lnb, w_ref, bt_ref, mixed_s, vnb_s, tm):
    is_a = lax.broadcasted_iota(jnp.int32, (1, 128), 1) < HD
    u = us_ref[:, 0:SW]
    vs = us_ref[:, SW:NUS]
    ug, tu = _gelu(u)
    vg, tv = _gelu(vs)
    mu = jnp.mean(vg, axis=-1, keepdims=True)
    xc = vg - mu
    rstd = lax.rsqrt(jnp.mean(xc * xc, axis=-1, keepdims=True) + EPS)
    vhat = xc * rstd
    vnb_s[...] = (vhat * lng + lnb).astype(BF16)
    rr = lax.broadcasted_iota(jnp.int32, (CH, CH), 0)
    cc = lax.broadcasted_iota(jnp.int32, (CH, CH), 1)
    tril = cc <= rr
    for jj in range(4):
        wa = jnp.where(tril, w_ref[2 * jj], 0.0).astype(BF16)
        wb = jnp.where(tril, w_ref[2 * jj + 1], 0.0).astype(BF16)
        ba = bt_ref[:, 2 * jj:2 * jj + 1]
        bb = bt_ref[:, 2 * jj + 1:2 * jj + 2]
        for ch in range(tm // CH):
            rs, cs = slice(ch * CH, (ch + 1) * CH), slice(jj * 128, (jj + 1) * 128)
            vn2 = vnb_s[rs, cs]
            mixed_s[rs, cs] = jnp.where(is_a, _nn(wa, vn2) + ba, _nn(wb, vn2) + bb)
    mixed = mixed_s[...]
    return u, vs, ug, tu, tv, vhat, rstd, mixed, ug * mixed


def _sgu_out(us, yatt, x, lng, lnb, sgw, sgbt, gatt, gsg, wout, gpm):
    T = us.shape[0]
    tm = _tile(T, 256)

    def body(us_ref, ya_ref, x_ref, lng_ref, lnb_ref, w_ref, bt_ref, ga_ref, gs_ref, wo_ref, gp_ref,
             h1_ref, yb_ref, o_ref, mixed_s, vnb_s):
        ysg = _sgu_forward(us_ref, lng_ref[...], lnb_ref[...], w_ref, bt_ref, mixed_s, vnb_s, tm)[-1]
        yb_ref[:, 0:AW] = _rms_fwd(ya_ref[...], ga_ref[...]).astype(BF16)
        yb_ref[:, AW:D] = _rms_fwd(ysg, gs_ref[...]).astype(BF16)
        o = _nn(yb_ref[...], wo_ref[...])
        o_ref[...] = o
        h1_ref[...] = x_ref[...] + _rms_fwd(o, gp_ref[...])

    row = lambda n: pl.BlockSpec((tm, n), lambda i: (i, 0))
    return pl.pallas_call(
        body,
        name="sgu_out",
        grid=(T // tm,),
        in_specs=[row(NUS), row(AW), row(D), _full((1, SW)), _full((1, SW)), _full((8, CH, CH)), _full((CH, 8)),
                  _full((1, AW)), _full((1, SW)), _full((D, D)), _full((1, D))],
        out_specs=[row(D), row(D), row(D)],
        out_shape=[jax.ShapeDtypeStruct((T, D), F32), jax.ShapeDtypeStruct((T, D), BF16), jax.ShapeDtypeStruct((T, D), F32)],
        scratch_shapes=[pltpu.VMEM((tm, SW), F32), pltpu.VMEM((tm, SW), BF16)],
        compiler_params=_params(1),
    )(us, yatt, x, lng, lnb, sgw, sgbt, gatt, gsg, wout, gpm)


def _ffn_fwd(h1, gpre, w1g, w2g, gpost):
    T = h1.shape[0]
    tm = _tile(T, 512)
    nb, hb = w1g.shape[0], w1g.shape[2]

    def body(h1_ref, gpre_ref, w1_ref, w2_ref, gpost_ref, c2_ref, f1_ref, ff_ref, h2_ref, h2b_ref):
        h1 = h1_ref[...]
        c2 = _rms_fwd(h1, gpre_ref[...]).astype(BF16)
        c2_ref[...] = c2
        ff = jnp.zeros((tm, D), F32)
        for j in range(nb):
            f1 = _nn(c2, w1_ref[j])
            f1_ref[:, j * hb:(j + 1) * hb] = f1.astype(BF16)
            r = jnp.maximum(f1, 0.0)
            ff = ff + _nn((r * r).astype(BF16), w2_ref[j])
        ff_ref[...] = ff
        h2 = h1 + _rms_fwd(ff, gpost_ref[...])
        h2_ref[...] = h2
        h2b_ref[...] = h2.astype(BF16)

    row = lambda n: pl.BlockSpec((tm, n), lambda i: (i, 0))
    once = lambda shape: pl.BlockSpec(shape, lambda i: (0,) * len(shape), pipeline_mode=pl.Buffered(1))
    return pl.pallas_call(
        body,
        name="ffn_fwd",
        grid=(T // tm,),
        in_specs=[row(D), _full((1, D)), once((nb, D, hb)), once((nb, hb, D)), _full((1, D))],
        out_specs=[row(D), row(DFF), row(D), row(D), row(D)],
        out_shape=[jax.ShapeDtypeStruct((T, D), BF16), jax.ShapeDtypeStruct((T, DFF), BF16),
                   jax.ShapeDtypeStruct((T, D), F32), jax.ShapeDtypeStruct((T, D), F32),
                   jax.ShapeDtypeStruct((T, D), BF16)],
        compiler_params=_params(1, VMEM_LIMIT_BIG),
    )(h1, gpre, w1g, w2g, gpost)


def _ple_loss(h2, p, tgt, wg, bg, wpe):
    T = h2.shape[0]
    tm = _tile(T, 512)

    def body(h2_ref, p_ref, t_ref, wg_ref, bg_ref, wpe_ref, dh2_ref, dpre_ref, dpe_ref, pb_ref, dbg_ref, loss_ref):
        @pl.when(pl.program_id(0) == 0)
        def _():
            dbg_ref[...] = jnp.zeros_like(dbg_ref)
            loss_ref[...] = jnp.zeros_like(loss_ref)

        h2 = h2_ref[...]
        gate = jax.nn.sigmoid(_nn(h2.astype(BF16), wg_ref[...]) + bg_ref[...])
        pb = p_ref[...].astype(BF16)
        pb_ref[...] = pb
        pe = _nn(pb, wpe_ref[...])
        diff = (h2 + gate * pe) - t_ref[...]
        loss_ref[...] += jnp.sum(diff * diff)
        dh3 = diff * (1.0 / D)
        dpre = (dh3 * pe) * (gate * (1.0 - gate))
        dpre_b = dpre.astype(BF16)
        dpre_ref[...] = dpre_b
        dpe_ref[...] = (dh3 * gate).astype(BF16)
        dbg_ref[...] += jnp.sum(dpre, axis=0, keepdims=True)
        dh2_ref[...] = dh3 + _nt(dpre_b, wg_ref[...])

    row = lambda n: pl.BlockSpec((tm, n), lambda i: (i, 0))
    return pl.pallas_call(
        body,
        name="ple_loss",
        grid=(T // tm,),
        in_specs=[row(D), row(PLE), row(D), _full((D, D)), _full((1, D)), _full((PLE, D))],
        out_specs=[row(D), row(D), row(D), row(PLE), _full((1, D)), _full((8, 128))],
        out_shape=[jax.ShapeDtypeStruct((T, D), F32), jax.ShapeDtypeStruct((T, D), BF16),
                   jax.ShapeDtypeStruct((T, D), BF16), jax.ShapeDtypeStruct((T, PLE), BF16),
                   jax.ShapeDtypeStruct((1, D), F32), jax.ShapeDtypeStruct((8, 128), F32)],
        compiler_params=_params(1),
    )(h2, p, tgt, wg, bg, wpe)


def _ffn_bwd(dh2, ff, h1, f1, w1g, w2g, gpost, gpre):
    T = dh2.shape[0]
    tm = _tile(T, 512)
    nb, hb = w1g.shape[0], w1g.shape[2]

    def body(dh2_ref, ff_ref, h1_ref, f1_ref, w1_ref, w2_ref, gpost_ref, gpre_ref,
             dffb_ref, df1_ref, dh1_ref, dgpost_ref, dgpre_ref):
        @pl.when(pl.program_id(0) == 0)
        def _():
            dgpost_ref[...] = jnp.zeros_like(dgpost_ref)
            dgpre_ref[...] = jnp.zeros_like(dgpre_ref)

        dh2 = dh2_ref[...]
        dff, dg = _rms_bwd(dh2, ff_ref[...], gpost_ref[...])
        dffb = dff.astype(BF16)
        dffb_ref[...] = dffb
        dgpost_ref[...] += dg
        dc2 = jnp.zeros((tm, D), F32)
        for j in range(nb):
            cols = slice(j * hb, (j + 1) * hb)
            dact = _nt(dffb, w2_ref[j])
            df1 = (dact * (2.0 * jnp.maximum(f1_ref[:, cols].astype(F32), 0.0))).astype(BF16)
            df1_ref[:, cols] = df1
            dc2 = dc2 + _nt(df1, w1_ref[j])
        dx, dg = _rms_bwd(dc2, h1_ref[...], gpre_ref[...])
        dh1_ref[...] = dh2 + dx
        dgpre_ref[...] += dg

    row = lambda n: pl.BlockSpec((tm, n), lambda i: (i, 0))
    once = lambda shape: pl.BlockSpec(shape, lambda i: (0,) * len(shape), pipeline_mode=pl.Buffered(1))
    return pl.pallas_call(
        body,
        name="ffn_bwd",
        grid=(T // tm,),
        in_specs=[row(D), row(D), row(D), row(DFF), once((nb, D, hb)), once((nb, hb, D)), _full((1, D)), _full((1, D))],
        out_specs=[row(D), row(DFF), row(D), _full((1, D)), _full((1, D))],
        out_shape=[jax.ShapeDtypeStruct((T, D), BF16), jax.ShapeDtypeStruct((T, DFF), BF16),
                   jax.ShapeDtypeStruct((T, D), F32), jax.ShapeDtypeStruct((1, D), F32),
                   jax.ShapeDtypeStruct((1, D), F32)],
        compiler_params=_params(1, VMEM_LIMIT_BIG),
    )(dh2, ff, h1, f1, w1g, w2g, gpost, gpre)


def _mix_bwd(dh1, o, us, yatt, lng, lnb, sgw, sgwt, sgbt, gatt, gsg, wout, gpm):
    T = dh1.shape[0]
    tm = _tile(T, 256)

    def body(dh1_ref, o_ref, us_ref, ya_ref, lng_ref, lnb_ref, w_ref, wt_ref, bt_ref, ga_ref, gs_ref, wo_ref, gp_ref,
             dob_ref, dya_ref, dus_ref, dw_ref, dbt_ref, dlng_ref, dlnb_ref, dga_ref, dgs_ref, dgp_ref,
             mixed_s, vnb_s, dvn_s):
        @pl.when(pl.program_id(0) == 0)
        def _():
            for r in (dw_ref, dbt_ref, dlng_ref, dlnb_ref, dga_ref, dgs_ref, dgp_ref):
                r[...] = jnp.zeros_like(r)

        is_a = lax.broadcasted_iota(jnp.int32, (1, 128), 1) < HD
        lane = lax.broadcasted_iota(jnp.int32, (1, 128), 1)
        do, dg = _rms_bwd(dh1_ref[...], o_ref[...], gp_ref[...])
        dgp_ref[...] += dg
        dob = do.astype(BF16)
        dob_ref[...] = dob
        dy = _nt(dob, wo_ref[...])
        datt, dg = _rms_bwd(dy[:, 0:AW], ya_ref[...], ga_ref[...])
        dga_ref[...] += dg
        dya_ref[...] = datt

        lng = lng_ref[...]
        u, vs, ug, tu, tv, vhat, rstd, mixed, ysg = _sgu_forward(us_ref, lng, lnb_ref[...], w_ref, bt_ref, mixed_s, vnb_s, tm)
        dysg, dg = _rms_bwd(dy[:, AW:D], ysg, gs_ref[...])
        dgs_ref[...] += dg
        dus_ref[:, 0:SW] = ((dysg * mixed) * _gelu_grad(u, tu)).astype(BF16)
        dmix = dysg * ug

        rr = lax.broadcasted_iota(jnp.int32, (CH, CH), 0)
        cc = lax.broadcasted_iota(jnp.int32, (CH, CH), 1)
        tril = cc <= rr
        triu = cc >= rr
        for jj in range(4):
            wta = jnp.where(triu, wt_ref[2 * jj], 0.0).astype(BF16)
            wtb = jnp.where(triu, wt_ref[2 * jj + 1], 0.0).astype(BF16)
            for ch in range(tm // CH):
                rs, cs = slice(ch * CH, (ch + 1) * CH), slice(jj * 128, (jj + 1) * 128)
                dm2 = dmix[rs, cs]
                dma = jnp.where(is_a, dm2, 0.0)
                dmb = jnp.where(is_a, 0.0, dm2)
                dma_b, dmb_b = dma.astype(BF16), dmb.astype(BF16)
                vn2 = vnb_s[rs, cs]
                dw_ref[2 * jj] += jnp.where(tril, _nt(dma_b, vn2), 0.0)
                dw_ref[2 * jj + 1] += jnp.where(tril, _nt(dmb_b, vn2), 0.0)
                dvn_s[rs, cs] = _nn(wta, dma_b) + _nn(wtb, dmb_b)
                dba = jnp.sum(dma, axis=1, keepdims=True)
                dbb = jnp.sum(dmb, axis=1, keepdims=True)
                dbt_ref[...] += jnp.where(lane == 2 * jj, dba, 0.0) + jnp.where(lane == 2 * jj + 1, dbb, 0.0)

        dvn = dvn_s[...]
        dlng_ref[...] += jnp.sum(dvn * vhat, axis=0, keepdims=True)
        dlnb_ref[...] += jnp.sum(dvn, axis=0, keepdims=True)
        dvh = dvn * lng
        dvg = rstd * (dvh - jnp.mean(dvh, axis=-1, keepdims=True) - vhat * jnp.mean(dvh * vhat, axis=-1, keepdims=True))
        dus_ref[:, SW:NUS] = (dvg * _gelu_grad(vs, tv)).astype(BF16)

    row = lambda n: pl.BlockSpec((tm, n), lambda i: (i, 0))
    return pl.pallas_call(
        body,
        name="mix_bwd",
        grid=(T // tm,),
        in_specs=[row(D), row(D), row(NUS), row(AW), _full((1, SW)), _full((1, SW)), _full((8, CH, CH)), _full((8, CH, CH)),
                  _full((CH, 8)), _full((1, AW)), _full((1, SW)), _full((D, D)), _full((1, D))],
        out_specs=[row(D), row(AW), row(NUS), _full((8, CH, CH)), _full((CH, 128)), _full((1, SW)), _full((1, SW)),
                   _full((1, AW)), _full((1, SW)), _full((1, D))],
        out_shape=[jax.ShapeDtypeStruct((T, D), BF16), jax.ShapeDtypeStruct((T, AW), F32), jax.ShapeDtypeStruct((T, NUS), BF16),
                   jax.ShapeDtypeStruct((8, CH, CH), F32), jax.ShapeDtypeStruct((CH, 128), F32),
                   jax.ShapeDtypeStruct((1, SW), F32), jax.ShapeDtypeStruct((1, SW), F32),
                   jax.ShapeDtypeStruct((1, AW), F32), jax.ShapeDtypeStruct((1, SW), F32), jax.ShapeDtypeStruct((1, D), F32)],
        scratch_shapes=[pltpu.VMEM((tm, SW), F32), pltpu.VMEM((tm, SW), BF16), pltpu.VMEM((tm, SW), F32)],
        compiler_params=_params(1),
    )(dh1, o, us, yatt, lng, lnb, sgw, sgwt, sgbt, gatt, gsg, wout, gpm)


def _attn_prep(qkv, dya, yatt, lsec, after):
    T = dya.shape[0]
    tm = _tile(T, 512)

    def body(q_ref, do_ref, o_ref, l_ref, after_ref, qw_ref, dow_ref):
        do = do_ref[...]
        feat = lax.broadcasted_iota(jnp.int32, (AW, 128), 0)
        head = lax.broadcasted_iota(jnp.int32, (AW, 128), 1)
        sel = jnp.where((feat >= head * HD) & (feat < (head + 1) * HD), 1.0, 0.0)
        delta = jnp.dot(do * o_ref[...], sel, precision=lax.Precision.HIGHEST, preferred_element_type=F32)
        for hp in range(4):
            cols = slice(hp * 128, (hp + 1) * 128)
            for hh in range(2):
                base = (2 * hp + hh) * 256
                lc = l_ref[:, hp * 128 + hh:hp * 128 + hh + 1]
                d_h = delta[:, 2 * hp + hh:2 * hp + hh + 1]
                qw_ref[:, base:base + 128] = _head_rows(q_ref[:, cols], hh, 0.125)
                qw_ref[:, base + 128:base + 256] = _aug_lanes(tm, AUG_A if hh == 0 else AUG_B, _split3(-lc))
                dow_ref[:, base:base + 128] = _head_rows(do[:, cols], hh, 1.0).astype(BF16)
                dow_ref[:, base + 128:base + 256] = _aug_lanes(tm, None, _split3(-d_h))

    row = lambda n: pl.BlockSpec((tm, n), lambda i: (i, 0))
    return pl.pallas_call(
        body,
        name="attn_prep",
        grid=(T // tm,),
        in_specs=[row(AW)] * 4 + [_full(after.shape)],
        out_specs=[row(4 * AW)] * 2,
        out_shape=[jax.ShapeDtypeStruct((T, 4 * AW), BF16)] * 2,
        compiler_params=_params(1),
    )(qkv, dya, yatt, lsec, after)


def _attn_bwd(kk, vk, kt, qw, dow):
    T = kk.shape[0]
    tq = _tile(T, 512)
    tk = tq
    nq = T // tq

    def body(kk_ref, vk_ref, kt_ref, qw_ref, dow_ref, dqt_ref, dcq_ref, dk_ref, dv_ref, dck_ref, dk_s, dv_s, dck_s):
        j, i = _fold_cols(pl.program_id(1), pl.program_id(2), nq)
        sub8 = lax.broadcasted_iota(jnp.int32, (8, 1), 0)
        lane = lax.broadcasted_iota(jnp.int32, (1, 128), 1)

        @pl.when((pl.program_id(1) == 0) & (pl.program_id(2) == 0))
        def _():
            dqt_ref[...] = jnp.zeros_like(dqt_ref)
            dcq_ref[...] = jnp.zeros_like(dcq_ref)

        @pl.when(i == j)
        def _():
            dk_s[...] = jnp.zeros_like(dk_s)
            dv_s[...] = jnp.zeros_like(dv_s)
            dck_s[...] = jnp.zeros_like(dck_s)

        def step(masked):
            cols = pl.ds(pl.multiple_of(i * tq, tq), tq)
            kk2, vk2 = kk_ref[...], vk_ref[...]
            kt2 = kt_ref[...] * 0.125
            dcq = jnp.zeros((8, tq), F32)
            dck = jnp.zeros((tk, 128), F32)
            dv = jnp.zeros((tk, 128), F32)
            dk = jnp.zeros((tk, 128), F32)
            dqts = []
            for hh in range(2):
                qw_h = qw_ref[:, hh * 256:(hh + 1) * 256]
                dow_h = dow_ref[:, hh * 256:(hh + 1) * 256]
                pr = jnp.exp(_nt(kk2, qw_h))
                if masked:
                    key = lax.broadcasted_iota(jnp.int32, (tk, tq), 0)
                    qry = lax.broadcasted_iota(jnp.int32, (tk, tq), 1)
                    pr = jnp.where(key <= qry, pr, 0.0)
                ds = pr * _nt(vk2, dow_h)
                ds_b = ds.astype(BF16)
                dv = dv + _nn(pr.astype(BF16), dow_h[:, 0:128])
                dk = dk + _nn(ds_b, qw_h[:, 0:128])
                dqts.append(_nn(kt2, ds_b))
                dcq = dcq + jnp.where(sub8 == hh, jnp.sum(ds, axis=0, keepdims=True), 0.0)
                dck = dck - jnp.where(lane == hh, jnp.sum(ds, axis=1, keepdims=True), 0.0)
            dv_s[...] += dv
            dk_s[...] += dk
            dck_s[...] += dck
            dcq_ref[:, cols] += dcq
            sub = lax.broadcasted_iota(jnp.int32, (128, 1), 0)
            dqt_ref[:, cols] += jnp.where(sub < HD, dqts[0], dqts[1])

        @pl.when(i > j)
        def _():
            step(False)

        @pl.when(i == j)
        def _():
            step(True)

        @pl.when(i == nq - 1)
        def _():
            dk_ref[...] = dk_s[...].astype(BF16)
            dv_ref[...] = dv_s[...].astype(BF16)
            dck_ref[...] = dck_s[...]

    kj = lambda r, t: _fold_cols(r, t, nq)[0]
    qi = lambda r, t: _fold_cols(r, t, nq)[1]
    krow = lambda hp, r, t: (kj(r, t), hp)
    qrow = lambda hp, r, t: (qi(r, t), hp)
    return pl.pallas_call(
        body,
        name="attn_bwd",
        grid=_fold_grid(nq),
        in_specs=[
            pl.BlockSpec((tk, 256), krow),
            pl.BlockSpec((tk, 256), krow),
            pl.BlockSpec((128, tk), lambda hp, r, t: (hp, kj(r, t))),
            pl.BlockSpec((tq, 512), qrow),
            pl.BlockSpec((tq, 512), qrow),
        ],
        out_specs=[
            pl.BlockSpec((128, T), lambda hp, r, t: (hp, 0)),
            pl.BlockSpec((None, 8, T), lambda hp, r, t: (hp, 0, 0)),
            pl.BlockSpec((tk, 128), krow),
            pl.BlockSpec((tk, 128), krow),
            pl.BlockSpec((tk, 128), krow),
        ],
        out_shape=[jax.ShapeDtypeStruct((AW, T), F32), jax.ShapeDtypeStruct((4, 8, T), F32),
                   jax.ShapeDtypeStruct((T, AW), BF16), jax.ShapeDtypeStruct((T, AW), BF16),
                   jax.ShapeDtypeStruct((T, AW), F32)],
        scratch_shapes=[pltpu.VMEM((tk, 128), F32), pltpu.VMEM((tk, 128), F32), pltpu.VMEM((tk, 128), F32)],
        compiler_params=_params(3),
    )(kk, vk, kt, qw, dow)


def _fgate_bwd(dcq, dck, fz, fb):
    T = dck.shape[0]
    tb = _tile(T, 512)
    nb = T // tb

    def body(dcq_ref, dck_ref, fz_ref, fb_ref, df_ref, dfb_ref, carry):
        @pl.when(pl.program_id(0) == 0)
        def _():
            carry[...] = jnp.zeros_like(carry)
            dfb_ref[...] = jnp.zeros_like(dfb_ref)

        dcv = dcq_ref[...] + dck_ref[...]
        r = lax.broadcasted_iota(jnp.int32, (tb, tb), 0)
        cc = lax.broadcasted_iota(jnp.int32, (tb, tb), 1)
        tri = (cc >= r).astype(F32)
        dlf = jnp.dot(tri, dcv, precision=lax.Precision.HIGHEST, preferred_element_type=F32) + carry[...]
        carry[...] = carry[...] + jnp.sum(dcv, axis=0, keepdims=True)
        lane = lax.broadcasted_iota(jnp.int32, (tb, FPAD), 1)
        df = jnp.where(lane < 8, dlf * jax.nn.sigmoid(-(fz_ref[...] + fb_ref[...])), 0.0)
        df_ref[...] = df.astype(BF16)
        dfb_ref[...] += jnp.sum(df, axis=0, keepdims=True)

    rev = pl.BlockSpec((tb, FPAD), lambda i: (nb - 1 - i, 0))
    return pl.pallas_call(
        body,
        name="fgate_bwd",
        grid=(nb,),
        in_specs=[rev, rev, rev, _full((1, FPAD))],
        out_specs=[rev, _full((1, FPAD))],
        out_shape=[jax.ShapeDtypeStruct((T, FPAD), BF16), jax.ShapeDtypeStruct((1, FPAD), F32)],
        scratch_shapes=[pltpu.VMEM((1, FPAD), F32)],
        compiler_params=_params(1),
    )(dcq, dck, fz, fb)


def _inproj_bwd(dq, dk, dv, dus, dfz, wp, x, dh1, g):
    T = x.shape[0]
    tm = _tile(T, 512)

    def body(dq_ref, dk_ref, dv_ref, dus_ref, dfz_ref, w_ref, x_ref, dh1_ref, g_ref, gx_ref, dg_ref):
        @pl.when(pl.program_id(0) == 0)
        def _():
            dg_ref[...] = jnp.zeros_like(dg_ref)

        da = _nt(dq_ref[...].astype(BF16), w_ref[:, 0:AW])
        da += _nt(dk_ref[...], w_ref[:, AW:2 * AW])
        da += _nt(dv_ref[...], w_ref[:, 2 * AW:NQKV])
        da += _nt(dus_ref[...], w_ref[:, NQKV:NQKV + NUS])
        da += _nt(dfz_ref[...], w_ref[:, NQKV + NUS:ZP])
        dx, dg = _rms_bwd(da, x_ref[...], g_ref[...])
        gx_ref[...] = dh1_ref[...] + dx
        dg_ref[...] += dg

    row = lambda n: pl.BlockSpec((tm, n), lambda i: (i, 0))
    return pl.pallas_call(
        body,
        name="inproj_bwd",
        grid=(T // tm,),
        in_specs=[row(AW), row(AW), row(AW), row(NUS), row(FPAD), _full((D, ZP)), row(D), row(D), _full((1, D))],
        out_specs=[row(D), _full((1, D))],
        out_shape=[jax.ShapeDtypeStruct((T, D), F32), jax.ShapeDtypeStruct((1, D), F32)],
        compiler_params=_params(1),
    )(dq, dk, dv, dus, dfz, wp, x, dh1, g)


def _sq_relu(f1):
    r = jnp.maximum(f1.astype(F32), 0.0)
    return (r * r).astype(BF16)


def _wgrad(a, bs, name, a_fn=None):
    T, K = a.shape
    tt = _tile(T, 512)
    nb = len(bs)

    def out_dims(b, layout):
        N = b.shape[1]
        if layout is None:
            return (K, N)
        return (N // layout[1], K, layout[1]) if layout[0] == "col" else (K // layout[1], layout[1], N)

    shapes = [out_dims(b, layout) for b, layout in bs]

    def body(*refs):
        a_ref, b_refs, o_refs = refs[0], refs[1:1 + nb], refs[1 + nb:]

        @pl.when(pl.program_id(0) == 0)
        def _():
            for o_ref in o_refs:
                o_ref[...] = jnp.zeros_like(o_ref)

        av = a_ref[...] if a_fn is None else a_fn(a_ref[...])
        at = av.astype(BF16).T
        for (b, layout), b_ref, o_ref in zip(bs, b_refs, o_refs):
            if layout is None:
                o_ref[...] += _nn(at, b_ref[...].astype(BF16))
            elif layout[0] == "col":
                n = layout[1]
                for k in range(b.shape[1] // n):
                    o_ref[k] += _nn(at, b_ref[:, k * n:(k + 1) * n].astype(BF16))
            else:
                n = layout[1]
                bv = b_ref[...].astype(BF16)
                for k in range(K // n):
                    o_ref[k] += _nn(at[k * n:(k + 1) * n, :], bv)

    once = lambda shape: pl.BlockSpec(shape, lambda t: (0,) * len(shape), pipeline_mode=pl.Buffered(1))
    res = pl.pallas_call(
        body,
        name=name,
        grid=(T // tt,),
        in_specs=[pl.BlockSpec((tt, K), lambda t: (t, 0))] + [pl.BlockSpec((tt, b.shape[1]), lambda t: (t, 0)) for b, _ in bs],
        out_specs=[once(s) for s in shapes],
        out_shape=[jax.ShapeDtypeStruct(s, F32) for s in shapes],
        compiler_params=_params(1, VMEM_LIMIT_BIG),
    )(a, *[b for b, _ in bs])
    return res


def _adam_math(w, g, m, v):
    m = ADAM_B1 * m + (1.0 - ADAM_B1) * g
    v = ADAM_B2 * v + (1.0 - ADAM_B2) * (g * g)
    m_hat = m / (1.0 - ADAM_B1 ** ADAM_STEP)
    v_hat = v / (1.0 - ADAM_B2 ** ADAM_STEP)
    delta = -ADAM_LR * (m_hat / (jnp.sqrt(v_hat) + ADAM_EPS) + ADAM_WD * w)
    return delta, m, v


def _adam(parts, w, m, v, name):
    R, C = w.shape
    br = 128 if R % 128 == 0 else R

    def body(p_ref, w_ref, m_ref, v_ref, g_ref, d_ref, nm_ref, nv_ref):
        g = p_ref[0]
        for s in range(1, NDEV):
            g = g + p_ref[s]
        g_ref[...] = g
        d_ref[...], nm_ref[...], nv_ref[...] = _adam_math(w_ref[...], g, m_ref[...], v_ref[...])

    blk = pl.BlockSpec((br, C), lambda i: (i, 0))
    return pl.pallas_call(
        body,
        name=name,
        grid=(R // br,),
        in_specs=[pl.BlockSpec((NDEV, br, C), lambda i: (0, i, 0)), blk, blk, blk],
        out_specs=[blk] * 4,
        out_shape=[jax.ShapeDtypeStruct((R, C), F32)] * 4,
        compiler_params=_params(1),
    )(parts, w, m, v)


_SMALL = (("sg_w", 8 * CH * CH), ("f_bias", 8), ("sg_ln_g", SW), ("sg_ln_b", SW), ("sg_b", 8 * CH), ("att_out_g", AW),
          ("sg_out_g", SW), ("pre_mix_g", D), ("post_mix_g", D), ("pre_ffn_g", D), ("post_ffn_g", D), ("ple_gate_b", D))
_SEG = 8 * 128


def _seg_rows(size):
    return 8 * (-(-size // _SEG))


def _pack(vals):
    parts = []
    for name, size in _SMALL:
        flat = vals[name].reshape(-1)
        rows = _seg_rows(size)
        parts.append(jnp.pad(flat, (0, rows * 128 - size)).reshape(rows, 128))
    return jnp.concatenate(parts, axis=0)


def _unpack(packed, shapes):
    out, r = {}, 0
    for name, size in _SMALL:
        rows = _seg_rows(size)
        out[name] = packed[r:r + rows].reshape(-1)[:size].reshape(shapes[name])
        r += rows
    return out


def kernel(x, p, w_in, f_bias, sg_ln_g, sg_ln_b, sg_w, sg_b, att_out_g, sg_out_g, w_out, pre_mix_g, post_mix_g, pre_ffn_g, post_ffn_g, w_ff1, w_ff2, ple_w, ple_gate_w, ple_gate_b, loss_target, m_w_in, m_f_bias, m_sg_ln_g, m_sg_ln_b, m_sg_w, m_sg_b, m_att_out_g, m_sg_out_g, m_w_out, m_pre_mix_g, m_post_mix_g, m_pre_ffn_g, m_post_ffn_g, m_w_ff1, m_w_ff2, m_ple_w, m_ple_gate_w, m_ple_gate_b, v_w_in, v_f_bias, v_sg_ln_g, v_sg_ln_b, v_sg_w, v_sg_b, v_att_out_g, v_sg_out_g, v_w_out, v_pre_mix_g, v_post_mix_g, v_pre_ffn_g, v_post_ffn_g, v_w_ff1, v_w_ff2, v_ple_w, v_ple_gate_w, v_ple_gate_b):
    small_w = dict(sg_w=sg_w, f_bias=f_bias, sg_ln_g=sg_ln_g, sg_ln_b=sg_ln_b, sg_b=sg_b, att_out_g=att_out_g,
                   sg_out_g=sg_out_g, pre_mix_g=pre_mix_g, post_mix_g=post_mix_g, pre_ffn_g=pre_ffn_g,
                   post_ffn_g=post_ffn_g, ple_gate_b=ple_gate_b)
    small_m = dict(sg_w=m_sg_w, f_bias=m_f_bias, sg_ln_g=m_sg_ln_g, sg_ln_b=m_sg_ln_b, sg_b=m_sg_b, att_out_g=m_att_out_g,
                   sg_out_g=m_sg_out_g, pre_mix_g=m_pre_mix_g, post_mix_g=m_post_mix_g, pre_ffn_g=m_pre_ffn_g,
                   post_ffn_g=m_post_ffn_g, ple_gate_b=m_ple_gate_b)
    small_v = dict(sg_w=v_sg_w, f_bias=v_f_bias, sg_ln_g=v_sg_ln_g, sg_ln_b=v_sg_ln_b, sg_b=v_sg_b, att_out_g=v_att_out_g,
                   sg_out_g=v_sg_out_g, pre_mix_g=v_pre_mix_g, post_mix_g=v_post_mix_g, pre_ffn_g=v_pre_ffn_g,
                   post_ffn_g=v_post_ffn_g, ple_gate_b=v_ple_gate_b)
    big = dict(w_in=(w_in, m_w_in, v_w_in), w_out=(w_out, m_w_out, v_w_out), w_ff1=(w_ff1, m_w_ff1, v_w_ff1),
               w_ff2=(w_ff2, m_w_ff2, v_w_ff2), ple_w=(ple_w, m_ple_w, v_ple_w),
               ple_gate_w=(ple_gate_w, m_ple_gate_w, v_ple_gate_w))

    xt, pt, tgt = x[0], p[0, 0], loss_target[0]
    ws = W_IN_COLS // NDEV

    (gw_in,) = _xchg([w_in[0].astype(BF16)], True, "gather_w_in")
    rest = _xchg_start([w_out[0].astype(BF16), w_ff1[0].astype(BF16), w_ff2[0].astype(BF16), ple_w[0].astype(BF16),
                        ple_gate_w[0].astype(BF16)], True, "gather_rest_start")
    win = jnp.transpose(gw_in, (1, 0, 2)).reshape(D, W_IN_COLS)
    wp = jnp.concatenate([win[:, 0:NQKV], win[:, NQKV + 8:W_IN_COLS], win[:, NQKV:NQKV + 8],
                          jnp.zeros((D, FPAD - 8), BF16)], axis=1)

    fb = jnp.pad(f_bias.astype(F32), ((0, 0), (0, FPAD - 8)))
    sgw = sg_w[0]
    sgwt = jnp.transpose(sg_w[0], (0, 2, 1))
    sgbt = jnp.transpose(sg_b[0])

    qkv, kt, vt, us, fz, ab = _inproj(xt, pre_mix_g + rest[4][0, 0], wp)
    ct, kk, vk = _fcum(fz, fb, qkv)
    yatt, lsec = _attn_fwd(qkv, kk, vt, ct)
    gw_out, gw1, gw2, gwpe, gwg = _xchg_wait(rest, True, yatt, "gather_rest_wait")
    wout = gw_out.reshape(D, D)
    wg = gwg.reshape(D, D)
    wpe = jnp.transpose(gwpe, (1, 0, 2)).reshape(PLE, D)
    h1, yb, o = _sgu_out(us, yatt, xt, sg_ln_g, sg_ln_b, sgw, sgbt, att_out_g, sg_out_g, wout, post_mix_g)
    c2b, f1b, ff, h2, h2b = _ffn_fwd(h1, pre_ffn_g, gw1, gw2, post_ffn_g)
    dh2, dpre, dpe, pb, dbg, loss_acc = _ple_loss(h2, pt, tgt, wg, ple_gate_b, wpe)
    loss = lax.psum(loss_acc[0, 0] * (0.5 / D), ("x", "y", "c"))

    (g_pe,) = _wgrad(pb, [(dpe, ("col", D // NDEV))], "wgrad_ple")
    (g_g,) = _wgrad(h2b, [(dpre, ("row", D // NDEV))], "wgrad_gate")
    dffb, df1, dh1, dgpostffn, dgpreffn = _ffn_bwd(dh2, ff, h1, f1b, gw1, gw2, post_ffn_g, pre_ffn_g)
    (g_1,) = _wgrad(c2b, [(df1, ("col", DFF // NDEV))], "wgrad_ff1")
    (g_2,) = _wgrad(f1b, [(dffb, ("row", DFF // NDEV))], "wgrad_ff2", a_fn=_sq_relu)
    early = _xchg_start([g_1, g_2, g_pe, g_g], False, "scatter_early_start")
    dob, dya, dus, dsgw, dsgbt, dlng, dlnb, dgatt, dgsg, dgpostmix = _mix_bwd(
        dh1, o, us, yatt, sg_ln_g, sg_ln_b, sgw, sgwt, sgbt, att_out_g, sg_out_g, wout, post_mix_g + early[4][0, 0])
    (g_out,) = _wgrad(yb, [(dob, ("row", D // NDEV))], "wgrad_out")
    mid = _xchg_start([g_out], False, "scatter_mid_start")
    qw, dow = _attn_prep(qkv, dya, yatt, lsec, mid[4])
    dqt, dcq, dk, dv, dck = _attn_bwd(kk, vk, kt, qw, dow)
    dq = jnp.transpose(dqt)
    dcq = jnp.pad(jnp.transpose(dcq[:, 0:2, :].reshape(8, -1)), ((0, 0), (0, FPAD - 8)))
    dck = jnp.pad(dck.reshape(-1, 4, 128)[:, :, 0:2].reshape(-1, 8), ((0, 0), (0, FPAD - 8)))
    dfz, dfb = _fgate_bwd(dcq, dck, fz, fb)

    gq, gk, gv, gus, gf = _wgrad(ab, [(dq, None), (dk, None), (dv, None), (dus, None), (dfz, None)], "wgrad_in")
    g_in = jnp.concatenate([gq, gk, gv, gf[:, 0:8], gus], axis=1)
    g_in = jnp.transpose(g_in.reshape(D, NDEV, ws), (1, 0, 2))
    late = _xchg_start([g_in], False, "scatter_late_start")
    grad_x, dgpremix = _inproj_bwd(dq, dk, dv, dus, dfz, wp, xt, dh1, pre_mix_g + late[4][0, 0])

    small_g = dict(sg_w=dsgw, f_bias=dfb[:, 0:8], sg_ln_g=dlng, sg_ln_b=dlnb, sg_b=jnp.transpose(dsgbt[:, 0:8]),
                   att_out_g=dgatt, sg_out_g=dgsg, pre_mix_g=dgpremix, post_mix_g=dgpostmix, pre_ffn_g=dgpreffn,
                   post_ffn_g=dgpostffn, ple_gate_b=dbg)

    r_1, r_2, r_pe, r_g = _xchg_wait(early, False, grad_x, "scatter_early_wait")
    (r_out,) = _xchg_wait(mid, False, grad_x, "scatter_mid_wait")
    (r_small,) = _xchg([_pack(small_g)], True, "gather_small_grads")

    res = {}

    def adam_big(name, parts):
        w, m, v = big[name]
        res[name] = [t[None] for t in _adam(parts, w[0], m[0], v[0], "adam_" + name)]

    for name, parts in (("w_out", r_out), ("w_ff1", r_1), ("w_ff2", r_2), ("ple_w", r_pe), ("ple_gate_w", r_g)):
        adam_big(name, parts)
    (r_in,) = _xchg_wait(late, False, res["w_ff1"][0], "scatter_late_wait")
    adam_big("w_in", r_in)
    shapes = {k: a.shape for k, a in small_w.items()}
    small = [_unpack(t, shapes) for t in _adam(r_small, _pack(small_w), _pack(small_m), _pack(small_v), "adam_small")]
    for name, _ in _SMALL:
        res[name] = [s[name] for s in small]

    order = ["w_in", "f_bias", "sg_ln_g", "sg_ln_b", "sg_w", "sg_b", "att_out_g", "sg_out_g", "w_out", "pre_mix_g",
             "post_mix_g", "pre_ffn_g", "post_ffn_g", "w_ff1", "w_ff2", "ple_w", "ple_gate_w", "ple_gate_b"]
    outs = [loss, grad_x[None]]
    for kind in range(4):
        outs += [res[name][kind] for name in order]
    return tuple(outs)
```

```python
import jax
import jax.numpy as jnp
from jax import lax
from jax.experimental import pallas as pl
from jax.experimental.pallas import tpu as pltpu

F32 = jnp.float32
BF16 = jnp.bfloat16

NDEV = 8
D = 1024
AW = 512
SW = 512
HD = 64
CH = 128
DFF = 4096
PLE = 256
NQKV = 3 * AW
NUS = 2 * SW
FPAD = 128
ZP = NQKV + NUS + FPAD
W_IN_COLS = 2568
EPS = 1e-6
MASKV = -1e30
GELU_K = 0.7978845608028654
GELU_C = 0.044715

ADAM_LR = 0.001
ADAM_B1 = 0.9
ADAM_B2 = 0.999
ADAM_EPS = 1e-08
ADAM_WD = 0.01
ADAM_STEP = 10

VMEM_LIMIT = 48 * 1024 * 1024
VMEM_LIMIT_BIG = 60 * 1024 * 1024


def _nn(a, b):
    return jnp.dot(a, b, preferred_element_type=F32)


def _nt(a, b):
    return lax.dot_general(a, b, (((1,), (1,)), ((), ())), preferred_element_type=F32)


def _tn(a, b):
    return lax.dot_general(a, b, (((0,), (0,)), ((), ())), preferred_element_type=F32)


def _tile(n, pref):
    return min(n, pref)


def _params(n_axes, vmem=VMEM_LIMIT):
    return pltpu.CompilerParams(dimension_semantics=("arbitrary",) * n_axes, vmem_limit_bytes=vmem)


def _full(shape):
    nd = len(shape)
    return pl.BlockSpec(shape, lambda *_: (0,) * nd)


def _rms_fwd(x, g):
    r = lax.rsqrt(jnp.mean(x * x, axis=-1, keepdims=True) + EPS)
    return x * r * g


def _rms_bwd(dy, x, g):
    n = x.shape[-1]
    r = lax.rsqrt(jnp.mean(x * x, axis=-1, keepdims=True) + EPS)
    u = dy * g
    s = jnp.sum(x * u, axis=-1, keepdims=True)
    dx = r * u - x * (r * r * r * (s * (1.0 / n)))
    dg = jnp.sum(dy * (x * r), axis=0, keepdims=True)
    return dx, dg


def _gelu(x):
    t = jnp.tanh(GELU_K * (x + GELU_C * (x * x * x)))
    return x * (0.5 * (1.0 + t)), t


def _gelu_grad(x, t):
    return 0.5 * (1.0 + t) + 0.5 * x * (1.0 - t * t) * (GELU_K * (1.0 + 3.0 * GELU_C * x * x))


def _xchg(arrs, gather, name):
    n = len(arrs)
    out_shapes = [jax.ShapeDtypeStruct(((NDEV,) + a.shape) if gather else a.shape, a.dtype) for a in arrs]

    def body(*refs):
        ins, outs = refs[:n], refs[n:2 * n]
        send, recv, loc = refs[2 * n:]
        x, y, c = lax.axis_index("x"), lax.axis_index("y"), lax.axis_index("c")
        me = 4 * x + 2 * y + c
        local = []
        for a in range(n):
            cp = pltpu.make_async_copy(ins[a] if gather else ins[a].at[me], outs[a].at[me], loc.at[a])
            cp.start()
            local.append(cp)
        remote = []
        for k in range(1, NDEV):
            px = 1 - x if (k >> 2) & 1 else x
            py = 1 - y if (k >> 1) & 1 else y
            pc = 1 - c if k & 1 else c
            peer = 4 * px + 2 * py + pc
            for a in range(n):
                cp = pltpu.make_async_remote_copy(
                    src_ref=ins[a] if gather else ins[a].at[peer],
                    dst_ref=outs[a].at[me],
                    send_sem=send.at[a, k - 1],
                    recv_sem=recv.at[a, k - 1],
                    device_id=(px, py, pc),
                    device_id_type=pl.DeviceIdType.MESH,
                )
                cp.start()
                remote.append(cp)
        for cp in remote:
            cp.wait_send()
            cp.wait_recv()
        for cp in local:
            cp.wait()

    hbm = pl.BlockSpec(memory_space=pltpu.HBM)
    return pl.pallas_call(
        body,
        name=name,
        out_shape=out_shapes,
        in_specs=[hbm] * n,
        out_specs=[hbm] * n,
        scratch_shapes=[
            pltpu.SemaphoreType.DMA((n, NDEV - 1)),
            pltpu.SemaphoreType.DMA((n, NDEV - 1)),
            pltpu.SemaphoreType.DMA((n,)),
        ],
    )(*arrs)


def _peers(x, y, c):
    out = []
    for k in range(1, NDEV):
        out.append((1 - x if (k >> 2) & 1 else x, 1 - y if (k >> 1) & 1 else y, 1 - c if k & 1 else c))
    return out


def _xchg_start(arrs, gather, name):
    n = len(arrs)
    me = 4 * lax.axis_index("x") + 2 * lax.axis_index("y") + lax.axis_index("c")
    lands = []
    for a in arrs:
        shape = ((NDEV,) + a.shape) if gather else a.shape
        own = a[None] if gather else lax.dynamic_slice_in_dim(a, me, 1, axis=0)
        lands.append(lax.dynamic_update_slice_in_dim(lax.empty(shape, a.dtype), own, me, axis=0))

    def body(*refs):
        ins, lnd = refs[:n], refs[n:2 * n]
        send, recv, token = refs[2 * n:3 * n], refs[3 * n:4 * n], refs[-1]
        x, y, c = lax.axis_index("x"), lax.axis_index("y"), lax.axis_index("c")
        mine = 4 * x + 2 * y + c
        for px, py, pc in _peers(x, y, c):
            peer = 4 * px + 2 * py + pc
            for a in range(n):
                pltpu.make_async_remote_copy(
                    src_ref=ins[a] if gather else ins[a].at[peer],
                    dst_ref=lnd[a].at[mine],
                    send_sem=send[a],
                    recv_sem=recv[a],
                    device_id=(px, py, pc),
                    device_id_type=pl.DeviceIdType.MESH,
                ).start()
        token[...] = jnp.zeros_like(token)

    hbm = pl.BlockSpec(memory_space=pltpu.HBM)
    sem = pl.BlockSpec(memory_space=pltpu.SEMAPHORE)
    res = pl.pallas_call(
        body,
        name=name,
        out_shape=(*[pltpu.SemaphoreType.DMA(())] * (2 * n),
                   *[pltpu.HBM(a.shape, a.dtype) for a in arrs], *[pltpu.HBM(l.shape, l.dtype) for l in lands],
                   jax.ShapeDtypeStruct((8, 128), F32)),
        in_specs=[hbm] * (2 * n),
        out_specs=(*([sem] * (2 * n)), *([hbm] * (2 * n)), pl.BlockSpec(memory_space=pltpu.VMEM)),
        input_output_aliases={i: 2 * n + i for i in range(2 * n)},
        compiler_params=pltpu.CompilerParams(has_side_effects=pltpu.SideEffectType.DATAFLOW_SIDE_EFFECTING),
    )(*[pltpu.with_memory_space_constraint(a, pltpu.HBM) for a in arrs],
      *[pltpu.with_memory_space_constraint(l, pltpu.HBM) for l in lands])
    return list(res[0:n]), list(res[n:2 * n]), list(res[2 * n:3 * n]), list(res[3 * n:4 * n]), res[-1]


def _xchg_wait(started, gather, after, name):
    send, recv, srcs, lands, _ = started
    n = len(srcs)

    def body(*refs):
        lnd = refs[n:2 * n]
        send, recv = refs[2 * n:3 * n], refs[3 * n:4 * n]
        me = (lax.axis_index("x"), lax.axis_index("y"), lax.axis_index("c"))
        for a in range(n):
            seven = lnd[a].at[pl.ds(0, NDEV - 1)]
            cp = pltpu.make_async_remote_copy(src_ref=seven, dst_ref=seven, send_sem=send[a], recv_sem=recv[a],
                                              device_id=me, device_id_type=pl.DeviceIdType.MESH)
            cp.wait_send()
            cp.wait_recv()

    hbm = pl.BlockSpec(memory_space=pltpu.HBM)
    sem = pl.BlockSpec(memory_space=pltpu.SEMAPHORE)
    res = pl.pallas_call(
        body,
        name=name,
        out_shape=tuple([pltpu.HBM(a.shape, a.dtype) for a in srcs] + [pltpu.HBM(l.shape, l.dtype) for l in lands]),
        in_specs=[hbm] * (2 * n) + [sem] * (2 * n) + [pl.BlockSpec(memory_space=pl.ANY)],
        out_specs=tuple([hbm] * (2 * n)),
        input_output_aliases={i: i for i in range(2 * n)},
        compiler_params=pltpu.CompilerParams(has_side_effects=pltpu.SideEffectType.DATAFLOW_SIDE_EFFECTING),
    )(*srcs, *lands, *send, *recv, after)
    return list(res[n:])


def _inproj(x, g, wp):
    T = x.shape[0]
    tm = _tile(T, 512)

    def body(x_ref, g_ref, w_ref, qkv_ref, kt_ref, vt_ref, us_ref, fz_ref, ab_ref):
        a = _rms_fwd(x_ref[...], g_ref[...]).astype(BF16)
        ab_ref[...] = a
        qkv_ref[:, 0:AW] = _nn(a, w_ref[:, 0:AW]).astype(BF16)
        kk = _nn(a, w_ref[:, AW:2 * AW])
        qkv_ref[:, AW:2 * AW] = kk.astype(BF16)
        kt_ref[...] = kk.T.astype(BF16)
        vv = _nn(a, w_ref[:, 2 * AW:NQKV])
        qkv_ref[:, 2 * AW:NQKV] = vv.astype(BF16)
        vt_ref[...] = vv.T.astype(BF16)
        us_ref[...] = _nn(a, w_ref[:, NQKV:NQKV + NUS])
        fz_ref[...] = _nn(a, w_ref[:, NQKV + NUS:ZP])

    row = lambda n: pl.BlockSpec((tm, n), lambda i: (i, 0))
    col = pl.BlockSpec((AW, tm), lambda i: (0, i))
    return pl.pallas_call(
        body,
        name="inproj",
        grid=(T // tm,),
        in_specs=[row(D), _full((1, D)), _full((D, ZP))],
        out_specs=[row(NQKV), col, col, row(NUS), row(FPAD), row(D)],
        out_shape=[
            jax.ShapeDtypeStruct((T, NQKV), BF16),
            jax.ShapeDtypeStruct((AW, T), BF16),
            jax.ShapeDtypeStruct((AW, T), BF16),
            jax.ShapeDtypeStruct((T, NUS), F32),
            jax.ShapeDtypeStruct((T, FPAD), F32),
            jax.ShapeDtypeStruct((T, D), BF16),
        ],
        compiler_params=_params(1),
    )(x, g, wp)


def _log_sigmoid(z):
    return jnp.minimum(z, 0.0) - jnp.log1p(jnp.exp(-jnp.abs(z)))


def _split3(x):
    hi = x.astype(BF16)
    r1 = x - hi.astype(F32)
    mid = r1.astype(BF16)
    lo = (r1 - mid.astype(F32)).astype(BF16)
    return hi, mid, lo


AUG_A, AUG_B, AUG_ONE = 0, 3, 6


def _aug_lanes(rows, first_one, pieces):
    lane = lax.broadcasted_iota(jnp.int32, (rows, 128), 1)
    out = jnp.zeros((rows, 128), F32)
    if first_one is not None:
        out = jnp.where((lane >= first_one) & (lane < first_one + 3), 1.0, out)
    for n, piece in enumerate(pieces):
        out = jnp.where(lane == AUG_ONE + n, piece.astype(F32), out)
    return out.astype(BF16)


def _fcum(fz, fb, qkv):
    T = fz.shape[0]
    tb = _tile(T, 512)

    def body(fz_ref, fb_ref, k_ref, v_ref, ct_ref, kk_ref, vk_ref, carry):
        @pl.when(pl.program_id(0) == 0)
        def _():
            carry[...] = jnp.zeros_like(carry)

        lf = _log_sigmoid(fz_ref[...] + fb_ref[...])
        r = lax.broadcasted_iota(jnp.int32, (tb, tb), 0)
        cc = lax.broadcasted_iota(jnp.int32, (tb, tb), 1)
        tri = (cc <= r).astype(F32)
        cs = jnp.dot(tri, lf, precision=lax.Precision.HIGHEST, preferred_element_type=F32) + carry[...]
        ct_ref[...] = cs.T[0:8, :]
        carry[...] = carry[...] + jnp.sum(lf, axis=0, keepdims=True)

        pieces = _split3(cs)
        src = lax.broadcasted_iota(jnp.int32, (128, 128), 0)
        dst = lax.broadcasted_iota(jnp.int32, (128, 128), 1)
        ones = jnp.where((dst[0:1, :] >= AUG_ONE) & (dst[0:1, :] < AUG_ONE + 3), 1.0, 0.0)
        for hp in range(4):
            aug = jnp.zeros((tb, 128), F32) + ones
            for n, piece in enumerate(pieces):
                sel = jnp.where(((src == 2 * hp) & (dst == AUG_A + n)) | ((src == 2 * hp + 1) & (dst == AUG_B + n)), -1.0, 0.0)
                aug = aug + _nn(piece, sel.astype(BF16))
            aug = aug.astype(BF16)
            kk_ref[:, hp * 256:hp * 256 + 128] = k_ref[:, hp * 128:(hp + 1) * 128]
            kk_ref[:, hp * 256 + 128:(hp + 1) * 256] = aug
            vk_ref[:, hp * 256:hp * 256 + 128] = v_ref[:, hp * 128:(hp + 1) * 128]
            vk_ref[:, hp * 256 + 128:(hp + 1) * 256] = aug

    row = lambda n, c: pl.BlockSpec((tb, n), lambda i: (i, c))
    return pl.pallas_call(
        body,
        name="fcum",
        grid=(T // tb,),
        in_specs=[row(FPAD, 0), _full((1, FPAD)), row(AW, 1), row(AW, 2)],
        out_specs=[pl.BlockSpec((8, tb), lambda i: (0, i)), row(2 * AW, 0), row(2 * AW, 0)],
        out_shape=[jax.ShapeDtypeStruct((8, T), F32), jax.ShapeDtypeStruct((T, 2 * AW), BF16),
                   jax.ShapeDtypeStruct((T, 2 * AW), BF16)],
        scratch_shapes=[pltpu.VMEM((1, FPAD), F32)],
        compiler_params=_params(1),
    )(fz, fb, qkv, qkv)


def _fold_rows(r, t, nq):
    if nq == 1:
        return r, t
    low = t <= r
    return jnp.where(low, r, nq - 1 - r), jnp.where(low, t, t - r - 1)


def _fold_cols(r, t, nq):
    if nq == 1:
        return r, t
    first = t < nq - r
    j = jnp.where(first, r, nq - 1 - r)
    return j, jnp.where(first, r + t, nq - 1 - r + (t - (nq - r)))


PAIRS = 2


def _fold_grid(nq):
    assert nq == 1 or nq % 2 == 0
    return (4 // PAIRS, 1, 1) if nq == 1 else (4 // PAIRS, nq // 2, nq + 1)


def _head_rows(x2, hh, scale):
    is_a = lax.broadcasted_iota(jnp.int32, (1, 128), 1) < HD
    keep = is_a if hh == 0 else jnp.logical_not(is_a)
    return jnp.where(keep, x2, jnp.zeros_like(x2)) * scale


def _attn_fwd(qkv, kk, vt, ct):
    T = qkv.shape[0]
    tq = _tile(T, 512)
    tk = tq
    nq = T // tq

    def body(q_ref, kk_ref, vt_ref, ctq_ref, o_ref, lsec_ref, qw_s, m_s, l_s, acc_s):
        i, j = _fold_rows(pl.program_id(1), pl.program_id(2), nq)
        sub8 = lax.broadcasted_iota(jnp.int32, (8, 1), 0)

        def cref_of(pp, hh):
            head = 2 * (PAIRS * pl.program_id(0) + pp) + hh
            return jnp.sum(jnp.where(sub8 == head, ctq_ref[:, 0:1], 0.0), axis=0, keepdims=True)

        @pl.when(j == 0)
        def _():
            for pp in range(PAIRS):
                q2 = q_ref[:, pp * 128:(pp + 1) * 128]
                for hh in range(2):
                    rows = slice(hh * tq, (hh + 1) * tq)
                    qw_s[pp, rows, 0:128] = _head_rows(q2, hh, 0.125)
                    qw_s[pp, rows, 128:256] = _aug_lanes(tq, AUG_A if hh == 0 else AUG_B, _split3(cref_of(pp, hh)))
            m_s[...] = jnp.full_like(m_s, MASKV)
            l_s[...] = jnp.zeros_like(l_s)
            acc_s[...] = jnp.zeros_like(acc_s)

        def step(masked):
            for pp in range(PAIRS):
                s2 = _nt(kk_ref[:, pp * 256:(pp + 1) * 256], qw_s[pp])
                vt2 = vt_ref[pp * 128:(pp + 1) * 128, :]
                for hh in range(2):
                    n = 2 * pp + hh
                    s = s2[:, hh * tq:(hh + 1) * tq]
                    if masked:
                        key = lax.broadcasted_iota(jnp.int32, (tk, tq), 0)
                        qry = lax.broadcasted_iota(jnp.int32, (tk, tq), 1)
                        s = jnp.where(key <= qry, s, MASKV)
                    m_prev = m_s[n]
                    m_new = jnp.maximum(m_prev, jnp.max(s, axis=0, keepdims=True))
                    pr = jnp.exp(s - m_new)
                    alpha = jnp.exp(m_prev - m_new)
                    l_s[n] = alpha * l_s[n] + jnp.sum(pr, axis=0, keepdims=True)
                    m_s[n] = m_new
                    acc_s[n] = alpha * acc_s[n] + _nn(vt2, pr.astype(BF16))

        @pl.when(j < i)
        def _():
            step(False)

        @pl.when(j == i)
        def _():
            step(True)
            sub = lax.broadcasted_iota(jnp.int32, (128, 1), 0)
            for pp in range(PAIRS):
                a, b = 2 * pp, 2 * pp + 1
                ot = jnp.where(sub < HD, acc_s[a] * (1.0 / l_s[a]), acc_s[b] * (1.0 / l_s[b]))
                o_ref[:, pp * 128:(pp + 1) * 128] = ot.T
                lrow = [m_s[2 * pp + hh] + jnp.log(l_s[2 * pp + hh]) - cref_of(pp, hh) for hh in range(2)]
                lsec_ref[:, pp * 128:(pp + 1) * 128] = jnp.where(sub == 0, lrow[0], jnp.where(sub == 1, lrow[1], 0.0)).T

    qi = lambda r, t: _fold_rows(r, t, nq)[0]
    kj = lambda r, t: _fold_rows(r, t, nq)[1]
    return pl.pallas_call(
        body,
        name="attn_fwd",
        grid=_fold_grid(nq),
        in_specs=[
            pl.BlockSpec((tq, PAIRS * 128), lambda g, r, t: (qi(r, t), g)),
            pl.BlockSpec((tk, PAIRS * 256), lambda g, r, t: (kj(r, t), g)),
            pl.BlockSpec((PAIRS * 128, tk), lambda g, r, t: (g, kj(r, t))),
            pl.BlockSpec((8, tq), lambda g, r, t: (0, qi(r, t))),
        ],
        out_specs=[pl.BlockSpec((tq, PAIRS * 128), lambda g, r, t: (qi(r, t), g))] * 2,
        out_shape=[jax.ShapeDtypeStruct((T, AW), F32)] * 2,
        scratch_shapes=[pltpu.VMEM((PAIRS, 2 * tq, 256), BF16), pltpu.VMEM((2 * PAIRS, 1, tq), F32),
                        pltpu.VMEM((2 * PAIRS, 1, tq), F32), pltpu.VMEM((2 * PAIRS, 128, tq), F32)],
        compiler_params=_params(3),
    )(qkv, kk, vt, ct)


def _sgu_forward(us_ref, lng, lnb, w_ref, bt_ref, mixed_s, vnb_s, tm):
    is_a = lax.broadcasted_iota(jnp.int32, (1, 128), 1) < HD
    u = us_ref[:, 0:SW]
    vs = us_ref[:, SW:NUS]
    ug, tu = _gelu(u)
    vg, tv = _gelu(vs)
    mu = jnp.mean(vg, axis=-1, keepdims=True)
    xc = vg - mu
    rstd = lax.rsqrt(jnp.mean(xc * xc, axis=-1, keepdims=True) + EPS)
    vhat = xc * rstd
    vnb_s[...] = (vhat * lng + lnb).astype(BF16)
    rr = lax.broadcasted_iota(jnp.int32, (CH, CH), 0)
    cc = lax.broadcasted_iota(jnp.int32, (CH, CH), 1)
    tril = cc <= rr
    for jj in range(4):
        wa = jnp.where(tril, w_ref[2 * jj], 0.0).astype(BF16)
        wb = jnp.where(tril, w_ref[2 * jj + 1], 0.0).astype(BF16)
        ba = bt_ref[:, 2 * jj:2 * jj + 1]
        bb = bt_ref[:, 2 * jj + 1:2 * jj + 2]
        for ch in range(tm // CH):
            rs, cs = slice(ch * CH, (ch + 1) * CH), slice(jj * 128, (jj + 1) * 128)
            vn2 = vnb_s[rs, cs]
            mixed_s[rs, cs] = jnp.where(is_a, _nn(wa, vn2) + ba, _nn(wb, vn2) + bb)
    mixed = mixed_s[...]
    return u, vs, ug, tu, tv, vhat, rstd, mixed, ug * mixed


def _sgu_out(us, yatt, x, lng, lnb, sgw, sgbt, gatt, gsg, wout, gpm):
    T = us.shape[0]
    tm = _tile(T, 256)

    def body(us_ref, ya_ref, x_ref, lng_ref, lnb_ref, w_ref, bt_ref, ga_ref, gs_ref, wo_ref, gp_ref,
             h1_ref, yb_ref, o_ref, mixed_s, vnb_s):
        ysg = _sgu_forward(us_ref, lng_ref[...], lnb_ref[...], w_ref, bt_ref, mixed_s, vnb_s, tm)[-1]
        yb_ref[:, 0:AW] = _rms_fwd(ya_ref[...], ga_ref[...]).astype(BF16)
        yb_ref[:, AW:D] = _rms_fwd(ysg, gs_ref[...]).astype(BF16)
        o = _nn(yb_ref[...], wo_ref[...])
        o_ref[...] = o
        h1_ref[...] = x_ref[...] + _rms_fwd(o, gp_ref[...])

    row = lambda n: pl.BlockSpec((tm, n), lambda i: (i, 0))
    return pl.pallas_call(
        body,
        name="sgu_out",
        grid=(T // tm,),
        in_specs=[row(NUS), row(AW), row(D), _full((1, SW)), _full((1, SW)), _full((8, CH, CH)), _full((CH, 8)),
                  _full((1, AW)), _full((1, SW)), _full((D, D)), _full((1, D))],
        out_specs=[row(D), row(D), row(D)],
        out_shape=[jax.ShapeDtypeStruct((T, D), F32), jax.ShapeDtypeStruct((T, D), BF16), jax.ShapeDtypeStruct((T, D), F32)],
        scratch_shapes=[pltpu.VMEM((tm, SW), F32), pltpu.VMEM((tm, SW), BF16)],
        compiler_params=_params(1),
    )(us, yatt, x, lng, lnb, sgw, sgbt, gatt, gsg, wout, gpm)


def _ffn_fwd(h1, gpre, w1g, w2g, gpost):
    T = h1.shape[0]
    tm = _tile(T, 512)
    nb, hb = w1g.shape[0], w1g.shape[2]

    def body(h1_ref, gpre_ref, w1_ref, w2_ref, gpost_ref, c2_ref, f1_ref, ff_ref, h2_ref, h2b_ref):
        h1 = h1_ref[...]
        c2 = _rms_fwd(h1, gpre_ref[...]).astype(BF16)
        c2_ref[...] = c2
        ff = jnp.zeros((tm, D), F32)
        for j in range(nb):
            f1 = _nn(c2, w1_ref[j])
            f1_ref[:, j * hb:(j + 1) * hb] = f1.astype(BF16)
            r = jnp.maximum(f1, 0.0)
            ff = ff + _nn((r * r).astype(BF16), w2_ref[j])
        ff_ref[...] = ff
        h2 = h1 + _rms_fwd(ff, gpost_ref[...])
        h2_ref[...] = h2
        h2b_ref[...] = h2.astype(BF16)

    row = lambda n: pl.BlockSpec((tm, n), lambda i: (i, 0))
    once = lambda shape: pl.BlockSpec(shape, lambda i: (0,) * len(shape), pipeline_mode=pl.Buffered(1))
    return pl.pallas_call(
        body,
        name="ffn_fwd",
        grid=(T // tm,),
        in_specs=[row(D), _full((1, D)), once((nb, D, hb)), once((nb, hb, D)), _full((1, D))],
        out_specs=[row(D), row(DFF), row(D), row(D), row(D)],
        out_shape=[jax.ShapeDtypeStruct((T, D), BF16), jax.ShapeDtypeStruct((T, DFF), BF16),
                   jax.ShapeDtypeStruct((T, D), F32), jax.ShapeDtypeStruct((T, D), F32),
                   jax.ShapeDtypeStruct((T, D), BF16)],
        compiler_params=_params(1, VMEM_LIMIT_BIG),
    )(h1, gpre, w1g, w2g, gpost)


def _ple_loss(h2, p, tgt, wg, bg, wpe):
    T = h2.shape[0]
    tm = _tile(T, 512)

    def body(h2_ref, p_ref, t_ref, wg_ref, bg_ref, wpe_ref, dh2_ref, dpre_ref, dpe_ref, pb_ref, dbg_ref, loss_ref):
        @pl.when(pl.program_id(0) == 0)
        def _():
            dbg_ref[...] = jnp.zeros_like(dbg_ref)
            loss_ref[...] = jnp.zeros_like(loss_ref)

        h2 = h2_ref[...]
        gate = jax.nn.sigmoid(_nn(h2.astype(BF16), wg_ref[...]) + bg_ref[...])
        pb = p_ref[...].astype(BF16)
        pb_ref[...] = pb
        pe = _nn(pb, wpe_ref[...])
        diff = (h2 + gate * pe) - t_ref[...]
        loss_ref[...] += jnp.sum(diff * diff)
        dh3 = diff * (1.0 / D)
        dpre = (dh3 * pe) * (gate * (1.0 - gate))
        dpre_b = dpre.astype(BF16)
        dpre_ref[...] = dpre_b
        dpe_ref[...] = (dh3 * gate).astype(BF16)
        dbg_ref[...] += jnp.sum(dpre, axis=0, keepdims=True)
        dh2_ref[...] = dh3 + _nt(dpre_b, wg_ref[...])

    row = lambda n: pl.BlockSpec((tm, n), lambda i: (i, 0))
    return pl.pallas_call(
        body,
        name="ple_loss",
        grid=(T // tm,),
        in_specs=[row(D), row(PLE), row(D), _full((D, D)), _full((1, D)), _full((PLE, D))],
        out_specs=[row(D), row(D), row(D), row(PLE), _full((1, D)), _full((8, 128))],
        out_shape=[jax.ShapeDtypeStruct((T, D), F32), jax.ShapeDtypeStruct((T, D), BF16),
                   jax.ShapeDtypeStruct((T, D), BF16), jax.ShapeDtypeStruct((T, PLE), BF16),
                   jax.ShapeDtypeStruct((1, D), F32), jax.ShapeDtypeStruct((8, 128), F32)],
        compiler_params=_params(1),
    )(h2, p, tgt, wg, bg, wpe)


def _ffn_bwd(dh2, ff, h1, f1, w1g, w2g, gpost, gpre):
    T = dh2.shape[0]
    tm = _tile(T, 512)
    nb, hb = w1g.shape[0], w1g.shape[2]

    def body(dh2_ref, ff_ref, h1_ref, f1_ref, w1_ref, w2_ref, gpost_ref, gpre_ref,
             dffb_ref, df1_ref, dh1_ref, dgpost_ref, dgpre_ref):
        @pl.when(pl.program_id(0) == 0)
        def _():
            dgpost_ref[...] = jnp.zeros_like(dgpost_ref)
            dgpre_ref[...] = jnp.zeros_like(dgpre_ref)

        dh2 = dh2_ref[...]
        dff, dg = _rms_bwd(dh2, ff_ref[...], gpost_ref[...])
        dffb = dff.astype(BF16)
        dffb_ref[...] = dffb
        dgpost_ref[...] += dg
        dc2 = jnp.zeros((tm, D), F32)
        for j in range(nb):
            cols = slice(j * hb, (j + 1) * hb)
            dact = _nt(dffb, w2_ref[j])
            df1 = (dact * (2.0 * jnp.maximum(f1_ref[:, cols].astype(F32), 0.0))).astype(BF16)
            df1_ref[:, cols] = df1
            dc2 = dc2 + _nt(df1, w1_ref[j])
        dx, dg = _rms_bwd(dc2, h1_ref[...], gpre_ref[...])
        dh1_ref[...] = dh2 + dx
        dgpre_ref[...] += dg

    row = lambda n: pl.BlockSpec((tm, n), lambda i: (i, 0))
    once = lambda shape: pl.BlockSpec(shape, lambda i: (0,) * len(shape), pipeline_mode=pl.Buffered(1))
    return pl.pallas_call(
        body,
        name="ffn_bwd",
        grid=(T // tm,),
        in_specs=[row(D), row(D), row(D), row(DFF), once((nb, D, hb)), once((nb, hb, D)), _full((1, D)), _full((1, D))],
        out_specs=[row(D), row(DFF), row(D), _full((1, D)), _full((1, D))],
        out_shape=[jax.ShapeDtypeStruct((T, D), BF16), jax.ShapeDtypeStruct((T, DFF), BF16),
                   jax.ShapeDtypeStruct((T, D), F32), jax.ShapeDtypeStruct((1, D), F32),
                   jax.ShapeDtypeStruct((1, D), F32)],
        compiler_params=_params(1, VMEM_LIMIT_BIG),
    )(dh2, ff, h1, f1, w1g, w2g, gpost, gpre)


def _mix_bwd(dh1, o, us, yatt, lng, lnb, sgw, sgwt, sgbt, gatt, gsg, wout, gpm):
    T = dh1.shape[0]
    tm = _tile(T, 256)

    def body(dh1_ref, o_ref, us_ref, ya_ref, lng_ref, lnb_ref, w_ref, wt_ref, bt_ref, ga_ref, gs_ref, wo_ref, gp_ref,
             dob_ref, dya_ref, dus_ref, dw_ref, dbt_ref, dlng_ref, dlnb_ref, dga_ref, dgs_ref, dgp_ref,
             mixed_s, vnb_s, dvn_s):
        @pl.when(pl.program_id(0) == 0)
        def _():
            for r in (dw_ref, dbt_ref, dlng_ref, dlnb_ref, dga_ref, dgs_ref, dgp_ref):
                r[...] = jnp.zeros_like(r)

        is_a = lax.broadcasted_iota(jnp.int32, (1, 128), 1) < HD
        lane = lax.broadcasted_iota(jnp.int32, (1, 128), 1)
        do, dg = _rms_bwd(dh1_ref[...], o_ref[...], gp_ref[...])
        dgp_ref[...] += dg
        dob = do.astype(BF16)
        dob_ref[...] = dob
        dy = _nt(dob, wo_ref[...])
        datt, dg = _rms_bwd(dy[:, 0:AW], ya_ref[...], ga_ref[...])
        dga_ref[...] += dg
        dya_ref[...] = datt

        lng = lng_ref[...]
        u, vs, ug, tu, tv, vhat, rstd, mixed, ysg = _sgu_forward(us_ref, lng, lnb_ref[...], w_ref, bt_ref, mixed_s, vnb_s, tm)
        dysg, dg = _rms_bwd(dy[:, AW:D], ysg, gs_ref[...])
        dgs_ref[...] += dg
        dus_ref[:, 0:SW] = ((dysg * mixed) * _gelu_grad(u, tu)).astype(BF16)
        dmix = dysg * ug

        rr = lax.broadcasted_iota(jnp.int32, (CH, CH), 0)
        cc = lax.broadcasted_iota(jnp.int32, (CH, CH), 1)
        tril = cc <= rr
        triu = cc >= rr
        for jj in range(4):
            wta = jnp.where(triu, wt_ref[2 * jj], 0.0).astype(BF16)
            wtb = jnp.where(triu, wt_ref[2 * jj + 1], 0.0).astype(BF16)
            for ch in range(tm // CH):
                rs, cs = slice(ch * CH, (ch + 1) * CH), slice(jj * 128, (jj + 1) * 128)
                dm2 = dmix[rs, cs]
                dma = jnp.where(is_a, dm2, 0.0)
                dmb = jnp.where(is_a, 0.0, dm2)
                dma_b, dmb_b = dma.astype(BF16), dmb.astype(BF16)
                vn2 = vnb_s[rs, cs]
                dw_ref[2 * jj] += jnp.where(tril, _nt(dma_b, vn2), 0.0)
                dw_ref[2 * jj + 1] += jnp.where(tril, _nt(dmb_b, vn2), 0.0)
                dvn_s[rs, cs] = _nn(wta, dma_b) + _nn(wtb, dmb_b)
                dba = jnp.sum(dma, axis=1, keepdims=True)
                dbb = jnp.sum(dmb, axis=1, keepdims=True)
                dbt_ref[...] += jnp.where(lane == 2 * jj, dba, 0.0) + jnp.where(lane == 2 * jj + 1, dbb, 0.0)

        dvn = dvn_s[...]
        dlng_ref[...] += jnp.sum(dvn * vhat, axis=0, keepdims=True)
        dlnb_ref[...] += jnp.sum(dvn, axis=0, keepdims=True)
        dvh = dvn * lng
        dvg = rstd * (dvh - jnp.mean(dvh, axis=-1, keepdims=True) - vhat * jnp.mean(dvh * vhat, axis=-1, keepdims=True))
        dus_ref[:, SW:NUS] = (dvg * _gelu_grad(vs, tv)).astype(BF16)

    row = lambda n: pl.BlockSpec((tm, n), lambda i: (i, 0))
    return pl.pallas_call(
        body,
        name="mix_bwd",
        grid=(T // tm,),
        in_specs=[row(D), row(D), row(NUS), row(AW), _full((1, SW)), _full((1, SW)), _full((8, CH, CH)), _full((8, CH, CH)),
                  _full((CH, 8)), _full((1, AW)), _full((1, SW)), _full((D, D)), _full((1, D))],
        out_specs=[row(D), row(AW), row(NUS), _full((8, CH, CH)), _full((CH, 128)), _full((1, SW)), _full((1, SW)),
                   _full((1, AW)), _full((1, SW)), _full((1, D))],
        out_shape=[jax.ShapeDtypeStruct((T, D), BF16), jax.ShapeDtypeStruct((T, AW), F32), jax.ShapeDtypeStruct((T, NUS), BF16),
                   jax.ShapeDtypeStruct((8, CH, CH), F32), jax.ShapeDtypeStruct((CH, 128), F32),
                   jax.ShapeDtypeStruct((1, SW), F32), jax.ShapeDtypeStruct((1, SW), F32),
                   jax.ShapeDtypeStruct((1, AW), F32), jax.ShapeDtypeStruct((1, SW), F32), jax.ShapeDtypeStruct((1, D), F32)],
        scratch_shapes=[pltpu.VMEM((tm, SW), F32), pltpu.VMEM((tm, SW), BF16), pltpu.VMEM((tm, SW), F32)],
        compiler_params=_params(1),
    )(dh1, o, us, yatt, lng, lnb, sgw, sgwt, sgbt, gatt, gsg, wout, gpm)


def _attn_prep(qkv, dya, yatt, lsec, after):
    T = dya.shape[0]
    tm = _tile(T, 512)

    def body(q_ref, do_ref, o_ref, l_ref, after_ref, qw_ref, dow_ref):
        do = do_ref[...]
        feat = lax.broadcasted_iota(jnp.int32, (AW, 128), 0)
        head = lax.broadcasted_iota(jnp.int32, (AW, 128), 1)
        sel = jnp.where((feat >= head * HD) & (feat < (head + 1) * HD), 1.0, 0.0)
        delta = jnp.dot(do * o_ref[...], sel, precision=lax.Precision.HIGHEST, preferred_element_type=F32)
        for hp in range(4):
            cols = slice(hp * 128, (hp + 1) * 128)
            for hh in range(2):
                base = (2 * hp + hh) * 256
                lc = l_ref[:, hp * 128 + hh:hp * 128 + hh + 1]
                d_h = delta[:, 2 * hp + hh:2 * hp + hh + 1]
                qw_ref[:, base:base + 128] = _head_rows(q_ref[:, cols], hh, 0.125)
                qw_ref[:, base + 128:base + 256] = _aug_lanes(tm, AUG_A if hh == 0 else AUG_B, _split3(-lc))
                dow_ref[:, base:base + 128] = _head_rows(do[:, cols], hh, 1.0).astype(BF16)
                dow_ref[:, base + 128:base + 256] = _aug_lanes(tm, None, _split3(-d_h))

    row = lambda n: pl.BlockSpec((tm, n), lambda i: (i, 0))
    return pl.pallas_call(
        body,
        name="attn_prep",
        grid=(T // tm,),
        in_specs=[row(AW)] * 4 + [_full(after.shape)],
        out_specs=[row(4 * AW)] * 2,
        out_shape=[jax.ShapeDtypeStruct((T, 4 * AW), BF16)] * 2,
        compiler_params=_params(1),
    )(qkv, dya, yatt, lsec, after)


def _attn_bwd(kk, vk, kt, qw, dow):
    T = kk.shape[0]
    tq = _tile(T, 512)
    tk = tq
    nq = T // tq

    def body(kk_ref, vk_ref, kt_ref, qw_ref, dow_ref, dqt_ref, dcq_ref, dk_ref, dv_ref, dck_ref, dk_s, dv_s, dck_s):
        j, i = _fold_cols(pl.program_id(1), pl.program_id(2), nq)
        sub8 = lax.broadcasted_iota(jnp.int32, (8, 1), 0)
        lane = lax.broadcasted_iota(jnp.int32, (1, 128), 1)

        @pl.when((pl.program_id(1) == 0) & (pl.program_id(2) == 0))
        def _():
            dqt_ref[...] = jnp.zeros_like(dqt_ref)
            dcq_ref[...] = jnp.zeros_like(dcq_ref)

        @pl.when(i == j)
        def _():
            dk_s[...] = jnp.zeros_like(dk_s)
            dv_s[...] = jnp.zeros_like(dv_s)
            dck_s[...] = jnp.zeros_like(dck_s)

        def step(masked):
            cols = pl.ds(pl.multiple_of(i * tq, tq), tq)
            sub = lax.broadcasted_iota(jnp.int32, (128, 1), 0)
            for pp in range(PAIRS):
                lanes = slice(pp * 128, (pp + 1) * 128)
                kk2, vk2 = kk_ref[:, pp * 256:(pp + 1) * 256], vk_ref[:, pp * 256:(pp + 1) * 256]
                kt2 = kt_ref[lanes, :] * 0.125
                dcq = jnp.zeros((8, tq), F32)
                dck = jnp.zeros((tk, 128), F32)
                dv = jnp.zeros((tk, 128), F32)
                dk = jnp.zeros((tk, 128), F32)
                dqts = []
                for hh in range(2):
                    base = (2 * pp + hh) * 256
                    qw_h = qw_ref[:, base:base + 256]
                    dow_h = dow_ref[:, base:base + 256]
                    pr = jnp.exp(_nt(kk2, qw_h))
                    if masked:
                        key = lax.broadcasted_iota(jnp.int32, (tk, tq), 0)
                        qry = lax.broadcasted_iota(jnp.int32, (tk, tq), 1)
                        pr = jnp.where(key <= qry, pr, 0.0)
                    ds = pr * _nt(vk2, dow_h)
                    ds_b = ds.astype(BF16)
                    dv = dv + _nn(pr.astype(BF16), dow_h[:, 0:128])
                    dk = dk + _nn(ds_b, qw_h[:, 0:128])
                    dqts.append(_nn(kt2, ds_b))
                    dcq = dcq + jnp.where(sub8 == hh, jnp.sum(ds, axis=0, keepdims=True), 0.0)
                    dck = dck - jnp.where(lane == hh, jnp.sum(ds, axis=1, keepdims=True), 0.0)
                dv_s[:, lanes] += dv
                dk_s[:, lanes] += dk
                dck_s[:, lanes] += dck
                dcq_ref[pp, :, cols] += dcq
                dqt_ref[lanes, cols] += jnp.where(sub < HD, dqts[0], dqts[1])

        @pl.when(i > j)
        def _():
            step(False)

        @pl.when(i == j)
        def _():
            step(True)

        @pl.when(i == nq - 1)
        def _():
            dk_ref[...] = dk_s[...].astype(BF16)
            dv_ref[...] = dv_s[...].astype(BF16)
            dck_ref[...] = dck_s[...]

    kj = lambda r, t: _fold_cols(r, t, nq)[0]
    qi = lambda r, t: _fold_cols(r, t, nq)[1]
    krow = lambda g, r, t: (kj(r, t), g)
    qrow = lambda g, r, t: (qi(r, t), g)
    return pl.pallas_call(
        body,
        name="attn_bwd",
        grid=_fold_grid(nq),
        in_specs=[
            pl.BlockSpec((tk, PAIRS * 256), krow),
            pl.BlockSpec((tk, PAIRS * 256), krow),
            pl.BlockSpec((PAIRS * 128, tk), lambda g, r, t: (g, kj(r, t))),
            pl.BlockSpec((tq, PAIRS * 512), qrow),
            pl.BlockSpec((tq, PAIRS * 512), qrow),
        ],
        out_specs=[
            pl.BlockSpec((PAIRS * 128, T), lambda g, r, t: (g, 0)),
            pl.BlockSpec((PAIRS, 8, T), lambda g, r, t: (g, 0, 0)),
            pl.BlockSpec((tk, PAIRS * 128), krow),
            pl.BlockSpec((tk, PAIRS * 128), krow),
            pl.BlockSpec((tk, PAIRS * 128), krow),
        ],
        out_shape=[jax.ShapeDtypeStruct((AW, T), F32), jax.ShapeDtypeStruct((4, 8, T), F32),
                   jax.ShapeDtypeStruct((T, AW), BF16), jax.ShapeDtypeStruct((T, AW), BF16),
                   jax.ShapeDtypeStruct((T, AW), F32)],
        scratch_shapes=[pltpu.VMEM((tk, PAIRS * 128), F32), pltpu.VMEM((tk, PAIRS * 128), F32),
                        pltpu.VMEM((tk, PAIRS * 128), F32)],
        compiler_params=_params(3),
    )(kk, vk, kt, qw, dow)


def _fgate_bwd(dcq, dck, fz, fb):
    T = dck.shape[0]
    tb = _tile(T, 512)
    nb = T // tb

    def body(dcq_ref, dck_ref, fz_ref, fb_ref, df_ref, dfb_ref, carry):
        @pl.when(pl.program_id(0) == 0)
        def _():
            carry[...] = jnp.zeros_like(carry)
            dfb_ref[...] = jnp.zeros_like(dfb_ref)

        dcv = dcq_ref[...] + dck_ref[...]
        r = lax.broadcasted_iota(jnp.int32, (tb, tb), 0)
        cc = lax.broadcasted_iota(jnp.int32, (tb, tb), 1)
        tri = (cc >= r).astype(F32)
        dlf = jnp.dot(tri, dcv, precision=lax.Precision.HIGHEST, preferred_element_type=F32) + carry[...]
        carry[...] = carry[...] + jnp.sum(dcv, axis=0, keepdims=True)
        lane = lax.broadcasted_iota(jnp.int32, (tb, FPAD), 1)
        df = jnp.where(lane < 8, dlf * jax.nn.sigmoid(-(fz_ref[...] + fb_ref[...])), 0.0)
        df_ref[...] = df.astype(BF16)
        dfb_ref[...] += jnp.sum(df, axis=0, keepdims=True)

    rev = pl.BlockSpec((tb, FPAD), lambda i: (nb - 1 - i, 0))
    return pl.pallas_call(
        body,
        name="fgate_bwd",
        grid=(nb,),
        in_specs=[rev, rev, rev, _full((1, FPAD))],
        out_specs=[rev, _full((1, FPAD))],
        out_shape=[jax.ShapeDtypeStruct((T, FPAD), BF16), jax.ShapeDtypeStruct((1, FPAD), F32)],
        scratch_shapes=[pltpu.VMEM((1, FPAD), F32)],
        compiler_params=_params(1),
    )(dcq, dck, fz, fb)


def _inproj_bwd(dq, dk, dv, dus, dfz, wp, x, dh1, g):
    T = x.shape[0]
    tm = _tile(T, 512)

    def body(dq_ref, dk_ref, dv_ref, dus_ref, dfz_ref, w_ref, x_ref, dh1_ref, g_ref, gx_ref, dg_ref):
        @pl.when(pl.program_id(0) == 0)
        def _():
            dg_ref[...] = jnp.zeros_like(dg_ref)

        da = _nt(dq_ref[...].astype(BF16), w_ref[:, 0:AW])
        da += _nt(dk_ref[...], w_ref[:, AW:2 * AW])
        da += _nt(dv_ref[...], w_ref[:, 2 * AW:NQKV])
        da += _nt(dus_ref[...], w_ref[:, NQKV:NQKV + NUS])
        da += _nt(dfz_ref[...], w_ref[:, NQKV + NUS:ZP])
        dx, dg = _rms_bwd(da, x_ref[...], g_ref[...])
        gx_ref[...] = dh1_ref[...] + dx
        dg_ref[...] += dg

    row = lambda n: pl.BlockSpec((tm, n), lambda i: (i, 0))
    return pl.pallas_call(
        body,
        name="inproj_bwd",
        grid=(T // tm,),
        in_specs=[row(AW), row(AW), row(AW), row(NUS), row(FPAD), _full((D, ZP)), row(D), row(D), _full((1, D))],
        out_specs=[row(D), _full((1, D))],
        out_shape=[jax.ShapeDtypeStruct((T, D), F32), jax.ShapeDtypeStruct((1, D), F32)],
        compiler_params=_params(1),
    )(dq, dk, dv, dus, dfz, wp, x, dh1, g)


def _sq_relu(f1):
    r = jnp.maximum(f1.astype(F32), 0.0)
    return (r * r).astype(BF16)


def _wgrad(a, bs, name, a_fn=None):
    T, K = a.shape
    tt = _tile(T, 512)
    nb = len(bs)

    def out_dims(b, layout):
        N = b.shape[1]
        if layout is None:
            return (K, N)
        return (N // layout[1], K, layout[1]) if layout[0] == "col" else (K // layout[1], layout[1], N)

    shapes = [out_dims(b, layout) for b, layout in bs]

    def body(*refs):
        a_ref, b_refs, o_refs = refs[0], refs[1:1 + nb], refs[1 + nb:]

        @pl.when(pl.program_id(0) == 0)
        def _():
            for o_ref in o_refs:
                o_ref[...] = jnp.zeros_like(o_ref)

        av = a_ref[...] if a_fn is None else a_fn(a_ref[...])
        at = av.astype(BF16).T
        for (b, layout), b_ref, o_ref in zip(bs, b_refs, o_refs):
            if layout is None:
                o_ref[...] += _nn(at, b_ref[...].astype(BF16))
            elif layout[0] == "col":
                n = layout[1]
                for k in range(b.shape[1] // n):
                    o_ref[k] += _nn(at, b_ref[:, k * n:(k + 1) * n].astype(BF16))
            else:
                n = layout[1]
                bv = b_ref[...].astype(BF16)
                for k in range(K // n):
                    o_ref[k] += _nn(at[k * n:(k + 1) * n, :], bv)

    once = lambda shape: pl.BlockSpec(shape, lambda t: (0,) * len(shape), pipeline_mode=pl.Buffered(1))
    res = pl.pallas_call(
        body,
        name=name,
        grid=(T // tt,),
        in_specs=[pl.BlockSpec((tt, K), lambda t: (t, 0))] + [pl.BlockSpec((tt, b.shape[1]), lambda t: (t, 0)) for b, _ in bs],
        out_specs=[once(s) for s in shapes],
        out_shape=[jax.ShapeDtypeStruct(s, F32) for s in shapes],
        compiler_params=_params(1, VMEM_LIMIT_BIG),
    )(a, *[b for b, _ in bs])
    return res


def _adam_math(w, g, m, v):
    m = ADAM_B1 * m + (1.0 - ADAM_B1) * g
    v = ADAM_B2 * v + (1.0 - ADAM_B2) * (g * g)
    m_hat = m / (1.0 - ADAM_B1 ** ADAM_STEP)
    v_hat = v / (1.0 - ADAM_B2 ** ADAM_STEP)
    delta = -ADAM_LR * (m_hat / (jnp.sqrt(v_hat) + ADAM_EPS) + ADAM_WD * w)
    return delta, m, v


def _adam(parts, w, m, v, name):
    R, C = w.shape
    br = 128 if R % 128 == 0 else R

    def body(p_ref, w_ref, m_ref, v_ref, g_ref, d_ref, nm_ref, nv_ref):
        g = p_ref[0].astype(F32)
        for s in range(1, NDEV):
            g = g + p_ref[s].astype(F32)
        g_ref[...] = g
        d_ref[...], nm_ref[...], nv_ref[...] = _adam_math(w_ref[...], g, m_ref[...], v_ref[...])

    blk = pl.BlockSpec((br, C), lambda i: (i, 0))
    return pl.pallas_call(
        body,
        name=name,
        grid=(R // br,),
        in_specs=[pl.BlockSpec((NDEV, br, C), lambda i: (0, i, 0)), blk, blk, blk],
        out_specs=[blk] * 4,
        out_shape=[jax.ShapeDtypeStruct((R, C), F32)] * 4,
        compiler_params=_params(1),
    )(parts, w, m, v)


_SMALL = (("sg_w", 8 * CH * CH), ("f_bias", 8), ("sg_ln_g", SW), ("sg_ln_b", SW), ("sg_b", 8 * CH), ("att_out_g", AW),
          ("sg_out_g", SW), ("pre_mix_g", D), ("post_mix_g", D), ("pre_ffn_g", D), ("post_ffn_g", D), ("ple_gate_b", D))
_SEG = 8 * 128


def _seg_rows(size):
    return 8 * (-(-size // _SEG))


def _pack(vals):
    parts = []
    for name, size in _SMALL:
        flat = vals[name].reshape(-1)
        rows = _seg_rows(size)
        parts.append(jnp.pad(flat, (0, rows * 128 - size)).reshape(rows, 128))
    return jnp.concatenate(parts, axis=0)


def _unpack(packed, shapes):
    out, r = {}, 0
    for name, size in _SMALL:
        rows = _seg_rows(size)
        out[name] = packed[r:r + rows].reshape(-1)[:size].reshape(shapes[name])
        r += rows
    return out


def kernel(x, p, w_in, f_bias, sg_ln_g, sg_ln_b, sg_w, sg_b, att_out_g, sg_out_g, w_out, pre_mix_g, post_mix_g, pre_ffn_g, post_ffn_g, w_ff1, w_ff2, ple_w, ple_gate_w, ple_gate_b, loss_target, m_w_in, m_f_bias, m_sg_ln_g, m_sg_ln_b, m_sg_w, m_sg_b, m_att_out_g, m_sg_out_g, m_w_out, m_pre_mix_g, m_post_mix_g, m_pre_ffn_g, m_post_ffn_g, m_w_ff1, m_w_ff2, m_ple_w, m_ple_gate_w, m_ple_gate_b, v_w_in, v_f_bias, v_sg_ln_g, v_sg_ln_b, v_sg_w, v_sg_b, v_att_out_g, v_sg_out_g, v_w_out, v_pre_mix_g, v_post_mix_g, v_pre_ffn_g, v_post_ffn_g, v_w_ff1, v_w_ff2, v_ple_w, v_ple_gate_w, v_ple_gate_b):
    small_w = dict(sg_w=sg_w, f_bias=f_bias, sg_ln_g=sg_ln_g, sg_ln_b=sg_ln_b, sg_b=sg_b, att_out_g=att_out_g,
                   sg_out_g=sg_out_g, pre_mix_g=pre_mix_g, post_mix_g=post_mix_g, pre_ffn_g=pre_ffn_g,
                   post_ffn_g=post_ffn_g, ple_gate_b=ple_gate_b)
    small_m = dict(sg_w=m_sg_w, f_bias=m_f_bias, sg_ln_g=m_sg_ln_g, sg_ln_b=m_sg_ln_b, sg_b=m_sg_b, att_out_g=m_att_out_g,
                   sg_out_g=m_sg_out_g, pre_mix_g=m_pre_mix_g, post_mix_g=m_post_mix_g, pre_ffn_g=m_pre_ffn_g,
                   post_ffn_g=m_post_ffn_g, ple_gate_b=m_ple_gate_b)
    small_v = dict(sg_w=v_sg_w, f_bias=v_f_bias, sg_ln_g=v_sg_ln_g, sg_ln_b=v_sg_ln_b, sg_b=v_sg_b, att_out_g=v_att_out_g,
                   sg_out_g=v_sg_out_g, pre_mix_g=v_pre_mix_g, post_mix_g=v_post_mix_g, pre_ffn_g=v_pre_ffn_g,
                   post_ffn_g=v_post_ffn_g, ple_gate_b=v_ple_gate_b)
    big = dict(w_in=(w_in, m_w_in, v_w_in), w_out=(w_out, m_w_out, v_w_out), w_ff1=(w_ff1, m_w_ff1, v_w_ff1),
               w_ff2=(w_ff2, m_w_ff2, v_w_ff2), ple_w=(ple_w, m_ple_w, v_ple_w),
               ple_gate_w=(ple_gate_w, m_ple_gate_w, v_ple_gate_w))

    xt, pt, tgt = x[0], p[0, 0], loss_target[0]
    ws = W_IN_COLS // NDEV

    (gw_in,) = _xchg([w_in[0].astype(BF16)], True, "gather_w_in")
    rest = _xchg_start([w_out[0].astype(BF16), w_ff1[0].astype(BF16), w_ff2[0].astype(BF16), ple_w[0].astype(BF16),
                        ple_gate_w[0].astype(BF16)], True, "gather_rest_start")
    win = jnp.transpose(gw_in, (1, 0, 2)).reshape(D, W_IN_COLS)
    wp = jnp.concatenate([win[:, 0:NQKV], win[:, NQKV + 8:W_IN_COLS], win[:, NQKV:NQKV + 8],
                          jnp.zeros((D, FPAD - 8), BF16)], axis=1)

    fb = jnp.pad(f_bias.astype(F32), ((0, 0), (0, FPAD - 8)))
    sgw = sg_w[0]
    sgwt = jnp.transpose(sg_w[0], (0, 2, 1))
    sgbt = jnp.transpose(sg_b[0])

    qkv, kt, vt, us, fz, ab = _inproj(xt, pre_mix_g + rest[4][0, 0], wp)
    ct, kk, vk = _fcum(fz, fb, qkv)
    yatt, lsec = _attn_fwd(qkv, kk, vt, ct)
    gw_out, gw1, gw2, gwpe, gwg = _xchg_wait(rest, True, yatt, "gather_rest_wait")
    wout = gw_out.reshape(D, D)
    wg = gwg.reshape(D, D)
    wpe = jnp.transpose(gwpe, (1, 0, 2)).reshape(PLE, D)
    h1, yb, o = _sgu_out(us, yatt, xt, sg_ln_g, sg_ln_b, sgw, sgbt, att_out_g, sg_out_g, wout, post_mix_g)
    c2b, f1b, ff, h2, h2b = _ffn_fwd(h1, pre_ffn_g, gw1, gw2, post_ffn_g)
    dh2, dpre, dpe, pb, dbg, loss_acc = _ple_loss(h2, pt, tgt, wg, ple_gate_b, wpe)
    loss = lax.psum(loss_acc[0, 0] * (0.5 / D), ("x", "y", "c"))

    (g_pe,) = _wgrad(pb, [(dpe, ("col", D // NDEV))], "wgrad_ple")
    (g_g,) = _wgrad(h2b, [(dpre, ("row", D // NDEV))], "wgrad_gate")
    dffb, df1, dh1, dgpostffn, dgpreffn = _ffn_bwd(dh2, ff, h1, f1b, gw1, gw2, post_ffn_g, pre_ffn_g)
    (g_1,) = _wgrad(c2b, [(df1, ("col", DFF // NDEV))], "wgrad_ff1")
    (g_2,) = _wgrad(f1b, [(dffb, ("row", DFF // NDEV))], "wgrad_ff2", a_fn=_sq_relu)
    early = _xchg_start([g_1, g_2, g_pe, g_g], False, "scatter_early_start")
    dob, dya, dus, dsgw, dsgbt, dlng, dlnb, dgatt, dgsg, dgpostmix = _mix_bwd(
        dh1, o, us, yatt, sg_ln_g, sg_ln_b, sgw, sgwt, sgbt, att_out_g, sg_out_g, wout, post_mix_g + early[4][0, 0])
    (g_out,) = _wgrad(yb, [(dob, ("row", D // NDEV))], "wgrad_out")
    mid = _xchg_start([g_out], False, "scatter_mid_start")
    qw, dow = _attn_prep(qkv, dya, yatt, lsec, mid[4])
    dqt, dcq, dk, dv, dck = _attn_bwd(kk, vk, kt, qw, dow)
    dq = jnp.transpose(dqt)
    dcq = jnp.pad(jnp.transpose(dcq[:, 0:2, :].reshape(8, -1)), ((0, 0), (0, FPAD - 8)))
    dck = jnp.pad(dck.reshape(-1, 4, 128)[:, :, 0:2].reshape(-1, 8), ((0, 0), (0, FPAD - 8)))
    dfz, dfb = _fgate_bwd(dcq, dck, fz, fb)

    gq, gk, gv, gus, gf = _wgrad(ab, [(dq, None), (dk, None), (dv, None), (dus, None), (dfz, None)], "wgrad_in")
    g_in = jnp.concatenate([gq, gk, gv, gf[:, 0:8], gus], axis=1)
    g_in = jnp.transpose(g_in.reshape(D, NDEV, ws), (1, 0, 2))
    late = _xchg_start([g_in.astype(BF16)], False, "scatter_late_start")
    grad_x, dgpremix = _inproj_bwd(dq, dk, dv, dus, dfz, wp, xt, dh1, pre_mix_g + late[4][0, 0])

    small_g = dict(sg_w=dsgw, f_bias=dfb[:, 0:8], sg_ln_g=dlng, sg_ln_b=dlnb, sg_b=jnp.transpose(dsgbt[:, 0:8]),
                   att_out_g=dgatt, sg_out_g=dgsg, pre_mix_g=dgpremix, post_mix_g=dgpostmix, pre_ffn_g=dgpreffn,
                   post_ffn_g=dgpostffn, ple_gate_b=dbg)

    r_1, r_2, r_pe, r_g = _xchg_wait(early, False, grad_x, "scatter_early_wait")
    (r_out,) = _xchg_wait(mid, False, grad_x, "scatter_mid_wait")
    (r_small,) = _xchg([_pack(small_g)], True, "gather_small_grads")

    res = {}

    def adam_big(name, parts):
        w, m, v = big[name]
        res[name] = [t[None] for t in _adam(parts, w[0], m[0], v[0], "adam_" + name)]

    for name, parts in (("w_out", r_out), ("w_ff1", r_1), ("w_ff2", r_2), ("ple_w", r_pe), ("ple_gate_w", r_g)):
        adam_big(name, parts)
    (r_in,) = _xchg_wait(late, False, res["w_ff1"][0], "scatter_late_wait")
    adam_big("w_in", r_in)
    shapes = {k: a.shape for k, a in small_w.items()}
    small = [_unpack(t, shapes) for t in _adam(r_small, _pack(small_w), _pack(small_m), _pack(small_v), "adam_small")]
    for name, _ in _SMALL:
        res[name] = [s[name] for s in small]

    order = ["w_in", "f_bias", "sg_ln_g", "sg_ln_b", "sg_w", "sg_b", "att_out_g", "sg_out_g", "w_out", "pre_mix_g",
             "post_mix_g", "pre_ffn_g", "post_ffn_g", "w_ff1", "w_ff2", "ple_w", "ple_gate_w", "ple_gate_b"]
    outs = [loss, grad_x[None]]
    for kind in range(4):
        outs += [res[name][kind] for name in order]
    return tuple(outs)
```

```python
import jax
import jax.numpy as jnp
from jax import lax
from jax.experimental import pallas as pl
from jax.experimental.pallas import tpu as pltpu

F32 = jnp.float32
BF16 = jnp.bfloat16

NDEV = 8
D = 1024
AW = 512
SW = 512
HD = 64
CH = 128
DFF = 4096
PLE = 256
NQKV = 3 * AW
NUS = 2 * SW
FPAD = 128
ZP = NQKV + NUS + FPAD
W_IN_COLS = 2568
EPS = 1e-6
MASKV = -1e30
GELU_K = 0.7978845608028654
GELU_C = 0.044715

ADAM_LR = 0.001
ADAM_B1 = 0.9
ADAM_B2 = 0.999
ADAM_EPS = 1e-08
ADAM_WD = 0.01
ADAM_STEP = 10

VMEM_LIMIT = 48 * 1024 * 1024
VMEM_LIMIT_BIG = 60 * 1024 * 1024


def _nn(a, b):
    return jnp.dot(a, b, preferred_element_type=F32)


def _nt(a, b):
    return lax.dot_general(a, b, (((1,), (1,)), ((), ())), preferred_element_type=F32)


def _tn(a, b):
    return lax.dot_general(a, b, (((0,), (0,)), ((), ())), preferred_element_type=F32)


def _tile(n, pref):
    return min(n, pref)


def _params(n_axes, vmem=VMEM_LIMIT):
    return pltpu.CompilerParams(dimension_semantics=("arbitrary",) * n_axes, vmem_limit_bytes=vmem)


def _full(shape):
    nd = len(shape)
    return pl.BlockSpec(shape, lambda *_: (0,) * nd)


def _rms_fwd(x, g):
    r = lax.rsqrt(jnp.mean(x * x, axis=-1, keepdims=True) + EPS)
    return x * r * g


def _rms_bwd(dy, x, g):
    n = x.shape[-1]
    r = lax.rsqrt(jnp.mean(x * x, axis=-1, keepdims=True) + EPS)
    u = dy * g
    s = jnp.sum(x * u, axis=-1, keepdims=True)
    dx = r * u - x * (r * r * r * (s * (1.0 / n)))
    dg = jnp.sum(dy * (x * r), axis=0, keepdims=True)
    return dx, dg


def _gelu(x):
    t = jnp.tanh(GELU_K * (x + GELU_C * (x * x * x)))
    return x * (0.5 * (1.0 + t)), t


def _gelu_grad(x, t):
    return 0.5 * (1.0 + t) + 0.5 * x * (1.0 - t * t) * (GELU_K * (1.0 + 3.0 * GELU_C * x * x))


def _xchg(arrs, gather, name):
    n = len(arrs)
    out_shapes = [jax.ShapeDtypeStruct(((NDEV,) + a.shape) if gather else a.shape, a.dtype) for a in arrs]

    def body(*refs):
        ins, outs = refs[:n], refs[n:2 * n]
        send, recv, loc = refs[2 * n:]
        x, y, c = lax.axis_index("x"), lax.axis_index("y"), lax.axis_index("c")
        me = 4 * x + 2 * y + c
        local = []
        for a in range(n):
            cp = pltpu.make_async_copy(ins[a] if gather else ins[a].at[me], outs[a].at[me], loc.at[a])
            cp.start()
            local.append(cp)
        remote = []
        for k in range(1, NDEV):
            px = 1 - x if (k >> 2) & 1 else x
            py = 1 - y if (k >> 1) & 1 else y
            pc = 1 - c if k & 1 else c
            peer = 4 * px + 2 * py + pc
            for a in range(n):
                cp = pltpu.make_async_remote_copy(
                    src_ref=ins[a] if gather else ins[a].at[peer],
                    dst_ref=outs[a].at[me],
                    send_sem=send.at[a, k - 1],
                    recv_sem=recv.at[a, k - 1],
                    device_id=(px, py, pc),
                    device_id_type=pl.DeviceIdType.MESH,
                )
                cp.start()
                remote.append(cp)
        for cp in remote:
            cp.wait_send()
            cp.wait_recv()
        for cp in local:
            cp.wait()

    hbm = pl.BlockSpec(memory_space=pltpu.HBM)
    return pl.pallas_call(
        body,
        name=name,
        out_shape=out_shapes,
        in_specs=[hbm] * n,
        out_specs=[hbm] * n,
        scratch_shapes=[
            pltpu.SemaphoreType.DMA((n, NDEV - 1)),
            pltpu.SemaphoreType.DMA((n, NDEV - 1)),
            pltpu.SemaphoreType.DMA((n,)),
        ],
    )(*arrs)


def _peers(x, y, c):
    out = []
    for k in range(1, NDEV):
        out.append((1 - x if (k >> 2) & 1 else x, 1 - y if (k >> 1) & 1 else y, 1 - c if k & 1 else c))
    return out


def _xchg_start(arrs, gather, name):
    n = len(arrs)
    me = 4 * lax.axis_index("x") + 2 * lax.axis_index("y") + lax.axis_index("c")
    lands = []
    for a in arrs:
        shape = ((NDEV,) + a.shape) if gather else a.shape
        own = a[None] if gather else lax.dynamic_slice_in_dim(a, me, 1, axis=0)
        lands.append(lax.dynamic_update_slice_in_dim(lax.empty(shape, a.dtype), own, me, axis=0))

    def body(*refs):
        ins, lnd = refs[:n], refs[n:2 * n]
        send, recv, token = refs[2 * n:3 * n], refs[3 * n:4 * n], refs[-1]
        x, y, c = lax.axis_index("x"), lax.axis_index("y"), lax.axis_index("c")
        mine = 4 * x + 2 * y + c
        for px, py, pc in _peers(x, y, c):
            peer = 4 * px + 2 * py + pc
            for a in range(n):
                pltpu.make_async_remote_copy(
                    src_ref=ins[a] if gather else ins[a].at[peer],
                    dst_ref=lnd[a].at[mine],
                    send_sem=send[a],
                    recv_sem=recv[a],
                    device_id=(px, py, pc),
                    device_id_type=pl.DeviceIdType.MESH,
                ).start()
        token[...] = jnp.zeros_like(token)

    hbm = pl.BlockSpec(memory_space=pltpu.HBM)
    sem = pl.BlockSpec(memory_space=pltpu.SEMAPHORE)
    res = pl.pallas_call(
        body,
        name=name,
        out_shape=(*[pltpu.SemaphoreType.DMA(())] * (2 * n),
                   *[pltpu.HBM(a.shape, a.dtype) for a in arrs], *[pltpu.HBM(l.shape, l.dtype) for l in lands],
                   jax.ShapeDtypeStruct((8, 128), F32)),
        in_specs=[hbm] * (2 * n),
        out_specs=(*([sem] * (2 * n)), *([hbm] * (2 * n)), pl.BlockSpec(memory_space=pltpu.VMEM)),
        input_output_aliases={i: 2 * n + i for i in range(2 * n)},
        compiler_params=pltpu.CompilerParams(has_side_effects=pltpu.SideEffectType.DATAFLOW_SIDE_EFFECTING),
    )(*[pltpu.with_memory_space_constraint(a, pltpu.HBM) for a in arrs],
      *[pltpu.with_memory_space_constraint(l, pltpu.HBM) for l in lands])
    return list(res[0:n]), list(res[n:2 * n]), list(res[2 * n:3 * n]), list(res[3 * n:4 * n]), res[-1]


def _xchg_wait(started, gather, after, name):
    send, recv, srcs, lands, _ = started
    n = len(srcs)

    def body(*refs):
        lnd = refs[n:2 * n]
        send, recv = refs[2 * n:3 * n], refs[3 * n:4 * n]
        me = (lax.axis_index("x"), lax.axis_index("y"), lax.axis_index("c"))
        for a in range(n):
            seven = lnd[a].at[pl.ds(0, NDEV - 1)]
            cp = pltpu.make_async_remote_copy(src_ref=seven, dst_ref=seven, send_sem=send[a], recv_sem=recv[a],
                                              device_id=me, device_id_type=pl.DeviceIdType.MESH)
            cp.wait_send()
            cp.wait_recv()

    hbm = pl.BlockSpec(memory_space=pltpu.HBM)
    sem = pl.BlockSpec(memory_space=pltpu.SEMAPHORE)
    res = pl.pallas_call(
        body,
        name=name,
        out_shape=tuple([pltpu.HBM(a.shape, a.dtype) for a in srcs] + [pltpu.HBM(l.shape, l.dtype) for l in lands]),
        in_specs=[hbm] * (2 * n) + [sem] * (2 * n) + [pl.BlockSpec(memory_space=pl.ANY)],
        out_specs=tuple([hbm] * (2 * n)),
        input_output_aliases={i: i for i in range(2 * n)},
        compiler_params=pltpu.CompilerParams(has_side_effects=pltpu.SideEffectType.DATAFLOW_SIDE_EFFECTING),
    )(*srcs, *lands, *send, *recv, after)
    return list(res[n:])


def _inproj(x, g, wp):
    T = x.shape[0]
    tm = _tile(T, 512)

    def body(x_ref, g_ref, w_ref, qkv_ref, kt_ref, vt_ref, us_ref, fz_ref, ab_ref):
        a = _rms_fwd(x_ref[...], g_ref[...]).astype(BF16)
        ab_ref[...] = a
        qkv_ref[:, 0:AW] = _nn(a, w_ref[:, 0:AW]).astype(BF16)
        kk = _nn(a, w_ref[:, AW:2 * AW])
        qkv_ref[:, AW:2 * AW] = kk.astype(BF16)
        kt_ref[...] = kk.T.astype(BF16)
        vv = _nn(a, w_ref[:, 2 * AW:NQKV])
        qkv_ref[:, 2 * AW:NQKV] = vv.astype(BF16)
        vt_ref[...] = vv.T.astype(BF16)
        us_ref[...] = _nn(a, w_ref[:, NQKV:NQKV + NUS])
        fz_ref[...] = _nn(a, w_ref[:, NQKV + NUS:ZP])

    row = lambda n: pl.BlockSpec((tm, n), lambda i: (i, 0))
    col = pl.BlockSpec((AW, tm), lambda i: (0, i))
    return pl.pallas_call(
        body,
        name="inproj",
        grid=(T // tm,),
        in_specs=[row(D), _full((1, D)), _full((D, ZP))],
        out_specs=[row(NQKV), col, col, row(NUS), row(FPAD), row(D)],
        out_shape=[
            jax.ShapeDtypeStruct((T, NQKV), BF16),
            jax.ShapeDtypeStruct((AW, T), BF16),
            jax.ShapeDtypeStruct((AW, T), BF16),
            jax.ShapeDtypeStruct((T, NUS), F32),
            jax.ShapeDtypeStruct((T, FPAD), F32),
            jax.ShapeDtypeStruct((T, D), BF16),
        ],
        compiler_params=_params(1),
    )(x, g, wp)


def _log_sigmoid(z):
    return jnp.minimum(z, 0.0) - jnp.log1p(jnp.exp(-jnp.abs(z)))


def _split3(x):
    hi = x.astype(BF16)
    r1 = x - hi.astype(F32)
    mid = r1.astype(BF16)
    lo = (r1 - mid.astype(F32)).astype(BF16)
    return hi, mid, lo


AUG_A, AUG_B, AUG_ONE = 0, 3, 6


def _aug_lanes(rows, first_one, pieces):
    lane = lax.broadcasted_iota(jnp.int32, (rows, 128), 1)
    out = jnp.zeros((rows, 128), F32)
    if first_one is not None:
        out = jnp.where((lane >= first_one) & (lane < first_one + 3), 1.0, out)
    for n, piece in enumerate(pieces):
        out = jnp.where(lane == AUG_ONE + n, piece.astype(F32), out)
    return out.astype(BF16)


def _fcum(fz, fb, qkv):
    T = fz.shape[0]
    tb = _tile(T, 512)

    def body(fz_ref, fb_ref, k_ref, v_ref, ct_ref, kk_ref, vk_ref, carry):
        @pl.when(pl.program_id(0) == 0)
        def _():
            carry[...] = jnp.zeros_like(carry)

        lf = _log_sigmoid(fz_ref[...] + fb_ref[...])
        r = lax.broadcasted_iota(jnp.int32, (tb, tb), 0)
        cc = lax.broadcasted_iota(jnp.int32, (tb, tb), 1)
        tri = (cc <= r).astype(F32)
        cs = jnp.dot(tri, lf, precision=lax.Precision.HIGHEST, preferred_element_type=F32) + carry[...]
        ct_ref[...] = cs.T[0:8, :]
        carry[...] = carry[...] + jnp.sum(lf, axis=0, keepdims=True)

        pieces = _split3(cs)
        src = lax.broadcasted_iota(jnp.int32, (128, 128), 0)
        dst = lax.broadcasted_iota(jnp.int32, (128, 128), 1)
        ones = jnp.where((dst[0:1, :] >= AUG_ONE) & (dst[0:1, :] < AUG_ONE + 3), 1.0, 0.0)
        for hp in range(4):
            aug = jnp.zeros((tb, 128), F32) + ones
            for n, piece in enumerate(pieces):
                sel = jnp.where(((src == 2 * hp) & (dst == AUG_A + n)) | ((src == 2 * hp + 1) & (dst == AUG_B + n)), -1.0, 0.0)
                aug = aug + _nn(piece, sel.astype(BF16))
            aug = aug.astype(BF16)
            kk_ref[:, hp * 256:hp * 256 + 128] = k_ref[:, hp * 128:(hp + 1) * 128]
            kk_ref[:, hp * 256 + 128:(hp + 1) * 256] = aug
            vk_ref[:, hp * 256:hp * 256 + 128] = v_ref[:, hp * 128:(hp + 1) * 128]
            vk_ref[:, hp * 256 + 128:(hp + 1) * 256] = aug

    row = lambda n, c: pl.BlockSpec((tb, n), lambda i: (i, c))
    return pl.pallas_call(
        body,
        name="fcum",
        grid=(T // tb,),
        in_specs=[row(FPAD, 0), _full((1, FPAD)), row(AW, 1), row(AW, 2)],
        out_specs=[pl.BlockSpec((8, tb), lambda i: (0, i)), row(2 * AW, 0), row(2 * AW, 0)],
        out_shape=[jax.ShapeDtypeStruct((8, T), F32), jax.ShapeDtypeStruct((T, 2 * AW), BF16),
                   jax.ShapeDtypeStruct((T, 2 * AW), BF16)],
        scratch_shapes=[pltpu.VMEM((1, FPAD), F32)],
        compiler_params=_params(1),
    )(fz, fb, qkv, qkv)


def _fold_rows(r, t, nq):
    if nq == 1:
        return r, t
    low = t <= r
    return jnp.where(low, r, nq - 1 - r), jnp.where(low, t, t - r - 1)


def _fold_cols(r, t, nq):
    if nq == 1:
        return r, t
    first = t < nq - r
    j = jnp.where(first, r, nq - 1 - r)
    return j, jnp.where(first, r + t, nq - 1 - r + (t - (nq - r)))


PAIRS = 4


def _fold_grid(nq):
    assert nq == 1 or nq % 2 == 0
    return (4 // PAIRS, 1, 1) if nq == 1 else (4 // PAIRS, nq // 2, nq + 1)


def _head_rows(x2, hh, scale):
    is_a = lax.broadcasted_iota(jnp.int32, (1, 128), 1) < HD
    keep = is_a if hh == 0 else jnp.logical_not(is_a)
    return jnp.where(keep, x2, jnp.zeros_like(x2)) * scale


def _attn_fwd(qkv, kk, vt, ct):
    T = qkv.shape[0]
    tq = _tile(T, 512)
    tk = tq
    nq = T // tq

    def body(q_ref, kk_ref, vt_ref, ctq_ref, o_ref, lsec_ref, qw_s, m_s, l_s, acc_s):
        i, j = _fold_rows(pl.program_id(1), pl.program_id(2), nq)
        sub8 = lax.broadcasted_iota(jnp.int32, (8, 1), 0)

        def cref_of(pp, hh):
            head = 2 * (PAIRS * pl.program_id(0) + pp) + hh
            return jnp.sum(jnp.where(sub8 == head, ctq_ref[:, 0:1], 0.0), axis=0, keepdims=True)

        @pl.when(j == 0)
        def _():
            for pp in range(PAIRS):
                q2 = q_ref[:, pp * 128:(pp + 1) * 128]
                for hh in range(2):
                    rows = slice(hh * tq, (hh + 1) * tq)
                    qw_s[pp, rows, 0:128] = _head_rows(q2, hh, 0.125)
                    qw_s[pp, rows, 128:256] = _aug_lanes(tq, AUG_A if hh == 0 else AUG_B, _split3(cref_of(pp, hh)))
            m_s[...] = jnp.full_like(m_s, MASKV)
            l_s[...] = jnp.zeros_like(l_s)
            acc_s[...] = jnp.zeros_like(acc_s)

        def step(masked):
            for pp in range(PAIRS):
                s2 = _nt(kk_ref[:, pp * 256:(pp + 1) * 256], qw_s[pp])
                vt2 = vt_ref[pp * 128:(pp + 1) * 128, :]
                for hh in range(2):
                    n = 2 * pp + hh
                    s = s2[:, hh * tq:(hh + 1) * tq]
                    if masked:
                        key = lax.broadcasted_iota(jnp.int32, (tk, tq), 0)
                        qry = lax.broadcasted_iota(jnp.int32, (tk, tq), 1)
                        s = jnp.where(key <= qry, s, MASKV)
                    m_prev = m_s[n]
                    m_new = jnp.maximum(m_prev, jnp.max(s, axis=0, keepdims=True))
                    pr = jnp.exp(s - m_new)
                    alpha = jnp.exp(m_prev - m_new)
                    l_s[n] = alpha * l_s[n] + jnp.sum(pr, axis=0, keepdims=True)
                    m_s[n] = m_new
                    acc_s[n] = alpha * acc_s[n] + _nn(vt2, pr.astype(BF16))

        @pl.when(j < i)
        def _():
            step(False)

        @pl.when(j == i)
        def _():
            step(True)
            sub = lax.broadcasted_iota(jnp.int32, (128, 1), 0)
            for pp in range(PAIRS):
                a, b = 2 * pp, 2 * pp + 1
                ot = jnp.where(sub < HD, acc_s[a] * (1.0 / l_s[a]), acc_s[b] * (1.0 / l_s[b]))
                o_ref[:, pp * 128:(pp + 1) * 128] = ot.T
                lrow = [m_s[2 * pp + hh] + jnp.log(l_s[2 * pp + hh]) - cref_of(pp, hh) for hh in range(2)]
                lsec_ref[:, pp * 128:(pp + 1) * 128] = jnp.where(sub == 0, lrow[0], jnp.where(sub == 1, lrow[1], 0.0)).T

    qi = lambda r, t: _fold_rows(r, t, nq)[0]
    kj = lambda r, t: _fold_rows(r, t, nq)[1]
    return pl.pallas_call(
        body,
        name="attn_fwd",
        grid=_fold_grid(nq),
        in_specs=[
            pl.BlockSpec((tq, PAIRS * 128), lambda g, r, t: (qi(r, t), g)),
            pl.BlockSpec((tk, PAIRS * 256), lambda g, r, t: (kj(r, t), g)),
            pl.BlockSpec((PAIRS * 128, tk), lambda g, r, t: (g, kj(r, t))),
            pl.BlockSpec((8, tq), lambda g, r, t: (0, qi(r, t))),
        ],
        out_specs=[pl.BlockSpec((tq, PAIRS * 128), lambda g, r, t: (qi(r, t), g))] * 2,
        out_shape=[jax.ShapeDtypeStruct((T, AW), F32)] * 2,
        scratch_shapes=[pltpu.VMEM((PAIRS, 2 * tq, 256), BF16), pltpu.VMEM((2 * PAIRS, 1, tq), F32),
                        pltpu.VMEM((2 * PAIRS, 1, tq), F32), pltpu.VMEM((2 * PAIRS, 128, tq), F32)],
        compiler_params=_params(3),
    )(qkv, kk, vt, ct)


def _sgu_forward(us_ref, lng, lnb, w_ref, bt_ref, mixed_s, vnb_s, tm):
    is_a = lax.broadcasted_iota(jnp.int32, (1, 128), 1) < HD
    u = us_ref[:, 0:SW]
    vs = us_ref[:, SW:NUS]
    ug, tu = _gelu(u)
    vg, tv = _gelu(vs)
    mu = jnp.mean(vg, axis=-1, keepdims=True)
    xc = vg - mu
    rstd = lax.rsqrt(jnp.mean(xc * xc, axis=-1, keepdims=True) + EPS)
    vhat = xc * rstd
    vnb_s[...] = (vhat * lng + lnb).astype(BF16)
    rr = lax.broadcasted_iota(jnp.int32, (CH, CH), 0)
    cc = lax.broadcasted_iota(jnp.int32, (CH, CH), 1)
    tril = cc <= rr
    for jj in range(4):
        wa = jnp.where(tril, w_ref[2 * jj], 0.0).astype(BF16)
        wb = jnp.where(tril, w_ref[2 * jj + 1], 0.0).astype(BF16)
        ba = bt_ref[:, 2 * jj:2 * jj + 1]
        bb = bt_ref[:, 2 * jj + 1:2 * jj + 2]
        for ch in range(tm // CH):
            rs, cs = slice(ch * CH, (ch + 1) * CH), slice(jj * 128, (jj + 1) * 128)
            vn2 = vnb_s[rs, cs]
            mixed_s[rs, cs] = jnp.where(is_a, _nn(wa, vn2) + ba, _nn(wb, vn2) + bb)
    mixed = mixed_s[...]
    return u, vs, ug, tu, tv, vhat, rstd, mixed, ug * mixed


def _sgu_out(us, yatt, x, lng, lnb, sgw, sgbt, gatt, gsg, wout, gpm):
    T = us.shape[0]
    tm = _tile(T, 256)

    def body(us_ref, ya_ref, x_ref, lng_ref, lnb_ref, w_ref, bt_ref, ga_ref, gs_ref, wo_ref, gp_ref,
             h1_ref, yb_ref, o_ref, mixed_s, vnb_s):
        ysg = _sgu_forward(us_ref, lng_ref[...], lnb_ref[...], w_ref, bt_ref, mixed_s, vnb_s, tm)[-1]
        yb_ref[:, 0:AW] = _rms_fwd(ya_ref[...], ga_ref[...]).astype(BF16)
        yb_ref[:, AW:D] = _rms_fwd(ysg, gs_ref[...]).astype(BF16)
        o = _nn(yb_ref[...], wo_ref[...])
        o_ref[...] = o
        h1_ref[...] = x_ref[...] + _rms_fwd(o, gp_ref[...])

    row = lambda n: pl.BlockSpec((tm, n), lambda i: (i, 0))
    return pl.pallas_call(
        body,
        name="sgu_out",
        grid=(T // tm,),
        in_specs=[row(NUS), row(AW), row(D), _full((1, SW)), _full((1, SW)), _full((8, CH, CH)), _full((CH, 8)),
                  _full((1, AW)), _full((1, SW)), _full((D, D)), _full((1, D))],
        out_specs=[row(D), row(D), row(D)],
        out_shape=[jax.ShapeDtypeStruct((T, D), F32), jax.ShapeDtypeStruct((T, D), BF16), jax.ShapeDtypeStruct((T, D), F32)],
        scratch_shapes=[pltpu.VMEM((tm, SW), F32), pltpu.VMEM((tm, SW), BF16)],
        compiler_params=_params(1),
    )(us, yatt, x, lng, lnb, sgw, sgbt, gatt, gsg, wout, gpm)


def _ffn_fwd(h1, gpre, w1g, w2g, gpost):
    T = h1.shape[0]
    tm = _tile(T, 512)
    nb, hb = w1g.shape[0], w1g.shape[2]

    def body(h1_ref, gpre_ref, w1_ref, w2_ref, gpost_ref, c2_ref, f1_ref, ff_ref, h2_ref, h2b_ref):
        h1 = h1_ref[...]
        c2 = _rms_fwd(h1, gpre_ref[...]).astype(BF16)
        c2_ref[...] = c2
        ff = jnp.zeros((tm, D), F32)
        for j in range(nb):
            f1 = _nn(c2, w1_ref[j])
            f1_ref[:, j * hb:(j + 1) * hb] = f1.astype(BF16)
            r = jnp.maximum(f1, 0.0)
            ff = ff + _nn((r * r).astype(BF16), w2_ref[j])
        ff_ref[...] = ff
        h2 = h1 + _rms_fwd(ff, gpost_ref[...])
        h2_ref[...] = h2
        h2b_ref[...] = h2.astype(BF16)

    row = lambda n: pl.BlockSpec((tm, n), lambda i: (i, 0))
    once = lambda shape: pl.BlockSpec(shape, lambda i: (0,) * len(shape), pipeline_mode=pl.Buffered(1))
    return pl.pallas_call(
        body,
        name="ffn_fwd",
        grid=(T // tm,),
        in_specs=[row(D), _full((1, D)), once((nb, D, hb)), once((nb, hb, D)), _full((1, D))],
        out_specs=[row(D), row(DFF), row(D), row(D), row(D)],
        out_shape=[jax.ShapeDtypeStruct((T, D), BF16), jax.ShapeDtypeStruct((T, DFF), BF16),
                   jax.ShapeDtypeStruct((T, D), F32), jax.ShapeDtypeStruct((T, D), F32),
                   jax.ShapeDtypeStruct((T, D), BF16)],
        compiler_params=_params(1, VMEM_LIMIT_BIG),
    )(h1, gpre, w1g, w2g, gpost)


def _ple_loss(h2, p, tgt, wg, bg, wpe):
    T = h2.shape[0]
    tm = _tile(T, 512)

    def body(h2_ref, p_ref, t_ref, wg_ref, bg_ref, wpe_ref, dh2_ref, dpre_ref, dpe_ref, pb_ref, dbg_ref, loss_ref):
        @pl.when(pl.program_id(0) == 0)
        def _():
            dbg_ref[...] = jnp.zeros_like(dbg_ref)
            loss_ref[...] = jnp.zeros_like(loss_ref)

        h2 = h2_ref[...]
        gate = jax.nn.sigmoid(_nn(h2.astype(BF16), wg_ref[...]) + bg_ref[...])
        pb = p_ref[...].astype(BF16)
        pb_ref[...] = pb
        pe = _nn(pb, wpe_ref[...])
        diff = (h2 + gate * pe) - t_ref[...]
        loss_ref[...] += jnp.sum(diff * diff)
        dh3 = diff * (1.0 / D)
        dpre = (dh3 * pe) * (gate * (1.0 - gate))
        dpre_b = dpre.astype(BF16)
        dpre_ref[...] = dpre_b
        dpe_ref[...] = (dh3 * gate).astype(BF16)
        dbg_ref[...] += jnp.sum(dpre, axis=0, keepdims=True)
        dh2_ref[...] = dh3 + _nt(dpre_b, wg_ref[...])

    row = lambda n: pl.BlockSpec((tm, n), lambda i: (i, 0))
    return pl.pallas_call(
        body,
        name="ple_loss",
        grid=(T // tm,),
        in_specs=[row(D), row(PLE), row(D), _full((D, D)), _full((1, D)), _full((PLE, D))],
        out_specs=[row(D), row(D), row(D), row(PLE), _full((1, D)), _full((8, 128))],
        out_shape=[jax.ShapeDtypeStruct((T, D), F32), jax.ShapeDtypeStruct((T, D), BF16),
                   jax.ShapeDtypeStruct((T, D), BF16), jax.ShapeDtypeStruct((T, PLE), BF16),
                   jax.ShapeDtypeStruct((1, D), F32), jax.ShapeDtypeStruct((8, 128), F32)],
        compiler_params=_params(1),
    )(h2, p, tgt, wg, bg, wpe)


def _ffn_bwd(dh2, ff, h1, f1, w1g, w2g, gpost, gpre):
    T = dh2.shape[0]
    tm = _tile(T, 512)
    nb, hb = w1g.shape[0], w1g.shape[2]

    def body(dh2_ref, ff_ref, h1_ref, f1_ref, w1_ref, w2_ref, gpost_ref, gpre_ref,
             dffb_ref, df1_ref, dh1_ref, dgpost_ref, dgpre_ref):
        @pl.when(pl.program_id(0) == 0)
        def _():
            dgpost_ref[...] = jnp.zeros_like(dgpost_ref)
            dgpre_ref[...] = jnp.zeros_like(dgpre_ref)

        dh2 = dh2_ref[...]
        dff, dg = _rms_bwd(dh2, ff_ref[...], gpost_ref[...])
        dffb = dff.astype(BF16)
        dffb_ref[...] = dffb
        dgpost_ref[...] += dg
        dc2 = jnp.zeros((tm, D), F32)
        for j in range(nb):
            cols = slice(j * hb, (j + 1) * hb)
            dact = _nt(dffb, w2_ref[j])
            df1 = (dact * (2.0 * jnp.maximum(f1_ref[:, cols].astype(F32), 0.0))).astype(BF16)
            df1_ref[:, cols] = df1
            dc2 = dc2 + _nt(df1, w1_ref[j])
        dx, dg = _rms_bwd(dc2, h1_ref[...], gpre_ref[...])
        dh1_ref[...] = dh2 + dx
        dgpre_ref[...] += dg

    row = lambda n: pl.BlockSpec((tm, n), lambda i: (i, 0))
    once = lambda shape: pl.BlockSpec(shape, lambda i: (0,) * len(shape), pipeline_mode=pl.Buffered(1))
    return pl.pallas_call(
        body,
        name="ffn_bwd",
        grid=(T // tm,),
        in_specs=[row(D), row(D), row(D), row(DFF), once((nb, D, hb)), once((nb, hb, D)), _full((1, D)), _full((1, D))],
        out_specs=[row(D), row(DFF), row(D), _full((1, D)), _full((1, D))],
        out_shape=[jax.ShapeDtypeStruct((T, D), BF16), jax.ShapeDtypeStruct((T, DFF), BF16),
                   jax.ShapeDtypeStruct((T, D), F32), jax.ShapeDtypeStruct((1, D), F32),
                   jax.ShapeDtypeStruct((1, D), F32)],
        compiler_params=_params(1, VMEM_LIMIT_BIG),
    )(dh2, ff, h1, f1, w1g, w2g, gpost, gpre)


def _mix_bwd(dh1, o, us, yatt, lng, lnb, sgw, sgwt, sgbt, gatt, gsg, wout, gpm):
    T = dh1.shape[0]
    tm = _tile(T, 256)

    def body(dh1_ref, o_ref, us_ref, ya_ref, lng_ref, lnb_ref, w_ref, wt_ref, bt_ref, ga_ref, gs_ref, wo_ref, gp_ref,
             dob_ref, dya_ref, dus_ref, dw_ref, dbt_ref, dlng_ref, dlnb_ref, dga_ref, dgs_ref, dgp_ref,
             mixed_s, vnb_s, dvn_s):
        @pl.when(pl.program_id(0) == 0)
        def _():
            for r in (dw_ref, dbt_ref, dlng_ref, dlnb_ref, dga_ref, dgs_ref, dgp_ref):
                r[...] = jnp.zeros_like(r)

        is_a = lax.broadcasted_iota(jnp.int32, (1, 128), 1) < HD
        lane = lax.broadcasted_iota(jnp.int32, (1, 128), 1)
        do, dg = _rms_bwd(dh1_ref[...], o_ref[...], gp_ref[...])
        dgp_ref[...] += dg
        dob = do.astype(BF16)
        dob_ref[...] = dob
        dy = _nt(dob, wo_ref[...])
        datt, dg = _rms_bwd(dy[:, 0:AW], ya_ref[...], ga_ref[...])
        dga_ref[...] += dg
        dya_ref[...] = datt

        lng = lng_ref[...]
        u, vs, ug, tu, tv, vhat, rstd, mixed, ysg = _sgu_forward(us_ref, lng, lnb_ref[...], w_ref, bt_ref, mixed_s, vnb_s, tm)
        dysg, dg = _rms_bwd(dy[:, AW:D], ysg, gs_ref[...])
        dgs_ref[...] += dg
        dus_ref[:, 0:SW] = ((dysg * mixed) * _gelu_grad(u, tu)).astype(BF16)
        dmix = dysg * ug

        rr = lax.broadcasted_iota(jnp.int32, (CH, CH), 0)
        cc = lax.broadcasted_iota(jnp.int32, (CH, CH), 1)
        tril = cc <= rr
        triu = cc >= rr
        for jj in range(4):
            wta = jnp.where(triu, wt_ref[2 * jj], 0.0).astype(BF16)
            wtb = jnp.where(triu, wt_ref[2 * jj + 1], 0.0).astype(BF16)
            for ch in range(tm // CH):
                rs, cs = slice(ch * CH, (ch + 1) * CH), slice(jj * 128, (jj + 1) * 128)
                dm2 = dmix[rs, cs]
                dma = jnp.where(is_a, dm2, 0.0)
                dmb = jnp.where(is_a, 0.0, dm2)
                dma_b, dmb_b = dma.astype(BF16), dmb.astype(BF16)
                vn2 = vnb_s[rs, cs]
                dw_ref[2 * jj] += jnp.where(tril, _nt(dma_b, vn2), 0.0)
                dw_ref[2 * jj + 1] += jnp.where(tril, _nt(dmb_b, vn2), 0.0)
                dvn_s[rs, cs] = _nn(wta, dma_b) + _nn(wtb, dmb_b)
                dba = jnp.sum(dma, axis=1, keepdims=True)
                dbb = jnp.sum(dmb, axis=1, keepdims=True)
                dbt_ref[...] += jnp.where(lane == 2 * jj, dba, 0.0) + jnp.where(lane == 2 * jj + 1, dbb, 0.0)

        dvn = dvn_s[...]
        dlng_ref[...] += jnp.sum(dvn * vhat, axis=0, keepdims=True)
        dlnb_ref[...] += jnp.sum(dvn, axis=0, keepdims=True)
        dvh = dvn * lng
        dvg = rstd * (dvh - jnp.mean(dvh, axis=-1, keepdims=True) - vhat * jnp.mean(dvh * vhat, axis=-1, keepdims=True))
        dus_ref[:, SW:NUS] = (dvg * _gelu_grad(vs, tv)).astype(BF16)

    row = lambda n: pl.BlockSpec((tm, n), lambda i: (i, 0))
    return pl.pallas_call(
        body,
        name="mix_bwd",
        grid=(T // tm,),
        in_specs=[row(D), row(D), row(NUS), row(AW), _full((1, SW)), _full((1, SW)), _full((8, CH, CH)), _full((8, CH, CH)),
                  _full((CH, 8)), _full((1, AW)), _full((1, SW)), _full((D, D)), _full((1, D))],
        out_specs=[row(D), row(AW), row(NUS), _full((8, CH, CH)), _full((CH, 128)), _full((1, SW)), _full((1, SW)),
                   _full((1, AW)), _full((1, SW)), _full((1, D))],
        out_shape=[jax.ShapeDtypeStruct((T, D), BF16), jax.ShapeDtypeStruct((T, AW), F32), jax.ShapeDtypeStruct((T, NUS), BF16),
                   jax.ShapeDtypeStruct((8, CH, CH), F32), jax.ShapeDtypeStruct((CH, 128), F32),
                   jax.ShapeDtypeStruct((1, SW), F32), jax.ShapeDtypeStruct((1, SW), F32),
                   jax.ShapeDtypeStruct((1, AW), F32), jax.ShapeDtypeStruct((1, SW), F32), jax.ShapeDtypeStruct((1, D), F32)],
        scratch_shapes=[pltpu.VMEM((tm, SW), F32), pltpu.VMEM((tm, SW), BF16), pltpu.VMEM((tm, SW), F32)],
        compiler_params=_params(1),
    )(dh1, o, us, yatt, lng, lnb, sgw, sgwt, sgbt, gatt, gsg, wout, gpm)


def _attn_prep(qkv, dya, yatt, lsec, after):
    T = dya.shape[0]
    tm = _tile(T, 512)

    def body(q_ref, do_ref, o_ref, l_ref, after_ref, qw_ref, dow_ref):
        do = do_ref[...]
        feat = lax.broadcasted_iota(jnp.int32, (AW, 128), 0)
        head = lax.broadcasted_iota(jnp.int32, (AW, 128), 1)
        sel = jnp.where((feat >= head * HD) & (feat < (head + 1) * HD), 1.0, 0.0)
        delta = jnp.dot(do * o_ref[...], sel, precision=lax.Precision.HIGHEST, preferred_element_type=F32)
        for hp in range(4):
            cols = slice(hp * 128, (hp + 1) * 128)
            for hh in range(2):
                base = (2 * hp + hh) * 256
                lc = l_ref[:, hp * 128 + hh:hp * 128 + hh + 1]
                d_h = delta[:, 2 * hp + hh:2 * hp + hh + 1]
                qw_ref[:, base:base + 128] = _head_rows(q_ref[:, cols], hh, 0.125)
                qw_ref[:, base + 128:base + 256] = _aug_lanes(tm, AUG_A if hh == 0 else AUG_B, _split3(-lc))
                dow_ref[:, base:base + 128] = _head_rows(do[:, cols], hh, 1.0).astype(BF16)
                dow_ref[:, base + 128:base + 256] = _aug_lanes(tm, None, _split3(-d_h))

    row = lambda n: pl.BlockSpec((tm, n), lambda i: (i, 0))
    return pl.pallas_call(
        body,
        name="attn_prep",
        grid=(T // tm,),
        in_specs=[row(AW)] * 4 + [_full(after.shape)],
        out_specs=[row(4 * AW)] * 2,
        out_shape=[jax.ShapeDtypeStruct((T, 4 * AW), BF16)] * 2,
        compiler_params=_params(1),
    )(qkv, dya, yatt, lsec, after)


def _attn_bwd(kk, vk, kt, qw, dow):
    T = kk.shape[0]
    tq = _tile(T, 512)
    tk = tq
    nq = T // tq

    def body(kk_ref, vk_ref, kt_ref, qw_ref, dow_ref, dqt_ref, dcq_ref, dk_ref, dv_ref, dck_ref, dk_s, dv_s, dck_s):
        j, i = _fold_cols(pl.program_id(1), pl.program_id(2), nq)
        sub8 = lax.broadcasted_iota(jnp.int32, (8, 1), 0)
        lane = lax.broadcasted_iota(jnp.int32, (1, 128), 1)

        @pl.when((pl.program_id(1) == 0) & (pl.program_id(2) == 0))
        def _():
            dqt_ref[...] = jnp.zeros_like(dqt_ref)
            dcq_ref[...] = jnp.zeros_like(dcq_ref)

        @pl.when(i == j)
        def _():
            dk_s[...] = jnp.zeros_like(dk_s)
            dv_s[...] = jnp.zeros_like(dv_s)
            dck_s[...] = jnp.zeros_like(dck_s)

        def step(masked):
            cols = pl.ds(pl.multiple_of(i * tq, tq), tq)
            sub = lax.broadcasted_iota(jnp.int32, (128, 1), 0)
            for pp in range(PAIRS):
                lanes = slice(pp * 128, (pp + 1) * 128)
                kk2, vk2 = kk_ref[:, pp * 256:(pp + 1) * 256], vk_ref[:, pp * 256:(pp + 1) * 256]
                kt2 = kt_ref[lanes, :] * 0.125
                dcq = jnp.zeros((8, tq), F32)
                dck = jnp.zeros((tk, 128), F32)
                dv = jnp.zeros((tk, 128), F32)
                dk = jnp.zeros((tk, 128), F32)
                dqts = []
                for hh in range(2):
                    base = (2 * pp + hh) * 256
                    qw_h = qw_ref[:, base:base + 256]
                    dow_h = dow_ref[:, base:base + 256]
                    pr = jnp.exp(_nt(kk2, qw_h))
                    if masked:
                        key = lax.broadcasted_iota(jnp.int32, (tk, tq), 0)
                        qry = lax.broadcasted_iota(jnp.int32, (tk, tq), 1)
                        pr = jnp.where(key <= qry, pr, 0.0)
                    ds = pr * _nt(vk2, dow_h)
                    ds_b = ds.astype(BF16)
                    dv = dv + _nn(pr.astype(BF16), dow_h[:, 0:128])
                    dk = dk + _nn(ds_b, qw_h[:, 0:128])
                    dqts.append(_nn(kt2, ds_b))
                    dcq = dcq + jnp.where(sub8 == hh, jnp.sum(ds, axis=0, keepdims=True), 0.0)
                    dck = dck - jnp.where(lane == hh, jnp.sum(ds, axis=1, keepdims=True), 0.0)
                dv_s[:, lanes] += dv
                dk_s[:, lanes] += dk
                dck_s[:, lanes] += dck
                dcq_ref[pp, :, cols] += dcq
                dqt_ref[lanes, cols] += jnp.where(sub < HD, dqts[0], dqts[1])

        @pl.when(i > j)
        def _():
            step(False)

        @pl.when(i == j)
        def _():
            step(True)

        @pl.when(i == nq - 1)
        def _():
            dk_ref[...] = dk_s[...].astype(BF16)
            dv_ref[...] = dv_s[...].astype(BF16)
            dck_ref[...] = dck_s[...]

    kj = lambda r, t: _fold_cols(r, t, nq)[0]
    qi = lambda r, t: _fold_cols(r, t, nq)[1]
    krow = lambda g, r, t: (kj(r, t), g)
    qrow = lambda g, r, t: (qi(r, t), g)
    return pl.pallas_call(
        body,
        name="attn_bwd",
        grid=_fold_grid(nq),
        in_specs=[
            pl.BlockSpec((tk, PAIRS * 256), krow),
            pl.BlockSpec((tk, PAIRS * 256), krow),
            pl.BlockSpec((PAIRS * 128, tk), lambda g, r, t: (g, kj(r, t))),
            pl.BlockSpec((tq, PAIRS * 512), qrow),
            pl.BlockSpec((tq, PAIRS * 512), qrow),
        ],
        out_specs=[
            pl.BlockSpec((PAIRS * 128, T), lambda g, r, t: (g, 0), pipeline_mode=pl.Buffered(1)),
            pl.BlockSpec((PAIRS, 8, T), lambda g, r, t: (g, 0, 0), pipeline_mode=pl.Buffered(1)),
            pl.BlockSpec((tk, PAIRS * 128), krow),
            pl.BlockSpec((tk, PAIRS * 128), krow),
            pl.BlockSpec((tk, PAIRS * 128), krow),
        ],
        out_shape=[jax.ShapeDtypeStruct((AW, T), F32), jax.ShapeDtypeStruct((4, 8, T), F32),
                   jax.ShapeDtypeStruct((T, AW), BF16), jax.ShapeDtypeStruct((T, AW), BF16),
                   jax.ShapeDtypeStruct((T, AW), F32)],
        scratch_shapes=[pltpu.VMEM((tk, PAIRS * 128), F32), pltpu.VMEM((tk, PAIRS * 128), F32),
                        pltpu.VMEM((tk, PAIRS * 128), F32)],
        compiler_params=_params(3),
    )(kk, vk, kt, qw, dow)


def _fgate_bwd(dcq, dck, fz, fb):
    T = dck.shape[0]
    tb = _tile(T, 512)
    nb = T // tb

    def body(dcq_ref, dck_ref, fz_ref, fb_ref, df_ref, dfb_ref, carry):
        @pl.when(pl.program_id(0) == 0)
        def _():
            carry[...] = jnp.zeros_like(carry)
            dfb_ref[...] = jnp.zeros_like(dfb_ref)

        dcv = dcq_ref[...] + dck_ref[...]
        r = lax.broadcasted_iota(jnp.int32, (tb, tb), 0)
        cc = lax.broadcasted_iota(jnp.int32, (tb, tb), 1)
        tri = (cc >= r).astype(F32)
        dlf = jnp.dot(tri, dcv, precision=lax.Precision.HIGHEST, preferred_element_type=F32) + carry[...]
        carry[...] = carry[...] + jnp.sum(dcv, axis=0, keepdims=True)
        lane = lax.broadcasted_iota(jnp.int32, (tb, FPAD), 1)
        df = jnp.where(lane < 8, dlf * jax.nn.sigmoid(-(fz_ref[...] + fb_ref[...])), 0.0)
        df_ref[...] = df.astype(BF16)
        dfb_ref[...] += jnp.sum(df, axis=0, keepdims=True)

    rev = pl.BlockSpec((tb, FPAD), lambda i: (nb - 1 - i, 0))
    return pl.pallas_call(
        body,
        name="fgate_bwd",
        grid=(nb,),
        in_specs=[rev, rev, rev, _full((1, FPAD))],
        out_specs=[rev, _full((1, FPAD))],
        out_shape=[jax.ShapeDtypeStruct((T, FPAD), BF16), jax.ShapeDtypeStruct((1, FPAD), F32)],
        scratch_shapes=[pltpu.VMEM((1, FPAD), F32)],
        compiler_params=_params(1),
    )(dcq, dck, fz, fb)


def _inproj_bwd(dq, dk, dv, dus, dfz, wp, x, dh1, g):
    T = x.shape[0]
    tm = _tile(T, 512)

    def body(dq_ref, dk_ref, dv_ref, dus_ref, dfz_ref, w_ref, x_ref, dh1_ref, g_ref, gx_ref, dg_ref):
        @pl.when(pl.program_id(0) == 0)
        def _():
            dg_ref[...] = jnp.zeros_like(dg_ref)

        da = _nt(dq_ref[...].astype(BF16), w_ref[:, 0:AW])
        da += _nt(dk_ref[...], w_ref[:, AW:2 * AW])
        da += _nt(dv_ref[...], w_ref[:, 2 * AW:NQKV])
        da += _nt(dus_ref[...], w_ref[:, NQKV:NQKV + NUS])
        da += _nt(dfz_ref[...], w_ref[:, NQKV + NUS:ZP])
        dx, dg = _rms_bwd(da, x_ref[...], g_ref[...])
        gx_ref[...] = dh1_ref[...] + dx
        dg_ref[...] += dg

    row = lambda n: pl.BlockSpec((tm, n), lambda i: (i, 0))
    return pl.pallas_call(
        body,
        name="inproj_bwd",
        grid=(T // tm,),
        in_specs=[row(AW), row(AW), row(AW), row(NUS), row(FPAD), _full((D, ZP)), row(D), row(D), _full((1, D))],
        out_specs=[row(D), _full((1, D))],
        out_shape=[jax.ShapeDtypeStruct((T, D), F32), jax.ShapeDtypeStruct((1, D), F32)],
        compiler_params=_params(1),
    )(dq, dk, dv, dus, dfz, wp, x, dh1, g)


def _sq_relu(f1):
    r = jnp.maximum(f1.astype(F32), 0.0)
    return (r * r).astype(BF16)


def _wgrad(a, bs, name, a_fn=None):
    T, K = a.shape
    tt = _tile(T, 512)
    nb = len(bs)

    def out_dims(b, layout):
        N = b.shape[1]
        if layout is None:
            return (K, N)
        return (N // layout[1], K, layout[1]) if layout[0] == "col" else (K // layout[1], layout[1], N)

    shapes = [out_dims(b, layout) for b, layout in bs]

    def body(*refs):
        a_ref, b_refs, o_refs = refs[0], refs[1:1 + nb], refs[1 + nb:]

        @pl.when(pl.program_id(0) == 0)
        def _():
            for o_ref in o_refs:
                o_ref[...] = jnp.zeros_like(o_ref)

        av = a_ref[...] if a_fn is None else a_fn(a_ref[...])
        at = av.astype(BF16).T
        for (b, layout), b_ref, o_ref in zip(bs, b_refs, o_refs):
            if layout is None:
                o_ref[...] += _nn(at, b_ref[...].astype(BF16))
            elif layout[0] == "col":
                n = layout[1]
                for k in range(b.shape[1] // n):
                    o_ref[k] += _nn(at, b_ref[:, k * n:(k + 1) * n].astype(BF16))
            else:
                n = layout[1]
                bv = b_ref[...].astype(BF16)
                for k in range(K // n):
                    o_ref[k] += _nn(at[k * n:(k + 1) * n, :], bv)

    once = lambda shape: pl.BlockSpec(shape, lambda t: (0,) * len(shape), pipeline_mode=pl.Buffered(1))
    res = pl.pallas_call(
        body,
        name=name,
        grid=(T // tt,),
        in_specs=[pl.BlockSpec((tt, K), lambda t: (t, 0))] + [pl.BlockSpec((tt, b.shape[1]), lambda t: (t, 0)) for b, _ in bs],
        out_specs=[once(s) for s in shapes],
        out_shape=[jax.ShapeDtypeStruct(s, F32) for s in shapes],
        compiler_params=_params(1, VMEM_LIMIT_BIG),
    )(a, *[b for b, _ in bs])
    return res


def _adam_math(w, g, m, v):
    m = ADAM_B1 * m + (1.0 - ADAM_B1) * g
    v = ADAM_B2 * v + (1.0 - ADAM_B2) * (g * g)
    m_hat = m / (1.0 - ADAM_B1 ** ADAM_STEP)
    v_hat = v / (1.0 - ADAM_B2 ** ADAM_STEP)
    delta = -ADAM_LR * (m_hat / (jnp.sqrt(v_hat) + ADAM_EPS) + ADAM_WD * w)
    return delta, m, v


def _adam(parts, w, m, v, name):
    R, C = w.shape
    br = 128 if R % 128 == 0 else R

    def body(p_ref, w_ref, m_ref, v_ref, g_ref, d_ref, nm_ref, nv_ref):
        g = p_ref[0].astype(F32)
        for s in range(1, NDEV):
            g = g + p_ref[s].astype(F32)
        g_ref[...] = g
        d_ref[...], nm_ref[...], nv_ref[...] = _adam_math(w_ref[...], g, m_ref[...], v_ref[...])

    blk = pl.BlockSpec((br, C), lambda i: (i, 0))
    return pl.pallas_call(
        body,
        name=name,
        grid=(R // br,),
        in_specs=[pl.BlockSpec((NDEV, br, C), lambda i: (0, i, 0)), blk, blk, blk],
        out_specs=[blk] * 4,
        out_shape=[jax.ShapeDtypeStruct((R, C), F32)] * 4,
        compiler_params=_params(1),
    )(parts, w, m, v)


_SMALL = (("sg_w", 8 * CH * CH), ("f_bias", 8), ("sg_ln_g", SW), ("sg_ln_b", SW), ("sg_b", 8 * CH), ("att_out_g", AW),
          ("sg_out_g", SW), ("pre_mix_g", D), ("post_mix_g", D), ("pre_ffn_g", D), ("post_ffn_g", D), ("ple_gate_b", D))
_SEG = 8 * 128


def _seg_rows(size):
    return 8 * (-(-size // _SEG))


def _pack(vals):
    parts = []
    for name, size in _SMALL:
        flat = vals[name].reshape(-1)
        rows = _seg_rows(size)
        parts.append(jnp.pad(flat, (0, rows * 128 - size)).reshape(rows, 128))
    return jnp.concatenate(parts, axis=0)


def _unpack(packed, shapes):
    out, r = {}, 0
    for name, size in _SMALL:
        rows = _seg_rows(size)
        out[name] = packed[r:r + rows].reshape(-1)[:size].reshape(shapes[name])
        r += rows
    return out


def kernel(x, p, w_in, f_bias, sg_ln_g, sg_ln_b, sg_w, sg_b, att_out_g, sg_out_g, w_out, pre_mix_g, post_mix_g, pre_ffn_g, post_ffn_g, w_ff1, w_ff2, ple_w, ple_gate_w, ple_gate_b, loss_target, m_w_in, m_f_bias, m_sg_ln_g, m_sg_ln_b, m_sg_w, m_sg_b, m_att_out_g, m_sg_out_g, m_w_out, m_pre_mix_g, m_post_mix_g, m_pre_ffn_g, m_post_ffn_g, m_w_ff1, m_w_ff2, m_ple_w, m_ple_gate_w, m_ple_gate_b, v_w_in, v_f_bias, v_sg_ln_g, v_sg_ln_b, v_sg_w, v_sg_b, v_att_out_g, v_sg_out_g, v_w_out, v_pre_mix_g, v_post_mix_g, v_pre_ffn_g, v_post_ffn_g, v_w_ff1, v_w_ff2, v_ple_w, v_ple_gate_w, v_ple_gate_b):
    small_w = dict(sg_w=sg_w, f_bias=f_bias, sg_ln_g=sg_ln_g, sg_ln_b=sg_ln_b, sg_b=sg_b, att_out_g=att_out_g,
                   sg_out_g=sg_out_g, pre_mix_g=pre_mix_g, post_mix_g=post_mix_g, pre_ffn_g=pre_ffn_g,
                   post_ffn_g=post_ffn_g, ple_gate_b=ple_gate_b)
    small_m = dict(sg_w=m_sg_w, f_bias=m_f_bias, sg_ln_g=m_sg_ln_g, sg_ln_b=m_sg_ln_b, sg_b=m_sg_b, att_out_g=m_att_out_g,
                   sg_out_g=m_sg_out_g, pre_mix_g=m_pre_mix_g, post_mix_g=m_post_mix_g, pre_ffn_g=m_pre_ffn_g,
                   post_ffn_g=m_post_ffn_g, ple_gate_b=m_ple_gate_b)
    small_v = dict(sg_w=v_sg_w, f_bias=v_f_bias, sg_ln_g=v_sg_ln_g, sg_ln_b=v_sg_ln_b, sg_b=v_sg_b, att_out_g=v_att_out_g,
                   sg_out_g=v_sg_out_g, pre_mix_g=v_pre_mix_g, post_mix_g=v_post_mix_g, pre_ffn_g=v_pre_ffn_g,
                   post_ffn_g=v_post_ffn_g, ple_gate_b=v_ple_gate_b)
    big = dict(w_in=(w_in, m_w_in, v_w_in), w_out=(w_out, m_w_out, v_w_out), w_ff1=(w_ff1, m_w_ff1, v_w_ff1),
               w_ff2=(w_ff2, m_w_ff2, v_w_ff2), ple_w=(ple_w, m_ple_w, v_ple_w),
               ple_gate_w=(ple_gate_w, m_ple_gate_w, v_ple_gate_w))

    xt, pt, tgt = x[0], p[0, 0], loss_target[0]
    ws = W_IN_COLS // NDEV

    (gw_in,) = _xchg([w_in[0].astype(BF16)], True, "gather_w_in")
    rest = _xchg_start([w_out[0].astype(BF16), w_ff1[0].astype(BF16), w_ff2[0].astype(BF16), ple_w[0].astype(BF16),
                        ple_gate_w[0].astype(BF16)], True, "gather_rest_start")
    win = jnp.transpose(gw_in, (1, 0, 2)).reshape(D, W_IN_COLS)
    wp = jnp.concatenate([win[:, 0:NQKV], win[:, NQKV + 8:W_IN_COLS], win[:, NQKV:NQKV + 8],
                          jnp.zeros((D, FPAD - 8), BF16)], axis=1)

    fb = jnp.pad(f_bias.astype(F32), ((0, 0), (0, FPAD - 8)))
    sgw = sg_w[0]
    sgwt = jnp.transpose(sg_w[0], (0, 2, 1))
    sgbt = jnp.transpose(sg_b[0])

    qkv, kt, vt, us, fz, ab = _inproj(xt, pre_mix_g + rest[4][0, 0], wp)
    ct, kk, vk = _fcum(fz, fb, qkv)
    yatt, lsec = _attn_fwd(qkv, kk, vt, ct)
    gw_out, gw1, gw2, gwpe, gwg = _xchg_wait(rest, True, yatt, "gather_rest_wait")
    wout = gw_out.reshape(D, D)
    wg = gwg.reshape(D, D)
    wpe = jnp.transpose(gwpe, (1, 0, 2)).reshape(PLE, D)
    h1, yb, o = _sgu_out(us, yatt, xt, sg_ln_g, sg_ln_b, sgw, sgbt, att_out_g, sg_out_g, wout, post_mix_g)
    c2b, f1b, ff, h2, h2b = _ffn_fwd(h1, pre_ffn_g, gw1, gw2, post_ffn_g)
    dh2, dpre, dpe, pb, dbg, loss_acc = _ple_loss(h2, pt, tgt, wg, ple_gate_b, wpe)
    loss = lax.psum(loss_acc[0, 0] * (0.5 / D), ("x", "y", "c"))

    (g_pe,) = _wgrad(pb, [(dpe, ("col", D // NDEV))], "wgrad_ple")
    (g_g,) = _wgrad(h2b, [(dpre, ("row", D // NDEV))], "wgrad_gate")
    dffb, df1, dh1, dgpostffn, dgpreffn = _ffn_bwd(dh2, ff, h1, f1b, gw1, gw2, post_ffn_g, pre_ffn_g)
    (g_1,) = _wgrad(c2b, [(df1, ("col", DFF // NDEV))], "wgrad_ff1")
    (g_2,) = _wgrad(f1b, [(dffb, ("row", DFF // NDEV))], "wgrad_ff2", a_fn=_sq_relu)
    early = _xchg_start([g_1, g_2, g_pe, g_g], False, "scatter_early_start")
    dob, dya, dus, dsgw, dsgbt, dlng, dlnb, dgatt, dgsg, dgpostmix = _mix_bwd(
        dh1, o, us, yatt, sg_ln_g, sg_ln_b, sgw, sgwt, sgbt, att_out_g, sg_out_g, wout, post_mix_g + early[4][0, 0])
    (g_out,) = _wgrad(yb, [(dob, ("row", D // NDEV))], "wgrad_out")
    mid = _xchg_start([g_out], False, "scatter_mid_start")
    qw, dow = _attn_prep(qkv, dya, yatt, lsec, mid[4])
    dqt, dcq, dk, dv, dck = _attn_bwd(kk, vk, kt, qw, dow)
    dq = jnp.transpose(dqt)
    dcq = jnp.pad(jnp.transpose(dcq[:, 0:2, :].reshape(8, -1)), ((0, 0), (0, FPAD - 8)))
    dck = jnp.pad(dck.reshape(-1, 4, 128)[:, :, 0:2].reshape(-1, 8), ((0, 0), (0, FPAD - 8)))
    dfz, dfb = _fgate_bwd(dcq, dck, fz, fb)

    gq, gk, gv, gus, gf = _wgrad(ab, [(dq, None), (dk, None), (dv, None), (dus, None), (dfz, None)], "wgrad_in")
    g_in = jnp.concatenate([gq, gk, gv, gf[:, 0:8], gus], axis=1)
    g_in = jnp.transpose(g_in.reshape(D, NDEV, ws), (1, 0, 2))
    late = _xchg_start([g_in.astype(BF16)], False, "scatter_late_start")
    grad_x, dgpremix = _inproj_bwd(dq, dk, dv, dus, dfz, wp, xt, dh1, pre_mix_g + late[4][0, 0])

    small_g = dict(sg_w=dsgw, f_bias=dfb[:, 0:8], sg_ln_g=dlng, sg_ln_b=dlnb, sg_b=jnp.transpose(dsgbt[:, 0:8]),
                   att_out_g=dgatt, sg_out_g=dgsg, pre_mix_g=dgpremix, post_mix_g=dgpostmix, pre_ffn_g=dgpreffn,
                   post_ffn_g=dgpostffn, ple_gate_b=dbg)

    r_1, r_2, r_pe, r_g = _xchg_wait(early, False, grad_x, "scatter_early_wait")
    (r_out,) = _xchg_wait(mid, False, grad_x, "scatter_mid_wait")
    (r_small,) = _xchg([_pack(small_g)], True, "gather_small_grads")

    res = {}

    def adam_big(name, parts):
        w, m, v = big[name]
        res[name] = [t[None] for t in _adam(parts, w[0], m[0], v[0], "adam_" + name)]

    for name, parts in (("w_out", r_out), ("w_ff1", r_1), ("w_ff2", r_2), ("ple_w", r_pe), ("ple_gate_w", r_g)):
        adam_big(name, parts)
    (r_in,) = _xchg_wait(late, False, res["w_ff1"][0], "scatter_late_wait")
    adam_big("w_in", r_in)
    shapes = {k: a.shape for k, a in small_w.items()}
    small = [_unpack(t, shapes) for t in _adam(r_small, _pack(small_w), _pack(small_m), _pack(small_v), "adam_small")]
    for name, _ in _SMALL:
        res[name] = [s[name] for s in small]

    order = ["w_in", "f_bias", "sg_ln_g", "sg_ln_b", "sg_w", "sg_b", "att_out_g", "sg_out_g", "w_out", "pre_mix_g",
             "post_mix_g", "pre_ffn_g", "post_ffn_g", "w_ff1", "w_ff2", "ple_w", "ple_gate_w", "ple_gate_b"]
    outs = [loss, grad_x[None]]
    for kind in range(4):
        outs += [res[name][kind] for name in order]
    return tuple(outs)
```

```python
import jax
import jax.numpy as jnp
from jax import lax
from jax.experimental import pallas as pl
from jax.experimental.pallas import tpu as pltpu

F32 = jnp.float32
BF16 = jnp.bfloat16

NDEV = 8
D = 1024
AW = 512
SW = 512
HD = 64
CH = 128
DFF = 4096
PLE = 256
NQKV = 3 * AW
NUS = 2 * SW
FPAD = 128
ZP = NQKV + NUS + FPAD
W_IN_COLS = 2568
EPS = 1e-6
MASKV = -1e30
GELU_K = 0.7978845608028654
GELU_C = 0.044715

ADAM_LR = 0.001
ADAM_B1 = 0.9
ADAM_B2 = 0.999
ADAM_EPS = 1e-08
ADAM_WD = 0.01
ADAM_STEP = 10

VMEM_LIMIT = 48 * 1024 * 1024
VMEM_LIMIT_BIG = 60 * 1024 * 1024


def _nn(a, b):
    return jnp.dot(a, b, preferred_element_type=F32)


def _nt(a, b):
    return lax.dot_general(a, b, (((1,), (1,)), ((), ())), preferred_element_type=F32)


def _tn(a, b):
    return lax.dot_general(a, b, (((0,), (0,)), ((), ())), preferred_element_type=F32)


def _tile(n, pref):
    return min(n, pref)


def _params(n_axes, vmem=VMEM_LIMIT):
    return pltpu.CompilerParams(dimension_semantics=("arbitrary",) * n_axes, vmem_limit_bytes=vmem)


def _full(shape):
    nd = len(shape)
    return pl.BlockSpec(shape, lambda *_: (0,) * nd)


def _rms_fwd(x, g):
    r = lax.rsqrt(jnp.mean(x * x, axis=-1, keepdims=True) + EPS)
    return x * r * g


def _rms_bwd(dy, x, g):
    n = x.shape[-1]
    r = lax.rsqrt(jnp.mean(x * x, axis=-1, keepdims=True) + EPS)
    u = dy * g
    s = jnp.sum(x * u, axis=-1, keepdims=True)
    dx = r * u - x * (r * r * r * (s * (1.0 / n)))
    dg = jnp.sum(dy * (x * r), axis=0, keepdims=True)
    return dx, dg


def _gelu(x):
    t = jnp.tanh(GELU_K * (x + GELU_C * (x * x * x)))
    return x * (0.5 * (1.0 + t)), t


def _gelu_grad(x, t):
    return 0.5 * (1.0 + t) + 0.5 * x * (1.0 - t * t) * (GELU_K * (1.0 + 3.0 * GELU_C * x * x))


def _xchg(arrs, gather, name):
    n = len(arrs)
    out_shapes = [jax.ShapeDtypeStruct(((NDEV,) + a.shape) if gather else a.shape, a.dtype) for a in arrs]

    def body(*refs):
        ins, outs = refs[:n], refs[n:2 * n]
        send, recv, loc = refs[2 * n:]
        x, y, c = lax.axis_index("x"), lax.axis_index("y"), lax.axis_index("c")
        me = 4 * x + 2 * y + c
        local = []
        for a in range(n):
            cp = pltpu.make_async_copy(ins[a] if gather else ins[a].at[me], outs[a].at[me], loc.at[a])
            cp.start()
            local.append(cp)
        remote = []
        for k in range(1, NDEV):
            px = 1 - x if (k >> 2) & 1 else x
            py = 1 - y if (k >> 1) & 1 else y
            pc = 1 - c if k & 1 else c
            peer = 4 * px + 2 * py + pc
            for a in range(n):
                cp = pltpu.make_async_remote_copy(
                    src_ref=ins[a] if gather else ins[a].at[peer],
                    dst_ref=outs[a].at[me],
                    send_sem=send.at[a, k - 1],
                    recv_sem=recv.at[a, k - 1],
                    device_id=(px, py, pc),
                    device_id_type=pl.DeviceIdType.MESH,
                )
                cp.start()
                remote.append(cp)
        for cp in remote:
            cp.wait_send()
            cp.wait_recv()
        for cp in local:
            cp.wait()

    hbm = pl.BlockSpec(memory_space=pltpu.HBM)
    return pl.pallas_call(
        body,
        name=name,
        out_shape=out_shapes,
        in_specs=[hbm] * n,
        out_specs=[hbm] * n,
        scratch_shapes=[
            pltpu.SemaphoreType.DMA((n, NDEV - 1)),
            pltpu.SemaphoreType.DMA((n, NDEV - 1)),
            pltpu.SemaphoreType.DMA((n,)),
        ],
    )(*arrs)


def _peers(x, y, c):
    out = []
    for k in range(1, NDEV):
        out.append((1 - x if (k >> 2) & 1 else x, 1 - y if (k >> 1) & 1 else y, 1 - c if k & 1 else c))
    return out


def _xchg_start(arrs, gather, name):
    n = len(arrs)
    me = 4 * lax.axis_index("x") + 2 * lax.axis_index("y") + lax.axis_index("c")
    lands = []
    for a in arrs:
        shape = ((NDEV,) + a.shape) if gather else a.shape
        own = a[None] if gather else lax.dynamic_slice_in_dim(a, me, 1, axis=0)
        lands.append(lax.dynamic_update_slice_in_dim(lax.empty(shape, a.dtype), own, me, axis=0))

    def body(*refs):
        ins, lnd = refs[:n], refs[n:2 * n]
        send, recv, token = refs[2 * n:3 * n], refs[3 * n:4 * n], refs[-1]
        x, y, c = lax.axis_index("x"), lax.axis_index("y"), lax.axis_index("c")
        mine = 4 * x + 2 * y + c
        for px, py, pc in _peers(x, y, c):
            peer = 4 * px + 2 * py + pc
            for a in range(n):
                pltpu.make_async_remote_copy(
                    src_ref=ins[a] if gather else ins[a].at[peer],
                    dst_ref=lnd[a].at[mine],
                    send_sem=send[a],
                    recv_sem=recv[a],
                    device_id=(px, py, pc),
                    device_id_type=pl.DeviceIdType.MESH,
                ).start()
        token[...] = jnp.zeros_like(token)

    hbm = pl.BlockSpec(memory_space=pltpu.HBM)
    sem = pl.BlockSpec(memory_space=pltpu.SEMAPHORE)
    res = pl.pallas_call(
        body,
        name=name,
        out_shape=(*[pltpu.SemaphoreType.DMA(())] * (2 * n),
                   *[pltpu.HBM(a.shape, a.dtype) for a in arrs], *[pltpu.HBM(l.shape, l.dtype) for l in lands],
                   jax.ShapeDtypeStruct((8, 128), F32)),
        in_specs=[hbm] * (2 * n),
        out_specs=(*([sem] * (2 * n)), *([hbm] * (2 * n)), pl.BlockSpec(memory_space=pltpu.VMEM)),
        input_output_aliases={i: 2 * n + i for i in range(2 * n)},
        compiler_params=pltpu.CompilerParams(has_side_effects=pltpu.SideEffectType.DATAFLOW_SIDE_EFFECTING),
    )(*[pltpu.with_memory_space_constraint(a, pltpu.HBM) for a in arrs],
      *[pltpu.with_memory_space_constraint(l, pltpu.HBM) for l in lands])
    return list(res[0:n]), list(res[n:2 * n]), list(res[2 * n:3 * n]), list(res[3 * n:4 * n]), res[-1]


def _xchg_wait(started, gather, after, name):
    send, recv, srcs, lands, _ = started
    n = len(srcs)

    def body(*refs):
        lnd = refs[n:2 * n]
        send, recv = refs[2 * n:3 * n], refs[3 * n:4 * n]
        me = (lax.axis_index("x"), lax.axis_index("y"), lax.axis_index("c"))
        for a in range(n):
            seven = lnd[a].at[pl.ds(0, NDEV - 1)]
            cp = pltpu.make_async_remote_copy(src_ref=seven, dst_ref=seven, send_sem=send[a], recv_sem=recv[a],
                                              device_id=me, device_id_type=pl.DeviceIdType.MESH)
            cp.wait_send()
            cp.wait_recv()

    hbm = pl.BlockSpec(memory_space=pltpu.HBM)
    sem = pl.BlockSpec(memory_space=pltpu.SEMAPHORE)
    res = pl.pallas_call(
        body,
        name=name,
        out_shape=tuple([pltpu.HBM(a.shape, a.dtype) for a in srcs] + [pltpu.HBM(l.shape, l.dtype) for l in lands]),
        in_specs=[hbm] * (2 * n) + [sem] * (2 * n) + [pl.BlockSpec(memory_space=pl.ANY)],
        out_specs=tuple([hbm] * (2 * n)),
        input_output_aliases={i: i for i in range(2 * n)},
        compiler_params=pltpu.CompilerParams(has_side_effects=pltpu.SideEffectType.DATAFLOW_SIDE_EFFECTING),
    )(*srcs, *lands, *send, *recv, after)
    return list(res[n:])


def _inproj(x, g, wp):
    T = x.shape[0]
    tm = _tile(T, 512)

    def body(x_ref, g_ref, w_ref, qkv_ref, kt_ref, vt_ref, us_ref, fz_ref, ab_ref):
        a = _rms_fwd(x_ref[...], g_ref[...]).astype(BF16)
        ab_ref[...] = a
        qkv_ref[:, 0:AW] = _nn(a, w_ref[:, 0:AW]).astype(BF16)
        kk = _nn(a, w_ref[:, AW:2 * AW])
        qkv_ref[:, AW:2 * AW] = kk.astype(BF16)
        kt_ref[...] = kk.T.astype(BF16)
        vv = _nn(a, w_ref[:, 2 * AW:NQKV])
        qkv_ref[:, 2 * AW:NQKV] = vv.astype(BF16)
        vt_ref[...] = vv.T.astype(BF16)
        us_ref[...] = _nn(a, w_ref[:, NQKV:NQKV + NUS])
        fz_ref[...] = _nn(a, w_ref[:, NQKV + NUS:ZP])

    row = lambda n: pl.BlockSpec((tm, n), lambda i: (i, 0))
    col = pl.BlockSpec((AW, tm), lambda i: (0, i))
    return pl.pallas_call(
        body,
        name="inproj",
        grid=(T // tm,),
        in_specs=[row(D), _full((1, D)), _full((D, ZP))],
        out_specs=[row(NQKV), col, col, row(NUS), row(FPAD), row(D)],
        out_shape=[
            jax.ShapeDtypeStruct((T, NQKV), BF16),
            jax.ShapeDtypeStruct((AW, T), BF16),
            jax.ShapeDtypeStruct((AW, T), BF16),
            jax.ShapeDtypeStruct((T, NUS), F32),
            jax.ShapeDtypeStruct((T, FPAD), F32),
            jax.ShapeDtypeStruct((T, D), BF16),
        ],
        compiler_params=_params(1),
    )(x, g, wp)


def _log_sigmoid(z):
    return jnp.minimum(z, 0.0) - jnp.log1p(jnp.exp(-jnp.abs(z)))


def _split3(x):
    hi = x.astype(BF16)
    r1 = x - hi.astype(F32)
    mid = r1.astype(BF16)
    lo = (r1 - mid.astype(F32)).astype(BF16)
    return hi, mid, lo


AUG_A, AUG_B, AUG_ONE = 0, 3, 6


def _aug_lanes(rows, first_one, pieces):
    lane = lax.broadcasted_iota(jnp.int32, (rows, 128), 1)
    out = jnp.zeros((rows, 128), F32)
    if first_one is not None:
        out = jnp.where((lane >= first_one) & (lane < first_one + 3), 1.0, out)
    for n, piece in enumerate(pieces):
        out = jnp.where(lane == AUG_ONE + n, piece.astype(F32), out)
    return out.astype(BF16)


def _fcum(fz, fb, qkv):
    T = fz.shape[0]
    tb = _tile(T, 512)

    def body(fz_ref, fb_ref, k_ref, v_ref, ct_ref, kk_ref, vk_ref, carry):
        @pl.when(pl.program_id(0) == 0)
        def _():
            carry[...] = jnp.zeros_like(carry)

        lf = _log_sigmoid(fz_ref[...] + fb_ref[...])
        r = lax.broadcasted_iota(jnp.int32, (tb, tb), 0)
        cc = lax.broadcasted_iota(jnp.int32, (tb, tb), 1)
        tri = (cc <= r).astype(F32)
        cs = jnp.dot(tri, lf, precision=lax.Precision.HIGHEST, preferred_element_type=F32) + carry[...]
        ct_ref[...] = cs.T[0:8, :]
        carry[...] = carry[...] + jnp.sum(lf, axis=0, keepdims=True)

        pieces = _split3(cs)
        src = lax.broadcasted_iota(jnp.int32, (128, 128), 0)
        dst = lax.broadcasted_iota(jnp.int32, (128, 128), 1)
        ones = jnp.where((dst[0:1, :] >= AUG_ONE) & (dst[0:1, :] < AUG_ONE + 3), 1.0, 0.0)
        for hp in range(4):
            aug = jnp.zeros((tb, 128), F32) + ones
            for n, piece in enumerate(pieces):
                sel = jnp.where(((src == 2 * hp) & (dst == AUG_A + n)) | ((src == 2 * hp + 1) & (dst == AUG_B + n)), -1.0, 0.0)
                aug = aug + _nn(piece, sel.astype(BF16))
            aug = aug.astype(BF16)
            kk_ref[:, hp * 256:hp * 256 + 128] = k_ref[:, hp * 128:(hp + 1) * 128]
            kk_ref[:, hp * 256 + 128:(hp + 1) * 256] = aug
            vk_ref[:, hp * 256:hp * 256 + 128] = v_ref[:, hp * 128:(hp + 1) * 128]
            vk_ref[:, hp * 256 + 128:(hp + 1) * 256] = aug

    row = lambda n, c: pl.BlockSpec((tb, n), lambda i: (i, c))
    return pl.pallas_call(
        body,
        name="fcum",
        grid=(T // tb,),
        in_specs=[row(FPAD, 0), _full((1, FPAD)), row(AW, 1), row(AW, 2)],
        out_specs=[pl.BlockSpec((8, tb), lambda i: (0, i)), row(2 * AW, 0), row(2 * AW, 0)],
        out_shape=[jax.ShapeDtypeStruct((8, T), F32), jax.ShapeDtypeStruct((T, 2 * AW), BF16),
                   jax.ShapeDtypeStruct((T, 2 * AW), BF16)],
        scratch_shapes=[pltpu.VMEM((1, FPAD), F32)],
        compiler_params=_params(1),
    )(fz, fb, qkv, qkv)


def _fold_rows(r, t, nq):
    if nq == 1:
        return r, t
    low = t <= r
    return jnp.where(low, r, nq - 1 - r), jnp.where(low, t, t - r - 1)


def _fold_cols(r, t, nq):
    if nq == 1:
        return r, t
    first = t < nq - r
    j = jnp.where(first, r, nq - 1 - r)
    return j, jnp.where(first, r + t, nq - 1 - r + (t - (nq - r)))


PAIRS = 4


def _fold_grid(nq):
    assert nq == 1 or nq % 2 == 0
    return (4 // PAIRS, 1, 1) if nq == 1 else (4 // PAIRS, nq // 2, nq + 1)


def _head_rows(x2, hh, scale):
    is_a = lax.broadcasted_iota(jnp.int32, (1, 128), 1) < HD
    keep = is_a if hh == 0 else jnp.logical_not(is_a)
    return jnp.where(keep, x2, jnp.zeros_like(x2)) * scale


def _attn_fwd(qkv, kk, vt, ct):
    T = qkv.shape[0]
    tq = _tile(T, 512)
    tk = tq
    nq = T // tq

    def body(q_ref, kk_ref, vt_ref, ctq_ref, o_ref, lsec_ref, qw_s, m_s, l_s, acc_s):
        i, j = _fold_rows(pl.program_id(1), pl.program_id(2), nq)
        sub8 = lax.broadcasted_iota(jnp.int32, (8, 1), 0)

        def cref_of(pp, hh):
            head = 2 * (PAIRS * pl.program_id(0) + pp) + hh
            return jnp.sum(jnp.where(sub8 == head, ctq_ref[:, 0:1], 0.0), axis=0, keepdims=True)

        @pl.when(j == 0)
        def _():
            for pp in range(PAIRS):
                q2 = q_ref[:, pp * 128:(pp + 1) * 128]
                for hh in range(2):
                    rows = slice(hh * tq, (hh + 1) * tq)
                    qw_s[pp, rows, 0:128] = _head_rows(q2, hh, 0.125)
                    qw_s[pp, rows, 128:256] = _aug_lanes(tq, AUG_A if hh == 0 else AUG_B, _split3(cref_of(pp, hh)))
            m_s[...] = jnp.full_like(m_s, MASKV)
            l_s[...] = jnp.zeros_like(l_s)
            acc_s[...] = jnp.zeros_like(acc_s)

        def step(masked):
            for pp in range(PAIRS):
                s2 = _nt(kk_ref[:, pp * 256:(pp + 1) * 256], qw_s[pp])
                vt2 = vt_ref[pp * 128:(pp + 1) * 128, :]
                for hh in range(2):
                    n = 2 * pp + hh
                    s = s2[:, hh * tq:(hh + 1) * tq]
                    if masked:
                        key = lax.broadcasted_iota(jnp.int32, (tk, tq), 0)
                        qry = lax.broadcasted_iota(jnp.int32, (tk, tq), 1)
                        s = jnp.where(key <= qry, s, MASKV)
                    m_prev = m_s[n]
                    m_new = jnp.maximum(m_prev, jnp.max(s, axis=0, keepdims=True))
                    pr = jnp.exp(s - m_new)
                    alpha = jnp.exp(m_prev - m_new)
                    l_s[n] = alpha * l_s[n] + jnp.sum(pr, axis=0, keepdims=True)
                    m_s[n] = m_new
                    acc_s[n] = alpha * acc_s[n] + _nn(vt2, pr.astype(BF16))

        @pl.when(j < i)
        def _():
            step(False)

        @pl.when(j == i)
        def _():
            step(True)
            sub = lax.broadcasted_iota(jnp.int32, (128, 1), 0)
            for pp in range(PAIRS):
                a, b = 2 * pp, 2 * pp + 1
                ot = jnp.where(sub < HD, acc_s[a] * (1.0 / l_s[a]), acc_s[b] * (1.0 / l_s[b]))
                o_ref[:, pp * 128:(pp + 1) * 128] = ot.T
                lrow = [m_s[2 * pp + hh] + jnp.log(l_s[2 * pp + hh]) - cref_of(pp, hh) for hh in range(2)]
                lsec_ref[:, pp * 128:(pp + 1) * 128] = jnp.where(sub == 0, lrow[0], jnp.where(sub == 1, lrow[1], 0.0)).T

    qi = lambda r, t: _fold_rows(r, t, nq)[0]
    kj = lambda r, t: _fold_rows(r, t, nq)[1]
    return pl.pallas_call(
        body,
        name="attn_fwd",
        grid=_fold_grid(nq),
        in_specs=[
            pl.BlockSpec((tq, PAIRS * 128), lambda g, r, t: (qi(r, t), g)),
            pl.BlockSpec((tk, PAIRS * 256), lambda g, r, t: (kj(r, t), g)),
            pl.BlockSpec((PAIRS * 128, tk), lambda g, r, t: (g, kj(r, t))),
            pl.BlockSpec((8, tq), lambda g, r, t: (0, qi(r, t))),
        ],
        out_specs=[pl.BlockSpec((tq, PAIRS * 128), lambda g, r, t: (qi(r, t), g))] * 2,
        out_shape=[jax.ShapeDtypeStruct((T, AW), F32)] * 2,
        scratch_shapes=[pltpu.VMEM((PAIRS, 2 * tq, 256), BF16), pltpu.VMEM((2 * PAIRS, 1, tq), F32),
                        pltpu.VMEM((2 * PAIRS, 1, tq), F32), pltpu.VMEM((2 * PAIRS, 128, tq), F32)],
        compiler_params=_params(3),
    )(qkv, kk, vt, ct)


def _sgu_forward(us_ref, lng, lnb, w_ref, bt_ref, mixed_s, vnb_s, tm):
    is_a = lax.broadcasted_iota(jnp.int32, (1, 128), 1) < HD
    u = us_ref[:, 0:SW]
    vs = us_ref[:, SW:NUS]
    ug, tu = _gelu(u)
    vg, tv = _gelu(vs)
    mu = jnp.mean(vg, axis=-1, keepdims=True)
    xc = vg - mu
    rstd = lax.rsqrt(jnp.mean(xc * xc, axis=-1, keepdims=True) + EPS)
    vhat = xc * rstd
    vnb_s[...] = (vhat * lng + lnb).astype(BF16)
    rr = lax.broadcasted_iota(jnp.int32, (CH, CH), 0)
    cc = lax.broadcasted_iota(jnp.int32, (CH, CH), 1)
    tril = cc <= rr
    for jj in range(4):
        wa = jnp.where(tril, w_ref[2 * jj], 0.0).astype(BF16)
        wb = jnp.where(tril, w_ref[2 * jj + 1], 0.0).astype(BF16)
        ba = bt_ref[:, 2 * jj:2 * jj + 1]
        bb = bt_ref[:, 2 * jj + 1:2 * jj + 2]
        for ch in range(tm // CH):
            rs, cs = slice(ch * CH, (ch + 1) * CH), slice(jj * 128, (jj + 1) * 128)
            vn2 = vnb_s[rs, cs]
            mixed_s[rs, cs] = jnp.where(is_a, _nn(wa, vn2) + ba, _nn(wb, vn2) + bb)
    mixed = mixed_s[...]
    return u, vs, ug, tu, tv, vhat, rstd, mixed, ug * mixed


def _sgu_out(us, yatt, x, lng, lnb, sgw, sgbt, gatt, gsg, wout, gpm):
    T = us.shape[0]
    tm = _tile(T, 256)

    def body(us_ref, ya_ref, x_ref, lng_ref, lnb_ref, w_ref, bt_ref, ga_ref, gs_ref, wo_ref, gp_ref,
             h1_ref, yb_ref, o_ref, mixed_s, vnb_s):
        ysg = _sgu_forward(us_ref, lng_ref[...], lnb_ref[...], w_ref, bt_ref, mixed_s, vnb_s, tm)[-1]
        yb_ref[:, 0:AW] = _rms_fwd(ya_ref[...], ga_ref[...]).astype(BF16)
        yb_ref[:, AW:D] = _rms_fwd(ysg, gs_ref[...]).astype(BF16)
        o = _nn(yb_ref[...], wo_ref[...])
        o_ref[...] = o
        h1_ref[...] = x_ref[...] + _rms_fwd(o, gp_ref[...])

    row = lambda n: pl.BlockSpec((tm, n), lambda i: (i, 0))
    return pl.pallas_call(
        body,
        name="sgu_out",
        grid=(T // tm,),
        in_specs=[row(NUS), row(AW), row(D), _full((1, SW)), _full((1, SW)), _full((8, CH, CH)), _full((CH, 8)),
                  _full((1, AW)), _full((1, SW)), _full((D, D)), _full((1, D))],
        out_specs=[row(D), row(D), row(D)],
        out_shape=[jax.ShapeDtypeStruct((T, D), F32), jax.ShapeDtypeStruct((T, D), BF16), jax.ShapeDtypeStruct((T, D), F32)],
        scratch_shapes=[pltpu.VMEM((tm, SW), F32), pltpu.VMEM((tm, SW), BF16)],
        compiler_params=_params(1),
    )(us, yatt, x, lng, lnb, sgw, sgbt, gatt, gsg, wout, gpm)


def _ffn_fwd(h1, gpre, w1g, w2g, gpost):
    T = h1.shape[0]
    tm = _tile(T, 512)
    nb, hb = w1g.shape[0], w1g.shape[2]

    def body(h1_ref, gpre_ref, w1_ref, w2_ref, gpost_ref, c2_ref, f1_ref, ff_ref, h2_ref, h2b_ref):
        h1 = h1_ref[...]
        c2 = _rms_fwd(h1, gpre_ref[...]).astype(BF16)
        c2_ref[...] = c2
        ff = jnp.zeros((tm, D), F32)
        for j in range(nb):
            f1 = _nn(c2, w1_ref[j])
            f1_ref[:, j * hb:(j + 1) * hb] = f1.astype(BF16)
            r = jnp.maximum(f1, 0.0)
            ff = ff + _nn((r * r).astype(BF16), w2_ref[j])
        ff_ref[...] = ff
        h2 = h1 + _rms_fwd(ff, gpost_ref[...])
        h2_ref[...] = h2
        h2b_ref[...] = h2.astype(BF16)

    row = lambda n: pl.BlockSpec((tm, n), lambda i: (i, 0))
    once = lambda shape: pl.BlockSpec(shape, lambda i: (0,) * len(shape), pipeline_mode=pl.Buffered(1))
    return pl.pallas_call(
        body,
        name="ffn_fwd",
        grid=(T // tm,),
        in_specs=[row(D), _full((1, D)), once((nb, D, hb)), once((nb, hb, D)), _full((1, D))],
        out_specs=[row(D), row(DFF), row(D), row(D), row(D)],
        out_shape=[jax.ShapeDtypeStruct((T, D), BF16), jax.ShapeDtypeStruct((T, DFF), BF16),
                   jax.ShapeDtypeStruct((T, D), F32), jax.ShapeDtypeStruct((T, D), F32),
                   jax.ShapeDtypeStruct((T, D), BF16)],
        compiler_params=_params(1, VMEM_LIMIT_BIG),
    )(h1, gpre, w1g, w2g, gpost)


def _ple_loss(h2, p, tgt, wg, bg, wpe):
    T = h2.shape[0]
    tm = _tile(T, 512)

    def body(h2_ref, p_ref, t_ref, wg_ref, bg_ref, wpe_ref, dh2_ref, dpre_ref, dpe_ref, pb_ref, dbg_ref, loss_ref):
        @pl.when(pl.program_id(0) == 0)
        def _():
            dbg_ref[...] = jnp.zeros_like(dbg_ref)
            loss_ref[...] = jnp.zeros_like(loss_ref)

        h2 = h2_ref[...]
        gate = jax.nn.sigmoid(_nn(h2.astype(BF16), wg_ref[...]) + bg_ref[...])
        pb = p_ref[...].astype(BF16)
        pb_ref[...] = pb
        pe = _nn(pb, wpe_ref[...])
        diff = (h2 + gate * pe) - t_ref[...]
        loss_ref[...] += jnp.sum(diff * diff)
        dh3 = diff * (1.0 / D)
        dpre = (dh3 * pe) * (gate * (1.0 - gate))
        dpre_b = dpre.astype(BF16)
        dpre_ref[...] = dpre_b
        dpe_ref[...] = (dh3 * gate).astype(BF16)
        dbg_ref[...] += jnp.sum(dpre, axis=0, keepdims=True)
        dh2_ref[...] = dh3 + _nt(dpre_b, wg_ref[...])

    row = lambda n: pl.BlockSpec((tm, n), lambda i: (i, 0))
    return pl.pallas_call(
        body,
        name="ple_loss",
        grid=(T // tm,),
        in_specs=[row(D), row(PLE), row(D), _full((D, D)), _full((1, D)), _full((PLE, D))],
        out_specs=[row(D), row(D), row(D), row(PLE), _full((1, D)), _full((8, 128))],
        out_shape=[jax.ShapeDtypeStruct((T, D), F32), jax.ShapeDtypeStruct((T, D), BF16),
                   jax.ShapeDtypeStruct((T, D), BF16), jax.ShapeDtypeStruct((T, PLE), BF16),
                   jax.ShapeDtypeStruct((1, D), F32), jax.ShapeDtypeStruct((8, 128), F32)],
        compiler_params=_params(1),
    )(h2, p, tgt, wg, bg, wpe)


def _ffn_bwd(dh2, ff, h1, f1, w1g, w2g, gpost, gpre):
    T = dh2.shape[0]
    tm = _tile(T, 512)
    nb, hb = w1g.shape[0], w1g.shape[2]

    def body(dh2_ref, ff_ref, h1_ref, f1_ref, w1_ref, w2_ref, gpost_ref, gpre_ref,
             dffb_ref, df1_ref, dh1_ref, dgpost_ref, dgpre_ref):
        @pl.when(pl.program_id(0) == 0)
        def _():
            dgpost_ref[...] = jnp.zeros_like(dgpost_ref)
            dgpre_ref[...] = jnp.zeros_like(dgpre_ref)

        dh2 = dh2_ref[...]
        dff, dg = _rms_bwd(dh2, ff_ref[...], gpost_ref[...])
        dffb = dff.astype(BF16)
        dffb_ref[...] = dffb
        dgpost_ref[...] += dg
        dc2 = jnp.zeros((tm, D), F32)
        for j in range(nb):
            cols = slice(j * hb, (j + 1) * hb)
            dact = _nt(dffb, w2_ref[j])
            df1 = (dact * (2.0 * jnp.maximum(f1_ref[:, cols].astype(F32), 0.0))).astype(BF16)
            df1_ref[:, cols] = df1
            dc2 = dc2 + _nt(df1, w1_ref[j])
        dx, dg = _rms_bwd(dc2, h1_ref[...], gpre_ref[...])
        dh1_ref[...] = dh2 + dx
        dgpre_ref[...] += dg

    row = lambda n: pl.BlockSpec((tm, n), lambda i: (i, 0))
    once = lambda shape: pl.BlockSpec(shape, lambda i: (0,) * len(shape), pipeline_mode=pl.Buffered(1))
    return pl.pallas_call(
        body,
        name="ffn_bwd",
        grid=(T // tm,),
        in_specs=[row(D), row(D), row(D), row(DFF), once((nb, D, hb)), once((nb, hb, D)), _full((1, D)), _full((1, D))],
        out_specs=[row(D), row(DFF), row(D), _full((1, D)), _full((1, D))],
        out_shape=[jax.ShapeDtypeStruct((T, D), BF16), jax.ShapeDtypeStruct((T, DFF), BF16),
                   jax.ShapeDtypeStruct((T, D), F32), jax.ShapeDtypeStruct((1, D), F32),
                   jax.ShapeDtypeStruct((1, D), F32)],
        compiler_params=_params(1, VMEM_LIMIT_BIG),
    )(dh2, ff, h1, f1, w1g, w2g, gpost, gpre)


def _mix_bwd(dh1, o, us, yatt, lng, lnb, sgw, sgwt, sgbt, gatt, gsg, wout, gpm):
    T = dh1.shape[0]
    tm = _tile(T, 256)

    def body(dh1_ref, o_ref, us_ref, ya_ref, lng_ref, lnb_ref, w_ref, wt_ref, bt_ref, ga_ref, gs_ref, wo_ref, gp_ref,
             dob_ref, dya_ref, dus_ref, dw_ref, dbt_ref, dlng_ref, dlnb_ref, dga_ref, dgs_ref, dgp_ref,
             mixed_s, vnb_s, dvn_s):
        @pl.when(pl.program_id(0) == 0)
        def _():
            for r in (dw_ref, dbt_ref, dlng_ref, dlnb_ref, dga_ref, dgs_ref, dgp_ref):
                r[...] = jnp.zeros_like(r)

        is_a = lax.broadcasted_iota(jnp.int32, (1, 128), 1) < HD
        lane = lax.broadcasted_iota(jnp.int32, (1, 128), 1)
        do, dg = _rms_bwd(dh1_ref[...], o_ref[...], gp_ref[...])
        dgp_ref[...] += dg
        dob = do.astype(BF16)
        dob_ref[...] = dob
        dy = _nt(dob, wo_ref[...])
        datt, dg = _rms_bwd(dy[:, 0:AW], ya_ref[...], ga_ref[...])
        dga_ref[...] += dg
        dya_ref[...] = datt

        lng = lng_ref[...]
        u, vs, ug, tu, tv, vhat, rstd, mixed, ysg = _sgu_forward(us_ref, lng, lnb_ref[...], w_ref, bt_ref, mixed_s, vnb_s, tm)
        dysg, dg = _rms_bwd(dy[:, AW:D], ysg, gs_ref[...])
        dgs_ref[...] += dg
        dus_ref[:, 0:SW] = ((dysg * mixed) * _gelu_grad(u, tu)).astype(BF16)
        dmix = dysg * ug

        rr = lax.broadcasted_iota(jnp.int32, (CH, CH), 0)
        cc = lax.broadcasted_iota(jnp.int32, (CH, CH), 1)
        tril = cc <= rr
        triu = cc >= rr
        for jj in range(4):
            wta = jnp.where(triu, wt_ref[2 * jj], 0.0).astype(BF16)
            wtb = jnp.where(triu, wt_ref[2 * jj + 1], 0.0).astype(BF16)
            for ch in range(tm // CH):
                rs, cs = slice(ch * CH, (ch + 1) * CH), slice(jj * 128, (jj + 1) * 128)
                dm2 = dmix[rs, cs]
                dma = jnp.where(is_a, dm2, 0.0)
                dmb = jnp.where(is_a, 0.0, dm2)
                dma_b, dmb_b = dma.astype(BF16), dmb.astype(BF16)
                vn2 = vnb_s[rs, cs]
                dw_ref[2 * jj] += jnp.where(tril, _nt(dma_b, vn2), 0.0)
                dw_ref[2 * jj + 1] += jnp.where(tril, _nt(dmb_b, vn2), 0.0)
                dvn_s[rs, cs] = _nn(wta, dma_b) + _nn(wtb, dmb_b)
                dba = jnp.sum(dma, axis=1, keepdims=True)
                dbb = jnp.sum(dmb, axis=1, keepdims=True)
                dbt_ref[...] += jnp.where(lane == 2 * jj, dba, 0.0) + jnp.where(lane == 2 * jj + 1, dbb, 0.0)

        dvn = dvn_s[...]
        dlng_ref[...] += jnp.sum(dvn * vhat, axis=0, keepdims=True)
        dlnb_ref[...] += jnp.sum(dvn, axis=0, keepdims=True)
        dvh = dvn * lng
        dvg = rstd * (dvh - jnp.mean(dvh, axis=-1, keepdims=True) - vhat * jnp.mean(dvh * vhat, axis=-1, keepdims=True))
        dus_ref[:, SW:NUS] = (dvg * _gelu_grad(vs, tv)).astype(BF16)

    row = lambda n: pl.BlockSpec((tm, n), lambda i: (i, 0))
    return pl.pallas_call(
        body,
        name="mix_bwd",
        grid=(T // tm,),
        in_specs=[row(D), row(D), row(NUS), row(AW), _full((1, SW)), _full((1, SW)), _full((8, CH, CH)), _full((8, CH, CH)),
                  _full((CH, 8)), _full((1, AW)), _full((1, SW)), _full((D, D)), _full((1, D))],
        out_specs=[row(D), row(AW), row(NUS), _full((8, CH, CH)), _full((CH, 128)), _full((1, SW)), _full((1, SW)),
                   _full((1, AW)), _full((1, SW)), _full((1, D))],
        out_shape=[jax.ShapeDtypeStruct((T, D), BF16), jax.ShapeDtypeStruct((T, AW), F32), jax.ShapeDtypeStruct((T, NUS), BF16),
                   jax.ShapeDtypeStruct((8, CH, CH), F32), jax.ShapeDtypeStruct((CH, 128), F32),
                   jax.ShapeDtypeStruct((1, SW), F32), jax.ShapeDtypeStruct((1, SW), F32),
                   jax.ShapeDtypeStruct((1, AW), F32), jax.ShapeDtypeStruct((1, SW), F32), jax.ShapeDtypeStruct((1, D), F32)],
        scratch_shapes=[pltpu.VMEM((tm, SW), F32), pltpu.VMEM((tm, SW), BF16), pltpu.VMEM((tm, SW), F32)],
        compiler_params=_params(1),
    )(dh1, o, us, yatt, lng, lnb, sgw, sgwt, sgbt, gatt, gsg, wout, gpm)


def _attn_prep(qkv, dya, yatt, lsec, after):
    T = dya.shape[0]
    tm = _tile(T, 512)

    def body(q_ref, do_ref, o_ref, l_ref, after_ref, qw_ref, dow_ref):
        do = do_ref[...]
        feat = lax.broadcasted_iota(jnp.int32, (AW, 128), 0)
        head = lax.broadcasted_iota(jnp.int32, (AW, 128), 1)
        sel = jnp.where((feat >= head * HD) & (feat < (head + 1) * HD), 1.0, 0.0)
        delta = jnp.dot(do * o_ref[...], sel, precision=lax.Precision.HIGHEST, preferred_element_type=F32)
        for hp in range(4):
            cols = slice(hp * 128, (hp + 1) * 128)
            for hh in range(2):
                base = (2 * hp + hh) * 256
                lc = l_ref[:, hp * 128 + hh:hp * 128 + hh + 1]
                d_h = delta[:, 2 * hp + hh:2 * hp + hh + 1]
                qw_ref[:, base:base + 128] = _head_rows(q_ref[:, cols], hh, 0.125)
                qw_ref[:, base + 128:base + 256] = _aug_lanes(tm, AUG_A if hh == 0 else AUG_B, _split3(-lc))
                dow_ref[:, base:base + 128] = _head_rows(do[:, cols], hh, 1.0).astype(BF16)
                dow_ref[:, base + 128:base + 256] = _aug_lanes(tm, None, _split3(-d_h))

    row = lambda n: pl.BlockSpec((tm, n), lambda i: (i, 0))
    return pl.pallas_call(
        body,
        name="attn_prep",
        grid=(T // tm,),
        in_specs=[row(AW)] * 4 + [_full(after.shape)],
        out_specs=[row(4 * AW)] * 2,
        out_shape=[jax.ShapeDtypeStruct((T, 4 * AW), BF16)] * 2,
        compiler_params=_params(1),
    )(qkv, dya, yatt, lsec, after)


def _attn_bwd(kk, vk, kt, qw, dow):
    T = kk.shape[0]
    tq = _tile(T, 512)
    tk = tq
    nq = T // tq

    def body(kk_ref, vk_ref, kt_ref, qw_ref, dow_ref, dqt_ref, dcq_ref, dk_ref, dv_ref, dck_ref, dk_s, dv_s, dck_s):
        j, i = _fold_cols(pl.program_id(1), pl.program_id(2), nq)
        sub8 = lax.broadcasted_iota(jnp.int32, (8, 1), 0)
        lane = lax.broadcasted_iota(jnp.int32, (1, 128), 1)

        @pl.when((pl.program_id(1) == 0) & (pl.program_id(2) == 0))
        def _():
            dqt_ref[...] = jnp.zeros_like(dqt_ref)
            dcq_ref[...] = jnp.zeros_like(dcq_ref)

        @pl.when(i == j)
        def _():
            dk_s[...] = jnp.zeros_like(dk_s)
            dv_s[...] = jnp.zeros_like(dv_s)
            dck_s[...] = jnp.zeros_like(dck_s)

        def step(masked):
            cols = pl.ds(pl.multiple_of(i * tq, tq), tq)
            sub = lax.broadcasted_iota(jnp.int32, (128, 1), 0)
            for pp in range(PAIRS):
                lanes = slice(pp * 128, (pp + 1) * 128)
                kk2, vk2 = kk_ref[:, pp * 256:(pp + 1) * 256], vk_ref[:, pp * 256:(pp + 1) * 256]
                kt2 = kt_ref[lanes, :] * 0.125
                dcq = jnp.zeros((8, tq), F32)
                dck = jnp.zeros((tk, 128), F32)
                dv = jnp.zeros((tk, 128), F32)
                dk = jnp.zeros((tk, 128), F32)
                dqts = []
                for hh in range(2):
                    base = (2 * pp + hh) * 256
                    qw_h = qw_ref[:, base:base + 256]
                    dow_h = dow_ref[:, base:base + 256]
                    pr = jnp.exp(_nt(kk2, qw_h))
                    if masked:
                        key = lax.broadcasted_iota(jnp.int32, (tk, tq), 0)
                        qry = lax.broadcasted_iota(jnp.int32, (tk, tq), 1)
                        pr = jnp.where(key <= qry, pr, 0.0)
                    ds = pr * _nt(vk2, dow_h)
                    ds_b = ds.astype(BF16)
                    dv = dv + _nn(pr.astype(BF16), dow_h[:, 0:128])
                    dk = dk + _nn(ds_b, qw_h[:, 0:128])
                    dqts.append(_nn(kt2, ds_b))
                    dcq = dcq + jnp.where(sub8 == hh, jnp.sum(ds, axis=0, keepdims=True), 0.0)
                    dck = dck - jnp.where(lane == hh, jnp.sum(ds, axis=1, keepdims=True), 0.0)
                dv_s[:, lanes] += dv
                dk_s[:, lanes] += dk
                dck_s[:, lanes] += dck
                dcq_ref[pp, :, cols] += dcq
                dqt_ref[lanes, cols] += jnp.where(sub < HD, dqts[0], dqts[1])

        @pl.when(i > j)
        def _():
            step(False)

        @pl.when(i == j)
        def _():
            step(True)

        @pl.when(i == nq - 1)
        def _():
            dk_ref[...] = dk_s[...].astype(BF16)
            dv_ref[...] = dv_s[...].astype(BF16)
            dck_ref[...] = dck_s[...]

    kj = lambda r, t: _fold_cols(r, t, nq)[0]
    qi = lambda r, t: _fold_cols(r, t, nq)[1]
    krow = lambda g, r, t: (kj(r, t), g)
    qrow = lambda g, r, t: (qi(r, t), g)
    return pl.pallas_call(
        body,
        name="attn_bwd",
        grid=_fold_grid(nq),
        in_specs=[
            pl.BlockSpec((tk, PAIRS * 256), krow),
            pl.BlockSpec((tk, PAIRS * 256), krow),
            pl.BlockSpec((PAIRS * 128, tk), lambda g, r, t: (g, kj(r, t))),
            pl.BlockSpec((tq, PAIRS * 512), qrow),
            pl.BlockSpec((tq, PAIRS * 512), qrow),
        ],
        out_specs=[
            pl.BlockSpec((PAIRS * 128, T), lambda g, r, t: (g, 0), pipeline_mode=pl.Buffered(1)),
            pl.BlockSpec((PAIRS, 8, T), lambda g, r, t: (g, 0, 0), pipeline_mode=pl.Buffered(1)),
            pl.BlockSpec((tk, PAIRS * 128), krow),
            pl.BlockSpec((tk, PAIRS * 128), krow),
            pl.BlockSpec((tk, PAIRS * 128), krow),
        ],
        out_shape=[jax.ShapeDtypeStruct((AW, T), F32), jax.ShapeDtypeStruct((4, 8, T), F32),
                   jax.ShapeDtypeStruct((T, AW), BF16), jax.ShapeDtypeStruct((T, AW), BF16),
                   jax.ShapeDtypeStruct((T, AW), F32)],
        scratch_shapes=[pltpu.VMEM((tk, PAIRS * 128), F32), pltpu.VMEM((tk, PAIRS * 128), F32),
                        pltpu.VMEM((tk, PAIRS * 128), F32)],
        compiler_params=_params(3),
    )(kk, vk, kt, qw, dow)


def _fgate_bwd(dcq, dck, fz, fb):
    T = dck.shape[0]
    tb = _tile(T, 512)
    nb = T // tb

    def body(dcq_ref, dck_ref, fz_ref, fb_ref, df_ref, dfb_ref, carry):
        @pl.when(pl.program_id(0) == 0)
        def _():
            carry[...] = jnp.zeros_like(carry)
            dfb_ref[...] = jnp.zeros_like(dfb_ref)

        dcv = dcq_ref[...] + dck_ref[...]
        r = lax.broadcasted_iota(jnp.int32, (tb, tb), 0)
        cc = lax.broadcasted_iota(jnp.int32, (tb, tb), 1)
        tri = (cc >= r).astype(F32)
        dlf = jnp.dot(tri, dcv, precision=lax.Precision.HIGHEST, preferred_element_type=F32) + carry[...]
        carry[...] = carry[...] + jnp.sum(dcv, axis=0, keepdims=True)
        lane = lax.broadcasted_iota(jnp.int32, (tb, FPAD), 1)
        df = jnp.where(lane < 8, dlf * jax.nn.sigmoid(-(fz_ref[...] + fb_ref[...])), 0.0)
        df_ref[...] = df.astype(BF16)
        dfb_ref[...] += jnp.sum(df, axis=0, keepdims=True)

    rev = pl.BlockSpec((tb, FPAD), lambda i: (nb - 1 - i, 0))
    return pl.pallas_call(
        body,
        name="fgate_bwd",
        grid=(nb,),
        in_specs=[rev, rev, rev, _full((1, FPAD))],
        out_specs=[rev, _full((1, FPAD))],
        out_shape=[jax.ShapeDtypeStruct((T, FPAD), BF16), jax.ShapeDtypeStruct((1, FPAD), F32)],
        scratch_shapes=[pltpu.VMEM((1, FPAD), F32)],
        compiler_params=_params(1),
    )(dcq, dck, fz, fb)


def _inproj_bwd(dq, dk, dv, dus, dfz, wp, x, dh1, g):
    T = x.shape[0]
    tm = _tile(T, 512)

    def body(dq_ref, dk_ref, dv_ref, dus_ref, dfz_ref, w_ref, x_ref, dh1_ref, g_ref, gx_ref, dg_ref):
        @pl.when(pl.program_id(0) == 0)
        def _():
            dg_ref[...] = jnp.zeros_like(dg_ref)

        da = _nt(dq_ref[...].astype(BF16), w_ref[:, 0:AW])
        da += _nt(dk_ref[...], w_ref[:, AW:2 * AW])
        da += _nt(dv_ref[...], w_ref[:, 2 * AW:NQKV])
        da += _nt(dus_ref[...], w_ref[:, NQKV:NQKV + NUS])
        da += _nt(dfz_ref[...], w_ref[:, NQKV + NUS:ZP])
        dx, dg = _rms_bwd(da, x_ref[...], g_ref[...])
        gx_ref[...] = dh1_ref[...] + dx
        dg_ref[...] += dg

    row = lambda n: pl.BlockSpec((tm, n), lambda i: (i, 0))
    return pl.pallas_call(
        body,
        name="inproj_bwd",
        grid=(T // tm,),
        in_specs=[row(AW), row(AW), row(AW), row(NUS), row(FPAD), _full((D, ZP)), row(D), row(D), _full((1, D))],
        out_specs=[row(D), _full((1, D))],
        out_shape=[jax.ShapeDtypeStruct((T, D), F32), jax.ShapeDtypeStruct((1, D), F32)],
        compiler_params=_params(1),
    )(dq, dk, dv, dus, dfz, wp, x, dh1, g)


def _sq_relu(f1):
    r = jnp.maximum(f1.astype(F32), 0.0)
    return (r * r).astype(BF16)


def _wgrad(a, bs, name, a_fn=None):
    T, K = a.shape
    tt = _tile(T, 512)
    nb = len(bs)

    def out_dims(b, layout):
        N = b.shape[1]
        if layout is None:
            return (K, N)
        return (N // layout[1], K, layout[1]) if layout[0] == "col" else (K // layout[1], layout[1], N)

    shapes = [out_dims(b, layout) for b, layout in bs]

    def body(*refs):
        a_ref, b_refs, o_refs = refs[0], refs[1:1 + nb], refs[1 + nb:]

        @pl.when(pl.program_id(0) == 0)
        def _():
            for o_ref in o_refs:
                o_ref[...] = jnp.zeros_like(o_ref)

        av = a_ref[...] if a_fn is None else a_fn(a_ref[...])
        at = av.astype(BF16).T
        for (b, layout), b_ref, o_ref in zip(bs, b_refs, o_refs):
            if layout is None:
                o_ref[...] += _nn(at, b_ref[...].astype(BF16))
            elif layout[0] == "col":
                n = layout[1]
                for k in range(b.shape[1] // n):
                    o_ref[k] += _nn(at, b_ref[:, k * n:(k + 1) * n].astype(BF16))
            else:
                n = layout[1]
                bv = b_ref[...].astype(BF16)
                for k in range(K // n):
                    o_ref[k] += _nn(at[k * n:(k + 1) * n, :], bv)

    once = lambda shape: pl.BlockSpec(shape, lambda t: (0,) * len(shape), pipeline_mode=pl.Buffered(1))
    res = pl.pallas_call(
        body,
        name=name,
        grid=(T // tt,),
        in_specs=[pl.BlockSpec((tt, K), lambda t: (t, 0))] + [pl.BlockSpec((tt, b.shape[1]), lambda t: (t, 0)) for b, _ in bs],
        out_specs=[once(s) for s in shapes],
        out_shape=[jax.ShapeDtypeStruct(s, F32) for s in shapes],
        compiler_params=_params(1, VMEM_LIMIT_BIG),
    )(a, *[b for b, _ in bs])
    return res


def _adam_math(w, g, m, v):
    m = ADAM_B1 * m + (1.0 - ADAM_B1) * g
    v = ADAM_B2 * v + (1.0 - ADAM_B2) * (g * g)
    m_hat = m / (1.0 - ADAM_B1 ** ADAM_STEP)
    v_hat = v / (1.0 - ADAM_B2 ** ADAM_STEP)
    delta = -ADAM_LR * (m_hat / (jnp.sqrt(v_hat) + ADAM_EPS) + ADAM_WD * w)
    return delta, m, v


def _adam(parts, w, m, v, name):
    R, C = w.shape
    br = 128 if R % 128 == 0 else R

    def body(p_ref, w_ref, m_ref, v_ref, g_ref, d_ref, nm_ref, nv_ref):
        g = p_ref[0].astype(F32)
        for s in range(1, NDEV):
            g = g + p_ref[s].astype(F32)
        g_ref[...] = g
        d_ref[...], nm_ref[...], nv_ref[...] = _adam_math(w_ref[...], g, m_ref[...], v_ref[...])

    blk = pl.BlockSpec((br, C), lambda i: (i, 0))
    return pl.pallas_call(
        body,
        name=name,
        grid=(R // br,),
        in_specs=[pl.BlockSpec((NDEV, br, C), lambda i: (0, i, 0)), blk, blk, blk],
        out_specs=[blk] * 4,
        out_shape=[jax.ShapeDtypeStruct((R, C), F32)] * 4,
        compiler_params=_params(1),
    )(parts, w, m, v)


_SMALL = (("sg_w", 8 * CH * CH), ("f_bias", 8), ("sg_ln_g", SW), ("sg_ln_b", SW), ("sg_b", 8 * CH), ("att_out_g", AW),
          ("sg_out_g", SW), ("pre_mix_g", D), ("post_mix_g", D), ("pre_ffn_g", D), ("post_ffn_g", D), ("ple_gate_b", D))
_SEG = 8 * 128


def _seg_rows(size):
    return 8 * (-(-size // _SEG))


def _pack(vals, loss_acc):
    parts = []
    for name, size in _SMALL:
        flat = vals[name].reshape(-1)
        rows = _seg_rows(size)
        parts.append(jnp.pad(flat, (0, rows * 128 - size)).reshape(rows, 128))
    parts.append(loss_acc)
    return jnp.concatenate(parts, axis=0)


def _adam_small(parts, ws, ms, vs):
    n = len(_SMALL)
    names = [name for name, _ in _SMALL]

    def body(*refs):
        p_ref, w_refs, m_refs, v_refs = refs[0], refs[1:1 + n], refs[1 + n:1 + 2 * n], refs[1 + 2 * n:1 + 3 * n]
        loss_ref, outs = refs[1 + 3 * n], refs[2 + 3 * n:]
        g_all = p_ref[0]
        for s in range(1, NDEV):
            g_all = g_all + p_ref[s]
        r = 0
        for k, (name, size) in enumerate(_SMALL):
            dst = [outs[kind * n + k] for kind in range(4)]

            def update(g, idx):
                vals = (g,) + _adam_math(w_refs[k][idx], g, m_refs[k][idx], v_refs[k][idx])
                for d, val in zip(dst, vals):
                    d[idx] = val

            if name == "sg_w":
                for grp in range(8):
                    update(g_all[r + grp * CH:r + (grp + 1) * CH, :], (0, grp))
            elif name == "sg_b":
                update(g_all[r:r + 8, :], (0,))
            elif name == "f_bias":
                update(g_all[r:r + 1, 0:8], (slice(None),))
            else:
                update(jnp.concatenate([g_all[r + q:r + q + 1, :] for q in range(size // 128)], axis=1), (slice(None),))
            r += _seg_rows(size)
        loss_ref[...] = g_all[r:r + 1, 0:1] * (0.5 / D)

    arrs = [parts] + [d[name] for d in (ws, ms, vs) for name in names]
    res = pl.pallas_call(
        body,
        name="adam_small",
        in_specs=[_full(a.shape) for a in arrs],
        out_specs=[_full((1, 1))] + [_full(ws[name].shape) for _ in range(4) for name in names],
        out_shape=[jax.ShapeDtypeStruct((1, 1), F32)] + [jax.ShapeDtypeStruct(ws[name].shape, F32) for _ in range(4) for name in names],
        compiler_params=pltpu.CompilerParams(vmem_limit_bytes=VMEM_LIMIT),
    )(*arrs)
    return res[0], {name: [res[1 + kind * n + k] for kind in range(4)] for k, name in enumerate(names)}


def kernel(x, p, w_in, f_bias, sg_ln_g, sg_ln_b, sg_w, sg_b, att_out_g, sg_out_g, w_out, pre_mix_g, post_mix_g, pre_ffn_g, post_ffn_g, w_ff1, w_ff2, ple_w, ple_gate_w, ple_gate_b, loss_target, m_w_in, m_f_bias, m_sg_ln_g, m_sg_ln_b, m_sg_w, m_sg_b, m_att_out_g, m_sg_out_g, m_w_out, m_pre_mix_g, m_post_mix_g, m_pre_ffn_g, m_post_ffn_g, m_w_ff1, m_w_ff2, m_ple_w, m_ple_gate_w, m_ple_gate_b, v_w_in, v_f_bias, v_sg_ln_g, v_sg_ln_b, v_sg_w, v_sg_b, v_att_out_g, v_sg_out_g, v_w_out, v_pre_mix_g, v_post_mix_g, v_pre_ffn_g, v_post_ffn_g, v_w_ff1, v_w_ff2, v_ple_w, v_ple_gate_w, v_ple_gate_b):
    small_w = dict(sg_w=sg_w, f_bias=f_bias, sg_ln_g=sg_ln_g, sg_ln_b=sg_ln_b, sg_b=sg_b, att_out_g=att_out_g,
                   sg_out_g=sg_out_g, pre_mix_g=pre_mix_g, post_mix_g=post_mix_g, pre_ffn_g=pre_ffn_g,
                   post_ffn_g=post_ffn_g, ple_gate_b=ple_gate_b)
    small_m = dict(sg_w=m_sg_w, f_bias=m_f_bias, sg_ln_g=m_sg_ln_g, sg_ln_b=m_sg_ln_b, sg_b=m_sg_b, att_out_g=m_att_out_g,
                   sg_out_g=m_sg_out_g, pre_mix_g=m_pre_mix_g, post_mix_g=m_post_mix_g, pre_ffn_g=m_pre_ffn_g,
                   post_ffn_g=m_post_ffn_g, ple_gate_b=m_ple_gate_b)
    small_v = dict(sg_w=v_sg_w, f_bias=v_f_bias, sg_ln_g=v_sg_ln_g, sg_ln_b=v_sg_ln_b, sg_b=v_sg_b, att_out_g=v_att_out_g,
                   sg_out_g=v_sg_out_g, pre_mix_g=v_pre_mix_g, post_mix_g=v_post_mix_g, pre_ffn_g=v_pre_ffn_g,
                   post_ffn_g=v_post_ffn_g, ple_gate_b=v_ple_gate_b)
    big = dict(w_in=(w_in, m_w_in, v_w_in), w_out=(w_out, m_w_out, v_w_out), w_ff1=(w_ff1, m_w_ff1, v_w_ff1),
               w_ff2=(w_ff2, m_w_ff2, v_w_ff2), ple_w=(ple_w, m_ple_w, v_ple_w),
               ple_gate_w=(ple_gate_w, m_ple_gate_w, v_ple_gate_w))

    xt, pt, tgt = x[0], p[0, 0], loss_target[0]
    ws = W_IN_COLS // NDEV

    (gw_in,) = _xchg([w_in[0].astype(BF16)], True, "gather_w_in")
    rest = _xchg_start([w_out[0].astype(BF16), w_ff1[0].astype(BF16), w_ff2[0].astype(BF16), ple_w[0].astype(BF16),
                        ple_gate_w[0].astype(BF16)], True, "gather_rest_start")
    win = jnp.transpose(gw_in, (1, 0, 2)).reshape(D, W_IN_COLS)
    wp = jnp.concatenate([win[:, 0:NQKV], win[:, NQKV + 8:W_IN_COLS], win[:, NQKV:NQKV + 8],
                          jnp.zeros((D, FPAD - 8), BF16)], axis=1)

    fb = jnp.pad(f_bias.astype(F32), ((0, 0), (0, FPAD - 8)))
    sgw = sg_w[0]
    sgwt = jnp.transpose(sg_w[0], (0, 2, 1))
    sgbt = jnp.transpose(sg_b[0])

    qkv, kt, vt, us, fz, ab = _inproj(xt, pre_mix_g + rest[4][0, 0], wp)
    ct, kk, vk = _fcum(fz, fb, qkv)
    yatt, lsec = _attn_fwd(qkv, kk, vt, ct)
    gw_out, gw1, gw2, gwpe, gwg = _xchg_wait(rest, True, yatt, "gather_rest_wait")
    wout = gw_out.reshape(D, D)
    wg = gwg.reshape(D, D)
    wpe = jnp.transpose(gwpe, (1, 0, 2)).reshape(PLE, D)
    h1, yb, o = _sgu_out(us, yatt, xt, sg_ln_g, sg_ln_b, sgw, sgbt, att_out_g, sg_out_g, wout, post_mix_g)
    c2b, f1b, ff, h2, h2b = _ffn_fwd(h1, pre_ffn_g, gw1, gw2, post_ffn_g)
    dh2, dpre, dpe, pb, dbg, loss_acc = _ple_loss(h2, pt, tgt, wg, ple_gate_b, wpe)

    (g_pe,) = _wgrad(pb, [(dpe, ("col", D // NDEV))], "wgrad_ple")
    (g_g,) = _wgrad(h2b, [(dpre, ("row", D // NDEV))], "wgrad_gate")
    dffb, df1, dh1, dgpostffn, dgpreffn = _ffn_bwd(dh2, ff, h1, f1b, gw1, gw2, post_ffn_g, pre_ffn_g)
    (g_1,) = _wgrad(c2b, [(df1, ("col", DFF // NDEV))], "wgrad_ff1")
    (g_2,) = _wgrad(f1b, [(dffb, ("row", DFF // NDEV))], "wgrad_ff2", a_fn=_sq_relu)
    early = _xchg_start([g_1, g_2, g_pe, g_g], False, "scatter_early_start")
    dob, dya, dus, dsgw, dsgbt, dlng, dlnb, dgatt, dgsg, dgpostmix = _mix_bwd(
        dh1, o, us, yatt, sg_ln_g, sg_ln_b, sgw, sgwt, sgbt, att_out_g, sg_out_g, wout, post_mix_g + early[4][0, 0])
    (g_out,) = _wgrad(yb, [(dob, ("row", D // NDEV))], "wgrad_out")
    mid = _xchg_start([g_out], False, "scatter_mid_start")
    qw, dow = _attn_prep(qkv, dya, yatt, lsec, mid[4])
    dqt, dcq, dk, dv, dck = _attn_bwd(kk, vk, kt, qw, dow)
    dq = jnp.transpose(dqt)
    dcq = jnp.pad(jnp.transpose(dcq[:, 0:2, :].reshape(8, -1)), ((0, 0), (0, FPAD - 8)))
    dck = jnp.pad(dck.reshape(-1, 4, 128)[:, :, 0:2].reshape(-1, 8), ((0, 0), (0, FPAD - 8)))
    dfz, dfb = _fgate_bwd(dcq, dck, fz, fb)

    gq, gk, gv, gus, gf = _wgrad(ab, [(dq, None), (dk, None), (dv, None), (dus, None), (dfz, None)], "wgrad_in")
    g_in = jnp.concatenate([gq, gk, gv, gf[:, 0:8], gus], axis=1)
    g_in = jnp.transpose(g_in.reshape(D, NDEV, ws), (1, 0, 2))
    late = _xchg_start([g_in.astype(BF16)], False, "scatter_late_start")
    grad_x, dgpremix = _inproj_bwd(dq, dk, dv, dus, dfz, wp, xt, dh1, pre_mix_g + late[4][0, 0])

    small_g = dict(sg_w=dsgw, f_bias=dfb[:, 0:8], sg_ln_g=dlng, sg_ln_b=dlnb, sg_b=jnp.transpose(dsgbt[:, 0:8]),
                   att_out_g=dgatt, sg_out_g=dgsg, pre_mix_g=dgpremix, post_mix_g=dgpostmix, pre_ffn_g=dgpreffn,
                   post_ffn_g=dgpostffn, ple_gate_b=dbg)

    r_1, r_2, r_pe, r_g = _xchg_wait(early, False, grad_x, "scatter_early_wait")
    (r_out,) = _xchg_wait(mid, False, grad_x, "scatter_mid_wait")
    (r_small,) = _xchg([_pack(small_g, loss_acc)], True, "gather_small_grads")

    res = {}

    def adam_big(name, parts):
        w, m, v = big[name]
        res[name] = [t[None] for t in _adam(parts, w[0], m[0], v[0], "adam_" + name)]

    for name, parts in (("w_out", r_out), ("w_ff1", r_1), ("w_ff2", r_2), ("ple_w", r_pe), ("ple_gate_w", r_g)):
        adam_big(name, parts)
    (r_in,) = _xchg_wait(late, False, res["w_ff1"][0], "scatter_late_wait")
    adam_big("w_in", r_in)
    loss, small = _adam_small(r_small, small_w, small_m, small_v)
    res.update(small)

    order = ["w_in", "f_bias", "sg_ln_g", "sg_ln_b", "sg_w", "sg_b", "att_out_g", "sg_out_g", "w_out", "pre_mix_g",
             "post_mix_g", "pre_ffn_g", "post_ffn_g", "w_ff1", "w_ff2", "ple_w", "ple_gate_w", "ple_gate_b"]
    outs = [loss[0, 0], grad_x[None]]
    for kind in range(4):
        outs += [res[name][kind] for name in order]
    return tuple(outs)
```

```python
import jax
import jax.numpy as jnp
from jax import lax
from jax.experimental import pallas as pl
from jax.experimental.pallas import tpu as pltpu

F32 = jnp.float32
BF16 = jnp.bfloat16

NDEV = 8
D = 1024
AW = 512
SW = 512
HD = 64
CH = 128
DFF = 4096
PLE = 256
NQKV = 3 * AW
NUS = 2 * SW
FPAD = 128
ZP = NQKV + NUS + FPAD
W_IN_COLS = 2568
EPS = 1e-6
MASKV = -1e30
GELU_K = 0.7978845608028654
GELU_C = 0.044715

ADAM_LR = 0.001
ADAM_B1 = 0.9
ADAM_B2 = 0.999
ADAM_EPS = 1e-08
ADAM_WD = 0.01
ADAM_STEP = 10

VMEM_LIMIT = 48 * 1024 * 1024
VMEM_LIMIT_BIG = 60 * 1024 * 1024


def _nn(a, b):
    return jnp.dot(a, b, preferred_element_type=F32)


def _nt(a, b):
    return lax.dot_general(a, b, (((1,), (1,)), ((), ())), preferred_element_type=F32)


def _tn(a, b):
    return lax.dot_general(a, b, (((0,), (0,)), ((), ())), preferred_element_type=F32)


def _tile(n, pref):
    return min(n, pref)


def _params(n_axes, vmem=VMEM_LIMIT):
    return pltpu.CompilerParams(dimension_semantics=("arbitrary",) * n_axes, vmem_limit_bytes=vmem)


def _full(shape):
    nd = len(shape)
    return pl.BlockSpec(shape, lambda *_: (0,) * nd)


def _rms_fwd(x, g):
    r = lax.rsqrt(jnp.mean(x * x, axis=-1, keepdims=True) + EPS)
    return x * r * g


def _rms_bwd(dy, x, g):
    n = x.shape[-1]
    r = lax.rsqrt(jnp.mean(x * x, axis=-1, keepdims=True) + EPS)
    u = dy * g
    s = jnp.sum(x * u, axis=-1, keepdims=True)
    dx = r * u - x * (r * r * r * (s * (1.0 / n)))
    dg = jnp.sum(dy * (x * r), axis=0, keepdims=True)
    return dx, dg


def _gelu(x):
    t = jnp.tanh(GELU_K * (x + GELU_C * (x * x * x)))
    return x * (0.5 * (1.0 + t)), t


def _gelu_grad(x, t):
    return 0.5 * (1.0 + t) + 0.5 * x * (1.0 - t * t) * (GELU_K * (1.0 + 3.0 * GELU_C * x * x))


def _xchg(arrs, gather, name):
    n = len(arrs)
    out_shapes = [jax.ShapeDtypeStruct(((NDEV,) + a.shape) if gather else a.shape, a.dtype) for a in arrs]

    def body(*refs):
        ins, outs = refs[:n], refs[n:2 * n]
        send, recv, loc = refs[2 * n:]
        x, y, c = lax.axis_index("x"), lax.axis_index("y"), lax.axis_index("c")
        me = 4 * x + 2 * y + c
        local = []
        for a in range(n):
            cp = pltpu.make_async_copy(ins[a] if gather else ins[a].at[me], outs[a].at[me], loc.at[a])
            cp.start()
            local.append(cp)
        remote = []
        for k in range(1, NDEV):
            px = 1 - x if (k >> 2) & 1 else x
            py = 1 - y if (k >> 1) & 1 else y
            pc = 1 - c if k & 1 else c
            peer = 4 * px + 2 * py + pc
            for a in range(n):
                cp = pltpu.make_async_remote_copy(
                    src_ref=ins[a] if gather else ins[a].at[peer],
                    dst_ref=outs[a].at[me],
                    send_sem=send.at[a, k - 1],
                    recv_sem=recv.at[a, k - 1],
                    device_id=(px, py, pc),
                    device_id_type=pl.DeviceIdType.MESH,
                )
                cp.start()
                remote.append(cp)
        for cp in remote:
            cp.wait_send()
            cp.wait_recv()
        for cp in local:
            cp.wait()

    hbm = pl.BlockSpec(memory_space=pltpu.HBM)
    return pl.pallas_call(
        body,
        name=name,
        out_shape=out_shapes,
        in_specs=[hbm] * n,
        out_specs=[hbm] * n,
        scratch_shapes=[
            pltpu.SemaphoreType.DMA((n, NDEV - 1)),
            pltpu.SemaphoreType.DMA((n, NDEV - 1)),
            pltpu.SemaphoreType.DMA((n,)),
        ],
    )(*arrs)


def _peers(x, y, c):
    out = []
    for k in range(1, NDEV):
        out.append((1 - x if (k >> 2) & 1 else x, 1 - y if (k >> 1) & 1 else y, 1 - c if k & 1 else c))
    return out


def _xchg_start(arrs, gather, name):
    n = len(arrs)
    me = 4 * lax.axis_index("x") + 2 * lax.axis_index("y") + lax.axis_index("c")
    lands = []
    for a in arrs:
        shape = ((NDEV,) + a.shape) if gather else a.shape
        own = a[None] if gather else lax.dynamic_slice_in_dim(a, me, 1, axis=0)
        lands.append(lax.dynamic_update_slice_in_dim(lax.empty(shape, a.dtype), own, me, axis=0))

    def body(*refs):
        ins, lnd = refs[:n], refs[n:2 * n]
        send, recv, token = refs[2 * n:3 * n], refs[3 * n:4 * n], refs[-1]
        x, y, c = lax.axis_index("x"), lax.axis_index("y"), lax.axis_index("c")
        mine = 4 * x + 2 * y + c
        for px, py, pc in _peers(x, y, c):
            peer = 4 * px + 2 * py + pc
            for a in range(n):
                pltpu.make_async_remote_copy(
                    src_ref=ins[a] if gather else ins[a].at[peer],
                    dst_ref=lnd[a].at[mine],
                    send_sem=send[a],
                    recv_sem=recv[a],
                    device_id=(px, py, pc),
                    device_id_type=pl.DeviceIdType.MESH,
                ).start()
        token[...] = jnp.zeros_like(token)

    hbm = pl.BlockSpec(memory_space=pltpu.HBM)
    sem = pl.BlockSpec(memory_space=pltpu.SEMAPHORE)
    res = pl.pallas_call(
        body,
        name=name,
        out_shape=(*[pltpu.SemaphoreType.DMA(())] * (2 * n),
                   *[pltpu.HBM(a.shape, a.dtype) for a in arrs], *[pltpu.HBM(l.shape, l.dtype) for l in lands],
                   jax.ShapeDtypeStruct((8, 128), F32)),
        in_specs=[hbm] * (2 * n),
        out_specs=(*([sem] * (2 * n)), *([hbm] * (2 * n)), pl.BlockSpec(memory_space=pltpu.VMEM)),
        input_output_aliases={i: 2 * n + i for i in range(2 * n)},
        compiler_params=pltpu.CompilerParams(has_side_effects=pltpu.SideEffectType.DATAFLOW_SIDE_EFFECTING),
    )(*[pltpu.with_memory_space_constraint(a, pltpu.HBM) for a in arrs],
      *[pltpu.with_memory_space_constraint(l, pltpu.HBM) for l in lands])
    return list(res[0:n]), list(res[n:2 * n]), list(res[2 * n:3 * n]), list(res[3 * n:4 * n]), res[-1]


def _xchg_wait(started, gather, after, name):
    send, recv, srcs, lands, _ = started
    n = len(srcs)

    def body(*refs):
        lnd = refs[n:2 * n]
        send, recv = refs[2 * n:3 * n], refs[3 * n:4 * n]
        me = (lax.axis_index("x"), lax.axis_index("y"), lax.axis_index("c"))
        for a in range(n):
            seven = lnd[a].at[pl.ds(0, NDEV - 1)]
            cp = pltpu.make_async_remote_copy(src_ref=seven, dst_ref=seven, send_sem=send[a], recv_sem=recv[a],
                                              device_id=me, device_id_type=pl.DeviceIdType.MESH)
            cp.wait_send()
            cp.wait_recv()

    hbm = pl.BlockSpec(memory_space=pltpu.HBM)
    sem = pl.BlockSpec(memory_space=pltpu.SEMAPHORE)
    res = pl.pallas_call(
        body,
        name=name,
        out_shape=tuple([pltpu.HBM(a.shape, a.dtype) for a in srcs] + [pltpu.HBM(l.shape, l.dtype) for l in lands]),
        in_specs=[hbm] * (2 * n) + [sem] * (2 * n) + [pl.BlockSpec(memory_space=pl.ANY)],
        out_specs=tuple([hbm] * (2 * n)),
        input_output_aliases={i: i for i in range(2 * n)},
        compiler_params=pltpu.CompilerParams(has_side_effects=pltpu.SideEffectType.DATAFLOW_SIDE_EFFECTING),
    )(*srcs, *lands, *send, *recv, after)
    return list(res[n:])


def _inproj(x, g, wp):
    T = x.shape[0]
    tm = _tile(T, 512)

    def body(x_ref, g_ref, w_ref, qkv_ref, kt_ref, vt_ref, us_ref, fz_ref, ab_ref):
        a = _rms_fwd(x_ref[...], g_ref[...]).astype(BF16)
        ab_ref[...] = a
        qkv_ref[:, 0:AW] = _nn(a, w_ref[:, 0:AW]).astype(BF16)
        kk = _nn(a, w_ref[:, AW:2 * AW])
        qkv_ref[:, AW:2 * AW] = kk.astype(BF16)
        kt_ref[...] = kk.T.astype(BF16)
        vv = _nn(a, w_ref[:, 2 * AW:NQKV])
        qkv_ref[:, 2 * AW:NQKV] = vv.astype(BF16)
        vt_ref[...] = vv.T.astype(BF16)
        us_ref[...] = _nn(a, w_ref[:, NQKV:NQKV + NUS])
        fz_ref[...] = _nn(a, w_ref[:, NQKV + NUS:ZP])

    row = lambda n: pl.BlockSpec((tm, n), lambda i: (i, 0))
    col = pl.BlockSpec((AW, tm), lambda i: (0, i))
    return pl.pallas_call(
        body,
        name="inproj",
        grid=(T // tm,),
        in_specs=[row(D), _full((1, D)), _full((D, ZP))],
        out_specs=[row(NQKV), col, col, row(NUS), row(FPAD), row(D)],
        out_shape=[
            jax.ShapeDtypeStruct((T, NQKV), BF16),
            jax.ShapeDtypeStruct((AW, T), BF16),
            jax.ShapeDtypeStruct((AW, T), BF16),
            jax.ShapeDtypeStruct((T, NUS), F32),
            jax.ShapeDtypeStruct((T, FPAD), F32),
            jax.ShapeDtypeStruct((T, D), BF16),
        ],
        compiler_params=_params(1),
    )(x, g, wp)


def _log_sigmoid(z):
    return jnp.minimum(z, 0.0) - jnp.log1p(jnp.exp(-jnp.abs(z)))


def _split3(x):
    hi = x.astype(BF16)
    r1 = x - hi.astype(F32)
    mid = r1.astype(BF16)
    lo = (r1 - mid.astype(F32)).astype(BF16)
    return hi, mid, lo


AUG_A, AUG_B, AUG_ONE = 0, 3, 6


def _aug_lanes(rows, first_one, pieces):
    lane = lax.broadcasted_iota(jnp.int32, (rows, 128), 1)
    out = jnp.zeros((rows, 128), F32)
    if first_one is not None:
        out = jnp.where((lane >= first_one) & (lane < first_one + 3), 1.0, out)
    for n, piece in enumerate(pieces):
        out = jnp.where(lane == AUG_ONE + n, piece.astype(F32), out)
    return out.astype(BF16)


def _fcum(fz, fb, qkv):
    T = fz.shape[0]
    tb = _tile(T, 512)

    def body(fz_ref, fb_ref, k_ref, v_ref, ct_ref, kk_ref, vk_ref, carry):
        @pl.when(pl.program_id(0) == 0)
        def _():
            carry[...] = jnp.zeros_like(carry)

        lf = _log_sigmoid(fz_ref[...] + fb_ref[...])
        r = lax.broadcasted_iota(jnp.int32, (tb, tb), 0)
        cc = lax.broadcasted_iota(jnp.int32, (tb, tb), 1)
        tri = (cc <= r).astype(F32)
        cs = jnp.dot(tri, lf, precision=lax.Precision.HIGHEST, preferred_element_type=F32) + carry[...]
        ct_ref[...] = cs.T[0:8, :]
        carry[...] = carry[...] + jnp.sum(lf, axis=0, keepdims=True)

        pieces = _split3(cs)
        src = lax.broadcasted_iota(jnp.int32, (128, 128), 0)
        dst = lax.broadcasted_iota(jnp.int32, (128, 128), 1)
        ones = jnp.where((dst[0:1, :] >= AUG_ONE) & (dst[0:1, :] < AUG_ONE + 3), 1.0, 0.0)
        for hp in range(4):
            aug = jnp.zeros((tb, 128), F32) + ones
            for n, piece in enumerate(pieces):
                sel = jnp.where(((src == 2 * hp) & (dst == AUG_A + n)) | ((src == 2 * hp + 1) & (dst == AUG_B + n)), -1.0, 0.0)
                aug = aug + _nn(piece, sel.astype(BF16))
            aug = aug.astype(BF16)
            kk_ref[:, hp * 256:hp * 256 + 128] = k_ref[:, hp * 128:(hp + 1) * 128]
            kk_ref[:, hp * 256 + 128:(hp + 1) * 256] = aug
            vk_ref[:, hp * 256:hp * 256 + 128] = v_ref[:, hp * 128:(hp + 1) * 128]
            vk_ref[:, hp * 256 + 128:(hp + 1) * 256] = aug

    row = lambda n, c: pl.BlockSpec((tb, n), lambda i: (i, c))
    return pl.pallas_call(
        body,
        name="fcum",
        grid=(T // tb,),
        in_specs=[row(FPAD, 0), _full((1, FPAD)), row(AW, 1), row(AW, 2)],
        out_specs=[pl.BlockSpec((8, tb), lambda i: (0, i)), row(2 * AW, 0), row(2 * AW, 0)],
        out_shape=[jax.ShapeDtypeStruct((8, T), F32), jax.ShapeDtypeStruct((T, 2 * AW), BF16),
                   jax.ShapeDtypeStruct((T, 2 * AW), BF16)],
        scratch_shapes=[pltpu.VMEM((1, FPAD), F32)],
        compiler_params=_params(1),
    )(fz, fb, qkv, qkv)


def _fold_rows(r, t, nq):
    if nq == 1:
        return r, t
    low = t <= r
    return jnp.where(low, r, nq - 1 - r), jnp.where(low, t, t - r - 1)


def _fold_cols(r, t, nq):
    if nq == 1:
        return r, t
    first = t < nq - r
    j = jnp.where(first, r, nq - 1 - r)
    return j, jnp.where(first, r + t, nq - 1 - r + (t - (nq - r)))


PAIRS = 4


def _fold_grid(nq):
    assert nq == 1 or nq % 2 == 0
    return (4 // PAIRS, 1, 1) if nq == 1 else (4 // PAIRS, nq // 2, nq + 1)


def _head_rows(x2, hh, scale):
    is_a = lax.broadcasted_iota(jnp.int32, (1, 128), 1) < HD
    keep = is_a if hh == 0 else jnp.logical_not(is_a)
    return jnp.where(keep, x2, jnp.zeros_like(x2)) * scale


def _attn_fwd(qkv, kk, vt, ct):
    T = qkv.shape[0]
    tq = _tile(T, 512)
    tk = tq
    nq = T // tq

    def body(q_ref, kk_ref, vt_ref, ctq_ref, o_ref, lsec_ref, qw_s, m_s, l_s, acc_s):
        i, j = _fold_rows(pl.program_id(1), pl.program_id(2), nq)
        sub8 = lax.broadcasted_iota(jnp.int32, (8, 1), 0)

        def cref_of(pp, hh):
            head = 2 * (PAIRS * pl.program_id(0) + pp) + hh
            return jnp.sum(jnp.where(sub8 == head, ctq_ref[:, 0:1], 0.0), axis=0, keepdims=True)

        @pl.when(j == 0)
        def _():
            for pp in range(PAIRS):
                q2 = q_ref[:, pp * 128:(pp + 1) * 128]
                for hh in range(2):
                    rows = slice(hh * tq, (hh + 1) * tq)
                    qw_s[pp, rows, 0:128] = _head_rows(q2, hh, 0.125)
                    qw_s[pp, rows, 128:256] = _aug_lanes(tq, AUG_A if hh == 0 else AUG_B, _split3(cref_of(pp, hh)))
            m_s[...] = jnp.full_like(m_s, MASKV)
            l_s[...] = jnp.zeros_like(l_s)
            acc_s[...] = jnp.zeros_like(acc_s)

        def step(masked):
            if masked:
                causal = lax.broadcasted_iota(jnp.int32, (tk, tq), 0) <= lax.broadcasted_iota(jnp.int32, (tk, tq), 1)
            logits = lambda pp: _nt(kk_ref[:, pp * 256:(pp + 1) * 256], qw_s[pp])
            s_next = logits(0)
            for pp in range(PAIRS):
                s2 = s_next
                if pp + 1 < PAIRS:
                    s_next = logits(pp + 1)
                vt2 = vt_ref[pp * 128:(pp + 1) * 128, :]
                for hh in range(2):
                    n = 2 * pp + hh
                    s = s2[:, hh * tq:(hh + 1) * tq]
                    if masked:
                        s = jnp.where(causal, s, MASKV)
                    m_prev = m_s[n]
                    m_new = jnp.maximum(m_prev, jnp.max(s, axis=0, keepdims=True))
                    pr = jnp.exp(s - m_new)
                    alpha = jnp.exp(m_prev - m_new)
                    l_s[n] = alpha * l_s[n] + jnp.sum(pr, axis=0, keepdims=True)
                    m_s[n] = m_new
                    acc_s[n] = alpha * acc_s[n] + _nn(vt2, pr.astype(BF16))

        @pl.when(j < i)
        def _():
            step(False)

        @pl.when(j == i)
        def _():
            step(True)
            sub = lax.broadcasted_iota(jnp.int32, (128, 1), 0)
            for pp in range(PAIRS):
                a, b = 2 * pp, 2 * pp + 1
                ot = jnp.where(sub < HD, acc_s[a] * (1.0 / l_s[a]), acc_s[b] * (1.0 / l_s[b]))
                o_ref[:, pp * 128:(pp + 1) * 128] = ot.T
                lrow = [m_s[2 * pp + hh] + jnp.log(l_s[2 * pp + hh]) - cref_of(pp, hh) for hh in range(2)]
                lsec_ref[:, pp * 128:(pp + 1) * 128] = jnp.where(sub == 0, lrow[0], jnp.where(sub == 1, lrow[1], 0.0)).T

    qi = lambda r, t: _fold_rows(r, t, nq)[0]
    kj = lambda r, t: _fold_rows(r, t, nq)[1]
    return pl.pallas_call(
        body,
        name="attn_fwd",
        grid=_fold_grid(nq),
        in_specs=[
            pl.BlockSpec((tq, PAIRS * 128), lambda g, r, t: (qi(r, t), g)),
            pl.BlockSpec((tk, PAIRS * 256), lambda g, r, t: (kj(r, t), g)),
            pl.BlockSpec((PAIRS * 128, tk), lambda g, r, t: (g, kj(r, t))),
            pl.BlockSpec((8, tq), lambda g, r, t: (0, qi(r, t))),
        ],
        out_specs=[pl.BlockSpec((tq, PAIRS * 128), lambda g, r, t: (qi(r, t), g))] * 2,
        out_shape=[jax.ShapeDtypeStruct((T, AW), F32)] * 2,
        scratch_shapes=[pltpu.VMEM((PAIRS, 2 * tq, 256), BF16), pltpu.VMEM((2 * PAIRS, 1, tq), F32),
                        pltpu.VMEM((2 * PAIRS, 1, tq), F32), pltpu.VMEM((2 * PAIRS, 128, tq), F32)],
        compiler_params=_params(3),
    )(qkv, kk, vt, ct)


def _sgu_forward(us_ref, lng, lnb, w_ref, bt_ref, mixed_s, vnb_s, tm):
    is_a = lax.broadcasted_iota(jnp.int32, (1, 128), 1) < HD
    u = us_ref[:, 0:SW]
    vs = us_ref[:, SW:NUS]
    ug, tu = _gelu(u)
    vg, tv = _gelu(vs)
    mu = jnp.mean(vg, axis=-1, keepdims=True)
    xc = vg - mu
    rstd = lax.rsqrt(jnp.mean(xc * xc, axis=-1, keepdims=True) + EPS)
    vhat = xc * rstd
    vnb_s[...] = (vhat * lng + lnb).astype(BF16)
    rr = lax.broadcasted_iota(jnp.int32, (CH, CH), 0)
    cc = lax.broadcasted_iota(jnp.int32, (CH, CH), 1)
    tril = cc <= rr
    for jj in range(4):
        wa = jnp.where(tril, w_ref[2 * jj], 0.0).astype(BF16)
        wb = jnp.where(tril, w_ref[2 * jj + 1], 0.0).astype(BF16)
        ba = bt_ref[:, 2 * jj:2 * jj + 1]
        bb = bt_ref[:, 2 * jj + 1:2 * jj + 2]
        for ch in range(tm // CH):
            rs, cs = slice(ch * CH, (ch + 1) * CH), slice(jj * 128, (jj + 1) * 128)
            vn2 = vnb_s[rs, cs]
            mixed_s[rs, cs] = jnp.where(is_a, _nn(wa, vn2) + ba, _nn(wb, vn2) + bb)
    mixed = mixed_s[...]
    return u, vs, ug, tu, tv, vhat, rstd, mixed, ug * mixed


def _sgu_out(us, yatt, x, lng, lnb, sgw, sgbt, gatt, gsg, wout, gpm):
    T = us.shape[0]
    tm = _tile(T, 256)

    def body(us_ref, ya_ref, x_ref, lng_ref, lnb_ref, w_ref, bt_ref, ga_ref, gs_ref, wo_ref, gp_ref,
             h1_ref, yb_ref, o_ref, mixed_s, vnb_s):
        ysg = _sgu_forward(us_ref, lng_ref[...], lnb_ref[...], w_ref, bt_ref, mixed_s, vnb_s, tm)[-1]
        yb_ref[:, 0:AW] = _rms_fwd(ya_ref[...], ga_ref[...]).astype(BF16)
        yb_ref[:, AW:D] = _rms_fwd(ysg, gs_ref[...]).astype(BF16)
        o = _nn(yb_ref[...], wo_ref[...])
        o_ref[...] = o
        h1_ref[...] = x_ref[...] + _rms_fwd(o, gp_ref[...])

    row = lambda n: pl.BlockSpec((tm, n), lambda i: (i, 0))
    return pl.pallas_call(
        body,
        name="sgu_out",
        grid=(T // tm,),
        in_specs=[row(NUS), row(AW), row(D), _full((1, SW)), _full((1, SW)), _full((8, CH, CH)), _full((CH, 8)),
                  _full((1, AW)), _full((1, SW)), _full((D, D)), _full((1, D))],
        out_specs=[row(D), row(D), row(D)],
        out_shape=[jax.ShapeDtypeStruct((T, D), F32), jax.ShapeDtypeStruct((T, D), BF16), jax.ShapeDtypeStruct((T, D), F32)],
        scratch_shapes=[pltpu.VMEM((tm, SW), F32), pltpu.VMEM((tm, SW), BF16)],
        compiler_params=_params(1),
    )(us, yatt, x, lng, lnb, sgw, sgbt, gatt, gsg, wout, gpm)


def _ffn_fwd(h1, gpre, w1g, w2g, gpost):
    T = h1.shape[0]
    tm = _tile(T, 512)
    nb, hb = w1g.shape[0], w1g.shape[2]

    def body(h1_ref, gpre_ref, w1_ref, w2_ref, gpost_ref, c2_ref, f1_ref, ff_ref, h2_ref, h2b_ref):
        h1 = h1_ref[...]
        c2 = _rms_fwd(h1, gpre_ref[...]).astype(BF16)
        c2_ref[...] = c2
        ff = jnp.zeros((tm, D), F32)
        for j in range(nb):
            f1 = _nn(c2, w1_ref[j])
            f1_ref[:, j * hb:(j + 1) * hb] = f1.astype(BF16)
            r = jnp.maximum(f1, 0.0)
            ff = ff + _nn((r * r).astype(BF16), w2_ref[j])
        ff_ref[...] = ff
        h2 = h1 + _rms_fwd(ff, gpost_ref[...])
        h2_ref[...] = h2
        h2b_ref[...] = h2.astype(BF16)

    row = lambda n: pl.BlockSpec((tm, n), lambda i: (i, 0))
    once = lambda shape: pl.BlockSpec(shape, lambda i: (0,) * len(shape), pipeline_mode=pl.Buffered(1))
    return pl.pallas_call(
        body,
        name="ffn_fwd",
        grid=(T // tm,),
        in_specs=[row(D), _full((1, D)), once((nb, D, hb)), once((nb, hb, D)), _full((1, D))],
        out_specs=[row(D), row(DFF), row(D), row(D), row(D)],
        out_shape=[jax.ShapeDtypeStruct((T, D), BF16), jax.ShapeDtypeStruct((T, DFF), BF16),
                   jax.ShapeDtypeStruct((T, D), F32), jax.ShapeDtypeStruct((T, D), F32),
                   jax.ShapeDtypeStruct((T, D), BF16)],
        compiler_params=_params(1, VMEM_LIMIT_BIG),
    )(h1, gpre, w1g, w2g, gpost)


def _ple_loss(h2, p, tgt, wg, bg, wpe):
    T = h2.shape[0]
    tm = _tile(T, 512)

    def body(h2_ref, p_ref, t_ref, wg_ref, bg_ref, wpe_ref, dh2_ref, dpre_ref, dpe_ref, pb_ref, dbg_ref, loss_ref):
        @pl.when(pl.program_id(0) == 0)
        def _():
            dbg_ref[...] = jnp.zeros_like(dbg_ref)
            loss_ref[...] = jnp.zeros_like(loss_ref)

        h2 = h2_ref[...]
        gate = jax.nn.sigmoid(_nn(h2.astype(BF16), wg_ref[...]) + bg_ref[...])
        pb = p_ref[...].astype(BF16)
        pb_ref[...] = pb
        pe = _nn(pb, wpe_ref[...])
        diff = (h2 + gate * pe) - t_ref[...]
        loss_ref[...] += jnp.sum(diff * diff)
        dh3 = diff * (1.0 / D)
        dpre = (dh3 * pe) * (gate * (1.0 - gate))
        dpre_b = dpre.astype(BF16)
        dpre_ref[...] = dpre_b
        dpe_ref[...] = (dh3 * gate).astype(BF16)
        dbg_ref[...] += jnp.sum(dpre, axis=0, keepdims=True)
        dh2_ref[...] = dh3 + _nt(dpre_b, wg_ref[...])

    row = lambda n: pl.BlockSpec((tm, n), lambda i: (i, 0))
    return pl.pallas_call(
        body,
        name="ple_loss",
        grid=(T // tm,),
        in_specs=[row(D), row(PLE), row(D), _full((D, D)), _full((1, D)), _full((PLE, D))],
        out_specs=[row(D), row(D), row(D), row(PLE), _full((1, D)), _full((8, 128))],
        out_shape=[jax.ShapeDtypeStruct((T, D), F32), jax.ShapeDtypeStruct((T, D), BF16),
                   jax.ShapeDtypeStruct((T, D), BF16), jax.ShapeDtypeStruct((T, PLE), BF16),
                   jax.ShapeDtypeStruct((1, D), F32), jax.ShapeDtypeStruct((8, 128), F32)],
        compiler_params=_params(1),
    )(h2, p, tgt, wg, bg, wpe)


def _ffn_bwd(dh2, ff, h1, f1, w1g, w2g, gpost, gpre):
    T = dh2.shape[0]
    tm = _tile(T, 512)
    nb, hb = w1g.shape[0], w1g.shape[2]

    def body(dh2_ref, ff_ref, h1_ref, f1_ref, w1_ref, w2_ref, gpost_ref, gpre_ref,
             dffb_ref, df1_ref, dh1_ref, dgpost_ref, dgpre_ref):
        @pl.when(pl.program_id(0) == 0)
        def _():
            dgpost_ref[...] = jnp.zeros_like(dgpost_ref)
            dgpre_ref[...] = jnp.zeros_like(dgpre_ref)

        dh2 = dh2_ref[...]
        dff, dg = _rms_bwd(dh2, ff_ref[...], gpost_ref[...])
        dffb = dff.astype(BF16)
        dffb_ref[...] = dffb
        dgpost_ref[...] += dg
        dc2 = jnp.zeros((tm, D), F32)
        for j in range(nb):
            cols = slice(j * hb, (j + 1) * hb)
            dact = _nt(dffb, w2_ref[j])
            df1 = (dact * (2.0 * jnp.maximum(f1_ref[:, cols].astype(F32), 0.0))).astype(BF16)
            df1_ref[:, cols] = df1
            dc2 = dc2 + _nt(df1, w1_ref[j])
        dx, dg = _rms_bwd(dc2, h1_ref[...], gpre_ref[...])
        dh1_ref[...] = dh2 + dx
        dgpre_ref[...] += dg

    row = lambda n: pl.BlockSpec((tm, n), lambda i: (i, 0))
    once = lambda shape: pl.BlockSpec(shape, lambda i: (0,) * len(shape), pipeline_mode=pl.Buffered(1))
    return pl.pallas_call(
        body,
        name="ffn_bwd",
        grid=(T // tm,),
        in_specs=[row(D), row(D), row(D), row(DFF), once((nb, D, hb)), once((nb, hb, D)), _full((1, D)), _full((1, D))],
        out_specs=[row(D), row(DFF), row(D), _full((1, D)), _full((1, D))],
        out_shape=[jax.ShapeDtypeStruct((T, D), BF16), jax.ShapeDtypeStruct((T, DFF), BF16),
                   jax.ShapeDtypeStruct((T, D), F32), jax.ShapeDtypeStruct((1, D), F32),
                   jax.ShapeDtypeStruct((1, D), F32)],
        compiler_params=_params(1, VMEM_LIMIT_BIG),
    )(dh2, ff, h1, f1, w1g, w2g, gpost, gpre)


def _mix_bwd(dh1, o, us, yatt, lng, lnb, sgw, sgwt, sgbt, gatt, gsg, wout, gpm):
    T = dh1.shape[0]
    tm = _tile(T, 256)

    def body(dh1_ref, o_ref, us_ref, ya_ref, lng_ref, lnb_ref, w_ref, wt_ref, bt_ref, ga_ref, gs_ref, wo_ref, gp_ref,
             dob_ref, dya_ref, dus_ref, dw_ref, dbt_ref, dlng_ref, dlnb_ref, dga_ref, dgs_ref, dgp_ref,
             mixed_s, vnb_s, dvn_s):
        @pl.when(pl.program_id(0) == 0)
        def _():
            for r in (dw_ref, dbt_ref, dlng_ref, dlnb_ref, dga_ref, dgs_ref, dgp_ref):
                r[...] = jnp.zeros_like(r)

        is_a = lax.broadcasted_iota(jnp.int32, (1, 128), 1) < HD
        lane = lax.broadcasted_iota(jnp.int32, (1, 128), 1)
        do, dg = _rms_bwd(dh1_ref[...], o_ref[...], gp_ref[...])
        dgp_ref[...] += dg
        dob = do.astype(BF16)
        dob_ref[...] = dob
        dy = _nt(dob, wo_ref[...])
        datt, dg = _rms_bwd(dy[:, 0:AW], ya_ref[...], ga_ref[...])
        dga_ref[...] += dg
        dya_ref[...] = datt

        lng = lng_ref[...]
        u, vs, ug, tu, tv, vhat, rstd, mixed, ysg = _sgu_forward(us_ref, lng, lnb_ref[...], w_ref, bt_ref, mixed_s, vnb_s, tm)
        dysg, dg = _rms_bwd(dy[:, AW:D], ysg, gs_ref[...])
        dgs_ref[...] += dg
        dus_ref[:, 0:SW] = ((dysg * mixed) * _gelu_grad(u, tu)).astype(BF16)
        dmix = dysg * ug

        rr = lax.broadcasted_iota(jnp.int32, (CH, CH), 0)
        cc = lax.broadcasted_iota(jnp.int32, (CH, CH), 1)
        tril = cc <= rr
        triu = cc >= rr
        for jj in range(4):
            wta = jnp.where(triu, wt_ref[2 * jj], 0.0).astype(BF16)
            wtb = jnp.where(triu, wt_ref[2 * jj + 1], 0.0).astype(BF16)
            for ch in range(tm // CH):
                rs, cs = slice(ch * CH, (ch + 1) * CH), slice(jj * 128, (jj + 1) * 128)
                dm2 = dmix[rs, cs]
                dma = jnp.where(is_a, dm2, 0.0)
                dmb = jnp.where(is_a, 0.0, dm2)
                dma_b, dmb_b = dma.astype(BF16), dmb.astype(BF16)
                vn2 = vnb_s[rs, cs]
                dw_ref[2 * jj] += jnp.where(tril, _nt(dma_b, vn2), 0.0)
                dw_ref[2 * jj + 1] += jnp.where(tril, _nt(dmb_b, vn2), 0.0)
                dvn_s[rs, cs] = _nn(wta, dma_b) + _nn(wtb, dmb_b)
                dba = jnp.sum(dma, axis=1, keepdims=True)
                dbb = jnp.sum(dmb, axis=1, keepdims=True)
                dbt_ref[...] += jnp.where(lane == 2 * jj, dba, 0.0) + jnp.where(lane == 2 * jj + 1, dbb, 0.0)

        dvn = dvn_s[...]
        dlng_ref[...] += jnp.sum(dvn * vhat, axis=0, keepdims=True)
        dlnb_ref[...] += jnp.sum(dvn, axis=0, keepdims=True)
        dvh = dvn * lng
        dvg = rstd * (dvh - jnp.mean(dvh, axis=-1, keepdims=True) - vhat * jnp.mean(dvh * vhat, axis=-1, keepdims=True))
        dus_ref[:, SW:NUS] = (dvg * _gelu_grad(vs, tv)).astype(BF16)

    row = lambda n: pl.BlockSpec((tm, n), lambda i: (i, 0))
    return pl.pallas_call(
        body,
        name="mix_bwd",
        grid=(T // tm,),
        in_specs=[row(D), row(D), row(NUS), row(AW), _full((1, SW)), _full((1, SW)), _full((8, CH, CH)), _full((8, CH, CH)),
                  _full((CH, 8)), _full((1, AW)), _full((1, SW)), _full((D, D)), _full((1, D))],
        out_specs=[row(D), row(AW), row(NUS), _full((8, CH, CH)), _full((CH, 128)), _full((1, SW)), _full((1, SW)),
                   _full((1, AW)), _full((1, SW)), _full((1, D))],
        out_shape=[jax.ShapeDtypeStruct((T, D), BF16), jax.ShapeDtypeStruct((T, AW), F32), jax.ShapeDtypeStruct((T, NUS), BF16),
                   jax.ShapeDtypeStruct((8, CH, CH), F32), jax.ShapeDtypeStruct((CH, 128), F32),
                   jax.ShapeDtypeStruct((1, SW), F32), jax.ShapeDtypeStruct((1, SW), F32),
                   jax.ShapeDtypeStruct((1, AW), F32), jax.ShapeDtypeStruct((1, SW), F32), jax.ShapeDtypeStruct((1, D), F32)],
        scratch_shapes=[pltpu.VMEM((tm, SW), F32), pltpu.VMEM((tm, SW), BF16), pltpu.VMEM((tm, SW), F32)],
        compiler_params=_params(1),
    )(dh1, o, us, yatt, lng, lnb, sgw, sgwt, sgbt, gatt, gsg, wout, gpm)


def _attn_prep(qkv, dya, yatt, lsec, after):
    T = dya.shape[0]
    tm = _tile(T, 512)

    def body(q_ref, do_ref, o_ref, l_ref, after_ref, qw_ref, dow_ref):
        do = do_ref[...]
        feat = lax.broadcasted_iota(jnp.int32, (AW, 128), 0)
        head = lax.broadcasted_iota(jnp.int32, (AW, 128), 1)
        sel = jnp.where((feat >= head * HD) & (feat < (head + 1) * HD), 1.0, 0.0)
        delta = jnp.dot(do * o_ref[...], sel, precision=lax.Precision.HIGHEST, preferred_element_type=F32)
        for hp in range(4):
            cols = slice(hp * 128, (hp + 1) * 128)
            for hh in range(2):
                base = (2 * hp + hh) * 256
                lc = l_ref[:, hp * 128 + hh:hp * 128 + hh + 1]
                d_h = delta[:, 2 * hp + hh:2 * hp + hh + 1]
                qw_ref[:, base:base + 128] = _head_rows(q_ref[:, cols], hh, 0.125)
                qw_ref[:, base + 128:base + 256] = _aug_lanes(tm, AUG_A if hh == 0 else AUG_B, _split3(-lc))
                dow_ref[:, base:base + 128] = _head_rows(do[:, cols], hh, 1.0).astype(BF16)
                dow_ref[:, base + 128:base + 256] = _aug_lanes(tm, None, _split3(-d_h))

    row = lambda n: pl.BlockSpec((tm, n), lambda i: (i, 0))
    return pl.pallas_call(
        body,
        name="attn_prep",
        grid=(T // tm,),
        in_specs=[row(AW)] * 4 + [_full(after.shape)],
        out_specs=[row(4 * AW)] * 2,
        out_shape=[jax.ShapeDtypeStruct((T, 4 * AW), BF16)] * 2,
        compiler_params=_params(1),
    )(qkv, dya, yatt, lsec, after)


def _attn_bwd(kk, vk, kt, qw, dow):
    T = kk.shape[0]
    tq = _tile(T, 512)
    tk = tq
    nq = T // tq

    def body(kk_ref, vk_ref, kt_ref, qw_ref, dow_ref, dqt_ref, dcq_ref, dk_ref, dv_ref, dck_ref, dk_s, dv_s, dck_s):
        j, i = _fold_cols(pl.program_id(1), pl.program_id(2), nq)
        sub8 = lax.broadcasted_iota(jnp.int32, (8, 1), 0)
        lane = lax.broadcasted_iota(jnp.int32, (1, 128), 1)

        @pl.when((pl.program_id(1) == 0) & (pl.program_id(2) == 0))
        def _():
            dqt_ref[...] = jnp.zeros_like(dqt_ref)
            dcq_ref[...] = jnp.zeros_like(dcq_ref)

        @pl.when(i == j)
        def _():
            dk_s[...] = jnp.zeros_like(dk_s)
            dv_s[...] = jnp.zeros_like(dv_s)
            dck_s[...] = jnp.zeros_like(dck_s)

        def step(masked):
            cols = pl.ds(pl.multiple_of(i * tq, tq), tq)
            sub = lax.broadcasted_iota(jnp.int32, (128, 1), 0)
            if masked:
                causal = lax.broadcasted_iota(jnp.int32, (tk, tq), 0) <= lax.broadcasted_iota(jnp.int32, (tk, tq), 1)

            def logits(n):
                pair, base = n // 2, n * 256
                return (_nt(kk_ref[:, pair * 256:(pair + 1) * 256], qw_ref[:, base:base + 256]),
                        _nt(vk_ref[:, pair * 256:(pair + 1) * 256], dow_ref[:, base:base + 256]))

            ahead = logits(0)
            for pp in range(PAIRS):
                lanes = slice(pp * 128, (pp + 1) * 128)
                kt2 = kt_ref[lanes, :] * 0.125
                dcq = jnp.zeros((8, tq), F32)
                dck = jnp.zeros((tk, 128), F32)
                dv = jnp.zeros((tk, 128), F32)
                dk = jnp.zeros((tk, 128), F32)
                dqts = []
                for hh in range(2):
                    base = (2 * pp + hh) * 256
                    qw_h = qw_ref[:, base:base + 256]
                    dow_h = dow_ref[:, base:base + 256]
                    logp, dp = ahead
                    if 2 * pp + hh + 1 < 2 * PAIRS:
                        ahead = logits(2 * pp + hh + 1)
                    pr = jnp.exp(logp)
                    if masked:
                        pr = jnp.where(causal, pr, 0.0)
                    ds = pr * dp
                    ds_b = ds.astype(BF16)
                    dv = dv + _nn(pr.astype(BF16), dow_h[:, 0:128])
                    dk = dk + _nn(ds_b, qw_h[:, 0:128])
                    dqts.append(_nn(kt2, ds_b))
                    dcq = dcq + jnp.where(sub8 == hh, jnp.sum(ds, axis=0, keepdims=True), 0.0)
                    dck = dck - jnp.where(lane == hh, jnp.sum(ds, axis=1, keepdims=True), 0.0)
                dv_s[:, lanes] += dv
                dk_s[:, lanes] += dk
                dck_s[:, lanes] += dck
                dcq_ref[pp, :, cols] += dcq
                dqt_ref[lanes, cols] += jnp.where(sub < HD, dqts[0], dqts[1])

        @pl.when(i > j)
        def _():
            step(False)

        @pl.when(i == j)
        def _():
            step(True)

        @pl.when(i == nq - 1)
        def _():
            dk_ref[...] = dk_s[...].astype(BF16)
            dv_ref[...] = dv_s[...].astype(BF16)
            dck_ref[...] = dck_s[...]

    kj = lambda r, t: _fold_cols(r, t, nq)[0]
    qi = lambda r, t: _fold_cols(r, t, nq)[1]
    krow = lambda g, r, t: (kj(r, t), g)
    qrow = lambda g, r, t: (qi(r, t), g)
    return pl.pallas_call(
        body,
        name="attn_bwd",
        grid=_fold_grid(nq),
        in_specs=[
            pl.BlockSpec((tk, PAIRS * 256), krow),
            pl.BlockSpec((tk, PAIRS * 256), krow),
            pl.BlockSpec((PAIRS * 128, tk), lambda g, r, t: (g, kj(r, t))),
            pl.BlockSpec((tq, PAIRS * 512), qrow),
            pl.BlockSpec((tq, PAIRS * 512), qrow),
        ],
        out_specs=[
            pl.BlockSpec((PAIRS * 128, T), lambda g, r, t: (g, 0), pipeline_mode=pl.Buffered(1)),
            pl.BlockSpec((PAIRS, 8, T), lambda g, r, t: (g, 0, 0), pipeline_mode=pl.Buffered(1)),
            pl.BlockSpec((tk, PAIRS * 128), krow),
            pl.BlockSpec((tk, PAIRS * 128), krow),
            pl.BlockSpec((tk, PAIRS * 128), krow),
        ],
        out_shape=[jax.ShapeDtypeStruct((AW, T), F32), jax.ShapeDtypeStruct((4, 8, T), F32),
                   jax.ShapeDtypeStruct((T, AW), BF16), jax.ShapeDtypeStruct((T, AW), BF16),
                   jax.ShapeDtypeStruct((T, AW), F32)],
        scratch_shapes=[pltpu.VMEM((tk, PAIRS * 128), F32), pltpu.VMEM((tk, PAIRS * 128), F32),
                        pltpu.VMEM((tk, PAIRS * 128), F32)],
        compiler_params=_params(3),
    )(kk, vk, kt, qw, dow)


def _fgate_bwd(dcq, dck, fz, fb):
    T = dck.shape[0]
    tb = _tile(T, 512)
    nb = T // tb

    def body(dcq_ref, dck_ref, fz_ref, fb_ref, df_ref, dfb_ref, carry):
        @pl.when(pl.program_id(0) == 0)
        def _():
            carry[...] = jnp.zeros_like(carry)
            dfb_ref[...] = jnp.zeros_like(dfb_ref)

        dcv = dcq_ref[...] + dck_ref[...]
        r = lax.broadcasted_iota(jnp.int32, (tb, tb), 0)
        cc = lax.broadcasted_iota(jnp.int32, (tb, tb), 1)
        tri = (cc >= r).astype(F32)
        dlf = jnp.dot(tri, dcv, precision=lax.Precision.HIGHEST, preferred_element_type=F32) + carry[...]
        carry[...] = carry[...] + jnp.sum(dcv, axis=0, keepdims=True)
        lane = lax.broadcasted_iota(jnp.int32, (tb, FPAD), 1)
        df = jnp.where(lane < 8, dlf * jax.nn.sigmoid(-(fz_ref[...] + fb_ref[...])), 0.0)
        df_ref[...] = df.astype(BF16)
        dfb_ref[...] += jnp.sum(df, axis=0, keepdims=True)

    rev = pl.BlockSpec((tb, FPAD), lambda i: (nb - 1 - i, 0))
    return pl.pallas_call(
        body,
        name="fgate_bwd",
        grid=(nb,),
        in_specs=[rev, rev, rev, _full((1, FPAD))],
        out_specs=[rev, _full((1, FPAD))],
        out_shape=[jax.ShapeDtypeStruct((T, FPAD), BF16), jax.ShapeDtypeStruct((1, FPAD), F32)],
        scratch_shapes=[pltpu.VMEM((1, FPAD), F32)],
        compiler_params=_params(1),
    )(dcq, dck, fz, fb)


def _inproj_bwd(dq, dk, dv, dus, dfz, wp, x, dh1, g):
    T = x.shape[0]
    tm = _tile(T, 512)

    def body(dq_ref, dk_ref, dv_ref, dus_ref, dfz_ref, w_ref, x_ref, dh1_ref, g_ref, gx_ref, dg_ref):
        @pl.when(pl.program_id(0) == 0)
        def _():
            dg_ref[...] = jnp.zeros_like(dg_ref)

        da = _nt(dq_ref[...].astype(BF16), w_ref[:, 0:AW])
        da += _nt(dk_ref[...], w_ref[:, AW:2 * AW])
        da += _nt(dv_ref[...], w_ref[:, 2 * AW:NQKV])
        da += _nt(dus_ref[...], w_ref[:, NQKV:NQKV + NUS])
        da += _nt(dfz_ref[...], w_ref[:, NQKV + NUS:ZP])
        dx, dg = _rms_bwd(da, x_ref[...], g_ref[...])
        gx_ref[...] = dh1_ref[...] + dx
        dg_ref[...] += dg

    row = lambda n: pl.BlockSpec((tm, n), lambda i: (i, 0))
    return pl.pallas_call(
        body,
        name="inproj_bwd",
        grid=(T // tm,),
        in_specs=[row(AW), row(AW), row(AW), row(NUS), row(FPAD), _full((D, ZP)), row(D), row(D), _full((1, D))],
        out_specs=[row(D), _full((1, D))],
        out_shape=[jax.ShapeDtypeStruct((T, D), F32), jax.ShapeDtypeStruct((1, D), F32)],
        compiler_params=_params(1),
    )(dq, dk, dv, dus, dfz, wp, x, dh1, g)


def _sq_relu(f1):
    r = jnp.maximum(f1.astype(F32), 0.0)
    return (r * r).astype(BF16)


def _wgrad(a, bs, name, a_fn=None):
    T, K = a.shape
    tt = _tile(T, 512)
    nb = len(bs)

    def out_dims(b, layout):
        N = b.shape[1]
        if layout is None:
            return (K, N)
        return (N // layout[1], K, layout[1]) if layout[0] == "col" else (K // layout[1], layout[1], N)

    shapes = [out_dims(b, layout) for b, layout in bs]

    def body(*refs):
        a_ref, b_refs, o_refs = refs[0], refs[1:1 + nb], refs[1 + nb:]

        @pl.when(pl.program_id(0) == 0)
        def _():
            for o_ref in o_refs:
                o_ref[...] = jnp.zeros_like(o_ref)

        av = a_ref[...] if a_fn is None else a_fn(a_ref[...])
        at = av.astype(BF16).T
        for (b, layout), b_ref, o_ref in zip(bs, b_refs, o_refs):
            if layout is None:
                o_ref[...] += _nn(at, b_ref[...].astype(BF16))
            elif layout[0] == "col":
                n = layout[1]
                for k in range(b.shape[1] // n):
                    o_ref[k] += _nn(at, b_ref[:, k * n:(k + 1) * n].astype(BF16))
            else:
                n = layout[1]
                bv = b_ref[...].astype(BF16)
                for k in range(K // n):
                    o_ref[k] += _nn(at[k * n:(k + 1) * n, :], bv)

    once = lambda shape: pl.BlockSpec(shape, lambda t: (0,) * len(shape), pipeline_mode=pl.Buffered(1))
    res = pl.pallas_call(
        body,
        name=name,
        grid=(T // tt,),
        in_specs=[pl.BlockSpec((tt, K), lambda t: (t, 0))] + [pl.BlockSpec((tt, b.shape[1]), lambda t: (t, 0)) for b, _ in bs],
        out_specs=[once(s) for s in shapes],
        out_shape=[jax.ShapeDtypeStruct(s, F32) for s in shapes],
        compiler_params=_params(1, VMEM_LIMIT_BIG),
    )(a, *[b for b, _ in bs])
    return res


def _adam_math(w, g, m, v):
    m = ADAM_B1 * m + (1.0 - ADAM_B1) * g
    v = ADAM_B2 * v + (1.0 - ADAM_B2) * (g * g)
    m_hat = m / (1.0 - ADAM_B1 ** ADAM_STEP)
    v_hat = v / (1.0 - ADAM_B2 ** ADAM_STEP)
    delta = -ADAM_LR * (m_hat / (jnp.sqrt(v_hat) + ADAM_EPS) + ADAM_WD * w)
    return delta, m, v


def _adam(parts, w, m, v, name):
    R, C = w.shape
    br = 128 if R % 128 == 0 else R

    def body(p_ref, w_ref, m_ref, v_ref, g_ref, d_ref, nm_ref, nv_ref):
        g = p_ref[0].astype(F32)
        for s in range(1, NDEV):
            g = g + p_ref[s].astype(F32)
        g_ref[...] = g
        d_ref[...], nm_ref[...], nv_ref[...] = _adam_math(w_ref[...], g, m_ref[...], v_ref[...])

    blk = pl.BlockSpec((br, C), lambda i: (i, 0))
    return pl.pallas_call(
        body,
        name=name,
        grid=(R // br,),
        in_specs=[pl.BlockSpec((NDEV, br, C), lambda i: (0, i, 0)), blk, blk, blk],
        out_specs=[blk] * 4,
        out_shape=[jax.ShapeDtypeStruct((R, C), F32)] * 4,
        compiler_params=_params(1),
    )(parts, w, m, v)


_SMALL = (("sg_w", 8 * CH * CH), ("f_bias", 8), ("sg_ln_g", SW), ("sg_ln_b", SW), ("sg_b", 8 * CH), ("att_out_g", AW),
          ("sg_out_g", SW), ("pre_mix_g", D), ("post_mix_g", D), ("pre_ffn_g", D), ("post_ffn_g", D), ("ple_gate_b", D))
_SEG = 8 * 128


def _seg_rows(size):
    return 8 * (-(-size // _SEG))


def _pack(vals, loss_acc):
    parts = []
    for name, size in _SMALL:
        flat = vals[name].reshape(-1)
        rows = _seg_rows(size)
        parts.append(jnp.pad(flat, (0, rows * 128 - size)).reshape(rows, 128))
    parts.append(loss_acc)
    return jnp.concatenate(parts, axis=0)


def _adam_small(parts, ws, ms, vs):
    n = len(_SMALL)
    names = [name for name, _ in _SMALL]

    def body(*refs):
        p_ref, w_refs, m_refs, v_refs = refs[0], refs[1:1 + n], refs[1 + n:1 + 2 * n], refs[1 + 2 * n:1 + 3 * n]
        loss_ref, outs = refs[1 + 3 * n], refs[2 + 3 * n:]
        g_all = p_ref[0]
        for s in range(1, NDEV):
            g_all = g_all + p_ref[s]
        r = 0
        for k, (name, size) in enumerate(_SMALL):
            dst = [outs[kind * n + k] for kind in range(4)]

            def update(g, idx):
                vals = (g,) + _adam_math(w_refs[k][idx], g, m_refs[k][idx], v_refs[k][idx])
                for d, val in zip(dst, vals):
                    d[idx] = val

            if name == "sg_w":
                for grp in range(8):
                    update(g_all[r + grp * CH:r + (grp + 1) * CH, :], (0, grp))
            elif name == "sg_b":
                update(g_all[r:r + 8, :], (0,))
            elif name == "f_bias":
                update(g_all[r:r + 1, 0:8], (slice(None),))
            else:
                update(jnp.concatenate([g_all[r + q:r + q + 1, :] for q in range(size // 128)], axis=1), (slice(None),))
            r += _seg_rows(size)
        loss_ref[...] = g_all[r:r + 1, 0:1] * (0.5 / D)

    arrs = [parts] + [d[name] for d in (ws, ms, vs) for name in names]
    res = pl.pallas_call(
        body,
        name="adam_small",
        in_specs=[_full(a.shape) for a in arrs],
        out_specs=[_full((1, 1))] + [_full(ws[name].shape) for _ in range(4) for name in names],
        out_shape=[jax.ShapeDtypeStruct((1, 1), F32)] + [jax.ShapeDtypeStruct(ws[name].shape, F32) for _ in range(4) for name in names],
        compiler_params=pltpu.CompilerParams(vmem_limit_bytes=VMEM_LIMIT),
    )(*arrs)
    return res[0], {name: [res[1 + kind * n + k] for kind in range(4)] for k, name in enumerate(names)}


def kernel(x, p, w_in, f_bias, sg_ln_g, sg_ln_b, sg_w, sg_b, att_out_g, sg_out_g, w_out, pre_mix_g, post_mix_g, pre_ffn_g, post_ffn_g, w_ff1, w_ff2, ple_w, ple_gate_w, ple_gate_b, loss_target, m_w_in, m_f_bias, m_sg_ln_g, m_sg_ln_b, m_sg_w, m_sg_b, m_att_out_g, m_sg_out_g, m_w_out, m_pre_mix_g, m_post_mix_g, m_pre_ffn_g, m_post_ffn_g, m_w_ff1, m_w_ff2, m_ple_w, m_ple_gate_w, m_ple_gate_b, v_w_in, v_f_bias, v_sg_ln_g, v_sg_ln_b, v_sg_w, v_sg_b, v_att_out_g, v_sg_out_g, v_w_out, v_pre_mix_g, v_post_mix_g, v_pre_ffn_g, v_post_ffn_g, v_w_ff1, v_w_ff2, v_ple_w, v_ple_gate_w, v_ple_gate_b):
    small_w = dict(sg_w=sg_w, f_bias=f_bias, sg_ln_g=sg_ln_g, sg_ln_b=sg_ln_b, sg_b=sg_b, att_out_g=att_out_g,
                   sg_out_g=sg_out_g, pre_mix_g=pre_mix_g, post_mix_g=post_mix_g, pre_ffn_g=pre_ffn_g,
                   post_ffn_g=post_ffn_g, ple_gate_b=ple_gate_b)
    small_m = dict(sg_w=m_sg_w, f_bias=m_f_bias, sg_ln_g=m_sg_ln_g, sg_ln_b=m_sg_ln_b, sg_b=m_sg_b, att_out_g=m_att_out_g,
                   sg_out_g=m_sg_out_g, pre_mix_g=m_pre_mix_g, post_mix_g=m_post_mix_g, pre_ffn_g=m_pre_ffn_g,
                   post_ffn_g=m_post_ffn_g, ple_gate_b=m_ple_gate_b)
    small_v = dict(sg_w=v_sg_w, f_bias=v_f_bias, sg_ln_g=v_sg_ln_g, sg_ln_b=v_sg_ln_b, sg_b=v_sg_b, att_out_g=v_att_out_g,
                   sg_out_g=v_sg_out_g, pre_mix_g=v_pre_mix_g, post_mix_g=v_post_mix_g, pre_ffn_g=v_pre_ffn_g,
                   post_ffn_g=v_post_ffn_g, ple_gate_b=v_ple_gate_b)
    big = dict(w_in=(w_in, m_w_in, v_w_in), w_out=(w_out, m_w_out, v_w_out), w_ff1=(w_ff1, m_w_ff1, v_w_ff1),
               w_ff2=(w_ff2, m_w_ff2, v_w_ff2), ple_w=(ple_w, m_ple_w, v_ple_w),
               ple_gate_w=(ple_gate_w, m_ple_gate_w, v_ple_gate_w))

    xt, pt, tgt = x[0], p[0, 0], loss_target[0]
    ws = W_IN_COLS // NDEV

    (gw_in,) = _xchg([w_in[0].astype(BF16)], True, "gather_w_in")
    rest = _xchg_start([w_out[0].astype(BF16), w_ff1[0].astype(BF16), w_ff2[0].astype(BF16), ple_w[0].astype(BF16),
                        ple_gate_w[0].astype(BF16)], True, "gather_rest_start")
    win = jnp.transpose(gw_in, (1, 0, 2)).reshape(D, W_IN_COLS)
    wp = jnp.concatenate([win[:, 0:NQKV], win[:, NQKV + 8:W_IN_COLS], win[:, NQKV:NQKV + 8],
                          jnp.zeros((D, FPAD - 8), BF16)], axis=1)

    fb = jnp.pad(f_bias.astype(F32), ((0, 0), (0, FPAD - 8)))
    sgw = sg_w[0]
    sgwt = jnp.transpose(sg_w[0], (0, 2, 1))
    sgbt = jnp.transpose(sg_b[0])

    qkv, kt, vt, us, fz, ab = _inproj(xt, pre_mix_g + rest[4][0, 0], wp)
    ct, kk, vk = _fcum(fz, fb, qkv)
    yatt, lsec = _attn_fwd(qkv, kk, vt, ct)
    gw_out, gw1, gw2, gwpe, gwg = _xchg_wait(rest, True, yatt, "gather_rest_wait")
    wout = gw_out.reshape(D, D)
    wg = gwg.reshape(D, D)
    wpe = jnp.transpose(gwpe, (1, 0, 2)).reshape(PLE, D)
    h1, yb, o = _sgu_out(us, yatt, xt, sg_ln_g, sg_ln_b, sgw, sgbt, att_out_g, sg_out_g, wout, post_mix_g)
    c2b, f1b, ff, h2, h2b = _ffn_fwd(h1, pre_ffn_g, gw1, gw2, post_ffn_g)
    dh2, dpre, dpe, pb, dbg, loss_acc = _ple_loss(h2, pt, tgt, wg, ple_gate_b, wpe)

    (g_pe,) = _wgrad(pb, [(dpe, ("col", D // NDEV))], "wgrad_ple")
    (g_g,) = _wgrad(h2b, [(dpre, ("row", D // NDEV))], "wgrad_gate")
    dffb, df1, dh1, dgpostffn, dgpreffn = _ffn_bwd(dh2, ff, h1, f1b, gw1, gw2, post_ffn_g, pre_ffn_g)
    (g_1,) = _wgrad(c2b, [(df1, ("col", DFF // NDEV))], "wgrad_ff1")
    (g_2,) = _wgrad(f1b, [(dffb, ("row", DFF // NDEV))], "wgrad_ff2", a_fn=_sq_relu)
    early = _xchg_start([g_1, g_2, g_pe, g_g], False, "scatter_early_start")
    dob, dya, dus, dsgw, dsgbt, dlng, dlnb, dgatt, dgsg, dgpostmix = _mix_bwd(
        dh1, o, us, yatt, sg_ln_g, sg_ln_b, sgw, sgwt, sgbt, att_out_g, sg_out_g, wout, post_mix_g + early[4][0, 0])
    (g_out,) = _wgrad(yb, [(dob, ("row", D // NDEV))], "wgrad_out")
    mid = _xchg_start([g_out], False, "scatter_mid_start")
    qw, dow = _attn_prep(qkv, dya, yatt, lsec, mid[4])
    dqt, dcq, dk, dv, dck = _attn_bwd(kk, vk, kt, qw, dow)
    dq = jnp.transpose(dqt)
    dcq = jnp.pad(jnp.transpose(dcq[:, 0:2, :].reshape(8, -1)), ((0, 0), (0, FPAD - 8)))
    dck = jnp.pad(dck.reshape(-1, 4, 128)[:, :, 0:2].reshape(-1, 8), ((0, 0), (0, FPAD - 8)))
    dfz, dfb = _fgate_bwd(dcq, dck, fz, fb)

    gq, gk, gv, gus, gf = _wgrad(ab, [(dq, None), (dk, None), (dv, None), (dus, None), (dfz, None)], "wgrad_in")
    g_in = jnp.concatenate([gq, gk, gv, gf[:, 0:8], gus], axis=1)
    g_in = jnp.transpose(g_in.reshape(D, NDEV, ws), (1, 0, 2))
    late = _xchg_start([g_in.astype(BF16)], False, "scatter_late_start")
    grad_x, dgpremix = _inproj_bwd(dq, dk, dv, dus, dfz, wp, xt, dh1, pre_mix_g + late[4][0, 0])

    small_g = dict(sg_w=dsgw, f_bias=dfb[:, 0:8], sg_ln_g=dlng, sg_ln_b=dlnb, sg_b=jnp.transpose(dsgbt[:, 0:8]),
                   att_out_g=dgatt, sg_out_g=dgsg, pre_mix_g=dgpremix, post_mix_g=dgpostmix, pre_ffn_g=dgpreffn,
                   post_ffn_g=dgpostffn, ple_gate_b=dbg)

    r_1, r_2, r_pe, r_g = _xchg_wait(early, False, grad_x, "scatter_early_wait")
    (r_out,) = _xchg_wait(mid, False, grad_x, "scatter_mid_wait")
    (r_small,) = _xchg([_pack(small_g, loss_acc)], True, "gather_small_grads")

    res = {}

    def adam_big(name, parts):
        w, m, v = big[name]
        res[name] = [t[None] for t in _adam(parts, w[0], m[0], v[0], "adam_" + name)]

    for name, parts in (("w_out", r_out), ("w_ff1", r_1), ("w_ff2", r_2), ("ple_w", r_pe), ("ple_gate_w", r_g)):
        adam_big(name, parts)
    (r_in,) = _xchg_wait(late, False, res["w_ff1"][0], "scatter_late_wait")
    adam_big("w_in", r_in)
    loss, small = _adam_small(r_small, small_w, small_m, small_v)
    res.update(small)

    order = ["w_in", "f_bias", "sg_ln_g", "sg_ln_b", "sg_w", "sg_b", "att_out_g", "sg_out_g", "w_out", "pre_mix_g",
             "post_mix_g", "pre_ffn_g", "post_ffn_g", "w_ff1", "w_ff2", "ple_w", "ple_gate_w", "ple_gate_b"]
    outs = [loss[0, 0], grad_x[None]]
    for kind in range(4):
        outs += [res[name][kind] for name in order]
    return tuple(outs)
```

```python
import jax
import jax.numpy as jnp
from jax import lax
from jax.experimental import pallas as pl
from jax.experimental.pallas import tpu as pltpu

F32 = jnp.float32
BF16 = jnp.bfloat16

NDEV = 8
D = 1024
AW = 512
SW = 512
HD = 64
CH = 128
DFF = 4096
PLE = 256
NQKV = 3 * AW
NUS = 2 * SW
FPAD = 128
ZP = NQKV + NUS + FPAD
W_IN_COLS = 2568
EPS = 1e-6
MASKV = -1e30
GELU_K = 0.7978845608028654
GELU_C = 0.044715

ADAM_LR = 0.001
ADAM_B1 = 0.9
ADAM_B2 = 0.999
ADAM_EPS = 1e-08
ADAM_WD = 0.01
ADAM_STEP = 10

VMEM_LIMIT = 48 * 1024 * 1024
VMEM_LIMIT_BIG = 60 * 1024 * 1024


def _nn(a, b):
    return jnp.dot(a, b, preferred_element_type=F32)


def _nt(a, b):
    return lax.dot_general(a, b, (((1,), (1,)), ((), ())), preferred_element_type=F32)


def _tn(a, b):
    return lax.dot_general(a, b, (((0,), (0,)), ((), ())), preferred_element_type=F32)


def _tile(n, pref):
    return min(n, pref)


def _params(n_axes, vmem=VMEM_LIMIT):
    return pltpu.CompilerParams(dimension_semantics=("arbitrary",) * n_axes, vmem_limit_bytes=vmem)


def _full(shape):
    nd = len(shape)
    return pl.BlockSpec(shape, lambda *_: (0,) * nd)


def _rms_fwd(x, g):
    r = lax.rsqrt(jnp.mean(x * x, axis=-1, keepdims=True) + EPS)
    return x * r * g


def _rms_bwd(dy, x, g):
    n = x.shape[-1]
    r = lax.rsqrt(jnp.mean(x * x, axis=-1, keepdims=True) + EPS)
    u = dy * g
    s = jnp.sum(x * u, axis=-1, keepdims=True)
    dx = r * u - x * (r * r * r * (s * (1.0 / n)))
    dg = jnp.sum(dy * (x * r), axis=0, keepdims=True)
    return dx, dg


def _gelu(x):
    t = jnp.tanh(GELU_K * (x + GELU_C * (x * x * x)))
    return x * (0.5 * (1.0 + t)), t


def _gelu_grad(x, t):
    return 0.5 * (1.0 + t) + 0.5 * x * (1.0 - t * t) * (GELU_K * (1.0 + 3.0 * GELU_C * x * x))


def _xchg(arrs, gather, name):
    n = len(arrs)
    out_shapes = [jax.ShapeDtypeStruct(((NDEV,) + a.shape) if gather else a.shape, a.dtype) for a in arrs]

    def body(*refs):
        ins, outs = refs[:n], refs[n:2 * n]
        send, recv, loc = refs[2 * n:]
        x, y, c = lax.axis_index("x"), lax.axis_index("y"), lax.axis_index("c")
        me = 4 * x + 2 * y + c
        local = []
        for a in range(n):
            cp = pltpu.make_async_copy(ins[a] if gather else ins[a].at[me], outs[a].at[me], loc.at[a])
            cp.start()
            local.append(cp)
        remote = []
        for k in range(1, NDEV):
            px = 1 - x if (k >> 2) & 1 else x
            py = 1 - y if (k >> 1) & 1 else y
            pc = 1 - c if k & 1 else c
            peer = 4 * px + 2 * py + pc
            for a in range(n):
                cp = pltpu.make_async_remote_copy(
                    src_ref=ins[a] if gather else ins[a].at[peer],
                    dst_ref=outs[a].at[me],
                    send_sem=send.at[a, k - 1],
                    recv_sem=recv.at[a, k - 1],
                    device_id=(px, py, pc),
                    device_id_type=pl.DeviceIdType.MESH,
                )
                cp.start()
                remote.append(cp)
        for cp in remote:
            cp.wait_send()
            cp.wait_recv()
        for cp in local:
            cp.wait()

    hbm = pl.BlockSpec(memory_space=pltpu.HBM)
    return pl.pallas_call(
        body,
        name=name,
        out_shape=out_shapes,
        in_specs=[hbm] * n,
        out_specs=[hbm] * n,
        scratch_shapes=[
            pltpu.SemaphoreType.DMA((n, NDEV - 1)),
            pltpu.SemaphoreType.DMA((n, NDEV - 1)),
            pltpu.SemaphoreType.DMA((n,)),
        ],
    )(*arrs)


def _peers(x, y, c):
    out = []
    for k in range(1, NDEV):
        out.append((1 - x if (k >> 2) & 1 else x, 1 - y if (k >> 1) & 1 else y, 1 - c if k & 1 else c))
    return out


def _xchg_start(arrs, gather, name):
    n = len(arrs)
    me = 4 * lax.axis_index("x") + 2 * lax.axis_index("y") + lax.axis_index("c")
    lands = []
    for a in arrs:
        shape = ((NDEV,) + a.shape) if gather else a.shape
        own = a[None] if gather else lax.dynamic_slice_in_dim(a, me, 1, axis=0)
        lands.append(lax.dynamic_update_slice_in_dim(lax.empty(shape, a.dtype), own, me, axis=0))

    def body(*refs):
        ins, lnd = refs[:n], refs[n:2 * n]
        send, recv, token = refs[2 * n:3 * n], refs[3 * n:4 * n], refs[-1]
        x, y, c = lax.axis_index("x"), lax.axis_index("y"), lax.axis_index("c")
        mine = 4 * x + 2 * y + c
        for px, py, pc in _peers(x, y, c):
            peer = 4 * px + 2 * py + pc
            for a in range(n):
                pltpu.make_async_remote_copy(
                    src_ref=ins[a] if gather else ins[a].at[peer],
                    dst_ref=lnd[a].at[mine],
                    send_sem=send[a],
                    recv_sem=recv[a],
                    device_id=(px, py, pc),
                    device_id_type=pl.DeviceIdType.MESH,
                ).start()
        token[...] = jnp.zeros_like(token)

    hbm = pl.BlockSpec(memory_space=pltpu.HBM)
    sem = pl.BlockSpec(memory_space=pltpu.SEMAPHORE)
    res = pl.pallas_call(
        body,
        name=name,
        out_shape=(*[pltpu.SemaphoreType.DMA(())] * (2 * n),
                   *[pltpu.HBM(a.shape, a.dtype) for a in arrs], *[pltpu.HBM(l.shape, l.dtype) for l in lands],
                   jax.ShapeDtypeStruct((8, 128), F32)),
        in_specs=[hbm] * (2 * n),
        out_specs=(*([sem] * (2 * n)), *([hbm] * (2 * n)), pl.BlockSpec(memory_space=pltpu.VMEM)),
        input_output_aliases={i: 2 * n + i for i in range(2 * n)},
        compiler_params=pltpu.CompilerParams(has_side_effects=pltpu.SideEffectType.DATAFLOW_SIDE_EFFECTING),
    )(*[pltpu.with_memory_space_constraint(a, pltpu.HBM) for a in arrs],
      *[pltpu.with_memory_space_constraint(l, pltpu.HBM) for l in lands])
    return list(res[0:n]), list(res[n:2 * n]), list(res[2 * n:3 * n]), list(res[3 * n:4 * n]), res[-1]


def _xchg_wait(started, gather, after, name):
    send, recv, srcs, lands, _ = started
    n = len(srcs)

    def body(*refs):
        lnd = refs[n:2 * n]
        send, recv = refs[2 * n:3 * n], refs[3 * n:4 * n]
        me = (lax.axis_index("x"), lax.axis_index("y"), lax.axis_index("c"))
        for a in range(n):
            seven = lnd[a].at[pl.ds(0, NDEV - 1)]
            cp = pltpu.make_async_remote_copy(src_ref=seven, dst_ref=seven, send_sem=send[a], recv_sem=recv[a],
                                              device_id=me, device_id_type=pl.DeviceIdType.MESH)
            cp.wait_send()
            cp.wait_recv()

    hbm = pl.BlockSpec(memory_space=pltpu.HBM)
    sem = pl.BlockSpec(memory_space=pltpu.SEMAPHORE)
    res = pl.pallas_call(
        body,
        name=name,
        out_shape=tuple([pltpu.HBM(a.shape, a.dtype) for a in srcs] + [pltpu.HBM(l.shape, l.dtype) for l in lands]),
        in_specs=[hbm] * (2 * n) + [sem] * (2 * n) + [pl.BlockSpec(memory_space=pl.ANY)],
        out_specs=tuple([hbm] * (2 * n)),
        input_output_aliases={i: i for i in range(2 * n)},
        compiler_params=pltpu.CompilerParams(has_side_effects=pltpu.SideEffectType.DATAFLOW_SIDE_EFFECTING),
    )(*srcs, *lands, *send, *recv, after)
    return list(res[n:])


def _inproj(x, g, wp):
    T = x.shape[0]
    tm = _tile(T, 512)

    def body(x_ref, g_ref, w_ref, qkv_ref, kt_ref, vt_ref, us_ref, fz_ref, ab_ref):
        a = _rms_fwd(x_ref[...], g_ref[...]).astype(BF16)
        ab_ref[...] = a
        qkv_ref[:, 0:AW] = _nn(a, w_ref[:, 0:AW]).astype(BF16)
        kk = _nn(a, w_ref[:, AW:2 * AW])
        qkv_ref[:, AW:2 * AW] = kk.astype(BF16)
        kt_ref[...] = kk.T.astype(BF16)
        vv = _nn(a, w_ref[:, 2 * AW:NQKV])
        qkv_ref[:, 2 * AW:NQKV] = vv.astype(BF16)
        vt_ref[...] = vv.T.astype(BF16)
        us_ref[...] = _nn(a, w_ref[:, NQKV:NQKV + NUS])
        fz_ref[...] = _nn(a, w_ref[:, NQKV + NUS:ZP])

    row = lambda n: pl.BlockSpec((tm, n), lambda i: (i, 0))
    col = pl.BlockSpec((AW, tm), lambda i: (0, i))
    return pl.pallas_call(
        body,
        name="inproj",
        grid=(T // tm,),
        in_specs=[row(D), _full((1, D)), _full((D, ZP))],
        out_specs=[row(NQKV), col, col, row(NUS), row(FPAD), row(D)],
        out_shape=[
            jax.ShapeDtypeStruct((T, NQKV), BF16),
            jax.ShapeDtypeStruct((AW, T), BF16),
            jax.ShapeDtypeStruct((AW, T), BF16),
            jax.ShapeDtypeStruct((T, NUS), F32),
            jax.ShapeDtypeStruct((T, FPAD), F32),
            jax.ShapeDtypeStruct((T, D), BF16),
        ],
        compiler_params=_params(1),
    )(x, g, wp)


def _log_sigmoid(z):
    return jnp.minimum(z, 0.0) - jnp.log1p(jnp.exp(-jnp.abs(z)))


def _split3(x):
    hi = x.astype(BF16)
    r1 = x - hi.astype(F32)
    mid = r1.astype(BF16)
    lo = (r1 - mid.astype(F32)).astype(BF16)
    return hi, mid, lo


AUG_A, AUG_B, AUG_ONE = 0, 3, 6


def _aug_lanes(rows, first_one, pieces):
    lane = lax.broadcasted_iota(jnp.int32, (rows, 128), 1)
    out = jnp.zeros((rows, 128), F32)
    if first_one is not None:
        out = jnp.where((lane >= first_one) & (lane < first_one + 3), 1.0, out)
    for n, piece in enumerate(pieces):
        out = jnp.where(lane == AUG_ONE + n, piece.astype(F32), out)
    return out.astype(BF16)


def _fcum(fz, fb, qkv):
    T = fz.shape[0]
    tb = _tile(T, 512)

    def body(fz_ref, fb_ref, k_ref, v_ref, ct_ref, kk_ref, vk_ref, carry):
        @pl.when(pl.program_id(0) == 0)
        def _():
            carry[...] = jnp.zeros_like(carry)

        lf = _log_sigmoid(fz_ref[...] + fb_ref[...])
        r = lax.broadcasted_iota(jnp.int32, (tb, tb), 0)
        cc = lax.broadcasted_iota(jnp.int32, (tb, tb), 1)
        tri = (cc <= r).astype(F32)
        cs = jnp.dot(tri, lf, precision=lax.Precision.HIGHEST, preferred_element_type=F32) + carry[...]
        ct_ref[...] = cs.T[0:8, :]
        carry[...] = carry[...] + jnp.sum(lf, axis=0, keepdims=True)

        pieces = _split3(cs)
        src = lax.broadcasted_iota(jnp.int32, (128, 128), 0)
        dst = lax.broadcasted_iota(jnp.int32, (128, 128), 1)
        ones = jnp.where((dst[0:1, :] >= AUG_ONE) & (dst[0:1, :] < AUG_ONE + 3), 1.0, 0.0)
        for hp in range(4):
            aug = jnp.zeros((tb, 128), F32) + ones
            for n, piece in enumerate(pieces):
                sel = jnp.where(((src == 2 * hp) & (dst == AUG_A + n)) | ((src == 2 * hp + 1) & (dst == AUG_B + n)), -1.0, 0.0)
                aug = aug + _nn(piece, sel.astype(BF16))
            aug = aug.astype(BF16)
            kk_ref[:, hp * 256:hp * 256 + 128] = k_ref[:, hp * 128:(hp + 1) * 128]
            kk_ref[:, hp * 256 + 128:(hp + 1) * 256] = aug
            vk_ref[:, hp * 256:hp * 256 + 128] = v_ref[:, hp * 128:(hp + 1) * 128]
            vk_ref[:, hp * 256 + 128:(hp + 1) * 256] = aug

    row = lambda n, c: pl.BlockSpec((tb, n), lambda i: (i, c))
    return pl.pallas_call(
        body,
        name="fcum",
        grid=(T // tb,),
        in_specs=[row(FPAD, 0), _full((1, FPAD)), row(AW, 1), row(AW, 2)],
        out_specs=[pl.BlockSpec((8, tb), lambda i: (0, i)), row(2 * AW, 0), row(2 * AW, 0)],
        out_shape=[jax.ShapeDtypeStruct((8, T), F32), jax.ShapeDtypeStruct((T, 2 * AW), BF16),
                   jax.ShapeDtypeStruct((T, 2 * AW), BF16)],
        scratch_shapes=[pltpu.VMEM((1, FPAD), F32)],
        compiler_params=_params(1),
    )(fz, fb, qkv, qkv)


def _fold_rows(r, t, nq):
    if nq == 1:
        return r, t
    low = t <= r
    return jnp.where(low, r, nq - 1 - r), jnp.where(low, t, t - r - 1)


def _fold_cols(r, t, nq):
    if nq == 1:
        return r, t
    first = t < nq - r
    j = jnp.where(first, r, nq - 1 - r)
    return j, jnp.where(first, r + t, nq - 1 - r + (t - (nq - r)))


PAIRS = 4


def _fold_grid(nq):
    assert nq == 1 or nq % 2 == 0
    return (4 // PAIRS, 1, 1) if nq == 1 else (4 // PAIRS, nq // 2, nq + 1)


def _head_rows(x2, hh, scale):
    is_a = lax.broadcasted_iota(jnp.int32, (1, 128), 1) < HD
    keep = is_a if hh == 0 else jnp.logical_not(is_a)
    return jnp.where(keep, x2, jnp.zeros_like(x2)) * scale


def _attn_fwd(qkv, kk, vt, ct):
    T = qkv.shape[0]
    tq = _tile(T, 512)
    tk = tq
    nq = T // tq

    def body(q_ref, kk_ref, vt_ref, ctq_ref, o_ref, lsec_ref, qw_s, m_s, l_s, acc_s):
        i, j = _fold_rows(pl.program_id(1), pl.program_id(2), nq)
        sub8 = lax.broadcasted_iota(jnp.int32, (8, 1), 0)

        def cref_of(pp, hh):
            head = 2 * (PAIRS * pl.program_id(0) + pp) + hh
            return jnp.sum(jnp.where(sub8 == head, ctq_ref[:, 0:1], 0.0), axis=0, keepdims=True)

        @pl.when(j == 0)
        def _():
            for pp in range(PAIRS):
                q2 = q_ref[:, pp * 128:(pp + 1) * 128]
                for hh in range(2):
                    rows = slice(hh * tq, (hh + 1) * tq)
                    qw_s[pp, rows, 0:128] = _head_rows(q2, hh, 0.125)
                    qw_s[pp, rows, 128:256] = _aug_lanes(tq, AUG_A if hh == 0 else AUG_B, _split3(cref_of(pp, hh)))
            m_s[...] = jnp.full_like(m_s, MASKV)
            l_s[...] = jnp.zeros_like(l_s)
            acc_s[...] = jnp.zeros_like(acc_s)

        def step(masked):
            if masked:
                causal = lax.broadcasted_iota(jnp.int32, (tk, tq), 0) <= lax.broadcasted_iota(jnp.int32, (tk, tq), 1)
            logits = lambda pp: _nt(kk_ref[:, pp * 256:(pp + 1) * 256], qw_s[pp])
            s_next = logits(0)
            for pp in range(PAIRS):
                s2 = s_next
                if pp + 1 < PAIRS:
                    s_next = logits(pp + 1)
                vt2 = vt_ref[pp * 128:(pp + 1) * 128, :]
                for hh in range(2):
                    n = 2 * pp + hh
                    s = s2[:, hh * tq:(hh + 1) * tq]
                    if masked:
                        s = jnp.where(causal, s, MASKV)
                    m_prev = m_s[n]
                    m_new = jnp.maximum(m_prev, jnp.max(s, axis=0, keepdims=True))
                    pr = jnp.exp(s - m_new)
                    alpha = jnp.exp(m_prev - m_new)
                    l_s[n] = alpha * l_s[n] + jnp.sum(pr, axis=0, keepdims=True)
                    m_s[n] = m_new
                    acc_s[n] = alpha * acc_s[n] + _nn(vt2, pr.astype(BF16))

        @pl.when(j < i)
        def _():
            step(False)

        @pl.when(j == i)
        def _():
            step(True)
            sub = lax.broadcasted_iota(jnp.int32, (128, 1), 0)
            for pp in range(PAIRS):
                a, b = 2 * pp, 2 * pp + 1
                ot = jnp.where(sub < HD, acc_s[a] * (1.0 / l_s[a]), acc_s[b] * (1.0 / l_s[b]))
                o_ref[:, pp * 128:(pp + 1) * 128] = ot.T
                lrow = [m_s[2 * pp + hh] + jnp.log(l_s[2 * pp + hh]) - cref_of(pp, hh) for hh in range(2)]
                lsec_ref[:, pp * 128:(pp + 1) * 128] = jnp.where(sub == 0, lrow[0], jnp.where(sub == 1, lrow[1], 0.0)).T

    qi = lambda r, t: _fold_rows(r, t, nq)[0]
    kj = lambda r, t: _fold_rows(r, t, nq)[1]
    return pl.pallas_call(
        body,
        name="attn_fwd",
        grid=_fold_grid(nq),
        in_specs=[
            pl.BlockSpec((tq, PAIRS * 128), lambda g, r, t: (qi(r, t), g)),
            pl.BlockSpec((tk, PAIRS * 256), lambda g, r, t: (kj(r, t), g)),
            pl.BlockSpec((PAIRS * 128, tk), lambda g, r, t: (g, kj(r, t))),
            pl.BlockSpec((8, tq), lambda g, r, t: (0, qi(r, t))),
        ],
        out_specs=[pl.BlockSpec((tq, PAIRS * 128), lambda g, r, t: (qi(r, t), g))] * 2,
        out_shape=[jax.ShapeDtypeStruct((T, AW), F32)] * 2,
        scratch_shapes=[pltpu.VMEM((PAIRS, 2 * tq, 256), BF16), pltpu.VMEM((2 * PAIRS, 1, tq), F32),
                        pltpu.VMEM((2 * PAIRS, 1, tq), F32), pltpu.VMEM((2 * PAIRS, 128, tq), F32)],
        compiler_params=_params(3),
    )(qkv, kk, vt, ct)


def _sgu_forward(us_ref, lng, lnb, w_ref, bt_ref, mixed_s, vnb_s, tm):
    is_a = lax.broadcasted_iota(jnp.int32, (1, 128), 1) < HD
    u = us_ref[:, 0:SW]
    vs = us_ref[:, SW:NUS]
    ug, tu = _gelu(u)
    vg, tv = _gelu(vs)
    mu = jnp.mean(vg, axis=-1, keepdims=True)
    xc = vg - mu
    rstd = lax.rsqrt(jnp.mean(xc * xc, axis=-1, keepdims=True) + EPS)
    vhat = xc * rstd
    vnb_s[...] = (vhat * lng + lnb).astype(BF16)
    rr = lax.broadcasted_iota(jnp.int32, (CH, CH), 0)
    cc = lax.broadcasted_iota(jnp.int32, (CH, CH), 1)
    tril = cc <= rr
    for jj in range(4):
        wa = jnp.where(tril, w_ref[2 * jj], 0.0).astype(BF16)
        wb = jnp.where(tril, w_ref[2 * jj + 1], 0.0).astype(BF16)
        ba = bt_ref[:, 2 * jj:2 * jj + 1]
        bb = bt_ref[:, 2 * jj + 1:2 * jj + 2]
        for ch in range(tm // CH):
            rs, cs = slice(ch * CH, (ch + 1) * CH), slice(jj * 128, (jj + 1) * 128)
            vn2 = vnb_s[rs, cs]
            mixed_s[rs, cs] = jnp.where(is_a, _nn(wa, vn2) + ba, _nn(wb, vn2) + bb)
    mixed = mixed_s[...]
    return u, vs, ug, tu, tv, vhat, rstd, mixed, ug * mixed


def _sgu_out(us, yatt, x, lng, lnb, sgw, sgbt, gatt, gsg, wout, gpm):
    T = us.shape[0]
    tm = _tile(T, 256)

    def body(us_ref, ya_ref, x_ref, lng_ref, lnb_ref, w_ref, bt_ref, ga_ref, gs_ref, wo_ref, gp_ref,
             h1_ref, yb_ref, o_ref, mixed_s, vnb_s):
        ysg = _sgu_forward(us_ref, lng_ref[...], lnb_ref[...], w_ref, bt_ref, mixed_s, vnb_s, tm)[-1]
        yb_ref[:, 0:AW] = _rms_fwd(ya_ref[...], ga_ref[...]).astype(BF16)
        yb_ref[:, AW:D] = _rms_fwd(ysg, gs_ref[...]).astype(BF16)
        o = _nn(yb_ref[...], wo_ref[...])
        o_ref[...] = o
        h1_ref[...] = x_ref[...] + _rms_fwd(o, gp_ref[...])

    row = lambda n: pl.BlockSpec((tm, n), lambda i: (i, 0))
    return pl.pallas_call(
        body,
        name="sgu_out",
        grid=(T // tm,),
        in_specs=[row(NUS), row(AW), row(D), _full((1, SW)), _full((1, SW)), _full((8, CH, CH)), _full((CH, 8)),
                  _full((1, AW)), _full((1, SW)), _full((D, D)), _full((1, D))],
        out_specs=[row(D), row(D), row(D)],
        out_shape=[jax.ShapeDtypeStruct((T, D), F32), jax.ShapeDtypeStruct((T, D), BF16), jax.ShapeDtypeStruct((T, D), F32)],
        scratch_shapes=[pltpu.VMEM((tm, SW), F32), pltpu.VMEM((tm, SW), BF16)],
        compiler_params=_params(1),
    )(us, yatt, x, lng, lnb, sgw, sgbt, gatt, gsg, wout, gpm)


def _ffn_fwd(h1, gpre, w1g, w2g, gpost):
    T = h1.shape[0]
    tm = _tile(T, 512)
    nb, hb = w1g.shape[0], w1g.shape[2]

    def body(h1_ref, gpre_ref, w1_ref, w2_ref, gpost_ref, c2_ref, f1_ref, ff_ref, h2_ref):
        h1 = h1_ref[...]
        c2 = _rms_fwd(h1, gpre_ref[...]).astype(BF16)
        c2_ref[...] = c2
        ff = jnp.zeros((tm, D), F32)
        for j in range(nb):
            f1 = _nn(c2, w1_ref[j])
            f1_ref[:, j * hb:(j + 1) * hb] = f1.astype(BF16)
            r = jnp.maximum(f1, 0.0)
            ff = ff + _nn((r * r).astype(BF16), w2_ref[j])
        ff_ref[...] = ff
        h2_ref[...] = h1 + _rms_fwd(ff, gpost_ref[...])

    row = lambda n: pl.BlockSpec((tm, n), lambda i: (i, 0))
    once = lambda shape: pl.BlockSpec(shape, lambda i: (0,) * len(shape), pipeline_mode=pl.Buffered(1))
    return pl.pallas_call(
        body,
        name="ffn_fwd",
        grid=(T // tm,),
        in_specs=[row(D), _full((1, D)), once((nb, D, hb)), once((nb, hb, D)), _full((1, D))],
        out_specs=[row(D), row(DFF), row(D), row(D)],
        out_shape=[jax.ShapeDtypeStruct((T, D), BF16), jax.ShapeDtypeStruct((T, DFF), BF16),
                   jax.ShapeDtypeStruct((T, D), F32), jax.ShapeDtypeStruct((T, D), F32)],
        compiler_params=_params(1, VMEM_LIMIT_BIG),
    )(h1, gpre, w1g, w2g, gpost)


def _ple_loss(h2, p, tgt, wg, bg, wpe):
    T = h2.shape[0]
    tm = _tile(T, 512)

    def body(h2_ref, p_ref, t_ref, wg_ref, bg_ref, wpe_ref, dh2_ref, dbg_ref, loss_ref, dwg_ref, dwpe_ref):
        @pl.when(pl.program_id(0) == 0)
        def _():
            for r in (dbg_ref, loss_ref, dwg_ref, dwpe_ref):
                r[...] = jnp.zeros_like(r)

        h2 = h2_ref[...]
        h2b = h2.astype(BF16)
        gate = jax.nn.sigmoid(_nn(h2b, wg_ref[...]) + bg_ref[...])
        pb = p_ref[...].astype(BF16)
        pe = _nn(pb, wpe_ref[...])
        diff = (h2 + gate * pe) - t_ref[...]
        loss_ref[...] += jnp.sum(diff * diff)
        dh3 = diff * (1.0 / D)
        dpre = (dh3 * pe) * (gate * (1.0 - gate))
        dpre_b = dpre.astype(BF16)
        dbg_ref[...] += jnp.sum(dpre, axis=0, keepdims=True)
        dh2_ref[...] = dh3 + _nt(dpre_b, wg_ref[...])
        dwg_ref[...] += _nn(h2b.T, dpre_b)
        dwpe_ref[...] += _nn(pb.T, (dh3 * gate).astype(BF16))

    row = lambda n: pl.BlockSpec((tm, n), lambda i: (i, 0))
    once = lambda shape: pl.BlockSpec(shape, lambda i: (0,) * len(shape), pipeline_mode=pl.Buffered(1))
    return pl.pallas_call(
        body,
        name="ple_loss",
        grid=(T // tm,),
        in_specs=[row(D), row(PLE), row(D), _full((D, D)), _full((1, D)), _full((PLE, D))],
        out_specs=[row(D), _full((1, D)), _full((8, 128)), once((D, D)), once((PLE, D))],
        out_shape=[jax.ShapeDtypeStruct((T, D), F32), jax.ShapeDtypeStruct((1, D), F32),
                   jax.ShapeDtypeStruct((8, 128), F32), jax.ShapeDtypeStruct((D, D), F32),
                   jax.ShapeDtypeStruct((PLE, D), F32)],
        compiler_params=_params(1),
    )(h2, p, tgt, wg, bg, wpe)


def _ffn_bwd(dh2, ff, h1, f1, w1g, w2g, gpost, gpre):
    T = dh2.shape[0]
    tm = _tile(T, 512)
    nb, hb = w1g.shape[0], w1g.shape[2]

    def body(dh2_ref, ff_ref, h1_ref, f1_ref, w1_ref, w2_ref, gpost_ref, gpre_ref,
             dffb_ref, df1_ref, dh1_ref, dgpost_ref, dgpre_ref):
        @pl.when(pl.program_id(0) == 0)
        def _():
            dgpost_ref[...] = jnp.zeros_like(dgpost_ref)
            dgpre_ref[...] = jnp.zeros_like(dgpre_ref)

        dh2 = dh2_ref[...]
        dff, dg = _rms_bwd(dh2, ff_ref[...], gpost_ref[...])
        dffb = dff.astype(BF16)
        dffb_ref[...] = dffb
        dgpost_ref[...] += dg
        dc2 = jnp.zeros((tm, D), F32)
        for j in range(nb):
            cols = slice(j * hb, (j + 1) * hb)
            dact = _nt(dffb, w2_ref[j])
            df1 = (dact * (2.0 * jnp.maximum(f1_ref[:, cols].astype(F32), 0.0))).astype(BF16)
            df1_ref[:, cols] = df1
            dc2 = dc2 + _nt(df1, w1_ref[j])
        dx, dg = _rms_bwd(dc2, h1_ref[...], gpre_ref[...])
        dh1_ref[...] = dh2 + dx
        dgpre_ref[...] += dg

    row = lambda n: pl.BlockSpec((tm, n), lambda i: (i, 0))
    once = lambda shape: pl.BlockSpec(shape, lambda i: (0,) * len(shape), pipeline_mode=pl.Buffered(1))
    return pl.pallas_call(
        body,
        name="ffn_bwd",
        grid=(T // tm,),
        in_specs=[row(D), row(D), row(D), row(DFF), once((nb, D, hb)), once((nb, hb, D)), _full((1, D)), _full((1, D))],
        out_specs=[row(D), row(DFF), row(D), _full((1, D)), _full((1, D))],
        out_shape=[jax.ShapeDtypeStruct((T, D), BF16), jax.ShapeDtypeStruct((T, DFF), BF16),
                   jax.ShapeDtypeStruct((T, D), F32), jax.ShapeDtypeStruct((1, D), F32),
                   jax.ShapeDtypeStruct((1, D), F32)],
        compiler_params=_params(1, VMEM_LIMIT_BIG),
    )(dh2, ff, h1, f1, w1g, w2g, gpost, gpre)


def _mix_bwd(dh1, o, us, yatt, lng, lnb, sgw, sgwt, sgbt, gatt, gsg, wout, gpm):
    T = dh1.shape[0]
    tm = _tile(T, 256)

    def body(dh1_ref, o_ref, us_ref, ya_ref, lng_ref, lnb_ref, w_ref, wt_ref, bt_ref, ga_ref, gs_ref, wo_ref, gp_ref,
             dob_ref, dya_ref, dus_ref, dw_ref, dbt_ref, dlng_ref, dlnb_ref, dga_ref, dgs_ref, dgp_ref,
             mixed_s, vnb_s, dvn_s):
        @pl.when(pl.program_id(0) == 0)
        def _():
            for r in (dw_ref, dbt_ref, dlng_ref, dlnb_ref, dga_ref, dgs_ref, dgp_ref):
                r[...] = jnp.zeros_like(r)

        is_a = lax.broadcasted_iota(jnp.int32, (1, 128), 1) < HD
        lane = lax.broadcasted_iota(jnp.int32, (1, 128), 1)
        do, dg = _rms_bwd(dh1_ref[...], o_ref[...], gp_ref[...])
        dgp_ref[...] += dg
        dob = do.astype(BF16)
        dob_ref[...] = dob
        dy = _nt(dob, wo_ref[...])
        datt, dg = _rms_bwd(dy[:, 0:AW], ya_ref[...], ga_ref[...])
        dga_ref[...] += dg
        dya_ref[...] = datt

        lng = lng_ref[...]
        u, vs, ug, tu, tv, vhat, rstd, mixed, ysg = _sgu_forward(us_ref, lng, lnb_ref[...], w_ref, bt_ref, mixed_s, vnb_s, tm)
        dysg, dg = _rms_bwd(dy[:, AW:D], ysg, gs_ref[...])
        dgs_ref[...] += dg
        dus_ref[:, 0:SW] = ((dysg * mixed) * _gelu_grad(u, tu)).astype(BF16)
        dmix = dysg * ug

        rr = lax.broadcasted_iota(jnp.int32, (CH, CH), 0)
        cc = lax.broadcasted_iota(jnp.int32, (CH, CH), 1)
        tril = cc <= rr
        triu = cc >= rr
        for jj in range(4):
            wta = jnp.where(triu, wt_ref[2 * jj], 0.0).astype(BF16)
            wtb = jnp.where(triu, wt_ref[2 * jj + 1], 0.0).astype(BF16)
            for ch in range(tm // CH):
                rs, cs = slice(ch * CH, (ch + 1) * CH), slice(jj * 128, (jj + 1) * 128)
                dm2 = dmix[rs, cs]
                dma = jnp.where(is_a, dm2, 0.0)
                dmb = jnp.where(is_a, 0.0, dm2)
                dma_b, dmb_b = dma.astype(BF16), dmb.astype(BF16)
                vn2 = vnb_s[rs, cs]
                dw_ref[2 * jj] += jnp.where(tril, _nt(dma_b, vn2), 0.0)
                dw_ref[2 * jj + 1] += jnp.where(tril, _nt(dmb_b, vn2), 0.0)
                dvn_s[rs, cs] = _nn(wta, dma_b) + _nn(wtb, dmb_b)
                dba = jnp.sum(dma, axis=1, keepdims=True)
                dbb = jnp.sum(dmb, axis=1, keepdims=True)
                dbt_ref[...] += jnp.where(lane == 2 * jj, dba, 0.0) + jnp.where(lane == 2 * jj + 1, dbb, 0.0)

        dvn = dvn_s[...]
        dlng_ref[...] += jnp.sum(dvn * vhat, axis=0, keepdims=True)
        dlnb_ref[...] += jnp.sum(dvn, axis=0, keepdims=True)
        dvh = dvn * lng
        dvg = rstd * (dvh - jnp.mean(dvh, axis=-1, keepdims=True) - vhat * jnp.mean(dvh * vhat, axis=-1, keepdims=True))
        dus_ref[:, SW:NUS] = (dvg * _gelu_grad(vs, tv)).astype(BF16)

    row = lambda n: pl.BlockSpec((tm, n), lambda i: (i, 0))
    return pl.pallas_call(
        body,
        name="mix_bwd",
        grid=(T // tm,),
        in_specs=[row(D), row(D), row(NUS), row(AW), _full((1, SW)), _full((1, SW)), _full((8, CH, CH)), _full((8, CH, CH)),
                  _full((CH, 8)), _full((1, AW)), _full((1, SW)), _full((D, D)), _full((1, D))],
        out_specs=[row(D), row(AW), row(NUS), _full((8, CH, CH)), _full((CH, 128)), _full((1, SW)), _full((1, SW)),
                   _full((1, AW)), _full((1, SW)), _full((1, D))],
        out_shape=[jax.ShapeDtypeStruct((T, D), BF16), jax.ShapeDtypeStruct((T, AW), F32), jax.ShapeDtypeStruct((T, NUS), BF16),
                   jax.ShapeDtypeStruct((8, CH, CH), F32), jax.ShapeDtypeStruct((CH, 128), F32),
                   jax.ShapeDtypeStruct((1, SW), F32), jax.ShapeDtypeStruct((1, SW), F32),
                   jax.ShapeDtypeStruct((1, AW), F32), jax.ShapeDtypeStruct((1, SW), F32), jax.ShapeDtypeStruct((1, D), F32)],
        scratch_shapes=[pltpu.VMEM((tm, SW), F32), pltpu.VMEM((tm, SW), BF16), pltpu.VMEM((tm, SW), F32)],
        compiler_params=_params(1),
    )(dh1, o, us, yatt, lng, lnb, sgw, sgwt, sgbt, gatt, gsg, wout, gpm)


def _attn_prep(qkv, dya, yatt, lsec, after):
    T = dya.shape[0]
    tm = _tile(T, 512)

    def body(q_ref, do_ref, o_ref, l_ref, after_ref, qw_ref, dow_ref):
        do = do_ref[...]
        feat = lax.broadcasted_iota(jnp.int32, (AW, 128), 0)
        head = lax.broadcasted_iota(jnp.int32, (AW, 128), 1)
        sel = jnp.where((feat >= head * HD) & (feat < (head + 1) * HD), 1.0, 0.0)
        delta = jnp.dot(do * o_ref[...], sel, precision=lax.Precision.HIGHEST, preferred_element_type=F32)
        for hp in range(4):
            cols = slice(hp * 128, (hp + 1) * 128)
            for hh in range(2):
                base = (2 * hp + hh) * 256
                lc = l_ref[:, hp * 128 + hh:hp * 128 + hh + 1]
                d_h = delta[:, 2 * hp + hh:2 * hp + hh + 1]
                qw_ref[:, base:base + 128] = _head_rows(q_ref[:, cols], hh, 0.125)
                qw_ref[:, base + 128:base + 256] = _aug_lanes(tm, AUG_A if hh == 0 else AUG_B, _split3(-lc))
                dow_ref[:, base:base + 128] = _head_rows(do[:, cols], hh, 1.0).astype(BF16)
                dow_ref[:, base + 128:base + 256] = _aug_lanes(tm, None, _split3(-d_h))

    row = lambda n: pl.BlockSpec((tm, n), lambda i: (i, 0))
    return pl.pallas_call(
        body,
        name="attn_prep",
        grid=(T // tm,),
        in_specs=[row(AW)] * 4 + [_full(after.shape)],
        out_specs=[row(4 * AW)] * 2,
        out_shape=[jax.ShapeDtypeStruct((T, 4 * AW), BF16)] * 2,
        compiler_params=_params(1),
    )(qkv, dya, yatt, lsec, after)


def _attn_bwd(kk, vk, kt, qw, dow):
    T = kk.shape[0]
    tq = _tile(T, 512)
    tk = tq
    nq = T // tq

    def body(kk_ref, vk_ref, kt_ref, qw_ref, dow_ref, dqt_ref, dcq_ref, dk_ref, dv_ref, dck_ref, dk_s, dv_s, dck_s):
        j, i = _fold_cols(pl.program_id(1), pl.program_id(2), nq)
        sub8 = lax.broadcasted_iota(jnp.int32, (8, 1), 0)
        lane = lax.broadcasted_iota(jnp.int32, (1, 128), 1)

        @pl.when((pl.program_id(1) == 0) & (pl.program_id(2) == 0))
        def _():
            dqt_ref[...] = jnp.zeros_like(dqt_ref)
            dcq_ref[...] = jnp.zeros_like(dcq_ref)

        @pl.when(i == j)
        def _():
            dk_s[...] = jnp.zeros_like(dk_s)
            dv_s[...] = jnp.zeros_like(dv_s)
            dck_s[...] = jnp.zeros_like(dck_s)

        def step(masked):
            cols = pl.ds(pl.multiple_of(i * tq, tq), tq)
            sub = lax.broadcasted_iota(jnp.int32, (128, 1), 0)
            if masked:
                causal = lax.broadcasted_iota(jnp.int32, (tk, tq), 0) <= lax.broadcasted_iota(jnp.int32, (tk, tq), 1)

            def logits(n):
                pair, base = n // 2, n * 256
                return (_nt(kk_ref[:, pair * 256:(pair + 1) * 256], qw_ref[:, base:base + 256]),
                        _nt(vk_ref[:, pair * 256:(pair + 1) * 256], dow_ref[:, base:base + 256]))

            ahead = logits(0)
            for pp in range(PAIRS):
                lanes = slice(pp * 128, (pp + 1) * 128)
                kt2 = kt_ref[lanes, :] * 0.125
                dcq = jnp.zeros((8, tq), F32)
                dck = jnp.zeros((tk, 128), F32)
                dv = jnp.zeros((tk, 128), F32)
                dk = jnp.zeros((tk, 128), F32)
                dqts = []
                for hh in range(2):
                    base = (2 * pp + hh) * 256
                    qw_h = qw_ref[:, base:base + 256]
                    dow_h = dow_ref[:, base:base + 256]
                    logp, dp = ahead
                    if 2 * pp + hh + 1 < 2 * PAIRS:
                        ahead = logits(2 * pp + hh + 1)
                    pr = jnp.exp(logp)
                    if masked:
                        pr = jnp.where(causal, pr, 0.0)
                    ds = pr * dp
                    ds_b = ds.astype(BF16)
                    dv = dv + _nn(pr.astype(BF16), dow_h[:, 0:128])
                    dk = dk + _nn(ds_b, qw_h[:, 0:128])
                    dqts.append(_nn(kt2, ds_b))
                    dcq = dcq + jnp.where(sub8 == hh, jnp.sum(ds, axis=0, keepdims=True), 0.0)
                    dck = dck - jnp.where(lane == hh, jnp.sum(ds, axis=1, keepdims=True), 0.0)
                dv_s[:, lanes] += dv
                dk_s[:, lanes] += dk
                dck_s[:, lanes] += dck
                dcq_ref[pp, :, cols] += dcq
                dqt_ref[lanes, cols] += jnp.where(sub < HD, dqts[0], dqts[1])

        @pl.when(i > j)
        def _():
            step(False)

        @pl.when(i == j)
        def _():
            step(True)

        @pl.when(i == nq - 1)
        def _():
            dk_ref[...] = dk_s[...].astype(BF16)
            dv_ref[...] = dv_s[...].astype(BF16)
            dck_ref[...] = dck_s[...]

    kj = lambda r, t: _fold_cols(r, t, nq)[0]
    qi = lambda r, t: _fold_cols(r, t, nq)[1]
    krow = lambda g, r, t: (kj(r, t), g)
    qrow = lambda g, r, t: (qi(r, t), g)
    return pl.pallas_call(
        body,
        name="attn_bwd",
        grid=_fold_grid(nq),
        in_specs=[
            pl.BlockSpec((tk, PAIRS * 256), krow),
            pl.BlockSpec((tk, PAIRS * 256), krow),
            pl.BlockSpec((PAIRS * 128, tk), lambda g, r, t: (g, kj(r, t))),
            pl.BlockSpec((tq, PAIRS * 512), qrow),
            pl.BlockSpec((tq, PAIRS * 512), qrow),
        ],
        out_specs=[
            pl.BlockSpec((PAIRS * 128, T), lambda g, r, t: (g, 0), pipeline_mode=pl.Buffered(1)),
            pl.BlockSpec((PAIRS, 8, T), lambda g, r, t: (g, 0, 0), pipeline_mode=pl.Buffered(1)),
            pl.BlockSpec((tk, PAIRS * 128), krow),
            pl.BlockSpec((tk, PAIRS * 128), krow),
            pl.BlockSpec((tk, PAIRS * 128), krow),
        ],
        out_shape=[jax.ShapeDtypeStruct((AW, T), F32), jax.ShapeDtypeStruct((4, 8, T), F32),
                   jax.ShapeDtypeStruct((T, AW), BF16), jax.ShapeDtypeStruct((T, AW), BF16),
                   jax.ShapeDtypeStruct((T, AW), F32)],
        scratch_shapes=[pltpu.VMEM((tk, PAIRS * 128), F32), pltpu.VMEM((tk, PAIRS * 128), F32),
                        pltpu.VMEM((tk, PAIRS * 128), F32)],
        compiler_params=_params(3),
    )(kk, vk, kt, qw, dow)


def _fgate_bwd(dcq, dck, fz, fb):
    T = dck.shape[0]
    tb = _tile(T, 512)
    nb = T // tb

    def body(dcq_ref, dck_ref, fz_ref, fb_ref, df_ref, dfb_ref, carry):
        @pl.when(pl.program_id(0) == 0)
        def _():
            carry[...] = jnp.zeros_like(carry)
            dfb_ref[...] = jnp.zeros_like(dfb_ref)

        dcv = dcq_ref[...] + dck_ref[...]
        r = lax.broadcasted_iota(jnp.int32, (tb, tb), 0)
        cc = lax.broadcasted_iota(jnp.int32, (tb, tb), 1)
        tri = (cc >= r).astype(F32)
        dlf = jnp.dot(tri, dcv, precision=lax.Precision.HIGHEST, preferred_element_type=F32) + carry[...]
        carry[...] = carry[...] + jnp.sum(dcv, axis=0, keepdims=True)
        lane = lax.broadcasted_iota(jnp.int32, (tb, FPAD), 1)
        df = jnp.where(lane < 8, dlf * jax.nn.sigmoid(-(fz_ref[...] + fb_ref[...])), 0.0)
        df_ref[...] = df.astype(BF16)
        dfb_ref[...] += jnp.sum(df, axis=0, keepdims=True)

    rev = pl.BlockSpec((tb, FPAD), lambda i: (nb - 1 - i, 0))
    return pl.pallas_call(
        body,
        name="fgate_bwd",
        grid=(nb,),
        in_specs=[rev, rev, rev, _full((1, FPAD))],
        out_specs=[rev, _full((1, FPAD))],
        out_shape=[jax.ShapeDtypeStruct((T, FPAD), BF16), jax.ShapeDtypeStruct((1, FPAD), F32)],
        scratch_shapes=[pltpu.VMEM((1, FPAD), F32)],
        compiler_params=_params(1),
    )(dcq, dck, fz, fb)


def _inproj_bwd(dqt, dk, dv, dus, dfz, wp, wqt, x, dh1, g):
    T = x.shape[0]
    tm = _tile(T, 512)

    def body(dq_ref, dk_ref, dv_ref, dus_ref, dfz_ref, w_ref, wqt_ref, x_ref, dh1_ref, g_ref, gx_ref, dg_ref):
        @pl.when(pl.program_id(0) == 0)
        def _():
            dg_ref[...] = jnp.zeros_like(dg_ref)

        da = _tn(dq_ref[...].astype(BF16), wqt_ref[...])
        da += _nt(dk_ref[...], w_ref[:, AW:2 * AW])
        da += _nt(dv_ref[...], w_ref[:, 2 * AW:NQKV])
        da += _nt(dus_ref[...], w_ref[:, NQKV:NQKV + NUS])
        da += _nt(dfz_ref[...], w_ref[:, NQKV + NUS:ZP])
        dx, dg = _rms_bwd(da, x_ref[...], g_ref[...])
        gx_ref[...] = dh1_ref[...] + dx
        dg_ref[...] += dg

    row = lambda n: pl.BlockSpec((tm, n), lambda i: (i, 0))
    return pl.pallas_call(
        body,
        name="inproj_bwd",
        grid=(T // tm,),
        in_specs=[pl.BlockSpec((AW, tm), lambda i: (0, i)), row(AW), row(AW), row(NUS), row(FPAD), _full((D, ZP)),
                  _full((AW, D)), row(D), row(D), _full((1, D))],
        out_specs=[row(D), _full((1, D))],
        out_shape=[jax.ShapeDtypeStruct((T, D), F32), jax.ShapeDtypeStruct((1, D), F32)],
        compiler_params=_params(1),
    )(dqt, dk, dv, dus, dfz, wp, wqt, x, dh1, g)


def _sq_relu(f1):
    r = jnp.maximum(f1.astype(F32), 0.0)
    return (r * r).astype(BF16)


def _wgrad(a, bs, name, a_fn=None):
    T, K = a.shape
    tt = _tile(T, 512)
    nb = len(bs)

    def out_dims(b, layout):
        if layout == "t":
            return (K, b.shape[0])
        N = b.shape[1]
        if layout is None:
            return (K, N)
        return (N // layout[1], K, layout[1]) if layout[0] == "col" else (K // layout[1], layout[1], N)

    shapes = [out_dims(b, layout) for b, layout in bs]

    def body(*refs):
        a_ref, b_refs, o_refs = refs[0], refs[1:1 + nb], refs[1 + nb:]

        @pl.when(pl.program_id(0) == 0)
        def _():
            for o_ref in o_refs:
                o_ref[...] = jnp.zeros_like(o_ref)

        av = a_ref[...] if a_fn is None else a_fn(a_ref[...])
        at = av.astype(BF16).T
        for (b, layout), b_ref, o_ref in zip(bs, b_refs, o_refs):
            if layout is None:
                o_ref[...] += _nn(at, b_ref[...].astype(BF16))
            elif layout == "t":
                o_ref[...] += _nt(at, b_ref[...].astype(BF16))
            elif layout[0] == "col":
                n = layout[1]
                for k in range(b.shape[1] // n):
                    o_ref[k] += _nn(at, b_ref[:, k * n:(k + 1) * n].astype(BF16))
            else:
                n = layout[1]
                bv = b_ref[...].astype(BF16)
                for k in range(K // n):
                    o_ref[k] += _nn(at[k * n:(k + 1) * n, :], bv)

    once = lambda shape: pl.BlockSpec(shape, lambda t: (0,) * len(shape), pipeline_mode=pl.Buffered(1))
    res = pl.pallas_call(
        body,
        name=name,
        grid=(T // tt,),
        in_specs=[pl.BlockSpec((tt, K), lambda t: (t, 0))] + [
            pl.BlockSpec((b.shape[0], tt), lambda t: (0, t)) if layout == "t" else pl.BlockSpec((tt, b.shape[1]), lambda t: (t, 0))
            for b, layout in bs],
        out_specs=[once(s) for s in shapes],
        out_shape=[jax.ShapeDtypeStruct(s, F32) for s in shapes],
        compiler_params=_params(1, VMEM_LIMIT_BIG),
    )(a, *[b for b, _ in bs])
    return res


def _adam_math(w, g, m, v):
    m = ADAM_B1 * m + (1.0 - ADAM_B1) * g
    v = ADAM_B2 * v + (1.0 - ADAM_B2) * (g * g)
    m_hat = m / (1.0 - ADAM_B1 ** ADAM_STEP)
    v_hat = v / (1.0 - ADAM_B2 ** ADAM_STEP)
    delta = -ADAM_LR * (m_hat / (jnp.sqrt(v_hat) + ADAM_EPS) + ADAM_WD * w)
    return delta, m, v


def _adam(parts, w, m, v, name):
    R, C = w.shape
    br = 128 if R % 128 == 0 else R

    def body(p_ref, w_ref, m_ref, v_ref, g_ref, d_ref, nm_ref, nv_ref):
        g = p_ref[0].astype(F32)
        for s in range(1, NDEV):
            g = g + p_ref[s].astype(F32)
        g_ref[...] = g
        d_ref[...], nm_ref[...], nv_ref[...] = _adam_math(w_ref[...], g, m_ref[...], v_ref[...])

    blk = pl.BlockSpec((br, C), lambda i: (i, 0))
    return pl.pallas_call(
        body,
        name=name,
        grid=(R // br,),
        in_specs=[pl.BlockSpec((NDEV, br, C), lambda i: (0, i, 0)), blk, blk, blk],
        out_specs=[blk] * 4,
        out_shape=[jax.ShapeDtypeStruct((R, C), F32)] * 4,
        compiler_params=_params(1),
    )(parts, w, m, v)


_SMALL = (("sg_w", 8 * CH * CH), ("f_bias", 8), ("sg_ln_g", SW), ("sg_ln_b", SW), ("sg_b", 8 * CH), ("att_out_g", AW),
          ("sg_out_g", SW), ("pre_mix_g", D), ("post_mix_g", D), ("pre_ffn_g", D), ("post_ffn_g", D), ("ple_gate_b", D))
_SEG = 8 * 128


def _seg_rows(size):
    return 8 * (-(-size // _SEG))


def _pack(vals, loss_acc):
    parts = []
    for name, size in _SMALL:
        flat = vals[name].reshape(-1)
        rows = _seg_rows(size)
        parts.append(jnp.pad(flat, (0, rows * 128 - size)).reshape(rows, 128))
    parts.append(loss_acc)
    return jnp.concatenate(parts, axis=0)


def _adam_small(parts, ws, ms, vs):
    n = len(_SMALL)
    names = [name for name, _ in _SMALL]

    def body(*refs):
        p_ref, w_refs, m_refs, v_refs = refs[0], refs[1:1 + n], refs[1 + n:1 + 2 * n], refs[1 + 2 * n:1 + 3 * n]
        loss_ref, outs = refs[1 + 3 * n], refs[2 + 3 * n:]
        g_all = p_ref[0]
        for s in range(1, NDEV):
            g_all = g_all + p_ref[s]
        r = 0
        for k, (name, size) in enumerate(_SMALL):
            dst = [outs[kind * n + k] for kind in range(4)]

            def update(g, idx):
                vals = (g,) + _adam_math(w_refs[k][idx], g, m_refs[k][idx], v_refs[k][idx])
                for d, val in zip(dst, vals):
                    d[idx] = val

            if name == "sg_w":
                for grp in range(8):
                    update(g_all[r + grp * CH:r + (grp + 1) * CH, :], (0, grp))
            elif name == "sg_b":
                update(g_all[r:r + 8, :], (0,))
            elif name == "f_bias":
                update(g_all[r:r + 1, 0:8], (slice(None),))
            else:
                update(jnp.concatenate([g_all[r + q:r + q + 1, :] for q in range(size // 128)], axis=1), (slice(None),))
            r += _seg_rows(size)
        loss_ref[...] = g_all[r:r + 1, 0:1] * (0.5 / D)

    arrs = [parts] + [d[name] for d in (ws, ms, vs) for name in names]
    res = pl.pallas_call(
        body,
        name="adam_small",
        in_specs=[_full(a.shape) for a in arrs],
        out_specs=[_full((1, 1))] + [_full(ws[name].shape) for _ in range(4) for name in names],
        out_shape=[jax.ShapeDtypeStruct((1, 1), F32)] + [jax.ShapeDtypeStruct(ws[name].shape, F32) for _ in range(4) for name in names],
        compiler_params=pltpu.CompilerParams(vmem_limit_bytes=VMEM_LIMIT),
    )(*arrs)
    return res[0], {name: [res[1 + kind * n + k] for kind in range(4)] for k, name in enumerate(names)}


def kernel(x, p, w_in, f_bias, sg_ln_g, sg_ln_b, sg_w, sg_b, att_out_g, sg_out_g, w_out, pre_mix_g, post_mix_g, pre_ffn_g, post_ffn_g, w_ff1, w_ff2, ple_w, ple_gate_w, ple_gate_b, loss_target, m_w_in, m_f_bias, m_sg_ln_g, m_sg_ln_b, m_sg_w, m_sg_b, m_att_out_g, m_sg_out_g, m_w_out, m_pre_mix_g, m_post_mix_g, m_pre_ffn_g, m_post_ffn_g, m_w_ff1, m_w_ff2, m_ple_w, m_ple_gate_w, m_ple_gate_b, v_w_in, v_f_bias, v_sg_ln_g, v_sg_ln_b, v_sg_w, v_sg_b, v_att_out_g, v_sg_out_g, v_w_out, v_pre_mix_g, v_post_mix_g, v_pre_ffn_g, v_post_ffn_g, v_w_ff1, v_w_ff2, v_ple_w, v_ple_gate_w, v_ple_gate_b):
    small_w = dict(sg_w=sg_w, f_bias=f_bias, sg_ln_g=sg_ln_g, sg_ln_b=sg_ln_b, sg_b=sg_b, att_out_g=att_out_g,
                   sg_out_g=sg_out_g, pre_mix_g=pre_mix_g, post_mix_g=post_mix_g, pre_ffn_g=pre_ffn_g,
                   post_ffn_g=post_ffn_g, ple_gate_b=ple_gate_b)
    small_m = dict(sg_w=m_sg_w, f_bias=m_f_bias, sg_ln_g=m_sg_ln_g, sg_ln_b=m_sg_ln_b, sg_b=m_sg_b, att_out_g=m_att_out_g,
                   sg_out_g=m_sg_out_g, pre_mix_g=m_pre_mix_g, post_mix_g=m_post_mix_g, pre_ffn_g=m_pre_ffn_g,
                   post_ffn_g=m_post_ffn_g, ple_gate_b=m_ple_gate_b)
    small_v = dict(sg_w=v_sg_w, f_bias=v_f_bias, sg_ln_g=v_sg_ln_g, sg_ln_b=v_sg_ln_b, sg_b=v_sg_b, att_out_g=v_att_out_g,
                   sg_out_g=v_sg_out_g, pre_mix_g=v_pre_mix_g, post_mix_g=v_post_mix_g, pre_ffn_g=v_pre_ffn_g,
                   post_ffn_g=v_post_ffn_g, ple_gate_b=v_ple_gate_b)
    big = dict(w_in=(w_in, m_w_in, v_w_in), w_out=(w_out, m_w_out, v_w_out), w_ff1=(w_ff1, m_w_ff1, v_w_ff1),
               w_ff2=(w_ff2, m_w_ff2, v_w_ff2), ple_w=(ple_w, m_ple_w, v_ple_w),
               ple_gate_w=(ple_gate_w, m_ple_gate_w, v_ple_gate_w))

    xt, pt, tgt = x[0], p[0, 0], loss_target[0]
    ws = W_IN_COLS // NDEV

    (gw_in,) = _xchg([w_in[0].astype(BF16)], True, "gather_w_in")
    rest = _xchg_start([w_out[0].astype(BF16), w_ff1[0].astype(BF16), w_ff2[0].astype(BF16), ple_w[0].astype(BF16),
                        ple_gate_w[0].astype(BF16)], True, "gather_rest_start")
    win = jnp.transpose(gw_in, (1, 0, 2)).reshape(D, W_IN_COLS)
    wp = jnp.concatenate([win[:, 0:NQKV], win[:, NQKV + 8:W_IN_COLS], win[:, NQKV:NQKV + 8],
                          jnp.zeros((D, FPAD - 8), BF16)], axis=1)

    fb = jnp.pad(f_bias.astype(F32), ((0, 0), (0, FPAD - 8)))
    sgw = sg_w[0]
    sgwt = jnp.transpose(sg_w[0], (0, 2, 1))
    sgbt = jnp.transpose(sg_b[0])

    qkv, kt, vt, us, fz, ab = _inproj(xt, pre_mix_g + rest[4][0, 0], wp)
    ct, kk, vk = _fcum(fz, fb, qkv)
    yatt, lsec = _attn_fwd(qkv, kk, vt, ct)
    gw_out, gw1, gw2, gwpe, gwg = _xchg_wait(rest, True, yatt, "gather_rest_wait")
    wout = gw_out.reshape(D, D)
    wg = gwg.reshape(D, D)
    wpe = jnp.transpose(gwpe, (1, 0, 2)).reshape(PLE, D)
    h1, yb, o = _sgu_out(us, yatt, xt, sg_ln_g, sg_ln_b, sgw, sgbt, att_out_g, sg_out_g, wout, post_mix_g)
    c2b, f1b, ff, h2 = _ffn_fwd(h1, pre_ffn_g, gw1, gw2, post_ffn_g)
    dh2, dbg, loss_acc, g_g, g_pe = _ple_loss(h2, pt, tgt, wg, ple_gate_b, wpe)
    g_g = g_g.reshape(NDEV, D // NDEV, D)
    g_pe = jnp.transpose(g_pe.reshape(PLE, NDEV, D // NDEV), (1, 0, 2))

    dffb, df1, dh1, dgpostffn, dgpreffn = _ffn_bwd(dh2, ff, h1, f1b, gw1, gw2, post_ffn_g, pre_ffn_g)
    (g_1,) = _wgrad(c2b, [(df1, ("col", DFF // NDEV))], "wgrad_ff1")
    (g_2,) = _wgrad(f1b, [(dffb, ("row", DFF // NDEV))], "wgrad_ff2", a_fn=_sq_relu)
    early = _xchg_start([g_1, g_2, g_pe, g_g], False, "scatter_early_start")
    dob, dya, dus, dsgw, dsgbt, dlng, dlnb, dgatt, dgsg, dgpostmix = _mix_bwd(
        dh1, o, us, yatt, sg_ln_g, sg_ln_b, sgw, sgwt, sgbt, att_out_g, sg_out_g, wout, post_mix_g + early[4][0, 0])
    (g_out,) = _wgrad(yb, [(dob, ("row", D // NDEV))], "wgrad_out")
    mid = _xchg_start([g_out], False, "scatter_mid_start")
    qw, dow = _attn_prep(qkv, dya, yatt, lsec, mid[4])
    dqt, dcq, dk, dv, dck = _attn_bwd(kk, vk, kt, qw, dow)
    dcq = jnp.pad(jnp.transpose(dcq[:, 0:2, :].reshape(8, -1)), ((0, 0), (0, FPAD - 8)))
    dck = jnp.pad(dck.reshape(-1, 4, 128)[:, :, 0:2].reshape(-1, 8), ((0, 0), (0, FPAD - 8)))
    dfz, dfb = _fgate_bwd(dcq, dck, fz, fb)

    gq, gk, gv, gus, gf = _wgrad(ab, [(dqt, "t"), (dk, None), (dv, None), (dus, None), (dfz, None)], "wgrad_in")
    g_in = jnp.concatenate([gq, gk, gv, gf[:, 0:8], gus], axis=1)
    g_in = jnp.transpose(g_in.reshape(D, NDEV, ws), (1, 0, 2))
    late = _xchg_start([g_in.astype(BF16)], False, "scatter_late_start")
    grad_x, dgpremix = _inproj_bwd(dqt, dk, dv, dus, dfz, wp, jnp.transpose(wp[:, 0:AW]), xt, dh1,
                                   pre_mix_g + late[4][0, 0])

    small_g = dict(sg_w=dsgw, f_bias=dfb[:, 0:8], sg_ln_g=dlng, sg_ln_b=dlnb, sg_b=jnp.transpose(dsgbt[:, 0:8]),
                   att_out_g=dgatt, sg_out_g=dgsg, pre_mix_g=dgpremix, post_mix_g=dgpostmix, pre_ffn_g=dgpreffn,
                   post_ffn_g=dgpostffn, ple_gate_b=dbg)

    r_1, r_2, r_pe, r_g = _xchg_wait(early, False, grad_x, "scatter_early_wait")
    (r_out,) = _xchg_wait(mid, False, grad_x, "scatter_mid_wait")
    (r_small,) = _xchg([_pack(small_g, loss_acc)], True, "gather_small_grads")

    res = {}

    def adam_big(name, parts):
        w, m, v = big[name]
        res[name] = [t[None] for t in _adam(parts, w[0], m[0], v[0], "adam_" + name)]

    for name, parts in (("w_out", r_out), ("w_ff1", r_1), ("w_ff2", r_2), ("ple_w", r_pe), ("ple_gate_w", r_g)):
        adam_big(name, parts)
    (r_in,) = _xchg_wait(late, False, res["w_ff1"][0], "scatter_late_wait")
    adam_big("w_in", r_in)
    loss, small = _adam_small(r_small, small_w, small_m, small_v)
    res.update(small)

    order = ["w_in", "f_bias", "sg_ln_g", "sg_ln_b", "sg_w", "sg_b", "att_out_g", "sg_out_g", "w_out", "pre_mix_g",
             "post_mix_g", "pre_ffn_g", "post_ffn_g", "w_ff1", "w_ff2", "ple_w", "ple_gate_w", "ple_gate_b"]
    outs = [loss[0, 0], grad_x[None]]
    for kind in range(4):
        outs += [res[name][kind] for name in order]
    return tuple(outs)
```

```python
import jax
import jax.numpy as jnp
from jax import lax
from jax.experimental import pallas as pl
from jax.experimental.pallas import tpu as pltpu

F32 = jnp.float32
BF16 = jnp.bfloat16

NDEV = 8
D = 1024
AW = 512
SW = 512
HD = 64
CH = 128
DFF = 4096
PLE = 256
NQKV = 3 * AW
NUS = 2 * SW
FPAD = 128
ZP = NQKV + NUS + FPAD
W_IN_COLS = 2568
EPS = 1e-6
MASKV = -1e30
GELU_K = 0.7978845608028654
GELU_C = 0.044715

ADAM_LR = 0.001
ADAM_B1 = 0.9
ADAM_B2 = 0.999
ADAM_EPS = 1e-08
ADAM_WD = 0.01
ADAM_STEP = 10

VMEM_LIMIT = 48 * 1024 * 1024
VMEM_LIMIT_BIG = 60 * 1024 * 1024


def _nn(a, b):
    return jnp.dot(a, b, preferred_element_type=F32)


def _nt(a, b):
    return lax.dot_general(a, b, (((1,), (1,)), ((), ())), preferred_element_type=F32)


def _tn(a, b):
    return lax.dot_general(a, b, (((0,), (0,)), ((), ())), preferred_element_type=F32)


def _tile(n, pref):
    return min(n, pref)


def _params(n_axes, vmem=VMEM_LIMIT):
    return pltpu.CompilerParams(dimension_semantics=("arbitrary",) * n_axes, vmem_limit_bytes=vmem)


def _full(shape):
    nd = len(shape)
    return pl.BlockSpec(shape, lambda *_: (0,) * nd)


def _rms_fwd(x, g):
    r = lax.rsqrt(jnp.mean(x * x, axis=-1, keepdims=True) + EPS)
    return x * r * g


def _rms_bwd(dy, x, g):
    n = x.shape[-1]
    r = lax.rsqrt(jnp.mean(x * x, axis=-1, keepdims=True) + EPS)
    u = dy * g
    s = jnp.sum(x * u, axis=-1, keepdims=True)
    dx = r * u - x * (r * r * r * (s * (1.0 / n)))
    dg = jnp.sum(dy * (x * r), axis=0, keepdims=True)
    return dx, dg


def _gelu(x):
    t = jnp.tanh(GELU_K * (x + GELU_C * (x * x * x)))
    return x * (0.5 * (1.0 + t)), t


def _gelu_grad(x, t):
    return 0.5 * (1.0 + t) + 0.5 * x * (1.0 - t * t) * (GELU_K * (1.0 + 3.0 * GELU_C * x * x))


def _gather(arr, name):
    def body(x_ref, out_ref, send, recv, loc):
        x, y, c = lax.axis_index("x"), lax.axis_index("y"), lax.axis_index("c")
        me, sibling = (x, y, c), (x, y, 1 - c)
        chips = [(1 - x, y), (x, 1 - y), (1 - x, 1 - y)]

        def slot(px, py, pc):
            return out_ref.at[4 * px + 2 * py + pc]

        def copy(k, block, to, src=None):
            return pltpu.make_async_remote_copy(
                src_ref=slot(*block) if src is None else src, dst_ref=slot(*block), send_sem=send.at[k],
                recv_sem=recv.at[k], device_id=to, device_id_type=pl.DeviceIdType.MESH)

        mine = pltpu.make_async_copy(x_ref, slot(*me), loc)
        mine.start()
        first = [copy(0, me, sibling, src=x_ref)] + [copy(1 + j, me, (*chip, c), src=x_ref) for j, chip in enumerate(chips)]
        for cp in first:
            cp.start()
        passed = [copy(4 + j, (*chip, c), sibling) for j, chip in enumerate(chips)]
        for j, chip in enumerate(chips):
            copy(1 + j, (*chip, c), me).wait_recv()
            passed[j].start()
        copy(0, sibling, me).wait_recv()
        for j, chip in enumerate(chips):
            copy(4 + j, (*chip, 1 - c), me).wait_recv()
        for cp in first + passed:
            cp.wait_send()
        mine.wait()

    hbm = pl.BlockSpec(memory_space=pltpu.HBM)
    return pl.pallas_call(
        body,
        name=name,
        out_shape=jax.ShapeDtypeStruct((NDEV,) + arr.shape, arr.dtype),
        in_specs=[hbm],
        out_specs=hbm,
        scratch_shapes=[pltpu.SemaphoreType.DMA((NDEV - 1,)), pltpu.SemaphoreType.DMA((NDEV - 1,)), pltpu.SemaphoreType.DMA],
    )(arr)


def _peers(x, y, c):
    out = []
    for k in range(1, NDEV):
        out.append((1 - x if (k >> 2) & 1 else x, 1 - y if (k >> 1) & 1 else y, 1 - c if k & 1 else c))
    return out


def _xchg_start(arrs, gather, name):
    n = len(arrs)
    me = 4 * lax.axis_index("x") + 2 * lax.axis_index("y") + lax.axis_index("c")
    lands = []
    for a in arrs:
        shape = ((NDEV,) + a.shape) if gather else a.shape
        own = a[None] if gather else lax.dynamic_slice_in_dim(a, me, 1, axis=0)
        lands.append(lax.dynamic_update_slice_in_dim(lax.empty(shape, a.dtype), own, me, axis=0))

    def body(*refs):
        ins, lnd = refs[:n], refs[n:2 * n]
        send, recv, token = refs[2 * n:3 * n], refs[3 * n:4 * n], refs[-1]
        x, y, c = lax.axis_index("x"), lax.axis_index("y"), lax.axis_index("c")
        mine = 4 * x + 2 * y + c
        for px, py, pc in _peers(x, y, c):
            peer = 4 * px + 2 * py + pc
            for a in range(n):
                pltpu.make_async_remote_copy(
                    src_ref=ins[a] if gather else ins[a].at[peer],
                    dst_ref=lnd[a].at[mine],
                    send_sem=send[a],
                    recv_sem=recv[a],
                    device_id=(px, py, pc),
                    device_id_type=pl.DeviceIdType.MESH,
                ).start()
        token[...] = jnp.zeros_like(token)

    hbm = pl.BlockSpec(memory_space=pltpu.HBM)
    sem = pl.BlockSpec(memory_space=pltpu.SEMAPHORE)
    res = pl.pallas_call(
        body,
        name=name,
        out_shape=(*[pltpu.SemaphoreType.DMA(())] * (2 * n),
                   *[pltpu.HBM(a.shape, a.dtype) for a in arrs], *[pltpu.HBM(l.shape, l.dtype) for l in lands],
                   jax.ShapeDtypeStruct((8, 128), F32)),
        in_specs=[hbm] * (2 * n),
        out_specs=(*([sem] * (2 * n)), *([hbm] * (2 * n)), pl.BlockSpec(memory_space=pltpu.VMEM)),
        input_output_aliases={i: 2 * n + i for i in range(2 * n)},
        compiler_params=pltpu.CompilerParams(has_side_effects=pltpu.SideEffectType.DATAFLOW_SIDE_EFFECTING),
    )(*[pltpu.with_memory_space_constraint(a, pltpu.HBM) for a in arrs],
      *[pltpu.with_memory_space_constraint(l, pltpu.HBM) for l in lands])
    return list(res[0:n]), list(res[n:2 * n]), list(res[2 * n:3 * n]), list(res[3 * n:4 * n]), res[-1]


def _xchg_wait(started, gather, after, name):
    send, recv, srcs, lands, _ = started
    n = len(srcs)

    def body(*refs):
        lnd = refs[n:2 * n]
        send, recv = refs[2 * n:3 * n], refs[3 * n:4 * n]
        me = (lax.axis_index("x"), lax.axis_index("y"), lax.axis_index("c"))
        for a in range(n):
            seven = lnd[a].at[pl.ds(0, NDEV - 1)]
            cp = pltpu.make_async_remote_copy(src_ref=seven, dst_ref=seven, send_sem=send[a], recv_sem=recv[a],
                                              device_id=me, device_id_type=pl.DeviceIdType.MESH)
            cp.wait_send()
            cp.wait_recv()

    hbm = pl.BlockSpec(memory_space=pltpu.HBM)
    sem = pl.BlockSpec(memory_space=pltpu.SEMAPHORE)
    res = pl.pallas_call(
        body,
        name=name,
        out_shape=tuple([pltpu.HBM(a.shape, a.dtype) for a in srcs] + [pltpu.HBM(l.shape, l.dtype) for l in lands]),
        in_specs=[hbm] * (2 * n) + [sem] * (2 * n) + [pl.BlockSpec(memory_space=pl.ANY)],
        out_specs=tuple([hbm] * (2 * n)),
        input_output_aliases={i: i for i in range(2 * n)},
        compiler_params=pltpu.CompilerParams(has_side_effects=pltpu.SideEffectType.DATAFLOW_SIDE_EFFECTING),
    )(*srcs, *lands, *send, *recv, after)
    return list(res[n:])


def _inproj(x, g, wp):
    T = x.shape[0]
    tm = _tile(T, 512)

    def body(x_ref, g_ref, w_ref, qkv_ref, kt_ref, vt_ref, us_ref, fz_ref, ab_ref):
        a = _rms_fwd(x_ref[...], g_ref[...]).astype(BF16)
        ab_ref[...] = a
        qkv_ref[:, 0:AW] = _nn(a, w_ref[:, 0:AW]).astype(BF16)
        kk = _nn(a, w_ref[:, AW:2 * AW])
        qkv_ref[:, AW:2 * AW] = kk.astype(BF16)
        kt_ref[...] = kk.T.astype(BF16)
        vv = _nn(a, w_ref[:, 2 * AW:NQKV])
        qkv_ref[:, 2 * AW:NQKV] = vv.astype(BF16)
        vt_ref[...] = vv.T.astype(BF16)
        us_ref[...] = _nn(a, w_ref[:, NQKV:NQKV + NUS])
        fz_ref[...] = _nn(a, w_ref[:, NQKV + NUS:ZP])

    row = lambda n: pl.BlockSpec((tm, n), lambda i: (i, 0))
    col = pl.BlockSpec((AW, tm), lambda i: (0, i))
    return pl.pallas_call(
        body,
        name="inproj",
        grid=(T // tm,),
        in_specs=[row(D), _full((1, D)), _full((D, ZP))],
        out_specs=[row(NQKV), col, col, row(NUS), row(FPAD), row(D)],
        out_shape=[
            jax.ShapeDtypeStruct((T, NQKV), BF16),
            jax.ShapeDtypeStruct((AW, T), BF16),
            jax.ShapeDtypeStruct((AW, T), BF16),
            jax.ShapeDtypeStruct((T, NUS), F32),
            jax.ShapeDtypeStruct((T, FPAD), F32),
            jax.ShapeDtypeStruct((T, D), BF16),
        ],
        compiler_params=_params(1),
    )(x, g, wp)


def _log_sigmoid(z):
    return jnp.minimum(z, 0.0) - jnp.log1p(jnp.exp(-jnp.abs(z)))


def _split3(x):
    hi = x.astype(BF16)
    r1 = x - hi.astype(F32)
    mid = r1.astype(BF16)
    lo = (r1 - mid.astype(F32)).astype(BF16)
    return hi, mid, lo


AUG_A, AUG_B, AUG_ONE = 0, 3, 6


def _aug_lanes(rows, first_one, pieces):
    lane = lax.broadcasted_iota(jnp.int32, (rows, 128), 1)
    out = jnp.zeros((rows, 128), F32)
    if first_one is not None:
        out = jnp.where((lane >= first_one) & (lane < first_one + 3), 1.0, out)
    for n, piece in enumerate(pieces):
        out = jnp.where(lane == AUG_ONE + n, piece.astype(F32), out)
    return out.astype(BF16)


def _fcum(fz, fb, qkv):
    T = fz.shape[0]
    tb = _tile(T, 512)

    def body(fz_ref, fb_ref, k_ref, v_ref, ct_ref, kk_ref, vk_ref, carry):
        @pl.when(pl.program_id(0) == 0)
        def _():
            carry[...] = jnp.zeros_like(carry)

        lf = _log_sigmoid(fz_ref[...] + fb_ref[...])
        r = lax.broadcasted_iota(jnp.int32, (tb, tb), 0)
        cc = lax.broadcasted_iota(jnp.int32, (tb, tb), 1)
        tri = (cc <= r).astype(F32)
        cs = jnp.dot(tri, lf, precision=lax.Precision.HIGHEST, preferred_element_type=F32) + carry[...]
        ct_ref[...] = cs.T[0:8, :]
        carry[...] = carry[...] + jnp.sum(lf, axis=0, keepdims=True)

        pieces = _split3(cs)
        src = lax.broadcasted_iota(jnp.int32, (128, 128), 0)
        dst = lax.broadcasted_iota(jnp.int32, (128, 128), 1)
        ones = jnp.where((dst[0:1, :] >= AUG_ONE) & (dst[0:1, :] < AUG_ONE + 3), 1.0, 0.0)
        for hp in range(4):
            aug = jnp.zeros((tb, 128), F32) + ones
            for n, piece in enumerate(pieces):
                sel = jnp.where(((src == 2 * hp) & (dst == AUG_A + n)) | ((src == 2 * hp + 1) & (dst == AUG_B + n)), -1.0, 0.0)
                aug = aug + _nn(piece, sel.astype(BF16))
            aug = aug.astype(BF16)
            kk_ref[:, hp * 256:hp * 256 + 128] = k_ref[:, hp * 128:(hp + 1) * 128]
            kk_ref[:, hp * 256 + 128:(hp + 1) * 256] = aug
            vk_ref[:, hp * 256:hp * 256 + 128] = v_ref[:, hp * 128:(hp + 1) * 128]
            vk_ref[:, hp * 256 + 128:(hp + 1) * 256] = aug

    row = lambda n, c: pl.BlockSpec((tb, n), lambda i: (i, c))
    return pl.pallas_call(
        body,
        name="fcum",
        grid=(T // tb,),
        in_specs=[row(FPAD, 0), _full((1, FPAD)), row(AW, 1), row(AW, 2)],
        out_specs=[pl.BlockSpec((8, tb), lambda i: (0, i)), row(2 * AW, 0), row(2 * AW, 0)],
        out_shape=[jax.ShapeDtypeStruct((8, T), F32), jax.ShapeDtypeStruct((T, 2 * AW), BF16),
                   jax.ShapeDtypeStruct((T, 2 * AW), BF16)],
        scratch_shapes=[pltpu.VMEM((1, FPAD), F32)],
        compiler_params=_params(1),
    )(fz, fb, qkv, qkv)


def _fold_rows(r, t, nq):
    if nq == 1:
        return r, t
    low = t <= r
    return jnp.where(low, r, nq - 1 - r), jnp.where(low, t, t - r - 1)


def _fold_cols(r, t, nq):
    if nq == 1:
        return r, t
    first = t < nq - r
    j = jnp.where(first, r, nq - 1 - r)
    return j, jnp.where(first, r + t, nq - 1 - r + (t - (nq - r)))


PAIRS = 4


def _fold_grid(nq):
    assert nq == 1 or nq % 2 == 0
    return (4 // PAIRS, 1, 1) if nq == 1 else (4 // PAIRS, nq // 2, nq + 1)


def _head_rows(x2, hh, scale):
    is_a = lax.broadcasted_iota(jnp.int32, (1, 128), 1) < HD
    keep = is_a if hh == 0 else jnp.logical_not(is_a)
    return jnp.where(keep, x2, jnp.zeros_like(x2)) * scale


def _attn_fwd(qkv, kk, vt, ct):
    T = qkv.shape[0]
    tq = _tile(T, 512)
    tk = tq
    nq = T // tq

    def body(q_ref, kk_ref, vt_ref, ctq_ref, o_ref, lsec_ref, qw_s, m_s, l_s, acc_s):
        i, j = _fold_rows(pl.program_id(1), pl.program_id(2), nq)
        sub8 = lax.broadcasted_iota(jnp.int32, (8, 1), 0)

        def cref_of(pp, hh):
            head = 2 * (PAIRS * pl.program_id(0) + pp) + hh
            return jnp.sum(jnp.where(sub8 == head, ctq_ref[:, 0:1], 0.0), axis=0, keepdims=True)

        @pl.when(j == 0)
        def _():
            for pp in range(PAIRS):
                q2 = q_ref[:, pp * 128:(pp + 1) * 128]
                for hh in range(2):
                    rows = slice(hh * tq, (hh + 1) * tq)
                    qw_s[pp, rows, 0:128] = _head_rows(q2, hh, 0.125)
                    qw_s[pp, rows, 128:256] = _aug_lanes(tq, AUG_A if hh == 0 else AUG_B, _split3(cref_of(pp, hh)))
            m_s[...] = jnp.full_like(m_s, MASKV)
            l_s[...] = jnp.zeros_like(l_s)
            acc_s[...] = jnp.zeros_like(acc_s)

        def step(masked):
            if masked:
                causal = lax.broadcasted_iota(jnp.int32, (tk, tq), 0) <= lax.broadcasted_iota(jnp.int32, (tk, tq), 1)
            logits = lambda pp: _nt(kk_ref[:, pp * 256:(pp + 1) * 256], qw_s[pp])
            s_next = logits(0)
            for pp in range(PAIRS):
                s2 = s_next
                if pp + 1 < PAIRS:
                    s_next = logits(pp + 1)
                vt2 = vt_ref[pp * 128:(pp + 1) * 128, :]
                for hh in range(2):
                    n = 2 * pp + hh
                    s = s2[:, hh * tq:(hh + 1) * tq]
                    if masked:
                        s = jnp.where(causal, s, MASKV)
                    m_prev = m_s[n]
                    m_new = jnp.maximum(m_prev, jnp.max(s, axis=0, keepdims=True))
                    pr = jnp.exp(s - m_new)
                    alpha = jnp.exp(m_prev - m_new)
                    l_s[n] = alpha * l_s[n] + jnp.sum(pr, axis=0, keepdims=True)
                    m_s[n] = m_new
                    acc_s[n] = alpha * acc_s[n] + _nn(vt2, pr.astype(BF16))

        @pl.when(j < i)
        def _():
            step(False)

        @pl.when(j == i)
        def _():
            step(True)
            sub = lax.broadcasted_iota(jnp.int32, (128, 1), 0)
            for pp in range(PAIRS):
                a, b = 2 * pp, 2 * pp + 1
                ot = jnp.where(sub < HD, acc_s[a] * (1.0 / l_s[a]), acc_s[b] * (1.0 / l_s[b]))
                o_ref[:, pp * 128:(pp + 1) * 128] = ot.T
                lrow = [m_s[2 * pp + hh] + jnp.log(l_s[2 * pp + hh]) - cref_of(pp, hh) for hh in range(2)]
                lsec_ref[:, pp * 128:(pp + 1) * 128] = jnp.where(sub == 0, lrow[0], jnp.where(sub == 1, lrow[1], 0.0)).T

    qi = lambda r, t: _fold_rows(r, t, nq)[0]
    kj = lambda r, t: _fold_rows(r, t, nq)[1]
    return pl.pallas_call(
        body,
        name="attn_fwd",
        grid=_fold_grid(nq),
        in_specs=[
            pl.BlockSpec((tq, PAIRS * 128), lambda g, r, t: (qi(r, t), g)),
            pl.BlockSpec((tk, PAIRS * 256), lambda g, r, t: (kj(r, t), g)),
            pl.BlockSpec((PAIRS * 128, tk), lambda g, r, t: (g, kj(r, t))),
            pl.BlockSpec((8, tq), lambda g, r, t: (0, qi(r, t))),
        ],
        out_specs=[pl.BlockSpec((tq, PAIRS * 128), lambda g, r, t: (qi(r, t), g))] * 2,
        out_shape=[jax.ShapeDtypeStruct((T, AW), F32)] * 2,
        scratch_shapes=[pltpu.VMEM((PAIRS, 2 * tq, 256), BF16), pltpu.VMEM((2 * PAIRS, 1, tq), F32),
                        pltpu.VMEM((2 * PAIRS, 1, tq), F32), pltpu.VMEM((2 * PAIRS, 128, tq), F32)],
        compiler_params=_params(3),
    )(qkv, kk, vt, ct)


def _sgu_forward(us_ref, lng, lnb, w_ref, bt_ref, mixed_s, vnb_s, tm):
    is_a = lax.broadcasted_iota(jnp.int32, (1, 128), 1) < HD
    u = us_ref[:, 0:SW]
    vs = us_ref[:, SW:NUS]
    ug, tu = _gelu(u)
    vg, tv = _gelu(vs)
    mu = jnp.mean(vg, axis=-1, keepdims=True)
    xc = vg - mu
    rstd = lax.rsqrt(jnp.mean(xc * xc, axis=-1, keepdims=True) + EPS)
    vhat = xc * rstd
    vnb_s[...] = (vhat * lng + lnb).astype(BF16)
    rr = lax.broadcasted_iota(jnp.int32, (CH, CH), 0)
    cc = lax.broadcasted_iota(jnp.int32, (CH, CH), 1)
    tril = cc <= rr
    for jj in range(4):
        wa = jnp.where(tril, w_ref[2 * jj], 0.0).astype(BF16)
        wb = jnp.where(tril, w_ref[2 * jj + 1], 0.0).astype(BF16)
        ba = bt_ref[:, 2 * jj:2 * jj + 1]
        bb = bt_ref[:, 2 * jj + 1:2 * jj + 2]
        for ch in range(tm // CH):
            rs, cs = slice(ch * CH, (ch + 1) * CH), slice(jj * 128, (jj + 1) * 128)
            vn2 = vnb_s[rs, cs]
            mixed_s[rs, cs] = jnp.where(is_a, _nn(wa, vn2) + ba, _nn(wb, vn2) + bb)
    mixed = mixed_s[...]
    return u, vs, ug, tu, tv, vhat, rstd, mixed, ug * mixed


def _sgu_out(us, yatt, x, lng, lnb, sgw, sgbt, gatt, gsg, wout, gpm):
    T = us.shape[0]
    tm = _tile(T, 256)

    def body(us_ref, ya_ref, x_ref, lng_ref, lnb_ref, w_ref, bt_ref, ga_ref, gs_ref, wo_ref, gp_ref,
             h1_ref, yb_ref, o_ref, mixed_s, vnb_s):
        ysg = _sgu_forward(us_ref, lng_ref[...], lnb_ref[...], w_ref, bt_ref, mixed_s, vnb_s, tm)[-1]
        yb_ref[:, 0:AW] = _rms_fwd(ya_ref[...], ga_ref[...]).astype(BF16)
        yb_ref[:, AW:D] = _rms_fwd(ysg, gs_ref[...]).astype(BF16)
        o = _nn(yb_ref[...], wo_ref[...])
        o_ref[...] = o
        h1_ref[...] = x_ref[...] + _rms_fwd(o, gp_ref[...])

    row = lambda n: pl.BlockSpec((tm, n), lambda i: (i, 0))
    return pl.pallas_call(
        body,
        name="sgu_out",
        grid=(T // tm,),
        in_specs=[row(NUS), row(AW), row(D), _full((1, SW)), _full((1, SW)), _full((8, CH, CH)), _full((CH, 8)),
                  _full((1, AW)), _full((1, SW)), _full((D, D)), _full((1, D))],
        out_specs=[row(D), row(D), row(D)],
        out_shape=[jax.ShapeDtypeStruct((T, D), F32), jax.ShapeDtypeStruct((T, D), BF16), jax.ShapeDtypeStruct((T, D), F32)],
        scratch_shapes=[pltpu.VMEM((tm, SW), F32), pltpu.VMEM((tm, SW), BF16)],
        compiler_params=_params(1),
    )(us, yatt, x, lng, lnb, sgw, sgbt, gatt, gsg, wout, gpm)


def _ffn_fwd(h1, gpre, w1g, w2g, gpost):
    T = h1.shape[0]
    tm = _tile(T, 512)
    nb, hb = w1g.shape[0], w1g.shape[2]

    def body(h1_ref, gpre_ref, w1_ref, w2_ref, gpost_ref, c2_ref, f1_ref, ff_ref, h2_ref):
        h1 = h1_ref[...]
        c2 = _rms_fwd(h1, gpre_ref[...]).astype(BF16)
        c2_ref[...] = c2
        ff = jnp.zeros((tm, D), F32)
        for j in range(nb):
            f1 = _nn(c2, w1_ref[j])
            f1_ref[:, j * hb:(j + 1) * hb] = f1.astype(BF16)
            r = jnp.maximum(f1, 0.0)
            ff = ff + _nn((r * r).astype(BF16), w2_ref[j])
        ff_ref[...] = ff
        h2_ref[...] = h1 + _rms_fwd(ff, gpost_ref[...])

    row = lambda n: pl.BlockSpec((tm, n), lambda i: (i, 0))
    once = lambda shape: pl.BlockSpec(shape, lambda i: (0,) * len(shape), pipeline_mode=pl.Buffered(1))
    return pl.pallas_call(
        body,
        name="ffn_fwd",
        grid=(T // tm,),
        in_specs=[row(D), _full((1, D)), once((nb, D, hb)), once((nb, hb, D)), _full((1, D))],
        out_specs=[row(D), row(DFF), row(D), row(D)],
        out_shape=[jax.ShapeDtypeStruct((T, D), BF16), jax.ShapeDtypeStruct((T, DFF), BF16),
                   jax.ShapeDtypeStruct((T, D), F32), jax.ShapeDtypeStruct((T, D), F32)],
        compiler_params=_params(1, VMEM_LIMIT_BIG),
    )(h1, gpre, w1g, w2g, gpost)


def _ple_loss(h2, p, tgt, wg, bg, wpe):
    T = h2.shape[0]
    tm = _tile(T, 512)

    def body(h2_ref, p_ref, t_ref, wg_ref, bg_ref, wpe_ref, dh2_ref, dbg_ref, loss_ref, dwg_ref, dwpe_ref):
        @pl.when(pl.program_id(0) == 0)
        def _():
            for r in (dbg_ref, loss_ref, dwg_ref, dwpe_ref):
                r[...] = jnp.zeros_like(r)

        h2 = h2_ref[...]
        h2b = h2.astype(BF16)
        gate = jax.nn.sigmoid(_nn(h2b, wg_ref[...]) + bg_ref[...])
        pb = p_ref[...].astype(BF16)
        pe = _nn(pb, wpe_ref[...])
        diff = (h2 + gate * pe) - t_ref[...]
        loss_ref[...] += jnp.sum(diff * diff)
        dh3 = diff * (1.0 / D)
        dpre = (dh3 * pe) * (gate * (1.0 - gate))
        dpre_b = dpre.astype(BF16)
        dbg_ref[...] += jnp.sum(dpre, axis=0, keepdims=True)
        dh2_ref[...] = dh3 + _nt(dpre_b, wg_ref[...])
        dwg_ref[...] += _nn(h2b.T, dpre_b)
        dwpe_ref[...] += _nn(pb.T, (dh3 * gate).astype(BF16))

    row = lambda n: pl.BlockSpec((tm, n), lambda i: (i, 0))
    once = lambda shape: pl.BlockSpec(shape, lambda i: (0,) * len(shape), pipeline_mode=pl.Buffered(1))
    return pl.pallas_call(
        body,
        name="ple_loss",
        grid=(T // tm,),
        in_specs=[row(D), row(PLE), row(D), _full((D, D)), _full((1, D)), _full((PLE, D))],
        out_specs=[row(D), _full((1, D)), _full((8, 128)), once((D, D)), once((PLE, D))],
        out_shape=[jax.ShapeDtypeStruct((T, D), F32), jax.ShapeDtypeStruct((1, D), F32),
                   jax.ShapeDtypeStruct((8, 128), F32), jax.ShapeDtypeStruct((D, D), F32),
                   jax.ShapeDtypeStruct((PLE, D), F32)],
        compiler_params=_params(1),
    )(h2, p, tgt, wg, bg, wpe)


def _ffn_bwd(dh2, ff, h1, f1, w1g, w2g, gpost, gpre):
    T = dh2.shape[0]
    tm = _tile(T, 512)
    nb, hb = w1g.shape[0], w1g.shape[2]

    def body(dh2_ref, ff_ref, h1_ref, f1_ref, w1_ref, w2_ref, gpost_ref, gpre_ref,
             dffb_ref, df1_ref, dh1_ref, dgpost_ref, dgpre_ref):
        @pl.when(pl.program_id(0) == 0)
        def _():
            dgpost_ref[...] = jnp.zeros_like(dgpost_ref)
            dgpre_ref[...] = jnp.zeros_like(dgpre_ref)

        dh2 = dh2_ref[...]
        dff, dg = _rms_bwd(dh2, ff_ref[...], gpost_ref[...])
        dffb = dff.astype(BF16)
        dffb_ref[...] = dffb
        dgpost_ref[...] += dg
        dc2 = jnp.zeros((tm, D), F32)
        for j in range(nb):
            cols = slice(j * hb, (j + 1) * hb)
            dact = _nt(dffb, w2_ref[j])
            df1 = (dact * (2.0 * jnp.maximum(f1_ref[:, cols].astype(F32), 0.0))).astype(BF16)
            df1_ref[:, cols] = df1
            dc2 = dc2 + _nt(df1, w1_ref[j])
        dx, dg = _rms_bwd(dc2, h1_ref[...], gpre_ref[...])
        dh1_ref[...] = dh2 + dx
        dgpre_ref[...] += dg

    row = lambda n: pl.BlockSpec((tm, n), lambda i: (i, 0))
    once = lambda shape: pl.BlockSpec(shape, lambda i: (0,) * len(shape), pipeline_mode=pl.Buffered(1))
    return pl.pallas_call(
        body,
        name="ffn_bwd",
        grid=(T // tm,),
        in_specs=[row(D), row(D), row(D), row(DFF), once((nb, D, hb)), once((nb, hb, D)), _full((1, D)), _full((1, D))],
        out_specs=[row(D), row(DFF), row(D), _full((1, D)), _full((1, D))],
        out_shape=[jax.ShapeDtypeStruct((T, D), BF16), jax.ShapeDtypeStruct((T, DFF), BF16),
                   jax.ShapeDtypeStruct((T, D), F32), jax.ShapeDtypeStruct((1, D), F32),
                   jax.ShapeDtypeStruct((1, D), F32)],
        compiler_params=_params(1, VMEM_LIMIT_BIG),
    )(dh2, ff, h1, f1, w1g, w2g, gpost, gpre)


def _mix_bwd(dh1, o, us, yatt, lng, lnb, sgw, sgwt, sgbt, gatt, gsg, wout, gpm):
    T = dh1.shape[0]
    tm = _tile(T, 256)

    def body(dh1_ref, o_ref, us_ref, ya_ref, lng_ref, lnb_ref, w_ref, wt_ref, bt_ref, ga_ref, gs_ref, wo_ref, gp_ref,
             dob_ref, dya_ref, dus_ref, dw_ref, dbt_ref, dlng_ref, dlnb_ref, dga_ref, dgs_ref, dgp_ref,
             mixed_s, vnb_s, dvn_s):
        @pl.when(pl.program_id(0) == 0)
        def _():
            for r in (dw_ref, dbt_ref, dlng_ref, dlnb_ref, dga_ref, dgs_ref, dgp_ref):
                r[...] = jnp.zeros_like(r)

        is_a = lax.broadcasted_iota(jnp.int32, (1, 128), 1) < HD
        lane = lax.broadcasted_iota(jnp.int32, (1, 128), 1)
        do, dg = _rms_bwd(dh1_ref[...], o_ref[...], gp_ref[...])
        dgp_ref[...] += dg
        dob = do.astype(BF16)
        dob_ref[...] = dob
        dy = _nt(dob, wo_ref[...])
        datt, dg = _rms_bwd(dy[:, 0:AW], ya_ref[...], ga_ref[...])
        dga_ref[...] += dg
        dya_ref[...] = datt

        lng = lng_ref[...]
        u, vs, ug, tu, tv, vhat, rstd, mixed, ysg = _sgu_forward(us_ref, lng, lnb_ref[...], w_ref, bt_ref, mixed_s, vnb_s, tm)
        dysg, dg = _rms_bwd(dy[:, AW:D], ysg, gs_ref[...])
        dgs_ref[...] += dg
        dus_ref[:, 0:SW] = ((dysg * mixed) * _gelu_grad(u, tu)).astype(BF16)
        dmix = dysg * ug

        rr = lax.broadcasted_iota(jnp.int32, (CH, CH), 0)
        cc = lax.broadcasted_iota(jnp.int32, (CH, CH), 1)
        tril = cc <= rr
        triu = cc >= rr
        for jj in range(4):
            wta = jnp.where(triu, wt_ref[2 * jj], 0.0).astype(BF16)
            wtb = jnp.where(triu, wt_ref[2 * jj + 1], 0.0).astype(BF16)
            for ch in range(tm // CH):
                rs, cs = slice(ch * CH, (ch + 1) * CH), slice(jj * 128, (jj + 1) * 128)
                dm2 = dmix[rs, cs]
                dma = jnp.where(is_a, dm2, 0.0)
                dmb = jnp.where(is_a, 0.0, dm2)
                dma_b, dmb_b = dma.astype(BF16), dmb.astype(BF16)
                vn2 = vnb_s[rs, cs]
                dw_ref[2 * jj] += jnp.where(tril, _nt(dma_b, vn2), 0.0)
                dw_ref[2 * jj + 1] += jnp.where(tril, _nt(dmb_b, vn2), 0.0)
                dvn_s[rs, cs] = _nn(wta, dma_b) + _nn(wtb, dmb_b)
                dba = jnp.sum(dma, axis=1, keepdims=True)
                dbb = jnp.sum(dmb, axis=1, keepdims=True)
                dbt_ref[...] += jnp.where(lane == 2 * jj, dba, 0.0) + jnp.where(lane == 2 * jj + 1, dbb, 0.0)

        dvn = dvn_s[...]
        dlng_ref[...] += jnp.sum(dvn * vhat, axis=0, keepdims=True)
        dlnb_ref[...] += jnp.sum(dvn, axis=0, keepdims=True)
        dvh = dvn * lng
        dvg = rstd * (dvh - jnp.mean(dvh, axis=-1, keepdims=True) - vhat * jnp.mean(dvh * vhat, axis=-1, keepdims=True))
        dus_ref[:, SW:NUS] = (dvg * _gelu_grad(vs, tv)).astype(BF16)

    row = lambda n: pl.BlockSpec((tm, n), lambda i: (i, 0))
    return pl.pallas_call(
        body,
        name="mix_bwd",
        grid=(T // tm,),
        in_specs=[row(D), row(D), row(NUS), row(AW), _full((1, SW)), _full((1, SW)), _full((8, CH, CH)), _full((8, CH, CH)),
                  _full((CH, 8)), _full((1, AW)), _full((1, SW)), _full((D, D)), _full((1, D))],
        out_specs=[row(D), row(AW), row(NUS), _full((8, CH, CH)), _full((CH, 128)), _full((1, SW)), _full((1, SW)),
                   _full((1, AW)), _full((1, SW)), _full((1, D))],
        out_shape=[jax.ShapeDtypeStruct((T, D), BF16), jax.ShapeDtypeStruct((T, AW), F32), jax.ShapeDtypeStruct((T, NUS), BF16),
                   jax.ShapeDtypeStruct((8, CH, CH), F32), jax.ShapeDtypeStruct((CH, 128), F32),
                   jax.ShapeDtypeStruct((1, SW), F32), jax.ShapeDtypeStruct((1, SW), F32),
                   jax.ShapeDtypeStruct((1, AW), F32), jax.ShapeDtypeStruct((1, SW), F32), jax.ShapeDtypeStruct((1, D), F32)],
        scratch_shapes=[pltpu.VMEM((tm, SW), F32), pltpu.VMEM((tm, SW), BF16), pltpu.VMEM((tm, SW), F32)],
        compiler_params=_params(1),
    )(dh1, o, us, yatt, lng, lnb, sgw, sgwt, sgbt, gatt, gsg, wout, gpm)


def _attn_prep(qkv, dya, yatt, lsec, after):
    T = dya.shape[0]
    tm = _tile(T, 512)

    def body(q_ref, do_ref, o_ref, l_ref, after_ref, qw_ref, dow_ref):
        do = do_ref[...]
        feat = lax.broadcasted_iota(jnp.int32, (AW, 128), 0)
        head = lax.broadcasted_iota(jnp.int32, (AW, 128), 1)
        sel = jnp.where((feat >= head * HD) & (feat < (head + 1) * HD), 1.0, 0.0)
        delta = jnp.dot(do * o_ref[...], sel, precision=lax.Precision.HIGHEST, preferred_element_type=F32)
        for hp in range(4):
            cols = slice(hp * 128, (hp + 1) * 128)
            for hh in range(2):
                base = (2 * hp + hh) * 256
                lc = l_ref[:, hp * 128 + hh:hp * 128 + hh + 1]
                d_h = delta[:, 2 * hp + hh:2 * hp + hh + 1]
                qw_ref[:, base:base + 128] = _head_rows(q_ref[:, cols], hh, 0.125)
                qw_ref[:, base + 128:base + 256] = _aug_lanes(tm, AUG_A if hh == 0 else AUG_B, _split3(-lc))
                dow_ref[:, base:base + 128] = _head_rows(do[:, cols], hh, 1.0).astype(BF16)
                dow_ref[:, base + 128:base + 256] = _aug_lanes(tm, None, _split3(-d_h))

    row = lambda n: pl.BlockSpec((tm, n), lambda i: (i, 0))
    return pl.pallas_call(
        body,
        name="attn_prep",
        grid=(T // tm,),
        in_specs=[row(AW)] * 4 + [_full(after.shape)],
        out_specs=[row(4 * AW)] * 2,
        out_shape=[jax.ShapeDtypeStruct((T, 4 * AW), BF16)] * 2,
        compiler_params=_params(1),
    )(qkv, dya, yatt, lsec, after)


def _attn_bwd(kk, vk, kt, qw, dow):
    T = kk.shape[0]
    tq = _tile(T, 512)
    tk = tq
    nq = T // tq

    def body(kk_ref, vk_ref, kt_ref, qw_ref, dow_ref, dqt_ref, dcq_ref, dk_ref, dv_ref, dck_ref, dk_s, dv_s, dck_s):
        j, i = _fold_cols(pl.program_id(1), pl.program_id(2), nq)
        sub8 = lax.broadcasted_iota(jnp.int32, (8, 1), 0)
        lane = lax.broadcasted_iota(jnp.int32, (1, 128), 1)

        @pl.when((pl.program_id(1) == 0) & (pl.program_id(2) == 0))
        def _():
            dqt_ref[...] = jnp.zeros_like(dqt_ref)
            dcq_ref[...] = jnp.zeros_like(dcq_ref)

        @pl.when(i == j)
        def _():
            dk_s[...] = jnp.zeros_like(dk_s)
            dv_s[...] = jnp.zeros_like(dv_s)
            dck_s[...] = jnp.zeros_like(dck_s)

        def step(masked):
            cols = pl.ds(pl.multiple_of(i * tq, tq), tq)
            sub = lax.broadcasted_iota(jnp.int32, (128, 1), 0)
            if masked:
                causal = lax.broadcasted_iota(jnp.int32, (tk, tq), 0) <= lax.broadcasted_iota(jnp.int32, (tk, tq), 1)

            def logits(n):
                pair, base = n // 2, n * 256
                return (_nt(kk_ref[:, pair * 256:(pair + 1) * 256], qw_ref[:, base:base + 256]),
                        _nt(vk_ref[:, pair * 256:(pair + 1) * 256], dow_ref[:, base:base + 256]))

            ahead = logits(0)
            for pp in range(PAIRS):
                lanes = slice(pp * 128, (pp + 1) * 128)
                kt2 = kt_ref[lanes, :] * 0.125
                dcq = jnp.zeros((8, tq), F32)
                dck = jnp.zeros((tk, 128), F32)
                dv = jnp.zeros((tk, 128), F32)
                dk = jnp.zeros((tk, 128), F32)
                dqts = []
                for hh in range(2):
                    base = (2 * pp + hh) * 256
                    qw_h = qw_ref[:, base:base + 256]
                    dow_h = dow_ref[:, base:base + 256]
                    logp, dp = ahead
                    if 2 * pp + hh + 1 < 2 * PAIRS:
                        ahead = logits(2 * pp + hh + 1)
                    pr = jnp.exp(logp)
                    if masked:
                        pr = jnp.where(causal, pr, 0.0)
                    ds = pr * dp
                    ds_b = ds.astype(BF16)
                    dv = dv + _nn(pr.astype(BF16), dow_h[:, 0:128])
                    dk = dk + _nn(ds_b, qw_h[:, 0:128])
                    dqts.append(_nn(kt2, ds_b))
                    dcq = dcq + jnp.where(sub8 == hh, jnp.sum(ds, axis=0, keepdims=True), 0.0)
                    dck = dck - jnp.where(lane == hh, jnp.sum(ds, axis=1, keepdims=True), 0.0)
                dv_s[:, lanes] += dv
                dk_s[:, lanes] += dk
                dck_s[:, lanes] += dck
                dcq_ref[pp, :, cols] += dcq
                dqt_ref[lanes, cols] += jnp.where(sub < HD, dqts[0], dqts[1])

        @pl.when(i > j)
        def _():
            step(False)

        @pl.when(i == j)
        def _():
            step(True)

        @pl.when(i == nq - 1)
        def _():
            dk_ref[...] = dk_s[...].astype(BF16)
            dv_ref[...] = dv_s[...].astype(BF16)
            dck_ref[...] = dck_s[...]

    kj = lambda r, t: _fold_cols(r, t, nq)[0]
    qi = lambda r, t: _fold_cols(r, t, nq)[1]
    krow = lambda g, r, t: (kj(r, t), g)
    qrow = lambda g, r, t: (qi(r, t), g)
    return pl.pallas_call(
        body,
        name="attn_bwd",
        grid=_fold_grid(nq),
        in_specs=[
            pl.BlockSpec((tk, PAIRS * 256), krow),
            pl.BlockSpec((tk, PAIRS * 256), krow),
            pl.BlockSpec((PAIRS * 128, tk), lambda g, r, t: (g, kj(r, t))),
            pl.BlockSpec((tq, PAIRS * 512), qrow),
            pl.BlockSpec((tq, PAIRS * 512), qrow),
        ],
        out_specs=[
            pl.BlockSpec((PAIRS * 128, T), lambda g, r, t: (g, 0), pipeline_mode=pl.Buffered(1)),
            pl.BlockSpec((PAIRS, 8, T), lambda g, r, t: (g, 0, 0), pipeline_mode=pl.Buffered(1)),
            pl.BlockSpec((tk, PAIRS * 128), krow),
            pl.BlockSpec((tk, PAIRS * 128), krow),
            pl.BlockSpec((tk, PAIRS * 128), krow),
        ],
        out_shape=[jax.ShapeDtypeStruct((AW, T), F32), jax.ShapeDtypeStruct((4, 8, T), F32),
                   jax.ShapeDtypeStruct((T, AW), BF16), jax.ShapeDtypeStruct((T, AW), BF16),
                   jax.ShapeDtypeStruct((T, AW), F32)],
        scratch_shapes=[pltpu.VMEM((tk, PAIRS * 128), F32), pltpu.VMEM((tk, PAIRS * 128), F32),
                        pltpu.VMEM((tk, PAIRS * 128), F32)],
        compiler_params=_params(3),
    )(kk, vk, kt, qw, dow)


def _fgate_bwd(dcq, dck, fz, fb):
    T = dck.shape[0]
    tb = _tile(T, 512)
    nb = T // tb

    def body(dcq_ref, dck_ref, fz_ref, fb_ref, df_ref, dfb_ref, carry):
        @pl.when(pl.program_id(0) == 0)
        def _():
            carry[...] = jnp.zeros_like(carry)
            dfb_ref[...] = jnp.zeros_like(dfb_ref)

        dcv = dcq_ref[...] + dck_ref[...]
        r = lax.broadcasted_iota(jnp.int32, (tb, tb), 0)
        cc = lax.broadcasted_iota(jnp.int32, (tb, tb), 1)
        tri = (cc >= r).astype(F32)
        dlf = jnp.dot(tri, dcv, precision=lax.Precision.HIGHEST, preferred_element_type=F32) + carry[...]
        carry[...] = carry[...] + jnp.sum(dcv, axis=0, keepdims=True)
        lane = lax.broadcasted_iota(jnp.int32, (tb, FPAD), 1)
        df = jnp.where(lane < 8, dlf * jax.nn.sigmoid(-(fz_ref[...] + fb_ref[...])), 0.0)
        df_ref[...] = df.astype(BF16)
        dfb_ref[...] += jnp.sum(df, axis=0, keepdims=True)

    rev = pl.BlockSpec((tb, FPAD), lambda i: (nb - 1 - i, 0))
    return pl.pallas_call(
        body,
        name="fgate_bwd",
        grid=(nb,),
        in_specs=[rev, rev, rev, _full((1, FPAD))],
        out_specs=[rev, _full((1, FPAD))],
        out_shape=[jax.ShapeDtypeStruct((T, FPAD), BF16), jax.ShapeDtypeStruct((1, FPAD), F32)],
        scratch_shapes=[pltpu.VMEM((1, FPAD), F32)],
        compiler_params=_params(1),
    )(dcq, dck, fz, fb)


def _inproj_bwd(dqt, dk, dv, dus, dfz, wp, wqt, x, dh1, g):
    T = x.shape[0]
    tm = _tile(T, 512)

    def body(dq_ref, dk_ref, dv_ref, dus_ref, dfz_ref, w_ref, wqt_ref, x_ref, dh1_ref, g_ref, gx_ref, dg_ref):
        @pl.when(pl.program_id(0) == 0)
        def _():
            dg_ref[...] = jnp.zeros_like(dg_ref)

        da = _tn(dq_ref[...].astype(BF16), wqt_ref[...])
        da += _nt(dk_ref[...], w_ref[:, AW:2 * AW])
        da += _nt(dv_ref[...], w_ref[:, 2 * AW:NQKV])
        da += _nt(dus_ref[...], w_ref[:, NQKV:NQKV + NUS])
        da += _nt(dfz_ref[...], w_ref[:, NQKV + NUS:ZP])
        dx, dg = _rms_bwd(da, x_ref[...], g_ref[...])
        gx_ref[...] = dh1_ref[...] + dx
        dg_ref[...] += dg

    row = lambda n: pl.BlockSpec((tm, n), lambda i: (i, 0))
    return pl.pallas_call(
        body,
        name="inproj_bwd",
        grid=(T // tm,),
        in_specs=[pl.BlockSpec((AW, tm), lambda i: (0, i)), row(AW), row(AW), row(NUS), row(FPAD), _full((D, ZP)),
                  _full((AW, D)), row(D), row(D), _full((1, D))],
        out_specs=[row(D), _full((1, D))],
        out_shape=[jax.ShapeDtypeStruct((T, D), F32), jax.ShapeDtypeStruct((1, D), F32)],
        compiler_params=_params(1),
    )(dqt, dk, dv, dus, dfz, wp, wqt, x, dh1, g)


def _sq_relu(f1):
    r = jnp.maximum(f1.astype(F32), 0.0)
    return (r * r).astype(BF16)


def _wgrad(a, bs, name, a_fn=None):
    T, K = a.shape
    tt = _tile(T, 512)
    nb = len(bs)

    def out_dims(b, layout):
        if layout == "t":
            return (K, b.shape[0])
        N = b.shape[1]
        if layout is None:
            return (K, N)
        return (N // layout[1], K, layout[1]) if layout[0] == "col" else (K // layout[1], layout[1], N)

    shapes = [out_dims(b, layout) for b, layout in bs]

    def body(*refs):
        a_ref, b_refs, o_refs = refs[0], refs[1:1 + nb], refs[1 + nb:]

        @pl.when(pl.program_id(0) == 0)
        def _():
            for o_ref in o_refs:
                o_ref[...] = jnp.zeros_like(o_ref)

        av = a_ref[...] if a_fn is None else a_fn(a_ref[...])
        at = av.astype(BF16).T
        for (b, layout), b_ref, o_ref in zip(bs, b_refs, o_refs):
            if layout is None:
                o_ref[...] += _nn(at, b_ref[...].astype(BF16))
            elif layout == "t":
                o_ref[...] += _nt(at, b_ref[...].astype(BF16))
            elif layout[0] == "col":
                n = layout[1]
                for k in range(b.shape[1] // n):
                    o_ref[k] += _nn(at, b_ref[:, k * n:(k + 1) * n].astype(BF16))
            else:
                n = layout[1]
                bv = b_ref[...].astype(BF16)
                for k in range(K // n):
                    o_ref[k] += _nn(at[k * n:(k + 1) * n, :], bv)

    once = lambda shape: pl.BlockSpec(shape, lambda t: (0,) * len(shape), pipeline_mode=pl.Buffered(1))
    res = pl.pallas_call(
        body,
        name=name,
        grid=(T // tt,),
        in_specs=[pl.BlockSpec((tt, K), lambda t: (t, 0))] + [
            pl.BlockSpec((b.shape[0], tt), lambda t: (0, t)) if layout == "t" else pl.BlockSpec((tt, b.shape[1]), lambda t: (t, 0))
            for b, layout in bs],
        out_specs=[once(s) for s in shapes],
        out_shape=[jax.ShapeDtypeStruct(s, F32) for s in shapes],
        compiler_params=_params(1, VMEM_LIMIT_BIG),
    )(a, *[b for b, _ in bs])
    return res


def _adam_math(w, g, m, v):
    m = ADAM_B1 * m + (1.0 - ADAM_B1) * g
    v = ADAM_B2 * v + (1.0 - ADAM_B2) * (g * g)
    m_hat = m / (1.0 - ADAM_B1 ** ADAM_STEP)
    v_hat = v / (1.0 - ADAM_B2 ** ADAM_STEP)
    delta = -ADAM_LR * (m_hat / (jnp.sqrt(v_hat) + ADAM_EPS) + ADAM_WD * w)
    return delta, m, v


def _adam(parts, w, m, v, name):
    R, C = w.shape
    br = 128 if R % 128 == 0 else R

    def body(p_ref, w_ref, m_ref, v_ref, g_ref, d_ref, nm_ref, nv_ref):
        g = p_ref[0].astype(F32)
        for s in range(1, NDEV):
            g = g + p_ref[s].astype(F32)
        g_ref[...] = g
        d_ref[...], nm_ref[...], nv_ref[...] = _adam_math(w_ref[...], g, m_ref[...], v_ref[...])

    blk = pl.BlockSpec((br, C), lambda i: (i, 0))
    return pl.pallas_call(
        body,
        name=name,
        grid=(R // br,),
        in_specs=[pl.BlockSpec((NDEV, br, C), lambda i: (0, i, 0)), blk, blk, blk],
        out_specs=[blk] * 4,
        out_shape=[jax.ShapeDtypeStruct((R, C), F32)] * 4,
        compiler_params=_params(1),
    )(parts, w, m, v)


_SMALL = (("sg_w", 8 * CH * CH), ("f_bias", 8), ("sg_ln_g", SW), ("sg_ln_b", SW), ("sg_b", 8 * CH), ("att_out_g", AW),
          ("sg_out_g", SW), ("pre_mix_g", D), ("post_mix_g", D), ("pre_ffn_g", D), ("post_ffn_g", D), ("ple_gate_b", D))
_SEG = 8 * 128


def _seg_rows(size):
    return 8 * (-(-size // _SEG))


def _pack(vals, loss_acc):
    parts = []
    for name, size in _SMALL:
        flat = vals[name].reshape(-1)
        rows = _seg_rows(size)
        parts.append(jnp.pad(flat, (0, rows * 128 - size)).reshape(rows, 128))
    parts.append(loss_acc)
    return jnp.concatenate(parts, axis=0)


def _adam_small(parts, ws, ms, vs):
    n = len(_SMALL)
    names = [name for name, _ in _SMALL]

    def body(*refs):
        p_ref, w_refs, m_refs, v_refs = refs[0], refs[1:1 + n], refs[1 + n:1 + 2 * n], refs[1 + 2 * n:1 + 3 * n]
        loss_ref, outs = refs[1 + 3 * n], refs[2 + 3 * n:]
        g_all = p_ref[0]
        for s in range(1, NDEV):
            g_all = g_all + p_ref[s]
        r = 0
        for k, (name, size) in enumerate(_SMALL):
            dst = [outs[kind * n + k] for kind in range(4)]

            def update(g, idx):
                vals = (g,) + _adam_math(w_refs[k][idx], g, m_refs[k][idx], v_refs[k][idx])
                for d, val in zip(dst, vals):
                    d[idx] = val

            if name == "sg_w":
                for grp in range(8):
                    update(g_all[r + grp * CH:r + (grp + 1) * CH, :], (0, grp))
            elif name == "sg_b":
                update(g_all[r:r + 8, :], (0,))
            elif name == "f_bias":
                update(g_all[r:r + 1, 0:8], (slice(None),))
            else:
                update(jnp.concatenate([g_all[r + q:r + q + 1, :] for q in range(size // 128)], axis=1), (slice(None),))
            r += _seg_rows(size)
        loss_ref[...] = g_all[r:r + 1, 0:1] * (0.5 / D)

    arrs = [parts] + [d[name] for d in (ws, ms, vs) for name in names]
    res = pl.pallas_call(
        body,
        name="adam_small",
        in_specs=[_full(a.shape) for a in arrs],
        out_specs=[_full((1, 1))] + [_full(ws[name].shape) for _ in range(4) for name in names],
        out_shape=[jax.ShapeDtypeStruct((1, 1), F32)] + [jax.ShapeDtypeStruct(ws[name].shape, F32) for _ in range(4) for name in names],
        compiler_params=pltpu.CompilerParams(vmem_limit_bytes=VMEM_LIMIT),
    )(*arrs)
    return res[0], {name: [res[1 + kind * n + k] for kind in range(4)] for k, name in enumerate(names)}


def kernel(x, p, w_in, f_bias, sg_ln_g, sg_ln_b, sg_w, sg_b, att_out_g, sg_out_g, w_out, pre_mix_g, post_mix_g, pre_ffn_g, post_ffn_g, w_ff1, w_ff2, ple_w, ple_gate_w, ple_gate_b, loss_target, m_w_in, m_f_bias, m_sg_ln_g, m_sg_ln_b, m_sg_w, m_sg_b, m_att_out_g, m_sg_out_g, m_w_out, m_pre_mix_g, m_post_mix_g, m_pre_ffn_g, m_post_ffn_g, m_w_ff1, m_w_ff2, m_ple_w, m_ple_gate_w, m_ple_gate_b, v_w_in, v_f_bias, v_sg_ln_g, v_sg_ln_b, v_sg_w, v_sg_b, v_att_out_g, v_sg_out_g, v_w_out, v_pre_mix_g, v_post_mix_g, v_pre_ffn_g, v_post_ffn_g, v_w_ff1, v_w_ff2, v_ple_w, v_ple_gate_w, v_ple_gate_b):
    small_w = dict(sg_w=sg_w, f_bias=f_bias, sg_ln_g=sg_ln_g, sg_ln_b=sg_ln_b, sg_b=sg_b, att_out_g=att_out_g,
                   sg_out_g=sg_out_g, pre_mix_g=pre_mix_g, post_mix_g=post_mix_g, pre_ffn_g=pre_ffn_g,
                   post_ffn_g=post_ffn_g, ple_gate_b=ple_gate_b)
    small_m = dict(sg_w=m_sg_w, f_bias=m_f_bias, sg_ln_g=m_sg_ln_g, sg_ln_b=m_sg_ln_b, sg_b=m_sg_b, att_out_g=m_att_out_g,
                   sg_out_g=m_sg_out_g, pre_mix_g=m_pre_mix_g, post_mix_g=m_post_mix_g, pre_ffn_g=m_pre_ffn_g,
                   post_ffn_g=m_post_ffn_g, ple_gate_b=m_ple_gate_b)
    small_v = dict(sg_w=v_sg_w, f_bias=v_f_bias, sg_ln_g=v_sg_ln_g, sg_ln_b=v_sg_ln_b, sg_b=v_sg_b, att_out_g=v_att_out_g,
                   sg_out_g=v_sg_out_g, pre_mix_g=v_pre_mix_g, post_mix_g=v_post_mix_g, pre_ffn_g=v_pre_ffn_g,
                   post_ffn_g=v_post_ffn_g, ple_gate_b=v_ple_gate_b)
    big = dict(w_in=(w_in, m_w_in, v_w_in), w_out=(w_out, m_w_out, v_w_out), w_ff1=(w_ff1, m_w_ff1, v_w_ff1),
               w_ff2=(w_ff2, m_w_ff2, v_w_ff2), ple_w=(ple_w, m_ple_w, v_ple_w),
               ple_gate_w=(ple_gate_w, m_ple_gate_w, v_ple_gate_w))

    xt, pt, tgt = x[0], p[0, 0], loss_target[0]
    ws = W_IN_COLS // NDEV

    gw_in = _gather(w_in[0].astype(BF16), "gather_w_in")
    rest = _xchg_start([w_out[0].astype(BF16), w_ff1[0].astype(BF16), w_ff2[0].astype(BF16), ple_w[0].astype(BF16),
                        ple_gate_w[0].astype(BF16)], True, "gather_rest_start")
    win = jnp.transpose(gw_in, (1, 0, 2)).reshape(D, W_IN_COLS)
    wp = jnp.concatenate([win[:, 0:NQKV], win[:, NQKV + 8:W_IN_COLS], win[:, NQKV:NQKV + 8],
                          jnp.zeros((D, FPAD - 8), BF16)], axis=1)

    fb = jnp.pad(f_bias.astype(F32), ((0, 0), (0, FPAD - 8)))
    sgw = sg_w[0]
    sgwt = jnp.transpose(sg_w[0], (0, 2, 1))
    sgbt = jnp.transpose(sg_b[0])

    qkv, kt, vt, us, fz, ab = _inproj(xt, pre_mix_g + rest[4][0, 0], wp)
    ct, kk, vk = _fcum(fz, fb, qkv)
    yatt, lsec = _attn_fwd(qkv, kk, vt, ct)
    gw_out, gw1, gw2, gwpe, gwg = _xchg_wait(rest, True, yatt, "gather_rest_wait")
    wout = gw_out.reshape(D, D)
    wg = gwg.reshape(D, D)
    wpe = jnp.transpose(gwpe, (1, 0, 2)).reshape(PLE, D)
    h1, yb, o = _sgu_out(us, yatt, xt, sg_ln_g, sg_ln_b, sgw, sgbt, att_out_g, sg_out_g, wout, post_mix_g)
    c2b, f1b, ff, h2 = _ffn_fwd(h1, pre_ffn_g, gw1, gw2, post_ffn_g)
    dh2, dbg, loss_acc, g_g, g_pe = _ple_loss(h2, pt, tgt, wg, ple_gate_b, wpe)
    g_g = g_g.reshape(NDEV, D // NDEV, D)
    g_pe = jnp.transpose(g_pe.reshape(PLE, NDEV, D // NDEV), (1, 0, 2))

    dffb, df1, dh1, dgpostffn, dgpreffn = _ffn_bwd(dh2, ff, h1, f1b, gw1, gw2, post_ffn_g, pre_ffn_g)
    (g_1,) = _wgrad(c2b, [(df1, ("col", DFF // NDEV))], "wgrad_ff1")
    (g_2,) = _wgrad(f1b, [(dffb, ("row", DFF // NDEV))], "wgrad_ff2", a_fn=_sq_relu)
    early = _xchg_start([g_1, g_2, g_pe, g_g], False, "scatter_early_start")
    dob, dya, dus, dsgw, dsgbt, dlng, dlnb, dgatt, dgsg, dgpostmix = _mix_bwd(
        dh1, o, us, yatt, sg_ln_g, sg_ln_b, sgw, sgwt, sgbt, att_out_g, sg_out_g, wout, post_mix_g + early[4][0, 0])
    (g_out,) = _wgrad(yb, [(dob, ("row", D // NDEV))], "wgrad_out")
    mid = _xchg_start([g_out], False, "scatter_mid_start")
    qw, dow = _attn_prep(qkv, dya, yatt, lsec, mid[4])
    dqt, dcq, dk, dv, dck = _attn_bwd(kk, vk, kt, qw, dow)
    dcq = jnp.pad(jnp.transpose(dcq[:, 0:2, :].reshape(8, -1)), ((0, 0), (0, FPAD - 8)))
    dck = jnp.pad(dck.reshape(-1, 4, 128)[:, :, 0:2].reshape(-1, 8), ((0, 0), (0, FPAD - 8)))
    dfz, dfb = _fgate_bwd(dcq, dck, fz, fb)

    gq, gk, gv, gus, gf = _wgrad(ab, [(dqt, "t"), (dk, None), (dv, None), (dus, None), (dfz, None)], "wgrad_in")
    g_in = jnp.concatenate([gq, gk, gv, gf[:, 0:8], gus], axis=1)
    g_in = jnp.transpose(g_in.reshape(D, NDEV, ws), (1, 0, 2))
    late = _xchg_start([g_in.astype(BF16)], False, "scatter_late_start")
    grad_x, dgpremix = _inproj_bwd(dqt, dk, dv, dus, dfz, wp, jnp.transpose(wp[:, 0:AW]), xt, dh1,
                                   pre_mix_g + late[4][0, 0])

    small_g = dict(sg_w=dsgw, f_bias=dfb[:, 0:8], sg_ln_g=dlng, sg_ln_b=dlnb, sg_b=jnp.transpose(dsgbt[:, 0:8]),
                   att_out_g=dgatt, sg_out_g=dgsg, pre_mix_g=dgpremix, post_mix_g=dgpostmix, pre_ffn_g=dgpreffn,
                   post_ffn_g=dgpostffn, ple_gate_b=dbg)

    r_1, r_2, r_pe, r_g = _xchg_wait(early, False, grad_x, "scatter_early_wait")
    (r_out,) = _xchg_wait(mid, False, grad_x, "scatter_mid_wait")
    r_small = _gather(_pack(small_g, loss_acc), "gather_small_grads")

    res = {}

    def adam_big(name, parts):
        w, m, v = big[name]
        res[name] = [t[None] for t in _adam(parts, w[0], m[0], v[0], "adam_" + name)]

    for name, parts in (("w_out", r_out), ("w_ff1", r_1), ("w_ff2", r_2), ("ple_w", r_pe), ("ple_gate_w", r_g)):
        adam_big(name, parts)
    (r_in,) = _xchg_wait(late, False, res["w_ff1"][0], "scatter_late_wait")
    adam_big("w_in", r_in)
    loss, small = _adam_small(r_small, small_w, small_m, small_v)
    res.update(small)

    order = ["w_in", "f_bias", "sg_ln_g", "sg_ln_b", "sg_w", "sg_b", "att_out_g", "sg_out_g", "w_out", "pre_mix_g",
             "post_mix_g", "pre_ffn_g", "post_ffn_g", "w_ff1", "w_ff2", "ple_w", "ple_gate_w", "ple_gate_b"]
    outs = [loss[0, 0], grad_x[None]]
    for kind in range(4):
        outs += [res[name][kind] for name in order]
    return tuple(outs)
```

```python
import jax
import jax.numpy as jnp
from jax import lax
from jax.experimental import pallas as pl
from jax.experimental.pallas import tpu as pltpu

F32 = jnp.float32
BF16 = jnp.bfloat16

NDEV = 8
D = 1024
AW = 512
SW = 512
HD = 64
CH = 128
DFF = 4096
PLE = 256
NQKV = 3 * AW
NUS = 2 * SW
FPAD = 128
ZP = NQKV + NUS + FPAD
W_IN_COLS = 2568
EPS = 1e-6
MASKV = -1e30
GELU_K = 0.7978845608028654
GELU_C = 0.044715

ADAM_LR = 0.001
ADAM_B1 = 0.9
ADAM_B2 = 0.999
ADAM_EPS = 1e-08
ADAM_WD = 0.01
ADAM_STEP = 10

VMEM_LIMIT = 48 * 1024 * 1024
VMEM_LIMIT_BIG = 60 * 1024 * 1024


def _nn(a, b):
    return jnp.dot(a, b, preferred_element_type=F32)


def _nt(a, b):
    return lax.dot_general(a, b, (((1,), (1,)), ((), ())), preferred_element_type=F32)


def _tn(a, b):
    return lax.dot_general(a, b, (((0,), (0,)), ((), ())), preferred_element_type=F32)


def _tile(n, pref):
    return min(n, pref)


def _params(n_axes, vmem=VMEM_LIMIT):
    return pltpu.CompilerParams(dimension_semantics=("arbitrary",) * n_axes, vmem_limit_bytes=vmem)


def _full(shape):
    nd = len(shape)
    return pl.BlockSpec(shape, lambda *_: (0,) * nd)


def _rms_fwd(x, g):
    r = lax.rsqrt(jnp.mean(x * x, axis=-1, keepdims=True) + EPS)
    return x * r * g


def _rms_bwd(dy, x, g):
    n = x.shape[-1]
    r = lax.rsqrt(jnp.mean(x * x, axis=-1, keepdims=True) + EPS)
    u = dy * g
    s = jnp.sum(x * u, axis=-1, keepdims=True)
    dx = r * u - x * (r * r * r * (s * (1.0 / n)))
    dg = jnp.sum(dy * (x * r), axis=0, keepdims=True)
    return dx, dg


def _gelu(x):
    t = jnp.tanh(x * (GELU_K + (GELU_K * GELU_C) * (x * x)))
    return x * (0.5 + 0.5 * t), t


def _gelu_grad(x, t):
    return (0.5 + 0.5 * t) + (0.5 * x) * (1.0 - t * t) * (GELU_K + (3.0 * GELU_K * GELU_C) * (x * x))


def _gather(arr, name):
    def body(x_ref, out_ref, send, recv, loc):
        x, y, c = lax.axis_index("x"), lax.axis_index("y"), lax.axis_index("c")
        me, sibling = (x, y, c), (x, y, 1 - c)
        chips = [(1 - x, y), (x, 1 - y), (1 - x, 1 - y)]

        def slot(px, py, pc):
            return out_ref.at[4 * px + 2 * py + pc]

        def copy(k, block, to, src=None):
            return pltpu.make_async_remote_copy(
                src_ref=slot(*block) if src is None else src, dst_ref=slot(*block), send_sem=send.at[k],
                recv_sem=recv.at[k], device_id=to, device_id_type=pl.DeviceIdType.MESH)

        mine = pltpu.make_async_copy(x_ref, slot(*me), loc)
        mine.start()
        first = [copy(0, me, sibling, src=x_ref)] + [copy(1 + j, me, (*chip, c), src=x_ref) for j, chip in enumerate(chips)]
        for cp in first:
            cp.start()
        passed = [copy(4 + j, (*chip, c), sibling) for j, chip in enumerate(chips)]
        for j, chip in enumerate(chips):
            copy(1 + j, (*chip, c), me).wait_recv()
            passed[j].start()
        copy(0, sibling, me).wait_recv()
        for j, chip in enumerate(chips):
            copy(4 + j, (*chip, 1 - c), me).wait_recv()
        for cp in first + passed:
            cp.wait_send()
        mine.wait()

    hbm = pl.BlockSpec(memory_space=pltpu.HBM)
    return pl.pallas_call(
        body,
        name=name,
        out_shape=jax.ShapeDtypeStruct((NDEV,) + arr.shape, arr.dtype),
        in_specs=[hbm],
        out_specs=hbm,
        scratch_shapes=[pltpu.SemaphoreType.DMA((NDEV - 1,)), pltpu.SemaphoreType.DMA((NDEV - 1,)), pltpu.SemaphoreType.DMA],
    )(arr)


def _peers(x, y, c):
    out = []
    for k in range(1, NDEV):
        out.append((1 - x if (k >> 2) & 1 else x, 1 - y if (k >> 1) & 1 else y, 1 - c if k & 1 else c))
    return out


def _xchg_start(arrs, gather, name):
    n = len(arrs)
    me = 4 * lax.axis_index("x") + 2 * lax.axis_index("y") + lax.axis_index("c")
    lands = []
    for a in arrs:
        shape = ((NDEV,) + a.shape) if gather else a.shape
        own = a[None] if gather else lax.dynamic_slice_in_dim(a, me, 1, axis=0)
        lands.append(lax.dynamic_update_slice_in_dim(lax.empty(shape, a.dtype), own, me, axis=0))

    def body(*refs):
        ins, lnd = refs[:n], refs[n:2 * n]
        send, recv, token = refs[2 * n:3 * n], refs[3 * n:4 * n], refs[-1]
        x, y, c = lax.axis_index("x"), lax.axis_index("y"), lax.axis_index("c")
        mine = 4 * x + 2 * y + c
        for px, py, pc in _peers(x, y, c):
            peer = 4 * px + 2 * py + pc
            for a in range(n):
                pltpu.make_async_remote_copy(
                    src_ref=ins[a] if gather else ins[a].at[peer],
                    dst_ref=lnd[a].at[mine],
                    send_sem=send[a],
                    recv_sem=recv[a],
                    device_id=(px, py, pc),
                    device_id_type=pl.DeviceIdType.MESH,
                ).start()
        token[...] = jnp.zeros_like(token)

    hbm = pl.BlockSpec(memory_space=pltpu.HBM)
    sem = pl.BlockSpec(memory_space=pltpu.SEMAPHORE)
    res = pl.pallas_call(
        body,
        name=name,
        out_shape=(*[pltpu.SemaphoreType.DMA(())] * (2 * n),
                   *[pltpu.HBM(a.shape, a.dtype) for a in arrs], *[pltpu.HBM(l.shape, l.dtype) for l in lands],
                   jax.ShapeDtypeStruct((8, 128), F32)),
        in_specs=[hbm] * (2 * n),
        out_specs=(*([sem] * (2 * n)), *([hbm] * (2 * n)), pl.BlockSpec(memory_space=pltpu.VMEM)),
        input_output_aliases={i: 2 * n + i for i in range(2 * n)},
        compiler_params=pltpu.CompilerParams(has_side_effects=pltpu.SideEffectType.DATAFLOW_SIDE_EFFECTING),
    )(*[pltpu.with_memory_space_constraint(a, pltpu.HBM) for a in arrs],
      *[pltpu.with_memory_space_constraint(l, pltpu.HBM) for l in lands])
    return list(res[0:n]), list(res[n:2 * n]), list(res[2 * n:3 * n]), list(res[3 * n:4 * n]), res[-1]


def _xchg_wait(started, gather, after, name):
    send, recv, srcs, lands, _ = started
    n = len(srcs)

    def body(*refs):
        lnd = refs[n:2 * n]
        send, recv = refs[2 * n:3 * n], refs[3 * n:4 * n]
        me = (lax.axis_index("x"), lax.axis_index("y"), lax.axis_index("c"))
        for a in range(n):
            seven = lnd[a].at[pl.ds(0, NDEV - 1)]
            cp = pltpu.make_async_remote_copy(src_ref=seven, dst_ref=seven, send_sem=send[a], recv_sem=recv[a],
                                              device_id=me, device_id_type=pl.DeviceIdType.MESH)
            cp.wait_send()
            cp.wait_recv()

    hbm = pl.BlockSpec(memory_space=pltpu.HBM)
    sem = pl.BlockSpec(memory_space=pltpu.SEMAPHORE)
    res = pl.pallas_call(
        body,
        name=name,
        out_shape=tuple([pltpu.HBM(a.shape, a.dtype) for a in srcs] + [pltpu.HBM(l.shape, l.dtype) for l in lands]),
        in_specs=[hbm] * (2 * n) + [sem] * (2 * n) + [pl.BlockSpec(memory_space=pl.ANY)],
        out_specs=tuple([hbm] * (2 * n)),
        input_output_aliases={i: i for i in range(2 * n)},
        compiler_params=pltpu.CompilerParams(has_side_effects=pltpu.SideEffectType.DATAFLOW_SIDE_EFFECTING),
    )(*srcs, *lands, *send, *recv, after)
    return list(res[n:])


def _inproj(x, g, wp):
    T = x.shape[0]
    tm = _tile(T, 512)

    def body(x_ref, g_ref, w_ref, qkv_ref, kt_ref, vt_ref, us_ref, fz_ref, ab_ref):
        a = _rms_fwd(x_ref[...], g_ref[...]).astype(BF16)
        ab_ref[...] = a
        qkv_ref[:, 0:AW] = _nn(a, w_ref[:, 0:AW]).astype(BF16)
        kk = _nn(a, w_ref[:, AW:2 * AW])
        qkv_ref[:, AW:2 * AW] = kk.astype(BF16)
        kt_ref[...] = kk.T.astype(BF16)
        vv = _nn(a, w_ref[:, 2 * AW:NQKV])
        qkv_ref[:, 2 * AW:NQKV] = vv.astype(BF16)
        vt_ref[...] = vv.T.astype(BF16)
        us_ref[...] = _nn(a, w_ref[:, NQKV:NQKV + NUS])
        fz_ref[...] = _nn(a, w_ref[:, NQKV + NUS:ZP])

    row = lambda n: pl.BlockSpec((tm, n), lambda i: (i, 0))
    col = pl.BlockSpec((AW, tm), lambda i: (0, i))
    return pl.pallas_call(
        body,
        name="inproj",
        grid=(T // tm,),
        in_specs=[row(D), _full((1, D)), _full((D, ZP))],
        out_specs=[row(NQKV), col, col, row(NUS), row(FPAD), row(D)],
        out_shape=[
            jax.ShapeDtypeStruct((T, NQKV), BF16),
            jax.ShapeDtypeStruct((AW, T), BF16),
            jax.ShapeDtypeStruct((AW, T), BF16),
            jax.ShapeDtypeStruct((T, NUS), F32),
            jax.ShapeDtypeStruct((T, FPAD), F32),
            jax.ShapeDtypeStruct((T, D), BF16),
        ],
        compiler_params=_params(1),
    )(x, g, wp)


def _log_sigmoid(z):
    return jnp.minimum(z, 0.0) - jnp.log1p(jnp.exp(-jnp.abs(z)))


def _split3(x):
    hi = x.astype(BF16)
    r1 = x - hi.astype(F32)
    mid = r1.astype(BF16)
    lo = (r1 - mid.astype(F32)).astype(BF16)
    return hi, mid, lo


AUG_A, AUG_B, AUG_ONE = 0, 3, 6


def _aug_lanes(rows, first_one, pieces):
    lane = lax.broadcasted_iota(jnp.int32, (rows, 128), 1)
    out = jnp.zeros((rows, 128), F32)
    if first_one is not None:
        out = jnp.where((lane >= first_one) & (lane < first_one + 3), 1.0, out)
    for n, piece in enumerate(pieces):
        out = jnp.where(lane == AUG_ONE + n, piece.astype(F32), out)
    return out.astype(BF16)


def _fcum(fz, fb, qkv):
    T = fz.shape[0]
    tb = _tile(T, 512)

    def body(fz_ref, fb_ref, k_ref, v_ref, ct_ref, kk_ref, vk_ref, carry):
        @pl.when(pl.program_id(0) == 0)
        def _():
            carry[...] = jnp.zeros_like(carry)

        lf = _log_sigmoid(fz_ref[...] + fb_ref[...])
        r = lax.broadcasted_iota(jnp.int32, (tb, tb), 0)
        cc = lax.broadcasted_iota(jnp.int32, (tb, tb), 1)
        tri = (cc <= r).astype(F32)
        cs = jnp.dot(tri, lf, precision=lax.Precision.HIGHEST, preferred_element_type=F32) + carry[...]
        ct_ref[...] = cs.T[0:8, :]
        carry[...] = carry[...] + jnp.sum(lf, axis=0, keepdims=True)

        pieces = _split3(cs)
        src = lax.broadcasted_iota(jnp.int32, (128, 128), 0)
        dst = lax.broadcasted_iota(jnp.int32, (128, 128), 1)
        ones = jnp.where((dst[0:1, :] >= AUG_ONE) & (dst[0:1, :] < AUG_ONE + 3), 1.0, 0.0)
        for hp in range(4):
            aug = jnp.zeros((tb, 128), F32) + ones
            for n, piece in enumerate(pieces):
                sel = jnp.where(((src == 2 * hp) & (dst == AUG_A + n)) | ((src == 2 * hp + 1) & (dst == AUG_B + n)), -1.0, 0.0)
                aug = aug + _nn(piece, sel.astype(BF16))
            aug = aug.astype(BF16)
            kk_ref[:, hp * 256:hp * 256 + 128] = k_ref[:, hp * 128:(hp + 1) * 128]
            kk_ref[:, hp * 256 + 128:(hp + 1) * 256] = aug
            vk_ref[:, hp * 256:hp * 256 + 128] = v_ref[:, hp * 128:(hp + 1) * 128]
            vk_ref[:, hp * 256 + 128:(hp + 1) * 256] = aug

    row = lambda n, c: pl.BlockSpec((tb, n), lambda i: (i, c))
    return pl.pallas_call(
        body,
        name="fcum",
        grid=(T // tb,),
        in_specs=[row(FPAD, 0), _full((1, FPAD)), row(AW, 1), row(AW, 2)],
        out_specs=[pl.BlockSpec((8, tb), lambda i: (0, i)), row(2 * AW, 0), row(2 * AW, 0)],
        out_shape=[jax.ShapeDtypeStruct((8, T), F32), jax.ShapeDtypeStruct((T, 2 * AW), BF16),
                   jax.ShapeDtypeStruct((T, 2 * AW), BF16)],
        scratch_shapes=[pltpu.VMEM((1, FPAD), F32)],
        compiler_params=_params(1),
    )(fz, fb, qkv, qkv)


def _fold_rows(r, t, nq):
    if nq == 1:
        return r, t
    low = t <= r
    return jnp.where(low, r, nq - 1 - r), jnp.where(low, t, t - r - 1)


def _fold_cols(r, t, nq):
    if nq == 1:
        return r, t
    first = t < nq - r
    j = jnp.where(first, r, nq - 1 - r)
    return j, jnp.where(first, r + t, nq - 1 - r + (t - (nq - r)))


PAIRS = 4


def _fold_grid(nq):
    assert nq == 1 or nq % 2 == 0
    return (4 // PAIRS, 1, 1) if nq == 1 else (4 // PAIRS, nq // 2, nq + 1)


def _head_rows(x2, hh, scale):
    is_a = lax.broadcasted_iota(jnp.int32, (1, 128), 1) < HD
    keep = is_a if hh == 0 else jnp.logical_not(is_a)
    return jnp.where(keep, x2, jnp.zeros_like(x2)) * scale


def _attn_fwd(qkv, kk, vt, ct):
    T = qkv.shape[0]
    tq = _tile(T, 512)
    tk = tq
    nq = T // tq

    def body(q_ref, kk_ref, vt_ref, ctq_ref, o_ref, lsec_ref, qw_s, m_s, l_s, acc_s):
        i, j = _fold_rows(pl.program_id(1), pl.program_id(2), nq)
        sub8 = lax.broadcasted_iota(jnp.int32, (8, 1), 0)

        def cref_of(pp, hh):
            head = 2 * (PAIRS * pl.program_id(0) + pp) + hh
            return jnp.sum(jnp.where(sub8 == head, ctq_ref[:, 0:1], 0.0), axis=0, keepdims=True)

        @pl.when(j == 0)
        def _():
            for pp in range(PAIRS):
                q2 = q_ref[:, pp * 128:(pp + 1) * 128]
                for hh in range(2):
                    rows = slice(hh * tq, (hh + 1) * tq)
                    qw_s[pp, rows, 0:128] = _head_rows(q2, hh, 0.125)
                    qw_s[pp, rows, 128:256] = _aug_lanes(tq, AUG_A if hh == 0 else AUG_B, _split3(cref_of(pp, hh)))
            m_s[...] = jnp.full_like(m_s, MASKV)
            l_s[...] = jnp.zeros_like(l_s)
            acc_s[...] = jnp.zeros_like(acc_s)

        def step(masked):
            if masked:
                causal = lax.broadcasted_iota(jnp.int32, (tk, tq), 0) <= lax.broadcasted_iota(jnp.int32, (tk, tq), 1)
            logits = lambda pp: _nt(kk_ref[:, pp * 256:(pp + 1) * 256], qw_s[pp])
            s_next = logits(0)
            for pp in range(PAIRS):
                s2 = s_next
                if pp + 1 < PAIRS:
                    s_next = logits(pp + 1)
                vt2 = vt_ref[pp * 128:(pp + 1) * 128, :]
                for hh in range(2):
                    n = 2 * pp + hh
                    s = s2[:, hh * tq:(hh + 1) * tq]
                    if masked:
                        s = jnp.where(causal, s, MASKV)
                    m_prev = m_s[n]
                    m_new = jnp.maximum(m_prev, jnp.max(s, axis=0, keepdims=True))
                    pr = jnp.exp(s - m_new)
                    alpha = jnp.exp(m_prev - m_new)
                    l_s[n] = alpha * l_s[n] + jnp.sum(pr, axis=0, keepdims=True)
                    m_s[n] = m_new
                    acc_s[n] = alpha * acc_s[n] + _nn(vt2, pr.astype(BF16))

        @pl.when(j < i)
        def _():
            step(False)

        @pl.when(j == i)
        def _():
            step(True)
            sub = lax.broadcasted_iota(jnp.int32, (128, 1), 0)
            for pp in range(PAIRS):
                a, b = 2 * pp, 2 * pp + 1
                ot = jnp.where(sub < HD, acc_s[a] * (1.0 / l_s[a]), acc_s[b] * (1.0 / l_s[b]))
                o_ref[:, pp * 128:(pp + 1) * 128] = ot.T
                lrow = [m_s[2 * pp + hh] + jnp.log(l_s[2 * pp + hh]) - cref_of(pp, hh) for hh in range(2)]
                lsec_ref[:, pp * 128:(pp + 1) * 128] = jnp.where(sub == 0, lrow[0], jnp.where(sub == 1, lrow[1], 0.0)).T

    qi = lambda r, t: _fold_rows(r, t, nq)[0]
    kj = lambda r, t: _fold_rows(r, t, nq)[1]
    return pl.pallas_call(
        body,
        name="attn_fwd",
        grid=_fold_grid(nq),
        in_specs=[
            pl.BlockSpec((tq, PAIRS * 128), lambda g, r, t: (qi(r, t), g)),
            pl.BlockSpec((tk, PAIRS * 256), lambda g, r, t: (kj(r, t), g)),
            pl.BlockSpec((PAIRS * 128, tk), lambda g, r, t: (g, kj(r, t))),
            pl.BlockSpec((8, tq), lambda g, r, t: (0, qi(r, t))),
        ],
        out_specs=[pl.BlockSpec((tq, PAIRS * 128), lambda g, r, t: (qi(r, t), g))] * 2,
        out_shape=[jax.ShapeDtypeStruct((T, AW), F32)] * 2,
        scratch_shapes=[pltpu.VMEM((PAIRS, 2 * tq, 256), BF16), pltpu.VMEM((2 * PAIRS, 1, tq), F32),
                        pltpu.VMEM((2 * PAIRS, 1, tq), F32), pltpu.VMEM((2 * PAIRS, 128, tq), F32)],
        compiler_params=_params(3),
    )(qkv, kk, vt, ct)


def _sgu_forward(us_ref, lng, lnb, w_ref, bt_ref, mixed_s, vnb_s, tm):
    is_a = lax.broadcasted_iota(jnp.int32, (1, 128), 1) < HD
    u = us_ref[:, 0:SW]
    vs = us_ref[:, SW:NUS]
    ug, tu = _gelu(u)
    vg, tv = _gelu(vs)
    mu = jnp.mean(vg, axis=-1, keepdims=True)
    xc = vg - mu
    rstd = lax.rsqrt(jnp.mean(xc * xc, axis=-1, keepdims=True) + EPS)
    vhat = xc * rstd
    vnb_s[...] = (vhat * lng + lnb).astype(BF16)
    rr = lax.broadcasted_iota(jnp.int32, (CH, CH), 0)
    cc = lax.broadcasted_iota(jnp.int32, (CH, CH), 1)
    tril = cc <= rr
    for jj in range(4):
        wa = jnp.where(tril, w_ref[2 * jj], 0.0).astype(BF16)
        wb = jnp.where(tril, w_ref[2 * jj + 1], 0.0).astype(BF16)
        ba = bt_ref[:, 2 * jj:2 * jj + 1]
        bb = bt_ref[:, 2 * jj + 1:2 * jj + 2]
        for ch in range(tm // CH):
            rs, cs = slice(ch * CH, (ch + 1) * CH), slice(jj * 128, (jj + 1) * 128)
            vn2 = vnb_s[rs, cs]
            mixed_s[rs, cs] = jnp.where(is_a, _nn(wa, vn2) + ba, _nn(wb, vn2) + bb)
    mixed = mixed_s[...]
    return u, vs, ug, tu, tv, vhat, rstd, mixed, ug * mixed


def _sgu_out(us, yatt, x, lng, lnb, sgw, sgbt, gatt, gsg, wout, gpm):
    T = us.shape[0]
    tm = _tile(T, 512)

    def body(us_ref, ya_ref, x_ref, lng_ref, lnb_ref, w_ref, bt_ref, ga_ref, gs_ref, wo_ref, gp_ref,
             h1_ref, yb_ref, o_ref, mixed_s, vnb_s):
        ysg = _sgu_forward(us_ref, lng_ref[...], lnb_ref[...], w_ref, bt_ref, mixed_s, vnb_s, tm)[-1]
        yb_ref[:, 0:AW] = _rms_fwd(ya_ref[...], ga_ref[...]).astype(BF16)
        yb_ref[:, AW:D] = _rms_fwd(ysg, gs_ref[...]).astype(BF16)
        o = _nn(yb_ref[...], wo_ref[...])
        o_ref[...] = o
        h1_ref[...] = x_ref[...] + _rms_fwd(o, gp_ref[...])

    row = lambda n: pl.BlockSpec((tm, n), lambda i: (i, 0))
    return pl.pallas_call(
        body,
        name="sgu_out",
        grid=(T // tm,),
        in_specs=[row(NUS), row(AW), row(D), _full((1, SW)), _full((1, SW)), _full((8, CH, CH)), _full((CH, 8)),
                  _full((1, AW)), _full((1, SW)), _full((D, D)), _full((1, D))],
        out_specs=[row(D), row(D), row(D)],
        out_shape=[jax.ShapeDtypeStruct((T, D), F32), jax.ShapeDtypeStruct((T, D), BF16), jax.ShapeDtypeStruct((T, D), F32)],
        scratch_shapes=[pltpu.VMEM((tm, SW), F32), pltpu.VMEM((tm, SW), BF16)],
        compiler_params=_params(1),
    )(us, yatt, x, lng, lnb, sgw, sgbt, gatt, gsg, wout, gpm)


def _ffn_fwd(h1, gpre, w1g, w2g, gpost):
    T = h1.shape[0]
    tm = _tile(T, 512)
    nb, hb = w1g.shape[0], w1g.shape[2]

    def body(h1_ref, gpre_ref, w1_ref, w2_ref, gpost_ref, c2_ref, f1_ref, ff_ref, h2_ref):
        h1 = h1_ref[...]
        c2 = _rms_fwd(h1, gpre_ref[...]).astype(BF16)
        c2_ref[...] = c2
        ff = jnp.zeros((tm, D), F32)
        for j in range(nb):
            f1 = _nn(c2, w1_ref[j])
            f1_ref[:, j * hb:(j + 1) * hb] = f1.astype(BF16)
            r = jnp.maximum(f1, 0.0)
            ff = ff + _nn((r * r).astype(BF16), w2_ref[j])
        ff_ref[...] = ff
        h2_ref[...] = h1 + _rms_fwd(ff, gpost_ref[...])

    row = lambda n: pl.BlockSpec((tm, n), lambda i: (i, 0))
    once = lambda shape: pl.BlockSpec(shape, lambda i: (0,) * len(shape), pipeline_mode=pl.Buffered(1))
    return pl.pallas_call(
        body,
        name="ffn_fwd",
        grid=(T // tm,),
        in_specs=[row(D), _full((1, D)), once((nb, D, hb)), once((nb, hb, D)), _full((1, D))],
        out_specs=[row(D), row(DFF), row(D), row(D)],
        out_shape=[jax.ShapeDtypeStruct((T, D), BF16), jax.ShapeDtypeStruct((T, DFF), BF16),
                   jax.ShapeDtypeStruct((T, D), F32), jax.ShapeDtypeStruct((T, D), F32)],
        compiler_params=_params(1, VMEM_LIMIT_BIG),
    )(h1, gpre, w1g, w2g, gpost)


def _ple_loss(h2, p, tgt, wg, bg, wpe):
    T = h2.shape[0]
    tm = _tile(T, 512)

    def body(h2_ref, p_ref, t_ref, wg_ref, bg_ref, wpe_ref, dh2_ref, dbg_ref, loss_ref, dwg_ref, dwpe_ref):
        @pl.when(pl.program_id(0) == 0)
        def _():
            for r in (dbg_ref, loss_ref, dwg_ref, dwpe_ref):
                r[...] = jnp.zeros_like(r)

        h2 = h2_ref[...]
        h2b = h2.astype(BF16)
        gate = jax.nn.sigmoid(_nn(h2b, wg_ref[...]) + bg_ref[...])
        pb = p_ref[...].astype(BF16)
        pe = _nn(pb, wpe_ref[...])
        diff = (h2 + gate * pe) - t_ref[...]
        loss_ref[...] += jnp.sum(diff * diff)
        dh3 = diff * (1.0 / D)
        dpre = (dh3 * pe) * (gate * (1.0 - gate))
        dpre_b = dpre.astype(BF16)
        dbg_ref[...] += jnp.sum(dpre, axis=0, keepdims=True)
        dh2_ref[...] = dh3 + _nt(dpre_b, wg_ref[...])
        dwg_ref[...] += _nn(h2b.T, dpre_b)
        dwpe_ref[...] += _nn(pb.T, (dh3 * gate).astype(BF16))

    row = lambda n: pl.BlockSpec((tm, n), lambda i: (i, 0))
    once = lambda shape: pl.BlockSpec(shape, lambda i: (0,) * len(shape), pipeline_mode=pl.Buffered(1))
    return pl.pallas_call(
        body,
        name="ple_loss",
        grid=(T // tm,),
        in_specs=[row(D), row(PLE), row(D), _full((D, D)), _full((1, D)), _full((PLE, D))],
        out_specs=[row(D), _full((1, D)), _full((8, 128)), once((D, D)), once((PLE, D))],
        out_shape=[jax.ShapeDtypeStruct((T, D), F32), jax.ShapeDtypeStruct((1, D), F32),
                   jax.ShapeDtypeStruct((8, 128), F32), jax.ShapeDtypeStruct((D, D), F32),
                   jax.ShapeDtypeStruct((PLE, D), F32)],
        compiler_params=_params(1),
    )(h2, p, tgt, wg, bg, wpe)


def _ffn_bwd(dh2, ff, h1, f1, w1g, w2g, gpost, gpre):
    T = dh2.shape[0]
    tm = _tile(T, 512)
    nb, hb = w1g.shape[0], w1g.shape[2]

    def body(dh2_ref, ff_ref, h1_ref, f1_ref, w1_ref, w2_ref, gpost_ref, gpre_ref,
             dffb_ref, df1_ref, dh1_ref, dgpost_ref, dgpre_ref):
        @pl.when(pl.program_id(0) == 0)
        def _():
            dgpost_ref[...] = jnp.zeros_like(dgpost_ref)
            dgpre_ref[...] = jnp.zeros_like(dgpre_ref)

        dh2 = dh2_ref[...]
        dff, dg = _rms_bwd(dh2, ff_ref[...], gpost_ref[...])
        dffb = dff.astype(BF16)
        dffb_ref[...] = dffb
        dgpost_ref[...] += dg
        dc2 = jnp.zeros((tm, D), F32)
        for j in range(nb):
            cols = slice(j * hb, (j + 1) * hb)
            dact = _nt(dffb, w2_ref[j])
            df1 = (dact * (2.0 * jnp.maximum(f1_ref[:, cols].astype(F32), 0.0))).astype(BF16)
            df1_ref[:, cols] = df1
            dc2 = dc2 + _nt(df1, w1_ref[j])
        dx, dg = _rms_bwd(dc2, h1_ref[...], gpre_ref[...])
        dh1_ref[...] = dh2 + dx
        dgpre_ref[...] += dg

    row = lambda n: pl.BlockSpec((tm, n), lambda i: (i, 0))
    once = lambda shape: pl.BlockSpec(shape, lambda i: (0,) * len(shape), pipeline_mode=pl.Buffered(1))
    return pl.pallas_call(
        body,
        name="ffn_bwd",
        grid=(T // tm,),
        in_specs=[row(D), row(D), row(D), row(DFF), once((nb, D, hb)), once((nb, hb, D)), _full((1, D)), _full((1, D))],
        out_specs=[row(D), row(DFF), row(D), _full((1, D)), _full((1, D))],
        out_shape=[jax.ShapeDtypeStruct((T, D), BF16), jax.ShapeDtypeStruct((T, DFF), BF16),
                   jax.ShapeDtypeStruct((T, D), F32), jax.ShapeDtypeStruct((1, D), F32),
                   jax.ShapeDtypeStruct((1, D), F32)],
        compiler_params=_params(1, VMEM_LIMIT_BIG),
    )(dh2, ff, h1, f1, w1g, w2g, gpost, gpre)


def _mix_bwd(dh1, o, us, yatt, lng, lnb, sgw, sgwt, sgbt, gatt, gsg, wout, gpm):
    T = dh1.shape[0]
    tm = _tile(T, 512)

    def body(dh1_ref, o_ref, us_ref, ya_ref, lng_ref, lnb_ref, w_ref, wt_ref, bt_ref, ga_ref, gs_ref, wo_ref, gp_ref,
             dob_ref, dya_ref, dus_ref, dw_ref, dbt_ref, dlng_ref, dlnb_ref, dga_ref, dgs_ref, dgp_ref,
             mixed_s, vnb_s, dvn_s):
        @pl.when(pl.program_id(0) == 0)
        def _():
            for r in (dw_ref, dbt_ref, dlng_ref, dlnb_ref, dga_ref, dgs_ref, dgp_ref):
                r[...] = jnp.zeros_like(r)

        is_a = lax.broadcasted_iota(jnp.int32, (1, 128), 1) < HD
        lane = lax.broadcasted_iota(jnp.int32, (1, 128), 1)
        do, dg = _rms_bwd(dh1_ref[...], o_ref[...], gp_ref[...])
        dgp_ref[...] += dg
        dob = do.astype(BF16)
        dob_ref[...] = dob
        dy = _nt(dob, wo_ref[...])
        datt, dg = _rms_bwd(dy[:, 0:AW], ya_ref[...], ga_ref[...])
        dga_ref[...] += dg
        dya_ref[...] = datt

        lng = lng_ref[...]
        u, vs, ug, tu, tv, vhat, rstd, mixed, ysg = _sgu_forward(us_ref, lng, lnb_ref[...], w_ref, bt_ref, mixed_s, vnb_s, tm)
        dysg, dg = _rms_bwd(dy[:, AW:D], ysg, gs_ref[...])
        dgs_ref[...] += dg
        dus_ref[:, 0:SW] = ((dysg * mixed) * _gelu_grad(u, tu)).astype(BF16)
        dmix = dysg * ug

        rr = lax.broadcasted_iota(jnp.int32, (CH, CH), 0)
        cc = lax.broadcasted_iota(jnp.int32, (CH, CH), 1)
        tril = cc <= rr
        triu = cc >= rr
        for jj in range(4):
            wta = jnp.where(triu, wt_ref[2 * jj], 0.0).astype(BF16)
            wtb = jnp.where(triu, wt_ref[2 * jj + 1], 0.0).astype(BF16)
            for ch in range(tm // CH):
                rs, cs = slice(ch * CH, (ch + 1) * CH), slice(jj * 128, (jj + 1) * 128)
                dm2 = dmix[rs, cs]
                dma = jnp.where(is_a, dm2, 0.0)
                dmb = jnp.where(is_a, 0.0, dm2)
                dma_b, dmb_b = dma.astype(BF16), dmb.astype(BF16)
                vn2 = vnb_s[rs, cs]
                dw_ref[2 * jj] += jnp.where(tril, _nt(dma_b, vn2), 0.0)
                dw_ref[2 * jj + 1] += jnp.where(tril, _nt(dmb_b, vn2), 0.0)
                dvn_s[rs, cs] = _nn(wta, dma_b) + _nn(wtb, dmb_b)
                dba = jnp.sum(dma, axis=1, keepdims=True)
                dbb = jnp.sum(dmb, axis=1, keepdims=True)
                dbt_ref[...] += jnp.where(lane == 2 * jj, dba, 0.0) + jnp.where(lane == 2 * jj + 1, dbb, 0.0)

        dvn = dvn_s[...]
        dlng_ref[...] += jnp.sum(dvn * vhat, axis=0, keepdims=True)
        dlnb_ref[...] += jnp.sum(dvn, axis=0, keepdims=True)
        dvh = dvn * lng
        dvg = rstd * (dvh - jnp.mean(dvh, axis=-1, keepdims=True) - vhat * jnp.mean(dvh * vhat, axis=-1, keepdims=True))
        dus_ref[:, SW:NUS] = (dvg * _gelu_grad(vs, tv)).astype(BF16)

    row = lambda n: pl.BlockSpec((tm, n), lambda i: (i, 0))
    return pl.pallas_call(
        body,
        name="mix_bwd",
        grid=(T // tm,),
        in_specs=[row(D), row(D), row(NUS), row(AW), _full((1, SW)), _full((1, SW)), _full((8, CH, CH)), _full((8, CH, CH)),
                  _full((CH, 8)), _full((1, AW)), _full((1, SW)), _full((D, D)), _full((1, D))],
        out_specs=[row(D), row(AW), row(NUS), _full((8, CH, CH)), _full((CH, 128)), _full((1, SW)), _full((1, SW)),
                   _full((1, AW)), _full((1, SW)), _full((1, D))],
        out_shape=[jax.ShapeDtypeStruct((T, D), BF16), jax.ShapeDtypeStruct((T, AW), F32), jax.ShapeDtypeStruct((T, NUS), BF16),
                   jax.ShapeDtypeStruct((8, CH, CH), F32), jax.ShapeDtypeStruct((CH, 128), F32),
                   jax.ShapeDtypeStruct((1, SW), F32), jax.ShapeDtypeStruct((1, SW), F32),
                   jax.ShapeDtypeStruct((1, AW), F32), jax.ShapeDtypeStruct((1, SW), F32), jax.ShapeDtypeStruct((1, D), F32)],
        scratch_shapes=[pltpu.VMEM((tm, SW), F32), pltpu.VMEM((tm, SW), BF16), pltpu.VMEM((tm, SW), F32)],
        compiler_params=_params(1),
    )(dh1, o, us, yatt, lng, lnb, sgw, sgwt, sgbt, gatt, gsg, wout, gpm)


def _attn_prep(qkv, dya, yatt, lsec, after):
    T = dya.shape[0]
    tm = _tile(T, 512)

    def body(q_ref, do_ref, o_ref, l_ref, after_ref, qw_ref, dow_ref):
        do = do_ref[...]
        feat = lax.broadcasted_iota(jnp.int32, (AW, 128), 0)
        head = lax.broadcasted_iota(jnp.int32, (AW, 128), 1)
        sel = jnp.where((feat >= head * HD) & (feat < (head + 1) * HD), 1.0, 0.0)
        delta = jnp.dot(do * o_ref[...], sel, precision=lax.Precision.HIGHEST, preferred_element_type=F32)
        for hp in range(4):
            cols = slice(hp * 128, (hp + 1) * 128)
            for hh in range(2):
                base = (2 * hp + hh) * 256
                lc = l_ref[:, hp * 128 + hh:hp * 128 + hh + 1]
                d_h = delta[:, 2 * hp + hh:2 * hp + hh + 1]
                qw_ref[:, base:base + 128] = _head_rows(q_ref[:, cols], hh, 0.125)
                qw_ref[:, base + 128:base + 256] = _aug_lanes(tm, AUG_A if hh == 0 else AUG_B, _split3(-lc))
                dow_ref[:, base:base + 128] = _head_rows(do[:, cols], hh, 1.0).astype(BF16)
                dow_ref[:, base + 128:base + 256] = _aug_lanes(tm, None, _split3(-d_h))

    row = lambda n: pl.BlockSpec((tm, n), lambda i: (i, 0))
    return pl.pallas_call(
        body,
        name="attn_prep",
        grid=(T // tm,),
        in_specs=[row(AW)] * 4 + [_full(after.shape)],
        out_specs=[row(4 * AW)] * 2,
        out_shape=[jax.ShapeDtypeStruct((T, 4 * AW), BF16)] * 2,
        compiler_params=_params(1),
    )(qkv, dya, yatt, lsec, after)


def _attn_bwd(kk, vk, kt, qw, dow):
    T = kk.shape[0]
    tq = _tile(T, 512)
    tk = tq
    nq = T // tq

    def body(kk_ref, vk_ref, kt_ref, qw_ref, dow_ref, dqt_ref, dcq_ref, dk_ref, dv_ref, dck_ref, dk_s, dv_s, dck_s):
        j, i = _fold_cols(pl.program_id(1), pl.program_id(2), nq)
        sub8 = lax.broadcasted_iota(jnp.int32, (8, 1), 0)
        lane = lax.broadcasted_iota(jnp.int32, (1, 128), 1)

        @pl.when((pl.program_id(1) == 0) & (pl.program_id(2) == 0))
        def _():
            dqt_ref[...] = jnp.zeros_like(dqt_ref)
            dcq_ref[...] = jnp.zeros_like(dcq_ref)

        @pl.when(i == j)
        def _():
            dk_s[...] = jnp.zeros_like(dk_s)
            dv_s[...] = jnp.zeros_like(dv_s)
            dck_s[...] = jnp.zeros_like(dck_s)

        def step(masked):
            cols = pl.ds(pl.multiple_of(i * tq, tq), tq)
            sub = lax.broadcasted_iota(jnp.int32, (128, 1), 0)
            if masked:
                causal = lax.broadcasted_iota(jnp.int32, (tk, tq), 0) <= lax.broadcasted_iota(jnp.int32, (tk, tq), 1)

            def logits(n):
                pair, base = n // 2, n * 256
                return (_nt(kk_ref[:, pair * 256:(pair + 1) * 256], qw_ref[:, base:base + 256]),
                        _nt(vk_ref[:, pair * 256:(pair + 1) * 256], dow_ref[:, base:base + 256]))

            ahead = logits(0)
            for pp in range(PAIRS):
                lanes = slice(pp * 128, (pp + 1) * 128)
                kt2 = kt_ref[lanes, :] * 0.125
                dcq = jnp.zeros((8, tq), F32)
                dck = jnp.zeros((tk, 128), F32)
                dv = jnp.zeros((tk, 128), F32)
                dk = jnp.zeros((tk, 128), F32)
                dqts = []
                for hh in range(2):
                    base = (2 * pp + hh) * 256
                    qw_h = qw_ref[:, base:base + 256]
                    dow_h = dow_ref[:, base:base + 256]
                    logp, dp = ahead
                    if 2 * pp + hh + 1 < 2 * PAIRS:
                        ahead = logits(2 * pp + hh + 1)
                    pr = jnp.exp(logp)
                    if masked:
                        pr = jnp.where(causal, pr, 0.0)
                    ds = pr * dp
                    ds_b = ds.astype(BF16)
                    dv = dv + _nn(pr.astype(BF16), dow_h[:, 0:128])
                    dk = dk + _nn(ds_b, qw_h[:, 0:128])
                    dqts.append(_nn(kt2, ds_b))
                    dcq = dcq + jnp.where(sub8 == hh, jnp.sum(ds, axis=0, keepdims=True), 0.0)
                    dck = dck - jnp.where(lane == hh, jnp.sum(ds, axis=1, keepdims=True), 0.0)
                dv_s[:, lanes] += dv
                dk_s[:, lanes] += dk
                dck_s[:, lanes] += dck
                dcq_ref[pp, :, cols] += dcq
                dqt_ref[lanes, cols] += jnp.where(sub < HD, dqts[0], dqts[1])

        @pl.when(i > j)
        def _():
            step(False)

        @pl.when(i == j)
        def _():
            step(True)

        @pl.when(i == nq - 1)
        def _():
            dk_ref[...] = dk_s[...].astype(BF16)
            dv_ref[...] = dv_s[...].astype(BF16)
            dck_ref[...] = dck_s[...]

    kj = lambda r, t: _fold_cols(r, t, nq)[0]
    qi = lambda r, t: _fold_cols(r, t, nq)[1]
    krow = lambda g, r, t: (kj(r, t), g)
    qrow = lambda g, r, t: (qi(r, t), g)
    return pl.pallas_call(
        body,
        name="attn_bwd",
        grid=_fold_grid(nq),
        in_specs=[
            pl.BlockSpec((tk, PAIRS * 256), krow),
            pl.BlockSpec((tk, PAIRS * 256), krow),
            pl.BlockSpec((PAIRS * 128, tk), lambda g, r, t: (g, kj(r, t))),
            pl.BlockSpec((tq, PAIRS * 512), qrow),
            pl.BlockSpec((tq, PAIRS * 512), qrow),
        ],
        out_specs=[
            pl.BlockSpec((PAIRS * 128, T), lambda g, r, t: (g, 0), pipeline_mode=pl.Buffered(1)),
            pl.BlockSpec((PAIRS, 8, T), lambda g, r, t: (g, 0, 0), pipeline_mode=pl.Buffered(1)),
            pl.BlockSpec((tk, PAIRS * 128), krow),
            pl.BlockSpec((tk, PAIRS * 128), krow),
            pl.BlockSpec((tk, PAIRS * 128), krow),
        ],
        out_shape=[jax.ShapeDtypeStruct((AW, T), F32), jax.ShapeDtypeStruct((4, 8, T), F32),
                   jax.ShapeDtypeStruct((T, AW), BF16), jax.ShapeDtypeStruct((T, AW), BF16),
                   jax.ShapeDtypeStruct((T, AW), F32)],
        scratch_shapes=[pltpu.VMEM((tk, PAIRS * 128), F32), pltpu.VMEM((tk, PAIRS * 128), F32),
                        pltpu.VMEM((tk, PAIRS * 128), F32)],
        compiler_params=_params(3),
    )(kk, vk, kt, qw, dow)


def _fgate_bwd(dcq, dck, fz, fb):
    T = dck.shape[0]
    tb = _tile(T, 512)
    nb = T // tb

    def body(dcq_ref, dck_ref, fz_ref, fb_ref, df_ref, dfb_ref, carry):
        @pl.when(pl.program_id(0) == 0)
        def _():
            carry[...] = jnp.zeros_like(carry)
            dfb_ref[...] = jnp.zeros_like(dfb_ref)

        dcv = dcq_ref[...] + dck_ref[...]
        r = lax.broadcasted_iota(jnp.int32, (tb, tb), 0)
        cc = lax.broadcasted_iota(jnp.int32, (tb, tb), 1)
        tri = (cc >= r).astype(F32)
        dlf = jnp.dot(tri, dcv, precision=lax.Precision.HIGHEST, preferred_element_type=F32) + carry[...]
        carry[...] = carry[...] + jnp.sum(dcv, axis=0, keepdims=True)
        lane = lax.broadcasted_iota(jnp.int32, (tb, FPAD), 1)
        df = jnp.where(lane < 8, dlf * jax.nn.sigmoid(-(fz_ref[...] + fb_ref[...])), 0.0)
        df_ref[...] = df.astype(BF16)
        dfb_ref[...] += jnp.sum(df, axis=0, keepdims=True)

    rev = pl.BlockSpec((tb, FPAD), lambda i: (nb - 1 - i, 0))
    return pl.pallas_call(
        body,
        name="fgate_bwd",
        grid=(nb,),
        in_specs=[rev, rev, rev, _full((1, FPAD))],
        out_specs=[rev, _full((1, FPAD))],
        out_shape=[jax.ShapeDtypeStruct((T, FPAD), BF16), jax.ShapeDtypeStruct((1, FPAD), F32)],
        scratch_shapes=[pltpu.VMEM((1, FPAD), F32)],
        compiler_params=_params(1),
    )(dcq, dck, fz, fb)


def _inproj_bwd(dqt, dk, dv, dus, dfz, wp, wqt, x, dh1, g):
    T = x.shape[0]
    tm = _tile(T, 512)

    def body(dq_ref, dk_ref, dv_ref, dus_ref, dfz_ref, w_ref, wqt_ref, x_ref, dh1_ref, g_ref, gx_ref, dg_ref):
        @pl.when(pl.program_id(0) == 0)
        def _():
            dg_ref[...] = jnp.zeros_like(dg_ref)

        da = _tn(dq_ref[...].astype(BF16), wqt_ref[...])
        da += _nt(dk_ref[...], w_ref[:, AW:2 * AW])
        da += _nt(dv_ref[...], w_ref[:, 2 * AW:NQKV])
        da += _nt(dus_ref[...], w_ref[:, NQKV:NQKV + NUS])
        da += _nt(dfz_ref[...], w_ref[:, NQKV + NUS:ZP])
        dx, dg = _rms_bwd(da, x_ref[...], g_ref[...])
        gx_ref[...] = dh1_ref[...] + dx
        dg_ref[...] += dg

    row = lambda n: pl.BlockSpec((tm, n), lambda i: (i, 0))
    return pl.pallas_call(
        body,
        name="inproj_bwd",
        grid=(T // tm,),
        in_specs=[pl.BlockSpec((AW, tm), lambda i: (0, i)), row(AW), row(AW), row(NUS), row(FPAD), _full((D, ZP)),
                  _full((AW, D)), row(D), row(D), _full((1, D))],
        out_specs=[row(D), _full((1, D))],
        out_shape=[jax.ShapeDtypeStruct((T, D), F32), jax.ShapeDtypeStruct((1, D), F32)],
        compiler_params=_params(1),
    )(dqt, dk, dv, dus, dfz, wp, wqt, x, dh1, g)


def _sq_relu(f1):
    r = jnp.maximum(f1.astype(F32), 0.0)
    return (r * r).astype(BF16)


def _wgrad(a, bs, name, a_fn=None, out_dtype=F32):
    T, K = a.shape
    tt = _tile(T, 512)
    nb = len(bs)
    narrow = out_dtype != F32

    def out_dims(b, layout):
        if layout == "t":
            return (K, b.shape[0])
        N = b.shape[1]
        if layout is None:
            return (K, N)
        return (N // layout[1], K, layout[1]) if layout[0] == "col" else (K // layout[1], layout[1], N)

    shapes = [out_dims(b, layout) for b, layout in bs]

    def body(*refs):
        a_ref, b_refs, o_refs = refs[0], refs[1:1 + nb], refs[1 + nb:1 + 2 * nb]
        accs = refs[1 + 2 * nb:] if narrow else o_refs

        @pl.when(pl.program_id(0) == 0)
        def _():
            for acc in accs:
                acc[...] = jnp.zeros_like(acc)

        av = a_ref[...] if a_fn is None else a_fn(a_ref[...])
        at = av.astype(BF16).T
        for (b, layout), b_ref, o_ref in zip(bs, b_refs, accs):
            if layout is None:
                o_ref[...] += _nn(at, b_ref[...].astype(BF16))
            elif layout == "t":
                o_ref[...] += _nt(at, b_ref[...].astype(BF16))
            elif layout[0] == "col":
                n = layout[1]
                for k in range(b.shape[1] // n):
                    o_ref[k] += _nn(at, b_ref[:, k * n:(k + 1) * n].astype(BF16))
            else:
                n = layout[1]
                bv = b_ref[...].astype(BF16)
                for k in range(K // n):
                    o_ref[k] += _nn(at[k * n:(k + 1) * n, :], bv)

        if narrow:
            @pl.when(pl.program_id(0) == T // tt - 1)
            def _():
                for o_ref, acc in zip(o_refs, accs):
                    o_ref[...] = acc[...].astype(out_dtype)

    once = lambda shape: pl.BlockSpec(shape, lambda t: (0,) * len(shape), pipeline_mode=pl.Buffered(1))
    res = pl.pallas_call(
        body,
        name=name,
        grid=(T // tt,),
        in_specs=[pl.BlockSpec((tt, K), lambda t: (t, 0))] + [
            pl.BlockSpec((b.shape[0], tt), lambda t: (0, t)) if layout == "t" else pl.BlockSpec((tt, b.shape[1]), lambda t: (t, 0))
            for b, layout in bs],
        out_specs=[once(s) for s in shapes],
        out_shape=[jax.ShapeDtypeStruct(s, out_dtype) for s in shapes],
        scratch_shapes=[pltpu.VMEM(s, F32) for s in shapes] if narrow else [],
        compiler_params=_params(1, VMEM_LIMIT_BIG),
    )(a, *[b for b, _ in bs])
    return res


def _adam_math(w, g, m, v):
    m = ADAM_B1 * m + (1.0 - ADAM_B1) * g
    v = ADAM_B2 * v + (1.0 - ADAM_B2) * (g * g)
    m_hat = m / (1.0 - ADAM_B1 ** ADAM_STEP)
    v_hat = v / (1.0 - ADAM_B2 ** ADAM_STEP)
    delta = -ADAM_LR * (m_hat / (jnp.sqrt(v_hat) + ADAM_EPS) + ADAM_WD * w)
    return delta, m, v


def _adam(parts, w, m, v, name):
    R, C = w.shape
    br = 128 if R % 128 == 0 else R

    def body(p_ref, w_ref, m_ref, v_ref, g_ref, d_ref, nm_ref, nv_ref):
        g = p_ref[0].astype(F32)
        for s in range(1, NDEV):
            g = g + p_ref[s].astype(F32)
        g_ref[...] = g
        d_ref[...], nm_ref[...], nv_ref[...] = _adam_math(w_ref[...], g, m_ref[...], v_ref[...])

    blk = pl.BlockSpec((br, C), lambda i: (i, 0))
    return pl.pallas_call(
        body,
        name=name,
        grid=(R // br,),
        in_specs=[pl.BlockSpec((NDEV, br, C), lambda i: (0, i, 0)), blk, blk, blk],
        out_specs=[blk] * 4,
        out_shape=[jax.ShapeDtypeStruct((R, C), F32)] * 4,
        compiler_params=_params(1),
    )(parts, w, m, v)


_SMALL = (("sg_w", 8 * CH * CH), ("f_bias", 8), ("sg_ln_g", SW), ("sg_ln_b", SW), ("sg_b", 8 * CH), ("att_out_g", AW),
          ("sg_out_g", SW), ("pre_mix_g", D), ("post_mix_g", D), ("pre_ffn_g", D), ("post_ffn_g", D), ("ple_gate_b", D))
_SEG = 8 * 128


def _seg_rows(size):
    return 8 * (-(-size // _SEG))


def _pack(vals, loss_acc):
    parts = []
    for name, size in _SMALL:
        flat = vals[name].reshape(-1)
        rows = _seg_rows(size)
        parts.append(jnp.pad(flat, (0, rows * 128 - size)).reshape(rows, 128))
    parts.append(loss_acc)
    return jnp.concatenate(parts, axis=0)


def _adam_small(parts, ws, ms, vs):
    n = len(_SMALL)
    names = [name for name, _ in _SMALL]

    def body(*refs):
        p_ref, w_refs, m_refs, v_refs = refs[0], refs[1:1 + n], refs[1 + n:1 + 2 * n], refs[1 + 2 * n:1 + 3 * n]
        loss_ref, outs = refs[1 + 3 * n], refs[2 + 3 * n:]
        g_all = p_ref[0]
        for s in range(1, NDEV):
            g_all = g_all + p_ref[s]
        r = 0
        for k, (name, size) in enumerate(_SMALL):
            dst = [outs[kind * n + k] for kind in range(4)]

            def update(g, idx):
                vals = (g,) + _adam_math(w_refs[k][idx], g, m_refs[k][idx], v_refs[k][idx])
                for d, val in zip(dst, vals):
                    d[idx] = val

            if name == "sg_w":
                for grp in range(8):
                    update(g_all[r + grp * CH:r + (grp + 1) * CH, :], (0, grp))
            elif name == "sg_b":
                update(g_all[r:r + 8, :], (0,))
            elif name == "f_bias":
                update(g_all[r:r + 1, 0:8], (slice(None),))
            else:
                update(jnp.concatenate([g_all[r + q:r + q + 1, :] for q in range(size // 128)], axis=1), (slice(None),))
            r += _seg_rows(size)
        loss_ref[...] = g_all[r:r + 1, 0:1] * (0.5 / D)

    arrs = [parts] + [d[name] for d in (ws, ms, vs) for name in names]
    res = pl.pallas_call(
        body,
        name="adam_small",
        in_specs=[_full(a.shape) for a in arrs],
        out_specs=[_full((1, 1))] + [_full(ws[name].shape) for _ in range(4) for name in names],
        out_shape=[jax.ShapeDtypeStruct((1, 1), F32)] + [jax.ShapeDtypeStruct(ws[name].shape, F32) for _ in range(4) for name in names],
        compiler_params=pltpu.CompilerParams(vmem_limit_bytes=VMEM_LIMIT),
    )(*arrs)
    return res[0], {name: [res[1 + kind * n + k] for kind in range(4)] for k, name in enumerate(names)}


def kernel(x, p, w_in, f_bias, sg_ln_g, sg_ln_b, sg_w, sg_b, att_out_g, sg_out_g, w_out, pre_mix_g, post_mix_g, pre_ffn_g, post_ffn_g, w_ff1, w_ff2, ple_w, ple_gate_w, ple_gate_b, loss_target, m_w_in, m_f_bias, m_sg_ln_g, m_sg_ln_b, m_sg_w, m_sg_b, m_att_out_g, m_sg_out_g, m_w_out, m_pre_mix_g, m_post_mix_g, m_pre_ffn_g, m_post_ffn_g, m_w_ff1, m_w_ff2, m_ple_w, m_ple_gate_w, m_ple_gate_b, v_w_in, v_f_bias, v_sg_ln_g, v_sg_ln_b, v_sg_w, v_sg_b, v_att_out_g, v_sg_out_g, v_w_out, v_pre_mix_g, v_post_mix_g, v_pre_ffn_g, v_post_ffn_g, v_w_ff1, v_w_ff2, v_ple_w, v_ple_gate_w, v_ple_gate_b):
    small_w = dict(sg_w=sg_w, f_bias=f_bias, sg_ln_g=sg_ln_g, sg_ln_b=sg_ln_b, sg_b=sg_b, att_out_g=att_out_g,
                   sg_out_g=sg_out_g, pre_mix_g=pre_mix_g, post_mix_g=post_mix_g, pre_ffn_g=pre_ffn_g,
                   post_ffn_g=post_ffn_g, ple_gate_b=ple_gate_b)
    small_m = dict(sg_w=m_sg_w, f_bias=m_f_bias, sg_ln_g=m_sg_ln_g, sg_ln_b=m_sg_ln_b, sg_b=m_sg_b, att_out_g=m_att_out_g,
                   sg_out_g=m_sg_out_g, pre_mix_g=m_pre_mix_g, post_mix_g=m_post_mix_g, pre_ffn_g=m_pre_ffn_g,
                   post_ffn_g=m_post_ffn_g, ple_gate_b=m_ple_gate_b)
    small_v = dict(sg_w=v_sg_w, f_bias=v_f_bias, sg_ln_g=v_sg_ln_g, sg_ln_b=v_sg_ln_b, sg_b=v_sg_b, att_out_g=v_att_out_g,
                   sg_out_g=v_sg_out_g, pre_mix_g=v_pre_mix_g, post_mix_g=v_post_mix_g, pre_ffn_g=v_pre_ffn_g,
                   post_ffn_g=v_post_ffn_g, ple_gate_b=v_ple_gate_b)
    big = dict(w_in=(w_in, m_w_in, v_w_in), w_out=(w_out, m_w_out, v_w_out), w_ff1=(w_ff1, m_w_ff1, v_w_ff1),
               w_ff2=(w_ff2, m_w_ff2, v_w_ff2), ple_w=(ple_w, m_ple_w, v_ple_w),
               ple_gate_w=(ple_gate_w, m_ple_gate_w, v_ple_gate_w))

    xt, pt, tgt = x[0], p[0, 0], loss_target[0]
    ws = W_IN_COLS // NDEV

    gw_in = _gather(w_in[0].astype(BF16), "gather_w_in")
    rest = _xchg_start([w_out[0].astype(BF16), w_ff1[0].astype(BF16), w_ff2[0].astype(BF16), ple_w[0].astype(BF16),
                        ple_gate_w[0].astype(BF16)], True, "gather_rest_start")
    win = jnp.transpose(gw_in, (1, 0, 2)).reshape(D, W_IN_COLS)
    wp = jnp.concatenate([win[:, 0:NQKV], win[:, NQKV + 8:W_IN_COLS], win[:, NQKV:NQKV + 8],
                          jnp.zeros((D, FPAD - 8), BF16)], axis=1)

    fb = jnp.pad(f_bias.astype(F32), ((0, 0), (0, FPAD - 8)))
    sgw = sg_w[0]
    sgwt = jnp.transpose(sg_w[0], (0, 2, 1))
    sgbt = jnp.transpose(sg_b[0])

    qkv, kt, vt, us, fz, ab = _inproj(xt, pre_mix_g + rest[4][0, 0], wp)
    ct, kk, vk = _fcum(fz, fb, qkv)
    yatt, lsec = _attn_fwd(qkv, kk, vt, ct)
    gw_out, gw1, gw2, gwpe, gwg = _xchg_wait(rest, True, yatt, "gather_rest_wait")
    wout = gw_out.reshape(D, D)
    wg = gwg.reshape(D, D)
    wpe = jnp.transpose(gwpe, (1, 0, 2)).reshape(PLE, D)
    h1, yb, o = _sgu_out(us, yatt, xt, sg_ln_g, sg_ln_b, sgw, sgbt, att_out_g, sg_out_g, wout, post_mix_g)
    c2b, f1b, ff, h2 = _ffn_fwd(h1, pre_ffn_g, gw1, gw2, post_ffn_g)
    dh2, dbg, loss_acc, g_g, g_pe = _ple_loss(h2, pt, tgt, wg, ple_gate_b, wpe)
    g_g = g_g.reshape(NDEV, D // NDEV, D)
    g_pe = jnp.transpose(g_pe.reshape(PLE, NDEV, D // NDEV), (1, 0, 2))

    dffb, df1, dh1, dgpostffn, dgpreffn = _ffn_bwd(dh2, ff, h1, f1b, gw1, gw2, post_ffn_g, pre_ffn_g)
    (g_1,) = _wgrad(c2b, [(df1, ("col", DFF // NDEV))], "wgrad_ff1")
    (g_2,) = _wgrad(f1b, [(dffb, ("row", DFF // NDEV))], "wgrad_ff2", a_fn=_sq_relu)
    early = _xchg_start([g_1, g_2, g_pe, g_g], False, "scatter_early_start")
    dob, dya, dus, dsgw, dsgbt, dlng, dlnb, dgatt, dgsg, dgpostmix = _mix_bwd(
        dh1, o, us, yatt, sg_ln_g, sg_ln_b, sgw, sgwt, sgbt, att_out_g, sg_out_g, wout, post_mix_g + early[4][0, 0])
    (g_out,) = _wgrad(yb, [(dob, ("row", D // NDEV))], "wgrad_out")
    mid = _xchg_start([g_out], False, "scatter_mid_start")
    qw, dow = _attn_prep(qkv, dya, yatt, lsec, mid[4])
    dqt, dcq, dk, dv, dck = _attn_bwd(kk, vk, kt, qw, dow)
    dcq = jnp.pad(jnp.transpose(dcq[:, 0:2, :].reshape(8, -1)), ((0, 0), (0, FPAD - 8)))
    dck = jnp.pad(dck.reshape(-1, 4, 128)[:, :, 0:2].reshape(-1, 8), ((0, 0), (0, FPAD - 8)))
    dfz, dfb = _fgate_bwd(dcq, dck, fz, fb)

    gq, gk, gv, gus, gf = _wgrad(ab, [(dqt, "t"), (dk, None), (dv, None), (dus, None), (dfz, None)], "wgrad_in",
                                 out_dtype=BF16)
    g_in = jnp.concatenate([gq, gk, gv, gf[:, 0:8], gus], axis=1)
    g_in = jnp.transpose(g_in.reshape(D, NDEV, ws), (1, 0, 2))
    late = _xchg_start([g_in], False, "scatter_late_start")
    grad_x, dgpremix = _inproj_bwd(dqt, dk, dv, dus, dfz, wp, jnp.transpose(wp[:, 0:AW]), xt, dh1,
                                   pre_mix_g + late[4][0, 0])

    small_g = dict(sg_w=dsgw, f_bias=dfb[:, 0:8], sg_ln_g=dlng, sg_ln_b=dlnb, sg_b=jnp.transpose(dsgbt[:, 0:8]),
                   att_out_g=dgatt, sg_out_g=dgsg, pre_mix_g=dgpremix, post_mix_g=dgpostmix, pre_ffn_g=dgpreffn,
                   post_ffn_g=dgpostffn, ple_gate_b=dbg)

    r_1, r_2, r_pe, r_g = _xchg_wait(early, False, grad_x, "scatter_early_wait")
    (r_out,) = _xchg_wait(mid, False, grad_x, "scatter_mid_wait")
    r_small = _gather(_pack(small_g, loss_acc), "gather_small_grads")

    res = {}

    def adam_big(name, parts):
        w, m, v = big[name]
        res[name] = [t[None] for t in _adam(parts, w[0], m[0], v[0], "adam_" + name)]

    for name, parts in (("w_out", r_out), ("w_ff1", r_1), ("w_ff2", r_2), ("ple_w", r_pe), ("ple_gate_w", r_g)):
        adam_big(name, parts)
    (r_in,) = _xchg_wait(late, False, res["w_ff1"][0], "scatter_late_wait")
    adam_big("w_in", r_in)
    loss, small = _adam_small(r_small, small_w, small_m, small_v)
    res.update(small)

    order = ["w_in", "f_bias", "sg_ln_g", "sg_ln_b", "sg_w", "sg_b", "att_out_g", "sg_out_g", "w_out", "pre_mix_g",
             "post_mix_g", "pre_ffn_g", "post_ffn_g", "w_ff1", "w_ff2", "ple_w", "ple_gate_w", "ple_gate_b"]
    outs = [loss[0, 0], grad_x[None]]
    for kind in range(4):
        outs += [res[name][kind] for name in order]
    return tuple(outs)
```

```python
import jax
import jax.numpy as jnp
from jax import lax
from jax.experimental import pallas as pl
from jax.experimental.pallas import tpu as pltpu

F32 = jnp.float32
BF16 = jnp.bfloat16

NDEV = 8
D = 1024
AW = 512
SW = 512
HD = 64
CH = 128
DFF = 4096
PLE = 256
NQKV = 3 * AW
NUS = 2 * SW
FPAD = 128
ZP = NQKV + NUS + FPAD
W_IN_COLS = 2568
EPS = 1e-6
MASKV = -1e30
GELU_K = 0.7978845608028654
GELU_C = 0.044715

ADAM_LR = 0.001
ADAM_B1 = 0.9
ADAM_B2 = 0.999
ADAM_EPS = 1e-08
ADAM_WD = 0.01
ADAM_STEP = 10

VMEM_LIMIT = 48 * 1024 * 1024
VMEM_LIMIT_BIG = 60 * 1024 * 1024


def _nn(a, b):
    return jnp.dot(a, b, preferred_element_type=F32)


def _nt(a, b):
    return lax.dot_general(a, b, (((1,), (1,)), ((), ())), preferred_element_type=F32)


def _tn(a, b):
    return lax.dot_general(a, b, (((0,), (0,)), ((), ())), preferred_element_type=F32)


def _tile(n, pref):
    return min(n, pref)


def _params(n_axes, vmem=VMEM_LIMIT):
    return pltpu.CompilerParams(dimension_semantics=("arbitrary",) * n_axes, vmem_limit_bytes=vmem)


def _full(shape):
    nd = len(shape)
    return pl.BlockSpec(shape, lambda *_: (0,) * nd)


def _rms_fwd(x, g):
    r = lax.rsqrt(jnp.mean(x * x, axis=-1, keepdims=True) + EPS)
    return x * r * g


def _rms_bwd(dy, x, g):
    n = x.shape[-1]
    r = lax.rsqrt(jnp.mean(x * x, axis=-1, keepdims=True) + EPS)
    u = dy * g
    s = jnp.sum(x * u, axis=-1, keepdims=True)
    dx = r * u - x * (r * r * r * (s * (1.0 / n)))
    dg = jnp.sum(dy * (x * r), axis=0, keepdims=True)
    return dx, dg


def _gelu(x):
    t = jnp.tanh(x * (GELU_K + (GELU_K * GELU_C) * (x * x)))
    return x * (0.5 + 0.5 * t), t


def _gelu_grad(x, t):
    return (0.5 + 0.5 * t) + (0.5 * x) * (1.0 - t * t) * (GELU_K + (3.0 * GELU_K * GELU_C) * (x * x))


def _gather(arr, name):
    def body(x_ref, out_ref, send, recv, loc):
        x, y, c = lax.axis_index("x"), lax.axis_index("y"), lax.axis_index("c")
        me, sibling = (x, y, c), (x, y, 1 - c)
        chips = [(1 - x, y), (x, 1 - y), (1 - x, 1 - y)]

        def slot(px, py, pc):
            return out_ref.at[4 * px + 2 * py + pc]

        def copy(k, block, to, src=None):
            return pltpu.make_async_remote_copy(
                src_ref=slot(*block) if src is None else src, dst_ref=slot(*block), send_sem=send.at[k],
                recv_sem=recv.at[k], device_id=to, device_id_type=pl.DeviceIdType.MESH)

        mine = pltpu.make_async_copy(x_ref, slot(*me), loc)
        mine.start()
        first = [copy(0, me, sibling, src=x_ref)] + [copy(1 + j, me, (*chip, c), src=x_ref) for j, chip in enumerate(chips)]
        for cp in first:
            cp.start()
        passed = [copy(4 + j, (*chip, c), sibling) for j, chip in enumerate(chips)]
        for j, chip in enumerate(chips):
            copy(1 + j, (*chip, c), me).wait_recv()
            passed[j].start()
        copy(0, sibling, me).wait_recv()
        for j, chip in enumerate(chips):
            copy(4 + j, (*chip, 1 - c), me).wait_recv()
        for cp in first + passed:
            cp.wait_send()
        mine.wait()

    hbm = pl.BlockSpec(memory_space=pltpu.HBM)
    return pl.pallas_call(
        body,
        name=name,
        out_shape=jax.ShapeDtypeStruct((NDEV,) + arr.shape, arr.dtype),
        in_specs=[hbm],
        out_specs=hbm,
        scratch_shapes=[pltpu.SemaphoreType.DMA((NDEV - 1,)), pltpu.SemaphoreType.DMA((NDEV - 1,)), pltpu.SemaphoreType.DMA],
    )(arr)


def _peers(x, y, c):
    out = []
    for k in range(1, NDEV):
        out.append((1 - x if (k >> 2) & 1 else x, 1 - y if (k >> 1) & 1 else y, 1 - c if k & 1 else c))
    return out


def _xchg_start(arrs, gather, name):
    n = len(arrs)
    me = 4 * lax.axis_index("x") + 2 * lax.axis_index("y") + lax.axis_index("c")
    lands = []
    for a in arrs:
        shape = ((NDEV,) + a.shape) if gather else a.shape
        own = a[None] if gather else lax.dynamic_slice_in_dim(a, me, 1, axis=0)
        lands.append(lax.dynamic_update_slice_in_dim(lax.empty(shape, a.dtype), own, me, axis=0))

    def body(*refs):
        ins, lnd = refs[:n], refs[n:2 * n]
        send, recv, token = refs[2 * n:3 * n], refs[3 * n:4 * n], refs[-1]
        x, y, c = lax.axis_index("x"), lax.axis_index("y"), lax.axis_index("c")
        mine = 4 * x + 2 * y + c
        for px, py, pc in _peers(x, y, c):
            peer = 4 * px + 2 * py + pc
            for a in range(n):
                pltpu.make_async_remote_copy(
                    src_ref=ins[a] if gather else ins[a].at[peer],
                    dst_ref=lnd[a].at[mine],
                    send_sem=send[a],
                    recv_sem=recv[a],
                    device_id=(px, py, pc),
                    device_id_type=pl.DeviceIdType.MESH,
                ).start()
        token[...] = jnp.zeros_like(token)

    hbm = pl.BlockSpec(memory_space=pltpu.HBM)
    sem = pl.BlockSpec(memory_space=pltpu.SEMAPHORE)
    res = pl.pallas_call(
        body,
        name=name,
        out_shape=(*[pltpu.SemaphoreType.DMA(())] * (2 * n),
                   *[pltpu.HBM(a.shape, a.dtype) for a in arrs], *[pltpu.HBM(l.shape, l.dtype) for l in lands],
                   jax.ShapeDtypeStruct((8, 128), F32)),
        in_specs=[hbm] * (2 * n),
        out_specs=(*([sem] * (2 * n)), *([hbm] * (2 * n)), pl.BlockSpec(memory_space=pltpu.VMEM)),
        input_output_aliases={i: 2 * n + i for i in range(2 * n)},
        compiler_params=pltpu.CompilerParams(has_side_effects=pltpu.SideEffectType.DATAFLOW_SIDE_EFFECTING),
    )(*[pltpu.with_memory_space_constraint(a, pltpu.HBM) for a in arrs],
      *[pltpu.with_memory_space_constraint(l, pltpu.HBM) for l in lands])
    return list(res[0:n]), list(res[n:2 * n]), list(res[2 * n:3 * n]), list(res[3 * n:4 * n]), res[-1]


def _xchg_wait(started, gather, after, name):
    send, recv, srcs, lands, _ = started
    n = len(srcs)

    def body(*refs):
        lnd = refs[n:2 * n]
        send, recv = refs[2 * n:3 * n], refs[3 * n:4 * n]
        me = (lax.axis_index("x"), lax.axis_index("y"), lax.axis_index("c"))
        for a in range(n):
            seven = lnd[a].at[pl.ds(0, NDEV - 1)]
            cp = pltpu.make_async_remote_copy(src_ref=seven, dst_ref=seven, send_sem=send[a], recv_sem=recv[a],
                                              device_id=me, device_id_type=pl.DeviceIdType.MESH)
            cp.wait_send()
            cp.wait_recv()

    hbm = pl.BlockSpec(memory_space=pltpu.HBM)
    sem = pl.BlockSpec(memory_space=pltpu.SEMAPHORE)
    res = pl.pallas_call(
        body,
        name=name,
        out_shape=tuple([pltpu.HBM(a.shape, a.dtype) for a in srcs] + [pltpu.HBM(l.shape, l.dtype) for l in lands]),
        in_specs=[hbm] * (2 * n) + [sem] * (2 * n) + [pl.BlockSpec(memory_space=pl.ANY)],
        out_specs=tuple([hbm] * (2 * n)),
        input_output_aliases={i: i for i in range(2 * n)},
        compiler_params=pltpu.CompilerParams(has_side_effects=pltpu.SideEffectType.DATAFLOW_SIDE_EFFECTING),
    )(*srcs, *lands, *send, *recv, after)
    return list(res[n:])


def _inproj(x, g, wp):
    T = x.shape[0]
    tm = _tile(T, 512)

    def body(x_ref, g_ref, w_ref, qkv_ref, kt_ref, vt_ref, us_ref, fz_ref, ab_ref):
        a = _rms_fwd(x_ref[...], g_ref[...]).astype(BF16)
        ab_ref[...] = a
        qkv_ref[:, 0:AW] = _nn(a, w_ref[:, 0:AW]).astype(BF16)
        kk = _nn(a, w_ref[:, AW:2 * AW])
        qkv_ref[:, AW:2 * AW] = kk.astype(BF16)
        kt_ref[...] = kk.T.astype(BF16)
        vv = _nn(a, w_ref[:, 2 * AW:NQKV])
        qkv_ref[:, 2 * AW:NQKV] = vv.astype(BF16)
        vt_ref[...] = vv.T.astype(BF16)
        us_ref[...] = _nn(a, w_ref[:, NQKV:NQKV + NUS])
        fz_ref[...] = _nn(a, w_ref[:, NQKV + NUS:ZP])

    row = lambda n: pl.BlockSpec((tm, n), lambda i: (i, 0))
    col = pl.BlockSpec((AW, tm), lambda i: (0, i))
    return pl.pallas_call(
        body,
        name="inproj",
        grid=(T // tm,),
        in_specs=[row(D), _full((1, D)), _full((D, ZP))],
        out_specs=[row(NQKV), col, col, row(NUS), row(FPAD), row(D)],
        out_shape=[
            jax.ShapeDtypeStruct((T, NQKV), BF16),
            jax.ShapeDtypeStruct((AW, T), BF16),
            jax.ShapeDtypeStruct((AW, T), BF16),
            jax.ShapeDtypeStruct((T, NUS), F32),
            jax.ShapeDtypeStruct((T, FPAD), F32),
            jax.ShapeDtypeStruct((T, D), BF16),
        ],
        compiler_params=_params(1),
    )(x, g, wp)


def _log_sigmoid(z):
    return jnp.minimum(z, 0.0) - jnp.log1p(jnp.exp(-jnp.abs(z)))


def _split3(x):
    hi = x.astype(BF16)
    r1 = x - hi.astype(F32)
    mid = r1.astype(BF16)
    lo = (r1 - mid.astype(F32)).astype(BF16)
    return hi, mid, lo


AUG_A, AUG_B, AUG_ONE = 0, 3, 6


def _aug_lanes(rows, first_one, pieces):
    lane = lax.broadcasted_iota(jnp.int32, (rows, 128), 1)
    out = jnp.zeros((rows, 128), F32)
    if first_one is not None:
        out = jnp.where((lane >= first_one) & (lane < first_one + 3), 1.0, out)
    for n, piece in enumerate(pieces):
        out = jnp.where(lane == AUG_ONE + n, piece.astype(F32), out)
    return out.astype(BF16)


def _fcum(fz, fb, qkv):
    T = fz.shape[0]
    tb = _tile(T, 512)

    def body(fz_ref, fb_ref, k_ref, v_ref, ct_ref, kk_ref, vk_ref, carry):
        @pl.when(pl.program_id(0) == 0)
        def _():
            carry[...] = jnp.zeros_like(carry)

        lf = _log_sigmoid(fz_ref[...] + fb_ref[...])
        r = lax.broadcasted_iota(jnp.int32, (tb, tb), 0)
        cc = lax.broadcasted_iota(jnp.int32, (tb, tb), 1)
        tri = (cc <= r).astype(F32)
        cs = jnp.dot(tri, lf, precision=lax.Precision.HIGHEST, preferred_element_type=F32) + carry[...]
        ct_ref[...] = cs.T[0:8, :]
        carry[...] = carry[...] + jnp.sum(lf, axis=0, keepdims=True)

        pieces = _split3(cs)
        src = lax.broadcasted_iota(jnp.int32, (128, 128), 0)
        dst = lax.broadcasted_iota(jnp.int32, (128, 128), 1)
        ones = jnp.where((dst[0:1, :] >= AUG_ONE) & (dst[0:1, :] < AUG_ONE + 3), 1.0, 0.0)
        for hp in range(4):
            aug = jnp.zeros((tb, 128), F32) + ones
            for n, piece in enumerate(pieces):
                sel = jnp.where(((src == 2 * hp) & (dst == AUG_A + n)) | ((src == 2 * hp + 1) & (dst == AUG_B + n)), -1.0, 0.0)
                aug = aug + _nn(piece, sel.astype(BF16))
            aug = aug.astype(BF16)
            kk_ref[:, hp * 256:hp * 256 + 128] = k_ref[:, hp * 128:(hp + 1) * 128]
            kk_ref[:, hp * 256 + 128:(hp + 1) * 256] = aug
            vk_ref[:, hp * 256:hp * 256 + 128] = v_ref[:, hp * 128:(hp + 1) * 128]
            vk_ref[:, hp * 256 + 128:(hp + 1) * 256] = aug

    row = lambda n, c: pl.BlockSpec((tb, n), lambda i: (i, c))
    return pl.pallas_call(
        body,
        name="fcum",
        grid=(T // tb,),
        in_specs=[row(FPAD, 0), _full((1, FPAD)), row(AW, 1), row(AW, 2)],
        out_specs=[pl.BlockSpec((8, tb), lambda i: (0, i)), row(2 * AW, 0), row(2 * AW, 0)],
        out_shape=[jax.ShapeDtypeStruct((8, T), F32), jax.ShapeDtypeStruct((T, 2 * AW), BF16),
                   jax.ShapeDtypeStruct((T, 2 * AW), BF16)],
        scratch_shapes=[pltpu.VMEM((1, FPAD), F32)],
        compiler_params=_params(1),
    )(fz, fb, qkv, qkv)


def _fold_rows(r, t, nq):
    if nq == 1:
        return r, t
    low = t <= r
    return jnp.where(low, r, nq - 1 - r), jnp.where(low, t, t - r - 1)


def _fold_cols(r, t, nq):
    if nq == 1:
        return r, t
    first = t < nq - r
    j = jnp.where(first, r, nq - 1 - r)
    return j, jnp.where(first, r + t, nq - 1 - r + (t - (nq - r)))


PAIRS = 4


def _fold_grid(nq):
    assert (nq == 1 or nq % 2 == 0) and PAIRS == 4
    return (4 // PAIRS, 1, 1) if nq == 1 else (4 // PAIRS, nq // 2, nq + 1)


def _head_rows(x2, hh, scale):
    is_a = lax.broadcasted_iota(jnp.int32, (1, 128), 1) < HD
    keep = is_a if hh == 0 else jnp.logical_not(is_a)
    return jnp.where(keep, x2, jnp.zeros_like(x2)) * scale


def _attn_fwd(qkv, kk, vt, ct):
    T = qkv.shape[0]
    tq = _tile(T, 512)
    tk = tq
    nq = T // tq

    def body(q_ref, kk_ref, vt_ref, ctq_ref, o_ref, lsec_ref, qw_s, m_s, l_s, acc_s):
        i, j = _fold_rows(pl.program_id(1), pl.program_id(2), nq)
        sub8 = lax.broadcasted_iota(jnp.int32, (8, 1), 0)

        def cref_of(pp, hh):
            head = 2 * (PAIRS * pl.program_id(0) + pp) + hh
            return jnp.sum(jnp.where(sub8 == head, ctq_ref[:, 0:1], 0.0), axis=0, keepdims=True)

        @pl.when(j == 0)
        def _():
            for pp in range(PAIRS):
                q2 = q_ref[:, pp * 128:(pp + 1) * 128]
                for hh in range(2):
                    rows = slice(hh * tq, (hh + 1) * tq)
                    qw_s[pp, rows, 0:128] = _head_rows(q2, hh, 0.125)
                    qw_s[pp, rows, 128:256] = _aug_lanes(tq, AUG_A if hh == 0 else AUG_B, _split3(cref_of(pp, hh)))
            m_s[...] = jnp.full_like(m_s, MASKV)
            l_s[...] = jnp.zeros_like(l_s)
            acc_s[...] = jnp.zeros_like(acc_s)

        def step(masked):
            if masked:
                causal = lax.broadcasted_iota(jnp.int32, (tk, tq), 0) <= lax.broadcasted_iota(jnp.int32, (tk, tq), 1)
            logits = lambda pp: _nt(kk_ref[:, pp * 256:(pp + 1) * 256], qw_s[pp])
            s_next = logits(0)
            for pp in range(PAIRS):
                s2 = s_next
                if pp + 1 < PAIRS:
                    s_next = logits(pp + 1)
                vt2 = vt_ref[pp * 128:(pp + 1) * 128, :]
                for hh in range(2):
                    n = 2 * pp + hh
                    s = s2[:, hh * tq:(hh + 1) * tq]
                    if masked:
                        s = jnp.where(causal, s, MASKV)
                    m_prev = m_s[n]
                    m_new = jnp.maximum(m_prev, jnp.max(s, axis=0, keepdims=True))
                    pr = jnp.exp(s - m_new)
                    alpha = jnp.exp(m_prev - m_new)
                    l_s[n] = alpha * l_s[n] + jnp.sum(pr, axis=0, keepdims=True)
                    m_s[n] = m_new
                    acc_s[n] = alpha * acc_s[n] + _nn(vt2, pr.astype(BF16))

        @pl.when(j < i)
        def _():
            step(False)

        @pl.when(j == i)
        def _():
            step(True)
            sub = lax.broadcasted_iota(jnp.int32, (128, 1), 0)
            for pp in range(PAIRS):
                a, b = 2 * pp, 2 * pp + 1
                ot = jnp.where(sub < HD, acc_s[a] * (1.0 / l_s[a]), acc_s[b] * (1.0 / l_s[b]))
                o_ref[:, pp * 128:(pp + 1) * 128] = ot.T
                lrow = [m_s[2 * pp + hh] + jnp.log(l_s[2 * pp + hh]) - cref_of(pp, hh) for hh in range(2)]
                lsec_ref[:, pp * 128:(pp + 1) * 128] = jnp.where(sub == 0, lrow[0], jnp.where(sub == 1, lrow[1], 0.0)).T

    qi = lambda r, t: _fold_rows(r, t, nq)[0]
    kj = lambda r, t: _fold_rows(r, t, nq)[1]
    return pl.pallas_call(
        body,
        name="attn_fwd",
        grid=_fold_grid(nq),
        in_specs=[
            pl.BlockSpec((tq, PAIRS * 128), lambda g, r, t: (qi(r, t), g)),
            pl.BlockSpec((tk, PAIRS * 256), lambda g, r, t: (kj(r, t), g)),
            pl.BlockSpec((PAIRS * 128, tk), lambda g, r, t: (g, kj(r, t))),
            pl.BlockSpec((8, tq), lambda g, r, t: (0, qi(r, t))),
        ],
        out_specs=[pl.BlockSpec((tq, PAIRS * 128), lambda g, r, t: (qi(r, t), g))] * 2,
        out_shape=[jax.ShapeDtypeStruct((T, AW), F32)] * 2,
        scratch_shapes=[pltpu.VMEM((PAIRS, 2 * tq, 256), BF16), pltpu.VMEM((2 * PAIRS, 1, tq), F32),
                        pltpu.VMEM((2 * PAIRS, 1, tq), F32), pltpu.VMEM((2 * PAIRS, 128, tq), F32)],
        compiler_params=_params(3),
    )(qkv, kk, vt, ct)


def _sgu_forward(us_ref, lng, lnb, w_ref, bt_ref, mixed_s, vnb_s, tm):
    is_a = lax.broadcasted_iota(jnp.int32, (1, 128), 1) < HD
    u = us_ref[:, 0:SW]
    vs = us_ref[:, SW:NUS]
    ug, tu = _gelu(u)
    vg, tv = _gelu(vs)
    mu = jnp.mean(vg, axis=-1, keepdims=True)
    xc = vg - mu
    rstd = lax.rsqrt(jnp.mean(xc * xc, axis=-1, keepdims=True) + EPS)
    vhat = xc * rstd
    vnb_s[...] = (vhat * lng + lnb).astype(BF16)
    rr = lax.broadcasted_iota(jnp.int32, (CH, CH), 0)
    cc = lax.broadcasted_iota(jnp.int32, (CH, CH), 1)
    tril = cc <= rr
    for jj in range(4):
        wa = jnp.where(tril, w_ref[2 * jj], 0.0).astype(BF16)
        wb = jnp.where(tril, w_ref[2 * jj + 1], 0.0).astype(BF16)
        ba = bt_ref[:, 2 * jj:2 * jj + 1]
        bb = bt_ref[:, 2 * jj + 1:2 * jj + 2]
        for ch in range(tm // CH):
            rs, cs = slice(ch * CH, (ch + 1) * CH), slice(jj * 128, (jj + 1) * 128)
            vn2 = vnb_s[rs, cs]
            mixed_s[rs, cs] = jnp.where(is_a, _nn(wa, vn2) + ba, _nn(wb, vn2) + bb)
    mixed = mixed_s[...]
    return u, vs, ug, tu, tv, vhat, rstd, mixed, ug * mixed


def _sgu_out(us, yatt, x, lng, lnb, sgw, sgbt, gatt, gsg, wout, gpm):
    T = us.shape[0]
    tm = _tile(T, 512)

    def body(us_ref, ya_ref, x_ref, lng_ref, lnb_ref, w_ref, bt_ref, ga_ref, gs_ref, wo_ref, gp_ref,
             h1_ref, yb_ref, o_ref, mixed_s, vnb_s):
        ysg = _sgu_forward(us_ref, lng_ref[...], lnb_ref[...], w_ref, bt_ref, mixed_s, vnb_s, tm)[-1]
        yb_ref[:, 0:AW] = _rms_fwd(ya_ref[...], ga_ref[...]).astype(BF16)
        yb_ref[:, AW:D] = _rms_fwd(ysg, gs_ref[...]).astype(BF16)
        o = _nn(yb_ref[...], wo_ref[...])
        o_ref[...] = o
        h1_ref[...] = x_ref[...] + _rms_fwd(o, gp_ref[...])

    row = lambda n: pl.BlockSpec((tm, n), lambda i: (i, 0))
    return pl.pallas_call(
        body,
        name="sgu_out",
        grid=(T // tm,),
        in_specs=[row(NUS), row(AW), row(D), _full((1, SW)), _full((1, SW)), _full((8, CH, CH)), _full((CH, 8)),
                  _full((1, AW)), _full((1, SW)), _full((D, D)), _full((1, D))],
        out_specs=[row(D), row(D), row(D)],
        out_shape=[jax.ShapeDtypeStruct((T, D), F32), jax.ShapeDtypeStruct((T, D), BF16), jax.ShapeDtypeStruct((T, D), F32)],
        scratch_shapes=[pltpu.VMEM((tm, SW), F32), pltpu.VMEM((tm, SW), BF16)],
        compiler_params=_params(1),
    )(us, yatt, x, lng, lnb, sgw, sgbt, gatt, gsg, wout, gpm)


def _ffn_fwd(h1, gpre, w1g, w2g, gpost):
    T = h1.shape[0]
    tm = _tile(T, 512)
    nb, hb = w1g.shape[0], w1g.shape[2]

    def body(h1_ref, gpre_ref, w1_ref, w2_ref, gpost_ref, c2_ref, f1_ref, ff_ref, h2_ref):
        h1 = h1_ref[...]
        c2 = _rms_fwd(h1, gpre_ref[...]).astype(BF16)
        c2_ref[...] = c2
        ff = jnp.zeros((tm, D), F32)
        for j in range(nb):
            f1 = _nn(c2, w1_ref[j])
            f1_ref[:, j * hb:(j + 1) * hb] = f1.astype(BF16)
            r = jnp.maximum(f1, 0.0)
            ff = ff + _nn((r * r).astype(BF16), w2_ref[j])
        ff_ref[...] = ff
        h2_ref[...] = h1 + _rms_fwd(ff, gpost_ref[...])

    row = lambda n: pl.BlockSpec((tm, n), lambda i: (i, 0))
    once = lambda shape: pl.BlockSpec(shape, lambda i: (0,) * len(shape), pipeline_mode=pl.Buffered(1))
    return pl.pallas_call(
        body,
        name="ffn_fwd",
        grid=(T // tm,),
        in_specs=[row(D), _full((1, D)), once((nb, D, hb)), once((nb, hb, D)), _full((1, D))],
        out_specs=[row(D), row(DFF), row(D), row(D)],
        out_shape=[jax.ShapeDtypeStruct((T, D), BF16), jax.ShapeDtypeStruct((T, DFF), BF16),
                   jax.ShapeDtypeStruct((T, D), F32), jax.ShapeDtypeStruct((T, D), F32)],
        compiler_params=_params(1, VMEM_LIMIT_BIG),
    )(h1, gpre, w1g, w2g, gpost)


def _ple_loss(h2, p, tgt, wg, bg, wpe):
    T = h2.shape[0]
    tm = _tile(T, 512)

    def body(h2_ref, p_ref, t_ref, wg_ref, bg_ref, wpe_ref, dh2_ref, dbg_ref, loss_ref, dwg_ref, dwpe_ref):
        @pl.when(pl.program_id(0) == 0)
        def _():
            for r in (dbg_ref, loss_ref, dwg_ref, dwpe_ref):
                r[...] = jnp.zeros_like(r)

        h2 = h2_ref[...]
        h2b = h2.astype(BF16)
        gate = jax.nn.sigmoid(_nn(h2b, wg_ref[...]) + bg_ref[...])
        pb = p_ref[...].astype(BF16)
        pe = _nn(pb, wpe_ref[...])
        diff = (h2 + gate * pe) - t_ref[...]
        loss_ref[...] += jnp.sum(diff * diff)
        dh3 = diff * (1.0 / D)
        dpre = (dh3 * pe) * (gate * (1.0 - gate))
        dpre_b = dpre.astype(BF16)
        dbg_ref[...] += jnp.sum(dpre, axis=0, keepdims=True)
        dh2_ref[...] = dh3 + _nt(dpre_b, wg_ref[...])
        dwg_ref[...] += _nn(h2b.T, dpre_b)
        dwpe_ref[...] += _nn(pb.T, (dh3 * gate).astype(BF16))

    row = lambda n: pl.BlockSpec((tm, n), lambda i: (i, 0))
    once = lambda shape: pl.BlockSpec(shape, lambda i: (0,) * len(shape), pipeline_mode=pl.Buffered(1))
    return pl.pallas_call(
        body,
        name="ple_loss",
        grid=(T // tm,),
        in_specs=[row(D), row(PLE), row(D), _full((D, D)), _full((1, D)), _full((PLE, D))],
        out_specs=[row(D), _full((1, D)), _full((8, 128)), once((D, D)), once((PLE, D))],
        out_shape=[jax.ShapeDtypeStruct((T, D), F32), jax.ShapeDtypeStruct((1, D), F32),
                   jax.ShapeDtypeStruct((8, 128), F32), jax.ShapeDtypeStruct((D, D), F32),
                   jax.ShapeDtypeStruct((PLE, D), F32)],
        compiler_params=_params(1),
    )(h2, p, tgt, wg, bg, wpe)


def _ffn_bwd(dh2, ff, h1, f1, w1g, w2g, gpost, gpre):
    T = dh2.shape[0]
    tm = _tile(T, 512)
    nb, hb = w1g.shape[0], w1g.shape[2]

    def body(dh2_ref, ff_ref, h1_ref, f1_ref, w1_ref, w2_ref, gpost_ref, gpre_ref,
             dffb_ref, df1_ref, dh1_ref, dgpost_ref, dgpre_ref):
        @pl.when(pl.program_id(0) == 0)
        def _():
            dgpost_ref[...] = jnp.zeros_like(dgpost_ref)
            dgpre_ref[...] = jnp.zeros_like(dgpre_ref)

        dh2 = dh2_ref[...]
        dff, dg = _rms_bwd(dh2, ff_ref[...], gpost_ref[...])
        dffb = dff.astype(BF16)
        dffb_ref[...] = dffb
        dgpost_ref[...] += dg
        dc2 = jnp.zeros((tm, D), F32)
        for j in range(nb):
            cols = slice(j * hb, (j + 1) * hb)
            dact = _nt(dffb, w2_ref[j])
            df1 = (dact * (2.0 * jnp.maximum(f1_ref[:, cols].astype(F32), 0.0))).astype(BF16)
            df1_ref[:, cols] = df1
            dc2 = dc2 + _nt(df1, w1_ref[j])
        dx, dg = _rms_bwd(dc2, h1_ref[...], gpre_ref[...])
        dh1_ref[...] = dh2 + dx
        dgpre_ref[...] += dg

    row = lambda n: pl.BlockSpec((tm, n), lambda i: (i, 0))
    once = lambda shape: pl.BlockSpec(shape, lambda i: (0,) * len(shape), pipeline_mode=pl.Buffered(1))
    return pl.pallas_call(
        body,
        name="ffn_bwd",
        grid=(T // tm,),
        in_specs=[row(D), row(D), row(D), row(DFF), once((nb, D, hb)), once((nb, hb, D)), _full((1, D)), _full((1, D))],
        out_specs=[row(D), row(DFF), row(D), _full((1, D)), _full((1, D))],
        out_shape=[jax.ShapeDtypeStruct((T, D), BF16), jax.ShapeDtypeStruct((T, DFF), BF16),
                   jax.ShapeDtypeStruct((T, D), F32), jax.ShapeDtypeStruct((1, D), F32),
                   jax.ShapeDtypeStruct((1, D), F32)],
        compiler_params=_params(1, VMEM_LIMIT_BIG),
    )(dh2, ff, h1, f1, w1g, w2g, gpost, gpre)


def _mix_bwd(dh1, o, us, yatt, lng, lnb, sgw, sgwt, sgbt, gatt, gsg, wout, gpm):
    T = dh1.shape[0]
    tm = _tile(T, 512)

    def body(dh1_ref, o_ref, us_ref, ya_ref, lng_ref, lnb_ref, w_ref, wt_ref, bt_ref, ga_ref, gs_ref, wo_ref, gp_ref,
             dob_ref, dya_ref, dus_ref, dw_ref, dbt_ref, dlng_ref, dlnb_ref, dga_ref, dgs_ref, dgp_ref,
             mixed_s, vnb_s, dvn_s):
        @pl.when(pl.program_id(0) == 0)
        def _():
            for r in (dw_ref, dbt_ref, dlng_ref, dlnb_ref, dga_ref, dgs_ref, dgp_ref):
                r[...] = jnp.zeros_like(r)

        is_a = lax.broadcasted_iota(jnp.int32, (1, 128), 1) < HD
        lane = lax.broadcasted_iota(jnp.int32, (1, 128), 1)
        do, dg = _rms_bwd(dh1_ref[...], o_ref[...], gp_ref[...])
        dgp_ref[...] += dg
        dob = do.astype(BF16)
        dob_ref[...] = dob
        dy = _nt(dob, wo_ref[...])
        datt, dg = _rms_bwd(dy[:, 0:AW], ya_ref[...], ga_ref[...])
        dga_ref[...] += dg
        dya_ref[...] = datt

        lng = lng_ref[...]
        u, vs, ug, tu, tv, vhat, rstd, mixed, ysg = _sgu_forward(us_ref, lng, lnb_ref[...], w_ref, bt_ref, mixed_s, vnb_s, tm)
        dysg, dg = _rms_bwd(dy[:, AW:D], ysg, gs_ref[...])
        dgs_ref[...] += dg
        dus_ref[:, 0:SW] = ((dysg * mixed) * _gelu_grad(u, tu)).astype(BF16)
        dmix = dysg * ug

        rr = lax.broadcasted_iota(jnp.int32, (CH, CH), 0)
        cc = lax.broadcasted_iota(jnp.int32, (CH, CH), 1)
        tril = cc <= rr
        triu = cc >= rr
        for jj in range(4):
            wta = jnp.where(triu, wt_ref[2 * jj], 0.0).astype(BF16)
            wtb = jnp.where(triu, wt_ref[2 * jj + 1], 0.0).astype(BF16)
            for ch in range(tm // CH):
                rs, cs = slice(ch * CH, (ch + 1) * CH), slice(jj * 128, (jj + 1) * 128)
                dm2 = dmix[rs, cs]
                dma = jnp.where(is_a, dm2, 0.0)
                dmb = jnp.where(is_a, 0.0, dm2)
                dma_b, dmb_b = dma.astype(BF16), dmb.astype(BF16)
                vn2 = vnb_s[rs, cs]
                dw_ref[2 * jj] += jnp.where(tril, _nt(dma_b, vn2), 0.0)
                dw_ref[2 * jj + 1] += jnp.where(tril, _nt(dmb_b, vn2), 0.0)
                dvn_s[rs, cs] = _nn(wta, dma_b) + _nn(wtb, dmb_b)
                dba = jnp.sum(dma, axis=1, keepdims=True)
                dbb = jnp.sum(dmb, axis=1, keepdims=True)
                dbt_ref[...] += jnp.where(lane == 2 * jj, dba, 0.0) + jnp.where(lane == 2 * jj + 1, dbb, 0.0)

        dvn = dvn_s[...]
        dlng_ref[...] += jnp.sum(dvn * vhat, axis=0, keepdims=True)
        dlnb_ref[...] += jnp.sum(dvn, axis=0, keepdims=True)
        dvh = dvn * lng
        dvg = rstd * (dvh - jnp.mean(dvh, axis=-1, keepdims=True) - vhat * jnp.mean(dvh * vhat, axis=-1, keepdims=True))
        dus_ref[:, SW:NUS] = (dvg * _gelu_grad(vs, tv)).astype(BF16)

    row = lambda n: pl.BlockSpec((tm, n), lambda i: (i, 0))
    return pl.pallas_call(
        body,
        name="mix_bwd",
        grid=(T // tm,),
        in_specs=[row(D), row(D), row(NUS), row(AW), _full((1, SW)), _full((1, SW)), _full((8, CH, CH)), _full((8, CH, CH)),
                  _full((CH, 8)), _full((1, AW)), _full((1, SW)), _full((D, D)), _full((1, D))],
        out_specs=[row(D), row(AW), row(NUS), _full((8, CH, CH)), _full((CH, 128)), _full((1, SW)), _full((1, SW)),
                   _full((1, AW)), _full((1, SW)), _full((1, D))],
        out_shape=[jax.ShapeDtypeStruct((T, D), BF16), jax.ShapeDtypeStruct((T, AW), F32), jax.ShapeDtypeStruct((T, NUS), BF16),
                   jax.ShapeDtypeStruct((8, CH, CH), F32), jax.ShapeDtypeStruct((CH, 128), F32),
                   jax.ShapeDtypeStruct((1, SW), F32), jax.ShapeDtypeStruct((1, SW), F32),
                   jax.ShapeDtypeStruct((1, AW), F32), jax.ShapeDtypeStruct((1, SW), F32), jax.ShapeDtypeStruct((1, D), F32)],
        scratch_shapes=[pltpu.VMEM((tm, SW), F32), pltpu.VMEM((tm, SW), BF16), pltpu.VMEM((tm, SW), F32)],
        compiler_params=_params(1),
    )(dh1, o, us, yatt, lng, lnb, sgw, sgwt, sgbt, gatt, gsg, wout, gpm)


def _attn_prep(qkv, dya, yatt, lsec, after):
    T = dya.shape[0]
    tm = _tile(T, 512)

    def body(q_ref, do_ref, o_ref, l_ref, after_ref, qw_ref, dow_ref):
        do = do_ref[...]
        feat = lax.broadcasted_iota(jnp.int32, (AW, 128), 0)
        head = lax.broadcasted_iota(jnp.int32, (AW, 128), 1)
        sel = jnp.where((feat >= head * HD) & (feat < (head + 1) * HD), 1.0, 0.0)
        delta = jnp.dot(do * o_ref[...], sel, precision=lax.Precision.HIGHEST, preferred_element_type=F32)
        for hp in range(4):
            cols = slice(hp * 128, (hp + 1) * 128)
            for hh in range(2):
                base = (2 * hp + hh) * 256
                lc = l_ref[:, hp * 128 + hh:hp * 128 + hh + 1]
                d_h = delta[:, 2 * hp + hh:2 * hp + hh + 1]
                qw_ref[:, base:base + 128] = _head_rows(q_ref[:, cols], hh, 0.125)
                qw_ref[:, base + 128:base + 256] = _aug_lanes(tm, AUG_A if hh == 0 else AUG_B, _split3(-lc))
                dow_ref[:, base:base + 128] = _head_rows(do[:, cols], hh, 1.0).astype(BF16)
                dow_ref[:, base + 128:base + 256] = _aug_lanes(tm, None, _split3(-d_h))

    row = lambda n: pl.BlockSpec((tm, n), lambda i: (i, 0))
    return pl.pallas_call(
        body,
        name="attn_prep",
        grid=(T // tm,),
        in_specs=[row(AW)] * 4 + [_full(after.shape)],
        out_specs=[row(4 * AW)] * 2,
        out_shape=[jax.ShapeDtypeStruct((T, 4 * AW), BF16)] * 2,
        compiler_params=_params(1),
    )(qkv, dya, yatt, lsec, after)


def _attn_bwd(kk, vk, kt, qw, dow):
    T = kk.shape[0]
    tq = _tile(T, 512)
    tk = tq
    nq = T // tq

    def body(kk_ref, vk_ref, kt_ref, qw_ref, dow_ref, dqt_ref, dcq_ref, dk_ref, dv_ref, dck_ref, dk_s, dv_s, dck_s):
        j, i = _fold_cols(pl.program_id(1), pl.program_id(2), nq)
        sub8 = lax.broadcasted_iota(jnp.int32, (8, 1), 0)
        lane = lax.broadcasted_iota(jnp.int32, (1, 128), 1)

        @pl.when((pl.program_id(1) == 0) & (pl.program_id(2) == 0))
        def _():
            dqt_ref[...] = jnp.zeros_like(dqt_ref)
            dcq_ref[...] = jnp.zeros_like(dcq_ref)

        @pl.when(i == j)
        def _():
            dk_s[...] = jnp.zeros_like(dk_s)
            dv_s[...] = jnp.zeros_like(dv_s)
            dck_s[...] = jnp.zeros_like(dck_s)

        def step(masked):
            cols = pl.ds(pl.multiple_of(i * tq, tq), tq)
            sub = lax.broadcasted_iota(jnp.int32, (128, 1), 0)
            if masked:
                causal = lax.broadcasted_iota(jnp.int32, (tk, tq), 0) <= lax.broadcasted_iota(jnp.int32, (tk, tq), 1)

            def logits(n):
                pair, base = n // 2, n * 256
                return (_nt(kk_ref[:, pair * 256:(pair + 1) * 256], qw_ref[:, base:base + 256]),
                        _nt(vk_ref[:, pair * 256:(pair + 1) * 256], dow_ref[:, base:base + 256]))

            ahead = logits(0)
            dcq = jnp.zeros((8, tq), F32)
            dck = jnp.zeros((tk, 128), F32)
            for pp in range(PAIRS):
                lanes = slice(pp * 128, (pp + 1) * 128)
                kt2 = kt_ref[lanes, :] * 0.125
                dv = jnp.zeros((tk, 128), F32)
                dk = jnp.zeros((tk, 128), F32)
                dqts = []
                for hh in range(2):
                    head = 2 * pp + hh
                    base = head * 256
                    qw_h = qw_ref[:, base:base + 256]
                    dow_h = dow_ref[:, base:base + 256]
                    logp, dp = ahead
                    if head + 1 < 2 * PAIRS:
                        ahead = logits(head + 1)
                    pr = jnp.exp(logp)
                    if masked:
                        pr = jnp.where(causal, pr, 0.0)
                    ds = pr * dp
                    ds_b = ds.astype(BF16)
                    dv = dv + _nn(pr.astype(BF16), dow_h[:, 0:128])
                    dk = dk + _nn(ds_b, qw_h[:, 0:128])
                    dqts.append(_nn(kt2, ds_b))
                    dcq = dcq + jnp.where(sub8 == head, jnp.sum(ds, axis=0, keepdims=True), 0.0)
                    dck = dck - jnp.where(lane == head, jnp.sum(ds, axis=1, keepdims=True), 0.0)
                dv_s[:, lanes] += dv
                dk_s[:, lanes] += dk
                dqt_ref[lanes, cols] += jnp.where(sub < HD, dqts[0], dqts[1])
            dck_s[...] += dck
            dcq_ref[:, cols] += dcq

        @pl.when(i > j)
        def _():
            step(False)

        @pl.when(i == j)
        def _():
            step(True)

        @pl.when(i == nq - 1)
        def _():
            dk_ref[...] = dk_s[...].astype(BF16)
            dv_ref[...] = dv_s[...].astype(BF16)
            dck_ref[...] = dck_s[...]

    kj = lambda r, t: _fold_cols(r, t, nq)[0]
    qi = lambda r, t: _fold_cols(r, t, nq)[1]
    krow = lambda g, r, t: (kj(r, t), g)
    qrow = lambda g, r, t: (qi(r, t), g)
    return pl.pallas_call(
        body,
        name="attn_bwd",
        grid=_fold_grid(nq),
        in_specs=[
            pl.BlockSpec((tk, PAIRS * 256), krow),
            pl.BlockSpec((tk, PAIRS * 256), krow),
            pl.BlockSpec((PAIRS * 128, tk), lambda g, r, t: (g, kj(r, t))),
            pl.BlockSpec((tq, PAIRS * 512), qrow),
            pl.BlockSpec((tq, PAIRS * 512), qrow),
        ],
        out_specs=[
            pl.BlockSpec((PAIRS * 128, T), lambda g, r, t: (g, 0), pipeline_mode=pl.Buffered(1)),
            pl.BlockSpec((8, T), lambda g, r, t: (0, 0), pipeline_mode=pl.Buffered(1)),
            pl.BlockSpec((tk, PAIRS * 128), krow),
            pl.BlockSpec((tk, PAIRS * 128), krow),
            pl.BlockSpec((tk, 128), lambda g, r, t: (kj(r, t), 0)),
        ],
        out_shape=[jax.ShapeDtypeStruct((AW, T), F32), jax.ShapeDtypeStruct((8, T), F32),
                   jax.ShapeDtypeStruct((T, AW), BF16), jax.ShapeDtypeStruct((T, AW), BF16),
                   jax.ShapeDtypeStruct((T, FPAD), F32)],
        scratch_shapes=[pltpu.VMEM((tk, PAIRS * 128), F32), pltpu.VMEM((tk, PAIRS * 128), F32),
                        pltpu.VMEM((tk, 128), F32)],
        compiler_params=_params(3),
    )(kk, vk, kt, qw, dow)


def _fgate_bwd(dcq, dck, fz, fb):
    T = dck.shape[0]
    tb = _tile(T, 512)
    nb = T // tb

    def body(dcq_ref, dck_ref, fz_ref, fb_ref, df_ref, dfb_ref, carry):
        @pl.when(pl.program_id(0) == 0)
        def _():
            carry[...] = jnp.zeros_like(carry)
            dfb_ref[...] = jnp.zeros_like(dfb_ref)

        head = lax.broadcasted_iota(jnp.int32, (8, FPAD), 0)
        eye = jnp.where(head == lax.broadcasted_iota(jnp.int32, (8, FPAD), 1), 1.0, 0.0)
        dcv = dck_ref[...] + lax.dot_general(dcq_ref[...], eye, (((0,), (0,)), ((), ())),
                                             precision=lax.Precision.HIGHEST, preferred_element_type=F32)
        r = lax.broadcasted_iota(jnp.int32, (tb, tb), 0)
        cc = lax.broadcasted_iota(jnp.int32, (tb, tb), 1)
        tri = (cc >= r).astype(F32)
        dlf = jnp.dot(tri, dcv, precision=lax.Precision.HIGHEST, preferred_element_type=F32) + carry[...]
        carry[...] = carry[...] + jnp.sum(dcv, axis=0, keepdims=True)
        lane = lax.broadcasted_iota(jnp.int32, (tb, FPAD), 1)
        df = jnp.where(lane < 8, dlf * jax.nn.sigmoid(-(fz_ref[...] + fb_ref[...])), 0.0)
        df_ref[...] = df.astype(BF16)
        dfb_ref[...] += jnp.sum(df, axis=0, keepdims=True)

    rev = pl.BlockSpec((tb, FPAD), lambda i: (nb - 1 - i, 0))
    return pl.pallas_call(
        body,
        name="fgate_bwd",
        grid=(nb,),
        in_specs=[pl.BlockSpec((8, tb), lambda i: (0, nb - 1 - i)), rev, rev, _full((1, FPAD))],
        out_specs=[rev, _full((1, FPAD))],
        out_shape=[jax.ShapeDtypeStruct((T, FPAD), BF16), jax.ShapeDtypeStruct((1, FPAD), F32)],
        scratch_shapes=[pltpu.VMEM((1, FPAD), F32)],
        compiler_params=_params(1),
    )(dcq, dck, fz, fb)


def _inproj_bwd(dqt, dk, dv, dus, dfz, wp, wqt, x, dh1, g):
    T = x.shape[0]
    tm = _tile(T, 512)

    def body(dq_ref, dk_ref, dv_ref, dus_ref, dfz_ref, w_ref, wqt_ref, x_ref, dh1_ref, g_ref, gx_ref, dg_ref):
        @pl.when(pl.program_id(0) == 0)
        def _():
            dg_ref[...] = jnp.zeros_like(dg_ref)

        da = _tn(dq_ref[...].astype(BF16), wqt_ref[...])
        da += _nt(dk_ref[...], w_ref[:, AW:2 * AW])
        da += _nt(dv_ref[...], w_ref[:, 2 * AW:NQKV])
        da += _nt(dus_ref[...], w_ref[:, NQKV:NQKV + NUS])
        da += _nt(dfz_ref[...], w_ref[:, NQKV + NUS:ZP])
        dx, dg = _rms_bwd(da, x_ref[...], g_ref[...])
        gx_ref[...] = dh1_ref[...] + dx
        dg_ref[...] += dg

    row = lambda n: pl.BlockSpec((tm, n), lambda i: (i, 0))
    return pl.pallas_call(
        body,
        name="inproj_bwd",
        grid=(T // tm,),
        in_specs=[pl.BlockSpec((AW, tm), lambda i: (0, i)), row(AW), row(AW), row(NUS), row(FPAD), _full((D, ZP)),
                  _full((AW, D)), row(D), row(D), _full((1, D))],
        out_specs=[row(D), _full((1, D))],
        out_shape=[jax.ShapeDtypeStruct((T, D), F32), jax.ShapeDtypeStruct((1, D), F32)],
        compiler_params=_params(1),
    )(dqt, dk, dv, dus, dfz, wp, wqt, x, dh1, g)


def _sq_relu(f1):
    r = jnp.maximum(f1.astype(F32), 0.0)
    return (r * r).astype(BF16)


def _wgrad(a, bs, name, a_fn=None, out_dtype=F32):
    T, K = a.shape
    tt = _tile(T, 512)
    nb = len(bs)
    narrow = out_dtype != F32

    def out_dims(b, layout):
        if layout == "t":
            return (K, b.shape[0])
        N = b.shape[1]
        if layout is None:
            return (K, N)
        return (N // layout[1], K, layout[1]) if layout[0] == "col" else (K // layout[1], layout[1], N)

    shapes = [out_dims(b, layout) for b, layout in bs]

    def body(*refs):
        a_ref, b_refs, o_refs = refs[0], refs[1:1 + nb], refs[1 + nb:1 + 2 * nb]
        accs = refs[1 + 2 * nb:] if narrow else o_refs

        @pl.when(pl.program_id(0) == 0)
        def _():
            for acc in accs:
                acc[...] = jnp.zeros_like(acc)

        av = a_ref[...] if a_fn is None else a_fn(a_ref[...])
        at = av.astype(BF16).T
        for (b, layout), b_ref, o_ref in zip(bs, b_refs, accs):
            if layout is None:
                o_ref[...] += _nn(at, b_ref[...].astype(BF16))
            elif layout == "t":
                o_ref[...] += _nt(at, b_ref[...].astype(BF16))
            elif layout[0] == "col":
                n = layout[1]
                for k in range(b.shape[1] // n):
                    o_ref[k] += _nn(at, b_ref[:, k * n:(k + 1) * n].astype(BF16))
            else:
                n = layout[1]
                bv = b_ref[...].astype(BF16)
                for k in range(K // n):
                    o_ref[k] += _nn(at[k * n:(k + 1) * n, :], bv)

        if narrow:
            @pl.when(pl.program_id(0) == T // tt - 1)
            def _():
                for o_ref, acc in zip(o_refs, accs):
                    o_ref[...] = acc[...].astype(out_dtype)

    once = lambda shape: pl.BlockSpec(shape, lambda t: (0,) * len(shape), pipeline_mode=pl.Buffered(1))
    res = pl.pallas_call(
        body,
        name=name,
        grid=(T // tt,),
        in_specs=[pl.BlockSpec((tt, K), lambda t: (t, 0))] + [
            pl.BlockSpec((b.shape[0], tt), lambda t: (0, t)) if layout == "t" else pl.BlockSpec((tt, b.shape[1]), lambda t: (t, 0))
            for b, layout in bs],
        out_specs=[once(s) for s in shapes],
        out_shape=[jax.ShapeDtypeStruct(s, out_dtype) for s in shapes],
        scratch_shapes=[pltpu.VMEM(s, F32) for s in shapes] if narrow else [],
        compiler_params=_params(1, VMEM_LIMIT_BIG),
    )(a, *[b for b, _ in bs])
    return res


def _adam_math(w, g, m, v):
    m = ADAM_B1 * m + (1.0 - ADAM_B1) * g
    v = ADAM_B2 * v + (1.0 - ADAM_B2) * (g * g)
    m_hat = m / (1.0 - ADAM_B1 ** ADAM_STEP)
    v_hat = v / (1.0 - ADAM_B2 ** ADAM_STEP)
    delta = -ADAM_LR * (m_hat / (jnp.sqrt(v_hat) + ADAM_EPS) + ADAM_WD * w)
    return delta, m, v


def _adam(parts, w, m, v, name):
    R, C = w.shape
    br = 128 if R % 128 == 0 else R

    def body(p_ref, w_ref, m_ref, v_ref, g_ref, d_ref, nm_ref, nv_ref):
        g = p_ref[0].astype(F32)
        for s in range(1, NDEV):
            g = g + p_ref[s].astype(F32)
        g_ref[...] = g
        d_ref[...], nm_ref[...], nv_ref[...] = _adam_math(w_ref[...], g, m_ref[...], v_ref[...])

    blk = pl.BlockSpec((br, C), lambda i: (i, 0))
    return pl.pallas_call(
        body,
        name=name,
        grid=(R // br,),
        in_specs=[pl.BlockSpec((NDEV, br, C), lambda i: (0, i, 0)), blk, blk, blk],
        out_specs=[blk] * 4,
        out_shape=[jax.ShapeDtypeStruct((R, C), F32)] * 4,
        compiler_params=_params(1),
    )(parts, w, m, v)


_SMALL = (("sg_w", 8 * CH * CH), ("f_bias", 8), ("sg_ln_g", SW), ("sg_ln_b", SW), ("sg_b", 8 * CH), ("att_out_g", AW),
          ("sg_out_g", SW), ("pre_mix_g", D), ("post_mix_g", D), ("pre_ffn_g", D), ("post_ffn_g", D), ("ple_gate_b", D))
_SEG = 8 * 128


def _seg_rows(size):
    return 8 * (-(-size // _SEG))


def _pack(vals, loss_acc):
    parts = []
    for name, size in _SMALL:
        flat = vals[name].reshape(-1)
        rows = _seg_rows(size)
        parts.append(jnp.pad(flat, (0, rows * 128 - size)).reshape(rows, 128))
    parts.append(loss_acc)
    return jnp.concatenate(parts, axis=0)


def _adam_small(parts, ws, ms, vs):
    n = len(_SMALL)
    names = [name for name, _ in _SMALL]

    def body(*refs):
        p_ref, w_refs, m_refs, v_refs = refs[0], refs[1:1 + n], refs[1 + n:1 + 2 * n], refs[1 + 2 * n:1 + 3 * n]
        loss_ref, outs = refs[1 + 3 * n], refs[2 + 3 * n:]
        g_all = p_ref[0]
        for s in range(1, NDEV):
            g_all = g_all + p_ref[s]
        r = 0
        for k, (name, size) in enumerate(_SMALL):
            dst = [outs[kind * n + k] for kind in range(4)]

            def update(g, idx):
                vals = (g,) + _adam_math(w_refs[k][idx], g, m_refs[k][idx], v_refs[k][idx])
                for d, val in zip(dst, vals):
                    d[idx] = val

            if name == "sg_w":
                for grp in range(8):
                    update(g_all[r + grp * CH:r + (grp + 1) * CH, :], (0, grp))
            elif name == "sg_b":
                update(g_all[r:r + 8, :], (0,))
            elif name == "f_bias":
                update(g_all[r:r + 1, 0:8], (slice(None),))
            else:
                update(jnp.concatenate([g_all[r + q:r + q + 1, :] for q in range(size // 128)], axis=1), (slice(None),))
            r += _seg_rows(size)
        loss_ref[...] = g_all[r:r + 1, 0:1] * (0.5 / D)

    arrs = [parts] + [d[name] for d in (ws, ms, vs) for name in names]
    res = pl.pallas_call(
        body,
        name="adam_small",
        in_specs=[_full(a.shape) for a in arrs],
        out_specs=[_full((1, 1))] + [_full(ws[name].shape) for _ in range(4) for name in names],
        out_shape=[jax.ShapeDtypeStruct((1, 1), F32)] + [jax.ShapeDtypeStruct(ws[name].shape, F32) for _ in range(4) for name in names],
        compiler_params=pltpu.CompilerParams(vmem_limit_bytes=VMEM_LIMIT),
    )(*arrs)
    return res[0], {name: [res[1 + kind * n + k] for kind in range(4)] for k, name in enumerate(names)}


def kernel(x, p, w_in, f_bias, sg_ln_g, sg_ln_b, sg_w, sg_b, att_out_g, sg_out_g, w_out, pre_mix_g, post_mix_g, pre_ffn_g, post_ffn_g, w_ff1, w_ff2, ple_w, ple_gate_w, ple_gate_b, loss_target, m_w_in, m_f_bias, m_sg_ln_g, m_sg_ln_b, m_sg_w, m_sg_b, m_att_out_g, m_sg_out_g, m_w_out, m_pre_mix_g, m_post_mix_g, m_pre_ffn_g, m_post_ffn_g, m_w_ff1, m_w_ff2, m_ple_w, m_ple_gate_w, m_ple_gate_b, v_w_in, v_f_bias, v_sg_ln_g, v_sg_ln_b, v_sg_w, v_sg_b, v_att_out_g, v_sg_out_g, v_w_out, v_pre_mix_g, v_post_mix_g, v_pre_ffn_g, v_post_ffn_g, v_w_ff1, v_w_ff2, v_ple_w, v_ple_gate_w, v_ple_gate_b):
    small_w = dict(sg_w=sg_w, f_bias=f_bias, sg_ln_g=sg_ln_g, sg_ln_b=sg_ln_b, sg_b=sg_b, att_out_g=att_out_g,
                   sg_out_g=sg_out_g, pre_mix_g=pre_mix_g, post_mix_g=post_mix_g, pre_ffn_g=pre_ffn_g,
                   post_ffn_g=post_ffn_g, ple_gate_b=ple_gate_b)
    small_m = dict(sg_w=m_sg_w, f_bias=m_f_bias, sg_ln_g=m_sg_ln_g, sg_ln_b=m_sg_ln_b, sg_b=m_sg_b, att_out_g=m_att_out_g,
                   sg_out_g=m_sg_out_g, pre_mix_g=m_pre_mix_g, post_mix_g=m_post_mix_g, pre_ffn_g=m_pre_ffn_g,
                   post_ffn_g=m_post_ffn_g, ple_gate_b=m_ple_gate_b)
    small_v = dict(sg_w=v_sg_w, f_bias=v_f_bias, sg_ln_g=v_sg_ln_g, sg_ln_b=v_sg_ln_b, sg_b=v_sg_b, att_out_g=v_att_out_g,
                   sg_out_g=v_sg_out_g, pre_mix_g=v_pre_mix_g, post_mix_g=v_post_mix_g, pre_ffn_g=v_pre_ffn_g,
                   post_ffn_g=v_post_ffn_g, ple_gate_b=v_ple_gate_b)
    big = dict(w_in=(w_in, m_w_in, v_w_in), w_out=(w_out, m_w_out, v_w_out), w_ff1=(w_ff1, m_w_ff1, v_w_ff1),
               w_ff2=(w_ff2, m_w_ff2, v_w_ff2), ple_w=(ple_w, m_ple_w, v_ple_w),
               ple_gate_w=(ple_gate_w, m_ple_gate_w, v_ple_gate_w))

    xt, pt, tgt = x[0], p[0, 0], loss_target[0]
    ws = W_IN_COLS // NDEV

    gw_in = _gather(w_in[0].astype(BF16), "gather_w_in")
    rest = _xchg_start([w_out[0].astype(BF16), w_ff1[0].astype(BF16), w_ff2[0].astype(BF16), ple_w[0].astype(BF16),
                        ple_gate_w[0].astype(BF16)], True, "gather_rest_start")
    win = jnp.transpose(gw_in, (1, 0, 2)).reshape(D, W_IN_COLS)
    wp = jnp.concatenate([win[:, 0:NQKV], win[:, NQKV + 8:W_IN_COLS], win[:, NQKV:NQKV + 8],
                          jnp.zeros((D, FPAD - 8), BF16)], axis=1)

    fb = jnp.pad(f_bias.astype(F32), ((0, 0), (0, FPAD - 8)))
    sgw = sg_w[0]
    sgwt = jnp.transpose(sg_w[0], (0, 2, 1))
    sgbt = jnp.transpose(sg_b[0])

    qkv, kt, vt, us, fz, ab = _inproj(xt, pre_mix_g + rest[4][0, 0], wp)
    ct, kk, vk = _fcum(fz, fb, qkv)
    yatt, lsec = _attn_fwd(qkv, kk, vt, ct)
    gw_out, gw1, gw2, gwpe, gwg = _xchg_wait(rest, True, yatt, "gather_rest_wait")
    wout = gw_out.reshape(D, D)
    wg = gwg.reshape(D, D)
    wpe = jnp.transpose(gwpe, (1, 0, 2)).reshape(PLE, D)
    h1, yb, o = _sgu_out(us, yatt, xt, sg_ln_g, sg_ln_b, sgw, sgbt, att_out_g, sg_out_g, wout, post_mix_g)
    c2b, f1b, ff, h2 = _ffn_fwd(h1, pre_ffn_g, gw1, gw2, post_ffn_g)
    dh2, dbg, loss_acc, g_g, g_pe = _ple_loss(h2, pt, tgt, wg, ple_gate_b, wpe)
    g_g = g_g.reshape(NDEV, D // NDEV, D)
    g_pe = jnp.transpose(g_pe.reshape(PLE, NDEV, D // NDEV), (1, 0, 2))

    dffb, df1, dh1, dgpostffn, dgpreffn = _ffn_bwd(dh2, ff, h1, f1b, gw1, gw2, post_ffn_g, pre_ffn_g)
    (g_1,) = _wgrad(c2b, [(df1, ("col", DFF // NDEV))], "wgrad_ff1")
    (g_2,) = _wgrad(f1b, [(dffb, ("row", DFF // NDEV))], "wgrad_ff2", a_fn=_sq_relu)
    early = _xchg_start([g_1, g_2, g_pe, g_g], False, "scatter_early_start")
    dob, dya, dus, dsgw, dsgbt, dlng, dlnb, dgatt, dgsg, dgpostmix = _mix_bwd(
        dh1, o, us, yatt, sg_ln_g, sg_ln_b, sgw, sgwt, sgbt, att_out_g, sg_out_g, wout, post_mix_g + early[4][0, 0])
    (g_out,) = _wgrad(yb, [(dob, ("row", D // NDEV))], "wgrad_out")
    mid = _xchg_start([g_out], False, "scatter_mid_start")
    qw, dow = _attn_prep(qkv, dya, yatt, lsec, mid[4])
    dqt, dcq, dk, dv, dck = _attn_bwd(kk, vk, kt, qw, dow)
    dfz, dfb = _fgate_bwd(dcq, dck, fz, fb)

    gq, gk, gv, gus, gf = _wgrad(ab, [(dqt, "t"), (dk, None), (dv, None), (dus, None), (dfz, None)], "wgrad_in",
                                 out_dtype=BF16)
    g_in = jnp.concatenate([gq, gk, gv, gf[:, 0:8], gus], axis=1)
    g_in = jnp.transpose(g_in.reshape(D, NDEV, ws), (1, 0, 2))
    late = _xchg_start([g_in], False, "scatter_late_start")
    grad_x, dgpremix = _inproj_bwd(dqt, dk, dv, dus, dfz, wp, jnp.transpose(wp[:, 0:AW]), xt, dh1,
                                   pre_mix_g + late[4][0, 0])

    small_g = dict(sg_w=dsgw, f_bias=dfb[:, 0:8], sg_ln_g=dlng, sg_ln_b=dlnb, sg_b=jnp.transpose(dsgbt[:, 0:8]),
                   att_out_g=dgatt, sg_out_g=dgsg, pre_mix_g=dgpremix, post_mix_g=dgpostmix, pre_ffn_g=dgpreffn,
                   post_ffn_g=dgpostffn, ple_gate_b=dbg)

    r_1, r_2, r_pe, r_g = _xchg_wait(early, False, grad_x, "scatter_early_wait")
    (r_out,) = _xchg_wait(mid, False, grad_x, "scatter_mid_wait")
    r_small = _gather(_pack(small_g, loss_acc), "gather_small_grads")

    res = {}

    def adam_big(name, parts):
        w, m, v = big[name]
        res[name] = [t[None] for t in _adam(parts, w[0], m[0], v[0], "adam_" + name)]

    for name, parts in (("w_out", r_out), ("w_ff1", r_1), ("w_ff2", r_2), ("ple_w", r_pe), ("ple_gate_w", r_g)):
        adam_big(name, parts)
    (r_in,) = _xchg_wait(late, False, res["w_ff1"][0], "scatter_late_wait")
    adam_big("w_in", r_in)
    loss, small = _adam_small(r_small, small_w, small_m, small_v)
    res.update(small)

    order = ["w_in", "f_bias", "sg_ln_g", "sg_ln_b", "sg_w", "sg_b", "att_out_g", "sg_out_g", "w_out", "pre_mix_g",
             "post_mix_g", "pre_ffn_g", "post_ffn_g", "w_ff1", "w_ff2", "ple_w", "ple_gate_w", "ple_gate_b"]
    outs = [loss[0, 0], grad_x[None]]
    for kind in range(4):
        outs += [res[name][kind] for name in order]
    return tuple(outs)
```

```python
import jax
import jax.numpy as jnp
from jax import lax
from jax.experimental import pallas as pl
from jax.experimental.pallas import tpu as pltpu

F32 = jnp.float32
BF16 = jnp.bfloat16

NDEV = 8
D = 1024
AW = 512
SW = 512
HD = 64
CH = 128
DFF = 4096
PLE = 256
NQKV = 3 * AW
NUS = 2 * SW
FPAD = 128
ZP = NQKV + NUS + FPAD
W_IN_COLS = 2568
EPS = 1e-6
MASKV = -1e30
GELU_K = 0.7978845608028654
GELU_C = 0.044715

ADAM_LR = 0.001
ADAM_B1 = 0.9
ADAM_B2 = 0.999
ADAM_EPS = 1e-08
ADAM_WD = 0.01
ADAM_STEP = 10

VMEM_LIMIT = 48 * 1024 * 1024
VMEM_LIMIT_BIG = 60 * 1024 * 1024


def _nn(a, b):
    return jnp.dot(a, b, preferred_element_type=F32)


def _nt(a, b):
    return lax.dot_general(a, b, (((1,), (1,)), ((), ())), preferred_element_type=F32)


def _tn(a, b):
    return lax.dot_general(a, b, (((0,), (0,)), ((), ())), preferred_element_type=F32)


def _tile(n, pref):
    return min(n, pref)


def _params(n_axes, vmem=VMEM_LIMIT):
    return pltpu.CompilerParams(dimension_semantics=("arbitrary",) * n_axes, vmem_limit_bytes=vmem)


def _full(shape):
    nd = len(shape)
    return pl.BlockSpec(shape, lambda *_: (0,) * nd)


def _rms_fwd(x, g):
    r = lax.rsqrt(jnp.mean(x * x, axis=-1, keepdims=True) + EPS)
    return x * r * g


def _rms_bwd(dy, x, g):
    n = x.shape[-1]
    r = lax.rsqrt(jnp.mean(x * x, axis=-1, keepdims=True) + EPS)
    u = dy * g
    s = jnp.sum(x * u, axis=-1, keepdims=True)
    dx = r * u - x * (r * r * r * (s * (1.0 / n)))
    dg = jnp.sum(dy * (x * r), axis=0, keepdims=True)
    return dx, dg


def _gelu(x):
    t = jnp.tanh(x * (GELU_K + (GELU_K * GELU_C) * (x * x)))
    return x * (0.5 + 0.5 * t), t


def _gelu_grad(x, t):
    return (0.5 + 0.5 * t) + (0.5 * x) * (1.0 - t * t) * (GELU_K + (3.0 * GELU_K * GELU_C) * (x * x))


def _gather(arr, name):
    def body(x_ref, out_ref, send, recv, loc):
        x, y, c = lax.axis_index("x"), lax.axis_index("y"), lax.axis_index("c")
        me, sibling = (x, y, c), (x, y, 1 - c)
        chips = [(1 - x, y), (x, 1 - y), (1 - x, 1 - y)]

        def slot(px, py, pc):
            return out_ref.at[4 * px + 2 * py + pc]

        def copy(k, block, to, src=None):
            return pltpu.make_async_remote_copy(
                src_ref=slot(*block) if src is None else src, dst_ref=slot(*block), send_sem=send.at[k],
                recv_sem=recv.at[k], device_id=to, device_id_type=pl.DeviceIdType.MESH)

        mine = pltpu.make_async_copy(x_ref, slot(*me), loc)
        mine.start()
        first = [copy(0, me, sibling, src=x_ref)] + [copy(1 + j, me, (*chip, c), src=x_ref) for j, chip in enumerate(chips)]
        for cp in first:
            cp.start()
        passed = [copy(4 + j, (*chip, c), sibling) for j, chip in enumerate(chips)]
        for j, chip in enumerate(chips):
            copy(1 + j, (*chip, c), me).wait_recv()
            passed[j].start()
        copy(0, sibling, me).wait_recv()
        for j, chip in enumerate(chips):
            copy(4 + j, (*chip, 1 - c), me).wait_recv()
        for cp in first + passed:
            cp.wait_send()
        mine.wait()

    hbm = pl.BlockSpec(memory_space=pltpu.HBM)
    return pl.pallas_call(
        body,
        name=name,
        out_shape=jax.ShapeDtypeStruct((NDEV,) + arr.shape, arr.dtype),
        in_specs=[hbm],
        out_specs=hbm,
        scratch_shapes=[pltpu.SemaphoreType.DMA((NDEV - 1,)), pltpu.SemaphoreType.DMA((NDEV - 1,)), pltpu.SemaphoreType.DMA],
    )(arr)


def _peers(x, y, c):
    out = []
    for k in range(1, NDEV):
        out.append((1 - x if (k >> 2) & 1 else x, 1 - y if (k >> 1) & 1 else y, 1 - c if k & 1 else c))
    return out


def _xchg_start(arrs, gather, name):
    n = len(arrs)
    me = 4 * lax.axis_index("x") + 2 * lax.axis_index("y") + lax.axis_index("c")
    lands = []
    for a in arrs:
        shape = ((NDEV,) + a.shape) if gather else a.shape
        own = a[None] if gather else lax.dynamic_slice_in_dim(a, me, 1, axis=0)
        lands.append(lax.dynamic_update_slice_in_dim(lax.empty(shape, a.dtype), own, me, axis=0))

    def body(*refs):
        ins, lnd = refs[:n], refs[n:2 * n]
        send, recv, token = refs[2 * n:3 * n], refs[3 * n:4 * n], refs[-1]
        x, y, c = lax.axis_index("x"), lax.axis_index("y"), lax.axis_index("c")
        mine = 4 * x + 2 * y + c
        for px, py, pc in _peers(x, y, c):
            peer = 4 * px + 2 * py + pc
            for a in range(n):
                pltpu.make_async_remote_copy(
                    src_ref=ins[a] if gather else ins[a].at[peer],
                    dst_ref=lnd[a].at[mine],
                    send_sem=send[a],
                    recv_sem=recv[a],
                    device_id=(px, py, pc),
                    device_id_type=pl.DeviceIdType.MESH,
                ).start()
        token[...] = jnp.zeros_like(token)

    hbm = pl.BlockSpec(memory_space=pltpu.HBM)
    sem = pl.BlockSpec(memory_space=pltpu.SEMAPHORE)
    res = pl.pallas_call(
        body,
        name=name,
        out_shape=(*[pltpu.SemaphoreType.DMA(())] * (2 * n),
                   *[pltpu.HBM(a.shape, a.dtype) for a in arrs], *[pltpu.HBM(l.shape, l.dtype) for l in lands],
                   jax.ShapeDtypeStruct((8, 128), F32)),
        in_specs=[hbm] * (2 * n),
        out_specs=(*([sem] * (2 * n)), *([hbm] * (2 * n)), pl.BlockSpec(memory_space=pltpu.VMEM)),
        input_output_aliases={i: 2 * n + i for i in range(2 * n)},
        compiler_params=pltpu.CompilerParams(has_side_effects=pltpu.SideEffectType.DATAFLOW_SIDE_EFFECTING),
    )(*[pltpu.with_memory_space_constraint(a, pltpu.HBM) for a in arrs],
      *[pltpu.with_memory_space_constraint(l, pltpu.HBM) for l in lands])
    return list(res[0:n]), list(res[n:2 * n]), list(res[2 * n:3 * n]), list(res[3 * n:4 * n]), res[-1]


def _xchg_wait(started, gather, after, name):
    send, recv, srcs, lands, _ = started
    n = len(srcs)

    def body(*refs):
        lnd = refs[n:2 * n]
        send, recv = refs[2 * n:3 * n], refs[3 * n:4 * n]
        me = (lax.axis_index("x"), lax.axis_index("y"), lax.axis_index("c"))
        for a in range(n):
            seven = lnd[a].at[pl.ds(0, NDEV - 1)]
            cp = pltpu.make_async_remote_copy(src_ref=seven, dst_ref=seven, send_sem=send[a], recv_sem=recv[a],
                                              device_id=me, device_id_type=pl.DeviceIdType.MESH)
            cp.wait_send()
            cp.wait_recv()

    hbm = pl.BlockSpec(memory_space=pltpu.HBM)
    sem = pl.BlockSpec(memory_space=pltpu.SEMAPHORE)
    res = pl.pallas_call(
        body,
        name=name,
        out_shape=tuple([pltpu.HBM(a.shape, a.dtype) for a in srcs] + [pltpu.HBM(l.shape, l.dtype) for l in lands]),
        in_specs=[hbm] * (2 * n) + [sem] * (2 * n) + [pl.BlockSpec(memory_space=pl.ANY)],
        out_specs=tuple([hbm] * (2 * n)),
        input_output_aliases={i: i for i in range(2 * n)},
        compiler_params=pltpu.CompilerParams(has_side_effects=pltpu.SideEffectType.DATAFLOW_SIDE_EFFECTING),
    )(*srcs, *lands, *send, *recv, after)
    return list(res[n:])


def _inproj(x, g, wp):
    T = x.shape[0]
    tm = _tile(T, 512)

    def body(x_ref, g_ref, w_ref, qkv_ref, kt_ref, vt_ref, us_ref, fz_ref, ab_ref):
        a = _rms_fwd(x_ref[...], g_ref[...]).astype(BF16)
        ab_ref[...] = a
        qkv_ref[:, 0:AW] = _nn(a, w_ref[:, 0:AW]).astype(BF16)
        kk = _nn(a, w_ref[:, AW:2 * AW])
        qkv_ref[:, AW:2 * AW] = kk.astype(BF16)
        kt_ref[...] = kk.T.astype(BF16)
        vv = _nn(a, w_ref[:, 2 * AW:NQKV])
        qkv_ref[:, 2 * AW:NQKV] = vv.astype(BF16)
        vt_ref[...] = vv.T.astype(BF16)
        us_ref[...] = _nn(a, w_ref[:, NQKV:NQKV + NUS])
        fz_ref[...] = _nn(a, w_ref[:, NQKV + NUS:ZP])

    row = lambda n: pl.BlockSpec((tm, n), lambda i: (i, 0))
    col = pl.BlockSpec((AW, tm), lambda i: (0, i))
    return pl.pallas_call(
        body,
        name="inproj",
        grid=(T // tm,),
        in_specs=[row(D), _full((1, D)), _full((D, ZP))],
        out_specs=[row(NQKV), col, col, row(NUS), row(FPAD), row(D)],
        out_shape=[
            jax.ShapeDtypeStruct((T, NQKV), BF16),
            jax.ShapeDtypeStruct((AW, T), BF16),
            jax.ShapeDtypeStruct((AW, T), BF16),
            jax.ShapeDtypeStruct((T, NUS), F32),
            jax.ShapeDtypeStruct((T, FPAD), F32),
            jax.ShapeDtypeStruct((T, D), BF16),
        ],
        compiler_params=_params(1),
    )(x, g, wp)


def _log_sigmoid(z):
    return jnp.minimum(z, 0.0) - jnp.log1p(jnp.exp(-jnp.abs(z)))


def _split3(x):
    hi = x.astype(BF16)
    r1 = x - hi.astype(F32)
    mid = r1.astype(BF16)
    lo = (r1 - mid.astype(F32)).astype(BF16)
    return hi, mid, lo


AUG_A, AUG_B, AUG_ONE = 0, 3, 6


def _aug_lanes(rows, first_one, pieces):
    lane = lax.broadcasted_iota(jnp.int32, (rows, 128), 1)
    out = jnp.zeros((rows, 128), F32)
    if first_one is not None:
        out = jnp.where((lane >= first_one) & (lane < first_one + 3), 1.0, out)
    for n, piece in enumerate(pieces):
        out = jnp.where(lane == AUG_ONE + n, piece.astype(F32), out)
    return out.astype(BF16)


def _fcum(fz, fb, qkv):
    T = fz.shape[0]
    tb = _tile(T, 512)

    def body(fz_ref, fb_ref, k_ref, v_ref, ct_ref, kk_ref, vk_ref, carry, tri_s, sel_s):
        @pl.when(pl.program_id(0) == 0)
        def _():
            carry[...] = jnp.zeros_like(carry)
            r = lax.broadcasted_iota(jnp.int32, (tb, tb), 0)
            cc = lax.broadcasted_iota(jnp.int32, (tb, tb), 1)
            tri_s[...] = (cc <= r).astype(F32)
            src = lax.broadcasted_iota(jnp.int32, (128, 128), 0)
            dst = lax.broadcasted_iota(jnp.int32, (128, 128), 1)
            for hp in range(4):
                for n in range(3):
                    pick = ((src == 2 * hp) & (dst == AUG_A + n)) | ((src == 2 * hp + 1) & (dst == AUG_B + n))
                    sel_s[3 * hp + n] = jnp.where(pick, -1.0, 0.0).astype(BF16)

        lf = _log_sigmoid(fz_ref[...] + fb_ref[...])
        cs = jnp.dot(tri_s[...], lf, precision=lax.Precision.HIGHEST, preferred_element_type=F32) + carry[...]
        ct_ref[...] = cs.T[0:8, :]
        carry[...] = carry[...] + jnp.sum(lf, axis=0, keepdims=True)

        pieces = _split3(cs)
        lane = lax.broadcasted_iota(jnp.int32, (1, 128), 1)
        ones = jnp.where((lane >= AUG_ONE) & (lane < AUG_ONE + 3), 1.0, 0.0)
        for hp in range(4):
            aug = jnp.zeros((tb, 128), F32) + ones
            for n, piece in enumerate(pieces):
                aug = aug + _nn(piece, sel_s[3 * hp + n])
            aug = aug.astype(BF16)
            kk_ref[:, hp * 256:hp * 256 + 128] = k_ref[:, hp * 128:(hp + 1) * 128]
            kk_ref[:, hp * 256 + 128:(hp + 1) * 256] = aug
            vk_ref[:, hp * 256:hp * 256 + 128] = v_ref[:, hp * 128:(hp + 1) * 128]
            vk_ref[:, hp * 256 + 128:(hp + 1) * 256] = aug

    row = lambda n, c: pl.BlockSpec((tb, n), lambda i: (i, c))
    return pl.pallas_call(
        body,
        name="fcum",
        grid=(T // tb,),
        in_specs=[row(FPAD, 0), _full((1, FPAD)), row(AW, 1), row(AW, 2)],
        out_specs=[pl.BlockSpec((8, tb), lambda i: (0, i)), row(2 * AW, 0), row(2 * AW, 0)],
        out_shape=[jax.ShapeDtypeStruct((8, T), F32), jax.ShapeDtypeStruct((T, 2 * AW), BF16),
                   jax.ShapeDtypeStruct((T, 2 * AW), BF16)],
        scratch_shapes=[pltpu.VMEM((1, FPAD), F32), pltpu.VMEM((tb, tb), F32), pltpu.VMEM((12, 128, 128), BF16)],
        compiler_params=_params(1),
    )(fz, fb, qkv, qkv)


def _fold_rows(r, t, nq):
    if nq == 1:
        return r, t
    low = t <= r
    return jnp.where(low, r, nq - 1 - r), jnp.where(low, t, t - r - 1)


def _fold_cols(r, t, nq):
    if nq == 1:
        return r, t
    first = t < nq - r
    j = jnp.where(first, r, nq - 1 - r)
    return j, jnp.where(first, r + t, nq - 1 - r + (t - (nq - r)))


PAIRS = 4


def _fold_grid(nq):
    assert (nq == 1 or nq % 2 == 0) and PAIRS == 4
    return (4 // PAIRS, 1, 1) if nq == 1 else (4 // PAIRS, nq // 2, nq + 1)


def _head_rows(x2, hh, scale):
    is_a = lax.broadcasted_iota(jnp.int32, (1, 128), 1) < HD
    keep = is_a if hh == 0 else jnp.logical_not(is_a)
    return jnp.where(keep, x2, jnp.zeros_like(x2)) * scale


def _attn_fwd(qkv, kk, vt, ct):
    T = qkv.shape[0]
    tq = _tile(T, 512)
    tk = tq
    nq = T // tq

    def body(q_ref, kk_ref, vt_ref, ctq_ref, o_ref, lsec_ref, qw_s, m_s, l_s, acc_s):
        i, j = _fold_rows(pl.program_id(1), pl.program_id(2), nq)
        sub8 = lax.broadcasted_iota(jnp.int32, (8, 1), 0)

        def cref_of(pp, hh):
            head = 2 * (PAIRS * pl.program_id(0) + pp) + hh
            return jnp.sum(jnp.where(sub8 == head, ctq_ref[:, 0:1], 0.0), axis=0, keepdims=True)

        @pl.when(j == 0)
        def _():
            for pp in range(PAIRS):
                q2 = q_ref[:, pp * 128:(pp + 1) * 128]
                for hh in range(2):
                    rows = slice(hh * tq, (hh + 1) * tq)
                    qw_s[pp, rows, 0:128] = _head_rows(q2, hh, 0.125)
                    qw_s[pp, rows, 128:256] = _aug_lanes(tq, AUG_A if hh == 0 else AUG_B, _split3(cref_of(pp, hh)))
            m_s[...] = jnp.full_like(m_s, MASKV)
            l_s[...] = jnp.zeros_like(l_s)
            acc_s[...] = jnp.zeros_like(acc_s)

        def step(masked):
            if masked:
                causal = lax.broadcasted_iota(jnp.int32, (tk, tq), 0) <= lax.broadcasted_iota(jnp.int32, (tk, tq), 1)
            logits = lambda pp: _nt(kk_ref[:, pp * 256:(pp + 1) * 256], qw_s[pp])
            s_next = logits(0)
            for pp in range(PAIRS):
                s2 = s_next
                if pp + 1 < PAIRS:
                    s_next = logits(pp + 1)
                vt2 = vt_ref[pp * 128:(pp + 1) * 128, :]
                for hh in range(2):
                    n = 2 * pp + hh
                    s = s2[:, hh * tq:(hh + 1) * tq]
                    if masked:
                        s = jnp.where(causal, s, MASKV)
                    m_prev = m_s[n]
                    m_new = jnp.maximum(m_prev, jnp.max(s, axis=0, keepdims=True))
                    pr = jnp.exp(s - m_new)
                    alpha = jnp.exp(m_prev - m_new)
                    l_s[n] = alpha * l_s[n] + jnp.sum(pr, axis=0, keepdims=True)
                    m_s[n] = m_new
                    acc_s[n] = alpha * acc_s[n] + _nn(vt2, pr.astype(BF16))

        @pl.when(j < i)
        def _():
            step(False)

        @pl.when(j == i)
        def _():
            step(True)
            sub = lax.broadcasted_iota(jnp.int32, (128, 1), 0)
            for pp in range(PAIRS):
                a, b = 2 * pp, 2 * pp + 1
                ot = jnp.where(sub < HD, acc_s[a] * (1.0 / l_s[a]), acc_s[b] * (1.0 / l_s[b]))
                o_ref[:, pp * 128:(pp + 1) * 128] = ot.T
                lrow = [m_s[2 * pp + hh] + jnp.log(l_s[2 * pp + hh]) - cref_of(pp, hh) for hh in range(2)]
                lsec_ref[:, pp * 128:(pp + 1) * 128] = jnp.where(sub == 0, lrow[0], jnp.where(sub == 1, lrow[1], 0.0)).T

    qi = lambda r, t: _fold_rows(r, t, nq)[0]
    kj = lambda r, t: _fold_rows(r, t, nq)[1]
    return pl.pallas_call(
        body,
        name="attn_fwd",
        grid=_fold_grid(nq),
        in_specs=[
            pl.BlockSpec((tq, PAIRS * 128), lambda g, r, t: (qi(r, t), g)),
            pl.BlockSpec((tk, PAIRS * 256), lambda g, r, t: (kj(r, t), g)),
            pl.BlockSpec((PAIRS * 128, tk), lambda g, r, t: (g, kj(r, t))),
            pl.BlockSpec((8, tq), lambda g, r, t: (0, qi(r, t))),
        ],
        out_specs=[pl.BlockSpec((tq, PAIRS * 128), lambda g, r, t: (qi(r, t), g))] * 2,
        out_shape=[jax.ShapeDtypeStruct((T, AW), F32)] * 2,
        scratch_shapes=[pltpu.VMEM((PAIRS, 2 * tq, 256), BF16), pltpu.VMEM((2 * PAIRS, 1, tq), F32),
                        pltpu.VMEM((2 * PAIRS, 1, tq), F32), pltpu.VMEM((2 * PAIRS, 128, tq), F32)],
        compiler_params=_params(3),
    )(qkv, kk, vt, ct)


def _sgu_forward(us_ref, lng, lnb, w_ref, bt_ref, mixed_s, vnb_s, tm):
    is_a = lax.broadcasted_iota(jnp.int32, (1, 128), 1) < HD
    u = us_ref[:, 0:SW]
    vs = us_ref[:, SW:NUS]
    ug, tu = _gelu(u)
    vg, tv = _gelu(vs)
    mu = jnp.mean(vg, axis=-1, keepdims=True)
    xc = vg - mu
    rstd = lax.rsqrt(jnp.mean(xc * xc, axis=-1, keepdims=True) + EPS)
    vhat = xc * rstd
    vnb_s[...] = (vhat * lng + lnb).astype(BF16)
    rr = lax.broadcasted_iota(jnp.int32, (CH, CH), 0)
    cc = lax.broadcasted_iota(jnp.int32, (CH, CH), 1)
    tril = cc <= rr
    for jj in range(4):
        wa = jnp.where(tril, w_ref[2 * jj], 0.0).astype(BF16)
        wb = jnp.where(tril, w_ref[2 * jj + 1], 0.0).astype(BF16)
        ba = bt_ref[:, 2 * jj:2 * jj + 1]
        bb = bt_ref[:, 2 * jj + 1:2 * jj + 2]
        for ch in range(tm // CH):
            rs, cs = slice(ch * CH, (ch + 1) * CH), slice(jj * 128, (jj + 1) * 128)
            vn2 = vnb_s[rs, cs]
            mixed_s[rs, cs] = jnp.where(is_a, _nn(wa, vn2) + ba, _nn(wb, vn2) + bb)
    mixed = mixed_s[...]
    return u, vs, ug, tu, tv, vhat, rstd, mixed, ug * mixed


def _sgu_out(us, yatt, x, lng, lnb, sgw, sgbt, gatt, gsg, wout, gpm):
    T = us.shape[0]
    tm = _tile(T, 512)

    def body(us_ref, ya_ref, x_ref, lng_ref, lnb_ref, w_ref, bt_ref, ga_ref, gs_ref, wo_ref, gp_ref,
             h1_ref, yb_ref, o_ref, mixed_s, vnb_s):
        ysg = _sgu_forward(us_ref, lng_ref[...], lnb_ref[...], w_ref, bt_ref, mixed_s, vnb_s, tm)[-1]
        yb_ref[:, 0:AW] = _rms_fwd(ya_ref[...], ga_ref[...]).astype(BF16)
        yb_ref[:, AW:D] = _rms_fwd(ysg, gs_ref[...]).astype(BF16)
        o = _nn(yb_ref[...], wo_ref[...])
        o_ref[...] = o
        h1_ref[...] = x_ref[...] + _rms_fwd(o, gp_ref[...])

    row = lambda n: pl.BlockSpec((tm, n), lambda i: (i, 0))
    return pl.pallas_call(
        body,
        name="sgu_out",
        grid=(T // tm,),
        in_specs=[row(NUS), row(AW), row(D), _full((1, SW)), _full((1, SW)), _full((8, CH, CH)), _full((CH, 8)),
                  _full((1, AW)), _full((1, SW)), _full((D, D)), _full((1, D))],
        out_specs=[row(D), row(D), row(D)],
        out_shape=[jax.ShapeDtypeStruct((T, D), F32), jax.ShapeDtypeStruct((T, D), BF16), jax.ShapeDtypeStruct((T, D), F32)],
        scratch_shapes=[pltpu.VMEM((tm, SW), F32), pltpu.VMEM((tm, SW), BF16)],
        compiler_params=_params(1),
    )(us, yatt, x, lng, lnb, sgw, sgbt, gatt, gsg, wout, gpm)


def _ffn_fwd(h1, gpre, w1g, w2g, gpost):
    T = h1.shape[0]
    tm = _tile(T, 512)
    nb, hb = w1g.shape[0], w1g.shape[2]

    def body(h1_ref, gpre_ref, w1_ref, w2_ref, gpost_ref, c2_ref, f1_ref, ff_ref, h2_ref):
        h1 = h1_ref[...]
        c2 = _rms_fwd(h1, gpre_ref[...]).astype(BF16)
        c2_ref[...] = c2
        ff = jnp.zeros((tm, D), F32)
        for j in range(nb):
            f1 = _nn(c2, w1_ref[j])
            f1_ref[:, j * hb:(j + 1) * hb] = f1.astype(BF16)
            r = jnp.maximum(f1, 0.0)
            ff = ff + _nn((r * r).astype(BF16), w2_ref[j])
        ff_ref[...] = ff
        h2_ref[...] = h1 + _rms_fwd(ff, gpost_ref[...])

    row = lambda n: pl.BlockSpec((tm, n), lambda i: (i, 0))
    once = lambda shape: pl.BlockSpec(shape, lambda i: (0,) * len(shape), pipeline_mode=pl.Buffered(1))
    return pl.pallas_call(
        body,
        name="ffn_fwd",
        grid=(T // tm,),
        in_specs=[row(D), _full((1, D)), once((nb, D, hb)), once((nb, hb, D)), _full((1, D))],
        out_specs=[row(D), row(DFF), row(D), row(D)],
        out_shape=[jax.ShapeDtypeStruct((T, D), BF16), jax.ShapeDtypeStruct((T, DFF), BF16),
                   jax.ShapeDtypeStruct((T, D), F32), jax.ShapeDtypeStruct((T, D), F32)],
        compiler_params=_params(1, VMEM_LIMIT_BIG),
    )(h1, gpre, w1g, w2g, gpost)


def _ple_loss(h2, p, tgt, wg, bg, wpe):
    T = h2.shape[0]
    tm = _tile(T, 512)

    def body(h2_ref, p_ref, t_ref, wg_ref, bg_ref, wpe_ref, dh2_ref, dbg_ref, loss_ref, dwg_ref, dwpe_ref):
        @pl.when(pl.program_id(0) == 0)
        def _():
            for r in (dbg_ref, loss_ref, dwg_ref, dwpe_ref):
                r[...] = jnp.zeros_like(r)

        h2 = h2_ref[...]
        h2b = h2.astype(BF16)
        gate = jax.nn.sigmoid(_nn(h2b, wg_ref[...]) + bg_ref[...])
        pb = p_ref[...].astype(BF16)
        pe = _nn(pb, wpe_ref[...])
        diff = (h2 + gate * pe) - t_ref[...]
        loss_ref[...] += jnp.sum(diff * diff)
        dh3 = diff * (1.0 / D)
        dpre = (dh3 * pe) * (gate * (1.0 - gate))
        dpre_b = dpre.astype(BF16)
        dbg_ref[...] += jnp.sum(dpre, axis=0, keepdims=True)
        dh2_ref[...] = dh3 + _nt(dpre_b, wg_ref[...])
        dwg_ref[...] += _nn(h2b.T, dpre_b)
        dwpe_ref[...] += _nn(pb.T, (dh3 * gate).astype(BF16))

    row = lambda n: pl.BlockSpec((tm, n), lambda i: (i, 0))
    once = lambda shape: pl.BlockSpec(shape, lambda i: (0,) * len(shape), pipeline_mode=pl.Buffered(1))
    return pl.pallas_call(
        body,
        name="ple_loss",
        grid=(T // tm,),
        in_specs=[row(D), row(PLE), row(D), _full((D, D)), _full((1, D)), _full((PLE, D))],
        out_specs=[row(D), _full((1, D)), _full((8, 128)), once((D, D)), once((PLE, D))],
        out_shape=[jax.ShapeDtypeStruct((T, D), F32), jax.ShapeDtypeStruct((1, D), F32),
                   jax.ShapeDtypeStruct((8, 128), F32), jax.ShapeDtypeStruct((D, D), F32),
                   jax.ShapeDtypeStruct((PLE, D), F32)],
        compiler_params=_params(1),
    )(h2, p, tgt, wg, bg, wpe)


def _ffn_bwd(dh2, ff, h1, f1, w1g, w2g, gpost, gpre):
    T = dh2.shape[0]
    tm = _tile(T, 512)
    nb, hb = w1g.shape[0], w1g.shape[2]

    def body(dh2_ref, ff_ref, h1_ref, f1_ref, w1_ref, w2_ref, gpost_ref, gpre_ref,
             dffb_ref, df1_ref, dh1_ref, dgpost_ref, dgpre_ref):
        @pl.when(pl.program_id(0) == 0)
        def _():
            dgpost_ref[...] = jnp.zeros_like(dgpost_ref)
            dgpre_ref[...] = jnp.zeros_like(dgpre_ref)

        dh2 = dh2_ref[...]
        dff, dg = _rms_bwd(dh2, ff_ref[...], gpost_ref[...])
        dffb = dff.astype(BF16)
        dffb_ref[...] = dffb
        dgpost_ref[...] += dg
        dc2 = jnp.zeros((tm, D), F32)
        for j in range(nb):
            cols = slice(j * hb, (j + 1) * hb)
            dact = _nt(dffb, w2_ref[j])
            df1 = (dact * (2.0 * jnp.maximum(f1_ref[:, cols].astype(F32), 0.0))).astype(BF16)
            df1_ref[:, cols] = df1
            dc2 = dc2 + _nt(df1, w1_ref[j])
        dx, dg = _rms_bwd(dc2, h1_ref[...], gpre_ref[...])
        dh1_ref[...] = dh2 + dx
        dgpre_ref[...] += dg

    row = lambda n: pl.BlockSpec((tm, n), lambda i: (i, 0))
    once = lambda shape: pl.BlockSpec(shape, lambda i: (0,) * len(shape), pipeline_mode=pl.Buffered(1))
    return pl.pallas_call(
        body,
        name="ffn_bwd",
        grid=(T // tm,),
        in_specs=[row(D), row(D), row(D), row(DFF), once((nb, D, hb)), once((nb, hb, D)), _full((1, D)), _full((1, D))],
        out_specs=[row(D), row(DFF), row(D), _full((1, D)), _full((1, D))],
        out_shape=[jax.ShapeDtypeStruct((T, D), BF16), jax.ShapeDtypeStruct((T, DFF), BF16),
                   jax.ShapeDtypeStruct((T, D), F32), jax.ShapeDtypeStruct((1, D), F32),
                   jax.ShapeDtypeStruct((1, D), F32)],
        compiler_params=_params(1, VMEM_LIMIT_BIG),
    )(dh2, ff, h1, f1, w1g, w2g, gpost, gpre)


def _mix_bwd(dh1, o, us, yatt, qkv, lsec, lng, lnb, sgw, sgwt, sgbt, gatt, gsg, wout, gpm):
    T = dh1.shape[0]
    tm = _tile(T, 512)

    def body(dh1_ref, o_ref, us_ref, ya_ref, q_ref, l_ref, lng_ref, lnb_ref, w_ref, wt_ref, bt_ref, ga_ref, gs_ref, wo_ref,
             gp_ref, dob_ref, qw_ref, dow_ref, dus_ref, dw_ref, dbt_ref, dlng_ref, dlnb_ref, dga_ref, dgs_ref, dgp_ref,
             mixed_s, vnb_s, dvn_s):
        @pl.when(pl.program_id(0) == 0)
        def _():
            for r in (dw_ref, dbt_ref, dlng_ref, dlnb_ref, dga_ref, dgs_ref, dgp_ref):
                r[...] = jnp.zeros_like(r)

        is_a = lax.broadcasted_iota(jnp.int32, (1, 128), 1) < HD
        lane = lax.broadcasted_iota(jnp.int32, (1, 128), 1)
        do, dg = _rms_bwd(dh1_ref[...], o_ref[...], gp_ref[...])
        dgp_ref[...] += dg
        dob = do.astype(BF16)
        dob_ref[...] = dob
        dy = _nt(dob, wo_ref[...])
        ya = ya_ref[...]
        datt, dg = _rms_bwd(dy[:, 0:AW], ya, ga_ref[...])
        dga_ref[...] += dg
        _attn_operands(q_ref[...], datt, ya, l_ref[...], qw_ref, dow_ref)

        lng = lng_ref[...]
        u, vs, ug, tu, tv, vhat, rstd, mixed, ysg = _sgu_forward(us_ref, lng, lnb_ref[...], w_ref, bt_ref, mixed_s, vnb_s, tm)
        dysg, dg = _rms_bwd(dy[:, AW:D], ysg, gs_ref[...])
        dgs_ref[...] += dg
        dus_ref[:, 0:SW] = ((dysg * mixed) * _gelu_grad(u, tu)).astype(BF16)
        dmix = dysg * ug

        rr = lax.broadcasted_iota(jnp.int32, (CH, CH), 0)
        cc = lax.broadcasted_iota(jnp.int32, (CH, CH), 1)
        tril = cc <= rr
        triu = cc >= rr
        for jj in range(4):
            wta = jnp.where(triu, wt_ref[2 * jj], 0.0).astype(BF16)
            wtb = jnp.where(triu, wt_ref[2 * jj + 1], 0.0).astype(BF16)
            for ch in range(tm // CH):
                rs, cs = slice(ch * CH, (ch + 1) * CH), slice(jj * 128, (jj + 1) * 128)
                dm2 = dmix[rs, cs]
                dma = jnp.where(is_a, dm2, 0.0)
                dmb = jnp.where(is_a, 0.0, dm2)
                dma_b, dmb_b = dma.astype(BF16), dmb.astype(BF16)
                vn2 = vnb_s[rs, cs]
                dw_ref[2 * jj] += jnp.where(tril, _nt(dma_b, vn2), 0.0)
                dw_ref[2 * jj + 1] += jnp.where(tril, _nt(dmb_b, vn2), 0.0)
                dvn_s[rs, cs] = _nn(wta, dma_b) + _nn(wtb, dmb_b)
                dba = jnp.sum(dma, axis=1, keepdims=True)
                dbb = jnp.sum(dmb, axis=1, keepdims=True)
                dbt_ref[...] += jnp.where(lane == 2 * jj, dba, 0.0) + jnp.where(lane == 2 * jj + 1, dbb, 0.0)

        dvn = dvn_s[...]
        dlng_ref[...] += jnp.sum(dvn * vhat, axis=0, keepdims=True)
        dlnb_ref[...] += jnp.sum(dvn, axis=0, keepdims=True)
        dvh = dvn * lng
        dvg = rstd * (dvh - jnp.mean(dvh, axis=-1, keepdims=True) - vhat * jnp.mean(dvh * vhat, axis=-1, keepdims=True))
        dus_ref[:, SW:NUS] = (dvg * _gelu_grad(vs, tv)).astype(BF16)

    row = lambda n: pl.BlockSpec((tm, n), lambda i: (i, 0))
    return pl.pallas_call(
        body,
        name="mix_bwd",
        grid=(T // tm,),
        in_specs=[row(D), row(D), row(NUS), row(AW), row(AW), row(AW), _full((1, SW)), _full((1, SW)), _full((8, CH, CH)),
                  _full((8, CH, CH)), _full((CH, 8)), _full((1, AW)), _full((1, SW)), _full((D, D)), _full((1, D))],
        out_specs=[row(D), row(4 * AW), row(4 * AW), row(NUS), _full((8, CH, CH)), _full((CH, 128)), _full((1, SW)),
                   _full((1, SW)), _full((1, AW)), _full((1, SW)), _full((1, D))],
        out_shape=[jax.ShapeDtypeStruct((T, D), BF16), jax.ShapeDtypeStruct((T, 4 * AW), BF16),
                   jax.ShapeDtypeStruct((T, 4 * AW), BF16), jax.ShapeDtypeStruct((T, NUS), BF16),
                   jax.ShapeDtypeStruct((8, CH, CH), F32), jax.ShapeDtypeStruct((CH, 128), F32),
                   jax.ShapeDtypeStruct((1, SW), F32), jax.ShapeDtypeStruct((1, SW), F32),
                   jax.ShapeDtypeStruct((1, AW), F32), jax.ShapeDtypeStruct((1, SW), F32), jax.ShapeDtypeStruct((1, D), F32)],
        scratch_shapes=[pltpu.VMEM((tm, SW), F32), pltpu.VMEM((tm, SW), BF16), pltpu.VMEM((tm, SW), F32)],
        compiler_params=_params(1, VMEM_LIMIT_BIG),
    )(dh1, o, us, yatt, qkv, lsec, lng, lnb, sgw, sgwt, sgbt, gatt, gsg, wout, gpm)


def _attn_operands(q, do, o, lsec, qw_ref, dow_ref):
    rows = do.shape[0]
    feat = lax.broadcasted_iota(jnp.int32, (AW, 128), 0)
    head = lax.broadcasted_iota(jnp.int32, (AW, 128), 1)
    sel = jnp.where((feat >= head * HD) & (feat < (head + 1) * HD), 1.0, 0.0)
    delta = jnp.dot(do * o, sel, precision=lax.Precision.HIGHEST, preferred_element_type=F32)
    for hp in range(4):
        cols = slice(hp * 128, (hp + 1) * 128)
        for hh in range(2):
            base = (2 * hp + hh) * 256
            lc = lsec[:, hp * 128 + hh:hp * 128 + hh + 1]
            d_h = delta[:, 2 * hp + hh:2 * hp + hh + 1]
            qw_ref[:, base:base + 128] = _head_rows(q[:, cols], hh, 0.125)
            qw_ref[:, base + 128:base + 256] = _aug_lanes(rows, AUG_A if hh == 0 else AUG_B, _split3(-lc))
            dow_ref[:, base:base + 128] = _head_rows(do[:, cols], hh, 1.0).astype(BF16)
            dow_ref[:, base + 128:base + 256] = _aug_lanes(rows, None, _split3(-d_h))


def _attn_bwd(kk, vk, kt, qw, dow, after):
    T = kk.shape[0]
    tq = _tile(T, 512)
    tk = tq
    nq = T // tq

    def body(kk_ref, vk_ref, kt_ref, qw_ref, dow_ref, after_ref, dqt_ref, dcq_ref, dk_ref, dv_ref, dck_ref, dk_s, dv_s, dck_s):
        j, i = _fold_cols(pl.program_id(1), pl.program_id(2), nq)
        sub8 = lax.broadcasted_iota(jnp.int32, (8, 1), 0)
        lane = lax.broadcasted_iota(jnp.int32, (1, 128), 1)

        @pl.when((pl.program_id(1) == 0) & (pl.program_id(2) == 0))
        def _():
            dqt_ref[...] = jnp.zeros_like(dqt_ref)
            dcq_ref[...] = jnp.zeros_like(dcq_ref)

        @pl.when(i == j)
        def _():
            dk_s[...] = jnp.zeros_like(dk_s)
            dv_s[...] = jnp.zeros_like(dv_s)
            dck_s[...] = jnp.zeros_like(dck_s)

        def step(masked):
            cols = pl.ds(pl.multiple_of(i * tq, tq), tq)
            sub = lax.broadcasted_iota(jnp.int32, (128, 1), 0)
            if masked:
                causal = lax.broadcasted_iota(jnp.int32, (tk, tq), 0) <= lax.broadcasted_iota(jnp.int32, (tk, tq), 1)

            def logits(n):
                pair, base = n // 2, n * 256
                return (_nt(kk_ref[:, pair * 256:(pair + 1) * 256], qw_ref[:, base:base + 256]),
                        _nt(vk_ref[:, pair * 256:(pair + 1) * 256], dow_ref[:, base:base + 256]))

            ahead = logits(0)
            dcq = jnp.zeros((8, tq), F32)
            dck = jnp.zeros((tk, 128), F32)
            for pp in range(PAIRS):
                lanes = slice(pp * 128, (pp + 1) * 128)
                kt2 = kt_ref[lanes, :] * 0.125
                dv = jnp.zeros((tk, 128), F32)
                dk = jnp.zeros((tk, 128), F32)
                dqts = []
                for hh in range(2):
                    head = 2 * pp + hh
                    base = head * 256
                    qw_h = qw_ref[:, base:base + 256]
                    dow_h = dow_ref[:, base:base + 256]
                    logp, dp = ahead
                    if head + 1 < 2 * PAIRS:
                        ahead = logits(head + 1)
                    pr = jnp.exp(logp)
                    if masked:
                        pr = jnp.where(causal, pr, 0.0)
                    ds = pr * dp
                    ds_b = ds.astype(BF16)
                    dv = dv + _nn(pr.astype(BF16), dow_h[:, 0:128])
                    dk = dk + _nn(ds_b, qw_h[:, 0:128])
                    dqts.append(_nn(kt2, ds_b))
                    dcq = dcq + jnp.where(sub8 == head, jnp.sum(ds, axis=0, keepdims=True), 0.0)
                    dck = dck - jnp.where(lane == head, jnp.sum(ds, axis=1, keepdims=True), 0.0)
                dv_s[:, lanes] += dv
                dk_s[:, lanes] += dk
                dqt_ref[lanes, cols] += jnp.where(sub < HD, dqts[0], dqts[1])
            dck_s[...] += dck
            dcq_ref[:, cols] += dcq

        @pl.when(i > j)
        def _():
            step(False)

        @pl.when(i == j)
        def _():
            step(True)

        @pl.when(i == nq - 1)
        def _():
            dk_ref[...] = dk_s[...].astype(BF16)
            dv_ref[...] = dv_s[...].astype(BF16)
            dck_ref[...] = dck_s[...]

    kj = lambda r, t: _fold_cols(r, t, nq)[0]
    qi = lambda r, t: _fold_cols(r, t, nq)[1]
    krow = lambda g, r, t: (kj(r, t), g)
    qrow = lambda g, r, t: (qi(r, t), g)
    return pl.pallas_call(
        body,
        name="attn_bwd",
        grid=_fold_grid(nq),
        in_specs=[
            pl.BlockSpec((tk, PAIRS * 256), krow),
            pl.BlockSpec((tk, PAIRS * 256), krow),
            pl.BlockSpec((PAIRS * 128, tk), lambda g, r, t: (g, kj(r, t))),
            pl.BlockSpec((tq, PAIRS * 512), qrow),
            pl.BlockSpec((tq, PAIRS * 512), qrow),
            pl.BlockSpec(after.shape, lambda g, r, t: (0,) * after.ndim),
        ],
        out_specs=[
            pl.BlockSpec((PAIRS * 128, T), lambda g, r, t: (g, 0), pipeline_mode=pl.Buffered(1)),
            pl.BlockSpec((8, T), lambda g, r, t: (0, 0), pipeline_mode=pl.Buffered(1)),
            pl.BlockSpec((tk, PAIRS * 128), krow),
            pl.BlockSpec((tk, PAIRS * 128), krow),
            pl.BlockSpec((tk, 128), lambda g, r, t: (kj(r, t), 0)),
        ],
        out_shape=[jax.ShapeDtypeStruct((AW, T), F32), jax.ShapeDtypeStruct((8, T), F32),
                   jax.ShapeDtypeStruct((T, AW), BF16), jax.ShapeDtypeStruct((T, AW), BF16),
                   jax.ShapeDtypeStruct((T, FPAD), F32)],
        scratch_shapes=[pltpu.VMEM((tk, PAIRS * 128), F32), pltpu.VMEM((tk, PAIRS * 128), F32),
                        pltpu.VMEM((tk, 128), F32)],
        compiler_params=_params(3),
    )(kk, vk, kt, qw, dow, after)


def _fgate_bwd(dcq, dck, fz, fb):
    T = dck.shape[0]
    tb = _tile(T, 512)
    nb = T // tb

    def body(dcq_ref, dck_ref, fz_ref, fb_ref, df_ref, dfb_ref, carry):
        @pl.when(pl.program_id(0) == 0)
        def _():
            carry[...] = jnp.zeros_like(carry)
            dfb_ref[...] = jnp.zeros_like(dfb_ref)

        head = lax.broadcasted_iota(jnp.int32, (8, FPAD), 0)
        eye = jnp.where(head == lax.broadcasted_iota(jnp.int32, (8, FPAD), 1), 1.0, 0.0)
        dcv = dck_ref[...] + lax.dot_general(dcq_ref[...], eye, (((0,), (0,)), ((), ())),
                                             precision=lax.Precision.HIGHEST, preferred_element_type=F32)
        r = lax.broadcasted_iota(jnp.int32, (tb, tb), 0)
        cc = lax.broadcasted_iota(jnp.int32, (tb, tb), 1)
        tri = (cc >= r).astype(F32)
        dlf = jnp.dot(tri, dcv, precision=lax.Precision.HIGHEST, preferred_element_type=F32) + carry[...]
        carry[...] = carry[...] + jnp.sum(dcv, axis=0, keepdims=True)
        lane = lax.broadcasted_iota(jnp.int32, (tb, FPAD), 1)
        df = jnp.where(lane < 8, dlf * jax.nn.sigmoid(-(fz_ref[...] + fb_ref[...])), 0.0)
        df_ref[...] = df.astype(BF16)
        dfb_ref[...] += jnp.sum(df, axis=0, keepdims=True)

    rev = pl.BlockSpec((tb, FPAD), lambda i: (nb - 1 - i, 0))
    return pl.pallas_call(
        body,
        name="fgate_bwd",
        grid=(nb,),
        in_specs=[pl.BlockSpec((8, tb), lambda i: (0, nb - 1 - i)), rev, rev, _full((1, FPAD))],
        out_specs=[rev, _full((1, FPAD))],
        out_shape=[jax.ShapeDtypeStruct((T, FPAD), BF16), jax.ShapeDtypeStruct((1, FPAD), F32)],
        scratch_shapes=[pltpu.VMEM((1, FPAD), F32)],
        compiler_params=_params(1),
    )(dcq, dck, fz, fb)


def _inproj_bwd(dqt, dk, dv, dus, dfz, wp, wqt, x, dh1, g):
    T = x.shape[0]
    tm = _tile(T, 512)

    def body(dq_ref, dk_ref, dv_ref, dus_ref, dfz_ref, w_ref, wqt_ref, x_ref, dh1_ref, g_ref, gx_ref, dg_ref):
        @pl.when(pl.program_id(0) == 0)
        def _():
            dg_ref[...] = jnp.zeros_like(dg_ref)

        da = _tn(dq_ref[...].astype(BF16), wqt_ref[...])
        da += _nt(dk_ref[...], w_ref[:, AW:2 * AW])
        da += _nt(dv_ref[...], w_ref[:, 2 * AW:NQKV])
        da += _nt(dus_ref[...], w_ref[:, NQKV:NQKV + NUS])
        da += _nt(dfz_ref[...], w_ref[:, NQKV + NUS:ZP])
        dx, dg = _rms_bwd(da, x_ref[...], g_ref[...])
        gx_ref[...] = dh1_ref[...] + dx
        dg_ref[...] += dg

    row = lambda n: pl.BlockSpec((tm, n), lambda i: (i, 0))
    return pl.pallas_call(
        body,
        name="inproj_bwd",
        grid=(T // tm,),
        in_specs=[pl.BlockSpec((AW, tm), lambda i: (0, i)), row(AW), row(AW), row(NUS), row(FPAD), _full((D, ZP)),
                  _full((AW, D)), row(D), row(D), _full((1, D))],
        out_specs=[row(D), _full((1, D))],
        out_shape=[jax.ShapeDtypeStruct((T, D), F32), jax.ShapeDtypeStruct((1, D), F32)],
        compiler_params=_params(1),
    )(dqt, dk, dv, dus, dfz, wp, wqt, x, dh1, g)


def _sq_relu(f1):
    r = jnp.maximum(f1.astype(F32), 0.0)
    return (r * r).astype(BF16)


def _wgrad(a, bs, name, a_fn=None, out_dtype=F32):
    T, K = a.shape
    tt = _tile(T, 512)
    nb = len(bs)
    narrow = out_dtype != F32

    def out_dims(b, layout):
        if layout == "t":
            return (K, b.shape[0])
        N = b.shape[1]
        if layout is None:
            return (K, N)
        return (N // layout[1], K, layout[1]) if layout[0] == "col" else (K // layout[1], layout[1], N)

    shapes = [out_dims(b, layout) for b, layout in bs]

    def body(*refs):
        a_ref, b_refs, o_refs = refs[0], refs[1:1 + nb], refs[1 + nb:1 + 2 * nb]
        accs = refs[1 + 2 * nb:] if narrow else o_refs

        @pl.when(pl.program_id(0) == 0)
        def _():
            for acc in accs:
                acc[...] = jnp.zeros_like(acc)

        av = a_ref[...] if a_fn is None else a_fn(a_ref[...])
        at = av.astype(BF16).T
        for (b, layout), b_ref, o_ref in zip(bs, b_refs, accs):
            if layout is None:
                o_ref[...] += _nn(at, b_ref[...].astype(BF16))
            elif layout == "t":
                o_ref[...] += _nt(at, b_ref[...].astype(BF16))
            elif layout[0] == "col":
                n = layout[1]
                for k in range(b.shape[1] // n):
                    o_ref[k] += _nn(at, b_ref[:, k * n:(k + 1) * n].astype(BF16))
            else:
                n = layout[1]
                bv = b_ref[...].astype(BF16)
                for k in range(K // n):
                    o_ref[k] += _nn(at[k * n:(k + 1) * n, :], bv)

        if narrow:
            @pl.when(pl.program_id(0) == T // tt - 1)
            def _():
                for o_ref, acc in zip(o_refs, accs):
                    o_ref[...] = acc[...].astype(out_dtype)

    once = lambda shape: pl.BlockSpec(shape, lambda t: (0,) * len(shape), pipeline_mode=pl.Buffered(1))
    res = pl.pallas_call(
        body,
        name=name,
        grid=(T // tt,),
        in_specs=[pl.BlockSpec((tt, K), lambda t: (t, 0))] + [
            pl.BlockSpec((b.shape[0], tt), lambda t: (0, t)) if layout == "t" else pl.BlockSpec((tt, b.shape[1]), lambda t: (t, 0))
            for b, layout in bs],
        out_specs=[once(s) for s in shapes],
        out_shape=[jax.ShapeDtypeStruct(s, out_dtype) for s in shapes],
        scratch_shapes=[pltpu.VMEM(s, F32) for s in shapes] if narrow else [],
        compiler_params=_params(1, VMEM_LIMIT_BIG),
    )(a, *[b for b, _ in bs])
    return res


def _adam_math(w, g, m, v):
    m = ADAM_B1 * m + (1.0 - ADAM_B1) * g
    v = ADAM_B2 * v + (1.0 - ADAM_B2) * (g * g)
    m_hat = m / (1.0 - ADAM_B1 ** ADAM_STEP)
    v_hat = v / (1.0 - ADAM_B2 ** ADAM_STEP)
    delta = -ADAM_LR * (m_hat / (jnp.sqrt(v_hat) + ADAM_EPS) + ADAM_WD * w)
    return delta, m, v


def _adam(parts, w, m, v, name):
    R, C = w.shape
    br = 128 if R % 128 == 0 else R

    def body(p_ref, w_ref, m_ref, v_ref, g_ref, d_ref, nm_ref, nv_ref):
        g = p_ref[0].astype(F32)
        for s in range(1, NDEV):
            g = g + p_ref[s].astype(F32)
        g_ref[...] = g
        d_ref[...], nm_ref[...], nv_ref[...] = _adam_math(w_ref[...], g, m_ref[...], v_ref[...])

    blk = pl.BlockSpec((br, C), lambda i: (i, 0))
    return pl.pallas_call(
        body,
        name=name,
        grid=(R // br,),
        in_specs=[pl.BlockSpec((NDEV, br, C), lambda i: (0, i, 0)), blk, blk, blk],
        out_specs=[blk] * 4,
        out_shape=[jax.ShapeDtypeStruct((R, C), F32)] * 4,
        compiler_params=_params(1),
    )(parts, w, m, v)


_SMALL = (("sg_w", 8 * CH * CH), ("f_bias", 8), ("sg_ln_g", SW), ("sg_ln_b", SW), ("sg_b", 8 * CH), ("att_out_g", AW),
          ("sg_out_g", SW), ("pre_mix_g", D), ("post_mix_g", D), ("pre_ffn_g", D), ("post_ffn_g", D), ("ple_gate_b", D))
_SEG = 8 * 128


def _seg_rows(size):
    return 8 * (-(-size // _SEG))


def _pack(vals, loss_acc):
    parts = []
    for name, size in _SMALL:
        flat = vals[name].reshape(-1)
        rows = _seg_rows(size)
        parts.append(jnp.pad(flat, (0, rows * 128 - size)).reshape(rows, 128))
    parts.append(loss_acc)
    return jnp.concatenate(parts, axis=0)


def _adam_small(parts, ws, ms, vs):
    n = len(_SMALL)
    names = [name for name, _ in _SMALL]

    def body(*refs):
        p_ref, w_refs, m_refs, v_refs = refs[0], refs[1:1 + n], refs[1 + n:1 + 2 * n], refs[1 + 2 * n:1 + 3 * n]
        loss_ref, outs = refs[1 + 3 * n], refs[2 + 3 * n:]
        g_all = p_ref[0]
        for s in range(1, NDEV):
            g_all = g_all + p_ref[s]
        r = 0
        for k, (name, size) in enumerate(_SMALL):
            dst = [outs[kind * n + k] for kind in range(4)]

            def update(g, idx):
                vals = (g,) + _adam_math(w_refs[k][idx], g, m_refs[k][idx], v_refs[k][idx])
                for d, val in zip(dst, vals):
                    d[idx] = val

            if name == "sg_w":
                for grp in range(8):
                    update(g_all[r + grp * CH:r + (grp + 1) * CH, :], (0, grp))
            elif name == "sg_b":
                update(g_all[r:r + 8, :], (0,))
            elif name == "f_bias":
                update(g_all[r:r + 1, 0:8], (slice(None),))
            else:
                update(jnp.concatenate([g_all[r + q:r + q + 1, :] for q in range(size // 128)], axis=1), (slice(None),))
            r += _seg_rows(size)
        loss_ref[...] = g_all[r:r + 1, 0:1] * (0.5 / D)

    arrs = [parts] + [d[name] for d in (ws, ms, vs) for name in names]
    res = pl.pallas_call(
        body,
        name="adam_small",
        in_specs=[_full(a.shape) for a in arrs],
        out_specs=[_full((1, 1))] + [_full(ws[name].shape) for _ in range(4) for name in names],
        out_shape=[jax.ShapeDtypeStruct((1, 1), F32)] + [jax.ShapeDtypeStruct(ws[name].shape, F32) for _ in range(4) for name in names],
        compiler_params=pltpu.CompilerParams(vmem_limit_bytes=VMEM_LIMIT),
    )(*arrs)
    return res[0], {name: [res[1 + kind * n + k] for kind in range(4)] for k, name in enumerate(names)}


def kernel(x, p, w_in, f_bias, sg_ln_g, sg_ln_b, sg_w, sg_b, att_out_g, sg_out_g, w_out, pre_mix_g, post_mix_g, pre_ffn_g, post_ffn_g, w_ff1, w_ff2, ple_w, ple_gate_w, ple_gate_b, loss_target, m_w_in, m_f_bias, m_sg_ln_g, m_sg_ln_b, m_sg_w, m_sg_b, m_att_out_g, m_sg_out_g, m_w_out, m_pre_mix_g, m_post_mix_g, m_pre_ffn_g, m_post_ffn_g, m_w_ff1, m_w_ff2, m_ple_w, m_ple_gate_w, m_ple_gate_b, v_w_in, v_f_bias, v_sg_ln_g, v_sg_ln_b, v_sg_w, v_sg_b, v_att_out_g, v_sg_out_g, v_w_out, v_pre_mix_g, v_post_mix_g, v_pre_ffn_g, v_post_ffn_g, v_w_ff1, v_w_ff2, v_ple_w, v_ple_gate_w, v_ple_gate_b):
    small_w = dict(sg_w=sg_w, f_bias=f_bias, sg_ln_g=sg_ln_g, sg_ln_b=sg_ln_b, sg_b=sg_b, att_out_g=att_out_g,
                   sg_out_g=sg_out_g, pre_mix_g=pre_mix_g, post_mix_g=post_mix_g, pre_ffn_g=pre_ffn_g,
                   post_ffn_g=post_ffn_g, ple_gate_b=ple_gate_b)
    small_m = dict(sg_w=m_sg_w, f_bias=m_f_bias, sg_ln_g=m_sg_ln_g, sg_ln_b=m_sg_ln_b, sg_b=m_sg_b, att_out_g=m_att_out_g,
                   sg_out_g=m_sg_out_g, pre_mix_g=m_pre_mix_g, post_mix_g=m_post_mix_g, pre_ffn_g=m_pre_ffn_g,
                   post_ffn_g=m_post_ffn_g, ple_gate_b=m_ple_gate_b)
    small_v = dict(sg_w=v_sg_w, f_bias=v_f_bias, sg_ln_g=v_sg_ln_g, sg_ln_b=v_sg_ln_b, sg_b=v_sg_b, att_out_g=v_att_out_g,
                   sg_out_g=v_sg_out_g, pre_mix_g=v_pre_mix_g, post_mix_g=v_post_mix_g, pre_ffn_g=v_pre_ffn_g,
                   post_ffn_g=v_post_ffn_g, ple_gate_b=v_ple_gate_b)
    big = dict(w_in=(w_in, m_w_in, v_w_in), w_out=(w_out, m_w_out, v_w_out), w_ff1=(w_ff1, m_w_ff1, v_w_ff1),
               w_ff2=(w_ff2, m_w_ff2, v_w_ff2), ple_w=(ple_w, m_ple_w, v_ple_w),
               ple_gate_w=(ple_gate_w, m_ple_gate_w, v_ple_gate_w))

    xt, pt, tgt = x[0], p[0, 0], loss_target[0]
    ws = W_IN_COLS // NDEV

    gw_in = _gather(w_in[0].astype(BF16), "gather_w_in")
    rest = _xchg_start([w_out[0].astype(BF16), w_ff1[0].astype(BF16), w_ff2[0].astype(BF16), ple_w[0].astype(BF16),
                        ple_gate_w[0].astype(BF16)], True, "gather_rest_start")
    win = jnp.transpose(gw_in, (1, 0, 2)).reshape(D, W_IN_COLS)
    wp = jnp.concatenate([win[:, 0:NQKV], win[:, NQKV + 8:W_IN_COLS], win[:, NQKV:NQKV + 8],
                          jnp.zeros((D, FPAD - 8), BF16)], axis=1)

    fb = jnp.pad(f_bias.astype(F32), ((0, 0), (0, FPAD - 8)))
    sgw = sg_w[0]
    sgwt = jnp.transpose(sg_w[0], (0, 2, 1))
    sgbt = jnp.transpose(sg_b[0])

    qkv, kt, vt, us, fz, ab = _inproj(xt, pre_mix_g + rest[4][0, 0], wp)
    ct, kk, vk = _fcum(fz, fb, qkv)
    yatt, lsec = _attn_fwd(qkv, kk, vt, ct)
    gw_out, gw1, gw2, gwpe, gwg = _xchg_wait(rest, True, yatt, "gather_rest_wait")
    wout = gw_out.reshape(D, D)
    wg = gwg.reshape(D, D)
    wpe = jnp.transpose(gwpe, (1, 0, 2)).reshape(PLE, D)
    h1, yb, o = _sgu_out(us, yatt, xt, sg_ln_g, sg_ln_b, sgw, sgbt, att_out_g, sg_out_g, wout, post_mix_g)
    c2b, f1b, ff, h2 = _ffn_fwd(h1, pre_ffn_g, gw1, gw2, post_ffn_g)
    dh2, dbg, loss_acc, g_g, g_pe = _ple_loss(h2, pt, tgt, wg, ple_gate_b, wpe)
    g_g = g_g.reshape(NDEV, D // NDEV, D)
    g_pe = jnp.transpose(g_pe.reshape(PLE, NDEV, D // NDEV), (1, 0, 2))

    dffb, df1, dh1, dgpostffn, dgpreffn = _ffn_bwd(dh2, ff, h1, f1b, gw1, gw2, post_ffn_g, pre_ffn_g)
    (g_1,) = _wgrad(c2b, [(df1, ("col", DFF // NDEV))], "wgrad_ff1")
    (g_2,) = _wgrad(f1b, [(dffb, ("row", DFF // NDEV))], "wgrad_ff2", a_fn=_sq_relu)
    early = _xchg_start([g_1, g_2, g_pe, g_g], False, "scatter_early_start")
    dob, qw, dow, dus, dsgw, dsgbt, dlng, dlnb, dgatt, dgsg, dgpostmix = _mix_bwd(
        dh1, o, us, yatt, qkv, lsec, sg_ln_g, sg_ln_b, sgw, sgwt, sgbt, att_out_g, sg_out_g, wout,
        post_mix_g + early[4][0, 0])
    (g_out,) = _wgrad(yb, [(dob, ("row", D // NDEV))], "wgrad_out")
    mid = _xchg_start([g_out], False, "scatter_mid_start")
    dqt, dcq, dk, dv, dck = _attn_bwd(kk, vk, kt, qw, dow, mid[4])
    dfz, dfb = _fgate_bwd(dcq, dck, fz, fb)

    gq, gk, gv, gus, gf = _wgrad(ab, [(dqt, "t"), (dk, None), (dv, None), (dus, None), (dfz, None)], "wgrad_in",
                                 out_dtype=BF16)
    g_in = jnp.concatenate([gq, gk, gv, gf[:, 0:8], gus], axis=1)
    g_in = jnp.transpose(g_in.reshape(D, NDEV, ws), (1, 0, 2))
    late = _xchg_start([g_in], False, "scatter_late_start")
    grad_x, dgpremix = _inproj_bwd(dqt, dk, dv, dus, dfz, wp, jnp.transpose(wp[:, 0:AW]), xt, dh1,
                                   pre_mix_g + late[4][0, 0])

    small_g = dict(sg_w=dsgw, f_bias=dfb[:, 0:8], sg_ln_g=dlng, sg_ln_b=dlnb, sg_b=jnp.transpose(dsgbt[:, 0:8]),
                   att_out_g=dgatt, sg_out_g=dgsg, pre_mix_g=dgpremix, post_mix_g=dgpostmix, pre_ffn_g=dgpreffn,
                   post_ffn_g=dgpostffn, ple_gate_b=dbg)

    r_1, r_2, r_pe, r_g = _xchg_wait(early, False, grad_x, "scatter_early_wait")
    (r_out,) = _xchg_wait(mid, False, grad_x, "scatter_mid_wait")
    r_small = _gather(_pack(small_g, loss_acc), "gather_small_grads")

    res = {}

    def adam_big(name, parts):
        w, m, v = big[name]
        res[name] = [t[None] for t in _adam(parts, w[0], m[0], v[0], "adam_" + name)]

    for name, parts in (("w_out", r_out), ("w_ff1", r_1), ("w_ff2", r_2), ("ple_w", r_pe), ("ple_gate_w", r_g)):
        adam_big(name, parts)
    (r_in,) = _xchg_wait(late, False, res["w_ff1"][0], "scatter_late_wait")
    adam_big("w_in", r_in)
    loss, small = _adam_small(r_small, small_w, small_m, small_v)
    res.update(small)

    order = ["w_in", "f_bias", "sg_ln_g", "sg_ln_b", "sg_w", "sg_b", "att_out_g", "sg_out_g", "w_out", "pre_mix_g",
             "post_mix_g", "pre_ffn_g", "post_ffn_g", "w_ff1", "w_ff2", "ple_w", "ple_gate_w", "ple_gate_b"]
    outs = [loss[0, 0], grad_x[None]]
    for kind in range(4):
        outs += [res[name][kind] for name in order]
    return tuple(outs)
```

```python
import jax
import jax.numpy as jnp
from jax import lax
from jax.experimental import pallas as pl
from jax.experimental.pallas import tpu as pltpu

F32 = jnp.float32
BF16 = jnp.bfloat16

NDEV = 8
D = 1024
AW = 512
SW = 512
HD = 64
CH = 128
DFF = 4096
PLE = 256
NQKV = 3 * AW
NUS = 2 * SW
FPAD = 128
ZP = NQKV + NUS + FPAD
W_IN_COLS = 2568
EPS = 1e-6
MASKV = -1e30
GELU_K = 0.7978845608028654
GELU_C = 0.044715

ADAM_LR = 0.001
ADAM_B1 = 0.9
ADAM_B2 = 0.999
ADAM_EPS = 1e-08
ADAM_WD = 0.01
ADAM_STEP = 10

VMEM_LIMIT = 48 * 1024 * 1024
VMEM_LIMIT_BIG = 60 * 1024 * 1024


def _nn(a, b):
    return jnp.dot(a, b, preferred_element_type=F32)


def _nt(a, b):
    return lax.dot_general(a, b, (((1,), (1,)), ((), ())), preferred_element_type=F32)


def _tn(a, b):
    return lax.dot_general(a, b, (((0,), (0,)), ((), ())), preferred_element_type=F32)


def _tile(n, pref):
    return min(n, pref)


def _params(n_axes, vmem=VMEM_LIMIT):
    return pltpu.CompilerParams(dimension_semantics=("arbitrary",) * n_axes, vmem_limit_bytes=vmem)


def _full(shape):
    nd = len(shape)
    return pl.BlockSpec(shape, lambda *_: (0,) * nd)


def _rms_fwd(x, g):
    r = lax.rsqrt(jnp.mean(x * x, axis=-1, keepdims=True) + EPS)
    return x * r * g


def _rms_bwd(dy, x, g):
    n = x.shape[-1]
    r = lax.rsqrt(jnp.mean(x * x, axis=-1, keepdims=True) + EPS)
    u = dy * g
    s = jnp.sum(x * u, axis=-1, keepdims=True)
    dx = r * u - x * (r * r * r * (s * (1.0 / n)))
    dg = jnp.sum(dy * (x * r), axis=0, keepdims=True)
    return dx, dg


def _gelu(x):
    t = jnp.tanh(x * (GELU_K + (GELU_K * GELU_C) * (x * x)))
    return x * (0.5 + 0.5 * t), t


def _gelu_grad(x, t):
    return (0.5 + 0.5 * t) + (0.5 * x) * (1.0 - t * t) * (GELU_K + (3.0 * GELU_K * GELU_C) * (x * x))


def _gather(arr, name):
    def body(x_ref, out_ref, send, recv, loc):
        x, y, c = lax.axis_index("x"), lax.axis_index("y"), lax.axis_index("c")
        me, sibling = (x, y, c), (x, y, 1 - c)
        chips = [(1 - x, y), (x, 1 - y), (1 - x, 1 - y)]

        def slot(px, py, pc):
            return out_ref.at[4 * px + 2 * py + pc]

        def copy(k, block, to, src=None):
            return pltpu.make_async_remote_copy(
                src_ref=slot(*block) if src is None else src, dst_ref=slot(*block), send_sem=send.at[k],
                recv_sem=recv.at[k], device_id=to, device_id_type=pl.DeviceIdType.MESH)

        mine = pltpu.make_async_copy(x_ref, slot(*me), loc)
        mine.start()
        first = [copy(0, me, sibling, src=x_ref)] + [copy(1 + j, me, (*chip, c), src=x_ref) for j, chip in enumerate(chips)]
        for cp in first:
            cp.start()
        passed = [copy(4 + j, (*chip, c), sibling) for j, chip in enumerate(chips)]
        for j, chip in enumerate(chips):
            copy(1 + j, (*chip, c), me).wait_recv()
            passed[j].start()
        copy(0, sibling, me).wait_recv()
        for j, chip in enumerate(chips):
            copy(4 + j, (*chip, 1 - c), me).wait_recv()
        for cp in first + passed:
            cp.wait_send()
        mine.wait()

    hbm = pl.BlockSpec(memory_space=pltpu.HBM)
    return pl.pallas_call(
        body,
        name=name,
        out_shape=jax.ShapeDtypeStruct((NDEV,) + arr.shape, arr.dtype),
        in_specs=[hbm],
        out_specs=hbm,
        scratch_shapes=[pltpu.SemaphoreType.DMA((NDEV - 1,)), pltpu.SemaphoreType.DMA((NDEV - 1,)), pltpu.SemaphoreType.DMA],
    )(arr)


def _peers(x, y, c):
    out = []
    for k in range(1, NDEV):
        out.append((1 - x if (k >> 2) & 1 else x, 1 - y if (k >> 1) & 1 else y, 1 - c if k & 1 else c))
    return out


def _xchg_start(arrs, gather, name):
    n = len(arrs)
    me = 4 * lax.axis_index("x") + 2 * lax.axis_index("y") + lax.axis_index("c")
    lands = []
    for a in arrs:
        shape = ((NDEV,) + a.shape) if gather else a.shape
        own = a[None] if gather else lax.dynamic_slice_in_dim(a, me, 1, axis=0)
        lands.append(lax.dynamic_update_slice_in_dim(lax.empty(shape, a.dtype), own, me, axis=0))

    def body(*refs):
        ins, lnd = refs[:n], refs[n:2 * n]
        send, recv, token = refs[2 * n:3 * n], refs[3 * n:4 * n], refs[-1]
        x, y, c = lax.axis_index("x"), lax.axis_index("y"), lax.axis_index("c")
        mine = 4 * x + 2 * y + c
        for px, py, pc in _peers(x, y, c):
            peer = 4 * px + 2 * py + pc
            for a in range(n):
                pltpu.make_async_remote_copy(
                    src_ref=ins[a] if gather else ins[a].at[peer],
                    dst_ref=lnd[a].at[mine],
                    send_sem=send[a],
                    recv_sem=recv[a],
                    device_id=(px, py, pc),
                    device_id_type=pl.DeviceIdType.MESH,
                ).start()
        token[...] = jnp.zeros_like(token)

    hbm = pl.BlockSpec(memory_space=pltpu.HBM)
    sem = pl.BlockSpec(memory_space=pltpu.SEMAPHORE)
    res = pl.pallas_call(
        body,
        name=name,
        out_shape=(*[pltpu.SemaphoreType.DMA(())] * (2 * n),
                   *[pltpu.HBM(a.shape, a.dtype) for a in arrs], *[pltpu.HBM(l.shape, l.dtype) for l in lands],
                   jax.ShapeDtypeStruct((8, 128), F32)),
        in_specs=[hbm] * (2 * n),
        out_specs=(*([sem] * (2 * n)), *([hbm] * (2 * n)), pl.BlockSpec(memory_space=pltpu.VMEM)),
        input_output_aliases={i: 2 * n + i for i in range(2 * n)},
        compiler_params=pltpu.CompilerParams(has_side_effects=pltpu.SideEffectType.DATAFLOW_SIDE_EFFECTING),
    )(*[pltpu.with_memory_space_constraint(a, pltpu.HBM) for a in arrs],
      *[pltpu.with_memory_space_constraint(l, pltpu.HBM) for l in lands])
    return list(res[0:n]), list(res[n:2 * n]), list(res[2 * n:3 * n]), list(res[3 * n:4 * n]), res[-1]


def _xchg_wait(started, gather, after, name):
    send, recv, srcs, lands, _ = started
    n = len(srcs)

    def body(*refs):
        lnd = refs[n:2 * n]
        send, recv = refs[2 * n:3 * n], refs[3 * n:4 * n]
        me = (lax.axis_index("x"), lax.axis_index("y"), lax.axis_index("c"))
        for a in range(n):
            seven = lnd[a].at[pl.ds(0, NDEV - 1)]
            cp = pltpu.make_async_remote_copy(src_ref=seven, dst_ref=seven, send_sem=send[a], recv_sem=recv[a],
                                              device_id=me, device_id_type=pl.DeviceIdType.MESH)
            cp.wait_send()
            cp.wait_recv()

    hbm = pl.BlockSpec(memory_space=pltpu.HBM)
    sem = pl.BlockSpec(memory_space=pltpu.SEMAPHORE)
    res = pl.pallas_call(
        body,
        name=name,
        out_shape=tuple([pltpu.HBM(a.shape, a.dtype) for a in srcs] + [pltpu.HBM(l.shape, l.dtype) for l in lands]),
        in_specs=[hbm] * (2 * n) + [sem] * (2 * n) + [pl.BlockSpec(memory_space=pl.ANY)],
        out_specs=tuple([hbm] * (2 * n)),
        input_output_aliases={i: i for i in range(2 * n)},
        compiler_params=pltpu.CompilerParams(has_side_effects=pltpu.SideEffectType.DATAFLOW_SIDE_EFFECTING),
    )(*srcs, *lands, *send, *recv, after)
    return list(res[n:])


def _inproj(x, g, wp):
    T = x.shape[0]
    tm = _tile(T, 512)

    def body(x_ref, g_ref, w_ref, qkv_ref, kt_ref, vt_ref, us_ref, fz_ref, ab_ref):
        a = _rms_fwd(x_ref[...], g_ref[...]).astype(BF16)
        ab_ref[...] = a
        qkv_ref[:, 0:AW] = _nn(a, w_ref[:, 0:AW]).astype(BF16)
        kk = _nn(a, w_ref[:, AW:2 * AW])
        qkv_ref[:, AW:2 * AW] = kk.astype(BF16)
        kt_ref[...] = kk.T.astype(BF16)
        vv = _nn(a, w_ref[:, 2 * AW:NQKV])
        qkv_ref[:, 2 * AW:NQKV] = vv.astype(BF16)
        vt_ref[...] = vv.T.astype(BF16)
        us_ref[...] = _nn(a, w_ref[:, NQKV:NQKV + NUS])
        fz_ref[...] = _nn(a, w_ref[:, NQKV + NUS:ZP])

    row = lambda n: pl.BlockSpec((tm, n), lambda i: (i, 0))
    col = pl.BlockSpec((AW, tm), lambda i: (0, i))
    return pl.pallas_call(
        body,
        name="inproj",
        grid=(T // tm,),
        in_specs=[row(D), _full((1, D)), _full((D, ZP))],
        out_specs=[row(NQKV), col, col, row(NUS), row(FPAD), row(D)],
        out_shape=[
            jax.ShapeDtypeStruct((T, NQKV), BF16),
            jax.ShapeDtypeStruct((AW, T), BF16),
            jax.ShapeDtypeStruct((AW, T), BF16),
            jax.ShapeDtypeStruct((T, NUS), F32),
            jax.ShapeDtypeStruct((T, FPAD), F32),
            jax.ShapeDtypeStruct((T, D), BF16),
        ],
        compiler_params=_params(1),
    )(x, g, wp)


def _log_sigmoid(z):
    return jnp.minimum(z, 0.0) - jnp.log1p(jnp.exp(-jnp.abs(z)))


def _split3(x):
    hi = x.astype(BF16)
    r1 = x - hi.astype(F32)
    mid = r1.astype(BF16)
    lo = (r1 - mid.astype(F32)).astype(BF16)
    return hi, mid, lo


AUG_A, AUG_B, AUG_ONE = 0, 3, 6


def _aug_lanes(rows, first_one, pieces):
    lane = lax.broadcasted_iota(jnp.int32, (rows, 128), 1)
    out = jnp.zeros((rows, 128), F32)
    if first_one is not None:
        out = jnp.where((lane >= first_one) & (lane < first_one + 3), 1.0, out)
    for n, piece in enumerate(pieces):
        out = jnp.where(lane == AUG_ONE + n, piece.astype(F32), out)
    return out.astype(BF16)


def _fcum(fz, fb, qkv):
    T = fz.shape[0]
    tb = _tile(T, 512)

    def body(fz_ref, fb_ref, k_ref, v_ref, ct_ref, kk_ref, vk_ref, carry, tri_s, sel_s):
        @pl.when(pl.program_id(0) == 0)
        def _():
            carry[...] = jnp.zeros_like(carry)
            src = lax.broadcasted_iota(jnp.int32, (128, 128), 0)
            dst = lax.broadcasted_iota(jnp.int32, (128, 128), 1)
            tri_s[...] = (dst <= src).astype(F32)
            for hp in range(4):
                for n in range(3):
                    pick = ((src == 2 * hp) & (dst == AUG_A + n)) | ((src == 2 * hp + 1) & (dst == AUG_B + n))
                    sel_s[3 * hp + n] = jnp.where(pick, -1.0, 0.0).astype(BF16)

        lf = _log_sigmoid(fz_ref[...] + fb_ref[...])
        run = carry[...]
        parts = []
        for blk in range(tb // 128):
            lf_b = lf[blk * 128:(blk + 1) * 128, :]
            parts.append(jnp.dot(tri_s[...], lf_b, precision=lax.Precision.HIGHEST, preferred_element_type=F32) + run)
            run = run + jnp.sum(lf_b, axis=0, keepdims=True)
        carry[...] = run
        cs = jnp.concatenate(parts, axis=0)
        ct_ref[...] = cs.T[0:8, :]

        pieces = _split3(cs)
        lane = lax.broadcasted_iota(jnp.int32, (1, 128), 1)
        ones = jnp.where((lane >= AUG_ONE) & (lane < AUG_ONE + 3), 1.0, 0.0)
        for hp in range(4):
            aug = jnp.zeros((tb, 128), F32) + ones
            for n, piece in enumerate(pieces):
                aug = aug + _nn(piece, sel_s[3 * hp + n])
            aug = aug.astype(BF16)
            kk_ref[:, hp * 256:hp * 256 + 128] = k_ref[:, hp * 128:(hp + 1) * 128]
            kk_ref[:, hp * 256 + 128:(hp + 1) * 256] = aug
            vk_ref[:, hp * 256:hp * 256 + 128] = v_ref[:, hp * 128:(hp + 1) * 128]
            vk_ref[:, hp * 256 + 128:(hp + 1) * 256] = aug

    row = lambda n, c: pl.BlockSpec((tb, n), lambda i: (i, c))
    return pl.pallas_call(
        body,
        name="fcum",
        grid=(T // tb,),
        in_specs=[row(FPAD, 0), _full((1, FPAD)), row(AW, 1), row(AW, 2)],
        out_specs=[pl.BlockSpec((8, tb), lambda i: (0, i)), row(2 * AW, 0), row(2 * AW, 0)],
        out_shape=[jax.ShapeDtypeStruct((8, T), F32), jax.ShapeDtypeStruct((T, 2 * AW), BF16),
                   jax.ShapeDtypeStruct((T, 2 * AW), BF16)],
        scratch_shapes=[pltpu.VMEM((1, FPAD), F32), pltpu.VMEM((128, 128), F32), pltpu.VMEM((12, 128, 128), BF16)],
        compiler_params=_params(1),
    )(fz, fb, qkv, qkv)


def _fold_rows(r, t, nq):
    if nq == 1:
        return r, t
    low = t <= r
    return jnp.where(low, r, nq - 1 - r), jnp.where(low, t, t - r - 1)


def _fold_cols(r, t, nq):
    if nq == 1:
        return r, t
    first = t < nq - r
    j = jnp.where(first, r, nq - 1 - r)
    return j, jnp.where(first, r + t, nq - 1 - r + (t - (nq - r)))


PAIRS = 4


def _fold_grid(nq):
    assert (nq == 1 or nq % 2 == 0) and PAIRS == 4
    return (4 // PAIRS, 1, 1) if nq == 1 else (4 // PAIRS, nq // 2, nq + 1)


def _head_rows(x2, hh, scale):
    is_a = lax.broadcasted_iota(jnp.int32, (1, 128), 1) < HD
    keep = is_a if hh == 0 else jnp.logical_not(is_a)
    return jnp.where(keep, x2, jnp.zeros_like(x2)) * scale


def _attn_fwd(qkv, kk, vt, ct):
    T = qkv.shape[0]
    tq = _tile(T, 512)
    tk = tq
    nq = T // tq

    def body(q_ref, kk_ref, vt_ref, ctq_ref, o_ref, lsec_ref, qw_s, m_s, l_s, acc_s):
        i, j = _fold_rows(pl.program_id(1), pl.program_id(2), nq)
        sub8 = lax.broadcasted_iota(jnp.int32, (8, 1), 0)

        def cref_of(pp, hh):
            head = 2 * (PAIRS * pl.program_id(0) + pp) + hh
            return jnp.sum(jnp.where(sub8 == head, ctq_ref[:, 0:1], 0.0), axis=0, keepdims=True)

        @pl.when(j == 0)
        def _():
            for pp in range(PAIRS):
                q2 = q_ref[:, pp * 128:(pp + 1) * 128]
                for hh in range(2):
                    rows = slice(hh * tq, (hh + 1) * tq)
                    qw_s[pp, rows, 0:128] = _head_rows(q2, hh, 0.125)
                    qw_s[pp, rows, 128:256] = _aug_lanes(tq, AUG_A if hh == 0 else AUG_B, _split3(cref_of(pp, hh)))
            m_s[...] = jnp.full_like(m_s, MASKV)
            l_s[...] = jnp.zeros_like(l_s)
            acc_s[...] = jnp.zeros_like(acc_s)

        def step(masked):
            if masked:
                causal = lax.broadcasted_iota(jnp.int32, (tk, tq), 0) <= lax.broadcasted_iota(jnp.int32, (tk, tq), 1)
            logits = lambda pp: _nt(kk_ref[:, pp * 256:(pp + 1) * 256], qw_s[pp])
            s_next = logits(0)
            for pp in range(PAIRS):
                s2 = s_next
                if pp + 1 < PAIRS:
                    s_next = logits(pp + 1)
                vt2 = vt_ref[pp * 128:(pp + 1) * 128, :]
                for hh in range(2):
                    n = 2 * pp + hh
                    s = s2[:, hh * tq:(hh + 1) * tq]
                    if masked:
                        s = jnp.where(causal, s, MASKV)
                    m_prev = m_s[n]
                    m_new = jnp.maximum(m_prev, jnp.max(s, axis=0, keepdims=True))
                    pr = jnp.exp(s - m_new)
                    alpha = jnp.exp(m_prev - m_new)
                    l_s[n] = alpha * l_s[n] + jnp.sum(pr, axis=0, keepdims=True)
                    m_s[n] = m_new
                    acc_s[n] = alpha * acc_s[n] + _nn(vt2, pr.astype(BF16))

        @pl.when(j < i)
        def _():
            step(False)

        @pl.when(j == i)
        def _():
            step(True)
            sub = lax.broadcasted_iota(jnp.int32, (128, 1), 0)
            for pp in range(PAIRS):
                a, b = 2 * pp, 2 * pp + 1
                ot = jnp.where(sub < HD, acc_s[a] * (1.0 / l_s[a]), acc_s[b] * (1.0 / l_s[b]))
                o_ref[:, pp * 128:(pp + 1) * 128] = ot.T
                lrow = [m_s[2 * pp + hh] + jnp.log(l_s[2 * pp + hh]) - cref_of(pp, hh) for hh in range(2)]
                lsec_ref[:, pp * 128:(pp + 1) * 128] = jnp.where(sub == 0, lrow[0], jnp.where(sub == 1, lrow[1], 0.0)).T

    qi = lambda r, t: _fold_rows(r, t, nq)[0]
    kj = lambda r, t: _fold_rows(r, t, nq)[1]
    return pl.pallas_call(
        body,
        name="attn_fwd",
        grid=_fold_grid(nq),
        in_specs=[
            pl.BlockSpec((tq, PAIRS * 128), lambda g, r, t: (qi(r, t), g)),
            pl.BlockSpec((tk, PAIRS * 256), lambda g, r, t: (kj(r, t), g)),
            pl.BlockSpec((PAIRS * 128, tk), lambda g, r, t: (g, kj(r, t))),
            pl.BlockSpec((8, tq), lambda g, r, t: (0, qi(r, t))),
        ],
        out_specs=[pl.BlockSpec((tq, PAIRS * 128), lambda g, r, t: (qi(r, t), g))] * 2,
        out_shape=[jax.ShapeDtypeStruct((T, AW), F32)] * 2,
        scratch_shapes=[pltpu.VMEM((PAIRS, 2 * tq, 256), BF16), pltpu.VMEM((2 * PAIRS, 1, tq), F32),
                        pltpu.VMEM((2 * PAIRS, 1, tq), F32), pltpu.VMEM((2 * PAIRS, 128, tq), F32)],
        compiler_params=_params(3),
    )(qkv, kk, vt, ct)


def _sgu_forward(us_ref, lng, lnb, w_ref, bt_ref, mixed_s, vnb_s, tm):
    is_a = lax.broadcasted_iota(jnp.int32, (1, 128), 1) < HD
    u = us_ref[:, 0:SW]
    vs = us_ref[:, SW:NUS]
    ug, tu = _gelu(u)
    vg, tv = _gelu(vs)
    mu = jnp.mean(vg, axis=-1, keepdims=True)
    xc = vg - mu
    rstd = lax.rsqrt(jnp.mean(xc * xc, axis=-1, keepdims=True) + EPS)
    vhat = xc * rstd
    vnb_s[...] = (vhat * lng + lnb).astype(BF16)
    rr = lax.broadcasted_iota(jnp.int32, (CH, CH), 0)
    cc = lax.broadcasted_iota(jnp.int32, (CH, CH), 1)
    tril = cc <= rr
    for jj in range(4):
        wa = jnp.where(tril, w_ref[2 * jj], 0.0).astype(BF16)
        wb = jnp.where(tril, w_ref[2 * jj + 1], 0.0).astype(BF16)
        ba = bt_ref[:, 2 * jj:2 * jj + 1]
        bb = bt_ref[:, 2 * jj + 1:2 * jj + 2]
        for ch in range(tm // CH):
            rs, cs = slice(ch * CH, (ch + 1) * CH), slice(jj * 128, (jj + 1) * 128)
            vn2 = vnb_s[rs, cs]
            mixed_s[rs, cs] = jnp.where(is_a, _nn(wa, vn2) + ba, _nn(wb, vn2) + bb)
    mixed = mixed_s[...]
    return u, vs, ug, tu, tv, vhat, rstd, mixed, ug * mixed


def _sgu_out(us, yatt, x, lng, lnb, sgw, sgbt, gatt, gsg, wout, gpm):
    T = us.shape[0]
    tm = _tile(T, 512)

    def body(us_ref, ya_ref, x_ref, lng_ref, lnb_ref, w_ref, bt_ref, ga_ref, gs_ref, wo_ref, gp_ref,
             h1_ref, yb_ref, o_ref, mixed_s, vnb_s):
        ysg = _sgu_forward(us_ref, lng_ref[...], lnb_ref[...], w_ref, bt_ref, mixed_s, vnb_s, tm)[-1]
        yb_ref[:, 0:AW] = _rms_fwd(ya_ref[...], ga_ref[...]).astype(BF16)
        yb_ref[:, AW:D] = _rms_fwd(ysg, gs_ref[...]).astype(BF16)
        o = _nn(yb_ref[...], wo_ref[...])
        o_ref[...] = o
        h1_ref[...] = x_ref[...] + _rms_fwd(o, gp_ref[...])

    row = lambda n: pl.BlockSpec((tm, n), lambda i: (i, 0))
    return pl.pallas_call(
        body,
        name="sgu_out",
        grid=(T // tm,),
        in_specs=[row(NUS), row(AW), row(D), _full((1, SW)), _full((1, SW)), _full((8, CH, CH)), _full((CH, 8)),
                  _full((1, AW)), _full((1, SW)), _full((D, D)), _full((1, D))],
        out_specs=[row(D), row(D), row(D)],
        out_shape=[jax.ShapeDtypeStruct((T, D), F32), jax.ShapeDtypeStruct((T, D), BF16), jax.ShapeDtypeStruct((T, D), F32)],
        scratch_shapes=[pltpu.VMEM((tm, SW), F32), pltpu.VMEM((tm, SW), BF16)],
        compiler_params=_params(1),
    )(us, yatt, x, lng, lnb, sgw, sgbt, gatt, gsg, wout, gpm)


def _ffn_fwd(h1, gpre, w1g, w2g, gpost):
    T = h1.shape[0]
    tm = _tile(T, 512)
    nb, hb = w1g.shape[0], w1g.shape[2]

    def body(h1_ref, gpre_ref, w1_ref, w2_ref, gpost_ref, c2_ref, f1_ref, ff_ref, h2_ref):
        h1 = h1_ref[...]
        c2 = _rms_fwd(h1, gpre_ref[...]).astype(BF16)
        c2_ref[...] = c2
        ff = jnp.zeros((tm, D), F32)
        for j in range(nb):
            f1 = _nn(c2, w1_ref[j])
            f1_ref[:, j * hb:(j + 1) * hb] = f1.astype(BF16)
            r = jnp.maximum(f1, 0.0)
            ff = ff + _nn((r * r).astype(BF16), w2_ref[j])
        ff_ref[...] = ff
        h2_ref[...] = h1 + _rms_fwd(ff, gpost_ref[...])

    row = lambda n: pl.BlockSpec((tm, n), lambda i: (i, 0))
    once = lambda shape: pl.BlockSpec(shape, lambda i: (0,) * len(shape), pipeline_mode=pl.Buffered(1))
    return pl.pallas_call(
        body,
        name="ffn_fwd",
        grid=(T // tm,),
        in_specs=[row(D), _full((1, D)), once((nb, D, hb)), once((nb, hb, D)), _full((1, D))],
        out_specs=[row(D), row(DFF), row(D), row(D)],
        out_shape=[jax.ShapeDtypeStruct((T, D), BF16), jax.ShapeDtypeStruct((T, DFF), BF16),
                   jax.ShapeDtypeStruct((T, D), F32), jax.ShapeDtypeStruct((T, D), F32)],
        compiler_params=_params(1, VMEM_LIMIT_BIG),
    )(h1, gpre, w1g, w2g, gpost)


def _ple_loss(h2, p, tgt, wg, bg, wpe):
    T = h2.shape[0]
    tm = _tile(T, 512)

    def body(h2_ref, p_ref, t_ref, wg_ref, bg_ref, wpe_ref, dh2_ref, dbg_ref, loss_ref, dwg_ref, dwpe_ref):
        @pl.when(pl.program_id(0) == 0)
        def _():
            for r in (dbg_ref, loss_ref, dwg_ref, dwpe_ref):
                r[...] = jnp.zeros_like(r)

        h2 = h2_ref[...]
        h2b = h2.astype(BF16)
        gate = jax.nn.sigmoid(_nn(h2b, wg_ref[...]) + bg_ref[...])
        pb = p_ref[...].astype(BF16)
        pe = _nn(pb, wpe_ref[...])
        diff = (h2 + gate * pe) - t_ref[...]
        loss_ref[...] += jnp.sum(diff * diff)
        dh3 = diff * (1.0 / D)
        dpre = (dh3 * pe) * (gate * (1.0 - gate))
        dpre_b = dpre.astype(BF16)
        dbg_ref[...] += jnp.sum(dpre, axis=0, keepdims=True)
        dh2_ref[...] = dh3 + _nt(dpre_b, wg_ref[...])
        dwg_ref[...] += _nn(h2b.T, dpre_b)
        dwpe_ref[...] += _nn(pb.T, (dh3 * gate).astype(BF16))

    row = lambda n: pl.BlockSpec((tm, n), lambda i: (i, 0))
    once = lambda shape: pl.BlockSpec(shape, lambda i: (0,) * len(shape), pipeline_mode=pl.Buffered(1))
    return pl.pallas_call(
        body,
        name="ple_loss",
        grid=(T // tm,),
        in_specs=[row(D), row(PLE), row(D), _full((D, D)), _full((1, D)), _full((PLE, D))],
        out_specs=[row(D), _full((1, D)), _full((8, 128)), once((D, D)), once((PLE, D))],
        out_shape=[jax.ShapeDtypeStruct((T, D), F32), jax.ShapeDtypeStruct((1, D), F32),
                   jax.ShapeDtypeStruct((8, 128), F32), jax.ShapeDtypeStruct((D, D), F32),
                   jax.ShapeDtypeStruct((PLE, D), F32)],
        compiler_params=_params(1),
    )(h2, p, tgt, wg, bg, wpe)


def _ffn_bwd(dh2, ff, h1, f1, w1g, w2g, gpost, gpre):
    T = dh2.shape[0]
    tm = _tile(T, 512)
    nb, hb = w1g.shape[0], w1g.shape[2]

    def body(dh2_ref, ff_ref, h1_ref, f1_ref, w1_ref, w2_ref, gpost_ref, gpre_ref,
             dffb_ref, df1_ref, dh1_ref, dgpost_ref, dgpre_ref):
        @pl.when(pl.program_id(0) == 0)
        def _():
            dgpost_ref[...] = jnp.zeros_like(dgpost_ref)
            dgpre_ref[...] = jnp.zeros_like(dgpre_ref)

        dh2 = dh2_ref[...]
        dff, dg = _rms_bwd(dh2, ff_ref[...], gpost_ref[...])
        dffb = dff.astype(BF16)
        dffb_ref[...] = dffb
        dgpost_ref[...] += dg
        dc2 = jnp.zeros((tm, D), F32)
        for j in range(nb):
            cols = slice(j * hb, (j + 1) * hb)
            dact = _nt(dffb, w2_ref[j])
            df1 = (dact * (2.0 * jnp.maximum(f1_ref[:, cols].astype(F32), 0.0))).astype(BF16)
            df1_ref[:, cols] = df1
            dc2 = dc2 + _nt(df1, w1_ref[j])
        dx, dg = _rms_bwd(dc2, h1_ref[...], gpre_ref[...])
        dh1_ref[...] = dh2 + dx
        dgpre_ref[...] += dg

    row = lambda n: pl.BlockSpec((tm, n), lambda i: (i, 0))
    once = lambda shape: pl.BlockSpec(shape, lambda i: (0,) * len(shape), pipeline_mode=pl.Buffered(1))
    return pl.pallas_call(
        body,
        name="ffn_bwd",
        grid=(T // tm,),
        in_specs=[row(D), row(D), row(D), row(DFF), once((nb, D, hb)), once((nb, hb, D)), _full((1, D)), _full((1, D))],
        out_specs=[row(D), row(DFF), row(D), _full((1, D)), _full((1, D))],
        out_shape=[jax.ShapeDtypeStruct((T, D), BF16), jax.ShapeDtypeStruct((T, DFF), BF16),
                   jax.ShapeDtypeStruct((T, D), F32), jax.ShapeDtypeStruct((1, D), F32),
                   jax.ShapeDtypeStruct((1, D), F32)],
        compiler_params=_params(1, VMEM_LIMIT_BIG),
    )(dh2, ff, h1, f1, w1g, w2g, gpost, gpre)


def _mix_bwd(dh1, o, us, yatt, qkv, lsec, lng, lnb, sgw, sgwt, sgbt, gatt, gsg, wout, gpm):
    T = dh1.shape[0]
    tm = _tile(T, 512)

    def body(dh1_ref, o_ref, us_ref, ya_ref, q_ref, l_ref, lng_ref, lnb_ref, w_ref, wt_ref, bt_ref, ga_ref, gs_ref, wo_ref,
             gp_ref, dob_ref, qw_ref, dow_ref, dus_ref, dw_ref, dbt_ref, dlng_ref, dlnb_ref, dga_ref, dgs_ref, dgp_ref,
             mixed_s, vnb_s, dvn_s):
        @pl.when(pl.program_id(0) == 0)
        def _():
            for r in (dw_ref, dbt_ref, dlng_ref, dlnb_ref, dga_ref, dgs_ref, dgp_ref):
                r[...] = jnp.zeros_like(r)

        is_a = lax.broadcasted_iota(jnp.int32, (1, 128), 1) < HD
        lane = lax.broadcasted_iota(jnp.int32, (1, 128), 1)
        do, dg = _rms_bwd(dh1_ref[...], o_ref[...], gp_ref[...])
        dgp_ref[...] += dg
        dob = do.astype(BF16)
        dob_ref[...] = dob
        dy = _nt(dob, wo_ref[...])
        ya = ya_ref[...]
        datt, dg = _rms_bwd(dy[:, 0:AW], ya, ga_ref[...])
        dga_ref[...] += dg
        _attn_operands(q_ref[...], datt, ya, l_ref[...], qw_ref, dow_ref)

        lng = lng_ref[...]
        u, vs, ug, tu, tv, vhat, rstd, mixed, ysg = _sgu_forward(us_ref, lng, lnb_ref[...], w_ref, bt_ref, mixed_s, vnb_s, tm)
        dysg, dg = _rms_bwd(dy[:, AW:D], ysg, gs_ref[...])
        dgs_ref[...] += dg
        dus_ref[:, 0:SW] = ((dysg * mixed) * _gelu_grad(u, tu)).astype(BF16)
        dmix = dysg * ug

        rr = lax.broadcasted_iota(jnp.int32, (CH, CH), 0)
        cc = lax.broadcasted_iota(jnp.int32, (CH, CH), 1)
        tril = cc <= rr
        triu = cc >= rr
        for jj in range(4):
            wta = jnp.where(triu, wt_ref[2 * jj], 0.0).astype(BF16)
            wtb = jnp.where(triu, wt_ref[2 * jj + 1], 0.0).astype(BF16)
            for ch in range(tm // CH):
                rs, cs = slice(ch * CH, (ch + 1) * CH), slice(jj * 128, (jj + 1) * 128)
                dm2 = dmix[rs, cs]
                dma = jnp.where(is_a, dm2, 0.0)
                dmb = jnp.where(is_a, 0.0, dm2)
                dma_b, dmb_b = dma.astype(BF16), dmb.astype(BF16)
                vn2 = vnb_s[rs, cs]
                dw_ref[2 * jj] += jnp.where(tril, _nt(dma_b, vn2), 0.0)
                dw_ref[2 * jj + 1] += jnp.where(tril, _nt(dmb_b, vn2), 0.0)
                dvn_s[rs, cs] = _nn(wta, dma_b) + _nn(wtb, dmb_b)
                dba = jnp.sum(dma, axis=1, keepdims=True)
                dbb = jnp.sum(dmb, axis=1, keepdims=True)
                dbt_ref[...] += jnp.where(lane == 2 * jj, dba, 0.0) + jnp.where(lane == 2 * jj + 1, dbb, 0.0)

        dvn = dvn_s[...]
        dlng_ref[...] += jnp.sum(dvn * vhat, axis=0, keepdims=True)
        dlnb_ref[...] += jnp.sum(dvn, axis=0, keepdims=True)
        dvh = dvn * lng
        dvg = rstd * (dvh - jnp.mean(dvh, axis=-1, keepdims=True) - vhat * jnp.mean(dvh * vhat, axis=-1, keepdims=True))
        dus_ref[:, SW:NUS] = (dvg * _gelu_grad(vs, tv)).astype(BF16)

    row = lambda n: pl.BlockSpec((tm, n), lambda i: (i, 0))
    return pl.pallas_call(
        body,
        name="mix_bwd",
        grid=(T // tm,),
        in_specs=[row(D), row(D), row(NUS), row(AW), row(AW), row(AW), _full((1, SW)), _full((1, SW)), _full((8, CH, CH)),
                  _full((8, CH, CH)), _full((CH, 8)), _full((1, AW)), _full((1, SW)), _full((D, D)), _full((1, D))],
        out_specs=[row(D), row(4 * AW), row(4 * AW), row(NUS), _full((8, CH, CH)), _full((CH, 128)), _full((1, SW)),
                   _full((1, SW)), _full((1, AW)), _full((1, SW)), _full((1, D))],
        out_shape=[jax.ShapeDtypeStruct((T, D), BF16), jax.ShapeDtypeStruct((T, 4 * AW), BF16),
                   jax.ShapeDtypeStruct((T, 4 * AW), BF16), jax.ShapeDtypeStruct((T, NUS), BF16),
                   jax.ShapeDtypeStruct((8, CH, CH), F32), jax.ShapeDtypeStruct((CH, 128), F32),
                   jax.ShapeDtypeStruct((1, SW), F32), jax.ShapeDtypeStruct((1, SW), F32),
                   jax.ShapeDtypeStruct((1, AW), F32), jax.ShapeDtypeStruct((1, SW), F32), jax.ShapeDtypeStruct((1, D), F32)],
        scratch_shapes=[pltpu.VMEM((tm, SW), F32), pltpu.VMEM((tm, SW), BF16), pltpu.VMEM((tm, SW), F32)],
        compiler_params=_params(1, VMEM_LIMIT_BIG),
    )(dh1, o, us, yatt, qkv, lsec, lng, lnb, sgw, sgwt, sgbt, gatt, gsg, wout, gpm)


def _attn_operands(q, do, o, lsec, qw_ref, dow_ref):
    rows = do.shape[0]
    feat = lax.broadcasted_iota(jnp.int32, (AW, 128), 0)
    head = lax.broadcasted_iota(jnp.int32, (AW, 128), 1)
    sel = jnp.where((feat >= head * HD) & (feat < (head + 1) * HD), 1.0, 0.0)
    delta = jnp.dot(do * o, sel, precision=lax.Precision.HIGHEST, preferred_element_type=F32)
    for hp in range(4):
        cols = slice(hp * 128, (hp + 1) * 128)
        for hh in range(2):
            base = (2 * hp + hh) * 256
            lc = lsec[:, hp * 128 + hh:hp * 128 + hh + 1]
            d_h = delta[:, 2 * hp + hh:2 * hp + hh + 1]
            qw_ref[:, base:base + 128] = _head_rows(q[:, cols], hh, 0.125)
            qw_ref[:, base + 128:base + 256] = _aug_lanes(rows, AUG_A if hh == 0 else AUG_B, _split3(-lc))
            dow_ref[:, base:base + 128] = _head_rows(do[:, cols], hh, 1.0).astype(BF16)
            dow_ref[:, base + 128:base + 256] = _aug_lanes(rows, None, _split3(-d_h))


def _attn_bwd(kk, vk, kt, qw, dow, after):
    T = kk.shape[0]
    tq = _tile(T, 512)
    tk = tq
    nq = T // tq

    def body(kk_ref, vk_ref, kt_ref, qw_ref, dow_ref, after_ref, dqt_ref, dcq_ref, dk_ref, dv_ref, dck_ref, dk_s, dv_s, dck_s):
        j, i = _fold_cols(pl.program_id(1), pl.program_id(2), nq)
        sub8 = lax.broadcasted_iota(jnp.int32, (8, 1), 0)
        lane = lax.broadcasted_iota(jnp.int32, (1, 128), 1)

        @pl.when((pl.program_id(1) == 0) & (pl.program_id(2) == 0))
        def _():
            dqt_ref[...] = jnp.zeros_like(dqt_ref)
            dcq_ref[...] = jnp.zeros_like(dcq_ref)

        @pl.when(i == j)
        def _():
            dk_s[...] = jnp.zeros_like(dk_s)
            dv_s[...] = jnp.zeros_like(dv_s)
            dck_s[...] = jnp.zeros_like(dck_s)

        def step(masked):
            cols = pl.ds(pl.multiple_of(i * tq, tq), tq)
            sub = lax.broadcasted_iota(jnp.int32, (128, 1), 0)
            if masked:
                causal = lax.broadcasted_iota(jnp.int32, (tk, tq), 0) <= lax.broadcasted_iota(jnp.int32, (tk, tq), 1)

            def logits(n):
                pair, base = n // 2, n * 256
                return (_nt(kk_ref[:, pair * 256:(pair + 1) * 256], qw_ref[:, base:base + 256]),
                        _nt(vk_ref[:, pair * 256:(pair + 1) * 256], dow_ref[:, base:base + 256]))

            ahead = logits(0)
            dcq = jnp.zeros((8, tq), F32)
            dck = jnp.zeros((tk, 128), F32)
            for pp in range(PAIRS):
                lanes = slice(pp * 128, (pp + 1) * 128)
                kt2 = kt_ref[lanes, :] * 0.125
                dv = jnp.zeros((tk, 128), F32)
                dk = jnp.zeros((tk, 128), F32)
                dqts = []
                for hh in range(2):
                    head = 2 * pp + hh
                    base = head * 256
                    qw_h = qw_ref[:, base:base + 256]
                    dow_h = dow_ref[:, base:base + 256]
                    logp, dp = ahead
                    if head + 1 < 2 * PAIRS:
                        ahead = logits(head + 1)
                    pr = jnp.exp(logp)
                    if masked:
                        pr = jnp.where(causal, pr, 0.0)
                    ds = pr * dp
                    ds_b = ds.astype(BF16)
                    dv = dv + _nn(pr.astype(BF16), dow_h[:, 0:128])
                    dk = dk + _nn(ds_b, qw_h[:, 0:128])
                    dqts.append(_nn(kt2, ds_b))
                    dcq = dcq + jnp.where(sub8 == head, jnp.sum(ds, axis=0, keepdims=True), 0.0)
                    dck = dck - jnp.where(lane == head, jnp.sum(ds, axis=1, keepdims=True), 0.0)
                dv_s[:, lanes] += dv
                dk_s[:, lanes] += dk
                dqt_ref[lanes, cols] += jnp.where(sub < HD, dqts[0], dqts[1])
            dck_s[...] += dck
            dcq_ref[:, cols] += dcq

        @pl.when(i > j)
        def _():
            step(False)

        @pl.when(i == j)
        def _():
            step(True)

        @pl.when(i == nq - 1)
        def _():
            dk_ref[...] = dk_s[...].astype(BF16)
            dv_ref[...] = dv_s[...].astype(BF16)
            dck_ref[...] = dck_s[...]

    kj = lambda r, t: _fold_cols(r, t, nq)[0]
    qi = lambda r, t: _fold_cols(r, t, nq)[1]
    krow = lambda g, r, t: (kj(r, t), g)
    qrow = lambda g, r, t: (qi(r, t), g)
    return pl.pallas_call(
        body,
        name="attn_bwd",
        grid=_fold_grid(nq),
        in_specs=[
            pl.BlockSpec((tk, PAIRS * 256), krow),
            pl.BlockSpec((tk, PAIRS * 256), krow),
            pl.BlockSpec((PAIRS * 128, tk), lambda g, r, t: (g, kj(r, t))),
            pl.BlockSpec((tq, PAIRS * 512), qrow),
            pl.BlockSpec((tq, PAIRS * 512), qrow),
            pl.BlockSpec(after.shape, lambda g, r, t: (0,) * after.ndim),
        ],
        out_specs=[
            pl.BlockSpec((PAIRS * 128, T), lambda g, r, t: (g, 0), pipeline_mode=pl.Buffered(1)),
            pl.BlockSpec((8, T), lambda g, r, t: (0, 0), pipeline_mode=pl.Buffered(1)),
            pl.BlockSpec((tk, PAIRS * 128), krow),
            pl.BlockSpec((tk, PAIRS * 128), krow),
            pl.BlockSpec((tk, 128), lambda g, r, t: (kj(r, t), 0)),
        ],
        out_shape=[jax.ShapeDtypeStruct((AW, T), F32), jax.ShapeDtypeStruct((8, T), F32),
                   jax.ShapeDtypeStruct((T, AW), BF16), jax.ShapeDtypeStruct((T, AW), BF16),
                   jax.ShapeDtypeStruct((T, FPAD), F32)],
        scratch_shapes=[pltpu.VMEM((tk, PAIRS * 128), F32), pltpu.VMEM((tk, PAIRS * 128), F32),
                        pltpu.VMEM((tk, 128), F32)],
        compiler_params=_params(3),
    )(kk, vk, kt, qw, dow, after)


def _fgate_bwd(dcq, dck, fz, fb):
    T = dck.shape[0]
    tb = _tile(T, 512)
    nb = T // tb

    def body(dcq_ref, dck_ref, fz_ref, fb_ref, df_ref, dfb_ref, carry):
        @pl.when(pl.program_id(0) == 0)
        def _():
            carry[...] = jnp.zeros_like(carry)
            dfb_ref[...] = jnp.zeros_like(dfb_ref)

        head = lax.broadcasted_iota(jnp.int32, (8, FPAD), 0)
        eye = jnp.where(head == lax.broadcasted_iota(jnp.int32, (8, FPAD), 1), 1.0, 0.0)
        dcv = dck_ref[...] + lax.dot_general(dcq_ref[...], eye, (((0,), (0,)), ((), ())),
                                             precision=lax.Precision.HIGHEST, preferred_element_type=F32)
        r = lax.broadcasted_iota(jnp.int32, (128, 128), 0)
        cc = lax.broadcasted_iota(jnp.int32, (128, 128), 1)
        tri = (cc >= r).astype(F32)
        run = carry[...]
        parts = []
        for blk in reversed(range(tb // 128)):
            dc_b = dcv[blk * 128:(blk + 1) * 128, :]
            parts.append(jnp.dot(tri, dc_b, precision=lax.Precision.HIGHEST, preferred_element_type=F32) + run)
            run = run + jnp.sum(dc_b, axis=0, keepdims=True)
        carry[...] = run
        dlf = jnp.concatenate(parts[::-1], axis=0)
        lane = lax.broadcasted_iota(jnp.int32, (tb, FPAD), 1)
        df = jnp.where(lane < 8, dlf * jax.nn.sigmoid(-(fz_ref[...] + fb_ref[...])), 0.0)
        df_ref[...] = df.astype(BF16)
        dfb_ref[...] += jnp.sum(df, axis=0, keepdims=True)

    rev = pl.BlockSpec((tb, FPAD), lambda i: (nb - 1 - i, 0))
    return pl.pallas_call(
        body,
        name="fgate_bwd",
        grid=(nb,),
        in_specs=[pl.BlockSpec((8, tb), lambda i: (0, nb - 1 - i)), rev, rev, _full((1, FPAD))],
        out_specs=[rev, _full((1, FPAD))],
        out_shape=[jax.ShapeDtypeStruct((T, FPAD), BF16), jax.ShapeDtypeStruct((1, FPAD), F32)],
        scratch_shapes=[pltpu.VMEM((1, FPAD), F32)],
        compiler_params=_params(1),
    )(dcq, dck, fz, fb)


def _inproj_bwd(dqt, dk, dv, dus, dfz, wp, wqt, x, dh1, g):
    T = x.shape[0]
    tm = _tile(T, 512)

    def body(dq_ref, dk_ref, dv_ref, dus_ref, dfz_ref, w_ref, wqt_ref, x_ref, dh1_ref, g_ref, gx_ref, dg_ref):
        @pl.when(pl.program_id(0) == 0)
        def _():
            dg_ref[...] = jnp.zeros_like(dg_ref)

        da = _tn(dq_ref[...].astype(BF16), wqt_ref[...])
        da += _nt(dk_ref[...], w_ref[:, AW:2 * AW])
        da += _nt(dv_ref[...], w_ref[:, 2 * AW:NQKV])
        da += _nt(dus_ref[...], w_ref[:, NQKV:NQKV + NUS])
        da += _nt(dfz_ref[...], w_ref[:, NQKV + NUS:ZP])
        dx, dg = _rms_bwd(da, x_ref[...], g_ref[...])
        gx_ref[...] = dh1_ref[...] + dx
        dg_ref[...] += dg

    row = lambda n: pl.BlockSpec((tm, n), lambda i: (i, 0))
    return pl.pallas_call(
        body,
        name="inproj_bwd",
        grid=(T // tm,),
        in_specs=[pl.BlockSpec((AW, tm), lambda i: (0, i)), row(AW), row(AW), row(NUS), row(FPAD), _full((D, ZP)),
                  _full((AW, D)), row(D), row(D), _full((1, D))],
        out_specs=[row(D), _full((1, D))],
        out_shape=[jax.ShapeDtypeStruct((T, D), F32), jax.ShapeDtypeStruct((1, D), F32)],
        compiler_params=_params(1),
    )(dqt, dk, dv, dus, dfz, wp, wqt, x, dh1, g)


def _sq_relu(f1):
    r = jnp.maximum(f1.astype(F32), 0.0)
    return (r * r).astype(BF16)


def _wgrad(a, bs, name, a_fn=None, out_dtype=F32):
    T, K = a.shape
    tt = _tile(T, 512)
    nb = len(bs)
    narrow = out_dtype != F32

    def out_dims(b, layout):
        if layout == "t":
            return (K, b.shape[0])
        N = b.shape[1]
        if layout is None:
            return (K, N)
        return (N // layout[1], K, layout[1]) if layout[0] == "col" else (K // layout[1], layout[1], N)

    shapes = [out_dims(b, layout) for b, layout in bs]

    def body(*refs):
        a_ref, b_refs, o_refs = refs[0], refs[1:1 + nb], refs[1 + nb:1 + 2 * nb]
        accs = refs[1 + 2 * nb:] if narrow else o_refs

        @pl.when(pl.program_id(0) == 0)
        def _():
            for acc in accs:
                acc[...] = jnp.zeros_like(acc)

        av = a_ref[...] if a_fn is None else a_fn(a_ref[...])
        at = av.astype(BF16).T
        for (b, layout), b_ref, o_ref in zip(bs, b_refs, accs):
            if layout is None:
                o_ref[...] += _nn(at, b_ref[...].astype(BF16))
            elif layout == "t":
                o_ref[...] += _nt(at, b_ref[...].astype(BF16))
            elif layout[0] == "col":
                n = layout[1]
                for k in range(b.shape[1] // n):
                    o_ref[k] += _nn(at, b_ref[:, k * n:(k + 1) * n].astype(BF16))
            else:
                n = layout[1]
                bv = b_ref[...].astype(BF16)
                for k in range(K // n):
                    o_ref[k] += _nn(at[k * n:(k + 1) * n, :], bv)

        if narrow:
            @pl.when(pl.program_id(0) == T // tt - 1)
            def _():
                for o_ref, acc in zip(o_refs, accs):
                    o_ref[...] = acc[...].astype(out_dtype)

    once = lambda shape: pl.BlockSpec(shape, lambda t: (0,) * len(shape), pipeline_mode=pl.Buffered(1))
    res = pl.pallas_call(
        body,
        name=name,
        grid=(T // tt,),
        in_specs=[pl.BlockSpec((tt, K), lambda t: (t, 0))] + [
            pl.BlockSpec((b.shape[0], tt), lambda t: (0, t)) if layout == "t" else pl.BlockSpec((tt, b.shape[1]), lambda t: (t, 0))
            for b, layout in bs],
        out_specs=[once(s) for s in shapes],
        out_shape=[jax.ShapeDtypeStruct(s, out_dtype) for s in shapes],
        scratch_shapes=[pltpu.VMEM(s, F32) for s in shapes] if narrow else [],
        compiler_params=_params(1, VMEM_LIMIT_BIG),
    )(a, *[b for b, _ in bs])
    return res


def _adam_math(w, g, m, v):
    m = ADAM_B1 * m + (1.0 - ADAM_B1) * g
    v = ADAM_B2 * v + (1.0 - ADAM_B2) * (g * g)
    m_hat = m / (1.0 - ADAM_B1 ** ADAM_STEP)
    v_hat = v / (1.0 - ADAM_B2 ** ADAM_STEP)
    delta = -ADAM_LR * (m_hat / (jnp.sqrt(v_hat) + ADAM_EPS) + ADAM_WD * w)
    return delta, m, v


def _adam(parts, w, m, v, name):
    R, C = w.shape
    br = 128 if R % 128 == 0 else R

    def body(p_ref, w_ref, m_ref, v_ref, g_ref, d_ref, nm_ref, nv_ref):
        g = p_ref[0].astype(F32)
        for s in range(1, NDEV):
            g = g + p_ref[s].astype(F32)
        g_ref[...] = g
        d_ref[...], nm_ref[...], nv_ref[...] = _adam_math(w_ref[...], g, m_ref[...], v_ref[...])

    blk = pl.BlockSpec((br, C), lambda i: (i, 0))
    return pl.pallas_call(
        body,
        name=name,
        grid=(R // br,),
        in_specs=[pl.BlockSpec((NDEV, br, C), lambda i: (0, i, 0)), blk, blk, blk],
        out_specs=[blk] * 4,
        out_shape=[jax.ShapeDtypeStruct((R, C), F32)] * 4,
        compiler_params=_params(1),
    )(parts, w, m, v)


_SMALL = (("sg_w", 8 * CH * CH), ("f_bias", 8), ("sg_ln_g", SW), ("sg_ln_b", SW), ("sg_b", 8 * CH), ("att_out_g", AW),
          ("sg_out_g", SW), ("pre_mix_g", D), ("post_mix_g", D), ("pre_ffn_g", D), ("post_ffn_g", D), ("ple_gate_b", D))
_SEG = 8 * 128


def _seg_rows(size):
    return 8 * (-(-size // _SEG))


def _pack(vals, loss_acc):
    parts = []
    for name, size in _SMALL:
        flat = vals[name].reshape(-1)
        rows = _seg_rows(size)
        parts.append(jnp.pad(flat, (0, rows * 128 - size)).reshape(rows, 128))
    parts.append(loss_acc)
    return jnp.concatenate(parts, axis=0)


def _adam_small(parts, ws, ms, vs):
    n = len(_SMALL)
    names = [name for name, _ in _SMALL]

    def body(*refs):
        p_ref, w_refs, m_refs, v_refs = refs[0], refs[1:1 + n], refs[1 + n:1 + 2 * n], refs[1 + 2 * n:1 + 3 * n]
        loss_ref, outs = refs[1 + 3 * n], refs[2 + 3 * n:]
        g_all = p_ref[0]
        for s in range(1, NDEV):
            g_all = g_all + p_ref[s]
        r = 0
        for k, (name, size) in enumerate(_SMALL):
            dst = [outs[kind * n + k] for kind in range(4)]

            def update(g, idx):
                vals = (g,) + _adam_math(w_refs[k][idx], g, m_refs[k][idx], v_refs[k][idx])
                for d, val in zip(dst, vals):
                    d[idx] = val

            if name == "sg_w":
                for grp in range(8):
                    update(g_all[r + grp * CH:r + (grp + 1) * CH, :], (0, grp))
            elif name == "sg_b":
                update(g_all[r:r + 8, :], (0,))
            elif name == "f_bias":
                update(g_all[r:r + 1, 0:8], (slice(None),))
            else:
                update(jnp.concatenate([g_all[r + q:r + q + 1, :] for q in range(size // 128)], axis=1), (slice(None),))
            r += _seg_rows(size)
        loss_ref[...] = g_all[r:r + 1, 0:1] * (0.5 / D)

    arrs = [parts] + [d[name] for d in (ws, ms, vs) for name in names]
    res = pl.pallas_call(
        body,
        name="adam_small",
        in_specs=[_full(a.shape) for a in arrs],
        out_specs=[_full((1, 1))] + [_full(ws[name].shape) for _ in range(4) for name in names],
        out_shape=[jax.ShapeDtypeStruct((1, 1), F32)] + [jax.ShapeDtypeStruct(ws[name].shape, F32) for _ in range(4) for name in names],
        compiler_params=pltpu.CompilerParams(vmem_limit_bytes=VMEM_LIMIT),
    )(*arrs)
    return res[0], {name: [res[1 + kind * n + k] for kind in range(4)] for k, name in enumerate(names)}


def kernel(x, p, w_in, f_bias, sg_ln_g, sg_ln_b, sg_w, sg_b, att_out_g, sg_out_g, w_out, pre_mix_g, post_mix_g, pre_ffn_g, post_ffn_g, w_ff1, w_ff2, ple_w, ple_gate_w, ple_gate_b, loss_target, m_w_in, m_f_bias, m_sg_ln_g, m_sg_ln_b, m_sg_w, m_sg_b, m_att_out_g, m_sg_out_g, m_w_out, m_pre_mix_g, m_post_mix_g, m_pre_ffn_g, m_post_ffn_g, m_w_ff1, m_w_ff2, m_ple_w, m_ple_gate_w, m_ple_gate_b, v_w_in, v_f_bias, v_sg_ln_g, v_sg_ln_b, v_sg_w, v_sg_b, v_att_out_g, v_sg_out_g, v_w_out, v_pre_mix_g, v_post_mix_g, v_pre_ffn_g, v_post_ffn_g, v_w_ff1, v_w_ff2, v_ple_w, v_ple_gate_w, v_ple_gate_b):
    small_w = dict(sg_w=sg_w, f_bias=f_bias, sg_ln_g=sg_ln_g, sg_ln_b=sg_ln_b, sg_b=sg_b, att_out_g=att_out_g,
                   sg_out_g=sg_out_g, pre_mix_g=pre_mix_g, post_mix_g=post_mix_g, pre_ffn_g=pre_ffn_g,
                   post_ffn_g=post_ffn_g, ple_gate_b=ple_gate_b)
    small_m = dict(sg_w=m_sg_w, f_bias=m_f_bias, sg_ln_g=m_sg_ln_g, sg_ln_b=m_sg_ln_b, sg_b=m_sg_b, att_out_g=m_att_out_g,
                   sg_out_g=m_sg_out_g, pre_mix_g=m_pre_mix_g, post_mix_g=m_post_mix_g, pre_ffn_g=m_pre_ffn_g,
                   post_ffn_g=m_post_ffn_g, ple_gate_b=m_ple_gate_b)
    small_v = dict(sg_w=v_sg_w, f_bias=v_f_bias, sg_ln_g=v_sg_ln_g, sg_ln_b=v_sg_ln_b, sg_b=v_sg_b, att_out_g=v_att_out_g,
                   sg_out_g=v_sg_out_g, pre_mix_g=v_pre_mix_g, post_mix_g=v_post_mix_g, pre_ffn_g=v_pre_ffn_g,
                   post_ffn_g=v_post_ffn_g, ple_gate_b=v_ple_gate_b)
    big = dict(w_in=(w_in, m_w_in, v_w_in), w_out=(w_out, m_w_out, v_w_out), w_ff1=(w_ff1, m_w_ff1, v_w_ff1),
               w_ff2=(w_ff2, m_w_ff2, v_w_ff2), ple_w=(ple_w, m_ple_w, v_ple_w),
               ple_gate_w=(ple_gate_w, m_ple_gate_w, v_ple_gate_w))

    xt, pt, tgt = x[0], p[0, 0], loss_target[0]
    ws = W_IN_COLS // NDEV

    gw_in = _gather(w_in[0].astype(BF16), "gather_w_in")
    rest = _xchg_start([w_out[0].astype(BF16), w_ff1[0].astype(BF16), w_ff2[0].astype(BF16), ple_w[0].astype(BF16),
                        ple_gate_w[0].astype(BF16)], True, "gather_rest_start")
    win = jnp.transpose(gw_in, (1, 0, 2)).reshape(D, W_IN_COLS)
    wp = jnp.concatenate([win[:, 0:NQKV], win[:, NQKV + 8:W_IN_COLS], win[:, NQKV:NQKV + 8],
                          jnp.zeros((D, FPAD - 8), BF16)], axis=1)

    fb = jnp.pad(f_bias.astype(F32), ((0, 0), (0, FPAD - 8)))
    sgw = sg_w[0]
    sgwt = jnp.transpose(sg_w[0], (0, 2, 1))
    sgbt = jnp.transpose(sg_b[0])

    qkv, kt, vt, us, fz, ab = _inproj(xt, pre_mix_g + rest[4][0, 0], wp)
    ct, kk, vk = _fcum(fz, fb, qkv)
    yatt, lsec = _attn_fwd(qkv, kk, vt, ct)
    gw_out, gw1, gw2, gwpe, gwg = _xchg_wait(rest, True, yatt, "gather_rest_wait")
    wout = gw_out.reshape(D, D)
    wg = gwg.reshape(D, D)
    wpe = jnp.transpose(gwpe, (1, 0, 2)).reshape(PLE, D)
    h1, yb, o = _sgu_out(us, yatt, xt, sg_ln_g, sg_ln_b, sgw, sgbt, att_out_g, sg_out_g, wout, post_mix_g)
    c2b, f1b, ff, h2 = _ffn_fwd(h1, pre_ffn_g, gw1, gw2, post_ffn_g)
    dh2, dbg, loss_acc, g_g, g_pe = _ple_loss(h2, pt, tgt, wg, ple_gate_b, wpe)
    g_g = g_g.reshape(NDEV, D // NDEV, D)
    g_pe = jnp.transpose(g_pe.reshape(PLE, NDEV, D // NDEV), (1, 0, 2))

    dffb, df1, dh1, dgpostffn, dgpreffn = _ffn_bwd(dh2, ff, h1, f1b, gw1, gw2, post_ffn_g, pre_ffn_g)
    (g_1,) = _wgrad(c2b, [(df1, ("col", DFF // NDEV))], "wgrad_ff1")
    (g_2,) = _wgrad(f1b, [(dffb, ("row", DFF // NDEV))], "wgrad_ff2", a_fn=_sq_relu)
    early = _xchg_start([g_1, g_2, g_pe, g_g], False, "scatter_early_start")
    dob, qw, dow, dus, dsgw, dsgbt, dlng, dlnb, dgatt, dgsg, dgpostmix = _mix_bwd(
        dh1, o, us, yatt, qkv, lsec, sg_ln_g, sg_ln_b, sgw, sgwt, sgbt, att_out_g, sg_out_g, wout,
        post_mix_g + early[4][0, 0])
    (g_out,) = _wgrad(yb, [(dob, ("row", D // NDEV))], "wgrad_out")
    mid = _xchg_start([g_out], False, "scatter_mid_start")
    dqt, dcq, dk, dv, dck = _attn_bwd(kk, vk, kt, qw, dow, mid[4])
    dfz, dfb = _fgate_bwd(dcq, dck, fz, fb)

    gq, gk, gv, gus, gf = _wgrad(ab, [(dqt, "t"), (dk, None), (dv, None), (dus, None), (dfz, None)], "wgrad_in",
                                 out_dtype=BF16)
    g_in = jnp.concatenate([gq, gk, gv, gf[:, 0:8], gus], axis=1)
    g_in = jnp.transpose(g_in.reshape(D, NDEV, ws), (1, 0, 2))
    late = _xchg_start([g_in], False, "scatter_late_start")
    grad_x, dgpremix = _inproj_bwd(dqt, dk, dv, dus, dfz, wp, jnp.transpose(wp[:, 0:AW]), xt, dh1,
                                   pre_mix_g + late[4][0, 0])

    small_g = dict(sg_w=dsgw, f_bias=dfb[:, 0:8], sg_ln_g=dlng, sg_ln_b=dlnb, sg_b=jnp.transpose(dsgbt[:, 0:8]),
                   att_out_g=dgatt, sg_out_g=dgsg, pre_mix_g=dgpremix, post_mix_g=dgpostmix, pre_ffn_g=dgpreffn,
                   post_ffn_g=dgpostffn, ple_gate_b=dbg)

    r_1, r_2, r_pe, r_g = _xchg_wait(early, False, grad_x, "scatter_early_wait")
    (r_out,) = _xchg_wait(mid, False, grad_x, "scatter_mid_wait")
    r_small = _gather(_pack(small_g, loss_acc), "gather_small_grads")

    res = {}

    def adam_big(name, parts):
        w, m, v = big[name]
        res[name] = [t[None] for t in _adam(parts, w[0], m[0], v[0], "adam_" + name)]

    for name, parts in (("w_out", r_out), ("w_ff1", r_1), ("w_ff2", r_2), ("ple_w", r_pe), ("ple_gate_w", r_g)):
        adam_big(name, parts)
    (r_in,) = _xchg_wait(late, False, res["w_ff1"][0], "scatter_late_wait")
    adam_big("w_in", r_in)
    loss, small = _adam_small(r_small, small_w, small_m, small_v)
    res.update(small)

    order = ["w_in", "f_bias", "sg_ln_g", "sg_ln_b", "sg_w", "sg_b", "att_out_g", "sg_out_g", "w_out", "pre_mix_g",
             "post_mix_g", "pre_ffn_g", "post_ffn_g", "w_ff1", "w_ff2", "ple_w", "ple_gate_w", "ple_gate_b"]
    outs = [loss[0, 0], grad_x[None]]
    for kind in range(4):
        outs += [res[name][kind] for name in order]
    return tuple(outs)
```

```python
import jax
import jax.numpy as jnp
from jax import lax
from jax.experimental import pallas as pl
from jax.experimental.pallas import tpu as pltpu

F32 = jnp.float32
BF16 = jnp.bfloat16

NDEV = 8
D = 1024
AW = 512
SW = 512
HD = 64
CH = 128
DFF = 4096
PLE = 256
NQKV = 3 * AW
NUS = 2 * SW
FPAD = 128
ZP = NQKV + NUS + FPAD
W_IN_COLS = 2568
EPS = 1e-6
MASKV = -1e30
GELU_K = 0.7978845608028654
GELU_C = 0.044715

ADAM_LR = 0.001
ADAM_B1 = 0.9
ADAM_B2 = 0.999
ADAM_EPS = 1e-08
ADAM_WD = 0.01
ADAM_STEP = 10

VMEM_LIMIT = 48 * 1024 * 1024
VMEM_LIMIT_BIG = 60 * 1024 * 1024


def _nn(a, b):
    return jnp.dot(a, b, preferred_element_type=F32)


def _nt(a, b):
    return lax.dot_general(a, b, (((1,), (1,)), ((), ())), preferred_element_type=F32)


def _tn(a, b):
    return lax.dot_general(a, b, (((0,), (0,)), ((), ())), preferred_element_type=F32)


def _tile(n, pref):
    return min(n, pref)


def _params(n_axes, vmem=VMEM_LIMIT):
    return pltpu.CompilerParams(dimension_semantics=("arbitrary",) * n_axes, vmem_limit_bytes=vmem)


def _full(shape):
    nd = len(shape)
    return pl.BlockSpec(shape, lambda *_: (0,) * nd)


def _rms_fwd(x, g):
    r = lax.rsqrt(jnp.mean(x * x, axis=-1, keepdims=True) + EPS)
    return x * r * g


def _rms_bwd(dy, x, g):
    n = x.shape[-1]
    r = lax.rsqrt(jnp.mean(x * x, axis=-1, keepdims=True) + EPS)
    u = dy * g
    s = jnp.sum(x * u, axis=-1, keepdims=True)
    dx = r * u - x * (r * r * r * (s * (1.0 / n)))
    dg = jnp.sum(dy * (x * r), axis=0, keepdims=True)
    return dx, dg


def _gelu(x):
    t = jnp.tanh(x * (GELU_K + (GELU_K * GELU_C) * (x * x)))
    return x * (0.5 + 0.5 * t), t


def _gelu_grad(x, t):
    return (0.5 + 0.5 * t) + (0.5 * x) * (1.0 - t * t) * (GELU_K + (3.0 * GELU_K * GELU_C) * (x * x))


def _gather(arr, name):
    def body(x_ref, out_ref, send, recv, loc):
        x, y, c = lax.axis_index("x"), lax.axis_index("y"), lax.axis_index("c")
        me, sibling = (x, y, c), (x, y, 1 - c)
        chips = [(1 - x, y), (x, 1 - y), (1 - x, 1 - y)]

        def slot(px, py, pc):
            return out_ref.at[4 * px + 2 * py + pc]

        def copy(k, block, to, src=None):
            return pltpu.make_async_remote_copy(
                src_ref=slot(*block) if src is None else src, dst_ref=slot(*block), send_sem=send.at[k],
                recv_sem=recv.at[k], device_id=to, device_id_type=pl.DeviceIdType.MESH)

        mine = pltpu.make_async_copy(x_ref, slot(*me), loc)
        mine.start()
        first = [copy(0, me, sibling, src=x_ref)] + [copy(1 + j, me, (*chip, c), src=x_ref) for j, chip in enumerate(chips)]
        for cp in first:
            cp.start()
        passed = [copy(4 + j, (*chip, c), sibling) for j, chip in enumerate(chips)]
        for j, chip in enumerate(chips):
            copy(1 + j, (*chip, c), me).wait_recv()
            passed[j].start()
        copy(0, sibling, me).wait_recv()
        for j, chip in enumerate(chips):
            copy(4 + j, (*chip, 1 - c), me).wait_recv()
        for cp in first + passed:
            cp.wait_send()
        mine.wait()

    hbm = pl.BlockSpec(memory_space=pltpu.HBM)
    return pl.pallas_call(
        body,
        name=name,
        out_shape=jax.ShapeDtypeStruct((NDEV,) + arr.shape, arr.dtype),
        in_specs=[hbm],
        out_specs=hbm,
        scratch_shapes=[pltpu.SemaphoreType.DMA((NDEV - 1,)), pltpu.SemaphoreType.DMA((NDEV - 1,)), pltpu.SemaphoreType.DMA],
    )(arr)


def _peers(x, y, c):
    out = []
    for k in range(1, NDEV):
        out.append((1 - x if (k >> 2) & 1 else x, 1 - y if (k >> 1) & 1 else y, 1 - c if k & 1 else c))
    return out


def _xchg_start(arrs, gather, name):
    n = len(arrs)
    me = 4 * lax.axis_index("x") + 2 * lax.axis_index("y") + lax.axis_index("c")
    lands = []
    for a in arrs:
        shape = ((NDEV,) + a.shape) if gather else a.shape
        own = a[None] if gather else lax.dynamic_slice_in_dim(a, me, 1, axis=0)
        lands.append(lax.dynamic_update_slice_in_dim(lax.empty(shape, a.dtype), own, me, axis=0))

    def body(*refs):
        ins, lnd = refs[:n], refs[n:2 * n]
        send, recv, token = refs[2 * n:3 * n], refs[3 * n:4 * n], refs[-1]
        x, y, c = lax.axis_index("x"), lax.axis_index("y"), lax.axis_index("c")
        mine = 4 * x + 2 * y + c
        for px, py, pc in _peers(x, y, c):
            peer = 4 * px + 2 * py + pc
            for a in range(n):
                pltpu.make_async_remote_copy(
                    src_ref=ins[a] if gather else ins[a].at[peer],
                    dst_ref=lnd[a].at[mine],
                    send_sem=send[a],
                    recv_sem=recv[a],
                    device_id=(px, py, pc),
                    device_id_type=pl.DeviceIdType.MESH,
                ).start()
        token[...] = jnp.zeros_like(token)

    hbm = pl.BlockSpec(memory_space=pltpu.HBM)
    sem = pl.BlockSpec(memory_space=pltpu.SEMAPHORE)
    res = pl.pallas_call(
        body,
        name=name,
        out_shape=(*[pltpu.SemaphoreType.DMA(())] * (2 * n),
                   *[pltpu.HBM(a.shape, a.dtype) for a in arrs], *[pltpu.HBM(l.shape, l.dtype) for l in lands],
                   jax.ShapeDtypeStruct((8, 128), F32)),
        in_specs=[hbm] * (2 * n),
        out_specs=(*([sem] * (2 * n)), *([hbm] * (2 * n)), pl.BlockSpec(memory_space=pltpu.VMEM)),
        input_output_aliases={i: 2 * n + i for i in range(2 * n)},
        compiler_params=pltpu.CompilerParams(has_side_effects=pltpu.SideEffectType.DATAFLOW_SIDE_EFFECTING),
    )(*[pltpu.with_memory_space_constraint(a, pltpu.HBM) for a in arrs],
      *[pltpu.with_memory_space_constraint(l, pltpu.HBM) for l in lands])
    return list(res[0:n]), list(res[n:2 * n]), list(res[2 * n:3 * n]), list(res[3 * n:4 * n]), res[-1]


def _xchg_wait(started, gather, after, name):
    send, recv, srcs, lands, _ = started
    n = len(srcs)

    def body(*refs):
        lnd = refs[n:2 * n]
        send, recv = refs[2 * n:3 * n], refs[3 * n:4 * n]
        me = (lax.axis_index("x"), lax.axis_index("y"), lax.axis_index("c"))
        for a in range(n):
            seven = lnd[a].at[pl.ds(0, NDEV - 1)]
            cp = pltpu.make_async_remote_copy(src_ref=seven, dst_ref=seven, send_sem=send[a], recv_sem=recv[a],
                                              device_id=me, device_id_type=pl.DeviceIdType.MESH)
            cp.wait_send()
            cp.wait_recv()

    hbm = pl.BlockSpec(memory_space=pltpu.HBM)
    sem = pl.BlockSpec(memory_space=pltpu.SEMAPHORE)
    res = pl.pallas_call(
        body,
        name=name,
        out_shape=tuple([pltpu.HBM(a.shape, a.dtype) for a in srcs] + [pltpu.HBM(l.shape, l.dtype) for l in lands]),
        in_specs=[hbm] * (2 * n) + [sem] * (2 * n) + [pl.BlockSpec(memory_space=pl.ANY)],
        out_specs=tuple([hbm] * (2 * n)),
        input_output_aliases={i: i for i in range(2 * n)},
        compiler_params=pltpu.CompilerParams(has_side_effects=pltpu.SideEffectType.DATAFLOW_SIDE_EFFECTING),
    )(*srcs, *lands, *send, *recv, after)
    return list(res[n:])


def _inproj(x, g, wp):
    T = x.shape[0]
    tm = _tile(T, 512)

    def body(x_ref, g_ref, w_ref, qkv_ref, kt_ref, vt_ref, us_ref, fz_ref, ab_ref):
        a = _rms_fwd(x_ref[...], g_ref[...]).astype(BF16)
        ab_ref[...] = a
        qkv_ref[:, 0:AW] = _nn(a, w_ref[:, 0:AW]).astype(BF16)
        kk = _nn(a, w_ref[:, AW:2 * AW])
        qkv_ref[:, AW:2 * AW] = kk.astype(BF16)
        kt_ref[...] = kk.T.astype(BF16)
        vv = _nn(a, w_ref[:, 2 * AW:NQKV])
        qkv_ref[:, 2 * AW:NQKV] = vv.astype(BF16)
        vt_ref[...] = vv.T.astype(BF16)
        us_ref[...] = _nn(a, w_ref[:, NQKV:NQKV + NUS])
        fz_ref[...] = _nn(a, w_ref[:, NQKV + NUS:ZP])

    row = lambda n: pl.BlockSpec((tm, n), lambda i: (i, 0))
    col = pl.BlockSpec((AW, tm), lambda i: (0, i))
    return pl.pallas_call(
        body,
        name="inproj",
        grid=(T // tm,),
        in_specs=[row(D), _full((1, D)), _full((D, ZP))],
        out_specs=[row(NQKV), col, col, row(NUS), row(FPAD), row(D)],
        out_shape=[
            jax.ShapeDtypeStruct((T, NQKV), BF16),
            jax.ShapeDtypeStruct((AW, T), BF16),
            jax.ShapeDtypeStruct((AW, T), BF16),
            jax.ShapeDtypeStruct((T, NUS), F32),
            jax.ShapeDtypeStruct((T, FPAD), F32),
            jax.ShapeDtypeStruct((T, D), BF16),
        ],
        compiler_params=_params(1),
    )(x, g, wp)


def _log_sigmoid(z):
    return jnp.minimum(z, 0.0) - jnp.log1p(jnp.exp(-jnp.abs(z)))


def _split3(x):
    hi = x.astype(BF16)
    r1 = x - hi.astype(F32)
    mid = r1.astype(BF16)
    lo = (r1 - mid.astype(F32)).astype(BF16)
    return hi, mid, lo


AUG_A, AUG_B, AUG_ONE = 0, 3, 6


def _aug_lanes(rows, first_one, pieces):
    lane = lax.broadcasted_iota(jnp.int32, (rows, 128), 1)
    out = jnp.zeros((rows, 128), F32)
    if first_one is not None:
        out = jnp.where((lane >= first_one) & (lane < first_one + 3), 1.0, out)
    for n, piece in enumerate(pieces):
        out = jnp.where(lane == AUG_ONE + n, piece.astype(F32), out)
    return out.astype(BF16)


def _fcum(fz, fb, qkv):
    T = fz.shape[0]
    tb = _tile(T, 512)

    def body(fz_ref, fb_ref, k_ref, v_ref, ct_ref, kk_ref, vk_ref, carry, tri_s, sel_s):
        @pl.when(pl.program_id(0) == 0)
        def _():
            carry[...] = jnp.zeros_like(carry)
            src = lax.broadcasted_iota(jnp.int32, (128, 128), 0)
            dst = lax.broadcasted_iota(jnp.int32, (128, 128), 1)
            tri_s[...] = (dst <= src).astype(F32)
            for hp in range(4):
                for n in range(3):
                    pick = ((src == 2 * hp) & (dst == AUG_A + n)) | ((src == 2 * hp + 1) & (dst == AUG_B + n))
                    sel_s[3 * hp + n] = jnp.where(pick, -1.0, 0.0).astype(BF16)

        lf = _log_sigmoid(fz_ref[...] + fb_ref[...])
        run = carry[...]
        parts = []
        for blk in range(tb // 128):
            lf_b = lf[blk * 128:(blk + 1) * 128, :]
            parts.append(jnp.dot(tri_s[...], lf_b, precision=lax.Precision.HIGHEST, preferred_element_type=F32) + run)
            run = run + jnp.sum(lf_b, axis=0, keepdims=True)
        carry[...] = run
        cs = jnp.concatenate(parts, axis=0)
        ct_ref[...] = cs.T[0:8, :]

        pieces = _split3(cs)
        lane = lax.broadcasted_iota(jnp.int32, (1, 128), 1)
        ones = jnp.where((lane >= AUG_ONE) & (lane < AUG_ONE + 3), 1.0, 0.0)
        for hp in range(4):
            aug = jnp.zeros((tb, 128), F32) + ones
            for n, piece in enumerate(pieces):
                aug = aug + _nn(piece, sel_s[3 * hp + n])
            aug = aug.astype(BF16)
            kk_ref[:, hp * 256:hp * 256 + 128] = k_ref[:, hp * 128:(hp + 1) * 128]
            kk_ref[:, hp * 256 + 128:(hp + 1) * 256] = aug
            vk_ref[:, hp * 256:hp * 256 + 128] = v_ref[:, hp * 128:(hp + 1) * 128]
            vk_ref[:, hp * 256 + 128:(hp + 1) * 256] = aug

    row = lambda n, c: pl.BlockSpec((tb, n), lambda i: (i, c))
    return pl.pallas_call(
        body,
        name="fcum",
        grid=(T // tb,),
        in_specs=[row(FPAD, 0), _full((1, FPAD)), row(AW, 1), row(AW, 2)],
        out_specs=[pl.BlockSpec((8, tb), lambda i: (0, i)), row(2 * AW, 0), row(2 * AW, 0)],
        out_shape=[jax.ShapeDtypeStruct((8, T), F32), jax.ShapeDtypeStruct((T, 2 * AW), BF16),
                   jax.ShapeDtypeStruct((T, 2 * AW), BF16)],
        scratch_shapes=[pltpu.VMEM((1, FPAD), F32), pltpu.VMEM((128, 128), F32), pltpu.VMEM((12, 128, 128), BF16)],
        compiler_params=_params(1),
    )(fz, fb, qkv, qkv)


def _fold_rows(r, t, nq):
    if nq == 1:
        return r, t
    low = t <= r
    return jnp.where(low, r, nq - 1 - r), jnp.where(low, t, t - r - 1)


def _fold_cols(r, t, nq):
    if nq == 1:
        return r, t
    first = t < nq - r
    j = jnp.where(first, r, nq - 1 - r)
    return j, jnp.where(first, r + t, nq - 1 - r + (t - (nq - r)))


PAIRS = 4


def _fold_grid(nq):
    assert (nq == 1 or nq % 2 == 0) and PAIRS == 4
    return (4 // PAIRS, 1, 1) if nq == 1 else (4 // PAIRS, nq // 2, nq + 1)


def _head_rows(x2, hh, scale):
    is_a = lax.broadcasted_iota(jnp.int32, (1, 128), 1) < HD
    keep = is_a if hh == 0 else jnp.logical_not(is_a)
    return jnp.where(keep, x2, jnp.zeros_like(x2)) * scale


def _attn_fwd(qkv, kk, vt, ct):
    T = qkv.shape[0]
    tq = _tile(T, 512)
    tk = tq
    nq = T // tq

    def body(q_ref, kk_ref, vt_ref, ctq_ref, o_ref, lsec_ref, qw_s, m_s, l_s, acc_s):
        i, j = _fold_rows(pl.program_id(1), pl.program_id(2), nq)
        sub8 = lax.broadcasted_iota(jnp.int32, (8, 1), 0)

        def cref_of(pp, hh):
            head = 2 * (PAIRS * pl.program_id(0) + pp) + hh
            return jnp.sum(jnp.where(sub8 == head, ctq_ref[:, 0:1], 0.0), axis=0, keepdims=True)

        @pl.when(j == 0)
        def _():
            for pp in range(PAIRS):
                q2 = q_ref[:, pp * 128:(pp + 1) * 128]
                for hh in range(2):
                    rows = slice(hh * tq, (hh + 1) * tq)
                    qw_s[pp, rows, 0:128] = _head_rows(q2, hh, 0.125)
                    qw_s[pp, rows, 128:256] = _aug_lanes(tq, AUG_A if hh == 0 else AUG_B, _split3(cref_of(pp, hh)))
            m_s[...] = jnp.full_like(m_s, MASKV)
            l_s[...] = jnp.zeros_like(l_s)
            acc_s[...] = jnp.zeros_like(acc_s)

        def step(masked):
            if masked:
                causal = lax.broadcasted_iota(jnp.int32, (tk, tq), 0) <= lax.broadcasted_iota(jnp.int32, (tk, tq), 1)
            logits = lambda pp: _nt(kk_ref[:, pp * 256:(pp + 1) * 256], qw_s[pp])
            s_next = logits(0)
            for pp in range(PAIRS):
                s2 = s_next
                if pp + 1 < PAIRS:
                    s_next = logits(pp + 1)
                vt2 = vt_ref[pp * 128:(pp + 1) * 128, :]
                for hh in range(2):
                    n = 2 * pp + hh
                    s = s2[:, hh * tq:(hh + 1) * tq]
                    if masked:
                        s = jnp.where(causal, s, MASKV)
                    m_prev = m_s[n]
                    m_new = jnp.maximum(m_prev, jnp.max(s, axis=0, keepdims=True))
                    pr = jnp.exp(s - m_new)
                    alpha = jnp.exp(m_prev - m_new)
                    l_s[n] = alpha * l_s[n] + jnp.sum(pr, axis=0, keepdims=True)
                    m_s[n] = m_new
                    acc_s[n] = alpha * acc_s[n] + _nn(vt2, pr.astype(BF16))

        @pl.when(j < i)
        def _():
            step(False)

        @pl.when(j == i)
        def _():
            step(True)
            sub = lax.broadcasted_iota(jnp.int32, (128, 1), 0)
            for pp in range(PAIRS):
                a, b = 2 * pp, 2 * pp + 1
                ot = jnp.where(sub < HD, acc_s[a] * (1.0 / l_s[a]), acc_s[b] * (1.0 / l_s[b]))
                o_ref[:, pp * 128:(pp + 1) * 128] = ot.T
                lrow = [m_s[2 * pp + hh] + jnp.log(l_s[2 * pp + hh]) - cref_of(pp, hh) for hh in range(2)]
                lsec_ref[:, pp * 128:(pp + 1) * 128] = jnp.where(sub == 0, lrow[0], jnp.where(sub == 1, lrow[1], 0.0)).T

    qi = lambda r, t: _fold_rows(r, t, nq)[0]
    kj = lambda r, t: _fold_rows(r, t, nq)[1]
    return pl.pallas_call(
        body,
        name="attn_fwd",
        grid=_fold_grid(nq),
        in_specs=[
            pl.BlockSpec((tq, PAIRS * 128), lambda g, r, t: (qi(r, t), g)),
            pl.BlockSpec((tk, PAIRS * 256), lambda g, r, t: (kj(r, t), g)),
            pl.BlockSpec((PAIRS * 128, tk), lambda g, r, t: (g, kj(r, t))),
            pl.BlockSpec((8, tq), lambda g, r, t: (0, qi(r, t))),
        ],
        out_specs=[pl.BlockSpec((tq, PAIRS * 128), lambda g, r, t: (qi(r, t), g))] * 2,
        out_shape=[jax.ShapeDtypeStruct((T, AW), F32)] * 2,
        scratch_shapes=[pltpu.VMEM((PAIRS, 2 * tq, 256), BF16), pltpu.VMEM((2 * PAIRS, 1, tq), F32),
                        pltpu.VMEM((2 * PAIRS, 1, tq), F32), pltpu.VMEM((2 * PAIRS, 128, tq), F32)],
        compiler_params=_params(3),
    )(qkv, kk, vt, ct)


def _sgu_forward(us_ref, lng, lnb, w_ref, bt_ref, mixed_s, vnb_s, tm):
    is_a = lax.broadcasted_iota(jnp.int32, (1, 128), 1) < HD
    u = us_ref[:, 0:SW]
    vs = us_ref[:, SW:NUS]
    ug, tu = _gelu(u)
    vg, tv = _gelu(vs)
    mu = jnp.mean(vg, axis=-1, keepdims=True)
    xc = vg - mu
    rstd = lax.rsqrt(jnp.mean(xc * xc, axis=-1, keepdims=True) + EPS)
    vhat = xc * rstd
    vnb_s[...] = (vhat * lng + lnb).astype(BF16)
    rr = lax.broadcasted_iota(jnp.int32, (CH, CH), 0)
    cc = lax.broadcasted_iota(jnp.int32, (CH, CH), 1)
    tril = cc <= rr
    for jj in range(4):
        wa = jnp.where(tril, w_ref[2 * jj], 0.0).astype(BF16)
        wb = jnp.where(tril, w_ref[2 * jj + 1], 0.0).astype(BF16)
        ba = bt_ref[:, 2 * jj:2 * jj + 1]
        bb = bt_ref[:, 2 * jj + 1:2 * jj + 2]
        for ch in range(tm // CH):
            rs, cs = slice(ch * CH, (ch + 1) * CH), slice(jj * 128, (jj + 1) * 128)
            vn2 = vnb_s[rs, cs]
            mixed_s[rs, cs] = jnp.where(is_a, _nn(wa, vn2) + ba, _nn(wb, vn2) + bb)
    mixed = mixed_s[...]
    return u, vs, ug, tu, tv, vhat, rstd, mixed, ug * mixed


def _sgu_out(us, yatt, x, lng, lnb, sgw, sgbt, gatt, gsg, wout, gpm):
    T = us.shape[0]
    tm = _tile(T, 512)

    def body(us_ref, ya_ref, x_ref, lng_ref, lnb_ref, w_ref, bt_ref, ga_ref, gs_ref, wo_ref, gp_ref,
             h1_ref, yb_ref, o_ref, mixed_s, vnb_s):
        ysg = _sgu_forward(us_ref, lng_ref[...], lnb_ref[...], w_ref, bt_ref, mixed_s, vnb_s, tm)[-1]
        yb_ref[:, 0:AW] = _rms_fwd(ya_ref[...], ga_ref[...]).astype(BF16)
        yb_ref[:, AW:D] = _rms_fwd(ysg, gs_ref[...]).astype(BF16)
        o = _nn(yb_ref[...], wo_ref[...])
        o_ref[...] = o
        h1_ref[...] = x_ref[...] + _rms_fwd(o, gp_ref[...])

    row = lambda n: pl.BlockSpec((tm, n), lambda i: (i, 0))
    return pl.pallas_call(
        body,
        name="sgu_out",
        grid=(T // tm,),
        in_specs=[row(NUS), row(AW), row(D), _full((1, SW)), _full((1, SW)), _full((8, CH, CH)), _full((CH, 8)),
                  _full((1, AW)), _full((1, SW)), _full((D, D)), _full((1, D))],
        out_specs=[row(D), row(D), row(D)],
        out_shape=[jax.ShapeDtypeStruct((T, D), F32), jax.ShapeDtypeStruct((T, D), BF16), jax.ShapeDtypeStruct((T, D), F32)],
        scratch_shapes=[pltpu.VMEM((tm, SW), F32), pltpu.VMEM((tm, SW), BF16)],
        compiler_params=_params(1),
    )(us, yatt, x, lng, lnb, sgw, sgbt, gatt, gsg, wout, gpm)


def _ffn_fwd(h1, gpre, w1g, w2g, gpost):
    T = h1.shape[0]
    tm = _tile(T, 512)
    nb, hb = w1g.shape[0], w1g.shape[2]

    def body(h1_ref, gpre_ref, w1_ref, w2_ref, gpost_ref, c2_ref, f1_ref, ff_ref, h2_ref):
        h1 = h1_ref[...]
        c2 = _rms_fwd(h1, gpre_ref[...]).astype(BF16)
        c2_ref[...] = c2
        ff = jnp.zeros((tm, D), F32)
        for j in range(nb):
            f1 = _nn(c2, w1_ref[j])
            f1_ref[:, j * hb:(j + 1) * hb] = f1.astype(BF16)
            r = jnp.maximum(f1, 0.0)
            ff = ff + _nn((r * r).astype(BF16), w2_ref[j])
        ff_ref[...] = ff
        h2_ref[...] = h1 + _rms_fwd(ff, gpost_ref[...])

    row = lambda n: pl.BlockSpec((tm, n), lambda i: (i, 0))
    once = lambda shape: pl.BlockSpec(shape, lambda i: (0,) * len(shape), pipeline_mode=pl.Buffered(1))
    return pl.pallas_call(
        body,
        name="ffn_fwd",
        grid=(T // tm,),
        in_specs=[row(D), _full((1, D)), once((nb, D, hb)), once((nb, hb, D)), _full((1, D))],
        out_specs=[row(D), row(DFF), row(D), row(D)],
        out_shape=[jax.ShapeDtypeStruct((T, D), BF16), jax.ShapeDtypeStruct((T, DFF), BF16),
                   jax.ShapeDtypeStruct((T, D), F32), jax.ShapeDtypeStruct((T, D), F32)],
        compiler_params=_params(1, VMEM_LIMIT_BIG),
    )(h1, gpre, w1g, w2g, gpost)


def _ple_loss(h2, p, tgt, wg, bg, wpe):
    T = h2.shape[0]
    tm = _tile(T, 512)

    def body(h2_ref, p_ref, t_ref, wg_ref, bg_ref, wpe_ref, dh2_ref, dbg_ref, loss_ref, dwg_ref, dwpe_ref):
        @pl.when(pl.program_id(0) == 0)
        def _():
            for r in (dbg_ref, loss_ref, dwg_ref, dwpe_ref):
                r[...] = jnp.zeros_like(r)

        h2 = h2_ref[...]
        h2b = h2.astype(BF16)
        gate = jax.nn.sigmoid(_nn(h2b, wg_ref[...]) + bg_ref[...])
        pb = p_ref[...].astype(BF16)
        pe = _nn(pb, wpe_ref[...])
        diff = (h2 + gate * pe) - t_ref[...]
        loss_ref[...] += jnp.sum(diff * diff)
        dh3 = diff * (1.0 / D)
        dpre = (dh3 * pe) * (gate * (1.0 - gate))
        dpre_b = dpre.astype(BF16)
        dbg_ref[...] += jnp.sum(dpre, axis=0, keepdims=True)
        dh2_ref[...] = dh3 + _nt(dpre_b, wg_ref[...])
        dwg_ref[...] += _nn(h2b.T, dpre_b)
        dwpe_ref[...] += _nn(pb.T, (dh3 * gate).astype(BF16))

    row = lambda n: pl.BlockSpec((tm, n), lambda i: (i, 0))
    once = lambda shape: pl.BlockSpec(shape, lambda i: (0,) * len(shape), pipeline_mode=pl.Buffered(1))
    return pl.pallas_call(
        body,
        name="ple_loss",
        grid=(T // tm,),
        in_specs=[row(D), row(PLE), row(D), _full((D, D)), _full((1, D)), _full((PLE, D))],
        out_specs=[row(D), _full((1, D)), _full((8, 128)), once((D, D)), once((PLE, D))],
        out_shape=[jax.ShapeDtypeStruct((T, D), F32), jax.ShapeDtypeStruct((1, D), F32),
                   jax.ShapeDtypeStruct((8, 128), F32), jax.ShapeDtypeStruct((D, D), F32),
                   jax.ShapeDtypeStruct((PLE, D), F32)],
        compiler_params=_params(1),
    )(h2, p, tgt, wg, bg, wpe)


def _ffn_bwd(dh2, ff, h1, f1, w1g, w2g, gpost, gpre):
    T = dh2.shape[0]
    tm = _tile(T, 512)
    nb, hb = w1g.shape[0], w1g.shape[2]

    def body(dh2_ref, ff_ref, h1_ref, f1_ref, w1_ref, w2_ref, gpost_ref, gpre_ref,
             dffb_ref, df1_ref, dh1_ref, dgpost_ref, dgpre_ref):
        @pl.when(pl.program_id(0) == 0)
        def _():
            dgpost_ref[...] = jnp.zeros_like(dgpost_ref)
            dgpre_ref[...] = jnp.zeros_like(dgpre_ref)

        dh2 = dh2_ref[...]
        dff, dg = _rms_bwd(dh2, ff_ref[...], gpost_ref[...])
        dffb = dff.astype(BF16)
        dffb_ref[...] = dffb
        dgpost_ref[...] += dg
        dc2 = jnp.zeros((tm, D), F32)
        for j in range(nb):
            cols = slice(j * hb, (j + 1) * hb)
            dact = _nt(dffb, w2_ref[j])
            df1 = (dact * (2.0 * jnp.maximum(f1_ref[:, cols].astype(F32), 0.0))).astype(BF16)
            df1_ref[:, cols] = df1
            dc2 = dc2 + _nt(df1, w1_ref[j])
        dx, dg = _rms_bwd(dc2, h1_ref[...], gpre_ref[...])
        dh1_ref[...] = dh2 + dx
        dgpre_ref[...] += dg

    row = lambda n: pl.BlockSpec((tm, n), lambda i: (i, 0))
    once = lambda shape: pl.BlockSpec(shape, lambda i: (0,) * len(shape), pipeline_mode=pl.Buffered(1))
    return pl.pallas_call(
        body,
        name="ffn_bwd",
        grid=(T // tm,),
        in_specs=[row(D), row(D), row(D), row(DFF), once((nb, D, hb)), once((nb, hb, D)), _full((1, D)), _full((1, D))],
        out_specs=[row(D), row(DFF), row(D), _full((1, D)), _full((1, D))],
        out_shape=[jax.ShapeDtypeStruct((T, D), BF16), jax.ShapeDtypeStruct((T, DFF), BF16),
                   jax.ShapeDtypeStruct((T, D), F32), jax.ShapeDtypeStruct((1, D), F32),
                   jax.ShapeDtypeStruct((1, D), F32)],
        compiler_params=_params(1, VMEM_LIMIT_BIG),
    )(dh2, ff, h1, f1, w1g, w2g, gpost, gpre)


def _mix_bwd(dh1, o, us, yatt, qkv, lsec, yb, lng, lnb, sgw, sgwt, sgbt, gatt, gsg, wout, gpm):
    T = dh1.shape[0]
    tm = _tile(T, 512)

    def body(dh1_ref, o_ref, us_ref, ya_ref, q_ref, l_ref, yb_ref, lng_ref, lnb_ref, w_ref, wt_ref, bt_ref, ga_ref, gs_ref,
             wo_ref, gp_ref, qw_ref, dow_ref, dus_ref, dwo_ref, dw_ref, dbt_ref, dlng_ref, dlnb_ref, dga_ref, dgs_ref,
             dgp_ref, mixed_s, vnb_s, dvn_s):
        @pl.when(pl.program_id(0) == 0)
        def _():
            for r in (dwo_ref, dw_ref, dbt_ref, dlng_ref, dlnb_ref, dga_ref, dgs_ref, dgp_ref):
                r[...] = jnp.zeros_like(r)

        is_a = lax.broadcasted_iota(jnp.int32, (1, 128), 1) < HD
        lane = lax.broadcasted_iota(jnp.int32, (1, 128), 1)
        do, dg = _rms_bwd(dh1_ref[...], o_ref[...], gp_ref[...])
        dgp_ref[...] += dg
        dob = do.astype(BF16)
        dwo_ref[...] += _nn(yb_ref[...].T, dob)
        dy = _nt(dob, wo_ref[...])
        ya = ya_ref[...]
        datt, dg = _rms_bwd(dy[:, 0:AW], ya, ga_ref[...])
        dga_ref[...] += dg
        _attn_operands(q_ref[...], datt, ya, l_ref[...], qw_ref, dow_ref)

        lng = lng_ref[...]
        u, vs, ug, tu, tv, vhat, rstd, mixed, ysg = _sgu_forward(us_ref, lng, lnb_ref[...], w_ref, bt_ref, mixed_s, vnb_s, tm)
        dysg, dg = _rms_bwd(dy[:, AW:D], ysg, gs_ref[...])
        dgs_ref[...] += dg
        dus_ref[:, 0:SW] = ((dysg * mixed) * _gelu_grad(u, tu)).astype(BF16)
        dmix = dysg * ug

        rr = lax.broadcasted_iota(jnp.int32, (CH, CH), 0)
        cc = lax.broadcasted_iota(jnp.int32, (CH, CH), 1)
        tril = cc <= rr
        triu = cc >= rr
        for jj in range(4):
            wta = jnp.where(triu, wt_ref[2 * jj], 0.0).astype(BF16)
            wtb = jnp.where(triu, wt_ref[2 * jj + 1], 0.0).astype(BF16)
            for ch in range(tm // CH):
                rs, cs = slice(ch * CH, (ch + 1) * CH), slice(jj * 128, (jj + 1) * 128)
                dm2 = dmix[rs, cs]
                dma = jnp.where(is_a, dm2, 0.0)
                dmb = jnp.where(is_a, 0.0, dm2)
                dma_b, dmb_b = dma.astype(BF16), dmb.astype(BF16)
                vn2 = vnb_s[rs, cs]
                dw_ref[2 * jj] += jnp.where(tril, _nt(dma_b, vn2), 0.0)
                dw_ref[2 * jj + 1] += jnp.where(tril, _nt(dmb_b, vn2), 0.0)
                dvn_s[rs, cs] = _nn(wta, dma_b) + _nn(wtb, dmb_b)
                dba = jnp.sum(dma, axis=1, keepdims=True)
                dbb = jnp.sum(dmb, axis=1, keepdims=True)
                dbt_ref[...] += jnp.where(lane == 2 * jj, dba, 0.0) + jnp.where(lane == 2 * jj + 1, dbb, 0.0)

        dvn = dvn_s[...]
        dlng_ref[...] += jnp.sum(dvn * vhat, axis=0, keepdims=True)
        dlnb_ref[...] += jnp.sum(dvn, axis=0, keepdims=True)
        dvh = dvn * lng
        dvg = rstd * (dvh - jnp.mean(dvh, axis=-1, keepdims=True) - vhat * jnp.mean(dvh * vhat, axis=-1, keepdims=True))
        dus_ref[:, SW:NUS] = (dvg * _gelu_grad(vs, tv)).astype(BF16)

    row = lambda n: pl.BlockSpec((tm, n), lambda i: (i, 0))
    return pl.pallas_call(
        body,
        name="mix_bwd",
        grid=(T // tm,),
        in_specs=[row(D), row(D), row(NUS), row(AW), row(AW), row(AW), row(D), _full((1, SW)), _full((1, SW)),
                  _full((8, CH, CH)), _full((8, CH, CH)), _full((CH, 8)), _full((1, AW)), _full((1, SW)), _full((D, D)),
                  _full((1, D))],
        out_specs=[row(4 * AW), row(4 * AW), row(NUS),
                   pl.BlockSpec((D, D), lambda i: (0, 0), pipeline_mode=pl.Buffered(1)), _full((8, CH, CH)),
                   _full((CH, 128)), _full((1, SW)), _full((1, SW)), _full((1, AW)), _full((1, SW)), _full((1, D))],
        out_shape=[jax.ShapeDtypeStruct((T, 4 * AW), BF16),
                   jax.ShapeDtypeStruct((T, 4 * AW), BF16), jax.ShapeDtypeStruct((T, NUS), BF16),
                   jax.ShapeDtypeStruct((D, D), F32),
                   jax.ShapeDtypeStruct((8, CH, CH), F32), jax.ShapeDtypeStruct((CH, 128), F32),
                   jax.ShapeDtypeStruct((1, SW), F32), jax.ShapeDtypeStruct((1, SW), F32),
                   jax.ShapeDtypeStruct((1, AW), F32), jax.ShapeDtypeStruct((1, SW), F32), jax.ShapeDtypeStruct((1, D), F32)],
        scratch_shapes=[pltpu.VMEM((tm, SW), F32), pltpu.VMEM((tm, SW), BF16), pltpu.VMEM((tm, SW), F32)],
        compiler_params=_params(1, VMEM_LIMIT_BIG),
    )(dh1, o, us, yatt, qkv, lsec, yb, lng, lnb, sgw, sgwt, sgbt, gatt, gsg, wout, gpm)


def _attn_operands(q, do, o, lsec, qw_ref, dow_ref):
    rows = do.shape[0]
    feat = lax.broadcasted_iota(jnp.int32, (AW, 128), 0)
    head = lax.broadcasted_iota(jnp.int32, (AW, 128), 1)
    sel = jnp.where((feat >= head * HD) & (feat < (head + 1) * HD), 1.0, 0.0)
    delta = jnp.dot(do * o, sel, precision=lax.Precision.HIGHEST, preferred_element_type=F32)
    for hp in range(4):
        cols = slice(hp * 128, (hp + 1) * 128)
        for hh in range(2):
            base = (2 * hp + hh) * 256
            lc = lsec[:, hp * 128 + hh:hp * 128 + hh + 1]
            d_h = delta[:, 2 * hp + hh:2 * hp + hh + 1]
            qw_ref[:, base:base + 128] = _head_rows(q[:, cols], hh, 0.125)
            qw_ref[:, base + 128:base + 256] = _aug_lanes(rows, AUG_A if hh == 0 else AUG_B, _split3(-lc))
            dow_ref[:, base:base + 128] = _head_rows(do[:, cols], hh, 1.0).astype(BF16)
            dow_ref[:, base + 128:base + 256] = _aug_lanes(rows, None, _split3(-d_h))


def _attn_bwd(kk, vk, kt, qw, dow, after):
    T = kk.shape[0]
    tq = _tile(T, 512)
    tk = tq
    nq = T // tq

    def body(kk_ref, vk_ref, kt_ref, qw_ref, dow_ref, after_ref, dqt_ref, dcq_ref, dk_ref, dv_ref, dck_ref, dk_s, dv_s, dck_s):
        j, i = _fold_cols(pl.program_id(1), pl.program_id(2), nq)
        sub8 = lax.broadcasted_iota(jnp.int32, (8, 1), 0)
        lane = lax.broadcasted_iota(jnp.int32, (1, 128), 1)

        @pl.when((pl.program_id(1) == 0) & (pl.program_id(2) == 0))
        def _():
            dqt_ref[...] = jnp.zeros_like(dqt_ref)
            dcq_ref[...] = jnp.zeros_like(dcq_ref)

        @pl.when(i == j)
        def _():
            dk_s[...] = jnp.zeros_like(dk_s)
            dv_s[...] = jnp.zeros_like(dv_s)
            dck_s[...] = jnp.zeros_like(dck_s)

        def step(masked):
            cols = pl.ds(pl.multiple_of(i * tq, tq), tq)
            sub = lax.broadcasted_iota(jnp.int32, (128, 1), 0)
            if masked:
                causal = lax.broadcasted_iota(jnp.int32, (tk, tq), 0) <= lax.broadcasted_iota(jnp.int32, (tk, tq), 1)

            def logits(n):
                pair, base = n // 2, n * 256
                return (_nt(kk_ref[:, pair * 256:(pair + 1) * 256], qw_ref[:, base:base + 256]),
                        _nt(vk_ref[:, pair * 256:(pair + 1) * 256], dow_ref[:, base:base + 256]))

            ahead = logits(0)
            dcq = jnp.zeros((8, tq), F32)
            dck = jnp.zeros((tk, 128), F32)
            for pp in range(PAIRS):
                lanes = slice(pp * 128, (pp + 1) * 128)
                kt2 = kt_ref[lanes, :] * 0.125
                dv = jnp.zeros((tk, 128), F32)
                dk = jnp.zeros((tk, 128), F32)
                dqts = []
                for hh in range(2):
                    head = 2 * pp + hh
                    base = head * 256
                    qw_h = qw_ref[:, base:base + 256]
                    dow_h = dow_ref[:, base:base + 256]
                    logp, dp = ahead
                    if head + 1 < 2 * PAIRS:
                        ahead = logits(head + 1)
                    pr = jnp.exp(logp)
                    if masked:
                        pr = jnp.where(causal, pr, 0.0)
                    ds = pr * dp
                    ds_b = ds.astype(BF16)
                    dv = dv + _nn(pr.astype(BF16), dow_h[:, 0:128])
                    dk = dk + _nn(ds_b, qw_h[:, 0:128])
                    dqts.append(_nn(kt2, ds_b))
                    dcq = dcq + jnp.where(sub8 == head, jnp.sum(ds, axis=0, keepdims=True), 0.0)
                    dck = dck - jnp.where(lane == head, jnp.sum(ds, axis=1, keepdims=True), 0.0)
                dv_s[:, lanes] += dv
                dk_s[:, lanes] += dk
                dqt_ref[lanes, cols] += jnp.where(sub < HD, dqts[0], dqts[1])
            dck_s[...] += dck
            dcq_ref[:, cols] += dcq

        @pl.when(i > j)
        def _():
            step(False)

        @pl.when(i == j)
        def _():
            step(True)

        @pl.when(i == nq - 1)
        def _():
            dk_ref[...] = dk_s[...].astype(BF16)
            dv_ref[...] = dv_s[...].astype(BF16)
            dck_ref[...] = dck_s[...]

    kj = lambda r, t: _fold_cols(r, t, nq)[0]
    qi = lambda r, t: _fold_cols(r, t, nq)[1]
    krow = lambda g, r, t: (kj(r, t), g)
    qrow = lambda g, r, t: (qi(r, t), g)
    return pl.pallas_call(
        body,
        name="attn_bwd",
        grid=_fold_grid(nq),
        in_specs=[
            pl.BlockSpec((tk, PAIRS * 256), krow),
            pl.BlockSpec((tk, PAIRS * 256), krow),
            pl.BlockSpec((PAIRS * 128, tk), lambda g, r, t: (g, kj(r, t))),
            pl.BlockSpec((tq, PAIRS * 512), qrow),
            pl.BlockSpec((tq, PAIRS * 512), qrow),
            pl.BlockSpec(after.shape, lambda g, r, t: (0,) * after.ndim),
        ],
        out_specs=[
            pl.BlockSpec((PAIRS * 128, T), lambda g, r, t: (g, 0), pipeline_mode=pl.Buffered(1)),
            pl.BlockSpec((8, T), lambda g, r, t: (0, 0), pipeline_mode=pl.Buffered(1)),
            pl.BlockSpec((tk, PAIRS * 128), krow),
            pl.BlockSpec((tk, PAIRS * 128), krow),
            pl.BlockSpec((tk, 128), lambda g, r, t: (kj(r, t), 0)),
        ],
        out_shape=[jax.ShapeDtypeStruct((AW, T), F32), jax.ShapeDtypeStruct((8, T), F32),
                   jax.ShapeDtypeStruct((T, AW), BF16), jax.ShapeDtypeStruct((T, AW), BF16),
                   jax.ShapeDtypeStruct((T, FPAD), F32)],
        scratch_shapes=[pltpu.VMEM((tk, PAIRS * 128), F32), pltpu.VMEM((tk, PAIRS * 128), F32),
                        pltpu.VMEM((tk, 128), F32)],
        compiler_params=_params(3),
    )(kk, vk, kt, qw, dow, after)


def _fgate_bwd(dcq, dck, fz, fb):
    T = dck.shape[0]
    tb = _tile(T, 512)
    nb = T // tb

    def body(dcq_ref, dck_ref, fz_ref, fb_ref, df_ref, dfb_ref, carry):
        @pl.when(pl.program_id(0) == 0)
        def _():
            carry[...] = jnp.zeros_like(carry)
            dfb_ref[...] = jnp.zeros_like(dfb_ref)

        head = lax.broadcasted_iota(jnp.int32, (8, FPAD), 0)
        eye = jnp.where(head == lax.broadcasted_iota(jnp.int32, (8, FPAD), 1), 1.0, 0.0)
        dcv = dck_ref[...] + lax.dot_general(dcq_ref[...], eye, (((0,), (0,)), ((), ())),
                                             precision=lax.Precision.HIGHEST, preferred_element_type=F32)
        r = lax.broadcasted_iota(jnp.int32, (128, 128), 0)
        cc = lax.broadcasted_iota(jnp.int32, (128, 128), 1)
        tri = (cc >= r).astype(F32)
        run = carry[...]
        parts = []
        for blk in reversed(range(tb // 128)):
            dc_b = dcv[blk * 128:(blk + 1) * 128, :]
            parts.append(jnp.dot(tri, dc_b, precision=lax.Precision.HIGHEST, preferred_element_type=F32) + run)
            run = run + jnp.sum(dc_b, axis=0, keepdims=True)
        carry[...] = run
        dlf = jnp.concatenate(parts[::-1], axis=0)
        lane = lax.broadcasted_iota(jnp.int32, (tb, FPAD), 1)
        df = jnp.where(lane < 8, dlf * jax.nn.sigmoid(-(fz_ref[...] + fb_ref[...])), 0.0)
        df_ref[...] = df.astype(BF16)
        dfb_ref[...] += jnp.sum(df, axis=0, keepdims=True)

    rev = pl.BlockSpec((tb, FPAD), lambda i: (nb - 1 - i, 0))
    return pl.pallas_call(
        body,
        name="fgate_bwd",
        grid=(nb,),
        in_specs=[pl.BlockSpec((8, tb), lambda i: (0, nb - 1 - i)), rev, rev, _full((1, FPAD))],
        out_specs=[rev, _full((1, FPAD))],
        out_shape=[jax.ShapeDtypeStruct((T, FPAD), BF16), jax.ShapeDtypeStruct((1, FPAD), F32)],
        scratch_shapes=[pltpu.VMEM((1, FPAD), F32)],
        compiler_params=_params(1),
    )(dcq, dck, fz, fb)


def _inproj_bwd(dqt, dk, dv, dus, dfz, wp, wqt, x, dh1, g):
    T = x.shape[0]
    tm = _tile(T, 512)

    def body(dq_ref, dk_ref, dv_ref, dus_ref, dfz_ref, w_ref, wqt_ref, x_ref, dh1_ref, g_ref, gx_ref, dg_ref):
        @pl.when(pl.program_id(0) == 0)
        def _():
            dg_ref[...] = jnp.zeros_like(dg_ref)

        da = _tn(dq_ref[...].astype(BF16), wqt_ref[...])
        da += _nt(dk_ref[...], w_ref[:, AW:2 * AW])
        da += _nt(dv_ref[...], w_ref[:, 2 * AW:NQKV])
        da += _nt(dus_ref[...], w_ref[:, NQKV:NQKV + NUS])
        da += _nt(dfz_ref[...], w_ref[:, NQKV + NUS:ZP])
        dx, dg = _rms_bwd(da, x_ref[...], g_ref[...])
        gx_ref[...] = dh1_ref[...] + dx
        dg_ref[...] += dg

    row = lambda n: pl.BlockSpec((tm, n), lambda i: (i, 0))
    return pl.pallas_call(
        body,
        name="inproj_bwd",
        grid=(T // tm,),
        in_specs=[pl.BlockSpec((AW, tm), lambda i: (0, i)), row(AW), row(AW), row(NUS), row(FPAD), _full((D, ZP)),
                  _full((AW, D)), row(D), row(D), _full((1, D))],
        out_specs=[row(D), _full((1, D))],
        out_shape=[jax.ShapeDtypeStruct((T, D), F32), jax.ShapeDtypeStruct((1, D), F32)],
        compiler_params=_params(1),
    )(dqt, dk, dv, dus, dfz, wp, wqt, x, dh1, g)


def _sq_relu(f1):
    r = jnp.maximum(f1.astype(F32), 0.0)
    return (r * r).astype(BF16)


def _wgrad(a, bs, name, a_fn=None, out_dtype=F32):
    T, K = a.shape
    tt = _tile(T, 512)
    nb = len(bs)
    narrow = out_dtype != F32

    def out_dims(b, layout):
        if layout == "t":
            return (K, b.shape[0])
        N = b.shape[1]
        if layout is None:
            return (K, N)
        return (N // layout[1], K, layout[1]) if layout[0] == "col" else (K // layout[1], layout[1], N)

    shapes = [out_dims(b, layout) for b, layout in bs]

    def body(*refs):
        a_ref, b_refs, o_refs = refs[0], refs[1:1 + nb], refs[1 + nb:1 + 2 * nb]
        accs = refs[1 + 2 * nb:] if narrow else o_refs

        @pl.when(pl.program_id(0) == 0)
        def _():
            for acc in accs:
                acc[...] = jnp.zeros_like(acc)

        av = a_ref[...] if a_fn is None else a_fn(a_ref[...])
        at = av.astype(BF16).T
        for (b, layout), b_ref, o_ref in zip(bs, b_refs, accs):
            if layout is None:
                o_ref[...] += _nn(at, b_ref[...].astype(BF16))
            elif layout == "t":
                o_ref[...] += _nt(at, b_ref[...].astype(BF16))
            elif layout[0] == "col":
                n = layout[1]
                for k in range(b.shape[1] // n):
                    o_ref[k] += _nn(at, b_ref[:, k * n:(k + 1) * n].astype(BF16))
            else:
                n = layout[1]
                bv = b_ref[...].astype(BF16)
                for k in range(K // n):
                    o_ref[k] += _nn(at[k * n:(k + 1) * n, :], bv)

        if narrow:
            @pl.when(pl.program_id(0) == T // tt - 1)
            def _():
                for o_ref, acc in zip(o_refs, accs):
                    o_ref[...] = acc[...].astype(out_dtype)

    once = lambda shape: pl.BlockSpec(shape, lambda t: (0,) * len(shape), pipeline_mode=pl.Buffered(1))
    res = pl.pallas_call(
        body,
        name=name,
        grid=(T // tt,),
        in_specs=[pl.BlockSpec((tt, K), lambda t: (t, 0))] + [
            pl.BlockSpec((b.shape[0], tt), lambda t: (0, t)) if layout == "t" else pl.BlockSpec((tt, b.shape[1]), lambda t: (t, 0))
            for b, layout in bs],
        out_specs=[once(s) for s in shapes],
        out_shape=[jax.ShapeDtypeStruct(s, out_dtype) for s in shapes],
        scratch_shapes=[pltpu.VMEM(s, F32) for s in shapes] if narrow else [],
        compiler_params=_params(1, VMEM_LIMIT_BIG),
    )(a, *[b for b, _ in bs])
    return res


def _adam_math(w, g, m, v):
    m = ADAM_B1 * m + (1.0 - ADAM_B1) * g
    v = ADAM_B2 * v + (1.0 - ADAM_B2) * (g * g)
    m_hat = m / (1.0 - ADAM_B1 ** ADAM_STEP)
    v_hat = v / (1.0 - ADAM_B2 ** ADAM_STEP)
    delta = -ADAM_LR * (m_hat / (jnp.sqrt(v_hat) + ADAM_EPS) + ADAM_WD * w)
    return delta, m, v


def _adam(parts, w, m, v, name):
    R, C = w.shape
    br = 128 if R % 128 == 0 else R

    def body(p_ref, w_ref, m_ref, v_ref, g_ref, d_ref, nm_ref, nv_ref):
        g = p_ref[0].astype(F32)
        for s in range(1, NDEV):
            g = g + p_ref[s].astype(F32)
        g_ref[...] = g
        d_ref[...], nm_ref[...], nv_ref[...] = _adam_math(w_ref[...], g, m_ref[...], v_ref[...])

    blk = pl.BlockSpec((br, C), lambda i: (i, 0))
    return pl.pallas_call(
        body,
        name=name,
        grid=(R // br,),
        in_specs=[pl.BlockSpec((NDEV, br, C), lambda i: (0, i, 0)), blk, blk, blk],
        out_specs=[blk] * 4,
        out_shape=[jax.ShapeDtypeStruct((R, C), F32)] * 4,
        compiler_params=_params(1),
    )(parts, w, m, v)


_SMALL = (("sg_w", 8 * CH * CH), ("f_bias", 8), ("sg_ln_g", SW), ("sg_ln_b", SW), ("sg_b", 8 * CH), ("att_out_g", AW),
          ("sg_out_g", SW), ("pre_mix_g", D), ("post_mix_g", D), ("pre_ffn_g", D), ("post_ffn_g", D), ("ple_gate_b", D))
_SEG = 8 * 128


def _seg_rows(size):
    return 8 * (-(-size // _SEG))


def _pack(vals, loss_acc):
    parts = []
    for name, size in _SMALL:
        flat = vals[name].reshape(-1)
        rows = _seg_rows(size)
        parts.append(jnp.pad(flat, (0, rows * 128 - size)).reshape(rows, 128))
    parts.append(loss_acc)
    return jnp.concatenate(parts, axis=0)


def _adam_small(parts, ws, ms, vs):
    n = len(_SMALL)
    names = [name for name, _ in _SMALL]

    def body(*refs):
        p_ref, w_refs, m_refs, v_refs = refs[0], refs[1:1 + n], refs[1 + n:1 + 2 * n], refs[1 + 2 * n:1 + 3 * n]
        loss_ref, outs = refs[1 + 3 * n], refs[2 + 3 * n:]
        g_all = p_ref[0]
        for s in range(1, NDEV):
            g_all = g_all + p_ref[s]
        r = 0
        for k, (name, size) in enumerate(_SMALL):
            dst = [outs[kind * n + k] for kind in range(4)]

            def update(g, idx):
                vals = (g,) + _adam_math(w_refs[k][idx], g, m_refs[k][idx], v_refs[k][idx])
                for d, val in zip(dst, vals):
                    d[idx] = val

            if name == "sg_w":
                for grp in range(8):
                    update(g_all[r + grp * CH:r + (grp + 1) * CH, :], (0, grp))
            elif name == "sg_b":
                update(g_all[r:r + 8, :], (0,))
            elif name == "f_bias":
                update(g_all[r:r + 1, 0:8], (slice(None),))
            else:
                update(jnp.concatenate([g_all[r + q:r + q + 1, :] for q in range(size // 128)], axis=1), (slice(None),))
            r += _seg_rows(size)
        loss_ref[...] = g_all[r:r + 1, 0:1] * (0.5 / D)

    arrs = [parts] + [d[name] for d in (ws, ms, vs) for name in names]
    res = pl.pallas_call(
        body,
        name="adam_small",
        in_specs=[_full(a.shape) for a in arrs],
        out_specs=[_full((1, 1))] + [_full(ws[name].shape) for _ in range(4) for name in names],
        out_shape=[jax.ShapeDtypeStruct((1, 1), F32)] + [jax.ShapeDtypeStruct(ws[name].shape, F32) for _ in range(4) for name in names],
        compiler_params=pltpu.CompilerParams(vmem_limit_bytes=VMEM_LIMIT),
    )(*arrs)
    return res[0], {name: [res[1 + kind * n + k] for kind in range(4)] for k, name in enumerate(names)}


def kernel(x, p, w_in, f_bias, sg_ln_g, sg_ln_b, sg_w, sg_b, att_out_g, sg_out_g, w_out, pre_mix_g, post_mix_g, pre_ffn_g, post_ffn_g, w_ff1, w_ff2, ple_w, ple_gate_w, ple_gate_b, loss_target, m_w_in, m_f_bias, m_sg_ln_g, m_sg_ln_b, m_sg_w, m_sg_b, m_att_out_g, m_sg_out_g, m_w_out, m_pre_mix_g, m_post_mix_g, m_pre_ffn_g, m_post_ffn_g, m_w_ff1, m_w_ff2, m_ple_w, m_ple_gate_w, m_ple_gate_b, v_w_in, v_f_bias, v_sg_ln_g, v_sg_ln_b, v_sg_w, v_sg_b, v_att_out_g, v_sg_out_g, v_w_out, v_pre_mix_g, v_post_mix_g, v_pre_ffn_g, v_post_ffn_g, v_w_ff1, v_w_ff2, v_ple_w, v_ple_gate_w, v_ple_gate_b):
    small_w = dict(sg_w=sg_w, f_bias=f_bias, sg_ln_g=sg_ln_g, sg_ln_b=sg_ln_b, sg_b=sg_b, att_out_g=att_out_g,
                   sg_out_g=sg_out_g, pre_mix_g=pre_mix_g, post_mix_g=post_mix_g, pre_ffn_g=pre_ffn_g,
                   post_ffn_g=post_ffn_g, ple_gate_b=ple_gate_b)
    small_m = dict(sg_w=m_sg_w, f_bias=m_f_bias, sg_ln_g=m_sg_ln_g, sg_ln_b=m_sg_ln_b, sg_b=m_sg_b, att_out_g=m_att_out_g,
                   sg_out_g=m_sg_out_g, pre_mix_g=m_pre_mix_g, post_mix_g=m_post_mix_g, pre_ffn_g=m_pre_ffn_g,
                   post_ffn_g=m_post_ffn_g, ple_gate_b=m_ple_gate_b)
    small_v = dict(sg_w=v_sg_w, f_bias=v_f_bias, sg_ln_g=v_sg_ln_g, sg_ln_b=v_sg_ln_b, sg_b=v_sg_b, att_out_g=v_att_out_g,
                   sg_out_g=v_sg_out_g, pre_mix_g=v_pre_mix_g, post_mix_g=v_post_mix_g, pre_ffn_g=v_pre_ffn_g,
                   post_ffn_g=v_post_ffn_g, ple_gate_b=v_ple_gate_b)
    big = dict(w_in=(w_in, m_w_in, v_w_in), w_out=(w_out, m_w_out, v_w_out), w_ff1=(w_ff1, m_w_ff1, v_w_ff1),
               w_ff2=(w_ff2, m_w_ff2, v_w_ff2), ple_w=(ple_w, m_ple_w, v_ple_w),
               ple_gate_w=(ple_gate_w, m_ple_gate_w, v_ple_gate_w))

    xt, pt, tgt = x[0], p[0, 0], loss_target[0]
    ws = W_IN_COLS // NDEV

    gw_in = _gather(w_in[0].astype(BF16), "gather_w_in")
    rest = _xchg_start([w_out[0].astype(BF16), w_ff1[0].astype(BF16), w_ff2[0].astype(BF16), ple_w[0].astype(BF16),
                        ple_gate_w[0].astype(BF16)], True, "gather_rest_start")
    win = jnp.transpose(gw_in, (1, 0, 2)).reshape(D, W_IN_COLS)
    wp = jnp.concatenate([win[:, 0:NQKV], win[:, NQKV + 8:W_IN_COLS], win[:, NQKV:NQKV + 8],
                          jnp.zeros((D, FPAD - 8), BF16)], axis=1)

    fb = jnp.pad(f_bias.astype(F32), ((0, 0), (0, FPAD - 8)))
    sgw = sg_w[0]
    sgwt = jnp.transpose(sg_w[0], (0, 2, 1))
    sgbt = jnp.transpose(sg_b[0])

    qkv, kt, vt, us, fz, ab = _inproj(xt, pre_mix_g + rest[4][0, 0], wp)
    ct, kk, vk = _fcum(fz, fb, qkv)
    yatt, lsec = _attn_fwd(qkv, kk, vt, ct)
    gw_out, gw1, gw2, gwpe, gwg = _xchg_wait(rest, True, yatt, "gather_rest_wait")
    wout = gw_out.reshape(D, D)
    wg = gwg.reshape(D, D)
    wpe = jnp.transpose(gwpe, (1, 0, 2)).reshape(PLE, D)
    h1, yb, o = _sgu_out(us, yatt, xt, sg_ln_g, sg_ln_b, sgw, sgbt, att_out_g, sg_out_g, wout, post_mix_g)
    c2b, f1b, ff, h2 = _ffn_fwd(h1, pre_ffn_g, gw1, gw2, post_ffn_g)
    dh2, dbg, loss_acc, g_g, g_pe = _ple_loss(h2, pt, tgt, wg, ple_gate_b, wpe)
    g_g = g_g.reshape(NDEV, D // NDEV, D)
    g_pe = jnp.transpose(g_pe.reshape(PLE, NDEV, D // NDEV), (1, 0, 2))

    dffb, df1, dh1, dgpostffn, dgpreffn = _ffn_bwd(dh2, ff, h1, f1b, gw1, gw2, post_ffn_g, pre_ffn_g)
    (g_1,) = _wgrad(c2b, [(df1, ("col", DFF // NDEV))], "wgrad_ff1")
    (g_2,) = _wgrad(f1b, [(dffb, ("row", DFF // NDEV))], "wgrad_ff2", a_fn=_sq_relu)
    early = _xchg_start([g_1, g_2, g_pe, g_g], False, "scatter_early_start")
    qw, dow, dus, g_out, dsgw, dsgbt, dlng, dlnb, dgatt, dgsg, dgpostmix = _mix_bwd(
        dh1, o, us, yatt, qkv, lsec, yb, sg_ln_g, sg_ln_b, sgw, sgwt, sgbt, att_out_g, sg_out_g, wout,
        post_mix_g + early[4][0, 0])
    mid = _xchg_start([g_out.reshape(NDEV, D // NDEV, D)], False, "scatter_mid_start")
    dqt, dcq, dk, dv, dck = _attn_bwd(kk, vk, kt, qw, dow, mid[4])
    dfz, dfb = _fgate_bwd(dcq, dck, fz, fb)

    gq, gk, gv, gus, gf = _wgrad(ab, [(dqt, "t"), (dk, None), (dv, None), (dus, None), (dfz, None)], "wgrad_in",
                                 out_dtype=BF16)
    g_in = jnp.concatenate([gq, gk, gv, gf[:, 0:8], gus], axis=1)
    g_in = jnp.transpose(g_in.reshape(D, NDEV, ws), (1, 0, 2))
    late = _xchg_start([g_in], False, "scatter_late_start")
    grad_x, dgpremix = _inproj_bwd(dqt, dk, dv, dus, dfz, wp, jnp.transpose(wp[:, 0:AW]), xt, dh1,
                                   pre_mix_g + late[4][0, 0])

    small_g = dict(sg_w=dsgw, f_bias=dfb[:, 0:8], sg_ln_g=dlng, sg_ln_b=dlnb, sg_b=jnp.transpose(dsgbt[:, 0:8]),
                   att_out_g=dgatt, sg_out_g=dgsg, pre_mix_g=dgpremix, post_mix_g=dgpostmix, pre_ffn_g=dgpreffn,
                   post_ffn_g=dgpostffn, ple_gate_b=dbg)

    r_1, r_2, r_pe, r_g = _xchg_wait(early, False, grad_x, "scatter_early_wait")
    (r_out,) = _xchg_wait(mid, False, grad_x, "scatter_mid_wait")
    r_small = _gather(_pack(small_g, loss_acc), "gather_small_grads")

    res = {}

    def adam_big(name, parts):
        w, m, v = big[name]
        res[name] = [t[None] for t in _adam(parts, w[0], m[0], v[0], "adam_" + name)]

    for name, parts in (("w_out", r_out), ("w_ff1", r_1), ("w_ff2", r_2), ("ple_w", r_pe), ("ple_gate_w", r_g)):
        adam_big(name, parts)
    (r_in,) = _xchg_wait(late, False, res["w_ff1"][0], "scatter_late_wait")
    adam_big("w_in", r_in)
    loss, small = _adam_small(r_small, small_w, small_m, small_v)
    res.update(small)

    order = ["w_in", "f_bias", "sg_ln_g", "sg_ln_b", "sg_w", "sg_b", "att_out_g", "sg_out_g", "w_out", "pre_mix_g",
             "post_mix_g", "pre_ffn_g", "post_ffn_g", "w_ff1", "w_ff2", "ple_w", "ple_gate_w", "ple_gate_b"]
    outs = [loss[0, 0], grad_x[None]]
    for kind in range(4):
        outs += [res[name][kind] for name in order]
    return tuple(outs)
```

```python
import jax
import jax.numpy as jnp
from jax import lax
from jax.experimental import pallas as pl
from jax.experimental.pallas import tpu as pltpu

F32 = jnp.float32
BF16 = jnp.bfloat16

NDEV = 8
D = 1024
AW = 512
SW = 512
HD = 64
CH = 128
DFF = 4096
PLE = 256
NQKV = 3 * AW
NUS = 2 * SW
FPAD = 128
ZP = NQKV + NUS + FPAD
W_IN_COLS = 2568
EPS = 1e-6
MASKV = -1e30
GELU_K = 0.7978845608028654
GELU_C = 0.044715

ADAM_LR = 0.001
ADAM_B1 = 0.9
ADAM_B2 = 0.999
ADAM_EPS = 1e-08
ADAM_WD = 0.01
ADAM_STEP = 10

VMEM_LIMIT = 48 * 1024 * 1024
VMEM_LIMIT_BIG = 60 * 1024 * 1024


def _nn(a, b):
    return jnp.dot(a, b, preferred_element_type=F32)


def _nt(a, b):
    return lax.dot_general(a, b, (((1,), (1,)), ((), ())), preferred_element_type=F32)


def _tn(a, b):
    return lax.dot_general(a, b, (((0,), (0,)), ((), ())), preferred_element_type=F32)


def _tile(n, pref):
    return min(n, pref)


def _params(n_axes, vmem=VMEM_LIMIT):
    return pltpu.CompilerParams(dimension_semantics=("arbitrary",) * n_axes, vmem_limit_bytes=vmem)


def _full(shape):
    nd = len(shape)
    return pl.BlockSpec(shape, lambda *_: (0,) * nd)


def _rms_fwd(x, g):
    r = lax.rsqrt(jnp.mean(x * x, axis=-1, keepdims=True) + EPS)
    return x * r * g


def _rms_bwd(dy, x, g):
    n = x.shape[-1]
    r = lax.rsqrt(jnp.mean(x * x, axis=-1, keepdims=True) + EPS)
    u = dy * g
    s = jnp.sum(x * u, axis=-1, keepdims=True)
    dx = r * u - x * (r * r * r * (s * (1.0 / n)))
    dg = jnp.sum(dy * (x * r), axis=0, keepdims=True)
    return dx, dg


def _gelu(x):
    t = jnp.tanh(x * (GELU_K + (GELU_K * GELU_C) * (x * x)))
    return x * (0.5 + 0.5 * t), t


def _gelu_grad(x, t):
    return (0.5 + 0.5 * t) + (0.5 * x) * (1.0 - t * t) * (GELU_K + (3.0 * GELU_K * GELU_C) * (x * x))


def _gather(arr, name):
    def body(x_ref, out_ref, send, recv, loc):
        x, y, c = lax.axis_index("x"), lax.axis_index("y"), lax.axis_index("c")
        me, sibling = (x, y, c), (x, y, 1 - c)
        chips = [(1 - x, y), (x, 1 - y), (1 - x, 1 - y)]

        def slot(px, py, pc):
            return out_ref.at[4 * px + 2 * py + pc]

        def copy(k, block, to, src=None):
            return pltpu.make_async_remote_copy(
                src_ref=slot(*block) if src is None else src, dst_ref=slot(*block), send_sem=send.at[k],
                recv_sem=recv.at[k], device_id=to, device_id_type=pl.DeviceIdType.MESH)

        mine = pltpu.make_async_copy(x_ref, slot(*me), loc)
        mine.start()
        first = [copy(0, me, sibling, src=x_ref)] + [copy(1 + j, me, (*chip, c), src=x_ref) for j, chip in enumerate(chips)]
        for cp in first:
            cp.start()
        passed = [copy(4 + j, (*chip, c), sibling) for j, chip in enumerate(chips)]
        for j, chip in enumerate(chips):
            copy(1 + j, (*chip, c), me).wait_recv()
            passed[j].start()
        copy(0, sibling, me).wait_recv()
        for j, chip in enumerate(chips):
            copy(4 + j, (*chip, 1 - c), me).wait_recv()
        for cp in first + passed:
            cp.wait_send()
        mine.wait()

    hbm = pl.BlockSpec(memory_space=pltpu.HBM)
    return pl.pallas_call(
        body,
        name=name,
        out_shape=jax.ShapeDtypeStruct((NDEV,) + arr.shape, arr.dtype),
        in_specs=[hbm],
        out_specs=hbm,
        scratch_shapes=[pltpu.SemaphoreType.DMA((NDEV - 1,)), pltpu.SemaphoreType.DMA((NDEV - 1,)), pltpu.SemaphoreType.DMA],
    )(arr)


def _peers(x, y, c):
    out = []
    for k in range(1, NDEV):
        out.append((1 - x if (k >> 2) & 1 else x, 1 - y if (k >> 1) & 1 else y, 1 - c if k & 1 else c))
    return out


def _xchg_start(arrs, gather, name):
    n = len(arrs)
    me = 4 * lax.axis_index("x") + 2 * lax.axis_index("y") + lax.axis_index("c")
    lands = []
    for a in arrs:
        shape = ((NDEV,) + a.shape) if gather else a.shape
        own = a[None] if gather else lax.dynamic_slice_in_dim(a, me, 1, axis=0)
        lands.append(lax.dynamic_update_slice_in_dim(lax.empty(shape, a.dtype), own, me, axis=0))

    def body(*refs):
        ins, lnd = refs[:n], refs[n:2 * n]
        send, recv, token = refs[2 * n:3 * n], refs[3 * n:4 * n], refs[-1]
        x, y, c = lax.axis_index("x"), lax.axis_index("y"), lax.axis_index("c")
        mine = 4 * x + 2 * y + c
        for px, py, pc in _peers(x, y, c):
            peer = 4 * px + 2 * py + pc
            for a in range(n):
                pltpu.make_async_remote_copy(
                    src_ref=ins[a] if gather else ins[a].at[peer],
                    dst_ref=lnd[a].at[mine],
                    send_sem=send[a],
                    recv_sem=recv[a],
                    device_id=(px, py, pc),
                    device_id_type=pl.DeviceIdType.MESH,
                ).start()
        token[...] = jnp.zeros_like(token)

    hbm = pl.BlockSpec(memory_space=pltpu.HBM)
    sem = pl.BlockSpec(memory_space=pltpu.SEMAPHORE)
    res = pl.pallas_call(
        body,
        name=name,
        out_shape=(*[pltpu.SemaphoreType.DMA(())] * (2 * n),
                   *[pltpu.HBM(a.shape, a.dtype) for a in arrs], *[pltpu.HBM(l.shape, l.dtype) for l in lands],
                   jax.ShapeDtypeStruct((8, 128), F32)),
        in_specs=[hbm] * (2 * n),
        out_specs=(*([sem] * (2 * n)), *([hbm] * (2 * n)), pl.BlockSpec(memory_space=pltpu.VMEM)),
        input_output_aliases={i: 2 * n + i for i in range(2 * n)},
        compiler_params=pltpu.CompilerParams(has_side_effects=pltpu.SideEffectType.DATAFLOW_SIDE_EFFECTING),
    )(*[pltpu.with_memory_space_constraint(a, pltpu.HBM) for a in arrs],
      *[pltpu.with_memory_space_constraint(l, pltpu.HBM) for l in lands])
    return list(res[0:n]), list(res[n:2 * n]), list(res[2 * n:3 * n]), list(res[3 * n:4 * n]), res[-1]


def _xchg_wait(started, gather, after, name):
    send, recv, srcs, lands, _ = started
    n = len(srcs)

    def body(*refs):
        lnd = refs[n:2 * n]
        send, recv = refs[2 * n:3 * n], refs[3 * n:4 * n]
        me = (lax.axis_index("x"), lax.axis_index("y"), lax.axis_index("c"))
        for a in range(n):
            seven = lnd[a].at[pl.ds(0, NDEV - 1)]
            cp = pltpu.make_async_remote_copy(src_ref=seven, dst_ref=seven, send_sem=send[a], recv_sem=recv[a],
                                              device_id=me, device_id_type=pl.DeviceIdType.MESH)
            cp.wait_send()
            cp.wait_recv()

    hbm = pl.BlockSpec(memory_space=pltpu.HBM)
    sem = pl.BlockSpec(memory_space=pltpu.SEMAPHORE)
    res = pl.pallas_call(
        body,
        name=name,
        out_shape=tuple([pltpu.HBM(a.shape, a.dtype) for a in srcs] + [pltpu.HBM(l.shape, l.dtype) for l in lands]),
        in_specs=[hbm] * (2 * n) + [sem] * (2 * n) + [pl.BlockSpec(memory_space=pl.ANY)],
        out_specs=tuple([hbm] * (2 * n)),
        input_output_aliases={i: i for i in range(2 * n)},
        compiler_params=pltpu.CompilerParams(has_side_effects=pltpu.SideEffectType.DATAFLOW_SIDE_EFFECTING),
    )(*srcs, *lands, *send, *recv, after)
    return list(res[n:])


def _inproj(x, g, wtp):
    T = x.shape[0]
    tm = _tile(T, 512)

    def body(x_ref, g_ref, w_ref, qkv_ref, kt_ref, vt_ref, us_ref, fz_ref, ab_ref):
        a = _rms_fwd(x_ref[...], g_ref[...]).astype(BF16)
        ab_ref[...] = a
        qkv_ref[:, 0:AW] = _nt(a, w_ref[0:AW, :]).astype(BF16)
        kk = _nt(a, w_ref[AW:2 * AW, :])
        qkv_ref[:, AW:2 * AW] = kk.astype(BF16)
        kt_ref[...] = kk.T.astype(BF16)
        vv = _nt(a, w_ref[2 * AW:NQKV, :])
        qkv_ref[:, 2 * AW:NQKV] = vv.astype(BF16)
        vt_ref[...] = vv.T.astype(BF16)
        us_ref[...] = _nt(a, w_ref[NQKV:NQKV + NUS, :])
        fz_ref[...] = _nt(a, w_ref[NQKV + NUS:ZP, :])

    row = lambda n: pl.BlockSpec((tm, n), lambda i: (i, 0))
    col = pl.BlockSpec((AW, tm), lambda i: (0, i))
    return pl.pallas_call(
        body,
        name="inproj",
        grid=(T // tm,),
        in_specs=[row(D), _full((1, D)), _full((ZP, D))],
        out_specs=[row(NQKV), col, col, row(NUS), row(FPAD), row(D)],
        out_shape=[
            jax.ShapeDtypeStruct((T, NQKV), BF16),
            jax.ShapeDtypeStruct((AW, T), BF16),
            jax.ShapeDtypeStruct((AW, T), BF16),
            jax.ShapeDtypeStruct((T, NUS), F32),
            jax.ShapeDtypeStruct((T, FPAD), F32),
            jax.ShapeDtypeStruct((T, D), BF16),
        ],
        compiler_params=_params(1),
    )(x, g, wtp)


def _log_sigmoid(z):
    return jnp.minimum(z, 0.0) - jnp.log1p(jnp.exp(-jnp.abs(z)))


def _split3(x):
    hi = x.astype(BF16)
    r1 = x - hi.astype(F32)
    mid = r1.astype(BF16)
    lo = (r1 - mid.astype(F32)).astype(BF16)
    return hi, mid, lo


AUG_A, AUG_B, AUG_ONE = 0, 3, 6


def _aug_lanes(rows, first_one, pieces):
    lane = lax.broadcasted_iota(jnp.int32, (rows, 128), 1)
    out = jnp.zeros((rows, 128), F32)
    if first_one is not None:
        out = jnp.where((lane >= first_one) & (lane < first_one + 3), 1.0, out)
    for n, piece in enumerate(pieces):
        out = jnp.where(lane == AUG_ONE + n, piece.astype(F32), out)
    return out.astype(BF16)


def _fcum(fz, fb, qkv):
    T = fz.shape[0]
    tb = _tile(T, 512)

    def body(fz_ref, fb_ref, k_ref, v_ref, ct_ref, kk_ref, vk_ref, carry, tri_s, sel_s):
        @pl.when(pl.program_id(0) == 0)
        def _():
            carry[...] = jnp.zeros_like(carry)
            src = lax.broadcasted_iota(jnp.int32, (128, 128), 0)
            dst = lax.broadcasted_iota(jnp.int32, (128, 128), 1)
            tri_s[...] = (dst <= src).astype(F32)
            for hp in range(4):
                for n in range(3):
                    pick = ((src == 2 * hp) & (dst == AUG_A + n)) | ((src == 2 * hp + 1) & (dst == AUG_B + n))
                    sel_s[3 * hp + n] = jnp.where(pick, -1.0, 0.0).astype(BF16)

        lf = _log_sigmoid(fz_ref[...] + fb_ref[...])
        run = carry[...]
        parts = []
        for blk in range(tb // 128):
            lf_b = lf[blk * 128:(blk + 1) * 128, :]
            parts.append(jnp.dot(tri_s[...], lf_b, precision=lax.Precision.HIGHEST, preferred_element_type=F32) + run)
            run = run + jnp.sum(lf_b, axis=0, keepdims=True)
        carry[...] = run
        cs = jnp.concatenate(parts, axis=0)
        ct_ref[...] = cs.T[0:8, :]

        pieces = _split3(cs)
        lane = lax.broadcasted_iota(jnp.int32, (1, 128), 1)
        ones = jnp.where((lane >= AUG_ONE) & (lane < AUG_ONE + 3), 1.0, 0.0)
        for hp in range(4):
            aug = jnp.zeros((tb, 128), F32) + ones
            for n, piece in enumerate(pieces):
                aug = aug + _nn(piece, sel_s[3 * hp + n])
            aug = aug.astype(BF16)
            kk_ref[:, hp * 256:hp * 256 + 128] = k_ref[:, hp * 128:(hp + 1) * 128]
            kk_ref[:, hp * 256 + 128:(hp + 1) * 256] = aug
            vk_ref[:, hp * 256:hp * 256 + 128] = v_ref[:, hp * 128:(hp + 1) * 128]
            vk_ref[:, hp * 256 + 128:(hp + 1) * 256] = aug

    row = lambda n, c: pl.BlockSpec((tb, n), lambda i: (i, c))
    return pl.pallas_call(
        body,
        name="fcum",
        grid=(T // tb,),
        in_specs=[row(FPAD, 0), _full((1, FPAD)), row(AW, 1), row(AW, 2)],
        out_specs=[pl.BlockSpec((8, tb), lambda i: (0, i)), row(2 * AW, 0), row(2 * AW, 0)],
        out_shape=[jax.ShapeDtypeStruct((8, T), F32), jax.ShapeDtypeStruct((T, 2 * AW), BF16),
                   jax.ShapeDtypeStruct((T, 2 * AW), BF16)],
        scratch_shapes=[pltpu.VMEM((1, FPAD), F32), pltpu.VMEM((128, 128), F32), pltpu.VMEM((12, 128, 128), BF16)],
        compiler_params=_params(1),
    )(fz, fb, qkv, qkv)


def _fold_rows(r, t, nq):
    if nq == 1:
        return r, t
    low = t <= r
    return jnp.where(low, r, nq - 1 - r), jnp.where(low, t, t - r - 1)


def _fold_cols(r, t, nq):
    if nq == 1:
        return r, t
    first = t < nq - r
    j = jnp.where(first, r, nq - 1 - r)
    return j, jnp.where(first, r + t, nq - 1 - r + (t - (nq - r)))


PAIRS = 4


def _fold_grid(nq):
    assert (nq == 1 or nq % 2 == 0) and PAIRS == 4
    return (4 // PAIRS, 1, 1) if nq == 1 else (4 // PAIRS, nq // 2, nq + 1)


def _head_rows(x2, hh, scale):
    is_a = lax.broadcasted_iota(jnp.int32, (1, 128), 1) < HD
    keep = is_a if hh == 0 else jnp.logical_not(is_a)
    return jnp.where(keep, x2, jnp.zeros_like(x2)) * scale


def _attn_fwd(qkv, kk, vt, ct):
    T = qkv.shape[0]
    tq = _tile(T, 512)
    tk = tq
    nq = T // tq

    def body(q_ref, kk_ref, vt_ref, ctq_ref, o_ref, lsec_ref, qw_s, m_s, l_s, acc_s):
        i, j = _fold_rows(pl.program_id(1), pl.program_id(2), nq)
        sub8 = lax.broadcasted_iota(jnp.int32, (8, 1), 0)

        def cref_of(pp, hh):
            head = 2 * (PAIRS * pl.program_id(0) + pp) + hh
            return jnp.sum(jnp.where(sub8 == head, ctq_ref[:, 0:1], 0.0), axis=0, keepdims=True)

        @pl.when(j == 0)
        def _():
            for pp in range(PAIRS):
                q2 = q_ref[:, pp * 128:(pp + 1) * 128]
                for hh in range(2):
                    rows = slice(hh * tq, (hh + 1) * tq)
                    qw_s[pp, rows, 0:128] = _head_rows(q2, hh, 0.125)
                    qw_s[pp, rows, 128:256] = _aug_lanes(tq, AUG_A if hh == 0 else AUG_B, _split3(cref_of(pp, hh)))
            m_s[...] = jnp.full_like(m_s, MASKV)
            l_s[...] = jnp.zeros_like(l_s)
            acc_s[...] = jnp.zeros_like(acc_s)

        def step(masked):
            if masked:
                causal = lax.broadcasted_iota(jnp.int32, (tk, tq), 0) <= lax.broadcasted_iota(jnp.int32, (tk, tq), 1)
            logits = lambda pp: _nt(kk_ref[:, pp * 256:(pp + 1) * 256], qw_s[pp])
            s_next = logits(0)
            for pp in range(PAIRS):
                s2 = s_next
                if pp + 1 < PAIRS:
                    s_next = logits(pp + 1)
                vt2 = vt_ref[pp * 128:(pp + 1) * 128, :]
                for hh in range(2):
                    n = 2 * pp + hh
                    s = s2[:, hh * tq:(hh + 1) * tq]
                    if masked:
                        s = jnp.where(causal, s, MASKV)
                    m_prev = m_s[n]
                    m_new = jnp.maximum(m_prev, jnp.max(s, axis=0, keepdims=True))
                    pr = jnp.exp(s - m_new)
                    alpha = jnp.exp(m_prev - m_new)
                    l_s[n] = alpha * l_s[n] + jnp.sum(pr, axis=0, keepdims=True)
                    m_s[n] = m_new
                    acc_s[n] = alpha * acc_s[n] + _nn(vt2, pr.astype(BF16))

        @pl.when(j < i)
        def _():
            step(False)

        @pl.when(j == i)
        def _():
            step(True)
            sub = lax.broadcasted_iota(jnp.int32, (128, 1), 0)
            for pp in range(PAIRS):
                a, b = 2 * pp, 2 * pp + 1
                ot = jnp.where(sub < HD, acc_s[a] * (1.0 / l_s[a]), acc_s[b] * (1.0 / l_s[b]))
                o_ref[:, pp * 128:(pp + 1) * 128] = ot.T
                lrow = [m_s[2 * pp + hh] + jnp.log(l_s[2 * pp + hh]) - cref_of(pp, hh) for hh in range(2)]
                lsec_ref[:, pp * 128:(pp + 1) * 128] = jnp.where(sub == 0, lrow[0], jnp.where(sub == 1, lrow[1], 0.0)).T

    qi = lambda r, t: _fold_rows(r, t, nq)[0]
    kj = lambda r, t: _fold_rows(r, t, nq)[1]
    return pl.pallas_call(
        body,
        name="attn_fwd",
        grid=_fold_grid(nq),
        in_specs=[
            pl.BlockSpec((tq, PAIRS * 128), lambda g, r, t: (qi(r, t), g)),
            pl.BlockSpec((tk, PAIRS * 256), lambda g, r, t: (kj(r, t), g)),
            pl.BlockSpec((PAIRS * 128, tk), lambda g, r, t: (g, kj(r, t))),
            pl.BlockSpec((8, tq), lambda g, r, t: (0, qi(r, t))),
        ],
        out_specs=[pl.BlockSpec((tq, PAIRS * 128), lambda g, r, t: (qi(r, t), g))] * 2,
        out_shape=[jax.ShapeDtypeStruct((T, AW), F32)] * 2,
        scratch_shapes=[pltpu.VMEM((PAIRS, 2 * tq, 256), BF16), pltpu.VMEM((2 * PAIRS, 1, tq), F32),
                        pltpu.VMEM((2 * PAIRS, 1, tq), F32), pltpu.VMEM((2 * PAIRS, 128, tq), F32)],
        compiler_params=_params(3),
    )(qkv, kk, vt, ct)


def _sgu_forward(us_ref, lng, lnb, w_ref, bt_ref, mixed_s, vnb_s, tm):
    is_a = lax.broadcasted_iota(jnp.int32, (1, 128), 1) < HD
    u = us_ref[:, 0:SW]
    vs = us_ref[:, SW:NUS]
    ug, tu = _gelu(u)
    vg, tv = _gelu(vs)
    mu = jnp.mean(vg, axis=-1, keepdims=True)
    xc = vg - mu
    rstd = lax.rsqrt(jnp.mean(xc * xc, axis=-1, keepdims=True) + EPS)
    vhat = xc * rstd
    vnb_s[...] = (vhat * lng + lnb).astype(BF16)
    rr = lax.broadcasted_iota(jnp.int32, (CH, CH), 0)
    cc = lax.broadcasted_iota(jnp.int32, (CH, CH), 1)
    tril = cc <= rr
    for jj in range(4):
        wa = jnp.where(tril, w_ref[2 * jj], 0.0).astype(BF16)
        wb = jnp.where(tril, w_ref[2 * jj + 1], 0.0).astype(BF16)
        ba = bt_ref[:, 2 * jj:2 * jj + 1]
        bb = bt_ref[:, 2 * jj + 1:2 * jj + 2]
        for ch in range(tm // CH):
            rs, cs = slice(ch * CH, (ch + 1) * CH), slice(jj * 128, (jj + 1) * 128)
            vn2 = vnb_s[rs, cs]
            mixed_s[rs, cs] = jnp.where(is_a, _nn(wa, vn2) + ba, _nn(wb, vn2) + bb)
    mixed = mixed_s[...]
    return u, vs, ug, tu, tv, vhat, rstd, mixed, ug * mixed


def _sgu_out(us, yatt, x, lng, lnb, sgw, sgbt, gatt, gsg, wout, gpm):
    T = us.shape[0]
    tm = _tile(T, 512)

    def body(us_ref, ya_ref, x_ref, lng_ref, lnb_ref, w_ref, bt_ref, ga_ref, gs_ref, wo_ref, gp_ref,
             h1_ref, yb_ref, o_ref, mixed_s, vnb_s):
        ysg = _sgu_forward(us_ref, lng_ref[...], lnb_ref[...], w_ref, bt_ref, mixed_s, vnb_s, tm)[-1]
        yb_ref[:, 0:AW] = _rms_fwd(ya_ref[...], ga_ref[...]).astype(BF16)
        yb_ref[:, AW:D] = _rms_fwd(ysg, gs_ref[...]).astype(BF16)
        o = _nn(yb_ref[...], wo_ref[...])
        o_ref[...] = o
        h1_ref[...] = x_ref[...] + _rms_fwd(o, gp_ref[...])

    row = lambda n: pl.BlockSpec((tm, n), lambda i: (i, 0))
    return pl.pallas_call(
        body,
        name="sgu_out",
        grid=(T // tm,),
        in_specs=[row(NUS), row(AW), row(D), _full((1, SW)), _full((1, SW)), _full((8, CH, CH)), _full((CH, 8)),
                  _full((1, AW)), _full((1, SW)), _full((D, D)), _full((1, D))],
        out_specs=[row(D), row(D), row(D)],
        out_shape=[jax.ShapeDtypeStruct((T, D), F32), jax.ShapeDtypeStruct((T, D), BF16), jax.ShapeDtypeStruct((T, D), F32)],
        scratch_shapes=[pltpu.VMEM((tm, SW), F32), pltpu.VMEM((tm, SW), BF16)],
        compiler_params=_params(1),
    )(us, yatt, x, lng, lnb, sgw, sgbt, gatt, gsg, wout, gpm)


def _ffn_fwd(h1, gpre, w1g, w2g, gpost):
    T = h1.shape[0]
    tm = _tile(T, 512)
    nb, hb = w1g.shape[0], w1g.shape[2]

    def body(h1_ref, gpre_ref, w1_ref, w2_ref, gpost_ref, c2_ref, f1_ref, ff_ref, h2_ref):
        h1 = h1_ref[...]
        c2 = _rms_fwd(h1, gpre_ref[...]).astype(BF16)
        c2_ref[...] = c2
        ff = jnp.zeros((tm, D), F32)
        for j in range(nb):
            f1 = _nn(c2, w1_ref[j])
            f1_ref[:, j * hb:(j + 1) * hb] = f1.astype(BF16)
            r = jnp.maximum(f1, 0.0)
            ff = ff + _nn((r * r).astype(BF16), w2_ref[j])
        ff_ref[...] = ff
        h2_ref[...] = h1 + _rms_fwd(ff, gpost_ref[...])

    row = lambda n: pl.BlockSpec((tm, n), lambda i: (i, 0))
    once = lambda shape: pl.BlockSpec(shape, lambda i: (0,) * len(shape), pipeline_mode=pl.Buffered(1))
    return pl.pallas_call(
        body,
        name="ffn_fwd",
        grid=(T // tm,),
        in_specs=[row(D), _full((1, D)), once((nb, D, hb)), once((nb, hb, D)), _full((1, D))],
        out_specs=[row(D), row(DFF), row(D), row(D)],
        out_shape=[jax.ShapeDtypeStruct((T, D), BF16), jax.ShapeDtypeStruct((T, DFF), BF16),
                   jax.ShapeDtypeStruct((T, D), F32), jax.ShapeDtypeStruct((T, D), F32)],
        compiler_params=_params(1, VMEM_LIMIT_BIG),
    )(h1, gpre, w1g, w2g, gpost)


def _ple_loss(h2, p, tgt, wg, bg, wpe):
    T = h2.shape[0]
    tm = _tile(T, 512)

    def body(h2_ref, p_ref, t_ref, wg_ref, bg_ref, wpe_ref, dh2_ref, dbg_ref, loss_ref, dwg_ref, dwpe_ref):
        @pl.when(pl.program_id(0) == 0)
        def _():
            for r in (dbg_ref, loss_ref, dwg_ref, dwpe_ref):
                r[...] = jnp.zeros_like(r)

        h2 = h2_ref[...]
        h2b = h2.astype(BF16)
        gate = jax.nn.sigmoid(_nn(h2b, wg_ref[...]) + bg_ref[...])
        pb = p_ref[...].astype(BF16)
        pe = _nn(pb, wpe_ref[...])
        diff = (h2 + gate * pe) - t_ref[...]
        loss_ref[...] += jnp.sum(diff * diff)
        dh3 = diff * (1.0 / D)
        dpre = (dh3 * pe) * (gate * (1.0 - gate))
        dpre_b = dpre.astype(BF16)
        dbg_ref[...] += jnp.sum(dpre, axis=0, keepdims=True)
        dh2_ref[...] = dh3 + _nt(dpre_b, wg_ref[...])
        dwg_ref[...] += _nn(h2b.T, dpre_b)
        dwpe_ref[...] += _nn(pb.T, (dh3 * gate).astype(BF16))

    row = lambda n: pl.BlockSpec((tm, n), lambda i: (i, 0))
    once = lambda shape: pl.BlockSpec(shape, lambda i: (0,) * len(shape), pipeline_mode=pl.Buffered(1))
    return pl.pallas_call(
        body,
        name="ple_loss",
        grid=(T // tm,),
        in_specs=[row(D), row(PLE), row(D), _full((D, D)), _full((1, D)), _full((PLE, D))],
        out_specs=[row(D), _full((1, D)), _full((8, 128)), once((D, D)), once((PLE, D))],
        out_shape=[jax.ShapeDtypeStruct((T, D), F32), jax.ShapeDtypeStruct((1, D), F32),
                   jax.ShapeDtypeStruct((8, 128), F32), jax.ShapeDtypeStruct((D, D), F32),
                   jax.ShapeDtypeStruct((PLE, D), F32)],
        compiler_params=_params(1),
    )(h2, p, tgt, wg, bg, wpe)


def _ffn_bwd(dh2, ff, h1, f1, w1g, w2g, gpost, gpre):
    T = dh2.shape[0]
    tm = _tile(T, 512)
    nb, hb = w1g.shape[0], w1g.shape[2]

    def body(dh2_ref, ff_ref, h1_ref, f1_ref, w1_ref, w2_ref, gpost_ref, gpre_ref,
             dffb_ref, df1_ref, dh1_ref, dgpost_ref, dgpre_ref):
        @pl.when(pl.program_id(0) == 0)
        def _():
            dgpost_ref[...] = jnp.zeros_like(dgpost_ref)
            dgpre_ref[...] = jnp.zeros_like(dgpre_ref)

        dh2 = dh2_ref[...]
        dff, dg = _rms_bwd(dh2, ff_ref[...], gpost_ref[...])
        dffb = dff.astype(BF16)
        dffb_ref[...] = dffb
        dgpost_ref[...] += dg
        dc2 = jnp.zeros((tm, D), F32)
        for j in range(nb):
            cols = slice(j * hb, (j + 1) * hb)
            dact = _nt(dffb, w2_ref[j])
            df1 = (dact * (2.0 * jnp.maximum(f1_ref[:, cols].astype(F32), 0.0))).astype(BF16)
            df1_ref[:, cols] = df1
            dc2 = dc2 + _nt(df1, w1_ref[j])
        dx, dg = _rms_bwd(dc2, h1_ref[...], gpre_ref[...])
        dh1_ref[...] = dh2 + dx
        dgpre_ref[...] += dg

    row = lambda n: pl.BlockSpec((tm, n), lambda i: (i, 0))
    once = lambda shape: pl.BlockSpec(shape, lambda i: (0,) * len(shape), pipeline_mode=pl.Buffered(1))
    return pl.pallas_call(
        body,
        name="ffn_bwd",
        grid=(T // tm,),
        in_specs=[row(D), row(D), row(D), row(DFF), once((nb, D, hb)), once((nb, hb, D)), _full((1, D)), _full((1, D))],
        out_specs=[row(D), row(DFF), row(D), _full((1, D)), _full((1, D))],
        out_shape=[jax.ShapeDtypeStruct((T, D), BF16), jax.ShapeDtypeStruct((T, DFF), BF16),
                   jax.ShapeDtypeStruct((T, D), F32), jax.ShapeDtypeStruct((1, D), F32),
                   jax.ShapeDtypeStruct((1, D), F32)],
        compiler_params=_params(1, VMEM_LIMIT_BIG),
    )(dh2, ff, h1, f1, w1g, w2g, gpost, gpre)


def _mix_bwd(dh1, o, us, yatt, qkv, lsec, yb, lng, lnb, sgw, sgwt, sgbt, gatt, gsg, wout, gpm):
    T = dh1.shape[0]
    tm = _tile(T, 512)

    def body(dh1_ref, o_ref, us_ref, ya_ref, q_ref, l_ref, yb_ref, lng_ref, lnb_ref, w_ref, wt_ref, bt_ref, ga_ref, gs_ref,
             wo_ref, gp_ref, qw_ref, dow_ref, dus_ref, dwo_ref, dw_ref, dbt_ref, dlng_ref, dlnb_ref, dga_ref, dgs_ref,
             dgp_ref, mixed_s, vnb_s, dvn_s):
        @pl.when(pl.program_id(0) == 0)
        def _():
            for r in (dwo_ref, dw_ref, dbt_ref, dlng_ref, dlnb_ref, dga_ref, dgs_ref, dgp_ref):
                r[...] = jnp.zeros_like(r)

        is_a = lax.broadcasted_iota(jnp.int32, (1, 128), 1) < HD
        lane = lax.broadcasted_iota(jnp.int32, (1, 128), 1)
        do, dg = _rms_bwd(dh1_ref[...], o_ref[...], gp_ref[...])
        dgp_ref[...] += dg
        dob = do.astype(BF16)
        dwo_ref[...] += _nn(yb_ref[...].T, dob)
        dy = _nt(dob, wo_ref[...])
        ya = ya_ref[...]
        datt, dg = _rms_bwd(dy[:, 0:AW], ya, ga_ref[...])
        dga_ref[...] += dg
        _attn_operands(q_ref[...], datt, ya, l_ref[...], qw_ref, dow_ref)

        lng = lng_ref[...]
        u, vs, ug, tu, tv, vhat, rstd, mixed, ysg = _sgu_forward(us_ref, lng, lnb_ref[...], w_ref, bt_ref, mixed_s, vnb_s, tm)
        dysg, dg = _rms_bwd(dy[:, AW:D], ysg, gs_ref[...])
        dgs_ref[...] += dg
        dus_ref[:, 0:SW] = ((dysg * mixed) * _gelu_grad(u, tu)).astype(BF16)
        dmix = dysg * ug

        rr = lax.broadcasted_iota(jnp.int32, (CH, CH), 0)
        cc = lax.broadcasted_iota(jnp.int32, (CH, CH), 1)
        tril = cc <= rr
        triu = cc >= rr
        for jj in range(4):
            wta = jnp.where(triu, wt_ref[2 * jj], 0.0).astype(BF16)
            wtb = jnp.where(triu, wt_ref[2 * jj + 1], 0.0).astype(BF16)
            for ch in range(tm // CH):
                rs, cs = slice(ch * CH, (ch + 1) * CH), slice(jj * 128, (jj + 1) * 128)
                dm2 = dmix[rs, cs]
                dma = jnp.where(is_a, dm2, 0.0)
                dmb = jnp.where(is_a, 0.0, dm2)
                dma_b, dmb_b = dma.astype(BF16), dmb.astype(BF16)
                vn2 = vnb_s[rs, cs]
                dw_ref[2 * jj] += jnp.where(tril, _nt(dma_b, vn2), 0.0)
                dw_ref[2 * jj + 1] += jnp.where(tril, _nt(dmb_b, vn2), 0.0)
                dvn_s[rs, cs] = _nn(wta, dma_b) + _nn(wtb, dmb_b)
                dba = jnp.sum(dma, axis=1, keepdims=True)
                dbb = jnp.sum(dmb, axis=1, keepdims=True)
                dbt_ref[...] += jnp.where(lane == 2 * jj, dba, 0.0) + jnp.where(lane == 2 * jj + 1, dbb, 0.0)

        dvn = dvn_s[...]
        dlng_ref[...] += jnp.sum(dvn * vhat, axis=0, keepdims=True)
        dlnb_ref[...] += jnp.sum(dvn, axis=0, keepdims=True)
        dvh = dvn * lng
        dvg = rstd * (dvh - jnp.mean(dvh, axis=-1, keepdims=True) - vhat * jnp.mean(dvh * vhat, axis=-1, keepdims=True))
        dus_ref[:, SW:NUS] = (dvg * _gelu_grad(vs, tv)).astype(BF16)

    row = lambda n: pl.BlockSpec((tm, n), lambda i: (i, 0))
    return pl.pallas_call(
        body,
        name="mix_bwd",
        grid=(T // tm,),
        in_specs=[row(D), row(D), row(NUS), row(AW), row(AW), row(AW), row(D), _full((1, SW)), _full((1, SW)),
                  _full((8, CH, CH)), _full((8, CH, CH)), _full((CH, 8)), _full((1, AW)), _full((1, SW)), _full((D, D)),
                  _full((1, D))],
        out_specs=[row(4 * AW), row(4 * AW), row(NUS),
                   pl.BlockSpec((D, D), lambda i: (0, 0), pipeline_mode=pl.Buffered(1)), _full((8, CH, CH)),
                   _full((CH, 128)), _full((1, SW)), _full((1, SW)), _full((1, AW)), _full((1, SW)), _full((1, D))],
        out_shape=[jax.ShapeDtypeStruct((T, 4 * AW), BF16),
                   jax.ShapeDtypeStruct((T, 4 * AW), BF16), jax.ShapeDtypeStruct((T, NUS), BF16),
                   jax.ShapeDtypeStruct((D, D), F32),
                   jax.ShapeDtypeStruct((8, CH, CH), F32), jax.ShapeDtypeStruct((CH, 128), F32),
                   jax.ShapeDtypeStruct((1, SW), F32), jax.ShapeDtypeStruct((1, SW), F32),
                   jax.ShapeDtypeStruct((1, AW), F32), jax.ShapeDtypeStruct((1, SW), F32), jax.ShapeDtypeStruct((1, D), F32)],
        scratch_shapes=[pltpu.VMEM((tm, SW), F32), pltpu.VMEM((tm, SW), BF16), pltpu.VMEM((tm, SW), F32)],
        compiler_params=_params(1, VMEM_LIMIT_BIG),
    )(dh1, o, us, yatt, qkv, lsec, yb, lng, lnb, sgw, sgwt, sgbt, gatt, gsg, wout, gpm)


def _attn_operands(q, do, o, lsec, qw_ref, dow_ref):
    rows = do.shape[0]
    feat = lax.broadcasted_iota(jnp.int32, (AW, 128), 0)
    head = lax.broadcasted_iota(jnp.int32, (AW, 128), 1)
    sel = jnp.where((feat >= head * HD) & (feat < (head + 1) * HD), 1.0, 0.0)
    delta = jnp.dot(do * o, sel, precision=lax.Precision.HIGHEST, preferred_element_type=F32)
    for hp in range(4):
        cols = slice(hp * 128, (hp + 1) * 128)
        for hh in range(2):
            base = (2 * hp + hh) * 256
            lc = lsec[:, hp * 128 + hh:hp * 128 + hh + 1]
            d_h = delta[:, 2 * hp + hh:2 * hp + hh + 1]
            qw_ref[:, base:base + 128] = _head_rows(q[:, cols], hh, 0.125)
            qw_ref[:, base + 128:base + 256] = _aug_lanes(rows, AUG_A if hh == 0 else AUG_B, _split3(-lc))
            dow_ref[:, base:base + 128] = _head_rows(do[:, cols], hh, 1.0).astype(BF16)
            dow_ref[:, base + 128:base + 256] = _aug_lanes(rows, None, _split3(-d_h))


def _attn_bwd(kk, vk, kt, qw, dow, after):
    T = kk.shape[0]
    tq = _tile(T, 512)
    tk = tq
    nq = T // tq

    def body(kk_ref, vk_ref, kt_ref, qw_ref, dow_ref, after_ref, dqt_ref, dcq_ref, dk_ref, dv_ref, dck_ref, dk_s, dv_s, dck_s):
        j, i = _fold_cols(pl.program_id(1), pl.program_id(2), nq)
        sub8 = lax.broadcasted_iota(jnp.int32, (8, 1), 0)
        lane = lax.broadcasted_iota(jnp.int32, (1, 128), 1)

        @pl.when((pl.program_id(1) == 0) & (pl.program_id(2) == 0))
        def _():
            dqt_ref[...] = jnp.zeros_like(dqt_ref)
            dcq_ref[...] = jnp.zeros_like(dcq_ref)

        @pl.when(i == j)
        def _():
            dk_s[...] = jnp.zeros_like(dk_s)
            dv_s[...] = jnp.zeros_like(dv_s)
            dck_s[...] = jnp.zeros_like(dck_s)

        def step(masked):
            cols = pl.ds(pl.multiple_of(i * tq, tq), tq)
            sub = lax.broadcasted_iota(jnp.int32, (128, 1), 0)
            if masked:
                causal = lax.broadcasted_iota(jnp.int32, (tk, tq), 0) <= lax.broadcasted_iota(jnp.int32, (tk, tq), 1)

            def logits(n):
                pair, base = n // 2, n * 256
                return (_nt(kk_ref[:, pair * 256:(pair + 1) * 256], qw_ref[:, base:base + 256]),
                        _nt(vk_ref[:, pair * 256:(pair + 1) * 256], dow_ref[:, base:base + 256]))

            ahead = logits(0)
            dcq = jnp.zeros((8, tq), F32)
            dck = jnp.zeros((tk, 128), F32)
            for pp in range(PAIRS):
                lanes = slice(pp * 128, (pp + 1) * 128)
                kt2 = kt_ref[lanes, :] * 0.125
                dv = jnp.zeros((tk, 128), F32)
                dk = jnp.zeros((tk, 128), F32)
                dqts = []
                for hh in range(2):
                    head = 2 * pp + hh
                    base = head * 256
                    qw_h = qw_ref[:, base:base + 256]
                    dow_h = dow_ref[:, base:base + 256]
                    logp, dp = ahead
                    if head + 1 < 2 * PAIRS:
                        ahead = logits(head + 1)
                    pr = jnp.exp(logp)
                    if masked:
                        pr = jnp.where(causal, pr, 0.0)
                    ds = pr * dp
                    ds_b = ds.astype(BF16)
                    dv = dv + _nn(pr.astype(BF16), dow_h[:, 0:128])
                    dk = dk + _nn(ds_b, qw_h[:, 0:128])
                    dqts.append(_nn(kt2, ds_b))
                    dcq = dcq + jnp.where(sub8 == head, jnp.sum(ds, axis=0, keepdims=True), 0.0)
                    dck = dck - jnp.where(lane == head, jnp.sum(ds, axis=1, keepdims=True), 0.0)
                dv_s[:, lanes] += dv
                dk_s[:, lanes] += dk
                dqt_ref[lanes, cols] += jnp.where(sub < HD, dqts[0], dqts[1])
            dck_s[...] += dck
            dcq_ref[:, cols] += dcq

        @pl.when(i > j)
        def _():
            step(False)

        @pl.when(i == j)
        def _():
            step(True)

        @pl.when(i == nq - 1)
        def _():
            dk_ref[...] = dk_s[...].astype(BF16)
            dv_ref[...] = dv_s[...].astype(BF16)
            dck_ref[...] = dck_s[...]

    kj = lambda r, t: _fold_cols(r, t, nq)[0]
    qi = lambda r, t: _fold_cols(r, t, nq)[1]
    krow = lambda g, r, t: (kj(r, t), g)
    qrow = lambda g, r, t: (qi(r, t), g)
    return pl.pallas_call(
        body,
        name="attn_bwd",
        grid=_fold_grid(nq),
        in_specs=[
            pl.BlockSpec((tk, PAIRS * 256), krow),
            pl.BlockSpec((tk, PAIRS * 256), krow),
            pl.BlockSpec((PAIRS * 128, tk), lambda g, r, t: (g, kj(r, t))),
            pl.BlockSpec((tq, PAIRS * 512), qrow),
            pl.BlockSpec((tq, PAIRS * 512), qrow),
            pl.BlockSpec(after.shape, lambda g, r, t: (0,) * after.ndim),
        ],
        out_specs=[
            pl.BlockSpec((PAIRS * 128, T), lambda g, r, t: (g, 0), pipeline_mode=pl.Buffered(1)),
            pl.BlockSpec((8, T), lambda g, r, t: (0, 0), pipeline_mode=pl.Buffered(1)),
            pl.BlockSpec((tk, PAIRS * 128), krow),
            pl.BlockSpec((tk, PAIRS * 128), krow),
            pl.BlockSpec((tk, 128), lambda g, r, t: (kj(r, t), 0)),
        ],
        out_shape=[jax.ShapeDtypeStruct((AW, T), F32), jax.ShapeDtypeStruct((8, T), F32),
                   jax.ShapeDtypeStruct((T, AW), BF16), jax.ShapeDtypeStruct((T, AW), BF16),
                   jax.ShapeDtypeStruct((T, FPAD), F32)],
        scratch_shapes=[pltpu.VMEM((tk, PAIRS * 128), F32), pltpu.VMEM((tk, PAIRS * 128), F32),
                        pltpu.VMEM((tk, 128), F32)],
        compiler_params=_params(3),
    )(kk, vk, kt, qw, dow, after)


def _fgate_bwd(dcq, dck, fz, fb):
    T = dck.shape[0]
    tb = _tile(T, 512)
    nb = T // tb

    def body(dcq_ref, dck_ref, fz_ref, fb_ref, df_ref, dfb_ref, carry):
        @pl.when(pl.program_id(0) == 0)
        def _():
            carry[...] = jnp.zeros_like(carry)
            dfb_ref[...] = jnp.zeros_like(dfb_ref)

        head = lax.broadcasted_iota(jnp.int32, (8, FPAD), 0)
        eye = jnp.where(head == lax.broadcasted_iota(jnp.int32, (8, FPAD), 1), 1.0, 0.0)
        dcv = dck_ref[...] + lax.dot_general(dcq_ref[...], eye, (((0,), (0,)), ((), ())),
                                             precision=lax.Precision.HIGHEST, preferred_element_type=F32)
        r = lax.broadcasted_iota(jnp.int32, (128, 128), 0)
        cc = lax.broadcasted_iota(jnp.int32, (128, 128), 1)
        tri = (cc >= r).astype(F32)
        run = carry[...]
        parts = []
        for blk in reversed(range(tb // 128)):
            dc_b = dcv[blk * 128:(blk + 1) * 128, :]
            parts.append(jnp.dot(tri, dc_b, precision=lax.Precision.HIGHEST, preferred_element_type=F32) + run)
            run = run + jnp.sum(dc_b, axis=0, keepdims=True)
        carry[...] = run
        dlf = jnp.concatenate(parts[::-1], axis=0)
        lane = lax.broadcasted_iota(jnp.int32, (tb, FPAD), 1)
        df = jnp.where(lane < 8, dlf * jax.nn.sigmoid(-(fz_ref[...] + fb_ref[...])), 0.0)
        df_ref[...] = df.astype(BF16)
        dfb_ref[...] += jnp.sum(df, axis=0, keepdims=True)

    rev = pl.BlockSpec((tb, FPAD), lambda i: (nb - 1 - i, 0))
    return pl.pallas_call(
        body,
        name="fgate_bwd",
        grid=(nb,),
        in_specs=[pl.BlockSpec((8, tb), lambda i: (0, nb - 1 - i)), rev, rev, _full((1, FPAD))],
        out_specs=[rev, _full((1, FPAD))],
        out_shape=[jax.ShapeDtypeStruct((T, FPAD), BF16), jax.ShapeDtypeStruct((1, FPAD), F32)],
        scratch_shapes=[pltpu.VMEM((1, FPAD), F32)],
        compiler_params=_params(1),
    )(dcq, dck, fz, fb)


def _inproj_bwd(dqt, dk, dv, dus, dfz, wtp, x, dh1, g):
    T = x.shape[0]
    tm = _tile(T, 512)

    def body(dq_ref, dk_ref, dv_ref, dus_ref, dfz_ref, w_ref, x_ref, dh1_ref, g_ref, gx_ref, dg_ref):
        @pl.when(pl.program_id(0) == 0)
        def _():
            dg_ref[...] = jnp.zeros_like(dg_ref)

        da = _tn(dq_ref[...].astype(BF16), w_ref[0:AW, :])
        da += _nn(dk_ref[...], w_ref[AW:2 * AW, :])
        da += _nn(dv_ref[...], w_ref[2 * AW:NQKV, :])
        da += _nn(dus_ref[...], w_ref[NQKV:NQKV + NUS, :])
        da += _nn(dfz_ref[...], w_ref[NQKV + NUS:ZP, :])
        dx, dg = _rms_bwd(da, x_ref[...], g_ref[...])
        gx_ref[...] = dh1_ref[...] + dx
        dg_ref[...] += dg

    row = lambda n: pl.BlockSpec((tm, n), lambda i: (i, 0))
    return pl.pallas_call(
        body,
        name="inproj_bwd",
        grid=(T // tm,),
        in_specs=[pl.BlockSpec((AW, tm), lambda i: (0, i)), row(AW), row(AW), row(NUS), row(FPAD), _full((ZP, D)),
                  row(D), row(D), _full((1, D))],
        out_specs=[row(D), _full((1, D))],
        out_shape=[jax.ShapeDtypeStruct((T, D), F32), jax.ShapeDtypeStruct((1, D), F32)],
        compiler_params=_params(1),
    )(dqt, dk, dv, dus, dfz, wtp, x, dh1, g)


def _sq_relu(f1):
    r = jnp.maximum(f1.astype(F32), 0.0)
    return (r * r).astype(BF16)


def _wgrad(a, bs, name, a_fn=None, out_dtype=F32):
    T, K = a.shape
    tt = _tile(T, 512)
    nb = len(bs)
    narrow = out_dtype != F32

    def out_dims(b, layout):
        if layout == "t":
            return (K, b.shape[0])
        N = b.shape[1]
        if layout is None:
            return (K, N)
        return (N // layout[1], K, layout[1]) if layout[0] == "col" else (K // layout[1], layout[1], N)

    shapes = [out_dims(b, layout) for b, layout in bs]

    def body(*refs):
        a_ref, b_refs, o_refs = refs[0], refs[1:1 + nb], refs[1 + nb:1 + 2 * nb]
        accs = refs[1 + 2 * nb:] if narrow else o_refs

        @pl.when(pl.program_id(0) == 0)
        def _():
            for acc in accs:
                acc[...] = jnp.zeros_like(acc)

        av = a_ref[...] if a_fn is None else a_fn(a_ref[...])
        at = av.astype(BF16).T
        for (b, layout), b_ref, o_ref in zip(bs, b_refs, accs):
            if layout is None:
                o_ref[...] += _nn(at, b_ref[...].astype(BF16))
            elif layout == "t":
                o_ref[...] += _nt(at, b_ref[...].astype(BF16))
            elif layout[0] == "col":
                n = layout[1]
                for k in range(b.shape[1] // n):
                    o_ref[k] += _nn(at, b_ref[:, k * n:(k + 1) * n].astype(BF16))
            else:
                n = layout[1]
                bv = b_ref[...].astype(BF16)
                for k in range(K // n):
                    o_ref[k] += _nn(at[k * n:(k + 1) * n, :], bv)

        if narrow:
            @pl.when(pl.program_id(0) == T // tt - 1)
            def _():
                for o_ref, acc in zip(o_refs, accs):
                    o_ref[...] = acc[...].astype(out_dtype)

    once = lambda shape: pl.BlockSpec(shape, lambda t: (0,) * len(shape), pipeline_mode=pl.Buffered(1))
    res = pl.pallas_call(
        body,
        name=name,
        grid=(T // tt,),
        in_specs=[pl.BlockSpec((tt, K), lambda t: (t, 0))] + [
            pl.BlockSpec((b.shape[0], tt), lambda t: (0, t)) if layout == "t" else pl.BlockSpec((tt, b.shape[1]), lambda t: (t, 0))
            for b, layout in bs],
        out_specs=[once(s) for s in shapes],
        out_shape=[jax.ShapeDtypeStruct(s, out_dtype) for s in shapes],
        scratch_shapes=[pltpu.VMEM(s, F32) for s in shapes] if narrow else [],
        compiler_params=_params(1, VMEM_LIMIT_BIG),
    )(a, *[b for b, _ in bs])
    return res


def _adam_math(w, g, m, v):
    m = ADAM_B1 * m + (1.0 - ADAM_B1) * g
    v = ADAM_B2 * v + (1.0 - ADAM_B2) * (g * g)
    m_hat = m / (1.0 - ADAM_B1 ** ADAM_STEP)
    v_hat = v / (1.0 - ADAM_B2 ** ADAM_STEP)
    delta = -ADAM_LR * (m_hat / (jnp.sqrt(v_hat) + ADAM_EPS) + ADAM_WD * w)
    return delta, m, v


def _adam(parts, w, m, v, name):
    R, C = w.shape
    br = 128 if R % 128 == 0 else R

    def body(p_ref, w_ref, m_ref, v_ref, g_ref, d_ref, nm_ref, nv_ref):
        g = p_ref[0].astype(F32)
        for s in range(1, NDEV):
            g = g + p_ref[s].astype(F32)
        g_ref[...] = g
        d_ref[...], nm_ref[...], nv_ref[...] = _adam_math(w_ref[...], g, m_ref[...], v_ref[...])

    blk = pl.BlockSpec((br, C), lambda i: (i, 0))
    return pl.pallas_call(
        body,
        name=name,
        grid=(R // br,),
        in_specs=[pl.BlockSpec((NDEV, br, C), lambda i: (0, i, 0)), blk, blk, blk],
        out_specs=[blk] * 4,
        out_shape=[jax.ShapeDtypeStruct((R, C), F32)] * 4,
        compiler_params=_params(1),
    )(parts, w, m, v)


_SMALL = (("sg_w", 8 * CH * CH), ("f_bias", 8), ("sg_ln_g", SW), ("sg_ln_b", SW), ("sg_b", 8 * CH), ("att_out_g", AW),
          ("sg_out_g", SW), ("pre_mix_g", D), ("post_mix_g", D), ("pre_ffn_g", D), ("post_ffn_g", D), ("ple_gate_b", D))
_SEG = 8 * 128


def _seg_rows(size):
    return 8 * (-(-size // _SEG))


def _pack(vals, loss_acc):
    parts = []
    for name, size in _SMALL:
        flat = vals[name].reshape(-1)
        rows = _seg_rows(size)
        parts.append(jnp.pad(flat, (0, rows * 128 - size)).reshape(rows, 128))
    parts.append(loss_acc)
    return jnp.concatenate(parts, axis=0)


def _adam_small(parts, ws, ms, vs):
    n = len(_SMALL)
    names = [name for name, _ in _SMALL]

    def body(*refs):
        p_ref, w_refs, m_refs, v_refs = refs[0], refs[1:1 + n], refs[1 + n:1 + 2 * n], refs[1 + 2 * n:1 + 3 * n]
        loss_ref, outs = refs[1 + 3 * n], refs[2 + 3 * n:]
        g_all = p_ref[0]
        for s in range(1, NDEV):
            g_all = g_all + p_ref[s]
        r = 0
        for k, (name, size) in enumerate(_SMALL):
            dst = [outs[kind * n + k] for kind in range(4)]

            def update(g, idx):
                vals = (g,) + _adam_math(w_refs[k][idx], g, m_refs[k][idx], v_refs[k][idx])
                for d, val in zip(dst, vals):
                    d[idx] = val

            if name == "sg_w":
                for grp in range(8):
                    update(g_all[r + grp * CH:r + (grp + 1) * CH, :], (0, grp))
            elif name == "sg_b":
                update(g_all[r:r + 8, :], (0,))
            elif name == "f_bias":
                update(g_all[r:r + 1, 0:8], (slice(None),))
            else:
                update(jnp.concatenate([g_all[r + q:r + q + 1, :] for q in range(size // 128)], axis=1), (slice(None),))
            r += _seg_rows(size)
        loss_ref[...] = g_all[r:r + 1, 0:1] * (0.5 / D)

    arrs = [parts] + [d[name] for d in (ws, ms, vs) for name in names]
    res = pl.pallas_call(
        body,
        name="adam_small",
        in_specs=[_full(a.shape) for a in arrs],
        out_specs=[_full((1, 1))] + [_full(ws[name].shape) for _ in range(4) for name in names],
        out_shape=[jax.ShapeDtypeStruct((1, 1), F32)] + [jax.ShapeDtypeStruct(ws[name].shape, F32) for _ in range(4) for name in names],
        compiler_params=pltpu.CompilerParams(vmem_limit_bytes=VMEM_LIMIT),
    )(*arrs)
    return res[0], {name: [res[1 + kind * n + k] for kind in range(4)] for k, name in enumerate(names)}


def kernel(x, p, w_in, f_bias, sg_ln_g, sg_ln_b, sg_w, sg_b, att_out_g, sg_out_g, w_out, pre_mix_g, post_mix_g, pre_ffn_g, post_ffn_g, w_ff1, w_ff2, ple_w, ple_gate_w, ple_gate_b, loss_target, m_w_in, m_f_bias, m_sg_ln_g, m_sg_ln_b, m_sg_w, m_sg_b, m_att_out_g, m_sg_out_g, m_w_out, m_pre_mix_g, m_post_mix_g, m_pre_ffn_g, m_post_ffn_g, m_w_ff1, m_w_ff2, m_ple_w, m_ple_gate_w, m_ple_gate_b, v_w_in, v_f_bias, v_sg_ln_g, v_sg_ln_b, v_sg_w, v_sg_b, v_att_out_g, v_sg_out_g, v_w_out, v_pre_mix_g, v_post_mix_g, v_pre_ffn_g, v_post_ffn_g, v_w_ff1, v_w_ff2, v_ple_w, v_ple_gate_w, v_ple_gate_b):
    small_w = dict(sg_w=sg_w, f_bias=f_bias, sg_ln_g=sg_ln_g, sg_ln_b=sg_ln_b, sg_b=sg_b, att_out_g=att_out_g,
                   sg_out_g=sg_out_g, pre_mix_g=pre_mix_g, post_mix_g=post_mix_g, pre_ffn_g=pre_ffn_g,
                   post_ffn_g=post_ffn_g, ple_gate_b=ple_gate_b)
    small_m = dict(sg_w=m_sg_w, f_bias=m_f_bias, sg_ln_g=m_sg_ln_g, sg_ln_b=m_sg_ln_b, sg_b=m_sg_b, att_out_g=m_att_out_g,
                   sg_out_g=m_sg_out_g, pre_mix_g=m_pre_mix_g, post_mix_g=m_post_mix_g, pre_ffn_g=m_pre_ffn_g,
                   post_ffn_g=m_post_ffn_g, ple_gate_b=m_ple_gate_b)
    small_v = dict(sg_w=v_sg_w, f_bias=v_f_bias, sg_ln_g=v_sg_ln_g, sg_ln_b=v_sg_ln_b, sg_b=v_sg_b, att_out_g=v_att_out_g,
                   sg_out_g=v_sg_out_g, pre_mix_g=v_pre_mix_g, post_mix_g=v_post_mix_g, pre_ffn_g=v_pre_ffn_g,
                   post_ffn_g=v_post_ffn_g, ple_gate_b=v_ple_gate_b)
    big = dict(w_in=(w_in, m_w_in, v_w_in), w_out=(w_out, m_w_out, v_w_out), w_ff1=(w_ff1, m_w_ff1, v_w_ff1),
               w_ff2=(w_ff2, m_w_ff2, v_w_ff2), ple_w=(ple_w, m_ple_w, v_ple_w),
               ple_gate_w=(ple_gate_w, m_ple_gate_w, v_ple_gate_w))

    xt, pt, tgt = x[0], p[0, 0], loss_target[0]
    ws = W_IN_COLS // NDEV

    gw_in = _gather(jnp.transpose(w_in[0]).astype(BF16), "gather_w_in")
    rest = _xchg_start([w_out[0].astype(BF16), w_ff1[0].astype(BF16), w_ff2[0].astype(BF16), ple_w[0].astype(BF16),
                        ple_gate_w[0].astype(BF16)], True, "gather_rest_start")
    wint = gw_in.reshape(W_IN_COLS, D)
    wtp = jnp.concatenate([wint[0:NQKV], wint[NQKV + 8:W_IN_COLS], wint[NQKV:NQKV + 8],
                           jnp.zeros((FPAD - 8, D), BF16)], axis=0)

    fb = jnp.pad(f_bias.astype(F32), ((0, 0), (0, FPAD - 8)))
    sgw = sg_w[0]
    sgwt = jnp.transpose(sg_w[0], (0, 2, 1))
    sgbt = jnp.transpose(sg_b[0])

    qkv, kt, vt, us, fz, ab = _inproj(xt, pre_mix_g + rest[4][0, 0], wtp)
    ct, kk, vk = _fcum(fz, fb, qkv)
    yatt, lsec = _attn_fwd(qkv, kk, vt, ct)
    gw_out, gw1, gw2, gwpe, gwg = _xchg_wait(rest, True, yatt, "gather_rest_wait")
    wout = gw_out.reshape(D, D)
    wg = gwg.reshape(D, D)
    wpe = jnp.transpose(gwpe, (1, 0, 2)).reshape(PLE, D)
    h1, yb, o = _sgu_out(us, yatt, xt, sg_ln_g, sg_ln_b, sgw, sgbt, att_out_g, sg_out_g, wout, post_mix_g)
    c2b, f1b, ff, h2 = _ffn_fwd(h1, pre_ffn_g, gw1, gw2, post_ffn_g)
    dh2, dbg, loss_acc, g_g, g_pe = _ple_loss(h2, pt, tgt, wg, ple_gate_b, wpe)
    g_g = g_g.reshape(NDEV, D // NDEV, D)
    g_pe = jnp.transpose(g_pe.reshape(PLE, NDEV, D // NDEV), (1, 0, 2))

    dffb, df1, dh1, dgpostffn, dgpreffn = _ffn_bwd(dh2, ff, h1, f1b, gw1, gw2, post_ffn_g, pre_ffn_g)
    (g_1,) = _wgrad(c2b, [(df1, ("col", DFF // NDEV))], "wgrad_ff1")
    (g_2,) = _wgrad(f1b, [(dffb, ("row", DFF // NDEV))], "wgrad_ff2", a_fn=_sq_relu)
    early = _xchg_start([g_1, g_2, g_pe, g_g], False, "scatter_early_start")
    qw, dow, dus, g_out, dsgw, dsgbt, dlng, dlnb, dgatt, dgsg, dgpostmix = _mix_bwd(
        dh1, o, us, yatt, qkv, lsec, yb, sg_ln_g, sg_ln_b, sgw, sgwt, sgbt, att_out_g, sg_out_g, wout,
        post_mix_g + early[4][0, 0])
    mid = _xchg_start([g_out.reshape(NDEV, D // NDEV, D)], False, "scatter_mid_start")
    dqt, dcq, dk, dv, dck = _attn_bwd(kk, vk, kt, qw, dow, mid[4])
    dfz, dfb = _fgate_bwd(dcq, dck, fz, fb)

    gq, gk, gv, gus, gf = _wgrad(ab, [(dqt, "t"), (dk, None), (dv, None), (dus, None), (dfz, None)], "wgrad_in",
                                 out_dtype=BF16)
    g_in = jnp.concatenate([gq, gk, gv, gf[:, 0:8], gus], axis=1)
    g_in = jnp.transpose(g_in.reshape(D, NDEV, ws), (1, 0, 2))
    late = _xchg_start([g_in], False, "scatter_late_start")
    grad_x, dgpremix = _inproj_bwd(dqt, dk, dv, dus, dfz, wtp, xt, dh1, pre_mix_g + late[4][0, 0])

    small_g = dict(sg_w=dsgw, f_bias=dfb[:, 0:8], sg_ln_g=dlng, sg_ln_b=dlnb, sg_b=jnp.transpose(dsgbt[:, 0:8]),
                   att_out_g=dgatt, sg_out_g=dgsg, pre_mix_g=dgpremix, post_mix_g=dgpostmix, pre_ffn_g=dgpreffn,
                   post_ffn_g=dgpostffn, ple_gate_b=dbg)

    r_1, r_2, r_pe, r_g = _xchg_wait(early, False, grad_x, "scatter_early_wait")
    (r_out,) = _xchg_wait(mid, False, grad_x, "scatter_mid_wait")
    r_small = _gather(_pack(small_g, loss_acc), "gather_small_grads")

    res = {}

    def adam_big(name, parts):
        w, m, v = big[name]
        res[name] = [t[None] for t in _adam(parts, w[0], m[0], v[0], "adam_" + name)]

    for name, parts in (("w_out", r_out), ("w_ff1", r_1), ("w_ff2", r_2), ("ple_w", r_pe), ("ple_gate_w", r_g)):
        adam_big(name, parts)
    (r_in,) = _xchg_wait(late, False, res["w_ff1"][0], "scatter_late_wait")
    adam_big("w_in", r_in)
    loss, small = _adam_small(r_small, small_w, small_m, small_v)
    res.update(small)

    order = ["w_in", "f_bias", "sg_ln_g", "sg_ln_b", "sg_w", "sg_b", "att_out_g", "sg_out_g", "w_out", "pre_mix_g",
             "post_mix_g", "pre_ffn_g", "post_ffn_g", "w_ff1", "w_ff2", "ple_w", "ple_gate_w", "ple_gate_b"]
    outs = [loss[0, 0], grad_x[None]]
    for kind in range(4):
        outs += [res[name][kind] for name in order]
    return tuple(outs)
```

```python
import jax
import jax.numpy as jnp
from jax import lax
from jax.experimental import pallas as pl
from jax.experimental.pallas import tpu as pltpu

F32 = jnp.float32
BF16 = jnp.bfloat16

NDEV = 8
D = 1024
AW = 512
SW = 512
HD = 64
CH = 128
DFF = 4096
PLE = 256
NQKV = 3 * AW
NUS = 2 * SW
FPAD = 128
ZP = NQKV + NUS + FPAD
W_IN_COLS = 2568
WS_PAD = 336
EPS = 1e-6
MASKV = -1e30
GELU_K = 0.7978845608028654
GELU_C = 0.044715

ADAM_LR = 0.001
ADAM_B1 = 0.9
ADAM_B2 = 0.999
ADAM_EPS = 1e-08
ADAM_WD = 0.01
ADAM_STEP = 10

VMEM_LIMIT = 48 * 1024 * 1024
VMEM_LIMIT_BIG = 60 * 1024 * 1024


def _nn(a, b):
    return jnp.dot(a, b, preferred_element_type=F32)


def _nt(a, b):
    return lax.dot_general(a, b, (((1,), (1,)), ((), ())), preferred_element_type=F32)


def _tn(a, b):
    return lax.dot_general(a, b, (((0,), (0,)), ((), ())), preferred_element_type=F32)


def _tile(n, pref):
    return min(n, pref)


def _params(n_axes, vmem=VMEM_LIMIT):
    return pltpu.CompilerParams(dimension_semantics=("arbitrary",) * n_axes, vmem_limit_bytes=vmem)


def _full(shape):
    nd = len(shape)
    return pl.BlockSpec(shape, lambda *_: (0,) * nd)


def _rms_fwd(x, g):
    r = lax.rsqrt(jnp.mean(x * x, axis=-1, keepdims=True) + EPS)
    return x * r * g


def _rms_bwd(dy, x, g):
    n = x.shape[-1]
    r = lax.rsqrt(jnp.mean(x * x, axis=-1, keepdims=True) + EPS)
    u = dy * g
    s = jnp.sum(x * u, axis=-1, keepdims=True)
    dx = r * u - x * (r * r * r * (s * (1.0 / n)))
    dg = jnp.sum(dy * (x * r), axis=0, keepdims=True)
    return dx, dg


def _gelu(x):
    t = jnp.tanh(x * (GELU_K + (GELU_K * GELU_C) * (x * x)))
    return x * (0.5 + 0.5 * t), t


def _gelu_grad(x, t):
    return (0.5 + 0.5 * t) + (0.5 * x) * (1.0 - t * t) * (GELU_K + (3.0 * GELU_K * GELU_C) * (x * x))


def _gather(arr, name):
    def body(x_ref, out_ref, send, recv, loc):
        x, y, c = lax.axis_index("x"), lax.axis_index("y"), lax.axis_index("c")
        me, sibling = (x, y, c), (x, y, 1 - c)
        chips = [(1 - x, y), (x, 1 - y), (1 - x, 1 - y)]

        def slot(px, py, pc):
            return out_ref.at[4 * px + 2 * py + pc]

        def copy(k, block, to, src=None):
            return pltpu.make_async_remote_copy(
                src_ref=slot(*block) if src is None else src, dst_ref=slot(*block), send_sem=send.at[k],
                recv_sem=recv.at[k], device_id=to, device_id_type=pl.DeviceIdType.MESH)

        mine = pltpu.make_async_copy(x_ref, slot(*me), loc)
        mine.start()
        first = [copy(0, me, sibling, src=x_ref)] + [copy(1 + j, me, (*chip, c), src=x_ref) for j, chip in enumerate(chips)]
        for cp in first:
            cp.start()
        passed = [copy(4 + j, (*chip, c), sibling) for j, chip in enumerate(chips)]
        for j, chip in enumerate(chips):
            copy(1 + j, (*chip, c), me).wait_recv()
            passed[j].start()
        copy(0, sibling, me).wait_recv()
        for j, chip in enumerate(chips):
            copy(4 + j, (*chip, 1 - c), me).wait_recv()
        for cp in first + passed:
            cp.wait_send()
        mine.wait()

    hbm = pl.BlockSpec(memory_space=pltpu.HBM)
    return pl.pallas_call(
        body,
        name=name,
        out_shape=jax.ShapeDtypeStruct((NDEV,) + arr.shape, arr.dtype),
        in_specs=[hbm],
        out_specs=hbm,
        scratch_shapes=[pltpu.SemaphoreType.DMA((NDEV - 1,)), pltpu.SemaphoreType.DMA((NDEV - 1,)), pltpu.SemaphoreType.DMA],
    )(arr)


def _peers(x, y, c):
    out = []
    for k in range(1, NDEV):
        out.append((1 - x if (k >> 2) & 1 else x, 1 - y if (k >> 1) & 1 else y, 1 - c if k & 1 else c))
    return out


def _xchg_start(arrs, gather, name):
    n = len(arrs)
    me = 4 * lax.axis_index("x") + 2 * lax.axis_index("y") + lax.axis_index("c")
    lands = []
    for a in arrs:
        shape = ((NDEV,) + a.shape) if gather else a.shape
        own = a[None] if gather else lax.dynamic_slice_in_dim(a, me, 1, axis=0)
        lands.append(lax.dynamic_update_slice_in_dim(lax.empty(shape, a.dtype), own, me, axis=0))

    def body(*refs):
        ins, lnd = refs[:n], refs[n:2 * n]
        send, recv, token = refs[2 * n:3 * n], refs[3 * n:4 * n], refs[-1]
        x, y, c = lax.axis_index("x"), lax.axis_index("y"), lax.axis_index("c")
        mine = 4 * x + 2 * y + c
        for px, py, pc in _peers(x, y, c):
            peer = 4 * px + 2 * py + pc
            for a in range(n):
                pltpu.make_async_remote_copy(
                    src_ref=ins[a] if gather else ins[a].at[peer],
                    dst_ref=lnd[a].at[mine],
                    send_sem=send[a],
                    recv_sem=recv[a],
                    device_id=(px, py, pc),
                    device_id_type=pl.DeviceIdType.MESH,
                ).start()
        token[...] = jnp.zeros_like(token)

    hbm = pl.BlockSpec(memory_space=pltpu.HBM)
    sem = pl.BlockSpec(memory_space=pltpu.SEMAPHORE)
    res = pl.pallas_call(
        body,
        name=name,
        out_shape=(*[pltpu.SemaphoreType.DMA(())] * (2 * n),
                   *[pltpu.HBM(a.shape, a.dtype) for a in arrs], *[pltpu.HBM(l.shape, l.dtype) for l in lands],
                   jax.ShapeDtypeStruct((8, 128), F32)),
        in_specs=[hbm] * (2 * n),
        out_specs=(*([sem] * (2 * n)), *([hbm] * (2 * n)), pl.BlockSpec(memory_space=pltpu.VMEM)),
        input_output_aliases={i: 2 * n + i for i in range(2 * n)},
        compiler_params=pltpu.CompilerParams(has_side_effects=pltpu.SideEffectType.DATAFLOW_SIDE_EFFECTING),
    )(*[pltpu.with_memory_space_constraint(a, pltpu.HBM) for a in arrs],
      *[pltpu.with_memory_space_constraint(l, pltpu.HBM) for l in lands])
    return list(res[0:n]), list(res[n:2 * n]), list(res[2 * n:3 * n]), list(res[3 * n:4 * n]), res[-1]


def _xchg_wait(started, gather, after, name):
    send, recv, srcs, lands, _ = started
    n = len(srcs)

    def body(*refs):
        lnd = refs[n:2 * n]
        send, recv = refs[2 * n:3 * n], refs[3 * n:4 * n]
        me = (lax.axis_index("x"), lax.axis_index("y"), lax.axis_index("c"))
        for a in range(n):
            seven = lnd[a].at[pl.ds(0, NDEV - 1)]
            cp = pltpu.make_async_remote_copy(src_ref=seven, dst_ref=seven, send_sem=send[a], recv_sem=recv[a],
                                              device_id=me, device_id_type=pl.DeviceIdType.MESH)
            cp.wait_send()
            cp.wait_recv()

    hbm = pl.BlockSpec(memory_space=pltpu.HBM)
    sem = pl.BlockSpec(memory_space=pltpu.SEMAPHORE)
    res = pl.pallas_call(
        body,
        name=name,
        out_shape=tuple([pltpu.HBM(a.shape, a.dtype) for a in srcs] + [pltpu.HBM(l.shape, l.dtype) for l in lands]),
        in_specs=[hbm] * (2 * n) + [sem] * (2 * n) + [pl.BlockSpec(memory_space=pl.ANY)],
        out_specs=tuple([hbm] * (2 * n)),
        input_output_aliases={i: i for i in range(2 * n)},
        compiler_params=pltpu.CompilerParams(has_side_effects=pltpu.SideEffectType.DATAFLOW_SIDE_EFFECTING),
    )(*srcs, *lands, *send, *recv, after)
    return list(res[n:])


def _inproj(x, g, wtp):
    T = x.shape[0]
    tm = _tile(T, 512)

    def body(x_ref, g_ref, w_ref, qkv_ref, kt_ref, vt_ref, us_ref, fz_ref, ab_ref):
        a = _rms_fwd(x_ref[...], g_ref[...]).astype(BF16)
        ab_ref[...] = a
        qkv_ref[:, 0:AW] = _nt(a, w_ref[0:AW, :]).astype(BF16)
        kk = _nt(a, w_ref[AW:2 * AW, :])
        qkv_ref[:, AW:2 * AW] = kk.astype(BF16)
        kt_ref[...] = kk.T.astype(BF16)
        vv = _nt(a, w_ref[2 * AW:NQKV, :])
        qkv_ref[:, 2 * AW:NQKV] = vv.astype(BF16)
        vt_ref[...] = vv.T.astype(BF16)
        us_ref[...] = _nt(a, w_ref[NQKV:NQKV + NUS, :])
        fz_ref[...] = _nt(a, w_ref[NQKV + NUS:ZP, :])

    row = lambda n: pl.BlockSpec((tm, n), lambda i: (i, 0))
    col = pl.BlockSpec((AW, tm), lambda i: (0, i))
    return pl.pallas_call(
        body,
        name="inproj",
        grid=(T // tm,),
        in_specs=[row(D), _full((1, D)), _full((ZP, D))],
        out_specs=[row(NQKV), col, col, row(NUS), row(FPAD), row(D)],
        out_shape=[
            jax.ShapeDtypeStruct((T, NQKV), BF16),
            jax.ShapeDtypeStruct((AW, T), BF16),
            jax.ShapeDtypeStruct((AW, T), BF16),
            jax.ShapeDtypeStruct((T, NUS), F32),
            jax.ShapeDtypeStruct((T, FPAD), F32),
            jax.ShapeDtypeStruct((T, D), BF16),
        ],
        compiler_params=_params(1),
    )(x, g, wtp)


def _log_sigmoid(z):
    return jnp.minimum(z, 0.0) - jnp.log1p(jnp.exp(-jnp.abs(z)))


def _split3(x):
    hi = x.astype(BF16)
    r1 = x - hi.astype(F32)
    mid = r1.astype(BF16)
    lo = (r1 - mid.astype(F32)).astype(BF16)
    return hi, mid, lo


AUG_A, AUG_B, AUG_ONE = 0, 3, 6


def _aug_lanes(rows, first_one, pieces):
    lane = lax.broadcasted_iota(jnp.int32, (rows, 128), 1)
    out = jnp.zeros((rows, 128), F32)
    if first_one is not None:
        out = jnp.where((lane >= first_one) & (lane < first_one + 3), 1.0, out)
    for n, piece in enumerate(pieces):
        out = jnp.where(lane == AUG_ONE + n, piece.astype(F32), out)
    return out.astype(BF16)


def _fcum(fz, fb, qkv):
    T = fz.shape[0]
    tb = _tile(T, 512)

    def body(fz_ref, fb_ref, k_ref, v_ref, ct_ref, kk_ref, vk_ref, carry, tri_s, sel_s):
        @pl.when(pl.program_id(0) == 0)
        def _():
            carry[...] = jnp.zeros_like(carry)
            src = lax.broadcasted_iota(jnp.int32, (128, 128), 0)
            dst = lax.broadcasted_iota(jnp.int32, (128, 128), 1)
            tri_s[...] = (dst <= src).astype(F32)
            for hp in range(4):
                for n in range(3):
                    pick = ((src == 2 * hp) & (dst == AUG_A + n)) | ((src == 2 * hp + 1) & (dst == AUG_B + n))
                    sel_s[3 * hp + n] = jnp.where(pick, -1.0, 0.0).astype(BF16)

        lf = _log_sigmoid(fz_ref[...] + fb_ref[...])
        run = carry[...]
        parts = []
        for blk in range(tb // 128):
            lf_b = lf[blk * 128:(blk + 1) * 128, :]
            parts.append(jnp.dot(tri_s[...], lf_b, precision=lax.Precision.HIGHEST, preferred_element_type=F32) + run)
            run = run + jnp.sum(lf_b, axis=0, keepdims=True)
        carry[...] = run
        cs = jnp.concatenate(parts, axis=0)
        ct_ref[...] = cs.T[0:8, :]

        pieces = _split3(cs)
        lane = lax.broadcasted_iota(jnp.int32, (1, 128), 1)
        ones = jnp.where((lane >= AUG_ONE) & (lane < AUG_ONE + 3), 1.0, 0.0)
        for hp in range(4):
            aug = jnp.zeros((tb, 128), F32) + ones
            for n, piece in enumerate(pieces):
                aug = aug + _nn(piece, sel_s[3 * hp + n])
            aug = aug.astype(BF16)
            kk_ref[:, hp * 256:hp * 256 + 128] = k_ref[:, hp * 128:(hp + 1) * 128]
            kk_ref[:, hp * 256 + 128:(hp + 1) * 256] = aug
            vk_ref[:, hp * 256:hp * 256 + 128] = v_ref[:, hp * 128:(hp + 1) * 128]
            vk_ref[:, hp * 256 + 128:(hp + 1) * 256] = aug

    row = lambda n, c: pl.BlockSpec((tb, n), lambda i: (i, c))
    return pl.pallas_call(
        body,
        name="fcum",
        grid=(T // tb,),
        in_specs=[row(FPAD, 0), _full((1, FPAD)), row(AW, 1), row(AW, 2)],
        out_specs=[pl.BlockSpec((8, tb), lambda i: (0, i)), row(2 * AW, 0), row(2 * AW, 0)],
        out_shape=[jax.ShapeDtypeStruct((8, T), F32), jax.ShapeDtypeStruct((T, 2 * AW), BF16),
                   jax.ShapeDtypeStruct((T, 2 * AW), BF16)],
        scratch_shapes=[pltpu.VMEM((1, FPAD), F32), pltpu.VMEM((128, 128), F32), pltpu.VMEM((12, 128, 128), BF16)],
        compiler_params=_params(1),
    )(fz, fb, qkv, qkv)


def _fold_rows(r, t, nq):
    if nq == 1:
        return r, t
    low = t <= r
    return jnp.where(low, r, nq - 1 - r), jnp.where(low, t, t - r - 1)


def _fold_cols(r, t, nq):
    if nq == 1:
        return r, t
    first = t < nq - r
    j = jnp.where(first, r, nq - 1 - r)
    return j, jnp.where(first, r + t, nq - 1 - r + (t - (nq - r)))


PAIRS = 4


def _fold_grid(nq):
    assert (nq == 1 or nq % 2 == 0) and PAIRS == 4
    return (4 // PAIRS, 1, 1) if nq == 1 else (4 // PAIRS, nq // 2, nq + 1)


def _head_rows(x2, hh, scale):
    is_a = lax.broadcasted_iota(jnp.int32, (1, 128), 1) < HD
    keep = is_a if hh == 0 else jnp.logical_not(is_a)
    return jnp.where(keep, x2, jnp.zeros_like(x2)) * scale


def _attn_fwd(qkv, kk, vt, ct):
    T = qkv.shape[0]
    tq = _tile(T, 512)
    tk = tq
    nq = T // tq

    def body(q_ref, kk_ref, vt_ref, ctq_ref, o_ref, lsec_ref, qw_s, m_s, l_s, acc_s):
        i, j = _fold_rows(pl.program_id(1), pl.program_id(2), nq)
        sub8 = lax.broadcasted_iota(jnp.int32, (8, 1), 0)

        def cref_of(pp, hh):
            head = 2 * (PAIRS * pl.program_id(0) + pp) + hh
            return jnp.sum(jnp.where(sub8 == head, ctq_ref[:, 0:1], 0.0), axis=0, keepdims=True)

        @pl.when(j == 0)
        def _():
            for pp in range(PAIRS):
                q2 = q_ref[:, pp * 128:(pp + 1) * 128]
                for hh in range(2):
                    rows = slice(hh * tq, (hh + 1) * tq)
                    qw_s[pp, rows, 0:128] = _head_rows(q2, hh, 0.125)
                    qw_s[pp, rows, 128:256] = _aug_lanes(tq, AUG_A if hh == 0 else AUG_B, _split3(cref_of(pp, hh)))
            m_s[...] = jnp.full_like(m_s, MASKV)
            l_s[...] = jnp.zeros_like(l_s)
            acc_s[...] = jnp.zeros_like(acc_s)

        def step(masked):
            if masked:
                causal = lax.broadcasted_iota(jnp.int32, (tk, tq), 0) <= lax.broadcasted_iota(jnp.int32, (tk, tq), 1)
            logits = lambda pp: _nt(kk_ref[:, pp * 256:(pp + 1) * 256], qw_s[pp])
            s_next = logits(0)
            for pp in range(PAIRS):
                s2 = s_next
                if pp + 1 < PAIRS:
                    s_next = logits(pp + 1)
                vt2 = vt_ref[pp * 128:(pp + 1) * 128, :]
                for hh in range(2):
                    n = 2 * pp + hh
                    s = s2[:, hh * tq:(hh + 1) * tq]
                    if masked:
                        s = jnp.where(causal, s, MASKV)
                    m_prev = m_s[n]
                    m_new = jnp.maximum(m_prev, jnp.max(s, axis=0, keepdims=True))
                    pr = jnp.exp(s - m_new)
                    alpha = jnp.exp(m_prev - m_new)
                    l_s[n] = alpha * l_s[n] + jnp.sum(pr, axis=0, keepdims=True)
                    m_s[n] = m_new
                    acc_s[n] = alpha * acc_s[n] + _nn(vt2, pr.astype(BF16))

        @pl.when(j < i)
        def _():
            step(False)

        @pl.when(j == i)
        def _():
            step(True)
            sub = lax.broadcasted_iota(jnp.int32, (128, 1), 0)
            for pp in range(PAIRS):
                a, b = 2 * pp, 2 * pp + 1
                ot = jnp.where(sub < HD, acc_s[a] * (1.0 / l_s[a]), acc_s[b] * (1.0 / l_s[b]))
                o_ref[:, pp * 128:(pp + 1) * 128] = ot.T
                lrow = [m_s[2 * pp + hh] + jnp.log(l_s[2 * pp + hh]) - cref_of(pp, hh) for hh in range(2)]
                lsec_ref[:, pp * 128:(pp + 1) * 128] = jnp.where(sub == 0, lrow[0], jnp.where(sub == 1, lrow[1], 0.0)).T

    qi = lambda r, t: _fold_rows(r, t, nq)[0]
    kj = lambda r, t: _fold_rows(r, t, nq)[1]
    return pl.pallas_call(
        body,
        name="attn_fwd",
        grid=_fold_grid(nq),
        in_specs=[
            pl.BlockSpec((tq, PAIRS * 128), lambda g, r, t: (qi(r, t), g)),
            pl.BlockSpec((tk, PAIRS * 256), lambda g, r, t: (kj(r, t), g)),
            pl.BlockSpec((PAIRS * 128, tk), lambda g, r, t: (g, kj(r, t))),
            pl.BlockSpec((8, tq), lambda g, r, t: (0, qi(r, t))),
        ],
        out_specs=[pl.BlockSpec((tq, PAIRS * 128), lambda g, r, t: (qi(r, t), g))] * 2,
        out_shape=[jax.ShapeDtypeStruct((T, AW), F32)] * 2,
        scratch_shapes=[pltpu.VMEM((PAIRS, 2 * tq, 256), BF16), pltpu.VMEM((2 * PAIRS, 1, tq), F32),
                        pltpu.VMEM((2 * PAIRS, 1, tq), F32), pltpu.VMEM((2 * PAIRS, 128, tq), F32)],
        compiler_params=_params(3),
    )(qkv, kk, vt, ct)


def _sgu_forward(us_ref, lng, lnb, w_ref, bt_ref, mixed_s, vnb_s, tm):
    is_a = lax.broadcasted_iota(jnp.int32, (1, 128), 1) < HD
    u = us_ref[:, 0:SW]
    vs = us_ref[:, SW:NUS]
    ug, tu = _gelu(u)
    vg, tv = _gelu(vs)
    mu = jnp.mean(vg, axis=-1, keepdims=True)
    xc = vg - mu
    rstd = lax.rsqrt(jnp.mean(xc * xc, axis=-1, keepdims=True) + EPS)
    vhat = xc * rstd
    vnb_s[...] = (vhat * lng + lnb).astype(BF16)
    rr = lax.broadcasted_iota(jnp.int32, (CH, CH), 0)
    cc = lax.broadcasted_iota(jnp.int32, (CH, CH), 1)
    tril = cc <= rr
    for jj in range(4):
        wa = jnp.where(tril, w_ref[2 * jj], 0.0).astype(BF16)
        wb = jnp.where(tril, w_ref[2 * jj + 1], 0.0).astype(BF16)
        ba = bt_ref[:, 2 * jj:2 * jj + 1]
        bb = bt_ref[:, 2 * jj + 1:2 * jj + 2]
        for ch in range(tm // CH):
            rs, cs = slice(ch * CH, (ch + 1) * CH), slice(jj * 128, (jj + 1) * 128)
            vn2 = vnb_s[rs, cs]
            mixed_s[rs, cs] = jnp.where(is_a, _nn(wa, vn2) + ba, _nn(wb, vn2) + bb)
    mixed = mixed_s[...]
    return u, vs, ug, tu, tv, vhat, rstd, mixed, ug * mixed


def _sgu_out(us, yatt, x, lng, lnb, sgw, sgbt, gatt, gsg, wout, gpm):
    T = us.shape[0]
    tm = _tile(T, 512)

    def body(us_ref, ya_ref, x_ref, lng_ref, lnb_ref, w_ref, bt_ref, ga_ref, gs_ref, wo_ref, gp_ref,
             h1_ref, yb_ref, o_ref, mixed_s, vnb_s):
        ysg = _sgu_forward(us_ref, lng_ref[...], lnb_ref[...], w_ref, bt_ref, mixed_s, vnb_s, tm)[-1]
        yb_ref[:, 0:AW] = _rms_fwd(ya_ref[...], ga_ref[...]).astype(BF16)
        yb_ref[:, AW:D] = _rms_fwd(ysg, gs_ref[...]).astype(BF16)
        o = _nn(yb_ref[...], wo_ref[...])
        o_ref[...] = o
        h1_ref[...] = x_ref[...] + _rms_fwd(o, gp_ref[...])

    row = lambda n: pl.BlockSpec((tm, n), lambda i: (i, 0))
    return pl.pallas_call(
        body,
        name="sgu_out",
        grid=(T // tm,),
        in_specs=[row(NUS), row(AW), row(D), _full((1, SW)), _full((1, SW)), _full((8, CH, CH)), _full((CH, 8)),
                  _full((1, AW)), _full((1, SW)), _full((D, D)), _full((1, D))],
        out_specs=[row(D), row(D), row(D)],
        out_shape=[jax.ShapeDtypeStruct((T, D), F32), jax.ShapeDtypeStruct((T, D), BF16), jax.ShapeDtypeStruct((T, D), F32)],
        scratch_shapes=[pltpu.VMEM((tm, SW), F32), pltpu.VMEM((tm, SW), BF16)],
        compiler_params=_params(1),
    )(us, yatt, x, lng, lnb, sgw, sgbt, gatt, gsg, wout, gpm)


def _ffn_fwd(h1, gpre, w1g, w2g, gpost):
    T = h1.shape[0]
    tm = _tile(T, 512)
    nb, hb = w1g.shape[0], w1g.shape[2]

    def body(h1_ref, gpre_ref, w1_ref, w2_ref, gpost_ref, c2_ref, f1_ref, ff_ref, h2_ref):
        h1 = h1_ref[...]
        c2 = _rms_fwd(h1, gpre_ref[...]).astype(BF16)
        c2_ref[...] = c2
        ff = jnp.zeros((tm, D), F32)
        for j in range(nb):
            f1 = _nn(c2, w1_ref[j])
            f1_ref[:, j * hb:(j + 1) * hb] = f1.astype(BF16)
            r = jnp.maximum(f1, 0.0)
            ff = ff + _nn((r * r).astype(BF16), w2_ref[j])
        ff_ref[...] = ff
        h2_ref[...] = h1 + _rms_fwd(ff, gpost_ref[...])

    row = lambda n: pl.BlockSpec((tm, n), lambda i: (i, 0))
    once = lambda shape: pl.BlockSpec(shape, lambda i: (0,) * len(shape), pipeline_mode=pl.Buffered(1))
    return pl.pallas_call(
        body,
        name="ffn_fwd",
        grid=(T // tm,),
        in_specs=[row(D), _full((1, D)), once((nb, D, hb)), once((nb, hb, D)), _full((1, D))],
        out_specs=[row(D), row(DFF), row(D), row(D)],
        out_shape=[jax.ShapeDtypeStruct((T, D), BF16), jax.ShapeDtypeStruct((T, DFF), BF16),
                   jax.ShapeDtypeStruct((T, D), F32), jax.ShapeDtypeStruct((T, D), F32)],
        compiler_params=_params(1, VMEM_LIMIT_BIG),
    )(h1, gpre, w1g, w2g, gpost)


def _ple_loss(h2, p, tgt, wg, bg, wpe):
    T = h2.shape[0]
    tm = _tile(T, 512)

    def body(h2_ref, p_ref, t_ref, wg_ref, bg_ref, wpe_ref, dh2_ref, dbg_ref, loss_ref, dwg_ref, dwpe_ref):
        @pl.when(pl.program_id(0) == 0)
        def _():
            for r in (dbg_ref, loss_ref, dwg_ref, dwpe_ref):
                r[...] = jnp.zeros_like(r)

        h2 = h2_ref[...]
        h2b = h2.astype(BF16)
        gate = jax.nn.sigmoid(_nn(h2b, wg_ref[...]) + bg_ref[...])
        pb = p_ref[...].astype(BF16)
        pe = _nn(pb, wpe_ref[...])
        diff = (h2 + gate * pe) - t_ref[...]
        loss_ref[...] += jnp.sum(diff * diff)
        dh3 = diff * (1.0 / D)
        dpre = (dh3 * pe) * (gate * (1.0 - gate))
        dpre_b = dpre.astype(BF16)
        dbg_ref[...] += jnp.sum(dpre, axis=0, keepdims=True)
        dh2_ref[...] = dh3 + _nt(dpre_b, wg_ref[...])
        dwg_ref[...] += _nn(h2b.T, dpre_b)
        dwpe_ref[...] += _nn(pb.T, (dh3 * gate).astype(BF16))

    row = lambda n: pl.BlockSpec((tm, n), lambda i: (i, 0))
    once = lambda shape: pl.BlockSpec(shape, lambda i: (0,) * len(shape), pipeline_mode=pl.Buffered(1))
    return pl.pallas_call(
        body,
        name="ple_loss",
        grid=(T // tm,),
        in_specs=[row(D), row(PLE), row(D), _full((D, D)), _full((1, D)), _full((PLE, D))],
        out_specs=[row(D), _full((1, D)), _full((8, 128)), once((D, D)), once((PLE, D))],
        out_shape=[jax.ShapeDtypeStruct((T, D), F32), jax.ShapeDtypeStruct((1, D), F32),
                   jax.ShapeDtypeStruct((8, 128), F32), jax.ShapeDtypeStruct((D, D), F32),
                   jax.ShapeDtypeStruct((PLE, D), F32)],
        compiler_params=_params(1),
    )(h2, p, tgt, wg, bg, wpe)


def _ffn_bwd(dh2, ff, h1, f1, w1g, w2g, gpost, gpre):
    T = dh2.shape[0]
    tm = _tile(T, 512)
    nb, hb = w1g.shape[0], w1g.shape[2]

    def body(dh2_ref, ff_ref, h1_ref, f1_ref, w1_ref, w2_ref, gpost_ref, gpre_ref,
             dffb_ref, df1_ref, dh1_ref, dgpost_ref, dgpre_ref):
        @pl.when(pl.program_id(0) == 0)
        def _():
            dgpost_ref[...] = jnp.zeros_like(dgpost_ref)
            dgpre_ref[...] = jnp.zeros_like(dgpre_ref)

        dh2 = dh2_ref[...]
        dff, dg = _rms_bwd(dh2, ff_ref[...], gpost_ref[...])
        dffb = dff.astype(BF16)
        dffb_ref[...] = dffb
        dgpost_ref[...] += dg
        dc2 = jnp.zeros((tm, D), F32)
        for j in range(nb):
            cols = slice(j * hb, (j + 1) * hb)
            dact = _nt(dffb, w2_ref[j])
            df1 = (dact * (2.0 * jnp.maximum(f1_ref[:, cols].astype(F32), 0.0))).astype(BF16)
            df1_ref[:, cols] = df1
            dc2 = dc2 + _nt(df1, w1_ref[j])
        dx, dg = _rms_bwd(dc2, h1_ref[...], gpre_ref[...])
        dh1_ref[...] = dh2 + dx
        dgpre_ref[...] += dg

    row = lambda n: pl.BlockSpec((tm, n), lambda i: (i, 0))
    once = lambda shape: pl.BlockSpec(shape, lambda i: (0,) * len(shape), pipeline_mode=pl.Buffered(1))
    return pl.pallas_call(
        body,
        name="ffn_bwd",
        grid=(T // tm,),
        in_specs=[row(D), row(D), row(D), row(DFF), once((nb, D, hb)), once((nb, hb, D)), _full((1, D)), _full((1, D))],
        out_specs=[row(D), row(DFF), row(D), _full((1, D)), _full((1, D))],
        out_shape=[jax.ShapeDtypeStruct((T, D), BF16), jax.ShapeDtypeStruct((T, DFF), BF16),
                   jax.ShapeDtypeStruct((T, D), F32), jax.ShapeDtypeStruct((1, D), F32),
                   jax.ShapeDtypeStruct((1, D), F32)],
        compiler_params=_params(1, VMEM_LIMIT_BIG),
    )(dh2, ff, h1, f1, w1g, w2g, gpost, gpre)


def _mix_bwd(dh1, o, us, yatt, qkv, lsec, yb, lng, lnb, sgw, sgwt, sgbt, gatt, gsg, wout, gpm):
    T = dh1.shape[0]
    tm = _tile(T, 512)

    def body(dh1_ref, o_ref, us_ref, ya_ref, q_ref, l_ref, yb_ref, lng_ref, lnb_ref, w_ref, wt_ref, bt_ref, ga_ref, gs_ref,
             wo_ref, gp_ref, qw_ref, dow_ref, dus_ref, dwo_ref, dw_ref, dbt_ref, dlng_ref, dlnb_ref, dga_ref, dgs_ref,
             dgp_ref, mixed_s, vnb_s, dvn_s):
        @pl.when(pl.program_id(0) == 0)
        def _():
            for r in (dwo_ref, dw_ref, dbt_ref, dlng_ref, dlnb_ref, dga_ref, dgs_ref, dgp_ref):
                r[...] = jnp.zeros_like(r)

        is_a = lax.broadcasted_iota(jnp.int32, (1, 128), 1) < HD
        lane = lax.broadcasted_iota(jnp.int32, (1, 128), 1)
        do, dg = _rms_bwd(dh1_ref[...], o_ref[...], gp_ref[...])
        dgp_ref[...] += dg
        dob = do.astype(BF16)
        dwo_ref[...] += _nn(yb_ref[...].T, dob)
        dy = _nt(dob, wo_ref[...])
        ya = ya_ref[...]
        datt, dg = _rms_bwd(dy[:, 0:AW], ya, ga_ref[...])
        dga_ref[...] += dg
        _attn_operands(q_ref[...], datt, ya, l_ref[...], qw_ref, dow_ref)

        lng = lng_ref[...]
        u, vs, ug, tu, tv, vhat, rstd, mixed, ysg = _sgu_forward(us_ref, lng, lnb_ref[...], w_ref, bt_ref, mixed_s, vnb_s, tm)
        dysg, dg = _rms_bwd(dy[:, AW:D], ysg, gs_ref[...])
        dgs_ref[...] += dg
        dus_ref[:, 0:SW] = ((dysg * mixed) * _gelu_grad(u, tu)).astype(BF16)
        dmix = dysg * ug

        rr = lax.broadcasted_iota(jnp.int32, (CH, CH), 0)
        cc = lax.broadcasted_iota(jnp.int32, (CH, CH), 1)
        tril = cc <= rr
        triu = cc >= rr
        for jj in range(4):
            wta = jnp.where(triu, wt_ref[2 * jj], 0.0).astype(BF16)
            wtb = jnp.where(triu, wt_ref[2 * jj + 1], 0.0).astype(BF16)
            for ch in range(tm // CH):
                rs, cs = slice(ch * CH, (ch + 1) * CH), slice(jj * 128, (jj + 1) * 128)
                dm2 = dmix[rs, cs]
                dma = jnp.where(is_a, dm2, 0.0)
                dmb = jnp.where(is_a, 0.0, dm2)
                dma_b, dmb_b = dma.astype(BF16), dmb.astype(BF16)
                vn2 = vnb_s[rs, cs]
                dw_ref[2 * jj] += jnp.where(tril, _nt(dma_b, vn2), 0.0)
                dw_ref[2 * jj + 1] += jnp.where(tril, _nt(dmb_b, vn2), 0.0)
                dvn_s[rs, cs] = _nn(wta, dma_b) + _nn(wtb, dmb_b)
                dba = jnp.sum(dma, axis=1, keepdims=True)
                dbb = jnp.sum(dmb, axis=1, keepdims=True)
                dbt_ref[...] += jnp.where(lane == 2 * jj, dba, 0.0) + jnp.where(lane == 2 * jj + 1, dbb, 0.0)

        dvn = dvn_s[...]
        dlng_ref[...] += jnp.sum(dvn * vhat, axis=0, keepdims=True)
        dlnb_ref[...] += jnp.sum(dvn, axis=0, keepdims=True)
        dvh = dvn * lng
        dvg = rstd * (dvh - jnp.mean(dvh, axis=-1, keepdims=True) - vhat * jnp.mean(dvh * vhat, axis=-1, keepdims=True))
        dus_ref[:, SW:NUS] = (dvg * _gelu_grad(vs, tv)).astype(BF16)

    row = lambda n: pl.BlockSpec((tm, n), lambda i: (i, 0))
    return pl.pallas_call(
        body,
        name="mix_bwd",
        grid=(T // tm,),
        in_specs=[row(D), row(D), row(NUS), row(AW), row(AW), row(AW), row(D), _full((1, SW)), _full((1, SW)),
                  _full((8, CH, CH)), _full((8, CH, CH)), _full((CH, 8)), _full((1, AW)), _full((1, SW)), _full((D, D)),
                  _full((1, D))],
        out_specs=[row(4 * AW), row(4 * AW), row(NUS),
                   pl.BlockSpec((D, D), lambda i: (0, 0), pipeline_mode=pl.Buffered(1)), _full((8, CH, CH)),
                   _full((CH, 128)), _full((1, SW)), _full((1, SW)), _full((1, AW)), _full((1, SW)), _full((1, D))],
        out_shape=[jax.ShapeDtypeStruct((T, 4 * AW), BF16),
                   jax.ShapeDtypeStruct((T, 4 * AW), BF16), jax.ShapeDtypeStruct((T, NUS), BF16),
                   jax.ShapeDtypeStruct((D, D), F32),
                   jax.ShapeDtypeStruct((8, CH, CH), F32), jax.ShapeDtypeStruct((CH, 128), F32),
                   jax.ShapeDtypeStruct((1, SW), F32), jax.ShapeDtypeStruct((1, SW), F32),
                   jax.ShapeDtypeStruct((1, AW), F32), jax.ShapeDtypeStruct((1, SW), F32), jax.ShapeDtypeStruct((1, D), F32)],
        scratch_shapes=[pltpu.VMEM((tm, SW), F32), pltpu.VMEM((tm, SW), BF16), pltpu.VMEM((tm, SW), F32)],
        compiler_params=_params(1, VMEM_LIMIT_BIG),
    )(dh1, o, us, yatt, qkv, lsec, yb, lng, lnb, sgw, sgwt, sgbt, gatt, gsg, wout, gpm)


def _attn_operands(q, do, o, lsec, qw_ref, dow_ref):
    rows = do.shape[0]
    feat = lax.broadcasted_iota(jnp.int32, (AW, 128), 0)
    head = lax.broadcasted_iota(jnp.int32, (AW, 128), 1)
    sel = jnp.where((feat >= head * HD) & (feat < (head + 1) * HD), 1.0, 0.0)
    delta = jnp.dot(do * o, sel, precision=lax.Precision.HIGHEST, preferred_element_type=F32)
    for hp in range(4):
        cols = slice(hp * 128, (hp + 1) * 128)
        for hh in range(2):
            base = (2 * hp + hh) * 256
            lc = lsec[:, hp * 128 + hh:hp * 128 + hh + 1]
            d_h = delta[:, 2 * hp + hh:2 * hp + hh + 1]
            qw_ref[:, base:base + 128] = _head_rows(q[:, cols], hh, 0.125)
            qw_ref[:, base + 128:base + 256] = _aug_lanes(rows, AUG_A if hh == 0 else AUG_B, _split3(-lc))
            dow_ref[:, base:base + 128] = _head_rows(do[:, cols], hh, 1.0).astype(BF16)
            dow_ref[:, base + 128:base + 256] = _aug_lanes(rows, None, _split3(-d_h))


def _attn_bwd(kk, vk, kt, qw, dow, after):
    T = kk.shape[0]
    tq = _tile(T, 512)
    tk = tq
    nq = T // tq

    def body(kk_ref, vk_ref, kt_ref, qw_ref, dow_ref, after_ref, dqt_ref, dcq_ref, dk_ref, dv_ref, dck_ref, dk_s, dv_s, dck_s):
        j, i = _fold_cols(pl.program_id(1), pl.program_id(2), nq)
        sub8 = lax.broadcasted_iota(jnp.int32, (8, 1), 0)
        lane = lax.broadcasted_iota(jnp.int32, (1, 128), 1)

        @pl.when((pl.program_id(1) == 0) & (pl.program_id(2) == 0))
        def _():
            dqt_ref[...] = jnp.zeros_like(dqt_ref)
            dcq_ref[...] = jnp.zeros_like(dcq_ref)

        @pl.when(i == j)
        def _():
            dk_s[...] = jnp.zeros_like(dk_s)
            dv_s[...] = jnp.zeros_like(dv_s)
            dck_s[...] = jnp.zeros_like(dck_s)

        def step(masked):
            cols = pl.ds(pl.multiple_of(i * tq, tq), tq)
            sub = lax.broadcasted_iota(jnp.int32, (128, 1), 0)
            if masked:
                causal = lax.broadcasted_iota(jnp.int32, (tk, tq), 0) <= lax.broadcasted_iota(jnp.int32, (tk, tq), 1)

            def logits(n):
                pair, base = n // 2, n * 256
                return (_nt(kk_ref[:, pair * 256:(pair + 1) * 256], qw_ref[:, base:base + 256]),
                        _nt(vk_ref[:, pair * 256:(pair + 1) * 256], dow_ref[:, base:base + 256]))

            ahead = logits(0)
            dcq = jnp.zeros((8, tq), F32)
            dck = jnp.zeros((tk, 128), F32)
            for pp in range(PAIRS):
                lanes = slice(pp * 128, (pp + 1) * 128)
                kt2 = kt_ref[lanes, :] * 0.125
                dv = jnp.zeros((tk, 128), F32)
                dk = jnp.zeros((tk, 128), F32)
                dqts = []
                for hh in range(2):
                    head = 2 * pp + hh
                    base = head * 256
                    qw_h = qw_ref[:, base:base + 256]
                    dow_h = dow_ref[:, base:base + 256]
                    logp, dp = ahead
                    if head + 1 < 2 * PAIRS:
                        ahead = logits(head + 1)
                    pr = jnp.exp(logp)
                    if masked:
                        pr = jnp.where(causal, pr, 0.0)
                    ds = pr * dp
                    ds_b = ds.astype(BF16)
                    dv = dv + _nn(pr.astype(BF16), dow_h[:, 0:128])
                    dk = dk + _nn(ds_b, qw_h[:, 0:128])
                    dqts.append(_nn(kt2, ds_b))
                    dcq = dcq + jnp.where(sub8 == head, jnp.sum(ds, axis=0, keepdims=True), 0.0)
                    dck = dck - jnp.where(lane == head, jnp.sum(ds, axis=1, keepdims=True), 0.0)
                dv_s[:, lanes] += dv
                dk_s[:, lanes] += dk
                dqt_ref[lanes, cols] += jnp.where(sub < HD, dqts[0], dqts[1])
            dck_s[...] += dck
            dcq_ref[:, cols] += dcq

        @pl.when(i > j)
        def _():
            step(False)

        @pl.when(i == j)
        def _():
            step(True)

        @pl.when(i == nq - 1)
        def _():
            dk_ref[...] = dk_s[...].astype(BF16)
            dv_ref[...] = dv_s[...].astype(BF16)
            dck_ref[...] = dck_s[...]

    kj = lambda r, t: _fold_cols(r, t, nq)[0]
    qi = lambda r, t: _fold_cols(r, t, nq)[1]
    krow = lambda g, r, t: (kj(r, t), g)
    qrow = lambda g, r, t: (qi(r, t), g)
    return pl.pallas_call(
        body,
        name="attn_bwd",
        grid=_fold_grid(nq),
        in_specs=[
            pl.BlockSpec((tk, PAIRS * 256), krow),
            pl.BlockSpec((tk, PAIRS * 256), krow),
            pl.BlockSpec((PAIRS * 128, tk), lambda g, r, t: (g, kj(r, t))),
            pl.BlockSpec((tq, PAIRS * 512), qrow),
            pl.BlockSpec((tq, PAIRS * 512), qrow),
            pl.BlockSpec(after.shape, lambda g, r, t: (0,) * after.ndim),
        ],
        out_specs=[
            pl.BlockSpec((PAIRS * 128, T), lambda g, r, t: (g, 0), pipeline_mode=pl.Buffered(1)),
            pl.BlockSpec((8, T), lambda g, r, t: (0, 0), pipeline_mode=pl.Buffered(1)),
            pl.BlockSpec((tk, PAIRS * 128), krow),
            pl.BlockSpec((tk, PAIRS * 128), krow),
            pl.BlockSpec((tk, 128), lambda g, r, t: (kj(r, t), 0)),
        ],
        out_shape=[jax.ShapeDtypeStruct((AW, T), F32), jax.ShapeDtypeStruct((8, T), F32),
                   jax.ShapeDtypeStruct((T, AW), BF16), jax.ShapeDtypeStruct((T, AW), BF16),
                   jax.ShapeDtypeStruct((T, FPAD), F32)],
        scratch_shapes=[pltpu.VMEM((tk, PAIRS * 128), F32), pltpu.VMEM((tk, PAIRS * 128), F32),
                        pltpu.VMEM((tk, 128), F32)],
        compiler_params=_params(3),
    )(kk, vk, kt, qw, dow, after)


def _fgate_bwd(dcq, dck, fz, fb):
    T = dck.shape[0]
    tb = _tile(T, 512)
    nb = T // tb

    def body(dcq_ref, dck_ref, fz_ref, fb_ref, df_ref, dfb_ref, carry):
        @pl.when(pl.program_id(0) == 0)
        def _():
            carry[...] = jnp.zeros_like(carry)
            dfb_ref[...] = jnp.zeros_like(dfb_ref)

        head = lax.broadcasted_iota(jnp.int32, (8, FPAD), 0)
        eye = jnp.where(head == lax.broadcasted_iota(jnp.int32, (8, FPAD), 1), 1.0, 0.0)
        dcv = dck_ref[...] + lax.dot_general(dcq_ref[...], eye, (((0,), (0,)), ((), ())),
                                             precision=lax.Precision.HIGHEST, preferred_element_type=F32)
        r = lax.broadcasted_iota(jnp.int32, (128, 128), 0)
        cc = lax.broadcasted_iota(jnp.int32, (128, 128), 1)
        tri = (cc >= r).astype(F32)
        run = carry[...]
        parts = []
        for blk in reversed(range(tb // 128)):
            dc_b = dcv[blk * 128:(blk + 1) * 128, :]
            parts.append(jnp.dot(tri, dc_b, precision=lax.Precision.HIGHEST, preferred_element_type=F32) + run)
            run = run + jnp.sum(dc_b, axis=0, keepdims=True)
        carry[...] = run
        dlf = jnp.concatenate(parts[::-1], axis=0)
        lane = lax.broadcasted_iota(jnp.int32, (tb, FPAD), 1)
        df = jnp.where(lane < 8, dlf * jax.nn.sigmoid(-(fz_ref[...] + fb_ref[...])), 0.0)
        df_ref[...] = df.astype(BF16)
        dfb_ref[...] += jnp.sum(df, axis=0, keepdims=True)

    rev = pl.BlockSpec((tb, FPAD), lambda i: (nb - 1 - i, 0))
    return pl.pallas_call(
        body,
        name="fgate_bwd",
        grid=(nb,),
        in_specs=[pl.BlockSpec((8, tb), lambda i: (0, nb - 1 - i)), rev, rev, _full((1, FPAD))],
        out_specs=[rev, _full((1, FPAD))],
        out_shape=[jax.ShapeDtypeStruct((T, FPAD), BF16), jax.ShapeDtypeStruct((1, FPAD), F32)],
        scratch_shapes=[pltpu.VMEM((1, FPAD), F32)],
        compiler_params=_params(1),
    )(dcq, dck, fz, fb)


def _inproj_bwd(dqt, dk, dv, dus, dfz, wtp, x, dh1, g):
    T = x.shape[0]
    tm = _tile(T, 512)

    def body(dq_ref, dk_ref, dv_ref, dus_ref, dfz_ref, w_ref, x_ref, dh1_ref, g_ref, gx_ref, dg_ref):
        @pl.when(pl.program_id(0) == 0)
        def _():
            dg_ref[...] = jnp.zeros_like(dg_ref)

        da = _tn(dq_ref[...].astype(BF16), w_ref[0:AW, :])
        da += _nn(dk_ref[...], w_ref[AW:2 * AW, :])
        da += _nn(dv_ref[...], w_ref[2 * AW:NQKV, :])
        da += _nn(dus_ref[...], w_ref[NQKV:NQKV + NUS, :])
        da += _nn(dfz_ref[...], w_ref[NQKV + NUS:ZP, :])
        dx, dg = _rms_bwd(da, x_ref[...], g_ref[...])
        gx_ref[...] = dh1_ref[...] + dx
        dg_ref[...] += dg

    row = lambda n: pl.BlockSpec((tm, n), lambda i: (i, 0))
    return pl.pallas_call(
        body,
        name="inproj_bwd",
        grid=(T // tm,),
        in_specs=[pl.BlockSpec((AW, tm), lambda i: (0, i)), row(AW), row(AW), row(NUS), row(FPAD), _full((ZP, D)),
                  row(D), row(D), _full((1, D))],
        out_specs=[row(D), _full((1, D))],
        out_shape=[jax.ShapeDtypeStruct((T, D), F32), jax.ShapeDtypeStruct((1, D), F32)],
        compiler_params=_params(1),
    )(dqt, dk, dv, dus, dfz, wtp, x, dh1, g)


def _sq_relu(f1):
    r = jnp.maximum(f1.astype(F32), 0.0)
    return (r * r).astype(BF16)


def _wgrad(a, bs, name, a_fn=None, out_dtype=F32):
    T, K = a.shape
    tt = _tile(T, 512)
    nb = len(bs)
    narrow = out_dtype != F32

    def out_dims(b, layout):
        if layout == "t":
            return (K, b.shape[0])
        N = b.shape[1]
        if layout is None:
            return (K, N)
        return (N // layout[1], K, layout[1]) if layout[0] == "col" else (K // layout[1], layout[1], N)

    shapes = [out_dims(b, layout) for b, layout in bs]

    def body(*refs):
        a_ref, b_refs, o_refs = refs[0], refs[1:1 + nb], refs[1 + nb:1 + 2 * nb]
        accs = refs[1 + 2 * nb:] if narrow else o_refs

        @pl.when(pl.program_id(0) == 0)
        def _():
            for acc in accs:
                acc[...] = jnp.zeros_like(acc)

        av = a_ref[...] if a_fn is None else a_fn(a_ref[...])
        at = av.astype(BF16).T
        for (b, layout), b_ref, o_ref in zip(bs, b_refs, accs):
            if layout is None:
                o_ref[...] += _nn(at, b_ref[...].astype(BF16))
            elif layout == "t":
                o_ref[...] += _nt(at, b_ref[...].astype(BF16))
            elif layout[0] == "col":
                n = layout[1]
                for k in range(b.shape[1] // n):
                    o_ref[k] += _nn(at, b_ref[:, k * n:(k + 1) * n].astype(BF16))
            else:
                n = layout[1]
                bv = b_ref[...].astype(BF16)
                for k in range(K // n):
                    o_ref[k] += _nn(at[k * n:(k + 1) * n, :], bv)

        if narrow:
            @pl.when(pl.program_id(0) == T // tt - 1)
            def _():
                for o_ref, acc in zip(o_refs, accs):
                    o_ref[...] = acc[...].astype(out_dtype)

    once = lambda shape: pl.BlockSpec(shape, lambda t: (0,) * len(shape), pipeline_mode=pl.Buffered(1))
    res = pl.pallas_call(
        body,
        name=name,
        grid=(T // tt,),
        in_specs=[pl.BlockSpec((tt, K), lambda t: (t, 0))] + [
            pl.BlockSpec((b.shape[0], tt), lambda t: (0, t)) if layout == "t" else pl.BlockSpec((tt, b.shape[1]), lambda t: (t, 0))
            for b, layout in bs],
        out_specs=[once(s) for s in shapes],
        out_shape=[jax.ShapeDtypeStruct(s, out_dtype) for s in shapes],
        scratch_shapes=[pltpu.VMEM(s, F32) for s in shapes] if narrow else [],
        compiler_params=_params(1, VMEM_LIMIT_BIG),
    )(a, *[b for b, _ in bs])
    return res


def _adam_math(w, g, m, v):
    m = ADAM_B1 * m + (1.0 - ADAM_B1) * g
    v = ADAM_B2 * v + (1.0 - ADAM_B2) * (g * g)
    m_hat = m / (1.0 - ADAM_B1 ** ADAM_STEP)
    v_hat = v / (1.0 - ADAM_B2 ** ADAM_STEP)
    delta = -ADAM_LR * (m_hat / (jnp.sqrt(v_hat) + ADAM_EPS) + ADAM_WD * w)
    return delta, m, v


def _adam(parts, w, m, v, name):
    R, C = w.shape
    br = 128 if R % 128 == 0 else R

    def body(p_ref, w_ref, m_ref, v_ref, g_ref, d_ref, nm_ref, nv_ref):
        g = p_ref[0].astype(F32)
        for s in range(1, NDEV):
            g = g + p_ref[s].astype(F32)
        g_ref[...] = g
        d_ref[...], nm_ref[...], nv_ref[...] = _adam_math(w_ref[...], g, m_ref[...], v_ref[...])

    blk = pl.BlockSpec((br, C), lambda i: (i, 0))
    return pl.pallas_call(
        body,
        name=name,
        grid=(R // br,),
        in_specs=[pl.BlockSpec((NDEV, br, C), lambda i: (0, i, 0)), blk, blk, blk],
        out_specs=[blk] * 4,
        out_shape=[jax.ShapeDtypeStruct((R, C), F32)] * 4,
        compiler_params=_params(1),
    )(parts, w, m, v)


_SMALL = (("sg_w", 8 * CH * CH), ("f_bias", 8), ("sg_ln_g", SW), ("sg_ln_b", SW), ("sg_b", 8 * CH), ("att_out_g", AW),
          ("sg_out_g", SW), ("pre_mix_g", D), ("post_mix_g", D), ("pre_ffn_g", D), ("post_ffn_g", D), ("ple_gate_b", D))
_SEG = 8 * 128


def _seg_rows(size):
    return 8 * (-(-size // _SEG))


def _pack(vals, loss_acc):
    parts = []
    for name, size in _SMALL:
        flat = vals[name].reshape(-1)
        rows = _seg_rows(size)
        parts.append(jnp.pad(flat, (0, rows * 128 - size)).reshape(rows, 128))
    parts.append(loss_acc)
    return jnp.concatenate(parts, axis=0)


def _adam_small(parts, ws, ms, vs):
    n = len(_SMALL)
    names = [name for name, _ in _SMALL]

    def body(*refs):
        p_ref, w_refs, m_refs, v_refs = refs[0], refs[1:1 + n], refs[1 + n:1 + 2 * n], refs[1 + 2 * n:1 + 3 * n]
        loss_ref, outs = refs[1 + 3 * n], refs[2 + 3 * n:]
        g_all = p_ref[0]
        for s in range(1, NDEV):
            g_all = g_all + p_ref[s]
        r = 0
        for k, (name, size) in enumerate(_SMALL):
            dst = [outs[kind * n + k] for kind in range(4)]

            def update(g, idx):
                vals = (g,) + _adam_math(w_refs[k][idx], g, m_refs[k][idx], v_refs[k][idx])
                for d, val in zip(dst, vals):
                    d[idx] = val

            if name == "sg_w":
                for grp in range(8):
                    update(g_all[r + grp * CH:r + (grp + 1) * CH, :], (0, grp))
            elif name == "sg_b":
                update(g_all[r:r + 8, :], (0,))
            elif name == "f_bias":
                update(g_all[r:r + 1, 0:8], (slice(None),))
            else:
                update(jnp.concatenate([g_all[r + q:r + q + 1, :] for q in range(size // 128)], axis=1), (slice(None),))
            r += _seg_rows(size)
        loss_ref[...] = g_all[r:r + 1, 0:1] * (0.5 / D)

    arrs = [parts] + [d[name] for d in (ws, ms, vs) for name in names]
    res = pl.pallas_call(
        body,
        name="adam_small",
        in_specs=[_full(a.shape) for a in arrs],
        out_specs=[_full((1, 1))] + [_full(ws[name].shape) for _ in range(4) for name in names],
        out_shape=[jax.ShapeDtypeStruct((1, 1), F32)] + [jax.ShapeDtypeStruct(ws[name].shape, F32) for _ in range(4) for name in names],
        compiler_params=pltpu.CompilerParams(vmem_limit_bytes=VMEM_LIMIT),
    )(*arrs)
    return res[0], {name: [res[1 + kind * n + k] for kind in range(4)] for k, name in enumerate(names)}


def kernel(x, p, w_in, f_bias, sg_ln_g, sg_ln_b, sg_w, sg_b, att_out_g, sg_out_g, w_out, pre_mix_g, post_mix_g, pre_ffn_g, post_ffn_g, w_ff1, w_ff2, ple_w, ple_gate_w, ple_gate_b, loss_target, m_w_in, m_f_bias, m_sg_ln_g, m_sg_ln_b, m_sg_w, m_sg_b, m_att_out_g, m_sg_out_g, m_w_out, m_pre_mix_g, m_post_mix_g, m_pre_ffn_g, m_post_ffn_g, m_w_ff1, m_w_ff2, m_ple_w, m_ple_gate_w, m_ple_gate_b, v_w_in, v_f_bias, v_sg_ln_g, v_sg_ln_b, v_sg_w, v_sg_b, v_att_out_g, v_sg_out_g, v_w_out, v_pre_mix_g, v_post_mix_g, v_pre_ffn_g, v_post_ffn_g, v_w_ff1, v_w_ff2, v_ple_w, v_ple_gate_w, v_ple_gate_b):
    small_w = dict(sg_w=sg_w, f_bias=f_bias, sg_ln_g=sg_ln_g, sg_ln_b=sg_ln_b, sg_b=sg_b, att_out_g=att_out_g,
                   sg_out_g=sg_out_g, pre_mix_g=pre_mix_g, post_mix_g=post_mix_g, pre_ffn_g=pre_ffn_g,
                   post_ffn_g=post_ffn_g, ple_gate_b=ple_gate_b)
    small_m = dict(sg_w=m_sg_w, f_bias=m_f_bias, sg_ln_g=m_sg_ln_g, sg_ln_b=m_sg_ln_b, sg_b=m_sg_b, att_out_g=m_att_out_g,
                   sg_out_g=m_sg_out_g, pre_mix_g=m_pre_mix_g, post_mix_g=m_post_mix_g, pre_ffn_g=m_pre_ffn_g,
                   post_ffn_g=m_post_ffn_g, ple_gate_b=m_ple_gate_b)
    small_v = dict(sg_w=v_sg_w, f_bias=v_f_bias, sg_ln_g=v_sg_ln_g, sg_ln_b=v_sg_ln_b, sg_b=v_sg_b, att_out_g=v_att_out_g,
                   sg_out_g=v_sg_out_g, pre_mix_g=v_pre_mix_g, post_mix_g=v_post_mix_g, pre_ffn_g=v_pre_ffn_g,
                   post_ffn_g=v_post_ffn_g, ple_gate_b=v_ple_gate_b)
    big = dict(w_in=(w_in, m_w_in, v_w_in), w_out=(w_out, m_w_out, v_w_out), w_ff1=(w_ff1, m_w_ff1, v_w_ff1),
               w_ff2=(w_ff2, m_w_ff2, v_w_ff2), ple_w=(ple_w, m_ple_w, v_ple_w),
               ple_gate_w=(ple_gate_w, m_ple_gate_w, v_ple_gate_w))

    xt, pt, tgt = x[0], p[0, 0], loss_target[0]
    ws = W_IN_COLS // NDEV

    gw_in = _gather(jnp.pad(jnp.transpose(w_in[0]).astype(BF16), ((0, WS_PAD - ws), (0, 0))), "gather_w_in")
    rest = _xchg_start([w_out[0].astype(BF16), w_ff1[0].astype(BF16), w_ff2[0].astype(BF16), ple_w[0].astype(BF16),
                        ple_gate_w[0].astype(BF16)], True, "gather_rest_start")
    wint = gw_in[:, 0:ws, :].reshape(W_IN_COLS, D)
    wtp = jnp.concatenate([wint[0:NQKV], wint[NQKV + 8:W_IN_COLS], wint[NQKV:NQKV + 8],
                           jnp.zeros((FPAD - 8, D), BF16)], axis=0)

    fb = jnp.pad(f_bias.astype(F32), ((0, 0), (0, FPAD - 8)))
    sgw = sg_w[0]
    sgwt = jnp.transpose(sg_w[0], (0, 2, 1))
    sgbt = jnp.transpose(sg_b[0])

    qkv, kt, vt, us, fz, ab = _inproj(xt, pre_mix_g + rest[4][0, 0], wtp)
    ct, kk, vk = _fcum(fz, fb, qkv)
    yatt, lsec = _attn_fwd(qkv, kk, vt, ct)
    gw_out, gw1, gw2, gwpe, gwg = _xchg_wait(rest, True, yatt, "gather_rest_wait")
    wout = gw_out.reshape(D, D)
    wg = gwg.reshape(D, D)
    wpe = jnp.transpose(gwpe, (1, 0, 2)).reshape(PLE, D)
    h1, yb, o = _sgu_out(us, yatt, xt, sg_ln_g, sg_ln_b, sgw, sgbt, att_out_g, sg_out_g, wout, post_mix_g)
    c2b, f1b, ff, h2 = _ffn_fwd(h1, pre_ffn_g, gw1, gw2, post_ffn_g)
    dh2, dbg, loss_acc, g_g, g_pe = _ple_loss(h2, pt, tgt, wg, ple_gate_b, wpe)
    g_g = g_g.reshape(NDEV, D // NDEV, D)
    g_pe = jnp.transpose(g_pe.reshape(PLE, NDEV, D // NDEV), (1, 0, 2))

    dffb, df1, dh1, dgpostffn, dgpreffn = _ffn_bwd(dh2, ff, h1, f1b, gw1, gw2, post_ffn_g, pre_ffn_g)
    (g_1,) = _wgrad(c2b, [(df1, ("col", DFF // NDEV))], "wgrad_ff1")
    (g_2,) = _wgrad(f1b, [(dffb, ("row", DFF // NDEV))], "wgrad_ff2", a_fn=_sq_relu)
    early = _xchg_start([g_1, g_2, g_pe, g_g], False, "scatter_early_start")
    qw, dow, dus, g_out, dsgw, dsgbt, dlng, dlnb, dgatt, dgsg, dgpostmix = _mix_bwd(
        dh1, o, us, yatt, qkv, lsec, yb, sg_ln_g, sg_ln_b, sgw, sgwt, sgbt, att_out_g, sg_out_g, wout,
        post_mix_g + early[4][0, 0])
    mid = _xchg_start([g_out.reshape(NDEV, D // NDEV, D)], False, "scatter_mid_start")
    dqt, dcq, dk, dv, dck = _attn_bwd(kk, vk, kt, qw, dow, mid[4])
    dfz, dfb = _fgate_bwd(dcq, dck, fz, fb)

    gq, gk, gv, gus, gf = _wgrad(ab, [(dqt, "t"), (dk, None), (dv, None), (dus, None), (dfz, None)], "wgrad_in",
                                 out_dtype=BF16)
    g_in = jnp.concatenate([gq, gk, gv, gf[:, 0:8], gus], axis=1)
    g_in = jnp.transpose(g_in.reshape(D, NDEV, ws), (1, 0, 2))
    late = _xchg_start([g_in], False, "scatter_late_start")
    grad_x, dgpremix = _inproj_bwd(dqt, dk, dv, dus, dfz, wtp, xt, dh1, pre_mix_g + late[4][0, 0])

    small_g = dict(sg_w=dsgw, f_bias=dfb[:, 0:8], sg_ln_g=dlng, sg_ln_b=dlnb, sg_b=jnp.transpose(dsgbt[:, 0:8]),
                   att_out_g=dgatt, sg_out_g=dgsg, pre_mix_g=dgpremix, post_mix_g=dgpostmix, pre_ffn_g=dgpreffn,
                   post_ffn_g=dgpostffn, ple_gate_b=dbg)

    r_1, r_2, r_pe, r_g = _xchg_wait(early, False, grad_x, "scatter_early_wait")
    (r_out,) = _xchg_wait(mid, False, grad_x, "scatter_mid_wait")
    r_small = _gather(_pack(small_g, loss_acc), "gather_small_grads")

    res = {}

    def adam_big(name, parts):
        w, m, v = big[name]
        res[name] = [t[None] for t in _adam(parts, w[0], m[0], v[0], "adam_" + name)]

    for name, parts in (("w_out", r_out), ("w_ff1", r_1), ("w_ff2", r_2), ("ple_w", r_pe), ("ple_gate_w", r_g)):
        adam_big(name, parts)
    (r_in,) = _xchg_wait(late, False, res["w_ff1"][0], "scatter_late_wait")
    adam_big("w_in", r_in)
    loss, small = _adam_small(r_small, small_w, small_m, small_v)
    res.update(small)

    order = ["w_in", "f_bias", "sg_ln_g", "sg_ln_b", "sg_w", "sg_b", "att_out_g", "sg_out_g", "w_out", "pre_mix_g",
             "post_mix_g", "pre_ffn_g", "post_ffn_g", "w_ff1", "w_ff2", "ple_w", "ple_gate_w", "ple_gate_b"]
    outs = [loss[0, 0], grad_x[None]]
    for kind in range(4):
        outs += [res[name][kind] for name in order]
    return tuple(outs)
```

```python
import jax
import jax.numpy as jnp
from jax import lax
from jax.experimental import pallas as pl
from jax.experimental.pallas import tpu as pltpu

F32 = jnp.float32
BF16 = jnp.bfloat16

NDEV = 8
D = 1024
AW = 512
SW = 512
HD = 64
CH = 128
DFF = 4096
PLE = 256
NQKV = 3 * AW
NUS = 2 * SW
FPAD = 128
ZP = NQKV + NUS + FPAD
W_IN_COLS = 2568
WS_PAD = 336
EPS = 1e-6
MASKV = -1e30
GELU_K = 0.7978845608028654
GELU_C = 0.044715

ADAM_LR = 0.001
ADAM_B1 = 0.9
ADAM_B2 = 0.999
ADAM_EPS = 1e-08
ADAM_WD = 0.01
ADAM_STEP = 10

VMEM_LIMIT = 48 * 1024 * 1024
VMEM_LIMIT_BIG = 60 * 1024 * 1024


def _nn(a, b):
    return jnp.dot(a, b, preferred_element_type=F32)


def _nt(a, b):
    return lax.dot_general(a, b, (((1,), (1,)), ((), ())), preferred_element_type=F32)


def _tn(a, b):
    return lax.dot_general(a, b, (((0,), (0,)), ((), ())), preferred_element_type=F32)


def _tile(n, pref):
    return min(n, pref)


def _params(n_axes, vmem=VMEM_LIMIT):
    return pltpu.CompilerParams(dimension_semantics=("arbitrary",) * n_axes, vmem_limit_bytes=vmem)


def _full(shape):
    nd = len(shape)
    return pl.BlockSpec(shape, lambda *_: (0,) * nd)


def _rms_fwd(x, g):
    r = lax.rsqrt(jnp.mean(x * x, axis=-1, keepdims=True) + EPS)
    return x * r * g


def _rms_bwd(dy, x, g):
    n = x.shape[-1]
    r = lax.rsqrt(jnp.mean(x * x, axis=-1, keepdims=True) + EPS)
    u = dy * g
    s = jnp.sum(x * u, axis=-1, keepdims=True)
    dx = r * u - x * (r * r * r * (s * (1.0 / n)))
    dg = jnp.sum(dy * (x * r), axis=0, keepdims=True)
    return dx, dg


def _gelu(x):
    t = jnp.tanh(x * (GELU_K + (GELU_K * GELU_C) * (x * x)))
    return x * (0.5 + 0.5 * t), t


def _gelu_grad(x, t):
    return (0.5 + 0.5 * t) + (0.5 * x) * (1.0 - t * t) * (GELU_K + (3.0 * GELU_K * GELU_C) * (x * x))


def _gather(arr, name):
    def body(x_ref, out_ref, send, recv, loc):
        x, y, c = lax.axis_index("x"), lax.axis_index("y"), lax.axis_index("c")
        me, sibling = (x, y, c), (x, y, 1 - c)
        chips = [(1 - x, y), (x, 1 - y), (1 - x, 1 - y)]

        def slot(px, py, pc):
            return out_ref.at[4 * px + 2 * py + pc]

        def copy(k, block, to, src=None):
            return pltpu.make_async_remote_copy(
                src_ref=slot(*block) if src is None else src, dst_ref=slot(*block), send_sem=send.at[k],
                recv_sem=recv.at[k], device_id=to, device_id_type=pl.DeviceIdType.MESH)

        mine = pltpu.make_async_copy(x_ref, slot(*me), loc)
        mine.start()
        first = [copy(0, me, sibling, src=x_ref)] + [copy(1 + j, me, (*chip, c), src=x_ref) for j, chip in enumerate(chips)]
        for cp in first:
            cp.start()
        passed = [copy(4 + j, (*chip, c), sibling) for j, chip in enumerate(chips)]
        for j, chip in enumerate(chips):
            copy(1 + j, (*chip, c), me).wait_recv()
            passed[j].start()
        copy(0, sibling, me).wait_recv()
        for j, chip in enumerate(chips):
            copy(4 + j, (*chip, 1 - c), me).wait_recv()
        for cp in first + passed:
            cp.wait_send()
        mine.wait()

    hbm = pl.BlockSpec(memory_space=pltpu.HBM)
    return pl.pallas_call(
        body,
        name=name,
        out_shape=jax.ShapeDtypeStruct((NDEV,) + arr.shape, arr.dtype),
        in_specs=[hbm],
        out_specs=hbm,
        scratch_shapes=[pltpu.SemaphoreType.DMA((NDEV - 1,)), pltpu.SemaphoreType.DMA((NDEV - 1,)), pltpu.SemaphoreType.DMA],
    )(arr)


def _peers(x, y, c):
    out = []
    for k in range(1, NDEV):
        out.append((1 - x if (k >> 2) & 1 else x, 1 - y if (k >> 1) & 1 else y, 1 - c if k & 1 else c))
    return out


def _xchg_start(arrs, gather, name):
    n = len(arrs)
    me = 4 * lax.axis_index("x") + 2 * lax.axis_index("y") + lax.axis_index("c")
    lands = []
    for a in arrs:
        shape = ((NDEV,) + a.shape) if gather else a.shape
        own = a[None] if gather else lax.dynamic_slice_in_dim(a, me, 1, axis=0)
        lands.append(lax.dynamic_update_slice_in_dim(lax.empty(shape, a.dtype), own, me, axis=0))

    def body(*refs):
        ins, lnd = refs[:n], refs[n:2 * n]
        send, recv, token = refs[2 * n:3 * n], refs[3 * n:4 * n], refs[-1]
        x, y, c = lax.axis_index("x"), lax.axis_index("y"), lax.axis_index("c")
        mine = 4 * x + 2 * y + c
        for px, py, pc in _peers(x, y, c):
            peer = 4 * px + 2 * py + pc
            for a in range(n):
                pltpu.make_async_remote_copy(
                    src_ref=ins[a] if gather else ins[a].at[peer],
                    dst_ref=lnd[a].at[mine],
                    send_sem=send[a],
                    recv_sem=recv[a],
                    device_id=(px, py, pc),
                    device_id_type=pl.DeviceIdType.MESH,
                ).start()
        token[...] = jnp.zeros_like(token)

    hbm = pl.BlockSpec(memory_space=pltpu.HBM)
    sem = pl.BlockSpec(memory_space=pltpu.SEMAPHORE)
    res = pl.pallas_call(
        body,
        name=name,
        out_shape=(*[pltpu.SemaphoreType.DMA(())] * (2 * n),
                   *[pltpu.HBM(a.shape, a.dtype) for a in arrs], *[pltpu.HBM(l.shape, l.dtype) for l in lands],
                   jax.ShapeDtypeStruct((8, 128), F32)),
        in_specs=[hbm] * (2 * n),
        out_specs=(*([sem] * (2 * n)), *([hbm] * (2 * n)), pl.BlockSpec(memory_space=pltpu.VMEM)),
        input_output_aliases={i: 2 * n + i for i in range(2 * n)},
        compiler_params=pltpu.CompilerParams(has_side_effects=pltpu.SideEffectType.DATAFLOW_SIDE_EFFECTING),
    )(*[pltpu.with_memory_space_constraint(a, pltpu.HBM) for a in arrs],
      *[pltpu.with_memory_space_constraint(l, pltpu.HBM) for l in lands])
    return list(res[0:n]), list(res[n:2 * n]), list(res[2 * n:3 * n]), list(res[3 * n:4 * n]), res[-1]


def _xchg_wait(started, gather, after, name):
    send, recv, srcs, lands, _ = started
    n = len(srcs)

    def body(*refs):
        lnd = refs[n:2 * n]
        send, recv = refs[2 * n:3 * n], refs[3 * n:4 * n]
        me = (lax.axis_index("x"), lax.axis_index("y"), lax.axis_index("c"))
        for a in range(n):
            seven = lnd[a].at[pl.ds(0, NDEV - 1)]
            cp = pltpu.make_async_remote_copy(src_ref=seven, dst_ref=seven, send_sem=send[a], recv_sem=recv[a],
                                              device_id=me, device_id_type=pl.DeviceIdType.MESH)
            cp.wait_send()
            cp.wait_recv()

    hbm = pl.BlockSpec(memory_space=pltpu.HBM)
    sem = pl.BlockSpec(memory_space=pltpu.SEMAPHORE)
    res = pl.pallas_call(
        body,
        name=name,
        out_shape=tuple([pltpu.HBM(a.shape, a.dtype) for a in srcs] + [pltpu.HBM(l.shape, l.dtype) for l in lands]),
        in_specs=[hbm] * (2 * n) + [sem] * (2 * n) + [pl.BlockSpec(memory_space=pl.ANY)],
        out_specs=tuple([hbm] * (2 * n)),
        input_output_aliases={i: i for i in range(2 * n)},
        compiler_params=pltpu.CompilerParams(has_side_effects=pltpu.SideEffectType.DATAFLOW_SIDE_EFFECTING),
    )(*srcs, *lands, *send, *recv, after)
    return list(res[n:])


def _inproj(x, g, wtp):
    T = x.shape[0]
    tm = _tile(T, 512)

    def body(x_ref, g_ref, w_ref, qkv_ref, kt_ref, vt_ref, us_ref, fz_ref, ab_ref):
        a = _rms_fwd(x_ref[...], g_ref[...]).astype(BF16)
        ab_ref[...] = a
        qkv_ref[:, 0:AW] = _nt(a, w_ref[0:AW, :]).astype(BF16)
        kk = _nt(a, w_ref[AW:2 * AW, :])
        qkv_ref[:, AW:2 * AW] = kk.astype(BF16)
        kt_ref[...] = kk.T.astype(BF16)
        vv = _nt(a, w_ref[2 * AW:NQKV, :])
        qkv_ref[:, 2 * AW:NQKV] = vv.astype(BF16)
        vt_ref[...] = vv.T.astype(BF16)
        us_ref[...] = _nt(a, w_ref[NQKV:NQKV + NUS, :])
        fz_ref[...] = _nt(a, w_ref[NQKV + NUS:ZP, :])

    row = lambda n: pl.BlockSpec((tm, n), lambda i: (i, 0))
    col = pl.BlockSpec((AW, tm), lambda i: (0, i))
    return pl.pallas_call(
        body,
        name="inproj",
        grid=(T // tm,),
        in_specs=[row(D), _full((1, D)), _full((ZP, D))],
        out_specs=[row(NQKV), col, col, row(NUS), row(FPAD), row(D)],
        out_shape=[
            jax.ShapeDtypeStruct((T, NQKV), BF16),
            jax.ShapeDtypeStruct((AW, T), BF16),
            jax.ShapeDtypeStruct((AW, T), BF16),
            jax.ShapeDtypeStruct((T, NUS), F32),
            jax.ShapeDtypeStruct((T, FPAD), F32),
            jax.ShapeDtypeStruct((T, D), BF16),
        ],
        compiler_params=_params(1),
    )(x, g, wtp)


def _log_sigmoid(z):
    return jnp.minimum(z, 0.0) - jnp.log1p(jnp.exp(-jnp.abs(z)))


def _split3(x):
    hi = x.astype(BF16)
    r1 = x - hi.astype(F32)
    mid = r1.astype(BF16)
    lo = (r1 - mid.astype(F32)).astype(BF16)
    return hi, mid, lo


AUG_A, AUG_B, AUG_ONE = 0, 3, 6


def _aug_lanes(rows, first_one, pieces):
    lane = lax.broadcasted_iota(jnp.int32, (rows, 128), 1)
    out = jnp.zeros((rows, 128), F32)
    if first_one is not None:
        out = jnp.where((lane >= first_one) & (lane < first_one + 3), 1.0, out)
    for n, piece in enumerate(pieces):
        out = jnp.where(lane == AUG_ONE + n, piece.astype(F32), out)
    return out.astype(BF16)


def _fcum(fz, fb, qkv):
    T = fz.shape[0]
    tb = _tile(T, 512)

    def body(fz_ref, fb_ref, k_ref, v_ref, ct_ref, kk_ref, vk_ref, carry, tri_s, sel_s):
        @pl.when(pl.program_id(0) == 0)
        def _():
            carry[...] = jnp.zeros_like(carry)
            src = lax.broadcasted_iota(jnp.int32, (128, 128), 0)
            dst = lax.broadcasted_iota(jnp.int32, (128, 128), 1)
            tri_s[...] = (dst <= src).astype(F32)
            for hp in range(4):
                for n in range(3):
                    pick = ((src == 2 * hp) & (dst == AUG_A + n)) | ((src == 2 * hp + 1) & (dst == AUG_B + n))
                    sel_s[3 * hp + n] = jnp.where(pick, -1.0, 0.0).astype(BF16)

        lf = _log_sigmoid(fz_ref[...] + fb_ref[...])
        run = carry[...]
        parts = []
        for blk in range(tb // 128):
            lf_b = lf[blk * 128:(blk + 1) * 128, :]
            parts.append(jnp.dot(tri_s[...], lf_b, precision=lax.Precision.HIGHEST, preferred_element_type=F32) + run)
            run = run + jnp.sum(lf_b, axis=0, keepdims=True)
        carry[...] = run
        cs = jnp.concatenate(parts, axis=0)
        ct_ref[...] = cs.T[0:8, :]

        pieces = _split3(cs)
        lane = lax.broadcasted_iota(jnp.int32, (1, 128), 1)
        ones = jnp.where((lane >= AUG_ONE) & (lane < AUG_ONE + 3), 1.0, 0.0)
        for hp in range(4):
            aug = jnp.zeros((tb, 128), F32) + ones
            for n, piece in enumerate(pieces):
                aug = aug + _nn(piece, sel_s[3 * hp + n])
            aug = aug.astype(BF16)
            kk_ref[:, hp * 256:hp * 256 + 128] = k_ref[:, hp * 128:(hp + 1) * 128]
            kk_ref[:, hp * 256 + 128:(hp + 1) * 256] = aug
            vk_ref[:, hp * 256:hp * 256 + 128] = v_ref[:, hp * 128:(hp + 1) * 128]
            vk_ref[:, hp * 256 + 128:(hp + 1) * 256] = aug

    row = lambda n, c: pl.BlockSpec((tb, n), lambda i: (i, c))
    return pl.pallas_call(
        body,
        name="fcum",
        grid=(T // tb,),
        in_specs=[row(FPAD, 0), _full((1, FPAD)), row(AW, 1), row(AW, 2)],
        out_specs=[pl.BlockSpec((8, tb), lambda i: (0, i)), row(2 * AW, 0), row(2 * AW, 0)],
        out_shape=[jax.ShapeDtypeStruct((8, T), F32), jax.ShapeDtypeStruct((T, 2 * AW), BF16),
                   jax.ShapeDtypeStruct((T, 2 * AW), BF16)],
        scratch_shapes=[pltpu.VMEM((1, FPAD), F32), pltpu.VMEM((128, 128), F32), pltpu.VMEM((12, 128, 128), BF16)],
        compiler_params=_params(1),
    )(fz, fb, qkv, qkv)


def _fold_rows(r, t, nq):
    if nq == 1:
        return r, t
    low = t <= r
    return jnp.where(low, r, nq - 1 - r), jnp.where(low, t, t - r - 1)


def _fold_cols(r, t, nq):
    if nq == 1:
        return r, t
    first = t < nq - r
    j = jnp.where(first, r, nq - 1 - r)
    return j, jnp.where(first, r + t, nq - 1 - r + (t - (nq - r)))


PAIRS = 4


def _fold_grid(nq):
    assert (nq == 1 or nq % 2 == 0) and PAIRS == 4
    return (4 // PAIRS, 1, 1) if nq == 1 else (4 // PAIRS, nq // 2, nq + 1)


def _head_rows(x2, hh, scale):
    is_a = lax.broadcasted_iota(jnp.int32, (1, 128), 1) < HD
    keep = is_a if hh == 0 else jnp.logical_not(is_a)
    return jnp.where(keep, x2, jnp.zeros_like(x2)) * scale


def _attn_fwd(qkv, kk, vt, ct):
    T = qkv.shape[0]
    tq = _tile(T, 512)
    tk = tq
    nq = T // tq

    def body(q_ref, kk_ref, vt_ref, ctq_ref, o_ref, lsec_ref, qw_s, m_s, l_s, acc_s):
        i, j = _fold_rows(pl.program_id(1), pl.program_id(2), nq)
        sub8 = lax.broadcasted_iota(jnp.int32, (8, 1), 0)

        def cref_of(pp, hh):
            head = 2 * (PAIRS * pl.program_id(0) + pp) + hh
            return jnp.sum(jnp.where(sub8 == head, ctq_ref[:, 0:1], 0.0), axis=0, keepdims=True)

        @pl.when(j == 0)
        def _():
            for pp in range(PAIRS):
                q2 = q_ref[:, pp * 128:(pp + 1) * 128]
                for hh in range(2):
                    rows = slice(hh * tq, (hh + 1) * tq)
                    qw_s[pp, rows, 0:128] = _head_rows(q2, hh, 0.125)
                    qw_s[pp, rows, 128:256] = _aug_lanes(tq, AUG_A if hh == 0 else AUG_B, _split3(cref_of(pp, hh)))
            m_s[...] = jnp.full_like(m_s, MASKV)
            l_s[...] = jnp.zeros_like(l_s)
            acc_s[...] = jnp.zeros_like(acc_s)

        def step(masked):
            if masked:
                causal = lax.broadcasted_iota(jnp.int32, (tk, tq), 0) <= lax.broadcasted_iota(jnp.int32, (tk, tq), 1)
            logits = lambda pp: _nt(kk_ref[:, pp * 256:(pp + 1) * 256], qw_s[pp])
            s_next = logits(0)
            for pp in range(PAIRS):
                s2 = s_next
                if pp + 1 < PAIRS:
                    s_next = logits(pp + 1)
                vt2 = vt_ref[pp * 128:(pp + 1) * 128, :]
                for hh in range(2):
                    n = 2 * pp + hh
                    s = s2[:, hh * tq:(hh + 1) * tq]
                    if masked:
                        s = jnp.where(causal, s, MASKV)
                    m_prev = m_s[n]
                    m_new = jnp.maximum(m_prev, jnp.max(s, axis=0, keepdims=True))
                    pr = jnp.exp(s - m_new)
                    alpha = jnp.exp(m_prev - m_new)
                    l_s[n] = alpha * l_s[n] + jnp.sum(pr, axis=0, keepdims=True)
                    m_s[n] = m_new
                    acc_s[n] = alpha * acc_s[n] + _nn(vt2, pr.astype(BF16))

        @pl.when(j < i)
        def _():
            step(False)

        @pl.when(j == i)
        def _():
            step(True)
            sub = lax.broadcasted_iota(jnp.int32, (128, 1), 0)
            for pp in range(PAIRS):
                a, b = 2 * pp, 2 * pp + 1
                ot = jnp.where(sub < HD, acc_s[a] * (1.0 / l_s[a]), acc_s[b] * (1.0 / l_s[b]))
                o_ref[:, pp * 128:(pp + 1) * 128] = ot.T
                lrow = [m_s[2 * pp + hh] + jnp.log(l_s[2 * pp + hh]) - cref_of(pp, hh) for hh in range(2)]
                lsec_ref[:, pp * 128:(pp + 1) * 128] = jnp.where(sub == 0, lrow[0], jnp.where(sub == 1, lrow[1], 0.0)).T

    qi = lambda r, t: _fold_rows(r, t, nq)[0]
    kj = lambda r, t: _fold_rows(r, t, nq)[1]
    return pl.pallas_call(
        body,
        name="attn_fwd",
        grid=_fold_grid(nq),
        in_specs=[
            pl.BlockSpec((tq, PAIRS * 128), lambda g, r, t: (qi(r, t), g)),
            pl.BlockSpec((tk, PAIRS * 256), lambda g, r, t: (kj(r, t), g)),
            pl.BlockSpec((PAIRS * 128, tk), lambda g, r, t: (g, kj(r, t))),
            pl.BlockSpec((8, tq), lambda g, r, t: (0, qi(r, t))),
        ],
        out_specs=[pl.BlockSpec((tq, PAIRS * 128), lambda g, r, t: (qi(r, t), g))] * 2,
        out_shape=[jax.ShapeDtypeStruct((T, AW), F32)] * 2,
        scratch_shapes=[pltpu.VMEM((PAIRS, 2 * tq, 256), BF16), pltpu.VMEM((2 * PAIRS, 1, tq), F32),
                        pltpu.VMEM((2 * PAIRS, 1, tq), F32), pltpu.VMEM((2 * PAIRS, 128, tq), F32)],
        compiler_params=_params(3),
    )(qkv, kk, vt, ct)


def _sgu_forward(us_ref, lng, lnb, w_ref, bt_ref, mixed_s, vnb_s, tm):
    is_a = lax.broadcasted_iota(jnp.int32, (1, 128), 1) < HD
    u = us_ref[:, 0:SW]
    vs = us_ref[:, SW:NUS]
    ug, tu = _gelu(u)
    vg, tv = _gelu(vs)
    mu = jnp.mean(vg, axis=-1, keepdims=True)
    xc = vg - mu
    rstd = lax.rsqrt(jnp.mean(xc * xc, axis=-1, keepdims=True) + EPS)
    vhat = xc * rstd
    vnb_s[...] = (vhat * lng + lnb).astype(BF16)
    rr = lax.broadcasted_iota(jnp.int32, (CH, CH), 0)
    cc = lax.broadcasted_iota(jnp.int32, (CH, CH), 1)
    tril = cc <= rr
    for jj in range(4):
        wa = jnp.where(tril, w_ref[2 * jj], 0.0).astype(BF16)
        wb = jnp.where(tril, w_ref[2 * jj + 1], 0.0).astype(BF16)
        ba = bt_ref[:, 2 * jj:2 * jj + 1]
        bb = bt_ref[:, 2 * jj + 1:2 * jj + 2]
        for ch in range(tm // CH):
            rs, cs = slice(ch * CH, (ch + 1) * CH), slice(jj * 128, (jj + 1) * 128)
            vn2 = vnb_s[rs, cs]
            mixed_s[rs, cs] = jnp.where(is_a, _nn(wa, vn2) + ba, _nn(wb, vn2) + bb)
    mixed = mixed_s[...]
    return u, vs, ug, tu, tv, vhat, rstd, mixed, ug * mixed


def _sgu_out(us, yatt, x, lng, lnb, sgw, sgbt, gatt, gsg, wout, gpm):
    T = us.shape[0]
    tm = _tile(T, 512)

    def body(us_ref, ya_ref, x_ref, lng_ref, lnb_ref, w_ref, bt_ref, ga_ref, gs_ref, wo_ref, gp_ref,
             h1_ref, yb_ref, o_ref, mixed_s, vnb_s):
        ysg = _sgu_forward(us_ref, lng_ref[...], lnb_ref[...], w_ref, bt_ref, mixed_s, vnb_s, tm)[-1]
        yb_ref[:, 0:AW] = _rms_fwd(ya_ref[...], ga_ref[...]).astype(BF16)
        yb_ref[:, AW:D] = _rms_fwd(ysg, gs_ref[...]).astype(BF16)
        o = _nn(yb_ref[...], wo_ref[...])
        o_ref[...] = o
        h1_ref[...] = x_ref[...] + _rms_fwd(o, gp_ref[...])

    row = lambda n: pl.BlockSpec((tm, n), lambda i: (i, 0))
    return pl.pallas_call(
        body,
        name="sgu_out",
        grid=(T // tm,),
        in_specs=[row(NUS), row(AW), row(D), _full((1, SW)), _full((1, SW)), _full((8, CH, CH)), _full((CH, 8)),
                  _full((1, AW)), _full((1, SW)), _full((D, D)), _full((1, D))],
        out_specs=[row(D), row(D), row(D)],
        out_shape=[jax.ShapeDtypeStruct((T, D), F32), jax.ShapeDtypeStruct((T, D), BF16), jax.ShapeDtypeStruct((T, D), F32)],
        scratch_shapes=[pltpu.VMEM((tm, SW), F32), pltpu.VMEM((tm, SW), BF16)],
        compiler_params=_params(1),
    )(us, yatt, x, lng, lnb, sgw, sgbt, gatt, gsg, wout, gpm)


def _ffn_fwd(h1, gpre, w1g, w2g, gpost):
    T = h1.shape[0]
    tm = _tile(T, 512)
    nb, hb = w1g.shape[0], w1g.shape[2]

    def body(h1_ref, gpre_ref, w1_ref, w2_ref, gpost_ref, c2_ref, f1_ref, ff_ref, h2_ref):
        h1 = h1_ref[...]
        c2 = _rms_fwd(h1, gpre_ref[...]).astype(BF16)
        c2_ref[...] = c2
        ff = jnp.zeros((tm, D), F32)
        for j in range(nb):
            f1 = _nn(c2, w1_ref[j])
            f1_ref[:, j * hb:(j + 1) * hb] = f1.astype(BF16)
            r = jnp.maximum(f1, 0.0)
            ff = ff + _nn((r * r).astype(BF16), w2_ref[j])
        ff_ref[...] = ff
        h2_ref[...] = h1 + _rms_fwd(ff, gpost_ref[...])

    row = lambda n: pl.BlockSpec((tm, n), lambda i: (i, 0))
    once = lambda shape: pl.BlockSpec(shape, lambda i: (0,) * len(shape), pipeline_mode=pl.Buffered(1))
    return pl.pallas_call(
        body,
        name="ffn_fwd",
        grid=(T // tm,),
        in_specs=[row(D), _full((1, D)), once((nb, D, hb)), once((nb, hb, D)), _full((1, D))],
        out_specs=[row(D), row(DFF), row(D), row(D)],
        out_shape=[jax.ShapeDtypeStruct((T, D), BF16), jax.ShapeDtypeStruct((T, DFF), BF16),
                   jax.ShapeDtypeStruct((T, D), F32), jax.ShapeDtypeStruct((T, D), F32)],
        compiler_params=_params(1, VMEM_LIMIT_BIG),
    )(h1, gpre, w1g, w2g, gpost)


def _ple_loss(h2, p, tgt, wg, bg, wpe):
    T = h2.shape[0]
    tm = _tile(T, 512)

    def body(h2_ref, p_ref, t_ref, wg_ref, bg_ref, wpe_ref, dh2_ref, dbg_ref, loss_ref, dwg_ref, dwpe_ref):
        @pl.when(pl.program_id(0) == 0)
        def _():
            for r in (dbg_ref, loss_ref, dwg_ref, dwpe_ref):
                r[...] = jnp.zeros_like(r)

        h2 = h2_ref[...]
        h2b = h2.astype(BF16)
        gate = jax.nn.sigmoid(_nn(h2b, wg_ref[...]) + bg_ref[...])
        pb = p_ref[...].astype(BF16)
        pe = _nn(pb, wpe_ref[...])
        diff = (h2 + gate * pe) - t_ref[...]
        loss_ref[...] += jnp.sum(diff * diff)
        dh3 = diff * (1.0 / D)
        dpre = (dh3 * pe) * (gate * (1.0 - gate))
        dpre_b = dpre.astype(BF16)
        dbg_ref[...] += jnp.sum(dpre, axis=0, keepdims=True)
        dh2_ref[...] = dh3 + _nt(dpre_b, wg_ref[...])
        dwg_ref[...] += _nn(h2b.T, dpre_b)
        dwpe_ref[...] += _nn(pb.T, (dh3 * gate).astype(BF16))

    row = lambda n: pl.BlockSpec((tm, n), lambda i: (i, 0))
    once = lambda shape: pl.BlockSpec(shape, lambda i: (0,) * len(shape), pipeline_mode=pl.Buffered(1))
    return pl.pallas_call(
        body,
        name="ple_loss",
        grid=(T // tm,),
        in_specs=[row(D), row(PLE), row(D), _full((D, D)), _full((1, D)), _full((PLE, D))],
        out_specs=[row(D), _full((1, D)), _full((8, 128)), once((D, D)), once((PLE, D))],
        out_shape=[jax.ShapeDtypeStruct((T, D), F32), jax.ShapeDtypeStruct((1, D), F32),
                   jax.ShapeDtypeStruct((8, 128), F32), jax.ShapeDtypeStruct((D, D), F32),
                   jax.ShapeDtypeStruct((PLE, D), F32)],
        compiler_params=_params(1),
    )(h2, p, tgt, wg, bg, wpe)


def _ffn_bwd(dh2, ff, h1, f1, w1g, w2g, gpost, gpre):
    T = dh2.shape[0]
    tm = _tile(T, 512)
    nb, hb = w1g.shape[0], w1g.shape[2]

    def body(dh2_ref, ff_ref, h1_ref, f1_ref, w1_ref, w2_ref, gpost_ref, gpre_ref,
             dffb_ref, df1_ref, dh1_ref, dgpost_ref, dgpre_ref):
        @pl.when(pl.program_id(0) == 0)
        def _():
            dgpost_ref[...] = jnp.zeros_like(dgpost_ref)
            dgpre_ref[...] = jnp.zeros_like(dgpre_ref)

        dh2 = dh2_ref[...]
        dff, dg = _rms_bwd(dh2, ff_ref[...], gpost_ref[...])
        dffb = dff.astype(BF16)
        dffb_ref[...] = dffb
        dgpost_ref[...] += dg
        dc2 = jnp.zeros((tm, D), F32)
        for j in range(nb):
            cols = slice(j * hb, (j + 1) * hb)
            dact = _nt(dffb, w2_ref[j])
            df1 = (dact * (2.0 * jnp.maximum(f1_ref[:, cols].astype(F32), 0.0))).astype(BF16)
            df1_ref[:, cols] = df1
            dc2 = dc2 + _nt(df1, w1_ref[j])
        dx, dg = _rms_bwd(dc2, h1_ref[...], gpre_ref[...])
        dh1_ref[...] = dh2 + dx
        dgpre_ref[...] += dg

    row = lambda n: pl.BlockSpec((tm, n), lambda i: (i, 0))
    once = lambda shape: pl.BlockSpec(shape, lambda i: (0,) * len(shape), pipeline_mode=pl.Buffered(1))
    return pl.pallas_call(
        body,
        name="ffn_bwd",
        grid=(T // tm,),
        in_specs=[row(D), row(D), row(D), row(DFF), once((nb, D, hb)), once((nb, hb, D)), _full((1, D)), _full((1, D))],
        out_specs=[row(D), row(DFF), row(D), _full((1, D)), _full((1, D))],
        out_shape=[jax.ShapeDtypeStruct((T, D), BF16), jax.ShapeDtypeStruct((T, DFF), BF16),
                   jax.ShapeDtypeStruct((T, D), F32), jax.ShapeDtypeStruct((1, D), F32),
                   jax.ShapeDtypeStruct((1, D), F32)],
        compiler_params=_params(1, VMEM_LIMIT_BIG),
    )(dh2, ff, h1, f1, w1g, w2g, gpost, gpre)


def _mix_bwd(dh1, o, us, yatt, qkv, lsec, yb, lng, lnb, sgw, sgwt, sgbt, gatt, gsg, wout, gpm):
    T = dh1.shape[0]
    tm = _tile(T, 512)

    def body(dh1_ref, o_ref, us_ref, ya_ref, q_ref, l_ref, yb_ref, lng_ref, lnb_ref, w_ref, wt_ref, bt_ref, ga_ref, gs_ref,
             wo_ref, gp_ref, qw_ref, dow_ref, dus_ref, dwo_ref, dw_ref, dbt_ref, dlng_ref, dlnb_ref, dga_ref, dgs_ref,
             dgp_ref, mixed_s, vnb_s, dvn_s):
        @pl.when(pl.program_id(0) == 0)
        def _():
            for r in (dwo_ref, dw_ref, dbt_ref, dlng_ref, dlnb_ref, dga_ref, dgs_ref, dgp_ref):
                r[...] = jnp.zeros_like(r)

        is_a = lax.broadcasted_iota(jnp.int32, (1, 128), 1) < HD
        lane = lax.broadcasted_iota(jnp.int32, (1, 128), 1)
        do, dg = _rms_bwd(dh1_ref[...], o_ref[...], gp_ref[...])
        dgp_ref[...] += dg
        dob = do.astype(BF16)
        dwo_ref[...] += _nn(yb_ref[...].T, dob)
        dy = _nt(dob, wo_ref[...])
        ya = ya_ref[...]
        datt, dg = _rms_bwd(dy[:, 0:AW], ya, ga_ref[...])
        dga_ref[...] += dg
        _attn_operands(q_ref[...], datt, ya, l_ref[...], qw_ref, dow_ref)

        lng = lng_ref[...]
        u, vs, ug, tu, tv, vhat, rstd, mixed, ysg = _sgu_forward(us_ref, lng, lnb_ref[...], w_ref, bt_ref, mixed_s, vnb_s, tm)
        dysg, dg = _rms_bwd(dy[:, AW:D], ysg, gs_ref[...])
        dgs_ref[...] += dg
        dus_ref[:, 0:SW] = ((dysg * mixed) * _gelu_grad(u, tu)).astype(BF16)
        dmix = dysg * ug

        rr = lax.broadcasted_iota(jnp.int32, (CH, CH), 0)
        cc = lax.broadcasted_iota(jnp.int32, (CH, CH), 1)
        tril = cc <= rr
        triu = cc >= rr
        for jj in range(4):
            wta = jnp.where(triu, wt_ref[2 * jj], 0.0).astype(BF16)
            wtb = jnp.where(triu, wt_ref[2 * jj + 1], 0.0).astype(BF16)
            for ch in range(tm // CH):
                rs, cs = slice(ch * CH, (ch + 1) * CH), slice(jj * 128, (jj + 1) * 128)
                dm2 = dmix[rs, cs]
                dma = jnp.where(is_a, dm2, 0.0)
                dmb = jnp.where(is_a, 0.0, dm2)
                dma_b, dmb_b = dma.astype(BF16), dmb.astype(BF16)
                vn2 = vnb_s[rs, cs]
                dw_ref[2 * jj] += jnp.where(tril, _nt(dma_b, vn2), 0.0)
                dw_ref[2 * jj + 1] += jnp.where(tril, _nt(dmb_b, vn2), 0.0)
                dvn_s[rs, cs] = _nn(wta, dma_b) + _nn(wtb, dmb_b)
                dba = jnp.sum(dma, axis=1, keepdims=True)
                dbb = jnp.sum(dmb, axis=1, keepdims=True)
                dbt_ref[...] += jnp.where(lane == 2 * jj, dba, 0.0) + jnp.where(lane == 2 * jj + 1, dbb, 0.0)

        dvn = dvn_s[...]
        dlng_ref[...] += jnp.sum(dvn * vhat, axis=0, keepdims=True)
        dlnb_ref[...] += jnp.sum(dvn, axis=0, keepdims=True)
        dvh = dvn * lng
        dvg = rstd * (dvh - jnp.mean(dvh, axis=-1, keepdims=True) - vhat * jnp.mean(dvh * vhat, axis=-1, keepdims=True))
        dus_ref[:, SW:NUS] = (dvg * _gelu_grad(vs, tv)).astype(BF16)

    row = lambda n: pl.BlockSpec((tm, n), lambda i: (i, 0))
    return pl.pallas_call(
        body,
        name="mix_bwd",
        grid=(T // tm,),
        in_specs=[row(D), row(D), row(NUS), row(AW), row(AW), row(AW), row(D), _full((1, SW)), _full((1, SW)),
                  _full((8, CH, CH)), _full((8, CH, CH)), _full((CH, 8)), _full((1, AW)), _full((1, SW)), _full((D, D)),
                  _full((1, D))],
        out_specs=[row(4 * AW), row(4 * AW), row(NUS),
                   pl.BlockSpec((D, D), lambda i: (0, 0), pipeline_mode=pl.Buffered(1)), _full((8, CH, CH)),
                   _full((CH, 128)), _full((1, SW)), _full((1, SW)), _full((1, AW)), _full((1, SW)), _full((1, D))],
        out_shape=[jax.ShapeDtypeStruct((T, 4 * AW), BF16),
                   jax.ShapeDtypeStruct((T, 4 * AW), BF16), jax.ShapeDtypeStruct((T, NUS), BF16),
                   jax.ShapeDtypeStruct((D, D), F32),
                   jax.ShapeDtypeStruct((8, CH, CH), F32), jax.ShapeDtypeStruct((CH, 128), F32),
                   jax.ShapeDtypeStruct((1, SW), F32), jax.ShapeDtypeStruct((1, SW), F32),
                   jax.ShapeDtypeStruct((1, AW), F32), jax.ShapeDtypeStruct((1, SW), F32), jax.ShapeDtypeStruct((1, D), F32)],
        scratch_shapes=[pltpu.VMEM((tm, SW), F32), pltpu.VMEM((tm, SW), BF16), pltpu.VMEM((tm, SW), F32)],
        compiler_params=_params(1, VMEM_LIMIT_BIG),
    )(dh1, o, us, yatt, qkv, lsec, yb, lng, lnb, sgw, sgwt, sgbt, gatt, gsg, wout, gpm)


def _attn_operands(q, do, o, lsec, qw_ref, dow_ref):
    rows = do.shape[0]
    feat = lax.broadcasted_iota(jnp.int32, (AW, 128), 0)
    head = lax.broadcasted_iota(jnp.int32, (AW, 128), 1)
    sel = jnp.where((feat >= head * HD) & (feat < (head + 1) * HD), 1.0, 0.0)
    delta = jnp.dot(do * o, sel, precision=lax.Precision.HIGHEST, preferred_element_type=F32)
    for hp in range(4):
        cols = slice(hp * 128, (hp + 1) * 128)
        for hh in range(2):
            base = (2 * hp + hh) * 256
            lc = lsec[:, hp * 128 + hh:hp * 128 + hh + 1]
            d_h = delta[:, 2 * hp + hh:2 * hp + hh + 1]
            qw_ref[:, base:base + 128] = _head_rows(q[:, cols], hh, 0.125)
            qw_ref[:, base + 128:base + 256] = _aug_lanes(rows, AUG_A if hh == 0 else AUG_B, _split3(-lc))
            dow_ref[:, base:base + 128] = _head_rows(do[:, cols], hh, 1.0).astype(BF16)
            dow_ref[:, base + 128:base + 256] = _aug_lanes(rows, None, _split3(-d_h))


def _attn_bwd(kk, vk, kt, qw, dow, after):
    T = kk.shape[0]
    tq = _tile(T, 512)
    tk = tq
    nq = T // tq

    def body(kk_ref, vk_ref, kt_ref, qw_ref, dow_ref, after_ref, dqt_ref, dcq_ref, dk_ref, dv_ref, dck_ref, dk_s, dv_s, dck_s):
        j, i = _fold_cols(pl.program_id(1), pl.program_id(2), nq)
        sub8 = lax.broadcasted_iota(jnp.int32, (8, 1), 0)
        lane = lax.broadcasted_iota(jnp.int32, (1, 128), 1)

        @pl.when((pl.program_id(1) == 0) & (pl.program_id(2) == 0))
        def _():
            dqt_ref[...] = jnp.zeros_like(dqt_ref)
            dcq_ref[...] = jnp.zeros_like(dcq_ref)

        @pl.when(i == j)
        def _():
            dk_s[...] = jnp.zeros_like(dk_s)
            dv_s[...] = jnp.zeros_like(dv_s)
            dck_s[...] = jnp.zeros_like(dck_s)

        def step(masked):
            cols = pl.ds(pl.multiple_of(i * tq, tq), tq)
            sub = lax.broadcasted_iota(jnp.int32, (128, 1), 0)
            if masked:
                causal = lax.broadcasted_iota(jnp.int32, (tk, tq), 0) <= lax.broadcasted_iota(jnp.int32, (tk, tq), 1)

            def logits(n):
                pair, base = n // 2, n * 256
                return (_nt(kk_ref[:, pair * 256:(pair + 1) * 256], qw_ref[:, base:base + 256]),
                        _nt(vk_ref[:, pair * 256:(pair + 1) * 256], dow_ref[:, base:base + 256]))

            ahead = logits(0)
            dcq = jnp.zeros((8, tq), F32)
            dck = jnp.zeros((tk, 128), F32)
            for pp in range(PAIRS):
                lanes = slice(pp * 128, (pp + 1) * 128)
                kt2 = kt_ref[lanes, :] * 0.125
                dv = jnp.zeros((tk, 128), F32)
                dk = jnp.zeros((tk, 128), F32)
                dqts = []
                for hh in range(2):
                    head = 2 * pp + hh
                    base = head * 256
                    qw_h = qw_ref[:, base:base + 256]
                    dow_h = dow_ref[:, base:base + 256]
                    logp, dp = ahead
                    if head + 1 < 2 * PAIRS:
                        ahead = logits(head + 1)
                    pr = jnp.exp(logp)
                    if masked:
                        pr = jnp.where(causal, pr, 0.0)
                    ds = pr * dp
                    ds_b = ds.astype(BF16)
                    dv = dv + _nn(pr.astype(BF16), dow_h[:, 0:128])
                    dk = dk + _nn(ds_b, qw_h[:, 0:128])
                    dqts.append(_nn(kt2, ds_b))
                    dcq = dcq + jnp.where(sub8 == head, jnp.sum(ds, axis=0, keepdims=True), 0.0)
                    dck = dck - jnp.where(lane == head, jnp.sum(ds, axis=1, keepdims=True), 0.0)
                dv_s[:, lanes] += dv
                dk_s[:, lanes] += dk
                dqt_ref[lanes, cols] += jnp.where(sub < HD, dqts[0], dqts[1])
            dck_s[...] += dck
            dcq_ref[:, cols] += dcq

        @pl.when(i > j)
        def _():
            step(False)

        @pl.when(i == j)
        def _():
            step(True)

        @pl.when(i == nq - 1)
        def _():
            dk_ref[...] = dk_s[...].astype(BF16)
            dv_ref[...] = dv_s[...].astype(BF16)
            dck_ref[...] = dck_s[...]

    kj = lambda r, t: _fold_cols(r, t, nq)[0]
    qi = lambda r, t: _fold_cols(r, t, nq)[1]
    krow = lambda g, r, t: (kj(r, t), g)
    qrow = lambda g, r, t: (qi(r, t), g)
    return pl.pallas_call(
        body,
        name="attn_bwd",
        grid=_fold_grid(nq),
        in_specs=[
            pl.BlockSpec((tk, PAIRS * 256), krow),
            pl.BlockSpec((tk, PAIRS * 256), krow),
            pl.BlockSpec((PAIRS * 128, tk), lambda g, r, t: (g, kj(r, t))),
            pl.BlockSpec((tq, PAIRS * 512), qrow),
            pl.BlockSpec((tq, PAIRS * 512), qrow),
            pl.BlockSpec(after.shape, lambda g, r, t: (0,) * after.ndim),
        ],
        out_specs=[
            pl.BlockSpec((PAIRS * 128, T), lambda g, r, t: (g, 0), pipeline_mode=pl.Buffered(1)),
            pl.BlockSpec((8, T), lambda g, r, t: (0, 0), pipeline_mode=pl.Buffered(1)),
            pl.BlockSpec((tk, PAIRS * 128), krow),
            pl.BlockSpec((tk, PAIRS * 128), krow),
            pl.BlockSpec((tk, 128), lambda g, r, t: (kj(r, t), 0)),
        ],
        out_shape=[jax.ShapeDtypeStruct((AW, T), F32), jax.ShapeDtypeStruct((8, T), F32),
                   jax.ShapeDtypeStruct((T, AW), BF16), jax.ShapeDtypeStruct((T, AW), BF16),
                   jax.ShapeDtypeStruct((T, FPAD), F32)],
        scratch_shapes=[pltpu.VMEM((tk, PAIRS * 128), F32), pltpu.VMEM((tk, PAIRS * 128), F32),
                        pltpu.VMEM((tk, 128), F32)],
        compiler_params=_params(3),
    )(kk, vk, kt, qw, dow, after)


def _fgate_bwd(dcq, dck, fz, fb):
    T = dck.shape[0]
    tb = _tile(T, 512)
    nb = T // tb

    def body(dcq_ref, dck_ref, fz_ref, fb_ref, df_ref, dfb_ref, carry):
        @pl.when(pl.program_id(0) == 0)
        def _():
            carry[...] = jnp.zeros_like(carry)
            dfb_ref[...] = jnp.zeros_like(dfb_ref)

        head = lax.broadcasted_iota(jnp.int32, (8, FPAD), 0)
        eye = jnp.where(head == lax.broadcasted_iota(jnp.int32, (8, FPAD), 1), 1.0, 0.0)
        dcv = dck_ref[...] + lax.dot_general(dcq_ref[...], eye, (((0,), (0,)), ((), ())),
                                             precision=lax.Precision.HIGHEST, preferred_element_type=F32)
        r = lax.broadcasted_iota(jnp.int32, (128, 128), 0)
        cc = lax.broadcasted_iota(jnp.int32, (128, 128), 1)
        tri = (cc >= r).astype(F32)
        run = carry[...]
        parts = []
        for blk in reversed(range(tb // 128)):
            dc_b = dcv[blk * 128:(blk + 1) * 128, :]
            parts.append(jnp.dot(tri, dc_b, precision=lax.Precision.HIGHEST, preferred_element_type=F32) + run)
            run = run + jnp.sum(dc_b, axis=0, keepdims=True)
        carry[...] = run
        dlf = jnp.concatenate(parts[::-1], axis=0)
        lane = lax.broadcasted_iota(jnp.int32, (tb, FPAD), 1)
        df = jnp.where(lane < 8, dlf * jax.nn.sigmoid(-(fz_ref[...] + fb_ref[...])), 0.0)
        df_ref[...] = df.astype(BF16)
        dfb_ref[...] += jnp.sum(df, axis=0, keepdims=True)

    rev = pl.BlockSpec((tb, FPAD), lambda i: (nb - 1 - i, 0))
    return pl.pallas_call(
        body,
        name="fgate_bwd",
        grid=(nb,),
        in_specs=[pl.BlockSpec((8, tb), lambda i: (0, nb - 1 - i)), rev, rev, _full((1, FPAD))],
        out_specs=[rev, _full((1, FPAD))],
        out_shape=[jax.ShapeDtypeStruct((T, FPAD), BF16), jax.ShapeDtypeStruct((1, FPAD), F32)],
        scratch_shapes=[pltpu.VMEM((1, FPAD), F32)],
        compiler_params=_params(1),
    )(dcq, dck, fz, fb)


def _inproj_bwd(dqt, dk, dv, dus, dfz, wtp, x, dh1, g):
    T = x.shape[0]
    tm = _tile(T, 512)

    def body(dq_ref, dk_ref, dv_ref, dus_ref, dfz_ref, w_ref, x_ref, dh1_ref, g_ref, gx_ref, dg_ref):
        @pl.when(pl.program_id(0) == 0)
        def _():
            dg_ref[...] = jnp.zeros_like(dg_ref)

        da = _tn(dq_ref[...].astype(BF16), w_ref[0:AW, :])
        da += _nn(dk_ref[...], w_ref[AW:2 * AW, :])
        da += _nn(dv_ref[...], w_ref[2 * AW:NQKV, :])
        da += _nn(dus_ref[...], w_ref[NQKV:NQKV + NUS, :])
        da += _nn(dfz_ref[...], w_ref[NQKV + NUS:ZP, :])
        dx, dg = _rms_bwd(da, x_ref[...], g_ref[...])
        gx_ref[...] = dh1_ref[...] + dx
        dg_ref[...] += dg

    row = lambda n: pl.BlockSpec((tm, n), lambda i: (i, 0))
    return pl.pallas_call(
        body,
        name="inproj_bwd",
        grid=(T // tm,),
        in_specs=[pl.BlockSpec((AW, tm), lambda i: (0, i)), row(AW), row(AW), row(NUS), row(FPAD), _full((ZP, D)),
                  row(D), row(D), _full((1, D))],
        out_specs=[row(D), _full((1, D))],
        out_shape=[jax.ShapeDtypeStruct((T, D), F32), jax.ShapeDtypeStruct((1, D), F32)],
        compiler_params=_params(1),
    )(dqt, dk, dv, dus, dfz, wtp, x, dh1, g)


def _sq_relu(f1):
    r = jnp.maximum(f1.astype(F32), 0.0)
    return (r * r).astype(BF16)


def _wgrad(a, bs, name, a_fn=None, out_dtype=F32):
    T, K = a.shape
    tt = _tile(T, 512)
    nb = len(bs)
    narrow = out_dtype != F32

    def out_dims(b, layout):
        if layout == "t":
            return (K, b.shape[0])
        N = b.shape[1]
        if layout is None:
            return (K, N)
        return (N // layout[1], K, layout[1]) if layout[0] == "col" else (K // layout[1], layout[1], N)

    shapes = [out_dims(b, layout) for b, layout in bs]

    def body(*refs):
        a_ref, b_refs, o_refs = refs[0], refs[1:1 + nb], refs[1 + nb:1 + 2 * nb]
        accs = refs[1 + 2 * nb:] if narrow else o_refs

        @pl.when(pl.program_id(0) == 0)
        def _():
            for acc in accs:
                acc[...] = jnp.zeros_like(acc)

        av = a_ref[...] if a_fn is None else a_fn(a_ref[...])
        at = av.astype(BF16).T
        for (b, layout), b_ref, o_ref in zip(bs, b_refs, accs):
            if layout is None:
                o_ref[...] += _nn(at, b_ref[...].astype(BF16))
            elif layout == "t":
                o_ref[...] += _nt(at, b_ref[...].astype(BF16))
            elif layout[0] == "col":
                n = layout[1]
                for k in range(b.shape[1] // n):
                    o_ref[k] += _nn(at, b_ref[:, k * n:(k + 1) * n].astype(BF16))
            else:
                n = layout[1]
                bv = b_ref[...].astype(BF16)
                for k in range(K // n):
                    o_ref[k] += _nn(at[k * n:(k + 1) * n, :], bv)

        if narrow:
            @pl.when(pl.program_id(0) == T // tt - 1)
            def _():
                for o_ref, acc in zip(o_refs, accs):
                    o_ref[...] = acc[...].astype(out_dtype)

    once = lambda shape: pl.BlockSpec(shape, lambda t: (0,) * len(shape), pipeline_mode=pl.Buffered(1))
    res = pl.pallas_call(
        body,
        name=name,
        grid=(T // tt,),
        in_specs=[pl.BlockSpec((tt, K), lambda t: (t, 0))] + [
            pl.BlockSpec((b.shape[0], tt), lambda t: (0, t)) if layout == "t" else pl.BlockSpec((tt, b.shape[1]), lambda t: (t, 0))
            for b, layout in bs],
        out_specs=[once(s) for s in shapes],
        out_shape=[jax.ShapeDtypeStruct(s, out_dtype) for s in shapes],
        scratch_shapes=[pltpu.VMEM(s, F32) for s in shapes] if narrow else [],
        compiler_params=_params(1, VMEM_LIMIT_BIG),
    )(a, *[b for b, _ in bs])
    return res


def _adam_math(w, g, m, v):
    m = ADAM_B1 * m + (1.0 - ADAM_B1) * g
    v = ADAM_B2 * v + (1.0 - ADAM_B2) * (g * g)
    m_hat = m / (1.0 - ADAM_B1 ** ADAM_STEP)
    v_hat = v / (1.0 - ADAM_B2 ** ADAM_STEP)
    delta = -ADAM_LR * (m_hat / (jnp.sqrt(v_hat) + ADAM_EPS) + ADAM_WD * w)
    return delta, m, v


def _adam(parts, w, m, v, name):
    R, C = w.shape
    br = 128 if R % 128 == 0 else R

    def body(p_ref, w_ref, m_ref, v_ref, g_ref, d_ref, nm_ref, nv_ref):
        g = p_ref[0].astype(F32)
        for s in range(1, NDEV):
            g = g + p_ref[s].astype(F32)
        g_ref[...] = g
        d_ref[...], nm_ref[...], nv_ref[...] = _adam_math(w_ref[...], g, m_ref[...], v_ref[...])

    blk = pl.BlockSpec((br, C), lambda i: (i, 0))
    return pl.pallas_call(
        body,
        name=name,
        grid=(R // br,),
        in_specs=[pl.BlockSpec((NDEV, br, C), lambda i: (0, i, 0)), blk, blk, blk],
        out_specs=[blk] * 4,
        out_shape=[jax.ShapeDtypeStruct((R, C), F32)] * 4,
        compiler_params=_params(1),
    )(parts, w, m, v)


_SMALL = (("sg_w", 8 * CH * CH), ("f_bias", 8), ("sg_ln_g", SW), ("sg_ln_b", SW), ("sg_b", 8 * CH), ("att_out_g", AW),
          ("sg_out_g", SW), ("pre_mix_g", D), ("post_mix_g", D), ("pre_ffn_g", D), ("post_ffn_g", D), ("ple_gate_b", D))
_SEG = 8 * 128


def _seg_rows(size):
    return 8 * (-(-size // _SEG))


def _pack(vals, loss_acc):
    parts = []
    for name, size in _SMALL:
        flat = vals[name].reshape(-1)
        rows = _seg_rows(size)
        parts.append(jnp.pad(flat, (0, rows * 128 - size)).reshape(rows, 128))
    parts.append(loss_acc)
    return jnp.concatenate(parts, axis=0)


def _adam_small(parts, ws, ms, vs):
    n = len(_SMALL)
    names = [name for name, _ in _SMALL]

    def body(*refs):
        p_ref, w_refs, m_refs, v_refs = refs[0], refs[1:1 + n], refs[1 + n:1 + 2 * n], refs[1 + 2 * n:1 + 3 * n]
        loss_ref, outs = refs[1 + 3 * n], refs[2 + 3 * n:]
        g_all = p_ref[0]
        for s in range(1, NDEV):
            g_all = g_all + p_ref[s]
        r = 0
        for k, (name, size) in enumerate(_SMALL):
            dst = [outs[kind * n + k] for kind in range(4)]

            def update(g, idx):
                vals = (g,) + _adam_math(w_refs[k][idx], g, m_refs[k][idx], v_refs[k][idx])
                for d, val in zip(dst, vals):
                    d[idx] = val

            if name == "sg_w":
                for grp in range(8):
                    update(g_all[r + grp * CH:r + (grp + 1) * CH, :], (0, grp))
            elif name == "sg_b":
                update(g_all[r:r + 8, :], (0,))
            elif name == "f_bias":
                update(g_all[r:r + 1, 0:8], (slice(None),))
            else:
                update(jnp.concatenate([g_all[r + q:r + q + 1, :] for q in range(size // 128)], axis=1), (slice(None),))
            r += _seg_rows(size)
        loss_ref[...] = g_all[r:r + 1, 0:1] * (0.5 / D)

    arrs = [parts] + [d[name] for d in (ws, ms, vs) for name in names]
    res = pl.pallas_call(
        body,
        name="adam_small",
        in_specs=[_full(a.shape) for a in arrs],
        out_specs=[_full((1, 1))] + [_full(ws[name].shape) for _ in range(4) for name in names],
        out_shape=[jax.ShapeDtypeStruct((1, 1), F32)] + [jax.ShapeDtypeStruct(ws[name].shape, F32) for _ in range(4) for name in names],
        compiler_params=pltpu.CompilerParams(vmem_limit_bytes=VMEM_LIMIT),
    )(*arrs)
    return res[0], {name: [res[1 + kind * n + k] for kind in range(4)] for k, name in enumerate(names)}


def kernel(x, p, w_in, f_bias, sg_ln_g, sg_ln_b, sg_w, sg_b, att_out_g, sg_out_g, w_out, pre_mix_g, post_mix_g, pre_ffn_g, post_ffn_g, w_ff1, w_ff2, ple_w, ple_gate_w, ple_gate_b, loss_target, m_w_in, m_f_bias, m_sg_ln_g, m_sg_ln_b, m_sg_w, m_sg_b, m_att_out_g, m_sg_out_g, m_w_out, m_pre_mix_g, m_post_mix_g, m_pre_ffn_g, m_post_ffn_g, m_w_ff1, m_w_ff2, m_ple_w, m_ple_gate_w, m_ple_gate_b, v_w_in, v_f_bias, v_sg_ln_g, v_sg_ln_b, v_sg_w, v_sg_b, v_att_out_g, v_sg_out_g, v_w_out, v_pre_mix_g, v_post_mix_g, v_pre_ffn_g, v_post_ffn_g, v_w_ff1, v_w_ff2, v_ple_w, v_ple_gate_w, v_ple_gate_b):
    small_w = dict(sg_w=sg_w, f_bias=f_bias, sg_ln_g=sg_ln_g, sg_ln_b=sg_ln_b, sg_b=sg_b, att_out_g=att_out_g,
                   sg_out_g=sg_out_g, pre_mix_g=pre_mix_g, post_mix_g=post_mix_g, pre_ffn_g=pre_ffn_g,
                   post_ffn_g=post_ffn_g, ple_gate_b=ple_gate_b)
    small_m = dict(sg_w=m_sg_w, f_bias=m_f_bias, sg_ln_g=m_sg_ln_g, sg_ln_b=m_sg_ln_b, sg_b=m_sg_b, att_out_g=m_att_out_g,
                   sg_out_g=m_sg_out_g, pre_mix_g=m_pre_mix_g, post_mix_g=m_post_mix_g, pre_ffn_g=m_pre_ffn_g,
                   post_ffn_g=m_post_ffn_g, ple_gate_b=m_ple_gate_b)
    small_v = dict(sg_w=v_sg_w, f_bias=v_f_bias, sg_ln_g=v_sg_ln_g, sg_ln_b=v_sg_ln_b, sg_b=v_sg_b, att_out_g=v_att_out_g,
                   sg_out_g=v_sg_out_g, pre_mix_g=v_pre_mix_g, post_mix_g=v_post_mix_g, pre_ffn_g=v_pre_ffn_g,
                   post_ffn_g=v_post_ffn_g, ple_gate_b=v_ple_gate_b)
    big = dict(w_in=(w_in, m_w_in, v_w_in), w_out=(w_out, m_w_out, v_w_out), w_ff1=(w_ff1, m_w_ff1, v_w_ff1),
               w_ff2=(w_ff2, m_w_ff2, v_w_ff2), ple_w=(ple_w, m_ple_w, v_ple_w),
               ple_gate_w=(ple_gate_w, m_ple_gate_w, v_ple_gate_w))

    xt, pt, tgt = x[0], p[0, 0], loss_target[0]
    ws = W_IN_COLS // NDEV

    gw_in = _gather(jnp.pad(jnp.transpose(w_in[0]).astype(BF16), ((0, WS_PAD - ws), (0, 0))), "gather_w_in")
    gw_in, others = lax.optimization_barrier((gw_in, [w_out[0].astype(BF16), w_ff1[0].astype(BF16), w_ff2[0].astype(BF16),
                                                      ple_w[0].astype(BF16), ple_gate_w[0].astype(BF16)]))
    rest = _xchg_start(others, True, "gather_rest_start")
    wint = gw_in[:, 0:ws, :].reshape(W_IN_COLS, D)
    wtp = jnp.concatenate([wint[0:NQKV], wint[NQKV + 8:W_IN_COLS], wint[NQKV:NQKV + 8],
                           jnp.zeros((FPAD - 8, D), BF16)], axis=0)

    fb = jnp.pad(f_bias.astype(F32), ((0, 0), (0, FPAD - 8)))
    sgw = sg_w[0]
    sgwt = jnp.transpose(sg_w[0], (0, 2, 1))
    sgbt = jnp.transpose(sg_b[0])

    qkv, kt, vt, us, fz, ab = _inproj(xt, pre_mix_g + rest[4][0, 0], wtp)
    ct, kk, vk = _fcum(fz, fb, qkv)
    yatt, lsec = _attn_fwd(qkv, kk, vt, ct)
    gw_out, gw1, gw2, gwpe, gwg = _xchg_wait(rest, True, yatt, "gather_rest_wait")
    wout = gw_out.reshape(D, D)
    wg = gwg.reshape(D, D)
    wpe = jnp.transpose(gwpe, (1, 0, 2)).reshape(PLE, D)
    h1, yb, o = _sgu_out(us, yatt, xt, sg_ln_g, sg_ln_b, sgw, sgbt, att_out_g, sg_out_g, wout, post_mix_g)
    c2b, f1b, ff, h2 = _ffn_fwd(h1, pre_ffn_g, gw1, gw2, post_ffn_g)
    dh2, dbg, loss_acc, g_g, g_pe = _ple_loss(h2, pt, tgt, wg, ple_gate_b, wpe)
    g_g = g_g.reshape(NDEV, D // NDEV, D)
    g_pe = jnp.transpose(g_pe.reshape(PLE, NDEV, D // NDEV), (1, 0, 2))

    dffb, df1, dh1, dgpostffn, dgpreffn = _ffn_bwd(dh2, ff, h1, f1b, gw1, gw2, post_ffn_g, pre_ffn_g)
    (g_1,) = _wgrad(c2b, [(df1, ("col", DFF // NDEV))], "wgrad_ff1")
    (g_2,) = _wgrad(f1b, [(dffb, ("row", DFF // NDEV))], "wgrad_ff2", a_fn=_sq_relu)
    early = _xchg_start([g_1, g_2, g_pe, g_g], False, "scatter_early_start")
    qw, dow, dus, g_out, dsgw, dsgbt, dlng, dlnb, dgatt, dgsg, dgpostmix = _mix_bwd(
        dh1, o, us, yatt, qkv, lsec, yb, sg_ln_g, sg_ln_b, sgw, sgwt, sgbt, att_out_g, sg_out_g, wout,
        post_mix_g + early[4][0, 0])
    mid = _xchg_start([g_out.reshape(NDEV, D // NDEV, D)], False, "scatter_mid_start")
    dqt, dcq, dk, dv, dck = _attn_bwd(kk, vk, kt, qw, dow, mid[4])
    dfz, dfb = _fgate_bwd(dcq, dck, fz, fb)

    gq, gk, gv, gus, gf = _wgrad(ab, [(dqt, "t"), (dk, None), (dv, None), (dus, None), (dfz, None)], "wgrad_in",
                                 out_dtype=BF16)
    g_in = jnp.concatenate([gq, gk, gv, gf[:, 0:8], gus], axis=1)
    g_in = jnp.transpose(g_in.reshape(D, NDEV, ws), (1, 0, 2))
    late = _xchg_start([g_in], False, "scatter_late_start")
    grad_x, dgpremix = _inproj_bwd(dqt, dk, dv, dus, dfz, wtp, xt, dh1, pre_mix_g + late[4][0, 0])

    small_g = dict(sg_w=dsgw, f_bias=dfb[:, 0:8], sg_ln_g=dlng, sg_ln_b=dlnb, sg_b=jnp.transpose(dsgbt[:, 0:8]),
                   att_out_g=dgatt, sg_out_g=dgsg, pre_mix_g=dgpremix, post_mix_g=dgpostmix, pre_ffn_g=dgpreffn,
                   post_ffn_g=dgpostffn, ple_gate_b=dbg)

    r_1, r_2, r_pe, r_g = _xchg_wait(early, False, grad_x, "scatter_early_wait")
    (r_out,) = _xchg_wait(mid, False, grad_x, "scatter_mid_wait")
    r_small = _gather(_pack(small_g, loss_acc), "gather_small_grads")

    res = {}

    def adam_big(name, parts):
        w, m, v = big[name]
        res[name] = [t[None] for t in _adam(parts, w[0], m[0], v[0], "adam_" + name)]

    for name, parts in (("w_out", r_out), ("w_ff1", r_1), ("w_ff2", r_2), ("ple_w", r_pe), ("ple_gate_w", r_g)):
        adam_big(name, parts)
    (r_in,) = _xchg_wait(late, False, res["w_ff1"][0], "scatter_late_wait")
    adam_big("w_in", r_in)
    loss, small = _adam_small(r_small, small_w, small_m, small_v)
    res.update(small)

    order = ["w_in", "f_bias", "sg_ln_g", "sg_ln_b", "sg_w", "sg_b", "att_out_g", "sg_out_g", "w_out", "pre_mix_g",
             "post_mix_g", "pre_ffn_g", "post_ffn_g", "w_ff1", "w_ff2", "ple_w", "ple_gate_w", "ple_gate_b"]
    outs = [loss[0, 0], grad_x[None]]
    for kind in range(4):
        outs += [res[name][kind] for name in order]
    return tuple(outs)
```

```python
import jax
import jax.numpy as jnp
from jax import lax
from jax.experimental import pallas as pl
from jax.experimental.pallas import tpu as pltpu

F32 = jnp.float32
BF16 = jnp.bfloat16

NDEV = 8
D = 1024
AW = 512
SW = 512
HD = 64
CH = 128
DFF = 4096
PLE = 256
NQKV = 3 * AW
NUS = 2 * SW
FPAD = 128
ZP = NQKV + NUS + FPAD
W_IN_COLS = 2568
WS_PAD = 336
EPS = 1e-6
MASKV = -1e30
GELU_K = 0.7978845608028654
GELU_C = 0.044715

ADAM_LR = 0.001
ADAM_B1 = 0.9
ADAM_B2 = 0.999
ADAM_EPS = 1e-08
ADAM_WD = 0.01
ADAM_STEP = 10

VMEM_LIMIT = 48 * 1024 * 1024
VMEM_LIMIT_BIG = 60 * 1024 * 1024


def _nn(a, b):
    return jnp.dot(a, b, preferred_element_type=F32)


def _nt(a, b):
    return lax.dot_general(a, b, (((1,), (1,)), ((), ())), preferred_element_type=F32)


def _tn(a, b):
    return lax.dot_general(a, b, (((0,), (0,)), ((), ())), preferred_element_type=F32)


def _tile(n, pref):
    return min(n, pref)


def _params(n_axes, vmem=VMEM_LIMIT):
    return pltpu.CompilerParams(dimension_semantics=("arbitrary",) * n_axes, vmem_limit_bytes=vmem)


def _full(shape):
    nd = len(shape)
    return pl.BlockSpec(shape, lambda *_: (0,) * nd)


def _rms_fwd(x, g):
    r = lax.rsqrt(jnp.mean(x * x, axis=-1, keepdims=True) + EPS)
    return x * r * g


def _rms_bwd(dy, x, g):
    n = x.shape[-1]
    r = lax.rsqrt(jnp.mean(x * x, axis=-1, keepdims=True) + EPS)
    u = dy * g
    s = jnp.sum(x * u, axis=-1, keepdims=True)
    dx = r * u - x * (r * r * r * (s * (1.0 / n)))
    dg = jnp.sum(dy * (x * r), axis=0, keepdims=True)
    return dx, dg


def _gelu(x):
    t = jnp.tanh(x * (GELU_K + (GELU_K * GELU_C) * (x * x)))
    return x * (0.5 + 0.5 * t), t


def _gelu_grad(x, t):
    return (0.5 + 0.5 * t) + (0.5 * x) * (1.0 - t * t) * (GELU_K + (3.0 * GELU_K * GELU_C) * (x * x))


def _gather(arr, name):
    def body(x_ref, out_ref, send, recv, loc):
        x, y, c = lax.axis_index("x"), lax.axis_index("y"), lax.axis_index("c")
        me, sibling = (x, y, c), (x, y, 1 - c)
        chips = [(1 - x, y), (x, 1 - y), (1 - x, 1 - y)]

        def slot(px, py, pc):
            return out_ref.at[4 * px + 2 * py + pc]

        def copy(k, block, to, src=None):
            return pltpu.make_async_remote_copy(
                src_ref=slot(*block) if src is None else src, dst_ref=slot(*block), send_sem=send.at[k],
                recv_sem=recv.at[k], device_id=to, device_id_type=pl.DeviceIdType.MESH)

        mine = pltpu.make_async_copy(x_ref, slot(*me), loc)
        mine.start()
        first = [copy(0, me, sibling, src=x_ref)] + [copy(1 + j, me, (*chip, c), src=x_ref) for j, chip in enumerate(chips)]
        for cp in first:
            cp.start()
        passed = [copy(4 + j, (*chip, c), sibling) for j, chip in enumerate(chips)]
        for j, chip in enumerate(chips):
            copy(1 + j, (*chip, c), me).wait_recv()
            passed[j].start()
        copy(0, sibling, me).wait_recv()
        for j, chip in enumerate(chips):
            copy(4 + j, (*chip, 1 - c), me).wait_recv()
        for cp in first + passed:
            cp.wait_send()
        mine.wait()

    hbm = pl.BlockSpec(memory_space=pltpu.HBM)
    return pl.pallas_call(
        body,
        name=name,
        out_shape=jax.ShapeDtypeStruct((NDEV,) + arr.shape, arr.dtype),
        in_specs=[hbm],
        out_specs=hbm,
        scratch_shapes=[pltpu.SemaphoreType.DMA((NDEV - 1,)), pltpu.SemaphoreType.DMA((NDEV - 1,)), pltpu.SemaphoreType.DMA],
    )(arr)


def _peers(x, y, c):
    out = []
    for k in range(1, NDEV):
        out.append((1 - x if (k >> 2) & 1 else x, 1 - y if (k >> 1) & 1 else y, 1 - c if k & 1 else c))
    return out


def _xchg_start(arrs, gather, name):
    n = len(arrs)
    me = 4 * lax.axis_index("x") + 2 * lax.axis_index("y") + lax.axis_index("c")
    lands = []
    for a in arrs:
        shape = ((NDEV,) + a.shape) if gather else a.shape
        own = a[None] if gather else lax.dynamic_slice_in_dim(a, me, 1, axis=0)
        lands.append(lax.dynamic_update_slice_in_dim(lax.empty(shape, a.dtype), own, me, axis=0))

    def body(*refs):
        ins, lnd = refs[:n], refs[n:2 * n]
        send, recv, token = refs[2 * n:3 * n], refs[3 * n:4 * n], refs[-1]
        x, y, c = lax.axis_index("x"), lax.axis_index("y"), lax.axis_index("c")
        mine = 4 * x + 2 * y + c
        for px, py, pc in _peers(x, y, c):
            peer = 4 * px + 2 * py + pc
            for a in range(n):
                pltpu.make_async_remote_copy(
                    src_ref=ins[a] if gather else ins[a].at[peer],
                    dst_ref=lnd[a].at[mine],
                    send_sem=send[a],
                    recv_sem=recv[a],
                    device_id=(px, py, pc),
                    device_id_type=pl.DeviceIdType.MESH,
                ).start()
        token[...] = jnp.zeros_like(token)

    hbm = pl.BlockSpec(memory_space=pltpu.HBM)
    sem = pl.BlockSpec(memory_space=pltpu.SEMAPHORE)
    res = pl.pallas_call(
        body,
        name=name,
        out_shape=(*[pltpu.SemaphoreType.DMA(())] * (2 * n),
                   *[pltpu.HBM(a.shape, a.dtype) for a in arrs], *[pltpu.HBM(l.shape, l.dtype) for l in lands],
                   jax.ShapeDtypeStruct((8, 128), F32)),
        in_specs=[hbm] * (2 * n),
        out_specs=(*([sem] * (2 * n)), *([hbm] * (2 * n)), pl.BlockSpec(memory_space=pltpu.VMEM)),
        input_output_aliases={i: 2 * n + i for i in range(2 * n)},
        compiler_params=pltpu.CompilerParams(has_side_effects=pltpu.SideEffectType.DATAFLOW_SIDE_EFFECTING),
    )(*[pltpu.with_memory_space_constraint(a, pltpu.HBM) for a in arrs],
      *[pltpu.with_memory_space_constraint(l, pltpu.HBM) for l in lands])
    return list(res[0:n]), list(res[n:2 * n]), list(res[2 * n:3 * n]), list(res[3 * n:4 * n]), res[-1]


def _xchg_wait(started, gather, after, name):
    send, recv, srcs, lands, _ = started
    n = len(srcs)

    def body(*refs):
        lnd = refs[n:2 * n]
        send, recv = refs[2 * n:3 * n], refs[3 * n:4 * n]
        me = (lax.axis_index("x"), lax.axis_index("y"), lax.axis_index("c"))
        for a in range(n):
            seven = lnd[a].at[pl.ds(0, NDEV - 1)]
            cp = pltpu.make_async_remote_copy(src_ref=seven, dst_ref=seven, send_sem=send[a], recv_sem=recv[a],
                                              device_id=me, device_id_type=pl.DeviceIdType.MESH)
            cp.wait_send()
            cp.wait_recv()

    hbm = pl.BlockSpec(memory_space=pltpu.HBM)
    sem = pl.BlockSpec(memory_space=pltpu.SEMAPHORE)
    res = pl.pallas_call(
        body,
        name=name,
        out_shape=tuple([pltpu.HBM(a.shape, a.dtype) for a in srcs] + [pltpu.HBM(l.shape, l.dtype) for l in lands]),
        in_specs=[hbm] * (2 * n) + [sem] * (2 * n) + [pl.BlockSpec(memory_space=pl.ANY)],
        out_specs=tuple([hbm] * (2 * n)),
        input_output_aliases={i: i for i in range(2 * n)},
        compiler_params=pltpu.CompilerParams(has_side_effects=pltpu.SideEffectType.DATAFLOW_SIDE_EFFECTING),
    )(*srcs, *lands, *send, *recv, after)
    return list(res[n:])


def _inproj(x, g, wtp):
    T = x.shape[0]
    tm = _tile(T, 512)

    def body(x_ref, g_ref, w_ref, qkv_ref, kt_ref, vt_ref, us_ref, fz_ref, ab_ref):
        a = _rms_fwd(x_ref[...], g_ref[...]).astype(BF16)
        ab_ref[...] = a
        qkv_ref[:, 0:AW] = _nt(a, w_ref[0:AW, :]).astype(BF16)
        kk = _nt(a, w_ref[AW:2 * AW, :])
        qkv_ref[:, AW:2 * AW] = kk.astype(BF16)
        kt_ref[...] = kk.T.astype(BF16)
        vv = _nt(a, w_ref[2 * AW:NQKV, :])
        qkv_ref[:, 2 * AW:NQKV] = vv.astype(BF16)
        vt_ref[...] = vv.T.astype(BF16)
        us_ref[...] = _nt(a, w_ref[NQKV:NQKV + NUS, :])
        fz_ref[...] = _nt(a, w_ref[NQKV + NUS:ZP, :])

    row = lambda n: pl.BlockSpec((tm, n), lambda i: (i, 0))
    col = pl.BlockSpec((AW, tm), lambda i: (0, i))
    return pl.pallas_call(
        body,
        name="inproj",
        grid=(T // tm,),
        in_specs=[row(D), _full((1, D)), _full((ZP, D))],
        out_specs=[row(NQKV), col, col, row(NUS), row(FPAD), row(D)],
        out_shape=[
            jax.ShapeDtypeStruct((T, NQKV), BF16),
            jax.ShapeDtypeStruct((AW, T), BF16),
            jax.ShapeDtypeStruct((AW, T), BF16),
            jax.ShapeDtypeStruct((T, NUS), F32),
            jax.ShapeDtypeStruct((T, FPAD), F32),
            jax.ShapeDtypeStruct((T, D), BF16),
        ],
        compiler_params=_params(1),
    )(x, g, wtp)


def _log_sigmoid(z):
    return jnp.minimum(z, 0.0) - jnp.log1p(jnp.exp(-jnp.abs(z)))


def _split3(x):
    hi = x.astype(BF16)
    r1 = x - hi.astype(F32)
    mid = r1.astype(BF16)
    lo = (r1 - mid.astype(F32)).astype(BF16)
    return hi, mid, lo


AUG_A, AUG_B, AUG_ONE = 0, 3, 6


def _aug_lanes(rows, first_one, pieces):
    lane = lax.broadcasted_iota(jnp.int32, (rows, 128), 1)
    out = jnp.zeros((rows, 128), F32)
    if first_one is not None:
        out = jnp.where((lane >= first_one) & (lane < first_one + 3), 1.0, out)
    for n, piece in enumerate(pieces):
        out = jnp.where(lane == AUG_ONE + n, piece.astype(F32), out)
    return out.astype(BF16)


def _fcum(fz, fb, qkv):
    T = fz.shape[0]
    tb = _tile(T, 512)

    def body(fz_ref, fb_ref, k_ref, v_ref, ct_ref, kk_ref, vk_ref, carry, tri_s, sel_s):
        @pl.when(pl.program_id(0) == 0)
        def _():
            carry[...] = jnp.zeros_like(carry)
            src = lax.broadcasted_iota(jnp.int32, (128, 128), 0)
            dst = lax.broadcasted_iota(jnp.int32, (128, 128), 1)
            tri_s[...] = (dst <= src).astype(F32)
            for hp in range(4):
                for n in range(3):
                    pick = ((src == 2 * hp) & (dst == AUG_A + n)) | ((src == 2 * hp + 1) & (dst == AUG_B + n))
                    sel_s[3 * hp + n] = jnp.where(pick, -1.0, 0.0).astype(BF16)

        lf = _log_sigmoid(fz_ref[...] + fb_ref[...])
        run = carry[...]
        parts = []
        for blk in range(tb // 128):
            lf_b = lf[blk * 128:(blk + 1) * 128, :]
            parts.append(jnp.dot(tri_s[...], lf_b, precision=lax.Precision.HIGHEST, preferred_element_type=F32) + run)
            run = run + jnp.sum(lf_b, axis=0, keepdims=True)
        carry[...] = run
        cs = jnp.concatenate(parts, axis=0)
        ct_ref[...] = cs.T[0:8, :]

        pieces = _split3(cs)
        lane = lax.broadcasted_iota(jnp.int32, (1, 128), 1)
        ones = jnp.where((lane >= AUG_ONE) & (lane < AUG_ONE + 3), 1.0, 0.0)
        for hp in range(4):
            aug = jnp.zeros((tb, 128), F32) + ones
            for n, piece in enumerate(pieces):
                aug = aug + _nn(piece, sel_s[3 * hp + n])
            aug = aug.astype(BF16)
            kk_ref[:, hp * 256:hp * 256 + 128] = k_ref[:, hp * 128:(hp + 1) * 128]
            kk_ref[:, hp * 256 + 128:(hp + 1) * 256] = aug
            vk_ref[:, hp * 256:hp * 256 + 128] = v_ref[:, hp * 128:(hp + 1) * 128]
            vk_ref[:, hp * 256 + 128:(hp + 1) * 256] = aug

    row = lambda n, c: pl.BlockSpec((tb, n), lambda i: (i, c))
    return pl.pallas_call(
        body,
        name="fcum",
        grid=(T // tb,),
        in_specs=[row(FPAD, 0), _full((1, FPAD)), row(AW, 1), row(AW, 2)],
        out_specs=[pl.BlockSpec((8, tb), lambda i: (0, i)), row(2 * AW, 0), row(2 * AW, 0)],
        out_shape=[jax.ShapeDtypeStruct((8, T), F32), jax.ShapeDtypeStruct((T, 2 * AW), BF16),
                   jax.ShapeDtypeStruct((T, 2 * AW), BF16)],
        scratch_shapes=[pltpu.VMEM((1, FPAD), F32), pltpu.VMEM((128, 128), F32), pltpu.VMEM((12, 128, 128), BF16)],
        compiler_params=_params(1),
    )(fz, fb, qkv, qkv)


def _fold_rows(r, t, nq):
    if nq == 1:
        return r, t
    low = t <= r
    return jnp.where(low, r, nq - 1 - r), jnp.where(low, t, t - r - 1)


def _fold_cols(r, t, nq):
    if nq == 1:
        return r, t
    first = t < nq - r
    j = jnp.where(first, r, nq - 1 - r)
    return j, jnp.where(first, r + t, nq - 1 - r + (t - (nq - r)))


PAIRS = 4


def _fold_grid(nq):
    assert (nq == 1 or nq % 2 == 0) and PAIRS == 4
    return (4 // PAIRS, 1, 1) if nq == 1 else (4 // PAIRS, nq // 2, nq + 1)


def _head_rows(x2, hh, scale):
    is_a = lax.broadcasted_iota(jnp.int32, (1, 128), 1) < HD
    keep = is_a if hh == 0 else jnp.logical_not(is_a)
    return jnp.where(keep, x2, jnp.zeros_like(x2)) * scale


def _attn_fwd(qkv, kk, vt, ct):
    T = qkv.shape[0]
    tq = _tile(T, 512)
    tk = tq
    nq = T // tq

    def body(q_ref, kk_ref, vt_ref, ctq_ref, o_ref, lsec_ref, qw_s, m_s, l_s, acc_s):
        i, j = _fold_rows(pl.program_id(1), pl.program_id(2), nq)
        sub8 = lax.broadcasted_iota(jnp.int32, (8, 1), 0)

        def cref_of(pp, hh):
            head = 2 * (PAIRS * pl.program_id(0) + pp) + hh
            return jnp.sum(jnp.where(sub8 == head, ctq_ref[:, 0:1], 0.0), axis=0, keepdims=True)

        @pl.when(j == 0)
        def _():
            for pp in range(PAIRS):
                q2 = q_ref[:, pp * 128:(pp + 1) * 128]
                for hh in range(2):
                    rows = slice(hh * tq, (hh + 1) * tq)
                    qw_s[pp, rows, 0:128] = _head_rows(q2, hh, 0.125)
                    qw_s[pp, rows, 128:256] = _aug_lanes(tq, AUG_A if hh == 0 else AUG_B, _split3(cref_of(pp, hh)))
            m_s[...] = jnp.full_like(m_s, MASKV)
            l_s[...] = jnp.zeros_like(l_s)
            acc_s[...] = jnp.zeros_like(acc_s)

        def step(masked):
            if masked:
                causal = lax.broadcasted_iota(jnp.int32, (tk, tq), 0) <= lax.broadcasted_iota(jnp.int32, (tk, tq), 1)
            logits = lambda pp: _nt(kk_ref[:, pp * 256:(pp + 1) * 256], qw_s[pp])
            s_next = logits(0)
            for pp in range(PAIRS):
                s2 = s_next
                if pp + 1 < PAIRS:
                    s_next = logits(pp + 1)
                vt2 = vt_ref[pp * 128:(pp + 1) * 128, :]
                for hh in range(2):
                    n = 2 * pp + hh
                    s = s2[:, hh * tq:(hh + 1) * tq]
                    if masked:
                        s = jnp.where(causal, s, MASKV)
                    m_prev = m_s[n]
                    m_new = jnp.maximum(m_prev, jnp.max(s, axis=0, keepdims=True))
                    pr = jnp.exp(s - m_new)
                    alpha = jnp.exp(m_prev - m_new)
                    l_s[n] = alpha * l_s[n] + jnp.sum(pr, axis=0, keepdims=True)
                    m_s[n] = m_new
                    acc_s[n] = alpha * acc_s[n] + _nn(vt2, pr.astype(BF16))

        @pl.when(j < i)
        def _():
            step(False)

        @pl.when(j == i)
        def _():
            step(True)
            sub = lax.broadcasted_iota(jnp.int32, (128, 1), 0)
            for pp in range(PAIRS):
                a, b = 2 * pp, 2 * pp + 1
                ot = jnp.where(sub < HD, acc_s[a] * (1.0 / l_s[a]), acc_s[b] * (1.0 / l_s[b]))
                o_ref[:, pp * 128:(pp + 1) * 128] = ot.T
                lrow = [m_s[2 * pp + hh] + jnp.log(l_s[2 * pp + hh]) - cref_of(pp, hh) for hh in range(2)]
                lsec_ref[:, pp * 128:(pp + 1) * 128] = jnp.where(sub == 0, lrow[0], jnp.where(sub == 1, lrow[1], 0.0)).T

    qi = lambda r, t: _fold_rows(r, t, nq)[0]
    kj = lambda r, t: _fold_rows(r, t, nq)[1]
    return pl.pallas_call(
        body,
        name="attn_fwd",
        grid=_fold_grid(nq),
        in_specs=[
            pl.BlockSpec((tq, PAIRS * 128), lambda g, r, t: (qi(r, t), g)),
            pl.BlockSpec((tk, PAIRS * 256), lambda g, r, t: (kj(r, t), g)),
            pl.BlockSpec((PAIRS * 128, tk), lambda g, r, t: (g, kj(r, t))),
            pl.BlockSpec((8, tq), lambda g, r, t: (0, qi(r, t))),
        ],
        out_specs=[pl.BlockSpec((tq, PAIRS * 128), lambda g, r, t: (qi(r, t), g))] * 2,
        out_shape=[jax.ShapeDtypeStruct((T, AW), F32)] * 2,
        scratch_shapes=[pltpu.VMEM((PAIRS, 2 * tq, 256), BF16), pltpu.VMEM((2 * PAIRS, 1, tq), F32),
                        pltpu.VMEM((2 * PAIRS, 1, tq), F32), pltpu.VMEM((2 * PAIRS, 128, tq), F32)],
        compiler_params=_params(3),
    )(qkv, kk, vt, ct)


def _sgu_forward(us_ref, lng, lnb, w_ref, bt_ref, mixed_s, vnb_s, tm):
    is_a = lax.broadcasted_iota(jnp.int32, (1, 128), 1) < HD
    u = us_ref[:, 0:SW]
    vs = us_ref[:, SW:NUS]
    ug, tu = _gelu(u)
    vg, tv = _gelu(vs)
    mu = jnp.mean(vg, axis=-1, keepdims=True)
    xc = vg - mu
    rstd = lax.rsqrt(jnp.mean(xc * xc, axis=-1, keepdims=True) + EPS)
    vhat = xc * rstd
    vnb_s[...] = (vhat * lng + lnb).astype(BF16)
    rr = lax.broadcasted_iota(jnp.int32, (CH, CH), 0)
    cc = lax.broadcasted_iota(jnp.int32, (CH, CH), 1)
    tril = cc <= rr
    for jj in range(4):
        wa = jnp.where(tril, w_ref[2 * jj], 0.0).astype(BF16)
        wb = jnp.where(tril, w_ref[2 * jj + 1], 0.0).astype(BF16)
        ba = bt_ref[:, 2 * jj:2 * jj + 1]
        bb = bt_ref[:, 2 * jj + 1:2 * jj + 2]
        for ch in range(tm // CH):
            rs, cs = slice(ch * CH, (ch + 1) * CH), slice(jj * 128, (jj + 1) * 128)
            vn2 = vnb_s[rs, cs]
            mixed_s[rs, cs] = jnp.where(is_a, _nn(wa, vn2) + ba, _nn(wb, vn2) + bb)
    mixed = mixed_s[...]
    return u, vs, ug, tu, tv, vhat, rstd, mixed, ug * mixed


def _sgu_out(us, yatt, x, lng, lnb, sgw, sgbt, gatt, gsg, wout, gpm):
    T = us.shape[0]
    tm = _tile(T, 512)

    def body(us_ref, ya_ref, x_ref, lng_ref, lnb_ref, w_ref, bt_ref, ga_ref, gs_ref, wo_ref, gp_ref,
             h1_ref, yb_ref, o_ref, mixed_s, vnb_s):
        ysg = _sgu_forward(us_ref, lng_ref[...], lnb_ref[...], w_ref, bt_ref, mixed_s, vnb_s, tm)[-1]
        yb_ref[:, 0:AW] = _rms_fwd(ya_ref[...], ga_ref[...]).astype(BF16)
        yb_ref[:, AW:D] = _rms_fwd(ysg, gs_ref[...]).astype(BF16)
        o = _nn(yb_ref[...], wo_ref[...])
        o_ref[...] = o
        h1_ref[...] = x_ref[...] + _rms_fwd(o, gp_ref[...])

    row = lambda n: pl.BlockSpec((tm, n), lambda i: (i, 0))
    return pl.pallas_call(
        body,
        name="sgu_out",
        grid=(T // tm,),
        in_specs=[row(NUS), row(AW), row(D), _full((1, SW)), _full((1, SW)), _full((8, CH, CH)), _full((CH, 8)),
                  _full((1, AW)), _full((1, SW)), _full((D, D)), _full((1, D))],
        out_specs=[row(D), row(D), row(D)],
        out_shape=[jax.ShapeDtypeStruct((T, D), F32), jax.ShapeDtypeStruct((T, D), BF16), jax.ShapeDtypeStruct((T, D), F32)],
        scratch_shapes=[pltpu.VMEM((tm, SW), F32), pltpu.VMEM((tm, SW), BF16)],
        compiler_params=_params(1),
    )(us, yatt, x, lng, lnb, sgw, sgbt, gatt, gsg, wout, gpm)


def _ffn_fwd(h1, gpre, w1g, w2g, gpost):
    T = h1.shape[0]
    tm = _tile(T, 512)
    nb, hb = w1g.shape[0], w1g.shape[2]

    def body(h1_ref, gpre_ref, w1_ref, w2_ref, gpost_ref, c2_ref, f1_ref, ff_ref, h2_ref):
        h1 = h1_ref[...]
        c2 = _rms_fwd(h1, gpre_ref[...]).astype(BF16)
        c2_ref[...] = c2
        ff = jnp.zeros((tm, D), F32)
        for j in range(nb):
            f1 = _nn(c2, w1_ref[j])
            f1_ref[:, j * hb:(j + 1) * hb] = f1.astype(BF16)
            r = jnp.maximum(f1, 0.0)
            ff = ff + _nn((r * r).astype(BF16), w2_ref[j])
        ff_ref[...] = ff
        h2_ref[...] = h1 + _rms_fwd(ff, gpost_ref[...])

    row = lambda n: pl.BlockSpec((tm, n), lambda i: (i, 0))
    once = lambda shape: pl.BlockSpec(shape, lambda i: (0,) * len(shape), pipeline_mode=pl.Buffered(1))
    return pl.pallas_call(
        body,
        name="ffn_fwd",
        grid=(T // tm,),
        in_specs=[row(D), _full((1, D)), once((nb, D, hb)), once((nb, hb, D)), _full((1, D))],
        out_specs=[row(D), row(DFF), row(D), row(D)],
        out_shape=[jax.ShapeDtypeStruct((T, D), BF16), jax.ShapeDtypeStruct((T, DFF), BF16),
                   jax.ShapeDtypeStruct((T, D), F32), jax.ShapeDtypeStruct((T, D), F32)],
        compiler_params=_params(1, VMEM_LIMIT_BIG),
    )(h1, gpre, w1g, w2g, gpost)


def _ple_loss(h2, p, tgt, wg, bg, wpe):
    T = h2.shape[0]
    tm = _tile(T, 512)

    def body(h2_ref, p_ref, t_ref, wg_ref, bg_ref, wpe_ref, dh2_ref, dbg_ref, loss_ref, dwg_ref, dwpe_ref):
        @pl.when(pl.program_id(0) == 0)
        def _():
            for r in (dbg_ref, loss_ref, dwg_ref, dwpe_ref):
                r[...] = jnp.zeros_like(r)

        h2 = h2_ref[...]
        h2b = h2.astype(BF16)
        gate = jax.nn.sigmoid(_nn(h2b, wg_ref[...]) + bg_ref[...])
        pb = p_ref[...].astype(BF16)
        pe = _nn(pb, wpe_ref[...])
        diff = (h2 + gate * pe) - t_ref[...]
        loss_ref[...] += jnp.sum(diff * diff)
        dh3 = diff * (1.0 / D)
        dpre = (dh3 * pe) * (gate * (1.0 - gate))
        dpre_b = dpre.astype(BF16)
        dbg_ref[...] += jnp.sum(dpre, axis=0, keepdims=True)
        dh2_ref[...] = dh3 + _nt(dpre_b, wg_ref[...])
        dwg_ref[...] += _nn(h2b.T, dpre_b)
        dwpe_ref[...] += _nn(pb.T, (dh3 * gate).astype(BF16))

    row = lambda n: pl.BlockSpec((tm, n), lambda i: (i, 0))
    once = lambda shape: pl.BlockSpec(shape, lambda i: (0,) * len(shape), pipeline_mode=pl.Buffered(1))
    return pl.pallas_call(
        body,
        name="ple_loss",
        grid=(T // tm,),
        in_specs=[row(D), row(PLE), row(D), _full((D, D)), _full((1, D)), _full((PLE, D))],
        out_specs=[row(D), _full((1, D)), _full((8, 128)), once((D, D)), once((PLE, D))],
        out_shape=[jax.ShapeDtypeStruct((T, D), F32), jax.ShapeDtypeStruct((1, D), F32),
                   jax.ShapeDtypeStruct((8, 128), F32), jax.ShapeDtypeStruct((D, D), F32),
                   jax.ShapeDtypeStruct((PLE, D), F32)],
        compiler_params=_params(1),
    )(h2, p, tgt, wg, bg, wpe)


def _ffn_bwd(dh2, ff, h1, f1, w1g, w2g, gpost, gpre):
    T = dh2.shape[0]
    tm = _tile(T, 512)
    nb, hb = w1g.shape[0], w1g.shape[2]

    def body(dh2_ref, ff_ref, h1_ref, f1_ref, w1_ref, w2_ref, gpost_ref, gpre_ref,
             dffb_ref, df1_ref, dh1_ref, dgpost_ref, dgpre_ref):
        @pl.when(pl.program_id(0) == 0)
        def _():
            dgpost_ref[...] = jnp.zeros_like(dgpost_ref)
            dgpre_ref[...] = jnp.zeros_like(dgpre_ref)

        dh2 = dh2_ref[...]
        dff, dg = _rms_bwd(dh2, ff_ref[...], gpost_ref[...])
        dffb = dff.astype(BF16)
        dffb_ref[...] = dffb
        dgpost_ref[...] += dg
        dc2 = jnp.zeros((tm, D), F32)
        for j in range(nb):
            cols = slice(j * hb, (j + 1) * hb)
            dact = _nt(dffb, w2_ref[j])
            df1 = (dact * (2.0 * jnp.maximum(f1_ref[:, cols].astype(F32), 0.0))).astype(BF16)
            df1_ref[:, cols] = df1
            dc2 = dc2 + _nt(df1, w1_ref[j])
        dx, dg = _rms_bwd(dc2, h1_ref[...], gpre_ref[...])
        dh1_ref[...] = dh2 + dx
        dgpre_ref[...] += dg

    row = lambda n: pl.BlockSpec((tm, n), lambda i: (i, 0))
    once = lambda shape: pl.BlockSpec(shape, lambda i: (0,) * len(shape), pipeline_mode=pl.Buffered(1))
    return pl.pallas_call(
        body,
        name="ffn_bwd",
        grid=(T // tm,),
        in_specs=[row(D), row(D), row(D), row(DFF), once((nb, D, hb)), once((nb, hb, D)), _full((1, D)), _full((1, D))],
        out_specs=[row(D), row(DFF), row(D), _full((1, D)), _full((1, D))],
        out_shape=[jax.ShapeDtypeStruct((T, D), BF16), jax.ShapeDtypeStruct((T, DFF), BF16),
                   jax.ShapeDtypeStruct((T, D), F32), jax.ShapeDtypeStruct((1, D), F32),
                   jax.ShapeDtypeStruct((1, D), F32)],
        compiler_params=_params(1, VMEM_LIMIT_BIG),
    )(dh2, ff, h1, f1, w1g, w2g, gpost, gpre)


def _mix_bwd(dh1, o, us, yatt, qkv, lsec, yb, lng, lnb, sgw, sgwt, sgbt, gatt, gsg, wout, gpm):
    T = dh1.shape[0]
    tm = _tile(T, 512)

    def body(dh1_ref, o_ref, us_ref, ya_ref, q_ref, l_ref, yb_ref, lng_ref, lnb_ref, w_ref, wt_ref, bt_ref, ga_ref, gs_ref,
             wo_ref, gp_ref, qw_ref, dow_ref, dus_ref, dwo_ref, dw_ref, dbt_ref, dlng_ref, dlnb_ref, dga_ref, dgs_ref,
             dgp_ref, mixed_s, vnb_s, dvn_s):
        @pl.when(pl.program_id(0) == 0)
        def _():
            for r in (dwo_ref, dw_ref, dbt_ref, dlng_ref, dlnb_ref, dga_ref, dgs_ref, dgp_ref):
                r[...] = jnp.zeros_like(r)

        is_a = lax.broadcasted_iota(jnp.int32, (1, 128), 1) < HD
        lane = lax.broadcasted_iota(jnp.int32, (1, 128), 1)
        do, dg = _rms_bwd(dh1_ref[...], o_ref[...], gp_ref[...])
        dgp_ref[...] += dg
        dob = do.astype(BF16)
        dwo_ref[...] += _nn(yb_ref[...].T, dob)
        dy = _nt(dob, wo_ref[...])
        ya = ya_ref[...]
        datt, dg = _rms_bwd(dy[:, 0:AW], ya, ga_ref[...])
        dga_ref[...] += dg
        _attn_operands(q_ref[...], datt, ya, l_ref[...], qw_ref, dow_ref)

        lng = lng_ref[...]
        u, vs, ug, tu, tv, vhat, rstd, mixed, ysg = _sgu_forward(us_ref, lng, lnb_ref[...], w_ref, bt_ref, mixed_s, vnb_s, tm)
        dysg, dg = _rms_bwd(dy[:, AW:D], ysg, gs_ref[...])
        dgs_ref[...] += dg
        dus_ref[:, 0:SW] = ((dysg * mixed) * _gelu_grad(u, tu)).astype(BF16)
        dmix = dysg * ug

        rr = lax.broadcasted_iota(jnp.int32, (CH, CH), 0)
        cc = lax.broadcasted_iota(jnp.int32, (CH, CH), 1)
        tril = cc <= rr
        triu = cc >= rr
        for jj in range(4):
            wta = jnp.where(triu, wt_ref[2 * jj], 0.0).astype(BF16)
            wtb = jnp.where(triu, wt_ref[2 * jj + 1], 0.0).astype(BF16)
            for ch in range(tm // CH):
                rs, cs = slice(ch * CH, (ch + 1) * CH), slice(jj * 128, (jj + 1) * 128)
                dm2 = dmix[rs, cs]
                dma = jnp.where(is_a, dm2, 0.0)
                dmb = jnp.where(is_a, 0.0, dm2)
                dma_b, dmb_b = dma.astype(BF16), dmb.astype(BF16)
                vn2 = vnb_s[rs, cs]
                dw_ref[2 * jj] += jnp.where(tril, _nt(dma_b, vn2), 0.0)
                dw_ref[2 * jj + 1] += jnp.where(tril, _nt(dmb_b, vn2), 0.0)
                dvn_s[rs, cs] = _nn(wta, dma_b) + _nn(wtb, dmb_b)
                dba = jnp.sum(dma, axis=1, keepdims=True)
                dbb = jnp.sum(dmb, axis=1, keepdims=True)
                dbt_ref[...] += jnp.where(lane == 2 * jj, dba, 0.0) + jnp.where(lane == 2 * jj + 1, dbb, 0.0)

        dvn = dvn_s[...]
        dlng_ref[...] += jnp.sum(dvn * vhat, axis=0, keepdims=True)
        dlnb_ref[...] += jnp.sum(dvn, axis=0, keepdims=True)
        dvh = dvn * lng
        dvg = rstd * (dvh - jnp.mean(dvh, axis=-1, keepdims=True) - vhat * jnp.mean(dvh * vhat, axis=-1, keepdims=True))
        dus_ref[:, SW:NUS] = (dvg * _gelu_grad(vs, tv)).astype(BF16)

    row = lambda n: pl.BlockSpec((tm, n), lambda i: (i, 0))
    return pl.pallas_call(
        body,
        name="mix_bwd",
        grid=(T // tm,),
        in_specs=[row(D), row(D), row(NUS), row(AW), row(AW), row(AW), row(D), _full((1, SW)), _full((1, SW)),
                  _full((8, CH, CH)), _full((8, CH, CH)), _full((CH, 8)), _full((1, AW)), _full((1, SW)), _full((D, D)),
                  _full((1, D))],
        out_specs=[row(4 * AW), row(4 * AW), row(NUS),
                   pl.BlockSpec((D, D), lambda i: (0, 0), pipeline_mode=pl.Buffered(1)), _full((8, CH, CH)),
                   _full((CH, 128)), _full((1, SW)), _full((1, SW)), _full((1, AW)), _full((1, SW)), _full((1, D))],
        out_shape=[jax.ShapeDtypeStruct((T, 4 * AW), BF16),
                   jax.ShapeDtypeStruct((T, 4 * AW), BF16), jax.ShapeDtypeStruct((T, NUS), BF16),
                   jax.ShapeDtypeStruct((D, D), F32),
                   jax.ShapeDtypeStruct((8, CH, CH), F32), jax.ShapeDtypeStruct((CH, 128), F32),
                   jax.ShapeDtypeStruct((1, SW), F32), jax.ShapeDtypeStruct((1, SW), F32),
                   jax.ShapeDtypeStruct((1, AW), F32), jax.ShapeDtypeStruct((1, SW), F32), jax.ShapeDtypeStruct((1, D), F32)],
        scratch_shapes=[pltpu.VMEM((tm, SW), F32), pltpu.VMEM((tm, SW), BF16), pltpu.VMEM((tm, SW), F32)],
        compiler_params=_params(1, VMEM_LIMIT_BIG),
    )(dh1, o, us, yatt, qkv, lsec, yb, lng, lnb, sgw, sgwt, sgbt, gatt, gsg, wout, gpm)


def _attn_operands(q, do, o, lsec, qw_ref, dow_ref):
    rows = do.shape[0]
    feat = lax.broadcasted_iota(jnp.int32, (AW, 128), 0)
    head = lax.broadcasted_iota(jnp.int32, (AW, 128), 1)
    sel = jnp.where((feat >= head * HD) & (feat < (head + 1) * HD), 1.0, 0.0)
    delta = jnp.dot(do * o, sel, precision=lax.Precision.HIGHEST, preferred_element_type=F32)
    for hp in range(4):
        cols = slice(hp * 128, (hp + 1) * 128)
        for hh in range(2):
            base = (2 * hp + hh) * 256
            lc = lsec[:, hp * 128 + hh:hp * 128 + hh + 1]
            d_h = delta[:, 2 * hp + hh:2 * hp + hh + 1]
            qw_ref[:, base:base + 128] = _head_rows(q[:, cols], hh, 0.125)
            qw_ref[:, base + 128:base + 256] = _aug_lanes(rows, AUG_A if hh == 0 else AUG_B, _split3(-lc))
            dow_ref[:, base:base + 128] = _head_rows(do[:, cols], hh, 1.0).astype(BF16)
            dow_ref[:, base + 128:base + 256] = _aug_lanes(rows, None, _split3(-d_h))


def _attn_bwd(kk, vk, kt, qw, dow, after):
    T = kk.shape[0]
    tq = _tile(T, 512)
    tk = tq
    nq = T // tq

    def body(kk_ref, vk_ref, kt_ref, qw_ref, dow_ref, after_ref, dqt_ref, dcq_ref, dk_ref, dv_ref, dck_ref, dk_s, dv_s, dck_s):
        j, i = _fold_cols(pl.program_id(1), pl.program_id(2), nq)
        sub8 = lax.broadcasted_iota(jnp.int32, (8, 1), 0)
        lane = lax.broadcasted_iota(jnp.int32, (1, 128), 1)

        @pl.when((pl.program_id(1) == 0) & (pl.program_id(2) == 0))
        def _():
            dqt_ref[...] = jnp.zeros_like(dqt_ref)
            dcq_ref[...] = jnp.zeros_like(dcq_ref)

        @pl.when(i == j)
        def _():
            dk_s[...] = jnp.zeros_like(dk_s)
            dv_s[...] = jnp.zeros_like(dv_s)
            dck_s[...] = jnp.zeros_like(dck_s)

        def step(masked):
            cols = pl.ds(pl.multiple_of(i * tq, tq), tq)
            sub = lax.broadcasted_iota(jnp.int32, (128, 1), 0)
            if masked:
                causal = lax.broadcasted_iota(jnp.int32, (tk, tq), 0) <= lax.broadcasted_iota(jnp.int32, (tk, tq), 1)

            def logits(n):
                pair, base = n // 2, n * 256
                return (_nt(kk_ref[:, pair * 256:(pair + 1) * 256], qw_ref[:, base:base + 256]),
                        _nt(vk_ref[:, pair * 256:(pair + 1) * 256], dow_ref[:, base:base + 256]))

            ahead = logits(0)
            dcq = jnp.zeros((8, tq), F32)
            dck = jnp.zeros((tk, 128), F32)
            for pp in range(PAIRS):
                lanes = slice(pp * 128, (pp + 1) * 128)
                kt2 = kt_ref[lanes, :] * 0.125
                dv = jnp.zeros((tk, 128), F32)
                dk = jnp.zeros((tk, 128), F32)
                dqts = []
                for hh in range(2):
                    head = 2 * pp + hh
                    base = head * 256
                    qw_h = qw_ref[:, base:base + 256]
                    dow_h = dow_ref[:, base:base + 256]
                    logp, dp = ahead
                    if head + 1 < 2 * PAIRS:
                        ahead = logits(head + 1)
                    pr = jnp.exp(logp)
                    if masked:
                        pr = jnp.where(causal, pr, 0.0)
                    ds = pr * dp
                    ds_b = ds.astype(BF16)
                    dv = dv + _nn(pr.astype(BF16), dow_h[:, 0:128])
                    dk = dk + _nn(ds_b, qw_h[:, 0:128])
                    dqts.append(_nn(kt2, ds_b))
                    dcq = dcq + jnp.where(sub8 == head, jnp.sum(ds, axis=0, keepdims=True), 0.0)
                    dck = dck - jnp.where(lane == head, jnp.sum(ds, axis=1, keepdims=True), 0.0)
                dv_s[:, lanes] += dv
                dk_s[:, lanes] += dk
                dqt_ref[lanes, cols] += jnp.where(sub < HD, dqts[0], dqts[1])
            dck_s[...] += dck
            dcq_ref[:, cols] += dcq

        @pl.when(i > j)
        def _():
            step(False)

        @pl.when(i == j)
        def _():
            step(True)

        @pl.when(i == nq - 1)
        def _():
            dk_ref[...] = dk_s[...].astype(BF16)
            dv_ref[...] = dv_s[...].astype(BF16)
            dck_ref[...] = dck_s[...]

    kj = lambda r, t: _fold_cols(r, t, nq)[0]
    qi = lambda r, t: _fold_cols(r, t, nq)[1]
    krow = lambda g, r, t: (kj(r, t), g)
    qrow = lambda g, r, t: (qi(r, t), g)
    return pl.pallas_call(
        body,
        name="attn_bwd",
        grid=_fold_grid(nq),
        in_specs=[
            pl.BlockSpec((tk, PAIRS * 256), krow),
            pl.BlockSpec((tk, PAIRS * 256), krow),
            pl.BlockSpec((PAIRS * 128, tk), lambda g, r, t: (g, kj(r, t))),
            pl.BlockSpec((tq, PAIRS * 512), qrow),
            pl.BlockSpec((tq, PAIRS * 512), qrow),
            pl.BlockSpec(after.shape, lambda g, r, t: (0,) * after.ndim),
        ],
        out_specs=[
            pl.BlockSpec((PAIRS * 128, T), lambda g, r, t: (g, 0), pipeline_mode=pl.Buffered(1)),
            pl.BlockSpec((8, T), lambda g, r, t: (0, 0), pipeline_mode=pl.Buffered(1)),
            pl.BlockSpec((tk, PAIRS * 128), krow),
            pl.BlockSpec((tk, PAIRS * 128), krow),
            pl.BlockSpec((tk, 128), lambda g, r, t: (kj(r, t), 0)),
        ],
        out_shape=[jax.ShapeDtypeStruct((AW, T), F32), jax.ShapeDtypeStruct((8, T), F32),
                   jax.ShapeDtypeStruct((T, AW), BF16), jax.ShapeDtypeStruct((T, AW), BF16),
                   jax.ShapeDtypeStruct((T, FPAD), F32)],
        scratch_shapes=[pltpu.VMEM((tk, PAIRS * 128), F32), pltpu.VMEM((tk, PAIRS * 128), F32),
                        pltpu.VMEM((tk, 128), F32)],
        compiler_params=_params(3),
    )(kk, vk, kt, qw, dow, after)


def _fgate_bwd(dcq, dck, fz, fb):
    T = dck.shape[0]
    tb = _tile(T, 512)
    nb = T // tb

    def body(dcq_ref, dck_ref, fz_ref, fb_ref, df_ref, dfb_ref, carry):
        @pl.when(pl.program_id(0) == 0)
        def _():
            carry[...] = jnp.zeros_like(carry)
            dfb_ref[...] = jnp.zeros_like(dfb_ref)

        head = lax.broadcasted_iota(jnp.int32, (8, FPAD), 0)
        eye = jnp.where(head == lax.broadcasted_iota(jnp.int32, (8, FPAD), 1), 1.0, 0.0)
        dcv = dck_ref[...] + lax.dot_general(dcq_ref[...], eye, (((0,), (0,)), ((), ())),
                                             precision=lax.Precision.HIGHEST, preferred_element_type=F32)
        r = lax.broadcasted_iota(jnp.int32, (128, 128), 0)
        cc = lax.broadcasted_iota(jnp.int32, (128, 128), 1)
        tri = (cc >= r).astype(F32)
        run = carry[...]
        parts = []
        for blk in reversed(range(tb // 128)):
            dc_b = dcv[blk * 128:(blk + 1) * 128, :]
            parts.append(jnp.dot(tri, dc_b, precision=lax.Precision.HIGHEST, preferred_element_type=F32) + run)
            run = run + jnp.sum(dc_b, axis=0, keepdims=True)
        carry[...] = run
        dlf = jnp.concatenate(parts[::-1], axis=0)
        lane = lax.broadcasted_iota(jnp.int32, (tb, FPAD), 1)
        df = jnp.where(lane < 8, dlf * jax.nn.sigmoid(-(fz_ref[...] + fb_ref[...])), 0.0)
        df_ref[...] = df.astype(BF16)
        dfb_ref[...] += jnp.sum(df, axis=0, keepdims=True)

    rev = pl.BlockSpec((tb, FPAD), lambda i: (nb - 1 - i, 0))
    return pl.pallas_call(
        body,
        name="fgate_bwd",
        grid=(nb,),
        in_specs=[pl.BlockSpec((8, tb), lambda i: (0, nb - 1 - i)), rev, rev, _full((1, FPAD))],
        out_specs=[rev, _full((1, FPAD))],
        out_shape=[jax.ShapeDtypeStruct((T, FPAD), BF16), jax.ShapeDtypeStruct((1, FPAD), F32)],
        scratch_shapes=[pltpu.VMEM((1, FPAD), F32)],
        compiler_params=_params(1),
    )(dcq, dck, fz, fb)


def _inproj_bwd(dqt, dk, dv, dus, dfz, wtp, x, dh1, g):
    T = x.shape[0]
    tm = _tile(T, 512)

    def body(dq_ref, dk_ref, dv_ref, dus_ref, dfz_ref, w_ref, x_ref, dh1_ref, g_ref, gx_ref, dg_ref):
        @pl.when(pl.program_id(0) == 0)
        def _():
            dg_ref[...] = jnp.zeros_like(dg_ref)

        da = _tn(dq_ref[...].astype(BF16), w_ref[0:AW, :])
        da += _nn(dk_ref[...], w_ref[AW:2 * AW, :])
        da += _nn(dv_ref[...], w_ref[2 * AW:NQKV, :])
        da += _nn(dus_ref[...], w_ref[NQKV:NQKV + NUS, :])
        da += _nn(dfz_ref[...], w_ref[NQKV + NUS:ZP, :])
        dx, dg = _rms_bwd(da, x_ref[...], g_ref[...])
        gx_ref[...] = dh1_ref[...] + dx
        dg_ref[...] += dg

    row = lambda n: pl.BlockSpec((tm, n), lambda i: (i, 0))
    return pl.pallas_call(
        body,
        name="inproj_bwd",
        grid=(T // tm,),
        in_specs=[pl.BlockSpec((AW, tm), lambda i: (0, i)), row(AW), row(AW), row(NUS), row(FPAD), _full((ZP, D)),
                  row(D), row(D), _full((1, D))],
        out_specs=[row(D), _full((1, D))],
        out_shape=[jax.ShapeDtypeStruct((T, D), F32), jax.ShapeDtypeStruct((1, D), F32)],
        compiler_params=_params(1),
    )(dqt, dk, dv, dus, dfz, wtp, x, dh1, g)


def _sq_relu(f1):
    r = jnp.maximum(f1.astype(F32), 0.0)
    return (r * r).astype(BF16)


def _wgrad(a, bs, name, a_fn=None, out_dtype=F32):
    T, K = a.shape
    tt = _tile(T, 1024)
    nb = len(bs)
    narrow = out_dtype != F32

    def out_dims(b, layout):
        if layout == "t":
            return (K, b.shape[0])
        N = b.shape[1]
        if layout is None:
            return (K, N)
        return (N // layout[1], K, layout[1]) if layout[0] == "col" else (K // layout[1], layout[1], N)

    shapes = [out_dims(b, layout) for b, layout in bs]

    def body(*refs):
        a_ref, b_refs, o_refs = refs[0], refs[1:1 + nb], refs[1 + nb:1 + 2 * nb]
        accs = refs[1 + 2 * nb:] if narrow else o_refs

        @pl.when(pl.program_id(0) == 0)
        def _():
            for acc in accs:
                acc[...] = jnp.zeros_like(acc)

        av = a_ref[...] if a_fn is None else a_fn(a_ref[...])
        at = av.astype(BF16).T
        for (b, layout), b_ref, o_ref in zip(bs, b_refs, accs):
            if layout is None:
                o_ref[...] += _nn(at, b_ref[...].astype(BF16))
            elif layout == "t":
                o_ref[...] += _nt(at, b_ref[...].astype(BF16))
            elif layout[0] == "col":
                n = layout[1]
                for k in range(b.shape[1] // n):
                    o_ref[k] += _nn(at, b_ref[:, k * n:(k + 1) * n].astype(BF16))
            else:
                n = layout[1]
                bv = b_ref[...].astype(BF16)
                for k in range(K // n):
                    o_ref[k] += _nn(at[k * n:(k + 1) * n, :], bv)

        if narrow:
            @pl.when(pl.program_id(0) == T // tt - 1)
            def _():
                for o_ref, acc in zip(o_refs, accs):
                    o_ref[...] = acc[...].astype(out_dtype)

    once = lambda shape: pl.BlockSpec(shape, lambda t: (0,) * len(shape), pipeline_mode=pl.Buffered(1))
    res = pl.pallas_call(
        body,
        name=name,
        grid=(T // tt,),
        in_specs=[pl.BlockSpec((tt, K), lambda t: (t, 0))] + [
            pl.BlockSpec((b.shape[0], tt), lambda t: (0, t)) if layout == "t" else pl.BlockSpec((tt, b.shape[1]), lambda t: (t, 0))
            for b, layout in bs],
        out_specs=[once(s) for s in shapes],
        out_shape=[jax.ShapeDtypeStruct(s, out_dtype) for s in shapes],
        scratch_shapes=[pltpu.VMEM(s, F32) for s in shapes] if narrow else [],
        compiler_params=_params(1, VMEM_LIMIT_BIG),
    )(a, *[b for b, _ in bs])
    return res


def _adam_math(w, g, m, v):
    m = ADAM_B1 * m + (1.0 - ADAM_B1) * g
    v = ADAM_B2 * v + (1.0 - ADAM_B2) * (g * g)
    m_hat = m / (1.0 - ADAM_B1 ** ADAM_STEP)
    v_hat = v / (1.0 - ADAM_B2 ** ADAM_STEP)
    delta = -ADAM_LR * (m_hat / (jnp.sqrt(v_hat) + ADAM_EPS) + ADAM_WD * w)
    return delta, m, v


def _adam(parts, w, m, v, name):
    R, C = w.shape
    br = 128 if R % 128 == 0 else R

    def body(p_ref, w_ref, m_ref, v_ref, g_ref, d_ref, nm_ref, nv_ref):
        g = p_ref[0].astype(F32)
        for s in range(1, NDEV):
            g = g + p_ref[s].astype(F32)
        g_ref[...] = g
        d_ref[...], nm_ref[...], nv_ref[...] = _adam_math(w_ref[...], g, m_ref[...], v_ref[...])

    blk = pl.BlockSpec((br, C), lambda i: (i, 0))
    return pl.pallas_call(
        body,
        name=name,
        grid=(R // br,),
        in_specs=[pl.BlockSpec((NDEV, br, C), lambda i: (0, i, 0)), blk, blk, blk],
        out_specs=[blk] * 4,
        out_shape=[jax.ShapeDtypeStruct((R, C), F32)] * 4,
        compiler_params=_params(1),
    )(parts, w, m, v)


_SMALL = (("sg_w", 8 * CH * CH), ("f_bias", 8), ("sg_ln_g", SW), ("sg_ln_b", SW), ("sg_b", 8 * CH), ("att_out_g", AW),
          ("sg_out_g", SW), ("pre_mix_g", D), ("post_mix_g", D), ("pre_ffn_g", D), ("post_ffn_g", D), ("ple_gate_b", D))
_SEG = 8 * 128


def _seg_rows(size):
    return 8 * (-(-size // _SEG))


def _pack(vals, loss_acc):
    parts = []
    for name, size in _SMALL:
        flat = vals[name].reshape(-1)
        rows = _seg_rows(size)
        parts.append(jnp.pad(flat, (0, rows * 128 - size)).reshape(rows, 128))
    parts.append(loss_acc)
    return jnp.concatenate(parts, axis=0)


def _adam_small(parts, ws, ms, vs):
    n = len(_SMALL)
    names = [name for name, _ in _SMALL]

    def body(*refs):
        p_ref, w_refs, m_refs, v_refs = refs[0], refs[1:1 + n], refs[1 + n:1 + 2 * n], refs[1 + 2 * n:1 + 3 * n]
        loss_ref, outs = refs[1 + 3 * n], refs[2 + 3 * n:]
        g_all = p_ref[0]
        for s in range(1, NDEV):
            g_all = g_all + p_ref[s]
        r = 0
        for k, (name, size) in enumerate(_SMALL):
            dst = [outs[kind * n + k] for kind in range(4)]

            def update(g, idx):
                vals = (g,) + _adam_math(w_refs[k][idx], g, m_refs[k][idx], v_refs[k][idx])
                for d, val in zip(dst, vals):
                    d[idx] = val

            if name == "sg_w":
                for grp in range(8):
                    update(g_all[r + grp * CH:r + (grp + 1) * CH, :], (0, grp))
            elif name == "sg_b":
                update(g_all[r:r + 8, :], (0,))
            elif name == "f_bias":
                update(g_all[r:r + 1, 0:8], (slice(None),))
            else:
                update(jnp.concatenate([g_all[r + q:r + q + 1, :] for q in range(size // 128)], axis=1), (slice(None),))
            r += _seg_rows(size)
        loss_ref[...] = g_all[r:r + 1, 0:1] * (0.5 / D)

    arrs = [parts] + [d[name] for d in (ws, ms, vs) for name in names]
    res = pl.pallas_call(
        body,
        name="adam_small",
        in_specs=[_full(a.shape) for a in arrs],
        out_specs=[_full((1, 1))] + [_full(ws[name].shape) for _ in range(4) for name in names],
        out_shape=[jax.ShapeDtypeStruct((1, 1), F32)] + [jax.ShapeDtypeStruct(ws[name].shape, F32) for _ in range(4) for name in names],
        compiler_params=pltpu.CompilerParams(vmem_limit_bytes=VMEM_LIMIT),
    )(*arrs)
    return res[0], {name: [res[1 + kind * n + k] for kind in range(4)] for k, name in enumerate(names)}


def kernel(x, p, w_in, f_bias, sg_ln_g, sg_ln_b, sg_w, sg_b, att_out_g, sg_out_g, w_out, pre_mix_g, post_mix_g, pre_ffn_g, post_ffn_g, w_ff1, w_ff2, ple_w, ple_gate_w, ple_gate_b, loss_target, m_w_in, m_f_bias, m_sg_ln_g, m_sg_ln_b, m_sg_w, m_sg_b, m_att_out_g, m_sg_out_g, m_w_out, m_pre_mix_g, m_post_mix_g, m_pre_ffn_g, m_post_ffn_g, m_w_ff1, m_w_ff2, m_ple_w, m_ple_gate_w, m_ple_gate_b, v_w_in, v_f_bias, v_sg_ln_g, v_sg_ln_b, v_sg_w, v_sg_b, v_att_out_g, v_sg_out_g, v_w_out, v_pre_mix_g, v_post_mix_g, v_pre_ffn_g, v_post_ffn_g, v_w_ff1, v_w_ff2, v_ple_w, v_ple_gate_w, v_ple_gate_b):
    small_w = dict(sg_w=sg_w, f_bias=f_bias, sg_ln_g=sg_ln_g, sg_ln_b=sg_ln_b, sg_b=sg_b, att_out_g=att_out_g,
                   sg_out_g=sg_out_g, pre_mix_g=pre_mix_g, post_mix_g=post_mix_g, pre_ffn_g=pre_ffn_g,
                   post_ffn_g=post_ffn_g, ple_gate_b=ple_gate_b)
    small_m = dict(sg_w=m_sg_w, f_bias=m_f_bias, sg_ln_g=m_sg_ln_g, sg_ln_b=m_sg_ln_b, sg_b=m_sg_b, att_out_g=m_att_out_g,
                   sg_out_g=m_sg_out_g, pre_mix_g=m_pre_mix_g, post_mix_g=m_post_mix_g, pre_ffn_g=m_pre_ffn_g,
                   post_ffn_g=m_post_ffn_g, ple_gate_b=m_ple_gate_b)
    small_v = dict(sg_w=v_sg_w, f_bias=v_f_bias, sg_ln_g=v_sg_ln_g, sg_ln_b=v_sg_ln_b, sg_b=v_sg_b, att_out_g=v_att_out_g,
                   sg_out_g=v_sg_out_g, pre_mix_g=v_pre_mix_g, post_mix_g=v_post_mix_g, pre_ffn_g=v_pre_ffn_g,
                   post_ffn_g=v_post_ffn_g, ple_gate_b=v_ple_gate_b)
    big = dict(w_in=(w_in, m_w_in, v_w_in), w_out=(w_out, m_w_out, v_w_out), w_ff1=(w_ff1, m_w_ff1, v_w_ff1),
               w_ff2=(w_ff2, m_w_ff2, v_w_ff2), ple_w=(ple_w, m_ple_w, v_ple_w),
               ple_gate_w=(ple_gate_w, m_ple_gate_w, v_ple_gate_w))

    xt, pt, tgt = x[0], p[0, 0], loss_target[0]
    ws = W_IN_COLS // NDEV

    gw_in = _gather(jnp.pad(jnp.transpose(w_in[0]).astype(BF16), ((0, WS_PAD - ws), (0, 0))), "gather_w_in")
    gw_in, others = lax.optimization_barrier((gw_in, [w_out[0].astype(BF16), w_ff1[0].astype(BF16), w_ff2[0].astype(BF16),
                                                      ple_w[0].astype(BF16), ple_gate_w[0].astype(BF16)]))
    rest = _xchg_start(others, True, "gather_rest_start")
    wint = gw_in[:, 0:ws, :].reshape(W_IN_COLS, D)
    wtp = jnp.concatenate([wint[0:NQKV], wint[NQKV + 8:W_IN_COLS], wint[NQKV:NQKV + 8],
                           jnp.zeros((FPAD - 8, D), BF16)], axis=0)

    fb = jnp.pad(f_bias.astype(F32), ((0, 0), (0, FPAD - 8)))
    sgw = sg_w[0]
    sgwt = jnp.transpose(sg_w[0], (0, 2, 1))
    sgbt = jnp.transpose(sg_b[0])

    qkv, kt, vt, us, fz, ab = _inproj(xt, pre_mix_g + rest[4][0, 0], wtp)
    ct, kk, vk = _fcum(fz, fb, qkv)
    yatt, lsec = _attn_fwd(qkv, kk, vt, ct)
    gw_out, gw1, gw2, gwpe, gwg = _xchg_wait(rest, True, yatt, "gather_rest_wait")
    wout = gw_out.reshape(D, D)
    wg = gwg.reshape(D, D)
    wpe = jnp.transpose(gwpe, (1, 0, 2)).reshape(PLE, D)
    h1, yb, o = _sgu_out(us, yatt, xt, sg_ln_g, sg_ln_b, sgw, sgbt, att_out_g, sg_out_g, wout, post_mix_g)
    c2b, f1b, ff, h2 = _ffn_fwd(h1, pre_ffn_g, gw1, gw2, post_ffn_g)
    dh2, dbg, loss_acc, g_g, g_pe = _ple_loss(h2, pt, tgt, wg, ple_gate_b, wpe)
    g_g = g_g.reshape(NDEV, D // NDEV, D)
    g_pe = jnp.transpose(g_pe.reshape(PLE, NDEV, D // NDEV), (1, 0, 2))

    dffb, df1, dh1, dgpostffn, dgpreffn = _ffn_bwd(dh2, ff, h1, f1b, gw1, gw2, post_ffn_g, pre_ffn_g)
    (g_1,) = _wgrad(c2b, [(df1, ("col", DFF // NDEV))], "wgrad_ff1")
    (g_2,) = _wgrad(f1b, [(dffb, ("row", DFF // NDEV))], "wgrad_ff2", a_fn=_sq_relu)
    early = _xchg_start([g_1, g_2, g_pe, g_g], False, "scatter_early_start")
    qw, dow, dus, g_out, dsgw, dsgbt, dlng, dlnb, dgatt, dgsg, dgpostmix = _mix_bwd(
        dh1, o, us, yatt, qkv, lsec, yb, sg_ln_g, sg_ln_b, sgw, sgwt, sgbt, att_out_g, sg_out_g, wout,
        post_mix_g + early[4][0, 0])
    mid = _xchg_start([g_out.reshape(NDEV, D // NDEV, D)], False, "scatter_mid_start")
    dqt, dcq, dk, dv, dck = _attn_bwd(kk, vk, kt, qw, dow, mid[4])
    dfz, dfb = _fgate_bwd(dcq, dck, fz, fb)

    gq, gk, gv, gus, gf = _wgrad(ab, [(dqt, "t"), (dk, None), (dv, None), (dus, None), (dfz, None)], "wgrad_in",
                                 out_dtype=BF16)
    g_in = jnp.concatenate([gq, gk, gv, gf[:, 0:8], gus], axis=1)
    g_in = jnp.transpose(g_in.reshape(D, NDEV, ws), (1, 0, 2))
    late = _xchg_start([g_in], False, "scatter_late_start")
    grad_x, dgpremix = _inproj_bwd(dqt, dk, dv, dus, dfz, wtp, xt, dh1, pre_mix_g + late[4][0, 0])

    small_g = dict(sg_w=dsgw, f_bias=dfb[:, 0:8], sg_ln_g=dlng, sg_ln_b=dlnb, sg_b=jnp.transpose(dsgbt[:, 0:8]),
                   att_out_g=dgatt, sg_out_g=dgsg, pre_mix_g=dgpremix, post_mix_g=dgpostmix, pre_ffn_g=dgpreffn,
                   post_ffn_g=dgpostffn, ple_gate_b=dbg)

    r_1, r_2, r_pe, r_g = _xchg_wait(early, False, grad_x, "scatter_early_wait")
    (r_out,) = _xchg_wait(mid, False, grad_x, "scatter_mid_wait")
    r_small = _gather(_pack(small_g, loss_acc), "gather_small_grads")

    res = {}

    def adam_big(name, parts):
        w, m, v = big[name]
        res[name] = [t[None] for t in _adam(parts, w[0], m[0], v[0], "adam_" + name)]

    for name, parts in (("w_out", r_out), ("w_ff1", r_1), ("w_ff2", r_2), ("ple_w", r_pe), ("ple_gate_w", r_g)):
        adam_big(name, parts)
    (r_in,) = _xchg_wait(late, False, res["w_ff1"][0], "scatter_late_wait")
    adam_big("w_in", r_in)
    loss, small = _adam_small(r_small, small_w, small_m, small_v)
    res.update(small)

    order = ["w_in", "f_bias", "sg_ln_g", "sg_ln_b", "sg_w", "sg_b", "att_out_g", "sg_out_g", "w_out", "pre_mix_g",
             "post_mix_g", "pre_ffn_g", "post_ffn_g", "w_ff1", "w_ff2", "ple_w", "ple_gate_w", "ple_gate_b"]
    outs = [loss[0, 0], grad_x[None]]
    for kind in range(4):
        outs += [res[name][kind] for name in order]
    return tuple(outs)
```

```python
import jax
import jax.numpy as jnp
from jax import lax
from jax.experimental import pallas as pl
from jax.experimental.pallas import tpu as pltpu

F32 = jnp.float32
BF16 = jnp.bfloat16

NDEV = 8
D = 1024
AW = 512
SW = 512
HD = 64
CH = 128
DFF = 4096
PLE = 256
NQKV = 3 * AW
NUS = 2 * SW
FPAD = 128
ZP = NQKV + NUS + FPAD
W_IN_COLS = 2568
WS_PAD = 336
EPS = 1e-6
MASKV = -1e30
GELU_K = 0.7978845608028654
GELU_C = 0.044715

ADAM_LR = 0.001
ADAM_B1 = 0.9
ADAM_B2 = 0.999
ADAM_EPS = 1e-08
ADAM_WD = 0.01
ADAM_STEP = 10

VMEM_LIMIT = 48 * 1024 * 1024
VMEM_LIMIT_BIG = 60 * 1024 * 1024


def _nn(a, b):
    return jnp.dot(a, b, preferred_element_type=F32)


def _nt(a, b):
    return lax.dot_general(a, b, (((1,), (1,)), ((), ())), preferred_element_type=F32)


def _tn(a, b):
    return lax.dot_general(a, b, (((0,), (0,)), ((), ())), preferred_element_type=F32)


def _tile(n, pref):
    return min(n, pref)


def _params(n_axes, vmem=VMEM_LIMIT):
    return pltpu.CompilerParams(dimension_semantics=("arbitrary",) * n_axes, vmem_limit_bytes=vmem)


def _full(shape):
    nd = len(shape)
    return pl.BlockSpec(shape, lambda *_: (0,) * nd)


def _rms_fwd(x, g):
    r = lax.rsqrt(jnp.mean(x * x, axis=-1, keepdims=True) + EPS)
    return x * r * g


def _rms_bwd(dy, x, g):
    n = x.shape[-1]
    r = lax.rsqrt(jnp.mean(x * x, axis=-1, keepdims=True) + EPS)
    u = dy * g
    s = jnp.sum(x * u, axis=-1, keepdims=True)
    dx = r * u - x * (r * r * r * (s * (1.0 / n)))
    dg = jnp.sum(dy * (x * r), axis=0, keepdims=True)
    return dx, dg


def _gelu(x):
    t = jnp.tanh(x * (GELU_K + (GELU_K * GELU_C) * (x * x)))
    return x * (0.5 + 0.5 * t), t


def _gelu_grad(x, t):
    return (0.5 + 0.5 * t) + (0.5 * x) * (1.0 - t * t) * (GELU_K + (3.0 * GELU_K * GELU_C) * (x * x))


def _gather(arr, name):
    def body(x_ref, out_ref, send, recv, loc):
        x, y, c = lax.axis_index("x"), lax.axis_index("y"), lax.axis_index("c")
        me, sibling = (x, y, c), (x, y, 1 - c)
        chips = [(1 - x, y), (x, 1 - y), (1 - x, 1 - y)]

        def slot(px, py, pc):
            return out_ref.at[4 * px + 2 * py + pc]

        def copy(k, block, to, src=None):
            return pltpu.make_async_remote_copy(
                src_ref=slot(*block) if src is None else src, dst_ref=slot(*block), send_sem=send.at[k],
                recv_sem=recv.at[k], device_id=to, device_id_type=pl.DeviceIdType.MESH)

        mine = pltpu.make_async_copy(x_ref, slot(*me), loc)
        mine.start()
        first = [copy(0, me, sibling, src=x_ref)] + [copy(1 + j, me, (*chip, c), src=x_ref) for j, chip in enumerate(chips)]
        for cp in first:
            cp.start()
        passed = [copy(4 + j, (*chip, c), sibling) for j, chip in enumerate(chips)]
        for j, chip in enumerate(chips):
            copy(1 + j, (*chip, c), me).wait_recv()
            passed[j].start()
        copy(0, sibling, me).wait_recv()
        for j, chip in enumerate(chips):
            copy(4 + j, (*chip, 1 - c), me).wait_recv()
        for cp in first + passed:
            cp.wait_send()
        mine.wait()

    hbm = pl.BlockSpec(memory_space=pltpu.HBM)
    return pl.pallas_call(
        body,
        name=name,
        out_shape=jax.ShapeDtypeStruct((NDEV,) + arr.shape, arr.dtype),
        in_specs=[hbm],
        out_specs=hbm,
        scratch_shapes=[pltpu.SemaphoreType.DMA((NDEV - 1,)), pltpu.SemaphoreType.DMA((NDEV - 1,)), pltpu.SemaphoreType.DMA],
    )(arr)


def _peers(x, y, c):
    out = []
    for k in range(1, NDEV):
        out.append((1 - x if (k >> 2) & 1 else x, 1 - y if (k >> 1) & 1 else y, 1 - c if k & 1 else c))
    return out


def _xchg_start(arrs, gather, name):
    n = len(arrs)
    me = 4 * lax.axis_index("x") + 2 * lax.axis_index("y") + lax.axis_index("c")
    lands = []
    for a in arrs:
        shape = ((NDEV,) + a.shape) if gather else a.shape
        own = a[None] if gather else lax.dynamic_slice_in_dim(a, me, 1, axis=0)
        lands.append(lax.dynamic_update_slice_in_dim(lax.empty(shape, a.dtype), own, me, axis=0))

    def body(*refs):
        ins, lnd = refs[:n], refs[n:2 * n]
        send, recv, token = refs[2 * n:3 * n], refs[3 * n:4 * n], refs[-1]
        x, y, c = lax.axis_index("x"), lax.axis_index("y"), lax.axis_index("c")
        mine = 4 * x + 2 * y + c
        for px, py, pc in _peers(x, y, c):
            peer = 4 * px + 2 * py + pc
            for a in range(n):
                pltpu.make_async_remote_copy(
                    src_ref=ins[a] if gather else ins[a].at[peer],
                    dst_ref=lnd[a].at[mine],
                    send_sem=send[a],
                    recv_sem=recv[a],
                    device_id=(px, py, pc),
                    device_id_type=pl.DeviceIdType.MESH,
                ).start()
        token[...] = jnp.zeros_like(token)

    hbm = pl.BlockSpec(memory_space=pltpu.HBM)
    sem = pl.BlockSpec(memory_space=pltpu.SEMAPHORE)
    res = pl.pallas_call(
        body,
        name=name,
        out_shape=(*[pltpu.SemaphoreType.DMA(())] * (2 * n),
                   *[pltpu.HBM(a.shape, a.dtype) for a in arrs], *[pltpu.HBM(l.shape, l.dtype) for l in lands],
                   jax.ShapeDtypeStruct((8, 128), F32)),
        in_specs=[hbm] * (2 * n),
        out_specs=(*([sem] * (2 * n)), *([hbm] * (2 * n)), pl.BlockSpec(memory_space=pltpu.VMEM)),
        input_output_aliases={i: 2 * n + i for i in range(2 * n)},
        compiler_params=pltpu.CompilerParams(has_side_effects=pltpu.SideEffectType.DATAFLOW_SIDE_EFFECTING),
    )(*[pltpu.with_memory_space_constraint(a, pltpu.HBM) for a in arrs],
      *[pltpu.with_memory_space_constraint(l, pltpu.HBM) for l in lands])
    return list(res[0:n]), list(res[n:2 * n]), list(res[2 * n:3 * n]), list(res[3 * n:4 * n]), res[-1]


def _xchg_wait(started, gather, after, name):
    send, recv, srcs, lands, _ = started
    n = len(srcs)

    def body(*refs):
        lnd = refs[n:2 * n]
        send, recv = refs[2 * n:3 * n], refs[3 * n:4 * n]
        me = (lax.axis_index("x"), lax.axis_index("y"), lax.axis_index("c"))
        for a in range(n):
            seven = lnd[a].at[pl.ds(0, NDEV - 1)]
            cp = pltpu.make_async_remote_copy(src_ref=seven, dst_ref=seven, send_sem=send[a], recv_sem=recv[a],
                                              device_id=me, device_id_type=pl.DeviceIdType.MESH)
            cp.wait_send()
            cp.wait_recv()

    hbm = pl.BlockSpec(memory_space=pltpu.HBM)
    sem = pl.BlockSpec(memory_space=pltpu.SEMAPHORE)
    res = pl.pallas_call(
        body,
        name=name,
        out_shape=tuple([pltpu.HBM(a.shape, a.dtype) for a in srcs] + [pltpu.HBM(l.shape, l.dtype) for l in lands]),
        in_specs=[hbm] * (2 * n) + [sem] * (2 * n) + [pl.BlockSpec(memory_space=pl.ANY)],
        out_specs=tuple([hbm] * (2 * n)),
        input_output_aliases={i: i for i in range(2 * n)},
        compiler_params=pltpu.CompilerParams(has_side_effects=pltpu.SideEffectType.DATAFLOW_SIDE_EFFECTING),
    )(*srcs, *lands, *send, *recv, after)
    return list(res[n:])


def _inproj(x, g, wtp):
    T = x.shape[0]
    tm = _tile(T, 512)

    def body(x_ref, g_ref, w_ref, qkv_ref, kt_ref, vt_ref, us_ref, fz_ref, ab_ref):
        a = _rms_fwd(x_ref[...], g_ref[...]).astype(BF16)
        ab_ref[...] = a
        qkv_ref[:, 0:AW] = _nt(a, w_ref[0:AW, :]).astype(BF16)
        kk = _nt(a, w_ref[AW:2 * AW, :])
        qkv_ref[:, AW:2 * AW] = kk.astype(BF16)
        kt_ref[...] = kk.T.astype(BF16)
        vv = _nt(a, w_ref[2 * AW:NQKV, :])
        qkv_ref[:, 2 * AW:NQKV] = vv.astype(BF16)
        vvt = vv.T.astype(BF16)
        one_row = jnp.where(lax.broadcasted_iota(jnp.int32, (HD, tm), 0) == 0, 1.0, 0.0).astype(BF16)
        for h in range(AW // HD):
            vt_ref[2 * h * HD:(2 * h + 1) * HD, :] = vvt[h * HD:(h + 1) * HD, :]
            vt_ref[(2 * h + 1) * HD:(2 * h + 2) * HD, :] = one_row
        us_ref[...] = _nt(a, w_ref[NQKV:NQKV + NUS, :])
        fz_ref[...] = _nt(a, w_ref[NQKV + NUS:ZP, :])

    row = lambda n: pl.BlockSpec((tm, n), lambda i: (i, 0))
    col = lambda n: pl.BlockSpec((n, tm), lambda i: (0, i))
    return pl.pallas_call(
        body,
        name="inproj",
        grid=(T // tm,),
        in_specs=[row(D), _full((1, D)), _full((ZP, D))],
        out_specs=[row(NQKV), col(AW), col(2 * AW), row(NUS), row(FPAD), row(D)],
        out_shape=[
            jax.ShapeDtypeStruct((T, NQKV), BF16),
            jax.ShapeDtypeStruct((AW, T), BF16),
            jax.ShapeDtypeStruct((2 * AW, T), BF16),
            jax.ShapeDtypeStruct((T, NUS), F32),
            jax.ShapeDtypeStruct((T, FPAD), F32),
            jax.ShapeDtypeStruct((T, D), BF16),
        ],
        compiler_params=_params(1),
    )(x, g, wtp)


def _log_sigmoid(z):
    return jnp.minimum(z, 0.0) - jnp.log1p(jnp.exp(-jnp.abs(z)))


def _split3(x):
    hi = x.astype(BF16)
    r1 = x - hi.astype(F32)
    mid = r1.astype(BF16)
    lo = (r1 - mid.astype(F32)).astype(BF16)
    return hi, mid, lo


AUG_A, AUG_B, AUG_ONE = 0, 3, 6


def _aug_lanes(rows, first_one, pieces):
    lane = lax.broadcasted_iota(jnp.int32, (rows, 128), 1)
    out = jnp.zeros((rows, 128), F32)
    if first_one is not None:
        out = jnp.where((lane >= first_one) & (lane < first_one + 3), 1.0, out)
    for n, piece in enumerate(pieces):
        out = jnp.where(lane == AUG_ONE + n, piece.astype(F32), out)
    return out.astype(BF16)


def _fcum(fz, fb, qkv):
    T = fz.shape[0]
    tb = _tile(T, 512)

    def body(fz_ref, fb_ref, k_ref, v_ref, ct_ref, kk_ref, vk_ref, carry, tri_s, sel_s):
        @pl.when(pl.program_id(0) == 0)
        def _():
            carry[...] = jnp.zeros_like(carry)
            src = lax.broadcasted_iota(jnp.int32, (128, 128), 0)
            dst = lax.broadcasted_iota(jnp.int32, (128, 128), 1)
            tri_s[...] = (dst <= src).astype(F32)
            for hp in range(4):
                for n in range(3):
                    pick = ((src == 2 * hp) & (dst == AUG_A + n)) | ((src == 2 * hp + 1) & (dst == AUG_B + n))
                    sel_s[3 * hp + n] = jnp.where(pick, -1.0, 0.0).astype(BF16)

        lf = _log_sigmoid(fz_ref[...] + fb_ref[...])
        run = carry[...]
        parts = []
        for blk in range(tb // 128):
            lf_b = lf[blk * 128:(blk + 1) * 128, :]
            parts.append(jnp.dot(tri_s[...], lf_b, precision=lax.Precision.HIGHEST, preferred_element_type=F32) + run)
            run = run + jnp.sum(lf_b, axis=0, keepdims=True)
        carry[...] = run
        cs = jnp.concatenate(parts, axis=0)
        ct_ref[...] = cs.T[0:8, :]

        pieces = _split3(cs)
        lane = lax.broadcasted_iota(jnp.int32, (1, 128), 1)
        ones = jnp.where((lane >= AUG_ONE) & (lane < AUG_ONE + 3), 1.0, 0.0)
        for hp in range(4):
            aug = jnp.zeros((tb, 128), F32) + ones
            for n, piece in enumerate(pieces):
                aug = aug + _nn(piece, sel_s[3 * hp + n])
            aug = aug.astype(BF16)
            kk_ref[:, hp * 256:hp * 256 + 128] = k_ref[:, hp * 128:(hp + 1) * 128]
            kk_ref[:, hp * 256 + 128:(hp + 1) * 256] = aug
            vk_ref[:, hp * 256:hp * 256 + 128] = v_ref[:, hp * 128:(hp + 1) * 128]
            vk_ref[:, hp * 256 + 128:(hp + 1) * 256] = aug

    row = lambda n, c: pl.BlockSpec((tb, n), lambda i: (i, c))
    return pl.pallas_call(
        body,
        name="fcum",
        grid=(T // tb,),
        in_specs=[row(FPAD, 0), _full((1, FPAD)), row(AW, 1), row(AW, 2)],
        out_specs=[pl.BlockSpec((8, tb), lambda i: (0, i)), row(2 * AW, 0), row(2 * AW, 0)],
        out_shape=[jax.ShapeDtypeStruct((8, T), F32), jax.ShapeDtypeStruct((T, 2 * AW), BF16),
                   jax.ShapeDtypeStruct((T, 2 * AW), BF16)],
        scratch_shapes=[pltpu.VMEM((1, FPAD), F32), pltpu.VMEM((128, 128), F32), pltpu.VMEM((12, 128, 128), BF16)],
        compiler_params=_params(1),
    )(fz, fb, qkv, qkv)


def _fold_rows(r, t, nq):
    if nq == 1:
        return r, t
    low = t <= r
    return jnp.where(low, r, nq - 1 - r), jnp.where(low, t, t - r - 1)


def _fold_cols(r, t, nq):
    if nq == 1:
        return r, t
    first = t < nq - r
    j = jnp.where(first, r, nq - 1 - r)
    return j, jnp.where(first, r + t, nq - 1 - r + (t - (nq - r)))


PAIRS = 4


def _fold_grid(nq):
    assert (nq == 1 or nq % 2 == 0) and PAIRS == 4
    return (4 // PAIRS, 1, 1) if nq == 1 else (4 // PAIRS, nq // 2, nq + 1)


def _head_rows(x2, hh, scale):
    is_a = lax.broadcasted_iota(jnp.int32, (1, 128), 1) < HD
    keep = is_a if hh == 0 else jnp.logical_not(is_a)
    return jnp.where(keep, x2, jnp.zeros_like(x2)) * scale


def _attn_fwd(qkv, kk, vt, ct):
    T = qkv.shape[0]
    tq = _tile(T, 512)
    tk = tq
    nq = T // tq

    def body(q_ref, kk_ref, vt_ref, ctq_ref, o_ref, lsec_ref, qw_s, m_s, acc_s):
        i, j = _fold_rows(pl.program_id(1), pl.program_id(2), nq)
        sub8 = lax.broadcasted_iota(jnp.int32, (8, 1), 0)

        def cref_of(pp, hh):
            head = 2 * (PAIRS * pl.program_id(0) + pp) + hh
            return jnp.sum(jnp.where(sub8 == head, ctq_ref[:, 0:1], 0.0), axis=0, keepdims=True)

        @pl.when(j == 0)
        def _():
            for pp in range(PAIRS):
                q2 = q_ref[:, pp * 128:(pp + 1) * 128]
                for hh in range(2):
                    rows = slice(hh * tq, (hh + 1) * tq)
                    qw_s[pp, rows, 0:128] = _head_rows(q2, hh, 0.125)
                    qw_s[pp, rows, 128:256] = _aug_lanes(tq, AUG_A if hh == 0 else AUG_B, _split3(cref_of(pp, hh)))
            m_s[...] = jnp.full_like(m_s, MASKV)
            acc_s[...] = jnp.zeros_like(acc_s)

        def step(masked):
            if masked:
                causal = lax.broadcasted_iota(jnp.int32, (tk, tq), 0) <= lax.broadcasted_iota(jnp.int32, (tk, tq), 1)
            logits = lambda pp: _nt(kk_ref[:, pp * 256:(pp + 1) * 256], qw_s[pp])
            s_next = logits(0)
            for pp in range(PAIRS):
                s2 = s_next
                if pp + 1 < PAIRS:
                    s_next = logits(pp + 1)
                for hh in range(2):
                    n = 2 * pp + hh
                    s = s2[:, hh * tq:(hh + 1) * tq]
                    if masked:
                        s = jnp.where(causal, s, MASKV)
                    m_prev = m_s[n]
                    m_new = jnp.maximum(m_prev, jnp.max(s, axis=0, keepdims=True))
                    pr = jnp.exp(s - m_new)
                    m_s[n] = m_new
                    acc_s[n] = jnp.exp(m_prev - m_new) * acc_s[n] + _nn(vt_ref[n * 128:(n + 1) * 128, :], pr.astype(BF16))

        @pl.when(j < i)
        def _():
            step(False)

        @pl.when(j == i)
        def _():
            step(True)
            sub = lax.broadcasted_iota(jnp.int32, (128, 1), 0)
            for pp in range(PAIRS):
                outs, lrow = [], []
                for hh in range(2):
                    n = 2 * pp + hh
                    den = acc_s[n, HD:HD + 1, :]
                    outs.append(acc_s[n, 0:HD, :] * (1.0 / den))
                    lrow.append(m_s[n] + jnp.log(den) - cref_of(pp, hh))
                o_ref[:, pp * 128:(pp + 1) * 128] = jnp.concatenate(outs, axis=0).T
                lsec_ref[:, pp * 128:(pp + 1) * 128] = jnp.where(sub == 0, lrow[0], jnp.where(sub == 1, lrow[1], 0.0)).T

    qi = lambda r, t: _fold_rows(r, t, nq)[0]
    kj = lambda r, t: _fold_rows(r, t, nq)[1]
    return pl.pallas_call(
        body,
        name="attn_fwd",
        grid=_fold_grid(nq),
        in_specs=[
            pl.BlockSpec((tq, PAIRS * 128), lambda g, r, t: (qi(r, t), g)),
            pl.BlockSpec((tk, PAIRS * 256), lambda g, r, t: (kj(r, t), g)),
            pl.BlockSpec((PAIRS * 256, tk), lambda g, r, t: (g, kj(r, t))),
            pl.BlockSpec((8, tq), lambda g, r, t: (0, qi(r, t))),
        ],
        out_specs=[pl.BlockSpec((tq, PAIRS * 128), lambda g, r, t: (qi(r, t), g))] * 2,
        out_shape=[jax.ShapeDtypeStruct((T, AW), F32)] * 2,
        scratch_shapes=[pltpu.VMEM((PAIRS, 2 * tq, 256), BF16), pltpu.VMEM((2 * PAIRS, 1, tq), F32),
                        pltpu.VMEM((2 * PAIRS, 128, tq), F32)],
        compiler_params=_params(3),
    )(qkv, kk, vt, ct)


def _sgu_forward(us_ref, lng, lnb, w_ref, bt_ref, mixed_s, vnb_s, tm):
    is_a = lax.broadcasted_iota(jnp.int32, (1, 128), 1) < HD
    u = us_ref[:, 0:SW]
    vs = us_ref[:, SW:NUS]
    ug, tu = _gelu(u)
    vg, tv = _gelu(vs)
    mu = jnp.mean(vg, axis=-1, keepdims=True)
    xc = vg - mu
    rstd = lax.rsqrt(jnp.mean(xc * xc, axis=-1, keepdims=True) + EPS)
    vhat = xc * rstd
    vnb_s[...] = (vhat * lng + lnb).astype(BF16)
    rr = lax.broadcasted_iota(jnp.int32, (CH, CH), 0)
    cc = lax.broadcasted_iota(jnp.int32, (CH, CH), 1)
    tril = cc <= rr
    for jj in range(4):
        wa = jnp.where(tril, w_ref[2 * jj], 0.0).astype(BF16)
        wb = jnp.where(tril, w_ref[2 * jj + 1], 0.0).astype(BF16)
        ba = bt_ref[:, 2 * jj:2 * jj + 1]
        bb = bt_ref[:, 2 * jj + 1:2 * jj + 2]
        for ch in range(tm // CH):
            rs, cs = slice(ch * CH, (ch + 1) * CH), slice(jj * 128, (jj + 1) * 128)
            vn2 = vnb_s[rs, cs]
            mixed_s[rs, cs] = jnp.where(is_a, _nn(wa, vn2) + ba, _nn(wb, vn2) + bb)
    mixed = mixed_s[...]
    return u, vs, ug, tu, tv, vhat, rstd, mixed, ug * mixed


def _sgu_out(us, yatt, x, lng, lnb, sgw, sgbt, gatt, gsg, wout, gpm):
    T = us.shape[0]
    tm = _tile(T, 512)

    def body(us_ref, ya_ref, x_ref, lng_ref, lnb_ref, w_ref, bt_ref, ga_ref, gs_ref, wo_ref, gp_ref,
             h1_ref, yb_ref, o_ref, mixed_s, vnb_s):
        ysg = _sgu_forward(us_ref, lng_ref[...], lnb_ref[...], w_ref, bt_ref, mixed_s, vnb_s, tm)[-1]
        yb_ref[:, 0:AW] = _rms_fwd(ya_ref[...], ga_ref[...]).astype(BF16)
        yb_ref[:, AW:D] = _rms_fwd(ysg, gs_ref[...]).astype(BF16)
        o = _nn(yb_ref[...], wo_ref[...])
        o_ref[...] = o
        h1_ref[...] = x_ref[...] + _rms_fwd(o, gp_ref[...])

    row = lambda n: pl.BlockSpec((tm, n), lambda i: (i, 0))
    return pl.pallas_call(
        body,
        name="sgu_out",
        grid=(T // tm,),
        in_specs=[row(NUS), row(AW), row(D), _full((1, SW)), _full((1, SW)), _full((8, CH, CH)), _full((CH, 8)),
                  _full((1, AW)), _full((1, SW)), _full((D, D)), _full((1, D))],
        out_specs=[row(D), row(D), row(D)],
        out_shape=[jax.ShapeDtypeStruct((T, D), F32), jax.ShapeDtypeStruct((T, D), BF16), jax.ShapeDtypeStruct((T, D), F32)],
        scratch_shapes=[pltpu.VMEM((tm, SW), F32), pltpu.VMEM((tm, SW), BF16)],
        compiler_params=_params(1),
    )(us, yatt, x, lng, lnb, sgw, sgbt, gatt, gsg, wout, gpm)


def _ffn_fwd(h1, gpre, w1g, w2g, gpost):
    T = h1.shape[0]
    tm = _tile(T, 512)
    nb, hb = w1g.shape[0], w1g.shape[2]

    def body(h1_ref, gpre_ref, w1_ref, w2_ref, gpost_ref, c2_ref, f1_ref, ff_ref, h2_ref):
        h1 = h1_ref[...]
        c2 = _rms_fwd(h1, gpre_ref[...]).astype(BF16)
        c2_ref[...] = c2
        ff = jnp.zeros((tm, D), F32)
        for j in range(nb):
            f1 = _nn(c2, w1_ref[j])
            f1_ref[:, j * hb:(j + 1) * hb] = f1.astype(BF16)
            r = jnp.maximum(f1, 0.0)
            ff = ff + _nn((r * r).astype(BF16), w2_ref[j])
        ff_ref[...] = ff
        h2_ref[...] = h1 + _rms_fwd(ff, gpost_ref[...])

    row = lambda n: pl.BlockSpec((tm, n), lambda i: (i, 0))
    once = lambda shape: pl.BlockSpec(shape, lambda i: (0,) * len(shape), pipeline_mode=pl.Buffered(1))
    return pl.pallas_call(
        body,
        name="ffn_fwd",
        grid=(T // tm,),
        in_specs=[row(D), _full((1, D)), once((nb, D, hb)), once((nb, hb, D)), _full((1, D))],
        out_specs=[row(D), row(DFF), row(D), row(D)],
        out_shape=[jax.ShapeDtypeStruct((T, D), BF16), jax.ShapeDtypeStruct((T, DFF), BF16),
                   jax.ShapeDtypeStruct((T, D), F32), jax.ShapeDtypeStruct((T, D), F32)],
        compiler_params=_params(1, VMEM_LIMIT_BIG),
    )(h1, gpre, w1g, w2g, gpost)


def _ple_loss(h2, p, tgt, wg, bg, wpe):
    T = h2.shape[0]
    tm = _tile(T, 512)

    def body(h2_ref, p_ref, t_ref, wg_ref, bg_ref, wpe_ref, dh2_ref, dbg_ref, loss_ref, dwg_ref, dwpe_ref):
        @pl.when(pl.program_id(0) == 0)
        def _():
            for r in (dbg_ref, loss_ref, dwg_ref, dwpe_ref):
                r[...] = jnp.zeros_like(r)

        h2 = h2_ref[...]
        h2b = h2.astype(BF16)
        gate = jax.nn.sigmoid(_nn(h2b, wg_ref[...]) + bg_ref[...])
        pb = p_ref[...].astype(BF16)
        pe = _nn(pb, wpe_ref[...])
        diff = (h2 + gate * pe) - t_ref[...]
        loss_ref[...] += jnp.sum(diff * diff)
        dh3 = diff * (1.0 / D)
        dpre = (dh3 * pe) * (gate * (1.0 - gate))
        dpre_b = dpre.astype(BF16)
        dbg_ref[...] += jnp.sum(dpre, axis=0, keepdims=True)
        dh2_ref[...] = dh3 + _nt(dpre_b, wg_ref[...])
        dwg_ref[...] += _nn(h2b.T, dpre_b)
        dwpe_ref[...] += _nn(pb.T, (dh3 * gate).astype(BF16))

    row = lambda n: pl.BlockSpec((tm, n), lambda i: (i, 0))
    once = lambda shape: pl.BlockSpec(shape, lambda i: (0,) * len(shape), pipeline_mode=pl.Buffered(1))
    return pl.pallas_call(
        body,
        name="ple_loss",
        grid=(T // tm,),
        in_specs=[row(D), row(PLE), row(D), _full((D, D)), _full((1, D)), _full((PLE, D))],
        out_specs=[row(D), _full((1, D)), _full((8, 128)), once((D, D)), once((PLE, D))],
        out_shape=[jax.ShapeDtypeStruct((T, D), F32), jax.ShapeDtypeStruct((1, D), F32),
                   jax.ShapeDtypeStruct((8, 128), F32), jax.ShapeDtypeStruct((D, D), F32),
                   jax.ShapeDtypeStruct((PLE, D), F32)],
        compiler_params=_params(1),
    )(h2, p, tgt, wg, bg, wpe)


def _ffn_bwd(dh2, ff, h1, f1, w1g, w2g, gpost, gpre):
    T = dh2.shape[0]
    tm = _tile(T, 512)
    nb, hb = w1g.shape[0], w1g.shape[2]

    def body(dh2_ref, ff_ref, h1_ref, f1_ref, w1_ref, w2_ref, gpost_ref, gpre_ref,
             dffb_ref, df1_ref, dh1_ref, dgpost_ref, dgpre_ref):
        @pl.when(pl.program_id(0) == 0)
        def _():
            dgpost_ref[...] = jnp.zeros_like(dgpost_ref)
            dgpre_ref[...] = jnp.zeros_like(dgpre_ref)

        dh2 = dh2_ref[...]
        dff, dg = _rms_bwd(dh2, ff_ref[...], gpost_ref[...])
        dffb = dff.astype(BF16)
        dffb_ref[...] = dffb
        dgpost_ref[...] += dg
        dc2 = jnp.zeros((tm, D), F32)
        for j in range(nb):
            cols = slice(j * hb, (j + 1) * hb)
            dact = _nt(dffb, w2_ref[j])
            df1 = (dact * (2.0 * jnp.maximum(f1_ref[:, cols].astype(F32), 0.0))).astype(BF16)
            df1_ref[:, cols] = df1
            dc2 = dc2 + _nt(df1, w1_ref[j])
        dx, dg = _rms_bwd(dc2, h1_ref[...], gpre_ref[...])
        dh1_ref[...] = dh2 + dx
        dgpre_ref[...] += dg

    row = lambda n: pl.BlockSpec((tm, n), lambda i: (i, 0))
    once = lambda shape: pl.BlockSpec(shape, lambda i: (0,) * len(shape), pipeline_mode=pl.Buffered(1))
    return pl.pallas_call(
        body,
        name="ffn_bwd",
        grid=(T // tm,),
        in_specs=[row(D), row(D), row(D), row(DFF), once((nb, D, hb)), once((nb, hb, D)), _full((1, D)), _full((1, D))],
        out_specs=[row(D), row(DFF), row(D), _full((1, D)), _full((1, D))],
        out_shape=[jax.ShapeDtypeStruct((T, D), BF16), jax.ShapeDtypeStruct((T, DFF), BF16),
                   jax.ShapeDtypeStruct((T, D), F32), jax.ShapeDtypeStruct((1, D), F32),
                   jax.ShapeDtypeStruct((1, D), F32)],
        compiler_params=_params(1, VMEM_LIMIT_BIG),
    )(dh2, ff, h1, f1, w1g, w2g, gpost, gpre)


def _mix_bwd(dh1, o, us, yatt, qkv, lsec, yb, lng, lnb, sgw, sgwt, sgbt, gatt, gsg, wout, gpm):
    T = dh1.shape[0]
    tm = _tile(T, 512)

    def body(dh1_ref, o_ref, us_ref, ya_ref, q_ref, l_ref, yb_ref, lng_ref, lnb_ref, w_ref, wt_ref, bt_ref, ga_ref, gs_ref,
             wo_ref, gp_ref, qw_ref, dow_ref, dus_ref, dwo_ref, dw_ref, dbt_ref, dlng_ref, dlnb_ref, dga_ref, dgs_ref,
             dgp_ref, mixed_s, vnb_s, dvn_s):
        @pl.when(pl.program_id(0) == 0)
        def _():
            for r in (dwo_ref, dw_ref, dbt_ref, dlng_ref, dlnb_ref, dga_ref, dgs_ref, dgp_ref):
                r[...] = jnp.zeros_like(r)

        is_a = lax.broadcasted_iota(jnp.int32, (1, 128), 1) < HD
        lane = lax.broadcasted_iota(jnp.int32, (1, 128), 1)
        do, dg = _rms_bwd(dh1_ref[...], o_ref[...], gp_ref[...])
        dgp_ref[...] += dg
        dob = do.astype(BF16)
        dwo_ref[...] += _nn(yb_ref[...].T, dob)
        dy = _nt(dob, wo_ref[...])
        ya = ya_ref[...]
        datt, dg = _rms_bwd(dy[:, 0:AW], ya, ga_ref[...])
        dga_ref[...] += dg
        _attn_operands(q_ref[...], datt, ya, l_ref[...], qw_ref, dow_ref)

        lng = lng_ref[...]
        u, vs, ug, tu, tv, vhat, rstd, mixed, ysg = _sgu_forward(us_ref, lng, lnb_ref[...], w_ref, bt_ref, mixed_s, vnb_s, tm)
        dysg, dg = _rms_bwd(dy[:, AW:D], ysg, gs_ref[...])
        dgs_ref[...] += dg
        dus_ref[:, 0:SW] = ((dysg * mixed) * _gelu_grad(u, tu)).astype(BF16)
        dmix = dysg * ug

        rr = lax.broadcasted_iota(jnp.int32, (CH, CH), 0)
        cc = lax.broadcasted_iota(jnp.int32, (CH, CH), 1)
        tril = cc <= rr
        triu = cc >= rr
        for jj in range(4):
            wta = jnp.where(triu, wt_ref[2 * jj], 0.0).astype(BF16)
            wtb = jnp.where(triu, wt_ref[2 * jj + 1], 0.0).astype(BF16)
            for ch in range(tm // CH):
                rs, cs = slice(ch * CH, (ch + 1) * CH), slice(jj * 128, (jj + 1) * 128)
                dm2 = dmix[rs, cs]
                dma = jnp.where(is_a, dm2, 0.0)
                dmb = jnp.where(is_a, 0.0, dm2)
                dma_b, dmb_b = dma.astype(BF16), dmb.astype(BF16)
                vn2 = vnb_s[rs, cs]
                dw_ref[2 * jj] += jnp.where(tril, _nt(dma_b, vn2), 0.0)
                dw_ref[2 * jj + 1] += jnp.where(tril, _nt(dmb_b, vn2), 0.0)
                dvn_s[rs, cs] = _nn(wta, dma_b) + _nn(wtb, dmb_b)
                dba = jnp.sum(dma, axis=1, keepdims=True)
                dbb = jnp.sum(dmb, axis=1, keepdims=True)
                dbt_ref[...] += jnp.where(lane == 2 * jj, dba, 0.0) + jnp.where(lane == 2 * jj + 1, dbb, 0.0)

        dvn = dvn_s[...]
        dlng_ref[...] += jnp.sum(dvn * vhat, axis=0, keepdims=True)
        dlnb_ref[...] += jnp.sum(dvn, axis=0, keepdims=True)
        dvh = dvn * lng
        dvg = rstd * (dvh - jnp.mean(dvh, axis=-1, keepdims=True) - vhat * jnp.mean(dvh * vhat, axis=-1, keepdims=True))
        dus_ref[:, SW:NUS] = (dvg * _gelu_grad(vs, tv)).astype(BF16)

    row = lambda n: pl.BlockSpec((tm, n), lambda i: (i, 0))
    return pl.pallas_call(
        body,
        name="mix_bwd",
        grid=(T // tm,),
        in_specs=[row(D), row(D), row(NUS), row(AW), row(AW), row(AW), row(D), _full((1, SW)), _full((1, SW)),
                  _full((8, CH, CH)), _full((8, CH, CH)), _full((CH, 8)), _full((1, AW)), _full((1, SW)), _full((D, D)),
                  _full((1, D))],
        out_specs=[row(4 * AW), row(4 * AW), row(NUS),
                   pl.BlockSpec((D, D), lambda i: (0, 0), pipeline_mode=pl.Buffered(1)), _full((8, CH, CH)),
                   _full((CH, 128)), _full((1, SW)), _full((1, SW)), _full((1, AW)), _full((1, SW)), _full((1, D))],
        out_shape=[jax.ShapeDtypeStruct((T, 4 * AW), BF16),
                   jax.ShapeDtypeStruct((T, 4 * AW), BF16), jax.ShapeDtypeStruct((T, NUS), BF16),
                   jax.ShapeDtypeStruct((D, D), F32),
                   jax.ShapeDtypeStruct((8, CH, CH), F32), jax.ShapeDtypeStruct((CH, 128), F32),
                   jax.ShapeDtypeStruct((1, SW), F32), jax.ShapeDtypeStruct((1, SW), F32),
                   jax.ShapeDtypeStruct((1, AW), F32), jax.ShapeDtypeStruct((1, SW), F32), jax.ShapeDtypeStruct((1, D), F32)],
        scratch_shapes=[pltpu.VMEM((tm, SW), F32), pltpu.VMEM((tm, SW), BF16), pltpu.VMEM((tm, SW), F32)],
        compiler_params=_params(1, VMEM_LIMIT_BIG),
    )(dh1, o, us, yatt, qkv, lsec, yb, lng, lnb, sgw, sgwt, sgbt, gatt, gsg, wout, gpm)


def _attn_operands(q, do, o, lsec, qw_ref, dow_ref):
    rows = do.shape[0]
    feat = lax.broadcasted_iota(jnp.int32, (AW, 128), 0)
    head = lax.broadcasted_iota(jnp.int32, (AW, 128), 1)
    sel = jnp.where((feat >= head * HD) & (feat < (head + 1) * HD), 1.0, 0.0)
    delta = jnp.dot(do * o, sel, precision=lax.Precision.HIGHEST, preferred_element_type=F32)
    for hp in range(4):
        cols = slice(hp * 128, (hp + 1) * 128)
        for hh in range(2):
            base = (2 * hp + hh) * 256
            lc = lsec[:, hp * 128 + hh:hp * 128 + hh + 1]
            d_h = delta[:, 2 * hp + hh:2 * hp + hh + 1]
            qw_ref[:, base:base + 128] = _head_rows(q[:, cols], hh, 0.125)
            qw_ref[:, base + 128:base + 256] = _aug_lanes(rows, AUG_A if hh == 0 else AUG_B, _split3(-lc))
            dow_ref[:, base:base + 128] = _head_rows(do[:, cols], hh, 1.0).astype(BF16)
            dow_ref[:, base + 128:base + 256] = _aug_lanes(rows, None, _split3(-d_h))


def _attn_bwd(kk, vk, kt, qw, dow, after):
    T = kk.shape[0]
    tq = _tile(T, 512)
    tk = tq
    nq = T // tq

    def body(kk_ref, vk_ref, kt_ref, qw_ref, dow_ref, after_ref, dqt_ref, dcq_ref, dk_ref, dv_ref, dck_ref, dk_s, dv_s, dck_s):
        j, i = _fold_cols(pl.program_id(1), pl.program_id(2), nq)
        sub8 = lax.broadcasted_iota(jnp.int32, (8, 1), 0)
        lane = lax.broadcasted_iota(jnp.int32, (1, 128), 1)

        @pl.when((pl.program_id(1) == 0) & (pl.program_id(2) == 0))
        def _():
            dqt_ref[...] = jnp.zeros_like(dqt_ref)
            dcq_ref[...] = jnp.zeros_like(dcq_ref)

        @pl.when(i == j)
        def _():
            dk_s[...] = jnp.zeros_like(dk_s)
            dv_s[...] = jnp.zeros_like(dv_s)
            dck_s[...] = jnp.zeros_like(dck_s)

        def step(masked):
            cols = pl.ds(pl.multiple_of(i * tq, tq), tq)
            sub = lax.broadcasted_iota(jnp.int32, (128, 1), 0)
            if masked:
                causal = lax.broadcasted_iota(jnp.int32, (tk, tq), 0) <= lax.broadcasted_iota(jnp.int32, (tk, tq), 1)

            def logits(n):
                pair, base = n // 2, n * 256
                return (_nt(kk_ref[:, pair * 256:(pair + 1) * 256], qw_ref[:, base:base + 256]),
                        _nt(vk_ref[:, pair * 256:(pair + 1) * 256], dow_ref[:, base:base + 256]))

            ahead = logits(0)
            dcq = jnp.zeros((8, tq), F32)
            dck = jnp.zeros((tk, 128), F32)
            for pp in range(PAIRS):
                lanes = slice(pp * 128, (pp + 1) * 128)
                kt2 = kt_ref[lanes, :] * 0.125
                dv = jnp.zeros((tk, 128), F32)
                dk = jnp.zeros((tk, 128), F32)
                dqts = []
                for hh in range(2):
                    head = 2 * pp + hh
                    base = head * 256
                    qw_h = qw_ref[:, base:base + 256]
                    dow_h = dow_ref[:, base:base + 256]
                    logp, dp = ahead
                    if head + 1 < 2 * PAIRS:
                        ahead = logits(head + 1)
                    pr = jnp.exp(logp)
                    if masked:
                        pr = jnp.where(causal, pr, 0.0)
                    ds = pr * dp
                    ds_b = ds.astype(BF16)
                    dv = dv + _nn(pr.astype(BF16), dow_h[:, 0:128])
                    dk = dk + _nn(ds_b, qw_h[:, 0:128])
                    dqts.append(_nn(kt2, ds_b))
                    dcq = dcq + jnp.where(sub8 == head, jnp.sum(ds, axis=0, keepdims=True), 0.0)
                    dck = dck - jnp.where(lane == head, jnp.sum(ds, axis=1, keepdims=True), 0.0)
                dv_s[:, lanes] += dv
                dk_s[:, lanes] += dk
                dqt_ref[lanes, cols] += jnp.where(sub < HD, dqts[0], dqts[1])
            dck_s[...] += dck
            dcq_ref[:, cols] += dcq

        @pl.when(i > j)
        def _():
            step(False)

        @pl.when(i == j)
        def _():
            step(True)

        @pl.when(i == nq - 1)
        def _():
            dk_ref[...] = dk_s[...].astype(BF16)
            dv_ref[...] = dv_s[...].astype(BF16)
            dck_ref[...] = dck_s[...]

    kj = lambda r, t: _fold_cols(r, t, nq)[0]
    qi = lambda r, t: _fold_cols(r, t, nq)[1]
    krow = lambda g, r, t: (kj(r, t), g)
    qrow = lambda g, r, t: (qi(r, t), g)
    return pl.pallas_call(
        body,
        name="attn_bwd",
        grid=_fold_grid(nq),
        in_specs=[
            pl.BlockSpec((tk, PAIRS * 256), krow),
            pl.BlockSpec((tk, PAIRS * 256), krow),
            pl.BlockSpec((PAIRS * 128, tk), lambda g, r, t: (g, kj(r, t))),
            pl.BlockSpec((tq, PAIRS * 512), qrow),
            pl.BlockSpec((tq, PAIRS * 512), qrow),
            pl.BlockSpec(after.shape, lambda g, r, t: (0,) * after.ndim),
        ],
        out_specs=[
            pl.BlockSpec((PAIRS * 128, T), lambda g, r, t: (g, 0), pipeline_mode=pl.Buffered(1)),
            pl.BlockSpec((8, T), lambda g, r, t: (0, 0), pipeline_mode=pl.Buffered(1)),
            pl.BlockSpec((tk, PAIRS * 128), krow),
            pl.BlockSpec((tk, PAIRS * 128), krow),
            pl.BlockSpec((tk, 128), lambda g, r, t: (kj(r, t), 0)),
        ],
        out_shape=[jax.ShapeDtypeStruct((AW, T), F32), jax.ShapeDtypeStruct((8, T), F32),
                   jax.ShapeDtypeStruct((T, AW), BF16), jax.ShapeDtypeStruct((T, AW), BF16),
                   jax.ShapeDtypeStruct((T, FPAD), F32)],
        scratch_shapes=[pltpu.VMEM((tk, PAIRS * 128), F32), pltpu.VMEM((tk, PAIRS * 128), F32),
                        pltpu.VMEM((tk, 128), F32)],
        compiler_params=_params(3),
    )(kk, vk, kt, qw, dow, after)


def _fgate_bwd(dcq, dck, fz, fb):
    T = dck.shape[0]
    tb = _tile(T, 512)
    nb = T // tb

    def body(dcq_ref, dck_ref, fz_ref, fb_ref, df_ref, dfb_ref, carry):
        @pl.when(pl.program_id(0) == 0)
        def _():
            carry[...] = jnp.zeros_like(carry)
            dfb_ref[...] = jnp.zeros_like(dfb_ref)

        head = lax.broadcasted_iota(jnp.int32, (8, FPAD), 0)
        eye = jnp.where(head == lax.broadcasted_iota(jnp.int32, (8, FPAD), 1), 1.0, 0.0)
        dcv = dck_ref[...] + lax.dot_general(dcq_ref[...], eye, (((0,), (0,)), ((), ())),
                                             precision=lax.Precision.HIGHEST, preferred_element_type=F32)
        r = lax.broadcasted_iota(jnp.int32, (128, 128), 0)
        cc = lax.broadcasted_iota(jnp.int32, (128, 128), 1)
        tri = (cc >= r).astype(F32)
        run = carry[...]
        parts = []
        for blk in reversed(range(tb // 128)):
            dc_b = dcv[blk * 128:(blk + 1) * 128, :]
            parts.append(jnp.dot(tri, dc_b, precision=lax.Precision.HIGHEST, preferred_element_type=F32) + run)
            run = run + jnp.sum(dc_b, axis=0, keepdims=True)
        carry[...] = run
        dlf = jnp.concatenate(parts[::-1], axis=0)
        lane = lax.broadcasted_iota(jnp.int32, (tb, FPAD), 1)
        df = jnp.where(lane < 8, dlf * jax.nn.sigmoid(-(fz_ref[...] + fb_ref[...])), 0.0)
        df_ref[...] = df.astype(BF16)
        dfb_ref[...] += jnp.sum(df, axis=0, keepdims=True)

    rev = pl.BlockSpec((tb, FPAD), lambda i: (nb - 1 - i, 0))
    return pl.pallas_call(
        body,
        name="fgate_bwd",
        grid=(nb,),
        in_specs=[pl.BlockSpec((8, tb), lambda i: (0, nb - 1 - i)), rev, rev, _full((1, FPAD))],
        out_specs=[rev, _full((1, FPAD))],
        out_shape=[jax.ShapeDtypeStruct((T, FPAD), BF16), jax.ShapeDtypeStruct((1, FPAD), F32)],
        scratch_shapes=[pltpu.VMEM((1, FPAD), F32)],
        compiler_params=_params(1),
    )(dcq, dck, fz, fb)


def _inproj_bwd(dqt, dk, dv, dus, dfz, wtp, x, dh1, g):
    T = x.shape[0]
    tm = _tile(T, 512)

    def body(dq_ref, dk_ref, dv_ref, dus_ref, dfz_ref, w_ref, x_ref, dh1_ref, g_ref, gx_ref, dg_ref):
        @pl.when(pl.program_id(0) == 0)
        def _():
            dg_ref[...] = jnp.zeros_like(dg_ref)

        da = _tn(dq_ref[...].astype(BF16), w_ref[0:AW, :])
        da += _nn(dk_ref[...], w_ref[AW:2 * AW, :])
        da += _nn(dv_ref[...], w_ref[2 * AW:NQKV, :])
        da += _nn(dus_ref[...], w_ref[NQKV:NQKV + NUS, :])
        da += _nn(dfz_ref[...], w_ref[NQKV + NUS:ZP, :])
        dx, dg = _rms_bwd(da, x_ref[...], g_ref[...])
        gx_ref[...] = dh1_ref[...] + dx
        dg_ref[...] += dg

    row = lambda n: pl.BlockSpec((tm, n), lambda i: (i, 0))
    return pl.pallas_call(
        body,
        name="inproj_bwd",
        grid=(T // tm,),
        in_specs=[pl.BlockSpec((AW, tm), lambda i: (0, i)), row(AW), row(AW), row(NUS), row(FPAD), _full((ZP, D)),
                  row(D), row(D), _full((1, D))],
        out_specs=[row(D), _full((1, D))],
        out_shape=[jax.ShapeDtypeStruct((T, D), F32), jax.ShapeDtypeStruct((1, D), F32)],
        compiler_params=_params(1),
    )(dqt, dk, dv, dus, dfz, wtp, x, dh1, g)


def _sq_relu(f1):
    r = jnp.maximum(f1.astype(F32), 0.0)
    return (r * r).astype(BF16)


def _wgrad(a, bs, name, a_fn=None, out_dtype=F32):
    T, K = a.shape
    tt = _tile(T, 1024)
    nb = len(bs)
    narrow = out_dtype != F32

    def out_dims(b, layout):
        if layout == "t":
            return (K, b.shape[0])
        N = b.shape[1]
        if layout is None:
            return (K, N)
        return (N // layout[1], K, layout[1]) if layout[0] == "col" else (K // layout[1], layout[1], N)

    shapes = [out_dims(b, layout) for b, layout in bs]

    def body(*refs):
        a_ref, b_refs, o_refs = refs[0], refs[1:1 + nb], refs[1 + nb:1 + 2 * nb]
        accs = refs[1 + 2 * nb:] if narrow else o_refs

        @pl.when(pl.program_id(0) == 0)
        def _():
            for acc in accs:
                acc[...] = jnp.zeros_like(acc)

        av = a_ref[...] if a_fn is None else a_fn(a_ref[...])
        at = av.astype(BF16).T
        for (b, layout), b_ref, o_ref in zip(bs, b_refs, accs):
            if layout is None:
                o_ref[...] += _nn(at, b_ref[...].astype(BF16))
            elif layout == "t":
                o_ref[...] += _nt(at, b_ref[...].astype(BF16))
            elif layout[0] == "col":
                n = layout[1]
                for k in range(b.shape[1] // n):
                    o_ref[k] += _nn(at, b_ref[:, k * n:(k + 1) * n].astype(BF16))
            else:
                n = layout[1]
                bv = b_ref[...].astype(BF16)
                for k in range(K // n):
                    o_ref[k] += _nn(at[k * n:(k + 1) * n, :], bv)

        if narrow:
            @pl.when(pl.program_id(0) == T // tt - 1)
            def _():
                for o_ref, acc in zip(o_refs, accs):
                    o_ref[...] = acc[...].astype(out_dtype)

    once = lambda shape: pl.BlockSpec(shape, lambda t: (0,) * len(shape), pipeline_mode=pl.Buffered(1))
    res = pl.pallas_call(
        body,
        name=name,
        grid=(T // tt,),
        in_specs=[pl.BlockSpec((tt, K), lambda t: (t, 0))] + [
            pl.BlockSpec((b.shape[0], tt), lambda t: (0, t)) if layout == "t" else pl.BlockSpec((tt, b.shape[1]), lambda t: (t, 0))
            for b, layout in bs],
        out_specs=[once(s) for s in shapes],
        out_shape=[jax.ShapeDtypeStruct(s, out_dtype) for s in shapes],
        scratch_shapes=[pltpu.VMEM(s, F32) for s in shapes] if narrow else [],
        compiler_params=_params(1, VMEM_LIMIT_BIG),
    )(a, *[b for b, _ in bs])
    return res


def _adam_math(w, g, m, v):
    m = ADAM_B1 * m + (1.0 - ADAM_B1) * g
    v = ADAM_B2 * v + (1.0 - ADAM_B2) * (g * g)
    m_hat = m / (1.0 - ADAM_B1 ** ADAM_STEP)
    v_hat = v / (1.0 - ADAM_B2 ** ADAM_STEP)
    delta = -ADAM_LR * (m_hat / (jnp.sqrt(v_hat) + ADAM_EPS) + ADAM_WD * w)
    return delta, m, v


def _adam(parts, w, m, v, name):
    R, C = w.shape
    br = 128 if R % 128 == 0 else R

    def body(p_ref, w_ref, m_ref, v_ref, g_ref, d_ref, nm_ref, nv_ref):
        g = p_ref[0].astype(F32)
        for s in range(1, NDEV):
            g = g + p_ref[s].astype(F32)
        g_ref[...] = g
        d_ref[...], nm_ref[...], nv_ref[...] = _adam_math(w_ref[...], g, m_ref[...], v_ref[...])

    blk = pl.BlockSpec((br, C), lambda i: (i, 0))
    return pl.pallas_call(
        body,
        name=name,
        grid=(R // br,),
        in_specs=[pl.BlockSpec((NDEV, br, C), lambda i: (0, i, 0)), blk, blk, blk],
        out_specs=[blk] * 4,
        out_shape=[jax.ShapeDtypeStruct((R, C), F32)] * 4,
        compiler_params=_params(1),
    )(parts, w, m, v)


_SMALL = (("sg_w", 8 * CH * CH), ("f_bias", 8), ("sg_ln_g", SW), ("sg_ln_b", SW), ("sg_b", 8 * CH), ("att_out_g", AW),
          ("sg_out_g", SW), ("pre_mix_g", D), ("post_mix_g", D), ("pre_ffn_g", D), ("post_ffn_g", D), ("ple_gate_b", D))
_SEG = 8 * 128


def _seg_rows(size):
    return 8 * (-(-size // _SEG))


def _pack(vals, loss_acc):
    parts = []
    for name, size in _SMALL:
        flat = vals[name].reshape(-1)
        rows = _seg_rows(size)
        parts.append(jnp.pad(flat, (0, rows * 128 - size)).reshape(rows, 128))
    parts.append(loss_acc)
    return jnp.concatenate(parts, axis=0)


def _adam_small(parts, ws, ms, vs):
    n = len(_SMALL)
    names = [name for name, _ in _SMALL]

    def body(*refs):
        p_ref, w_refs, m_refs, v_refs = refs[0], refs[1:1 + n], refs[1 + n:1 + 2 * n], refs[1 + 2 * n:1 + 3 * n]
        loss_ref, outs = refs[1 + 3 * n], refs[2 + 3 * n:]
        g_all = p_ref[0]
        for s in range(1, NDEV):
            g_all = g_all + p_ref[s]
        r = 0
        for k, (name, size) in enumerate(_SMALL):
            dst = [outs[kind * n + k] for kind in range(4)]

            def update(g, idx):
                vals = (g,) + _adam_math(w_refs[k][idx], g, m_refs[k][idx], v_refs[k][idx])
                for d, val in zip(dst, vals):
                    d[idx] = val

            if name == "sg_w":
                for grp in range(8):
                    update(g_all[r + grp * CH:r + (grp + 1) * CH, :], (0, grp))
            elif name == "sg_b":
                update(g_all[r:r + 8, :], (0,))
            elif name == "f_bias":
                update(g_all[r:r + 1, 0:8], (slice(None),))
            else:
                update(jnp.concatenate([g_all[r + q:r + q + 1, :] for q in range(size // 128)], axis=1), (slice(None),))
            r += _seg_rows(size)
        loss_ref[...] = g_all[r:r + 1, 0:1] * (0.5 / D)

    arrs = [parts] + [d[name] for d in (ws, ms, vs) for name in names]
    res = pl.pallas_call(
        body,
        name="adam_small",
        in_specs=[_full(a.shape) for a in arrs],
        out_specs=[_full((1, 1))] + [_full(ws[name].shape) for _ in range(4) for name in names],
        out_shape=[jax.ShapeDtypeStruct((1, 1), F32)] + [jax.ShapeDtypeStruct(ws[name].shape, F32) for _ in range(4) for name in names],
        compiler_params=pltpu.CompilerParams(vmem_limit_bytes=VMEM_LIMIT),
    )(*arrs)
    return res[0], {name: [res[1 + kind * n + k] for kind in range(4)] for k, name in enumerate(names)}


def kernel(x, p, w_in, f_bias, sg_ln_g, sg_ln_b, sg_w, sg_b, att_out_g, sg_out_g, w_out, pre_mix_g, post_mix_g, pre_ffn_g, post_ffn_g, w_ff1, w_ff2, ple_w, ple_gate_w, ple_gate_b, loss_target, m_w_in, m_f_bias, m_sg_ln_g, m_sg_ln_b, m_sg_w, m_sg_b, m_att_out_g, m_sg_out_g, m_w_out, m_pre_mix_g, m_post_mix_g, m_pre_ffn_g, m_post_ffn_g, m_w_ff1, m_w_ff2, m_ple_w, m_ple_gate_w, m_ple_gate_b, v_w_in, v_f_bias, v_sg_ln_g, v_sg_ln_b, v_sg_w, v_sg_b, v_att_out_g, v_sg_out_g, v_w_out, v_pre_mix_g, v_post_mix_g, v_pre_ffn_g, v_post_ffn_g, v_w_ff1, v_w_ff2, v_ple_w, v_ple_gate_w, v_ple_gate_b):
    small_w = dict(sg_w=sg_w, f_bias=f_bias, sg_ln_g=sg_ln_g, sg_ln_b=sg_ln_b, sg_b=sg_b, att_out_g=att_out_g,
                   sg_out_g=sg_out_g, pre_mix_g=pre_mix_g, post_mix_g=post_mix_g, pre_ffn_g=pre_ffn_g,
                   post_ffn_g=post_ffn_g, ple_gate_b=ple_gate_b)
    small_m = dict(sg_w=m_sg_w, f_bias=m_f_bias, sg_ln_g=m_sg_ln_g, sg_ln_b=m_sg_ln_b, sg_b=m_sg_b, att_out_g=m_att_out_g,
                   sg_out_g=m_sg_out_g, pre_mix_g=m_pre_mix_g, post_mix_g=m_post_mix_g, pre_ffn_g=m_pre_ffn_g,
                   post_ffn_g=m_post_ffn_g, ple_gate_b=m_ple_gate_b)
    small_v = dict(sg_w=v_sg_w, f_bias=v_f_bias, sg_ln_g=v_sg_ln_g, sg_ln_b=v_sg_ln_b, sg_b=v_sg_b, att_out_g=v_att_out_g,
                   sg_out_g=v_sg_out_g, pre_mix_g=v_pre_mix_g, post_mix_g=v_post_mix_g, pre_ffn_g=v_pre_ffn_g,
                   post_ffn_g=v_post_ffn_g, ple_gate_b=v_ple_gate_b)
    big = dict(w_in=(w_in, m_w_in, v_w_in), w_out=(w_out, m_w_out, v_w_out), w_ff1=(w_ff1, m_w_ff1, v_w_ff1),
               w_ff2=(w_ff2, m_w_ff2, v_w_ff2), ple_w=(ple_w, m_ple_w, v_ple_w),
               ple_gate_w=(ple_gate_w, m_ple_gate_w, v_ple_gate_w))

    xt, pt, tgt = x[0], p[0, 0], loss_target[0]
    ws = W_IN_COLS // NDEV

    gw_in = _gather(jnp.pad(jnp.transpose(w_in[0]).astype(BF16), ((0, WS_PAD - ws), (0, 0))), "gather_w_in")
    gw_in, others = lax.optimization_barrier((gw_in, [w_out[0].astype(BF16), w_ff1[0].astype(BF16), w_ff2[0].astype(BF16),
                                                      ple_w[0].astype(BF16), ple_gate_w[0].astype(BF16)]))
    rest = _xchg_start(others, True, "gather_rest_start")
    wint = gw_in[:, 0:ws, :].reshape(W_IN_COLS, D)
    wtp = jnp.concatenate([wint[0:NQKV], wint[NQKV + 8:W_IN_COLS], wint[NQKV:NQKV + 8],
                           jnp.zeros((FPAD - 8, D), BF16)], axis=0)

    fb = jnp.pad(f_bias.astype(F32), ((0, 0), (0, FPAD - 8)))
    sgw = sg_w[0]
    sgwt = jnp.transpose(sg_w[0], (0, 2, 1))
    sgbt = jnp.transpose(sg_b[0])

    qkv, kt, vt, us, fz, ab = _inproj(xt, pre_mix_g + rest[4][0, 0], wtp)
    ct, kk, vk = _fcum(fz, fb, qkv)
    yatt, lsec = _attn_fwd(qkv, kk, vt, ct)
    gw_out, gw1, gw2, gwpe, gwg = _xchg_wait(rest, True, yatt, "gather_rest_wait")
    wout = gw_out.reshape(D, D)
    wg = gwg.reshape(D, D)
    wpe = jnp.transpose(gwpe, (1, 0, 2)).reshape(PLE, D)
    h1, yb, o = _sgu_out(us, yatt, xt, sg_ln_g, sg_ln_b, sgw, sgbt, att_out_g, sg_out_g, wout, post_mix_g)
    c2b, f1b, ff, h2 = _ffn_fwd(h1, pre_ffn_g, gw1, gw2, post_ffn_g)
    dh2, dbg, loss_acc, g_g, g_pe = _ple_loss(h2, pt, tgt, wg, ple_gate_b, wpe)
    g_g = g_g.reshape(NDEV, D // NDEV, D)
    g_pe = jnp.transpose(g_pe.reshape(PLE, NDEV, D // NDEV), (1, 0, 2))

    dffb, df1, dh1, dgpostffn, dgpreffn = _ffn_bwd(dh2, ff, h1, f1b, gw1, gw2, post_ffn_g, pre_ffn_g)
    (g_1,) = _wgrad(c2b, [(df1, ("col", DFF // NDEV))], "wgrad_ff1")
    (g_2,) = _wgrad(f1b, [(dffb, ("row", DFF // NDEV))], "wgrad_ff2", a_fn=_sq_relu)
    early = _xchg_start([g_1, g_2, g_pe, g_g], False, "scatter_early_start")
    qw, dow, dus, g_out, dsgw, dsgbt, dlng, dlnb, dgatt, dgsg, dgpostmix = _mix_bwd(
        dh1, o, us, yatt, qkv, lsec, yb, sg_ln_g, sg_ln_b, sgw, sgwt, sgbt, att_out_g, sg_out_g, wout,
        post_mix_g + early[4][0, 0])
    mid = _xchg_start([g_out.reshape(NDEV, D // NDEV, D)], False, "scatter_mid_start")
    dqt, dcq, dk, dv, dck = _attn_bwd(kk, vk, kt, qw, dow, mid[4])
    dfz, dfb = _fgate_bwd(dcq, dck, fz, fb)

    gq, gk, gv, gus, gf = _wgrad(ab, [(dqt, "t"), (dk, None), (dv, None), (dus, None), (dfz, None)], "wgrad_in",
                                 out_dtype=BF16)
    g_in = jnp.concatenate([gq, gk, gv, gf[:, 0:8], gus], axis=1)
    g_in = jnp.transpose(g_in.reshape(D, NDEV, ws), (1, 0, 2))
    late = _xchg_start([g_in], False, "scatter_late_start")
    grad_x, dgpremix = _inproj_bwd(dqt, dk, dv, dus, dfz, wtp, xt, dh1, pre_mix_g + late[4][0, 0])

    small_g = dict(sg_w=dsgw, f_bias=dfb[:, 0:8], sg_ln_g=dlng, sg_ln_b=dlnb, sg_b=jnp.transpose(dsgbt[:, 0:8]),
                   att_out_g=dgatt, sg_out_g=dgsg, pre_mix_g=dgpremix, post_mix_g=dgpostmix, pre_ffn_g=dgpreffn,
                   post_ffn_g=dgpostffn, ple_gate_b=dbg)

    r_1, r_2, r_pe, r_g = _xchg_wait(early, False, grad_x, "scatter_early_wait")
    (r_out,) = _xchg_wait(mid, False, grad_x, "scatter_mid_wait")
    r_small = _gather(_pack(small_g, loss_acc), "gather_small_grads")

    res = {}

    def adam_big(name, parts):
        w, m, v = big[name]
        res[name] = [t[None] for t in _adam(parts, w[0], m[0], v[0], "adam_" + name)]

    for name, parts in (("w_out", r_out), ("w_ff1", r_1), ("w_ff2", r_2), ("ple_w", r_pe), ("ple_gate_w", r_g)):
        adam_big(name, parts)
    (r_in,) = _xchg_wait(late, False, res["w_ff1"][0], "scatter_late_wait")
    adam_big("w_in", r_in)
    loss, small = _adam_small(r_small, small_w, small_m, small_v)
    res.update(small)

    order = ["w_in", "f_bias", "sg_ln_g", "sg_ln_b", "sg_w", "sg_b", "att_out_g", "sg_out_g", "w_out", "pre_mix_g",
             "post_mix_g", "pre_ffn_g", "post_ffn_g", "w_ff1", "w_ff2", "ple_w", "ple_gate_w", "ple_gate_b"]
    outs = [loss[0, 0], grad_x[None]]
    for kind in range(4):
        outs += [res[name][kind] for name in order]
    return tuple(outs)
```

```python
import jax
import jax.numpy as jnp
from jax import lax
from jax.experimental import pallas as pl
from jax.experimental.pallas import tpu as pltpu

F32 = jnp.float32
BF16 = jnp.bfloat16

NDEV = 8
D = 1024
AW = 512
SW = 512
HD = 64
CH = 128
DFF = 4096
PLE = 256
NQKV = 3 * AW
NUS = 2 * SW
FPAD = 128
ZP = NQKV + NUS + FPAD
W_IN_COLS = 2568
WS_PAD = 336
EPS = 1e-6
MASKV = -1e30
GELU_K = 0.7978845608028654
GELU_C = 0.044715

ADAM_LR = 0.001
ADAM_B1 = 0.9
ADAM_B2 = 0.999
ADAM_EPS = 1e-08
ADAM_WD = 0.01
ADAM_STEP = 10

VMEM_LIMIT = 48 * 1024 * 1024
VMEM_LIMIT_BIG = 60 * 1024 * 1024


def _nn(a, b):
    return jnp.dot(a, b, preferred_element_type=F32)


def _nt(a, b):
    return lax.dot_general(a, b, (((1,), (1,)), ((), ())), preferred_element_type=F32)


def _tn(a, b):
    return lax.dot_general(a, b, (((0,), (0,)), ((), ())), preferred_element_type=F32)


def _tile(n, pref):
    return min(n, pref)


def _params(n_axes, vmem=VMEM_LIMIT):
    return pltpu.CompilerParams(dimension_semantics=("arbitrary",) * n_axes, vmem_limit_bytes=vmem)


def _full(shape):
    nd = len(shape)
    return pl.BlockSpec(shape, lambda *_: (0,) * nd)


def _rms_fwd(x, g):
    r = lax.rsqrt(jnp.mean(x * x, axis=-1, keepdims=True) + EPS)
    return x * r * g


def _rms_bwd(dy, x, g):
    n = x.shape[-1]
    r = lax.rsqrt(jnp.mean(x * x, axis=-1, keepdims=True) + EPS)
    u = dy * g
    s = jnp.sum(x * u, axis=-1, keepdims=True)
    dx = r * u - x * (r * r * r * (s * (1.0 / n)))
    dg = jnp.sum(dy * (x * r), axis=0, keepdims=True)
    return dx, dg


def _gelu(x):
    t = jnp.tanh(x * (GELU_K + (GELU_K * GELU_C) * (x * x)))
    return x * (0.5 + 0.5 * t), t


def _gelu_grad(x, t):
    return (0.5 + 0.5 * t) + (0.5 * x) * (1.0 - t * t) * (GELU_K + (3.0 * GELU_K * GELU_C) * (x * x))


def _gather(arr, name):
    def body(x_ref, out_ref, send, recv, loc):
        x, y, c = lax.axis_index("x"), lax.axis_index("y"), lax.axis_index("c")
        me, sibling = (x, y, c), (x, y, 1 - c)
        chips = [(1 - x, y), (x, 1 - y), (1 - x, 1 - y)]

        def slot(px, py, pc):
            return out_ref.at[4 * px + 2 * py + pc]

        def copy(k, block, to, src=None):
            return pltpu.make_async_remote_copy(
                src_ref=slot(*block) if src is None else src, dst_ref=slot(*block), send_sem=send.at[k],
                recv_sem=recv.at[k], device_id=to, device_id_type=pl.DeviceIdType.MESH)

        mine = pltpu.make_async_copy(x_ref, slot(*me), loc)
        mine.start()
        first = [copy(0, me, sibling, src=x_ref)] + [copy(1 + j, me, (*chip, c), src=x_ref) for j, chip in enumerate(chips)]
        for cp in first:
            cp.start()
        passed = [copy(4 + j, (*chip, c), sibling) for j, chip in enumerate(chips)]
        for j, chip in enumerate(chips):
            copy(1 + j, (*chip, c), me).wait_recv()
            passed[j].start()
        copy(0, sibling, me).wait_recv()
        for j, chip in enumerate(chips):
            copy(4 + j, (*chip, 1 - c), me).wait_recv()
        for cp in first + passed:
            cp.wait_send()
        mine.wait()

    hbm = pl.BlockSpec(memory_space=pltpu.HBM)
    return pl.pallas_call(
        body,
        name=name,
        out_shape=jax.ShapeDtypeStruct((NDEV,) + arr.shape, arr.dtype),
        in_specs=[hbm],
        out_specs=hbm,
        scratch_shapes=[pltpu.SemaphoreType.DMA((NDEV - 1,)), pltpu.SemaphoreType.DMA((NDEV - 1,)), pltpu.SemaphoreType.DMA],
    )(arr)


def _peers(x, y, c):
    out = []
    for k in range(1, NDEV):
        out.append((1 - x if (k >> 2) & 1 else x, 1 - y if (k >> 1) & 1 else y, 1 - c if k & 1 else c))
    return out


def _xchg_start(arrs, gather, name):
    n = len(arrs)
    me = 4 * lax.axis_index("x") + 2 * lax.axis_index("y") + lax.axis_index("c")
    lands = []
    for a in arrs:
        shape = ((NDEV,) + a.shape) if gather else a.shape
        own = a[None] if gather else lax.dynamic_slice_in_dim(a, me, 1, axis=0)
        lands.append(lax.dynamic_update_slice_in_dim(lax.empty(shape, a.dtype), own, me, axis=0))

    def body(*refs):
        ins, lnd = refs[:n], refs[n:2 * n]
        send, recv, token = refs[2 * n:3 * n], refs[3 * n:4 * n], refs[-1]
        x, y, c = lax.axis_index("x"), lax.axis_index("y"), lax.axis_index("c")
        mine = 4 * x + 2 * y + c
        for px, py, pc in _peers(x, y, c):
            peer = 4 * px + 2 * py + pc
            for a in range(n):
                pltpu.make_async_remote_copy(
                    src_ref=ins[a] if gather else ins[a].at[peer],
                    dst_ref=lnd[a].at[mine],
                    send_sem=send[a],
                    recv_sem=recv[a],
                    device_id=(px, py, pc),
                    device_id_type=pl.DeviceIdType.MESH,
                ).start()
        token[...] = jnp.zeros_like(token)

    hbm = pl.BlockSpec(memory_space=pltpu.HBM)
    sem = pl.BlockSpec(memory_space=pltpu.SEMAPHORE)
    res = pl.pallas_call(
        body,
        name=name,
        out_shape=(*[pltpu.SemaphoreType.DMA(())] * (2 * n),
                   *[pltpu.HBM(a.shape, a.dtype) for a in arrs], *[pltpu.HBM(l.shape, l.dtype) for l in lands],
                   jax.ShapeDtypeStruct((8, 128), F32)),
        in_specs=[hbm] * (2 * n),
        out_specs=(*([sem] * (2 * n)), *([hbm] * (2 * n)), pl.BlockSpec(memory_space=pltpu.VMEM)),
        input_output_aliases={i: 2 * n + i for i in range(2 * n)},
        compiler_params=pltpu.CompilerParams(has_side_effects=pltpu.SideEffectType.DATAFLOW_SIDE_EFFECTING),
    )(*[pltpu.with_memory_space_constraint(a, pltpu.HBM) for a in arrs],
      *[pltpu.with_memory_space_constraint(l, pltpu.HBM) for l in lands])
    return list(res[0:n]), list(res[n:2 * n]), list(res[2 * n:3 * n]), list(res[3 * n:4 * n]), res[-1]


def _xchg_wait(started, gather, after, name):
    send, recv, srcs, lands, _ = started
    n = len(srcs)

    def body(*refs):
        lnd = refs[n:2 * n]
        send, recv = refs[2 * n:3 * n], refs[3 * n:4 * n]
        me = (lax.axis_index("x"), lax.axis_index("y"), lax.axis_index("c"))
        for a in range(n):
            seven = lnd[a].at[pl.ds(0, NDEV - 1)]
            cp = pltpu.make_async_remote_copy(src_ref=seven, dst_ref=seven, send_sem=send[a], recv_sem=recv[a],
                                              device_id=me, device_id_type=pl.DeviceIdType.MESH)
            cp.wait_send()
            cp.wait_recv()

    hbm = pl.BlockSpec(memory_space=pltpu.HBM)
    sem = pl.BlockSpec(memory_space=pltpu.SEMAPHORE)
    res = pl.pallas_call(
        body,
        name=name,
        out_shape=tuple([pltpu.HBM(a.shape, a.dtype) for a in srcs] + [pltpu.HBM(l.shape, l.dtype) for l in lands]),
        in_specs=[hbm] * (2 * n) + [sem] * (2 * n) + [pl.BlockSpec(memory_space=pl.ANY)],
        out_specs=tuple([hbm] * (2 * n)),
        input_output_aliases={i: i for i in range(2 * n)},
        compiler_params=pltpu.CompilerParams(has_side_effects=pltpu.SideEffectType.DATAFLOW_SIDE_EFFECTING),
    )(*srcs, *lands, *send, *recv, after)
    return list(res[n:])


def _inproj(x, g, wtp):
    T = x.shape[0]
    tm = _tile(T, 512)

    def body(x_ref, g_ref, w_ref, qkv_ref, kt_ref, vt_ref, us_ref, fz_ref, ab_ref):
        a = _rms_fwd(x_ref[...], g_ref[...]).astype(BF16)
        ab_ref[...] = a
        qkv_ref[:, 0:AW] = _nt(a, w_ref[0:AW, :]).astype(BF16)
        kk = _nt(a, w_ref[AW:2 * AW, :])
        qkv_ref[:, AW:2 * AW] = kk.astype(BF16)
        kt_ref[...] = kk.T.astype(BF16)
        vv = _nt(a, w_ref[2 * AW:NQKV, :])
        qkv_ref[:, 2 * AW:NQKV] = vv.astype(BF16)
        vvt = vv.T.astype(BF16)
        one_row = jnp.where(lax.broadcasted_iota(jnp.int32, (HD, tm), 0) == 0, 1.0, 0.0).astype(BF16)
        for h in range(AW // HD):
            vt_ref[2 * h * HD:(2 * h + 1) * HD, :] = vvt[h * HD:(h + 1) * HD, :]
            vt_ref[(2 * h + 1) * HD:(2 * h + 2) * HD, :] = one_row
        us_ref[...] = _nt(a, w_ref[NQKV:NQKV + NUS, :])
        fz_ref[...] = _nt(a, w_ref[NQKV + NUS:ZP, :])

    row = lambda n: pl.BlockSpec((tm, n), lambda i: (i, 0))
    col = lambda n: pl.BlockSpec((n, tm), lambda i: (0, i))
    return pl.pallas_call(
        body,
        name="inproj",
        grid=(T // tm,),
        in_specs=[row(D), _full((1, D)), _full((ZP, D))],
        out_specs=[row(NQKV), col(AW), col(2 * AW), row(NUS), row(FPAD), row(D)],
        out_shape=[
            jax.ShapeDtypeStruct((T, NQKV), BF16),
            jax.ShapeDtypeStruct((AW, T), BF16),
            jax.ShapeDtypeStruct((2 * AW, T), BF16),
            jax.ShapeDtypeStruct((T, NUS), F32),
            jax.ShapeDtypeStruct((T, FPAD), F32),
            jax.ShapeDtypeStruct((T, D), BF16),
        ],
        compiler_params=_params(1),
    )(x, g, wtp)


def _log_sigmoid(z):
    return jnp.minimum(z, 0.0) - jnp.log1p(jnp.exp(-jnp.abs(z)))


def _split3(x):
    hi = x.astype(BF16)
    r1 = x - hi.astype(F32)
    mid = r1.astype(BF16)
    lo = (r1 - mid.astype(F32)).astype(BF16)
    return hi, mid, lo


AUG_A, AUG_B, AUG_ONE = 0, 3, 6


def _aug_lanes(rows, first_one, pieces):
    lane = lax.broadcasted_iota(jnp.int32, (rows, 128), 1)
    out = jnp.zeros((rows, 128), F32)
    if first_one is not None:
        out = jnp.where((lane >= first_one) & (lane < first_one + 3), 1.0, out)
    for n, piece in enumerate(pieces):
        out = jnp.where(lane == AUG_ONE + n, piece.astype(F32), out)
    return out.astype(BF16)


def _fcum(fz, fb, qkv):
    T = fz.shape[0]
    tb = _tile(T, 512)

    def body(fz_ref, fb_ref, k_ref, v_ref, ct_ref, kk_ref, vk_ref, carry, tri_s, sel_s):
        @pl.when(pl.program_id(0) == 0)
        def _():
            carry[...] = jnp.zeros_like(carry)
            src = lax.broadcasted_iota(jnp.int32, (128, 128), 0)
            dst = lax.broadcasted_iota(jnp.int32, (128, 128), 1)
            tri_s[...] = (dst <= src).astype(F32)
            for hp in range(4):
                for n in range(3):
                    pick = ((src == 2 * hp) & (dst == AUG_A + n)) | ((src == 2 * hp + 1) & (dst == AUG_B + n))
                    sel_s[3 * hp + n] = jnp.where(pick, -1.0, 0.0).astype(BF16)

        lf = _log_sigmoid(fz_ref[...] + fb_ref[...])
        run = carry[...]
        parts = []
        for blk in range(tb // 128):
            lf_b = lf[blk * 128:(blk + 1) * 128, :]
            parts.append(jnp.dot(tri_s[...], lf_b, precision=lax.Precision.HIGHEST, preferred_element_type=F32) + run)
            run = run + jnp.sum(lf_b, axis=0, keepdims=True)
        carry[...] = run
        cs = jnp.concatenate(parts, axis=0)
        ct_ref[...] = cs.T[0:8, :]

        pieces = _split3(cs)
        lane = lax.broadcasted_iota(jnp.int32, (1, 128), 1)
        ones = jnp.where((lane >= AUG_ONE) & (lane < AUG_ONE + 3), 1.0, 0.0)
        for hp in range(4):
            aug = jnp.zeros((tb, 128), F32) + ones
            for n, piece in enumerate(pieces):
                aug = aug + _nn(piece, sel_s[3 * hp + n])
            aug = aug.astype(BF16)
            kk_ref[:, hp * 256:hp * 256 + 128] = k_ref[:, hp * 128:(hp + 1) * 128]
            kk_ref[:, hp * 256 + 128:(hp + 1) * 256] = aug
            vk_ref[:, hp * 256:hp * 256 + 128] = v_ref[:, hp * 128:(hp + 1) * 128]
            vk_ref[:, hp * 256 + 128:(hp + 1) * 256] = aug

    row = lambda n, c: pl.BlockSpec((tb, n), lambda i: (i, c))
    return pl.pallas_call(
        body,
        name="fcum",
        grid=(T // tb,),
        in_specs=[row(FPAD, 0), _full((1, FPAD)), row(AW, 1), row(AW, 2)],
        out_specs=[pl.BlockSpec((8, tb), lambda i: (0, i)), row(2 * AW, 0), row(2 * AW, 0)],
        out_shape=[jax.ShapeDtypeStruct((8, T), F32), jax.ShapeDtypeStruct((T, 2 * AW), BF16),
                   jax.ShapeDtypeStruct((T, 2 * AW), BF16)],
        scratch_shapes=[pltpu.VMEM((1, FPAD), F32), pltpu.VMEM((128, 128), F32), pltpu.VMEM((12, 128, 128), BF16)],
        compiler_params=_params(1),
    )(fz, fb, qkv, qkv)


def _fold_rows(r, t, nq):
    if nq == 1:
        return r, t
    low = t <= r
    return jnp.where(low, r, nq - 1 - r), jnp.where(low, t, t - r - 1)


def _fold_cols(r, t, nq):
    if nq == 1:
        return r, t
    first = t < nq - r
    j = jnp.where(first, r, nq - 1 - r)
    return j, jnp.where(first, r + t, nq - 1 - r + (t - (nq - r)))


PAIRS = 4


def _fold_grid(nq):
    assert (nq == 1 or nq % 2 == 0) and PAIRS == 4
    return (4 // PAIRS, 1, 1) if nq == 1 else (4 // PAIRS, nq // 2, nq + 1)


def _head_rows(x2, hh, scale):
    is_a = lax.broadcasted_iota(jnp.int32, (1, 128), 1) < HD
    keep = is_a if hh == 0 else jnp.logical_not(is_a)
    return jnp.where(keep, x2, jnp.zeros_like(x2)) * scale


def _attn_fwd(qkv, kk, vt, ct):
    T = qkv.shape[0]
    tq = _tile(T, 512)
    tk = tq
    nq = T // tq

    def body(q_ref, kk_ref, vt_ref, ctq_ref, o_ref, lsec_ref, qw_s, m_s, acc_s):
        i, j = _fold_rows(pl.program_id(1), pl.program_id(2), nq)
        sub8 = lax.broadcasted_iota(jnp.int32, (8, 1), 0)

        def cref_of(pp, hh):
            head = 2 * (PAIRS * pl.program_id(0) + pp) + hh
            return jnp.sum(jnp.where(sub8 == head, ctq_ref[:, 0:1], 0.0), axis=0, keepdims=True)

        @pl.when(j == 0)
        def _():
            for pp in range(PAIRS):
                q2 = q_ref[:, pp * 128:(pp + 1) * 128]
                for hh in range(2):
                    rows = slice(hh * tq, (hh + 1) * tq)
                    qw_s[pp, rows, 0:128] = _head_rows(q2, hh, 0.125)
                    qw_s[pp, rows, 128:256] = _aug_lanes(tq, AUG_A if hh == 0 else AUG_B, _split3(cref_of(pp, hh)))
            m_s[...] = jnp.full_like(m_s, MASKV)
            acc_s[...] = jnp.zeros_like(acc_s)

        def step(masked):
            if masked:
                causal = lax.broadcasted_iota(jnp.int32, (tk, tq), 0) <= lax.broadcasted_iota(jnp.int32, (tk, tq), 1)
            logits = lambda pp: _nt(kk_ref[:, pp * 256:(pp + 1) * 256], qw_s[pp])
            s_next = logits(0)
            for pp in range(PAIRS):
                s2 = s_next
                if pp + 1 < PAIRS:
                    s_next = logits(pp + 1)
                for hh in range(2):
                    n = 2 * pp + hh
                    s = s2[:, hh * tq:(hh + 1) * tq]
                    if masked:
                        s = jnp.where(causal, s, MASKV)
                    m_prev = m_s[n]
                    m_new = jnp.maximum(m_prev, jnp.max(s, axis=0, keepdims=True))
                    pr = jnp.exp(s - m_new)
                    m_s[n] = m_new
                    acc_s[n] = jnp.exp(m_prev - m_new) * acc_s[n] + _nn(vt_ref[n * 128:(n + 1) * 128, :], pr.astype(BF16))

        @pl.when(j < i)
        def _():
            step(False)

        @pl.when(j == i)
        def _():
            step(True)
            sub = lax.broadcasted_iota(jnp.int32, (128, 1), 0)
            for pp in range(PAIRS):
                outs, lrow = [], []
                for hh in range(2):
                    n = 2 * pp + hh
                    den = acc_s[n, HD:HD + 1, :]
                    outs.append(acc_s[n, 0:HD, :] * (1.0 / den))
                    lrow.append(m_s[n] + jnp.log(den) - cref_of(pp, hh))
                o_ref[:, pp * 128:(pp + 1) * 128] = jnp.concatenate(outs, axis=0).T
                lsec_ref[:, pp * 128:(pp + 1) * 128] = jnp.where(sub == 0, lrow[0], jnp.where(sub == 1, lrow[1], 0.0)).T

    qi = lambda r, t: _fold_rows(r, t, nq)[0]
    kj = lambda r, t: _fold_rows(r, t, nq)[1]
    return pl.pallas_call(
        body,
        name="attn_fwd",
        grid=_fold_grid(nq),
        in_specs=[
            pl.BlockSpec((tq, PAIRS * 128), lambda g, r, t: (qi(r, t), g)),
            pl.BlockSpec((tk, PAIRS * 256), lambda g, r, t: (kj(r, t), g)),
            pl.BlockSpec((PAIRS * 256, tk), lambda g, r, t: (g, kj(r, t))),
            pl.BlockSpec((8, tq), lambda g, r, t: (0, qi(r, t))),
        ],
        out_specs=[pl.BlockSpec((tq, PAIRS * 128), lambda g, r, t: (qi(r, t), g))] * 2,
        out_shape=[jax.ShapeDtypeStruct((T, AW), F32)] * 2,
        scratch_shapes=[pltpu.VMEM((PAIRS, 2 * tq, 256), BF16), pltpu.VMEM((2 * PAIRS, 1, tq), F32),
                        pltpu.VMEM((2 * PAIRS, 128, tq), F32)],
        compiler_params=_params(3),
    )(qkv, kk, vt, ct)


def _sgu_forward(us_ref, lng, lnb, w_ref, bt_ref, mixed_s, vnb_s, tm):
    is_a = lax.broadcasted_iota(jnp.int32, (1, 128), 1) < HD
    u = us_ref[:, 0:SW]
    vs = us_ref[:, SW:NUS]
    ug, tu = _gelu(u)
    vg, tv = _gelu(vs)
    mu = jnp.mean(vg, axis=-1, keepdims=True)
    xc = vg - mu
    rstd = lax.rsqrt(jnp.mean(xc * xc, axis=-1, keepdims=True) + EPS)
    vhat = xc * rstd
    vnb_s[...] = (vhat * lng + lnb).astype(BF16)
    rr = lax.broadcasted_iota(jnp.int32, (CH, CH), 0)
    cc = lax.broadcasted_iota(jnp.int32, (CH, CH), 1)
    tril = cc <= rr
    for jj in range(4):
        wa = jnp.where(tril, w_ref[2 * jj], 0.0).astype(BF16)
        wb = jnp.where(tril, w_ref[2 * jj + 1], 0.0).astype(BF16)
        ba = bt_ref[:, 2 * jj:2 * jj + 1]
        bb = bt_ref[:, 2 * jj + 1:2 * jj + 2]
        for ch in range(tm // CH):
            rs, cs = slice(ch * CH, (ch + 1) * CH), slice(jj * 128, (jj + 1) * 128)
            vn2 = vnb_s[rs, cs]
            mixed_s[rs, cs] = jnp.where(is_a, _nn(wa, vn2) + ba, _nn(wb, vn2) + bb)
    mixed = mixed_s[...]
    return u, vs, ug, tu, tv, vhat, rstd, mixed, ug * mixed


def _sgu_out(us, yatt, x, lng, lnb, sgw, sgbt, gatt, gsg, wout, gpm):
    T = us.shape[0]
    tm = _tile(T, 512)

    def body(us_ref, ya_ref, x_ref, lng_ref, lnb_ref, w_ref, bt_ref, ga_ref, gs_ref, wo_ref, gp_ref,
             h1_ref, yb_ref, o_ref, mixed_s, vnb_s):
        ysg = _sgu_forward(us_ref, lng_ref[...], lnb_ref[...], w_ref, bt_ref, mixed_s, vnb_s, tm)[-1]
        yb_ref[:, 0:AW] = _rms_fwd(ya_ref[...], ga_ref[...]).astype(BF16)
        yb_ref[:, AW:D] = _rms_fwd(ysg, gs_ref[...]).astype(BF16)
        o = _nn(yb_ref[...], wo_ref[...])
        o_ref[...] = o
        h1_ref[...] = x_ref[...] + _rms_fwd(o, gp_ref[...])

    row = lambda n: pl.BlockSpec((tm, n), lambda i: (i, 0))
    return pl.pallas_call(
        body,
        name="sgu_out",
        grid=(T // tm,),
        in_specs=[row(NUS), row(AW), row(D), _full((1, SW)), _full((1, SW)), _full((8, CH, CH)), _full((CH, 8)),
                  _full((1, AW)), _full((1, SW)), _full((D, D)), _full((1, D))],
        out_specs=[row(D), row(D), row(D)],
        out_shape=[jax.ShapeDtypeStruct((T, D), F32), jax.ShapeDtypeStruct((T, D), BF16), jax.ShapeDtypeStruct((T, D), F32)],
        scratch_shapes=[pltpu.VMEM((tm, SW), F32), pltpu.VMEM((tm, SW), BF16)],
        compiler_params=_params(1),
    )(us, yatt, x, lng, lnb, sgw, sgbt, gatt, gsg, wout, gpm)


def _ffn_fwd(h1, gpre, w1g, w2g, gpost):
    T = h1.shape[0]
    tm = _tile(T, 512)
    nb, hb = w1g.shape[0], w1g.shape[2]

    def body(h1_ref, gpre_ref, w1_ref, w2_ref, gpost_ref, c2_ref, f1_ref, ff_ref, h2_ref):
        h1 = h1_ref[...]
        c2 = _rms_fwd(h1, gpre_ref[...]).astype(BF16)
        c2_ref[...] = c2
        ff = jnp.zeros((tm, D), F32)
        for j in range(nb):
            f1 = _nn(c2, w1_ref[j])
            f1_ref[:, j * hb:(j + 1) * hb] = f1.astype(BF16)
            r = jnp.maximum(f1, 0.0)
            ff = ff + _nn((r * r).astype(BF16), w2_ref[j])
        ff_ref[...] = ff
        h2_ref[...] = h1 + _rms_fwd(ff, gpost_ref[...])

    row = lambda n: pl.BlockSpec((tm, n), lambda i: (i, 0))
    once = lambda shape: pl.BlockSpec(shape, lambda i: (0,) * len(shape), pipeline_mode=pl.Buffered(1))
    return pl.pallas_call(
        body,
        name="ffn_fwd",
        grid=(T // tm,),
        in_specs=[row(D), _full((1, D)), once((nb, D, hb)), once((nb, hb, D)), _full((1, D))],
        out_specs=[row(D), row(DFF), row(D), row(D)],
        out_shape=[jax.ShapeDtypeStruct((T, D), BF16), jax.ShapeDtypeStruct((T, DFF), BF16),
                   jax.ShapeDtypeStruct((T, D), F32), jax.ShapeDtypeStruct((T, D), F32)],
        compiler_params=_params(1, VMEM_LIMIT_BIG),
    )(h1, gpre, w1g, w2g, gpost)


def _ple_loss(h2, p, tgt, wg, bg, wpe):
    T = h2.shape[0]
    tm = _tile(T, 512)

    def body(h2_ref, p_ref, t_ref, wg_ref, bg_ref, wpe_ref, dh2_ref, dbg_ref, loss_ref, dwg_ref, dwpe_ref):
        @pl.when(pl.program_id(0) == 0)
        def _():
            for r in (dbg_ref, loss_ref, dwg_ref, dwpe_ref):
                r[...] = jnp.zeros_like(r)

        h2 = h2_ref[...]
        h2b = h2.astype(BF16)
        gate = jax.nn.sigmoid(_nn(h2b, wg_ref[...]) + bg_ref[...])
        pb = p_ref[...].astype(BF16)
        pe = _nn(pb, wpe_ref[...])
        diff = (h2 + gate * pe) - t_ref[...]
        loss_ref[...] += jnp.sum(diff * diff)
        dh3 = diff * (1.0 / D)
        dpre = (dh3 * pe) * (gate * (1.0 - gate))
        dpre_b = dpre.astype(BF16)
        dbg_ref[...] += jnp.sum(dpre, axis=0, keepdims=True)
        dh2_ref[...] = dh3 + _nt(dpre_b, wg_ref[...])
        dwg_ref[...] += _nn(h2b.T, dpre_b)
        dwpe_ref[...] += _nn(pb.T, (dh3 * gate).astype(BF16))

    row = lambda n: pl.BlockSpec((tm, n), lambda i: (i, 0))
    once = lambda shape: pl.BlockSpec(shape, lambda i: (0,) * len(shape), pipeline_mode=pl.Buffered(1))
    return pl.pallas_call(
        body,
        name="ple_loss",
        grid=(T // tm,),
        in_specs=[row(D), row(PLE), row(D), _full((D, D)), _full((1, D)), _full((PLE, D))],
        out_specs=[row(D), _full((1, D)), _full((8, 128)), once((D, D)), once((PLE, D))],
        out_shape=[jax.ShapeDtypeStruct((T, D), F32), jax.ShapeDtypeStruct((1, D), F32),
                   jax.ShapeDtypeStruct((8, 128), F32), jax.ShapeDtypeStruct((D, D), F32),
                   jax.ShapeDtypeStruct((PLE, D), F32)],
        compiler_params=_params(1),
    )(h2, p, tgt, wg, bg, wpe)


def _ffn_bwd(dh2, ff, h1, f1, w1g, w2g, gpost, gpre):
    T = dh2.shape[0]
    tm = _tile(T, 512)
    nb, hb = w1g.shape[0], w1g.shape[2]

    def body(dh2_ref, ff_ref, h1_ref, f1_ref, w1_ref, w2_ref, gpost_ref, gpre_ref,
             dffb_ref, df1_ref, dh1_ref, dgpost_ref, dgpre_ref):
        @pl.when(pl.program_id(0) == 0)
        def _():
            dgpost_ref[...] = jnp.zeros_like(dgpost_ref)
            dgpre_ref[...] = jnp.zeros_like(dgpre_ref)

        dh2 = dh2_ref[...]
        dff, dg = _rms_bwd(dh2, ff_ref[...], gpost_ref[...])
        dffb = dff.astype(BF16)
        dffb_ref[...] = dffb
        dgpost_ref[...] += dg
        dc2 = jnp.zeros((tm, D), F32)
        for j in range(nb):
            cols = slice(j * hb, (j + 1) * hb)
            dact = _nt(dffb, w2_ref[j])
            df1 = (dact * (2.0 * jnp.maximum(f1_ref[:, cols].astype(F32), 0.0))).astype(BF16)
            df1_ref[:, cols] = df1
            dc2 = dc2 + _nt(df1, w1_ref[j])
        dx, dg = _rms_bwd(dc2, h1_ref[...], gpre_ref[...])
        dh1_ref[...] = dh2 + dx
        dgpre_ref[...] += dg

    row = lambda n: pl.BlockSpec((tm, n), lambda i: (i, 0))
    once = lambda shape: pl.BlockSpec(shape, lambda i: (0,) * len(shape), pipeline_mode=pl.Buffered(1))
    return pl.pallas_call(
        body,
        name="ffn_bwd",
        grid=(T // tm,),
        in_specs=[row(D), row(D), row(D), row(DFF), once((nb, D, hb)), once((nb, hb, D)), _full((1, D)), _full((1, D))],
        out_specs=[row(D), row(DFF), row(D), _full((1, D)), _full((1, D))],
        out_shape=[jax.ShapeDtypeStruct((T, D), BF16), jax.ShapeDtypeStruct((T, DFF), BF16),
                   jax.ShapeDtypeStruct((T, D), F32), jax.ShapeDtypeStruct((1, D), F32),
                   jax.ShapeDtypeStruct((1, D), F32)],
        compiler_params=_params(1, VMEM_LIMIT_BIG),
    )(dh2, ff, h1, f1, w1g, w2g, gpost, gpre)


def _mix_bwd(dh1, o, us, yatt, qkv, lsec, yb, lng, lnb, sgw, sgwt, sgbt, gatt, gsg, wout, gpm):
    T = dh1.shape[0]
    tm = _tile(T, 512)

    def body(dh1_ref, o_ref, us_ref, ya_ref, q_ref, l_ref, yb_ref, lng_ref, lnb_ref, w_ref, wt_ref, bt_ref, ga_ref, gs_ref,
             wo_ref, gp_ref, qw_ref, dow_ref, dus_ref, dwo_ref, dw_ref, dbt_ref, dlng_ref, dlnb_ref, dga_ref, dgs_ref,
             dgp_ref, mixed_s, vnb_s, dvn_s):
        @pl.when(pl.program_id(0) == 0)
        def _():
            for r in (dwo_ref, dw_ref, dbt_ref, dlng_ref, dlnb_ref, dga_ref, dgs_ref, dgp_ref):
                r[...] = jnp.zeros_like(r)

        is_a = lax.broadcasted_iota(jnp.int32, (1, 128), 1) < HD
        lane = lax.broadcasted_iota(jnp.int32, (1, 128), 1)
        do, dg = _rms_bwd(dh1_ref[...], o_ref[...], gp_ref[...])
        dgp_ref[...] += dg
        dob = do.astype(BF16)
        dwo_ref[...] += _nn(yb_ref[...].T, dob)
        dy = _nt(dob, wo_ref[...])
        ya = ya_ref[...]
        datt, dg = _rms_bwd(dy[:, 0:AW], ya, ga_ref[...])
        dga_ref[...] += dg
        _attn_operands(q_ref[...], datt, ya, l_ref[...], qw_ref, dow_ref)

        lng = lng_ref[...]
        u, vs, ug, tu, tv, vhat, rstd, mixed, ysg = _sgu_forward(us_ref, lng, lnb_ref[...], w_ref, bt_ref, mixed_s, vnb_s, tm)
        dysg, dg = _rms_bwd(dy[:, AW:D], ysg, gs_ref[...])
        dgs_ref[...] += dg
        dus_ref[:, 0:SW] = ((dysg * mixed) * _gelu_grad(u, tu)).astype(BF16)
        dmix = dysg * ug

        rr = lax.broadcasted_iota(jnp.int32, (CH, CH), 0)
        cc = lax.broadcasted_iota(jnp.int32, (CH, CH), 1)
        tril = cc <= rr
        triu = cc >= rr
        for jj in range(4):
            wta = jnp.where(triu, wt_ref[2 * jj], 0.0).astype(BF16)
            wtb = jnp.where(triu, wt_ref[2 * jj + 1], 0.0).astype(BF16)
            for ch in range(tm // CH):
                rs, cs = slice(ch * CH, (ch + 1) * CH), slice(jj * 128, (jj + 1) * 128)
                dm2 = dmix[rs, cs]
                dma = jnp.where(is_a, dm2, 0.0)
                dmb = jnp.where(is_a, 0.0, dm2)
                dma_b, dmb_b = dma.astype(BF16), dmb.astype(BF16)
                vn2 = vnb_s[rs, cs]
                dw_ref[2 * jj] += jnp.where(tril, _nt(dma_b, vn2), 0.0)
                dw_ref[2 * jj + 1] += jnp.where(tril, _nt(dmb_b, vn2), 0.0)
                dvn_s[rs, cs] = _nn(wta, dma_b) + _nn(wtb, dmb_b)
                dba = jnp.sum(dma, axis=1, keepdims=True)
                dbb = jnp.sum(dmb, axis=1, keepdims=True)
                dbt_ref[...] += jnp.where(lane == 2 * jj, dba, 0.0) + jnp.where(lane == 2 * jj + 1, dbb, 0.0)

        dvn = dvn_s[...]
        dlng_ref[...] += jnp.sum(dvn * vhat, axis=0, keepdims=True)
        dlnb_ref[...] += jnp.sum(dvn, axis=0, keepdims=True)
        dvh = dvn * lng
        dvg = rstd * (dvh - jnp.mean(dvh, axis=-1, keepdims=True) - vhat * jnp.mean(dvh * vhat, axis=-1, keepdims=True))
        dus_ref[:, SW:NUS] = (dvg * _gelu_grad(vs, tv)).astype(BF16)

    row = lambda n: pl.BlockSpec((tm, n), lambda i: (i, 0))
    return pl.pallas_call(
        body,
        name="mix_bwd",
        grid=(T // tm,),
        in_specs=[row(D), row(D), row(NUS), row(AW), row(AW), row(AW), row(D), _full((1, SW)), _full((1, SW)),
                  _full((8, CH, CH)), _full((8, CH, CH)), _full((CH, 8)), _full((1, AW)), _full((1, SW)), _full((D, D)),
                  _full((1, D))],
        out_specs=[row(4 * AW), row(4 * AW), row(NUS),
                   pl.BlockSpec((D, D), lambda i: (0, 0), pipeline_mode=pl.Buffered(1)), _full((8, CH, CH)),
                   _full((CH, 128)), _full((1, SW)), _full((1, SW)), _full((1, AW)), _full((1, SW)), _full((1, D))],
        out_shape=[jax.ShapeDtypeStruct((T, 4 * AW), BF16),
                   jax.ShapeDtypeStruct((T, 4 * AW), BF16), jax.ShapeDtypeStruct((T, NUS), BF16),
                   jax.ShapeDtypeStruct((D, D), F32),
                   jax.ShapeDtypeStruct((8, CH, CH), F32), jax.ShapeDtypeStruct((CH, 128), F32),
                   jax.ShapeDtypeStruct((1, SW), F32), jax.ShapeDtypeStruct((1, SW), F32),
                   jax.ShapeDtypeStruct((1, AW), F32), jax.ShapeDtypeStruct((1, SW), F32), jax.ShapeDtypeStruct((1, D), F32)],
        scratch_shapes=[pltpu.VMEM((tm, SW), F32), pltpu.VMEM((tm, SW), BF16), pltpu.VMEM((tm, SW), F32)],
        compiler_params=_params(1, VMEM_LIMIT_BIG),
    )(dh1, o, us, yatt, qkv, lsec, yb, lng, lnb, sgw, sgwt, sgbt, gatt, gsg, wout, gpm)


def _attn_operands(q, do, o, lsec, qw_ref, dow_ref):
    rows = do.shape[0]
    feat = lax.broadcasted_iota(jnp.int32, (AW, 128), 0)
    head = lax.broadcasted_iota(jnp.int32, (AW, 128), 1)
    sel = jnp.where((feat >= head * HD) & (feat < (head + 1) * HD), 1.0, 0.0)
    delta = jnp.dot(do * o, sel, precision=lax.Precision.HIGHEST, preferred_element_type=F32)
    for hp in range(4):
        cols = slice(hp * 128, (hp + 1) * 128)
        for hh in range(2):
            base = (2 * hp + hh) * 256
            lc = lsec[:, hp * 128 + hh:hp * 128 + hh + 1]
            d_h = delta[:, 2 * hp + hh:2 * hp + hh + 1]
            qw_ref[:, base:base + 128] = _head_rows(q[:, cols], hh, 0.125)
            qw_ref[:, base + 128:base + 256] = _aug_lanes(rows, AUG_A if hh == 0 else AUG_B, _split3(-lc))
            dow_ref[:, base:base + 128] = _head_rows(do[:, cols], hh, 1.0).astype(BF16)
            dow_ref[:, base + 128:base + 256] = _aug_lanes(rows, None, _split3(-d_h))


def _attn_bwd(kk, vk, kt, qw, dow, after):
    T = kk.shape[0]
    tq = _tile(T, 512)
    tk = tq
    nq = T // tq

    def body(kk_ref, vk_ref, kt_ref, qw_ref, dow_ref, after_ref, dqt_ref, dcq_ref, dk_ref, dv_ref, dck_ref, dk_s, dv_s, dck_s):
        j, i = _fold_cols(pl.program_id(1), pl.program_id(2), nq)
        sub8 = lax.broadcasted_iota(jnp.int32, (8, 1), 0)
        lane = lax.broadcasted_iota(jnp.int32, (1, 128), 1)

        @pl.when((pl.program_id(1) == 0) & (pl.program_id(2) == 0))
        def _():
            dqt_ref[...] = jnp.zeros_like(dqt_ref)
            dcq_ref[...] = jnp.zeros_like(dcq_ref)

        @pl.when(i == j)
        def _():
            dk_s[...] = jnp.zeros_like(dk_s)
            dv_s[...] = jnp.zeros_like(dv_s)
            dck_s[...] = jnp.zeros_like(dck_s)

        def step(masked):
            cols = pl.ds(pl.multiple_of(i * tq, tq), tq)
            sub = lax.broadcasted_iota(jnp.int32, (128, 1), 0)
            if masked:
                causal = lax.broadcasted_iota(jnp.int32, (tk, tq), 0) <= lax.broadcasted_iota(jnp.int32, (tk, tq), 1)

            def logits(n):
                pair, base = n // 2, n * 256
                return (_nt(kk_ref[:, pair * 256:(pair + 1) * 256], qw_ref[:, base:base + 256]),
                        _nt(vk_ref[:, pair * 256:(pair + 1) * 256], dow_ref[:, base:base + 256]))

            ahead = logits(0)
            dcq = jnp.zeros((8, tq), F32)
            dck = jnp.zeros((tk, 128), F32)
            for pp in range(PAIRS):
                lanes = slice(pp * 128, (pp + 1) * 128)
                kt2 = kt_ref[lanes, :] * 0.125
                dv = jnp.zeros((tk, 128), F32)
                dk = jnp.zeros((tk, 128), F32)
                dqts = []
                for hh in range(2):
                    head = 2 * pp + hh
                    base = head * 256
                    qw_h = qw_ref[:, base:base + 256]
                    dow_h = dow_ref[:, base:base + 256]
                    logp, dp = ahead
                    if head + 1 < 2 * PAIRS:
                        ahead = logits(head + 1)
                    pr = jnp.exp(logp)
                    if masked:
                        pr = jnp.where(causal, pr, 0.0)
                    ds = pr * dp
                    ds_b = ds.astype(BF16)
                    dv = dv + _nn(pr.astype(BF16), dow_h[:, 0:128])
                    dk = dk + _nn(ds_b, qw_h[:, 0:128])
                    dqts.append(_nn(kt2, ds_b))
                    dcq = dcq + jnp.where(sub8 == head, jnp.sum(ds, axis=0, keepdims=True), 0.0)
                    dck = dck - jnp.where(lane == head, jnp.sum(ds, axis=1, keepdims=True), 0.0)
                dv_s[:, lanes] += dv
                dk_s[:, lanes] += dk
                dqt_ref[lanes, cols] += jnp.where(sub < HD, dqts[0], dqts[1])
            dck_s[...] += dck
            dcq_ref[:, cols] += dcq

        @pl.when(i > j)
        def _():
            step(False)

        @pl.when(i == j)
        def _():
            step(True)

        @pl.when(i == nq - 1)
        def _():
            dk_ref[...] = dk_s[...].astype(BF16)
            dv_ref[...] = dv_s[...].astype(BF16)
            dck_ref[...] = dck_s[...]

    kj = lambda r, t: _fold_cols(r, t, nq)[0]
    qi = lambda r, t: _fold_cols(r, t, nq)[1]
    krow = lambda g, r, t: (kj(r, t), g)
    qrow = lambda g, r, t: (qi(r, t), g)
    return pl.pallas_call(
        body,
        name="attn_bwd",
        grid=_fold_grid(nq),
        in_specs=[
            pl.BlockSpec((tk, PAIRS * 256), krow),
            pl.BlockSpec((tk, PAIRS * 256), krow),
            pl.BlockSpec((PAIRS * 128, tk), lambda g, r, t: (g, kj(r, t))),
            pl.BlockSpec((tq, PAIRS * 512), qrow),
            pl.BlockSpec((tq, PAIRS * 512), qrow),
            pl.BlockSpec(after.shape, lambda g, r, t: (0,) * after.ndim),
        ],
        out_specs=[
            pl.BlockSpec((PAIRS * 128, T), lambda g, r, t: (g, 0), pipeline_mode=pl.Buffered(1)),
            pl.BlockSpec((8, T), lambda g, r, t: (0, 0), pipeline_mode=pl.Buffered(1)),
            pl.BlockSpec((tk, PAIRS * 128), krow),
            pl.BlockSpec((tk, PAIRS * 128), krow),
            pl.BlockSpec((tk, 128), lambda g, r, t: (kj(r, t), 0)),
        ],
        out_shape=[jax.ShapeDtypeStruct((AW, T), F32), jax.ShapeDtypeStruct((8, T), F32),
                   jax.ShapeDtypeStruct((T, AW), BF16), jax.ShapeDtypeStruct((T, AW), BF16),
                   jax.ShapeDtypeStruct((T, FPAD), F32)],
        scratch_shapes=[pltpu.VMEM((tk, PAIRS * 128), F32), pltpu.VMEM((tk, PAIRS * 128), F32),
                        pltpu.VMEM((tk, 128), F32)],
        compiler_params=_params(3),
    )(kk, vk, kt, qw, dow, after)


def _fgate_bwd(dcq, dck, fz, fb):
    T = dck.shape[0]
    tb = _tile(T, 512)
    nb = T // tb

    def body(dcq_ref, dck_ref, fz_ref, fb_ref, df_ref, dfb_ref, carry):
        @pl.when(pl.program_id(0) == 0)
        def _():
            carry[...] = jnp.zeros_like(carry)
            dfb_ref[...] = jnp.zeros_like(dfb_ref)

        head = lax.broadcasted_iota(jnp.int32, (8, FPAD), 0)
        eye = jnp.where(head == lax.broadcasted_iota(jnp.int32, (8, FPAD), 1), 1.0, 0.0)
        dcv = dck_ref[...] + lax.dot_general(dcq_ref[...], eye, (((0,), (0,)), ((), ())),
                                             precision=lax.Precision.HIGHEST, preferred_element_type=F32)
        r = lax.broadcasted_iota(jnp.int32, (128, 128), 0)
        cc = lax.broadcasted_iota(jnp.int32, (128, 128), 1)
        tri = (cc >= r).astype(F32)
        run = carry[...]
        parts = []
        for blk in reversed(range(tb // 128)):
            dc_b = dcv[blk * 128:(blk + 1) * 128, :]
            parts.append(jnp.dot(tri, dc_b, precision=lax.Precision.HIGHEST, preferred_element_type=F32) + run)
            run = run + jnp.sum(dc_b, axis=0, keepdims=True)
        carry[...] = run
        dlf = jnp.concatenate(parts[::-1], axis=0)
        lane = lax.broadcasted_iota(jnp.int32, (tb, FPAD), 1)
        df = jnp.where(lane < 8, dlf * jax.nn.sigmoid(-(fz_ref[...] + fb_ref[...])), 0.0)
        df_ref[...] = df.astype(BF16)
        dfb_ref[...] += jnp.sum(df, axis=0, keepdims=True)

    rev = pl.BlockSpec((tb, FPAD), lambda i: (nb - 1 - i, 0))
    return pl.pallas_call(
        body,
        name="fgate_bwd",
        grid=(nb,),
        in_specs=[pl.BlockSpec((8, tb), lambda i: (0, nb - 1 - i)), rev, rev, _full((1, FPAD))],
        out_specs=[rev, _full((1, FPAD))],
        out_shape=[jax.ShapeDtypeStruct((T, FPAD), BF16), jax.ShapeDtypeStruct((1, FPAD), F32)],
        scratch_shapes=[pltpu.VMEM((1, FPAD), F32)],
        compiler_params=_params(1),
    )(dcq, dck, fz, fb)


def _inproj_bwd(dqt, dk, dv, dus, dfz, wtp, x, dh1, g):
    T = x.shape[0]
    tm = _tile(T, 512)

    def body(dq_ref, dk_ref, dv_ref, dus_ref, dfz_ref, w_ref, x_ref, dh1_ref, g_ref, gx_ref, dg_ref):
        @pl.when(pl.program_id(0) == 0)
        def _():
            dg_ref[...] = jnp.zeros_like(dg_ref)

        da = _tn(dq_ref[...].astype(BF16), w_ref[0:AW, :])
        da += _nn(dk_ref[...], w_ref[AW:2 * AW, :])
        da += _nn(dv_ref[...], w_ref[2 * AW:NQKV, :])
        da += _nn(dus_ref[...], w_ref[NQKV:NQKV + NUS, :])
        da += _nn(dfz_ref[...], w_ref[NQKV + NUS:ZP, :])
        dx, dg = _rms_bwd(da, x_ref[...], g_ref[...])
        gx_ref[...] = dh1_ref[...] + dx
        dg_ref[...] += dg

    row = lambda n: pl.BlockSpec((tm, n), lambda i: (i, 0))
    return pl.pallas_call(
        body,
        name="inproj_bwd",
        grid=(T // tm,),
        in_specs=[pl.BlockSpec((AW, tm), lambda i: (0, i)), row(AW), row(AW), row(NUS), row(FPAD), _full((ZP, D)),
                  row(D), row(D), _full((1, D))],
        out_specs=[row(D), _full((1, D))],
        out_shape=[jax.ShapeDtypeStruct((T, D), F32), jax.ShapeDtypeStruct((1, D), F32)],
        compiler_params=_params(1),
    )(dqt, dk, dv, dus, dfz, wtp, x, dh1, g)


def _sq_relu(f1):
    r = jnp.maximum(f1.astype(F32), 0.0)
    return (r * r).astype(BF16)


def _wgrad(a, bs, name, a_fn=None, out_dtype=F32):
    T, K = a.shape
    tt = _tile(T, 1024)
    nb = len(bs)
    narrow = out_dtype != F32

    def out_dims(b, layout):
        if layout == "t":
            return (K, b.shape[0])
        N = b.shape[1]
        if layout is None:
            return (K, N)
        return (N // layout[1], K, layout[1]) if layout[0] == "col" else (K // layout[1], layout[1], N)

    shapes = [out_dims(b, layout) for b, layout in bs]

    def body(*refs):
        a_ref, b_refs, o_refs = refs[0], refs[1:1 + nb], refs[1 + nb:1 + 2 * nb]
        accs = refs[1 + 2 * nb:] if narrow else o_refs

        @pl.when(pl.program_id(0) == 0)
        def _():
            for acc in accs:
                acc[...] = jnp.zeros_like(acc)

        av = a_ref[...] if a_fn is None else a_fn(a_ref[...])
        at = av.astype(BF16).T
        for (b, layout), b_ref, o_ref in zip(bs, b_refs, accs):
            if layout is None:
                o_ref[...] += _nn(at, b_ref[...].astype(BF16))
            elif layout == "t":
                o_ref[...] += _nt(at, b_ref[...].astype(BF16))
            elif layout[0] == "col":
                n = layout[1]
                for k in range(b.shape[1] // n):
                    o_ref[k] += _nn(at, b_ref[:, k * n:(k + 1) * n].astype(BF16))
            else:
                n = layout[1]
                bv = b_ref[...].astype(BF16)
                for k in range(K // n):
                    o_ref[k] += _nn(at[k * n:(k + 1) * n, :], bv)

        if narrow:
            @pl.when(pl.program_id(0) == T // tt - 1)
            def _():
                for o_ref, acc in zip(o_refs, accs):
                    o_ref[...] = acc[...].astype(out_dtype)

    once = lambda shape: pl.BlockSpec(shape, lambda t: (0,) * len(shape), pipeline_mode=pl.Buffered(1))
    res = pl.pallas_call(
        body,
        name=name,
        grid=(T // tt,),
        in_specs=[pl.BlockSpec((tt, K), lambda t: (t, 0))] + [
            pl.BlockSpec((b.shape[0], tt), lambda t: (0, t)) if layout == "t" else pl.BlockSpec((tt, b.shape[1]), lambda t: (t, 0))
            for b, layout in bs],
        out_specs=[once(s) for s in shapes],
        out_shape=[jax.ShapeDtypeStruct(s, out_dtype) for s in shapes],
        scratch_shapes=[pltpu.VMEM(s, F32) for s in shapes] if narrow else [],
        compiler_params=_params(1, VMEM_LIMIT_BIG),
    )(a, *[b for b, _ in bs])
    return res


def _adam_math(w, g, m, v):
    m = ADAM_B1 * m + (1.0 - ADAM_B1) * g
    v = ADAM_B2 * v + (1.0 - ADAM_B2) * (g * g)
    m_hat = m / (1.0 - ADAM_B1 ** ADAM_STEP)
    v_hat = v / (1.0 - ADAM_B2 ** ADAM_STEP)
    delta = -ADAM_LR * (m_hat / (jnp.sqrt(v_hat) + ADAM_EPS) + ADAM_WD * w)
    return delta, m, v


def _adam(parts, w, m, v, name):
    R, C = w.shape
    br = 128 if R % 128 == 0 else R

    def body(p_ref, w_ref, m_ref, v_ref, g_ref, d_ref, nm_ref, nv_ref):
        g = p_ref[0].astype(F32)
        for s in range(1, NDEV):
            g = g + p_ref[s].astype(F32)
        g_ref[...] = g
        d_ref[...], nm_ref[...], nv_ref[...] = _adam_math(w_ref[...], g, m_ref[...], v_ref[...])

    blk = pl.BlockSpec((br, C), lambda i: (i, 0))
    return pl.pallas_call(
        body,
        name=name,
        grid=(R // br,),
        in_specs=[pl.BlockSpec((NDEV, br, C), lambda i: (0, i, 0)), blk, blk, blk],
        out_specs=[blk] * 4,
        out_shape=[jax.ShapeDtypeStruct((R, C), F32)] * 4,
        compiler_params=_params(1),
    )(parts, w, m, v)


_SMALL = (("sg_w", 8 * CH * CH), ("f_bias", 8), ("sg_ln_g", SW), ("sg_ln_b", SW), ("sg_b", 8 * CH), ("att_out_g", AW),
          ("sg_out_g", SW), ("pre_mix_g", D), ("post_mix_g", D), ("pre_ffn_g", D), ("post_ffn_g", D), ("ple_gate_b", D))
_SEG = 8 * 128


def _seg_rows(size):
    return 8 * (-(-size // _SEG))


def _pack(vals, loss_acc):
    parts = []
    for name, size in _SMALL:
        flat = vals[name].reshape(-1)
        rows = _seg_rows(size)
        parts.append(jnp.pad(flat, (0, rows * 128 - size)).reshape(rows, 128))
    parts.append(loss_acc)
    return jnp.concatenate(parts, axis=0)


def _adam_small(parts, ws, ms, vs):
    n = len(_SMALL)
    names = [name for name, _ in _SMALL]

    def body(*refs):
        p_ref, w_refs, m_refs, v_refs = refs[0], refs[1:1 + n], refs[1 + n:1 + 2 * n], refs[1 + 2 * n:1 + 3 * n]
        loss_ref, outs = refs[1 + 3 * n], refs[2 + 3 * n:]
        g_all = p_ref[0]
        for s in range(1, NDEV):
            g_all = g_all + p_ref[s]
        r = 0
        for k, (name, size) in enumerate(_SMALL):
            dst = [outs[kind * n + k] for kind in range(4)]

            def update(g, idx):
                vals = (g,) + _adam_math(w_refs[k][idx], g, m_refs[k][idx], v_refs[k][idx])
                for d, val in zip(dst, vals):
                    d[idx] = val

            if name == "sg_w":
                for grp in range(8):
                    update(g_all[r + grp * CH:r + (grp + 1) * CH, :], (0, grp))
            elif name == "sg_b":
                update(g_all[r:r + 8, :], (0,))
            elif name == "f_bias":
                update(g_all[r:r + 1, 0:8], (slice(None),))
            else:
                update(jnp.concatenate([g_all[r + q:r + q + 1, :] for q in range(size // 128)], axis=1), (slice(None),))
            r += _seg_rows(size)
        loss_ref[...] = g_all[r:r + 1, 0:1] * (0.5 / D)

    arrs = [parts] + [d[name] for d in (ws, ms, vs) for name in names]
    res = pl.pallas_call(
        body,
        name="adam_small",
        in_specs=[_full(a.shape) for a in arrs],
        out_specs=[_full((1, 1))] + [_full(ws[name].shape) for _ in range(4) for name in names],
        out_shape=[jax.ShapeDtypeStruct((1, 1), F32)] + [jax.ShapeDtypeStruct(ws[name].shape, F32) for _ in range(4) for name in names],
        compiler_params=pltpu.CompilerParams(vmem_limit_bytes=VMEM_LIMIT),
    )(*arrs)
    return res[0], {name: [res[1 + kind * n + k] for kind in range(4)] for k, name in enumerate(names)}


def kernel(x, p, w_in, f_bias, sg_ln_g, sg_ln_b, sg_w, sg_b, att_out_g, sg_out_g, w_out, pre_mix_g, post_mix_g, pre_ffn_g, post_ffn_g, w_ff1, w_ff2, ple_w, ple_gate_w, ple_gate_b, loss_target, m_w_in, m_f_bias, m_sg_ln_g, m_sg_ln_b, m_sg_w, m_sg_b, m_att_out_g, m_sg_out_g, m_w_out, m_pre_mix_g, m_post_mix_g, m_pre_ffn_g, m_post_ffn_g, m_w_ff1, m_w_ff2, m_ple_w, m_ple_gate_w, m_ple_gate_b, v_w_in, v_f_bias, v_sg_ln_g, v_sg_ln_b, v_sg_w, v_sg_b, v_att_out_g, v_sg_out_g, v_w_out, v_pre_mix_g, v_post_mix_g, v_pre_ffn_g, v_post_ffn_g, v_w_ff1, v_w_ff2, v_ple_w, v_ple_gate_w, v_ple_gate_b):
    small_w = dict(sg_w=sg_w, f_bias=f_bias, sg_ln_g=sg_ln_g, sg_ln_b=sg_ln_b, sg_b=sg_b, att_out_g=att_out_g,
                   sg_out_g=sg_out_g, pre_mix_g=pre_mix_g, post_mix_g=post_mix_g, pre_ffn_g=pre_ffn_g,
                   post_ffn_g=post_ffn_g, ple_gate_b=ple_gate_b)
    small_m = dict(sg_w=m_sg_w, f_bias=m_f_bias, sg_ln_g=m_sg_ln_g, sg_ln_b=m_sg_ln_b, sg_b=m_sg_b, att_out_g=m_att_out_g,
                   sg_out_g=m_sg_out_g, pre_mix_g=m_pre_mix_g, post_mix_g=m_post_mix_g, pre_ffn_g=m_pre_ffn_g,
                   post_ffn_g=m_post_ffn_g, ple_gate_b=m_ple_gate_b)
    small_v = dict(sg_w=v_sg_w, f_bias=v_f_bias, sg_ln_g=v_sg_ln_g, sg_ln_b=v_sg_ln_b, sg_b=v_sg_b, att_out_g=v_att_out_g,
                   sg_out_g=v_sg_out_g, pre_mix_g=v_pre_mix_g, post_mix_g=v_post_mix_g, pre_ffn_g=v_pre_ffn_g,
                   post_ffn_g=v_post_ffn_g, ple_gate_b=v_ple_gate_b)
    big = dict(w_in=(w_in, m_w_in, v_w_in), w_out=(w_out, m_w_out, v_w_out), w_ff1=(w_ff1, m_w_ff1, v_w_ff1),
               w_ff2=(w_ff2, m_w_ff2, v_w_ff2), ple_w=(ple_w, m_ple_w, v_ple_w),
               ple_gate_w=(ple_gate_w, m_ple_gate_w, v_ple_gate_w))

    xt, pt, tgt = x[0], p[0, 0], loss_target[0]
    ws = W_IN_COLS // NDEV

    gw_in = _gather(jnp.pad(jnp.transpose(w_in[0]).astype(BF16), ((0, WS_PAD - ws), (0, 0))), "gather_w_in")
    gw_in, others = lax.optimization_barrier((gw_in, [w_out[0].astype(BF16), w_ff1[0].astype(BF16), w_ff2[0].astype(BF16),
                                                      ple_w[0].astype(BF16), ple_gate_w[0].astype(BF16)]))
    rest = _xchg_start(others, True, "gather_rest_start")
    wint = gw_in[:, 0:ws, :].reshape(W_IN_COLS, D)
    wtp = jnp.concatenate([wint[0:NQKV], wint[NQKV + 8:W_IN_COLS], wint[NQKV:NQKV + 8],
                           jnp.zeros((FPAD - 8, D), BF16)], axis=0)

    fb = jnp.pad(f_bias.astype(F32), ((0, 0), (0, FPAD - 8)))
    sgw = sg_w[0]
    sgwt = jnp.transpose(sg_w[0], (0, 2, 1))
    sgbt = jnp.transpose(sg_b[0])

    qkv, kt, vt, us, fz, ab = _inproj(xt, pre_mix_g + rest[4][0, 0], wtp)
    ct, kk, vk = _fcum(fz, fb, qkv)
    yatt, lsec = _attn_fwd(qkv, kk, vt, ct)
    gw_out, gw1, gw2, gwpe, gwg = _xchg_wait(rest, True, yatt, "gather_rest_wait")
    wout = gw_out.reshape(D, D)
    wg = gwg.reshape(D, D)
    wpe = jnp.transpose(gwpe, (1, 0, 2)).reshape(PLE, D)
    h1, yb, o = _sgu_out(us, yatt, xt, sg_ln_g, sg_ln_b, sgw, sgbt, att_out_g, sg_out_g, wout, post_mix_g)
    c2b, f1b, ff, h2 = _ffn_fwd(h1, pre_ffn_g, gw1, gw2, post_ffn_g)
    dh2, dbg, loss_acc, g_g, g_pe = _ple_loss(h2, pt, tgt, wg, ple_gate_b, wpe)
    g_g = g_g.reshape(NDEV, D // NDEV, D)
    g_pe = jnp.transpose(g_pe.reshape(PLE, NDEV, D // NDEV), (1, 0, 2))

    dffb, df1, dh1, dgpostffn, dgpreffn = _ffn_bwd(dh2, ff, h1, f1b, gw1, gw2, post_ffn_g, pre_ffn_g)
    (g_1,) = _wgrad(c2b, [(df1, ("col", DFF // NDEV))], "wgrad_ff1")
    (g_2,) = _wgrad(f1b, [(dffb, ("row", DFF // NDEV))], "wgrad_ff2", a_fn=_sq_relu)
    early = _xchg_start([g_1, g_2, g_pe, g_g], False, "scatter_early_start")
    qw, dow, dus, g_out, dsgw, dsgbt, dlng, dlnb, dgatt, dgsg, dgpostmix = _mix_bwd(
        dh1, o, us, yatt, qkv, lsec, yb, sg_ln_g, sg_ln_b, sgw, sgwt, sgbt, att_out_g, sg_out_g, wout,
        post_mix_g + early[4][0, 0])
    mid = _xchg_start([g_out.reshape(NDEV, D // NDEV, D)], False, "scatter_mid_start")
    dqt, dcq, dk, dv, dck = _attn_bwd(kk, vk, kt, qw, dow, mid[4])
    dfz, dfb = _fgate_bwd(dcq, dck, fz, fb)

    gq, gk, gv, gus, gf = _wgrad(ab, [(dqt, "t"), (dk, None), (dv, None), (dus, None), (dfz, None)], "wgrad_in",
                                 out_dtype=BF16)
    g_in = jnp.concatenate([gq, gk, gv, gf[:, 0:8], gus], axis=1)
    g_in = jnp.transpose(g_in.reshape(D, NDEV, ws), (1, 0, 2))
    late = _xchg_start([g_in], False, "scatter_late_start")
    grad_x, dgpremix = _inproj_bwd(dqt, dk, dv, dus, dfz, wtp, xt, dh1, pre_mix_g + late[4][0, 0])

    small_g = dict(sg_w=dsgw, f_bias=dfb[:, 0:8], sg_ln_g=dlng, sg_ln_b=dlnb, sg_b=jnp.transpose(dsgbt[:, 0:8]),
                   att_out_g=dgatt, sg_out_g=dgsg, pre_mix_g=dgpremix, post_mix_g=dgpostmix, pre_ffn_g=dgpreffn,
                   post_ffn_g=dgpostffn, ple_gate_b=dbg)

    small = _xchg_start([_pack(small_g, loss_acc)], True, "gather_small_start")
    r_1, r_2, r_pe, r_g = _xchg_wait(early, False, grad_x, "scatter_early_wait")
    (r_out,) = _xchg_wait(mid, False, grad_x, "scatter_mid_wait")

    res = {}

    def adam_big(name, parts):
        w, m, v = big[name]
        res[name] = [t[None] for t in _adam(parts, w[0], m[0], v[0], "adam_" + name)]

    for name, parts in (("w_out", r_out), ("w_ff1", r_1), ("w_ff2", r_2), ("ple_w", r_pe), ("ple_gate_w", r_g)):
        adam_big(name, parts)
    (r_in,) = _xchg_wait(late, False, res["w_ff1"][0], "scatter_late_wait")
    adam_big("w_in", r_in)
    (r_small,) = _xchg_wait(small, True, res["w_in"][0], "gather_small_wait")
    loss, small_res = _adam_small(r_small, small_w, small_m, small_v)
    res.update(small_res)

    order = ["w_in", "f_bias", "sg_ln_g", "sg_ln_b", "sg_w", "sg_b", "att_out_g", "sg_out_g", "w_out", "pre_mix_g",
             "post_mix_g", "pre_ffn_g", "post_ffn_g", "w_ff1", "w_ff2", "ple_w", "ple_gate_w", "ple_gate_b"]
    outs = [loss[0, 0], grad_x[None]]
    for kind in range(4):
        outs += [res[name][kind] for name in order]
    return tuple(outs)
```

```python
import jax
import jax.numpy as jnp
from jax import lax
from jax.experimental import pallas as pl
from jax.experimental.pallas import tpu as pltpu

F32 = jnp.float32
BF16 = jnp.bfloat16

NDEV = 8
D = 1024
AW = 512
SW = 512
HD = 64
CH = 128
DFF = 4096
PLE = 256
NQKV = 3 * AW
NUS = 2 * SW
FPAD = 128
ZP = NQKV + NUS + FPAD
W_IN_COLS = 2568
WS_PAD = 336
EPS = 1e-6
MASKV = -1e30
GELU_K = 0.7978845608028654
GELU_C = 0.044715

ADAM_LR = 0.001
ADAM_B1 = 0.9
ADAM_B2 = 0.999
ADAM_EPS = 1e-08
ADAM_WD = 0.01
ADAM_STEP = 10

VMEM_LIMIT = 48 * 1024 * 1024
VMEM_LIMIT_BIG = 60 * 1024 * 1024


def _nn(a, b):
    return jnp.dot(a, b, preferred_element_type=F32)


def _nt(a, b):
    return lax.dot_general(a, b, (((1,), (1,)), ((), ())), preferred_element_type=F32)


def _tn(a, b):
    return lax.dot_general(a, b, (((0,), (0,)), ((), ())), preferred_element_type=F32)


def _tile(n, pref):
    return min(n, pref)


def _params(n_axes, vmem=VMEM_LIMIT):
    return pltpu.CompilerParams(dimension_semantics=("arbitrary",) * n_axes, vmem_limit_bytes=vmem)


def _full(shape):
    nd = len(shape)
    return pl.BlockSpec(shape, lambda *_: (0,) * nd)


def _rms_fwd(x, g):
    r = lax.rsqrt(jnp.mean(x * x, axis=-1, keepdims=True) + EPS)
    return x * r * g


def _rms_bwd(dy, x, g):
    n = x.shape[-1]
    r = lax.rsqrt(jnp.mean(x * x, axis=-1, keepdims=True) + EPS)
    u = dy * g
    s = jnp.sum(x * u, axis=-1, keepdims=True)
    dx = r * u - x * (r * r * r * (s * (1.0 / n)))
    dg = jnp.sum(dy * (x * r), axis=0, keepdims=True)
    return dx, dg


def _gelu(x):
    t = jnp.tanh(x * (GELU_K + (GELU_K * GELU_C) * (x * x)))
    return x * (0.5 + 0.5 * t), t


def _gelu_grad(x, t):
    return (0.5 + 0.5 * t) + (0.5 * x) * (1.0 - t * t) * (GELU_K + (3.0 * GELU_K * GELU_C) * (x * x))


def _gather(arr, name):
    def body(x_ref, out_ref, send, recv, loc):
        x, y, c = lax.axis_index("x"), lax.axis_index("y"), lax.axis_index("c")
        me, sibling = (x, y, c), (x, y, 1 - c)
        chips = [(1 - x, y), (x, 1 - y), (1 - x, 1 - y)]

        def slot(px, py, pc):
            return out_ref.at[4 * px + 2 * py + pc]

        def copy(k, block, to, src=None):
            return pltpu.make_async_remote_copy(
                src_ref=slot(*block) if src is None else src, dst_ref=slot(*block), send_sem=send.at[k],
                recv_sem=recv.at[k], device_id=to, device_id_type=pl.DeviceIdType.MESH)

        mine = pltpu.make_async_copy(x_ref, slot(*me), loc)
        mine.start()
        first = [copy(0, me, sibling, src=x_ref)] + [copy(1 + j, me, (*chip, c), src=x_ref) for j, chip in enumerate(chips)]
        for cp in first:
            cp.start()
        passed = [copy(4 + j, (*chip, c), sibling) for j, chip in enumerate(chips)]
        for j, chip in enumerate(chips):
            copy(1 + j, (*chip, c), me).wait_recv()
            passed[j].start()
        copy(0, sibling, me).wait_recv()
        for j, chip in enumerate(chips):
            copy(4 + j, (*chip, 1 - c), me).wait_recv()
        for cp in first + passed:
            cp.wait_send()
        mine.wait()

    hbm = pl.BlockSpec(memory_space=pltpu.HBM)
    return pl.pallas_call(
        body,
        name=name,
        out_shape=jax.ShapeDtypeStruct((NDEV,) + arr.shape, arr.dtype),
        in_specs=[hbm],
        out_specs=hbm,
        scratch_shapes=[pltpu.SemaphoreType.DMA((NDEV - 1,)), pltpu.SemaphoreType.DMA((NDEV - 1,)), pltpu.SemaphoreType.DMA],
    )(arr)


def _peers(x, y, c):
    out = []
    for k in range(1, NDEV):
        out.append((1 - x if (k >> 2) & 1 else x, 1 - y if (k >> 1) & 1 else y, 1 - c if k & 1 else c))
    return out


def _xchg_start(arrs, gather, name):
    n = len(arrs)
    me = 4 * lax.axis_index("x") + 2 * lax.axis_index("y") + lax.axis_index("c")
    lands = []
    for a in arrs:
        shape = ((NDEV,) + a.shape) if gather else a.shape
        own = a[None] if gather else lax.dynamic_slice_in_dim(a, me, 1, axis=0)
        lands.append(lax.dynamic_update_slice_in_dim(lax.empty(shape, a.dtype), own, me, axis=0))

    def body(*refs):
        ins, lnd = refs[:n], refs[n:2 * n]
        send, recv, token = refs[2 * n:3 * n], refs[3 * n:4 * n], refs[-1]
        x, y, c = lax.axis_index("x"), lax.axis_index("y"), lax.axis_index("c")
        mine = 4 * x + 2 * y + c
        for px, py, pc in _peers(x, y, c):
            peer = 4 * px + 2 * py + pc
            for a in range(n):
                pltpu.make_async_remote_copy(
                    src_ref=ins[a] if gather else ins[a].at[peer],
                    dst_ref=lnd[a].at[mine],
                    send_sem=send[a],
                    recv_sem=recv[a],
                    device_id=(px, py, pc),
                    device_id_type=pl.DeviceIdType.MESH,
                ).start()
        token[...] = jnp.zeros_like(token)

    hbm = pl.BlockSpec(memory_space=pltpu.HBM)
    sem = pl.BlockSpec(memory_space=pltpu.SEMAPHORE)
    res = pl.pallas_call(
        body,
        name=name,
        out_shape=(*[pltpu.SemaphoreType.DMA(())] * (2 * n),
                   *[pltpu.HBM(a.shape, a.dtype) for a in arrs], *[pltpu.HBM(l.shape, l.dtype) for l in lands],
                   jax.ShapeDtypeStruct((8, 128), F32)),
        in_specs=[hbm] * (2 * n),
        out_specs=(*([sem] * (2 * n)), *([hbm] * (2 * n)), pl.BlockSpec(memory_space=pltpu.VMEM)),
        input_output_aliases={i: 2 * n + i for i in range(2 * n)},
        compiler_params=pltpu.CompilerParams(has_side_effects=pltpu.SideEffectType.DATAFLOW_SIDE_EFFECTING),
    )(*[pltpu.with_memory_space_constraint(a, pltpu.HBM) for a in arrs],
      *[pltpu.with_memory_space_constraint(l, pltpu.HBM) for l in lands])
    return list(res[0:n]), list(res[n:2 * n]), list(res[2 * n:3 * n]), list(res[3 * n:4 * n]), res[-1]


def _xchg_wait(started, gather, after, name):
    send, recv, srcs, lands, _ = started
    n = len(srcs)

    def body(*refs):
        lnd = refs[n:2 * n]
        send, recv = refs[2 * n:3 * n], refs[3 * n:4 * n]
        me = (lax.axis_index("x"), lax.axis_index("y"), lax.axis_index("c"))
        for a in range(n):
            seven = lnd[a].at[pl.ds(0, NDEV - 1)]
            cp = pltpu.make_async_remote_copy(src_ref=seven, dst_ref=seven, send_sem=send[a], recv_sem=recv[a],
                                              device_id=me, device_id_type=pl.DeviceIdType.MESH)
            cp.wait_send()
            cp.wait_recv()

    hbm = pl.BlockSpec(memory_space=pltpu.HBM)
    sem = pl.BlockSpec(memory_space=pltpu.SEMAPHORE)
    res = pl.pallas_call(
        body,
        name=name,
        out_shape=tuple([pltpu.HBM(a.shape, a.dtype) for a in srcs] + [pltpu.HBM(l.shape, l.dtype) for l in lands]),
        in_specs=[hbm] * (2 * n) + [sem] * (2 * n) + [pl.BlockSpec(memory_space=pl.ANY)],
        out_specs=tuple([hbm] * (2 * n)),
        input_output_aliases={i: i for i in range(2 * n)},
        compiler_params=pltpu.CompilerParams(has_side_effects=pltpu.SideEffectType.DATAFLOW_SIDE_EFFECTING),
    )(*srcs, *lands, *send, *recv, after)
    return list(res[n:])


def _inproj(x, g, wtp):
    T = x.shape[0]
    tm = _tile(T, 512)

    def body(x_ref, g_ref, w_ref, qkv_ref, kt_ref, vt_ref, us_ref, fz_ref, ab_ref):
        a = _rms_fwd(x_ref[...], g_ref[...]).astype(BF16)
        ab_ref[...] = a
        qkv_ref[:, 0:AW] = _nt(a, w_ref[0:AW, :]).astype(BF16)
        kk = _nt(a, w_ref[AW:2 * AW, :])
        qkv_ref[:, AW:2 * AW] = kk.astype(BF16)
        kt_ref[...] = kk.T.astype(BF16)
        vv = _nt(a, w_ref[2 * AW:NQKV, :])
        qkv_ref[:, 2 * AW:NQKV] = vv.astype(BF16)
        vvt = vv.T.astype(BF16)
        one_row = jnp.where(lax.broadcasted_iota(jnp.int32, (HD, tm), 0) == 0, 1.0, 0.0).astype(BF16)
        for h in range(AW // HD):
            vt_ref[2 * h * HD:(2 * h + 1) * HD, :] = vvt[h * HD:(h + 1) * HD, :]
            vt_ref[(2 * h + 1) * HD:(2 * h + 2) * HD, :] = one_row
        us_ref[...] = _nt(a, w_ref[NQKV:NQKV + NUS, :])
        fz_ref[...] = _nt(a, w_ref[NQKV + NUS:ZP, :])

    row = lambda n: pl.BlockSpec((tm, n), lambda i: (i, 0))
    col = lambda n: pl.BlockSpec((n, tm), lambda i: (0, i))
    return pl.pallas_call(
        body,
        name="inproj",
        grid=(T // tm,),
        in_specs=[row(D), _full((1, D)), _full((ZP, D))],
        out_specs=[row(NQKV), col(AW), col(2 * AW), row(NUS), row(FPAD), row(D)],
        out_shape=[
            jax.ShapeDtypeStruct((T, NQKV), BF16),
            jax.ShapeDtypeStruct((AW, T), BF16),
            jax.ShapeDtypeStruct((2 * AW, T), BF16),
            jax.ShapeDtypeStruct((T, NUS), F32),
            jax.ShapeDtypeStruct((T, FPAD), F32),
            jax.ShapeDtypeStruct((T, D), BF16),
        ],
        compiler_params=_params(1),
    )(x, g, wtp)


def _log_sigmoid(z):
    return jnp.minimum(z, 0.0) - jnp.log1p(jnp.exp(-jnp.abs(z)))


def _split3(x):
    hi = x.astype(BF16)
    r1 = x - hi.astype(F32)
    mid = r1.astype(BF16)
    lo = (r1 - mid.astype(F32)).astype(BF16)
    return hi, mid, lo


AUG_A, AUG_B, AUG_ONE = 0, 3, 6


def _aug_lanes(rows, first_one, pieces):
    lane = lax.broadcasted_iota(jnp.int32, (rows, 128), 1)
    out = jnp.zeros((rows, 128), F32)
    if first_one is not None:
        out = jnp.where((lane >= first_one) & (lane < first_one + 3), 1.0, out)
    for n, piece in enumerate(pieces):
        out = jnp.where(lane == AUG_ONE + n, piece.astype(F32), out)
    return out.astype(BF16)


def _fcum(fz, fb, qkv):
    T = fz.shape[0]
    tb = _tile(T, 512)

    def body(fz_ref, fb_ref, k_ref, v_ref, ct_ref, kk_ref, vk_ref, carry, tri_s, sel_s):
        @pl.when(pl.program_id(0) == 0)
        def _():
            carry[...] = jnp.zeros_like(carry)
            src = lax.broadcasted_iota(jnp.int32, (128, 128), 0)
            dst = lax.broadcasted_iota(jnp.int32, (128, 128), 1)
            tri_s[...] = (dst <= src).astype(F32)
            for hp in range(4):
                for n in range(3):
                    pick = ((src == 2 * hp) & (dst == AUG_A + n)) | ((src == 2 * hp + 1) & (dst == AUG_B + n))
                    sel_s[3 * hp + n] = jnp.where(pick, -1.0, 0.0).astype(BF16)

        lf = _log_sigmoid(fz_ref[...] + fb_ref[...])
        run = carry[...]
        parts = []
        for blk in range(tb // 128):
            lf_b = lf[blk * 128:(blk + 1) * 128, :]
            parts.append(jnp.dot(tri_s[...], lf_b, precision=lax.Precision.HIGHEST, preferred_element_type=F32) + run)
            run = run + jnp.sum(lf_b, axis=0, keepdims=True)
        carry[...] = run
        cs = jnp.concatenate(parts, axis=0)
        ct_ref[...] = cs.T[0:8, :]

        pieces = _split3(cs)
        lane = lax.broadcasted_iota(jnp.int32, (1, 128), 1)
        ones = jnp.where((lane >= AUG_ONE) & (lane < AUG_ONE + 3), 1.0, 0.0)
        for hp in range(4):
            aug = jnp.zeros((tb, 128), F32) + ones
            for n, piece in enumerate(pieces):
                aug = aug + _nn(piece, sel_s[3 * hp + n])
            aug = aug.astype(BF16)
            kk_ref[:, hp * 256:hp * 256 + 128] = k_ref[:, hp * 128:(hp + 1) * 128]
            kk_ref[:, hp * 256 + 128:(hp + 1) * 256] = aug
            vk_ref[:, hp * 256:hp * 256 + 128] = v_ref[:, hp * 128:(hp + 1) * 128]
            vk_ref[:, hp * 256 + 128:(hp + 1) * 256] = aug

    row = lambda n, c: pl.BlockSpec((tb, n), lambda i: (i, c))
    return pl.pallas_call(
        body,
        name="fcum",
        grid=(T // tb,),
        in_specs=[row(FPAD, 0), _full((1, FPAD)), row(AW, 1), row(AW, 2)],
        out_specs=[pl.BlockSpec((8, tb), lambda i: (0, i)), row(2 * AW, 0), row(2 * AW, 0)],
        out_shape=[jax.ShapeDtypeStruct((8, T), F32), jax.ShapeDtypeStruct((T, 2 * AW), BF16),
                   jax.ShapeDtypeStruct((T, 2 * AW), BF16)],
        scratch_shapes=[pltpu.VMEM((1, FPAD), F32), pltpu.VMEM((128, 128), F32), pltpu.VMEM((12, 128, 128), BF16)],
        compiler_params=_params(1),
    )(fz, fb, qkv, qkv)


def _fold_rows(r, t, nq):
    if nq == 1:
        return r, t
    low = t <= r
    return jnp.where(low, r, nq - 1 - r), jnp.where(low, t, t - r - 1)


def _fold_cols(r, t, nq):
    if nq == 1:
        return r, t
    first = t < nq - r
    j = jnp.where(first, r, nq - 1 - r)
    return j, jnp.where(first, r + t, nq - 1 - r + (t - (nq - r)))


PAIRS = 4


def _fold_grid(nq):
    assert (nq == 1 or nq % 2 == 0) and PAIRS == 4
    return (4 // PAIRS, 1, 1) if nq == 1 else (4 // PAIRS, nq // 2, nq + 1)


def _head_rows(x2, hh, scale):
    is_a = lax.broadcasted_iota(jnp.int32, (1, 128), 1) < HD
    keep = is_a if hh == 0 else jnp.logical_not(is_a)
    return jnp.where(keep, x2, jnp.zeros_like(x2)) * scale


def _attn_fwd(qkv, kk, vt, ct):
    T = qkv.shape[0]
    tq = _tile(T, 512)
    tk = tq
    nq = T // tq

    def body(q_ref, kk_ref, vt_ref, ctq_ref, o_ref, lsec_ref, qw_s, m_s, acc_s):
        i, j = _fold_rows(pl.program_id(1), pl.program_id(2), nq)
        sub8 = lax.broadcasted_iota(jnp.int32, (8, 1), 0)

        def cref_of(pp, hh):
            head = 2 * (PAIRS * pl.program_id(0) + pp) + hh
            return jnp.sum(jnp.where(sub8 == head, ctq_ref[:, 0:1], 0.0), axis=0, keepdims=True)

        @pl.when(j == 0)
        def _():
            for pp in range(PAIRS):
                q2 = q_ref[:, pp * 128:(pp + 1) * 128]
                for hh in range(2):
                    rows = slice(hh * tq, (hh + 1) * tq)
                    qw_s[pp, rows, 0:128] = _head_rows(q2, hh, 0.125)
                    qw_s[pp, rows, 128:256] = _aug_lanes(tq, AUG_A if hh == 0 else AUG_B, _split3(cref_of(pp, hh)))
            m_s[...] = jnp.full_like(m_s, MASKV)
            acc_s[...] = jnp.zeros_like(acc_s)

        def step(masked):
            if masked:
                causal = lax.broadcasted_iota(jnp.int32, (tk, tq), 0) <= lax.broadcasted_iota(jnp.int32, (tk, tq), 1)
            logits = lambda pp: _nt(kk_ref[:, pp * 256:(pp + 1) * 256], qw_s[pp])
            s_next = logits(0)
            for pp in range(PAIRS):
                s2 = s_next
                if pp + 1 < PAIRS:
                    s_next = logits(pp + 1)
                for hh in range(2):
                    n = 2 * pp + hh
                    s = s2[:, hh * tq:(hh + 1) * tq]
                    if masked:
                        s = jnp.where(causal, s, MASKV)
                    m_prev = m_s[n]
                    m_new = jnp.maximum(m_prev, jnp.max(s, axis=0, keepdims=True))
                    pr = jnp.exp(s - m_new)
                    m_s[n] = m_new
                    acc_s[n] = jnp.exp(m_prev - m_new) * acc_s[n] + _nn(vt_ref[n * 128:(n + 1) * 128, :], pr.astype(BF16))

        @pl.when(j < i)
        def _():
            step(False)

        @pl.when(j == i)
        def _():
            step(True)
            sub = lax.broadcasted_iota(jnp.int32, (128, 1), 0)
            for pp in range(PAIRS):
                outs, lrow = [], []
                for hh in range(2):
                    n = 2 * pp + hh
                    den = acc_s[n, HD:HD + 1, :]
                    outs.append(acc_s[n, 0:HD, :] * (1.0 / den))
                    lrow.append(m_s[n] + jnp.log(den) - cref_of(pp, hh))
                o_ref[:, pp * 128:(pp + 1) * 128] = jnp.concatenate(outs, axis=0).T
                lsec_ref[:, pp * 128:(pp + 1) * 128] = jnp.where(sub == 0, lrow[0], jnp.where(sub == 1, lrow[1], 0.0)).T

    qi = lambda r, t: _fold_rows(r, t, nq)[0]
    kj = lambda r, t: _fold_rows(r, t, nq)[1]
    return pl.pallas_call(
        body,
        name="attn_fwd",
        grid=_fold_grid(nq),
        in_specs=[
            pl.BlockSpec((tq, PAIRS * 128), lambda g, r, t: (qi(r, t), g)),
            pl.BlockSpec((tk, PAIRS * 256), lambda g, r, t: (kj(r, t), g)),
            pl.BlockSpec((PAIRS * 256, tk), lambda g, r, t: (g, kj(r, t))),
            pl.BlockSpec((8, tq), lambda g, r, t: (0, qi(r, t))),
        ],
        out_specs=[pl.BlockSpec((tq, PAIRS * 128), lambda g, r, t: (qi(r, t), g))] * 2,
        out_shape=[jax.ShapeDtypeStruct((T, AW), F32)] * 2,
        scratch_shapes=[pltpu.VMEM((PAIRS, 2 * tq, 256), BF16), pltpu.VMEM((2 * PAIRS, 1, tq), F32),
                        pltpu.VMEM((2 * PAIRS, 128, tq), F32)],
        compiler_params=_params(3),
    )(qkv, kk, vt, ct)


def _sgu_forward(us_ref, lng, lnb, w_ref, bt_ref, mixed_s, vnb_s, tm):
    is_a = lax.broadcasted_iota(jnp.int32, (1, 128), 1) < HD
    u = us_ref[:, 0:SW]
    vs = us_ref[:, SW:NUS]
    ug, tu = _gelu(u)
    vg, tv = _gelu(vs)
    mu = jnp.mean(vg, axis=-1, keepdims=True)
    xc = vg - mu
    rstd = lax.rsqrt(jnp.mean(xc * xc, axis=-1, keepdims=True) + EPS)
    vhat = xc * rstd
    vnb_s[...] = (vhat * lng + lnb).astype(BF16)
    rr = lax.broadcasted_iota(jnp.int32, (CH, CH), 0)
    cc = lax.broadcasted_iota(jnp.int32, (CH, CH), 1)
    tril = cc <= rr
    for jj in range(4):
        wa = jnp.where(tril, w_ref[2 * jj], 0.0).astype(BF16)
        wb = jnp.where(tril, w_ref[2 * jj + 1], 0.0).astype(BF16)
        ba = bt_ref[:, 2 * jj:2 * jj + 1]
        bb = bt_ref[:, 2 * jj + 1:2 * jj + 2]
        for ch in range(tm // CH):
            rs, cs = slice(ch * CH, (ch + 1) * CH), slice(jj * 128, (jj + 1) * 128)
            vn2 = vnb_s[rs, cs]
            mixed_s[rs, cs] = jnp.where(is_a, _nn(wa, vn2) + ba, _nn(wb, vn2) + bb)
    mixed = mixed_s[...]
    return u, vs, ug, tu, tv, vhat, rstd, mixed, ug * mixed


def _sgu_out(us, yatt, x, lng, lnb, sgw, sgbt, gatt, gsg, wout, gpm):
    T = us.shape[0]
    tm = _tile(T, 512)

    def body(us_ref, ya_ref, x_ref, lng_ref, lnb_ref, w_ref, bt_ref, ga_ref, gs_ref, wo_ref, gp_ref,
             h1_ref, yb_ref, mixed_s, vnb_s):
        ysg = _sgu_forward(us_ref, lng_ref[...], lnb_ref[...], w_ref, bt_ref, mixed_s, vnb_s, tm)[-1]
        yb_ref[:, 0:AW] = _rms_fwd(ya_ref[...], ga_ref[...]).astype(BF16)
        yb_ref[:, AW:D] = _rms_fwd(ysg, gs_ref[...]).astype(BF16)
        h1_ref[...] = x_ref[...] + _rms_fwd(_nn(yb_ref[...], wo_ref[...]), gp_ref[...])

    row = lambda n: pl.BlockSpec((tm, n), lambda i: (i, 0))
    return pl.pallas_call(
        body,
        name="sgu_out",
        grid=(T // tm,),
        in_specs=[row(NUS), row(AW), row(D), _full((1, SW)), _full((1, SW)), _full((8, CH, CH)), _full((CH, 8)),
                  _full((1, AW)), _full((1, SW)), _full((D, D)), _full((1, D))],
        out_specs=[row(D), row(D)],
        out_shape=[jax.ShapeDtypeStruct((T, D), F32), jax.ShapeDtypeStruct((T, D), BF16)],
        scratch_shapes=[pltpu.VMEM((tm, SW), F32), pltpu.VMEM((tm, SW), BF16)],
        compiler_params=_params(1),
    )(us, yatt, x, lng, lnb, sgw, sgbt, gatt, gsg, wout, gpm)


def _ffn_fwd(h1, gpre, w1g, w2g, gpost):
    T = h1.shape[0]
    tm = _tile(T, 512)
    nb, hb = w1g.shape[0], w1g.shape[2]

    def body(h1_ref, gpre_ref, w1_ref, w2_ref, gpost_ref, c2_ref, f1_ref, ff_ref, h2_ref):
        h1 = h1_ref[...]
        c2 = _rms_fwd(h1, gpre_ref[...]).astype(BF16)
        c2_ref[...] = c2
        ff = jnp.zeros((tm, D), F32)
        for j in range(nb):
            f1 = _nn(c2, w1_ref[j])
            f1_ref[:, j * hb:(j + 1) * hb] = f1.astype(BF16)
            r = jnp.maximum(f1, 0.0)
            ff = ff + _nn((r * r).astype(BF16), w2_ref[j])
        ff_ref[...] = ff
        h2_ref[...] = h1 + _rms_fwd(ff, gpost_ref[...])

    row = lambda n: pl.BlockSpec((tm, n), lambda i: (i, 0))
    once = lambda shape: pl.BlockSpec(shape, lambda i: (0,) * len(shape), pipeline_mode=pl.Buffered(1))
    return pl.pallas_call(
        body,
        name="ffn_fwd",
        grid=(T // tm,),
        in_specs=[row(D), _full((1, D)), once((nb, D, hb)), once((nb, hb, D)), _full((1, D))],
        out_specs=[row(D), row(DFF), row(D), row(D)],
        out_shape=[jax.ShapeDtypeStruct((T, D), BF16), jax.ShapeDtypeStruct((T, DFF), BF16),
                   jax.ShapeDtypeStruct((T, D), F32), jax.ShapeDtypeStruct((T, D), F32)],
        compiler_params=_params(1, VMEM_LIMIT_BIG),
    )(h1, gpre, w1g, w2g, gpost)


def _ple_loss(h2, p, tgt, wg, bg, wpe):
    T = h2.shape[0]
    tm = _tile(T, 512)

    def body(h2_ref, p_ref, t_ref, wg_ref, bg_ref, wpe_ref, dh2_ref, dbg_ref, loss_ref, dwg_ref, dwpe_ref):
        @pl.when(pl.program_id(0) == 0)
        def _():
            for r in (dbg_ref, loss_ref, dwg_ref, dwpe_ref):
                r[...] = jnp.zeros_like(r)

        h2 = h2_ref[...]
        h2b = h2.astype(BF16)
        gate = jax.nn.sigmoid(_nn(h2b, wg_ref[...]) + bg_ref[...])
        pb = p_ref[...].astype(BF16)
        pe = _nn(pb, wpe_ref[...])
        diff = (h2 + gate * pe) - t_ref[...]
        loss_ref[...] += jnp.sum(diff * diff)
        dh3 = diff * (1.0 / D)
        dpre = (dh3 * pe) * (gate * (1.0 - gate))
        dpre_b = dpre.astype(BF16)
        dbg_ref[...] += jnp.sum(dpre, axis=0, keepdims=True)
        dh2_ref[...] = dh3 + _nt(dpre_b, wg_ref[...])
        dwg_ref[...] += _nn(h2b.T, dpre_b)
        dwpe_ref[...] += _nn(pb.T, (dh3 * gate).astype(BF16))

    row = lambda n: pl.BlockSpec((tm, n), lambda i: (i, 0))
    once = lambda shape: pl.BlockSpec(shape, lambda i: (0,) * len(shape), pipeline_mode=pl.Buffered(1))
    return pl.pallas_call(
        body,
        name="ple_loss",
        grid=(T // tm,),
        in_specs=[row(D), row(PLE), row(D), _full((D, D)), _full((1, D)), _full((PLE, D))],
        out_specs=[row(D), _full((1, D)), _full((8, 128)), once((D, D)), once((PLE, D))],
        out_shape=[jax.ShapeDtypeStruct((T, D), F32), jax.ShapeDtypeStruct((1, D), F32),
                   jax.ShapeDtypeStruct((8, 128), F32), jax.ShapeDtypeStruct((D, D), F32),
                   jax.ShapeDtypeStruct((PLE, D), F32)],
        compiler_params=_params(1),
    )(h2, p, tgt, wg, bg, wpe)


def _ffn_bwd(dh2, ff, h1, f1, w1g, w2g, gpost, gpre):
    T = dh2.shape[0]
    tm = _tile(T, 512)
    nb, hb = w1g.shape[0], w1g.shape[2]

    def body(dh2_ref, ff_ref, h1_ref, f1_ref, w1_ref, w2_ref, gpost_ref, gpre_ref,
             dffb_ref, df1_ref, dh1_ref, dgpost_ref, dgpre_ref):
        @pl.when(pl.program_id(0) == 0)
        def _():
            dgpost_ref[...] = jnp.zeros_like(dgpost_ref)
            dgpre_ref[...] = jnp.zeros_like(dgpre_ref)

        dh2 = dh2_ref[...]
        dff, dg = _rms_bwd(dh2, ff_ref[...], gpost_ref[...])
        dffb = dff.astype(BF16)
        dffb_ref[...] = dffb
        dgpost_ref[...] += dg
        dc2 = jnp.zeros((tm, D), F32)
        for j in range(nb):
            cols = slice(j * hb, (j + 1) * hb)
            dact = _nt(dffb, w2_ref[j])
            df1 = (dact * (2.0 * jnp.maximum(f1_ref[:, cols].astype(F32), 0.0))).astype(BF16)
            df1_ref[:, cols] = df1
            dc2 = dc2 + _nt(df1, w1_ref[j])
        dx, dg = _rms_bwd(dc2, h1_ref[...], gpre_ref[...])
        dh1_ref[...] = dh2 + dx
        dgpre_ref[...] += dg

    row = lambda n: pl.BlockSpec((tm, n), lambda i: (i, 0))
    once = lambda shape: pl.BlockSpec(shape, lambda i: (0,) * len(shape), pipeline_mode=pl.Buffered(1))
    return pl.pallas_call(
        body,
        name="ffn_bwd",
        grid=(T // tm,),
        in_specs=[row(D), row(D), row(D), row(DFF), once((nb, D, hb)), once((nb, hb, D)), _full((1, D)), _full((1, D))],
        out_specs=[row(D), row(DFF), row(D), _full((1, D)), _full((1, D))],
        out_shape=[jax.ShapeDtypeStruct((T, D), BF16), jax.ShapeDtypeStruct((T, DFF), BF16),
                   jax.ShapeDtypeStruct((T, D), F32), jax.ShapeDtypeStruct((1, D), F32),
                   jax.ShapeDtypeStruct((1, D), F32)],
        compiler_params=_params(1, VMEM_LIMIT_BIG),
    )(dh2, ff, h1, f1, w1g, w2g, gpost, gpre)


def _mix_bwd(dh1, us, yatt, qkv, lsec, yb, lng, lnb, sgw, sgwt, sgbt, gatt, gsg, wout, gpm):
    T = dh1.shape[0]
    tm = _tile(T, 512)

    def body(dh1_ref, us_ref, ya_ref, q_ref, l_ref, yb_ref, lng_ref, lnb_ref, w_ref, wt_ref, bt_ref, ga_ref, gs_ref,
             wo_ref, gp_ref, qw_ref, dow_ref, dus_ref, dwo_ref, dw_ref, dbt_ref, dlng_ref, dlnb_ref, dga_ref, dgs_ref,
             dgp_ref, mixed_s, vnb_s, dvn_s):
        @pl.when(pl.program_id(0) == 0)
        def _():
            for r in (dwo_ref, dw_ref, dbt_ref, dlng_ref, dlnb_ref, dga_ref, dgs_ref, dgp_ref):
                r[...] = jnp.zeros_like(r)

        is_a = lax.broadcasted_iota(jnp.int32, (1, 128), 1) < HD
        lane = lax.broadcasted_iota(jnp.int32, (1, 128), 1)
        yb = yb_ref[...]
        do, dg = _rms_bwd(dh1_ref[...], _nn(yb, wo_ref[...]), gp_ref[...])
        dgp_ref[...] += dg
        dob = do.astype(BF16)
        dwo_ref[...] += _nn(yb.T, dob)
        dy = _nt(dob, wo_ref[...])
        ya = ya_ref[...]
        datt, dg = _rms_bwd(dy[:, 0:AW], ya, ga_ref[...])
        dga_ref[...] += dg
        _attn_operands(q_ref[...], datt, ya, l_ref[...], qw_ref, dow_ref)

        lng = lng_ref[...]
        u, vs, ug, tu, tv, vhat, rstd, mixed, ysg = _sgu_forward(us_ref, lng, lnb_ref[...], w_ref, bt_ref, mixed_s, vnb_s, tm)
        dysg, dg = _rms_bwd(dy[:, AW:D], ysg, gs_ref[...])
        dgs_ref[...] += dg
        dus_ref[:, 0:SW] = ((dysg * mixed) * _gelu_grad(u, tu)).astype(BF16)
        dmix = dysg * ug

        rr = lax.broadcasted_iota(jnp.int32, (CH, CH), 0)
        cc = lax.broadcasted_iota(jnp.int32, (CH, CH), 1)
        tril = cc <= rr
        triu = cc >= rr
        for jj in range(4):
            wta = jnp.where(triu, wt_ref[2 * jj], 0.0).astype(BF16)
            wtb = jnp.where(triu, wt_ref[2 * jj + 1], 0.0).astype(BF16)
            for ch in range(tm // CH):
                rs, cs = slice(ch * CH, (ch + 1) * CH), slice(jj * 128, (jj + 1) * 128)
                dm2 = dmix[rs, cs]
                dma = jnp.where(is_a, dm2, 0.0)
                dmb = jnp.where(is_a, 0.0, dm2)
                dma_b, dmb_b = dma.astype(BF16), dmb.astype(BF16)
                vn2 = vnb_s[rs, cs]
                dw_ref[2 * jj] += jnp.where(tril, _nt(dma_b, vn2), 0.0)
                dw_ref[2 * jj + 1] += jnp.where(tril, _nt(dmb_b, vn2), 0.0)
                dvn_s[rs, cs] = _nn(wta, dma_b) + _nn(wtb, dmb_b)
                dba = jnp.sum(dma, axis=1, keepdims=True)
                dbb = jnp.sum(dmb, axis=1, keepdims=True)
                dbt_ref[...] += jnp.where(lane == 2 * jj, dba, 0.0) + jnp.where(lane == 2 * jj + 1, dbb, 0.0)

        dvn = dvn_s[...]
        dlng_ref[...] += jnp.sum(dvn * vhat, axis=0, keepdims=True)
        dlnb_ref[...] += jnp.sum(dvn, axis=0, keepdims=True)
        dvh = dvn * lng
        dvg = rstd * (dvh - jnp.mean(dvh, axis=-1, keepdims=True) - vhat * jnp.mean(dvh * vhat, axis=-1, keepdims=True))
        dus_ref[:, SW:NUS] = (dvg * _gelu_grad(vs, tv)).astype(BF16)

    row = lambda n: pl.BlockSpec((tm, n), lambda i: (i, 0))
    return pl.pallas_call(
        body,
        name="mix_bwd",
        grid=(T // tm,),
        in_specs=[row(D), row(NUS), row(AW), row(AW), row(AW), row(D), _full((1, SW)), _full((1, SW)),
                  _full((8, CH, CH)), _full((8, CH, CH)), _full((CH, 8)), _full((1, AW)), _full((1, SW)), _full((D, D)),
                  _full((1, D))],
        out_specs=[row(4 * AW), row(4 * AW), row(NUS),
                   pl.BlockSpec((D, D), lambda i: (0, 0), pipeline_mode=pl.Buffered(1)), _full((8, CH, CH)),
                   _full((CH, 128)), _full((1, SW)), _full((1, SW)), _full((1, AW)), _full((1, SW)), _full((1, D))],
        out_shape=[jax.ShapeDtypeStruct((T, 4 * AW), BF16),
                   jax.ShapeDtypeStruct((T, 4 * AW), BF16), jax.ShapeDtypeStruct((T, NUS), BF16),
                   jax.ShapeDtypeStruct((D, D), F32),
                   jax.ShapeDtypeStruct((8, CH, CH), F32), jax.ShapeDtypeStruct((CH, 128), F32),
                   jax.ShapeDtypeStruct((1, SW), F32), jax.ShapeDtypeStruct((1, SW), F32),
                   jax.ShapeDtypeStruct((1, AW), F32), jax.ShapeDtypeStruct((1, SW), F32), jax.ShapeDtypeStruct((1, D), F32)],
        scratch_shapes=[pltpu.VMEM((tm, SW), F32), pltpu.VMEM((tm, SW), BF16), pltpu.VMEM((tm, SW), F32)],
        compiler_params=_params(1, VMEM_LIMIT_BIG),
    )(dh1, us, yatt, qkv, lsec, yb, lng, lnb, sgw, sgwt, sgbt, gatt, gsg, wout, gpm)


def _attn_operands(q, do, o, lsec, qw_ref, dow_ref):
    rows = do.shape[0]
    feat = lax.broadcasted_iota(jnp.int32, (AW, 128), 0)
    head = lax.broadcasted_iota(jnp.int32, (AW, 128), 1)
    sel = jnp.where((feat >= head * HD) & (feat < (head + 1) * HD), 1.0, 0.0)
    delta = jnp.dot(do * o, sel, precision=lax.Precision.HIGHEST, preferred_element_type=F32)
    for hp in range(4):
        cols = slice(hp * 128, (hp + 1) * 128)
        for hh in range(2):
            base = (2 * hp + hh) * 256
            lc = lsec[:, hp * 128 + hh:hp * 128 + hh + 1]
            d_h = delta[:, 2 * hp + hh:2 * hp + hh + 1]
            qw_ref[:, base:base + 128] = _head_rows(q[:, cols], hh, 0.125)
            qw_ref[:, base + 128:base + 256] = _aug_lanes(rows, AUG_A if hh == 0 else AUG_B, _split3(-lc))
            dow_ref[:, base:base + 128] = _head_rows(do[:, cols], hh, 1.0).astype(BF16)
            dow_ref[:, base + 128:base + 256] = _aug_lanes(rows, None, _split3(-d_h))


def _attn_bwd(kk, vk, kt, qw, dow, after):
    T = kk.shape[0]
    tq = _tile(T, 512)
    tk = tq
    nq = T // tq

    def body(kk_ref, vk_ref, kt_ref, qw_ref, dow_ref, after_ref, dqt_ref, dcq_ref, dk_ref, dv_ref, dck_ref, dk_s, dv_s, dck_s):
        j, i = _fold_cols(pl.program_id(1), pl.program_id(2), nq)
        sub8 = lax.broadcasted_iota(jnp.int32, (8, 1), 0)
        lane = lax.broadcasted_iota(jnp.int32, (1, 128), 1)

        @pl.when((pl.program_id(1) == 0) & (pl.program_id(2) == 0))
        def _():
            dqt_ref[...] = jnp.zeros_like(dqt_ref)
            dcq_ref[...] = jnp.zeros_like(dcq_ref)

        @pl.when(i == j)
        def _():
            dk_s[...] = jnp.zeros_like(dk_s)
            dv_s[...] = jnp.zeros_like(dv_s)
            dck_s[...] = jnp.zeros_like(dck_s)

        def step(masked):
            cols = pl.ds(pl.multiple_of(i * tq, tq), tq)
            sub = lax.broadcasted_iota(jnp.int32, (128, 1), 0)
            if masked:
                causal = lax.broadcasted_iota(jnp.int32, (tk, tq), 0) <= lax.broadcasted_iota(jnp.int32, (tk, tq), 1)

            def logits(n):
                pair, base = n // 2, n * 256
                return (_nt(kk_ref[:, pair * 256:(pair + 1) * 256], qw_ref[:, base:base + 256]),
                        _nt(vk_ref[:, pair * 256:(pair + 1) * 256], dow_ref[:, base:base + 256]))

            ahead = logits(0)
            dcq = jnp.zeros((8, tq), F32)
            dck = jnp.zeros((tk, 128), F32)
            for pp in range(PAIRS):
                lanes = slice(pp * 128, (pp + 1) * 128)
                kt2 = kt_ref[lanes, :] * 0.125
                dv = jnp.zeros((tk, 128), F32)
                dk = jnp.zeros((tk, 128), F32)
                dqts = []
                for hh in range(2):
                    head = 2 * pp + hh
                    base = head * 256
                    qw_h = qw_ref[:, base:base + 256]
                    dow_h = dow_ref[:, base:base + 256]
                    logp, dp = ahead
                    if head + 1 < 2 * PAIRS:
                        ahead = logits(head + 1)
                    pr = jnp.exp(logp)
                    if masked:
                        pr = jnp.where(causal, pr, 0.0)
                    ds = pr * dp
                    ds_b = ds.astype(BF16)
                    dv = dv + _nn(pr.astype(BF16), dow_h[:, 0:128])
                    dk = dk + _nn(ds_b, qw_h[:, 0:128])
                    dqts.append(_nn(kt2, ds_b))
                    dcq = dcq + jnp.where(sub8 == head, jnp.sum(ds, axis=0, keepdims=True), 0.0)
                    dck = dck - jnp.where(lane == head, jnp.sum(ds, axis=1, keepdims=True), 0.0)
                dv_s[:, lanes] += dv
                dk_s[:, lanes] += dk
                dqt_ref[lanes, cols] += jnp.where(sub < HD, dqts[0], dqts[1])
            dck_s[...] += dck
            dcq_ref[:, cols] += dcq

        @pl.when(i > j)
        def _():
            step(False)

        @pl.when(i == j)
        def _():
            step(True)

        @pl.when(i == nq - 1)
        def _():
            dk_ref[...] = dk_s[...].astype(BF16)
            dv_ref[...] = dv_s[...].astype(BF16)
            dck_ref[...] = dck_s[...]

    kj = lambda r, t: _fold_cols(r, t, nq)[0]
    qi = lambda r, t: _fold_cols(r, t, nq)[1]
    krow = lambda g, r, t: (kj(r, t), g)
    qrow = lambda g, r, t: (qi(r, t), g)
    return pl.pallas_call(
        body,
        name="attn_bwd",
        grid=_fold_grid(nq),
        in_specs=[
            pl.BlockSpec((tk, PAIRS * 256), krow),
            pl.BlockSpec((tk, PAIRS * 256), krow),
            pl.BlockSpec((PAIRS * 128, tk), lambda g, r, t: (g, kj(r, t))),
            pl.BlockSpec((tq, PAIRS * 512), qrow),
            pl.BlockSpec((tq, PAIRS * 512), qrow),
            pl.BlockSpec(after.shape, lambda g, r, t: (0,) * after.ndim),
        ],
        out_specs=[
            pl.BlockSpec((PAIRS * 128, T), lambda g, r, t: (g, 0), pipeline_mode=pl.Buffered(1)),
            pl.BlockSpec((8, T), lambda g, r, t: (0, 0), pipeline_mode=pl.Buffered(1)),
            pl.BlockSpec((tk, PAIRS * 128), krow),
            pl.BlockSpec((tk, PAIRS * 128), krow),
            pl.BlockSpec((tk, 128), lambda g, r, t: (kj(r, t), 0)),
        ],
        out_shape=[jax.ShapeDtypeStruct((AW, T), F32), jax.ShapeDtypeStruct((8, T), F32),
                   jax.ShapeDtypeStruct((T, AW), BF16), jax.ShapeDtypeStruct((T, AW), BF16),
                   jax.ShapeDtypeStruct((T, FPAD), F32)],
        scratch_shapes=[pltpu.VMEM((tk, PAIRS * 128), F32), pltpu.VMEM((tk, PAIRS * 128), F32),
                        pltpu.VMEM((tk, 128), F32)],
        compiler_params=_params(3),
    )(kk, vk, kt, qw, dow, after)


def _fgate_bwd(dcq, dck, fz, fb):
    T = dck.shape[0]
    tb = _tile(T, 512)
    nb = T // tb

    def body(dcq_ref, dck_ref, fz_ref, fb_ref, df_ref, dfb_ref, carry):
        @pl.when(pl.program_id(0) == 0)
        def _():
            carry[...] = jnp.zeros_like(carry)
            dfb_ref[...] = jnp.zeros_like(dfb_ref)

        head = lax.broadcasted_iota(jnp.int32, (8, FPAD), 0)
        eye = jnp.where(head == lax.broadcasted_iota(jnp.int32, (8, FPAD), 1), 1.0, 0.0)
        dcv = dck_ref[...] + lax.dot_general(dcq_ref[...], eye, (((0,), (0,)), ((), ())),
                                             precision=lax.Precision.HIGHEST, preferred_element_type=F32)
        r = lax.broadcasted_iota(jnp.int32, (128, 128), 0)
        cc = lax.broadcasted_iota(jnp.int32, (128, 128), 1)
        tri = (cc >= r).astype(F32)
        run = carry[...]
        parts = []
        for blk in reversed(range(tb // 128)):
            dc_b = dcv[blk * 128:(blk + 1) * 128, :]
            parts.append(jnp.dot(tri, dc_b, precision=lax.Precision.HIGHEST, preferred_element_type=F32) + run)
            run = run + jnp.sum(dc_b, axis=0, keepdims=True)
        carry[...] = run
        dlf = jnp.concatenate(parts[::-1], axis=0)
        lane = lax.broadcasted_iota(jnp.int32, (tb, FPAD), 1)
        df = jnp.where(lane < 8, dlf * jax.nn.sigmoid(-(fz_ref[...] + fb_ref[...])), 0.0)
        df_ref[...] = df.astype(BF16)
        dfb_ref[...] += jnp.sum(df, axis=0, keepdims=True)

    rev = pl.BlockSpec((tb, FPAD), lambda i: (nb - 1 - i, 0))
    return pl.pallas_call(
        body,
        name="fgate_bwd",
        grid=(nb,),
        in_specs=[pl.BlockSpec((8, tb), lambda i: (0, nb - 1 - i)), rev, rev, _full((1, FPAD))],
        out_specs=[rev, _full((1, FPAD))],
        out_shape=[jax.ShapeDtypeStruct((T, FPAD), BF16), jax.ShapeDtypeStruct((1, FPAD), F32)],
        scratch_shapes=[pltpu.VMEM((1, FPAD), F32)],
        compiler_params=_params(1),
    )(dcq, dck, fz, fb)


def _inproj_bwd(dqt, dk, dv, dus, dfz, wtp, x, dh1, g):
    T = x.shape[0]
    tm = _tile(T, 512)

    def body(dq_ref, dk_ref, dv_ref, dus_ref, dfz_ref, w_ref, x_ref, dh1_ref, g_ref, gx_ref, dg_ref):
        @pl.when(pl.program_id(0) == 0)
        def _():
            dg_ref[...] = jnp.zeros_like(dg_ref)

        da = _tn(dq_ref[...].astype(BF16), w_ref[0:AW, :])
        da += _nn(dk_ref[...], w_ref[AW:2 * AW, :])
        da += _nn(dv_ref[...], w_ref[2 * AW:NQKV, :])
        da += _nn(dus_ref[...], w_ref[NQKV:NQKV + NUS, :])
        da += _nn(dfz_ref[...], w_ref[NQKV + NUS:ZP, :])
        dx, dg = _rms_bwd(da, x_ref[...], g_ref[...])
        gx_ref[...] = dh1_ref[...] + dx
        dg_ref[...] += dg

    row = lambda n: pl.BlockSpec((tm, n), lambda i: (i, 0))
    return pl.pallas_call(
        body,
        name="inproj_bwd",
        grid=(T // tm,),
        in_specs=[pl.BlockSpec((AW, tm), lambda i: (0, i)), row(AW), row(AW), row(NUS), row(FPAD), _full((ZP, D)),
                  row(D), row(D), _full((1, D))],
        out_specs=[row(D), _full((1, D))],
        out_shape=[jax.ShapeDtypeStruct((T, D), F32), jax.ShapeDtypeStruct((1, D), F32)],
        compiler_params=_params(1),
    )(dqt, dk, dv, dus, dfz, wtp, x, dh1, g)


def _sq_relu(f1):
    r = jnp.maximum(f1.astype(F32), 0.0)
    return (r * r).astype(BF16)


def _wgrad(a, bs, name, a_fn=None, out_dtype=F32):
    T, K = a.shape
    tt = _tile(T, 1024)
    nb = len(bs)
    narrow = out_dtype != F32

    def out_dims(b, layout):
        if layout == "t":
            return (K, b.shape[0])
        N = b.shape[1]
        if layout is None:
            return (K, N)
        return (N // layout[1], K, layout[1]) if layout[0] == "col" else (K // layout[1], layout[1], N)

    shapes = [out_dims(b, layout) for b, layout in bs]

    def body(*refs):
        a_ref, b_refs, o_refs = refs[0], refs[1:1 + nb], refs[1 + nb:1 + 2 * nb]
        accs = refs[1 + 2 * nb:] if narrow else o_refs

        @pl.when(pl.program_id(0) == 0)
        def _():
            for acc in accs:
                acc[...] = jnp.zeros_like(acc)

        av = a_ref[...] if a_fn is None else a_fn(a_ref[...])
        at = av.astype(BF16).T
        for (b, layout), b_ref, o_ref in zip(bs, b_refs, accs):
            if layout is None:
                o_ref[...] += _nn(at, b_ref[...].astype(BF16))
            elif layout == "t":
                o_ref[...] += _nt(at, b_ref[...].astype(BF16))
            elif layout[0] == "col":
                n = layout[1]
                for k in range(b.shape[1] // n):
                    o_ref[k] += _nn(at, b_ref[:, k * n:(k + 1) * n].astype(BF16))
            else:
                n = layout[1]
                bv = b_ref[...].astype(BF16)
                for k in range(K // n):
                    o_ref[k] += _nn(at[k * n:(k + 1) * n, :], bv)

        if narrow:
            @pl.when(pl.program_id(0) == T // tt - 1)
            def _():
                for o_ref, acc in zip(o_refs, accs):
                    o_ref[...] = acc[...].astype(out_dtype)

    once = lambda shape: pl.BlockSpec(shape, lambda t: (0,) * len(shape), pipeline_mode=pl.Buffered(1))
    res = pl.pallas_call(
        body,
        name=name,
        grid=(T // tt,),
        in_specs=[pl.BlockSpec((tt, K), lambda t: (t, 0))] + [
            pl.BlockSpec((b.shape[0], tt), lambda t: (0, t)) if layout == "t" else pl.BlockSpec((tt, b.shape[1]), lambda t: (t, 0))
            for b, layout in bs],
        out_specs=[once(s) for s in shapes],
        out_shape=[jax.ShapeDtypeStruct(s, out_dtype) for s in shapes],
        scratch_shapes=[pltpu.VMEM(s, F32) for s in shapes] if narrow else [],
        compiler_params=_params(1, VMEM_LIMIT_BIG),
    )(a, *[b for b, _ in bs])
    return res


def _adam_math(w, g, m, v):
    m = ADAM_B1 * m + (1.0 - ADAM_B1) * g
    v = ADAM_B2 * v + (1.0 - ADAM_B2) * (g * g)
    m_hat = m / (1.0 - ADAM_B1 ** ADAM_STEP)
    v_hat = v / (1.0 - ADAM_B2 ** ADAM_STEP)
    delta = -ADAM_LR * (m_hat / (jnp.sqrt(v_hat) + ADAM_EPS) + ADAM_WD * w)
    return delta, m, v


def _adam(parts, w, m, v, name):
    R, C = w.shape
    br = 128 if R % 128 == 0 else R

    def body(p_ref, w_ref, m_ref, v_ref, g_ref, d_ref, nm_ref, nv_ref):
        g = p_ref[0].astype(F32)
        for s in range(1, NDEV):
            g = g + p_ref[s].astype(F32)
        g_ref[...] = g
        d_ref[...], nm_ref[...], nv_ref[...] = _adam_math(w_ref[...], g, m_ref[...], v_ref[...])

    blk = pl.BlockSpec((br, C), lambda i: (i, 0))
    return pl.pallas_call(
        body,
        name=name,
        grid=(R // br,),
        in_specs=[pl.BlockSpec((NDEV, br, C), lambda i: (0, i, 0)), blk, blk, blk],
        out_specs=[blk] * 4,
        out_shape=[jax.ShapeDtypeStruct((R, C), F32)] * 4,
        compiler_params=_params(1),
    )(parts, w, m, v)


_SMALL = (("sg_w", 8 * CH * CH), ("f_bias", 8), ("sg_ln_g", SW), ("sg_ln_b", SW), ("sg_b", 8 * CH), ("att_out_g", AW),
          ("sg_out_g", SW), ("pre_mix_g", D), ("post_mix_g", D), ("pre_ffn_g", D), ("post_ffn_g", D), ("ple_gate_b", D))
_SEG = 8 * 128


def _seg_rows(size):
    return 8 * (-(-size // _SEG))


def _pack(vals, loss_acc):
    parts = []
    for name, size in _SMALL:
        flat = vals[name].reshape(-1)
        rows = _seg_rows(size)
        parts.append(jnp.pad(flat, (0, rows * 128 - size)).reshape(rows, 128))
    parts.append(loss_acc)
    return jnp.concatenate(parts, axis=0)


def _adam_small(parts, ws, ms, vs):
    n = len(_SMALL)
    names = [name for name, _ in _SMALL]

    def body(*refs):
        p_ref, w_refs, m_refs, v_refs = refs[0], refs[1:1 + n], refs[1 + n:1 + 2 * n], refs[1 + 2 * n:1 + 3 * n]
        loss_ref, outs = refs[1 + 3 * n], refs[2 + 3 * n:]
        g_all = p_ref[0]
        for s in range(1, NDEV):
            g_all = g_all + p_ref[s]
        r = 0
        for k, (name, size) in enumerate(_SMALL):
            dst = [outs[kind * n + k] for kind in range(4)]

            def update(g, idx):
                vals = (g,) + _adam_math(w_refs[k][idx], g, m_refs[k][idx], v_refs[k][idx])
                for d, val in zip(dst, vals):
                    d[idx] = val

            if name == "sg_w":
                for grp in range(8):
                    update(g_all[r + grp * CH:r + (grp + 1) * CH, :], (0, grp))
            elif name == "sg_b":
                update(g_all[r:r + 8, :], (0,))
            elif name == "f_bias":
                update(g_all[r:r + 1, 0:8], (slice(None),))
            else:
                update(jnp.concatenate([g_all[r + q:r + q + 1, :] for q in range(size // 128)], axis=1), (slice(None),))
            r += _seg_rows(size)
        loss_ref[...] = g_all[r:r + 1, 0:1] * (0.5 / D)

    arrs = [parts] + [d[name] for d in (ws, ms, vs) for name in names]
    res = pl.pallas_call(
        body,
        name="adam_small",
        in_specs=[_full(a.shape) for a in arrs],
        out_specs=[_full((1, 1))] + [_full(ws[name].shape) for _ in range(4) for name in names],
        out_shape=[jax.ShapeDtypeStruct((1, 1), F32)] + [jax.ShapeDtypeStruct(ws[name].shape, F32) for _ in range(4) for name in names],
        compiler_params=pltpu.CompilerParams(vmem_limit_bytes=VMEM_LIMIT),
    )(*arrs)
    return res[0], {name: [res[1 + kind * n + k] for kind in range(4)] for k, name in enumerate(names)}


def kernel(x, p, w_in, f_bias, sg_ln_g, sg_ln_b, sg_w, sg_b, att_out_g, sg_out_g, w_out, pre_mix_g, post_mix_g, pre_ffn_g, post_ffn_g, w_ff1, w_ff2, ple_w, ple_gate_w, ple_gate_b, loss_target, m_w_in, m_f_bias, m_sg_ln_g, m_sg_ln_b, m_sg_w, m_sg_b, m_att_out_g, m_sg_out_g, m_w_out, m_pre_mix_g, m_post_mix_g, m_pre_ffn_g, m_post_ffn_g, m_w_ff1, m_w_ff2, m_ple_w, m_ple_gate_w, m_ple_gate_b, v_w_in, v_f_bias, v_sg_ln_g, v_sg_ln_b, v_sg_w, v_sg_b, v_att_out_g, v_sg_out_g, v_w_out, v_pre_mix_g, v_post_mix_g, v_pre_ffn_g, v_post_ffn_g, v_w_ff1, v_w_ff2, v_ple_w, v_ple_gate_w, v_ple_gate_b):
    small_w = dict(sg_w=sg_w, f_bias=f_bias, sg_ln_g=sg_ln_g, sg_ln_b=sg_ln_b, sg_b=sg_b, att_out_g=att_out_g,
                   sg_out_g=sg_out_g, pre_mix_g=pre_mix_g, post_mix_g=post_mix_g, pre_ffn_g=pre_ffn_g,
                   post_ffn_g=post_ffn_g, ple_gate_b=ple_gate_b)
    small_m = dict(sg_w=m_sg_w, f_bias=m_f_bias, sg_ln_g=m_sg_ln_g, sg_ln_b=m_sg_ln_b, sg_b=m_sg_b, att_out_g=m_att_out_g,
                   sg_out_g=m_sg_out_g, pre_mix_g=m_pre_mix_g, post_mix_g=m_post_mix_g, pre_ffn_g=m_pre_ffn_g,
                   post_ffn_g=m_post_ffn_g, ple_gate_b=m_ple_gate_b)
    small_v = dict(sg_w=v_sg_w, f_bias=v_f_bias, sg_ln_g=v_sg_ln_g, sg_ln_b=v_sg_ln_b, sg_b=v_sg_b, att_out_g=v_att_out_g,
                   sg_out_g=v_sg_out_g, pre_mix_g=v_pre_mix_g, post_mix_g=v_post_mix_g, pre_ffn_g=v_pre_ffn_g,
                   post_ffn_g=v_post_ffn_g, ple_gate_b=v_ple_gate_b)
    big = dict(w_in=(w_in, m_w_in, v_w_in), w_out=(w_out, m_w_out, v_w_out), w_ff1=(w_ff1, m_w_ff1, v_w_ff1),
               w_ff2=(w_ff2, m_w_ff2, v_w_ff2), ple_w=(ple_w, m_ple_w, v_ple_w),
               ple_gate_w=(ple_gate_w, m_ple_gate_w, v_ple_gate_w))

    xt, pt, tgt = x[0], p[0, 0], loss_target[0]
    ws = W_IN_COLS // NDEV

    gw_in = _gather(jnp.pad(jnp.transpose(w_in[0]).astype(BF16), ((0, WS_PAD - ws), (0, 0))), "gather_w_in")
    gw_in, others = lax.optimization_barrier((gw_in, [w_out[0].astype(BF16), w_ff1[0].astype(BF16), w_ff2[0].astype(BF16),
                                                      ple_w[0].astype(BF16), ple_gate_w[0].astype(BF16)]))
    rest = _xchg_start(others, True, "gather_rest_start")
    wint = gw_in[:, 0:ws, :].reshape(W_IN_COLS, D)
    wtp = jnp.concatenate([wint[0:NQKV], wint[NQKV + 8:W_IN_COLS], wint[NQKV:NQKV + 8],
                           jnp.zeros((FPAD - 8, D), BF16)], axis=0)

    fb = jnp.pad(f_bias.astype(F32), ((0, 0), (0, FPAD - 8)))
    sgw = sg_w[0]
    sgwt = jnp.transpose(sg_w[0], (0, 2, 1))
    sgbt = jnp.transpose(sg_b[0])

    qkv, kt, vt, us, fz, ab = _inproj(xt, pre_mix_g + rest[4][0, 0], wtp)
    ct, kk, vk = _fcum(fz, fb, qkv)
    yatt, lsec = _attn_fwd(qkv, kk, vt, ct)
    gw_out, gw1, gw2, gwpe, gwg = _xchg_wait(rest, True, yatt, "gather_rest_wait")
    wout = gw_out.reshape(D, D)
    wg = gwg.reshape(D, D)
    wpe = jnp.transpose(gwpe, (1, 0, 2)).reshape(PLE, D)
    h1, yb = _sgu_out(us, yatt, xt, sg_ln_g, sg_ln_b, sgw, sgbt, att_out_g, sg_out_g, wout, post_mix_g)
    c2b, f1b, ff, h2 = _ffn_fwd(h1, pre_ffn_g, gw1, gw2, post_ffn_g)
    dh2, dbg, loss_acc, g_g, g_pe = _ple_loss(h2, pt, tgt, wg, ple_gate_b, wpe)
    g_g = g_g.reshape(NDEV, D // NDEV, D)
    g_pe = jnp.transpose(g_pe.reshape(PLE, NDEV, D // NDEV), (1, 0, 2))

    dffb, df1, dh1, dgpostffn, dgpreffn = _ffn_bwd(dh2, ff, h1, f1b, gw1, gw2, post_ffn_g, pre_ffn_g)
    (g_1,) = _wgrad(c2b, [(df1, ("col", DFF // NDEV))], "wgrad_ff1")
    (g_2,) = _wgrad(f1b, [(dffb, ("row", DFF // NDEV))], "wgrad_ff2", a_fn=_sq_relu)
    early = _xchg_start([g_1, g_2, g_pe, g_g], False, "scatter_early_start")
    qw, dow, dus, g_out, dsgw, dsgbt, dlng, dlnb, dgatt, dgsg, dgpostmix = _mix_bwd(
        dh1, us, yatt, qkv, lsec, yb, sg_ln_g, sg_ln_b, sgw, sgwt, sgbt, att_out_g, sg_out_g, wout,
        post_mix_g + early[4][0, 0])
    mid = _xchg_start([g_out.reshape(NDEV, D // NDEV, D)], False, "scatter_mid_start")
    dqt, dcq, dk, dv, dck = _attn_bwd(kk, vk, kt, qw, dow, mid[4])
    dfz, dfb = _fgate_bwd(dcq, dck, fz, fb)

    gq, gk, gv, gus, gf = _wgrad(ab, [(dqt, "t"), (dk, None), (dv, None), (dus, None), (dfz, None)], "wgrad_in",
                                 out_dtype=BF16)
    g_in = jnp.concatenate([gq, gk, gv, gf[:, 0:8], gus], axis=1)
    g_in = jnp.transpose(g_in.reshape(D, NDEV, ws), (1, 0, 2))
    late = _xchg_start([g_in], False, "scatter_late_start")
    grad_x, dgpremix = _inproj_bwd(dqt, dk, dv, dus, dfz, wtp, xt, dh1, pre_mix_g + late[4][0, 0])

    small_g = dict(sg_w=dsgw, f_bias=dfb[:, 0:8], sg_ln_g=dlng, sg_ln_b=dlnb, sg_b=jnp.transpose(dsgbt[:, 0:8]),
                   att_out_g=dgatt, sg_out_g=dgsg, pre_mix_g=dgpremix, post_mix_g=dgpostmix, pre_ffn_g=dgpreffn,
                   post_ffn_g=dgpostffn, ple_gate_b=dbg)

    r_1, r_2, r_pe, r_g = _xchg_wait(early, False, grad_x, "scatter_early_wait")
    (r_out,) = _xchg_wait(mid, False, grad_x, "scatter_mid_wait")
    r_small = _gather(_pack(small_g, loss_acc), "gather_small_grads")

    res = {}

    def adam_big(name, parts):
        w, m, v = big[name]
        res[name] = [t[None] for t in _adam(parts, w[0], m[0], v[0], "adam_" + name)]

    for name, parts in (("w_out", r_out), ("w_ff1", r_1), ("w_ff2", r_2), ("ple_w", r_pe), ("ple_gate_w", r_g)):
        adam_big(name, parts)
    (r_in,) = _xchg_wait(late, False, res["w_ff1"][0], "scatter_late_wait")
    adam_big("w_in", r_in)
    loss, small = _adam_small(r_small, small_w, small_m, small_v)
    res.update(small)

    order = ["w_in", "f_bias", "sg_ln_g", "sg_ln_b", "sg_w", "sg_b", "att_out_g", "sg_out_g", "w_out", "pre_mix_g",
             "post_mix_g", "pre_ffn_g", "post_ffn_g", "w_ff1", "w_ff2", "ple_w", "ple_gate_w", "ple_gate_b"]
    outs = [loss[0, 0], grad_x[None]]
    for kind in range(4):
        outs += [res[name][kind] for name in order]
    return tuple(outs)
```

```python
import jax
import jax.numpy as jnp
from jax import lax
from jax.experimental import pallas as pl
from jax.experimental.pallas import tpu as pltpu

F32 = jnp.float32
BF16 = jnp.bfloat16

NDEV = 8
D = 1024
AW = 512
SW = 512
HD = 64
CH = 128
DFF = 4096
PLE = 256
NQKV = 3 * AW
NUS = 2 * SW
FPAD = 128
ZP = NQKV + NUS + FPAD
W_IN_COLS = 2568
WS_PAD = 336
EPS = 1e-6
MASKV = -1e30
GELU_K = 0.7978845608028654
GELU_C = 0.044715

ADAM_LR = 0.001
ADAM_B1 = 0.9
ADAM_B2 = 0.999
ADAM_EPS = 1e-08
ADAM_WD = 0.01
ADAM_STEP = 10

VMEM_LIMIT = 48 * 1024 * 1024
VMEM_LIMIT_BIG = 60 * 1024 * 1024


def _nn(a, b):
    return jnp.dot(a, b, preferred_element_type=F32)


def _nt(a, b):
    return lax.dot_general(a, b, (((1,), (1,)), ((), ())), preferred_element_type=F32)


def _tn(a, b):
    return lax.dot_general(a, b, (((0,), (0,)), ((), ())), preferred_element_type=F32)


def _tile(n, pref):
    return min(n, pref)


def _params(n_axes, vmem=VMEM_LIMIT):
    return pltpu.CompilerParams(dimension_semantics=("arbitrary",) * n_axes, vmem_limit_bytes=vmem)


def _full(shape):
    nd = len(shape)
    return pl.BlockSpec(shape, lambda *_: (0,) * nd)


def _rms_fwd(x, g):
    r = lax.rsqrt(jnp.mean(x * x, axis=-1, keepdims=True) + EPS)
    return x * r * g


def _rms_bwd(dy, x, g):
    n = x.shape[-1]
    r = lax.rsqrt(jnp.mean(x * x, axis=-1, keepdims=True) + EPS)
    u = dy * g
    s = jnp.sum(x * u, axis=-1, keepdims=True)
    dx = r * u - x * (r * r * r * (s * (1.0 / n)))
    dg = jnp.sum(dy * (x * r), axis=0, keepdims=True)
    return dx, dg


def _gelu(x):
    t = jnp.tanh(x * (GELU_K + (GELU_K * GELU_C) * (x * x)))
    return x * (0.5 + 0.5 * t), t


def _gelu_grad(x, t):
    return (0.5 + 0.5 * t) + (0.5 * x) * (1.0 - t * t) * (GELU_K + (3.0 * GELU_K * GELU_C) * (x * x))


def _gather(arr, name):
    def body(x_ref, out_ref, send, recv, loc):
        x, y, c = lax.axis_index("x"), lax.axis_index("y"), lax.axis_index("c")
        me, sibling = (x, y, c), (x, y, 1 - c)
        chips = [(1 - x, y), (x, 1 - y), (1 - x, 1 - y)]

        def slot(px, py, pc):
            return out_ref.at[4 * px + 2 * py + pc]

        def copy(k, block, to, src=None):
            return pltpu.make_async_remote_copy(
                src_ref=slot(*block) if src is None else src, dst_ref=slot(*block), send_sem=send.at[k],
                recv_sem=recv.at[k], device_id=to, device_id_type=pl.DeviceIdType.MESH)

        mine = pltpu.make_async_copy(x_ref, slot(*me), loc)
        mine.start()
        first = [copy(0, me, sibling, src=x_ref)] + [copy(1 + j, me, (*chip, c), src=x_ref) for j, chip in enumerate(chips)]
        for cp in first:
            cp.start()
        passed = [copy(4 + j, (*chip, c), sibling) for j, chip in enumerate(chips)]
        for j, chip in enumerate(chips):
            copy(1 + j, (*chip, c), me).wait_recv()
            passed[j].start()
        copy(0, sibling, me).wait_recv()
        for j, chip in enumerate(chips):
            copy(4 + j, (*chip, 1 - c), me).wait_recv()
        for cp in first + passed:
            cp.wait_send()
        mine.wait()

    hbm = pl.BlockSpec(memory_space=pltpu.HBM)
    return pl.pallas_call(
        body,
        name=name,
        out_shape=jax.ShapeDtypeStruct((NDEV,) + arr.shape, arr.dtype),
        in_specs=[hbm],
        out_specs=hbm,
        scratch_shapes=[pltpu.SemaphoreType.DMA((NDEV - 1,)), pltpu.SemaphoreType.DMA((NDEV - 1,)), pltpu.SemaphoreType.DMA],
    )(arr)


def _peers(x, y, c):
    out = []
    for k in range(1, NDEV):
        out.append((1 - x if (k >> 2) & 1 else x, 1 - y if (k >> 1) & 1 else y, 1 - c if k & 1 else c))
    return out


def _xchg_start(arrs, gather, name):
    n = len(arrs)
    me = 4 * lax.axis_index("x") + 2 * lax.axis_index("y") + lax.axis_index("c")
    lands = []
    for a in arrs:
        shape = ((NDEV,) + a.shape) if gather else a.shape
        own = a[None] if gather else lax.dynamic_slice_in_dim(a, me, 1, axis=0)
        lands.append(lax.dynamic_update_slice_in_dim(lax.empty(shape, a.dtype), own, me, axis=0))

    def body(*refs):
        ins, lnd = refs[:n], refs[n:2 * n]
        send, recv, token = refs[2 * n:3 * n], refs[3 * n:4 * n], refs[-1]
        x, y, c = lax.axis_index("x"), lax.axis_index("y"), lax.axis_index("c")
        mine = 4 * x + 2 * y + c
        for px, py, pc in _peers(x, y, c):
            peer = 4 * px + 2 * py + pc
            for a in range(n):
                pltpu.make_async_remote_copy(
                    src_ref=ins[a] if gather else ins[a].at[peer],
                    dst_ref=lnd[a].at[mine],
                    send_sem=send[a],
                    recv_sem=recv[a],
                    device_id=(px, py, pc),
                    device_id_type=pl.DeviceIdType.MESH,
                ).start()
        token[...] = jnp.zeros_like(token)

    hbm = pl.BlockSpec(memory_space=pltpu.HBM)
    sem = pl.BlockSpec(memory_space=pltpu.SEMAPHORE)
    res = pl.pallas_call(
        body,
        name=name,
        out_shape=(*[pltpu.SemaphoreType.DMA(())] * (2 * n),
                   *[pltpu.HBM(a.shape, a.dtype) for a in arrs], *[pltpu.HBM(l.shape, l.dtype) for l in lands],
                   jax.ShapeDtypeStruct((8, 128), F32)),
        in_specs=[hbm] * (2 * n),
        out_specs=(*([sem] * (2 * n)), *([hbm] * (2 * n)), pl.BlockSpec(memory_space=pltpu.VMEM)),
        input_output_aliases={i: 2 * n + i for i in range(2 * n)},
        compiler_params=pltpu.CompilerParams(has_side_effects=pltpu.SideEffectType.DATAFLOW_SIDE_EFFECTING),
    )(*[pltpu.with_memory_space_constraint(a, pltpu.HBM) for a in arrs],
      *[pltpu.with_memory_space_constraint(l, pltpu.HBM) for l in lands])
    return list(res[0:n]), list(res[n:2 * n]), list(res[2 * n:3 * n]), list(res[3 * n:4 * n]), res[-1]


def _xchg_wait(started, gather, after, name):
    send, recv, srcs, lands, _ = started
    n = len(srcs)

    def body(*refs):
        lnd = refs[n:2 * n]
        send, recv = refs[2 * n:3 * n], refs[3 * n:4 * n]
        me = (lax.axis_index("x"), lax.axis_index("y"), lax.axis_index("c"))
        for a in range(n):
            seven = lnd[a].at[pl.ds(0, NDEV - 1)]
            cp = pltpu.make_async_remote_copy(src_ref=seven, dst_ref=seven, send_sem=send[a], recv_sem=recv[a],
                                              device_id=me, device_id_type=pl.DeviceIdType.MESH)
            cp.wait_send()
            cp.wait_recv()

    hbm = pl.BlockSpec(memory_space=pltpu.HBM)
    sem = pl.BlockSpec(memory_space=pltpu.SEMAPHORE)
    res = pl.pallas_call(
        body,
        name=name,
        out_shape=tuple([pltpu.HBM(a.shape, a.dtype) for a in srcs] + [pltpu.HBM(l.shape, l.dtype) for l in lands]),
        in_specs=[hbm] * (2 * n) + [sem] * (2 * n) + [pl.BlockSpec(memory_space=pl.ANY)],
        out_specs=tuple([hbm] * (2 * n)),
        input_output_aliases={i: i for i in range(2 * n)},
        compiler_params=pltpu.CompilerParams(has_side_effects=pltpu.SideEffectType.DATAFLOW_SIDE_EFFECTING),
    )(*srcs, *lands, *send, *recv, after)
    return list(res[n:])


def _inproj(x, g, wtp):
    T = x.shape[0]
    tm = _tile(T, 512)

    def body(x_ref, g_ref, w_ref, qkv_ref, kt_ref, vt_ref, us_ref, fz_ref, ab_ref):
        a = _rms_fwd(x_ref[...], g_ref[...]).astype(BF16)
        ab_ref[...] = a
        qkv_ref[:, 0:AW] = _nt(a, w_ref[0:AW, :]).astype(BF16)
        kk = _nt(a, w_ref[AW:2 * AW, :])
        qkv_ref[:, AW:2 * AW] = kk.astype(BF16)
        kt_ref[...] = kk.T.astype(BF16)
        vv = _nt(a, w_ref[2 * AW:NQKV, :])
        qkv_ref[:, 2 * AW:NQKV] = vv.astype(BF16)
        vvt = vv.T.astype(BF16)
        one_row = jnp.where(lax.broadcasted_iota(jnp.int32, (HD, tm), 0) == 0, 1.0, 0.0).astype(BF16)
        for h in range(AW // HD):
            vt_ref[2 * h * HD:(2 * h + 1) * HD, :] = vvt[h * HD:(h + 1) * HD, :]
            vt_ref[(2 * h + 1) * HD:(2 * h + 2) * HD, :] = one_row
        us_ref[...] = _nt(a, w_ref[NQKV:NQKV + NUS, :])
        fz_ref[...] = _nt(a, w_ref[NQKV + NUS:ZP, :])

    row = lambda n: pl.BlockSpec((tm, n), lambda i: (i, 0))
    col = lambda n: pl.BlockSpec((n, tm), lambda i: (0, i))
    return pl.pallas_call(
        body,
        name="inproj",
        grid=(T // tm,),
        in_specs=[row(D), _full((1, D)), _full((ZP, D))],
        out_specs=[row(NQKV), col(AW), col(2 * AW), row(NUS), row(FPAD), row(D)],
        out_shape=[
            jax.ShapeDtypeStruct((T, NQKV), BF16),
            jax.ShapeDtypeStruct((AW, T), BF16),
            jax.ShapeDtypeStruct((2 * AW, T), BF16),
            jax.ShapeDtypeStruct((T, NUS), F32),
            jax.ShapeDtypeStruct((T, FPAD), F32),
            jax.ShapeDtypeStruct((T, D), BF16),
        ],
        compiler_params=_params(1),
    )(x, g, wtp)


def _log_sigmoid(z):
    return jnp.minimum(z, 0.0) - jnp.log1p(jnp.exp(-jnp.abs(z)))


def _split3(x):
    hi = x.astype(BF16)
    r1 = x - hi.astype(F32)
    mid = r1.astype(BF16)
    lo = (r1 - mid.astype(F32)).astype(BF16)
    return hi, mid, lo


AUG_A, AUG_B, AUG_ONE = 0, 3, 6


def _aug_lanes(rows, first_one, pieces):
    lane = lax.broadcasted_iota(jnp.int32, (rows, 128), 1)
    out = jnp.zeros((rows, 128), F32)
    if first_one is not None:
        out = jnp.where((lane >= first_one) & (lane < first_one + 3), 1.0, out)
    for n, piece in enumerate(pieces):
        out = jnp.where(lane == AUG_ONE + n, piece.astype(F32), out)
    return out.astype(BF16)


def _fcum(fz, fb, qkv):
    T = fz.shape[0]
    tb = _tile(T, 512)

    def body(fz_ref, fb_ref, k_ref, v_ref, ct_ref, kk_ref, vk_ref, carry, tri_s, sel_s):
        @pl.when(pl.program_id(0) == 0)
        def _():
            carry[...] = jnp.zeros_like(carry)
            src = lax.broadcasted_iota(jnp.int32, (128, 128), 0)
            dst = lax.broadcasted_iota(jnp.int32, (128, 128), 1)
            tri_s[...] = (dst <= src).astype(F32)
            for hp in range(4):
                for n in range(3):
                    pick = ((src == 2 * hp) & (dst == AUG_A + n)) | ((src == 2 * hp + 1) & (dst == AUG_B + n))
                    sel_s[3 * hp + n] = jnp.where(pick, -1.0, 0.0).astype(BF16)

        lf = _log_sigmoid(fz_ref[...] + fb_ref[...])
        run = carry[...]
        parts = []
        for blk in range(tb // 128):
            lf_b = lf[blk * 128:(blk + 1) * 128, :]
            parts.append(jnp.dot(tri_s[...], lf_b, precision=lax.Precision.HIGHEST, preferred_element_type=F32) + run)
            run = run + jnp.sum(lf_b, axis=0, keepdims=True)
        carry[...] = run
        cs = jnp.concatenate(parts, axis=0)
        ct_ref[...] = cs.T[0:8, :]

        pieces = _split3(cs)
        lane = lax.broadcasted_iota(jnp.int32, (1, 128), 1)
        ones = jnp.where((lane >= AUG_ONE) & (lane < AUG_ONE + 3), 1.0, 0.0)
        for hp in range(4):
            aug = jnp.zeros((tb, 128), F32) + ones
            for n, piece in enumerate(pieces):
                aug = aug + _nn(piece, sel_s[3 * hp + n])
            aug = aug.astype(BF16)
            kk_ref[:, hp * 256:hp * 256 + 128] = k_ref[:, hp * 128:(hp + 1) * 128]
            kk_ref[:, hp * 256 + 128:(hp + 1) * 256] = aug
            vk_ref[:, hp * 256:hp * 256 + 128] = v_ref[:, hp * 128:(hp + 1) * 128]
            vk_ref[:, hp * 256 + 128:(hp + 1) * 256] = aug

    row = lambda n, c: pl.BlockSpec((tb, n), lambda i: (i, c))
    return pl.pallas_call(
        body,
        name="fcum",
        grid=(T // tb,),
        in_specs=[row(FPAD, 0), _full((1, FPAD)), row(AW, 1), row(AW, 2)],
        out_specs=[pl.BlockSpec((8, tb), lambda i: (0, i)), row(2 * AW, 0), row(2 * AW, 0)],
        out_shape=[jax.ShapeDtypeStruct((8, T), F32), jax.ShapeDtypeStruct((T, 2 * AW), BF16),
                   jax.ShapeDtypeStruct((T, 2 * AW), BF16)],
        scratch_shapes=[pltpu.VMEM((1, FPAD), F32), pltpu.VMEM((128, 128), F32), pltpu.VMEM((12, 128, 128), BF16)],
        compiler_params=_params(1),
    )(fz, fb, qkv, qkv)


def _fold_rows(r, t, nq):
    if nq == 1:
        return r, t
    low = t <= r
    return jnp.where(low, r, nq - 1 - r), jnp.where(low, t, t - r - 1)


def _fold_cols(r, t, nq):
    if nq == 1:
        return r, t
    first = t < nq - r
    j = jnp.where(first, r, nq - 1 - r)
    return j, jnp.where(first, r + t, nq - 1 - r + (t - (nq - r)))


PAIRS = 4


def _fold_grid(nq):
    assert (nq == 1 or nq % 2 == 0) and PAIRS == 4
    return (4 // PAIRS, 1, 1) if nq == 1 else (4 // PAIRS, nq // 2, nq + 1)


def _head_rows(x2, hh, scale):
    is_a = lax.broadcasted_iota(jnp.int32, (1, 128), 1) < HD
    keep = is_a if hh == 0 else jnp.logical_not(is_a)
    return jnp.where(keep, x2, jnp.zeros_like(x2)) * scale


def _attn_fwd(qkv, kk, vt, ct):
    T = qkv.shape[0]
    tq = _tile(T, 512)
    tk = tq
    nq = T // tq

    def body(q_ref, kk_ref, vt_ref, ctq_ref, o_ref, lsec_ref, qw_s, m_s, acc_s):
        i, j = _fold_rows(pl.program_id(1), pl.program_id(2), nq)
        sub8 = lax.broadcasted_iota(jnp.int32, (8, 1), 0)

        def cref_of(pp, hh):
            head = 2 * (PAIRS * pl.program_id(0) + pp) + hh
            return jnp.sum(jnp.where(sub8 == head, ctq_ref[:, 0:1], 0.0), axis=0, keepdims=True)

        @pl.when(j == 0)
        def _():
            for pp in range(PAIRS):
                q2 = q_ref[:, pp * 128:(pp + 1) * 128]
                for hh in range(2):
                    rows = slice(hh * tq, (hh + 1) * tq)
                    qw_s[pp, rows, 0:128] = _head_rows(q2, hh, 0.125)
                    qw_s[pp, rows, 128:256] = _aug_lanes(tq, AUG_A if hh == 0 else AUG_B, _split3(cref_of(pp, hh)))
            m_s[...] = jnp.full_like(m_s, MASKV)
            acc_s[...] = jnp.zeros_like(acc_s)

        def step(masked):
            if masked:
                causal = lax.broadcasted_iota(jnp.int32, (tk, tq), 0) <= lax.broadcasted_iota(jnp.int32, (tk, tq), 1)
            logits = lambda pp: _nt(kk_ref[:, pp * 256:(pp + 1) * 256], qw_s[pp])
            s_next = logits(0)
            for pp in range(PAIRS):
                s2 = s_next
                if pp + 1 < PAIRS:
                    s_next = logits(pp + 1)
                for hh in range(2):
                    n = 2 * pp + hh
                    s = s2[:, hh * tq:(hh + 1) * tq]
                    if masked:
                        s = jnp.where(causal, s, MASKV)
                    m_prev = m_s[n]
                    m_new = jnp.maximum(m_prev, jnp.max(s, axis=0, keepdims=True))
                    pr = jnp.exp(s - m_new)
                    m_s[n] = m_new
                    acc_s[n] = jnp.exp(m_prev - m_new) * acc_s[n] + _nn(vt_ref[n * 128:(n + 1) * 128, :], pr.astype(BF16))

        @pl.when(j < i)
        def _():
            step(False)

        @pl.when(j == i)
        def _():
            step(True)
            sub = lax.broadcasted_iota(jnp.int32, (128, 1), 0)
            for pp in range(PAIRS):
                outs, lrow = [], []
                for hh in range(2):
                    n = 2 * pp + hh
                    den = acc_s[n, HD:HD + 1, :]
                    outs.append(acc_s[n, 0:HD, :] * (1.0 / den))
                    lrow.append(m_s[n] + jnp.log(den) - cref_of(pp, hh))
                o_ref[:, pp * 128:(pp + 1) * 128] = jnp.concatenate(outs, axis=0).T
                lsec_ref[:, pp * 128:(pp + 1) * 128] = jnp.where(sub == 0, lrow[0], jnp.where(sub == 1, lrow[1], 0.0)).T

    qi = lambda r, t: _fold_rows(r, t, nq)[0]
    kj = lambda r, t: _fold_rows(r, t, nq)[1]
    return pl.pallas_call(
        body,
        name="attn_fwd",
        grid=_fold_grid(nq),
        in_specs=[
            pl.BlockSpec((tq, PAIRS * 128), lambda g, r, t: (qi(r, t), g)),
            pl.BlockSpec((tk, PAIRS * 256), lambda g, r, t: (kj(r, t), g)),
            pl.BlockSpec((PAIRS * 256, tk), lambda g, r, t: (g, kj(r, t))),
            pl.BlockSpec((8, tq), lambda g, r, t: (0, qi(r, t))),
        ],
        out_specs=[pl.BlockSpec((tq, PAIRS * 128), lambda g, r, t: (qi(r, t), g))] * 2,
        out_shape=[jax.ShapeDtypeStruct((T, AW), F32)] * 2,
        scratch_shapes=[pltpu.VMEM((PAIRS, 2 * tq, 256), BF16), pltpu.VMEM((2 * PAIRS, 1, tq), F32),
                        pltpu.VMEM((2 * PAIRS, 128, tq), F32)],
        compiler_params=_params(3),
    )(qkv, kk, vt, ct)


def _sgu_forward(us_ref, lng, lnb, w_ref, bt_ref, mixed_s, vnb_s, tm):
    is_a = lax.broadcasted_iota(jnp.int32, (1, 128), 1) < HD
    u = us_ref[:, 0:SW]
    vs = us_ref[:, SW:NUS]
    ug, tu = _gelu(u)
    vg, tv = _gelu(vs)
    mu = jnp.mean(vg, axis=-1, keepdims=True)
    xc = vg - mu
    rstd = lax.rsqrt(jnp.mean(xc * xc, axis=-1, keepdims=True) + EPS)
    vhat = xc * rstd
    vnb_s[...] = (vhat * lng + lnb).astype(BF16)
    rr = lax.broadcasted_iota(jnp.int32, (CH, CH), 0)
    cc = lax.broadcasted_iota(jnp.int32, (CH, CH), 1)
    tril = cc <= rr
    for jj in range(4):
        wa = jnp.where(tril, w_ref[2 * jj], 0.0).astype(BF16)
        wb = jnp.where(tril, w_ref[2 * jj + 1], 0.0).astype(BF16)
        ba = bt_ref[:, 2 * jj:2 * jj + 1]
        bb = bt_ref[:, 2 * jj + 1:2 * jj + 2]
        for ch in range(tm // CH):
            rs, cs = slice(ch * CH, (ch + 1) * CH), slice(jj * 128, (jj + 1) * 128)
            vn2 = vnb_s[rs, cs]
            mixed_s[rs, cs] = jnp.where(is_a, _nn(wa, vn2) + ba, _nn(wb, vn2) + bb)
    mixed = mixed_s[...]
    return u, vs, ug, tu, tv, vhat, rstd, mixed, ug * mixed


def _sgu_out(us, yatt, x, lng, lnb, sgw, sgbt, gatt, gsg, wout, gpm):
    T = us.shape[0]
    tm = _tile(T, 512)

    def body(us_ref, ya_ref, x_ref, lng_ref, lnb_ref, w_ref, bt_ref, ga_ref, gs_ref, wo_ref, gp_ref,
             h1_ref, yb_ref, o_ref, mixed_s, vnb_s):
        ysg = _sgu_forward(us_ref, lng_ref[...], lnb_ref[...], w_ref, bt_ref, mixed_s, vnb_s, tm)[-1]
        yb_ref[:, 0:AW] = _rms_fwd(ya_ref[...], ga_ref[...]).astype(BF16)
        yb_ref[:, AW:D] = _rms_fwd(ysg, gs_ref[...]).astype(BF16)
        o = _nn(yb_ref[...], wo_ref[...])
        o_ref[...] = o
        h1_ref[...] = x_ref[...] + _rms_fwd(o, gp_ref[...])

    row = lambda n: pl.BlockSpec((tm, n), lambda i: (i, 0))
    return pl.pallas_call(
        body,
        name="sgu_out",
        grid=(T // tm,),
        in_specs=[row(NUS), row(AW), row(D), _full((1, SW)), _full((1, SW)), _full((8, CH, CH)), _full((CH, 8)),
                  _full((1, AW)), _full((1, SW)), _full((D, D)), _full((1, D))],
        out_specs=[row(D), row(D), row(D)],
        out_shape=[jax.ShapeDtypeStruct((T, D), F32), jax.ShapeDtypeStruct((T, D), BF16), jax.ShapeDtypeStruct((T, D), F32)],
        scratch_shapes=[pltpu.VMEM((tm, SW), F32), pltpu.VMEM((tm, SW), BF16)],
        compiler_params=_params(1),
    )(us, yatt, x, lng, lnb, sgw, sgbt, gatt, gsg, wout, gpm)


def _ffn_fwd(h1, gpre, w1g, w2g, gpost):
    T = h1.shape[0]
    tm = _tile(T, 512)
    nb, hb = w1g.shape[0], w1g.shape[2]

    def body(h1_ref, gpre_ref, w1_ref, w2_ref, gpost_ref, c2_ref, f1_ref, ff_ref, h2_ref):
        h1 = h1_ref[...]
        c2 = _rms_fwd(h1, gpre_ref[...]).astype(BF16)
        c2_ref[...] = c2
        ff = jnp.zeros((tm, D), F32)
        for j in range(nb):
            f1 = _nn(c2, w1_ref[j])
            f1_ref[:, j * hb:(j + 1) * hb] = f1.astype(BF16)
            r = jnp.maximum(f1, 0.0)
            ff = ff + _nn((r * r).astype(BF16), w2_ref[j])
        ff_ref[...] = ff
        h2_ref[...] = h1 + _rms_fwd(ff, gpost_ref[...])

    row = lambda n: pl.BlockSpec((tm, n), lambda i: (i, 0))
    once = lambda shape: pl.BlockSpec(shape, lambda i: (0,) * len(shape), pipeline_mode=pl.Buffered(1))
    return pl.pallas_call(
        body,
        name="ffn_fwd",
        grid=(T // tm,),
        in_specs=[row(D), _full((1, D)), once((nb, D, hb)), once((nb, hb, D)), _full((1, D))],
        out_specs=[row(D), row(DFF), row(D), row(D)],
        out_shape=[jax.ShapeDtypeStruct((T, D), BF16), jax.ShapeDtypeStruct((T, DFF), BF16),
                   jax.ShapeDtypeStruct((T, D), F32), jax.ShapeDtypeStruct((T, D), F32)],
        compiler_params=_params(1, VMEM_LIMIT_BIG),
    )(h1, gpre, w1g, w2g, gpost)


def _ple_loss(h2, p, tgt, wg, bg, wpe):
    T = h2.shape[0]
    tm = _tile(T, 512)

    def body(h2_ref, p_ref, t_ref, wg_ref, bg_ref, wpe_ref, dh2_ref, dbg_ref, loss_ref, dwg_ref, dwpe_ref):
        @pl.when(pl.program_id(0) == 0)
        def _():
            for r in (dbg_ref, loss_ref, dwg_ref, dwpe_ref):
                r[...] = jnp.zeros_like(r)

        h2 = h2_ref[...]
        h2b = h2.astype(BF16)
        gate = jax.nn.sigmoid(_nn(h2b, wg_ref[...]) + bg_ref[...])
        pb = p_ref[...].astype(BF16)
        pe = _nn(pb, wpe_ref[...])
        diff = (h2 + gate * pe) - t_ref[...]
        loss_ref[...] += jnp.sum(diff * diff)
        dh3 = diff * (1.0 / D)
        dpre = (dh3 * pe) * (gate * (1.0 - gate))
        dpre_b = dpre.astype(BF16)
        dbg_ref[...] += jnp.sum(dpre, axis=0, keepdims=True)
        dh2_ref[...] = dh3 + _nt(dpre_b, wg_ref[...])
        dwg_ref[...] += _nn(h2b.T, dpre_b)
        dwpe_ref[...] += _nn(pb.T, (dh3 * gate).astype(BF16))

    row = lambda n: pl.BlockSpec((tm, n), lambda i: (i, 0))
    once = lambda shape: pl.BlockSpec(shape, lambda i: (0,) * len(shape), pipeline_mode=pl.Buffered(1))
    return pl.pallas_call(
        body,
        name="ple_loss",
        grid=(T // tm,),
        in_specs=[row(D), row(PLE), row(D), _full((D, D)), _full((1, D)), _full((PLE, D))],
        out_specs=[row(D), _full((1, D)), _full((8, 128)), once((D, D)), once((PLE, D))],
        out_shape=[jax.ShapeDtypeStruct((T, D), F32), jax.ShapeDtypeStruct((1, D), F32),
                   jax.ShapeDtypeStruct((8, 128), F32), jax.ShapeDtypeStruct((D, D), F32),
                   jax.ShapeDtypeStruct((PLE, D), F32)],
        compiler_params=_params(1),
    )(h2, p, tgt, wg, bg, wpe)


def _ffn_bwd(dh2, ff, h1, f1, w1g, w2g, gpost, gpre):
    T = dh2.shape[0]
    tm = _tile(T, 512)
    nb, hb = w1g.shape[0], w1g.shape[2]

    def body(dh2_ref, ff_ref, h1_ref, f1_ref, w1_ref, w2_ref, gpost_ref, gpre_ref,
             dffb_ref, df1_ref, dh1_ref, dgpost_ref, dgpre_ref):
        @pl.when(pl.program_id(0) == 0)
        def _():
            dgpost_ref[...] = jnp.zeros_like(dgpost_ref)
            dgpre_ref[...] = jnp.zeros_like(dgpre_ref)

        dh2 = dh2_ref[...]
        dff, dg = _rms_bwd(dh2, ff_ref[...], gpost_ref[...])
        dffb = dff.astype(BF16)
        dffb_ref[...] = dffb
        dgpost_ref[...] += dg
        dc2 = jnp.zeros((tm, D), F32)
        for j in range(nb):
            cols = slice(j * hb, (j + 1) * hb)
            dact = _nt(dffb, w2_ref[j])
            df1 = (dact * (2.0 * jnp.maximum(f1_ref[:, cols].astype(F32), 0.0))).astype(BF16)
            df1_ref[:, cols] = df1
            dc2 = dc2 + _nt(df1, w1_ref[j])
        dx, dg = _rms_bwd(dc2, h1_ref[...], gpre_ref[...])
        dh1_ref[...] = dh2 + dx
        dgpre_ref[...] += dg

    row = lambda n: pl.BlockSpec((tm, n), lambda i: (i, 0))
    once = lambda shape: pl.BlockSpec(shape, lambda i: (0,) * len(shape), pipeline_mode=pl.Buffered(1))
    return pl.pallas_call(
        body,
        name="ffn_bwd",
        grid=(T // tm,),
        in_specs=[row(D), row(D), row(D), row(DFF), once((nb, D, hb)), once((nb, hb, D)), _full((1, D)), _full((1, D))],
        out_specs=[row(D), row(DFF), row(D), _full((1, D)), _full((1, D))],
        out_shape=[jax.ShapeDtypeStruct((T, D), BF16), jax.ShapeDtypeStruct((T, DFF), BF16),
                   jax.ShapeDtypeStruct((T, D), F32), jax.ShapeDtypeStruct((1, D), F32),
                   jax.ShapeDtypeStruct((1, D), F32)],
        compiler_params=_params(1, VMEM_LIMIT_BIG),
    )(dh2, ff, h1, f1, w1g, w2g, gpost, gpre)


def _mix_bwd(dh1, o, us, yatt, qkv, lsec, yb, lng, lnb, sgw, sgwt, sgbt, gatt, gsg, wout, gpm):
    T = dh1.shape[0]
    tm = _tile(T, 512)

    def body(dh1_ref, o_ref, us_ref, ya_ref, q_ref, l_ref, yb_ref, lng_ref, lnb_ref, w_ref, wt_ref, bt_ref, ga_ref, gs_ref,
             wo_ref, gp_ref, qw_ref, dow_ref, dus_ref, dwo_ref, dw_ref, dbt_ref, dlng_ref, dlnb_ref, dga_ref, dgs_ref,
             dgp_ref, mixed_s, vnb_s, dvn_s):
        @pl.when(pl.program_id(0) == 0)
        def _():
            for r in (dwo_ref, dw_ref, dbt_ref, dlng_ref, dlnb_ref, dga_ref, dgs_ref, dgp_ref):
                r[...] = jnp.zeros_like(r)

        is_a = lax.broadcasted_iota(jnp.int32, (1, 128), 1) < HD
        lane = lax.broadcasted_iota(jnp.int32, (1, 128), 1)
        do, dg = _rms_bwd(dh1_ref[...], o_ref[...], gp_ref[...])
        dgp_ref[...] += dg
        dob = do.astype(BF16)
        dwo_ref[...] += _nn(yb_ref[...].T, dob)
        dy = _nt(dob, wo_ref[...])
        ya = ya_ref[...]
        datt, dg = _rms_bwd(dy[:, 0:AW], ya, ga_ref[...])
        dga_ref[...] += dg
        _attn_operands(q_ref[...], datt, ya, l_ref[...], qw_ref, dow_ref)

        lng = lng_ref[...]
        u, vs, ug, tu, tv, vhat, rstd, mixed, ysg = _sgu_forward(us_ref, lng, lnb_ref[...], w_ref, bt_ref, mixed_s, vnb_s, tm)
        dysg, dg = _rms_bwd(dy[:, AW:D], ysg, gs_ref[...])
        dgs_ref[...] += dg
        dus_ref[:, 0:SW] = ((dysg * mixed) * _gelu_grad(u, tu)).astype(BF16)
        dmix = dysg * ug

        rr = lax.broadcasted_iota(jnp.int32, (CH, CH), 0)
        cc = lax.broadcasted_iota(jnp.int32, (CH, CH), 1)
        tril = cc <= rr
        triu = cc >= rr
        for jj in range(4):
            wta = jnp.where(triu, wt_ref[2 * jj], 0.0).astype(BF16)
            wtb = jnp.where(triu, wt_ref[2 * jj + 1], 0.0).astype(BF16)
            for ch in range(tm // CH):
                rs, cs = slice(ch * CH, (ch + 1) * CH), slice(jj * 128, (jj + 1) * 128)
                dm2 = dmix[rs, cs]
                dma = jnp.where(is_a, dm2, 0.0)
                dmb = jnp.where(is_a, 0.0, dm2)
                dma_b, dmb_b = dma.astype(BF16), dmb.astype(BF16)
                vn2 = vnb_s[rs, cs]
                dw_ref[2 * jj] += jnp.where(tril, _nt(dma_b, vn2), 0.0)
                dw_ref[2 * jj + 1] += jnp.where(tril, _nt(dmb_b, vn2), 0.0)
                dvn_s[rs, cs] = _nn(wta, dma_b) + _nn(wtb, dmb_b)
                dba = jnp.sum(dma, axis=1, keepdims=True)
                dbb = jnp.sum(dmb, axis=1, keepdims=True)
                dbt_ref[...] += jnp.where(lane == 2 * jj, dba, 0.0) + jnp.where(lane == 2 * jj + 1, dbb, 0.0)

        dvn = dvn_s[...]
        dlng_ref[...] += jnp.sum(dvn * vhat, axis=0, keepdims=True)
        dlnb_ref[...] += jnp.sum(dvn, axis=0, keepdims=True)
        dvh = dvn * lng
        dvg = rstd * (dvh - jnp.mean(dvh, axis=-1, keepdims=True) - vhat * jnp.mean(dvh * vhat, axis=-1, keepdims=True))
        dus_ref[:, SW:NUS] = (dvg * _gelu_grad(vs, tv)).astype(BF16)

    row = lambda n: pl.BlockSpec((tm, n), lambda i: (i, 0))
    return pl.pallas_call(
        body,
        name="mix_bwd",
        grid=(T // tm,),
        in_specs=[row(D), row(D), row(NUS), row(AW), row(AW), row(AW), row(D), _full((1, SW)), _full((1, SW)),
                  _full((8, CH, CH)), _full((8, CH, CH)), _full((CH, 8)), _full((1, AW)), _full((1, SW)), _full((D, D)),
                  _full((1, D))],
        out_specs=[row(4 * AW), row(4 * AW), row(NUS),
                   pl.BlockSpec((D, D), lambda i: (0, 0), pipeline_mode=pl.Buffered(1)), _full((8, CH, CH)),
                   _full((CH, 128)), _full((1, SW)), _full((1, SW)), _full((1, AW)), _full((1, SW)), _full((1, D))],
        out_shape=[jax.ShapeDtypeStruct((T, 4 * AW), BF16),
                   jax.ShapeDtypeStruct((T, 4 * AW), BF16), jax.ShapeDtypeStruct((T, NUS), BF16),
                   jax.ShapeDtypeStruct((D, D), F32),
                   jax.ShapeDtypeStruct((8, CH, CH), F32), jax.ShapeDtypeStruct((CH, 128), F32),
                   jax.ShapeDtypeStruct((1, SW), F32), jax.ShapeDtypeStruct((1, SW), F32),
                   jax.ShapeDtypeStruct((1, AW), F32), jax.ShapeDtypeStruct((1, SW), F32), jax.ShapeDtypeStruct((1, D), F32)],
        scratch_shapes=[pltpu.VMEM((tm, SW), F32), pltpu.VMEM((tm, SW), BF16), pltpu.VMEM((tm, SW), F32)],
        compiler_params=_params(1, VMEM_LIMIT_BIG),
    )(dh1, o, us, yatt, qkv, lsec, yb, lng, lnb, sgw, sgwt, sgbt, gatt, gsg, wout, gpm)


def _attn_operands(q, do, o, lsec, qw_ref, dow_ref):
    rows = do.shape[0]
    feat = lax.broadcasted_iota(jnp.int32, (AW, 128), 0)
    head = lax.broadcasted_iota(jnp.int32, (AW, 128), 1)
    sel = jnp.where((feat >= head * HD) & (feat < (head + 1) * HD), 1.0, 0.0)
    delta = jnp.dot(do * o, sel, precision=lax.Precision.HIGHEST, preferred_element_type=F32)
    for hp in range(4):
        cols = slice(hp * 128, (hp + 1) * 128)
        for hh in range(2):
            base = (2 * hp + hh) * 256
            lc = lsec[:, hp * 128 + hh:hp * 128 + hh + 1]
            d_h = delta[:, 2 * hp + hh:2 * hp + hh + 1]
            qw_ref[:, base:base + 128] = _head_rows(q[:, cols], hh, 0.125)
            qw_ref[:, base + 128:base + 256] = _aug_lanes(rows, AUG_A if hh == 0 else AUG_B, _split3(-lc))
            dow_ref[:, base:base + 128] = _head_rows(do[:, cols], hh, 1.0).astype(BF16)
            dow_ref[:, base + 128:base + 256] = _aug_lanes(rows, None, _split3(-d_h))


def _attn_bwd(kk, vk, kt, qw, dow, after):
    T = kk.shape[0]
    tq = _tile(T, 512)
    tk = tq
    nq = T // tq

    def body(kk_ref, vk_ref, kt_ref, qw_ref, dow_ref, after_ref, dqt_ref, dcq_ref, dk_ref, dv_ref, dck_ref, dk_s, dv_s, dck_s):
        j, i = _fold_cols(pl.program_id(1), pl.program_id(2), nq)
        sub8 = lax.broadcasted_iota(jnp.int32, (8, 1), 0)
        lane = lax.broadcasted_iota(jnp.int32, (1, 128), 1)

        @pl.when((pl.program_id(1) == 0) & (pl.program_id(2) == 0))
        def _():
            dqt_ref[...] = jnp.zeros_like(dqt_ref)
            dcq_ref[...] = jnp.zeros_like(dcq_ref)

        @pl.when(i == j)
        def _():
            dk_s[...] = jnp.zeros_like(dk_s)
            dv_s[...] = jnp.zeros_like(dv_s)
            dck_s[...] = jnp.zeros_like(dck_s)

        def step(masked):
            cols = pl.ds(pl.multiple_of(i * tq, tq), tq)
            sub = lax.broadcasted_iota(jnp.int32, (128, 1), 0)
            if masked:
                causal = lax.broadcasted_iota(jnp.int32, (tk, tq), 0) <= lax.broadcasted_iota(jnp.int32, (tk, tq), 1)

            def logits(n):
                pair, base = n // 2, n * 256
                return (_nt(kk_ref[:, pair * 256:(pair + 1) * 256], qw_ref[:, base:base + 256]),
                        _nt(vk_ref[:, pair * 256:(pair + 1) * 256], dow_ref[:, base:base + 256]))

            ahead = logits(0)
            dcq = jnp.zeros((8, tq), F32)
            dck = jnp.zeros((tk, 128), F32)
            for pp in range(PAIRS):
                lanes = slice(pp * 128, (pp + 1) * 128)
                kt2 = kt_ref[lanes, :] * 0.125
                dv = jnp.zeros((tk, 128), F32)
                dk = jnp.zeros((tk, 128), F32)
                dqts = []
                for hh in range(2):
                    head = 2 * pp + hh
                    base = head * 256
                    qw_h = qw_ref[:, base:base + 256]
                    dow_h = dow_ref[:, base:base + 256]
                    logp, dp = ahead
                    if head + 1 < 2 * PAIRS:
                        ahead = logits(head + 1)
                    pr = jnp.exp(logp)
                    if masked:
                        pr = jnp.where(causal, pr, 0.0)
                    ds = pr * dp
                    ds_b = ds.astype(BF16)
                    dv = dv + _nn(pr.astype(BF16), dow_h[:, 0:128])
                    dk = dk + _nn(ds_b, qw_h[:, 0:128])
                    dqts.append(_nn(kt2, ds_b))
                    dcq = dcq + jnp.where(sub8 == head, jnp.sum(ds, axis=0, keepdims=True), 0.0)
                    dck = dck - jnp.where(lane == head, jnp.sum(ds, axis=1, keepdims=True), 0.0)
                dv_s[:, lanes] += dv
                dk_s[:, lanes] += dk
                dqt_ref[lanes, cols] += jnp.where(sub < HD, dqts[0], dqts[1])
            dck_s[...] += dck
            dcq_ref[:, cols] += dcq

        @pl.when(i > j)
        def _():
            step(False)

        @pl.when(i == j)
        def _():
            step(True)

        @pl.when(i == nq - 1)
        def _():
            dk_ref[...] = dk_s[...].astype(BF16)
            dv_ref[...] = dv_s[...].astype(BF16)
            dck_ref[...] = dck_s[...]

    kj = lambda r, t: _fold_cols(r, t, nq)[0]
    qi = lambda r, t: _fold_cols(r, t, nq)[1]
    krow = lambda g, r, t: (kj(r, t), g)
    qrow = lambda g, r, t: (qi(r, t), g)
    return pl.pallas_call(
        body,
        name="attn_bwd",
        grid=_fold_grid(nq),
        in_specs=[
            pl.BlockSpec((tk, PAIRS * 256), krow),
            pl.BlockSpec((tk, PAIRS * 256), krow),
            pl.BlockSpec((PAIRS * 128, tk), lambda g, r, t: (g, kj(r, t))),
            pl.BlockSpec((tq, PAIRS * 512), qrow),
            pl.BlockSpec((tq, PAIRS * 512), qrow),
            pl.BlockSpec(after.shape, lambda g, r, t: (0,) * after.ndim),
        ],
        out_specs=[
            pl.BlockSpec((PAIRS * 128, T), lambda g, r, t: (g, 0), pipeline_mode=pl.Buffered(1)),
            pl.BlockSpec((8, T), lambda g, r, t: (0, 0), pipeline_mode=pl.Buffered(1)),
            pl.BlockSpec((tk, PAIRS * 128), krow),
            pl.BlockSpec((tk, PAIRS * 128), krow),
            pl.BlockSpec((tk, 128), lambda g, r, t: (kj(r, t), 0)),
        ],
        out_shape=[jax.ShapeDtypeStruct((AW, T), F32), jax.ShapeDtypeStruct((8, T), F32),
                   jax.ShapeDtypeStruct((T, AW), BF16), jax.ShapeDtypeStruct((T, AW), BF16),
                   jax.ShapeDtypeStruct((T, FPAD), F32)],
        scratch_shapes=[pltpu.VMEM((tk, PAIRS * 128), F32), pltpu.VMEM((tk, PAIRS * 128), F32),
                        pltpu.VMEM((tk, 128), F32)],
        compiler_params=_params(3),
    )(kk, vk, kt, qw, dow, after)


def _fgate_bwd(dcq, dck, fz, fb):
    T = dck.shape[0]
    tb = _tile(T, 512)
    nb = T // tb

    def body(dcq_ref, dck_ref, fz_ref, fb_ref, df_ref, dfb_ref, carry):
        @pl.when(pl.program_id(0) == 0)
        def _():
            carry[...] = jnp.zeros_like(carry)
            dfb_ref[...] = jnp.zeros_like(dfb_ref)

        head = lax.broadcasted_iota(jnp.int32, (8, FPAD), 0)
        eye = jnp.where(head == lax.broadcasted_iota(jnp.int32, (8, FPAD), 1), 1.0, 0.0)
        dcv = dck_ref[...] + lax.dot_general(dcq_ref[...], eye, (((0,), (0,)), ((), ())),
                                             precision=lax.Precision.HIGHEST, preferred_element_type=F32)
        r = lax.broadcasted_iota(jnp.int32, (128, 128), 0)
        cc = lax.broadcasted_iota(jnp.int32, (128, 128), 1)
        tri = (cc >= r).astype(F32)
        run = carry[...]
        parts = []
        for blk in reversed(range(tb // 128)):
            dc_b = dcv[blk * 128:(blk + 1) * 128, :]
            parts.append(jnp.dot(tri, dc_b, precision=lax.Precision.HIGHEST, preferred_element_type=F32) + run)
            run = run + jnp.sum(dc_b, axis=0, keepdims=True)
        carry[...] = run
        dlf = jnp.concatenate(parts[::-1], axis=0)
        lane = lax.broadcasted_iota(jnp.int32, (tb, FPAD), 1)
        df = jnp.where(lane < 8, dlf * jax.nn.sigmoid(-(fz_ref[...] + fb_ref[...])), 0.0)
        df_ref[...] = df.astype(BF16)
        dfb_ref[...] += jnp.sum(df, axis=0, keepdims=True)

    rev = pl.BlockSpec((tb, FPAD), lambda i: (nb - 1 - i, 0))
    return pl.pallas_call(
        body,
        name="fgate_bwd",
        grid=(nb,),
        in_specs=[pl.BlockSpec((8, tb), lambda i: (0, nb - 1 - i)), rev, rev, _full((1, FPAD))],
        out_specs=[rev, _full((1, FPAD))],
        out_shape=[jax.ShapeDtypeStruct((T, FPAD), BF16), jax.ShapeDtypeStruct((1, FPAD), F32)],
        scratch_shapes=[pltpu.VMEM((1, FPAD), F32)],
        compiler_params=_params(1),
    )(dcq, dck, fz, fb)


def _inproj_bwd(dqt, dk, dv, dus, dfz, wtp, x, dh1, g):
    T = x.shape[0]
    tm = _tile(T, 512)

    def body(dq_ref, dk_ref, dv_ref, dus_ref, dfz_ref, w_ref, x_ref, dh1_ref, g_ref, gx_ref, dg_ref):
        @pl.when(pl.program_id(0) == 0)
        def _():
            dg_ref[...] = jnp.zeros_like(dg_ref)

        da = _tn(dq_ref[...].astype(BF16), w_ref[0:AW, :])
        da += _nn(dk_ref[...], w_ref[AW:2 * AW, :])
        da += _nn(dv_ref[...], w_ref[2 * AW:NQKV, :])
        da += _nn(dus_ref[...], w_ref[NQKV:NQKV + NUS, :])
        da += _nn(dfz_ref[...], w_ref[NQKV + NUS:ZP, :])
        dx, dg = _rms_bwd(da, x_ref[...], g_ref[...])
        gx_ref[...] = dh1_ref[...] + dx
        dg_ref[...] += dg

    row = lambda n: pl.BlockSpec((tm, n), lambda i: (i, 0))
    return pl.pallas_call(
        body,
        name="inproj_bwd",
        grid=(T // tm,),
        in_specs=[pl.BlockSpec((AW, tm), lambda i: (0, i)), row(AW), row(AW), row(NUS), row(FPAD), _full((ZP, D)),
                  row(D), row(D), _full((1, D))],
        out_specs=[row(D), _full((1, D))],
        out_shape=[jax.ShapeDtypeStruct((T, D), F32), jax.ShapeDtypeStruct((1, D), F32)],
        compiler_params=_params(1),
    )(dqt, dk, dv, dus, dfz, wtp, x, dh1, g)


def _sq_relu(f1):
    r = jnp.maximum(f1.astype(F32), 0.0)
    return (r * r).astype(BF16)


def _wgrad(a, bs, name, a_fn=None, out_dtype=F32, transposed=False):
    T, K = a.shape
    tt = _tile(T, 1024)
    nb = len(bs)
    narrow = out_dtype != F32

    def out_dims(b, layout):
        if transposed:
            return (b.shape[0] if layout == "t" else b.shape[1], K)
        if layout == "t":
            return (K, b.shape[0])
        N = b.shape[1]
        if layout is None:
            return (K, N)
        return (N // layout[1], K, layout[1]) if layout[0] == "col" else (K // layout[1], layout[1], N)

    shapes = [out_dims(b, layout) for b, layout in bs]

    def body(*refs):
        a_ref, b_refs, o_refs = refs[0], refs[1:1 + nb], refs[1 + nb:1 + 2 * nb]
        accs = refs[1 + 2 * nb:] if narrow else o_refs

        @pl.when(pl.program_id(0) == 0)
        def _():
            for acc in accs:
                acc[...] = jnp.zeros_like(acc)

        av = (a_ref[...] if a_fn is None else a_fn(a_ref[...])).astype(BF16)
        at = None if transposed else av.T
        for (b, layout), b_ref, o_ref in zip(bs, b_refs, accs):
            if transposed:
                bv = b_ref[...].astype(BF16)
                o_ref[...] += _nn(bv, av) if layout == "t" else _tn(bv, av)
            elif layout is None:
                o_ref[...] += _nn(at, b_ref[...].astype(BF16))
            elif layout == "t":
                o_ref[...] += _nt(at, b_ref[...].astype(BF16))
            elif layout[0] == "col":
                n = layout[1]
                for k in range(b.shape[1] // n):
                    o_ref[k] += _nn(at, b_ref[:, k * n:(k + 1) * n].astype(BF16))
            else:
                n = layout[1]
                bv = b_ref[...].astype(BF16)
                for k in range(K // n):
                    o_ref[k] += _nn(at[k * n:(k + 1) * n, :], bv)

        if narrow:
            @pl.when(pl.program_id(0) == T // tt - 1)
            def _():
                for o_ref, acc in zip(o_refs, accs):
                    o_ref[...] = acc[...].astype(out_dtype)

    once = lambda shape: pl.BlockSpec(shape, lambda t: (0,) * len(shape), pipeline_mode=pl.Buffered(1))
    res = pl.pallas_call(
        body,
        name=name,
        grid=(T // tt,),
        in_specs=[pl.BlockSpec((tt, K), lambda t: (t, 0))] + [
            pl.BlockSpec((b.shape[0], tt), lambda t: (0, t)) if layout == "t" else pl.BlockSpec((tt, b.shape[1]), lambda t: (t, 0))
            for b, layout in bs],
        out_specs=[once(s) for s in shapes],
        out_shape=[jax.ShapeDtypeStruct(s, out_dtype) for s in shapes],
        scratch_shapes=[pltpu.VMEM(s, F32) for s in shapes] if narrow else [],
        compiler_params=_params(1, VMEM_LIMIT_BIG),
    )(a, *[b for b, _ in bs])
    return res


def _adam_math(w, g, m, v):
    m = ADAM_B1 * m + (1.0 - ADAM_B1) * g
    v = ADAM_B2 * v + (1.0 - ADAM_B2) * (g * g)
    m_hat = m / (1.0 - ADAM_B1 ** ADAM_STEP)
    v_hat = v / (1.0 - ADAM_B2 ** ADAM_STEP)
    delta = -ADAM_LR * (m_hat / (jnp.sqrt(v_hat) + ADAM_EPS) + ADAM_WD * w)
    return delta, m, v


def _adam(parts, w, m, v, name):
    R, C = w.shape
    br = 128 if R % 128 == 0 else R

    def body(p_ref, w_ref, m_ref, v_ref, g_ref, d_ref, nm_ref, nv_ref):
        g = p_ref[0].astype(F32)
        for s in range(1, NDEV):
            g = g + p_ref[s].astype(F32)
        g_ref[...] = g
        d_ref[...], nm_ref[...], nv_ref[...] = _adam_math(w_ref[...], g, m_ref[...], v_ref[...])

    blk = pl.BlockSpec((br, C), lambda i: (i, 0))
    return pl.pallas_call(
        body,
        name=name,
        grid=(R // br,),
        in_specs=[pl.BlockSpec((NDEV, br, C), lambda i: (0, i, 0)), blk, blk, blk],
        out_specs=[blk] * 4,
        out_shape=[jax.ShapeDtypeStruct((R, C), F32)] * 4,
        compiler_params=_params(1),
    )(parts, w, m, v)


_SMALL = (("sg_w", 8 * CH * CH), ("f_bias", 8), ("sg_ln_g", SW), ("sg_ln_b", SW), ("sg_b", 8 * CH), ("att_out_g", AW),
          ("sg_out_g", SW), ("pre_mix_g", D), ("post_mix_g", D), ("pre_ffn_g", D), ("post_ffn_g", D), ("ple_gate_b", D))
_SEG = 8 * 128


def _seg_rows(size):
    return 8 * (-(-size // _SEG))


def _pack(vals, loss_acc):
    parts = []
    for name, size in _SMALL:
        flat = vals[name].reshape(-1)
        rows = _seg_rows(size)
        parts.append(jnp.pad(flat, (0, rows * 128 - size)).reshape(rows, 128))
    parts.append(loss_acc)
    return jnp.concatenate(parts, axis=0)


def _adam_small(parts, ws, ms, vs):
    n = len(_SMALL)
    names = [name for name, _ in _SMALL]

    def body(*refs):
        p_ref, w_refs, m_refs, v_refs = refs[0], refs[1:1 + n], refs[1 + n:1 + 2 * n], refs[1 + 2 * n:1 + 3 * n]
        loss_ref, outs = refs[1 + 3 * n], refs[2 + 3 * n:]
        g_all = p_ref[0]
        for s in range(1, NDEV):
            g_all = g_all + p_ref[s]
        r = 0
        for k, (name, size) in enumerate(_SMALL):
            dst = [outs[kind * n + k] for kind in range(4)]

            def update(g, idx):
                vals = (g,) + _adam_math(w_refs[k][idx], g, m_refs[k][idx], v_refs[k][idx])
                for d, val in zip(dst, vals):
                    d[idx] = val

            if name == "sg_w":
                for grp in range(8):
                    update(g_all[r + grp * CH:r + (grp + 1) * CH, :], (0, grp))
            elif name == "sg_b":
                update(g_all[r:r + 8, :], (0,))
            elif name == "f_bias":
                update(g_all[r:r + 1, 0:8], (slice(None),))
            else:
                update(jnp.concatenate([g_all[r + q:r + q + 1, :] for q in range(size // 128)], axis=1), (slice(None),))
            r += _seg_rows(size)
        loss_ref[...] = g_all[r:r + 1, 0:1] * (0.5 / D)

    arrs = [parts] + [d[name] for d in (ws, ms, vs) for name in names]
    res = pl.pallas_call(
        body,
        name="adam_small",
        in_specs=[_full(a.shape) for a in arrs],
        out_specs=[_full((1, 1))] + [_full(ws[name].shape) for _ in range(4) for name in names],
        out_shape=[jax.ShapeDtypeStruct((1, 1), F32)] + [jax.ShapeDtypeStruct(ws[name].shape, F32) for _ in range(4) for name in names],
        compiler_params=pltpu.CompilerParams(vmem_limit_bytes=VMEM_LIMIT),
    )(*arrs)
    return res[0], {name: [res[1 + kind * n + k] for kind in range(4)] for k, name in enumerate(names)}


def kernel(x, p, w_in, f_bias, sg_ln_g, sg_ln_b, sg_w, sg_b, att_out_g, sg_out_g, w_out, pre_mix_g, post_mix_g, pre_ffn_g, post_ffn_g, w_ff1, w_ff2, ple_w, ple_gate_w, ple_gate_b, loss_target, m_w_in, m_f_bias, m_sg_ln_g, m_sg_ln_b, m_sg_w, m_sg_b, m_att_out_g, m_sg_out_g, m_w_out, m_pre_mix_g, m_post_mix_g, m_pre_ffn_g, m_post_ffn_g, m_w_ff1, m_w_ff2, m_ple_w, m_ple_gate_w, m_ple_gate_b, v_w_in, v_f_bias, v_sg_ln_g, v_sg_ln_b, v_sg_w, v_sg_b, v_att_out_g, v_sg_out_g, v_w_out, v_pre_mix_g, v_post_mix_g, v_pre_ffn_g, v_post_ffn_g, v_w_ff1, v_w_ff2, v_ple_w, v_ple_gate_w, v_ple_gate_b):
    small_w = dict(sg_w=sg_w, f_bias=f_bias, sg_ln_g=sg_ln_g, sg_ln_b=sg_ln_b, sg_b=sg_b, att_out_g=att_out_g,
                   sg_out_g=sg_out_g, pre_mix_g=pre_mix_g, post_mix_g=post_mix_g, pre_ffn_g=pre_ffn_g,
                   post_ffn_g=post_ffn_g, ple_gate_b=ple_gate_b)
    small_m = dict(sg_w=m_sg_w, f_bias=m_f_bias, sg_ln_g=m_sg_ln_g, sg_ln_b=m_sg_ln_b, sg_b=m_sg_b, att_out_g=m_att_out_g,
                   sg_out_g=m_sg_out_g, pre_mix_g=m_pre_mix_g, post_mix_g=m_post_mix_g, pre_ffn_g=m_pre_ffn_g,
                   post_ffn_g=m_post_ffn_g, ple_gate_b=m_ple_gate_b)
    small_v = dict(sg_w=v_sg_w, f_bias=v_f_bias, sg_ln_g=v_sg_ln_g, sg_ln_b=v_sg_ln_b, sg_b=v_sg_b, att_out_g=v_att_out_g,
                   sg_out_g=v_sg_out_g, pre_mix_g=v_pre_mix_g, post_mix_g=v_post_mix_g, pre_ffn_g=v_pre_ffn_g,
                   post_ffn_g=v_post_ffn_g, ple_gate_b=v_ple_gate_b)
    big = dict(w_in=(w_in, m_w_in, v_w_in), w_out=(w_out, m_w_out, v_w_out), w_ff1=(w_ff1, m_w_ff1, v_w_ff1),
               w_ff2=(w_ff2, m_w_ff2, v_w_ff2), ple_w=(ple_w, m_ple_w, v_ple_w),
               ple_gate_w=(ple_gate_w, m_ple_gate_w, v_ple_gate_w))

    xt, pt, tgt = x[0], p[0, 0], loss_target[0]
    ws = W_IN_COLS // NDEV

    gw_in = _gather(jnp.pad(jnp.transpose(w_in[0]).astype(BF16), ((0, WS_PAD - ws), (0, 0))), "gather_w_in")
    gw_in, others = lax.optimization_barrier((gw_in, [w_out[0].astype(BF16), w_ff1[0].astype(BF16), w_ff2[0].astype(BF16),
                                                      ple_w[0].astype(BF16), ple_gate_w[0].astype(BF16)]))
    rest = _xchg_start(others, True, "gather_rest_start")
    wint = gw_in[:, 0:ws, :].reshape(W_IN_COLS, D)
    wtp = jnp.concatenate([wint[0:NQKV], wint[NQKV + 8:W_IN_COLS], wint[NQKV:NQKV + 8],
                           jnp.zeros((FPAD - 8, D), BF16)], axis=0)

    fb = jnp.pad(f_bias.astype(F32), ((0, 0), (0, FPAD - 8)))
    sgw = sg_w[0]
    sgwt = jnp.transpose(sg_w[0], (0, 2, 1))
    sgbt = jnp.transpose(sg_b[0])

    qkv, kt, vt, us, fz, ab = _inproj(xt, pre_mix_g + rest[4][0, 0], wtp)
    ct, kk, vk = _fcum(fz, fb, qkv)
    yatt, lsec = _attn_fwd(qkv, kk, vt, ct)
    gw_out, gw1, gw2, gwpe, gwg = _xchg_wait(rest, True, yatt, "gather_rest_wait")
    wout = gw_out.reshape(D, D)
    wg = gwg.reshape(D, D)
    wpe = jnp.transpose(gwpe, (1, 0, 2)).reshape(PLE, D)
    h1, yb, o = _sgu_out(us, yatt, xt, sg_ln_g, sg_ln_b, sgw, sgbt, att_out_g, sg_out_g, wout, post_mix_g)
    c2b, f1b, ff, h2 = _ffn_fwd(h1, pre_ffn_g, gw1, gw2, post_ffn_g)
    dh2, dbg, loss_acc, g_g, g_pe = _ple_loss(h2, pt, tgt, wg, ple_gate_b, wpe)
    g_g = g_g.reshape(NDEV, D // NDEV, D)
    g_pe = jnp.transpose(g_pe.reshape(PLE, NDEV, D // NDEV), (1, 0, 2))

    dffb, df1, dh1, dgpostffn, dgpreffn = _ffn_bwd(dh2, ff, h1, f1b, gw1, gw2, post_ffn_g, pre_ffn_g)
    (g_1,) = _wgrad(c2b, [(df1, ("col", DFF // NDEV))], "wgrad_ff1")
    (g_2,) = _wgrad(f1b, [(dffb, ("row", DFF // NDEV))], "wgrad_ff2", a_fn=_sq_relu)
    early = _xchg_start([g_1, g_2, g_pe, g_g], False, "scatter_early_start")
    qw, dow, dus, g_out, dsgw, dsgbt, dlng, dlnb, dgatt, dgsg, dgpostmix = _mix_bwd(
        dh1, o, us, yatt, qkv, lsec, yb, sg_ln_g, sg_ln_b, sgw, sgwt, sgbt, att_out_g, sg_out_g, wout,
        post_mix_g + early[4][0, 0])
    mid = _xchg_start([g_out.reshape(NDEV, D // NDEV, D)], False, "scatter_mid_start")
    dqt, dcq, dk, dv, dck = _attn_bwd(kk, vk, kt, qw, dow, mid[4])
    dfz, dfb = _fgate_bwd(dcq, dck, fz, fb)

    gq, gk, gv, gus, gf = _wgrad(ab, [(dqt, "t"), (dk, None), (dv, None), (dus, None), (dfz, None)], "wgrad_in",
                                 out_dtype=BF16, transposed=True)
    g_in = jnp.concatenate([gq, gk, gv, gf[0:8], gus], axis=0).reshape(NDEV, ws, D)
    late = _xchg_start([g_in], False, "scatter_late_start")
    grad_x, dgpremix = _inproj_bwd(dqt, dk, dv, dus, dfz, wtp, xt, dh1, pre_mix_g + late[4][0, 0])

    small_g = dict(sg_w=dsgw, f_bias=dfb[:, 0:8], sg_ln_g=dlng, sg_ln_b=dlnb, sg_b=jnp.transpose(dsgbt[:, 0:8]),
                   att_out_g=dgatt, sg_out_g=dgsg, pre_mix_g=dgpremix, post_mix_g=dgpostmix, pre_ffn_g=dgpreffn,
                   post_ffn_g=dgpostffn, ple_gate_b=dbg)

    r_1, r_2, r_pe, r_g = _xchg_wait(early, False, grad_x, "scatter_early_wait")
    (r_out,) = _xchg_wait(mid, False, grad_x, "scatter_mid_wait")
    r_small = _gather(_pack(small_g, loss_acc), "gather_small_grads")

    res = {}

    def adam_big(name, parts):
        w, m, v = big[name]
        res[name] = [t[None] for t in _adam(parts, w[0], m[0], v[0], "adam_" + name)]

    for name, parts in (("w_out", r_out), ("w_ff1", r_1), ("w_ff2", r_2), ("ple_w", r_pe), ("ple_gate_w", r_g)):
        adam_big(name, parts)
    (r_in,) = _xchg_wait(late, False, res["w_ff1"][0], "scatter_late_wait")
    res["w_in"] = [jnp.transpose(t)[None] for t in _adam(r_in, *[jnp.transpose(t[0]) for t in big["w_in"]], "adam_w_in")]
    loss, small = _adam_small(r_small, small_w, small_m, small_v)
    res.update(small)

    order = ["w_in", "f_bias", "sg_ln_g", "sg_ln_b", "sg_w", "sg_b", "att_out_g", "sg_out_g", "w_out", "pre_mix_g",
             "post_mix_g", "pre_ffn_g", "post_ffn_g", "w_ff1", "w_ff2", "ple_w", "ple_gate_w", "ple_gate_b"]
    outs = [loss[0, 0], grad_x[None]]
    for kind in range(4):
        outs += [res[name][kind] for name in order]
    return tuple(outs)
```

```python
import jax
import jax.numpy as jnp
from jax import lax
from jax.experimental import pallas as pl
from jax.experimental.pallas import tpu as pltpu

F32 = jnp.float32
BF16 = jnp.bfloat16

NDEV = 8
D = 1024
AW = 512
SW = 512
HD = 64
CH = 128
DFF = 4096
PLE = 256
NQKV = 3 * AW
NUS = 2 * SW
FPAD = 128
ZP = NQKV + NUS + FPAD
W_IN_COLS = 2568
WS_PAD = 336
EPS = 1e-6
MASKV = -1e30
GELU_K = 0.7978845608028654
GELU_C = 0.044715

ADAM_LR = 0.001
ADAM_B1 = 0.9
ADAM_B2 = 0.999
ADAM_EPS = 1e-08
ADAM_WD = 0.01
ADAM_STEP = 10

VMEM_LIMIT = 48 * 1024 * 1024
VMEM_LIMIT_BIG = 60 * 1024 * 1024


def _nn(a, b):
    return jnp.dot(a, b, preferred_element_type=F32)


def _nt(a, b):
    return lax.dot_general(a, b, (((1,), (1,)), ((), ())), preferred_element_type=F32)


def _tn(a, b):
    return lax.dot_general(a, b, (((0,), (0,)), ((), ())), preferred_element_type=F32)


def _tile(n, pref):
    return min(n, pref)


def _params(n_axes, vmem=VMEM_LIMIT):
    return pltpu.CompilerParams(dimension_semantics=("arbitrary",) * n_axes, vmem_limit_bytes=vmem)


def _full(shape):
    nd = len(shape)
    return pl.BlockSpec(shape, lambda *_: (0,) * nd)


def _rms_fwd(x, g):
    r = lax.rsqrt(jnp.mean(x * x, axis=-1, keepdims=True) + EPS)
    return x * r * g


def _rms_bwd(dy, x, g):
    n = x.shape[-1]
    r = lax.rsqrt(jnp.mean(x * x, axis=-1, keepdims=True) + EPS)
    u = dy * g
    s = jnp.sum(x * u, axis=-1, keepdims=True)
    dx = r * u - x * (r * r * r * (s * (1.0 / n)))
    dg = jnp.sum(dy * (x * r), axis=0, keepdims=True)
    return dx, dg


def _gelu(x):
    t = jnp.tanh(x * (GELU_K + (GELU_K * GELU_C) * (x * x)))
    return x * (0.5 + 0.5 * t), t


def _gelu_grad(x, t):
    return (0.5 + 0.5 * t) + (0.5 * x) * (1.0 - t * t) * (GELU_K + (3.0 * GELU_K * GELU_C) * (x * x))


def _gather(arr, name):
    def body(x_ref, out_ref, send, recv, loc):
        x, y, c = lax.axis_index("x"), lax.axis_index("y"), lax.axis_index("c")
        me, sibling = (x, y, c), (x, y, 1 - c)
        chips = [(1 - x, y), (x, 1 - y), (1 - x, 1 - y)]

        def slot(px, py, pc):
            return out_ref.at[4 * px + 2 * py + pc]

        def copy(k, block, to, src=None):
            return pltpu.make_async_remote_copy(
                src_ref=slot(*block) if src is None else src, dst_ref=slot(*block), send_sem=send.at[k],
                recv_sem=recv.at[k], device_id=to, device_id_type=pl.DeviceIdType.MESH)

        mine = pltpu.make_async_copy(x_ref, slot(*me), loc)
        mine.start()
        first = [copy(0, me, sibling, src=x_ref)] + [copy(1 + j, me, (*chip, c), src=x_ref) for j, chip in enumerate(chips)]
        for cp in first:
            cp.start()
        passed = [copy(4 + j, (*chip, c), sibling) for j, chip in enumerate(chips)]
        for j, chip in enumerate(chips):
            copy(1 + j, (*chip, c), me).wait_recv()
            passed[j].start()
        copy(0, sibling, me).wait_recv()
        for j, chip in enumerate(chips):
            copy(4 + j, (*chip, 1 - c), me).wait_recv()
        for cp in first + passed:
            cp.wait_send()
        mine.wait()

    hbm = pl.BlockSpec(memory_space=pltpu.HBM)
    return pl.pallas_call(
        body,
        name=name,
        out_shape=jax.ShapeDtypeStruct((NDEV,) + arr.shape, arr.dtype),
        in_specs=[hbm],
        out_specs=hbm,
        scratch_shapes=[pltpu.SemaphoreType.DMA((NDEV - 1,)), pltpu.SemaphoreType.DMA((NDEV - 1,)), pltpu.SemaphoreType.DMA],
    )(arr)


def _peers(x, y, c):
    out = []
    for k in range(1, NDEV):
        out.append((1 - x if (k >> 2) & 1 else x, 1 - y if (k >> 1) & 1 else y, 1 - c if k & 1 else c))
    return out


def _xchg_start(arrs, gather, name):
    n = len(arrs)
    me = 4 * lax.axis_index("x") + 2 * lax.axis_index("y") + lax.axis_index("c")
    lands = []
    for a in arrs:
        shape = ((NDEV,) + a.shape) if gather else a.shape
        own = a[None] if gather else lax.dynamic_slice_in_dim(a, me, 1, axis=0)
        lands.append(lax.dynamic_update_slice_in_dim(lax.empty(shape, a.dtype), own, me, axis=0))

    def body(*refs):
        ins, lnd = refs[:n], refs[n:2 * n]
        send, recv, token = refs[2 * n:3 * n], refs[3 * n:4 * n], refs[-1]
        x, y, c = lax.axis_index("x"), lax.axis_index("y"), lax.axis_index("c")
        mine = 4 * x + 2 * y + c
        for px, py, pc in _peers(x, y, c):
            peer = 4 * px + 2 * py + pc
            for a in range(n):
                pltpu.make_async_remote_copy(
                    src_ref=ins[a] if gather else ins[a].at[peer],
                    dst_ref=lnd[a].at[mine],
                    send_sem=send[a],
                    recv_sem=recv[a],
                    device_id=(px, py, pc),
                    device_id_type=pl.DeviceIdType.MESH,
                ).start()
        token[...] = jnp.zeros_like(token)

    hbm = pl.BlockSpec(memory_space=pltpu.HBM)
    sem = pl.BlockSpec(memory_space=pltpu.SEMAPHORE)
    res = pl.pallas_call(
        body,
        name=name,
        out_shape=(*[pltpu.SemaphoreType.DMA(())] * (2 * n),
                   *[pltpu.HBM(a.shape, a.dtype) for a in arrs], *[pltpu.HBM(l.shape, l.dtype) for l in lands],
                   jax.ShapeDtypeStruct((8, 128), F32)),
        in_specs=[hbm] * (2 * n),
        out_specs=(*([sem] * (2 * n)), *([hbm] * (2 * n)), pl.BlockSpec(memory_space=pltpu.VMEM)),
        input_output_aliases={i: 2 * n + i for i in range(2 * n)},
        compiler_params=pltpu.CompilerParams(has_side_effects=pltpu.SideEffectType.DATAFLOW_SIDE_EFFECTING),
    )(*[pltpu.with_memory_space_constraint(a, pltpu.HBM) for a in arrs],
      *[pltpu.with_memory_space_constraint(l, pltpu.HBM) for l in lands])
    return list(res[0:n]), list(res[n:2 * n]), list(res[2 * n:3 * n]), list(res[3 * n:4 * n]), res[-1]


def _xchg_wait(started, gather, after, name):
    send, recv, srcs, lands, _ = started
    n = len(srcs)

    def body(*refs):
        lnd = refs[n:2 * n]
        send, recv = refs[2 * n:3 * n], refs[3 * n:4 * n]
        me = (lax.axis_index("x"), lax.axis_index("y"), lax.axis_index("c"))
        for a in range(n):
            seven = lnd[a].at[pl.ds(0, NDEV - 1)]
            cp = pltpu.make_async_remote_copy(src_ref=seven, dst_ref=seven, send_sem=send[a], recv_sem=recv[a],
                                              device_id=me, device_id_type=pl.DeviceIdType.MESH)
            cp.wait_send()
            cp.wait_recv()

    hbm = pl.BlockSpec(memory_space=pltpu.HBM)
    sem = pl.BlockSpec(memory_space=pltpu.SEMAPHORE)
    res = pl.pallas_call(
        body,
        name=name,
        out_shape=tuple([pltpu.HBM(a.shape, a.dtype) for a in srcs] + [pltpu.HBM(l.shape, l.dtype) for l in lands]),
        in_specs=[hbm] * (2 * n) + [sem] * (2 * n) + [pl.BlockSpec(memory_space=pl.ANY)],
        out_specs=tuple([hbm] * (2 * n)),
        input_output_aliases={i: i for i in range(2 * n)},
        compiler_params=pltpu.CompilerParams(has_side_effects=pltpu.SideEffectType.DATAFLOW_SIDE_EFFECTING),
    )(*srcs, *lands, *send, *recv, after)
    return list(res[n:])


def _inproj(x, g, wtp):
    T = x.shape[0]
    tm = _tile(T, 512)

    def body(x_ref, g_ref, w_ref, qkv_ref, kt_ref, vt_ref, us_ref, fz_ref, ab_ref):
        a = _rms_fwd(x_ref[...], g_ref[...]).astype(BF16)
        ab_ref[...] = a
        qkv_ref[:, 0:AW] = _nt(a, w_ref[0:AW, :]).astype(BF16)
        kk = _nt(a, w_ref[AW:2 * AW, :])
        qkv_ref[:, AW:2 * AW] = kk.astype(BF16)
        kt_ref[...] = kk.T.astype(BF16)
        vv = _nt(a, w_ref[2 * AW:NQKV, :])
        qkv_ref[:, 2 * AW:NQKV] = vv.astype(BF16)
        vvt = vv.T.astype(BF16)
        one_row = jnp.where(lax.broadcasted_iota(jnp.int32, (HD, tm), 0) == 0, 1.0, 0.0).astype(BF16)
        for h in range(AW // HD):
            vt_ref[2 * h * HD:(2 * h + 1) * HD, :] = vvt[h * HD:(h + 1) * HD, :]
            vt_ref[(2 * h + 1) * HD:(2 * h + 2) * HD, :] = one_row
        us_ref[...] = _nt(a, w_ref[NQKV:NQKV + NUS, :])
        fz_ref[...] = _nt(a, w_ref[NQKV + NUS:ZP, :])

    row = lambda n: pl.BlockSpec((tm, n), lambda i: (i, 0))
    col = lambda n: pl.BlockSpec((n, tm), lambda i: (0, i))
    return pl.pallas_call(
        body,
        name="inproj",
        grid=(T // tm,),
        in_specs=[row(D), _full((1, D)), _full((ZP, D))],
        out_specs=[row(NQKV), col(AW), col(2 * AW), row(NUS), row(FPAD), row(D)],
        out_shape=[
            jax.ShapeDtypeStruct((T, NQKV), BF16),
            jax.ShapeDtypeStruct((AW, T), BF16),
            jax.ShapeDtypeStruct((2 * AW, T), BF16),
            jax.ShapeDtypeStruct((T, NUS), F32),
            jax.ShapeDtypeStruct((T, FPAD), F32),
            jax.ShapeDtypeStruct((T, D), BF16),
        ],
        compiler_params=_params(1),
    )(x, g, wtp)


def _log_sigmoid(z):
    return jnp.minimum(z, 0.0) - jnp.log1p(jnp.exp(-jnp.abs(z)))


def _split3(x):
    hi = x.astype(BF16)
    r1 = x - hi.astype(F32)
    mid = r1.astype(BF16)
    lo = (r1 - mid.astype(F32)).astype(BF16)
    return hi, mid, lo


AUG_A, AUG_B, AUG_ONE = 0, 3, 6


def _aug_lanes(rows, first_one, pieces):
    lane = lax.broadcasted_iota(jnp.int32, (rows, 128), 1)
    out = jnp.zeros((rows, 128), F32)
    if first_one is not None:
        out = jnp.where((lane >= first_one) & (lane < first_one + 3), 1.0, out)
    for n, piece in enumerate(pieces):
        out = jnp.where(lane == AUG_ONE + n, piece.astype(F32), out)
    return out.astype(BF16)


def _fcum(fz, fb, qkv):
    T = fz.shape[0]
    tb = _tile(T, 512)

    def body(fz_ref, fb_ref, k_ref, v_ref, ct_ref, kk_ref, vk_ref, carry, tri_s, sel_s):
        @pl.when(pl.program_id(0) == 0)
        def _():
            carry[...] = jnp.zeros_like(carry)
            src = lax.broadcasted_iota(jnp.int32, (128, 128), 0)
            dst = lax.broadcasted_iota(jnp.int32, (128, 128), 1)
            tri_s[...] = (dst <= src).astype(F32)
            for hp in range(4):
                for n in range(3):
                    pick = ((src == 2 * hp) & (dst == AUG_A + n)) | ((src == 2 * hp + 1) & (dst == AUG_B + n))
                    sel_s[3 * hp + n] = jnp.where(pick, -1.0, 0.0).astype(BF16)

        lf = _log_sigmoid(fz_ref[...] + fb_ref[...])
        run = carry[...]
        parts = []
        for blk in range(tb // 128):
            lf_b = lf[blk * 128:(blk + 1) * 128, :]
            parts.append(jnp.dot(tri_s[...], lf_b, precision=lax.Precision.HIGHEST, preferred_element_type=F32) + run)
            run = run + jnp.sum(lf_b, axis=0, keepdims=True)
        carry[...] = run
        cs = jnp.concatenate(parts, axis=0)
        ct_ref[...] = cs.T[0:8, :]

        pieces = _split3(cs)
        lane = lax.broadcasted_iota(jnp.int32, (1, 128), 1)
        ones = jnp.where((lane >= AUG_ONE) & (lane < AUG_ONE + 3), 1.0, 0.0)
        for hp in range(4):
            aug = jnp.zeros((tb, 128), F32) + ones
            for n, piece in enumerate(pieces):
                aug = aug + _nn(piece, sel_s[3 * hp + n])
            aug = aug.astype(BF16)
            kk_ref[:, hp * 256:hp * 256 + 128] = k_ref[:, hp * 128:(hp + 1) * 128]
            kk_ref[:, hp * 256 + 128:(hp + 1) * 256] = aug
            vk_ref[:, hp * 256:hp * 256 + 128] = v_ref[:, hp * 128:(hp + 1) * 128]
            vk_ref[:, hp * 256 + 128:(hp + 1) * 256] = aug

    row = lambda n, c: pl.BlockSpec((tb, n), lambda i: (i, c))
    return pl.pallas_call(
        body,
        name="fcum",
        grid=(T // tb,),
        in_specs=[row(FPAD, 0), _full((1, FPAD)), row(AW, 1), row(AW, 2)],
        out_specs=[pl.BlockSpec((8, tb), lambda i: (0, i)), row(2 * AW, 0), row(2 * AW, 0)],
        out_shape=[jax.ShapeDtypeStruct((8, T), F32), jax.ShapeDtypeStruct((T, 2 * AW), BF16),
                   jax.ShapeDtypeStruct((T, 2 * AW), BF16)],
        scratch_shapes=[pltpu.VMEM((1, FPAD), F32), pltpu.VMEM((128, 128), F32), pltpu.VMEM((12, 128, 128), BF16)],
        compiler_params=_params(1),
    )(fz, fb, qkv, qkv)


def _fold_rows(r, t, nq):
    if nq == 1:
        return r, t
    low = t <= r
    return jnp.where(low, r, nq - 1 - r), jnp.where(low, t, t - r - 1)


def _fold_cols(r, t, nq):
    if nq == 1:
        return r, t
    first = t < nq - r
    j = jnp.where(first, r, nq - 1 - r)
    return j, jnp.where(first, r + t, nq - 1 - r + (t - (nq - r)))


PAIRS = 4


def _fold_grid(nq):
    assert (nq == 1 or nq % 2 == 0) and PAIRS == 4
    return (4 // PAIRS, 1, 1) if nq == 1 else (4 // PAIRS, nq // 2, nq + 1)


def _head_rows(x2, hh, scale):
    is_a = lax.broadcasted_iota(jnp.int32, (1, 128), 1) < HD
    keep = is_a if hh == 0 else jnp.logical_not(is_a)
    return jnp.where(keep, x2, jnp.zeros_like(x2)) * scale


def _attn_fwd(qkv, kk, vt, ct):
    T = qkv.shape[0]
    tq = _tile(T, 512)
    tk = tq
    nq = T // tq

    def body(q_ref, kk_ref, vt_ref, ctq_ref, o_ref, lsec_ref, qw_s, m_s, acc_s):
        i, j = _fold_rows(pl.program_id(1), pl.program_id(2), nq)
        sub8 = lax.broadcasted_iota(jnp.int32, (8, 1), 0)

        def cref_of(pp, hh):
            head = 2 * (PAIRS * pl.program_id(0) + pp) + hh
            return jnp.sum(jnp.where(sub8 == head, ctq_ref[:, 0:1], 0.0), axis=0, keepdims=True)

        @pl.when(j == 0)
        def _():
            for pp in range(PAIRS):
                q2 = q_ref[:, pp * 128:(pp + 1) * 128]
                for hh in range(2):
                    rows = slice(hh * tq, (hh + 1) * tq)
                    qw_s[pp, rows, 0:128] = _head_rows(q2, hh, 0.125)
                    qw_s[pp, rows, 128:256] = _aug_lanes(tq, AUG_A if hh == 0 else AUG_B, _split3(cref_of(pp, hh)))
            m_s[...] = jnp.full_like(m_s, MASKV)
            acc_s[...] = jnp.zeros_like(acc_s)

        def step(masked):
            if masked:
                causal = lax.broadcasted_iota(jnp.int32, (tk, tq), 0) <= lax.broadcasted_iota(jnp.int32, (tk, tq), 1)
            logits = lambda pp: _nt(kk_ref[:, pp * 256:(pp + 1) * 256], qw_s[pp])
            s_next = logits(0)
            for pp in range(PAIRS):
                s2 = s_next
                if pp + 1 < PAIRS:
                    s_next = logits(pp + 1)
                for hh in range(2):
                    n = 2 * pp + hh
                    s = s2[:, hh * tq:(hh + 1) * tq]
                    if masked:
                        s = jnp.where(causal, s, MASKV)
                    m_prev = m_s[n]
                    m_new = jnp.maximum(m_prev, jnp.max(s, axis=0, keepdims=True))
                    pr = jnp.exp(s - m_new)
                    m_s[n] = m_new
                    acc_s[n] = jnp.exp(m_prev - m_new) * acc_s[n] + _nn(vt_ref[n * 128:(n + 1) * 128, :], pr.astype(BF16))

        @pl.when(j < i)
        def _():
            step(False)

        @pl.when(j == i)
        def _():
            step(True)
            sub = lax.broadcasted_iota(jnp.int32, (128, 1), 0)
            for pp in range(PAIRS):
                outs, lrow = [], []
                for hh in range(2):
                    n = 2 * pp + hh
                    den = acc_s[n, HD:HD + 1, :]
                    outs.append(acc_s[n, 0:HD, :] * (1.0 / den))
                    lrow.append(m_s[n] + jnp.log(den) - cref_of(pp, hh))
                o_ref[:, pp * 128:(pp + 1) * 128] = jnp.concatenate(outs, axis=0).T
                lsec_ref[:, pp * 128:(pp + 1) * 128] = jnp.where(sub == 0, lrow[0], jnp.where(sub == 1, lrow[1], 0.0)).T

    qi = lambda r, t: _fold_rows(r, t, nq)[0]
    kj = lambda r, t: _fold_rows(r, t, nq)[1]
    return pl.pallas_call(
        body,
        name="attn_fwd",
        grid=_fold_grid(nq),
        in_specs=[
            pl.BlockSpec((tq, PAIRS * 128), lambda g, r, t: (qi(r, t), g)),
            pl.BlockSpec((tk, PAIRS * 256), lambda g, r, t: (kj(r, t), g)),
            pl.BlockSpec((PAIRS * 256, tk), lambda g, r, t: (g, kj(r, t))),
            pl.BlockSpec((8, tq), lambda g, r, t: (0, qi(r, t))),
        ],
        out_specs=[pl.BlockSpec((tq, PAIRS * 128), lambda g, r, t: (qi(r, t), g))] * 2,
        out_shape=[jax.ShapeDtypeStruct((T, AW), F32)] * 2,
        scratch_shapes=[pltpu.VMEM((PAIRS, 2 * tq, 256), BF16), pltpu.VMEM((2 * PAIRS, 1, tq), F32),
                        pltpu.VMEM((2 * PAIRS, 128, tq), F32)],
        compiler_params=_params(3),
    )(qkv, kk, vt, ct)


def _sgu_forward(us_ref, lng, lnb, w_ref, bt_ref, mixed_s, vnb_s, tm):
    is_a = lax.broadcasted_iota(jnp.int32, (1, 128), 1) < HD
    u = us_ref[:, 0:SW]
    vs = us_ref[:, SW:NUS]
    ug, tu = _gelu(u)
    vg, tv = _gelu(vs)
    mu = jnp.mean(vg, axis=-1, keepdims=True)
    xc = vg - mu
    rstd = lax.rsqrt(jnp.mean(xc * xc, axis=-1, keepdims=True) + EPS)
    vhat = xc * rstd
    vnb_s[...] = (vhat * lng + lnb).astype(BF16)
    rr = lax.broadcasted_iota(jnp.int32, (CH, CH), 0)
    cc = lax.broadcasted_iota(jnp.int32, (CH, CH), 1)
    tril = cc <= rr
    for jj in range(4):
        wa = jnp.where(tril, w_ref[2 * jj], 0.0).astype(BF16)
        wb = jnp.where(tril, w_ref[2 * jj + 1], 0.0).astype(BF16)
        ba = bt_ref[:, 2 * jj:2 * jj + 1]
        bb = bt_ref[:, 2 * jj + 1:2 * jj + 2]
        for ch in range(tm // CH):
            rs, cs = slice(ch * CH, (ch + 1) * CH), slice(jj * 128, (jj + 1) * 128)
            vn2 = vnb_s[rs, cs]
            mixed_s[rs, cs] = jnp.where(is_a, _nn(wa, vn2) + ba, _nn(wb, vn2) + bb)
    mixed = mixed_s[...]
    return u, vs, ug, tu, tv, vhat, rstd, mixed, ug * mixed


def _sgu_out(us, yatt, x, lng, lnb, sgw, sgbt, gatt, gsg, wout, gpm):
    T = us.shape[0]
    tm = _tile(T, 512)

    def body(us_ref, ya_ref, x_ref, lng_ref, lnb_ref, w_ref, bt_ref, ga_ref, gs_ref, wo_ref, gp_ref,
             h1_ref, yb_ref, o_ref, mixed_s, vnb_s):
        ysg = _sgu_forward(us_ref, lng_ref[...], lnb_ref[...], w_ref, bt_ref, mixed_s, vnb_s, tm)[-1]
        yb_ref[:, 0:AW] = _rms_fwd(ya_ref[...], ga_ref[...]).astype(BF16)
        yb_ref[:, AW:D] = _rms_fwd(ysg, gs_ref[...]).astype(BF16)
        o = _nn(yb_ref[...], wo_ref[...])
        o_ref[...] = o
        h1_ref[...] = x_ref[...] + _rms_fwd(o, gp_ref[...])

    row = lambda n: pl.BlockSpec((tm, n), lambda i: (i, 0))
    return pl.pallas_call(
        body,
        name="sgu_out",
        grid=(T // tm,),
        in_specs=[row(NUS), row(AW), row(D), _full((1, SW)), _full((1, SW)), _full((8, CH, CH)), _full((CH, 8)),
                  _full((1, AW)), _full((1, SW)), _full((D, D)), _full((1, D))],
        out_specs=[row(D), row(D), row(D)],
        out_shape=[jax.ShapeDtypeStruct((T, D), F32), jax.ShapeDtypeStruct((T, D), BF16), jax.ShapeDtypeStruct((T, D), F32)],
        scratch_shapes=[pltpu.VMEM((tm, SW), F32), pltpu.VMEM((tm, SW), BF16)],
        compiler_params=_params(1),
    )(us, yatt, x, lng, lnb, sgw, sgbt, gatt, gsg, wout, gpm)


def _ffn_fwd(h1, gpre, w1g, w2g, gpost):
    T = h1.shape[0]
    tm = _tile(T, 512)
    nb, hb = w1g.shape[0], w1g.shape[2]

    def body(h1_ref, gpre_ref, w1_ref, w2_ref, gpost_ref, c2_ref, f1_ref, ff_ref, h2_ref):
        h1 = h1_ref[...]
        c2 = _rms_fwd(h1, gpre_ref[...]).astype(BF16)
        c2_ref[...] = c2
        ff = jnp.zeros((tm, D), F32)
        for j in range(nb):
            f1 = _nn(c2, w1_ref[j])
            f1_ref[:, j * hb:(j + 1) * hb] = f1.astype(BF16)
            r = jnp.maximum(f1, 0.0)
            ff = ff + _nn((r * r).astype(BF16), w2_ref[j])
        ff_ref[...] = ff
        h2_ref[...] = h1 + _rms_fwd(ff, gpost_ref[...])

    row = lambda n: pl.BlockSpec((tm, n), lambda i: (i, 0))
    once = lambda shape: pl.BlockSpec(shape, lambda i: (0,) * len(shape), pipeline_mode=pl.Buffered(1))
    return pl.pallas_call(
        body,
        name="ffn_fwd",
        grid=(T // tm,),
        in_specs=[row(D), _full((1, D)), once((nb, D, hb)), once((nb, hb, D)), _full((1, D))],
        out_specs=[row(D), row(DFF), row(D), row(D)],
        out_shape=[jax.ShapeDtypeStruct((T, D), BF16), jax.ShapeDtypeStruct((T, DFF), BF16),
                   jax.ShapeDtypeStruct((T, D), F32), jax.ShapeDtypeStruct((T, D), F32)],
        compiler_params=_params(1, VMEM_LIMIT_BIG),
    )(h1, gpre, w1g, w2g, gpost)


def _ple_loss(h2, p, tgt, wg, bg, wpe):
    T = h2.shape[0]
    tm = _tile(T, 512)

    def body(h2_ref, p_ref, t_ref, wg_ref, bg_ref, wpe_ref, dh2_ref, dbg_ref, loss_ref, dwg_ref, dwpe_ref):
        @pl.when(pl.program_id(0) == 0)
        def _():
            for r in (dbg_ref, loss_ref, dwg_ref, dwpe_ref):
                r[...] = jnp.zeros_like(r)

        h2 = h2_ref[...]
        h2b = h2.astype(BF16)
        gate = jax.nn.sigmoid(_nn(h2b, wg_ref[...]) + bg_ref[...])
        pb = p_ref[...].astype(BF16)
        pe = _nn(pb, wpe_ref[...])
        diff = (h2 + gate * pe) - t_ref[...]
        loss_ref[...] += jnp.sum(diff * diff)
        dh3 = diff * (1.0 / D)
        dpre = (dh3 * pe) * (gate * (1.0 - gate))
        dpre_b = dpre.astype(BF16)
        dbg_ref[...] += jnp.sum(dpre, axis=0, keepdims=True)
        dh2_ref[...] = dh3 + _nt(dpre_b, wg_ref[...])
        dwg_ref[...] += _nn(h2b.T, dpre_b)
        dwpe_ref[...] += _nn(pb.T, (dh3 * gate).astype(BF16))

    row = lambda n: pl.BlockSpec((tm, n), lambda i: (i, 0))
    once = lambda shape: pl.BlockSpec(shape, lambda i: (0,) * len(shape), pipeline_mode=pl.Buffered(1))
    return pl.pallas_call(
        body,
        name="ple_loss",
        grid=(T // tm,),
        in_specs=[row(D), row(PLE), row(D), _full((D, D)), _full((1, D)), _full((PLE, D))],
        out_specs=[row(D), _full((1, D)), _full((8, 128)), once((D, D)), once((PLE, D))],
        out_shape=[jax.ShapeDtypeStruct((T, D), F32), jax.ShapeDtypeStruct((1, D), F32),
                   jax.ShapeDtypeStruct((8, 128), F32), jax.ShapeDtypeStruct((D, D), F32),
                   jax.ShapeDtypeStruct((PLE, D), F32)],
        compiler_params=_params(1),
    )(h2, p, tgt, wg, bg, wpe)


def _ffn_bwd(dh2, ff, h1, f1, w1g, w2g, gpost, gpre):
    T = dh2.shape[0]
    tm = _tile(T, 512)
    nb, hb = w1g.shape[0], w1g.shape[2]

    def body(dh2_ref, ff_ref, h1_ref, f1_ref, w1_ref, w2_ref, gpost_ref, gpre_ref,
             dffb_ref, df1_ref, dh1_ref, dgpost_ref, dgpre_ref):
        @pl.when(pl.program_id(0) == 0)
        def _():
            dgpost_ref[...] = jnp.zeros_like(dgpost_ref)
            dgpre_ref[...] = jnp.zeros_like(dgpre_ref)

        dh2 = dh2_ref[...]
        dff, dg = _rms_bwd(dh2, ff_ref[...], gpost_ref[...])
        dffb = dff.astype(BF16)
        dffb_ref[...] = dffb
        dgpost_ref[...] += dg
        dc2 = jnp.zeros((tm, D), F32)
        for j in range(nb):
            cols = slice(j * hb, (j + 1) * hb)
            dact = _nt(dffb, w2_ref[j])
            df1 = (dact * (2.0 * jnp.maximum(f1_ref[:, cols].astype(F32), 0.0))).astype(BF16)
            df1_ref[:, cols] = df1
            dc2 = dc2 + _nt(df1, w1_ref[j])
        dx, dg = _rms_bwd(dc2, h1_ref[...], gpre_ref[...])
        dh1_ref[...] = dh2 + dx
        dgpre_ref[...] += dg

    row = lambda n: pl.BlockSpec((tm, n), lambda i: (i, 0))
    once = lambda shape: pl.BlockSpec(shape, lambda i: (0,) * len(shape), pipeline_mode=pl.Buffered(1))
    return pl.pallas_call(
        body,
        name="ffn_bwd",
        grid=(T // tm,),
        in_specs=[row(D), row(D), row(D), row(DFF), once((nb, D, hb)), once((nb, hb, D)), _full((1, D)), _full((1, D))],
        out_specs=[row(D), row(DFF), row(D), _full((1, D)), _full((1, D))],
        out_shape=[jax.ShapeDtypeStruct((T, D), BF16), jax.ShapeDtypeStruct((T, DFF), BF16),
                   jax.ShapeDtypeStruct((T, D), F32), jax.ShapeDtypeStruct((1, D), F32),
                   jax.ShapeDtypeStruct((1, D), F32)],
        compiler_params=_params(1, VMEM_LIMIT_BIG),
    )(dh2, ff, h1, f1, w1g, w2g, gpost, gpre)


def _mix_bwd(dh1, o, us, yatt, qkv, lsec, yb, lng, lnb, sgw, sgwt, sgbt, gatt, gsg, wout, gpm):
    T = dh1.shape[0]
    tm = _tile(T, 512)

    def body(dh1_ref, o_ref, us_ref, ya_ref, q_ref, l_ref, yb_ref, lng_ref, lnb_ref, w_ref, wt_ref, bt_ref, ga_ref, gs_ref,
             wo_ref, gp_ref, qw_ref, dow_ref, dus_ref, dwo_ref, dw_ref, dbt_ref, dlng_ref, dlnb_ref, dga_ref, dgs_ref,
             dgp_ref, mixed_s, vnb_s, dvn_s):
        @pl.when(pl.program_id(0) == 0)
        def _():
            for r in (dwo_ref, dw_ref, dbt_ref, dlng_ref, dlnb_ref, dga_ref, dgs_ref, dgp_ref):
                r[...] = jnp.zeros_like(r)

        is_a = lax.broadcasted_iota(jnp.int32, (1, 128), 1) < HD
        lane = lax.broadcasted_iota(jnp.int32, (1, 128), 1)
        do, dg = _rms_bwd(dh1_ref[...], o_ref[...], gp_ref[...])
        dgp_ref[...] += dg
        dob = do.astype(BF16)
        dwo_ref[...] += _nn(yb_ref[...].T, dob)
        dy = _nt(dob, wo_ref[...])
        ya = ya_ref[...]
        datt, dg = _rms_bwd(dy[:, 0:AW], ya, ga_ref[...])
        dga_ref[...] += dg
        _attn_operands(q_ref[...], datt, ya, l_ref[...], qw_ref, dow_ref)

        lng = lng_ref[...]
        u, vs, ug, tu, tv, vhat, rstd, mixed, ysg = _sgu_forward(us_ref, lng, lnb_ref[...], w_ref, bt_ref, mixed_s, vnb_s, tm)
        dysg, dg = _rms_bwd(dy[:, AW:D], ysg, gs_ref[...])
        dgs_ref[...] += dg
        dus_ref[:, 0:SW] = ((dysg * mixed) * _gelu_grad(u, tu)).astype(BF16)
        dmix = dysg * ug

        rr = lax.broadcasted_iota(jnp.int32, (CH, CH), 0)
        cc = lax.broadcasted_iota(jnp.int32, (CH, CH), 1)
        tril = cc <= rr
        triu = cc >= rr
        for jj in range(4):
            wta = jnp.where(triu, wt_ref[2 * jj], 0.0).astype(BF16)
            wtb = jnp.where(triu, wt_ref[2 * jj + 1], 0.0).astype(BF16)
            for ch in range(tm // CH):
                rs, cs = slice(ch * CH, (ch + 1) * CH), slice(jj * 128, (jj + 1) * 128)
                dm2 = dmix[rs, cs]
                dma = jnp.where(is_a, dm2, 0.0)
                dmb = jnp.where(is_a, 0.0, dm2)
                dma_b, dmb_b = dma.astype(BF16), dmb.astype(BF16)
                vn2 = vnb_s[rs, cs]
                dw_ref[2 * jj] += jnp.where(tril, _nt(dma_b, vn2), 0.0)
                dw_ref[2 * jj + 1] += jnp.where(tril, _nt(dmb_b, vn2), 0.0)
                dvn_s[rs, cs] = _nn(wta, dma_b) + _nn(wtb, dmb_b)
                dba = jnp.sum(dma, axis=1, keepdims=True)
                dbb = jnp.sum(dmb, axis=1, keepdims=True)
                dbt_ref[...] += jnp.where(lane == 2 * jj, dba, 0.0) + jnp.where(lane == 2 * jj + 1, dbb, 0.0)

        dvn = dvn_s[...]
        dlng_ref[...] += jnp.sum(dvn * vhat, axis=0, keepdims=True)
        dlnb_ref[...] += jnp.sum(dvn, axis=0, keepdims=True)
        dvh = dvn * lng
        dvg = rstd * (dvh - jnp.mean(dvh, axis=-1, keepdims=True) - vhat * jnp.mean(dvh * vhat, axis=-1, keepdims=True))
        dus_ref[:, SW:NUS] = (dvg * _gelu_grad(vs, tv)).astype(BF16)

    row = lambda n: pl.BlockSpec((tm, n), lambda i: (i, 0))
    return pl.pallas_call(
        body,
        name="mix_bwd",
        grid=(T // tm,),
        in_specs=[row(D), row(D), row(NUS), row(AW), row(AW), row(AW), row(D), _full((1, SW)), _full((1, SW)),
                  _full((8, CH, CH)), _full((8, CH, CH)), _full((CH, 8)), _full((1, AW)), _full((1, SW)), _full((D, D)),
                  _full((1, D))],
        out_specs=[row(4 * AW), row(4 * AW), row(NUS),
                   pl.BlockSpec((D, D), lambda i: (0, 0), pipeline_mode=pl.Buffered(1)), _full((8, CH, CH)),
                   _full((CH, 128)), _full((1, SW)), _full((1, SW)), _full((1, AW)), _full((1, SW)), _full((1, D))],
        out_shape=[jax.ShapeDtypeStruct((T, 4 * AW), BF16),
                   jax.ShapeDtypeStruct((T, 4 * AW), BF16), jax.ShapeDtypeStruct((T, NUS), BF16),
                   jax.ShapeDtypeStruct((D, D), F32),
                   jax.ShapeDtypeStruct((8, CH, CH), F32), jax.ShapeDtypeStruct((CH, 128), F32),
                   jax.ShapeDtypeStruct((1, SW), F32), jax.ShapeDtypeStruct((1, SW), F32),
                   jax.ShapeDtypeStruct((1, AW), F32), jax.ShapeDtypeStruct((1, SW), F32), jax.ShapeDtypeStruct((1, D), F32)],
        scratch_shapes=[pltpu.VMEM((tm, SW), F32), pltpu.VMEM((tm, SW), BF16), pltpu.VMEM((tm, SW), F32)],
        compiler_params=_params(1, VMEM_LIMIT_BIG),
    )(dh1, o, us, yatt, qkv, lsec, yb, lng, lnb, sgw, sgwt, sgbt, gatt, gsg, wout, gpm)


def _attn_operands(q, do, o, lsec, qw_ref, dow_ref):
    rows = do.shape[0]
    feat = lax.broadcasted_iota(jnp.int32, (AW, 128), 0)
    head = lax.broadcasted_iota(jnp.int32, (AW, 128), 1)
    sel = jnp.where((feat >= head * HD) & (feat < (head + 1) * HD), 1.0, 0.0)
    delta = jnp.dot(do * o, sel, precision=lax.Precision.HIGHEST, preferred_element_type=F32)
    for hp in range(4):
        cols = slice(hp * 128, (hp + 1) * 128)
        for hh in range(2):
            base = (2 * hp + hh) * 256
            lc = lsec[:, hp * 128 + hh:hp * 128 + hh + 1]
            d_h = delta[:, 2 * hp + hh:2 * hp + hh + 1]
            qw_ref[:, base:base + 128] = _head_rows(q[:, cols], hh, 0.125)
            qw_ref[:, base + 128:base + 256] = _aug_lanes(rows, AUG_A if hh == 0 else AUG_B, _split3(-lc))
            dow_ref[:, base:base + 128] = _head_rows(do[:, cols], hh, 1.0).astype(BF16)
            dow_ref[:, base + 128:base + 256] = _aug_lanes(rows, None, _split3(-d_h))


def _attn_bwd(kk, vk, kt, qw, dow, after):
    T = kk.shape[0]
    tq = _tile(T, 512)
    tk = tq
    nq = T // tq

    def body(kk_ref, vk_ref, kt_ref, qw_ref, dow_ref, after_ref, dqt_ref, dcq_ref, dk_ref, dv_ref, dck_ref, dk_s, dv_s, dck_s):
        j, i = _fold_cols(pl.program_id(1), pl.program_id(2), nq)
        sub8 = lax.broadcasted_iota(jnp.int32, (8, 1), 0)
        lane = lax.broadcasted_iota(jnp.int32, (1, 128), 1)

        @pl.when((pl.program_id(1) == 0) & (pl.program_id(2) == 0))
        def _():
            dqt_ref[...] = jnp.zeros_like(dqt_ref)
            dcq_ref[...] = jnp.zeros_like(dcq_ref)

        @pl.when(i == j)
        def _():
            dk_s[...] = jnp.zeros_like(dk_s)
            dv_s[...] = jnp.zeros_like(dv_s)
            dck_s[...] = jnp.zeros_like(dck_s)

        def step(masked):
            cols = pl.ds(pl.multiple_of(i * tq, tq), tq)
            sub = lax.broadcasted_iota(jnp.int32, (128, 1), 0)
            if masked:
                causal = lax.broadcasted_iota(jnp.int32, (tk, tq), 0) <= lax.broadcasted_iota(jnp.int32, (tk, tq), 1)

            def logits(n):
                pair, base = n // 2, n * 256
                return (_nt(kk_ref[:, pair * 256:(pair + 1) * 256], qw_ref[:, base:base + 256]),
                        _nt(vk_ref[:, pair * 256:(pair + 1) * 256], dow_ref[:, base:base + 256]))

            ahead = logits(0)
            dcq = jnp.zeros((8, tq), F32)
            dck = jnp.zeros((tk, 128), F32)
            for pp in range(PAIRS):
                lanes = slice(pp * 128, (pp + 1) * 128)
                kt2 = kt_ref[lanes, :] * 0.125
                dv = jnp.zeros((tk, 128), F32)
                dk = jnp.zeros((tk, 128), F32)
                dqts = []
                for hh in range(2):
                    head = 2 * pp + hh
                    base = head * 256
                    qw_h = qw_ref[:, base:base + 256]
                    dow_h = dow_ref[:, base:base + 256]
                    logp, dp = ahead
                    if head + 1 < 2 * PAIRS:
                        ahead = logits(head + 1)
                    pr = jnp.exp(logp)
                    if masked:
                        pr = jnp.where(causal, pr, 0.0)
                    ds = pr * dp
                    ds_b = ds.astype(BF16)
                    dv = dv + _nn(pr.astype(BF16), dow_h[:, 0:128])
                    dk = dk + _nn(ds_b, qw_h[:, 0:128])
                    dqts.append(_nn(kt2, ds_b))
                    dcq = dcq + jnp.where(sub8 == head, jnp.sum(ds, axis=0, keepdims=True), 0.0)
                    dck = dck - jnp.where(lane == head, jnp.sum(ds, axis=1, keepdims=True), 0.0)
                dv_s[:, lanes] += dv
                dk_s[:, lanes] += dk
                dqt_ref[lanes, cols] += jnp.where(sub < HD, dqts[0], dqts[1])
            dck_s[...] += dck
            dcq_ref[:, cols] += dcq

        @pl.when(i > j)
        def _():
            step(False)

        @pl.when(i == j)
        def _():
            step(True)

        @pl.when(i == nq - 1)
        def _():
            dk_ref[...] = dk_s[...].astype(BF16)
            dv_ref[...] = dv_s[...].astype(BF16)
            dck_ref[...] = dck_s[...]

    kj = lambda r, t: _fold_cols(r, t, nq)[0]
    qi = lambda r, t: _fold_cols(r, t, nq)[1]
    krow = lambda g, r, t: (kj(r, t), g)
    qrow = lambda g, r, t: (qi(r, t), g)
    return pl.pallas_call(
        body,
        name="attn_bwd",
        grid=_fold_grid(nq),
        in_specs=[
            pl.BlockSpec((tk, PAIRS * 256), krow),
            pl.BlockSpec((tk, PAIRS * 256), krow),
            pl.BlockSpec((PAIRS * 128, tk), lambda g, r, t: (g, kj(r, t))),
            pl.BlockSpec((tq, PAIRS * 512), qrow),
            pl.BlockSpec((tq, PAIRS * 512), qrow),
            pl.BlockSpec(after.shape, lambda g, r, t: (0,) * after.ndim),
        ],
        out_specs=[
            pl.BlockSpec((PAIRS * 128, T), lambda g, r, t: (g, 0), pipeline_mode=pl.Buffered(1)),
            pl.BlockSpec((8, T), lambda g, r, t: (0, 0), pipeline_mode=pl.Buffered(1)),
            pl.BlockSpec((tk, PAIRS * 128), krow),
            pl.BlockSpec((tk, PAIRS * 128), krow),
            pl.BlockSpec((tk, 128), lambda g, r, t: (kj(r, t), 0)),
        ],
        out_shape=[jax.ShapeDtypeStruct((AW, T), F32), jax.ShapeDtypeStruct((8, T), F32),
                   jax.ShapeDtypeStruct((T, AW), BF16), jax.ShapeDtypeStruct((T, AW), BF16),
                   jax.ShapeDtypeStruct((T, FPAD), F32)],
        scratch_shapes=[pltpu.VMEM((tk, PAIRS * 128), F32), pltpu.VMEM((tk, PAIRS * 128), F32),
                        pltpu.VMEM((tk, 128), F32)],
        compiler_params=_params(3),
    )(kk, vk, kt, qw, dow, after)


def _fgate_bwd(dcq, dck, fz, fb):
    T = dck.shape[0]
    tb = _tile(T, 512)
    nb = T // tb

    def body(dcq_ref, dck_ref, fz_ref, fb_ref, df_ref, dfb_ref, carry):
        @pl.when(pl.program_id(0) == 0)
        def _():
            carry[...] = jnp.zeros_like(carry)
            dfb_ref[...] = jnp.zeros_like(dfb_ref)

        head = lax.broadcasted_iota(jnp.int32, (8, FPAD), 0)
        eye = jnp.where(head == lax.broadcasted_iota(jnp.int32, (8, FPAD), 1), 1.0, 0.0)
        dcv = dck_ref[...] + lax.dot_general(dcq_ref[...], eye, (((0,), (0,)), ((), ())),
                                             precision=lax.Precision.HIGHEST, preferred_element_type=F32)
        r = lax.broadcasted_iota(jnp.int32, (128, 128), 0)
        cc = lax.broadcasted_iota(jnp.int32, (128, 128), 1)
        tri = (cc >= r).astype(F32)
        run = carry[...]
        parts = []
        for blk in reversed(range(tb // 128)):
            dc_b = dcv[blk * 128:(blk + 1) * 128, :]
            parts.append(jnp.dot(tri, dc_b, precision=lax.Precision.HIGHEST, preferred_element_type=F32) + run)
            run = run + jnp.sum(dc_b, axis=0, keepdims=True)
        carry[...] = run
        dlf = jnp.concatenate(parts[::-1], axis=0)
        lane = lax.broadcasted_iota(jnp.int32, (tb, FPAD), 1)
        df = jnp.where(lane < 8, dlf * jax.nn.sigmoid(-(fz_ref[...] + fb_ref[...])), 0.0)
        df_ref[...] = df.astype(BF16)
        dfb_ref[...] += jnp.sum(df, axis=0, keepdims=True)

    rev = pl.BlockSpec((tb, FPAD), lambda i: (nb - 1 - i, 0))
    return pl.pallas_call(
        body,
        name="fgate_bwd",
        grid=(nb,),
        in_specs=[pl.BlockSpec((8, tb), lambda i: (0, nb - 1 - i)), rev, rev, _full((1, FPAD))],
        out_specs=[rev, _full((1, FPAD))],
        out_shape=[jax.ShapeDtypeStruct((T, FPAD), BF16), jax.ShapeDtypeStruct((1, FPAD), F32)],
        scratch_shapes=[pltpu.VMEM((1, FPAD), F32)],
        compiler_params=_params(1),
    )(dcq, dck, fz, fb)


def _inproj_bwd(dqt, dk, dv, dus, dfz, wtp, x, dh1, g):
    T = x.shape[0]
    tm = _tile(T, 512)

    def body(dq_ref, dk_ref, dv_ref, dus_ref, dfz_ref, w_ref, x_ref, dh1_ref, g_ref, gx_ref, dg_ref):
        @pl.when(pl.program_id(0) == 0)
        def _():
            dg_ref[...] = jnp.zeros_like(dg_ref)

        da = _tn(dq_ref[...].astype(BF16), w_ref[0:AW, :])
        da += _nn(dk_ref[...], w_ref[AW:2 * AW, :])
        da += _nn(dv_ref[...], w_ref[2 * AW:NQKV, :])
        da += _nn(dus_ref[...], w_ref[NQKV:NQKV + NUS, :])
        da += _nn(dfz_ref[...], w_ref[NQKV + NUS:ZP, :])
        dx, dg = _rms_bwd(da, x_ref[...], g_ref[...])
        gx_ref[...] = dh1_ref[...] + dx
        dg_ref[...] += dg

    row = lambda n: pl.BlockSpec((tm, n), lambda i: (i, 0))
    return pl.pallas_call(
        body,
        name="inproj_bwd",
        grid=(T // tm,),
        in_specs=[pl.BlockSpec((AW, tm), lambda i: (0, i)), row(AW), row(AW), row(NUS), row(FPAD), _full((ZP, D)),
                  row(D), row(D), _full((1, D))],
        out_specs=[row(D), _full((1, D))],
        out_shape=[jax.ShapeDtypeStruct((T, D), F32), jax.ShapeDtypeStruct((1, D), F32)],
        compiler_params=_params(1),
    )(dqt, dk, dv, dus, dfz, wtp, x, dh1, g)


def _sq_relu(f1):
    r = jnp.maximum(f1.astype(F32), 0.0)
    return (r * r).astype(BF16)


def _wgrad(a, bs, name, a_fn=None, out_dtype=F32, transposed=False):
    T, K = a.shape
    tt = _tile(T, 1024)
    nb = len(bs)
    narrow = out_dtype != F32

    def out_dims(b, layout):
        if transposed:
            return (b.shape[0] if layout == "t" else b.shape[1], K)
        if layout == "t":
            return (K, b.shape[0])
        N = b.shape[1]
        if layout is None:
            return (K, N)
        return (N // layout[1], K, layout[1]) if layout[0] == "col" else (K // layout[1], layout[1], N)

    shapes = [out_dims(b, layout) for b, layout in bs]

    def body(*refs):
        a_ref, b_refs, o_refs = refs[0], refs[1:1 + nb], refs[1 + nb:1 + 2 * nb]
        accs = refs[1 + 2 * nb:] if narrow else o_refs

        @pl.when(pl.program_id(0) == 0)
        def _():
            for acc in accs:
                acc[...] = jnp.zeros_like(acc)

        av = (a_ref[...] if a_fn is None else a_fn(a_ref[...])).astype(BF16)
        at = None if transposed else av.T
        for (b, layout), b_ref, o_ref in zip(bs, b_refs, accs):
            if transposed:
                bv = b_ref[...].astype(BF16)
                o_ref[...] += _nn(bv, av) if layout == "t" else _tn(bv, av)
            elif layout is None:
                o_ref[...] += _nn(at, b_ref[...].astype(BF16))
            elif layout == "t":
                o_ref[...] += _nt(at, b_ref[...].astype(BF16))
            elif layout[0] == "col":
                n = layout[1]
                for k in range(b.shape[1] // n):
                    o_ref[k] += _nn(at, b_ref[:, k * n:(k + 1) * n].astype(BF16))
            else:
                n = layout[1]
                bv = b_ref[...].astype(BF16)
                for k in range(K // n):
                    o_ref[k] += _nn(at[k * n:(k + 1) * n, :], bv)

        if narrow:
            @pl.when(pl.program_id(0) == T // tt - 1)
            def _():
                for o_ref, acc in zip(o_refs, accs):
                    o_ref[...] = acc[...].astype(out_dtype)

    once = lambda shape: pl.BlockSpec(shape, lambda t: (0,) * len(shape), pipeline_mode=pl.Buffered(1))
    res = pl.pallas_call(
        body,
        name=name,
        grid=(T // tt,),
        in_specs=[pl.BlockSpec((tt, K), lambda t: (t, 0))] + [
            pl.BlockSpec((b.shape[0], tt), lambda t: (0, t)) if layout == "t" else pl.BlockSpec((tt, b.shape[1]), lambda t: (t, 0))
            for b, layout in bs],
        out_specs=[once(s) for s in shapes],
        out_shape=[jax.ShapeDtypeStruct(s, out_dtype) for s in shapes],
        scratch_shapes=[pltpu.VMEM(s, F32) for s in shapes] if narrow else [],
        compiler_params=_params(1, VMEM_LIMIT_BIG),
    )(a, *[b for b, _ in bs])
    return res


def _adam_math(w, g, m, v):
    m = ADAM_B1 * m + (1.0 - ADAM_B1) * g
    v = ADAM_B2 * v + (1.0 - ADAM_B2) * (g * g)
    m_hat = m / (1.0 - ADAM_B1 ** ADAM_STEP)
    v_hat = v / (1.0 - ADAM_B2 ** ADAM_STEP)
    delta = -ADAM_LR * (m_hat / (jnp.sqrt(v_hat) + ADAM_EPS) + ADAM_WD * w)
    return delta, m, v


def _adam(parts, w, m, v, name):
    R, C = w.shape
    br = 128 if R % 128 == 0 else R

    def body(p_ref, w_ref, m_ref, v_ref, g_ref, d_ref, nm_ref, nv_ref):
        g = p_ref[0].astype(F32)
        for s in range(1, NDEV):
            g = g + p_ref[s].astype(F32)
        g_ref[...] = g
        d_ref[...], nm_ref[...], nv_ref[...] = _adam_math(w_ref[...], g, m_ref[...], v_ref[...])

    blk = pl.BlockSpec((br, C), lambda i: (i, 0))
    return pl.pallas_call(
        body,
        name=name,
        grid=(R // br,),
        in_specs=[pl.BlockSpec((NDEV, br, C), lambda i: (0, i, 0)), blk, blk, blk],
        out_specs=[blk] * 4,
        out_shape=[jax.ShapeDtypeStruct((R, C), F32)] * 4,
        compiler_params=_params(1),
    )(parts, w, m, v)


_SMALL = (("sg_w", 8 * CH * CH), ("f_bias", 8), ("sg_ln_g", SW), ("sg_ln_b", SW), ("sg_b", 8 * CH), ("att_out_g", AW),
          ("sg_out_g", SW), ("pre_mix_g", D), ("post_mix_g", D), ("pre_ffn_g", D), ("post_ffn_g", D), ("ple_gate_b", D))
_SEG = 8 * 128


def _seg_rows(size):
    return 8 * (-(-size // _SEG))


_LATE = ("f_bias", "pre_mix_g")


def _pack(vals, late, loss_acc=None):
    parts = []
    for name, size in _SMALL:
        if (name in _LATE) == late:
            rows = _seg_rows(size)
            parts.append(jnp.pad(vals[name].reshape(-1), (0, rows * 128 - size)).reshape(rows, 128))
    if loss_acc is not None:
        parts.append(loss_acc)
    return jnp.concatenate(parts, axis=0)


def _adam_small(parts_early, parts_late, ws, ms, vs):
    n = len(_SMALL)
    names = [name for name, _ in _SMALL]

    def body(*refs):
        pe_ref, pl_ref, refs = refs[0], refs[1], refs[1:]
        w_refs, m_refs, v_refs = refs[1:1 + n], refs[1 + n:1 + 2 * n], refs[1 + 2 * n:1 + 3 * n]
        loss_ref, outs = refs[1 + 3 * n], refs[2 + 3 * n:]
        sums, row = [], [0, 0]
        for p_ref in (pe_ref, pl_ref):
            tot = p_ref[0]
            for s in range(1, NDEV):
                tot = tot + p_ref[s]
            sums.append(tot)
        for k, (name, size) in enumerate(_SMALL):
            dst = [outs[kind * n + k] for kind in range(4)]
            which = 1 if name in _LATE else 0
            g_all, r = sums[which], row[which]

            def update(g, idx):
                vals = (g,) + _adam_math(w_refs[k][idx], g, m_refs[k][idx], v_refs[k][idx])
                for d, val in zip(dst, vals):
                    d[idx] = val

            if name == "sg_w":
                for grp in range(8):
                    update(g_all[r + grp * CH:r + (grp + 1) * CH, :], (0, grp))
            elif name == "sg_b":
                update(g_all[r:r + 8, :], (0,))
            elif name == "f_bias":
                update(g_all[r:r + 1, 0:8], (slice(None),))
            else:
                update(jnp.concatenate([g_all[r + q:r + q + 1, :] for q in range(size // 128)], axis=1), (slice(None),))
            row[which] += _seg_rows(size)
        loss_ref[...] = sums[0][row[0]:row[0] + 1, 0:1] * (0.5 / D)

    arrs = [parts_early, parts_late] + [d[name] for d in (ws, ms, vs) for name in names]
    res = pl.pallas_call(
        body,
        name="adam_small",
        in_specs=[_full(a.shape) for a in arrs],
        out_specs=[_full((1, 1))] + [_full(ws[name].shape) for _ in range(4) for name in names],
        out_shape=[jax.ShapeDtypeStruct((1, 1), F32)] + [jax.ShapeDtypeStruct(ws[name].shape, F32) for _ in range(4) for name in names],
        compiler_params=pltpu.CompilerParams(vmem_limit_bytes=VMEM_LIMIT),
    )(*arrs)
    return res[0], {name: [res[1 + kind * n + k] for kind in range(4)] for k, name in enumerate(names)}


def kernel(x, p, w_in, f_bias, sg_ln_g, sg_ln_b, sg_w, sg_b, att_out_g, sg_out_g, w_out, pre_mix_g, post_mix_g, pre_ffn_g, post_ffn_g, w_ff1, w_ff2, ple_w, ple_gate_w, ple_gate_b, loss_target, m_w_in, m_f_bias, m_sg_ln_g, m_sg_ln_b, m_sg_w, m_sg_b, m_att_out_g, m_sg_out_g, m_w_out, m_pre_mix_g, m_post_mix_g, m_pre_ffn_g, m_post_ffn_g, m_w_ff1, m_w_ff2, m_ple_w, m_ple_gate_w, m_ple_gate_b, v_w_in, v_f_bias, v_sg_ln_g, v_sg_ln_b, v_sg_w, v_sg_b, v_att_out_g, v_sg_out_g, v_w_out, v_pre_mix_g, v_post_mix_g, v_pre_ffn_g, v_post_ffn_g, v_w_ff1, v_w_ff2, v_ple_w, v_ple_gate_w, v_ple_gate_b):
    small_w = dict(sg_w=sg_w, f_bias=f_bias, sg_ln_g=sg_ln_g, sg_ln_b=sg_ln_b, sg_b=sg_b, att_out_g=att_out_g,
                   sg_out_g=sg_out_g, pre_mix_g=pre_mix_g, post_mix_g=post_mix_g, pre_ffn_g=pre_ffn_g,
                   post_ffn_g=post_ffn_g, ple_gate_b=ple_gate_b)
    small_m = dict(sg_w=m_sg_w, f_bias=m_f_bias, sg_ln_g=m_sg_ln_g, sg_ln_b=m_sg_ln_b, sg_b=m_sg_b, att_out_g=m_att_out_g,
                   sg_out_g=m_sg_out_g, pre_mix_g=m_pre_mix_g, post_mix_g=m_post_mix_g, pre_ffn_g=m_pre_ffn_g,
                   post_ffn_g=m_post_ffn_g, ple_gate_b=m_ple_gate_b)
    small_v = dict(sg_w=v_sg_w, f_bias=v_f_bias, sg_ln_g=v_sg_ln_g, sg_ln_b=v_sg_ln_b, sg_b=v_sg_b, att_out_g=v_att_out_g,
                   sg_out_g=v_sg_out_g, pre_mix_g=v_pre_mix_g, post_mix_g=v_post_mix_g, pre_ffn_g=v_pre_ffn_g,
                   post_ffn_g=v_post_ffn_g, ple_gate_b=v_ple_gate_b)
    big = dict(w_in=(w_in, m_w_in, v_w_in), w_out=(w_out, m_w_out, v_w_out), w_ff1=(w_ff1, m_w_ff1, v_w_ff1),
               w_ff2=(w_ff2, m_w_ff2, v_w_ff2), ple_w=(ple_w, m_ple_w, v_ple_w),
               ple_gate_w=(ple_gate_w, m_ple_gate_w, v_ple_gate_w))

    xt, pt, tgt = x[0], p[0, 0], loss_target[0]
    ws = W_IN_COLS // NDEV

    gw_in = _gather(jnp.pad(jnp.transpose(w_in[0]).astype(BF16), ((0, WS_PAD - ws), (0, 0))), "gather_w_in")
    gw_in, others = lax.optimization_barrier((gw_in, [w_out[0].astype(BF16), w_ff1[0].astype(BF16), w_ff2[0].astype(BF16),
                                                      ple_w[0].astype(BF16), ple_gate_w[0].astype(BF16)]))
    rest = _xchg_start(others, True, "gather_rest_start")
    wint = gw_in[:, 0:ws, :].reshape(W_IN_COLS, D)
    wtp = jnp.concatenate([wint[0:NQKV], wint[NQKV + 8:W_IN_COLS], wint[NQKV:NQKV + 8],
                           jnp.zeros((FPAD - 8, D), BF16)], axis=0)

    fb = jnp.pad(f_bias.astype(F32), ((0, 0), (0, FPAD - 8)))
    sgw = sg_w[0]
    sgwt = jnp.transpose(sg_w[0], (0, 2, 1))
    sgbt = jnp.transpose(sg_b[0])

    qkv, kt, vt, us, fz, ab = _inproj(xt, pre_mix_g + rest[4][0, 0], wtp)
    ct, kk, vk = _fcum(fz, fb, qkv)
    yatt, lsec = _attn_fwd(qkv, kk, vt, ct)
    gw_out, gw1, gw2, gwpe, gwg = _xchg_wait(rest, True, yatt, "gather_rest_wait")
    wout = gw_out.reshape(D, D)
    wg = gwg.reshape(D, D)
    wpe = jnp.transpose(gwpe, (1, 0, 2)).reshape(PLE, D)
    h1, yb, o = _sgu_out(us, yatt, xt, sg_ln_g, sg_ln_b, sgw, sgbt, att_out_g, sg_out_g, wout, post_mix_g)
    c2b, f1b, ff, h2 = _ffn_fwd(h1, pre_ffn_g, gw1, gw2, post_ffn_g)
    dh2, dbg, loss_acc, g_g, g_pe = _ple_loss(h2, pt, tgt, wg, ple_gate_b, wpe)
    g_g = g_g.reshape(NDEV, D // NDEV, D)
    g_pe = jnp.transpose(g_pe.reshape(PLE, NDEV, D // NDEV), (1, 0, 2))

    dffb, df1, dh1, dgpostffn, dgpreffn = _ffn_bwd(dh2, ff, h1, f1b, gw1, gw2, post_ffn_g, pre_ffn_g)
    (g_1,) = _wgrad(c2b, [(df1, ("col", DFF // NDEV))], "wgrad_ff1")
    (g_2,) = _wgrad(f1b, [(dffb, ("row", DFF // NDEV))], "wgrad_ff2", a_fn=_sq_relu)
    early = _xchg_start([g_1, g_2, g_pe, g_g], False, "scatter_early_start")
    qw, dow, dus, g_out, dsgw, dsgbt, dlng, dlnb, dgatt, dgsg, dgpostmix = _mix_bwd(
        dh1, o, us, yatt, qkv, lsec, yb, sg_ln_g, sg_ln_b, sgw, sgwt, sgbt, att_out_g, sg_out_g, wout,
        post_mix_g + early[4][0, 0])
    mid = _xchg_start([g_out.reshape(NDEV, D // NDEV, D)], False, "scatter_mid_start")
    small_g = dict(sg_w=dsgw, sg_ln_g=dlng, sg_ln_b=dlnb, sg_b=jnp.transpose(dsgbt[:, 0:8]), att_out_g=dgatt, sg_out_g=dgsg,
                   post_mix_g=dgpostmix, pre_ffn_g=dgpreffn, post_ffn_g=dgpostffn, ple_gate_b=dbg)
    small_early = _xchg_start([_pack(small_g, False, loss_acc)], True, "gather_small_early_start")
    dqt, dcq, dk, dv, dck = _attn_bwd(kk, vk, kt, qw, dow, mid[4] + small_early[4])
    dfz, dfb = _fgate_bwd(dcq, dck, fz, fb)

    gq, gk, gv, gus, gf = _wgrad(ab, [(dqt, "t"), (dk, None), (dv, None), (dus, None), (dfz, None)], "wgrad_in",
                                 out_dtype=BF16, transposed=True)
    g_in = jnp.concatenate([gq, gk, gv, gf[0:8], gus], axis=0).reshape(NDEV, ws, D)
    late = _xchg_start([g_in], False, "scatter_late_start")
    grad_x, dgpremix = _inproj_bwd(dqt, dk, dv, dus, dfz, wtp, xt, dh1, pre_mix_g + late[4][0, 0])

    r_1, r_2, r_pe, r_g = _xchg_wait(early, False, grad_x, "scatter_early_wait")
    (r_out,) = _xchg_wait(mid, False, grad_x, "scatter_mid_wait")
    (r_early,) = _xchg_wait(small_early, True, grad_x, "gather_small_early_wait")
    r_late = _gather(_pack(dict(f_bias=dfb[:, 0:8], pre_mix_g=dgpremix), True), "gather_small_late")

    res = {}

    def adam_big(name, parts):
        w, m, v = big[name]
        res[name] = [t[None] for t in _adam(parts, w[0], m[0], v[0], "adam_" + name)]

    for name, parts in (("w_out", r_out), ("w_ff1", r_1), ("w_ff2", r_2), ("ple_w", r_pe), ("ple_gate_w", r_g)):
        adam_big(name, parts)
    (r_in,) = _xchg_wait(late, False, res["w_ff1"][0], "scatter_late_wait")
    res["w_in"] = [jnp.transpose(t)[None] for t in _adam(r_in, *[jnp.transpose(t[0]) for t in big["w_in"]], "adam_w_in")]
    loss, small = _adam_small(r_early, r_late, small_w, small_m, small_v)
    res.update(small)

    order = ["w_in", "f_bias", "sg_ln_g", "sg_ln_b", "sg_w", "sg_b", "att_out_g", "sg_out_g", "w_out", "pre_mix_g",
             "post_mix_g", "pre_ffn_g", "post_ffn_g", "w_ff1", "w_ff2", "ple_w", "ple_gate_w", "ple_gate_b"]
    outs = [loss[0, 0], grad_x[None]]
    for kind in range(4):
        outs += [res[name][kind] for name in order]
    return tuple(outs)
```

```python
import jax
import jax.numpy as jnp
from jax import lax
from jax.experimental import pallas as pl
from jax.experimental.pallas import tpu as pltpu

F32 = jnp.float32
BF16 = jnp.bfloat16

NDEV = 8
D = 1024
AW = 512
SW = 512
HD = 64
CH = 128
DFF = 4096
PLE = 256
NQKV = 3 * AW
NUS = 2 * SW
FPAD = 128
ZP = NQKV + NUS + FPAD
W_IN_COLS = 2568
WS_PAD = 336
EPS = 1e-6
MASKV = -1e30
GELU_K = 0.7978845608028654
GELU_C = 0.044715

ADAM_LR = 0.001
ADAM_B1 = 0.9
ADAM_B2 = 0.999
ADAM_EPS = 1e-08
ADAM_WD = 0.01
ADAM_STEP = 10

VMEM_LIMIT = 48 * 1024 * 1024
VMEM_LIMIT_BIG = 60 * 1024 * 1024


def _nn(a, b):
    return jnp.dot(a, b, preferred_element_type=F32)


def _nt(a, b):
    return lax.dot_general(a, b, (((1,), (1,)), ((), ())), preferred_element_type=F32)


def _tn(a, b):
    return lax.dot_general(a, b, (((0,), (0,)), ((), ())), preferred_element_type=F32)


def _tile(n, pref):
    return min(n, pref)


def _params(n_axes, vmem=VMEM_LIMIT):
    return pltpu.CompilerParams(dimension_semantics=("arbitrary",) * n_axes, vmem_limit_bytes=vmem)


def _full(shape):
    nd = len(shape)
    return pl.BlockSpec(shape, lambda *_: (0,) * nd)


def _rms_fwd(x, g):
    r = lax.rsqrt(jnp.mean(x * x, axis=-1, keepdims=True) + EPS)
    return x * r * g


def _rms_bwd(dy, x, g):
    n = x.shape[-1]
    r = lax.rsqrt(jnp.mean(x * x, axis=-1, keepdims=True) + EPS)
    u = dy * g
    s = jnp.sum(x * u, axis=-1, keepdims=True)
    dx = r * u - x * (r * r * r * (s * (1.0 / n)))
    dg = jnp.sum(dy * (x * r), axis=0, keepdims=True)
    return dx, dg


def _gelu(x):
    t = jnp.tanh(x * (GELU_K + (GELU_K * GELU_C) * (x * x)))
    return x * (0.5 + 0.5 * t), t


def _gelu_grad(x, t):
    return (0.5 + 0.5 * t) + (0.5 * x) * (1.0 - t * t) * (GELU_K + (3.0 * GELU_K * GELU_C) * (x * x))


def _gather(arr, name):
    def body(x_ref, out_ref, send, recv, loc):
        x, y, c = lax.axis_index("x"), lax.axis_index("y"), lax.axis_index("c")
        me, sibling = (x, y, c), (x, y, 1 - c)
        chips = [(1 - x, y), (x, 1 - y), (1 - x, 1 - y)]

        def slot(px, py, pc):
            return out_ref.at[4 * px + 2 * py + pc]

        def copy(k, block, to, src=None):
            return pltpu.make_async_remote_copy(
                src_ref=slot(*block) if src is None else src, dst_ref=slot(*block), send_sem=send.at[k],
                recv_sem=recv.at[k], device_id=to, device_id_type=pl.DeviceIdType.MESH)

        mine = pltpu.make_async_copy(x_ref, slot(*me), loc)
        mine.start()
        first = [copy(0, me, sibling, src=x_ref)] + [copy(1 + j, me, (*chip, c), src=x_ref) for j, chip in enumerate(chips)]
        for cp in first:
            cp.start()
        passed = [copy(4 + j, (*chip, c), sibling) for j, chip in enumerate(chips)]
        for j, chip in enumerate(chips):
            copy(1 + j, (*chip, c), me).wait_recv()
            passed[j].start()
        copy(0, sibling, me).wait_recv()
        for j, chip in enumerate(chips):
            copy(4 + j, (*chip, 1 - c), me).wait_recv()
        for cp in first + passed:
            cp.wait_send()
        mine.wait()

    hbm = pl.BlockSpec(memory_space=pltpu.HBM)
    return pl.pallas_call(
        body,
        name=name,
        out_shape=jax.ShapeDtypeStruct((NDEV,) + arr.shape, arr.dtype),
        in_specs=[hbm],
        out_specs=hbm,
        scratch_shapes=[pltpu.SemaphoreType.DMA((NDEV - 1,)), pltpu.SemaphoreType.DMA((NDEV - 1,)), pltpu.SemaphoreType.DMA],
    )(arr)


def _peers(x, y, c):
    out = []
    for k in range(1, NDEV):
        out.append((1 - x if (k >> 2) & 1 else x, 1 - y if (k >> 1) & 1 else y, 1 - c if k & 1 else c))
    return out


def _xchg_start(arrs, gather, name):
    n = len(arrs)
    me = 4 * lax.axis_index("x") + 2 * lax.axis_index("y") + lax.axis_index("c")
    lands = []
    for a in arrs:
        shape = ((NDEV,) + a.shape) if gather else a.shape
        own = a[None] if gather else lax.dynamic_slice_in_dim(a, me, 1, axis=0)
        lands.append(lax.dynamic_update_slice_in_dim(lax.empty(shape, a.dtype), own, me, axis=0))

    def body(*refs):
        ins, lnd = refs[:n], refs[n:2 * n]
        send, recv, token = refs[2 * n:3 * n], refs[3 * n:4 * n], refs[-1]
        x, y, c = lax.axis_index("x"), lax.axis_index("y"), lax.axis_index("c")
        mine = 4 * x + 2 * y + c
        for px, py, pc in _peers(x, y, c):
            peer = 4 * px + 2 * py + pc
            for a in range(n):
                pltpu.make_async_remote_copy(
                    src_ref=ins[a] if gather else ins[a].at[peer],
                    dst_ref=lnd[a].at[mine],
                    send_sem=send[a],
                    recv_sem=recv[a],
                    device_id=(px, py, pc),
                    device_id_type=pl.DeviceIdType.MESH,
                ).start()
        token[...] = jnp.zeros_like(token)

    hbm = pl.BlockSpec(memory_space=pltpu.HBM)
    sem = pl.BlockSpec(memory_space=pltpu.SEMAPHORE)
    res = pl.pallas_call(
        body,
        name=name,
        out_shape=(*[pltpu.SemaphoreType.DMA(())] * (2 * n),
                   *[pltpu.HBM(a.shape, a.dtype) for a in arrs], *[pltpu.HBM(l.shape, l.dtype) for l in lands],
                   jax.ShapeDtypeStruct((8, 128), F32)),
        in_specs=[hbm] * (2 * n),
        out_specs=(*([sem] * (2 * n)), *([hbm] * (2 * n)), pl.BlockSpec(memory_space=pltpu.VMEM)),
        input_output_aliases={i: 2 * n + i for i in range(2 * n)},
        compiler_params=pltpu.CompilerParams(has_side_effects=pltpu.SideEffectType.DATAFLOW_SIDE_EFFECTING),
    )(*[pltpu.with_memory_space_constraint(a, pltpu.HBM) for a in arrs],
      *[pltpu.with_memory_space_constraint(l, pltpu.HBM) for l in lands])
    return list(res[0:n]), list(res[n:2 * n]), list(res[2 * n:3 * n]), list(res[3 * n:4 * n]), res[-1]


def _xchg_wait(started, gather, after, name):
    send, recv, srcs, lands, _ = started
    n = len(srcs)

    def body(*refs):
        lnd = refs[n:2 * n]
        send, recv = refs[2 * n:3 * n], refs[3 * n:4 * n]
        me = (lax.axis_index("x"), lax.axis_index("y"), lax.axis_index("c"))
        for a in range(n):
            seven = lnd[a].at[pl.ds(0, NDEV - 1)]
            cp = pltpu.make_async_remote_copy(src_ref=seven, dst_ref=seven, send_sem=send[a], recv_sem=recv[a],
                                              device_id=me, device_id_type=pl.DeviceIdType.MESH)
            cp.wait_send()
            cp.wait_recv()

    hbm = pl.BlockSpec(memory_space=pltpu.HBM)
    sem = pl.BlockSpec(memory_space=pltpu.SEMAPHORE)
    res = pl.pallas_call(
        body,
        name=name,
        out_shape=tuple([pltpu.HBM(a.shape, a.dtype) for a in srcs] + [pltpu.HBM(l.shape, l.dtype) for l in lands]),
        in_specs=[hbm] * (2 * n) + [sem] * (2 * n) + [pl.BlockSpec(memory_space=pl.ANY)],
        out_specs=tuple([hbm] * (2 * n)),
        input_output_aliases={i: i for i in range(2 * n)},
        compiler_params=pltpu.CompilerParams(has_side_effects=pltpu.SideEffectType.DATAFLOW_SIDE_EFFECTING),
    )(*srcs, *lands, *send, *recv, after)
    return list(res[n:])


def _inproj(x, g, wtp):
    T = x.shape[0]
    tm = _tile(T, 512)

    def body(x_ref, g_ref, w_ref, qkv_ref, kt_ref, vt_ref, us_ref, fz_ref, ab_ref):
        a = _rms_fwd(x_ref[...], g_ref[...]).astype(BF16)
        ab_ref[...] = a
        qkv_ref[:, 0:AW] = _nt(a, w_ref[0:AW, :]).astype(BF16)
        kk = _nt(a, w_ref[AW:2 * AW, :])
        qkv_ref[:, AW:2 * AW] = kk.astype(BF16)
        kt_ref[...] = kk.T.astype(BF16)
        vv = _nt(a, w_ref[2 * AW:NQKV, :])
        qkv_ref[:, 2 * AW:NQKV] = vv.astype(BF16)
        vvt = vv.T.astype(BF16)
        one_row = jnp.where(lax.broadcasted_iota(jnp.int32, (HD, tm), 0) == 0, 1.0, 0.0).astype(BF16)
        for h in range(AW // HD):
            vt_ref[2 * h * HD:(2 * h + 1) * HD, :] = vvt[h * HD:(h + 1) * HD, :]
            vt_ref[(2 * h + 1) * HD:(2 * h + 2) * HD, :] = one_row
        us_ref[...] = _nt(a, w_ref[NQKV:NQKV + NUS, :])
        fz_ref[...] = _nt(a, w_ref[NQKV + NUS:ZP, :])

    row = lambda n: pl.BlockSpec((tm, n), lambda i: (i, 0))
    col = lambda n: pl.BlockSpec((n, tm), lambda i: (0, i))
    return pl.pallas_call(
        body,
        name="inproj",
        grid=(T // tm,),
        in_specs=[row(D), _full((1, D)), _full((ZP, D))],
        out_specs=[row(NQKV), col(AW), col(2 * AW), row(NUS), row(FPAD), row(D)],
        out_shape=[
            jax.ShapeDtypeStruct((T, NQKV), BF16),
            jax.ShapeDtypeStruct((AW, T), BF16),
            jax.ShapeDtypeStruct((2 * AW, T), BF16),
            jax.ShapeDtypeStruct((T, NUS), F32),
            jax.ShapeDtypeStruct((T, FPAD), F32),
            jax.ShapeDtypeStruct((T, D), BF16),
        ],
        compiler_params=_params(1),
    )(x, g, wtp)


def _log_sigmoid(z):
    return jnp.minimum(z, 0.0) - jnp.log1p(jnp.exp(-jnp.abs(z)))


def _split3(x):
    hi = x.astype(BF16)
    r1 = x - hi.astype(F32)
    mid = r1.astype(BF16)
    lo = (r1 - mid.astype(F32)).astype(BF16)
    return hi, mid, lo


AUG_A, AUG_B, AUG_ONE = 0, 3, 6


def _aug_lanes(rows, first_one, pieces):
    lane = lax.broadcasted_iota(jnp.int32, (rows, 128), 1)
    out = jnp.zeros((rows, 128), F32)
    if first_one is not None:
        out = jnp.where((lane >= first_one) & (lane < first_one + 3), 1.0, out)
    for n, piece in enumerate(pieces):
        out = jnp.where(lane == AUG_ONE + n, piece.astype(F32), out)
    return out.astype(BF16)


def _fcum(fz, fb, qkv):
    T = fz.shape[0]
    tb = _tile(T, 512)

    def body(fz_ref, fb_ref, k_ref, v_ref, ct_ref, kk_ref, vk_ref, carry, tri_s, sel_s):
        @pl.when(pl.program_id(0) == 0)
        def _():
            carry[...] = jnp.zeros_like(carry)
            src = lax.broadcasted_iota(jnp.int32, (128, 128), 0)
            dst = lax.broadcasted_iota(jnp.int32, (128, 128), 1)
            tri_s[...] = (dst <= src).astype(F32)
            for hp in range(4):
                for n in range(3):
                    pick = ((src == 2 * hp) & (dst == AUG_A + n)) | ((src == 2 * hp + 1) & (dst == AUG_B + n))
                    sel_s[3 * hp + n] = jnp.where(pick, -1.0, 0.0).astype(BF16)

        lf = _log_sigmoid(fz_ref[...] + fb_ref[...])
        run = carry[...]
        parts = []
        for blk in range(tb // 128):
            lf_b = lf[blk * 128:(blk + 1) * 128, :]
            parts.append(jnp.dot(tri_s[...], lf_b, precision=lax.Precision.HIGHEST, preferred_element_type=F32) + run)
            run = run + jnp.sum(lf_b, axis=0, keepdims=True)
        carry[...] = run
        cs = jnp.concatenate(parts, axis=0)
        ct_ref[...] = cs.T[0:8, :]

        pieces = _split3(cs)
        lane = lax.broadcasted_iota(jnp.int32, (1, 128), 1)
        ones = jnp.where((lane >= AUG_ONE) & (lane < AUG_ONE + 3), 1.0, 0.0)
        for hp in range(4):
            aug = jnp.zeros((tb, 128), F32) + ones
            for n, piece in enumerate(pieces):
                aug = aug + _nn(piece, sel_s[3 * hp + n])
            aug = aug.astype(BF16)
            kk_ref[:, hp * 256:hp * 256 + 128] = k_ref[:, hp * 128:(hp + 1) * 128]
            kk_ref[:, hp * 256 + 128:(hp + 1) * 256] = aug
            vk_ref[:, hp * 256:hp * 256 + 128] = v_ref[:, hp * 128:(hp + 1) * 128]
            vk_ref[:, hp * 256 + 128:(hp + 1) * 256] = aug

    row = lambda n, c: pl.BlockSpec((tb, n), lambda i: (i, c))
    return pl.pallas_call(
        body,
        name="fcum",
        grid=(T // tb,),
        in_specs=[row(FPAD, 0), _full((1, FPAD)), row(AW, 1), row(AW, 2)],
        out_specs=[pl.BlockSpec((8, tb), lambda i: (0, i)), row(2 * AW, 0), row(2 * AW, 0)],
        out_shape=[jax.ShapeDtypeStruct((8, T), F32), jax.ShapeDtypeStruct((T, 2 * AW), BF16),
                   jax.ShapeDtypeStruct((T, 2 * AW), BF16)],
        scratch_shapes=[pltpu.VMEM((1, FPAD), F32), pltpu.VMEM((128, 128), F32), pltpu.VMEM((12, 128, 128), BF16)],
        compiler_params=_params(1),
    )(fz, fb, qkv, qkv)


def _fold_rows(r, t, nq):
    if nq == 1:
        return r, t
    low = t <= r
    return jnp.where(low, r, nq - 1 - r), jnp.where(low, t, t - r - 1)


def _fold_cols(r, t, nq):
    if nq == 1:
        return r, t
    first = t < nq - r
    j = jnp.where(first, r, nq - 1 - r)
    return j, jnp.where(first, r + t, nq - 1 - r + (t - (nq - r)))


PAIRS = 4


def _fold_grid(nq):
    assert (nq == 1 or nq % 2 == 0) and PAIRS == 4
    return (4 // PAIRS, 1, 1) if nq == 1 else (4 // PAIRS, nq // 2, nq + 1)


def _head_rows(x2, hh, scale):
    is_a = lax.broadcasted_iota(jnp.int32, (1, 128), 1) < HD
    keep = is_a if hh == 0 else jnp.logical_not(is_a)
    return jnp.where(keep, x2, jnp.zeros_like(x2)) * scale


def _attn_fwd(qkv, kk, vt, ct):
    T = qkv.shape[0]
    tq = _tile(T, 512)
    tk = tq
    nq = T // tq

    def body(q_ref, kk_ref, vt_ref, ctq_ref, o_ref, lsec_ref, qw_s, m_s, acc_s):
        i, j = _fold_rows(pl.program_id(1), pl.program_id(2), nq)
        sub8 = lax.broadcasted_iota(jnp.int32, (8, 1), 0)

        def cref_of(pp, hh):
            head = 2 * (PAIRS * pl.program_id(0) + pp) + hh
            return jnp.sum(jnp.where(sub8 == head, ctq_ref[:, 0:1], 0.0), axis=0, keepdims=True)

        @pl.when(j == 0)
        def _():
            for pp in range(PAIRS):
                q2 = q_ref[:, pp * 128:(pp + 1) * 128]
                for hh in range(2):
                    rows = slice(hh * tq, (hh + 1) * tq)
                    qw_s[pp, rows, 0:128] = _head_rows(q2, hh, 0.125)
                    qw_s[pp, rows, 128:256] = _aug_lanes(tq, AUG_A if hh == 0 else AUG_B, _split3(cref_of(pp, hh)))
            m_s[...] = jnp.full_like(m_s, MASKV)
            acc_s[...] = jnp.zeros_like(acc_s)

        def step(masked):
            if masked:
                causal = lax.broadcasted_iota(jnp.int32, (tk, tq), 0) <= lax.broadcasted_iota(jnp.int32, (tk, tq), 1)
            logits = lambda pp: _nt(kk_ref[:, pp * 256:(pp + 1) * 256], qw_s[pp])
            s_next = logits(0)
            for pp in range(PAIRS):
                s2 = s_next
                if pp + 1 < PAIRS:
                    s_next = logits(pp + 1)
                for hh in range(2):
                    n = 2 * pp + hh
                    s = s2[:, hh * tq:(hh + 1) * tq]
                    if masked:
                        s = jnp.where(causal, s, MASKV)
                    m_prev = m_s[n]
                    m_new = jnp.maximum(m_prev, jnp.max(s, axis=0, keepdims=True))
                    pr = jnp.exp(s - m_new)
                    m_s[n] = m_new
                    acc_s[n] = jnp.exp(m_prev - m_new) * acc_s[n] + _nn(vt_ref[n * 128:(n + 1) * 128, :], pr.astype(BF16))

        @pl.when(j < i)
        def _():
            step(False)

        @pl.when(j == i)
        def _():
            step(True)
            sub = lax.broadcasted_iota(jnp.int32, (128, 1), 0)
            for pp in range(PAIRS):
                outs, lrow = [], []
                for hh in range(2):
                    n = 2 * pp + hh
                    den = acc_s[n, HD:HD + 1, :]
                    outs.append(acc_s[n, 0:HD, :] * (1.0 / den))
                    lrow.append(m_s[n] + jnp.log(den) - cref_of(pp, hh))
                o_ref[:, pp * 128:(pp + 1) * 128] = jnp.concatenate(outs, axis=0).T
                lsec_ref[:, pp * 128:(pp + 1) * 128] = jnp.where(sub == 0, lrow[0], jnp.where(sub == 1, lrow[1], 0.0)).T

    qi = lambda r, t: _fold_rows(r, t, nq)[0]
    kj = lambda r, t: _fold_rows(r, t, nq)[1]
    return pl.pallas_call(
        body,
        name="attn_fwd",
        grid=_fold_grid(nq),
        in_specs=[
            pl.BlockSpec((tq, PAIRS * 128), lambda g, r, t: (qi(r, t), g)),
            pl.BlockSpec((tk, PAIRS * 256), lambda g, r, t: (kj(r, t), g)),
            pl.BlockSpec((PAIRS * 256, tk), lambda g, r, t: (g, kj(r, t))),
            pl.BlockSpec((8, tq), lambda g, r, t: (0, qi(r, t))),
        ],
        out_specs=[pl.BlockSpec((tq, PAIRS * 128), lambda g, r, t: (qi(r, t), g))] * 2,
        out_shape=[jax.ShapeDtypeStruct((T, AW), F32)] * 2,
        scratch_shapes=[pltpu.VMEM((PAIRS, 2 * tq, 256), BF16), pltpu.VMEM((2 * PAIRS, 1, tq), F32),
                        pltpu.VMEM((2 * PAIRS, 128, tq), F32)],
        compiler_params=_params(3),
    )(qkv, kk, vt, ct)


def _sgu_forward(us_ref, lng, lnb, w_ref, bt_ref, mixed_s, vnb_s, tm):
    is_a = lax.broadcasted_iota(jnp.int32, (1, 128), 1) < HD
    u = us_ref[:, 0:SW]
    vs = us_ref[:, SW:NUS]
    ug, tu = _gelu(u)
    vg, tv = _gelu(vs)
    mu = jnp.mean(vg, axis=-1, keepdims=True)
    xc = vg - mu
    rstd = lax.rsqrt(jnp.mean(xc * xc, axis=-1, keepdims=True) + EPS)
    vhat = xc * rstd
    vnb_s[...] = (vhat * lng + lnb).astype(BF16)
    rr = lax.broadcasted_iota(jnp.int32, (CH, CH), 0)
    cc = lax.broadcasted_iota(jnp.int32, (CH, CH), 1)
    tril = cc <= rr
    for jj in range(4):
        wa = jnp.where(tril, w_ref[2 * jj], 0.0).astype(BF16)
        wb = jnp.where(tril, w_ref[2 * jj + 1], 0.0).astype(BF16)
        ba = bt_ref[:, 2 * jj:2 * jj + 1]
        bb = bt_ref[:, 2 * jj + 1:2 * jj + 2]
        for ch in range(tm // CH):
            rs, cs = slice(ch * CH, (ch + 1) * CH), slice(jj * 128, (jj + 1) * 128)
            vn2 = vnb_s[rs, cs]
            mixed_s[rs, cs] = jnp.where(is_a, _nn(wa, vn2) + ba, _nn(wb, vn2) + bb)
    mixed = mixed_s[...]
    return u, vs, ug, tu, tv, vhat, rstd, mixed, ug * mixed


def _sgu_out(us, yatt, x, lng, lnb, sgw, sgbt, gatt, gsg, wout, gpm):
    T = us.shape[0]
    tm = _tile(T, 512)

    def body(us_ref, ya_ref, x_ref, lng_ref, lnb_ref, w_ref, bt_ref, ga_ref, gs_ref, wo_ref, gp_ref,
             h1_ref, yb_ref, o_ref, mixed_s, vnb_s):
        ysg = _sgu_forward(us_ref, lng_ref[...], lnb_ref[...], w_ref, bt_ref, mixed_s, vnb_s, tm)[-1]
        yb_ref[:, 0:AW] = _rms_fwd(ya_ref[...], ga_ref[...]).astype(BF16)
        yb_ref[:, AW:D] = _rms_fwd(ysg, gs_ref[...]).astype(BF16)
        o = _nn(yb_ref[...], wo_ref[...])
        o_ref[...] = o
        h1_ref[...] = x_ref[...] + _rms_fwd(o, gp_ref[...])

    row = lambda n: pl.BlockSpec((tm, n), lambda i: (i, 0))
    return pl.pallas_call(
        body,
        name="sgu_out",
        grid=(T // tm,),
        in_specs=[row(NUS), row(AW), row(D), _full((1, SW)), _full((1, SW)), _full((8, CH, CH)), _full((CH, 8)),
                  _full((1, AW)), _full((1, SW)), _full((D, D)), _full((1, D))],
        out_specs=[row(D), row(D), row(D)],
        out_shape=[jax.ShapeDtypeStruct((T, D), F32), jax.ShapeDtypeStruct((T, D), BF16), jax.ShapeDtypeStruct((T, D), F32)],
        scratch_shapes=[pltpu.VMEM((tm, SW), F32), pltpu.VMEM((tm, SW), BF16)],
        compiler_params=_params(1),
    )(us, yatt, x, lng, lnb, sgw, sgbt, gatt, gsg, wout, gpm)


def _ffn_fwd(h1, gpre, w1g, w2g, gpost):
    T = h1.shape[0]
    tm = _tile(T, 512)
    nb, hb = w1g.shape[0], w1g.shape[2]

    def body(h1_ref, gpre_ref, w1_ref, w2_ref, gpost_ref, c2_ref, f1_ref, ff_ref, h2_ref):
        h1 = h1_ref[...]
        c2 = _rms_fwd(h1, gpre_ref[...]).astype(BF16)
        c2_ref[...] = c2
        ff = jnp.zeros((tm, D), F32)
        for j in range(nb):
            f1 = _nn(c2, w1_ref[j])
            f1_ref[:, j * hb:(j + 1) * hb] = f1.astype(BF16)
            r = jnp.maximum(f1, 0.0)
            ff = ff + _nn((r * r).astype(BF16), w2_ref[j])
        ff_ref[...] = ff
        h2_ref[...] = h1 + _rms_fwd(ff, gpost_ref[...])

    row = lambda n: pl.BlockSpec((tm, n), lambda i: (i, 0))
    once = lambda shape: pl.BlockSpec(shape, lambda i: (0,) * len(shape), pipeline_mode=pl.Buffered(1))
    return pl.pallas_call(
        body,
        name="ffn_fwd",
        grid=(T // tm,),
        in_specs=[row(D), _full((1, D)), once((nb, D, hb)), once((nb, hb, D)), _full((1, D))],
        out_specs=[row(D), row(DFF), row(D), row(D)],
        out_shape=[jax.ShapeDtypeStruct((T, D), BF16), jax.ShapeDtypeStruct((T, DFF), BF16),
                   jax.ShapeDtypeStruct((T, D), F32), jax.ShapeDtypeStruct((T, D), F32)],
        compiler_params=_params(1, VMEM_LIMIT_BIG),
    )(h1, gpre, w1g, w2g, gpost)


def _ple_loss(h2, p, tgt, wg, bg, wpe):
    T = h2.shape[0]
    tm = _tile(T, 512)

    def body(h2_ref, p_ref, t_ref, wg_ref, bg_ref, wpe_ref, dh2_ref, dbg_ref, loss_ref, dwg_ref, dwpe_ref):
        @pl.when(pl.program_id(0) == 0)
        def _():
            for r in (dbg_ref, loss_ref, dwg_ref, dwpe_ref):
                r[...] = jnp.zeros_like(r)

        h2 = h2_ref[...]
        h2b = h2.astype(BF16)
        gate = jax.nn.sigmoid(_nn(h2b, wg_ref[...]) + bg_ref[...])
        pb = p_ref[...].astype(BF16)
        pe = _nn(pb, wpe_ref[...])
        diff = (h2 + gate * pe) - t_ref[...]
        loss_ref[...] += jnp.sum(diff * diff)
        dh3 = diff * (1.0 / D)
        dpre = (dh3 * pe) * (gate * (1.0 - gate))
        dpre_b = dpre.astype(BF16)
        dbg_ref[...] += jnp.sum(dpre, axis=0, keepdims=True)
        dh2_ref[...] = dh3 + _nt(dpre_b, wg_ref[...])
        dwg_ref[...] += _nn(h2b.T, dpre_b)
        dwpe_ref[...] += _nn(pb.T, (dh3 * gate).astype(BF16))

    row = lambda n: pl.BlockSpec((tm, n), lambda i: (i, 0))
    once = lambda shape: pl.BlockSpec(shape, lambda i: (0,) * len(shape), pipeline_mode=pl.Buffered(1))
    return pl.pallas_call(
        body,
        name="ple_loss",
        grid=(T // tm,),
        in_specs=[row(D), row(PLE), row(D), _full((D, D)), _full((1, D)), _full((PLE, D))],
        out_specs=[row(D), _full((1, D)), _full((8, 128)), once((D, D)), once((PLE, D))],
        out_shape=[jax.ShapeDtypeStruct((T, D), F32), jax.ShapeDtypeStruct((1, D), F32),
                   jax.ShapeDtypeStruct((8, 128), F32), jax.ShapeDtypeStruct((D, D), F32),
                   jax.ShapeDtypeStruct((PLE, D), F32)],
        compiler_params=_params(1),
    )(h2, p, tgt, wg, bg, wpe)


def _ffn_bwd(dh2, ff, h1, f1, w1g, w2g, gpost, gpre):
    T = dh2.shape[0]
    tm = _tile(T, 512)
    nb, hb = w1g.shape[0], w1g.shape[2]

    def body(dh2_ref, ff_ref, h1_ref, f1_ref, w1_ref, w2_ref, gpost_ref, gpre_ref,
             dffb_ref, df1_ref, dh1_ref, dgpost_ref, dgpre_ref):
        @pl.when(pl.program_id(0) == 0)
        def _():
            dgpost_ref[...] = jnp.zeros_like(dgpost_ref)
            dgpre_ref[...] = jnp.zeros_like(dgpre_ref)

        dh2 = dh2_ref[...]
        dff, dg = _rms_bwd(dh2, ff_ref[...], gpost_ref[...])
        dffb = dff.astype(BF16)
        dffb_ref[...] = dffb
        dgpost_ref[...] += dg
        dc2 = jnp.zeros((tm, D), F32)
        for j in range(nb):
            cols = slice(j * hb, (j + 1) * hb)
            dact = _nt(dffb, w2_ref[j])
            df1 = (dact * (2.0 * jnp.maximum(f1_ref[:, cols].astype(F32), 0.0))).astype(BF16)
            df1_ref[:, cols] = df1
            dc2 = dc2 + _nt(df1, w1_ref[j])
        dx, dg = _rms_bwd(dc2, h1_ref[...], gpre_ref[...])
        dh1_ref[...] = dh2 + dx
        dgpre_ref[...] += dg

    row = lambda n: pl.BlockSpec((tm, n), lambda i: (i, 0))
    once = lambda shape: pl.BlockSpec(shape, lambda i: (0,) * len(shape), pipeline_mode=pl.Buffered(1))
    return pl.pallas_call(
        body,
        name="ffn_bwd",
        grid=(T // tm,),
        in_specs=[row(D), row(D), row(D), row(DFF), once((nb, D, hb)), once((nb, hb, D)), _full((1, D)), _full((1, D))],
        out_specs=[row(D), row(DFF), row(D), _full((1, D)), _full((1, D))],
        out_shape=[jax.ShapeDtypeStruct((T, D), BF16), jax.ShapeDtypeStruct((T, DFF), BF16),
                   jax.ShapeDtypeStruct((T, D), F32), jax.ShapeDtypeStruct((1, D), F32),
                   jax.ShapeDtypeStruct((1, D), F32)],
        compiler_params=_params(1, VMEM_LIMIT_BIG),
    )(dh2, ff, h1, f1, w1g, w2g, gpost, gpre)


def _mix_bwd(dh1, o, us, yatt, qkv, lsec, yb, lng, lnb, sgw, sgwt, sgbt, gatt, gsg, wout, gpm):
    T = dh1.shape[0]
    tm = _tile(T, 512)

    def body(dh1_ref, o_ref, us_ref, ya_ref, q_ref, l_ref, yb_ref, lng_ref, lnb_ref, w_ref, wt_ref, bt_ref, ga_ref, gs_ref,
             wo_ref, gp_ref, qw_ref, dow_ref, dus_ref, dwo_ref, dw_ref, dbt_ref, dlng_ref, dlnb_ref, dga_ref, dgs_ref,
             dgp_ref, mixed_s, vnb_s, dvn_s):
        @pl.when(pl.program_id(0) == 0)
        def _():
            for r in (dwo_ref, dw_ref, dbt_ref, dlng_ref, dlnb_ref, dga_ref, dgs_ref, dgp_ref):
                r[...] = jnp.zeros_like(r)

        is_a = lax.broadcasted_iota(jnp.int32, (1, 128), 1) < HD
        lane = lax.broadcasted_iota(jnp.int32, (1, 128), 1)
        do, dg = _rms_bwd(dh1_ref[...], o_ref[...], gp_ref[...])
        dgp_ref[...] += dg
        dob = do.astype(BF16)
        dwo_ref[...] += _nn(yb_ref[...].T, dob)
        dy = _nt(dob, wo_ref[...])
        ya = ya_ref[...]
        datt, dg = _rms_bwd(dy[:, 0:AW], ya, ga_ref[...])
        dga_ref[...] += dg
        _attn_operands(q_ref[...], datt, ya, l_ref[...], qw_ref, dow_ref)

        lng = lng_ref[...]
        u, vs, ug, tu, tv, vhat, rstd, mixed, ysg = _sgu_forward(us_ref, lng, lnb_ref[...], w_ref, bt_ref, mixed_s, vnb_s, tm)
        dysg, dg = _rms_bwd(dy[:, AW:D], ysg, gs_ref[...])
        dgs_ref[...] += dg
        dus_ref[:, 0:SW] = ((dysg * mixed) * _gelu_grad(u, tu)).astype(BF16)
        dmix = dysg * ug

        rr = lax.broadcasted_iota(jnp.int32, (CH, CH), 0)
        cc = lax.broadcasted_iota(jnp.int32, (CH, CH), 1)
        tril = cc <= rr
        triu = cc >= rr
        for jj in range(4):
            wta = jnp.where(triu, wt_ref[2 * jj], 0.0).astype(BF16)
            wtb = jnp.where(triu, wt_ref[2 * jj + 1], 0.0).astype(BF16)
            for ch in range(tm // CH):
                rs, cs = slice(ch * CH, (ch + 1) * CH), slice(jj * 128, (jj + 1) * 128)
                dm2 = dmix[rs, cs]
                dma = jnp.where(is_a, dm2, 0.0)
                dmb = jnp.where(is_a, 0.0, dm2)
                dma_b, dmb_b = dma.astype(BF16), dmb.astype(BF16)
                vn2 = vnb_s[rs, cs]
                dw_ref[2 * jj] += jnp.where(tril, _nt(dma_b, vn2), 0.0)
                dw_ref[2 * jj + 1] += jnp.where(tril, _nt(dmb_b, vn2), 0.0)
                dvn_s[rs, cs] = _nn(wta, dma_b) + _nn(wtb, dmb_b)
                dba = jnp.sum(dma, axis=1, keepdims=True)
                dbb = jnp.sum(dmb, axis=1, keepdims=True)
                dbt_ref[...] += jnp.where(lane == 2 * jj, dba, 0.0) + jnp.where(lane == 2 * jj + 1, dbb, 0.0)

        dvn = dvn_s[...]
        dlng_ref[...] += jnp.sum(dvn * vhat, axis=0, keepdims=True)
        dlnb_ref[...] += jnp.sum(dvn, axis=0, keepdims=True)
        dvh = dvn * lng
        dvg = rstd * (dvh - jnp.mean(dvh, axis=-1, keepdims=True) - vhat * jnp.mean(dvh * vhat, axis=-1, keepdims=True))
        dus_ref[:, SW:NUS] = (dvg * _gelu_grad(vs, tv)).astype(BF16)

    row = lambda n: pl.BlockSpec((tm, n), lambda i: (i, 0))
    return pl.pallas_call(
        body,
        name="mix_bwd",
        grid=(T // tm,),
        in_specs=[row(D), row(D), row(NUS), row(AW), row(AW), row(AW), row(D), _full((1, SW)), _full((1, SW)),
                  _full((8, CH, CH)), _full((8, CH, CH)), _full((CH, 8)), _full((1, AW)), _full((1, SW)), _full((D, D)),
                  _full((1, D))],
        out_specs=[row(4 * AW), row(4 * AW), row(NUS),
                   pl.BlockSpec((D, D), lambda i: (0, 0), pipeline_mode=pl.Buffered(1)), _full((8, CH, CH)),
                   _full((CH, 128)), _full((1, SW)), _full((1, SW)), _full((1, AW)), _full((1, SW)), _full((1, D))],
        out_shape=[jax.ShapeDtypeStruct((T, 4 * AW), BF16),
                   jax.ShapeDtypeStruct((T, 4 * AW), BF16), jax.ShapeDtypeStruct((T, NUS), BF16),
                   jax.ShapeDtypeStruct((D, D), F32),
                   jax.ShapeDtypeStruct((8, CH, CH), F32), jax.ShapeDtypeStruct((CH, 128), F32),
                   jax.ShapeDtypeStruct((1, SW), F32), jax.ShapeDtypeStruct((1, SW), F32),
                   jax.ShapeDtypeStruct((1, AW), F32), jax.ShapeDtypeStruct((1, SW), F32), jax.ShapeDtypeStruct((1, D), F32)],
        scratch_shapes=[pltpu.VMEM((tm, SW), F32), pltpu.VMEM((tm, SW), BF16), pltpu.VMEM((tm, SW), F32)],
        compiler_params=_params(1, VMEM_LIMIT_BIG),
    )(dh1, o, us, yatt, qkv, lsec, yb, lng, lnb, sgw, sgwt, sgbt, gatt, gsg, wout, gpm)


def _attn_operands(q, do, o, lsec, qw_ref, dow_ref):
    rows = do.shape[0]
    feat = lax.broadcasted_iota(jnp.int32, (AW, 128), 0)
    head = lax.broadcasted_iota(jnp.int32, (AW, 128), 1)
    sel = jnp.where((feat >= head * HD) & (feat < (head + 1) * HD), 1.0, 0.0)
    delta = jnp.dot(do * o, sel, precision=lax.Precision.HIGHEST, preferred_element_type=F32)
    for hp in range(4):
        cols = slice(hp * 128, (hp + 1) * 128)
        for hh in range(2):
            base = (2 * hp + hh) * 256
            lc = lsec[:, hp * 128 + hh:hp * 128 + hh + 1]
            d_h = delta[:, 2 * hp + hh:2 * hp + hh + 1]
            qw_ref[:, base:base + 128] = _head_rows(q[:, cols], hh, 0.125)
            qw_ref[:, base + 128:base + 256] = _aug_lanes(rows, AUG_A if hh == 0 else AUG_B, _split3(-lc))
            dow_ref[:, base:base + 128] = _head_rows(do[:, cols], hh, 1.0).astype(BF16)
            dow_ref[:, base + 128:base + 256] = _aug_lanes(rows, None, _split3(-d_h))


def _attn_bwd(kk, vk, kt, qw, dow, after):
    T = kk.shape[0]
    tq = _tile(T, 512)
    tk = tq
    nq = T // tq

    def body(kk_ref, vk_ref, kt_ref, qw_ref, dow_ref, after_ref, dqt_ref, dcq_ref, dk_ref, dv_ref, dck_ref, dk_s, dv_s, dck_s):
        j, i = _fold_cols(pl.program_id(1), pl.program_id(2), nq)
        sub8 = lax.broadcasted_iota(jnp.int32, (8, 1), 0)
        lane = lax.broadcasted_iota(jnp.int32, (1, 128), 1)

        @pl.when((pl.program_id(1) == 0) & (pl.program_id(2) == 0))
        def _():
            dqt_ref[...] = jnp.zeros_like(dqt_ref)
            dcq_ref[...] = jnp.zeros_like(dcq_ref)

        @pl.when(i == j)
        def _():
            dk_s[...] = jnp.zeros_like(dk_s)
            dv_s[...] = jnp.zeros_like(dv_s)
            dck_s[...] = jnp.zeros_like(dck_s)

        def step(masked):
            cols = pl.ds(pl.multiple_of(i * tq, tq), tq)
            sub = lax.broadcasted_iota(jnp.int32, (128, 1), 0)
            if masked:
                causal = lax.broadcasted_iota(jnp.int32, (tk, tq), 0) <= lax.broadcasted_iota(jnp.int32, (tk, tq), 1)

            def logits(n):
                pair, base = n // 2, n * 256
                return (_nt(kk_ref[:, pair * 256:(pair + 1) * 256], qw_ref[:, base:base + 256]),
                        _nt(vk_ref[:, pair * 256:(pair + 1) * 256], dow_ref[:, base:base + 256]))

            ahead = logits(0)
            dcq = jnp.zeros((8, tq), F32)
            dck = jnp.zeros((tk, 128), F32)
            for pp in range(PAIRS):
                lanes = slice(pp * 128, (pp + 1) * 128)
                kt2 = kt_ref[lanes, :] * 0.125
                dv = jnp.zeros((tk, 128), F32)
                dk = jnp.zeros((tk, 128), F32)
                dqts = []
                for hh in range(2):
                    head = 2 * pp + hh
                    base = head * 256
                    qw_h = qw_ref[:, base:base + 256]
                    dow_h = dow_ref[:, base:base + 256]
                    logp, dp = ahead
                    if head + 1 < 2 * PAIRS:
                        ahead = logits(head + 1)
                    pr = jnp.exp(logp)
                    if masked:
                        pr = jnp.where(causal, pr, 0.0)
                    ds = pr * dp
                    ds_b = ds.astype(BF16)
                    dv = dv + _nn(pr.astype(BF16), dow_h[:, 0:128])
                    dk = dk + _nn(ds_b, qw_h[:, 0:128])
                    dqts.append(_nn(kt2, ds_b))
                    dcq = dcq + jnp.where(sub8 == head, jnp.sum(ds, axis=0, keepdims=True), 0.0)
                    dck = dck - jnp.where(lane == head, jnp.sum(ds, axis=1, keepdims=True), 0.0)
                dv_s[:, lanes] += dv
                dk_s[:, lanes] += dk
                dqt_ref[lanes, cols] += jnp.where(sub < HD, dqts[0], dqts[1])
            dck_s[...] += dck
            dcq_ref[:, cols] += dcq

        @pl.when(i > j)
        def _():
            step(False)

        @pl.when(i == j)
        def _():
            step(True)

        @pl.when(i == nq - 1)
        def _():
            dk_ref[...] = dk_s[...].astype(BF16)
            dv_ref[...] = dv_s[...].astype(BF16)
            dck_ref[...] = dck_s[...]

    kj = lambda r, t: _fold_cols(r, t, nq)[0]
    qi = lambda r, t: _fold_cols(r, t, nq)[1]
    krow = lambda g, r, t: (kj(r, t), g)
    qrow = lambda g, r, t: (qi(r, t), g)
    return pl.pallas_call(
        body,
        name="attn_bwd",
        grid=_fold_grid(nq),
        in_specs=[
            pl.BlockSpec((tk, PAIRS * 256), krow),
            pl.BlockSpec((tk, PAIRS * 256), krow),
            pl.BlockSpec((PAIRS * 128, tk), lambda g, r, t: (g, kj(r, t))),
            pl.BlockSpec((tq, PAIRS * 512), qrow),
            pl.BlockSpec((tq, PAIRS * 512), qrow),
            pl.BlockSpec(after.shape, lambda g, r, t: (0,) * after.ndim),
        ],
        out_specs=[
            pl.BlockSpec((PAIRS * 128, T), lambda g, r, t: (g, 0), pipeline_mode=pl.Buffered(1)),
            pl.BlockSpec((8, T), lambda g, r, t: (0, 0), pipeline_mode=pl.Buffered(1)),
            pl.BlockSpec((tk, PAIRS * 128), krow),
            pl.BlockSpec((tk, PAIRS * 128), krow),
            pl.BlockSpec((tk, 128), lambda g, r, t: (kj(r, t), 0)),
        ],
        out_shape=[jax.ShapeDtypeStruct((AW, T), F32), jax.ShapeDtypeStruct((8, T), F32),
                   jax.ShapeDtypeStruct((T, AW), BF16), jax.ShapeDtypeStruct((T, AW), BF16),
                   jax.ShapeDtypeStruct((T, FPAD), F32)],
        scratch_shapes=[pltpu.VMEM((tk, PAIRS * 128), F32), pltpu.VMEM((tk, PAIRS * 128), F32),
                        pltpu.VMEM((tk, 128), F32)],
        compiler_params=_params(3),
    )(kk, vk, kt, qw, dow, after)


def _fgate_bwd(dcq, dck, fz, fb):
    T = dck.shape[0]
    tb = _tile(T, 512)
    nb = T // tb

    def body(dcq_ref, dck_ref, fz_ref, fb_ref, df_ref, dfb_ref, carry):
        @pl.when(pl.program_id(0) == 0)
        def _():
            carry[...] = jnp.zeros_like(carry)
            dfb_ref[...] = jnp.zeros_like(dfb_ref)

        head = lax.broadcasted_iota(jnp.int32, (8, FPAD), 0)
        eye = jnp.where(head == lax.broadcasted_iota(jnp.int32, (8, FPAD), 1), 1.0, 0.0)
        dcv = dck_ref[...] + lax.dot_general(dcq_ref[...], eye, (((0,), (0,)), ((), ())),
                                             precision=lax.Precision.HIGHEST, preferred_element_type=F32)
        r = lax.broadcasted_iota(jnp.int32, (128, 128), 0)
        cc = lax.broadcasted_iota(jnp.int32, (128, 128), 1)
        tri = (cc >= r).astype(F32)
        run = carry[...]
        parts = []
        for blk in reversed(range(tb // 128)):
            dc_b = dcv[blk * 128:(blk + 1) * 128, :]
            parts.append(jnp.dot(tri, dc_b, precision=lax.Precision.HIGHEST, preferred_element_type=F32) + run)
            run = run + jnp.sum(dc_b, axis=0, keepdims=True)
        carry[...] = run
        dlf = jnp.concatenate(parts[::-1], axis=0)
        lane = lax.broadcasted_iota(jnp.int32, (tb, FPAD), 1)
        df = jnp.where(lane < 8, dlf * jax.nn.sigmoid(-(fz_ref[...] + fb_ref[...])), 0.0)
        df_ref[...] = df.astype(BF16)
        dfb_ref[...] += jnp.sum(df, axis=0, keepdims=True)

    rev = pl.BlockSpec((tb, FPAD), lambda i: (nb - 1 - i, 0))
    return pl.pallas_call(
        body,
        name="fgate_bwd",
        grid=(nb,),
        in_specs=[pl.BlockSpec((8, tb), lambda i: (0, nb - 1 - i)), rev, rev, _full((1, FPAD))],
        out_specs=[rev, _full((1, FPAD))],
        out_shape=[jax.ShapeDtypeStruct((T, FPAD), BF16), jax.ShapeDtypeStruct((1, FPAD), F32)],
        scratch_shapes=[pltpu.VMEM((1, FPAD), F32)],
        compiler_params=_params(1),
    )(dcq, dck, fz, fb)


def _inproj_bwd(dqt, dk, dv, dus, dfz, wtp, x, dh1, g, adam):
    T = x.shape[0]
    tm = _tile(T, 512)
    adam_in, adam_out, adam_shape = _adam_specs(adam[1], T // tm)

    def body(dq_ref, dk_ref, dv_ref, dus_ref, dfz_ref, w_ref, x_ref, dh1_ref, g_ref, *rest):
        gx_ref, dg_ref = rest[4:6]
        _adam_step(*rest[0:4], *rest[6:10])

        @pl.when(pl.program_id(0) == 0)
        def _():
            dg_ref[...] = jnp.zeros_like(dg_ref)

        da = _tn(dq_ref[...].astype(BF16), w_ref[0:AW, :])
        da += _nn(dk_ref[...], w_ref[AW:2 * AW, :])
        da += _nn(dv_ref[...], w_ref[2 * AW:NQKV, :])
        da += _nn(dus_ref[...], w_ref[NQKV:NQKV + NUS, :])
        da += _nn(dfz_ref[...], w_ref[NQKV + NUS:ZP, :])
        dx, dg = _rms_bwd(da, x_ref[...], g_ref[...])
        gx_ref[...] = dh1_ref[...] + dx
        dg_ref[...] += dg

    row = lambda n: pl.BlockSpec((tm, n), lambda i: (i, 0))
    return pl.pallas_call(
        body,
        name="inproj_bwd",
        grid=(T // tm,),
        in_specs=[pl.BlockSpec((AW, tm), lambda i: (0, i)), row(AW), row(AW), row(NUS), row(FPAD), _full((ZP, D)),
                  row(D), row(D), _full((1, D))] + adam_in,
        out_specs=[row(D), _full((1, D))] + adam_out,
        out_shape=[jax.ShapeDtypeStruct((T, D), F32), jax.ShapeDtypeStruct((1, D), F32)] + adam_shape,
        compiler_params=_params(1),
    )(dqt, dk, dv, dus, dfz, wtp, x, dh1, g, *adam)


def _sq_relu(f1):
    r = jnp.maximum(f1.astype(F32), 0.0)
    return (r * r).astype(BF16)


def _wgrad(a, bs, name, a_fn=None, out_dtype=F32, transposed=False, adam=None):
    T, K = a.shape
    tt = _tile(T, 1024)
    nb = len(bs)
    narrow = out_dtype != F32
    adam_in, adam_out, adam_shape = ([], [], []) if adam is None else _adam_specs(adam[1], T // tt)
    na = len(adam_in)

    def out_dims(b, layout):
        if transposed:
            return (b.shape[0] if layout == "t" else b.shape[1], K)
        if layout == "t":
            return (K, b.shape[0])
        N = b.shape[1]
        if layout is None:
            return (K, N)
        return (N // layout[1], K, layout[1]) if layout[0] == "col" else (K // layout[1], layout[1], N)

    shapes = [out_dims(b, layout) for b, layout in bs]

    def body(*refs):
        a_ref, b_refs, o_refs = refs[0], refs[1:1 + nb], refs[1 + nb + na:1 + 2 * nb + na]
        accs = refs[1 + 2 * nb + 2 * na:] if narrow else o_refs
        if adam is not None:
            _adam_step(*refs[1 + nb:1 + nb + na], *refs[1 + 2 * nb + na:1 + 2 * nb + 2 * na])

        @pl.when(pl.program_id(0) == 0)
        def _():
            for acc in accs:
                acc[...] = jnp.zeros_like(acc)

        av = (a_ref[...] if a_fn is None else a_fn(a_ref[...])).astype(BF16)
        at = None if transposed else av.T
        for (b, layout), b_ref, o_ref in zip(bs, b_refs, accs):
            if transposed:
                bv = b_ref[...].astype(BF16)
                o_ref[...] += _nn(bv, av) if layout == "t" else _tn(bv, av)
            elif layout is None:
                o_ref[...] += _nn(at, b_ref[...].astype(BF16))
            elif layout == "t":
                o_ref[...] += _nt(at, b_ref[...].astype(BF16))
            elif layout[0] == "col":
                n = layout[1]
                for k in range(b.shape[1] // n):
                    o_ref[k] += _nn(at, b_ref[:, k * n:(k + 1) * n].astype(BF16))
            else:
                n = layout[1]
                bv = b_ref[...].astype(BF16)
                for k in range(K // n):
                    o_ref[k] += _nn(at[k * n:(k + 1) * n, :], bv)

        if narrow:
            @pl.when(pl.program_id(0) == T // tt - 1)
            def _():
                for o_ref, acc in zip(o_refs, accs):
                    o_ref[...] = acc[...].astype(out_dtype)

    once = lambda shape: pl.BlockSpec(shape, lambda t: (0,) * len(shape), pipeline_mode=pl.Buffered(1))
    res = pl.pallas_call(
        body,
        name=name,
        grid=(T // tt,),
        in_specs=[pl.BlockSpec((tt, K), lambda t: (t, 0))] + [
            pl.BlockSpec((b.shape[0], tt), lambda t: (0, t)) if layout == "t" else pl.BlockSpec((tt, b.shape[1]), lambda t: (t, 0))
            for b, layout in bs] + adam_in,
        out_specs=[once(s) for s in shapes] + adam_out,
        out_shape=[jax.ShapeDtypeStruct(s, out_dtype) for s in shapes] + adam_shape,
        scratch_shapes=[pltpu.VMEM(s, F32) for s in shapes] if narrow else [],
        compiler_params=_params(1, VMEM_LIMIT_BIG),
    )(a, *[b for b, _ in bs], *(adam or ()))
    return res


def _adam_math(w, g, m, v):
    m = ADAM_B1 * m + (1.0 - ADAM_B1) * g
    v = ADAM_B2 * v + (1.0 - ADAM_B2) * (g * g)
    m_hat = m / (1.0 - ADAM_B1 ** ADAM_STEP)
    v_hat = v / (1.0 - ADAM_B2 ** ADAM_STEP)
    delta = -ADAM_LR * (m_hat / (jnp.sqrt(v_hat) + ADAM_EPS) + ADAM_WD * w)
    return delta, m, v


def _adam_step(p_ref, w_ref, m_ref, v_ref, g_ref, d_ref, nm_ref, nv_ref):
    g = p_ref[0].astype(F32)
    for s in range(1, NDEV):
        g = g + p_ref[s].astype(F32)
    g_ref[...] = g
    d_ref[...], nm_ref[...], nv_ref[...] = _adam_math(w_ref[...], g, m_ref[...], v_ref[...])


def _adam_specs(w, steps):
    R, C = w.shape
    br = R // steps
    assert br * steps == R and (br % 8 == 0 or steps == 1)
    blk = pl.BlockSpec((br, C), lambda i: (i, 0))
    return ([pl.BlockSpec((NDEV, br, C), lambda i: (0, i, 0)), blk, blk, blk], [blk] * 4,
            [jax.ShapeDtypeStruct((R, C), F32)] * 4)


def _adam(parts, w, m, v, name):
    R = w.shape[0]
    steps = R // 128 if R % 128 == 0 else 1
    in_specs, out_specs, out_shape = _adam_specs(w, steps)

    def body(*refs):
        _adam_step(*refs)

    return pl.pallas_call(
        body,
        name=name,
        grid=(steps,),
        in_specs=in_specs,
        out_specs=out_specs,
        out_shape=out_shape,
        compiler_params=_params(1),
    )(parts, w, m, v)


_SMALL = (("sg_w", 8 * CH * CH), ("f_bias", 8), ("sg_ln_g", SW), ("sg_ln_b", SW), ("sg_b", 8 * CH), ("att_out_g", AW),
          ("sg_out_g", SW), ("pre_mix_g", D), ("post_mix_g", D), ("pre_ffn_g", D), ("post_ffn_g", D), ("ple_gate_b", D))
_SEG = 8 * 128


def _seg_rows(size):
    return 8 * (-(-size // _SEG))


_LATE = ("f_bias", "pre_mix_g")


def _pack(vals, late, loss_acc=None):
    parts = []
    for name, size in _SMALL:
        if (name in _LATE) == late:
            rows = _seg_rows(size)
            parts.append(jnp.pad(vals[name].reshape(-1), (0, rows * 128 - size)).reshape(rows, 128))
    if loss_acc is not None:
        parts.append(loss_acc)
    return jnp.concatenate(parts, axis=0)


def _adam_small(parts_early, parts_late, ws, ms, vs):
    n = len(_SMALL)
    names = [name for name, _ in _SMALL]

    def body(*refs):
        pe_ref, pl_ref, refs = refs[0], refs[1], refs[1:]
        w_refs, m_refs, v_refs = refs[1:1 + n], refs[1 + n:1 + 2 * n], refs[1 + 2 * n:1 + 3 * n]
        loss_ref, outs = refs[1 + 3 * n], refs[2 + 3 * n:]
        sums, row = [], [0, 0]
        for p_ref in (pe_ref, pl_ref):
            tot = p_ref[0]
            for s in range(1, NDEV):
                tot = tot + p_ref[s]
            sums.append(tot)
        for k, (name, size) in enumerate(_SMALL):
            dst = [outs[kind * n + k] for kind in range(4)]
            which = 1 if name in _LATE else 0
            g_all, r = sums[which], row[which]

            def update(g, idx):
                vals = (g,) + _adam_math(w_refs[k][idx], g, m_refs[k][idx], v_refs[k][idx])
                for d, val in zip(dst, vals):
                    d[idx] = val

            if name == "sg_w":
                for grp in range(8):
                    update(g_all[r + grp * CH:r + (grp + 1) * CH, :], (0, grp))
            elif name == "sg_b":
                update(g_all[r:r + 8, :], (0,))
            elif name == "f_bias":
                update(g_all[r:r + 1, 0:8], (slice(None),))
            else:
                update(jnp.concatenate([g_all[r + q:r + q + 1, :] for q in range(size // 128)], axis=1), (slice(None),))
            row[which] += _seg_rows(size)
        loss_ref[...] = sums[0][row[0]:row[0] + 1, 0:1] * (0.5 / D)

    arrs = [parts_early, parts_late] + [d[name] for d in (ws, ms, vs) for name in names]
    res = pl.pallas_call(
        body,
        name="adam_small",
        in_specs=[_full(a.shape) for a in arrs],
        out_specs=[_full((1, 1))] + [_full(ws[name].shape) for _ in range(4) for name in names],
        out_shape=[jax.ShapeDtypeStruct((1, 1), F32)] + [jax.ShapeDtypeStruct(ws[name].shape, F32) for _ in range(4) for name in names],
        compiler_params=pltpu.CompilerParams(vmem_limit_bytes=VMEM_LIMIT),
    )(*arrs)
    return res[0], {name: [res[1 + kind * n + k] for kind in range(4)] for k, name in enumerate(names)}


def kernel(x, p, w_in, f_bias, sg_ln_g, sg_ln_b, sg_w, sg_b, att_out_g, sg_out_g, w_out, pre_mix_g, post_mix_g, pre_ffn_g, post_ffn_g, w_ff1, w_ff2, ple_w, ple_gate_w, ple_gate_b, loss_target, m_w_in, m_f_bias, m_sg_ln_g, m_sg_ln_b, m_sg_w, m_sg_b, m_att_out_g, m_sg_out_g, m_w_out, m_pre_mix_g, m_post_mix_g, m_pre_ffn_g, m_post_ffn_g, m_w_ff1, m_w_ff2, m_ple_w, m_ple_gate_w, m_ple_gate_b, v_w_in, v_f_bias, v_sg_ln_g, v_sg_ln_b, v_sg_w, v_sg_b, v_att_out_g, v_sg_out_g, v_w_out, v_pre_mix_g, v_post_mix_g, v_pre_ffn_g, v_post_ffn_g, v_w_ff1, v_w_ff2, v_ple_w, v_ple_gate_w, v_ple_gate_b):
    small_w = dict(sg_w=sg_w, f_bias=f_bias, sg_ln_g=sg_ln_g, sg_ln_b=sg_ln_b, sg_b=sg_b, att_out_g=att_out_g,
                   sg_out_g=sg_out_g, pre_mix_g=pre_mix_g, post_mix_g=post_mix_g, pre_ffn_g=pre_ffn_g,
                   post_ffn_g=post_ffn_g, ple_gate_b=ple_gate_b)
    small_m = dict(sg_w=m_sg_w, f_bias=m_f_bias, sg_ln_g=m_sg_ln_g, sg_ln_b=m_sg_ln_b, sg_b=m_sg_b, att_out_g=m_att_out_g,
                   sg_out_g=m_sg_out_g, pre_mix_g=m_pre_mix_g, post_mix_g=m_post_mix_g, pre_ffn_g=m_pre_ffn_g,
                   post_ffn_g=m_post_ffn_g, ple_gate_b=m_ple_gate_b)
    small_v = dict(sg_w=v_sg_w, f_bias=v_f_bias, sg_ln_g=v_sg_ln_g, sg_ln_b=v_sg_ln_b, sg_b=v_sg_b, att_out_g=v_att_out_g,
                   sg_out_g=v_sg_out_g, pre_mix_g=v_pre_mix_g, post_mix_g=v_post_mix_g, pre_ffn_g=v_pre_ffn_g,
                   post_ffn_g=v_post_ffn_g, ple_gate_b=v_ple_gate_b)
    big = dict(w_in=(w_in, m_w_in, v_w_in), w_out=(w_out, m_w_out, v_w_out), w_ff1=(w_ff1, m_w_ff1, v_w_ff1),
               w_ff2=(w_ff2, m_w_ff2, v_w_ff2), ple_w=(ple_w, m_ple_w, v_ple_w),
               ple_gate_w=(ple_gate_w, m_ple_gate_w, v_ple_gate_w))

    xt, pt, tgt = x[0], p[0, 0], loss_target[0]
    ws = W_IN_COLS // NDEV

    gw_in = _gather(jnp.pad(jnp.transpose(w_in[0]).astype(BF16), ((0, WS_PAD - ws), (0, 0))), "gather_w_in")
    gw_in, others = lax.optimization_barrier((gw_in, [w_out[0].astype(BF16), w_ff1[0].astype(BF16), w_ff2[0].astype(BF16),
                                                      ple_w[0].astype(BF16), ple_gate_w[0].astype(BF16)]))
    rest = _xchg_start(others, True, "gather_rest_start")
    wint = gw_in[:, 0:ws, :].reshape(W_IN_COLS, D)
    wtp = jnp.concatenate([wint[0:NQKV], wint[NQKV + 8:W_IN_COLS], wint[NQKV:NQKV + 8],
                           jnp.zeros((FPAD - 8, D), BF16)], axis=0)

    fb = jnp.pad(f_bias.astype(F32), ((0, 0), (0, FPAD - 8)))
    sgw = sg_w[0]
    sgwt = jnp.transpose(sg_w[0], (0, 2, 1))
    sgbt = jnp.transpose(sg_b[0])

    qkv, kt, vt, us, fz, ab = _inproj(xt, pre_mix_g + rest[4][0, 0], wtp)
    ct, kk, vk = _fcum(fz, fb, qkv)
    yatt, lsec = _attn_fwd(qkv, kk, vt, ct)
    gw_out, gw1, gw2, gwpe, gwg = _xchg_wait(rest, True, yatt, "gather_rest_wait")
    wout = gw_out.reshape(D, D)
    wg = gwg.reshape(D, D)
    wpe = jnp.transpose(gwpe, (1, 0, 2)).reshape(PLE, D)
    h1, yb, o = _sgu_out(us, yatt, xt, sg_ln_g, sg_ln_b, sgw, sgbt, att_out_g, sg_out_g, wout, post_mix_g)
    c2b, f1b, ff, h2 = _ffn_fwd(h1, pre_ffn_g, gw1, gw2, post_ffn_g)
    dh2, dbg, loss_acc, g_g, g_pe = _ple_loss(h2, pt, tgt, wg, ple_gate_b, wpe)
    g_g = g_g.reshape(NDEV, D // NDEV, D)
    g_pe = jnp.transpose(g_pe.reshape(PLE, NDEV, D // NDEV), (1, 0, 2))

    dffb, df1, dh1, dgpostffn, dgpreffn = _ffn_bwd(dh2, ff, h1, f1b, gw1, gw2, post_ffn_g, pre_ffn_g)
    (g_1,) = _wgrad(c2b, [(df1, ("col", DFF // NDEV))], "wgrad_ff1")
    (g_2,) = _wgrad(f1b, [(dffb, ("row", DFF // NDEV))], "wgrad_ff2", a_fn=_sq_relu)
    early = _xchg_start([g_1, g_2, g_pe, g_g], False, "scatter_early_start")
    qw, dow, dus, g_out, dsgw, dsgbt, dlng, dlnb, dgatt, dgsg, dgpostmix = _mix_bwd(
        dh1, o, us, yatt, qkv, lsec, yb, sg_ln_g, sg_ln_b, sgw, sgwt, sgbt, att_out_g, sg_out_g, wout,
        post_mix_g + early[4][0, 0])
    mid = _xchg_start([g_out.reshape(NDEV, D // NDEV, D)], False, "scatter_mid_start")
    small_g = dict(sg_w=dsgw, sg_ln_g=dlng, sg_ln_b=dlnb, sg_b=jnp.transpose(dsgbt[:, 0:8]), att_out_g=dgatt, sg_out_g=dgsg,
                   post_mix_g=dgpostmix, pre_ffn_g=dgpreffn, post_ffn_g=dgpostffn, ple_gate_b=dbg)
    small_early = _xchg_start([_pack(small_g, False, loss_acc)], True, "gather_small_early_start")
    dqt, dcq, dk, dv, dck = _attn_bwd(kk, vk, kt, qw, dow, mid[4] + small_early[4])
    dfz, dfb = _fgate_bwd(dcq, dck, fz, fb)

    r_1, r_2, r_pe, r_g = _xchg_wait(early, False, dfz, "scatter_early_wait")
    gq, gk, gv, gus, gf, *adam_1 = _wgrad(ab, [(dqt, "t"), (dk, None), (dv, None), (dus, None), (dfz, None)], "wgrad_in",
                                          out_dtype=BF16, transposed=True, adam=(r_1, *[t[0] for t in big["w_ff1"]]))
    g_in = jnp.concatenate([gq, gk, gv, gf[0:8], gus], axis=0).reshape(NDEV, ws, D)
    late = _xchg_start([g_in], False, "scatter_late_start")
    grad_x, dgpremix, *adam_2 = _inproj_bwd(dqt, dk, dv, dus, dfz, wtp, xt, dh1, pre_mix_g + late[4][0, 0],
                                            (r_2, *[t[0] for t in big["w_ff2"]]))

    (r_out,) = _xchg_wait(mid, False, grad_x, "scatter_mid_wait")
    (r_early,) = _xchg_wait(small_early, True, grad_x, "gather_small_early_wait")
    r_late = _gather(_pack(dict(f_bias=dfb[:, 0:8], pre_mix_g=dgpremix), True), "gather_small_late")

    res = dict(w_ff1=[t[None] for t in adam_1], w_ff2=[t[None] for t in adam_2])

    def adam_big(name, parts):
        w, m, v = big[name]
        res[name] = [t[None] for t in _adam(parts, w[0], m[0], v[0], "adam_" + name)]

    for name, parts in (("w_out", r_out), ("ple_w", r_pe), ("ple_gate_w", r_g)):
        adam_big(name, parts)
    (r_in,) = _xchg_wait(late, False, res["ple_gate_w"][0], "scatter_late_wait")
    res["w_in"] = [jnp.transpose(t)[None] for t in _adam(r_in, *[jnp.transpose(t[0]) for t in big["w_in"]], "adam_w_in")]
    loss, small = _adam_small(r_early, r_late, small_w, small_m, small_v)
    res.update(small)

    order = ["w_in", "f_bias", "sg_ln_g", "sg_ln_b", "sg_w", "sg_b", "att_out_g", "sg_out_g", "w_out", "pre_mix_g",
             "post_mix_g", "pre_ffn_g", "post_ffn_g", "w_ff1", "w_ff2", "ple_w", "ple_gate_w", "ple_gate_b"]
    outs = [loss[0, 0], grad_x[None]]
    for kind in range(4):
        outs += [res[name][kind] for name in order]
    return tuple(outs)
```

```python
import jax
import jax.numpy as jnp
from jax import lax
from jax.experimental import pallas as pl
from jax.experimental.pallas import tpu as pltpu

F32 = jnp.float32
BF16 = jnp.bfloat16

NDEV = 8
D = 1024
AW = 512
SW = 512
HD = 64
CH = 128
DFF = 4096
PLE = 256
NQKV = 3 * AW
NUS = 2 * SW
FPAD = 128
ZP = NQKV + NUS + FPAD
W_IN_COLS = 2568
WS_PAD = 336
EPS = 1e-6
MASKV = -1e30
GELU_K = 0.7978845608028654
GELU_C = 0.044715

ADAM_LR = 0.001
ADAM_B1 = 0.9
ADAM_B2 = 0.999
ADAM_EPS = 1e-08
ADAM_WD = 0.01
ADAM_STEP = 10

VMEM_LIMIT = 48 * 1024 * 1024
VMEM_LIMIT_BIG = 60 * 1024 * 1024


def _nn(a, b):
    return jnp.dot(a, b, preferred_element_type=F32)


def _nt(a, b):
    return lax.dot_general(a, b, (((1,), (1,)), ((), ())), preferred_element_type=F32)


def _tn(a, b):
    return lax.dot_general(a, b, (((0,), (0,)), ((), ())), preferred_element_type=F32)


def _tile(n, pref):
    return min(n, pref)


def _params(n_axes, vmem=VMEM_LIMIT):
    return pltpu.CompilerParams(dimension_semantics=("arbitrary",) * n_axes, vmem_limit_bytes=vmem)


def _full(shape):
    nd = len(shape)
    return pl.BlockSpec(shape, lambda *_: (0,) * nd)


def _rms_fwd(x, g):
    r = lax.rsqrt(jnp.mean(x * x, axis=-1, keepdims=True) + EPS)
    return x * r * g


def _rms_bwd(dy, x, g):
    n = x.shape[-1]
    r = lax.rsqrt(jnp.mean(x * x, axis=-1, keepdims=True) + EPS)
    u = dy * g
    s = jnp.sum(x * u, axis=-1, keepdims=True)
    dx = r * u - x * (r * r * r * (s * (1.0 / n)))
    dg = jnp.sum(dy * (x * r), axis=0, keepdims=True)
    return dx, dg


def _gelu(x):
    t = jnp.tanh(x * (GELU_K + (GELU_K * GELU_C) * (x * x)))
    return x * (0.5 + 0.5 * t), t


def _gelu_grad(x, t):
    return (0.5 + 0.5 * t) + (0.5 * x) * (1.0 - t * t) * (GELU_K + (3.0 * GELU_K * GELU_C) * (x * x))


def _gather(arr, name):
    def body(x_ref, out_ref, send, recv, loc):
        x, y, c = lax.axis_index("x"), lax.axis_index("y"), lax.axis_index("c")
        me, sibling = (x, y, c), (x, y, 1 - c)
        chips = [(1 - x, y), (x, 1 - y), (1 - x, 1 - y)]

        def slot(px, py, pc):
            return out_ref.at[4 * px + 2 * py + pc]

        def copy(k, block, to, src=None):
            return pltpu.make_async_remote_copy(
                src_ref=slot(*block) if src is None else src, dst_ref=slot(*block), send_sem=send.at[k],
                recv_sem=recv.at[k], device_id=to, device_id_type=pl.DeviceIdType.MESH)

        mine = pltpu.make_async_copy(x_ref, slot(*me), loc)
        mine.start()
        first = [copy(0, me, sibling, src=x_ref)] + [copy(1 + j, me, (*chip, c), src=x_ref) for j, chip in enumerate(chips)]
        for cp in first:
            cp.start()
        passed = [copy(4 + j, (*chip, c), sibling) for j, chip in enumerate(chips)]
        for j, chip in enumerate(chips):
            copy(1 + j, (*chip, c), me).wait_recv()
            passed[j].start()
        copy(0, sibling, me).wait_recv()
        for j, chip in enumerate(chips):
            copy(4 + j, (*chip, 1 - c), me).wait_recv()
        for cp in first + passed:
            cp.wait_send()
        mine.wait()

    hbm = pl.BlockSpec(memory_space=pltpu.HBM)
    return pl.pallas_call(
        body,
        name=name,
        out_shape=jax.ShapeDtypeStruct((NDEV,) + arr.shape, arr.dtype),
        in_specs=[hbm],
        out_specs=hbm,
        scratch_shapes=[pltpu.SemaphoreType.DMA((NDEV - 1,)), pltpu.SemaphoreType.DMA((NDEV - 1,)), pltpu.SemaphoreType.DMA],
    )(arr)


def _peers(x, y, c):
    out = []
    for k in range(1, NDEV):
        out.append((1 - x if (k >> 2) & 1 else x, 1 - y if (k >> 1) & 1 else y, 1 - c if k & 1 else c))
    return out


def _xchg_start(arrs, gather, name):
    n = len(arrs)
    me = 4 * lax.axis_index("x") + 2 * lax.axis_index("y") + lax.axis_index("c")
    lands = []
    for a in arrs:
        shape = ((NDEV,) + a.shape) if gather else a.shape
        own = a[None] if gather else lax.dynamic_slice_in_dim(a, me, 1, axis=0)
        lands.append(lax.dynamic_update_slice_in_dim(lax.empty(shape, a.dtype), own, me, axis=0))

    def body(*refs):
        ins, lnd = refs[:n], refs[n:2 * n]
        send, recv, token = refs[2 * n:3 * n], refs[3 * n:4 * n], refs[-1]
        x, y, c = lax.axis_index("x"), lax.axis_index("y"), lax.axis_index("c")
        mine = 4 * x + 2 * y + c
        for px, py, pc in _peers(x, y, c):
            peer = 4 * px + 2 * py + pc
            for a in range(n):
                pltpu.make_async_remote_copy(
                    src_ref=ins[a] if gather else ins[a].at[peer],
                    dst_ref=lnd[a].at[mine],
                    send_sem=send[a],
                    recv_sem=recv[a],
                    device_id=(px, py, pc),
                    device_id_type=pl.DeviceIdType.MESH,
                ).start()
        token[...] = jnp.zeros_like(token)

    hbm = pl.BlockSpec(memory_space=pltpu.HBM)
    sem = pl.BlockSpec(memory_space=pltpu.SEMAPHORE)
    res = pl.pallas_call(
        body,
        name=name,
        out_shape=(*[pltpu.SemaphoreType.DMA(())] * (2 * n),
                   *[pltpu.HBM(a.shape, a.dtype) for a in arrs], *[pltpu.HBM(l.shape, l.dtype) for l in lands],
                   jax.ShapeDtypeStruct((8, 128), F32)),
        in_specs=[hbm] * (2 * n),
        out_specs=(*([sem] * (2 * n)), *([hbm] * (2 * n)), pl.BlockSpec(memory_space=pltpu.VMEM)),
        input_output_aliases={i: 2 * n + i for i in range(2 * n)},
        compiler_params=pltpu.CompilerParams(has_side_effects=pltpu.SideEffectType.DATAFLOW_SIDE_EFFECTING),
    )(*[pltpu.with_memory_space_constraint(a, pltpu.HBM) for a in arrs],
      *[pltpu.with_memory_space_constraint(l, pltpu.HBM) for l in lands])
    return list(res[0:n]), list(res[n:2 * n]), list(res[2 * n:3 * n]), list(res[3 * n:4 * n]), res[-1]


def _xchg_wait(started, gather, after, name):
    send, recv, srcs, lands, _ = started
    n = len(srcs)

    def body(*refs):
        lnd = refs[n:2 * n]
        send, recv = refs[2 * n:3 * n], refs[3 * n:4 * n]
        me = (lax.axis_index("x"), lax.axis_index("y"), lax.axis_index("c"))
        for a in range(n):
            seven = lnd[a].at[pl.ds(0, NDEV - 1)]
            cp = pltpu.make_async_remote_copy(src_ref=seven, dst_ref=seven, send_sem=send[a], recv_sem=recv[a],
                                              device_id=me, device_id_type=pl.DeviceIdType.MESH)
            cp.wait_send()
            cp.wait_recv()

    hbm = pl.BlockSpec(memory_space=pltpu.HBM)
    sem = pl.BlockSpec(memory_space=pltpu.SEMAPHORE)
    res = pl.pallas_call(
        body,
        name=name,
        out_shape=tuple([pltpu.HBM(a.shape, a.dtype) for a in srcs] + [pltpu.HBM(l.shape, l.dtype) for l in lands]),
        in_specs=[hbm] * (2 * n) + [sem] * (2 * n) + [pl.BlockSpec(memory_space=pl.ANY)],
        out_specs=tuple([hbm] * (2 * n)),
        input_output_aliases={i: i for i in range(2 * n)},
        compiler_params=pltpu.CompilerParams(has_side_effects=pltpu.SideEffectType.DATAFLOW_SIDE_EFFECTING),
    )(*srcs, *lands, *send, *recv, after)
    return list(res[n:])


def _inproj(x, g, wtp):
    T = x.shape[0]
    tm = _tile(T, 512)

    def body(x_ref, g_ref, w_ref, qkv_ref, kt_ref, vt_ref, us_ref, fz_ref, ab_ref):
        a = _rms_fwd(x_ref[...], g_ref[...]).astype(BF16)
        ab_ref[...] = a
        qkv_ref[:, 0:AW] = _nt(a, w_ref[0:AW, :]).astype(BF16)
        kk = _nt(a, w_ref[AW:2 * AW, :])
        qkv_ref[:, AW:2 * AW] = kk.astype(BF16)
        kt_ref[...] = kk.T.astype(BF16)
        vv = _nt(a, w_ref[2 * AW:NQKV, :])
        qkv_ref[:, 2 * AW:NQKV] = vv.astype(BF16)
        vvt = vv.T.astype(BF16)
        one_row = jnp.where(lax.broadcasted_iota(jnp.int32, (HD, tm), 0) == 0, 1.0, 0.0).astype(BF16)
        for h in range(AW // HD):
            vt_ref[2 * h * HD:(2 * h + 1) * HD, :] = vvt[h * HD:(h + 1) * HD, :]
            vt_ref[(2 * h + 1) * HD:(2 * h + 2) * HD, :] = one_row
        us_ref[...] = _nt(a, w_ref[NQKV:NQKV + NUS, :])
        fz_ref[...] = _nt(a, w_ref[NQKV + NUS:ZP, :])

    row = lambda n: pl.BlockSpec((tm, n), lambda i: (i, 0))
    col = lambda n: pl.BlockSpec((n, tm), lambda i: (0, i))
    return pl.pallas_call(
        body,
        name="inproj",
        grid=(T // tm,),
        in_specs=[row(D), _full((1, D)), _full((ZP, D))],
        out_specs=[row(NQKV), col(AW), col(2 * AW), row(NUS), row(FPAD), row(D)],
        out_shape=[
            jax.ShapeDtypeStruct((T, NQKV), BF16),
            jax.ShapeDtypeStruct((AW, T), BF16),
            jax.ShapeDtypeStruct((2 * AW, T), BF16),
            jax.ShapeDtypeStruct((T, NUS), F32),
            jax.ShapeDtypeStruct((T, FPAD), F32),
            jax.ShapeDtypeStruct((T, D), BF16),
        ],
        compiler_params=_params(1),
    )(x, g, wtp)


def _log_sigmoid(z):
    return jnp.minimum(z, 0.0) - jnp.log1p(jnp.exp(-jnp.abs(z)))


def _split3(x):
    hi = x.astype(BF16)
    r1 = x - hi.astype(F32)
    mid = r1.astype(BF16)
    lo = (r1 - mid.astype(F32)).astype(BF16)
    return hi, mid, lo


AUG_A, AUG_B, AUG_ONE = 0, 3, 6


def _aug_lanes(rows, first_one, pieces):
    lane = lax.broadcasted_iota(jnp.int32, (rows, 128), 1)
    out = jnp.zeros((rows, 128), F32)
    if first_one is not None:
        out = jnp.where((lane >= first_one) & (lane < first_one + 3), 1.0, out)
    for n, piece in enumerate(pieces):
        out = jnp.where(lane == AUG_ONE + n, piece.astype(F32), out)
    return out.astype(BF16)


def _fcum(fz, fb, qkv):
    T = fz.shape[0]
    tb = _tile(T, 512)

    def body(fz_ref, fb_ref, k_ref, v_ref, ct_ref, kk_ref, vk_ref, carry, tri_s, sel_s):
        @pl.when(pl.program_id(0) == 0)
        def _():
            carry[...] = jnp.zeros_like(carry)
            src = lax.broadcasted_iota(jnp.int32, (128, 128), 0)
            dst = lax.broadcasted_iota(jnp.int32, (128, 128), 1)
            tri_s[...] = (dst <= src).astype(F32)
            for hp in range(4):
                for n in range(3):
                    pick = ((src == 2 * hp) & (dst == AUG_A + n)) | ((src == 2 * hp + 1) & (dst == AUG_B + n))
                    sel_s[3 * hp + n] = jnp.where(pick, -1.0, 0.0).astype(BF16)

        lf = _log_sigmoid(fz_ref[...] + fb_ref[...])
        run = carry[...]
        parts = []
        for blk in range(tb // 128):
            lf_b = lf[blk * 128:(blk + 1) * 128, :]
            parts.append(jnp.dot(tri_s[...], lf_b, precision=lax.Precision.HIGHEST, preferred_element_type=F32) + run)
            run = run + jnp.sum(lf_b, axis=0, keepdims=True)
        carry[...] = run
        cs = jnp.concatenate(parts, axis=0)
        ct_ref[...] = cs.T[0:8, :]

        pieces = _split3(cs)
        lane = lax.broadcasted_iota(jnp.int32, (1, 128), 1)
        ones = jnp.where((lane >= AUG_ONE) & (lane < AUG_ONE + 3), 1.0, 0.0)
        for hp in range(4):
            aug = jnp.zeros((tb, 128), F32) + ones
            for n, piece in enumerate(pieces):
                aug = aug + _nn(piece, sel_s[3 * hp + n])
            aug = aug.astype(BF16)
            kk_ref[:, hp * 256:hp * 256 + 128] = k_ref[:, hp * 128:(hp + 1) * 128]
            kk_ref[:, hp * 256 + 128:(hp + 1) * 256] = aug
            vk_ref[:, hp * 256:hp * 256 + 128] = v_ref[:, hp * 128:(hp + 1) * 128]
            vk_ref[:, hp * 256 + 128:(hp + 1) * 256] = aug

    row = lambda n, c: pl.BlockSpec((tb, n), lambda i: (i, c))
    return pl.pallas_call(
        body,
        name="fcum",
        grid=(T // tb,),
        in_specs=[row(FPAD, 0), _full((1, FPAD)), row(AW, 1), row(AW, 2)],
        out_specs=[pl.BlockSpec((8, tb), lambda i: (0, i)), row(2 * AW, 0), row(2 * AW, 0)],
        out_shape=[jax.ShapeDtypeStruct((8, T), F32), jax.ShapeDtypeStruct((T, 2 * AW), BF16),
                   jax.ShapeDtypeStruct((T, 2 * AW), BF16)],
        scratch_shapes=[pltpu.VMEM((1, FPAD), F32), pltpu.VMEM((128, 128), F32), pltpu.VMEM((12, 128, 128), BF16)],
        compiler_params=_params(1),
    )(fz, fb, qkv, qkv)


def _fold_rows(r, t, nq):
    if nq == 1:
        return r, t
    low = t <= r
    return jnp.where(low, r, nq - 1 - r), jnp.where(low, t, t - r - 1)


def _fold_cols(r, t, nq):
    if nq == 1:
        return r, t
    first = t < nq - r
    j = jnp.where(first, r, nq - 1 - r)
    return j, jnp.where(first, r + t, nq - 1 - r + (t - (nq - r)))


PAIRS = 4


def _fold_grid(nq):
    assert (nq == 1 or nq % 2 == 0) and PAIRS == 4
    return (4 // PAIRS, 1, 1) if nq == 1 else (4 // PAIRS, nq // 2, nq + 1)


def _head_rows(x2, hh, scale):
    is_a = lax.broadcasted_iota(jnp.int32, (1, 128), 1) < HD
    keep = is_a if hh == 0 else jnp.logical_not(is_a)
    return jnp.where(keep, x2, jnp.zeros_like(x2)) * scale


def _attn_fwd(qkv, kk, vt, ct):
    T = qkv.shape[0]
    tq = _tile(T, 512)
    tk = tq
    nq = T // tq

    def body(q_ref, kk_ref, vt_ref, ctq_ref, o_ref, lsec_ref, qw_s, m_s, acc_s):
        i, j = _fold_rows(pl.program_id(1), pl.program_id(2), nq)
        sub8 = lax.broadcasted_iota(jnp.int32, (8, 1), 0)

        def cref_of(pp, hh):
            head = 2 * (PAIRS * pl.program_id(0) + pp) + hh
            return jnp.sum(jnp.where(sub8 == head, ctq_ref[:, 0:1], 0.0), axis=0, keepdims=True)

        @pl.when(j == 0)
        def _():
            for pp in range(PAIRS):
                q2 = q_ref[:, pp * 128:(pp + 1) * 128]
                for hh in range(2):
                    rows = slice(hh * tq, (hh + 1) * tq)
                    qw_s[pp, rows, 0:128] = _head_rows(q2, hh, 0.125)
                    qw_s[pp, rows, 128:256] = _aug_lanes(tq, AUG_A if hh == 0 else AUG_B, _split3(cref_of(pp, hh)))
            m_s[...] = jnp.full_like(m_s, MASKV)
            acc_s[...] = jnp.zeros_like(acc_s)

        def step(masked):
            if masked:
                causal = lax.broadcasted_iota(jnp.int32, (tk, tq), 0) <= lax.broadcasted_iota(jnp.int32, (tk, tq), 1)
            logits = lambda pp: _nt(kk_ref[:, pp * 256:(pp + 1) * 256], qw_s[pp])
            s_next = logits(0)
            for pp in range(PAIRS):
                s2 = s_next
                if pp + 1 < PAIRS:
                    s_next = logits(pp + 1)
                for hh in range(2):
                    n = 2 * pp + hh
                    s = s2[:, hh * tq:(hh + 1) * tq]
                    if masked:
                        s = jnp.where(causal, s, MASKV)
                    m_prev = m_s[n]
                    m_new = jnp.maximum(m_prev, jnp.max(s, axis=0, keepdims=True))
                    pr = jnp.exp(s - m_new)
                    m_s[n] = m_new
                    acc_s[n] = jnp.exp(m_prev - m_new) * acc_s[n] + _nn(vt_ref[n * 128:(n + 1) * 128, :], pr.astype(BF16))

        @pl.when(j < i)
        def _():
            step(False)

        @pl.when(j == i)
        def _():
            step(True)
            sub = lax.broadcasted_iota(jnp.int32, (128, 1), 0)
            for pp in range(PAIRS):
                outs, lrow = [], []
                for hh in range(2):
                    n = 2 * pp + hh
                    den = acc_s[n, HD:HD + 1, :]
                    outs.append(acc_s[n, 0:HD, :] * (1.0 / den))
                    lrow.append(m_s[n] + jnp.log(den) - cref_of(pp, hh))
                o_ref[:, pp * 128:(pp + 1) * 128] = jnp.concatenate(outs, axis=0).T
                lsec_ref[:, pp * 128:(pp + 1) * 128] = jnp.where(sub == 0, lrow[0], jnp.where(sub == 1, lrow[1], 0.0)).T

    qi = lambda r, t: _fold_rows(r, t, nq)[0]
    kj = lambda r, t: _fold_rows(r, t, nq)[1]
    return pl.pallas_call(
        body,
        name="attn_fwd",
        grid=_fold_grid(nq),
        in_specs=[
            pl.BlockSpec((tq, PAIRS * 128), lambda g, r, t: (qi(r, t), g)),
            pl.BlockSpec((tk, PAIRS * 256), lambda g, r, t: (kj(r, t), g)),
            pl.BlockSpec((PAIRS * 256, tk), lambda g, r, t: (g, kj(r, t))),
            pl.BlockSpec((8, tq), lambda g, r, t: (0, qi(r, t))),
        ],
        out_specs=[pl.BlockSpec((tq, PAIRS * 128), lambda g, r, t: (qi(r, t), g))] * 2,
        out_shape=[jax.ShapeDtypeStruct((T, AW), F32)] * 2,
        scratch_shapes=[pltpu.VMEM((PAIRS, 2 * tq, 256), BF16), pltpu.VMEM((2 * PAIRS, 1, tq), F32),
                        pltpu.VMEM((2 * PAIRS, 128, tq), F32)],
        compiler_params=_params(3),
    )(qkv, kk, vt, ct)


def _sgu_forward(us_ref, lng, lnb, w_ref, bt_ref, mixed_s, vnb_s, tm):
    is_a = lax.broadcasted_iota(jnp.int32, (1, 128), 1) < HD
    u = us_ref[:, 0:SW]
    vs = us_ref[:, SW:NUS]
    ug, tu = _gelu(u)
    vg, tv = _gelu(vs)
    mu = jnp.mean(vg, axis=-1, keepdims=True)
    xc = vg - mu
    rstd = lax.rsqrt(jnp.mean(xc * xc, axis=-1, keepdims=True) + EPS)
    vhat = xc * rstd
    vnb_s[...] = (vhat * lng + lnb).astype(BF16)
    rr = lax.broadcasted_iota(jnp.int32, (CH, CH), 0)
    cc = lax.broadcasted_iota(jnp.int32, (CH, CH), 1)
    tril = cc <= rr
    for jj in range(4):
        wa = jnp.where(tril, w_ref[2 * jj], 0.0).astype(BF16)
        wb = jnp.where(tril, w_ref[2 * jj + 1], 0.0).astype(BF16)
        ba = bt_ref[:, 2 * jj:2 * jj + 1]
        bb = bt_ref[:, 2 * jj + 1:2 * jj + 2]
        for ch in range(tm // CH):
            rs, cs = slice(ch * CH, (ch + 1) * CH), slice(jj * 128, (jj + 1) * 128)
            vn2 = vnb_s[rs, cs]
            mixed_s[rs, cs] = jnp.where(is_a, _nn(wa, vn2) + ba, _nn(wb, vn2) + bb)
    mixed = mixed_s[...]
    return u, vs, ug, tu, tv, vhat, rstd, mixed, ug * mixed


def _sgu_out(us, yatt, x, lng, lnb, sgw, sgbt, gatt, gsg, wout, gpm):
    T = us.shape[0]
    tm = _tile(T, 512)

    def body(us_ref, ya_ref, x_ref, lng_ref, lnb_ref, w_ref, bt_ref, ga_ref, gs_ref, wo_ref, gp_ref,
             h1_ref, yb_ref, o_ref, mixed_s, vnb_s):
        ysg = _sgu_forward(us_ref, lng_ref[...], lnb_ref[...], w_ref, bt_ref, mixed_s, vnb_s, tm)[-1]
        yb_ref[:, 0:AW] = _rms_fwd(ya_ref[...], ga_ref[...]).astype(BF16)
        yb_ref[:, AW:D] = _rms_fwd(ysg, gs_ref[...]).astype(BF16)
        o = _nn(yb_ref[...], wo_ref[...])
        o_ref[...] = o
        h1_ref[...] = x_ref[...] + _rms_fwd(o, gp_ref[...])

    row = lambda n: pl.BlockSpec((tm, n), lambda i: (i, 0))
    return pl.pallas_call(
        body,
        name="sgu_out",
        grid=(T // tm,),
        in_specs=[row(NUS), row(AW), row(D), _full((1, SW)), _full((1, SW)), _full((8, CH, CH)), _full((CH, 8)),
                  _full((1, AW)), _full((1, SW)), _full((D, D)), _full((1, D))],
        out_specs=[row(D), row(D), row(D)],
        out_shape=[jax.ShapeDtypeStruct((T, D), F32), jax.ShapeDtypeStruct((T, D), BF16), jax.ShapeDtypeStruct((T, D), F32)],
        scratch_shapes=[pltpu.VMEM((tm, SW), F32), pltpu.VMEM((tm, SW), BF16)],
        compiler_params=_params(1),
    )(us, yatt, x, lng, lnb, sgw, sgbt, gatt, gsg, wout, gpm)


def _ffn_fwd(h1, gpre, w1g, w2g, gpost):
    T = h1.shape[0]
    tm = _tile(T, 512)
    nb, hb = w1g.shape[0], w1g.shape[2]

    def body(h1_ref, gpre_ref, w1_ref, w2_ref, gpost_ref, c2_ref, f1_ref, ff_ref, h2_ref):
        h1 = h1_ref[...]
        c2 = _rms_fwd(h1, gpre_ref[...]).astype(BF16)
        c2_ref[...] = c2
        ff = jnp.zeros((tm, D), F32)
        for j in range(nb):
            f1 = _nn(c2, w1_ref[j])
            f1_ref[:, j * hb:(j + 1) * hb] = f1.astype(BF16)
            r = jnp.maximum(f1, 0.0)
            ff = ff + _nn((r * r).astype(BF16), w2_ref[j])
        ff_ref[...] = ff
        h2_ref[...] = h1 + _rms_fwd(ff, gpost_ref[...])

    row = lambda n: pl.BlockSpec((tm, n), lambda i: (i, 0))
    once = lambda shape: pl.BlockSpec(shape, lambda i: (0,) * len(shape), pipeline_mode=pl.Buffered(1))
    return pl.pallas_call(
        body,
        name="ffn_fwd",
        grid=(T // tm,),
        in_specs=[row(D), _full((1, D)), once((nb, D, hb)), once((nb, hb, D)), _full((1, D))],
        out_specs=[row(D), row(DFF), row(D), row(D)],
        out_shape=[jax.ShapeDtypeStruct((T, D), BF16), jax.ShapeDtypeStruct((T, DFF), BF16),
                   jax.ShapeDtypeStruct((T, D), F32), jax.ShapeDtypeStruct((T, D), F32)],
        compiler_params=_params(1, VMEM_LIMIT_BIG),
    )(h1, gpre, w1g, w2g, gpost)


def _ple_loss(h2, p, tgt, wg, bg, wpe):
    T = h2.shape[0]
    tm = _tile(T, 512)

    def body(h2_ref, p_ref, t_ref, wg_ref, bg_ref, wpe_ref, dh2_ref, dbg_ref, loss_ref, dwg_ref, dwpe_ref):
        @pl.when(pl.program_id(0) == 0)
        def _():
            for r in (dbg_ref, loss_ref, dwg_ref, dwpe_ref):
                r[...] = jnp.zeros_like(r)

        h2 = h2_ref[...]
        h2b = h2.astype(BF16)
        gate = jax.nn.sigmoid(_nn(h2b, wg_ref[...]) + bg_ref[...])
        pb = p_ref[...].astype(BF16)
        pe = _nn(pb, wpe_ref[...])
        diff = (h2 + gate * pe) - t_ref[...]
        loss_ref[...] += jnp.sum(diff * diff)
        dh3 = diff * (1.0 / D)
        dpre = (dh3 * pe) * (gate * (1.0 - gate))
        dpre_b = dpre.astype(BF16)
        dbg_ref[...] += jnp.sum(dpre, axis=0, keepdims=True)
        dh2_ref[...] = dh3 + _nt(dpre_b, wg_ref[...])
        dwg_ref[...] += _nn(h2b.T, dpre_b)
        dwpe_ref[...] += _nn(pb.T, (dh3 * gate).astype(BF16))

    row = lambda n: pl.BlockSpec((tm, n), lambda i: (i, 0))
    once = lambda shape: pl.BlockSpec(shape, lambda i: (0,) * len(shape), pipeline_mode=pl.Buffered(1))
    return pl.pallas_call(
        body,
        name="ple_loss",
        grid=(T // tm,),
        in_specs=[row(D), row(PLE), row(D), _full((D, D)), _full((1, D)), _full((PLE, D))],
        out_specs=[row(D), _full((1, D)), _full((8, 128)), once((D, D)), once((PLE, D))],
        out_shape=[jax.ShapeDtypeStruct((T, D), F32), jax.ShapeDtypeStruct((1, D), F32),
                   jax.ShapeDtypeStruct((8, 128), F32), jax.ShapeDtypeStruct((D, D), F32),
                   jax.ShapeDtypeStruct((PLE, D), F32)],
        compiler_params=_params(1),
    )(h2, p, tgt, wg, bg, wpe)


def _ffn_bwd(dh2, ff, h1, f1, w1g, w2g, gpost, gpre):
    T = dh2.shape[0]
    tm = _tile(T, 512)
    nb, hb = w1g.shape[0], w1g.shape[2]

    def body(dh2_ref, ff_ref, h1_ref, f1_ref, w1_ref, w2_ref, gpost_ref, gpre_ref,
             dffb_ref, df1_ref, dh1_ref, dgpost_ref, dgpre_ref):
        @pl.when(pl.program_id(0) == 0)
        def _():
            dgpost_ref[...] = jnp.zeros_like(dgpost_ref)
            dgpre_ref[...] = jnp.zeros_like(dgpre_ref)

        dh2 = dh2_ref[...]
        dff, dg = _rms_bwd(dh2, ff_ref[...], gpost_ref[...])
        dffb = dff.astype(BF16)
        dffb_ref[...] = dffb
        dgpost_ref[...] += dg
        dc2 = jnp.zeros((tm, D), F32)
        for j in range(nb):
            cols = slice(j * hb, (j + 1) * hb)
            dact = _nt(dffb, w2_ref[j])
            df1 = (dact * (2.0 * jnp.maximum(f1_ref[:, cols].astype(F32), 0.0))).astype(BF16)
            df1_ref[:, cols] = df1
            dc2 = dc2 + _nt(df1, w1_ref[j])
        dx, dg = _rms_bwd(dc2, h1_ref[...], gpre_ref[...])
        dh1_ref[...] = dh2 + dx
        dgpre_ref[...] += dg

    row = lambda n: pl.BlockSpec((tm, n), lambda i: (i, 0))
    once = lambda shape: pl.BlockSpec(shape, lambda i: (0,) * len(shape), pipeline_mode=pl.Buffered(1))
    return pl.pallas_call(
        body,
        name="ffn_bwd",
        grid=(T // tm,),
        in_specs=[row(D), row(D), row(D), row(DFF), once((nb, D, hb)), once((nb, hb, D)), _full((1, D)), _full((1, D))],
        out_specs=[row(D), row(DFF), row(D), _full((1, D)), _full((1, D))],
        out_shape=[jax.ShapeDtypeStruct((T, D), BF16), jax.ShapeDtypeStruct((T, DFF), BF16),
                   jax.ShapeDtypeStruct((T, D), F32), jax.ShapeDtypeStruct((1, D), F32),
                   jax.ShapeDtypeStruct((1, D), F32)],
        compiler_params=_params(1, VMEM_LIMIT_BIG),
    )(dh2, ff, h1, f1, w1g, w2g, gpost, gpre)


def _mix_bwd(dh1, o, us, yatt, qkv, lsec, yb, lng, lnb, sgw, sgwt, sgbt, gatt, gsg, wout, gpm):
    T = dh1.shape[0]
    tm = _tile(T, 512)

    def body(dh1_ref, o_ref, us_ref, ya_ref, q_ref, l_ref, yb_ref, lng_ref, lnb_ref, w_ref, wt_ref, bt_ref, ga_ref, gs_ref,
             wo_ref, gp_ref, qw_ref, dow_ref, dus_ref, dwo_ref, dw_ref, dbt_ref, dlng_ref, dlnb_ref, dga_ref, dgs_ref,
             dgp_ref, mixed_s, vnb_s, dvn_s):
        @pl.when(pl.program_id(0) == 0)
        def _():
            for r in (dwo_ref, dw_ref, dbt_ref, dlng_ref, dlnb_ref, dga_ref, dgs_ref, dgp_ref):
                r[...] = jnp.zeros_like(r)

        is_a = lax.broadcasted_iota(jnp.int32, (1, 128), 1) < HD
        lane = lax.broadcasted_iota(jnp.int32, (1, 128), 1)
        do, dg = _rms_bwd(dh1_ref[...], o_ref[...], gp_ref[...])
        dgp_ref[...] += dg
        dob = do.astype(BF16)
        dwo_ref[...] += _nn(yb_ref[...].T, dob)
        dy = _nt(dob, wo_ref[...])
        ya = ya_ref[...]
        datt, dg = _rms_bwd(dy[:, 0:AW], ya, ga_ref[...])
        dga_ref[...] += dg
        _attn_operands(q_ref[...], datt, ya, l_ref[...], qw_ref, dow_ref)

        lng = lng_ref[...]
        u, vs, ug, tu, tv, vhat, rstd, mixed, ysg = _sgu_forward(us_ref, lng, lnb_ref[...], w_ref, bt_ref, mixed_s, vnb_s, tm)
        dysg, dg = _rms_bwd(dy[:, AW:D], ysg, gs_ref[...])
        dgs_ref[...] += dg
        dus_ref[:, 0:SW] = ((dysg * mixed) * _gelu_grad(u, tu)).astype(BF16)
        dmix = dysg * ug

        rr = lax.broadcasted_iota(jnp.int32, (CH, CH), 0)
        cc = lax.broadcasted_iota(jnp.int32, (CH, CH), 1)
        tril = cc <= rr
        triu = cc >= rr
        for jj in range(4):
            wta = jnp.where(triu, wt_ref[2 * jj], 0.0).astype(BF16)
            wtb = jnp.where(triu, wt_ref[2 * jj + 1], 0.0).astype(BF16)
            for ch in range(tm // CH):
                rs, cs = slice(ch * CH, (ch + 1) * CH), slice(jj * 128, (jj + 1) * 128)
                dm2 = dmix[rs, cs]
                dma = jnp.where(is_a, dm2, 0.0)
                dmb = jnp.where(is_a, 0.0, dm2)
                dma_b, dmb_b = dma.astype(BF16), dmb.astype(BF16)
                vn2 = vnb_s[rs, cs]
                dw_ref[2 * jj] += jnp.where(tril, _nt(dma_b, vn2), 0.0)
                dw_ref[2 * jj + 1] += jnp.where(tril, _nt(dmb_b, vn2), 0.0)
                dvn_s[rs, cs] = _nn(wta, dma_b) + _nn(wtb, dmb_b)
                dba = jnp.sum(dma, axis=1, keepdims=True)
                dbb = jnp.sum(dmb, axis=1, keepdims=True)
                dbt_ref[...] += jnp.where(lane == 2 * jj, dba, 0.0) + jnp.where(lane == 2 * jj + 1, dbb, 0.0)

        dvn = dvn_s[...]
        dlng_ref[...] += jnp.sum(dvn * vhat, axis=0, keepdims=True)
        dlnb_ref[...] += jnp.sum(dvn, axis=0, keepdims=True)
        dvh = dvn * lng
        dvg = rstd * (dvh - jnp.mean(dvh, axis=-1, keepdims=True) - vhat * jnp.mean(dvh * vhat, axis=-1, keepdims=True))
        dus_ref[:, SW:NUS] = (dvg * _gelu_grad(vs, tv)).astype(BF16)

    row = lambda n: pl.BlockSpec((tm, n), lambda i: (i, 0))
    return pl.pallas_call(
        body,
        name="mix_bwd",
        grid=(T // tm,),
        in_specs=[row(D), row(D), row(NUS), row(AW), row(AW), row(AW), row(D), _full((1, SW)), _full((1, SW)),
                  _full((8, CH, CH)), _full((8, CH, CH)), _full((CH, 8)), _full((1, AW)), _full((1, SW)), _full((D, D)),
                  _full((1, D))],
        out_specs=[row(4 * AW), row(4 * AW), row(NUS),
                   pl.BlockSpec((D, D), lambda i: (0, 0), pipeline_mode=pl.Buffered(1)), _full((8, CH, CH)),
                   _full((CH, 128)), _full((1, SW)), _full((1, SW)), _full((1, AW)), _full((1, SW)), _full((1, D))],
        out_shape=[jax.ShapeDtypeStruct((T, 4 * AW), BF16),
                   jax.ShapeDtypeStruct((T, 4 * AW), BF16), jax.ShapeDtypeStruct((T, NUS), BF16),
                   jax.ShapeDtypeStruct((D, D), F32),
                   jax.ShapeDtypeStruct((8, CH, CH), F32), jax.ShapeDtypeStruct((CH, 128), F32),
                   jax.ShapeDtypeStruct((1, SW), F32), jax.ShapeDtypeStruct((1, SW), F32),
                   jax.ShapeDtypeStruct((1, AW), F32), jax.ShapeDtypeStruct((1, SW), F32), jax.ShapeDtypeStruct((1, D), F32)],
        scratch_shapes=[pltpu.VMEM((tm, SW), F32), pltpu.VMEM((tm, SW), BF16), pltpu.VMEM((tm, SW), F32)],
        compiler_params=_params(1, VMEM_LIMIT_BIG),
    )(dh1, o, us, yatt, qkv, lsec, yb, lng, lnb, sgw, sgwt, sgbt, gatt, gsg, wout, gpm)


def _attn_operands(q, do, o, lsec, qw_ref, dow_ref):
    rows = do.shape[0]
    feat = lax.broadcasted_iota(jnp.int32, (AW, 128), 0)
    head = lax.broadcasted_iota(jnp.int32, (AW, 128), 1)
    sel = jnp.where((feat >= head * HD) & (feat < (head + 1) * HD), 1.0, 0.0)
    delta = jnp.dot(do * o, sel, precision=lax.Precision.HIGHEST, preferred_element_type=F32)
    for hp in range(4):
        cols = slice(hp * 128, (hp + 1) * 128)
        for hh in range(2):
            base = (2 * hp + hh) * 256
            lc = lsec[:, hp * 128 + hh:hp * 128 + hh + 1]
            d_h = delta[:, 2 * hp + hh:2 * hp + hh + 1]
            qw_ref[:, base:base + 128] = _head_rows(q[:, cols], hh, 0.125)
            qw_ref[:, base + 128:base + 256] = _aug_lanes(rows, AUG_A if hh == 0 else AUG_B, _split3(-lc))
            dow_ref[:, base:base + 128] = _head_rows(do[:, cols], hh, 1.0).astype(BF16)
            dow_ref[:, base + 128:base + 256] = _aug_lanes(rows, None, _split3(-d_h))


def _attn_bwd(kk, vk, kt, qw, dow, after):
    T = kk.shape[0]
    tq = _tile(T, 512)
    tk = tq
    nq = T // tq

    def body(kk_ref, vk_ref, kt_ref, qw_ref, dow_ref, after_ref, dqt_ref, dcq_ref, dk_ref, dv_ref, dck_ref, dk_s, dv_s, dck_s):
        j, i = _fold_cols(pl.program_id(1), pl.program_id(2), nq)
        sub8 = lax.broadcasted_iota(jnp.int32, (8, 1), 0)
        lane = lax.broadcasted_iota(jnp.int32, (1, 128), 1)

        @pl.when((pl.program_id(1) == 0) & (pl.program_id(2) == 0))
        def _():
            dqt_ref[...] = jnp.zeros_like(dqt_ref)
            dcq_ref[...] = jnp.zeros_like(dcq_ref)

        @pl.when(i == j)
        def _():
            dk_s[...] = jnp.zeros_like(dk_s)
            dv_s[...] = jnp.zeros_like(dv_s)
            dck_s[...] = jnp.zeros_like(dck_s)

        def step(masked):
            cols = pl.ds(pl.multiple_of(i * tq, tq), tq)
            sub = lax.broadcasted_iota(jnp.int32, (128, 1), 0)
            if masked:
                causal = lax.broadcasted_iota(jnp.int32, (tk, tq), 0) <= lax.broadcasted_iota(jnp.int32, (tk, tq), 1)

            def logits(n):
                pair, base = n // 2, n * 256
                return (_nt(kk_ref[:, pair * 256:(pair + 1) * 256], qw_ref[:, base:base + 256]),
                        _nt(vk_ref[:, pair * 256:(pair + 1) * 256], dow_ref[:, base:base + 256]))

            ahead = logits(0)
            dcq = jnp.zeros((8, tq), F32)
            dck = jnp.zeros((tk, 128), F32)
            for pp in range(PAIRS):
                lanes = slice(pp * 128, (pp + 1) * 128)
                kt2 = kt_ref[lanes, :] * 0.125
                dv = jnp.zeros((tk, 128), F32)
                dk = jnp.zeros((tk, 128), F32)
                dqts = []
                for hh in range(2):
                    head = 2 * pp + hh
                    base = head * 256
                    qw_h = qw_ref[:, base:base + 256]
                    dow_h = dow_ref[:, base:base + 256]
                    logp, dp = ahead
                    if head + 1 < 2 * PAIRS:
                        ahead = logits(head + 1)
                    pr = jnp.exp(logp)
                    if masked:
                        pr = jnp.where(causal, pr, 0.0)
                    ds = pr * dp
                    ds_b = ds.astype(BF16)
                    dv = dv + _nn(pr.astype(BF16), dow_h[:, 0:128])
                    dk = dk + _nn(ds_b, qw_h[:, 0:128])
                    dqts.append(_nn(kt2, ds_b))
                    dcq = dcq + jnp.where(sub8 == head, jnp.sum(ds, axis=0, keepdims=True), 0.0)
                    dck = dck - jnp.where(lane == head, jnp.sum(ds, axis=1, keepdims=True), 0.0)
                dv_s[:, lanes] += dv
                dk_s[:, lanes] += dk
                dqt_ref[lanes, cols] += jnp.where(sub < HD, dqts[0], dqts[1])
            dck_s[...] += dck
            dcq_ref[:, cols] += dcq

        @pl.when(i > j)
        def _():
            step(False)

        @pl.when(i == j)
        def _():
            step(True)

        @pl.when(i == nq - 1)
        def _():
            dk_ref[...] = dk_s[...].astype(BF16)
            dv_ref[...] = dv_s[...].astype(BF16)
            dck_ref[...] = dck_s[...]

    kj = lambda r, t: _fold_cols(r, t, nq)[0]
    qi = lambda r, t: _fold_cols(r, t, nq)[1]
    krow = lambda g, r, t: (kj(r, t), g)
    qrow = lambda g, r, t: (qi(r, t), g)
    return pl.pallas_call(
        body,
        name="attn_bwd",
        grid=_fold_grid(nq),
        in_specs=[
            pl.BlockSpec((tk, PAIRS * 256), krow),
            pl.BlockSpec((tk, PAIRS * 256), krow),
            pl.BlockSpec((PAIRS * 128, tk), lambda g, r, t: (g, kj(r, t))),
            pl.BlockSpec((tq, PAIRS * 512), qrow),
            pl.BlockSpec((tq, PAIRS * 512), qrow),
            pl.BlockSpec(after.shape, lambda g, r, t: (0,) * after.ndim),
        ],
        out_specs=[
            pl.BlockSpec((PAIRS * 128, T), lambda g, r, t: (g, 0), pipeline_mode=pl.Buffered(1)),
            pl.BlockSpec((8, T), lambda g, r, t: (0, 0), pipeline_mode=pl.Buffered(1)),
            pl.BlockSpec((tk, PAIRS * 128), krow),
            pl.BlockSpec((tk, PAIRS * 128), krow),
            pl.BlockSpec((tk, 128), lambda g, r, t: (kj(r, t), 0)),
        ],
        out_shape=[jax.ShapeDtypeStruct((AW, T), F32), jax.ShapeDtypeStruct((8, T), F32),
                   jax.ShapeDtypeStruct((T, AW), BF16), jax.ShapeDtypeStruct((T, AW), BF16),
                   jax.ShapeDtypeStruct((T, FPAD), F32)],
        scratch_shapes=[pltpu.VMEM((tk, PAIRS * 128), F32), pltpu.VMEM((tk, PAIRS * 128), F32),
                        pltpu.VMEM((tk, 128), F32)],
        compiler_params=_params(3),
    )(kk, vk, kt, qw, dow, after)


def _fgate_bwd(dcq, dck, fz, fb):
    T = dck.shape[0]
    tb = _tile(T, 512)
    nb = T // tb

    def body(dcq_ref, dck_ref, fz_ref, fb_ref, df_ref, dfb_ref, carry):
        @pl.when(pl.program_id(0) == 0)
        def _():
            carry[...] = jnp.zeros_like(carry)
            dfb_ref[...] = jnp.zeros_like(dfb_ref)

        head = lax.broadcasted_iota(jnp.int32, (8, FPAD), 0)
        eye = jnp.where(head == lax.broadcasted_iota(jnp.int32, (8, FPAD), 1), 1.0, 0.0)
        dcv = dck_ref[...] + lax.dot_general(dcq_ref[...], eye, (((0,), (0,)), ((), ())),
                                             precision=lax.Precision.HIGHEST, preferred_element_type=F32)
        r = lax.broadcasted_iota(jnp.int32, (128, 128), 0)
        cc = lax.broadcasted_iota(jnp.int32, (128, 128), 1)
        tri = (cc >= r).astype(F32)
        run = carry[...]
        parts = []
        for blk in reversed(range(tb // 128)):
            dc_b = dcv[blk * 128:(blk + 1) * 128, :]
            parts.append(jnp.dot(tri, dc_b, precision=lax.Precision.HIGHEST, preferred_element_type=F32) + run)
            run = run + jnp.sum(dc_b, axis=0, keepdims=True)
        carry[...] = run
        dlf = jnp.concatenate(parts[::-1], axis=0)
        lane = lax.broadcasted_iota(jnp.int32, (tb, FPAD), 1)
        df = jnp.where(lane < 8, dlf * jax.nn.sigmoid(-(fz_ref[...] + fb_ref[...])), 0.0)
        df_ref[...] = df.astype(BF16)
        dfb_ref[...] += jnp.sum(df, axis=0, keepdims=True)

    rev = pl.BlockSpec((tb, FPAD), lambda i: (nb - 1 - i, 0))
    return pl.pallas_call(
        body,
        name="fgate_bwd",
        grid=(nb,),
        in_specs=[pl.BlockSpec((8, tb), lambda i: (0, nb - 1 - i)), rev, rev, _full((1, FPAD))],
        out_specs=[rev, _full((1, FPAD))],
        out_shape=[jax.ShapeDtypeStruct((T, FPAD), BF16), jax.ShapeDtypeStruct((1, FPAD), F32)],
        scratch_shapes=[pltpu.VMEM((1, FPAD), F32)],
        compiler_params=_params(1),
    )(dcq, dck, fz, fb)


def _inproj_bwd(dqt, dk, dv, dus, dfz, wtp, x, dh1, g, adam):
    T = x.shape[0]
    tm = _tile(T, 512)
    adam_in, adam_out, adam_shape = _adam_specs(adam[1], T // tm)

    def body(dq_ref, dk_ref, dv_ref, dus_ref, dfz_ref, w_ref, x_ref, dh1_ref, g_ref, *rest):
        gx_ref, dg_ref = rest[4:6]
        _adam_step(*rest[0:4], *rest[6:10])

        @pl.when(pl.program_id(0) == 0)
        def _():
            dg_ref[...] = jnp.zeros_like(dg_ref)

        da = _tn(dq_ref[...].astype(BF16), w_ref[0:AW, :])
        da += _nn(dk_ref[...], w_ref[AW:2 * AW, :])
        da += _nn(dv_ref[...], w_ref[2 * AW:NQKV, :])
        da += _nn(dus_ref[...], w_ref[NQKV:NQKV + NUS, :])
        da += _nn(dfz_ref[...], w_ref[NQKV + NUS:ZP, :])
        dx, dg = _rms_bwd(da, x_ref[...], g_ref[...])
        gx_ref[...] = dh1_ref[...] + dx
        dg_ref[...] += dg

    row = lambda n: pl.BlockSpec((tm, n), lambda i: (i, 0))
    return pl.pallas_call(
        body,
        name="inproj_bwd",
        grid=(T // tm,),
        in_specs=[pl.BlockSpec((AW, tm), lambda i: (0, i)), row(AW), row(AW), row(NUS), row(FPAD), _full((ZP, D)),
                  row(D), row(D), _full((1, D))] + adam_in,
        out_specs=[row(D), _full((1, D))] + adam_out,
        out_shape=[jax.ShapeDtypeStruct((T, D), F32), jax.ShapeDtypeStruct((1, D), F32)] + adam_shape,
        compiler_params=_params(1),
    )(dqt, dk, dv, dus, dfz, wtp, x, dh1, g, *adam)


def _sq_relu(f1):
    r = jnp.maximum(f1.astype(F32), 0.0)
    return (r * r).astype(BF16)


def _wgrad(a, bs, name, a_fn=None, out_dtype=F32, transposed=False, adam=None):
    T, K = a.shape
    tt = _tile(T, 1024)
    nb = len(bs)
    narrow = out_dtype != F32
    adam_in, adam_out, adam_shape = ([], [], []) if adam is None else _adam_specs(adam[1], T // tt)
    na = len(adam_in)

    def out_dims(b, layout):
        if transposed:
            return (b.shape[0] if layout == "t" else b.shape[1], K)
        if layout == "t":
            return (K, b.shape[0])
        N = b.shape[1]
        if layout is None:
            return (K, N)
        return (N // layout[1], K, layout[1]) if layout[0] == "col" else (K // layout[1], layout[1], N)

    shapes = [out_dims(b, layout) for b, layout in bs]

    def body(*refs):
        a_ref, b_refs, o_refs = refs[0], refs[1:1 + nb], refs[1 + nb + na:1 + 2 * nb + na]
        accs = refs[1 + 2 * nb + 2 * na:] if narrow else o_refs
        if adam is not None:
            _adam_step(*refs[1 + nb:1 + nb + na], *refs[1 + 2 * nb + na:1 + 2 * nb + 2 * na])

        @pl.when(pl.program_id(0) == 0)
        def _():
            for acc in accs:
                acc[...] = jnp.zeros_like(acc)

        av = (a_ref[...] if a_fn is None else a_fn(a_ref[...])).astype(BF16)
        at = None if transposed else av.T
        for (b, layout), b_ref, o_ref in zip(bs, b_refs, accs):
            if transposed:
                bv = b_ref[...].astype(BF16)
                o_ref[...] += _nn(bv, av) if layout == "t" else _tn(bv, av)
            elif layout is None:
                o_ref[...] += _nn(at, b_ref[...].astype(BF16))
            elif layout == "t":
                o_ref[...] += _nt(at, b_ref[...].astype(BF16))
            elif layout[0] == "col":
                n = layout[1]
                for k in range(b.shape[1] // n):
                    o_ref[k] += _nn(at, b_ref[:, k * n:(k + 1) * n].astype(BF16))
            else:
                n = layout[1]
                bv = b_ref[...].astype(BF16)
                for k in range(K // n):
                    o_ref[k] += _nn(at[k * n:(k + 1) * n, :], bv)

        if narrow:
            @pl.when(pl.program_id(0) == T // tt - 1)
            def _():
                for o_ref, acc in zip(o_refs, accs):
                    o_ref[...] = acc[...].astype(out_dtype)

    once = lambda shape: pl.BlockSpec(shape, lambda t: (0,) * len(shape), pipeline_mode=pl.Buffered(1))
    res = pl.pallas_call(
        body,
        name=name,
        grid=(T // tt,),
        in_specs=[pl.BlockSpec((tt, K), lambda t: (t, 0))] + [
            pl.BlockSpec((b.shape[0], tt), lambda t: (0, t)) if layout == "t" else pl.BlockSpec((tt, b.shape[1]), lambda t: (t, 0))
            for b, layout in bs] + adam_in,
        out_specs=[once(s) for s in shapes] + adam_out,
        out_shape=[jax.ShapeDtypeStruct(s, out_dtype) for s in shapes] + adam_shape,
        scratch_shapes=[pltpu.VMEM(s, F32) for s in shapes] if narrow else [],
        compiler_params=_params(1, VMEM_LIMIT_BIG),
    )(a, *[b for b, _ in bs], *(adam or ()))
    return res


def _adam_math(w, g, m, v):
    m = ADAM_B1 * m + (1.0 - ADAM_B1) * g
    v = ADAM_B2 * v + (1.0 - ADAM_B2) * (g * g)
    m_hat = m / (1.0 - ADAM_B1 ** ADAM_STEP)
    v_hat = v / (1.0 - ADAM_B2 ** ADAM_STEP)
    delta = -ADAM_LR * (m_hat / (jnp.sqrt(v_hat) + ADAM_EPS) + ADAM_WD * w)
    return delta, m, v


def _adam_step(p_ref, w_ref, m_ref, v_ref, g_ref, d_ref, nm_ref, nv_ref):
    g = p_ref[0].astype(F32)
    for s in range(1, NDEV):
        g = g + p_ref[s].astype(F32)
    g_ref[...] = g
    d_ref[...], nm_ref[...], nv_ref[...] = _adam_math(w_ref[...], g, m_ref[...], v_ref[...])


def _adam_specs(w, steps):
    R, C = w.shape
    br = R // steps
    assert br * steps == R and (br % 8 == 0 or steps == 1)
    blk = pl.BlockSpec((br, C), lambda i: (i, 0))
    return ([pl.BlockSpec((NDEV, br, C), lambda i: (0, i, 0)), blk, blk, blk], [blk] * 4,
            [jax.ShapeDtypeStruct((R, C), F32)] * 4)


def _adam(parts, w, m, v, name):
    R = w.shape[0]
    steps = R // 128 if R % 128 == 0 else 1
    in_specs, out_specs, out_shape = _adam_specs(w, steps)

    def body(*refs):
        _adam_step(*refs)

    return pl.pallas_call(
        body,
        name=name,
        grid=(steps,),
        in_specs=in_specs,
        out_specs=out_specs,
        out_shape=out_shape,
        compiler_params=_params(1),
    )(parts, w, m, v)


def _adam_unit_rows(parts, w, m, v, name):
    R, _, C = w.shape
    blk = pl.BlockSpec((R, None, C), lambda i: (0, 0, 0))

    def body(*refs):
        _adam_step(*refs)

    return pl.pallas_call(
        body,
        name=name,
        grid=(1,),
        in_specs=[pl.BlockSpec((NDEV, R, C), lambda i: (0, 0, 0)), blk, blk, blk],
        out_specs=[blk] * 4,
        out_shape=[jax.ShapeDtypeStruct((R, 1, C), F32)] * 4,
        compiler_params=_params(1),
    )(parts, w, m, v)


_SMALL =(("sg_w", 8 * CH * CH), ("f_bias", 8), ("sg_ln_g", SW), ("sg_ln_b", SW), ("sg_b", 8 * CH), ("att_out_g", AW),
          ("sg_out_g", SW), ("pre_mix_g", D), ("post_mix_g", D), ("pre_ffn_g", D), ("post_ffn_g", D), ("ple_gate_b", D))
_SEG = 8 * 128


def _seg_rows(size):
    return 8 * (-(-size // _SEG))


_LATE = ("f_bias", "pre_mix_g")


def _pack(vals, late, loss_acc=None):
    parts = []
    for name, size in _SMALL:
        if (name in _LATE) == late:
            rows = _seg_rows(size)
            parts.append(jnp.pad(vals[name].reshape(-1), (0, rows * 128 - size)).reshape(rows, 128))
    if loss_acc is not None:
        parts.append(loss_acc)
    return jnp.concatenate(parts, axis=0)


def _adam_small(parts_early, parts_late, ws, ms, vs):
    n = len(_SMALL)
    names = [name for name, _ in _SMALL]

    def body(*refs):
        pe_ref, pl_ref, refs = refs[0], refs[1], refs[1:]
        w_refs, m_refs, v_refs = refs[1:1 + n], refs[1 + n:1 + 2 * n], refs[1 + 2 * n:1 + 3 * n]
        loss_ref, outs = refs[1 + 3 * n], refs[2 + 3 * n:]
        sums, row = [], [0, 0]
        for p_ref in (pe_ref, pl_ref):
            tot = p_ref[0]
            for s in range(1, NDEV):
                tot = tot + p_ref[s]
            sums.append(tot)
        for k, (name, size) in enumerate(_SMALL):
            dst = [outs[kind * n + k] for kind in range(4)]
            which = 1 if name in _LATE else 0
            g_all, r = sums[which], row[which]

            def update(g, idx):
                vals = (g,) + _adam_math(w_refs[k][idx], g, m_refs[k][idx], v_refs[k][idx])
                for d, val in zip(dst, vals):
                    d[idx] = val

            if name == "sg_w":
                for grp in range(8):
                    update(g_all[r + grp * CH:r + (grp + 1) * CH, :], (0, grp))
            elif name == "sg_b":
                update(g_all[r:r + 8, :], (0,))
            elif name == "f_bias":
                update(g_all[r:r + 1, 0:8], (slice(None),))
            else:
                update(jnp.concatenate([g_all[r + q:r + q + 1, :] for q in range(size // 128)], axis=1), (slice(None),))
            row[which] += _seg_rows(size)
        loss_ref[...] = sums[0][row[0]:row[0] + 1, 0:1] * (0.5 / D)

    arrs = [parts_early, parts_late] + [d[name] for d in (ws, ms, vs) for name in names]
    res = pl.pallas_call(
        body,
        name="adam_small",
        in_specs=[_full(a.shape) for a in arrs],
        out_specs=[_full((1, 1))] + [_full(ws[name].shape) for _ in range(4) for name in names],
        out_shape=[jax.ShapeDtypeStruct((1, 1), F32)] + [jax.ShapeDtypeStruct(ws[name].shape, F32) for _ in range(4) for name in names],
        compiler_params=pltpu.CompilerParams(vmem_limit_bytes=VMEM_LIMIT),
    )(*arrs)
    return res[0], {name: [res[1 + kind * n + k] for kind in range(4)] for k, name in enumerate(names)}


def kernel(x, p, w_in, f_bias, sg_ln_g, sg_ln_b, sg_w, sg_b, att_out_g, sg_out_g, w_out, pre_mix_g, post_mix_g, pre_ffn_g, post_ffn_g, w_ff1, w_ff2, ple_w, ple_gate_w, ple_gate_b, loss_target, m_w_in, m_f_bias, m_sg_ln_g, m_sg_ln_b, m_sg_w, m_sg_b, m_att_out_g, m_sg_out_g, m_w_out, m_pre_mix_g, m_post_mix_g, m_pre_ffn_g, m_post_ffn_g, m_w_ff1, m_w_ff2, m_ple_w, m_ple_gate_w, m_ple_gate_b, v_w_in, v_f_bias, v_sg_ln_g, v_sg_ln_b, v_sg_w, v_sg_b, v_att_out_g, v_sg_out_g, v_w_out, v_pre_mix_g, v_post_mix_g, v_pre_ffn_g, v_post_ffn_g, v_w_ff1, v_w_ff2, v_ple_w, v_ple_gate_w, v_ple_gate_b):
    small_w = dict(sg_w=sg_w, f_bias=f_bias, sg_ln_g=sg_ln_g, sg_ln_b=sg_ln_b, sg_b=sg_b, att_out_g=att_out_g,
                   sg_out_g=sg_out_g, pre_mix_g=pre_mix_g, post_mix_g=post_mix_g, pre_ffn_g=pre_ffn_g,
                   post_ffn_g=post_ffn_g, ple_gate_b=ple_gate_b)
    small_m = dict(sg_w=m_sg_w, f_bias=m_f_bias, sg_ln_g=m_sg_ln_g, sg_ln_b=m_sg_ln_b, sg_b=m_sg_b, att_out_g=m_att_out_g,
                   sg_out_g=m_sg_out_g, pre_mix_g=m_pre_mix_g, post_mix_g=m_post_mix_g, pre_ffn_g=m_pre_ffn_g,
                   post_ffn_g=m_post_ffn_g, ple_gate_b=m_ple_gate_b)
    small_v = dict(sg_w=v_sg_w, f_bias=v_f_bias, sg_ln_g=v_sg_ln_g, sg_ln_b=v_sg_ln_b, sg_b=v_sg_b, att_out_g=v_att_out_g,
                   sg_out_g=v_sg_out_g, pre_mix_g=v_pre_mix_g, post_mix_g=v_post_mix_g, pre_ffn_g=v_pre_ffn_g,
                   post_ffn_g=v_post_ffn_g, ple_gate_b=v_ple_gate_b)
    big = dict(w_in=(w_in, m_w_in, v_w_in), w_out=(w_out, m_w_out, v_w_out), w_ff1=(w_ff1, m_w_ff1, v_w_ff1),
               w_ff2=(w_ff2, m_w_ff2, v_w_ff2), ple_w=(ple_w, m_ple_w, v_ple_w),
               ple_gate_w=(ple_gate_w, m_ple_gate_w, v_ple_gate_w))

    xt, pt, tgt = x[0], p[0, 0], loss_target[0]
    ws = W_IN_COLS // NDEV

    gw_in = _gather(jnp.pad(jnp.transpose(w_in[0]).astype(BF16), ((0, WS_PAD - ws), (0, 0))), "gather_w_in")
    gw_in, others = lax.optimization_barrier((gw_in, [w_out[0].astype(BF16), w_ff1[0].astype(BF16), w_ff2[0].astype(BF16),
                                                      ple_w[0].astype(BF16), ple_gate_w[0].astype(BF16)]))
    rest = _xchg_start(others, True, "gather_rest_start")
    wint = gw_in[:, 0:ws, :].reshape(W_IN_COLS, D)
    wtp = jnp.concatenate([wint[0:NQKV], wint[NQKV + 8:W_IN_COLS], wint[NQKV:NQKV + 8],
                           jnp.zeros((FPAD - 8, D), BF16)], axis=0)

    fb = jnp.pad(f_bias.astype(F32), ((0, 0), (0, FPAD - 8)))
    sgw = sg_w[0]
    sgwt = jnp.transpose(sg_w[0], (0, 2, 1))
    sgbt = jnp.transpose(sg_b[0])

    qkv, kt, vt, us, fz, ab = _inproj(xt, pre_mix_g + rest[4][0, 0], wtp)
    ct, kk, vk = _fcum(fz, fb, qkv)
    yatt, lsec = _attn_fwd(qkv, kk, vt, ct)
    gw_out, gw1, gw2, gwpe, gwg = _xchg_wait(rest, True, yatt, "gather_rest_wait")
    wout = gw_out.reshape(D, D)
    wg = gwg.reshape(D, D)
    wpe = jnp.transpose(gwpe, (1, 0, 2)).reshape(PLE, D)
    h1, yb, o = _sgu_out(us, yatt, xt, sg_ln_g, sg_ln_b, sgw, sgbt, att_out_g, sg_out_g, wout, post_mix_g)
    c2b, f1b, ff, h2 = _ffn_fwd(h1, pre_ffn_g, gw1, gw2, post_ffn_g)
    dh2, dbg, loss_acc, g_g, g_pe = _ple_loss(h2, pt, tgt, wg, ple_gate_b, wpe)
    g_g = g_g.reshape(NDEV, D // NDEV, D)
    g_pe = jnp.transpose(g_pe.reshape(PLE, NDEV, D // NDEV), (1, 0, 2))

    dffb, df1, dh1, dgpostffn, dgpreffn = _ffn_bwd(dh2, ff, h1, f1b, gw1, gw2, post_ffn_g, pre_ffn_g)
    (g_1,) = _wgrad(c2b, [(df1, ("col", DFF // NDEV))], "wgrad_ff1")
    (g_2,) = _wgrad(f1b, [(dffb, ("row", DFF // NDEV))], "wgrad_ff2", a_fn=_sq_relu)
    early = _xchg_start([g_1, g_2, g_pe, g_g], False, "scatter_early_start")
    qw, dow, dus, g_out, dsgw, dsgbt, dlng, dlnb, dgatt, dgsg, dgpostmix = _mix_bwd(
        dh1, o, us, yatt, qkv, lsec, yb, sg_ln_g, sg_ln_b, sgw, sgwt, sgbt, att_out_g, sg_out_g, wout,
        post_mix_g + early[4][0, 0])
    mid = _xchg_start([g_out.reshape(NDEV, D // NDEV, D)], False, "scatter_mid_start")
    small_g = dict(sg_w=dsgw, sg_ln_g=dlng, sg_ln_b=dlnb, sg_b=jnp.transpose(dsgbt[:, 0:8]), att_out_g=dgatt, sg_out_g=dgsg,
                   post_mix_g=dgpostmix, pre_ffn_g=dgpreffn, post_ffn_g=dgpostffn, ple_gate_b=dbg)
    small_early = _xchg_start([_pack(small_g, False, loss_acc)], True, "gather_small_early_start")
    dqt, dcq, dk, dv, dck = _attn_bwd(kk, vk, kt, qw, dow, mid[4] + small_early[4])
    dfz, dfb = _fgate_bwd(dcq, dck, fz, fb)

    r_1, r_2, r_pe, r_g = _xchg_wait(early, False, dfz, "scatter_early_wait")
    gq, gk, gv, gus, gf, *adam_1 = _wgrad(ab, [(dqt, "t"), (dk, None), (dv, None), (dus, None), (dfz, None)], "wgrad_in",
                                          out_dtype=BF16, transposed=True, adam=(r_1, *[t[0] for t in big["w_ff1"]]))
    g_in = jnp.concatenate([gq, gk, gv, gf[0:8], gus], axis=0).reshape(NDEV, ws, D)
    late = _xchg_start([g_in], False, "scatter_late_start")
    grad_x, dgpremix, *adam_2 = _inproj_bwd(dqt, dk, dv, dus, dfz, wtp, xt, dh1, pre_mix_g + late[4][0, 0],
                                            (r_2, *[t[0] for t in big["w_ff2"]]))

    (r_out,) = _xchg_wait(mid, False, grad_x, "scatter_mid_wait")
    (r_early,) = _xchg_wait(small_early, True, grad_x, "gather_small_early_wait")
    r_late = _gather(_pack(dict(f_bias=dfb[:, 0:8], pre_mix_g=dgpremix), True), "gather_small_late")

    res = dict(w_ff1=[t[None] for t in adam_1], w_ff2=[t[None] for t in adam_2])

    def adam_big(name, parts):
        w, m, v = big[name]
        res[name] = [t[None] for t in _adam(parts, w[0], m[0], v[0], "adam_" + name)]

    for name, parts in (("w_out", r_out), ("ple_w", r_pe), ("ple_gate_w", r_g)):
        adam_big(name, parts)
    (r_in,) = _xchg_wait(late, False, res["ple_gate_w"][0], "scatter_late_wait")
    res["w_in"] = [jnp.transpose(t, (1, 2, 0)) for t in
                   _adam_unit_rows(r_in, *[jnp.transpose(t, (2, 0, 1)) for t in big["w_in"]], "adam_w_in")]
    loss, small = _adam_small(r_early, r_late, small_w, small_m, small_v)
    res.update(small)

    order = ["w_in", "f_bias", "sg_ln_g", "sg_ln_b", "sg_w", "sg_b", "att_out_g", "sg_out_g", "w_out", "pre_mix_g",
             "post_mix_g", "pre_ffn_g", "post_ffn_g", "w_ff1", "w_ff2", "ple_w", "ple_gate_w", "ple_gate_b"]
    outs = [loss[0, 0], grad_x[None]]
    for kind in range(4):
        outs += [res[name][kind] for name in order]
    return tuple(outs)
```

```python
import jax
import jax.numpy as jnp
from jax import lax
from jax.experimental import pallas as pl
from jax.experimental.pallas import tpu as pltpu

F32 = jnp.float32
BF16 = jnp.bfloat16

NDEV = 8
D = 1024
AW = 512
SW = 512
HD = 64
CH = 128
DFF = 4096
PLE = 256
NQKV = 3 * AW
NUS = 2 * SW
FPAD = 128
ZP = NQKV + NUS + FPAD
W_IN_COLS = 2568
WS_PAD = 336
EPS = 1e-6
MASKV = -1e30
GELU_K = 0.7978845608028654
GELU_C = 0.044715

ADAM_LR = 0.001
ADAM_B1 = 0.9
ADAM_B2 = 0.999
ADAM_EPS = 1e-08
ADAM_WD = 0.01
ADAM_STEP = 10

VMEM_LIMIT = 48 * 1024 * 1024
VMEM_LIMIT_BIG = 60 * 1024 * 1024


def _nn(a, b):
    return jnp.dot(a, b, preferred_element_type=F32)


def _nt(a, b):
    return lax.dot_general(a, b, (((1,), (1,)), ((), ())), preferred_element_type=F32)


def _tn(a, b):
    return lax.dot_general(a, b, (((0,), (0,)), ((), ())), preferred_element_type=F32)


def _tile(n, pref):
    return min(n, pref)


def _params(n_axes, vmem=VMEM_LIMIT):
    return pltpu.CompilerParams(dimension_semantics=("arbitrary",) * n_axes, vmem_limit_bytes=vmem)


def _full(shape):
    nd = len(shape)
    return pl.BlockSpec(shape, lambda *_: (0,) * nd)


def _rms_fwd(x, g):
    r = lax.rsqrt(jnp.mean(x * x, axis=-1, keepdims=True) + EPS)
    return x * r * g


def _rms_bwd(dy, x, g):
    n = x.shape[-1]
    r = lax.rsqrt(jnp.mean(x * x, axis=-1, keepdims=True) + EPS)
    u = dy * g
    s = jnp.sum(x * u, axis=-1, keepdims=True)
    dx = r * u - x * (r * r * r * (s * (1.0 / n)))
    dg = jnp.sum(dy * (x * r), axis=0, keepdims=True)
    return dx, dg


def _gelu(x):
    t = jnp.tanh(x * (GELU_K + (GELU_K * GELU_C) * (x * x)))
    return x * (0.5 + 0.5 * t), t


def _gelu_grad(x, t):
    return (0.5 + 0.5 * t) + (0.5 * x) * (1.0 - t * t) * (GELU_K + (3.0 * GELU_K * GELU_C) * (x * x))


def _gather(arr, name):
    def body(x_ref, out_ref, send, recv, loc):
        x, y, c = lax.axis_index("x"), lax.axis_index("y"), lax.axis_index("c")
        me, sibling = (x, y, c), (x, y, 1 - c)
        chips = [(1 - x, y), (x, 1 - y), (1 - x, 1 - y)]

        def slot(px, py, pc):
            return out_ref.at[4 * px + 2 * py + pc]

        def copy(k, block, to, src=None):
            return pltpu.make_async_remote_copy(
                src_ref=slot(*block) if src is None else src, dst_ref=slot(*block), send_sem=send.at[k],
                recv_sem=recv.at[k], device_id=to, device_id_type=pl.DeviceIdType.MESH)

        mine = pltpu.make_async_copy(x_ref, slot(*me), loc)
        mine.start()
        first = [copy(0, me, sibling, src=x_ref)] + [copy(1 + j, me, (*chip, c), src=x_ref) for j, chip in enumerate(chips)]
        for cp in first:
            cp.start()
        passed = [copy(4 + j, (*chip, c), sibling) for j, chip in enumerate(chips)]
        for j, chip in enumerate(chips):
            copy(1 + j, (*chip, c), me).wait_recv()
            passed[j].start()
        copy(0, sibling, me).wait_recv()
        for j, chip in enumerate(chips):
            copy(4 + j, (*chip, 1 - c), me).wait_recv()
        for cp in first + passed:
            cp.wait_send()
        mine.wait()

    hbm = pl.BlockSpec(memory_space=pltpu.HBM)
    return pl.pallas_call(
        body,
        name=name,
        out_shape=jax.ShapeDtypeStruct((NDEV,) + arr.shape, arr.dtype),
        in_specs=[hbm],
        out_specs=hbm,
        scratch_shapes=[pltpu.SemaphoreType.DMA((NDEV - 1,)), pltpu.SemaphoreType.DMA((NDEV - 1,)), pltpu.SemaphoreType.DMA],
    )(arr)


def _peers(x, y, c):
    out = []
    for k in range(1, NDEV):
        out.append((1 - x if (k >> 2) & 1 else x, 1 - y if (k >> 1) & 1 else y, 1 - c if k & 1 else c))
    return out


def _xchg_start(arrs, gather, name):
    n = len(arrs)
    me = 4 * lax.axis_index("x") + 2 * lax.axis_index("y") + lax.axis_index("c")
    lands = []
    for a in arrs:
        shape = ((NDEV,) + a.shape) if gather else a.shape
        own = a[None] if gather else lax.dynamic_slice_in_dim(a, me, 1, axis=0)
        lands.append(lax.dynamic_update_slice_in_dim(lax.empty(shape, a.dtype), own, me, axis=0))

    def body(*refs):
        ins, lnd = refs[:n], refs[n:2 * n]
        send, recv, token = refs[2 * n:3 * n], refs[3 * n:4 * n], refs[-1]
        x, y, c = lax.axis_index("x"), lax.axis_index("y"), lax.axis_index("c")
        mine = 4 * x + 2 * y + c
        for px, py, pc in _peers(x, y, c):
            peer = 4 * px + 2 * py + pc
            for a in range(n):
                pltpu.make_async_remote_copy(
                    src_ref=ins[a] if gather else ins[a].at[peer],
                    dst_ref=lnd[a].at[mine],
                    send_sem=send[a],
                    recv_sem=recv[a],
                    device_id=(px, py, pc),
                    device_id_type=pl.DeviceIdType.MESH,
                ).start()
        token[...] = jnp.zeros_like(token)

    hbm = pl.BlockSpec(memory_space=pltpu.HBM)
    sem = pl.BlockSpec(memory_space=pltpu.SEMAPHORE)
    res = pl.pallas_call(
        body,
        name=name,
        out_shape=(*[pltpu.SemaphoreType.DMA(())] * (2 * n),
                   *[pltpu.HBM(a.shape, a.dtype) for a in arrs], *[pltpu.HBM(l.shape, l.dtype) for l in lands],
                   jax.ShapeDtypeStruct((8, 128), F32)),
        in_specs=[hbm] * (2 * n),
        out_specs=(*([sem] * (2 * n)), *([hbm] * (2 * n)), pl.BlockSpec(memory_space=pltpu.VMEM)),
        input_output_aliases={i: 2 * n + i for i in range(2 * n)},
        compiler_params=pltpu.CompilerParams(has_side_effects=pltpu.SideEffectType.DATAFLOW_SIDE_EFFECTING),
    )(*[pltpu.with_memory_space_constraint(a, pltpu.HBM) for a in arrs],
      *[pltpu.with_memory_space_constraint(l, pltpu.HBM) for l in lands])
    return list(res[0:n]), list(res[n:2 * n]), list(res[2 * n:3 * n]), list(res[3 * n:4 * n]), res[-1]


def _xchg_wait(started, gather, after, name):
    send, recv, srcs, lands, _ = started
    n = len(srcs)

    def body(*refs):
        lnd = refs[n:2 * n]
        send, recv = refs[2 * n:3 * n], refs[3 * n:4 * n]
        me = (lax.axis_index("x"), lax.axis_index("y"), lax.axis_index("c"))
        for a in range(n):
            seven = lnd[a].at[pl.ds(0, NDEV - 1)]
            cp = pltpu.make_async_remote_copy(src_ref=seven, dst_ref=seven, send_sem=send[a], recv_sem=recv[a],
                                              device_id=me, device_id_type=pl.DeviceIdType.MESH)
            cp.wait_send()
            cp.wait_recv()

    hbm = pl.BlockSpec(memory_space=pltpu.HBM)
    sem = pl.BlockSpec(memory_space=pltpu.SEMAPHORE)
    res = pl.pallas_call(
        body,
        name=name,
        out_shape=tuple([pltpu.HBM(a.shape, a.dtype) for a in srcs] + [pltpu.HBM(l.shape, l.dtype) for l in lands]),
        in_specs=[hbm] * (2 * n) + [sem] * (2 * n) + [pl.BlockSpec(memory_space=pl.ANY)],
        out_specs=tuple([hbm] * (2 * n)),
        input_output_aliases={i: i for i in range(2 * n)},
        compiler_params=pltpu.CompilerParams(has_side_effects=pltpu.SideEffectType.DATAFLOW_SIDE_EFFECTING),
    )(*srcs, *lands, *send, *recv, after)
    return list(res[n:])


def _inproj(x, g, wtp):
    T = x.shape[0]
    tm = _tile(T, 512)

    def body(x_ref, g_ref, w_ref, qkv_ref, kt_ref, vt_ref, us_ref, fz_ref, ab_ref):
        a = _rms_fwd(x_ref[...], g_ref[...]).astype(BF16)
        ab_ref[...] = a
        qkv_ref[:, 0:AW] = _nt(a, w_ref[0:AW, :]).astype(BF16)
        kk = _nt(a, w_ref[AW:2 * AW, :])
        qkv_ref[:, AW:2 * AW] = kk.astype(BF16)
        kt_ref[...] = kk.T.astype(BF16)
        vv = _nt(a, w_ref[2 * AW:NQKV, :])
        qkv_ref[:, 2 * AW:NQKV] = vv.astype(BF16)
        vvt = vv.T.astype(BF16)
        one_row = jnp.where(lax.broadcasted_iota(jnp.int32, (HD, tm), 0) == 0, 1.0, 0.0).astype(BF16)
        for h in range(AW // HD):
            vt_ref[2 * h * HD:(2 * h + 1) * HD, :] = vvt[h * HD:(h + 1) * HD, :]
            vt_ref[(2 * h + 1) * HD:(2 * h + 2) * HD, :] = one_row
        us_ref[...] = _nt(a, w_ref[NQKV:NQKV + NUS, :])
        fz_ref[...] = _nt(a, w_ref[NQKV + NUS:ZP, :])

    row = lambda n: pl.BlockSpec((tm, n), lambda i: (i, 0))
    col = lambda n: pl.BlockSpec((n, tm), lambda i: (0, i))
    return pl.pallas_call(
        body,
        name="inproj",
        grid=(T // tm,),
        in_specs=[row(D), _full((1, D)), _full((ZP, D))],
        out_specs=[row(NQKV), col(AW), col(2 * AW), row(NUS), row(FPAD), row(D)],
        out_shape=[
            jax.ShapeDtypeStruct((T, NQKV), BF16),
            jax.ShapeDtypeStruct((AW, T), BF16),
            jax.ShapeDtypeStruct((2 * AW, T), BF16),
            jax.ShapeDtypeStruct((T, NUS), F32),
            jax.ShapeDtypeStruct((T, FPAD), F32),
            jax.ShapeDtypeStruct((T, D), BF16),
        ],
        compiler_params=_params(1),
    )(x, g, wtp)


def _log_sigmoid(z):
    return jnp.minimum(z, 0.0) - jnp.log1p(jnp.exp(-jnp.abs(z)))


def _split3(x):
    hi = x.astype(BF16)
    r1 = x - hi.astype(F32)
    mid = r1.astype(BF16)
    lo = (r1 - mid.astype(F32)).astype(BF16)
    return hi, mid, lo


AUG_A, AUG_B, AUG_ONE = 0, 3, 6


def _aug_lanes(rows, first_one, pieces):
    lane = lax.broadcasted_iota(jnp.int32, (rows, 128), 1)
    out = jnp.zeros((rows, 128), F32)
    if first_one is not None:
        out = jnp.where((lane >= first_one) & (lane < first_one + 3), 1.0, out)
    for n, piece in enumerate(pieces):
        out = jnp.where(lane == AUG_ONE + n, piece.astype(F32), out)
    return out.astype(BF16)


def _fcum(fz, fb, qkv):
    T = fz.shape[0]
    tb = _tile(T, 512)

    def body(fz_ref, fb_ref, k_ref, v_ref, ct_ref, kk_ref, vk_ref, carry, tri_s, sel_s):
        @pl.when(pl.program_id(0) == 0)
        def _():
            carry[...] = jnp.zeros_like(carry)
            src = lax.broadcasted_iota(jnp.int32, (128, 128), 0)
            dst = lax.broadcasted_iota(jnp.int32, (128, 128), 1)
            tri_s[...] = (dst <= src).astype(F32)
            for hp in range(4):
                for n in range(3):
                    pick = ((src == 2 * hp) & (dst == AUG_A + n)) | ((src == 2 * hp + 1) & (dst == AUG_B + n))
                    sel_s[3 * hp + n] = jnp.where(pick, -1.0, 0.0).astype(BF16)

        lf = _log_sigmoid(fz_ref[...] + fb_ref[...])
        run = carry[...]
        parts = []
        for blk in range(tb // 128):
            lf_b = lf[blk * 128:(blk + 1) * 128, :]
            parts.append(jnp.dot(tri_s[...], lf_b, precision=lax.Precision.HIGHEST, preferred_element_type=F32) + run)
            run = run + jnp.sum(lf_b, axis=0, keepdims=True)
        carry[...] = run
        cs = jnp.concatenate(parts, axis=0)
        ct_ref[...] = cs.T[0:8, :]

        pieces = _split3(cs)
        lane = lax.broadcasted_iota(jnp.int32, (1, 128), 1)
        ones = jnp.where((lane >= AUG_ONE) & (lane < AUG_ONE + 3), 1.0, 0.0)
        for hp in range(4):
            aug = jnp.zeros((tb, 128), F32) + ones
            for n, piece in enumerate(pieces):
                aug = aug + _nn(piece, sel_s[3 * hp + n])
            aug = aug.astype(BF16)
            kk_ref[:, hp * 256:hp * 256 + 128] = k_ref[:, hp * 128:(hp + 1) * 128]
            kk_ref[:, hp * 256 + 128:(hp + 1) * 256] = aug
            vk_ref[:, hp * 256:hp * 256 + 128] = v_ref[:, hp * 128:(hp + 1) * 128]
            vk_ref[:, hp * 256 + 128:(hp + 1) * 256] = aug

    row = lambda n, c: pl.BlockSpec((tb, n), lambda i: (i, c))
    return pl.pallas_call(
        body,
        name="fcum",
        grid=(T // tb,),
        in_specs=[row(FPAD, 0), _full((1, FPAD)), row(AW, 1), row(AW, 2)],
        out_specs=[pl.BlockSpec((8, tb), lambda i: (0, i)), row(2 * AW, 0), row(2 * AW, 0)],
        out_shape=[jax.ShapeDtypeStruct((8, T), F32), jax.ShapeDtypeStruct((T, 2 * AW), BF16),
                   jax.ShapeDtypeStruct((T, 2 * AW), BF16)],
        scratch_shapes=[pltpu.VMEM((1, FPAD), F32), pltpu.VMEM((128, 128), F32), pltpu.VMEM((12, 128, 128), BF16)],
        compiler_params=_params(1),
    )(fz, fb, qkv, qkv)


def _fold_rows(r, t, nq):
    if nq == 1:
        return r, t
    low = t <= r
    return jnp.where(low, r, nq - 1 - r), jnp.where(low, t, t - r - 1)


def _fold_cols(r, t, nq):
    if nq == 1:
        return r, t
    first = t < nq - r
    j = jnp.where(first, r, nq - 1 - r)
    return j, jnp.where(first, r + t, nq - 1 - r + (t - (nq - r)))


PAIRS = 4


def _fold_grid(nq):
    assert (nq == 1 or nq % 2 == 0) and PAIRS == 4
    return (4 // PAIRS, 1, 1) if nq == 1 else (4 // PAIRS, nq // 2, nq + 1)


def _head_rows(x2, hh, scale):
    is_a = lax.broadcasted_iota(jnp.int32, (1, 128), 1) < HD
    keep = is_a if hh == 0 else jnp.logical_not(is_a)
    return jnp.where(keep, x2, jnp.zeros_like(x2)) * scale


def _attn_fwd(qkv, kk, vt, ct):
    T = qkv.shape[0]
    tq = _tile(T, 512)
    tk = tq
    nq = T // tq

    def body(q_ref, kk_ref, vt_ref, ctq_ref, o_ref, lsec_ref, qw_s, m_s, acc_s):
        i, j = _fold_rows(pl.program_id(1), pl.program_id(2), nq)
        sub8 = lax.broadcasted_iota(jnp.int32, (8, 1), 0)

        def cref_of(pp, hh):
            head = 2 * (PAIRS * pl.program_id(0) + pp) + hh
            return jnp.sum(jnp.where(sub8 == head, ctq_ref[:, 0:1], 0.0), axis=0, keepdims=True)

        @pl.when(j == 0)
        def _():
            for pp in range(PAIRS):
                q2 = q_ref[:, pp * 128:(pp + 1) * 128]
                for hh in range(2):
                    rows = slice(hh * tq, (hh + 1) * tq)
                    qw_s[pp, rows, 0:128] = _head_rows(q2, hh, 0.125)
                    qw_s[pp, rows, 128:256] = _aug_lanes(tq, AUG_A if hh == 0 else AUG_B, _split3(cref_of(pp, hh)))
            m_s[...] = jnp.full_like(m_s, MASKV)
            acc_s[...] = jnp.zeros_like(acc_s)

        def step(masked):
            if masked:
                causal = lax.broadcasted_iota(jnp.int32, (tk, tq), 0) <= lax.broadcasted_iota(jnp.int32, (tk, tq), 1)
            logits = lambda pp: _nt(kk_ref[:, pp * 256:(pp + 1) * 256], qw_s[pp])
            s_next = logits(0)
            for pp in range(PAIRS):
                s2 = s_next
                if pp + 1 < PAIRS:
                    s_next = logits(pp + 1)
                for hh in range(2):
                    n = 2 * pp + hh
                    s = s2[:, hh * tq:(hh + 1) * tq]
                    if masked:
                        s = jnp.where(causal, s, MASKV)
                    m_prev = m_s[n]
                    m_new = jnp.maximum(m_prev, jnp.max(s, axis=0, keepdims=True))
                    pr = jnp.exp(s - m_new)
                    m_s[n] = m_new
                    acc_s[n] = jnp.exp(m_prev - m_new) * acc_s[n] + _nn(vt_ref[n * 128:(n + 1) * 128, :], pr.astype(BF16))

        @pl.when(j < i)
        def _():
            step(False)

        @pl.when(j == i)
        def _():
            step(True)
            sub = lax.broadcasted_iota(jnp.int32, (128, 1), 0)
            for pp in range(PAIRS):
                outs, lrow = [], []
                for hh in range(2):
                    n = 2 * pp + hh
                    den = acc_s[n, HD:HD + 1, :]
                    outs.append(acc_s[n, 0:HD, :] * (1.0 / den))
                    lrow.append(m_s[n] + jnp.log(den) - cref_of(pp, hh))
                o_ref[:, pp * 128:(pp + 1) * 128] = jnp.concatenate(outs, axis=0).T
                lsec_ref[:, pp * 128:(pp + 1) * 128] = jnp.where(sub == 0, lrow[0], jnp.where(sub == 1, lrow[1], 0.0)).T

    qi = lambda r, t: _fold_rows(r, t, nq)[0]
    kj = lambda r, t: _fold_rows(r, t, nq)[1]
    return pl.pallas_call(
        body,
        name="attn_fwd",
        grid=_fold_grid(nq),
        in_specs=[
            pl.BlockSpec((tq, PAIRS * 128), lambda g, r, t: (qi(r, t), g)),
            pl.BlockSpec((tk, PAIRS * 256), lambda g, r, t: (kj(r, t), g)),
            pl.BlockSpec((PAIRS * 256, tk), lambda g, r, t: (g, kj(r, t))),
            pl.BlockSpec((8, tq), lambda g, r, t: (0, qi(r, t))),
        ],
        out_specs=[pl.BlockSpec((tq, PAIRS * 128), lambda g, r, t: (qi(r, t), g))] * 2,
        out_shape=[jax.ShapeDtypeStruct((T, AW), F32)] * 2,
        scratch_shapes=[pltpu.VMEM((PAIRS, 2 * tq, 256), BF16), pltpu.VMEM((2 * PAIRS, 1, tq), F32),
                        pltpu.VMEM((2 * PAIRS, 128, tq), F32)],
        compiler_params=_params(3),
    )(qkv, kk, vt, ct)


def _sgu_forward(us_ref, lng, lnb, w_ref, bt_ref, mixed_s, vnb_s, tm):
    is_a = lax.broadcasted_iota(jnp.int32, (1, 128), 1) < HD
    u = us_ref[:, 0:SW]
    vs = us_ref[:, SW:NUS]
    ug, tu = _gelu(u)
    vg, tv = _gelu(vs)
    mu = jnp.mean(vg, axis=-1, keepdims=True)
    xc = vg - mu
    rstd = lax.rsqrt(jnp.mean(xc * xc, axis=-1, keepdims=True) + EPS)
    vhat = xc * rstd
    vnb_s[...] = (vhat * lng + lnb).astype(BF16)
    rr = lax.broadcasted_iota(jnp.int32, (CH, CH), 0)
    cc = lax.broadcasted_iota(jnp.int32, (CH, CH), 1)
    tril = cc <= rr
    for jj in range(4):
        wa = jnp.where(tril, w_ref[2 * jj], 0.0).astype(BF16)
        wb = jnp.where(tril, w_ref[2 * jj + 1], 0.0).astype(BF16)
        ba = bt_ref[:, 2 * jj:2 * jj + 1]
        bb = bt_ref[:, 2 * jj + 1:2 * jj + 2]
        for ch in range(tm // CH):
            rs, cs = slice(ch * CH, (ch + 1) * CH), slice(jj * 128, (jj + 1) * 128)
            vn2 = vnb_s[rs, cs]
            mixed_s[rs, cs] = jnp.where(is_a, _nn(wa, vn2) + ba, _nn(wb, vn2) + bb)
    mixed = mixed_s[...]
    return u, vs, ug, tu, tv, vhat, rstd, mixed, ug * mixed


def _sgu_out(us, yatt, x, lng, lnb, sgw, sgbt, gatt, gsg, wout, gpm):
    T = us.shape[0]
    tm = _tile(T, 512)

    def body(us_ref, ya_ref, x_ref, lng_ref, lnb_ref, w_ref, bt_ref, ga_ref, gs_ref, wo_ref, gp_ref,
             h1_ref, yb_ref, o_ref, mixed_s, vnb_s):
        ysg = _sgu_forward(us_ref, lng_ref[...], lnb_ref[...], w_ref, bt_ref, mixed_s, vnb_s, tm)[-1]
        yb_ref[:, 0:AW] = _rms_fwd(ya_ref[...], ga_ref[...]).astype(BF16)
        yb_ref[:, AW:D] = _rms_fwd(ysg, gs_ref[...]).astype(BF16)
        o = _nn(yb_ref[...], wo_ref[...])
        o_ref[...] = o
        h1_ref[...] = x_ref[...] + _rms_fwd(o, gp_ref[...])

    row = lambda n: pl.BlockSpec((tm, n), lambda i: (i, 0))
    return pl.pallas_call(
        body,
        name="sgu_out",
        grid=(T // tm,),
        in_specs=[row(NUS), row(AW), row(D), _full((1, SW)), _full((1, SW)), _full((8, CH, CH)), _full((CH, 8)),
                  _full((1, AW)), _full((1, SW)), _full((D, D)), _full((1, D))],
        out_specs=[row(D), row(D), row(D)],
        out_shape=[jax.ShapeDtypeStruct((T, D), F32), jax.ShapeDtypeStruct((T, D), BF16), jax.ShapeDtypeStruct((T, D), F32)],
        scratch_shapes=[pltpu.VMEM((tm, SW), F32), pltpu.VMEM((tm, SW), BF16)],
        compiler_params=_params(1),
    )(us, yatt, x, lng, lnb, sgw, sgbt, gatt, gsg, wout, gpm)


def _ffn_fwd(h1, gpre, w1g, w2g, gpost):
    T = h1.shape[0]
    tm = _tile(T, 512)
    nb, hb = w1g.shape[0], w1g.shape[2]

    def body(h1_ref, gpre_ref, w1_ref, w2_ref, gpost_ref, c2_ref, f1_ref, ff_ref, h2_ref):
        h1 = h1_ref[...]
        c2 = _rms_fwd(h1, gpre_ref[...]).astype(BF16)
        c2_ref[...] = c2
        ff = jnp.zeros((tm, D), F32)
        for j in range(nb):
            f1 = _nn(c2, w1_ref[j])
            f1_ref[:, j * hb:(j + 1) * hb] = f1.astype(BF16)
            r = jnp.maximum(f1, 0.0)
            ff = ff + _nn((r * r).astype(BF16), w2_ref[j])
        ff_ref[...] = ff
        h2_ref[...] = h1 + _rms_fwd(ff, gpost_ref[...])

    row = lambda n: pl.BlockSpec((tm, n), lambda i: (i, 0))
    once = lambda shape: pl.BlockSpec(shape, lambda i: (0,) * len(shape), pipeline_mode=pl.Buffered(1))
    return pl.pallas_call(
        body,
        name="ffn_fwd",
        grid=(T // tm,),
        in_specs=[row(D), _full((1, D)), once((nb, D, hb)), once((nb, hb, D)), _full((1, D))],
        out_specs=[row(D), row(DFF), row(D), row(D)],
        out_shape=[jax.ShapeDtypeStruct((T, D), BF16), jax.ShapeDtypeStruct((T, DFF), BF16),
                   jax.ShapeDtypeStruct((T, D), F32), jax.ShapeDtypeStruct((T, D), F32)],
        compiler_params=_params(1, VMEM_LIMIT_BIG),
    )(h1, gpre, w1g, w2g, gpost)


def _ple_loss(h2, p, tgt, wg, bg, wpe):
    T = h2.shape[0]
    tm = _tile(T, 512)

    def body(h2_ref, p_ref, t_ref, wg_ref, bg_ref, wpe_ref, dh2_ref, dbg_ref, loss_ref, dwg_ref, dwpe_ref):
        @pl.when(pl.program_id(0) == 0)
        def _():
            for r in (dbg_ref, loss_ref, dwg_ref, dwpe_ref):
                r[...] = jnp.zeros_like(r)

        h2 = h2_ref[...]
        h2b = h2.astype(BF16)
        gate = jax.nn.sigmoid(_nn(h2b, wg_ref[...]) + bg_ref[...])
        pb = p_ref[...].astype(BF16)
        pe = _nn(pb, wpe_ref[...])
        diff = (h2 + gate * pe) - t_ref[...]
        loss_ref[...] += jnp.sum(diff * diff)
        dh3 = diff * (1.0 / D)
        dpre = (dh3 * pe) * (gate * (1.0 - gate))
        dpre_b = dpre.astype(BF16)
        dbg_ref[...] += jnp.sum(dpre, axis=0, keepdims=True)
        dh2_ref[...] = dh3 + _nt(dpre_b, wg_ref[...])
        dwg_ref[...] += _nn(h2b.T, dpre_b)
        dwpe_ref[...] += _nn(pb.T, (dh3 * gate).astype(BF16))

    row = lambda n: pl.BlockSpec((tm, n), lambda i: (i, 0))
    once = lambda shape: pl.BlockSpec(shape, lambda i: (0,) * len(shape), pipeline_mode=pl.Buffered(1))
    return pl.pallas_call(
        body,
        name="ple_loss",
        grid=(T // tm,),
        in_specs=[row(D), row(PLE), row(D), _full((D, D)), _full((1, D)), _full((PLE, D))],
        out_specs=[row(D), _full((1, D)), _full((8, 128)), once((D, D)), once((PLE, D))],
        out_shape=[jax.ShapeDtypeStruct((T, D), F32), jax.ShapeDtypeStruct((1, D), F32),
                   jax.ShapeDtypeStruct((8, 128), F32), jax.ShapeDtypeStruct((D, D), F32),
                   jax.ShapeDtypeStruct((PLE, D), F32)],
        compiler_params=_params(1),
    )(h2, p, tgt, wg, bg, wpe)


def _ffn_bwd(dh2, ff, h1, f1, w1g, w2g, gpost, gpre):
    T = dh2.shape[0]
    tm = _tile(T, 512)
    nb, hb = w1g.shape[0], w1g.shape[2]

    def body(dh2_ref, ff_ref, h1_ref, f1_ref, w1_ref, w2_ref, gpost_ref, gpre_ref,
             dffb_ref, df1_ref, dh1_ref, dgpost_ref, dgpre_ref):
        @pl.when(pl.program_id(0) == 0)
        def _():
            dgpost_ref[...] = jnp.zeros_like(dgpost_ref)
            dgpre_ref[...] = jnp.zeros_like(dgpre_ref)

        dh2 = dh2_ref[...]
        dff, dg = _rms_bwd(dh2, ff_ref[...], gpost_ref[...])
        dffb = dff.astype(BF16)
        dffb_ref[...] = dffb
        dgpost_ref[...] += dg
        dc2 = jnp.zeros((tm, D), F32)
        for j in range(nb):
            cols = slice(j * hb, (j + 1) * hb)
            dact = _nt(dffb, w2_ref[j])
            df1 = (dact * (2.0 * jnp.maximum(f1_ref[:, cols].astype(F32), 0.0))).astype(BF16)
            df1_ref[:, cols] = df1
            dc2 = dc2 + _nt(df1, w1_ref[j])
        dx, dg = _rms_bwd(dc2, h1_ref[...], gpre_ref[...])
        dh1_ref[...] = dh2 + dx
        dgpre_ref[...] += dg

    row = lambda n: pl.BlockSpec((tm, n), lambda i: (i, 0))
    once = lambda shape: pl.BlockSpec(shape, lambda i: (0,) * len(shape), pipeline_mode=pl.Buffered(1))
    return pl.pallas_call(
        body,
        name="ffn_bwd",
        grid=(T // tm,),
        in_specs=[row(D), row(D), row(D), row(DFF), once((nb, D, hb)), once((nb, hb, D)), _full((1, D)), _full((1, D))],
        out_specs=[row(D), row(DFF), row(D), _full((1, D)), _full((1, D))],
        out_shape=[jax.ShapeDtypeStruct((T, D), BF16), jax.ShapeDtypeStruct((T, DFF), BF16),
                   jax.ShapeDtypeStruct((T, D), F32), jax.ShapeDtypeStruct((1, D), F32),
                   jax.ShapeDtypeStruct((1, D), F32)],
        compiler_params=_params(1, VMEM_LIMIT_BIG),
    )(dh2, ff, h1, f1, w1g, w2g, gpost, gpre)


def _mix_bwd(dh1, o, us, yatt, qkv, lsec, yb, lng, lnb, sgw, sgwt, sgbt, gatt, gsg, wout, gpm):
    T = dh1.shape[0]
    tm = _tile(T, 512)

    def body(dh1_ref, o_ref, us_ref, ya_ref, q_ref, l_ref, yb_ref, lng_ref, lnb_ref, w_ref, wt_ref, bt_ref, ga_ref, gs_ref,
             wo_ref, gp_ref, qw_ref, dow_ref, dus_ref, dwo_ref, dw_ref, dbt_ref, dlng_ref, dlnb_ref, dga_ref, dgs_ref,
             dgp_ref, mixed_s, vnb_s, dvn_s):
        @pl.when(pl.program_id(0) == 0)
        def _():
            for r in (dwo_ref, dw_ref, dbt_ref, dlng_ref, dlnb_ref, dga_ref, dgs_ref, dgp_ref):
                r[...] = jnp.zeros_like(r)

        is_a = lax.broadcasted_iota(jnp.int32, (1, 128), 1) < HD
        lane = lax.broadcasted_iota(jnp.int32, (1, 128), 1)
        do, dg = _rms_bwd(dh1_ref[...], o_ref[...], gp_ref[...])
        dgp_ref[...] += dg
        dob = do.astype(BF16)
        dwo_ref[...] += _nn(yb_ref[...].T, dob)
        dy = _nt(dob, wo_ref[...])
        ya = ya_ref[...]
        datt, dg = _rms_bwd(dy[:, 0:AW], ya, ga_ref[...])
        dga_ref[...] += dg
        _attn_operands(q_ref[...], datt, ya, l_ref[...], qw_ref, dow_ref)

        lng = lng_ref[...]
        u, vs, ug, tu, tv, vhat, rstd, mixed, ysg = _sgu_forward(us_ref, lng, lnb_ref[...], w_ref, bt_ref, mixed_s, vnb_s, tm)
        dysg, dg = _rms_bwd(dy[:, AW:D], ysg, gs_ref[...])
        dgs_ref[...] += dg
        dus_ref[:, 0:SW] = ((dysg * mixed) * _gelu_grad(u, tu)).astype(BF16)
        dmix = dysg * ug

        rr = lax.broadcasted_iota(jnp.int32, (CH, CH), 0)
        cc = lax.broadcasted_iota(jnp.int32, (CH, CH), 1)
        tril = cc <= rr
        triu = cc >= rr
        for jj in range(4):
            wta = jnp.where(triu, wt_ref[2 * jj], 0.0).astype(BF16)
            wtb = jnp.where(triu, wt_ref[2 * jj + 1], 0.0).astype(BF16)
            for ch in range(tm // CH):
                rs, cs = slice(ch * CH, (ch + 1) * CH), slice(jj * 128, (jj + 1) * 128)
                dm2 = dmix[rs, cs]
                dma = jnp.where(is_a, dm2, 0.0)
                dmb = jnp.where(is_a, 0.0, dm2)
                dma_b, dmb_b = dma.astype(BF16), dmb.astype(BF16)
                vn2 = vnb_s[rs, cs]
                dw_ref[2 * jj] += jnp.where(tril, _nt(dma_b, vn2), 0.0)
                dw_ref[2 * jj + 1] += jnp.where(tril, _nt(dmb_b, vn2), 0.0)
                dvn_s[rs, cs] = _nn(wta, dma_b) + _nn(wtb, dmb_b)
                dba = jnp.sum(dma, axis=1, keepdims=True)
                dbb = jnp.sum(dmb, axis=1, keepdims=True)
                dbt_ref[...] += jnp.where(lane == 2 * jj, dba, 0.0) + jnp.where(lane == 2 * jj + 1, dbb, 0.0)

        dvn = dvn_s[...]
        dlng_ref[...] += jnp.sum(dvn * vhat, axis=0, keepdims=True)
        dlnb_ref[...] += jnp.sum(dvn, axis=0, keepdims=True)
        dvh = dvn * lng
        dvg = rstd * (dvh - jnp.mean(dvh, axis=-1, keepdims=True) - vhat * jnp.mean(dvh * vhat, axis=-1, keepdims=True))
        dus_ref[:, SW:NUS] = (dvg * _gelu_grad(vs, tv)).astype(BF16)

    row = lambda n: pl.BlockSpec((tm, n), lambda i: (i, 0))
    return pl.pallas_call(
        body,
        name="mix_bwd",
        grid=(T // tm,),
        in_specs=[row(D), row(D), row(NUS), row(AW), row(AW), row(AW), row(D), _full((1, SW)), _full((1, SW)),
                  _full((8, CH, CH)), _full((8, CH, CH)), _full((CH, 8)), _full((1, AW)), _full((1, SW)), _full((D, D)),
                  _full((1, D))],
        out_specs=[row(4 * AW), row(4 * AW), row(NUS),
                   pl.BlockSpec((D, D), lambda i: (0, 0), pipeline_mode=pl.Buffered(1)), _full((8, CH, CH)),
                   _full((CH, 128)), _full((1, SW)), _full((1, SW)), _full((1, AW)), _full((1, SW)), _full((1, D))],
        out_shape=[jax.ShapeDtypeStruct((T, 4 * AW), BF16),
                   jax.ShapeDtypeStruct((T, 4 * AW), BF16), jax.ShapeDtypeStruct((T, NUS), BF16),
                   jax.ShapeDtypeStruct((D, D), F32),
                   jax.ShapeDtypeStruct((8, CH, CH), F32), jax.ShapeDtypeStruct((CH, 128), F32),
                   jax.ShapeDtypeStruct((1, SW), F32), jax.ShapeDtypeStruct((1, SW), F32),
                   jax.ShapeDtypeStruct((1, AW), F32), jax.ShapeDtypeStruct((1, SW), F32), jax.ShapeDtypeStruct((1, D), F32)],
        scratch_shapes=[pltpu.VMEM((tm, SW), F32), pltpu.VMEM((tm, SW), BF16), pltpu.VMEM((tm, SW), F32)],
        compiler_params=_params(1, VMEM_LIMIT_BIG),
    )(dh1, o, us, yatt, qkv, lsec, yb, lng, lnb, sgw, sgwt, sgbt, gatt, gsg, wout, gpm)


def _attn_operands(q, do, o, lsec, qw_ref, dow_ref):
    rows = do.shape[0]
    feat = lax.broadcasted_iota(jnp.int32, (AW, 128), 0)
    head = lax.broadcasted_iota(jnp.int32, (AW, 128), 1)
    sel = jnp.where((feat >= head * HD) & (feat < (head + 1) * HD), 1.0, 0.0)
    delta = jnp.dot(do * o, sel, precision=lax.Precision.HIGHEST, preferred_element_type=F32)
    for hp in range(4):
        cols = slice(hp * 128, (hp + 1) * 128)
        for hh in range(2):
            base = (2 * hp + hh) * 256
            lc = lsec[:, hp * 128 + hh:hp * 128 + hh + 1]
            d_h = delta[:, 2 * hp + hh:2 * hp + hh + 1]
            qw_ref[:, base:base + 128] = _head_rows(q[:, cols], hh, 0.125)
            qw_ref[:, base + 128:base + 256] = _aug_lanes(rows, AUG_A if hh == 0 else AUG_B, _split3(-lc))
            dow_ref[:, base:base + 128] = _head_rows(do[:, cols], hh, 1.0).astype(BF16)
            dow_ref[:, base + 128:base + 256] = _aug_lanes(rows, None, _split3(-d_h))


def _attn_bwd(kk, vk, kt, qw, dow, after):
    T = kk.shape[0]
    tq = _tile(T, 512)
    tk = tq
    nq = T // tq

    def body(kk_ref, vk_ref, kt_ref, qw_ref, dow_ref, after_ref, dqt_ref, dcq_ref, dk_ref, dv_ref, dck_ref, dk_s, dv_s, dck_s):
        j, i = _fold_cols(pl.program_id(1), pl.program_id(2), nq)
        sub8 = lax.broadcasted_iota(jnp.int32, (8, 1), 0)
        lane = lax.broadcasted_iota(jnp.int32, (1, 128), 1)

        @pl.when((pl.program_id(1) == 0) & (pl.program_id(2) == 0))
        def _():
            dqt_ref[...] = jnp.zeros_like(dqt_ref)
            dcq_ref[...] = jnp.zeros_like(dcq_ref)

        @pl.when(i == j)
        def _():
            dk_s[...] = jnp.zeros_like(dk_s)
            dv_s[...] = jnp.zeros_like(dv_s)
            dck_s[...] = jnp.zeros_like(dck_s)

        def step(masked):
            cols = pl.ds(pl.multiple_of(i * tq, tq), tq)
            sub = lax.broadcasted_iota(jnp.int32, (128, 1), 0)
            if masked:
                causal = lax.broadcasted_iota(jnp.int32, (tk, tq), 0) <= lax.broadcasted_iota(jnp.int32, (tk, tq), 1)

            def logits(n):
                pair, base = n // 2, n * 256
                return (_nt(kk_ref[:, pair * 256:(pair + 1) * 256], qw_ref[:, base:base + 256]),
                        _nt(vk_ref[:, pair * 256:(pair + 1) * 256], dow_ref[:, base:base + 256]))

            ahead = logits(0)
            dcq = jnp.zeros((8, tq), F32)
            dck = jnp.zeros((tk, 128), F32)
            for pp in range(PAIRS):
                lanes = slice(pp * 128, (pp + 1) * 128)
                kt2 = kt_ref[lanes, :] * 0.125
                dv = jnp.zeros((tk, 128), F32)
                dk = jnp.zeros((tk, 128), F32)
                dqts = []
                for hh in range(2):
                    head = 2 * pp + hh
                    base = head * 256
                    qw_h = qw_ref[:, base:base + 256]
                    dow_h = dow_ref[:, base:base + 256]
                    logp, dp = ahead
                    if head + 1 < 2 * PAIRS:
                        ahead = logits(head + 1)
                    pr = jnp.exp(logp)
                    if masked:
                        pr = jnp.where(causal, pr, 0.0)
                    ds = pr * dp
                    ds_b = ds.astype(BF16)
                    dv = dv + _nn(pr.astype(BF16), dow_h[:, 0:128])
                    dk = dk + _nn(ds_b, qw_h[:, 0:128])
                    dqts.append(_nn(kt2, ds_b))
                    dcq = dcq + jnp.where(sub8 == head, jnp.sum(ds, axis=0, keepdims=True), 0.0)
                    dck = dck - jnp.where(lane == head, jnp.sum(ds, axis=1, keepdims=True), 0.0)
                dv_s[:, lanes] += dv
                dk_s[:, lanes] += dk
                dqt_ref[lanes, cols] += jnp.where(sub < HD, dqts[0], dqts[1])
            dck_s[...] += dck
            dcq_ref[:, cols] += dcq

        @pl.when(i > j)
        def _():
            step(False)

        @pl.when(i == j)
        def _():
            step(True)

        @pl.when(i == nq - 1)
        def _():
            dk_ref[...] = dk_s[...].astype(BF16)
            dv_ref[...] = dv_s[...].astype(BF16)
            dck_ref[...] = dck_s[...]

    kj = lambda r, t: _fold_cols(r, t, nq)[0]
    qi = lambda r, t: _fold_cols(r, t, nq)[1]
    krow = lambda g, r, t: (kj(r, t), g)
    qrow = lambda g, r, t: (qi(r, t), g)
    return pl.pallas_call(
        body,
        name="attn_bwd",
        grid=_fold_grid(nq),
        in_specs=[
            pl.BlockSpec((tk, PAIRS * 256), krow),
            pl.BlockSpec((tk, PAIRS * 256), krow),
            pl.BlockSpec((PAIRS * 128, tk), lambda g, r, t: (g, kj(r, t))),
            pl.BlockSpec((tq, PAIRS * 512), qrow),
            pl.BlockSpec((tq, PAIRS * 512), qrow),
            pl.BlockSpec(after.shape, lambda g, r, t: (0,) * after.ndim),
        ],
        out_specs=[
            pl.BlockSpec((PAIRS * 128, T), lambda g, r, t: (g, 0), pipeline_mode=pl.Buffered(1)),
            pl.BlockSpec((8, T), lambda g, r, t: (0, 0), pipeline_mode=pl.Buffered(1)),
            pl.BlockSpec((tk, PAIRS * 128), krow),
            pl.BlockSpec((tk, PAIRS * 128), krow),
            pl.BlockSpec((tk, 128), lambda g, r, t: (kj(r, t), 0)),
        ],
        out_shape=[jax.ShapeDtypeStruct((AW, T), F32), jax.ShapeDtypeStruct((8, T), F32),
                   jax.ShapeDtypeStruct((T, AW), BF16), jax.ShapeDtypeStruct((T, AW), BF16),
                   jax.ShapeDtypeStruct((T, FPAD), F32)],
        scratch_shapes=[pltpu.VMEM((tk, PAIRS * 128), F32), pltpu.VMEM((tk, PAIRS * 128), F32),
                        pltpu.VMEM((tk, 128), F32)],
        compiler_params=_params(3),
    )(kk, vk, kt, qw, dow, after)


def _fgate_bwd(dcq, dck, fz, fb):
    T = dck.shape[0]
    tb = _tile(T, 512)
    nb = T // tb

    def body(dcq_ref, dck_ref, fz_ref, fb_ref, df_ref, dfb_ref, carry):
        @pl.when(pl.program_id(0) == 0)
        def _():
            carry[...] = jnp.zeros_like(carry)
            dfb_ref[...] = jnp.zeros_like(dfb_ref)

        head = lax.broadcasted_iota(jnp.int32, (8, FPAD), 0)
        eye = jnp.where(head == lax.broadcasted_iota(jnp.int32, (8, FPAD), 1), 1.0, 0.0)
        dcv = dck_ref[...] + lax.dot_general(dcq_ref[...], eye, (((0,), (0,)), ((), ())),
                                             precision=lax.Precision.HIGHEST, preferred_element_type=F32)
        r = lax.broadcasted_iota(jnp.int32, (128, 128), 0)
        cc = lax.broadcasted_iota(jnp.int32, (128, 128), 1)
        tri = (cc >= r).astype(F32)
        run = carry[...]
        parts = []
        for blk in reversed(range(tb // 128)):
            dc_b = dcv[blk * 128:(blk + 1) * 128, :]
            parts.append(jnp.dot(tri, dc_b, precision=lax.Precision.HIGHEST, preferred_element_type=F32) + run)
            run = run + jnp.sum(dc_b, axis=0, keepdims=True)
        carry[...] = run
        dlf = jnp.concatenate(parts[::-1], axis=0)
        lane = lax.broadcasted_iota(jnp.int32, (tb, FPAD), 1)
        df = jnp.where(lane < 8, dlf * jax.nn.sigmoid(-(fz_ref[...] + fb_ref[...])), 0.0)
        df_ref[...] = df.astype(BF16)
        dfb_ref[...] += jnp.sum(df, axis=0, keepdims=True)

    rev = pl.BlockSpec((tb, FPAD), lambda i: (nb - 1 - i, 0))
    return pl.pallas_call(
        body,
        name="fgate_bwd",
        grid=(nb,),
        in_specs=[pl.BlockSpec((8, tb), lambda i: (0, nb - 1 - i)), rev, rev, _full((1, FPAD))],
        out_specs=[rev, _full((1, FPAD))],
        out_shape=[jax.ShapeDtypeStruct((T, FPAD), BF16), jax.ShapeDtypeStruct((1, FPAD), F32)],
        scratch_shapes=[pltpu.VMEM((1, FPAD), F32)],
        compiler_params=_params(1),
    )(dcq, dck, fz, fb)


def _inproj_bwd(dqt, dk, dv, dus, dfz, wtp, x, dh1, g, adam):
    T = x.shape[0]
    tm = _tile(T, 512)
    adam_in, adam_out, adam_shape = _adam_specs(adam[1], T // tm)

    def body(dq_ref, dk_ref, dv_ref, dus_ref, dfz_ref, w_ref, x_ref, dh1_ref, g_ref, *rest):
        gx_ref, dg_ref = rest[4:6]
        _adam_step(*rest[0:4], *rest[6:10])

        @pl.when(pl.program_id(0) == 0)
        def _():
            dg_ref[...] = jnp.zeros_like(dg_ref)

        da = _tn(dq_ref[...].astype(BF16), w_ref[0:AW, :])
        da += _nn(dk_ref[...], w_ref[AW:2 * AW, :])
        da += _nn(dv_ref[...], w_ref[2 * AW:NQKV, :])
        da += _nn(dus_ref[...], w_ref[NQKV:NQKV + NUS, :])
        da += _nn(dfz_ref[...], w_ref[NQKV + NUS:ZP, :])
        dx, dg = _rms_bwd(da, x_ref[...], g_ref[...])
        gx_ref[...] = dh1_ref[...] + dx
        dg_ref[...] += dg

    row = lambda n: pl.BlockSpec((tm, n), lambda i: (i, 0))
    return pl.pallas_call(
        body,
        name="inproj_bwd",
        grid=(T // tm,),
        in_specs=[pl.BlockSpec((AW, tm), lambda i: (0, i)), row(AW), row(AW), row(NUS), row(FPAD), _full((ZP, D)),
                  row(D), row(D), _full((1, D))] + adam_in,
        out_specs=[row(D), _full((1, D))] + adam_out,
        out_shape=[jax.ShapeDtypeStruct((T, D), F32), jax.ShapeDtypeStruct((1, D), F32)] + adam_shape,
        compiler_params=_params(1),
    )(dqt, dk, dv, dus, dfz, wtp, x, dh1, g, *adam)


def _sq_relu(f1):
    r = jnp.maximum(f1.astype(F32), 0.0)
    return (r * r).astype(BF16)


def _wgrad(a, bs, name, a_fn=None, out_dtype=F32, transposed=False, adam=None, stack=None):
    T, K = a.shape
    tt = _tile(T, 1024)
    nb = len(bs)
    narrow = out_dtype != F32
    adam_in, adam_out, adam_shape = ([], [], []) if adam is None else _adam_specs(adam[1], T // tt)
    na = len(adam_in)
    assert stack is None or (transposed and all(first % 8 == 0 and rows % 8 == 0 for first, rows in stack))

    def out_dims(b, layout):
        if transposed:
            return (b.shape[0] if layout == "t" else b.shape[1], K)
        if layout == "t":
            return (K, b.shape[0])
        N = b.shape[1]
        if layout is None:
            return (K, N)
        return (N // layout[1], K, layout[1]) if layout[0] == "col" else (K // layout[1], layout[1], N)

    shapes = [out_dims(b, layout) for b, layout in bs]
    if stack is not None:
        shapes = [(max(first + rows for first, rows in stack), K)]
    n_out = len(shapes)

    def body(*refs):
        a_ref, b_refs, o_refs = refs[0], refs[1:1 + nb], refs[1 + nb + na:1 + nb + na + n_out]
        sums = refs[1 + nb + 2 * na + n_out:] if narrow else o_refs
        if adam is not None:
            _adam_step(*refs[1 + nb:1 + nb + na], *refs[1 + nb + na + n_out:1 + nb + 2 * na + n_out])

        @pl.when(pl.program_id(0) == 0)
        def _():
            for acc in sums:
                acc[...] = jnp.zeros_like(acc)

        accs = sums if stack is None else [sums[0].at[pl.ds(first, rows)] for first, rows in stack]
        av = (a_ref[...] if a_fn is None else a_fn(a_ref[...])).astype(BF16)
        at = None if transposed else av.T
        for (b, layout), b_ref, o_ref in zip(bs, b_refs, accs):
            if transposed:
                bv = b_ref[...].astype(BF16)
                o_ref[...] += (_nn(bv, av) if layout == "t" else _tn(bv, av))[0:o_ref.shape[0]]
            elif layout is None:
                o_ref[...] += _nn(at, b_ref[...].astype(BF16))
            elif layout == "t":
                o_ref[...] += _nt(at, b_ref[...].astype(BF16))
            elif layout[0] == "col":
                n = layout[1]
                for k in range(b.shape[1] // n):
                    o_ref[k] += _nn(at, b_ref[:, k * n:(k + 1) * n].astype(BF16))
            else:
                n = layout[1]
                bv = b_ref[...].astype(BF16)
                for k in range(K // n):
                    o_ref[k] += _nn(at[k * n:(k + 1) * n, :], bv)

        if narrow:
            @pl.when(pl.program_id(0) == T // tt - 1)
            def _():
                for o_ref, acc in zip(o_refs, sums):
                    o_ref[...] = acc[...].astype(out_dtype)

    once = lambda shape: pl.BlockSpec(shape, lambda t: (0,) * len(shape), pipeline_mode=pl.Buffered(1))
    res = pl.pallas_call(
        body,
        name=name,
        grid=(T // tt,),
        in_specs=[pl.BlockSpec((tt, K), lambda t: (t, 0))] + [
            pl.BlockSpec((b.shape[0], tt), lambda t: (0, t)) if layout == "t" else pl.BlockSpec((tt, b.shape[1]), lambda t: (t, 0))
            for b, layout in bs] + adam_in,
        out_specs=[once(s) for s in shapes] + adam_out,
        out_shape=[jax.ShapeDtypeStruct(s, out_dtype) for s in shapes] + adam_shape,
        scratch_shapes=[pltpu.VMEM(s, F32) for s in shapes] if narrow else [],
        compiler_params=_params(1, VMEM_LIMIT_BIG),
    )(a, *[b for b, _ in bs], *(adam or ()))
    return res


def _adam_math(w, g, m, v):
    m = ADAM_B1 * m + (1.0 - ADAM_B1) * g
    v = ADAM_B2 * v + (1.0 - ADAM_B2) * (g * g)
    m_hat = m / (1.0 - ADAM_B1 ** ADAM_STEP)
    v_hat = v / (1.0 - ADAM_B2 ** ADAM_STEP)
    delta = -ADAM_LR * (m_hat / (jnp.sqrt(v_hat) + ADAM_EPS) + ADAM_WD * w)
    return delta, m, v


def _adam_step(p_ref, w_ref, m_ref, v_ref, g_ref, d_ref, nm_ref, nv_ref):
    g = p_ref[0].astype(F32)
    for s in range(1, NDEV):
        g = g + p_ref[s].astype(F32)
    g_ref[...] = g
    d_ref[...], nm_ref[...], nv_ref[...] = _adam_math(w_ref[...], g, m_ref[...], v_ref[...])


def _adam_specs(w, steps):
    R, C = w.shape
    br = R // steps
    assert br * steps == R and (br % 8 == 0 or steps == 1)
    blk = pl.BlockSpec((br, C), lambda i: (i, 0))
    return ([pl.BlockSpec((NDEV, br, C), lambda i: (0, i, 0)), blk, blk, blk], [blk] * 4,
            [jax.ShapeDtypeStruct((R, C), F32)] * 4)


def _adam(parts, w, m, v, name):
    R = w.shape[0]
    steps = R // 128 if R % 128 == 0 else 1
    in_specs, out_specs, out_shape = _adam_specs(w, steps)

    def body(*refs):
        _adam_step(*refs)

    return pl.pallas_call(
        body,
        name=name,
        grid=(steps,),
        in_specs=in_specs,
        out_specs=out_specs,
        out_shape=out_shape,
        compiler_params=_params(1),
    )(parts, w, m, v)


def _adam_unit_rows(parts, w, m, v, name):
    R, _, C = w.shape
    blk = pl.BlockSpec((R, None, C), lambda i: (0, 0, 0))

    def body(*refs):
        _adam_step(*refs)

    return pl.pallas_call(
        body,
        name=name,
        grid=(1,),
        in_specs=[pl.BlockSpec((NDEV, R, C), lambda i: (0, 0, 0)), blk, blk, blk],
        out_specs=[blk] * 4,
        out_shape=[jax.ShapeDtypeStruct((R, 1, C), F32)] * 4,
        compiler_params=_params(1),
    )(parts, w, m, v)


_SMALL =(("sg_w", 8 * CH * CH), ("f_bias", 8), ("sg_ln_g", SW), ("sg_ln_b", SW), ("sg_b", 8 * CH), ("att_out_g", AW),
          ("sg_out_g", SW), ("pre_mix_g", D), ("post_mix_g", D), ("pre_ffn_g", D), ("post_ffn_g", D), ("ple_gate_b", D))
_SEG = 8 * 128


def _seg_rows(size):
    return 8 * (-(-size // _SEG))


_LATE = ("f_bias", "pre_mix_g")


def _pack(vals, late, loss_acc=None):
    parts = []
    for name, size in _SMALL:
        if (name in _LATE) == late:
            rows = _seg_rows(size)
            parts.append(jnp.pad(vals[name].reshape(-1), (0, rows * 128 - size)).reshape(rows, 128))
    if loss_acc is not None:
        parts.append(loss_acc)
    return jnp.concatenate(parts, axis=0)


def _adam_small(parts_early, parts_late, ws, ms, vs):
    n = len(_SMALL)
    names = [name for name, _ in _SMALL]

    def body(*refs):
        pe_ref, pl_ref, refs = refs[0], refs[1], refs[1:]
        w_refs, m_refs, v_refs = refs[1:1 + n], refs[1 + n:1 + 2 * n], refs[1 + 2 * n:1 + 3 * n]
        loss_ref, outs = refs[1 + 3 * n], refs[2 + 3 * n:]
        sums, row = [], [0, 0]
        for p_ref in (pe_ref, pl_ref):
            tot = p_ref[0]
            for s in range(1, NDEV):
                tot = tot + p_ref[s]
            sums.append(tot)
        for k, (name, size) in enumerate(_SMALL):
            dst = [outs[kind * n + k] for kind in range(4)]
            which = 1 if name in _LATE else 0
            g_all, r = sums[which], row[which]

            def update(g, idx):
                vals = (g,) + _adam_math(w_refs[k][idx], g, m_refs[k][idx], v_refs[k][idx])
                for d, val in zip(dst, vals):
                    d[idx] = val

            if name == "sg_w":
                for grp in range(8):
                    update(g_all[r + grp * CH:r + (grp + 1) * CH, :], (0, grp))
            elif name == "sg_b":
                update(g_all[r:r + 8, :], (0,))
            elif name == "f_bias":
                update(g_all[r:r + 1, 0:8], (slice(None),))
            else:
                update(jnp.concatenate([g_all[r + q:r + q + 1, :] for q in range(size // 128)], axis=1), (slice(None),))
            row[which] += _seg_rows(size)
        loss_ref[...] = sums[0][row[0]:row[0] + 1, 0:1] * (0.5 / D)

    arrs = [parts_early, parts_late] + [d[name] for d in (ws, ms, vs) for name in names]
    res = pl.pallas_call(
        body,
        name="adam_small",
        in_specs=[_full(a.shape) for a in arrs],
        out_specs=[_full((1, 1))] + [_full(ws[name].shape) for _ in range(4) for name in names],
        out_shape=[jax.ShapeDtypeStruct((1, 1), F32)] + [jax.ShapeDtypeStruct(ws[name].shape, F32) for _ in range(4) for name in names],
        compiler_params=pltpu.CompilerParams(vmem_limit_bytes=VMEM_LIMIT),
    )(*arrs)
    return res[0], {name: [res[1 + kind * n + k] for kind in range(4)] for k, name in enumerate(names)}


def kernel(x, p, w_in, f_bias, sg_ln_g, sg_ln_b, sg_w, sg_b, att_out_g, sg_out_g, w_out, pre_mix_g, post_mix_g, pre_ffn_g, post_ffn_g, w_ff1, w_ff2, ple_w, ple_gate_w, ple_gate_b, loss_target, m_w_in, m_f_bias, m_sg_ln_g, m_sg_ln_b, m_sg_w, m_sg_b, m_att_out_g, m_sg_out_g, m_w_out, m_pre_mix_g, m_post_mix_g, m_pre_ffn_g, m_post_ffn_g, m_w_ff1, m_w_ff2, m_ple_w, m_ple_gate_w, m_ple_gate_b, v_w_in, v_f_bias, v_sg_ln_g, v_sg_ln_b, v_sg_w, v_sg_b, v_att_out_g, v_sg_out_g, v_w_out, v_pre_mix_g, v_post_mix_g, v_pre_ffn_g, v_post_ffn_g, v_w_ff1, v_w_ff2, v_ple_w, v_ple_gate_w, v_ple_gate_b):
    small_w = dict(sg_w=sg_w, f_bias=f_bias, sg_ln_g=sg_ln_g, sg_ln_b=sg_ln_b, sg_b=sg_b, att_out_g=att_out_g,
                   sg_out_g=sg_out_g, pre_mix_g=pre_mix_g, post_mix_g=post_mix_g, pre_ffn_g=pre_ffn_g,
                   post_ffn_g=post_ffn_g, ple_gate_b=ple_gate_b)
    small_m = dict(sg_w=m_sg_w, f_bias=m_f_bias, sg_ln_g=m_sg_ln_g, sg_ln_b=m_sg_ln_b, sg_b=m_sg_b, att_out_g=m_att_out_g,
                   sg_out_g=m_sg_out_g, pre_mix_g=m_pre_mix_g, post_mix_g=m_post_mix_g, pre_ffn_g=m_pre_ffn_g,
                   post_ffn_g=m_post_ffn_g, ple_gate_b=m_ple_gate_b)
    small_v = dict(sg_w=v_sg_w, f_bias=v_f_bias, sg_ln_g=v_sg_ln_g, sg_ln_b=v_sg_ln_b, sg_b=v_sg_b, att_out_g=v_att_out_g,
                   sg_out_g=v_sg_out_g, pre_mix_g=v_pre_mix_g, post_mix_g=v_post_mix_g, pre_ffn_g=v_pre_ffn_g,
                   post_ffn_g=v_post_ffn_g, ple_gate_b=v_ple_gate_b)
    big = dict(w_in=(w_in, m_w_in, v_w_in), w_out=(w_out, m_w_out, v_w_out), w_ff1=(w_ff1, m_w_ff1, v_w_ff1),
               w_ff2=(w_ff2, m_w_ff2, v_w_ff2), ple_w=(ple_w, m_ple_w, v_ple_w),
               ple_gate_w=(ple_gate_w, m_ple_gate_w, v_ple_gate_w))

    xt, pt, tgt = x[0], p[0, 0], loss_target[0]
    ws = W_IN_COLS // NDEV

    gw_in = _gather(jnp.pad(jnp.transpose(w_in[0]).astype(BF16), ((0, WS_PAD - ws), (0, 0))), "gather_w_in")
    gw_in, others = lax.optimization_barrier((gw_in, [w_out[0].astype(BF16), w_ff1[0].astype(BF16), w_ff2[0].astype(BF16),
                                                      ple_w[0].astype(BF16), ple_gate_w[0].astype(BF16)]))
    rest = _xchg_start(others, True, "gather_rest_start")
    wint = gw_in[:, 0:ws, :].reshape(W_IN_COLS, D)
    wtp = jnp.concatenate([wint[0:NQKV], wint[NQKV + 8:W_IN_COLS], wint[NQKV:NQKV + 8],
                           jnp.zeros((FPAD - 8, D), BF16)], axis=0)

    fb = jnp.pad(f_bias.astype(F32), ((0, 0), (0, FPAD - 8)))
    sgw = sg_w[0]
    sgwt = jnp.transpose(sg_w[0], (0, 2, 1))
    sgbt = jnp.transpose(sg_b[0])

    qkv, kt, vt, us, fz, ab = _inproj(xt, pre_mix_g + rest[4][0, 0], wtp)
    ct, kk, vk = _fcum(fz, fb, qkv)
    yatt, lsec = _attn_fwd(qkv, kk, vt, ct)
    gw_out, gw1, gw2, gwpe, gwg = _xchg_wait(rest, True, yatt, "gather_rest_wait")
    wout = gw_out.reshape(D, D)
    wg = gwg.reshape(D, D)
    wpe = jnp.transpose(gwpe, (1, 0, 2)).reshape(PLE, D)
    h1, yb, o = _sgu_out(us, yatt, xt, sg_ln_g, sg_ln_b, sgw, sgbt, att_out_g, sg_out_g, wout, post_mix_g)
    c2b, f1b, ff, h2 = _ffn_fwd(h1, pre_ffn_g, gw1, gw2, post_ffn_g)
    dh2, dbg, loss_acc, g_g, g_pe = _ple_loss(h2, pt, tgt, wg, ple_gate_b, wpe)
    g_g = g_g.reshape(NDEV, D // NDEV, D)
    g_pe = jnp.transpose(g_pe.reshape(PLE, NDEV, D // NDEV), (1, 0, 2))

    dffb, df1, dh1, dgpostffn, dgpreffn = _ffn_bwd(dh2, ff, h1, f1b, gw1, gw2, post_ffn_g, pre_ffn_g)
    (g_1,) = _wgrad(c2b, [(df1, ("col", DFF // NDEV))], "wgrad_ff1")
    (g_2,) = _wgrad(f1b, [(dffb, ("row", DFF // NDEV))], "wgrad_ff2", a_fn=_sq_relu)
    early = _xchg_start([g_1, g_2, g_pe, g_g], False, "scatter_early_start")
    qw, dow, dus, g_out, dsgw, dsgbt, dlng, dlnb, dgatt, dgsg, dgpostmix = _mix_bwd(
        dh1, o, us, yatt, qkv, lsec, yb, sg_ln_g, sg_ln_b, sgw, sgwt, sgbt, att_out_g, sg_out_g, wout,
        post_mix_g + early[4][0, 0])
    mid = _xchg_start([g_out.reshape(NDEV, D // NDEV, D)], False, "scatter_mid_start")
    small_g = dict(sg_w=dsgw, sg_ln_g=dlng, sg_ln_b=dlnb, sg_b=jnp.transpose(dsgbt[:, 0:8]), att_out_g=dgatt, sg_out_g=dgsg,
                   post_mix_g=dgpostmix, pre_ffn_g=dgpreffn, post_ffn_g=dgpostffn, ple_gate_b=dbg)
    small_early = _xchg_start([_pack(small_g, False, loss_acc)], True, "gather_small_early_start")
    dqt, dcq, dk, dv, dck = _attn_bwd(kk, vk, kt, qw, dow, mid[4] + small_early[4])
    dfz, dfb = _fgate_bwd(dcq, dck, fz, fb)

    r_1, r_2, r_pe, r_g = _xchg_wait(early, False, dfz, "scatter_early_wait")
    g_in, *adam_1 = _wgrad(ab, [(dqt, "t"), (dk, None), (dv, None), (dus, None), (dfz, None)], "wgrad_in",
                           out_dtype=BF16, transposed=True, adam=(r_1, *[t[0] for t in big["w_ff1"]]),
                           stack=[(0, AW), (AW, AW), (2 * AW, AW), (NQKV + 8, NUS), (NQKV, 8)])
    g_in = g_in.reshape(NDEV, ws, D)
    late = _xchg_start([g_in], False, "scatter_late_start")
    grad_x, dgpremix, *adam_2 = _inproj_bwd(dqt, dk, dv, dus, dfz, wtp, xt, dh1, pre_mix_g + late[4][0, 0],
                                            (r_2, *[t[0] for t in big["w_ff2"]]))

    (r_out,) = _xchg_wait(mid, False, grad_x, "scatter_mid_wait")
    (r_early,) = _xchg_wait(small_early, True, grad_x, "gather_small_early_wait")
    r_late = _gather(_pack(dict(f_bias=dfb[:, 0:8], pre_mix_g=dgpremix), True), "gather_small_late")

    res = dict(w_ff1=[t[None] for t in adam_1], w_ff2=[t[None] for t in adam_2])

    def adam_big(name, parts):
        w, m, v = big[name]
        res[name] = [t[None] for t in _adam(parts, w[0], m[0], v[0], "adam_" + name)]

    for name, parts in (("w_out", r_out), ("ple_w", r_pe), ("ple_gate_w", r_g)):
        adam_big(name, parts)
    (r_in,) = _xchg_wait(late, False, res["ple_gate_w"][0], "scatter_late_wait")
    res["w_in"] = [jnp.transpose(t, (1, 2, 0)) for t in
                   _adam_unit_rows(r_in, *[jnp.transpose(t, (2, 0, 1)) for t in big["w_in"]], "adam_w_in")]
    loss, small = _adam_small(r_early, r_late, small_w, small_m, small_v)
    res.update(small)

    order = ["w_in", "f_bias", "sg_ln_g", "sg_ln_b", "sg_w", "sg_b", "att_out_g", "sg_out_g", "w_out", "pre_mix_g",
             "post_mix_g", "pre_ffn_g", "post_ffn_g", "w_ff1", "w_ff2", "ple_w", "ple_gate_w", "ple_gate_b"]
    outs = [loss[0, 0], grad_x[None]]
    for kind in range(4):
        outs += [res[name][kind] for name in order]
    return tuple(outs)
```

```python
import jax
import jax.numpy as jnp
from jax import lax
from jax.experimental import pallas as pl
from jax.experimental.pallas import tpu as pltpu

F32 = jnp.float32
BF16 = jnp.bfloat16

NDEV = 8
D = 1024
AW = 512
SW = 512
HD = 64
CH = 128
DFF = 4096
PLE = 256
NQKV = 3 * AW
NUS = 2 * SW
FPAD = 128
ZP = NQKV + NUS + FPAD
W_IN_COLS = 2568
WS_PAD = 336
EPS = 1e-6
MASKV = -1e30
GELU_K = 0.7978845608028654
GELU_C = 0.044715

ADAM_LR = 0.001
ADAM_B1 = 0.9
ADAM_B2 = 0.999
ADAM_EPS = 1e-08
ADAM_WD = 0.01
ADAM_STEP = 10

VMEM_LIMIT = 48 * 1024 * 1024
VMEM_LIMIT_BIG = 60 * 1024 * 1024


def _nn(a, b):
    return jnp.dot(a, b, preferred_element_type=F32)


def _nt(a, b):
    return lax.dot_general(a, b, (((1,), (1,)), ((), ())), preferred_element_type=F32)


def _tn(a, b):
    return lax.dot_general(a, b, (((0,), (0,)), ((), ())), preferred_element_type=F32)


def _tile(n, pref):
    return min(n, pref)


def _params(n_axes, vmem=VMEM_LIMIT):
    return pltpu.CompilerParams(dimension_semantics=("arbitrary",) * n_axes, vmem_limit_bytes=vmem)


def _full(shape):
    nd = len(shape)
    return pl.BlockSpec(shape, lambda *_: (0,) * nd)


def _rms_fwd(x, g):
    r = lax.rsqrt(jnp.mean(x * x, axis=-1, keepdims=True) + EPS)
    return x * r * g


def _rms_bwd(dy, x, g):
    n = x.shape[-1]
    r = lax.rsqrt(jnp.mean(x * x, axis=-1, keepdims=True) + EPS)
    u = dy * g
    s = jnp.sum(x * u, axis=-1, keepdims=True)
    dx = r * u - x * (r * r * r * (s * (1.0 / n)))
    dg = jnp.sum(dy * (x * r), axis=0, keepdims=True)
    return dx, dg


def _gelu(x):
    t = jnp.tanh(x * (GELU_K + (GELU_K * GELU_C) * (x * x)))
    return x * (0.5 + 0.5 * t), t


def _gelu_grad(x, t):
    return (0.5 + 0.5 * t) + (0.5 * x) * (1.0 - t * t) * (GELU_K + (3.0 * GELU_K * GELU_C) * (x * x))


def _gather(arr, name):
    def body(x_ref, out_ref, send, recv, loc):
        x, y, c = lax.axis_index("x"), lax.axis_index("y"), lax.axis_index("c")
        me, sibling = (x, y, c), (x, y, 1 - c)
        chips = [(1 - x, y), (x, 1 - y), (1 - x, 1 - y)]

        def slot(px, py, pc):
            return out_ref.at[4 * px + 2 * py + pc]

        def copy(k, block, to, src=None):
            return pltpu.make_async_remote_copy(
                src_ref=slot(*block) if src is None else src, dst_ref=slot(*block), send_sem=send.at[k],
                recv_sem=recv.at[k], device_id=to, device_id_type=pl.DeviceIdType.MESH)

        mine = pltpu.make_async_copy(x_ref, slot(*me), loc)
        mine.start()
        first = [copy(0, me, sibling, src=x_ref)] + [copy(1 + j, me, (*chip, c), src=x_ref) for j, chip in enumerate(chips)]
        for cp in first:
            cp.start()
        passed = [copy(4 + j, (*chip, c), sibling) for j, chip in enumerate(chips)]
        for j, chip in enumerate(chips):
            copy(1 + j, (*chip, c), me).wait_recv()
            passed[j].start()
        copy(0, sibling, me).wait_recv()
        for j, chip in enumerate(chips):
            copy(4 + j, (*chip, 1 - c), me).wait_recv()
        for cp in first + passed:
            cp.wait_send()
        mine.wait()

    hbm = pl.BlockSpec(memory_space=pltpu.HBM)
    return pl.pallas_call(
        body,
        name=name,
        out_shape=jax.ShapeDtypeStruct((NDEV,) + arr.shape, arr.dtype),
        in_specs=[hbm],
        out_specs=hbm,
        scratch_shapes=[pltpu.SemaphoreType.DMA((NDEV - 1,)), pltpu.SemaphoreType.DMA((NDEV - 1,)), pltpu.SemaphoreType.DMA],
    )(arr)


def _peers(x, y, c):
    out = []
    for k in range(1, NDEV):
        out.append((1 - x if (k >> 2) & 1 else x, 1 - y if (k >> 1) & 1 else y, 1 - c if k & 1 else c))
    return out


def _xchg_start(arrs, gather, name):
    n = len(arrs)
    me = 4 * lax.axis_index("x") + 2 * lax.axis_index("y") + lax.axis_index("c")
    lands = []
    for a in arrs:
        shape = ((NDEV,) + a.shape) if gather else a.shape
        own = a[None] if gather else lax.dynamic_slice_in_dim(a, me, 1, axis=0)
        lands.append(lax.dynamic_update_slice_in_dim(lax.empty(shape, a.dtype), own, me, axis=0))

    def body(*refs):
        ins, lnd = refs[:n], refs[n:2 * n]
        send, recv, token = refs[2 * n:3 * n], refs[3 * n:4 * n], refs[-1]
        x, y, c = lax.axis_index("x"), lax.axis_index("y"), lax.axis_index("c")
        mine = 4 * x + 2 * y + c
        for px, py, pc in _peers(x, y, c):
            peer = 4 * px + 2 * py + pc
            for a in range(n):
                pltpu.make_async_remote_copy(
                    src_ref=ins[a] if gather else ins[a].at[peer],
                    dst_ref=lnd[a].at[mine],
                    send_sem=send[a],
                    recv_sem=recv[a],
                    device_id=(px, py, pc),
                    device_id_type=pl.DeviceIdType.MESH,
                ).start()
        token[...] = jnp.zeros_like(token)

    hbm = pl.BlockSpec(memory_space=pltpu.HBM)
    sem = pl.BlockSpec(memory_space=pltpu.SEMAPHORE)
    res = pl.pallas_call(
        body,
        name=name,
        out_shape=(*[pltpu.SemaphoreType.DMA(())] * (2 * n),
                   *[pltpu.HBM(a.shape, a.dtype) for a in arrs], *[pltpu.HBM(l.shape, l.dtype) for l in lands],
                   jax.ShapeDtypeStruct((8, 128), F32)),
        in_specs=[hbm] * (2 * n),
        out_specs=(*([sem] * (2 * n)), *([hbm] * (2 * n)), pl.BlockSpec(memory_space=pltpu.VMEM)),
        input_output_aliases={i: 2 * n + i for i in range(2 * n)},
        compiler_params=pltpu.CompilerParams(has_side_effects=pltpu.SideEffectType.DATAFLOW_SIDE_EFFECTING),
    )(*[pltpu.with_memory_space_constraint(a, pltpu.HBM) for a in arrs],
      *[pltpu.with_memory_space_constraint(l, pltpu.HBM) for l in lands])
    return list(res[0:n]), list(res[n:2 * n]), list(res[2 * n:3 * n]), list(res[3 * n:4 * n]), res[-1]


def _xchg_wait(started, gather, after, name):
    send, recv, srcs, lands, _ = started
    n = len(srcs)

    def body(*refs):
        lnd = refs[n:2 * n]
        send, recv = refs[2 * n:3 * n], refs[3 * n:4 * n]
        me = (lax.axis_index("x"), lax.axis_index("y"), lax.axis_index("c"))
        for a in range(n):
            seven = lnd[a].at[pl.ds(0, NDEV - 1)]
            cp = pltpu.make_async_remote_copy(src_ref=seven, dst_ref=seven, send_sem=send[a], recv_sem=recv[a],
                                              device_id=me, device_id_type=pl.DeviceIdType.MESH)
            cp.wait_send()
            cp.wait_recv()

    hbm = pl.BlockSpec(memory_space=pltpu.HBM)
    sem = pl.BlockSpec(memory_space=pltpu.SEMAPHORE)
    res = pl.pallas_call(
        body,
        name=name,
        out_shape=tuple([pltpu.HBM(a.shape, a.dtype) for a in srcs] + [pltpu.HBM(l.shape, l.dtype) for l in lands]),
        in_specs=[hbm] * (2 * n) + [sem] * (2 * n) + [pl.BlockSpec(memory_space=pl.ANY)],
        out_specs=tuple([hbm] * (2 * n)),
        input_output_aliases={i: i for i in range(2 * n)},
        compiler_params=pltpu.CompilerParams(has_side_effects=pltpu.SideEffectType.DATAFLOW_SIDE_EFFECTING),
    )(*srcs, *lands, *send, *recv, after)
    return list(res[n:])


def _inproj(x, g, wtp):
    T = x.shape[0]
    tm = _tile(T, 512)

    def body(x_ref, g_ref, w_ref, qkv_ref, kt_ref, vt_ref, us_ref, fz_ref, ab_ref):
        a = _rms_fwd(x_ref[...], g_ref[...]).astype(BF16)
        ab_ref[...] = a
        qkv_ref[:, 0:AW] = _nt(a, w_ref[0:AW, :]).astype(BF16)
        kk = _nt(a, w_ref[AW:2 * AW, :])
        qkv_ref[:, AW:2 * AW] = kk.astype(BF16)
        kt_ref[...] = kk.T.astype(BF16)
        vv = _nt(a, w_ref[2 * AW:NQKV, :])
        qkv_ref[:, 2 * AW:NQKV] = vv.astype(BF16)
        vvt = vv.T.astype(BF16)
        one_row = jnp.where(lax.broadcasted_iota(jnp.int32, (HD, tm), 0) == 0, 1.0, 0.0).astype(BF16)
        for h in range(AW // HD):
            vt_ref[2 * h * HD:(2 * h + 1) * HD, :] = vvt[h * HD:(h + 1) * HD, :]
            vt_ref[(2 * h + 1) * HD:(2 * h + 2) * HD, :] = one_row
        us_ref[...] = _nt(a, w_ref[NQKV:NQKV + NUS, :])
        fz_ref[...] = _nt(a, w_ref[NQKV + NUS:ZP, :])

    row = lambda n: pl.BlockSpec((tm, n), lambda i: (i, 0))
    col = lambda n: pl.BlockSpec((n, tm), lambda i: (0, i))
    return pl.pallas_call(
        body,
        name="inproj",
        grid=(T // tm,),
        in_specs=[row(D), _full((1, D)), _full((ZP, D))],
        out_specs=[row(NQKV), col(AW), col(2 * AW), row(NUS), row(FPAD), row(D)],
        out_shape=[
            jax.ShapeDtypeStruct((T, NQKV), BF16),
            jax.ShapeDtypeStruct((AW, T), BF16),
            jax.ShapeDtypeStruct((2 * AW, T), BF16),
            jax.ShapeDtypeStruct((T, NUS), F32),
            jax.ShapeDtypeStruct((T, FPAD), F32),
            jax.ShapeDtypeStruct((T, D), BF16),
        ],
        compiler_params=_params(1),
    )(x, g, wtp)


def _log_sigmoid(z):
    return jnp.minimum(z, 0.0) - jnp.log1p(jnp.exp(-jnp.abs(z)))


def _split3(x):
    hi = x.astype(BF16)
    r1 = x - hi.astype(F32)
    mid = r1.astype(BF16)
    lo = (r1 - mid.astype(F32)).astype(BF16)
    return hi, mid, lo


AUG_A, AUG_B, AUG_ONE = 0, 3, 6


def _aug_lanes(rows, first_one, pieces):
    lane = lax.broadcasted_iota(jnp.int32, (rows, 128), 1)
    out = jnp.zeros((rows, 128), F32)
    if first_one is not None:
        out = jnp.where((lane >= first_one) & (lane < first_one + 3), 1.0, out)
    for n, piece in enumerate(pieces):
        out = jnp.where(lane == AUG_ONE + n, piece.astype(F32), out)
    return out.astype(BF16)


def _fcum(fz, fb, qkv):
    T = fz.shape[0]
    tb = _tile(T, 512)

    def body(fz_ref, fb_ref, k_ref, v_ref, ct_ref, kk_ref, vk_ref, carry, tri_s, sel_s):
        @pl.when(pl.program_id(0) == 0)
        def _():
            carry[...] = jnp.zeros_like(carry)
            src = lax.broadcasted_iota(jnp.int32, (128, 128), 0)
            dst = lax.broadcasted_iota(jnp.int32, (128, 128), 1)
            tri_s[...] = (dst <= src).astype(F32)
            for hp in range(4):
                for n in range(3):
                    pick = ((src == 2 * hp) & (dst == AUG_A + n)) | ((src == 2 * hp + 1) & (dst == AUG_B + n))
                    sel_s[3 * hp + n] = jnp.where(pick, -1.0, 0.0).astype(BF16)

        lf = _log_sigmoid(fz_ref[...] + fb_ref[...])
        run = carry[...]
        parts = []
        for blk in range(tb // 128):
            lf_b = lf[blk * 128:(blk + 1) * 128, :]
            parts.append(jnp.dot(tri_s[...], lf_b, precision=lax.Precision.HIGHEST, preferred_element_type=F32) + run)
            run = run + jnp.sum(lf_b, axis=0, keepdims=True)
        carry[...] = run
        cs = jnp.concatenate(parts, axis=0)
        ct_ref[...] = cs.T[0:8, :]

        pieces = _split3(cs)
        lane = lax.broadcasted_iota(jnp.int32, (1, 128), 1)
        ones = jnp.where((lane >= AUG_ONE) & (lane < AUG_ONE + 3), 1.0, 0.0)
        for hp in range(4):
            aug = jnp.zeros((tb, 128), F32) + ones
            for n, piece in enumerate(pieces):
                aug = aug + _nn(piece, sel_s[3 * hp + n])
            aug = aug.astype(BF16)
            kk_ref[:, hp * 256:hp * 256 + 128] = k_ref[:, hp * 128:(hp + 1) * 128]
            kk_ref[:, hp * 256 + 128:(hp + 1) * 256] = aug
            vk_ref[:, hp * 256:hp * 256 + 128] = v_ref[:, hp * 128:(hp + 1) * 128]
            vk_ref[:, hp * 256 + 128:(hp + 1) * 256] = aug

    row = lambda n, c: pl.BlockSpec((tb, n), lambda i: (i, c))
    return pl.pallas_call(
        body,
        name="fcum",
        grid=(T // tb,),
        in_specs=[row(FPAD, 0), _full((1, FPAD)), row(AW, 1), row(AW, 2)],
        out_specs=[pl.BlockSpec((8, tb), lambda i: (0, i)), row(2 * AW, 0), row(2 * AW, 0)],
        out_shape=[jax.ShapeDtypeStruct((8, T), F32), jax.ShapeDtypeStruct((T, 2 * AW), BF16),
                   jax.ShapeDtypeStruct((T, 2 * AW), BF16)],
        scratch_shapes=[pltpu.VMEM((1, FPAD), F32), pltpu.VMEM((128, 128), F32), pltpu.VMEM((12, 128, 128), BF16)],
        compiler_params=_params(1),
    )(fz, fb, qkv, qkv)


def _fold_rows(r, t, nq):
    if nq == 1:
        return r, t
    low = t <= r
    return jnp.where(low, r, nq - 1 - r), jnp.where(low, t, t - r - 1)


def _fold_cols(r, t, nq):
    if nq == 1:
        return r, t
    first = t < nq - r
    j = jnp.where(first, r, nq - 1 - r)
    return j, jnp.where(first, r + t, nq - 1 - r + (t - (nq - r)))


PAIRS = 4


def _fold_grid(nq):
    assert (nq == 1 or nq % 2 == 0) and PAIRS == 4
    return (4 // PAIRS, 1, 1) if nq == 1 else (4 // PAIRS, nq // 2, nq + 1)


def _head_rows(x2, hh, scale):
    is_a = lax.broadcasted_iota(jnp.int32, (1, 128), 1) < HD
    keep = is_a if hh == 0 else jnp.logical_not(is_a)
    return jnp.where(keep, x2, jnp.zeros_like(x2)) * scale


def _attn_fwd(qkv, kk, vt, ct):
    T = qkv.shape[0]
    tq = _tile(T, 512)
    tk = tq
    nq = T // tq

    def body(q_ref, kk_ref, vt_ref, ctq_ref, o_ref, lsec_ref, qw_s, m_s, acc_s):
        i, j = _fold_rows(pl.program_id(1), pl.program_id(2), nq)
        sub8 = lax.broadcasted_iota(jnp.int32, (8, 1), 0)

        def cref_of(pp, hh):
            head = 2 * (PAIRS * pl.program_id(0) + pp) + hh
            return jnp.sum(jnp.where(sub8 == head, ctq_ref[:, 0:1], 0.0), axis=0, keepdims=True)

        @pl.when(j == 0)
        def _():
            for pp in range(PAIRS):
                q2 = q_ref[:, pp * 128:(pp + 1) * 128]
                for hh in range(2):
                    rows = slice(hh * tq, (hh + 1) * tq)
                    qw_s[pp, rows, 0:128] = _head_rows(q2, hh, 0.125)
                    qw_s[pp, rows, 128:256] = _aug_lanes(tq, AUG_A if hh == 0 else AUG_B, _split3(cref_of(pp, hh)))
            m_s[...] = jnp.full_like(m_s, MASKV)
            acc_s[...] = jnp.zeros_like(acc_s)

        def step(masked):
            if masked:
                causal = lax.broadcasted_iota(jnp.int32, (tk, tq), 0) <= lax.broadcasted_iota(jnp.int32, (tk, tq), 1)
            logits = lambda pp: _nt(kk_ref[:, pp * 256:(pp + 1) * 256], qw_s[pp])
            s_next = logits(0)
            for pp in range(PAIRS):
                s2 = s_next
                if pp + 1 < PAIRS:
                    s_next = logits(pp + 1)
                for hh in range(2):
                    n = 2 * pp + hh
                    s = s2[:, hh * tq:(hh + 1) * tq]
                    if masked:
                        s = jnp.where(causal, s, MASKV)
                    m_prev = m_s[n]
                    m_new = jnp.maximum(m_prev, jnp.max(s, axis=0, keepdims=True))
                    pr = jnp.exp(s - m_new)
                    m_s[n] = m_new
                    acc_s[n] = jnp.exp(m_prev - m_new) * acc_s[n] + _nn(vt_ref[n * 128:(n + 1) * 128, :], pr.astype(BF16))

        @pl.when(j < i)
        def _():
            step(False)

        @pl.when(j == i)
        def _():
            step(True)
            sub = lax.broadcasted_iota(jnp.int32, (128, 1), 0)
            for pp in range(PAIRS):
                outs, lrow = [], []
                for hh in range(2):
                    n = 2 * pp + hh
                    den = acc_s[n, HD:HD + 1, :]
                    outs.append(acc_s[n, 0:HD, :] * (1.0 / den))
                    lrow.append(m_s[n] + jnp.log(den) - cref_of(pp, hh))
                o_ref[:, pp * 128:(pp + 1) * 128] = jnp.concatenate(outs, axis=0).T
                lsec_ref[:, pp * 128:(pp + 1) * 128] = jnp.where(sub == 0, lrow[0], jnp.where(sub == 1, lrow[1], 0.0)).T

    qi = lambda r, t: _fold_rows(r, t, nq)[0]
    kj = lambda r, t: _fold_rows(r, t, nq)[1]
    return pl.pallas_call(
        body,
        name="attn_fwd",
        grid=_fold_grid(nq),
        in_specs=[
            pl.BlockSpec((tq, PAIRS * 128), lambda g, r, t: (qi(r, t), g)),
            pl.BlockSpec((tk, PAIRS * 256), lambda g, r, t: (kj(r, t), g)),
            pl.BlockSpec((PAIRS * 256, tk), lambda g, r, t: (g, kj(r, t))),
            pl.BlockSpec((8, tq), lambda g, r, t: (0, qi(r, t))),
        ],
        out_specs=[pl.BlockSpec((tq, PAIRS * 128), lambda g, r, t: (qi(r, t), g))] * 2,
        out_shape=[jax.ShapeDtypeStruct((T, AW), F32)] * 2,
        scratch_shapes=[pltpu.VMEM((PAIRS, 2 * tq, 256), BF16), pltpu.VMEM((2 * PAIRS, 1, tq), F32),
                        pltpu.VMEM((2 * PAIRS, 128, tq), F32)],
        compiler_params=_params(3),
    )(qkv, kk, vt, ct)


def _sgu_forward(us_ref, lng, lnb, w_ref, bt_ref, mixed_s, vnb_s, tm):
    is_a = lax.broadcasted_iota(jnp.int32, (1, 128), 1) < HD
    u = us_ref[:, 0:SW]
    vs = us_ref[:, SW:NUS]
    ug, tu = _gelu(u)
    vg, tv = _gelu(vs)
    mu = jnp.mean(vg, axis=-1, keepdims=True)
    xc = vg - mu
    rstd = lax.rsqrt(jnp.mean(xc * xc, axis=-1, keepdims=True) + EPS)
    vhat = xc * rstd
    vnb_s[...] = (vhat * lng + lnb).astype(BF16)
    rr = lax.broadcasted_iota(jnp.int32, (CH, CH), 0)
    cc = lax.broadcasted_iota(jnp.int32, (CH, CH), 1)
    tril = cc <= rr
    for jj in range(4):
        wa = jnp.where(tril, w_ref[2 * jj], 0.0).astype(BF16)
        wb = jnp.where(tril, w_ref[2 * jj + 1], 0.0).astype(BF16)
        ba = bt_ref[:, 2 * jj:2 * jj + 1]
        bb = bt_ref[:, 2 * jj + 1:2 * jj + 2]
        for ch in range(tm // CH):
            rs, cs = slice(ch * CH, (ch + 1) * CH), slice(jj * 128, (jj + 1) * 128)
            vn2 = vnb_s[rs, cs]
            mixed_s[rs, cs] = jnp.where(is_a, _nn(wa, vn2) + ba, _nn(wb, vn2) + bb)
    mixed = mixed_s[...]
    return u, vs, ug, tu, tv, vhat, rstd, mixed, ug * mixed


def _sgu_out(us, yatt, x, lng, lnb, sgw, sgbt, gatt, gsg, wout, gpm):
    T = us.shape[0]
    tm = _tile(T, 512)

    def body(us_ref, ya_ref, x_ref, lng_ref, lnb_ref, w_ref, bt_ref, ga_ref, gs_ref, wo_ref, gp_ref,
             h1_ref, yb_ref, o_ref, mixed_s, vnb_s):
        ysg = _sgu_forward(us_ref, lng_ref[...], lnb_ref[...], w_ref, bt_ref, mixed_s, vnb_s, tm)[-1]
        yb_ref[:, 0:AW] = _rms_fwd(ya_ref[...], ga_ref[...]).astype(BF16)
        yb_ref[:, AW:D] = _rms_fwd(ysg, gs_ref[...]).astype(BF16)
        o = _nn(yb_ref[...], wo_ref[...])
        o_ref[...] = o
        h1_ref[...] = x_ref[...] + _rms_fwd(o, gp_ref[...])

    row = lambda n: pl.BlockSpec((tm, n), lambda i: (i, 0))
    return pl.pallas_call(
        body,
        name="sgu_out",
        grid=(T // tm,),
        in_specs=[row(NUS), row(AW), row(D), _full((1, SW)), _full((1, SW)), _full((8, CH, CH)), _full((CH, 8)),
                  _full((1, AW)), _full((1, SW)), _full((D, D)), _full((1, D))],
        out_specs=[row(D), row(D), row(D)],
        out_shape=[jax.ShapeDtypeStruct((T, D), F32), jax.ShapeDtypeStruct((T, D), BF16), jax.ShapeDtypeStruct((T, D), F32)],
        scratch_shapes=[pltpu.VMEM((tm, SW), F32), pltpu.VMEM((tm, SW), BF16)],
        compiler_params=_params(1),
    )(us, yatt, x, lng, lnb, sgw, sgbt, gatt, gsg, wout, gpm)


def _ffn_fwd(h1, gpre, w1g, w2g, gpost):
    T = h1.shape[0]
    tm = _tile(T, 512)
    nb, hb = w1g.shape[0], w1g.shape[2]

    def body(h1_ref, gpre_ref, w1_ref, w2_ref, gpost_ref, c2_ref, f1_ref, ff_ref, h2_ref):
        h1 = h1_ref[...]
        c2 = _rms_fwd(h1, gpre_ref[...]).astype(BF16)
        c2_ref[...] = c2
        ff = jnp.zeros((tm, D), F32)
        for j in range(nb):
            f1 = _nn(c2, w1_ref[j])
            f1_ref[:, j * hb:(j + 1) * hb] = f1.astype(BF16)
            r = jnp.maximum(f1, 0.0)
            ff = ff + _nn((r * r).astype(BF16), w2_ref[j])
        ff_ref[...] = ff
        h2_ref[...] = h1 + _rms_fwd(ff, gpost_ref[...])

    row = lambda n: pl.BlockSpec((tm, n), lambda i: (i, 0))
    once = lambda shape: pl.BlockSpec(shape, lambda i: (0,) * len(shape), pipeline_mode=pl.Buffered(1))
    return pl.pallas_call(
        body,
        name="ffn_fwd",
        grid=(T // tm,),
        in_specs=[row(D), _full((1, D)), once((nb, D, hb)), once((nb, hb, D)), _full((1, D))],
        out_specs=[row(D), row(DFF), row(D), row(D)],
        out_shape=[jax.ShapeDtypeStruct((T, D), BF16), jax.ShapeDtypeStruct((T, DFF), BF16),
                   jax.ShapeDtypeStruct((T, D), F32), jax.ShapeDtypeStruct((T, D), F32)],
        compiler_params=_params(1, VMEM_LIMIT_BIG),
    )(h1, gpre, w1g, w2g, gpost)


def _ple_loss(h2, p, tgt, wg, bg, wpe):
    T = h2.shape[0]
    tm = _tile(T, 512)

    def body(h2_ref, p_ref, t_ref, wg_ref, bg_ref, wpe_ref, dh2_ref, dbg_ref, loss_ref, dwg_ref, dwpe_ref):
        @pl.when(pl.program_id(0) == 0)
        def _():
            for r in (dbg_ref, loss_ref, dwg_ref, dwpe_ref):
                r[...] = jnp.zeros_like(r)

        h2 = h2_ref[...]
        h2b = h2.astype(BF16)
        gate = jax.nn.sigmoid(_nn(h2b, wg_ref[...]) + bg_ref[...])
        pb = p_ref[...].astype(BF16)
        pe = _nn(pb, wpe_ref[...])
        diff = (h2 + gate * pe) - t_ref[...]
        loss_ref[...] += jnp.sum(diff * diff)
        dh3 = diff * (1.0 / D)
        dpre = (dh3 * pe) * (gate * (1.0 - gate))
        dpre_b = dpre.astype(BF16)
        dbg_ref[...] += jnp.sum(dpre, axis=0, keepdims=True)
        dh2_ref[...] = dh3 + _nt(dpre_b, wg_ref[...])
        dwg_ref[...] += _nn(h2b.T, dpre_b)
        dwpe_ref[...] += _nn(pb.T, (dh3 * gate).astype(BF16))

    row = lambda n: pl.BlockSpec((tm, n), lambda i: (i, 0))
    once = lambda shape: pl.BlockSpec(shape, lambda i: (0,) * len(shape), pipeline_mode=pl.Buffered(1))
    return pl.pallas_call(
        body,
        name="ple_loss",
        grid=(T // tm,),
        in_specs=[row(D), row(PLE), row(D), _full((D, D)), _full((1, D)), _full((PLE, D))],
        out_specs=[row(D), _full((1, D)), _full((8, 128)), once((D, D)), once((PLE, D))],
        out_shape=[jax.ShapeDtypeStruct((T, D), F32), jax.ShapeDtypeStruct((1, D), F32),
                   jax.ShapeDtypeStruct((8, 128), F32), jax.ShapeDtypeStruct((D, D), F32),
                   jax.ShapeDtypeStruct((PLE, D), F32)],
        compiler_params=_params(1),
    )(h2, p, tgt, wg, bg, wpe)


def _ffn_bwd(dh2, ff, h1, f1, w1g, w2g, gpost, gpre):
    T = dh2.shape[0]
    tm = _tile(T, 512)
    nb, hb = w1g.shape[0], w1g.shape[2]

    def body(dh2_ref, ff_ref, h1_ref, f1_ref, w1_ref, w2_ref, gpost_ref, gpre_ref,
             dffb_ref, df1_ref, dh1_ref, dgpost_ref, dgpre_ref):
        @pl.when(pl.program_id(0) == 0)
        def _():
            dgpost_ref[...] = jnp.zeros_like(dgpost_ref)
            dgpre_ref[...] = jnp.zeros_like(dgpre_ref)

        dh2 = dh2_ref[...]
        dff, dg = _rms_bwd(dh2, ff_ref[...], gpost_ref[...])
        dffb = dff.astype(BF16)
        dffb_ref[...] = dffb
        dgpost_ref[...] += dg
        dc2 = jnp.zeros((tm, D), F32)
        for j in range(nb):
            cols = slice(j * hb, (j + 1) * hb)
            dact = _nt(dffb, w2_ref[j])
            df1 = (dact * (2.0 * jnp.maximum(f1_ref[:, cols].astype(F32), 0.0))).astype(BF16)
            df1_ref[:, cols] = df1
            dc2 = dc2 + _nt(df1, w1_ref[j])
        dx, dg = _rms_bwd(dc2, h1_ref[...], gpre_ref[...])
        dh1_ref[...] = dh2 + dx
        dgpre_ref[...] += dg

    row = lambda n: pl.BlockSpec((tm, n), lambda i: (i, 0))
    once = lambda shape: pl.BlockSpec(shape, lambda i: (0,) * len(shape), pipeline_mode=pl.Buffered(1))
    return pl.pallas_call(
        body,
        name="ffn_bwd",
        grid=(T // tm,),
        in_specs=[row(D), row(D), row(D), row(DFF), once((nb, D, hb)), once((nb, hb, D)), _full((1, D)), _full((1, D))],
        out_specs=[row(D), row(DFF), row(D), _full((1, D)), _full((1, D))],
        out_shape=[jax.ShapeDtypeStruct((T, D), BF16), jax.ShapeDtypeStruct((T, DFF), BF16),
                   jax.ShapeDtypeStruct((T, D), F32), jax.ShapeDtypeStruct((1, D), F32),
                   jax.ShapeDtypeStruct((1, D), F32)],
        compiler_params=_params(1, VMEM_LIMIT_BIG),
    )(dh2, ff, h1, f1, w1g, w2g, gpost, gpre)


def _mix_bwd(dh1, o, us, yatt, qkv, lsec, yb, lng, lnb, sgw, sgwt, sgbt, gatt, gsg, wout, gpm):
    T = dh1.shape[0]
    tm = _tile(T, 512)

    def body(dh1_ref, o_ref, us_ref, ya_ref, q_ref, l_ref, yb_ref, lng_ref, lnb_ref, w_ref, wt_ref, bt_ref, ga_ref, gs_ref,
             wo_ref, gp_ref, qw_ref, dow_ref, dus_ref, dwo_ref, dw_ref, dbt_ref, dlng_ref, dlnb_ref, dga_ref, dgs_ref,
             dgp_ref, mixed_s, vnb_s, dvn_s):
        @pl.when(pl.program_id(0) == 0)
        def _():
            for r in (dwo_ref, dw_ref, dbt_ref, dlng_ref, dlnb_ref, dga_ref, dgs_ref, dgp_ref):
                r[...] = jnp.zeros_like(r)

        is_a = lax.broadcasted_iota(jnp.int32, (1, 128), 1) < HD
        lane = lax.broadcasted_iota(jnp.int32, (1, 128), 1)
        do, dg = _rms_bwd(dh1_ref[...], o_ref[...], gp_ref[...])
        dgp_ref[...] += dg
        dob = do.astype(BF16)
        dwo_ref[...] += _nn(yb_ref[...].T, dob)
        dy = _nt(dob, wo_ref[...])
        ya = ya_ref[...]
        datt, dg = _rms_bwd(dy[:, 0:AW], ya, ga_ref[...])
        dga_ref[...] += dg
        _attn_operands(q_ref[...], datt, ya, l_ref[...], qw_ref, dow_ref)

        lng = lng_ref[...]
        u, vs, ug, tu, tv, vhat, rstd, mixed, ysg = _sgu_forward(us_ref, lng, lnb_ref[...], w_ref, bt_ref, mixed_s, vnb_s, tm)
        dysg, dg = _rms_bwd(dy[:, AW:D], ysg, gs_ref[...])
        dgs_ref[...] += dg
        dus_ref[:, 0:SW] = ((dysg * mixed) * _gelu_grad(u, tu)).astype(BF16)
        dmix = dysg * ug

        rr = lax.broadcasted_iota(jnp.int32, (CH, CH), 0)
        cc = lax.broadcasted_iota(jnp.int32, (CH, CH), 1)
        tril = cc <= rr
        triu = cc >= rr
        for jj in range(4):
            wta = jnp.where(triu, wt_ref[2 * jj], 0.0).astype(BF16)
            wtb = jnp.where(triu, wt_ref[2 * jj + 1], 0.0).astype(BF16)
            for ch in range(tm // CH):
                rs, cs = slice(ch * CH, (ch + 1) * CH), slice(jj * 128, (jj + 1) * 128)
                dm2 = dmix[rs, cs]
                dma = jnp.where(is_a, dm2, 0.0)
                dmb = jnp.where(is_a, 0.0, dm2)
                dma_b, dmb_b = dma.astype(BF16), dmb.astype(BF16)
                vn2 = vnb_s[rs, cs]
                dw_ref[2 * jj] += jnp.where(tril, _nt(dma_b, vn2), 0.0)
                dw_ref[2 * jj + 1] += jnp.where(tril, _nt(dmb_b, vn2), 0.0)
                dvn_s[rs, cs] = _nn(wta, dma_b) + _nn(wtb, dmb_b)
                dba = jnp.sum(dma, axis=1, keepdims=True)
                dbb = jnp.sum(dmb, axis=1, keepdims=True)
                dbt_ref[...] += jnp.where(lane == 2 * jj, dba, 0.0) + jnp.where(lane == 2 * jj + 1, dbb, 0.0)

        dvn = dvn_s[...]
        dlng_ref[...] += jnp.sum(dvn * vhat, axis=0, keepdims=True)
        dlnb_ref[...] += jnp.sum(dvn, axis=0, keepdims=True)
        dvh = dvn * lng
        dvg = rstd * (dvh - jnp.mean(dvh, axis=-1, keepdims=True) - vhat * jnp.mean(dvh * vhat, axis=-1, keepdims=True))
        dus_ref[:, SW:NUS] = (dvg * _gelu_grad(vs, tv)).astype(BF16)

    row = lambda n: pl.BlockSpec((tm, n), lambda i: (i, 0))
    return pl.pallas_call(
        body,
        name="mix_bwd",
        grid=(T // tm,),
        in_specs=[row(D), row(D), row(NUS), row(AW), row(AW), row(AW), row(D), _full((1, SW)), _full((1, SW)),
                  _full((8, CH, CH)), _full((8, CH, CH)), _full((CH, 8)), _full((1, AW)), _full((1, SW)), _full((D, D)),
                  _full((1, D))],
        out_specs=[row(4 * AW), row(4 * AW), row(NUS),
                   pl.BlockSpec((D, D), lambda i: (0, 0), pipeline_mode=pl.Buffered(1)), _full((8, CH, CH)),
                   _full((CH, 128)), _full((1, SW)), _full((1, SW)), _full((1, AW)), _full((1, SW)), _full((1, D))],
        out_shape=[jax.ShapeDtypeStruct((T, 4 * AW), BF16),
                   jax.ShapeDtypeStruct((T, 4 * AW), BF16), jax.ShapeDtypeStruct((T, NUS), BF16),
                   jax.ShapeDtypeStruct((D, D), F32),
                   jax.ShapeDtypeStruct((8, CH, CH), F32), jax.ShapeDtypeStruct((CH, 128), F32),
                   jax.ShapeDtypeStruct((1, SW), F32), jax.ShapeDtypeStruct((1, SW), F32),
                   jax.ShapeDtypeStruct((1, AW), F32), jax.ShapeDtypeStruct((1, SW), F32), jax.ShapeDtypeStruct((1, D), F32)],
        scratch_shapes=[pltpu.VMEM((tm, SW), F32), pltpu.VMEM((tm, SW), BF16), pltpu.VMEM((tm, SW), F32)],
        compiler_params=_params(1, VMEM_LIMIT_BIG),
    )(dh1, o, us, yatt, qkv, lsec, yb, lng, lnb, sgw, sgwt, sgbt, gatt, gsg, wout, gpm)


def _attn_operands(q, do, o, lsec, qw_ref, dow_ref):
    rows = do.shape[0]
    feat = lax.broadcasted_iota(jnp.int32, (AW, 128), 0)
    head = lax.broadcasted_iota(jnp.int32, (AW, 128), 1)
    sel = jnp.where((feat >= head * HD) & (feat < (head + 1) * HD), 1.0, 0.0)
    delta = jnp.dot(do * o, sel, precision=lax.Precision.HIGHEST, preferred_element_type=F32)
    for hp in range(4):
        cols = slice(hp * 128, (hp + 1) * 128)
        for hh in range(2):
            base = (2 * hp + hh) * 256
            lc = lsec[:, hp * 128 + hh:hp * 128 + hh + 1]
            d_h = delta[:, 2 * hp + hh:2 * hp + hh + 1]
            qw_ref[:, base:base + 128] = _head_rows(q[:, cols], hh, 0.125)
            qw_ref[:, base + 128:base + 256] = _aug_lanes(rows, AUG_A if hh == 0 else AUG_B, _split3(-lc))
            dow_ref[:, base:base + 128] = _head_rows(do[:, cols], hh, 1.0).astype(BF16)
            dow_ref[:, base + 128:base + 256] = _aug_lanes(rows, None, _split3(-d_h))


def _attn_bwd(kk, vk, kt, qw, dow, after):
    T = kk.shape[0]
    tq = _tile(T, 512)
    tk = tq
    nq = T // tq

    def body(kk_ref, vk_ref, kt_ref, qw_ref, dow_ref, after_ref, dqt_ref, dcq_ref, dk_ref, dv_ref, dck_ref, dk_s, dv_s, dck_s):
        j, i = _fold_cols(pl.program_id(1), pl.program_id(2), nq)
        sub8 = lax.broadcasted_iota(jnp.int32, (8, 1), 0)
        lane = lax.broadcasted_iota(jnp.int32, (1, 128), 1)

        @pl.when((pl.program_id(1) == 0) & (pl.program_id(2) == 0))
        def _():
            dqt_ref[...] = jnp.zeros_like(dqt_ref)
            dcq_ref[...] = jnp.zeros_like(dcq_ref)

        @pl.when(i == j)
        def _():
            dk_s[...] = jnp.zeros_like(dk_s)
            dv_s[...] = jnp.zeros_like(dv_s)
            dck_s[...] = jnp.zeros_like(dck_s)

        def step(masked):
            cols = pl.ds(pl.multiple_of(i * tq, tq), tq)
            sub = lax.broadcasted_iota(jnp.int32, (128, 1), 0)
            if masked:
                causal = lax.broadcasted_iota(jnp.int32, (tk, tq), 0) <= lax.broadcasted_iota(jnp.int32, (tk, tq), 1)

            def logits(n):
                pair, base = n // 2, n * 256
                return (_nt(kk_ref[:, pair * 256:(pair + 1) * 256], qw_ref[:, base:base + 256]),
                        _nt(vk_ref[:, pair * 256:(pair + 1) * 256], dow_ref[:, base:base + 256]))

            ahead = logits(0)
            dcq = jnp.zeros((8, tq), F32)
            dck = jnp.zeros((tk, 128), F32)
            for pp in range(PAIRS):
                lanes = slice(pp * 128, (pp + 1) * 128)
                kt2 = kt_ref[lanes, :] * 0.125
                dv = jnp.zeros((tk, 128), F32)
                dk = jnp.zeros((tk, 128), F32)
                dqts = []
                for hh in range(2):
                    head = 2 * pp + hh
                    base = head * 256
                    qw_h = qw_ref[:, base:base + 256]
                    dow_h = dow_ref[:, base:base + 256]
                    logp, dp = ahead
                    if head + 1 < 2 * PAIRS:
                        ahead = logits(head + 1)
                    pr = jnp.exp(logp)
                    if masked:
                        pr = jnp.where(causal, pr, 0.0)
                    ds = pr * dp
                    ds_b = ds.astype(BF16)
                    dv = dv + _nn(pr.astype(BF16), dow_h[:, 0:128])
                    dk = dk + _nn(ds_b, qw_h[:, 0:128])
                    dqts.append(_nn(kt2, ds_b))
                    dcq = dcq + jnp.where(sub8 == head, jnp.sum(ds, axis=0, keepdims=True), 0.0)
                    dck = dck - jnp.where(lane == head, jnp.sum(ds, axis=1, keepdims=True), 0.0)
                dv_s[:, lanes] += dv
                dk_s[:, lanes] += dk
                dqt_ref[lanes, cols] += jnp.where(sub < HD, dqts[0], dqts[1])
            dck_s[...] += dck
            dcq_ref[:, cols] += dcq

        @pl.when(i > j)
        def _():
            step(False)

        @pl.when(i == j)
        def _():
            step(True)

        @pl.when(i == nq - 1)
        def _():
            dk_ref[...] = dk_s[...].astype(BF16)
            dv_ref[...] = dv_s[...].astype(BF16)
            dck_ref[...] = dck_s[...]

    kj = lambda r, t: _fold_cols(r, t, nq)[0]
    qi = lambda r, t: _fold_cols(r, t, nq)[1]
    krow = lambda g, r, t: (kj(r, t), g)
    qrow = lambda g, r, t: (qi(r, t), g)
    return pl.pallas_call(
        body,
        name="attn_bwd",
        grid=_fold_grid(nq),
        in_specs=[
            pl.BlockSpec((tk, PAIRS * 256), krow),
            pl.BlockSpec((tk, PAIRS * 256), krow),
            pl.BlockSpec((PAIRS * 128, tk), lambda g, r, t: (g, kj(r, t))),
            pl.BlockSpec((tq, PAIRS * 512), qrow),
            pl.BlockSpec((tq, PAIRS * 512), qrow),
            pl.BlockSpec(after.shape, lambda g, r, t: (0,) * after.ndim),
        ],
        out_specs=[
            pl.BlockSpec((PAIRS * 128, T), lambda g, r, t: (g, 0), pipeline_mode=pl.Buffered(1)),
            pl.BlockSpec((8, T), lambda g, r, t: (0, 0), pipeline_mode=pl.Buffered(1)),
            pl.BlockSpec((tk, PAIRS * 128), krow),
            pl.BlockSpec((tk, PAIRS * 128), krow),
            pl.BlockSpec((tk, 128), lambda g, r, t: (kj(r, t), 0)),
        ],
        out_shape=[jax.ShapeDtypeStruct((AW, T), F32), jax.ShapeDtypeStruct((8, T), F32),
                   jax.ShapeDtypeStruct((T, AW), BF16), jax.ShapeDtypeStruct((T, AW), BF16),
                   jax.ShapeDtypeStruct((T, FPAD), F32)],
        scratch_shapes=[pltpu.VMEM((tk, PAIRS * 128), F32), pltpu.VMEM((tk, PAIRS * 128), F32),
                        pltpu.VMEM((tk, 128), F32)],
        compiler_params=_params(3),
    )(kk, vk, kt, qw, dow, after)


def _fgate_bwd(dcq, dck, fz, fb):
    T = dck.shape[0]
    tb = _tile(T, 2048)
    nb = T // tb

    def body(dcq_ref, dck_ref, fz_ref, fb_ref, df_ref, dfb_ref, carry):
        @pl.when(pl.program_id(0) == 0)
        def _():
            carry[...] = jnp.zeros_like(carry)
            dfb_ref[...] = jnp.zeros_like(dfb_ref)

        head = lax.broadcasted_iota(jnp.int32, (8, FPAD), 0)
        eye = jnp.where(head == lax.broadcasted_iota(jnp.int32, (8, FPAD), 1), 1.0, 0.0)
        dcv = dck_ref[...] + lax.dot_general(dcq_ref[...], eye, (((0,), (0,)), ((), ())),
                                             precision=lax.Precision.HIGHEST, preferred_element_type=F32)
        r = lax.broadcasted_iota(jnp.int32, (128, 128), 0)
        cc = lax.broadcasted_iota(jnp.int32, (128, 128), 1)
        tri = (cc >= r).astype(F32)
        run = carry[...]
        parts = []
        for blk in reversed(range(tb // 128)):
            dc_b = dcv[blk * 128:(blk + 1) * 128, :]
            parts.append(jnp.dot(tri, dc_b, precision=lax.Precision.HIGHEST, preferred_element_type=F32) + run)
            run = run + jnp.sum(dc_b, axis=0, keepdims=True)
        carry[...] = run
        dlf = jnp.concatenate(parts[::-1], axis=0)
        lane = lax.broadcasted_iota(jnp.int32, (tb, FPAD), 1)
        df = jnp.where(lane < 8, dlf * jax.nn.sigmoid(-(fz_ref[...] + fb_ref[...])), 0.0)
        df_ref[...] = df.astype(BF16)
        dfb_ref[...] += jnp.sum(df, axis=0, keepdims=True)

    rev = pl.BlockSpec((tb, FPAD), lambda i: (nb - 1 - i, 0))
    return pl.pallas_call(
        body,
        name="fgate_bwd",
        grid=(nb,),
        in_specs=[pl.BlockSpec((8, tb), lambda i: (0, nb - 1 - i)), rev, rev, _full((1, FPAD))],
        out_specs=[rev, _full((1, FPAD))],
        out_shape=[jax.ShapeDtypeStruct((T, FPAD), BF16), jax.ShapeDtypeStruct((1, FPAD), F32)],
        scratch_shapes=[pltpu.VMEM((1, FPAD), F32)],
        compiler_params=_params(1),
    )(dcq, dck, fz, fb)


def _inproj_bwd(dqt, dk, dv, dus, dfz, wtp, x, dh1, g, adam):
    T = x.shape[0]
    tm = _tile(T, 512)
    adam_in, adam_out, adam_shape = _adam_specs(adam[1], T // tm)

    def body(dq_ref, dk_ref, dv_ref, dus_ref, dfz_ref, w_ref, x_ref, dh1_ref, g_ref, *rest):
        gx_ref, dg_ref = rest[4:6]
        _adam_step(*rest[0:4], *rest[6:10])

        @pl.when(pl.program_id(0) == 0)
        def _():
            dg_ref[...] = jnp.zeros_like(dg_ref)

        da = _tn(dq_ref[...].astype(BF16), w_ref[0:AW, :])
        da += _nn(dk_ref[...], w_ref[AW:2 * AW, :])
        da += _nn(dv_ref[...], w_ref[2 * AW:NQKV, :])
        da += _nn(dus_ref[...], w_ref[NQKV:NQKV + NUS, :])
        da += _nn(dfz_ref[...], w_ref[NQKV + NUS:ZP, :])
        dx, dg = _rms_bwd(da, x_ref[...], g_ref[...])
        gx_ref[...] = dh1_ref[...] + dx
        dg_ref[...] += dg

    row = lambda n: pl.BlockSpec((tm, n), lambda i: (i, 0))
    return pl.pallas_call(
        body,
        name="inproj_bwd",
        grid=(T // tm,),
        in_specs=[pl.BlockSpec((AW, tm), lambda i: (0, i)), row(AW), row(AW), row(NUS), row(FPAD), _full((ZP, D)),
                  row(D), row(D), _full((1, D))] + adam_in,
        out_specs=[row(D), _full((1, D))] + adam_out,
        out_shape=[jax.ShapeDtypeStruct((T, D), F32), jax.ShapeDtypeStruct((1, D), F32)] + adam_shape,
        compiler_params=_params(1),
    )(dqt, dk, dv, dus, dfz, wtp, x, dh1, g, *adam)


def _sq_relu(f1):
    r = jnp.maximum(f1.astype(F32), 0.0)
    return (r * r).astype(BF16)


def _wgrad(a, bs, name, a_fn=None, out_dtype=F32, transposed=False, adam=None, stack=None):
    T, K = a.shape
    tt = _tile(T, 1024)
    nb = len(bs)
    narrow = out_dtype != F32
    adam_in, adam_out, adam_shape = ([], [], []) if adam is None else _adam_specs(adam[1], T // tt)
    na = len(adam_in)
    assert stack is None or (transposed and all(first % 8 == 0 and rows % 8 == 0 for first, rows in stack))

    def out_dims(b, layout):
        if transposed:
            return (b.shape[0] if layout == "t" else b.shape[1], K)
        if layout == "t":
            return (K, b.shape[0])
        N = b.shape[1]
        if layout is None:
            return (K, N)
        return (N // layout[1], K, layout[1]) if layout[0] == "col" else (K // layout[1], layout[1], N)

    shapes = [out_dims(b, layout) for b, layout in bs]
    if stack is not None:
        shapes = [(max(first + rows for first, rows in stack), K)]
    n_out = len(shapes)

    def body(*refs):
        a_ref, b_refs, o_refs = refs[0], refs[1:1 + nb], refs[1 + nb + na:1 + nb + na + n_out]
        sums = refs[1 + nb + 2 * na + n_out:] if narrow else o_refs
        if adam is not None:
            _adam_step(*refs[1 + nb:1 + nb + na], *refs[1 + nb + na + n_out:1 + nb + 2 * na + n_out])

        @pl.when(pl.program_id(0) == 0)
        def _():
            for acc in sums:
                acc[...] = jnp.zeros_like(acc)

        accs = sums if stack is None else [sums[0].at[pl.ds(first, rows)] for first, rows in stack]
        av = (a_ref[...] if a_fn is None else a_fn(a_ref[...])).astype(BF16)
        at = None if transposed else av.T
        for (b, layout), b_ref, o_ref in zip(bs, b_refs, accs):
            if transposed:
                bv = b_ref[...].astype(BF16)
                o_ref[...] += (_nn(bv, av) if layout == "t" else _tn(bv, av))[0:o_ref.shape[0]]
            elif layout is None:
                o_ref[...] += _nn(at, b_ref[...].astype(BF16))
            elif layout == "t":
                o_ref[...] += _nt(at, b_ref[...].astype(BF16))
            elif layout[0] == "col":
                n = layout[1]
                for k in range(b.shape[1] // n):
                    o_ref[k] += _nn(at, b_ref[:, k * n:(k + 1) * n].astype(BF16))
            else:
                n = layout[1]
                bv = b_ref[...].astype(BF16)
                for k in range(K // n):
                    o_ref[k] += _nn(at[k * n:(k + 1) * n, :], bv)

        if narrow:
            @pl.when(pl.program_id(0) == T // tt - 1)
            def _():
                for o_ref, acc in zip(o_refs, sums):
                    o_ref[...] = acc[...].astype(out_dtype)

    once = lambda shape: pl.BlockSpec(shape, lambda t: (0,) * len(shape), pipeline_mode=pl.Buffered(1))
    res = pl.pallas_call(
        body,
        name=name,
        grid=(T // tt,),
        in_specs=[pl.BlockSpec((tt, K), lambda t: (t, 0))] + [
            pl.BlockSpec((b.shape[0], tt), lambda t: (0, t)) if layout == "t" else pl.BlockSpec((tt, b.shape[1]), lambda t: (t, 0))
            for b, layout in bs] + adam_in,
        out_specs=[once(s) for s in shapes] + adam_out,
        out_shape=[jax.ShapeDtypeStruct(s, out_dtype) for s in shapes] + adam_shape,
        scratch_shapes=[pltpu.VMEM(s, F32) for s in shapes] if narrow else [],
        compiler_params=_params(1, VMEM_LIMIT_BIG),
    )(a, *[b for b, _ in bs], *(adam or ()))
    return res


def _adam_math(w, g, m, v):
    m = ADAM_B1 * m + (1.0 - ADAM_B1) * g
    v = ADAM_B2 * v + (1.0 - ADAM_B2) * (g * g)
    m_hat = m / (1.0 - ADAM_B1 ** ADAM_STEP)
    v_hat = v / (1.0 - ADAM_B2 ** ADAM_STEP)
    delta = -ADAM_LR * (m_hat / (jnp.sqrt(v_hat) + ADAM_EPS) + ADAM_WD * w)
    return delta, m, v


def _adam_step(p_ref, w_ref, m_ref, v_ref, g_ref, d_ref, nm_ref, nv_ref):
    g = p_ref[0].astype(F32)
    for s in range(1, NDEV):
        g = g + p_ref[s].astype(F32)
    g_ref[...] = g
    d_ref[...], nm_ref[...], nv_ref[...] = _adam_math(w_ref[...], g, m_ref[...], v_ref[...])


def _adam_specs(w, steps):
    R, C = w.shape
    br = R // steps
    assert br * steps == R and (br % 8 == 0 or steps == 1)
    blk = pl.BlockSpec((br, C), lambda i: (i, 0))
    return ([pl.BlockSpec((NDEV, br, C), lambda i: (0, i, 0)), blk, blk, blk], [blk] * 4,
            [jax.ShapeDtypeStruct((R, C), F32)] * 4)


def _adam(parts, w, m, v, name):
    R = w.shape[0]
    steps = R // 128 if R % 128 == 0 else 1
    in_specs, out_specs, out_shape = _adam_specs(w, steps)

    def body(*refs):
        _adam_step(*refs)

    return pl.pallas_call(
        body,
        name=name,
        grid=(steps,),
        in_specs=in_specs,
        out_specs=out_specs,
        out_shape=out_shape,
        compiler_params=_params(1),
    )(parts, w, m, v)


def _adam_unit_rows(parts, w, m, v, name):
    R, _, C = w.shape
    blk = pl.BlockSpec((R, None, C), lambda i: (0, 0, 0))

    def body(*refs):
        _adam_step(*refs)

    return pl.pallas_call(
        body,
        name=name,
        grid=(1,),
        in_specs=[pl.BlockSpec((NDEV, R, C), lambda i: (0, 0, 0)), blk, blk, blk],
        out_specs=[blk] * 4,
        out_shape=[jax.ShapeDtypeStruct((R, 1, C), F32)] * 4,
        compiler_params=_params(1),
    )(parts, w, m, v)


_SMALL =(("sg_w", 8 * CH * CH), ("f_bias", 8), ("sg_ln_g", SW), ("sg_ln_b", SW), ("sg_b", 8 * CH), ("att_out_g", AW),
          ("sg_out_g", SW), ("pre_mix_g", D), ("post_mix_g", D), ("pre_ffn_g", D), ("post_ffn_g", D), ("ple_gate_b", D))
_SEG = 8 * 128


def _seg_rows(size):
    return 8 * (-(-size // _SEG))


_LATE = ("f_bias", "pre_mix_g")


def _pack(vals, late, loss_acc=None):
    parts = []
    for name, size in _SMALL:
        if (name in _LATE) == late:
            rows = _seg_rows(size)
            parts.append(jnp.pad(vals[name].reshape(-1), (0, rows * 128 - size)).reshape(rows, 128))
    if loss_acc is not None:
        parts.append(loss_acc)
    return jnp.concatenate(parts, axis=0)


def _adam_small(parts_early, parts_late, ws, ms, vs):
    n = len(_SMALL)
    names = [name for name, _ in _SMALL]

    def body(*refs):
        pe_ref, pl_ref, refs = refs[0], refs[1], refs[1:]
        w_refs, m_refs, v_refs = refs[1:1 + n], refs[1 + n:1 + 2 * n], refs[1 + 2 * n:1 + 3 * n]
        loss_ref, outs = refs[1 + 3 * n], refs[2 + 3 * n:]
        sums, row = [], [0, 0]
        for p_ref in (pe_ref, pl_ref):
            tot = p_ref[0]
            for s in range(1, NDEV):
                tot = tot + p_ref[s]
            sums.append(tot)
        for k, (name, size) in enumerate(_SMALL):
            dst = [outs[kind * n + k] for kind in range(4)]
            which = 1 if name in _LATE else 0
            g_all, r = sums[which], row[which]

            def update(g, idx):
                vals = (g,) + _adam_math(w_refs[k][idx], g, m_refs[k][idx], v_refs[k][idx])
                for d, val in zip(dst, vals):
                    d[idx] = val

            if name == "sg_w":
                for grp in range(8):
                    update(g_all[r + grp * CH:r + (grp + 1) * CH, :], (0, grp))
            elif name == "sg_b":
                update(g_all[r:r + 8, :], (0,))
            elif name == "f_bias":
                update(g_all[r:r + 1, 0:8], (slice(None),))
            else:
                update(jnp.concatenate([g_all[r + q:r + q + 1, :] for q in range(size // 128)], axis=1), (slice(None),))
            row[which] += _seg_rows(size)
        loss_ref[...] = sums[0][row[0]:row[0] + 1, 0:1] * (0.5 / D)

    arrs = [parts_early, parts_late] + [d[name] for d in (ws, ms, vs) for name in names]
    res = pl.pallas_call(
        body,
        name="adam_small",
        in_specs=[_full(a.shape) for a in arrs],
        out_specs=[_full((1, 1))] + [_full(ws[name].shape) for _ in range(4) for name in names],
        out_shape=[jax.ShapeDtypeStruct((1, 1), F32)] + [jax.ShapeDtypeStruct(ws[name].shape, F32) for _ in range(4) for name in names],
        compiler_params=pltpu.CompilerParams(vmem_limit_bytes=VMEM_LIMIT),
    )(*arrs)
    return res[0], {name: [res[1 + kind * n + k] for kind in range(4)] for k, name in enumerate(names)}


def kernel(x, p, w_in, f_bias, sg_ln_g, sg_ln_b, sg_w, sg_b, att_out_g, sg_out_g, w_out, pre_mix_g, post_mix_g, pre_ffn_g, post_ffn_g, w_ff1, w_ff2, ple_w, ple_gate_w, ple_gate_b, loss_target, m_w_in, m_f_bias, m_sg_ln_g, m_sg_ln_b, m_sg_w, m_sg_b, m_att_out_g, m_sg_out_g, m_w_out, m_pre_mix_g, m_post_mix_g, m_pre_ffn_g, m_post_ffn_g, m_w_ff1, m_w_ff2, m_ple_w, m_ple_gate_w, m_ple_gate_b, v_w_in, v_f_bias, v_sg_ln_g, v_sg_ln_b, v_sg_w, v_sg_b, v_att_out_g, v_sg_out_g, v_w_out, v_pre_mix_g, v_post_mix_g, v_pre_ffn_g, v_post_ffn_g, v_w_ff1, v_w_ff2, v_ple_w, v_ple_gate_w, v_ple_gate_b):
    small_w = dict(sg_w=sg_w, f_bias=f_bias, sg_ln_g=sg_ln_g, sg_ln_b=sg_ln_b, sg_b=sg_b, att_out_g=att_out_g,
                   sg_out_g=sg_out_g, pre_mix_g=pre_mix_g, post_mix_g=post_mix_g, pre_ffn_g=pre_ffn_g,
                   post_ffn_g=post_ffn_g, ple_gate_b=ple_gate_b)
    small_m = dict(sg_w=m_sg_w, f_bias=m_f_bias, sg_ln_g=m_sg_ln_g, sg_ln_b=m_sg_ln_b, sg_b=m_sg_b, att_out_g=m_att_out_g,
                   sg_out_g=m_sg_out_g, pre_mix_g=m_pre_mix_g, post_mix_g=m_post_mix_g, pre_ffn_g=m_pre_ffn_g,
                   post_ffn_g=m_post_ffn_g, ple_gate_b=m_ple_gate_b)
    small_v = dict(sg_w=v_sg_w, f_bias=v_f_bias, sg_ln_g=v_sg_ln_g, sg_ln_b=v_sg_ln_b, sg_b=v_sg_b, att_out_g=v_att_out_g,
                   sg_out_g=v_sg_out_g, pre_mix_g=v_pre_mix_g, post_mix_g=v_post_mix_g, pre_ffn_g=v_pre_ffn_g,
                   post_ffn_g=v_post_ffn_g, ple_gate_b=v_ple_gate_b)
    big = dict(w_in=(w_in, m_w_in, v_w_in), w_out=(w_out, m_w_out, v_w_out), w_ff1=(w_ff1, m_w_ff1, v_w_ff1),
               w_ff2=(w_ff2, m_w_ff2, v_w_ff2), ple_w=(ple_w, m_ple_w, v_ple_w),
               ple_gate_w=(ple_gate_w, m_ple_gate_w, v_ple_gate_w))

    xt, pt, tgt = x[0], p[0, 0], loss_target[0]
    ws = W_IN_COLS // NDEV

    gw_in = _gather(jnp.pad(jnp.transpose(w_in[0]).astype(BF16), ((0, WS_PAD - ws), (0, 0))), "gather_w_in")
    gw_in, others = lax.optimization_barrier((gw_in, [w_out[0].astype(BF16), w_ff1[0].astype(BF16), w_ff2[0].astype(BF16),
                                                      ple_w[0].astype(BF16), ple_gate_w[0].astype(BF16)]))
    rest = _xchg_start(others, True, "gather_rest_start")
    wint = gw_in[:, 0:ws, :].reshape(W_IN_COLS, D)
    wtp = jnp.concatenate([wint[0:NQKV], wint[NQKV + 8:W_IN_COLS], wint[NQKV:NQKV + 8],
                           jnp.zeros((FPAD - 8, D), BF16)], axis=0)

    fb = jnp.pad(f_bias.astype(F32), ((0, 0), (0, FPAD - 8)))
    sgw = sg_w[0]
    sgwt = jnp.transpose(sg_w[0], (0, 2, 1))
    sgbt = jnp.transpose(sg_b[0])

    qkv, kt, vt, us, fz, ab = _inproj(xt, pre_mix_g + rest[4][0, 0], wtp)
    ct, kk, vk = _fcum(fz, fb, qkv)
    yatt, lsec = _attn_fwd(qkv, kk, vt, ct)
    gw_out, gw1, gw2, gwpe, gwg = _xchg_wait(rest, True, yatt, "gather_rest_wait")
    wout = gw_out.reshape(D, D)
    wg = gwg.reshape(D, D)
    wpe = jnp.transpose(gwpe, (1, 0, 2)).reshape(PLE, D)
    h1, yb, o = _sgu_out(us, yatt, xt, sg_ln_g, sg_ln_b, sgw, sgbt, att_out_g, sg_out_g, wout, post_mix_g)
    c2b, f1b, ff, h2 = _ffn_fwd(h1, pre_ffn_g, gw1, gw2, post_ffn_g)
    dh2, dbg, loss_acc, g_g, g_pe = _ple_loss(h2, pt, tgt, wg, ple_gate_b, wpe)
    g_g = g_g.reshape(NDEV, D // NDEV, D)
    g_pe = jnp.transpose(g_pe.reshape(PLE, NDEV, D // NDEV), (1, 0, 2))

    dffb, df1, dh1, dgpostffn, dgpreffn = _ffn_bwd(dh2, ff, h1, f1b, gw1, gw2, post_ffn_g, pre_ffn_g)
    (g_1,) = _wgrad(c2b, [(df1, ("col", DFF // NDEV))], "wgrad_ff1")
    (g_2,) = _wgrad(f1b, [(dffb, ("row", DFF // NDEV))], "wgrad_ff2", a_fn=_sq_relu)
    early = _xchg_start([g_1, g_2, g_pe, g_g], False, "scatter_early_start")
    qw, dow, dus, g_out, dsgw, dsgbt, dlng, dlnb, dgatt, dgsg, dgpostmix = _mix_bwd(
        dh1, o, us, yatt, qkv, lsec, yb, sg_ln_g, sg_ln_b, sgw, sgwt, sgbt, att_out_g, sg_out_g, wout,
        post_mix_g + early[4][0, 0])
    mid = _xchg_start([g_out.reshape(NDEV, D // NDEV, D)], False, "scatter_mid_start")
    small_g = dict(sg_w=dsgw, sg_ln_g=dlng, sg_ln_b=dlnb, sg_b=jnp.transpose(dsgbt[:, 0:8]), att_out_g=dgatt, sg_out_g=dgsg,
                   post_mix_g=dgpostmix, pre_ffn_g=dgpreffn, post_ffn_g=dgpostffn, ple_gate_b=dbg)
    small_early = _xchg_start([_pack(small_g, False, loss_acc)], True, "gather_small_early_start")
    dqt, dcq, dk, dv, dck = _attn_bwd(kk, vk, kt, qw, dow, mid[4] + small_early[4])
    dfz, dfb = _fgate_bwd(dcq, dck, fz, fb)

    r_1, r_2, r_pe, r_g = _xchg_wait(early, False, dfz, "scatter_early_wait")
    g_in, *adam_1 = _wgrad(ab, [(dqt, "t"), (dk, None), (dv, None), (dus, None), (dfz, None)], "wgrad_in",
                           out_dtype=BF16, transposed=True, adam=(r_1, *[t[0] for t in big["w_ff1"]]),
                           stack=[(0, AW), (AW, AW), (2 * AW, AW), (NQKV + 8, NUS), (NQKV, 8)])
    g_in = g_in.reshape(NDEV, ws, D)
    late = _xchg_start([g_in], False, "scatter_late_start")
    grad_x, dgpremix, *adam_2 = _inproj_bwd(dqt, dk, dv, dus, dfz, wtp, xt, dh1, pre_mix_g + late[4][0, 0],
                                            (r_2, *[t[0] for t in big["w_ff2"]]))

    (r_out,) = _xchg_wait(mid, False, grad_x, "scatter_mid_wait")
    (r_early,) = _xchg_wait(small_early, True, grad_x, "gather_small_early_wait")
    r_late = _gather(_pack(dict(f_bias=dfb[:, 0:8], pre_mix_g=dgpremix), True), "gather_small_late")

    res = dict(w_ff1=[t[None] for t in adam_1], w_ff2=[t[None] for t in adam_2])

    def adam_big(name, parts):
        w, m, v = big[name]
        res[name] = [t[None] for t in _adam(parts, w[0], m[0], v[0], "adam_" + name)]

    for name, parts in (("w_out", r_out), ("ple_w", r_pe), ("ple_gate_w", r_g)):
        adam_big(name, parts)
    (r_in,) = _xchg_wait(late, False, res["ple_gate_w"][0], "scatter_late_wait")
    res["w_in"] = [jnp.transpose(t, (1, 2, 0)) for t in
                   _adam_unit_rows(r_in, *[jnp.transpose(t, (2, 0, 1)) for t in big["w_in"]], "adam_w_in")]
    loss, small = _adam_small(r_early, r_late, small_w, small_m, small_v)
    res.update(small)

    order = ["w_in", "f_bias", "sg_ln_g", "sg_ln_b", "sg_w", "sg_b", "att_out_g", "sg_out_g", "w_out", "pre_mix_g",
             "post_mix_g", "pre_ffn_g", "post_ffn_g", "w_ff1", "w_ff2", "ple_w", "ple_gate_w", "ple_gate_b"]
    outs = [loss[0, 0], grad_x[None]]
    for kind in range(4):
        outs += [res[name][kind] for name in order]
    return tuple(outs)
```
